```python
import math
import jax, jax.numpy as jnp
from jax import lax
import numpy as np

D_MODEL = 2048
BATCH = 8
SEQ = 2048
DEPTH = 1

MEM_LEN = 256
D_RNN = 1024
RNN_BLOCKS = 8
RNN_BLOCK = D_RNN // RNN_BLOCKS
CONV_W = 4
LRU_C = 8.0
SWA_HEADS = 16
SWA_KV_HEADS = 2
SWA_GROUP = SWA_HEADS // SWA_KV_HEADS
SWA_HD = 64
D_SWA = SWA_HEADS * SWA_HD
D_SWA_KV = SWA_KV_HEADS * SWA_HD
WINDOW = 128
BLOCK = WINDOW
MEM_HEADS = 4
MEM_HD = 256
D_MEM = MEM_HEADS * MEM_HD
REL_BUCKETS = 32
REL_MAX_DIST = 128
N_BRANCH = 3
EPS = 1e-6
NEG_INF = -1e30

IN_SPLITS = (D_RNN, D_RNN, D_SWA, D_SWA_KV, D_SWA_KV, D_SWA, D_MEM, D_MEM, N_BRANCH * D_MODEL)
D_IN = 2 * D_RNN + 2 * D_SWA + 2 * D_SWA_KV + 2 * D_MEM + N_BRANCH * D_MODEL

kernel_name = "hybrid_rglru_swa_sink_memxattn_gated"


def rmsnorm(x, g):
    xf = x.astype(jnp.float32)
    y = xf * lax.rsqrt(jnp.mean(xf * xf, axis=-1, keepdims=True) + EPS)
    return (y * g.astype(jnp.float32)).astype(x.dtype)


def rel_bucket(dist):
    n = jnp.maximum(dist, 0)
    max_exact = REL_BUCKETS // 2
    ratio = jnp.log(jnp.maximum(n, 1).astype(jnp.float32) / max_exact) / math.log(REL_MAX_DIST / max_exact)
    large = max_exact + (ratio * (REL_BUCKETS - max_exact)).astype(jnp.int32)
    large = jnp.minimum(large, REL_BUCKETS - 1)
    return jnp.where(n < max_exact, n, large)


def rglru_branch(xr, conv_w, conv_b, w_a, b_a, w_x, b_x, lam):
    B, S, _ = xr.shape
    xp = jnp.pad(xr, ((0, 0), (CONV_W - 1, 0), (0, 0)))
    conv = conv_b + sum(xp[:, CONV_W - 1 - k: CONV_W - 1 - k + S] * conv_w[k] for k in range(CONV_W))
    cb = conv.reshape(B, S, RNN_BLOCKS, RNN_BLOCK)
    gate_r = jax.nn.sigmoid((jnp.einsum('bsni,nij->bsnj', cb, w_a).reshape(B, S, D_RNN) + b_a).astype(jnp.float32))
    gate_i = jax.nn.sigmoid((jnp.einsum('bsni,nij->bsnj', cb, w_x).reshape(B, S, D_RNN) + b_x).astype(jnp.float32))
    log_a = -LRU_C * gate_r * jax.nn.softplus(-lam.astype(jnp.float32))
    a = jnp.exp(log_a)
    mult = jnp.sqrt(-jnp.expm1(2.0 * log_a))
    is_start = (jnp.arange(S) == 0)[None, :, None]
    mult = jnp.where(is_start, 1.0, mult)
    b = mult * gate_i * conv.astype(jnp.float32)

    def combine(e1, e2):
        a1, b1 = e1
        a2, b2 = e2
        return a1 * a2, a2 * b1 + b2

    _, h = lax.associative_scan(combine, (a, b), axis=1)
    return h.astype(xr.dtype)


def swa_branch(q, k, v, sinks, rel_bias):
    B, S, _ = q.shape
    nb = S // BLOCK
    q = q.reshape(B, nb, BLOCK, SWA_KV_HEADS, SWA_GROUP, SWA_HD)
    k = k.reshape(B, nb, BLOCK, SWA_KV_HEADS, SWA_HD)
    v = v.reshape(B, nb, BLOCK, SWA_KV_HEADS, SWA_HD)

    def with_prev(t):
        prev = jnp.concatenate([jnp.zeros_like(t[:, :1]), t[:, :-1]], axis=1)
        return jnp.concatenate([prev, t], axis=2)

    kk = with_prev(k)
    vv = with_prev(v)
    logits = jnp.einsum('bnqhgd,bnkhd->bnhgqk', q, kk).astype(jnp.float32) * (SWA_HD ** -0.5)

    qi = jnp.arange(BLOCK)[:, None]
    kj = jnp.arange(2 * BLOCK)[None, :]
    dist = qi + BLOCK - kj
    in_window = (dist >= 0) & (dist < WINDOW)
    key_abs = jnp.arange(nb)[:, None, None] * BLOCK + kj[None] - BLOCK
    valid = in_window[None] & (key_abs >= 0)

    bias = rel_bias.astype(jnp.float32)[rel_bucket(dist)]
    bias = jnp.transpose(bias, (2, 0, 1)).reshape(SWA_KV_HEADS, SWA_GROUP, BLOCK, 2 * BLOCK)
    logits = logits + bias[None, None]
    logits = jnp.where(valid[None, :, None, None], logits, NEG_INF)

    sink = sinks.astype(jnp.float32).reshape(SWA_KV_HEADS, SWA_GROUP)[None, None, :, :, None, None]
    m = jnp.maximum(jnp.max(logits, axis=-1, keepdims=True), sink)
    p = jnp.exp(logits - m)
    denom = jnp.sum(p, axis=-1, keepdims=True) + jnp.exp(sink - m)
    probs = (p / denom).astype(v.dtype)
    out = jnp.einsum('bnhgqk,bnkhd->bnqhgd', probs, vv)
    return out.reshape(B, S, D_SWA)


def mem_branch(q, mk, mv):
    B, S, _ = q.shape
    M = mk.shape[1]
    q = q.reshape(B, S, MEM_HEADS, MEM_HD)
    mk = mk.reshape(B, M, MEM_HEADS, MEM_HD)
    mv = mv.reshape(B, M, MEM_HEADS, MEM_HD)
    logits = jnp.einsum('bshd,bmhd->bhsm', q, mk).astype(jnp.float32) * (MEM_HD ** -0.5)
    probs = jax.nn.softmax(logits, axis=-1).astype(mv.dtype)
    out = jnp.einsum('bhsm,bmhd->bshd', probs, mv)
    return out.reshape(B, S, D_MEM)


def _fwd_setup_inputs(seed: int = 0) -> dict:
    key = jax.random.key(seed)
    ks = jax.random.split(key, 24)
    f32 = jnp.float32
    nrm = lambda k, shape, s: jax.random.normal(k, shape, f32) * s
    L = DEPTH
    u = jax.random.uniform(ks[12], (L, D_RNN), f32, 0.9, 0.999)
    a0 = u ** (1.0 / LRU_C)
    lru_lambda = jnp.log(a0) - jnp.log1p(-a0)
    return {
        "x": nrm(ks[0], (BATCH, SEQ, D_MODEL), 1.0),
        "mem": nrm(ks[1], (BATCH, MEM_LEN, D_MODEL), 1.0),
        "pre_norm_g": 1.0 + nrm(ks[2], (L, D_MODEL), 0.02),
        "post_norm_g": 1.0 + nrm(ks[3], (L, D_MODEL), 0.02),
        "mem_norm_g": 1.0 + nrm(ks[4], (L, D_MODEL), 0.02),
        "w_in": nrm(ks[5], (L, D_MODEL, D_IN), D_MODEL ** -0.5),
        "conv_w": nrm(ks[6], (L, CONV_W, D_RNN), CONV_W ** -0.5),
        "conv_b": nrm(ks[7], (L, D_RNN), 0.02),
        "w_rg_a": nrm(ks[8], (L, RNN_BLOCKS, RNN_BLOCK, RNN_BLOCK), RNN_BLOCK ** -0.5),
        "b_rg_a": nrm(ks[9], (L, D_RNN), 0.02),
        "w_rg_x": nrm(ks[10], (L, RNN_BLOCKS, RNN_BLOCK, RNN_BLOCK), RNN_BLOCK ** -0.5),
        "b_rg_x": nrm(ks[11], (L, D_RNN), 0.02),
        "lru_lambda": lru_lambda,
        "swa_sinks": nrm(ks[13], (L, SWA_HEADS), 0.5),
        "rel_bias": nrm(ks[14], (REL_BUCKETS, SWA_HEADS), 0.5),
        "w_mem_kv": nrm(ks[15], (L, D_MODEL, 2 * D_MEM), D_MODEL ** -0.5),
        "w_br_rg": nrm(ks[16], (L, D_RNN, D_MODEL), D_RNN ** -0.5),
        "w_br_swa": nrm(ks[17], (L, D_SWA, D_MODEL), D_SWA ** -0.5),
        "w_br_mem": nrm(ks[18], (L, D_MEM, D_MODEL), D_MEM ** -0.5),
        "w_out": nrm(ks[19], (L, D_MODEL, D_MODEL), D_MODEL ** -0.5),
    }


def _fwd_reference(x, mem, pre_norm_g, post_norm_g, mem_norm_g, w_in, conv_w, conv_b, w_rg_a, b_rg_a,
              w_rg_x, b_rg_x, lru_lambda, swa_sinks, rel_bias, w_mem_kv, w_br_rg, w_br_swa,
              w_br_mem, w_out):
    B, S, D = x.shape
    split_at = np.cumsum(IN_SPLITS)[:-1].tolist()
    for l in range(DEPTH):
        h = rmsnorm(x, pre_norm_g[l])
        proj = jnp.einsum('bsd,de->bse', h, w_in[l])
        (xr, g_rg, q_s, k_s, v_s, g_swa, q_m, g_mem, gate_logits) = jnp.split(proj, split_at, axis=-1)

        y_rg = rglru_branch(xr, conv_w[l], conv_b[l], w_rg_a[l], b_rg_a[l], w_rg_x[l], b_rg_x[l],
                            lru_lambda[l]) * jax.nn.silu(g_rg)
        y_swa = swa_branch(q_s, k_s, v_s, swa_sinks[l], rel_bias) * jax.nn.silu(g_swa)
        memn = rmsnorm(mem, mem_norm_g[l])
        mkv = jnp.einsum('bmd,de->bme', memn, w_mem_kv[l])
        mk, mv = jnp.split(mkv, 2, axis=-1)
        y_mem = mem_branch(q_m, mk, mv) * jax.nn.silu(g_mem)

        gates = jax.nn.sigmoid(gate_logits.astype(jnp.float32)).astype(x.dtype).reshape(B, S, N_BRANCH, D)
        merged = (gates[:, :, 0] * jnp.einsum('bsr,rd->bsd', y_rg, w_br_rg[l])
                  + gates[:, :, 1] * jnp.einsum('bsr,rd->bsd', y_swa, w_br_swa[l])
                  + gates[:, :, 2] * jnp.einsum('bsr,rd->bsd', y_mem, w_br_mem[l]))
        out = jnp.einsum('bsd,de->bse', merged, w_out[l])
        x = x + rmsnorm(out, post_norm_g[l])
    return x


import jax as _jax
import jax.numpy as _jnp

TWIN_FORMAT = 'train_step'
FWD_PARAMS = ['x', 'mem', 'pre_norm_g', 'post_norm_g', 'mem_norm_g', 'w_in', 'conv_w', 'conv_b', 'w_rg_a', 'b_rg_a', 'w_rg_x', 'b_rg_x', 'lru_lambda', 'swa_sinks', 'rel_bias', 'w_mem_kv', 'w_br_rg', 'w_br_swa', 'w_br_mem', 'w_out']
TWIN_WEIGHTS = ['pre_norm_g', 'post_norm_g', 'mem_norm_g', 'w_in', 'conv_w', 'conv_b', 'w_rg_a', 'b_rg_a', 'w_rg_x', 'b_rg_x', 'lru_lambda', 'swa_sinks', 'rel_bias', 'w_mem_kv', 'w_br_rg', 'w_br_swa', 'w_br_mem', 'w_out']
TWIN_DIFF_INPUT = 'x'
TWIN_INPUTS = ['x', 'mem', 'pre_norm_g', 'post_norm_g', 'mem_norm_g', 'w_in', 'conv_w', 'conv_b', 'w_rg_a', 'b_rg_a', 'w_rg_x', 'b_rg_x', 'lru_lambda', 'swa_sinks', 'rel_bias', 'w_mem_kv', 'w_br_rg', 'w_br_swa', 'w_br_mem', 'w_out', 'loss_target', 'm_pre_norm_g', 'm_post_norm_g', 'm_mem_norm_g', 'm_w_in', 'm_conv_w', 'm_conv_b', 'm_w_rg_a', 'm_b_rg_a', 'm_w_rg_x', 'm_b_rg_x', 'm_lru_lambda', 'm_swa_sinks', 'm_rel_bias', 'm_w_mem_kv', 'm_w_br_rg', 'm_w_br_swa', 'm_w_br_mem', 'm_w_out', 'v_pre_norm_g', 'v_post_norm_g', 'v_mem_norm_g', 'v_w_in', 'v_conv_w', 'v_conv_b', 'v_w_rg_a', 'v_b_rg_a', 'v_w_rg_x', 'v_b_rg_x', 'v_lru_lambda', 'v_swa_sinks', 'v_rel_bias', 'v_w_mem_kv', 'v_w_br_rg', 'v_w_br_swa', 'v_w_br_mem', 'v_w_out']
TWIN_OUTPUTS = ['loss', 'grad_x', 'grad_pre_norm_g', 'grad_post_norm_g', 'grad_mem_norm_g', 'grad_w_in', 'grad_conv_w', 'grad_conv_b', 'grad_w_rg_a', 'grad_b_rg_a', 'grad_w_rg_x', 'grad_b_rg_x', 'grad_lru_lambda', 'grad_swa_sinks', 'grad_rel_bias', 'grad_w_mem_kv', 'grad_w_br_rg', 'grad_w_br_swa', 'grad_w_br_mem', 'grad_w_out', 'delta_pre_norm_g', 'delta_post_norm_g', 'delta_mem_norm_g', 'delta_w_in', 'delta_conv_w', 'delta_conv_b', 'delta_w_rg_a', 'delta_b_rg_a', 'delta_w_rg_x', 'delta_b_rg_x', 'delta_lru_lambda', 'delta_swa_sinks', 'delta_rel_bias', 'delta_w_mem_kv', 'delta_w_br_rg', 'delta_w_br_swa', 'delta_w_br_mem', 'delta_w_out', 'new_m_pre_norm_g', 'new_m_post_norm_g', 'new_m_mem_norm_g', 'new_m_w_in', 'new_m_conv_w', 'new_m_conv_b', 'new_m_w_rg_a', 'new_m_b_rg_a', 'new_m_w_rg_x', 'new_m_b_rg_x', 'new_m_lru_lambda', 'new_m_swa_sinks', 'new_m_rel_bias', 'new_m_w_mem_kv', 'new_m_w_br_rg', 'new_m_w_br_swa', 'new_m_w_br_mem', 'new_m_w_out', 'new_v_pre_norm_g', 'new_v_post_norm_g', 'new_v_mem_norm_g', 'new_v_w_in', 'new_v_conv_w', 'new_v_conv_b', 'new_v_w_rg_a', 'new_v_b_rg_a', 'new_v_w_rg_x', 'new_v_b_rg_x', 'new_v_lru_lambda', 'new_v_swa_sinks', 'new_v_rel_bias', 'new_v_w_mem_kv', 'new_v_w_br_rg', 'new_v_w_br_swa', 'new_v_w_br_mem', 'new_v_w_out']
TWIN_LEAF_KINDS = {'loss': 'loss', 'grad_x': 'grad_x', 'grad_pre_norm_g': 'grad_w', 'grad_post_norm_g': 'grad_w', 'grad_mem_norm_g': 'grad_w', 'grad_w_in': 'grad_w', 'grad_conv_w': 'grad_w', 'grad_conv_b': 'grad_w', 'grad_w_rg_a': 'grad_w', 'grad_b_rg_a': 'grad_w', 'grad_w_rg_x': 'grad_w', 'grad_b_rg_x': 'grad_w', 'grad_lru_lambda': 'grad_w', 'grad_swa_sinks': 'grad_w', 'grad_rel_bias': 'grad_w', 'grad_w_mem_kv': 'grad_w', 'grad_w_br_rg': 'grad_w', 'grad_w_br_swa': 'grad_w', 'grad_w_br_mem': 'grad_w', 'grad_w_out': 'grad_w', 'delta_pre_norm_g': 'delta_w', 'delta_post_norm_g': 'delta_w', 'delta_mem_norm_g': 'delta_w', 'delta_w_in': 'delta_w', 'delta_conv_w': 'delta_w', 'delta_conv_b': 'delta_w', 'delta_w_rg_a': 'delta_w', 'delta_b_rg_a': 'delta_w', 'delta_w_rg_x': 'delta_w', 'delta_b_rg_x': 'delta_w', 'delta_lru_lambda': 'delta_w', 'delta_swa_sinks': 'delta_w', 'delta_rel_bias': 'delta_w', 'delta_w_mem_kv': 'delta_w', 'delta_w_br_rg': 'delta_w', 'delta_w_br_swa': 'delta_w', 'delta_w_br_mem': 'delta_w', 'delta_w_out': 'delta_w', 'new_m_pre_norm_g': 'new_m', 'new_m_post_norm_g': 'new_m', 'new_m_mem_norm_g': 'new_m', 'new_m_w_in': 'new_m', 'new_m_conv_w': 'new_m', 'new_m_conv_b': 'new_m', 'new_m_w_rg_a': 'new_m', 'new_m_b_rg_a': 'new_m', 'new_m_w_rg_x': 'new_m', 'new_m_b_rg_x': 'new_m', 'new_m_lru_lambda': 'new_m', 'new_m_swa_sinks': 'new_m', 'new_m_rel_bias': 'new_m', 'new_m_w_mem_kv': 'new_m', 'new_m_w_br_rg': 'new_m', 'new_m_w_br_swa': 'new_m', 'new_m_w_br_mem': 'new_m', 'new_m_w_out': 'new_m', 'new_v_pre_norm_g': 'new_v', 'new_v_post_norm_g': 'new_v', 'new_v_mem_norm_g': 'new_v', 'new_v_w_in': 'new_v', 'new_v_conv_w': 'new_v', 'new_v_conv_b': 'new_v', 'new_v_w_rg_a': 'new_v', 'new_v_b_rg_a': 'new_v', 'new_v_w_rg_x': 'new_v', 'new_v_b_rg_x': 'new_v', 'new_v_lru_lambda': 'new_v', 'new_v_swa_sinks': 'new_v', 'new_v_rel_bias': 'new_v', 'new_v_w_mem_kv': 'new_v', 'new_v_w_br_rg': 'new_v', 'new_v_w_br_swa': 'new_v', 'new_v_w_br_mem': 'new_v', 'new_v_w_out': 'new_v'}


def _forward(args):
    return _fwd_reference(*[args[k] for k in FWD_PARAMS])


def _output_shape():
    out = _jax.eval_shape(lambda: _forward(_fwd_setup_inputs(0)))
    return out.shape, out.dtype

N_MICROBATCH = 1
ADAM_LR = 0.001
ADAM_B1 = 0.9
ADAM_B2 = 0.999
ADAM_EPS = 1e-08
ADAM_WD = 0.01
ADAM_STEP = 10
PER_EXAMPLE_BATCH_AXIS = {'x': 0, 'mem': 0, 'loss_target': 0}
SHARED_INPUTS = []
_WEIGHT_DTYPES = {'pre_norm_g': _jnp.float32, 'post_norm_g': _jnp.float32, 'mem_norm_g': _jnp.float32, 'w_in': _jnp.float32, 'conv_w': _jnp.float32, 'conv_b': _jnp.float32, 'w_rg_a': _jnp.float32, 'b_rg_a': _jnp.float32, 'w_rg_x': _jnp.float32, 'b_rg_x': _jnp.float32, 'lru_lambda': _jnp.float32, 'swa_sinks': _jnp.float32, 'rel_bias': _jnp.float32, 'w_mem_kv': _jnp.float32, 'w_br_rg': _jnp.float32, 'w_br_swa': _jnp.float32, 'w_br_mem': _jnp.float32, 'w_out': _jnp.float32}
MOMENT_SCALE = {'pre_norm_g': 1.448956e-01, 'post_norm_g': 8.003949e+00, 'mem_norm_g': 2.294106e-02, 'w_in': 5.696206e-02, 'conv_w': 1.451390e-01, 'conv_b': 2.170550e+00, 'w_rg_a': 5.099596e-02, 'b_rg_a': 3.714426e-02, 'w_rg_x': 9.229813e-02, 'b_rg_x': 4.550000e-02, 'lru_lambda': 6.852497e-02, 'swa_sinks': 3.860870e-02, 'rel_bias': 5.632723e-02, 'w_mem_kv': 2.234062e-02, 'w_br_rg': 9.765483e-02, 'w_br_swa': 3.302966e-02, 'w_br_mem': 1.596907e-02, 'w_out': 9.917263e-02}


def _to_microbatches(a, axis):
    t = _jnp.moveaxis(a, axis, 0)
    t = t.reshape((N_MICROBATCH, t.shape[0] // N_MICROBATCH) + t.shape[1:])
    return _jnp.moveaxis(t, 1, axis + 1)


def setup_inputs(seed: int = 0) -> dict:
    inp = _fwd_setup_inputs(seed)
    key = _jax.random.fold_in(_jax.random.key(seed), 7919)
    shape, _ = _output_shape()
    out = dict(inp)
    out["loss_target"] = _jax.random.normal(_jax.random.fold_in(key, 0), shape, _jnp.float32)
    for i, name in enumerate(TWIN_WEIGHTS):
        w = inp[name].astype(_jnp.float32)
        if MOMENT_SCALE is None:
            s = _jnp.sqrt(_jnp.mean(_jnp.square(w)) + 1e-30)
        else:
            s = MOMENT_SCALE[name]
        km, kv = _jax.random.split(_jax.random.fold_in(key, i + 1))
        out[name] = w
        out["m_" + name] = s * _jax.random.normal(km, w.shape, _jnp.float32)
        out["v_" + name] = (s * s) * _jax.random.uniform(kv, w.shape, _jnp.float32, 0.5, 1.5)
    if N_MICROBATCH > 1:
        for name, axis in PER_EXAMPLE_BATCH_AXIS.items():
            out[name] = _to_microbatches(out[name], axis)
    return {'x': out['x'], 'mem': out['mem'], 'pre_norm_g': out['pre_norm_g'], 'post_norm_g': out['post_norm_g'], 'mem_norm_g': out['mem_norm_g'], 'w_in': out['w_in'], 'conv_w': out['conv_w'], 'conv_b': out['conv_b'], 'w_rg_a': out['w_rg_a'], 'b_rg_a': out['b_rg_a'], 'w_rg_x': out['w_rg_x'], 'b_rg_x': out['b_rg_x'], 'lru_lambda': out['lru_lambda'], 'swa_sinks': out['swa_sinks'], 'rel_bias': out['rel_bias'], 'w_mem_kv': out['w_mem_kv'], 'w_br_rg': out['w_br_rg'], 'w_br_swa': out['w_br_swa'], 'w_br_mem': out['w_br_mem'], 'w_out': out['w_out'], 'loss_target': out['loss_target'], 'm_pre_norm_g': out['m_pre_norm_g'], 'm_post_norm_g': out['m_post_norm_g'], 'm_mem_norm_g': out['m_mem_norm_g'], 'm_w_in': out['m_w_in'], 'm_conv_w': out['m_conv_w'], 'm_conv_b': out['m_conv_b'], 'm_w_rg_a': out['m_w_rg_a'], 'm_b_rg_a': out['m_b_rg_a'], 'm_w_rg_x': out['m_w_rg_x'], 'm_b_rg_x': out['m_b_rg_x'], 'm_lru_lambda': out['m_lru_lambda'], 'm_swa_sinks': out['m_swa_sinks'], 'm_rel_bias': out['m_rel_bias'], 'm_w_mem_kv': out['m_w_mem_kv'], 'm_w_br_rg': out['m_w_br_rg'], 'm_w_br_swa': out['m_w_br_swa'], 'm_w_br_mem': out['m_w_br_mem'], 'm_w_out': out['m_w_out'], 'v_pre_norm_g': out['v_pre_norm_g'], 'v_post_norm_g': out['v_post_norm_g'], 'v_mem_norm_g': out['v_mem_norm_g'], 'v_w_in': out['v_w_in'], 'v_conv_w': out['v_conv_w'], 'v_conv_b': out['v_conv_b'], 'v_w_rg_a': out['v_w_rg_a'], 'v_b_rg_a': out['v_b_rg_a'], 'v_w_rg_x': out['v_w_rg_x'], 'v_b_rg_x': out['v_b_rg_x'], 'v_lru_lambda': out['v_lru_lambda'], 'v_swa_sinks': out['v_swa_sinks'], 'v_rel_bias': out['v_rel_bias'], 'v_w_mem_kv': out['v_w_mem_kv'], 'v_w_br_rg': out['v_w_br_rg'], 'v_w_br_swa': out['v_w_br_swa'], 'v_w_br_mem': out['v_w_br_mem'], 'v_w_out': out['v_w_out']}


def _loss(weights, diff, rest, loss_target):
    with _jax.named_scope("forward"):
        args = {**rest, TWIN_DIFF_INPUT: diff, **{k: w.astype(_WEIGHT_DTYPES[k]) for k, w in weights.items()}}
        y = _forward(args)
    with _jax.named_scope("loss_head"):
        err = _jnp.square(y.astype(_jnp.float32) - loss_target)
        return 0.5 * _jnp.sum(_jnp.mean(err, axis=-1)) if err.ndim else 0.5 * err


def _adamw(w, g, m, v):
    m = ADAM_B1 * m + (1.0 - ADAM_B1) * g
    v = ADAM_B2 * v + (1.0 - ADAM_B2) * _jnp.square(g)
    m_hat = m / (1.0 - ADAM_B1 ** ADAM_STEP)
    v_hat = v / (1.0 - ADAM_B2 ** ADAM_STEP)
    delta = -ADAM_LR * (m_hat / (_jnp.sqrt(v_hat) + ADAM_EPS) + ADAM_WD * w)
    return delta, m, v


def reference(x, mem, pre_norm_g, post_norm_g, mem_norm_g, w_in, conv_w, conv_b, w_rg_a, b_rg_a, w_rg_x, b_rg_x, lru_lambda, swa_sinks, rel_bias, w_mem_kv, w_br_rg, w_br_swa, w_br_mem, w_out, loss_target, m_pre_norm_g, m_post_norm_g, m_mem_norm_g, m_w_in, m_conv_w, m_conv_b, m_w_rg_a, m_b_rg_a, m_w_rg_x, m_b_rg_x, m_lru_lambda, m_swa_sinks, m_rel_bias, m_w_mem_kv, m_w_br_rg, m_w_br_swa, m_w_br_mem, m_w_out, v_pre_norm_g, v_post_norm_g, v_mem_norm_g, v_w_in, v_conv_w, v_conv_b, v_w_rg_a, v_b_rg_a, v_w_rg_x, v_b_rg_x, v_lru_lambda, v_swa_sinks, v_rel_bias, v_w_mem_kv, v_w_br_rg, v_w_br_swa, v_w_br_mem, v_w_out):
    given = dict(x=x, mem=mem, pre_norm_g=pre_norm_g, post_norm_g=post_norm_g, mem_norm_g=mem_norm_g, w_in=w_in, conv_w=conv_w, conv_b=conv_b, w_rg_a=w_rg_a, b_rg_a=b_rg_a, w_rg_x=w_rg_x, b_rg_x=b_rg_x, lru_lambda=lru_lambda, swa_sinks=swa_sinks, rel_bias=rel_bias, w_mem_kv=w_mem_kv, w_br_rg=w_br_rg, w_br_swa=w_br_swa, w_br_mem=w_br_mem, w_out=w_out, loss_target=loss_target, m_pre_norm_g=m_pre_norm_g, m_post_norm_g=m_post_norm_g, m_mem_norm_g=m_mem_norm_g, m_w_in=m_w_in, m_conv_w=m_conv_w, m_conv_b=m_conv_b, m_w_rg_a=m_w_rg_a, m_b_rg_a=m_b_rg_a, m_w_rg_x=m_w_rg_x, m_b_rg_x=m_b_rg_x, m_lru_lambda=m_lru_lambda, m_swa_sinks=m_swa_sinks, m_rel_bias=m_rel_bias, m_w_mem_kv=m_w_mem_kv, m_w_br_rg=m_w_br_rg, m_w_br_swa=m_w_br_swa, m_w_br_mem=m_w_br_mem, m_w_out=m_w_out, v_pre_norm_g=v_pre_norm_g, v_post_norm_g=v_post_norm_g, v_mem_norm_g=v_mem_norm_g, v_w_in=v_w_in, v_conv_w=v_conv_w, v_conv_b=v_conv_b, v_w_rg_a=v_w_rg_a, v_b_rg_a=v_b_rg_a, v_w_rg_x=v_w_rg_x, v_b_rg_x=v_b_rg_x, v_lru_lambda=v_lru_lambda, v_swa_sinks=v_swa_sinks, v_rel_bias=v_rel_bias, v_w_mem_kv=v_w_mem_kv, v_w_br_rg=v_w_br_rg, v_w_br_swa=v_w_br_swa, v_w_br_mem=v_w_br_mem, v_w_out=v_w_out)
    weights = {n: given[n] for n in TWIN_WEIGHTS}
    shared = {n: given[n] for n in SHARED_INPUTS}
    per_example = {n: given[n] for n in ['x', 'mem']}
    grad_fn = _jax.value_and_grad(_loss, argnums=(0, 1))

    def one_microbatch(ex, loss_target):
        ex = dict(ex)
        diff = ex.pop(TWIN_DIFF_INPUT)
        return grad_fn(weights, diff, {**shared, **ex}, loss_target)

    if N_MICROBATCH == 1:
        loss, (grad_w, grad_x) = one_microbatch(per_example, given["loss_target"])
    else:
        def body(carry, xs):
            loss_sum, grad_sum = carry
            l_k, (gw_k, gx_k) = one_microbatch(xs[0], xs[1])
            with _jax.named_scope("update"):
                return (loss_sum + l_k, _jax.tree.map(_jnp.add, grad_sum, gw_k)), gx_k

        init = (_jnp.zeros((), _jnp.float32), _jax.tree.map(_jnp.zeros_like, weights))
        (loss, grad_w), grad_x = _jax.lax.scan(body, init, (per_example, given["loss_target"]))
    with _jax.named_scope("update"):
        delta_w, new_m, new_v = {}, {}, {}
        for n in TWIN_WEIGHTS:
            delta_w[n], new_m[n], new_v[n] = _adamw(weights[n], grad_w[n], given["m_" + n], given["v_" + n])
    return (loss, grad_x, *[grad_w[n] for n in TWIN_WEIGHTS], *[delta_w[n] for n in TWIN_WEIGHTS],
            *[new_m[n] for n in TWIN_WEIGHTS], *[new_v[n] for n in TWIN_WEIGHTS])
```

```python
import math

import jax
import jax.numpy as jnp
import numpy as np
from jax import lax
from jax.experimental import pallas as pl
from jax.experimental.pallas import tpu as pltpu

F32, BF = jnp.float32, jnp.bfloat16
MESH = pl.DeviceIdType.MESH
AXES = ("x", "y", "c")
N_DEV = 8

D_MODEL = 2048
D_RNN = 1024
RNN_BLOCKS = 8
RNN_BLOCK = 128
CONV_W = 4
LRU_C = 8.0
SWA_HEADS = 16
SWA_KV_HEADS = 2
SWA_HD = 64
WINDOW = 128
MEM_HEADS = 4
MEM_HD = 256
D_MEM = 1024
REL_BUCKETS = 32
REL_MAX_DIST = 128
EPS = 1e-6
NEG_INF = -1e30
D_IN = 12544
SEGMENTS = (("xr", 0, 1024, F32), ("g_rg", 1024, 1024, F32), ("q_s", 2048, 1024, BF), ("kv", 3072, 256, BF),
            ("g_swa", 3328, 1024, F32), ("q_m", 4352, 1024, BF), ("g_mem", 5376, 1024, F32), ("gl", 6400, 6144, F32))
SEG_TILE = 256

ADAM_LR, ADAM_B1, ADAM_B2, ADAM_EPS, ADAM_WD, ADAM_STEP = 0.001, 0.9, 0.999, 1e-08, 0.01, 10

NN = (((1,), (0,)), ((), ()))
NT = (((1,), (1,)), ((), ()))
TN = (((0,), (0,)), ((), ()))
MIB = 2 ** 20


def _dot(a, b, dn):
    return lax.dot_general(a, b, dn, preferred_element_type=F32)


def _params(sem, vmem_mib=48):
    return pltpu.CompilerParams(dimension_semantics=sem, vmem_limit_bytes=vmem_mib * MIB)


def _sigmoid(z):
    return 1.0 / (1.0 + jnp.exp(-z))


def _softplus(z):
    return jnp.maximum(z, 0.0) + jnp.log(1.0 + jnp.exp(-jnp.abs(z)))


def _expm1(z):
    p = z * (1.0 + z * (0.5 + z * (1.0 / 6 + z * (1.0 / 24 + z * (1.0 / 120 + z * (1.0 / 720 + z * (1.0 / 5040 + z / 40320)))))))
    return jnp.where(jnp.abs(z) < 0.3, p, jnp.exp(z) - 1.0)


def _flat(p):
    return 4 * p[0] + 2 * p[1] + p[2]


def _all_gather(arrs, name):
    n = len(arrs)

    def body(*refs):
        ins, outs = refs[:n], refs[n:2 * n]
        send_sems, recv_sems, local_sems = refs[2 * n:]
        x, y, c = lax.axis_index("x"), lax.axis_index("y"), lax.axis_index("c")
        me, sibling = (x, y, c), (x, y, 1 - c)
        chips = [(1 - x, y), (x, 1 - y), (1 - x, 1 - y)]

        def copy(a, k, block, to, src=None):
            dst = outs[a].at[_flat(block)]
            return pltpu.make_async_remote_copy(src_ref=dst if src is None else src, dst_ref=dst,
                                                send_sem=send_sems.at[a * 7 + k], recv_sem=recv_sems.at[a * 7 + k],
                                                device_id=to, device_id_type=MESH)

        mine = [pltpu.make_async_copy(ins[a], outs[a].at[_flat(me)], local_sems.at[a]) for a in range(n)]
        for cp in mine:
            cp.start()
        first = []
        for a in range(n):
            first += [copy(a, 1 + j, me, (*chip, c), src=ins[a]) for j, chip in enumerate(chips)]
            first.append(copy(a, 0, me, sibling, src=ins[a]))
        for cp in first:
            cp.start()
        passed = []
        for j, chip in enumerate(chips):
            for a in range(n):
                copy(a, 1 + j, (*chip, c), me).wait_recv()
                fw = copy(a, 4 + j, (*chip, c), sibling)
                fw.start()
                passed.append(fw)
        for a in range(n):
            copy(a, 0, sibling, me).wait_recv()
            for j, chip in enumerate(chips):
                copy(a, 4 + j, (*chip, 1 - c), me).wait_recv()
        for cp in first + passed:
            cp.wait_send()
        for cp in mine:
            cp.wait()

    any_spec = pl.BlockSpec(memory_space=pl.ANY)
    return pl.pallas_call(
        body, name=name,
        out_shape=[jax.ShapeDtypeStruct((N_DEV,) + a.shape, a.dtype) for a in arrs],
        in_specs=[any_spec] * n, out_specs=[any_spec] * n,
        scratch_shapes=[pltpu.SemaphoreType.DMA((7 * n,)), pltpu.SemaphoreType.DMA((7 * n,)), pltpu.SemaphoreType.DMA((n,))],
    )(*arrs)


def _reduce_scatter_exchange(parts, name):
    n = len(parts)

    def body(*refs):
        ins, outs = refs[:n], refs[n:2 * n]
        send_sems, recv_sems, local_sems = refs[2 * n:]
        x, y, c = lax.axis_index("x"), lax.axis_index("y"), lax.axis_index("c")
        me = (x, y, c)
        peers = []
        for r in range(1, N_DEV):
            fx, fy, fc = (r >> 2) & 1, (r >> 1) & 1, r & 1
            peers.append((1 - x if fx else x, 1 - y if fy else y, 1 - c if fc else c))
        mine = [pltpu.make_async_copy(ins[a].at[_flat(me)], outs[a].at[_flat(me)], local_sems.at[a]) for a in range(n)]
        for cp in mine:
            cp.start()
        sends = []
        for a in range(n):
            for r, peer in enumerate(peers):
                sends.append(pltpu.make_async_remote_copy(
                    src_ref=ins[a].at[_flat(peer)], dst_ref=outs[a].at[_flat(me)],
                    send_sem=send_sems.at[a * 7 + r], recv_sem=recv_sems.at[a * 7 + r],
                    device_id=peer, device_id_type=MESH))
        for cp in sends:
            cp.start()
        for a in range(n):
            for r, peer in enumerate(peers):
                pltpu.make_async_remote_copy(
                    src_ref=ins[a].at[_flat(me)], dst_ref=outs[a].at[_flat(peer)],
                    send_sem=send_sems.at[a * 7 + r], recv_sem=recv_sems.at[a * 7 + r],
                    device_id=peer, device_id_type=MESH).wait_recv()
        for cp in sends:
            cp.wait_send()
        for cp in mine:
            cp.wait()

    any_spec = pl.BlockSpec(memory_space=pl.ANY)
    return pl.pallas_call(
        body, name=name,
        out_shape=[jax.ShapeDtypeStruct(a.shape, a.dtype) for a in parts],
        in_specs=[any_spec] * n, out_specs=[any_spec] * n,
        scratch_shapes=[pltpu.SemaphoreType.DMA((7 * n,)), pltpu.SemaphoreType.DMA((7 * n,)), pltpu.SemaphoreType.DMA((n,))],
    )(*parts)


def _matmul(a, b, mode, M, N, K, tm, tn, tk, out_dtype, name, b_noff=0, b_blocked=False, out_blocked=False, vmem_mib=48):
    nm, nn, nk = M // tm, N // tn, K // tk
    if mode == "nn":
        a_spec = pl.BlockSpec((tm, tk), lambda j, i, k: (i, k))
        b_spec = pl.BlockSpec((tk, tn), lambda j, i, k: (k, j + b_noff))
        dn = NN
    elif mode == "nt":
        a_spec = pl.BlockSpec((tm, tk), lambda j, i, k: (i, k))
        if b_blocked:
            b_spec = pl.BlockSpec((None, tn, tk), lambda j, i, k: (k, j, 0))
        else:
            b_spec = pl.BlockSpec((tn, tk), lambda j, i, k: (j + b_noff, k))
        dn = NT
    else:
        a_spec = pl.BlockSpec((tk, tm), lambda j, i, k: (k, i))
        b_spec = pl.BlockSpec((tk, tn), lambda j, i, k: (k, j + b_noff))
        dn = TN
    if out_blocked:
        out_shape = jax.ShapeDtypeStruct((nn, M, tn), out_dtype)
        out_spec = pl.BlockSpec((None, tm, tn), lambda j, i, k: (j, i, 0))
    else:
        out_shape = jax.ShapeDtypeStruct((M, N), out_dtype)
        out_spec = pl.BlockSpec((tm, tn), lambda j, i, k: (i, j))

    def body(a_ref, b_ref, o_ref, *scratch):
        if nk == 1:
            o_ref[...] = _dot(a_ref[...], b_ref[...], dn).astype(out_dtype)
        else:
            acc_ref, = scratch
            k = pl.program_id(2)

            @pl.when(k == 0)
            def _():
                acc_ref[...] = jnp.zeros_like(acc_ref)

            acc_ref[...] += _dot(a_ref[...], b_ref[...], dn)

            @pl.when(k == nk - 1)
            def _():
                o_ref[...] = acc_ref[...].astype(out_dtype)

    return pl.pallas_call(
        body, name=name, grid=(nn, nm, nk), in_specs=[a_spec, b_spec], out_specs=out_spec, out_shape=out_shape,
        scratch_shapes=[] if nk == 1 else [pltpu.VMEM((tm, tn), F32)],
        compiler_params=_params(("parallel", "parallel", "arbitrary"), vmem_mib),
    )(a, b)


def _rms_fwd(x, g, name):
    R, Dm = x.shape
    tr = min(R, 256)

    def body(x_ref, g_ref, h_ref):
        xv = x_ref[...]
        r = lax.rsqrt(jnp.mean(xv * xv, axis=-1, keepdims=True) + EPS)
        h_ref[...] = (xv * r * g_ref[...]).astype(BF)

    return pl.pallas_call(
        body, name=name, grid=(R // tr,),
        in_specs=[pl.BlockSpec((tr, Dm), lambda i: (i, 0)), pl.BlockSpec((1, Dm), lambda i: (0, 0))],
        out_specs=pl.BlockSpec((tr, Dm), lambda i: (i, 0)), out_shape=jax.ShapeDtypeStruct((R, Dm), BF),
        compiler_params=_params(("parallel",)),
    )(x, g)


def _rms_gain_grad(dn, x, name):
    R, Dm = x.shape

    def body(dn_ref, x_ref, o_ref):
        xv = x_ref[...]
        r = lax.rsqrt(jnp.mean(xv * xv, axis=-1, keepdims=True) + EPS)
        o_ref[...] = jnp.sum(dn_ref[...] * xv * r, axis=0, keepdims=True)

    return pl.pallas_call(
        body, name=name, out_shape=jax.ShapeDtypeStruct((1, Dm), F32),
        compiler_params=pltpu.CompilerParams(vmem_limit_bytes=32 * MIB),
    )(dn, x)


def _shift_down(v, k, head8, row, T):
    if k == 0:
        return v
    r = pltpu.roll(v, k, 0)
    hr = pltpu.roll(head8, k, 0)
    top = jnp.where(row[:8] < k, hr, r[:8])
    return jnp.concatenate([top, r[8:]], axis=0)


def _shift_up(v, k, tail8, row, T):
    if k == 0:
        return v
    r = pltpu.roll(v, T - k, 0)
    tr = pltpu.roll(tail8, 8 - k, 0)
    bot = jnp.where(row[:8] >= 8 - k, tr, r[T - 8:])
    return jnp.concatenate([r[:T - 8], bot], axis=0)


def _rglru_gates(u, head8, grow, row, T, cw_ref, cb_ref, wa_ref, ba_ref, wx_ref, bx_ref, lam_ref):
    us = [_shift_down(u, k, head8, row, T) for k in range(CONV_W)]
    acc = us[0] * cw_ref[0:1, :]
    for k in range(1, CONV_W):
        acc = acc + us[k] * cw_ref[k:k + 1, :]
    conv = cb_ref[...] + acc
    cbf = conv.astype(BF)
    r_ = _sigmoid(_dot(cbf, wa_ref[0], NN) + ba_ref[...])
    i_ = _sigmoid(_dot(cbf, wx_ref[0], NN) + bx_ref[...])
    sp = _softplus(-lam_ref[...])
    la = -LRU_C * r_ * sp
    a = jnp.exp(la)
    mult_raw = jnp.sqrt(-_expm1(2.0 * la))
    mult = jnp.where(grow == 0, 1.0, mult_raw)
    return us, conv, cbf, r_, i_, sp, a, mult_raw, mult


def _rglru_specs(T, nt, rev):
    tmap = (lambda n, t: (nt - 1 - t, n)) if rev else (lambda n, t: (t, n))
    hmap = ((lambda n, t: (jnp.maximum((nt - 1 - t) * (T // 8) - 1, 0), n)) if rev
            else (lambda n, t: (jnp.maximum(t * (T // 8) - 1, 0), n)))
    tile = pl.BlockSpec((T, RNN_BLOCK), tmap)
    halo = pl.BlockSpec((8, RNN_BLOCK), hmap)
    vec = pl.BlockSpec((1, RNN_BLOCK), lambda n, t: (0, n))
    cw = pl.BlockSpec((CONV_W, RNN_BLOCK), lambda n, t: (0, n))
    wblk = pl.BlockSpec((1, RNN_BLOCK, RNN_BLOCK), lambda n, t: (n, 0, 0))
    return tile, halo, vec, cw, wblk


def _rglru_fwd(xr, g, cw, cb, wa, ba, wx, bx, lam, T):
    S = xr.shape[0]
    nt = S // T

    def body(u_ref, uh_ref, g_ref, cw_ref, cb_ref, wa_ref, ba_ref, wx_ref, bx_ref, lam_ref, h_ref, y_ref, carry):
        t = pl.program_id(1)

        @pl.when(t == 0)
        def _():
            carry[...] = jnp.zeros_like(carry)

        row = lax.broadcasted_iota(jnp.int32, (T, RNN_BLOCK), 0)
        grow = row + t * T
        head8 = jnp.where(t > 0, uh_ref[...], 0.0)
        _, conv, _, _, i_, _, a, _, mult = _rglru_gates(u_ref[...], head8, grow, row, T, cw_ref, cb_ref, wa_ref, ba_ref,
                                                         wx_ref, bx_ref, lam_ref)
        b = mult * i_ * conv
        s = 1
        while s < T:
            keep = row >= s
            a_s = jnp.where(keep, pltpu.roll(a, s, 0), 1.0)
            b_s = jnp.where(keep, pltpu.roll(b, s, 0), 0.0)
            b = a * b_s + b
            a = a * a_s
            s *= 2
        h = b + a * carry[0:1, :]
        carry[...] = jnp.broadcast_to(h[T - 1:T, :], carry.shape)
        h_ref[...] = h
        gv = g_ref[...]
        y_ref[...] = (h * (gv * _sigmoid(gv))).astype(BF)

    tile, halo, vec, cwspec, wblk = _rglru_specs(T, nt, False)
    return pl.pallas_call(
        body, name="rglru_fwd", grid=(RNN_BLOCKS, nt),
        in_specs=[tile, halo, tile, cwspec, vec, wblk, vec, wblk, vec, vec],
        out_specs=[tile, tile],
        out_shape=[jax.ShapeDtypeStruct((S, D_RNN), F32), jax.ShapeDtypeStruct((S, D_RNN), BF)],
        scratch_shapes=[pltpu.VMEM((8, RNN_BLOCK), F32)],
        compiler_params=_params(("parallel", "arbitrary")),
    )(xr, xr, g, cw, cb, wa, ba, wx, bx, lam)


def _rglru_bwd(xr, g, h, dy, cw, cb, wa, ba, wx, bx, lam, T):
    S = xr.shape[0]
    nt = S // T

    def body(u_ref, uh_ref, g_ref, h_ref, hh_ref, dy_ref, cw_ref, cb_ref, wa_ref, ba_ref, wx_ref, bx_ref, lam_ref,
             du_ref, dg_ref, dwa_ref, dwx_ref, dvec_ref, c_dhh, c_a, c_dconv):
        t = pl.program_id(1)
        tt = nt - 1 - t

        @pl.when(t == 0)
        def _():
            c_dhh[...] = jnp.zeros_like(c_dhh)
            c_a[...] = jnp.zeros_like(c_a)
            c_dconv[...] = jnp.zeros_like(c_dconv)
            dwa_ref[...] = jnp.zeros_like(dwa_ref)
            dwx_ref[...] = jnp.zeros_like(dwx_ref)
            dvec_ref[...] = jnp.zeros_like(dvec_ref)

        row = lax.broadcasted_iota(jnp.int32, (T, RNN_BLOCK), 0)
        row8 = row[:8]
        grow = row + tt * T
        head8 = jnp.where(tt > 0, uh_ref[...], 0.0)
        us, conv, cbf, r_, i_, sp, a, mult_raw, mult = _rglru_gates(
            u_ref[...], head8, grow, row, T, cw_ref, cb_ref, wa_ref, ba_ref, wx_ref, bx_ref, lam_ref)
        hv = h_ref[...]
        hprev = _shift_down(hv, 1, jnp.where(tt > 0, hh_ref[...], 0.0), row, T)
        gv = g_ref[...]
        sg = _sigmoid(gv)
        dyv = dy_ref[...]
        dg_ref[...] = (dyv * hv * (sg * (1.0 + gv * (1.0 - sg)))).astype(BF)
        d = dyv * (gv * sg)
        A = _shift_up(a, 1, c_a[...], row, T)
        s = 1
        while s < T:
            keep = row < T - s
            A_s = jnp.where(keep, pltpu.roll(A, T - s, 0), 1.0)
            d_s = jnp.where(keep, pltpu.roll(d, T - s, 0), 0.0)
            d = A * d_s + d
            A = A * A_s
            s *= 2
        dhh = d + A * c_dhh[0:1, :]
        da = dhh * hprev
        dconv = dhh * mult * i_
        di = dhh * mult * conv
        dmult = dhh * i_ * conv
        dla = da * a - jnp.where(grow == 0, 0.0, dmult * (a * a) / mult_raw)
        dr = dla * (-LRU_C * sp)
        dsp = jnp.sum(dla * (-LRU_C * r_), axis=0, keepdims=True)
        dza = dr * r_ * (1.0 - r_)
        dzx = di * i_ * (1.0 - i_)
        dza_b, dzx_b = dza.astype(BF), dzx.astype(BF)
        dconv = dconv + _dot(dza_b, wa_ref[0], NT) + _dot(dzx_b, wx_ref[0], NT)
        dwa_ref[0] += _dot(cbf, dza_b, TN)
        dwx_ref[0] += _dot(cbf, dzx_b, TN)
        lam = lam_ref[...]
        rows = [jnp.sum(dconv * us[k], axis=0, keepdims=True) for k in range(CONV_W)]
        rows += [jnp.sum(dconv, axis=0, keepdims=True), jnp.sum(dza, axis=0, keepdims=True),
                 jnp.sum(dzx, axis=0, keepdims=True), dsp * (-_sigmoid(-lam))]
        upd = jnp.zeros((8, RNN_BLOCK), F32)
        for j, rv in enumerate(rows):
            upd = upd + jnp.where(row8 == j, rv, 0.0)
        dvec_ref[...] += upd
        tail8 = c_dconv[...]
        du = dconv * cw_ref[0:1, :]
        for k in range(1, CONV_W):
            du = du + _shift_up(dconv, k, tail8, row, T) * cw_ref[k:k + 1, :]
        du_ref[...] = du.astype(BF)
        c_dhh[...] = jnp.broadcast_to(dhh[0:1, :], c_dhh.shape)
        c_a[...] = jnp.broadcast_to(a[0:1, :], c_a.shape)
        c_dconv[...] = dconv[:8]

    tile, halo, vec, cwspec, wblk = _rglru_specs(T, nt, True)
    acc8 = pl.BlockSpec((8, RNN_BLOCK), lambda n, t: (0, n))
    return pl.pallas_call(
        body, name="rglru_bwd", grid=(RNN_BLOCKS, nt),
        in_specs=[tile, halo, tile, tile, halo, tile, cwspec, vec, wblk, vec, wblk, vec, vec],
        out_specs=[tile, tile, wblk, wblk, acc8],
        out_shape=[jax.ShapeDtypeStruct((S, D_RNN), BF), jax.ShapeDtypeStruct((S, D_RNN), BF),
                   jax.ShapeDtypeStruct((RNN_BLOCKS, RNN_BLOCK, RNN_BLOCK), F32),
                   jax.ShapeDtypeStruct((RNN_BLOCKS, RNN_BLOCK, RNN_BLOCK), F32),
                   jax.ShapeDtypeStruct((8, D_RNN), F32)],
        scratch_shapes=[pltpu.VMEM((8, RNN_BLOCK), F32)] * 3,
        compiler_params=_params(("parallel", "arbitrary")),
    )(xr, xr, g, h, h, dy, cw, cb, wa, ba, wx, bx, lam)


def _rel_bucket_map():
    qi = np.arange(WINDOW)[:, None]
    kj = np.arange(2 * WINDOW)[None, :]
    dist = jnp.asarray(qi + WINDOW - kj, jnp.int32)
    n = jnp.maximum(dist, 0)
    max_exact = REL_BUCKETS // 2
    ratio = jnp.log(jnp.maximum(n, 1).astype(F32) / max_exact) / math.log(REL_MAX_DIST / max_exact)
    large = jnp.minimum(max_exact + (ratio * (REL_BUCKETS - max_exact)).astype(jnp.int32), REL_BUCKETS - 1)
    return jnp.where(n < max_exact, n, large).astype(jnp.int32)


def _swa_common(n, kv_ref, bucket_ref, relb_ref, bias_scr):
    @pl.when(n == 0)
    def _():
        bk = bucket_ref[...]
        for h in range(SWA_HEADS):
            acc = jnp.zeros((WINDOW, 2 * WINDOW), F32)
            for b in range(REL_BUCKETS):
                acc = acc + jnp.where(bk == b, relb_ref[b, h], 0.0)
            bias_scr[h] = acc

    prev0 = pl.multiple_of(jnp.maximum(n - 1, 0) * WINDOW, WINDOW)
    cur0 = pl.multiple_of(n * WINDOW, WINDOW)
    kk = jnp.concatenate([kv_ref[pl.ds(prev0, WINDOW), :], kv_ref[pl.ds(cur0, WINDOW), :]], axis=0).astype(F32)
    rowi = lax.broadcasted_iota(jnp.int32, (WINDOW, 2 * WINDOW), 0)
    col = lax.broadcasted_iota(jnp.int32, (WINDOW, 2 * WINDOW), 1)
    no_prev = jnp.where(n > 0, 0, 4 * WINDOW)
    valid = jnp.logical_or(jnp.logical_and(col < WINDOW, col > rowi + no_prev),
                           jnp.logical_and(col >= WINDOW, (col - WINDOW) <= rowi))
    return kk, valid, prev0, cur0


def _half_pair(part, kvh):
    lo = lax.broadcasted_iota(jnp.int32, part.shape, 1) < SWA_HD
    if kvh == 0:
        pa = jnp.where(lo, part, 0.0)
        pb = pltpu.roll(pa, SWA_HD, 1)
    else:
        pb = jnp.where(lo, 0.0, part)
        pa = pltpu.roll(pb, SWA_HD, 1)
    return pa.astype(BF), pb.astype(BF)


def _swa_probs(q2, kx, bias, sink, valid):
    lg = _dot(q2, kx, NT) * (SWA_HD ** -0.5) + bias
    lg = jnp.where(valid, lg, NEG_INF)
    m = jnp.maximum(jnp.max(lg, axis=-1, keepdims=True), sink)
    e = jnp.exp(lg - m)
    es = jnp.exp(sink - m)
    den = jnp.sum(e, axis=-1, keepdims=True) + es
    return e / den, es / den


def _swa_fwd(q, kv, g, bucket, rel_bias, sinks):
    S = q.shape[0]
    nb = S // WINDOW

    def body(q_ref, kv_ref, g_ref, bucket_ref, relb_ref, sink_ref, o_ref, y_ref, bias_scr):
        n = pl.program_id(0)
        kk, valid, _, _ = _swa_common(n, kv_ref, bucket_ref, relb_ref, bias_scr)
        for kvh in range(SWA_KV_HEADS):
            ka, kb = _half_pair(kk[:, :128], kvh)
            va, vb = _half_pair(kk[:, 128:], kvh)
            for p in range(4):
                c0 = kvh * 512 + p * 128
                h0 = kvh * 8 + 2 * p
                q2 = q_ref[:, c0:c0 + 128]
                p0, _ = _swa_probs(q2, ka, bias_scr[h0], sink_ref[0, h0], valid)
                p1, _ = _swa_probs(q2, kb, bias_scr[h0 + 1], sink_ref[0, h0 + 1], valid)
                o2 = _dot(p0.astype(BF), va, NN) + _dot(p1.astype(BF), vb, NN)
                o_ref[:, c0:c0 + 128] = o2
                gv = g_ref[:, c0:c0 + 128]
                y_ref[:, c0:c0 + 128] = (o2 * (gv * _sigmoid(gv))).astype(BF)

    blk = pl.BlockSpec((WINDOW, 1024), lambda n: (n, 0))
    smem = pl.BlockSpec(memory_space=pltpu.SMEM)
    return pl.pallas_call(
        body, name="swa_fwd", grid=(nb,),
        in_specs=[blk, pl.BlockSpec((S, 256), lambda n: (0, 0)), blk, pl.BlockSpec((WINDOW, 2 * WINDOW), lambda n: (0, 0)), smem, smem],
        out_specs=[blk, blk],
        out_shape=[jax.ShapeDtypeStruct((S, 1024), F32), jax.ShapeDtypeStruct((S, 1024), BF)],
        scratch_shapes=[pltpu.VMEM((SWA_HEADS, WINDOW, 2 * WINDOW), F32)],
        compiler_params=_params(("arbitrary",)),
    )(q, kv, g, bucket, rel_bias, sinks)


def _swa_bwd(q, kv, g, o, dy, bucket, rel_bias, sinks):
    S = q.shape[0]
    nb = S // WINDOW

    def body(q_ref, kv_ref, g_ref, o_ref, dy_ref, bucket_ref, relb_ref, sink_ref,
             dq_ref, dg_ref, dkv_ref, dsink_ref, drel_ref, bias_scr, dbias_scr, dsink_scr):
        n = pl.program_id(0)

        @pl.when(n == 0)
        def _():
            dbias_scr[...] = jnp.zeros_like(dbias_scr)
            dsink_scr[...] = jnp.zeros_like(dsink_scr)
            dkv_ref[...] = jnp.zeros_like(dkv_ref)

        kk, valid, prev0, cur0 = _swa_common(n, kv_ref, bucket_ref, relb_ref, bias_scr)
        lane = lax.broadcasted_iota(jnp.int32, (WINDOW, 128), 1)
        lo256 = lax.broadcasted_iota(jnp.int32, (2 * WINDOW, 128), 1) < SWA_HD
        dks, dvs = [], []
        for kvh in range(SWA_KV_HEADS):
            ka, kb = _half_pair(kk[:, :128], kvh)
            va, vb = _half_pair(kk[:, 128:], kvh)
            dka = jnp.zeros((2 * WINDOW, 128), F32)
            dkb, dva, dvb = dka, dka, dka
            for p in range(4):
                c0 = kvh * 512 + p * 128
                h0 = kvh * 8 + 2 * p
                q2 = q_ref[:, c0:c0 + 128]
                gv = g_ref[:, c0:c0 + 128]
                sg = _sigmoid(gv)
                dyv = dy_ref[:, c0:c0 + 128]
                dg_ref[:, c0:c0 + 128] = (dyv * o_ref[:, c0:c0 + 128] * (sg * (1.0 + gv * (1.0 - sg)))).astype(BF)
                do2 = (dyv * (gv * sg)).astype(BF)
                dq2 = jnp.zeros((WINDOW, 128), F32)
                for half, (kx, vx) in enumerate(((ka, va), (kb, vb))):
                    hh = h0 + half
                    pr, ps = _swa_probs(q2, kx, bias_scr[hh], sink_ref[0, hh], valid)
                    dp = _dot(do2, vx, NT)
                    delta = jnp.sum(pr * dp, axis=-1, keepdims=True)
                    ds = pr * (dp - delta)
                    dbias_scr[hh] += ds
                    dsink_scr[...] += jnp.where(lane == hh, ps * delta, 0.0)
                    dsb = (ds * (SWA_HD ** -0.5)).astype(BF)
                    prb = pr.astype(BF)
                    dq2 = dq2 + _dot(dsb, kx, NN)
                    if half == 0:
                        dka = dka + _dot(dsb, q2, TN)
                        dva = dva + _dot(prb, do2, TN)
                    else:
                        dkb = dkb + _dot(dsb, q2, TN)
                        dvb = dvb + _dot(prb, do2, TN)
                dq_ref[:, c0:c0 + 128] = dq2.astype(BF)
            dks.append(jnp.where(lo256, dka, 0.0) + pltpu.roll(jnp.where(lo256, 0.0, dkb), SWA_HD, 1))
            dvs.append(jnp.where(lo256, dva, 0.0) + pltpu.roll(jnp.where(lo256, 0.0, dvb), SWA_HD, 1))
        dk = dks[0] + pltpu.roll(dks[1], SWA_HD, 1)
        dv = dvs[0] + pltpu.roll(dvs[1], SWA_HD, 1)
        dkv_ref[pl.ds(prev0, WINDOW), 0:128] += dk[:WINDOW]
        dkv_ref[pl.ds(prev0, WINDOW), 128:256] += dv[:WINDOW]
        dkv_ref[pl.ds(cur0, WINDOW), 0:128] += dk[WINDOW:]
        dkv_ref[pl.ds(cur0, WINDOW), 128:256] += dv[WINDOW:]

        @pl.when(n == nb - 1)
        def _():
            dsink_ref[...] = -jnp.sum(dsink_scr[...], axis=0, keepdims=True)
            bk = bucket_ref[...]
            r32 = lax.broadcasted_iota(jnp.int32, (REL_BUCKETS, 128), 0)
            l32 = lax.broadcasted_iota(jnp.int32, (REL_BUCKETS, 128), 1)
            acc = jnp.zeros((REL_BUCKETS, 128), F32)
            for b in range(REL_BUCKETS):
                mb = bk == b
                for h in range(SWA_HEADS):
                    t1 = jnp.sum(jnp.where(mb, dbias_scr[h], 0.0), axis=1, keepdims=True)
                    val = jnp.sum(t1, axis=0, keepdims=True)
                    acc = acc + jnp.where(jnp.logical_and(r32 == b, l32 == h), val, 0.0)
            drel_ref[...] = acc

    blk = pl.BlockSpec((WINDOW, 1024), lambda n: (n, 0))
    smem = pl.BlockSpec(memory_space=pltpu.SMEM)
    whole = lambda shape: pl.BlockSpec(shape, lambda n: (0, 0))
    return pl.pallas_call(
        body, name="swa_bwd", grid=(nb,),
        in_specs=[blk, whole((S, 256)), blk, blk, blk, whole((WINDOW, 2 * WINDOW)), smem, smem],
        out_specs=[blk, blk, whole((S, 256)), whole((1, 128)), whole((REL_BUCKETS, 128))],
        out_shape=[jax.ShapeDtypeStruct((S, 1024), BF), jax.ShapeDtypeStruct((S, 1024), BF),
                   jax.ShapeDtypeStruct((S, 256), F32), jax.ShapeDtypeStruct((1, 128), F32),
                   jax.ShapeDtypeStruct((REL_BUCKETS, 128), F32)],
        scratch_shapes=[pltpu.VMEM((SWA_HEADS, WINDOW, 2 * WINDOW), F32), pltpu.VMEM((SWA_HEADS, WINDOW, 2 * WINDOW), F32),
                        pltpu.VMEM((WINDOW, 128), F32)],
        compiler_params=_params(("arbitrary",)),
    )(q, kv, g, o, dy, bucket, rel_bias, sinks)


def _mem_probs(qh, mk):
    lg = _dot(qh, mk, NT) * (MEM_HD ** -0.5)
    e = jnp.exp(lg - jnp.max(lg, axis=-1, keepdims=True))
    return e / jnp.sum(e, axis=-1, keepdims=True)


def _mem_fwd(q, mkv, g):
    S = q.shape[0]
    M = mkv.shape[0]
    tq = 256

    def body(q_ref, mkv_ref, g_ref, o_ref, y_ref):
        for h in range(MEM_HEADS):
            c0 = h * MEM_HD
            pr = _mem_probs(q_ref[:, c0:c0 + MEM_HD], mkv_ref[:, c0:c0 + MEM_HD])
            o = _dot(pr.astype(BF), mkv_ref[:, D_MEM + c0:D_MEM + c0 + MEM_HD], NN)
            o_ref[:, c0:c0 + MEM_HD] = o
            gv = g_ref[:, c0:c0 + MEM_HD]
            y_ref[:, c0:c0 + MEM_HD] = (o * (gv * _sigmoid(gv))).astype(BF)

    blk = pl.BlockSpec((tq, D_MEM), lambda i: (i, 0))
    return pl.pallas_call(
        body, name="mem_fwd", grid=(S // tq,),
        in_specs=[blk, pl.BlockSpec((M, 2 * D_MEM), lambda i: (0, 0)), blk], out_specs=[blk, blk],
        out_shape=[jax.ShapeDtypeStruct((S, D_MEM), F32), jax.ShapeDtypeStruct((S, D_MEM), BF)],
        compiler_params=_params(("parallel",)),
    )(q, mkv, g)


def _mem_bwd(q, mkv, g, o, dy):
    S = q.shape[0]
    M = mkv.shape[0]
    tq = 256

    def body(q_ref, mkv_ref, g_ref, o_ref, dy_ref, dq_ref, dg_ref, dmkv_ref):
        @pl.when(pl.program_id(0) == 0)
        def _():
            dmkv_ref[...] = jnp.zeros_like(dmkv_ref)

        for h in range(MEM_HEADS):
            c0 = h * MEM_HD
            qh = q_ref[:, c0:c0 + MEM_HD]
            mk = mkv_ref[:, c0:c0 + MEM_HD]
            mv = mkv_ref[:, D_MEM + c0:D_MEM + c0 + MEM_HD]
            gv = g_ref[:, c0:c0 + MEM_HD]
            sg = _sigmoid(gv)
            dyv = dy_ref[:, c0:c0 + MEM_HD]
            dg_ref[:, c0:c0 + MEM_HD] = (dyv * o_ref[:, c0:c0 + MEM_HD] * (sg * (1.0 + gv * (1.0 - sg)))).astype(BF)
            do = (dyv * (gv * sg)).astype(BF)
            pr = _mem_probs(qh, mk)
            dp = _dot(do, mv, NT)
            ds = pr * (dp - jnp.sum(pr * dp, axis=-1, keepdims=True))
            dsb = (ds * (MEM_HD ** -0.5)).astype(BF)
            dq_ref[:, c0:c0 + MEM_HD] = _dot(dsb, mk, NN).astype(BF)
            dmkv_ref[:, c0:c0 + MEM_HD] += _dot(dsb, qh, TN)
            dmkv_ref[:, D_MEM + c0:D_MEM + c0 + MEM_HD] += _dot(pr.astype(BF), do, TN)

    blk = pl.BlockSpec((tq, D_MEM), lambda i: (i, 0))
    whole = pl.BlockSpec((M, 2 * D_MEM), lambda i: (0, 0))
    return pl.pallas_call(
        body, name="mem_bwd", grid=(S // tq,),
        in_specs=[blk, whole, blk, blk, blk], out_specs=[blk, blk, whole],
        out_shape=[jax.ShapeDtypeStruct((S, D_MEM), BF), jax.ShapeDtypeStruct((S, D_MEM), BF),
                   jax.ShapeDtypeStruct((M, 2 * D_MEM), F32)],
        compiler_params=_params(("arbitrary",)),
    )(q, mkv, g, o, dy)


def _merge_specs(tm):
    ytile = pl.BlockSpec((tm, 1024), lambda i, j: (i, 0))
    wblk = pl.BlockSpec((None, 1024, 256), lambda i, j: (j, 0, 0))
    gls = [pl.BlockSpec((tm, 256), (lambda i, j, br=br: (i, br * 8 + j))) for br in range(3)]
    otile = pl.BlockSpec((tm, 256), lambda i, j: (i, j))
    return ytile, wblk, gls, otile


def _merge_fwd(ys, ws, gl, tm):
    S = gl.shape[0]

    def body(y0, y1, y2, w0, w1, w2, g0, g1, g2, o_ref):
        acc = None
        for y_ref, w_ref, g_ref in ((y0, w0, g0), (y1, w1, g1), (y2, w2, g2)):
            term = _sigmoid(g_ref[...]) * _dot(y_ref[...], w_ref[...], NN)
            acc = term if acc is None else acc + term
        o_ref[...] = acc.astype(BF)

    ytile, wblk, gls, otile = _merge_specs(tm)
    return pl.pallas_call(
        body, name="merge_fwd", grid=(S // tm, 8),
        in_specs=[ytile] * 3 + [wblk] * 3 + gls, out_specs=otile,
        out_shape=jax.ShapeDtypeStruct((S, D_MODEL), BF),
        compiler_params=_params(("parallel", "arbitrary")),
    )(*ys, *ws, gl, gl, gl)


def _merge_bwd(dout, w_out, ys, ws, gl, tm):
    S = gl.shape[0]

    def body(do_ref, wo_ref, y0, y1, y2, w0, w1, w2, g0, g1, g2, dg0, dg1, dg2, dp0, dp1, dp2):
        dm = _dot(do_ref[...], wo_ref[...], NT)
        for y_ref, w_ref, g_ref, dg_ref, dp_ref in ((y0, w0, g0, dg0, dp0), (y1, w1, g1, dg1, dp1), (y2, w2, g2, dg2, dp2)):
            gate = _sigmoid(g_ref[...])
            pv = _dot(y_ref[...], w_ref[...], NN)
            dg_ref[...] = (dm * pv * gate * (1.0 - gate)).astype(BF)
            dp_ref[...] = (dm * gate).astype(BF)

    ytile, wblk, gls, otile = _merge_specs(tm)
    out = jax.ShapeDtypeStruct((S, D_MODEL), BF)
    return pl.pallas_call(
        body, name="merge_bwd", grid=(S // tm, 8),
        in_specs=[pl.BlockSpec((tm, D_MODEL), lambda i, j: (i, 0)), pl.BlockSpec((256, D_MODEL), lambda i, j: (j, 0))]
        + [ytile] * 3 + [wblk] * 3 + gls,
        out_specs=[otile] * 6, out_shape=[out] * 6,
        compiler_params=_params(("parallel", "arbitrary")),
    )(dout, w_out, *ys, *ws, gl, gl, gl)


def _out_loss(merged, w_out, x, target, post_g, tm):
    S = x.shape[0]

    def body(m_ref, w_ref, x_ref, t_ref, g_ref, dout_ref, dy_ref, loss_ref, dpost_ref):
        @pl.when(pl.program_id(0) == 0)
        def _():
            loss_ref[...] = jnp.zeros_like(loss_ref)
            dpost_ref[...] = jnp.zeros_like(dpost_ref)

        out = _dot(m_ref[...], w_ref[...], NN)
        r = lax.rsqrt(jnp.mean(out * out, axis=-1, keepdims=True) + EPS)
        nrm = out * r
        gv = g_ref[...]
        err = (x_ref[...] + nrm * gv) - t_ref[...]
        sq = jnp.sum(jnp.sum(err * err, axis=1, keepdims=True), axis=0, keepdims=True)
        loss_ref[...] += sq * (0.5 / D_MODEL)
        dy = err * (1.0 / D_MODEL)
        dy_ref[...] = dy
        dpost_ref[...] += jnp.sum(dy * nrm, axis=0, keepdims=True)
        dn = dy * gv
        dout_ref[...] = (r * (dn - nrm * jnp.mean(dn * nrm, axis=-1, keepdims=True))).astype(BF)

    row = pl.BlockSpec((tm, D_MODEL), lambda i: (i, 0))
    return pl.pallas_call(
        body, name="out_loss", grid=(S // tm,),
        in_specs=[row, pl.BlockSpec((D_MODEL, D_MODEL), lambda i: (0, 0)), row, row, pl.BlockSpec((1, D_MODEL), lambda i: (0, 0))],
        out_specs=[row, row, pl.BlockSpec((8, 128), lambda i: (0, 0)), pl.BlockSpec((1, D_MODEL), lambda i: (0, 0))],
        out_shape=[jax.ShapeDtypeStruct((S, D_MODEL), BF), jax.ShapeDtypeStruct((S, D_MODEL), F32),
                   jax.ShapeDtypeStruct((8, 128), F32), jax.ShapeDtypeStruct((1, D_MODEL), F32)],
        compiler_params=_params(("arbitrary",)),
    )(merged, w_out, x, target, post_g)


def _dh_dx(dproj, w_in, x, dy, pre_g, tm, tk):
    S = x.shape[0]
    nk = D_IN // tk

    def body(dp_ref, w_ref, x_ref, dy_ref, g_ref, dx_ref, dpre_ref, acc_ref):
        i, k = pl.program_id(0), pl.program_id(1)

        @pl.when(jnp.logical_and(i == 0, k == 0))
        def _():
            dpre_ref[...] = jnp.zeros_like(dpre_ref)

        @pl.when(k == 0)
        def _():
            acc_ref[...] = jnp.zeros_like(acc_ref)

        acc_ref[...] += _dot(dp_ref[...], w_ref[...], NT)

        @pl.when(k == nk - 1)
        def _():
            dh = acc_ref[...]
            xv = x_ref[...]
            r = lax.rsqrt(jnp.mean(xv * xv, axis=-1, keepdims=True) + EPS)
            nrm = xv * r
            dpre_ref[...] += jnp.sum(dh * nrm, axis=0, keepdims=True)
            dn = dh * g_ref[...]
            dx_ref[...] = r * (dn - nrm * jnp.mean(dn * nrm, axis=-1, keepdims=True)) + dy_ref[...]

    row = pl.BlockSpec((tm, D_MODEL), lambda i, k: (i, 0))
    vec = pl.BlockSpec((1, D_MODEL), lambda i, k: (0, 0))
    return pl.pallas_call(
        body, name="dh_dx", grid=(S // tm, nk),
        in_specs=[pl.BlockSpec((tm, tk), lambda i, k: (i, k)), pl.BlockSpec((D_MODEL, tk), lambda i, k: (0, k)), row, row, vec],
        out_specs=[row, vec],
        out_shape=[jax.ShapeDtypeStruct((S, D_MODEL), F32), jax.ShapeDtypeStruct((1, D_MODEL), F32)],
        scratch_shapes=[pltpu.VMEM((tm, D_MODEL), F32)],
        compiler_params=_params(("arbitrary", "arbitrary"), 56),
    )(dproj, w_in, x, dy, pre_g)


def _sum_parts(parts, name):
    P, R, C = parts.shape
    tr = 8
    for cand in (512, 440, 256, 200, 128, 64, 40, 8):
        if R % cand == 0:
            tr = cand
            break

    def body(p_ref, o_ref):
        acc = p_ref[0]
        for j in range(1, P):
            acc = acc + p_ref[j]
        o_ref[...] = acc

    return pl.pallas_call(
        body, name=name, grid=(R // tr,),
        in_specs=[pl.BlockSpec((P, tr, C), lambda i: (0, i, 0))], out_specs=pl.BlockSpec((tr, C), lambda i: (i, 0)),
        out_shape=jax.ShapeDtypeStruct((R, C), F32), compiler_params=_params(("parallel",)),
    )(parts)


def _adamw(parts, w, m, v, name):
    P, R, C = parts.shape
    tr = 128 if R % 128 == 0 else R
    c1 = 1.0 - ADAM_B1 ** ADAM_STEP
    c2 = 1.0 - ADAM_B2 ** ADAM_STEP

    def body(p_ref, w_ref, m_ref, v_ref, g_ref, d_ref, nm_ref, nv_ref):
        g = p_ref[0].astype(F32)
        for j in range(1, P):
            g = g + p_ref[j].astype(F32)
        nm = ADAM_B1 * m_ref[...] + (1.0 - ADAM_B1) * g
        nv = ADAM_B2 * v_ref[...] + (1.0 - ADAM_B2) * (g * g)
        g_ref[...] = g
        nm_ref[...] = nm
        nv_ref[...] = nv
        d_ref[...] = -ADAM_LR * ((nm / c1) / (jnp.sqrt(nv / c2) + ADAM_EPS) + ADAM_WD * w_ref[...])

    tile = pl.BlockSpec((tr, C), lambda i: (i, 0))
    out = jax.ShapeDtypeStruct((R, C), F32)
    return pl.pallas_call(
        body, name=name, grid=(R // tr,),
        in_specs=[pl.BlockSpec((P, tr, C), lambda i: (0, i, 0)), tile, tile, tile], out_specs=[tile] * 4, out_shape=[out] * 4,
        compiler_params=_params(("parallel",)),
    )(parts, w, m, v)


def _local_step(x, mem, target, pre_g, post_g, mem_g, w_in, conv_w, conv_b, w_a, b_a, w_x, b_x, lam, sinks, rel_bias,
                w_memkv, wbr, w_out):
    S = x.shape[0]
    M = mem.shape[0]
    tm = min(512, S)
    T = min(512, S // 2)
    bucket = _rel_bucket_map()

    h = _rms_fwd(x, pre_g, "pre_norm")
    memn = _rms_fwd(mem, mem_g, "mem_norm")
    seg = {}
    for name, c0, width, dt in SEGMENTS:
        seg[name] = _matmul(h, w_in, "nn", S, width, D_MODEL, S, SEG_TILE, D_MODEL, dt, "proj_" + name, b_noff=c0 // SEG_TILE)
    mkv = _matmul(memn, w_memkv, "nn", M, 2 * D_MEM, D_MODEL, M, 512, D_MODEL, BF, "mem_kv")

    h_rg, y_rg = _rglru_fwd(seg["xr"], seg["g_rg"], conv_w, conv_b, w_a, b_a, w_x, b_x, lam, T)
    o_swa, y_swa = _swa_fwd(seg["q_s"], seg["kv"], seg["g_swa"], bucket, rel_bias, sinks)
    o_mem, y_mem = _mem_fwd(seg["q_m"], mkv, seg["g_mem"])
    ys = (y_rg, y_swa, y_mem)
    merged = _merge_fwd(ys, wbr, seg["gl"], tm)
    dout, dy, loss, dpost = _out_loss(merged, w_out, x, target, post_g, min(256, S))

    dw_out = _matmul(merged, dout, "tn", D_MODEL, D_MODEL, S, 512, D_MODEL, S, BF, "dw_out")
    dgl0, dgl1, dgl2, dp0, dp1, dp2 = _merge_bwd(dout, w_out, ys, wbr, seg["gl"], tm)
    dys, dwbr = [], []
    for i, dp in enumerate((dp0, dp1, dp2)):
        dys.append(_matmul(dp, wbr[i], "nt", S, 1024, D_MODEL, tm, 1024, 256, F32, "dy_br%d" % i, b_blocked=True))
        dwbr.append(_matmul(ys[i], dp, "tn", 1024, D_MODEL, S, 1024, 256, S, BF, "dw_br%d" % i, out_blocked=True))

    dxr, dg_rg, dw_a, dw_x, dvec = _rglru_bwd(seg["xr"], seg["g_rg"], h_rg, dys[0], conv_w, conv_b, w_a, b_a, w_x, b_x, lam, T)
    dq_s, dg_swa, dkv, dsinks, drel = _swa_bwd(seg["q_s"], seg["kv"], seg["g_swa"], o_swa, dys[1], bucket, rel_bias, sinks)
    dq_m, dg_mem, dmkv = _mem_bwd(seg["q_m"], mkv, seg["g_mem"], o_mem, dys[2])

    dmkv_b = dmkv.astype(BF)
    dw_memkv = _matmul(memn, dmkv_b, "tn", D_MODEL, 2 * D_MEM, M, 1024, 2 * D_MEM, M, BF, "dw_memkv")
    dmemn = _matmul(dmkv_b, w_memkv, "nt", M, D_MODEL, 2 * D_MEM, M, 512, 2 * D_MEM, F32, "dmemn")
    dmem_g = _rms_gain_grad(dmemn, mem, "dmem_gain")

    dproj = jnp.concatenate([dxr, dg_rg, dq_s, dkv.astype(BF), dg_swa, dq_m, dg_mem, dgl0, dgl1, dgl2], axis=1)
    grad_x, dpre = _dh_dx(dproj, w_in, x, dy, pre_g, tm, 896)
    dw_in = _matmul(h, dproj, "tn", D_MODEL, D_IN, S, 512, 1792, S, BF, "dw_in")
    small = dict(pre=dpre, post=dpost, memg=dmem_g, dvec=dvec, dw_a=dw_a, dw_x=dw_x, dsinks=dsinks, drel=drel)
    return loss, grad_x, dw_in, dw_memkv, dwbr, dw_out, small


def _pad_rows(a, rows):
    a = a.reshape(-1, 128) if a.shape[-1] % 128 == 0 else jnp.pad(a, ((0, 0), (0, 128 - a.shape[-1])))
    return jnp.pad(a, ((0, rows - a.shape[0]), (0, 0))) if a.shape[0] < rows else a


def kernel(x, mem, pre_norm_g, post_norm_g, mem_norm_g, w_in, conv_w, conv_b, w_rg_a, b_rg_a, w_rg_x, b_rg_x, lru_lambda, swa_sinks, rel_bias, w_mem_kv, w_br_rg, w_br_swa, w_br_mem, w_out, loss_target, m_pre_norm_g, m_post_norm_g, m_mem_norm_g, m_w_in, m_conv_w, m_conv_b, m_w_rg_a, m_b_rg_a, m_w_rg_x, m_b_rg_x, m_lru_lambda, m_swa_sinks, m_rel_bias, m_w_mem_kv, m_w_br_rg, m_w_br_swa, m_w_br_mem, m_w_out, v_pre_norm_g, v_post_norm_g, v_mem_norm_g, v_w_in, v_conv_w, v_conv_b, v_w_rg_a, v_b_rg_a, v_w_rg_x, v_b_rg_x, v_lru_lambda, v_swa_sinks, v_rel_bias, v_w_mem_kv, v_w_br_rg, v_w_br_swa, v_w_br_mem, v_w_out):
    me = 4 * lax.axis_index("x") + 2 * lax.axis_index("y") + lax.axis_index("c")

    gathered = _all_gather([w_in[0].astype(BF), w_mem_kv[0].astype(BF), w_br_rg[0].astype(BF), w_br_swa[0].astype(BF),
                            w_br_mem[0].astype(BF), w_out[0].astype(BF), conv_w[0]], "gather_weights")
    w_in_f = jnp.transpose(gathered[0], (1, 0, 2)).reshape(D_MODEL, D_IN)
    w_memkv_f = gathered[1].reshape(D_MODEL, 2 * D_MEM)
    wbr = (gathered[2], gathered[3], gathered[4])
    w_out_f = gathered[5].reshape(D_MODEL, D_MODEL)
    conv_w_f = jnp.transpose(gathered[6], (1, 0, 2)).reshape(CONV_W, D_RNN)

    loss, grad_x, dw_in, dw_memkv, dwbr, dw_out, small = _local_step(
        x[0], mem[0], loss_target[0], pre_norm_g, post_norm_g, mem_norm_g, w_in_f, conv_w_f, conv_b,
        w_rg_a[0].astype(BF), b_rg_a, w_rg_x[0].astype(BF), b_rg_x, lru_lambda, swa_sinks, rel_bias, w_memkv_f, wbr, w_out_f)

    parts = [jnp.transpose(dw_in.reshape(D_MODEL, N_DEV, D_IN // N_DEV), (1, 0, 2)),
             dw_memkv.reshape(N_DEV, D_MODEL // N_DEV, 2 * D_MEM), dwbr[0], dwbr[1], dwbr[2],
             dw_out.reshape(N_DEV, D_MODEL // N_DEV, D_MODEL)]
    recv = _reduce_scatter_exchange(parts, "scatter_grads")

    pack = jnp.concatenate([small["pre"].reshape(16, 128), small["post"].reshape(16, 128), small["memg"].reshape(16, 128),
                            small["dvec"].reshape(64, 128), small["dw_a"].reshape(1024, 128), small["dw_x"].reshape(1024, 128),
                            _pad_rows(small["dsinks"], 8), small["drel"]], axis=0)
    gs = _sum_parts(_all_gather([pack], "gather_small")[0], "sum_small")
    g_pre, g_post, g_memg = gs[0:16].reshape(1, D_MODEL), gs[16:32].reshape(1, D_MODEL), gs[32:48].reshape(1, D_MODEL)
    gvec = gs[48:112].reshape(8, D_RNN)
    g_conv_w = lax.dynamic_slice(gvec[0:CONV_W], (0, me * RNN_BLOCK), (CONV_W, RNN_BLOCK))
    g_conv_b, g_b_a, g_b_x, g_lam = gvec[4:5], gvec[5:6], gvec[6:7], gvec[7:8]
    g_w_a = gs[112:1136].reshape(RNN_BLOCKS, RNN_BLOCK, RNN_BLOCK)
    g_w_x = gs[1136:2160].reshape(RNN_BLOCKS, RNN_BLOCK, RNN_BLOCK)
    g_sinks = gs[2160:2161, :SWA_HEADS]
    g_rel = gs[2168:2200, :SWA_HEADS]

    def packed(ts):
        pre, post, memg, cb, ba, bx, lm, wa, wx, sk, rel, cw = ts
        return jnp.concatenate([pre.reshape(16, 128), post.reshape(16, 128), memg.reshape(16, 128), cb.reshape(8, 128),
                                ba.reshape(8, 128), bx.reshape(8, 128), lm.reshape(8, 128), wa.reshape(1024, 128),
                                wx.reshape(1024, 128), _pad_rows(sk.reshape(1, SWA_HEADS), 8), _pad_rows(rel, 32),
                                _pad_rows(cw.reshape(CONV_W, RNN_BLOCK), 8)], axis=0)

    def unpacked(a):
        return (a[0:16].reshape(1, D_MODEL), a[16:32].reshape(1, D_MODEL), a[32:48].reshape(1, D_MODEL), a[48:56].reshape(1, D_RNN),
                a[56:64].reshape(1, D_RNN), a[64:72].reshape(1, D_RNN), a[72:80].reshape(1, D_RNN),
                a[80:1104].reshape(1, RNN_BLOCKS, RNN_BLOCK, RNN_BLOCK), a[1104:2128].reshape(1, RNN_BLOCKS, RNN_BLOCK, RNN_BLOCK),
                a[2128:2129, :SWA_HEADS], a[2136:2168, :SWA_HEADS], a[2168:2172].reshape(1, CONV_W, RNN_BLOCK))

    g_small = (g_pre, g_post, g_memg, g_conv_b, g_b_a, g_b_x, g_lam, g_w_a, g_w_x, g_sinks, g_rel, g_conv_w)
    w_small = (pre_norm_g, post_norm_g, mem_norm_g, conv_b, b_rg_a, b_rg_x, lru_lambda, w_rg_a, w_rg_x, swa_sinks, rel_bias, conv_w)
    m_small = (m_pre_norm_g, m_post_norm_g, m_mem_norm_g, m_conv_b, m_b_rg_a, m_b_rg_x, m_lru_lambda, m_w_rg_a, m_w_rg_x, m_swa_sinks, m_rel_bias, m_conv_w)
    v_small = (v_pre_norm_g, v_post_norm_g, v_mem_norm_g, v_conv_b, v_b_rg_a, v_b_rg_x, v_lru_lambda, v_w_rg_a, v_w_rg_x, v_swa_sinks, v_rel_bias, v_conv_w)
    sm = [unpacked(a) for a in _adamw(packed(g_small)[None], packed(w_small), packed(m_small), packed(v_small), "adamw_small")]

    big = []
    for j, (wt, mt, vt) in enumerate(((w_in, m_w_in, v_w_in), (w_mem_kv, m_w_mem_kv, v_w_mem_kv), (w_br_rg, m_w_br_rg, v_w_br_rg),
                                      (w_br_swa, m_w_br_swa, v_w_br_swa), (w_br_mem, m_w_br_mem, v_w_br_mem), (w_out, m_w_out, v_w_out))):
        big.append([a[None] for a in _adamw(recv[j], wt[0], mt[0], vt[0], "adamw_big%d" % j)])

    loss_total = lax.psum(loss[0, 0], AXES)

    def leaves(k):
        s = sm[k]
        return [s[0], s[1], s[2], big[0][k], s[11], s[3], s[7], s[4], s[8], s[5], s[6], s[9], s[10],
                big[1][k], big[2][k], big[3][k], big[4][k], big[5][k]]

    return (loss_total, grad_x[None], *leaves(0), *leaves(1), *leaves(2), *leaves(3))
```

```python
import math

import jax
import jax.numpy as jnp
import numpy as np
from jax import lax
from jax.experimental import pallas as pl
from jax.experimental.pallas import tpu as pltpu

F32, BF = jnp.float32, jnp.bfloat16
MESH = pl.DeviceIdType.MESH
AXES = ("x", "y", "c")
N_DEV = 8

D_MODEL = 2048
D_RNN = 1024
RNN_BLOCKS = 8
RNN_BLOCK = 128
CONV_W = 4
LRU_C = 8.0
SWA_HEADS = 16
SWA_KV_HEADS = 2
SWA_HD = 64
WINDOW = 128
MEM_HEADS = 4
MEM_HD = 256
D_MEM = 1024
REL_BUCKETS = 32
REL_MAX_DIST = 128
EPS = 1e-6
NEG_INF = -1e30
D_IN = 12544
SEGMENTS = (("xr", 0, 1024, F32), ("g_rg", 1024, 1024, F32), ("q_s", 2048, 1024, BF), ("kv", 3072, 256, BF),
            ("g_swa", 3328, 1024, F32), ("q_m", 4352, 1024, BF), ("g_mem", 5376, 1024, F32), ("gl", 6400, 6144, F32))
SEG_TILE = 256

ADAM_LR, ADAM_B1, ADAM_B2, ADAM_EPS, ADAM_WD, ADAM_STEP = 0.001, 0.9, 0.999, 1e-08, 0.01, 10

NN = (((1,), (0,)), ((), ()))
NT = (((1,), (1,)), ((), ()))
TN = (((0,), (0,)), ((), ()))
MIB = 2 ** 20


def _dot(a, b, dn):
    return lax.dot_general(a, b, dn, preferred_element_type=F32)


def _params(sem, vmem_mib=48):
    return pltpu.CompilerParams(dimension_semantics=sem, vmem_limit_bytes=vmem_mib * MIB)


def _sigmoid(z):
    return 1.0 / (1.0 + jnp.exp(-z))


def _softplus(z):
    return jnp.maximum(z, 0.0) + jnp.log(1.0 + jnp.exp(-jnp.abs(z)))


def _expm1(z):
    p = z * (1.0 + z * (0.5 + z * (1.0 / 6 + z * (1.0 / 24 + z * (1.0 / 120 + z * (1.0 / 720 + z * (1.0 / 5040 + z / 40320)))))))
    return jnp.where(jnp.abs(z) < 0.3, p, jnp.exp(z) - 1.0)


def _flat(p):
    return 4 * p[0] + 2 * p[1] + p[2]


def _all_gather(arrs, name):
    n = len(arrs)

    def body(*refs):
        ins, outs = refs[:n], refs[n:2 * n]
        send_sems, recv_sems, local_sems = refs[2 * n:]
        x, y, c = lax.axis_index("x"), lax.axis_index("y"), lax.axis_index("c")
        me, sibling = (x, y, c), (x, y, 1 - c)
        chips = [(1 - x, y), (x, 1 - y), (1 - x, 1 - y)]

        def copy(a, k, block, to, src=None):
            dst = outs[a].at[_flat(block)]
            return pltpu.make_async_remote_copy(src_ref=dst if src is None else src, dst_ref=dst,
                                                send_sem=send_sems.at[a * 7 + k], recv_sem=recv_sems.at[a * 7 + k],
                                                device_id=to, device_id_type=MESH)

        mine = [pltpu.make_async_copy(ins[a], outs[a].at[_flat(me)], local_sems.at[a]) for a in range(n)]
        for cp in mine:
            cp.start()
        first = []
        for a in range(n):
            first += [copy(a, 1 + j, me, (*chip, c), src=ins[a]) for j, chip in enumerate(chips)]
            first.append(copy(a, 0, me, sibling, src=ins[a]))
        for cp in first:
            cp.start()
        passed = []
        for j, chip in enumerate(chips):
            for a in range(n):
                copy(a, 1 + j, (*chip, c), me).wait_recv()
                fw = copy(a, 4 + j, (*chip, c), sibling)
                fw.start()
                passed.append(fw)
        for a in range(n):
            copy(a, 0, sibling, me).wait_recv()
            for j, chip in enumerate(chips):
                copy(a, 4 + j, (*chip, 1 - c), me).wait_recv()
        for cp in first + passed:
            cp.wait_send()
        for cp in mine:
            cp.wait()

    any_spec = pl.BlockSpec(memory_space=pl.ANY)
    return pl.pallas_call(
        body, name=name,
        out_shape=[jax.ShapeDtypeStruct((N_DEV,) + a.shape, a.dtype) for a in arrs],
        in_specs=[any_spec] * n, out_specs=[any_spec] * n,
        scratch_shapes=[pltpu.SemaphoreType.DMA((7 * n,)), pltpu.SemaphoreType.DMA((7 * n,)), pltpu.SemaphoreType.DMA((n,))],
    )(*arrs)


def _chip_peers(x, y):
    return [(1 - x, y), (x, 1 - y), (1 - x, 1 - y)]


def _chip(p):
    return 2 * p[0] + p[1]


def _plan_gather(n):
    def plan(x, y, c):
        out = []
        for a in range(n):
            for peer in [(x, y, 1 - c)] + [(*ch, c) for ch in _chip_peers(x, y)]:
                out.append((a, None, _flat((x, y, c)), peer, _flat(peer)))
        return out
    return plan


def _plan_scatter(n):
    def plan(x, y, c):
        out = []
        for a in range(n):
            for ch in _chip_peers(x, y):
                out.append((a, _chip(ch), _chip((x, y)), (*ch, c), _chip(ch)))
        return out
    return plan


HBM_SPEC = pl.BlockSpec(memory_space=pltpu.HBM)
SEM_SPEC = pl.BlockSpec(memory_space=pltpu.SEMAPHORE)


def _in_hbm(a):
    return pltpu.with_memory_space_constraint(a, pltpu.HBM)


def _exchange_start(srcs, lands, plan, name):
    n = len(srcs)
    count = len(plan(0, 0, 0))

    def body(*refs):
        src_refs, land_refs = refs[:n], refs[n:2 * n]
        send_sems, recv_sems = refs[2 * n], refs[2 * n + 1]
        token = refs[-1]
        x, y, c = lax.axis_index("x"), lax.axis_index("y"), lax.axis_index("c")
        for k, (a, si, di, peer, _) in enumerate(plan(x, y, c)):
            src = src_refs[a] if si is None else src_refs[a].at[si]
            pltpu.make_async_remote_copy(src_ref=src, dst_ref=land_refs[a].at[di], send_sem=send_sems.at[k],
                                         recv_sem=recv_sems.at[k], device_id=peer, device_id_type=MESH).start()
        token[...] = jnp.zeros_like(token)

    out = pl.pallas_call(
        body, name=name,
        out_shape=(pltpu.SemaphoreType.DMA((count,)), pltpu.SemaphoreType.DMA((count,)),
                   *[pltpu.HBM(a.shape, a.dtype) for a in srcs], *[pltpu.HBM(a.shape, a.dtype) for a in lands],
                   jax.ShapeDtypeStruct((8, 128), F32)),
        in_specs=[HBM_SPEC] * (2 * n),
        out_specs=(SEM_SPEC, SEM_SPEC, *([HBM_SPEC] * (2 * n)), pl.BlockSpec(memory_space=pltpu.VMEM)),
        input_output_aliases={i: 2 + i for i in range(2 * n)},
        compiler_params=pltpu.CompilerParams(has_side_effects=pltpu.SideEffectType.DATAFLOW_SIDE_EFFECTING),
    )(*[_in_hbm(a) for a in srcs], *[_in_hbm(a) for a in lands])
    return out[0], out[1], list(out[2:2 + n]), list(out[2 + n:2 + 2 * n]), out[-1]


def _exchange_wait(send_sems, recv_sems, srcs, lands, plan, after, name):
    n = len(srcs)

    def body(*refs):
        src_refs, land_refs = refs[:n], refs[n:2 * n]
        send_sems, recv_sems = refs[2 * n], refs[2 * n + 1]
        x, y, c = lax.axis_index("x"), lax.axis_index("y"), lax.axis_index("c")
        for k, (a, si, _, peer, ri) in enumerate(plan(x, y, c)):
            src = src_refs[a] if si is None else src_refs[a].at[si]
            cp = pltpu.make_async_remote_copy(src_ref=src, dst_ref=land_refs[a].at[ri], send_sem=send_sems.at[k],
                                              recv_sem=recv_sems.at[k], device_id=peer, device_id_type=MESH)
            cp.wait_send()
            cp.wait_recv()

    out = pl.pallas_call(
        body, name=name,
        out_shape=(*[pltpu.HBM(a.shape, a.dtype) for a in srcs], *[pltpu.HBM(a.shape, a.dtype) for a in lands]),
        in_specs=[HBM_SPEC] * (2 * n) + [SEM_SPEC, SEM_SPEC, pl.BlockSpec(memory_space=pl.ANY)],
        out_specs=tuple([HBM_SPEC] * (2 * n)),
        input_output_aliases={i: i for i in range(2 * n)},
        compiler_params=pltpu.CompilerParams(has_side_effects=pltpu.SideEffectType.DATAFLOW_SIDE_EFFECTING),
    )(*srcs, *lands, send_sems, recv_sems, after)
    return list(out[n:2 * n])


def _forward_to_sibling(lands, name):
    n = len(lands)

    def body(*refs):
        in_refs, out_refs = refs[:n], refs[n:2 * n]
        send_sems, recv_sems = refs[2 * n:]
        x, y, c = lax.axis_index("x"), lax.axis_index("y"), lax.axis_index("c")
        sibling = (x, y, 1 - c)

        def copy(a, j, slot):
            return pltpu.make_async_remote_copy(src_ref=in_refs[a].at[slot], dst_ref=out_refs[a].at[slot],
                                                send_sem=send_sems.at[a * 3 + j], recv_sem=recv_sems.at[a * 3 + j],
                                                device_id=sibling, device_id_type=MESH)

        sends = [copy(a, j, _flat((*ch, c))) for a in range(n) for j, ch in enumerate(_chip_peers(x, y))]
        for cp in sends:
            cp.start()
        for a in range(n):
            for j, ch in enumerate(_chip_peers(x, y)):
                copy(a, j, _flat((*ch, 1 - c))).wait_recv()
        for cp in sends:
            cp.wait_send()

    any_spec = pl.BlockSpec(memory_space=pl.ANY)
    return pl.pallas_call(
        body, name=name, out_shape=[jax.ShapeDtypeStruct(a.shape, a.dtype) for a in lands],
        in_specs=[any_spec] * n, out_specs=[any_spec] * n, input_output_aliases={a: a for a in range(n)},
        scratch_shapes=[pltpu.SemaphoreType.DMA((3 * n,)), pltpu.SemaphoreType.DMA((3 * n,))],
    )(*lands)


def _swap_with_sibling(parts, name):
    n = len(parts)

    def body(*refs):
        in_refs, out_refs = refs[:n], refs[n:2 * n]
        send_sems, recv_sems = refs[2 * n:]
        x, y, c = lax.axis_index("x"), lax.axis_index("y"), lax.axis_index("c")
        sends = [pltpu.make_async_remote_copy(src_ref=in_refs[a].at[1 - c], dst_ref=out_refs[a], send_sem=send_sems.at[a],
                                              recv_sem=recv_sems.at[a], device_id=(x, y, 1 - c), device_id_type=MESH)
                 for a in range(n)]
        for cp in sends:
            cp.start()
        for cp in sends:
            cp.wait()

    any_spec = pl.BlockSpec(memory_space=pl.ANY)
    return pl.pallas_call(
        body, name=name, out_shape=[jax.ShapeDtypeStruct(a.shape[1:], a.dtype) for a in parts],
        in_specs=[any_spec] * n, out_specs=[any_spec] * n,
        scratch_shapes=[pltpu.SemaphoreType.DMA((n,)), pltpu.SemaphoreType.DMA((n,))],
    )(*parts)


def _pair_sum(parts, got, core, name):
    _, _, R, C = parts.shape
    tr = 256 if R % 256 == 0 else R

    def body(c_ref, p_ref, g_ref, o_ref):
        o_ref[...] = (p_ref[...].astype(F32) + g_ref[...].astype(F32)).astype(o_ref.dtype)

    return pl.pallas_call(
        body, name=name,
        grid_spec=pltpu.PrefetchScalarGridSpec(
            num_scalar_prefetch=1, grid=(4, R // tr),
            in_specs=[pl.BlockSpec((None, None, tr, C), lambda j, i, c_ref: (c_ref[0], j, i, 0)),
                      pl.BlockSpec((None, tr, C), lambda j, i, c_ref: (j, i, 0))],
            out_specs=pl.BlockSpec((None, tr, C), lambda j, i, c_ref: (j, i, 0))),
        out_shape=jax.ShapeDtypeStruct((4, R, C), parts.dtype),
        compiler_params=_params(("parallel", "parallel")),
    )(core, parts, got)


def _matmul(a, b, mode, M, N, K, tm, tn, tk, out_dtype, name, b_noff=0, b_blocked=False, out_blocked=None, vmem_mib=48):
    nm, nn, nk = M // tm, N // tn, K // tk
    if mode == "nn":
        a_spec = pl.BlockSpec((tm, tk), lambda j, i, k: (i, k))
        b_spec = pl.BlockSpec((tk, tn), lambda j, i, k: (k, j + b_noff))
        dn = NN
    elif mode == "nt":
        a_spec = pl.BlockSpec((tm, tk), lambda j, i, k: (i, k))
        if b_blocked:
            b_spec = pl.BlockSpec((None, tn, tk), lambda j, i, k: (k, j, 0))
        else:
            b_spec = pl.BlockSpec((tn, tk), lambda j, i, k: (j + b_noff, k))
        dn = NT
    else:
        a_spec = pl.BlockSpec((tk, tm), lambda j, i, k: (k, i))
        b_spec = pl.BlockSpec((tk, tn), lambda j, i, k: (k, j + b_noff))
        dn = TN
    if out_blocked == "col":
        out_shape = jax.ShapeDtypeStruct((2, 4, M, tn), out_dtype)
        out_spec = pl.BlockSpec((None, None, tm, tn), lambda j, i, k: (j % 2, j // 2, i, 0))
    elif out_blocked == "row":
        out_shape = jax.ShapeDtypeStruct((2, 4, tm, N), out_dtype)
        out_spec = pl.BlockSpec((None, None, tm, tn), lambda j, i, k: (i % 2, i // 2, 0, j))
    else:
        out_shape = jax.ShapeDtypeStruct((M, N), out_dtype)
        out_spec = pl.BlockSpec((tm, tn), lambda j, i, k: (i, j))

    def body(a_ref, b_ref, o_ref, *scratch):
        if nk == 1:
            o_ref[...] = _dot(a_ref[...], b_ref[...], dn).astype(out_dtype)
        else:
            acc_ref, = scratch
            k = pl.program_id(2)

            @pl.when(k == 0)
            def _():
                acc_ref[...] = jnp.zeros_like(acc_ref)

            acc_ref[...] += _dot(a_ref[...], b_ref[...], dn)

            @pl.when(k == nk - 1)
            def _():
                o_ref[...] = acc_ref[...].astype(out_dtype)

    return pl.pallas_call(
        body, name=name, grid=(nn, nm, nk), in_specs=[a_spec, b_spec], out_specs=out_spec, out_shape=out_shape,
        scratch_shapes=[] if nk == 1 else [pltpu.VMEM((tm, tn), F32)],
        compiler_params=_params(("parallel", "parallel", "arbitrary"), vmem_mib),
    )(a, b)


def _rms_fwd(x, g, name):
    R, Dm = x.shape
    tr = min(R, 256)

    def body(x_ref, g_ref, h_ref):
        xv = x_ref[...]
        r = lax.rsqrt(jnp.mean(xv * xv, axis=-1, keepdims=True) + EPS)
        h_ref[...] = (xv * r * g_ref[...]).astype(BF)

    return pl.pallas_call(
        body, name=name, grid=(R // tr,),
        in_specs=[pl.BlockSpec((tr, Dm), lambda i: (i, 0)), pl.BlockSpec((1, Dm), lambda i: (0, 0))],
        out_specs=pl.BlockSpec((tr, Dm), lambda i: (i, 0)), out_shape=jax.ShapeDtypeStruct((R, Dm), BF),
        compiler_params=_params(("parallel",)),
    )(x, g)


def _rms_gain_grad(dn, x, name):
    R, Dm = x.shape

    def body(dn_ref, x_ref, o_ref):
        xv = x_ref[...]
        r = lax.rsqrt(jnp.mean(xv * xv, axis=-1, keepdims=True) + EPS)
        o_ref[...] = jnp.sum(dn_ref[...] * xv * r, axis=0, keepdims=True)

    return pl.pallas_call(
        body, name=name, out_shape=jax.ShapeDtypeStruct((1, Dm), F32),
        compiler_params=pltpu.CompilerParams(vmem_limit_bytes=32 * MIB),
    )(dn, x)


def _shift_down(v, k, head8, row, T):
    if k == 0:
        return v
    r = pltpu.roll(v, k, 0)
    hr = pltpu.roll(head8, k, 0)
    top = jnp.where(row[:8] < k, hr, r[:8])
    return jnp.concatenate([top, r[8:]], axis=0)


def _shift_up(v, k, tail8, row, T):
    if k == 0:
        return v
    r = pltpu.roll(v, T - k, 0)
    tr = pltpu.roll(tail8, 8 - k, 0)
    bot = jnp.where(row[:8] >= 8 - k, tr, r[T - 8:])
    return jnp.concatenate([r[:T - 8], bot], axis=0)


def _rglru_gates(u, head8, grow, row, T, cw_ref, cb_ref, wa_ref, ba_ref, wx_ref, bx_ref, lam_ref):
    us = [_shift_down(u, k, head8, row, T) for k in range(CONV_W)]
    acc = us[0] * cw_ref[0:1, :]
    for k in range(1, CONV_W):
        acc = acc + us[k] * cw_ref[k:k + 1, :]
    conv = cb_ref[...] + acc
    cbf = conv.astype(BF)
    r_ = _sigmoid(_dot(cbf, wa_ref[0], NN) + ba_ref[...])
    i_ = _sigmoid(_dot(cbf, wx_ref[0], NN) + bx_ref[...])
    sp = _softplus(-lam_ref[...])
    la = -LRU_C * r_ * sp
    a = jnp.exp(la)
    mult_raw = jnp.sqrt(-_expm1(2.0 * la))
    mult = jnp.where(grow == 0, 1.0, mult_raw)
    return us, conv, cbf, r_, i_, sp, a, mult_raw, mult


def _rglru_specs(T, nt, rev):
    tmap = (lambda n, t: (nt - 1 - t, n)) if rev else (lambda n, t: (t, n))
    hmap = ((lambda n, t: (jnp.maximum((nt - 1 - t) * (T // 8) - 1, 0), n)) if rev
            else (lambda n, t: (jnp.maximum(t * (T // 8) - 1, 0), n)))
    tile = pl.BlockSpec((T, RNN_BLOCK), tmap)
    halo = pl.BlockSpec((8, RNN_BLOCK), hmap)
    vec = pl.BlockSpec((1, RNN_BLOCK), lambda n, t: (0, n))
    cw = pl.BlockSpec((CONV_W, RNN_BLOCK), lambda n, t: (0, n))
    wblk = pl.BlockSpec((1, RNN_BLOCK, RNN_BLOCK), lambda n, t: (n, 0, 0))
    return tile, halo, vec, cw, wblk


def _rglru_fwd(xr, g, cw, cb, wa, ba, wx, bx, lam, T):
    S = xr.shape[0]
    nt = S // T

    def body(u_ref, uh_ref, g_ref, cw_ref, cb_ref, wa_ref, ba_ref, wx_ref, bx_ref, lam_ref, h_ref, y_ref, carry):
        t = pl.program_id(1)

        @pl.when(t == 0)
        def _():
            carry[...] = jnp.zeros_like(carry)

        row = lax.broadcasted_iota(jnp.int32, (T, RNN_BLOCK), 0)
        grow = row + t * T
        head8 = jnp.where(t > 0, uh_ref[...], 0.0)
        _, conv, _, _, i_, _, a, _, mult = _rglru_gates(u_ref[...], head8, grow, row, T, cw_ref, cb_ref, wa_ref, ba_ref,
                                                         wx_ref, bx_ref, lam_ref)
        b = mult * i_ * conv
        s = 1
        while s < T:
            keep = row >= s
            a_s = jnp.where(keep, pltpu.roll(a, s, 0), 1.0)
            b_s = jnp.where(keep, pltpu.roll(b, s, 0), 0.0)
            b = a * b_s + b
            a = a * a_s
            s *= 2
        h = b + a * carry[0:1, :]
        carry[...] = jnp.broadcast_to(h[T - 1:T, :], carry.shape)
        h_ref[...] = h
        gv = g_ref[...]
        y_ref[...] = (h * (gv * _sigmoid(gv))).astype(BF)

    tile, halo, vec, cwspec, wblk = _rglru_specs(T, nt, False)
    return pl.pallas_call(
        body, name="rglru_fwd", grid=(RNN_BLOCKS, nt),
        in_specs=[tile, halo, tile, cwspec, vec, wblk, vec, wblk, vec, vec],
        out_specs=[tile, tile],
        out_shape=[jax.ShapeDtypeStruct((S, D_RNN), F32), jax.ShapeDtypeStruct((S, D_RNN), BF)],
        scratch_shapes=[pltpu.VMEM((8, RNN_BLOCK), F32)],
        compiler_params=_params(("parallel", "arbitrary")),
    )(xr, xr, g, cw, cb, wa, ba, wx, bx, lam)


def _rglru_bwd(xr, g, h, dy, cw, cb, wa, ba, wx, bx, lam, T):
    S = xr.shape[0]
    nt = S // T

    def body(u_ref, uh_ref, g_ref, h_ref, hh_ref, dy_ref, cw_ref, cb_ref, wa_ref, ba_ref, wx_ref, bx_ref, lam_ref,
             du_ref, dg_ref, dwa_ref, dwx_ref, dvec_ref, c_dhh, c_a, c_dconv):
        t = pl.program_id(1)
        tt = nt - 1 - t

        @pl.when(t == 0)
        def _():
            c_dhh[...] = jnp.zeros_like(c_dhh)
            c_a[...] = jnp.zeros_like(c_a)
            c_dconv[...] = jnp.zeros_like(c_dconv)
            dwa_ref[...] = jnp.zeros_like(dwa_ref)
            dwx_ref[...] = jnp.zeros_like(dwx_ref)
            dvec_ref[...] = jnp.zeros_like(dvec_ref)

        row = lax.broadcasted_iota(jnp.int32, (T, RNN_BLOCK), 0)
        row8 = row[:8]
        grow = row + tt * T
        head8 = jnp.where(tt > 0, uh_ref[...], 0.0)
        us, conv, cbf, r_, i_, sp, a, mult_raw, mult = _rglru_gates(
            u_ref[...], head8, grow, row, T, cw_ref, cb_ref, wa_ref, ba_ref, wx_ref, bx_ref, lam_ref)
        hv = h_ref[...]
        hprev = _shift_down(hv, 1, jnp.where(tt > 0, hh_ref[...], 0.0), row, T)
        gv = g_ref[...]
        sg = _sigmoid(gv)
        dyv = dy_ref[...]
        dg_ref[...] = (dyv * hv * (sg * (1.0 + gv * (1.0 - sg)))).astype(BF)
        d = dyv * (gv * sg)
        A = _shift_up(a, 1, c_a[...], row, T)
        s = 1
        while s < T:
            keep = row < T - s
            A_s = jnp.where(keep, pltpu.roll(A, T - s, 0), 1.0)
            d_s = jnp.where(keep, pltpu.roll(d, T - s, 0), 0.0)
            d = A * d_s + d
            A = A * A_s
            s *= 2
        dhh = d + A * c_dhh[0:1, :]
        da = dhh * hprev
        dconv = dhh * mult * i_
        di = dhh * mult * conv
        dmult = dhh * i_ * conv
        dla = da * a - jnp.where(grow == 0, 0.0, dmult * (a * a) / mult_raw)
        dr = dla * (-LRU_C * sp)
        dsp = jnp.sum(dla * (-LRU_C * r_), axis=0, keepdims=True)
        dza = dr * r_ * (1.0 - r_)
        dzx = di * i_ * (1.0 - i_)
        dza_b, dzx_b = dza.astype(BF), dzx.astype(BF)
        dconv = dconv + _dot(dza_b, wa_ref[0], NT) + _dot(dzx_b, wx_ref[0], NT)
        dwa_ref[0] += _dot(cbf, dza_b, TN)
        dwx_ref[0] += _dot(cbf, dzx_b, TN)
        lam = lam_ref[...]
        rows = [jnp.sum(dconv * us[k], axis=0, keepdims=True) for k in range(CONV_W)]
        rows += [jnp.sum(dconv, axis=0, keepdims=True), jnp.sum(dza, axis=0, keepdims=True),
                 jnp.sum(dzx, axis=0, keepdims=True), dsp * (-_sigmoid(-lam))]
        upd = jnp.zeros((8, RNN_BLOCK), F32)
        for j, rv in enumerate(rows):
            upd = upd + jnp.where(row8 == j, rv, 0.0)
        dvec_ref[...] += upd
        tail8 = c_dconv[...]
        du = dconv * cw_ref[0:1, :]
        for k in range(1, CONV_W):
            du = du + _shift_up(dconv, k, tail8, row, T) * cw_ref[k:k + 1, :]
        du_ref[...] = du.astype(BF)
        c_dhh[...] = jnp.broadcast_to(dhh[0:1, :], c_dhh.shape)
        c_a[...] = jnp.broadcast_to(a[0:1, :], c_a.shape)
        c_dconv[...] = dconv[:8]

    tile, halo, vec, cwspec, wblk = _rglru_specs(T, nt, True)
    acc8 = pl.BlockSpec((8, RNN_BLOCK), lambda n, t: (0, n))
    return pl.pallas_call(
        body, name="rglru_bwd", grid=(RNN_BLOCKS, nt),
        in_specs=[tile, halo, tile, tile, halo, tile, cwspec, vec, wblk, vec, wblk, vec, vec],
        out_specs=[tile, tile, wblk, wblk, acc8],
        out_shape=[jax.ShapeDtypeStruct((S, D_RNN), BF), jax.ShapeDtypeStruct((S, D_RNN), BF),
                   jax.ShapeDtypeStruct((RNN_BLOCKS, RNN_BLOCK, RNN_BLOCK), F32),
                   jax.ShapeDtypeStruct((RNN_BLOCKS, RNN_BLOCK, RNN_BLOCK), F32),
                   jax.ShapeDtypeStruct((8, D_RNN), F32)],
        scratch_shapes=[pltpu.VMEM((8, RNN_BLOCK), F32)] * 3,
        compiler_params=_params(("parallel", "arbitrary")),
    )(xr, xr, g, h, h, dy, cw, cb, wa, ba, wx, bx, lam)


def _rel_bucket_map():
    qi = np.arange(WINDOW)[:, None]
    kj = np.arange(2 * WINDOW)[None, :]
    dist = jnp.asarray(qi + WINDOW - kj, jnp.int32)
    n = jnp.maximum(dist, 0)
    max_exact = REL_BUCKETS // 2
    ratio = jnp.log(jnp.maximum(n, 1).astype(F32) / max_exact) / math.log(REL_MAX_DIST / max_exact)
    large = jnp.minimum(max_exact + (ratio * (REL_BUCKETS - max_exact)).astype(jnp.int32), REL_BUCKETS - 1)
    return jnp.where(n < max_exact, n, large).astype(jnp.int32)


def _swa_common(n, kv_ref, bucket_ref, relb_ref, bias_scr):
    @pl.when(n == 0)
    def _():
        bk = bucket_ref[...]
        for h in range(SWA_HEADS):
            acc = jnp.zeros((WINDOW, 2 * WINDOW), F32)
            for b in range(REL_BUCKETS):
                acc = acc + jnp.where(bk == b, relb_ref[b, h], 0.0)
            bias_scr[h] = acc

    prev0 = pl.multiple_of(jnp.maximum(n - 1, 0) * WINDOW, WINDOW)
    cur0 = pl.multiple_of(n * WINDOW, WINDOW)
    kk = jnp.concatenate([kv_ref[pl.ds(prev0, WINDOW), :], kv_ref[pl.ds(cur0, WINDOW), :]], axis=0).astype(F32)
    rowi = lax.broadcasted_iota(jnp.int32, (WINDOW, 2 * WINDOW), 0)
    col = lax.broadcasted_iota(jnp.int32, (WINDOW, 2 * WINDOW), 1)
    no_prev = jnp.where(n > 0, 0, 4 * WINDOW)
    valid = jnp.logical_or(jnp.logical_and(col < WINDOW, col > rowi + no_prev),
                           jnp.logical_and(col >= WINDOW, (col - WINDOW) <= rowi))
    return kk, valid, prev0, cur0


def _half_pair(part, kvh):
    lo = lax.broadcasted_iota(jnp.int32, part.shape, 1) < SWA_HD
    if kvh == 0:
        pa = jnp.where(lo, part, 0.0)
        pb = pltpu.roll(pa, SWA_HD, 1)
    else:
        pb = jnp.where(lo, 0.0, part)
        pa = pltpu.roll(pb, SWA_HD, 1)
    return pa.astype(BF), pb.astype(BF)


def _swa_probs(q2, kx, bias, sink, valid):
    lg = _dot(q2, kx, NT) * (SWA_HD ** -0.5) + bias
    lg = jnp.where(valid, lg, NEG_INF)
    m = jnp.maximum(jnp.max(lg, axis=-1, keepdims=True), sink)
    e = jnp.exp(lg - m)
    es = jnp.exp(sink - m)
    den = jnp.sum(e, axis=-1, keepdims=True) + es
    return e / den, es / den


def _swa_fwd(q, kv, g, bucket, rel_bias, sinks):
    S = q.shape[0]
    nb = S // WINDOW

    def body(q_ref, kv_ref, g_ref, bucket_ref, relb_ref, sink_ref, o_ref, y_ref, bias_scr):
        n = pl.program_id(0)
        kk, valid, _, _ = _swa_common(n, kv_ref, bucket_ref, relb_ref, bias_scr)
        for kvh in range(SWA_KV_HEADS):
            ka, kb = _half_pair(kk[:, :128], kvh)
            va, vb = _half_pair(kk[:, 128:], kvh)
            for p in range(4):
                c0 = kvh * 512 + p * 128
                h0 = kvh * 8 + 2 * p
                q2 = q_ref[:, c0:c0 + 128]
                p0, _ = _swa_probs(q2, ka, bias_scr[h0], sink_ref[0, h0], valid)
                p1, _ = _swa_probs(q2, kb, bias_scr[h0 + 1], sink_ref[0, h0 + 1], valid)
                o2 = _dot(p0.astype(BF), va, NN) + _dot(p1.astype(BF), vb, NN)
                o_ref[:, c0:c0 + 128] = o2
                gv = g_ref[:, c0:c0 + 128]
                y_ref[:, c0:c0 + 128] = (o2 * (gv * _sigmoid(gv))).astype(BF)

    blk = pl.BlockSpec((WINDOW, 1024), lambda n: (n, 0))
    smem = pl.BlockSpec(memory_space=pltpu.SMEM)
    return pl.pallas_call(
        body, name="swa_fwd", grid=(nb,),
        in_specs=[blk, pl.BlockSpec((S, 256), lambda n: (0, 0)), blk, pl.BlockSpec((WINDOW, 2 * WINDOW), lambda n: (0, 0)), smem, smem],
        out_specs=[blk, blk],
        out_shape=[jax.ShapeDtypeStruct((S, 1024), F32), jax.ShapeDtypeStruct((S, 1024), BF)],
        scratch_shapes=[pltpu.VMEM((SWA_HEADS, WINDOW, 2 * WINDOW), F32)],
        compiler_params=_params(("arbitrary",)),
    )(q, kv, g, bucket, rel_bias, sinks)


def _swa_bwd(q, kv, g, o, dy, bucket, rel_bias, sinks):
    S = q.shape[0]
    nb = S // WINDOW

    def body(q_ref, kv_ref, g_ref, o_ref, dy_ref, bucket_ref, relb_ref, sink_ref,
             dq_ref, dg_ref, dkv_ref, dsink_ref, drel_ref, bias_scr, dbias_scr, dsink_scr):
        n = pl.program_id(0)

        @pl.when(n == 0)
        def _():
            dbias_scr[...] = jnp.zeros_like(dbias_scr)
            dsink_scr[...] = jnp.zeros_like(dsink_scr)
            dkv_ref[...] = jnp.zeros_like(dkv_ref)

        kk, valid, prev0, cur0 = _swa_common(n, kv_ref, bucket_ref, relb_ref, bias_scr)
        lane = lax.broadcasted_iota(jnp.int32, (WINDOW, 128), 1)
        lo256 = lax.broadcasted_iota(jnp.int32, (2 * WINDOW, 128), 1) < SWA_HD
        dks, dvs = [], []
        for kvh in range(SWA_KV_HEADS):
            ka, kb = _half_pair(kk[:, :128], kvh)
            va, vb = _half_pair(kk[:, 128:], kvh)
            dka = jnp.zeros((2 * WINDOW, 128), F32)
            dkb, dva, dvb = dka, dka, dka
            for p in range(4):
                c0 = kvh * 512 + p * 128
                h0 = kvh * 8 + 2 * p
                q2 = q_ref[:, c0:c0 + 128]
                gv = g_ref[:, c0:c0 + 128]
                sg = _sigmoid(gv)
                dyv = dy_ref[:, c0:c0 + 128]
                dg_ref[:, c0:c0 + 128] = (dyv * o_ref[:, c0:c0 + 128] * (sg * (1.0 + gv * (1.0 - sg)))).astype(BF)
                do2 = (dyv * (gv * sg)).astype(BF)
                dq2 = jnp.zeros((WINDOW, 128), F32)
                for half, (kx, vx) in enumerate(((ka, va), (kb, vb))):
                    hh = h0 + half
                    pr, ps = _swa_probs(q2, kx, bias_scr[hh], sink_ref[0, hh], valid)
                    dp = _dot(do2, vx, NT)
                    delta = jnp.sum(pr * dp, axis=-1, keepdims=True)
                    ds = pr * (dp - delta)
                    dbias_scr[hh] += ds
                    dsink_scr[...] += jnp.where(lane == hh, ps * delta, 0.0)
                    dsb = (ds * (SWA_HD ** -0.5)).astype(BF)
                    prb = pr.astype(BF)
                    dq2 = dq2 + _dot(dsb, kx, NN)
                    if half == 0:
                        dka = dka + _dot(dsb, q2, TN)
                        dva = dva + _dot(prb, do2, TN)
                    else:
                        dkb = dkb + _dot(dsb, q2, TN)
                        dvb = dvb + _dot(prb, do2, TN)
                dq_ref[:, c0:c0 + 128] = dq2.astype(BF)
            dks.append(jnp.where(lo256, dka, 0.0) + pltpu.roll(jnp.where(lo256, 0.0, dkb), SWA_HD, 1))
            dvs.append(jnp.where(lo256, dva, 0.0) + pltpu.roll(jnp.where(lo256, 0.0, dvb), SWA_HD, 1))
        dk = dks[0] + pltpu.roll(dks[1], SWA_HD, 1)
        dv = dvs[0] + pltpu.roll(dvs[1], SWA_HD, 1)
        dkv_ref[pl.ds(prev0, WINDOW), 0:128] += dk[:WINDOW]
        dkv_ref[pl.ds(prev0, WINDOW), 128:256] += dv[:WINDOW]
        dkv_ref[pl.ds(cur0, WINDOW), 0:128] += dk[WINDOW:]
        dkv_ref[pl.ds(cur0, WINDOW), 128:256] += dv[WINDOW:]

        @pl.when(n == nb - 1)
        def _():
            dsink_ref[...] = -jnp.sum(dsink_scr[...], axis=0, keepdims=True)
            bk = bucket_ref[...]
            r32 = lax.broadcasted_iota(jnp.int32, (REL_BUCKETS, 128), 0)
            l32 = lax.broadcasted_iota(jnp.int32, (REL_BUCKETS, 128), 1)
            acc = jnp.zeros((REL_BUCKETS, 128), F32)
            for b in range(REL_BUCKETS):
                mb = bk == b
                for h in range(SWA_HEADS):
                    t1 = jnp.sum(jnp.where(mb, dbias_scr[h], 0.0), axis=1, keepdims=True)
                    val = jnp.sum(t1, axis=0, keepdims=True)
                    acc = acc + jnp.where(jnp.logical_and(r32 == b, l32 == h), val, 0.0)
            drel_ref[...] = acc

    blk = pl.BlockSpec((WINDOW, 1024), lambda n: (n, 0))
    smem = pl.BlockSpec(memory_space=pltpu.SMEM)
    whole = lambda shape: pl.BlockSpec(shape, lambda n: (0, 0))
    return pl.pallas_call(
        body, name="swa_bwd", grid=(nb,),
        in_specs=[blk, whole((S, 256)), blk, blk, blk, whole((WINDOW, 2 * WINDOW)), smem, smem],
        out_specs=[blk, blk, whole((S, 256)), whole((1, 128)), whole((REL_BUCKETS, 128))],
        out_shape=[jax.ShapeDtypeStruct((S, 1024), BF), jax.ShapeDtypeStruct((S, 1024), BF),
                   jax.ShapeDtypeStruct((S, 256), F32), jax.ShapeDtypeStruct((1, 128), F32),
                   jax.ShapeDtypeStruct((REL_BUCKETS, 128), F32)],
        scratch_shapes=[pltpu.VMEM((SWA_HEADS, WINDOW, 2 * WINDOW), F32), pltpu.VMEM((SWA_HEADS, WINDOW, 2 * WINDOW), F32),
                        pltpu.VMEM((WINDOW, 128), F32)],
        compiler_params=_params(("arbitrary",)),
    )(q, kv, g, o, dy, bucket, rel_bias, sinks)


def _mem_probs(qh, mk):
    lg = _dot(qh, mk, NT) * (MEM_HD ** -0.5)
    e = jnp.exp(lg - jnp.max(lg, axis=-1, keepdims=True))
    return e / jnp.sum(e, axis=-1, keepdims=True)


def _mem_fwd(q, mkv, g):
    S = q.shape[0]
    M = mkv.shape[0]
    tq = 256

    def body(q_ref, mkv_ref, g_ref, o_ref, y_ref):
        for h in range(MEM_HEADS):
            c0 = h * MEM_HD
            pr = _mem_probs(q_ref[:, c0:c0 + MEM_HD], mkv_ref[:, c0:c0 + MEM_HD])
            o = _dot(pr.astype(BF), mkv_ref[:, D_MEM + c0:D_MEM + c0 + MEM_HD], NN)
            o_ref[:, c0:c0 + MEM_HD] = o
            gv = g_ref[:, c0:c0 + MEM_HD]
            y_ref[:, c0:c0 + MEM_HD] = (o * (gv * _sigmoid(gv))).astype(BF)

    blk = pl.BlockSpec((tq, D_MEM), lambda i: (i, 0))
    return pl.pallas_call(
        body, name="mem_fwd", grid=(S // tq,),
        in_specs=[blk, pl.BlockSpec((M, 2 * D_MEM), lambda i: (0, 0)), blk], out_specs=[blk, blk],
        out_shape=[jax.ShapeDtypeStruct((S, D_MEM), F32), jax.ShapeDtypeStruct((S, D_MEM), BF)],
        compiler_params=_params(("parallel",)),
    )(q, mkv, g)


def _mem_bwd(q, mkv, g, o, dy):
    S = q.shape[0]
    M = mkv.shape[0]
    tq = 256

    def body(q_ref, mkv_ref, g_ref, o_ref, dy_ref, dq_ref, dg_ref, dmkv_ref):
        @pl.when(pl.program_id(0) == 0)
        def _():
            dmkv_ref[...] = jnp.zeros_like(dmkv_ref)

        for h in range(MEM_HEADS):
            c0 = h * MEM_HD
            qh = q_ref[:, c0:c0 + MEM_HD]
            mk = mkv_ref[:, c0:c0 + MEM_HD]
            mv = mkv_ref[:, D_MEM + c0:D_MEM + c0 + MEM_HD]
            gv = g_ref[:, c0:c0 + MEM_HD]
            sg = _sigmoid(gv)
            dyv = dy_ref[:, c0:c0 + MEM_HD]
            dg_ref[:, c0:c0 + MEM_HD] = (dyv * o_ref[:, c0:c0 + MEM_HD] * (sg * (1.0 + gv * (1.0 - sg)))).astype(BF)
            do = (dyv * (gv * sg)).astype(BF)
            pr = _mem_probs(qh, mk)
            dp = _dot(do, mv, NT)
            ds = pr * (dp - jnp.sum(pr * dp, axis=-1, keepdims=True))
            dsb = (ds * (MEM_HD ** -0.5)).astype(BF)
            dq_ref[:, c0:c0 + MEM_HD] = _dot(dsb, mk, NN).astype(BF)
            dmkv_ref[:, c0:c0 + MEM_HD] += _dot(dsb, qh, TN)
            dmkv_ref[:, D_MEM + c0:D_MEM + c0 + MEM_HD] += _dot(pr.astype(BF), do, TN)

    blk = pl.BlockSpec((tq, D_MEM), lambda i: (i, 0))
    whole = pl.BlockSpec((M, 2 * D_MEM), lambda i: (0, 0))
    return pl.pallas_call(
        body, name="mem_bwd", grid=(S // tq,),
        in_specs=[blk, whole, blk, blk, blk], out_specs=[blk, blk, whole],
        out_shape=[jax.ShapeDtypeStruct((S, D_MEM), BF), jax.ShapeDtypeStruct((S, D_MEM), BF),
                   jax.ShapeDtypeStruct((M, 2 * D_MEM), F32)],
        compiler_params=_params(("arbitrary",)),
    )(q, mkv, g, o, dy)


def _merge_specs(tm):
    ytile = pl.BlockSpec((tm, 1024), lambda i, j: (i, 0))
    wblk = pl.BlockSpec((None, 1024, 256), lambda i, j: (j, 0, 0))
    gls = [pl.BlockSpec((tm, 256), (lambda i, j, br=br: (i, br * 8 + j))) for br in range(3)]
    otile = pl.BlockSpec((tm, 256), lambda i, j: (i, j))
    return ytile, wblk, gls, otile


def _merge_fwd(ys, ws, gl, tm):
    S = gl.shape[0]

    def body(y0, y1, y2, w0, w1, w2, g0, g1, g2, o_ref):
        acc = None
        for y_ref, w_ref, g_ref in ((y0, w0, g0), (y1, w1, g1), (y2, w2, g2)):
            term = _sigmoid(g_ref[...]) * _dot(y_ref[...], w_ref[...], NN)
            acc = term if acc is None else acc + term
        o_ref[...] = acc.astype(BF)

    ytile, wblk, gls, otile = _merge_specs(tm)
    return pl.pallas_call(
        body, name="merge_fwd", grid=(S // tm, 8),
        in_specs=[ytile] * 3 + [wblk] * 3 + gls, out_specs=otile,
        out_shape=jax.ShapeDtypeStruct((S, D_MODEL), BF),
        compiler_params=_params(("parallel", "arbitrary")),
    )(*ys, *ws, gl, gl, gl)


def _merge_bwd(dout, w_out, ys, ws, gl, tm):
    S = gl.shape[0]

    def body(do_ref, wo_ref, y0, y1, y2, w0, w1, w2, g0, g1, g2, dg0, dg1, dg2, dp0, dp1, dp2):
        dm = _dot(do_ref[...], wo_ref[...], NT)
        for y_ref, w_ref, g_ref, dg_ref, dp_ref in ((y0, w0, g0, dg0, dp0), (y1, w1, g1, dg1, dp1), (y2, w2, g2, dg2, dp2)):
            gate = _sigmoid(g_ref[...])
            pv = _dot(y_ref[...], w_ref[...], NN)
            dg_ref[...] = (dm * pv * gate * (1.0 - gate)).astype(BF)
            dp_ref[...] = (dm * gate).astype(BF)

    ytile, wblk, gls, otile = _merge_specs(tm)
    out = jax.ShapeDtypeStruct((S, D_MODEL), BF)
    return pl.pallas_call(
        body, name="merge_bwd", grid=(S // tm, 8),
        in_specs=[pl.BlockSpec((tm, D_MODEL), lambda i, j: (i, 0)), pl.BlockSpec((256, D_MODEL), lambda i, j: (j, 0))]
        + [ytile] * 3 + [wblk] * 3 + gls,
        out_specs=[otile] * 6, out_shape=[out] * 6,
        compiler_params=_params(("parallel", "arbitrary")),
    )(dout, w_out, *ys, *ws, gl, gl, gl)


def _out_loss(merged, w_out, x, target, post_g, tm):
    S = x.shape[0]

    def body(m_ref, w_ref, x_ref, t_ref, g_ref, dout_ref, dy_ref, loss_ref, dpost_ref):
        @pl.when(pl.program_id(0) == 0)
        def _():
            loss_ref[...] = jnp.zeros_like(loss_ref)
            dpost_ref[...] = jnp.zeros_like(dpost_ref)

        out = _dot(m_ref[...], w_ref[...], NN)
        r = lax.rsqrt(jnp.mean(out * out, axis=-1, keepdims=True) + EPS)
        nrm = out * r
        gv = g_ref[...]
        err = (x_ref[...] + nrm * gv) - t_ref[...]
        sq = jnp.sum(jnp.sum(err * err, axis=1, keepdims=True), axis=0, keepdims=True)
        loss_ref[...] += sq * (0.5 / D_MODEL)
        dy = err * (1.0 / D_MODEL)
        dy_ref[...] = dy
        dpost_ref[...] += jnp.sum(dy * nrm, axis=0, keepdims=True)
        dn = dy * gv
        dout_ref[...] = (r * (dn - nrm * jnp.mean(dn * nrm, axis=-1, keepdims=True))).astype(BF)

    row = pl.BlockSpec((tm, D_MODEL), lambda i: (i, 0))
    return pl.pallas_call(
        body, name="out_loss", grid=(S // tm,),
        in_specs=[row, pl.BlockSpec((D_MODEL, D_MODEL), lambda i: (0, 0)), row, row, pl.BlockSpec((1, D_MODEL), lambda i: (0, 0))],
        out_specs=[row, row, pl.BlockSpec((8, 128), lambda i: (0, 0)), pl.BlockSpec((1, D_MODEL), lambda i: (0, 0))],
        out_shape=[jax.ShapeDtypeStruct((S, D_MODEL), BF), jax.ShapeDtypeStruct((S, D_MODEL), F32),
                   jax.ShapeDtypeStruct((8, 128), F32), jax.ShapeDtypeStruct((1, D_MODEL), F32)],
        compiler_params=_params(("arbitrary",)),
    )(merged, w_out, x, target, post_g)


def _dh_dx(dproj, w_in, x, dy, pre_g, tm, tk):
    S = x.shape[0]
    nk = D_IN // tk

    def body(dp_ref, w_ref, x_ref, dy_ref, g_ref, dx_ref, dpre_ref, acc_ref):
        i, k = pl.program_id(0), pl.program_id(1)

        @pl.when(jnp.logical_and(i == 0, k == 0))
        def _():
            dpre_ref[...] = jnp.zeros_like(dpre_ref)

        @pl.when(k == 0)
        def _():
            acc_ref[...] = jnp.zeros_like(acc_ref)

        acc_ref[...] += _dot(dp_ref[...], w_ref[...], NT)

        @pl.when(k == nk - 1)
        def _():
            dh = acc_ref[...]
            xv = x_ref[...]
            r = lax.rsqrt(jnp.mean(xv * xv, axis=-1, keepdims=True) + EPS)
            nrm = xv * r
            dpre_ref[...] += jnp.sum(dh * nrm, axis=0, keepdims=True)
            dn = dh * g_ref[...]
            dx_ref[...] = r * (dn - nrm * jnp.mean(dn * nrm, axis=-1, keepdims=True)) + dy_ref[...]

    row = pl.BlockSpec((tm, D_MODEL), lambda i, k: (i, 0))
    vec = pl.BlockSpec((1, D_MODEL), lambda i, k: (0, 0))
    return pl.pallas_call(
        body, name="dh_dx", grid=(S // tm, nk),
        in_specs=[pl.BlockSpec((tm, tk), lambda i, k: (i, k)), pl.BlockSpec((D_MODEL, tk), lambda i, k: (0, k)), row, row, vec],
        out_specs=[row, vec],
        out_shape=[jax.ShapeDtypeStruct((S, D_MODEL), F32), jax.ShapeDtypeStruct((1, D_MODEL), F32)],
        scratch_shapes=[pltpu.VMEM((tm, D_MODEL), F32)],
        compiler_params=_params(("arbitrary", "arbitrary"), 56),
    )(dproj, w_in, x, dy, pre_g)


def _sum_parts(parts, name):
    P, R, C = parts.shape
    tr = max(t for t in range(8, 513, 8) if R % t == 0)

    def body(p_ref, o_ref):
        acc = p_ref[0]
        for j in range(1, P):
            acc = acc + p_ref[j]
        o_ref[...] = acc

    return pl.pallas_call(
        body, name=name, grid=(R // tr,),
        in_specs=[pl.BlockSpec((P, tr, C), lambda i: (0, i, 0))], out_specs=pl.BlockSpec((tr, C), lambda i: (i, 0)),
        out_shape=jax.ShapeDtypeStruct((R, C), F32), compiler_params=_params(("parallel",)),
    )(parts)


def _adamw(parts, w, m, v, name):
    P, R, C = parts.shape
    tr = 128 if R % 128 == 0 else R
    c1 = 1.0 - ADAM_B1 ** ADAM_STEP
    c2 = 1.0 - ADAM_B2 ** ADAM_STEP

    def body(p_ref, w_ref, m_ref, v_ref, g_ref, d_ref, nm_ref, nv_ref):
        g = p_ref[0].astype(F32)
        for j in range(1, P):
            g = g + p_ref[j].astype(F32)
        nm = ADAM_B1 * m_ref[...] + (1.0 - ADAM_B1) * g
        nv = ADAM_B2 * v_ref[...] + (1.0 - ADAM_B2) * (g * g)
        g_ref[...] = g
        nm_ref[...] = nm
        nv_ref[...] = nv
        d_ref[...] = -ADAM_LR * ((nm / c1) / (jnp.sqrt(nv / c2) + ADAM_EPS) + ADAM_WD * w_ref[...])

    tile = pl.BlockSpec((tr, C), lambda i: (i, 0))
    out = jax.ShapeDtypeStruct((R, C), F32)
    return pl.pallas_call(
        body, name=name, grid=(R // tr,),
        in_specs=[pl.BlockSpec((P, tr, C), lambda i: (0, i, 0)), tile, tile, tile], out_specs=[tile] * 4, out_shape=[out] * 4,
        compiler_params=_params(("parallel",)),
    )(parts, w, m, v)


def _forward_a(x, mem, pre_g, mem_g, w_in, conv_w, conv_b, w_a, b_a, w_x, b_x, lam, sinks, rel_bias):
    S = x.shape[0]
    st = dict(T=min(512, S // 2), tm=min(512, S), bucket=_rel_bucket_map())
    st["h"] = _rms_fwd(x, pre_g, "pre_norm")
    st["memn"] = _rms_fwd(mem, mem_g, "mem_norm")
    seg = {}
    for name, c0, width, dt in SEGMENTS:
        seg[name] = _matmul(st["h"], w_in, "nn", S, width, D_MODEL, S, SEG_TILE, D_MODEL, dt, "proj_" + name, b_noff=c0 // SEG_TILE)
    st["seg"] = seg
    st["h_rg"], st["y_rg"] = _rglru_fwd(seg["xr"], seg["g_rg"], conv_w, conv_b, w_a, b_a, w_x, b_x, lam, st["T"])
    st["o_swa"], st["y_swa"] = _swa_fwd(seg["q_s"], seg["kv"], seg["g_swa"], st["bucket"], rel_bias, sinks)
    return st


def _forward_b(st, x, target, post_g, w_memkv, wbr, w_out):
    S = x.shape[0]
    M = st["memn"].shape[0]
    seg = st["seg"]
    st["mkv"] = _matmul(st["memn"], w_memkv, "nn", M, 2 * D_MEM, D_MODEL, M, 512, D_MODEL, BF, "mem_kv")
    st["o_mem"], st["y_mem"] = _mem_fwd(seg["q_m"], st["mkv"], seg["g_mem"])
    st["ys"] = (st["y_rg"], st["y_swa"], st["y_mem"])
    st["merged"] = _merge_fwd(st["ys"], wbr, seg["gl"], st["tm"])
    st["dout"], st["dy"], st["loss"], st["dpost"] = _out_loss(st["merged"], w_out, x, target, post_g, min(256, S))
    return st


def _backward_a(st, mem, w_memkv, wbr, w_out, conv_w, conv_b, w_a, b_a, w_x, b_x, lam):
    S = st["h"].shape[0]
    M = mem.shape[0]
    seg, ys, tm = st["seg"], st["ys"], st["tm"]
    st["dw_out"] = _matmul(st["merged"], st["dout"], "tn", D_MODEL, D_MODEL, S, 256, D_MODEL, S, BF, "dw_out", out_blocked="row")
    dgl0, dgl1, dgl2, dp0, dp1, dp2 = _merge_bwd(st["dout"], w_out, ys, wbr, seg["gl"], tm)
    st["dgl"] = (dgl0, dgl1, dgl2)
    dys, dwbr = [], []
    for i, dp in enumerate((dp0, dp1, dp2)):
        dys.append(_matmul(dp, wbr[i], "nt", S, 1024, D_MODEL, tm, 1024, 256, F32, "dy_br%d" % i, b_blocked=True))
        dwbr.append(_matmul(ys[i], dp, "tn", 1024, D_MODEL, S, 1024, 256, S, BF, "dw_br%d" % i, out_blocked="col"))
    st["dys"], st["dwbr"] = dys, dwbr
    st["dq_m"], st["dg_mem"], dmkv = _mem_bwd(seg["q_m"], st["mkv"], seg["g_mem"], st["o_mem"], dys[2])
    dmkv_b = dmkv.astype(BF)
    st["dw_memkv"] = _matmul(st["memn"], dmkv_b, "tn", D_MODEL, 2 * D_MEM, M, 256, 2 * D_MEM, M, BF, "dw_memkv", out_blocked="row")
    dmemn = _matmul(dmkv_b, w_memkv, "nt", M, D_MODEL, 2 * D_MEM, M, 512, 2 * D_MEM, F32, "dmemn")
    st["dmem_g"] = _rms_gain_grad(dmemn, mem, "dmem_gain")
    st["dxr"], st["dg_rg"], st["dw_a"], st["dw_x"], st["dvec"] = _rglru_bwd(
        seg["xr"], seg["g_rg"], st["h_rg"], dys[0], conv_w, conv_b, w_a, b_a, w_x, b_x, lam, st["T"])
    return st


def _backward_b(st, rel_bias, sinks):
    S = st["h"].shape[0]
    seg = st["seg"]
    dq_s, dg_swa, dkv, st["dsinks"], st["drel"] = _swa_bwd(seg["q_s"], seg["kv"], seg["g_swa"], st["o_swa"], st["dys"][1],
                                                           st["bucket"], rel_bias, sinks)
    st["dproj"] = jnp.concatenate([st["dxr"], st["dg_rg"], dq_s, dkv.astype(BF), dg_swa, st["dq_m"], st["dg_mem"], *st["dgl"]], axis=1)
    st["dw_in"] = _matmul(st["h"], st["dproj"], "tn", D_MODEL, D_IN, S, 512, 1792, S, BF, "dw_in")
    return st


def _owner_blocks(a):
    return jnp.swapaxes(a.reshape((4, 2) + a.shape[1:]), 0, 1)


def _local_step(x, mem, target, pre_g, post_g, mem_g, w_in, conv_w, conv_b, w_a, b_a, w_x, b_x, lam, sinks, rel_bias,
                w_memkv, wbr, w_out):
    st = _forward_a(x, mem, pre_g, mem_g, w_in, conv_w, conv_b, w_a, b_a, w_x, b_x, lam, sinks, rel_bias)
    st = _forward_b(st, x, target, post_g, w_memkv, wbr, w_out)
    st = _backward_a(st, mem, w_memkv, wbr, w_out, conv_w, conv_b, w_a, b_a, w_x, b_x, lam)
    st = _backward_b(st, rel_bias, sinks)
    st["grad_x"], st["dpre"] = _dh_dx(st["dproj"], w_in, x, st["dy"], pre_g, st["tm"], 896)
    return st


def _pad_rows(a, rows):
    a = a.reshape(-1, 128) if a.shape[-1] % 128 == 0 else jnp.pad(a, ((0, 0), (0, 128 - a.shape[-1])))
    return jnp.pad(a, ((0, rows - a.shape[0]), (0, 0))) if a.shape[0] < rows else a


def kernel(x, mem, pre_norm_g, post_norm_g, mem_norm_g, w_in, conv_w, conv_b, w_rg_a, b_rg_a, w_rg_x, b_rg_x, lru_lambda, swa_sinks, rel_bias, w_mem_kv, w_br_rg, w_br_swa, w_br_mem, w_out, loss_target, m_pre_norm_g, m_post_norm_g, m_mem_norm_g, m_w_in, m_conv_w, m_conv_b, m_w_rg_a, m_b_rg_a, m_w_rg_x, m_b_rg_x, m_lru_lambda, m_swa_sinks, m_rel_bias, m_w_mem_kv, m_w_br_rg, m_w_br_swa, m_w_br_mem, m_w_out, v_pre_norm_g, v_post_norm_g, v_mem_norm_g, v_w_in, v_conv_w, v_conv_b, v_w_rg_a, v_b_rg_a, v_w_rg_x, v_b_rg_x, v_lru_lambda, v_swa_sinks, v_rel_bias, v_w_mem_kv, v_w_br_rg, v_w_br_swa, v_w_br_mem, v_w_out):
    cx, cy, cc = lax.axis_index("x"), lax.axis_index("y"), lax.axis_index("c")
    me = 4 * cx + 2 * cy + cc
    chip = 2 * cx + cy
    core = jnp.reshape(cc, (1,)).astype(jnp.int32)
    x0, mem0 = x[0], mem[0]
    w_a_b, w_x_b = w_rg_a[0].astype(BF), w_rg_x[0].astype(BF)

    def landing(own, slot, slots):
        return lax.dynamic_update_slice(lax.empty((slots,) + own.shape, own.dtype), own[None], (slot,) + (0,) * own.ndim)

    def scatter_start(parts, tag):
        got = _swap_with_sibling(parts, "scatter_%s_swap" % tag)
        sums = [_pair_sum(p, g, core, "scatter_%s_sum%d" % (tag, i)) for i, (p, g) in enumerate(zip(parts, got))]
        lands = [landing(lax.dynamic_index_in_dim(s, chip, 0, keepdims=False), chip, 4) for s in sums]
        return _exchange_start(sums, lands, _plan_scatter(len(sums)), "scatter_%s_start" % tag)

    g_in, g_cw = _all_gather([w_in[0].astype(BF), conv_w[0]], "gather_w_in")
    w_in_f = jnp.transpose(g_in, (1, 0, 2)).reshape(D_MODEL, D_IN)
    conv_w_f = jnp.transpose(g_cw, (1, 0, 2)).reshape(CONV_W, D_RNN)

    after_first = jnp.minimum(jnp.abs(g_cw[0, 0, 0]), 0.0).astype(BF)
    rest = [w.astype(BF) + after_first for w in (w_mem_kv[0], w_br_rg[0], w_br_swa[0], w_br_mem[0], w_out[0])]
    plan_g = _plan_gather(len(rest))
    g_send, g_recv, g_src, g_land, g_token = _exchange_start(rest, [landing(w, me, N_DEV) for w in rest], plan_g, "gather_rest_start")
    st = _forward_a(x0, mem0, pre_norm_g + g_token[0:1, 0:1], mem_norm_g, w_in_f, conv_w_f, conv_b, w_a_b, b_rg_a, w_x_b, b_rg_x,
                    lru_lambda, swa_sinks, rel_bias)
    g_land = _exchange_wait(g_send, g_recv, g_src, g_land, plan_g, st["y_swa"], "gather_rest_wait")
    g_land = _forward_to_sibling(g_land, "gather_rest_forward")
    w_memkv_f = g_land[0].reshape(D_MODEL, 2 * D_MEM)
    wbr = (g_land[1], g_land[2], g_land[3])
    w_out_f = g_land[4].reshape(D_MODEL, D_MODEL)

    st = _forward_b(st, x0, loss_target[0], post_norm_g, w_memkv_f, wbr, w_out_f)
    st = _backward_a(st, mem0, w_memkv_f, wbr, w_out_f, conv_w_f, conv_b, w_a_b, b_rg_a, w_x_b, b_rg_x, lru_lambda)
    parts_a = [st["dw_memkv"], st["dwbr"][0], st["dwbr"][1], st["dwbr"][2], st["dw_out"],
               _owner_blocks(st["dw_a"]), _owner_blocks(st["dw_x"])]
    plan_a = _plan_scatter(len(parts_a))
    a_send, a_recv, a_src, a_land, a_token = scatter_start(parts_a, "a")

    st = _backward_b(st, rel_bias, swa_sinks + a_token[0:1, 0:1])
    parts_b = [jnp.transpose(st["dw_in"].reshape(D_MODEL, 4, 2, D_IN // N_DEV), (2, 1, 0, 3))]
    plan_b = _plan_scatter(1)
    b_send, b_recv, b_src, b_land, b_token = scatter_start(parts_b, "b")
    grad_x, dpre = _dh_dx(st["dproj"], w_in_f, x0, st["dy"], pre_norm_g + b_token[0:1, 0:1], st["tm"], 896)
    a_land = _exchange_wait(a_send, a_recv, a_src, a_land, plan_a, grad_x, "scatter_a_wait")
    g_wa_blk = _sum_parts(a_land[5], "sum_w_rg_a")
    g_wx_blk = _sum_parts(a_land[6], "sum_w_rg_x")

    pack = jnp.concatenate([dpre.reshape(16, 128), st["dpost"].reshape(16, 128), st["dmem_g"].reshape(16, 128),
                            st["dvec"].reshape(64, 128), _pad_rows(st["dsinks"], 8), st["drel"], g_wa_blk, g_wx_blk], axis=0)
    gathered = _all_gather([pack], "gather_small")[0]
    gs = _sum_parts(gathered, "sum_small")
    g_pre, g_post, g_memg = gs[0:16].reshape(1, D_MODEL), gs[16:32].reshape(1, D_MODEL), gs[32:48].reshape(1, D_MODEL)
    gvec = gs[48:112].reshape(8, D_RNN)
    g_conv_w = lax.dynamic_slice(gvec[0:CONV_W], (0, me * RNN_BLOCK), (CONV_W, RNN_BLOCK))
    g_conv_b, g_b_a, g_b_x, g_lam = gvec[4:5], gvec[5:6], gvec[6:7], gvec[7:8]
    g_sinks = gs[112:113, :SWA_HEADS]
    g_rel = gs[120:152, :SWA_HEADS]
    g_w_a = gathered[:, 152:280]
    g_w_x = gathered[:, 280:408]

    def packed(ts):
        pre, post, memg, cb, ba, bx, lm, wa, wx, sk, rel, cw = ts
        return jnp.concatenate([pre.reshape(16, 128), post.reshape(16, 128), memg.reshape(16, 128), cb.reshape(8, 128),
                                ba.reshape(8, 128), bx.reshape(8, 128), lm.reshape(8, 128), wa.reshape(1024, 128),
                                wx.reshape(1024, 128), _pad_rows(sk.reshape(1, SWA_HEADS), 8), _pad_rows(rel, 32),
                                _pad_rows(cw.reshape(CONV_W, RNN_BLOCK), 8)], axis=0)

    def unpacked(a):
        return (a[0:16].reshape(1, D_MODEL), a[16:32].reshape(1, D_MODEL), a[32:48].reshape(1, D_MODEL), a[48:56].reshape(1, D_RNN),
                a[56:64].reshape(1, D_RNN), a[64:72].reshape(1, D_RNN), a[72:80].reshape(1, D_RNN),
                a[80:1104].reshape(1, RNN_BLOCKS, RNN_BLOCK, RNN_BLOCK), a[1104:2128].reshape(1, RNN_BLOCKS, RNN_BLOCK, RNN_BLOCK),
                a[2128:2129, :SWA_HEADS], a[2136:2168, :SWA_HEADS], a[2168:2172].reshape(1, CONV_W, RNN_BLOCK))

    g_small = (g_pre, g_post, g_memg, g_conv_b, g_b_a, g_b_x, g_lam, g_w_a, g_w_x, g_sinks, g_rel, g_conv_w)
    w_small = (pre_norm_g, post_norm_g, mem_norm_g, conv_b, b_rg_a, b_rg_x, lru_lambda, w_rg_a, w_rg_x, swa_sinks, rel_bias, conv_w)
    m_small = (m_pre_norm_g, m_post_norm_g, m_mem_norm_g, m_conv_b, m_b_rg_a, m_b_rg_x, m_lru_lambda, m_w_rg_a, m_w_rg_x, m_swa_sinks, m_rel_bias, m_conv_w)
    v_small = (v_pre_norm_g, v_post_norm_g, v_mem_norm_g, v_conv_b, v_b_rg_a, v_b_rg_x, v_lru_lambda, v_w_rg_a, v_w_rg_x, v_swa_sinks, v_rel_bias, v_conv_w)
    sm = [unpacked(a) for a in _adamw(packed(g_small)[None], packed(w_small), packed(m_small), packed(v_small), "adamw_small")]

    big = [None]
    for j, (wt, mt, vt) in enumerate(((w_mem_kv, m_w_mem_kv, v_w_mem_kv), (w_br_rg, m_w_br_rg, v_w_br_rg),
                                      (w_br_swa, m_w_br_swa, v_w_br_swa), (w_br_mem, m_w_br_mem, v_w_br_mem), (w_out, m_w_out, v_w_out))):
        big.append([a[None] for a in _adamw(a_land[j], wt[0], mt[0], vt[0], "adamw_big%d" % (j + 1))])
    b_land = _exchange_wait(b_send, b_recv, b_src, b_land, plan_b, big[5][1], "scatter_b_wait")
    big[0] = [a[None] for a in _adamw(b_land[0], w_in[0], m_w_in[0], v_w_in[0], "adamw_big0")]

    loss_total = lax.psum(st["loss"][0, 0], AXES)

    def leaves(k):
        s = sm[k]
        return [s[0], s[1], s[2], big[0][k], s[11], s[3], s[7], s[4], s[8], s[5], s[6], s[9], s[10],
                big[1][k], big[2][k], big[3][k], big[4][k], big[5][k]]

    return (loss_total, grad_x[None], *leaves(0), *leaves(1), *leaves(2), *leaves(3))
```

```python
import math

import jax
import jax.numpy as jnp
import numpy as np
from jax import lax
from jax.experimental import pallas as pl
from jax.experimental.pallas import tpu as pltpu

F32, BF = jnp.float32, jnp.bfloat16
MESH = pl.DeviceIdType.MESH
AXES = ("x", "y", "c")
N_DEV = 8

D_MODEL = 2048
D_RNN = 1024
RNN_BLOCKS = 8
RNN_BLOCK = 128
CONV_W = 4
LRU_C = 8.0
SWA_HEADS = 16
SWA_KV_HEADS = 2
SWA_HD = 64
WINDOW = 128
MEM_HEADS = 4
MEM_HD = 256
D_MEM = 1024
REL_BUCKETS = 32
REL_MAX_DIST = 128
EPS = 1e-6
NEG_INF = -1e30
D_IN = 12544
SEGMENTS = (("xr", 0, 1024, F32), ("g_rg", 1024, 1024, F32), ("q_s", 2048, 1024, BF), ("kv", 3072, 256, BF),
            ("g_swa", 3328, 1024, F32), ("q_m", 4352, 1024, BF), ("g_mem", 5376, 1024, F32), ("gl", 6400, 6144, F32))
SEG_TILE = 256

ADAM_LR, ADAM_B1, ADAM_B2, ADAM_EPS, ADAM_WD, ADAM_STEP = 0.001, 0.9, 0.999, 1e-08, 0.01, 10

NN = (((1,), (0,)), ((), ()))
NT = (((1,), (1,)), ((), ()))
TN = (((0,), (0,)), ((), ()))
MIB = 2 ** 20


def _dot(a, b, dn):
    return lax.dot_general(a, b, dn, preferred_element_type=F32)


def _params(sem, vmem_mib=48):
    return pltpu.CompilerParams(dimension_semantics=sem, vmem_limit_bytes=vmem_mib * MIB)


def _sigmoid(z):
    return 1.0 / (1.0 + jnp.exp(-z))


def _softplus(z):
    return jnp.maximum(z, 0.0) + jnp.log(1.0 + jnp.exp(-jnp.abs(z)))


def _expm1(z):
    p = z * (1.0 + z * (0.5 + z * (1.0 / 6 + z * (1.0 / 24 + z * (1.0 / 120 + z * (1.0 / 720 + z * (1.0 / 5040 + z / 40320)))))))
    return jnp.where(jnp.abs(z) < 0.3, p, jnp.exp(z) - 1.0)


def _flat(p):
    return 4 * p[0] + 2 * p[1] + p[2]


def _all_gather(arrs, name):
    n = len(arrs)

    def body(*refs):
        ins, outs = refs[:n], refs[n:2 * n]
        send_sems, recv_sems, local_sems = refs[2 * n:]
        x, y, c = lax.axis_index("x"), lax.axis_index("y"), lax.axis_index("c")
        me, sibling = (x, y, c), (x, y, 1 - c)
        chips = [(1 - x, y), (x, 1 - y), (1 - x, 1 - y)]

        def copy(a, k, block, to, src=None):
            dst = outs[a].at[_flat(block)]
            return pltpu.make_async_remote_copy(src_ref=dst if src is None else src, dst_ref=dst,
                                                send_sem=send_sems.at[a * 7 + k], recv_sem=recv_sems.at[a * 7 + k],
                                                device_id=to, device_id_type=MESH)

        mine = [pltpu.make_async_copy(ins[a], outs[a].at[_flat(me)], local_sems.at[a]) for a in range(n)]
        for cp in mine:
            cp.start()
        first = []
        for a in range(n):
            first += [copy(a, 1 + j, me, (*chip, c), src=ins[a]) for j, chip in enumerate(chips)]
            first.append(copy(a, 0, me, sibling, src=ins[a]))
        for cp in first:
            cp.start()
        passed = []
        for j, chip in enumerate(chips):
            for a in range(n):
                copy(a, 1 + j, (*chip, c), me).wait_recv()
                fw = copy(a, 4 + j, (*chip, c), sibling)
                fw.start()
                passed.append(fw)
        for a in range(n):
            copy(a, 0, sibling, me).wait_recv()
            for j, chip in enumerate(chips):
                copy(a, 4 + j, (*chip, 1 - c), me).wait_recv()
        for cp in first + passed:
            cp.wait_send()
        for cp in mine:
            cp.wait()

    any_spec = pl.BlockSpec(memory_space=pl.ANY)
    return pl.pallas_call(
        body, name=name,
        out_shape=[jax.ShapeDtypeStruct((N_DEV,) + a.shape, a.dtype) for a in arrs],
        in_specs=[any_spec] * n, out_specs=[any_spec] * n,
        scratch_shapes=[pltpu.SemaphoreType.DMA((7 * n,)), pltpu.SemaphoreType.DMA((7 * n,)), pltpu.SemaphoreType.DMA((n,))],
    )(*arrs)


def _chip_peers(x, y):
    return [(1 - x, y), (x, 1 - y), (1 - x, 1 - y)]


def _chip(p):
    return 2 * p[0] + p[1]


def _plan_gather(n):
    def plan(x, y, c):
        out = []
        for a in range(n):
            for peer in [(x, y, 1 - c)] + [(*ch, c) for ch in _chip_peers(x, y)]:
                out.append((a, None, _flat((x, y, c)), peer, _flat(peer)))
        return out
    return plan


def _plan_scatter(n):
    def plan(x, y, c):
        out = []
        for a in range(n):
            for ch in _chip_peers(x, y):
                out.append((a, _chip(ch), _chip((x, y)), (*ch, c), _chip(ch)))
        return out
    return plan


HBM_SPEC = pl.BlockSpec(memory_space=pltpu.HBM)
SEM_SPEC = pl.BlockSpec(memory_space=pltpu.SEMAPHORE)


def _in_hbm(a):
    return pltpu.with_memory_space_constraint(a, pltpu.HBM)


def _exchange_start(srcs, lands, plan, name):
    n = len(srcs)
    count = len(plan(0, 0, 0))

    def body(*refs):
        src_refs, land_refs = refs[:n], refs[n:2 * n]
        send_sems, recv_sems = refs[2 * n], refs[2 * n + 1]
        token = refs[-1]
        x, y, c = lax.axis_index("x"), lax.axis_index("y"), lax.axis_index("c")
        for k, (a, si, di, peer, _) in enumerate(plan(x, y, c)):
            src = src_refs[a] if si is None else src_refs[a].at[si]
            pltpu.make_async_remote_copy(src_ref=src, dst_ref=land_refs[a].at[di], send_sem=send_sems.at[k],
                                         recv_sem=recv_sems.at[k], device_id=peer, device_id_type=MESH).start()
        token[...] = jnp.zeros_like(token)

    out = pl.pallas_call(
        body, name=name,
        out_shape=(pltpu.SemaphoreType.DMA((count,)), pltpu.SemaphoreType.DMA((count,)),
                   *[pltpu.HBM(a.shape, a.dtype) for a in srcs], *[pltpu.HBM(a.shape, a.dtype) for a in lands],
                   jax.ShapeDtypeStruct((8, 128), F32)),
        in_specs=[HBM_SPEC] * (2 * n),
        out_specs=(SEM_SPEC, SEM_SPEC, *([HBM_SPEC] * (2 * n)), pl.BlockSpec(memory_space=pltpu.VMEM)),
        input_output_aliases={i: 2 + i for i in range(2 * n)},
        compiler_params=pltpu.CompilerParams(has_side_effects=pltpu.SideEffectType.DATAFLOW_SIDE_EFFECTING),
    )(*[_in_hbm(a) for a in srcs], *[_in_hbm(a) for a in lands])
    return out[0], out[1], list(out[2:2 + n]), list(out[2 + n:2 + 2 * n]), out[-1]


def _exchange_wait(send_sems, recv_sems, srcs, lands, plan, after, name):
    n = len(srcs)

    def body(*refs):
        src_refs, land_refs = refs[:n], refs[n:2 * n]
        send_sems, recv_sems = refs[2 * n], refs[2 * n + 1]
        x, y, c = lax.axis_index("x"), lax.axis_index("y"), lax.axis_index("c")
        for k, (a, si, _, peer, ri) in enumerate(plan(x, y, c)):
            src = src_refs[a] if si is None else src_refs[a].at[si]
            cp = pltpu.make_async_remote_copy(src_ref=src, dst_ref=land_refs[a].at[ri], send_sem=send_sems.at[k],
                                              recv_sem=recv_sems.at[k], device_id=peer, device_id_type=MESH)
            cp.wait_send()
            cp.wait_recv()

    out = pl.pallas_call(
        body, name=name,
        out_shape=(*[pltpu.HBM(a.shape, a.dtype) for a in srcs], *[pltpu.HBM(a.shape, a.dtype) for a in lands]),
        in_specs=[HBM_SPEC] * (2 * n) + [SEM_SPEC, SEM_SPEC, pl.BlockSpec(memory_space=pl.ANY)],
        out_specs=tuple([HBM_SPEC] * (2 * n)),
        input_output_aliases={i: i for i in range(2 * n)},
        compiler_params=pltpu.CompilerParams(has_side_effects=pltpu.SideEffectType.DATAFLOW_SIDE_EFFECTING),
    )(*srcs, *lands, send_sems, recv_sems, after)
    return list(out[n:2 * n])


def _forward_to_sibling(lands, name):
    n = len(lands)

    def body(*refs):
        in_refs, out_refs = refs[:n], refs[n:2 * n]
        send_sems, recv_sems = refs[2 * n:]
        x, y, c = lax.axis_index("x"), lax.axis_index("y"), lax.axis_index("c")
        sibling = (x, y, 1 - c)

        def copy(a, j, slot):
            return pltpu.make_async_remote_copy(src_ref=in_refs[a].at[slot], dst_ref=out_refs[a].at[slot],
                                                send_sem=send_sems.at[a * 3 + j], recv_sem=recv_sems.at[a * 3 + j],
                                                device_id=sibling, device_id_type=MESH)

        sends = [copy(a, j, _flat((*ch, c))) for a in range(n) for j, ch in enumerate(_chip_peers(x, y))]
        for cp in sends:
            cp.start()
        for a in range(n):
            for j, ch in enumerate(_chip_peers(x, y)):
                copy(a, j, _flat((*ch, 1 - c))).wait_recv()
        for cp in sends:
            cp.wait_send()

    any_spec = pl.BlockSpec(memory_space=pl.ANY)
    return pl.pallas_call(
        body, name=name, out_shape=[jax.ShapeDtypeStruct(a.shape, a.dtype) for a in lands],
        in_specs=[any_spec] * n, out_specs=[any_spec] * n, input_output_aliases={a: a for a in range(n)},
        scratch_shapes=[pltpu.SemaphoreType.DMA((3 * n,)), pltpu.SemaphoreType.DMA((3 * n,))],
    )(*lands)


def _swap_with_sibling(parts, name):
    n = len(parts)

    def body(*refs):
        in_refs, out_refs = refs[:n], refs[n:2 * n]
        send_sems, recv_sems = refs[2 * n:]
        x, y, c = lax.axis_index("x"), lax.axis_index("y"), lax.axis_index("c")
        sends = [pltpu.make_async_remote_copy(src_ref=in_refs[a].at[1 - c], dst_ref=out_refs[a], send_sem=send_sems.at[a],
                                              recv_sem=recv_sems.at[a], device_id=(x, y, 1 - c), device_id_type=MESH)
                 for a in range(n)]
        for cp in sends:
            cp.start()
        for cp in sends:
            cp.wait()

    any_spec = pl.BlockSpec(memory_space=pl.ANY)
    return pl.pallas_call(
        body, name=name, out_shape=[jax.ShapeDtypeStruct(a.shape[1:], a.dtype) for a in parts],
        in_specs=[any_spec] * n, out_specs=[any_spec] * n,
        scratch_shapes=[pltpu.SemaphoreType.DMA((n,)), pltpu.SemaphoreType.DMA((n,))],
    )(*parts)


def _pair_sum(parts, got, core, name):
    _, _, R, C = parts.shape
    tr = 256 if R % 256 == 0 else R

    def body(c_ref, p_ref, g_ref, o_ref):
        o_ref[...] = (p_ref[...].astype(F32) + g_ref[...].astype(F32)).astype(o_ref.dtype)

    return pl.pallas_call(
        body, name=name,
        grid_spec=pltpu.PrefetchScalarGridSpec(
            num_scalar_prefetch=1, grid=(4, R // tr),
            in_specs=[pl.BlockSpec((None, None, tr, C), lambda j, i, c_ref: (c_ref[0], j, i, 0)),
                      pl.BlockSpec((None, tr, C), lambda j, i, c_ref: (j, i, 0))],
            out_specs=pl.BlockSpec((None, tr, C), lambda j, i, c_ref: (j, i, 0))),
        out_shape=jax.ShapeDtypeStruct((4, R, C), parts.dtype),
        compiler_params=_params(("parallel", "parallel")),
    )(core, parts, got)


def _matmul(a, b, mode, M, N, K, tm, tn, tk, out_dtype, name, b_noff=0, a_moff=0, b_blocked=False, out_blocked=None, dep=None,
            vmem_mib=48):
    nm, nn, nk = M // tm, N // tn, K // tk
    if mode == "nn":
        a_spec = pl.BlockSpec((tm, tk), lambda j, i, k: (i, k))
        b_spec = pl.BlockSpec((tk, tn), lambda j, i, k: (k, j + b_noff))
        dn = NN
    elif mode == "nt":
        a_spec = pl.BlockSpec((tm, tk), lambda j, i, k: (i, k))
        if b_blocked:
            b_spec = pl.BlockSpec((None, tn, tk), lambda j, i, k: (k, j, 0))
        else:
            b_spec = pl.BlockSpec((tn, tk), lambda j, i, k: (j + b_noff, k))
        dn = NT
    else:
        a_spec = pl.BlockSpec((tk, tm), lambda j, i, k: (k, i + a_moff))
        b_spec = pl.BlockSpec((tk, tn), lambda j, i, k: (k, j + b_noff))
        dn = TN
    if out_blocked == "col":
        out_shape = jax.ShapeDtypeStruct((2, 4, M, tn), out_dtype)
        out_spec = pl.BlockSpec((None, None, tm, tn), lambda j, i, k: (j % 2, j // 2, i, 0))
    elif out_blocked == "row":
        out_shape = jax.ShapeDtypeStruct((2, 4, tm, N), out_dtype)
        out_spec = pl.BlockSpec((None, None, tm, tn), lambda j, i, k: (i % 2, i // 2, 0, j))
    else:
        out_shape = jax.ShapeDtypeStruct((M, N), out_dtype)
        out_spec = pl.BlockSpec((tm, tn), lambda j, i, k: (i, j))

    def body(a_ref, b_ref, *rest):
        o_ref, scratch = (rest[1], rest[2:]) if dep is not None else (rest[0], rest[1:])
        if nk == 1:
            o_ref[...] = _dot(a_ref[...], b_ref[...], dn).astype(out_dtype)
        else:
            acc_ref, = scratch
            k = pl.program_id(2)

            @pl.when(k == 0)
            def _():
                acc_ref[...] = jnp.zeros_like(acc_ref)

            acc_ref[...] += _dot(a_ref[...], b_ref[...], dn)

            @pl.when(k == nk - 1)
            def _():
                o_ref[...] = acc_ref[...].astype(out_dtype)

    return pl.pallas_call(
        body, name=name, grid=(nn, nm, nk),
        in_specs=[a_spec, b_spec] + ([] if dep is None else [pl.BlockSpec((8, 128), lambda j, i, k: (0, 0))]),
        out_specs=out_spec, out_shape=out_shape,
        scratch_shapes=[] if nk == 1 else [pltpu.VMEM((tm, tn), F32)],
        compiler_params=_params(("parallel", "parallel", "arbitrary"), vmem_mib),
    )(a, b, *([] if dep is None else [dep]))


def _rms_fwd(x, g, name):
    R, Dm = x.shape
    tr = min(R, 256)

    def body(x_ref, g_ref, h_ref):
        xv = x_ref[...]
        r = lax.rsqrt(jnp.mean(xv * xv, axis=-1, keepdims=True) + EPS)
        h_ref[...] = (xv * r * g_ref[...]).astype(BF)

    return pl.pallas_call(
        body, name=name, grid=(R // tr,),
        in_specs=[pl.BlockSpec((tr, Dm), lambda i: (i, 0)), pl.BlockSpec((1, Dm), lambda i: (0, 0))],
        out_specs=pl.BlockSpec((tr, Dm), lambda i: (i, 0)), out_shape=jax.ShapeDtypeStruct((R, Dm), BF),
        compiler_params=_params(("parallel",)),
    )(x, g)


def _rms_gain_grad(dn, x, name):
    R, Dm = x.shape

    def body(dn_ref, x_ref, o_ref):
        xv = x_ref[...]
        r = lax.rsqrt(jnp.mean(xv * xv, axis=-1, keepdims=True) + EPS)
        o_ref[...] = jnp.sum(dn_ref[...] * xv * r, axis=0, keepdims=True)

    return pl.pallas_call(
        body, name=name, out_shape=jax.ShapeDtypeStruct((1, Dm), F32),
        compiler_params=pltpu.CompilerParams(vmem_limit_bytes=32 * MIB),
    )(dn, x)


def _shift_down(v, k, head8, row, T):
    if k == 0:
        return v
    r = pltpu.roll(v, k, 0)
    hr = pltpu.roll(head8, k, 0)
    top = jnp.where(row[:8] < k, hr, r[:8])
    return jnp.concatenate([top, r[8:]], axis=0)


def _shift_up(v, k, tail8, row, T):
    if k == 0:
        return v
    r = pltpu.roll(v, T - k, 0)
    tr = pltpu.roll(tail8, 8 - k, 0)
    bot = jnp.where(row[:8] >= 8 - k, tr, r[T - 8:])
    return jnp.concatenate([r[:T - 8], bot], axis=0)


def _rglru_gates(u, head8, grow, row, T, cw_ref, cb_ref, wa_ref, ba_ref, wx_ref, bx_ref, lam_ref):
    us = [_shift_down(u, k, head8, row, T) for k in range(CONV_W)]
    acc = us[0] * cw_ref[0:1, :]
    for k in range(1, CONV_W):
        acc = acc + us[k] * cw_ref[k:k + 1, :]
    conv = cb_ref[...] + acc
    cbf = conv.astype(BF)
    r_ = _sigmoid(_dot(cbf, wa_ref[0], NN) + ba_ref[...])
    i_ = _sigmoid(_dot(cbf, wx_ref[0], NN) + bx_ref[...])
    sp = _softplus(-lam_ref[...])
    la = -LRU_C * r_ * sp
    a = jnp.exp(la)
    mult_raw = jnp.sqrt(-_expm1(2.0 * la))
    mult = jnp.where(grow == 0, 1.0, mult_raw)
    return us, conv, cbf, r_, i_, sp, a, mult_raw, mult


def _rglru_specs(T, nt, rev):
    tmap = (lambda n, t: (nt - 1 - t, n)) if rev else (lambda n, t: (t, n))
    hmap = ((lambda n, t: (jnp.maximum((nt - 1 - t) * (T // 8) - 1, 0), n)) if rev
            else (lambda n, t: (jnp.maximum(t * (T // 8) - 1, 0), n)))
    tile = pl.BlockSpec((T, RNN_BLOCK), tmap)
    halo = pl.BlockSpec((8, RNN_BLOCK), hmap)
    vec = pl.BlockSpec((1, RNN_BLOCK), lambda n, t: (0, n))
    cw = pl.BlockSpec((CONV_W, RNN_BLOCK), lambda n, t: (0, n))
    wblk = pl.BlockSpec((1, RNN_BLOCK, RNN_BLOCK), lambda n, t: (n, 0, 0))
    return tile, halo, vec, cw, wblk


def _rglru_fwd(xr, g, cw, cb, wa, ba, wx, bx, lam, T):
    S = xr.shape[0]
    nt = S // T

    def body(u_ref, uh_ref, g_ref, cw_ref, cb_ref, wa_ref, ba_ref, wx_ref, bx_ref, lam_ref, h_ref, y_ref, carry):
        t = pl.program_id(1)

        @pl.when(t == 0)
        def _():
            carry[...] = jnp.zeros_like(carry)

        row = lax.broadcasted_iota(jnp.int32, (T, RNN_BLOCK), 0)
        grow = row + t * T
        head8 = jnp.where(t > 0, uh_ref[...], 0.0)
        _, conv, _, _, i_, _, a, _, mult = _rglru_gates(u_ref[...], head8, grow, row, T, cw_ref, cb_ref, wa_ref, ba_ref,
                                                         wx_ref, bx_ref, lam_ref)
        b = mult * i_ * conv
        s = 1
        while s < T:
            keep = row >= s
            a_s = jnp.where(keep, pltpu.roll(a, s, 0), 1.0)
            b_s = jnp.where(keep, pltpu.roll(b, s, 0), 0.0)
            b = a * b_s + b
            a = a * a_s
            s *= 2
        h = b + a * carry[0:1, :]
        carry[...] = jnp.broadcast_to(h[T - 1:T, :], carry.shape)
        h_ref[...] = h
        gv = g_ref[...]
        y_ref[...] = (h * (gv * _sigmoid(gv))).astype(BF)

    tile, halo, vec, cwspec, wblk = _rglru_specs(T, nt, False)
    return pl.pallas_call(
        body, name="rglru_fwd", grid=(RNN_BLOCKS, nt),
        in_specs=[tile, halo, tile, cwspec, vec, wblk, vec, wblk, vec, vec],
        out_specs=[tile, tile],
        out_shape=[jax.ShapeDtypeStruct((S, D_RNN), F32), jax.ShapeDtypeStruct((S, D_RNN), BF)],
        scratch_shapes=[pltpu.VMEM((8, RNN_BLOCK), F32)],
        compiler_params=_params(("parallel", "arbitrary")),
    )(xr, xr, g, cw, cb, wa, ba, wx, bx, lam)


def _rglru_bwd(xr, g, h, dy, cw, cb, wa, ba, wx, bx, lam, T):
    S = xr.shape[0]
    nt = S // T

    def body(u_ref, uh_ref, g_ref, h_ref, hh_ref, dy_ref, cw_ref, cb_ref, wa_ref, ba_ref, wx_ref, bx_ref, lam_ref,
             du_ref, dg_ref, dwa_ref, dwx_ref, dvec_ref, c_dhh, c_a, c_dconv):
        t = pl.program_id(1)
        tt = nt - 1 - t

        @pl.when(t == 0)
        def _():
            c_dhh[...] = jnp.zeros_like(c_dhh)
            c_a[...] = jnp.zeros_like(c_a)
            c_dconv[...] = jnp.zeros_like(c_dconv)
            dwa_ref[...] = jnp.zeros_like(dwa_ref)
            dwx_ref[...] = jnp.zeros_like(dwx_ref)
            dvec_ref[...] = jnp.zeros_like(dvec_ref)

        row = lax.broadcasted_iota(jnp.int32, (T, RNN_BLOCK), 0)
        row8 = row[:8]
        grow = row + tt * T
        head8 = jnp.where(tt > 0, uh_ref[...], 0.0)
        us, conv, cbf, r_, i_, sp, a, mult_raw, mult = _rglru_gates(
            u_ref[...], head8, grow, row, T, cw_ref, cb_ref, wa_ref, ba_ref, wx_ref, bx_ref, lam_ref)
        hv = h_ref[...]
        hprev = _shift_down(hv, 1, jnp.where(tt > 0, hh_ref[...], 0.0), row, T)
        gv = g_ref[...]
        sg = _sigmoid(gv)
        dyv = dy_ref[...]
        dg_ref[...] = (dyv * hv * (sg * (1.0 + gv * (1.0 - sg)))).astype(BF)
        d = dyv * (gv * sg)
        A = _shift_up(a, 1, c_a[...], row, T)
        s = 1
        while s < T:
            keep = row < T - s
            A_s = jnp.where(keep, pltpu.roll(A, T - s, 0), 1.0)
            d_s = jnp.where(keep, pltpu.roll(d, T - s, 0), 0.0)
            d = A * d_s + d
            A = A * A_s
            s *= 2
        dhh = d + A * c_dhh[0:1, :]
        da = dhh * hprev
        dconv = dhh * mult * i_
        di = dhh * mult * conv
        dmult = dhh * i_ * conv
        dla = da * a - jnp.where(grow == 0, 0.0, dmult * (a * a) / mult_raw)
        dr = dla * (-LRU_C * sp)
        dsp = jnp.sum(dla * (-LRU_C * r_), axis=0, keepdims=True)
        dza = dr * r_ * (1.0 - r_)
        dzx = di * i_ * (1.0 - i_)
        dza_b, dzx_b = dza.astype(BF), dzx.astype(BF)
        dconv = dconv + _dot(dza_b, wa_ref[0], NT) + _dot(dzx_b, wx_ref[0], NT)
        dwa_ref[0] += _dot(cbf, dza_b, TN)
        dwx_ref[0] += _dot(cbf, dzx_b, TN)
        lam = lam_ref[...]
        rows = [jnp.sum(dconv * us[k], axis=0, keepdims=True) for k in range(CONV_W)]
        rows += [jnp.sum(dconv, axis=0, keepdims=True), jnp.sum(dza, axis=0, keepdims=True),
                 jnp.sum(dzx, axis=0, keepdims=True), dsp * (-_sigmoid(-lam))]
        upd = jnp.zeros((8, RNN_BLOCK), F32)
        for j, rv in enumerate(rows):
            upd = upd + jnp.where(row8 == j, rv, 0.0)
        dvec_ref[...] += upd
        tail8 = c_dconv[...]
        du = dconv * cw_ref[0:1, :]
        for k in range(1, CONV_W):
            du = du + _shift_up(dconv, k, tail8, row, T) * cw_ref[k:k + 1, :]
        du_ref[...] = du.astype(BF)
        c_dhh[...] = jnp.broadcast_to(dhh[0:1, :], c_dhh.shape)
        c_a[...] = jnp.broadcast_to(a[0:1, :], c_a.shape)
        c_dconv[...] = dconv[:8]

    tile, halo, vec, cwspec, wblk = _rglru_specs(T, nt, True)
    acc8 = pl.BlockSpec((8, RNN_BLOCK), lambda n, t: (0, n))
    return pl.pallas_call(
        body, name="rglru_bwd", grid=(RNN_BLOCKS, nt),
        in_specs=[tile, halo, tile, tile, halo, tile, cwspec, vec, wblk, vec, wblk, vec, vec],
        out_specs=[tile, tile, wblk, wblk, acc8],
        out_shape=[jax.ShapeDtypeStruct((S, D_RNN), BF), jax.ShapeDtypeStruct((S, D_RNN), BF),
                   jax.ShapeDtypeStruct((RNN_BLOCKS, RNN_BLOCK, RNN_BLOCK), F32),
                   jax.ShapeDtypeStruct((RNN_BLOCKS, RNN_BLOCK, RNN_BLOCK), F32),
                   jax.ShapeDtypeStruct((8, D_RNN), F32)],
        scratch_shapes=[pltpu.VMEM((8, RNN_BLOCK), F32)] * 3,
        compiler_params=_params(("parallel", "arbitrary")),
    )(xr, xr, g, h, h, dy, cw, cb, wa, ba, wx, bx, lam)


def _rel_bucket_map():
    qi = np.arange(WINDOW)[:, None]
    kj = np.arange(2 * WINDOW)[None, :]
    dist = jnp.asarray(qi + WINDOW - kj, jnp.int32)
    n = jnp.maximum(dist, 0)
    max_exact = REL_BUCKETS // 2
    ratio = jnp.log(jnp.maximum(n, 1).astype(F32) / max_exact) / math.log(REL_MAX_DIST / max_exact)
    large = jnp.minimum(max_exact + (ratio * (REL_BUCKETS - max_exact)).astype(jnp.int32), REL_BUCKETS - 1)
    return jnp.where(n < max_exact, n, large).astype(jnp.int32)


def _swa_common(n, kv_ref, bucket_ref, relb_ref, bias_scr):
    @pl.when(n == 0)
    def _():
        bk = bucket_ref[...]
        for h in range(SWA_HEADS):
            acc = jnp.zeros((WINDOW, 2 * WINDOW), F32)
            for b in range(REL_BUCKETS):
                acc = acc + jnp.where(bk == b, relb_ref[b, h], 0.0)
            bias_scr[h] = acc

    prev0 = pl.multiple_of(jnp.maximum(n - 1, 0) * WINDOW, WINDOW)
    cur0 = pl.multiple_of(n * WINDOW, WINDOW)
    kk = jnp.concatenate([kv_ref[pl.ds(prev0, WINDOW), :], kv_ref[pl.ds(cur0, WINDOW), :]], axis=0).astype(F32)
    rowi = lax.broadcasted_iota(jnp.int32, (WINDOW, 2 * WINDOW), 0)
    col = lax.broadcasted_iota(jnp.int32, (WINDOW, 2 * WINDOW), 1)
    no_prev = jnp.where(n > 0, 0, 4 * WINDOW)
    valid = jnp.logical_or(jnp.logical_and(col < WINDOW, col > rowi + no_prev),
                           jnp.logical_and(col >= WINDOW, (col - WINDOW) <= rowi))
    return kk, valid, prev0, cur0


def _half_pair(part, kvh):
    lo = lax.broadcasted_iota(jnp.int32, part.shape, 1) < SWA_HD
    if kvh == 0:
        pa = jnp.where(lo, part, 0.0)
        pb = pltpu.roll(pa, SWA_HD, 1)
    else:
        pb = jnp.where(lo, 0.0, part)
        pa = pltpu.roll(pb, SWA_HD, 1)
    return pa.astype(BF), pb.astype(BF)


def _swa_probs(q2, kx, bias, sink, valid):
    lg = _dot(q2, kx, NT) * (SWA_HD ** -0.5) + bias
    lg = jnp.where(valid, lg, NEG_INF)
    m = jnp.maximum(jnp.max(lg, axis=-1, keepdims=True), sink)
    e = jnp.exp(lg - m)
    es = jnp.exp(sink - m)
    den = jnp.sum(e, axis=-1, keepdims=True) + es
    return e / den, es / den


def _swa_fwd(q, kv, g, bucket, rel_bias, sinks):
    S = q.shape[0]
    nb = S // WINDOW

    def body(q_ref, kv_ref, g_ref, bucket_ref, relb_ref, sink_ref, o_ref, y_ref, bias_scr):
        n = pl.program_id(0)
        kk, valid, _, _ = _swa_common(n, kv_ref, bucket_ref, relb_ref, bias_scr)
        for kvh in range(SWA_KV_HEADS):
            ka, kb = _half_pair(kk[:, :128], kvh)
            va, vb = _half_pair(kk[:, 128:], kvh)
            for p in range(4):
                c0 = kvh * 512 + p * 128
                h0 = kvh * 8 + 2 * p
                q2 = q_ref[:, c0:c0 + 128]
                p0, _ = _swa_probs(q2, ka, bias_scr[h0], sink_ref[0, h0], valid)
                p1, _ = _swa_probs(q2, kb, bias_scr[h0 + 1], sink_ref[0, h0 + 1], valid)
                o2 = _dot(p0.astype(BF), va, NN) + _dot(p1.astype(BF), vb, NN)
                o_ref[:, c0:c0 + 128] = o2
                gv = g_ref[:, c0:c0 + 128]
                y_ref[:, c0:c0 + 128] = (o2 * (gv * _sigmoid(gv))).astype(BF)

    blk = pl.BlockSpec((WINDOW, 1024), lambda n: (n, 0))
    smem = pl.BlockSpec(memory_space=pltpu.SMEM)
    return pl.pallas_call(
        body, name="swa_fwd", grid=(nb,),
        in_specs=[blk, pl.BlockSpec((S, 256), lambda n: (0, 0)), blk, pl.BlockSpec((WINDOW, 2 * WINDOW), lambda n: (0, 0)), smem, smem],
        out_specs=[blk, blk],
        out_shape=[jax.ShapeDtypeStruct((S, 1024), F32), jax.ShapeDtypeStruct((S, 1024), BF)],
        scratch_shapes=[pltpu.VMEM((SWA_HEADS, WINDOW, 2 * WINDOW), F32)],
        compiler_params=_params(("arbitrary",)),
    )(q, kv, g, bucket, rel_bias, sinks)


def _swa_bwd(q, kv, g, o, dy, bucket, rel_bias, sinks):
    S = q.shape[0]
    nb = S // WINDOW

    def body(q_ref, kv_ref, g_ref, o_ref, dy_ref, bucket_ref, relb_ref, sink_ref,
             dq_ref, dg_ref, dkv_ref, dsink_ref, drel_ref, bias_scr, dbias_scr, dsink_scr):
        n = pl.program_id(0)

        @pl.when(n == 0)
        def _():
            dbias_scr[...] = jnp.zeros_like(dbias_scr)
            dsink_scr[...] = jnp.zeros_like(dsink_scr)
            dkv_ref[...] = jnp.zeros_like(dkv_ref)

        kk, valid, prev0, cur0 = _swa_common(n, kv_ref, bucket_ref, relb_ref, bias_scr)
        lane = lax.broadcasted_iota(jnp.int32, (WINDOW, 128), 1)
        lo256 = lax.broadcasted_iota(jnp.int32, (2 * WINDOW, 128), 1) < SWA_HD
        dks, dvs = [], []
        for kvh in range(SWA_KV_HEADS):
            ka, kb = _half_pair(kk[:, :128], kvh)
            va, vb = _half_pair(kk[:, 128:], kvh)
            dka = jnp.zeros((2 * WINDOW, 128), F32)
            dkb, dva, dvb = dka, dka, dka
            for p in range(4):
                c0 = kvh * 512 + p * 128
                h0 = kvh * 8 + 2 * p
                q2 = q_ref[:, c0:c0 + 128]
                gv = g_ref[:, c0:c0 + 128]
                sg = _sigmoid(gv)
                dyv = dy_ref[:, c0:c0 + 128]
                dg_ref[:, c0:c0 + 128] = (dyv * o_ref[:, c0:c0 + 128] * (sg * (1.0 + gv * (1.0 - sg)))).astype(BF)
                do2 = (dyv * (gv * sg)).astype(BF)
                dq2 = jnp.zeros((WINDOW, 128), F32)
                for half, (kx, vx) in enumerate(((ka, va), (kb, vb))):
                    hh = h0 + half
                    pr, ps = _swa_probs(q2, kx, bias_scr[hh], sink_ref[0, hh], valid)
                    dp = _dot(do2, vx, NT)
                    delta = jnp.sum(pr * dp, axis=-1, keepdims=True)
                    ds = pr * (dp - delta)
                    dbias_scr[hh] += ds
                    dsink_scr[...] += jnp.where(lane == hh, ps * delta, 0.0)
                    dsb = (ds * (SWA_HD ** -0.5)).astype(BF)
                    prb = pr.astype(BF)
                    dq2 = dq2 + _dot(dsb, kx, NN)
                    if half == 0:
                        dka = dka + _dot(dsb, q2, TN)
                        dva = dva + _dot(prb, do2, TN)
                    else:
                        dkb = dkb + _dot(dsb, q2, TN)
                        dvb = dvb + _dot(prb, do2, TN)
                dq_ref[:, c0:c0 + 128] = dq2.astype(BF)
            dks.append(jnp.where(lo256, dka, 0.0) + pltpu.roll(jnp.where(lo256, 0.0, dkb), SWA_HD, 1))
            dvs.append(jnp.where(lo256, dva, 0.0) + pltpu.roll(jnp.where(lo256, 0.0, dvb), SWA_HD, 1))
        dk = dks[0] + pltpu.roll(dks[1], SWA_HD, 1)
        dv = dvs[0] + pltpu.roll(dvs[1], SWA_HD, 1)
        dkv_ref[pl.ds(prev0, WINDOW), 0:128] += dk[:WINDOW]
        dkv_ref[pl.ds(prev0, WINDOW), 128:256] += dv[:WINDOW]
        dkv_ref[pl.ds(cur0, WINDOW), 0:128] += dk[WINDOW:]
        dkv_ref[pl.ds(cur0, WINDOW), 128:256] += dv[WINDOW:]

        @pl.when(n == nb - 1)
        def _():
            dsink_ref[...] = -jnp.sum(dsink_scr[...], axis=0, keepdims=True)
            bk = bucket_ref[...]
            r32 = lax.broadcasted_iota(jnp.int32, (REL_BUCKETS, 128), 0)
            l32 = lax.broadcasted_iota(jnp.int32, (REL_BUCKETS, 128), 1)
            acc = jnp.zeros((REL_BUCKETS, 128), F32)
            for b in range(REL_BUCKETS):
                mb = bk == b
                for h in range(SWA_HEADS):
                    t1 = jnp.sum(jnp.where(mb, dbias_scr[h], 0.0), axis=1, keepdims=True)
                    val = jnp.sum(t1, axis=0, keepdims=True)
                    acc = acc + jnp.where(jnp.logical_and(r32 == b, l32 == h), val, 0.0)
            drel_ref[...] = acc

    blk = pl.BlockSpec((WINDOW, 1024), lambda n: (n, 0))
    smem = pl.BlockSpec(memory_space=pltpu.SMEM)
    whole = lambda shape: pl.BlockSpec(shape, lambda n: (0, 0))
    return pl.pallas_call(
        body, name="swa_bwd", grid=(nb,),
        in_specs=[blk, whole((S, 256)), blk, blk, blk, whole((WINDOW, 2 * WINDOW)), smem, smem],
        out_specs=[blk, blk, whole((S, 256)), whole((1, 128)), whole((REL_BUCKETS, 128))],
        out_shape=[jax.ShapeDtypeStruct((S, 1024), BF), jax.ShapeDtypeStruct((S, 1024), BF),
                   jax.ShapeDtypeStruct((S, 256), F32), jax.ShapeDtypeStruct((1, 128), F32),
                   jax.ShapeDtypeStruct((REL_BUCKETS, 128), F32)],
        scratch_shapes=[pltpu.VMEM((SWA_HEADS, WINDOW, 2 * WINDOW), F32), pltpu.VMEM((SWA_HEADS, WINDOW, 2 * WINDOW), F32),
                        pltpu.VMEM((WINDOW, 128), F32)],
        compiler_params=_params(("arbitrary",)),
    )(q, kv, g, o, dy, bucket, rel_bias, sinks)


def _mem_probs(qh, mk):
    lg = _dot(qh, mk, NT) * (MEM_HD ** -0.5)
    e = jnp.exp(lg - jnp.max(lg, axis=-1, keepdims=True))
    return e / jnp.sum(e, axis=-1, keepdims=True)


def _mem_fwd(q, mkv, g):
    S = q.shape[0]
    M = mkv.shape[0]
    tq = 256

    def body(q_ref, mkv_ref, g_ref, o_ref, y_ref):
        for h in range(MEM_HEADS):
            c0 = h * MEM_HD
            pr = _mem_probs(q_ref[:, c0:c0 + MEM_HD], mkv_ref[:, c0:c0 + MEM_HD])
            o = _dot(pr.astype(BF), mkv_ref[:, D_MEM + c0:D_MEM + c0 + MEM_HD], NN)
            o_ref[:, c0:c0 + MEM_HD] = o
            gv = g_ref[:, c0:c0 + MEM_HD]
            y_ref[:, c0:c0 + MEM_HD] = (o * (gv * _sigmoid(gv))).astype(BF)

    blk = pl.BlockSpec((tq, D_MEM), lambda i: (i, 0))
    return pl.pallas_call(
        body, name="mem_fwd", grid=(S // tq,),
        in_specs=[blk, pl.BlockSpec((M, 2 * D_MEM), lambda i: (0, 0)), blk], out_specs=[blk, blk],
        out_shape=[jax.ShapeDtypeStruct((S, D_MEM), F32), jax.ShapeDtypeStruct((S, D_MEM), BF)],
        compiler_params=_params(("parallel",)),
    )(q, mkv, g)


def _mem_bwd(q, mkv, g, o, dy):
    S = q.shape[0]
    M = mkv.shape[0]
    tq = 256

    def body(q_ref, mkv_ref, g_ref, o_ref, dy_ref, dq_ref, dg_ref, dmkv_ref):
        @pl.when(pl.program_id(0) == 0)
        def _():
            dmkv_ref[...] = jnp.zeros_like(dmkv_ref)

        for h in range(MEM_HEADS):
            c0 = h * MEM_HD
            qh = q_ref[:, c0:c0 + MEM_HD]
            mk = mkv_ref[:, c0:c0 + MEM_HD]
            mv = mkv_ref[:, D_MEM + c0:D_MEM + c0 + MEM_HD]
            gv = g_ref[:, c0:c0 + MEM_HD]
            sg = _sigmoid(gv)
            dyv = dy_ref[:, c0:c0 + MEM_HD]
            dg_ref[:, c0:c0 + MEM_HD] = (dyv * o_ref[:, c0:c0 + MEM_HD] * (sg * (1.0 + gv * (1.0 - sg)))).astype(BF)
            do = (dyv * (gv * sg)).astype(BF)
            pr = _mem_probs(qh, mk)
            dp = _dot(do, mv, NT)
            ds = pr * (dp - jnp.sum(pr * dp, axis=-1, keepdims=True))
            dsb = (ds * (MEM_HD ** -0.5)).astype(BF)
            dq_ref[:, c0:c0 + MEM_HD] = _dot(dsb, mk, NN).astype(BF)
            dmkv_ref[:, c0:c0 + MEM_HD] += _dot(dsb, qh, TN)
            dmkv_ref[:, D_MEM + c0:D_MEM + c0 + MEM_HD] += _dot(pr.astype(BF), do, TN)

    blk = pl.BlockSpec((tq, D_MEM), lambda i: (i, 0))
    whole = pl.BlockSpec((M, 2 * D_MEM), lambda i: (0, 0))
    return pl.pallas_call(
        body, name="mem_bwd", grid=(S // tq,),
        in_specs=[blk, whole, blk, blk, blk], out_specs=[blk, blk, whole],
        out_shape=[jax.ShapeDtypeStruct((S, D_MEM), BF), jax.ShapeDtypeStruct((S, D_MEM), BF),
                   jax.ShapeDtypeStruct((M, 2 * D_MEM), F32)],
        compiler_params=_params(("arbitrary",)),
    )(q, mkv, g, o, dy)


def _merge_specs(tm):
    ytile = pl.BlockSpec((tm, 1024), lambda i, j: (i, 0))
    wblk = pl.BlockSpec((None, 1024, 256), lambda i, j: (j, 0, 0))
    gls = [pl.BlockSpec((tm, 256), (lambda i, j, br=br: (i, br * 8 + j))) for br in range(3)]
    otile = pl.BlockSpec((tm, 256), lambda i, j: (i, j))
    return ytile, wblk, gls, otile


def _merge_fwd(ys, ws, gl, tm):
    S = gl.shape[0]

    def body(y0, y1, y2, w0, w1, w2, g0, g1, g2, o_ref):
        acc = None
        for y_ref, w_ref, g_ref in ((y0, w0, g0), (y1, w1, g1), (y2, w2, g2)):
            term = _sigmoid(g_ref[...]) * _dot(y_ref[...], w_ref[...], NN)
            acc = term if acc is None else acc + term
        o_ref[...] = acc.astype(BF)

    ytile, wblk, gls, otile = _merge_specs(tm)
    return pl.pallas_call(
        body, name="merge_fwd", grid=(S // tm, 8),
        in_specs=[ytile] * 3 + [wblk] * 3 + gls, out_specs=otile,
        out_shape=jax.ShapeDtypeStruct((S, D_MODEL), BF),
        compiler_params=_params(("parallel", "arbitrary")),
    )(*ys, *ws, gl, gl, gl)


def _merge_bwd(dout, w_out, ys, ws, gl, tm):
    S = gl.shape[0]

    def body(do_ref, wo_ref, y0, y1, y2, w0, w1, w2, g0, g1, g2, dg0, dg1, dg2, dp0, dp1, dp2):
        dm = _dot(do_ref[...], wo_ref[...], NT)
        for y_ref, w_ref, g_ref, dg_ref, dp_ref in ((y0, w0, g0, dg0, dp0), (y1, w1, g1, dg1, dp1), (y2, w2, g2, dg2, dp2)):
            gate = _sigmoid(g_ref[...])
            pv = _dot(y_ref[...], w_ref[...], NN)
            dg_ref[...] = (dm * pv * gate * (1.0 - gate)).astype(BF)
            dp_ref[...] = (dm * gate).astype(BF)

    ytile, wblk, gls, otile = _merge_specs(tm)
    out = jax.ShapeDtypeStruct((S, D_MODEL), BF)
    return pl.pallas_call(
        body, name="merge_bwd", grid=(S // tm, 8),
        in_specs=[pl.BlockSpec((tm, D_MODEL), lambda i, j: (i, 0)), pl.BlockSpec((256, D_MODEL), lambda i, j: (j, 0))]
        + [ytile] * 3 + [wblk] * 3 + gls,
        out_specs=[otile] * 6, out_shape=[out] * 6,
        compiler_params=_params(("parallel", "arbitrary")),
    )(dout, w_out, *ys, *ws, gl, gl, gl)


def _out_loss(merged, w_out, x, target, post_g, tm):
    S = x.shape[0]

    def body(m_ref, w_ref, x_ref, t_ref, g_ref, dout_ref, dy_ref, loss_ref, dpost_ref):
        @pl.when(pl.program_id(0) == 0)
        def _():
            loss_ref[...] = jnp.zeros_like(loss_ref)
            dpost_ref[...] = jnp.zeros_like(dpost_ref)

        out = _dot(m_ref[...], w_ref[...], NN)
        r = lax.rsqrt(jnp.mean(out * out, axis=-1, keepdims=True) + EPS)
        nrm = out * r
        gv = g_ref[...]
        err = (x_ref[...] + nrm * gv) - t_ref[...]
        sq = jnp.sum(jnp.sum(err * err, axis=1, keepdims=True), axis=0, keepdims=True)
        loss_ref[...] += sq * (0.5 / D_MODEL)
        dy = err * (1.0 / D_MODEL)
        dy_ref[...] = dy
        dpost_ref[...] += jnp.sum(dy * nrm, axis=0, keepdims=True)
        dn = dy * gv
        dout_ref[...] = (r * (dn - nrm * jnp.mean(dn * nrm, axis=-1, keepdims=True))).astype(BF)

    row = pl.BlockSpec((tm, D_MODEL), lambda i: (i, 0))
    return pl.pallas_call(
        body, name="out_loss", grid=(S // tm,),
        in_specs=[row, pl.BlockSpec((D_MODEL, D_MODEL), lambda i: (0, 0)), row, row, pl.BlockSpec((1, D_MODEL), lambda i: (0, 0))],
        out_specs=[row, row, pl.BlockSpec((8, 128), lambda i: (0, 0)), pl.BlockSpec((1, D_MODEL), lambda i: (0, 0))],
        out_shape=[jax.ShapeDtypeStruct((S, D_MODEL), BF), jax.ShapeDtypeStruct((S, D_MODEL), F32),
                   jax.ShapeDtypeStruct((8, 128), F32), jax.ShapeDtypeStruct((1, D_MODEL), F32)],
        compiler_params=_params(("arbitrary",)),
    )(merged, w_out, x, target, post_g)


def _dh_dx(dproj, w_in, x, dy, pre_g, tm, tk):
    S = x.shape[0]
    nk = D_IN // tk

    def body(dp_ref, w_ref, x_ref, dy_ref, g_ref, dx_ref, dpre_ref, acc_ref):
        i, k = pl.program_id(0), pl.program_id(1)

        @pl.when(jnp.logical_and(i == 0, k == 0))
        def _():
            dpre_ref[...] = jnp.zeros_like(dpre_ref)

        @pl.when(k == 0)
        def _():
            acc_ref[...] = jnp.zeros_like(acc_ref)

        acc_ref[...] += _dot(dp_ref[...], w_ref[...], NT)

        @pl.when(k == nk - 1)
        def _():
            dh = acc_ref[...]
            xv = x_ref[...]
            r = lax.rsqrt(jnp.mean(xv * xv, axis=-1, keepdims=True) + EPS)
            nrm = xv * r
            dpre_ref[...] += jnp.sum(dh * nrm, axis=0, keepdims=True)
            dn = dh * g_ref[...]
            dx_ref[...] = r * (dn - nrm * jnp.mean(dn * nrm, axis=-1, keepdims=True)) + dy_ref[...]

    row = pl.BlockSpec((tm, D_MODEL), lambda i, k: (i, 0))
    vec = pl.BlockSpec((1, D_MODEL), lambda i, k: (0, 0))
    return pl.pallas_call(
        body, name="dh_dx", grid=(S // tm, nk),
        in_specs=[pl.BlockSpec((tm, tk), lambda i, k: (i, k)), pl.BlockSpec((D_MODEL, tk), lambda i, k: (0, k)), row, row, vec],
        out_specs=[row, vec],
        out_shape=[jax.ShapeDtypeStruct((S, D_MODEL), F32), jax.ShapeDtypeStruct((1, D_MODEL), F32)],
        scratch_shapes=[pltpu.VMEM((tm, D_MODEL), F32)],
        compiler_params=_params(("arbitrary", "arbitrary"), 56),
    )(dproj, w_in, x, dy, pre_g)


def _sum_parts(parts, name):
    P, R, C = parts.shape
    tr = max(t for t in range(8, 513, 8) if R % t == 0)

    def body(p_ref, o_ref):
        acc = p_ref[0]
        for j in range(1, P):
            acc = acc + p_ref[j]
        o_ref[...] = acc

    return pl.pallas_call(
        body, name=name, grid=(R // tr,),
        in_specs=[pl.BlockSpec((P, tr, C), lambda i: (0, i, 0))], out_specs=pl.BlockSpec((tr, C), lambda i: (i, 0)),
        out_shape=jax.ShapeDtypeStruct((R, C), F32), compiler_params=_params(("parallel",)),
    )(parts)


def _adamw(parts, w, m, v, name):
    groups = list(parts) if isinstance(parts, (list, tuple)) else [parts]
    P, rows, C = groups[0].shape
    R = rows * len(groups)
    tr = 128 if rows % 128 == 0 else rows
    per = rows // tr
    c1 = 1.0 - ADAM_B1 ** ADAM_STEP
    c2 = 1.0 - ADAM_B2 ** ADAM_STEP

    def body(*refs):
        p_refs = refs[:len(groups)]
        w_ref, m_ref, v_ref, g_ref, d_ref, nm_ref, nv_ref = refs[len(groups):]
        g = None
        for q, p_ref in enumerate(p_refs):
            gq = p_ref[0].astype(F32)
            for j in range(1, P):
                gq = gq + p_ref[j].astype(F32)
            g = gq if g is None else jnp.where(pl.program_id(0) // per == q, gq, g)
        nm = ADAM_B1 * m_ref[...] + (1.0 - ADAM_B1) * g
        nv = ADAM_B2 * v_ref[...] + (1.0 - ADAM_B2) * (g * g)
        g_ref[...] = g
        nm_ref[...] = nm
        nv_ref[...] = nv
        d_ref[...] = -ADAM_LR * ((nm / c1) / (jnp.sqrt(nv / c2) + ADAM_EPS) + ADAM_WD * w_ref[...])

    tile = pl.BlockSpec((tr, C), lambda i: (i, 0))
    out = jax.ShapeDtypeStruct((R, C), F32)
    return pl.pallas_call(
        body, name=name, grid=(R // tr,),
        in_specs=[pl.BlockSpec((P, tr, C), (lambda i, q=q: (0, jnp.clip(i - q * per, 0, per - 1), 0))) for q in range(len(groups))]
        + [tile, tile, tile], out_specs=[tile] * 4, out_shape=[out] * 4,
        compiler_params=_params(("parallel",)),
    )(*groups, w, m, v)


def _forward_a(x, mem, pre_g, mem_g, w_in, conv_w, conv_b, w_a, b_a, w_x, b_x, lam, sinks, rel_bias):
    S = x.shape[0]
    st = dict(T=min(512, S // 2), tm=min(512, S), bucket=_rel_bucket_map())
    st["h"] = _rms_fwd(x, pre_g, "pre_norm")
    st["memn"] = _rms_fwd(mem, mem_g, "mem_norm")
    seg = {}
    for name, c0, width, dt in SEGMENTS:
        seg[name] = _matmul(st["h"], w_in, "nn", S, width, D_MODEL, S, SEG_TILE, D_MODEL, dt, "proj_" + name, b_noff=c0 // SEG_TILE)
    st["seg"] = seg
    st["h_rg"], st["y_rg"] = _rglru_fwd(seg["xr"], seg["g_rg"], conv_w, conv_b, w_a, b_a, w_x, b_x, lam, st["T"])
    st["o_swa"], st["y_swa"] = _swa_fwd(seg["q_s"], seg["kv"], seg["g_swa"], st["bucket"], rel_bias, sinks)
    return st


def _forward_b(st, x, target, post_g, w_memkv, wbr, w_out):
    S = x.shape[0]
    M = st["memn"].shape[0]
    seg = st["seg"]
    st["mkv"] = _matmul(st["memn"], w_memkv, "nn", M, 2 * D_MEM, D_MODEL, M, 512, D_MODEL, BF, "mem_kv")
    st["o_mem"], st["y_mem"] = _mem_fwd(seg["q_m"], st["mkv"], seg["g_mem"])
    st["ys"] = (st["y_rg"], st["y_swa"], st["y_mem"])
    st["merged"] = _merge_fwd(st["ys"], wbr, seg["gl"], st["tm"])
    st["dout"], st["dy"], st["loss"], st["dpost"] = _out_loss(st["merged"], w_out, x, target, post_g, min(256, S))
    return st


def _backward_a(st, mem, w_memkv, wbr, w_out, conv_w, conv_b, w_a, b_a, w_x, b_x, lam):
    S = st["h"].shape[0]
    M = mem.shape[0]
    seg, ys, tm = st["seg"], st["ys"], st["tm"]
    st["dw_out"] = _matmul(st["merged"], st["dout"], "tn", D_MODEL, D_MODEL, S, 256, D_MODEL, S, BF, "dw_out", out_blocked="row")
    dgl0, dgl1, dgl2, dp0, dp1, dp2 = _merge_bwd(st["dout"], w_out, ys, wbr, seg["gl"], tm)
    st["dgl"] = (dgl0, dgl1, dgl2)
    dys, dwbr = [], []
    for i, dp in enumerate((dp0, dp1, dp2)):
        dys.append(_matmul(dp, wbr[i], "nt", S, 1024, D_MODEL, tm, 1024, 256, F32, "dy_br%d" % i, b_blocked=True))
        dwbr.append(_matmul(ys[i], dp, "tn", 1024, D_MODEL, S, 1024, 256, S, BF, "dw_br%d" % i, out_blocked="col"))
    st["dys"], st["dwbr"] = dys, dwbr
    st["dq_m"], st["dg_mem"], dmkv = _mem_bwd(seg["q_m"], st["mkv"], seg["g_mem"], st["o_mem"], dys[2])
    dmkv_b = dmkv.astype(BF)
    st["dw_memkv"] = _matmul(st["memn"], dmkv_b, "tn", D_MODEL, 2 * D_MEM, M, 256, 2 * D_MEM, M, BF, "dw_memkv", out_blocked="row")
    dmemn = _matmul(dmkv_b, w_memkv, "nt", M, D_MODEL, 2 * D_MEM, M, 512, 2 * D_MEM, F32, "dmemn")
    st["dmem_g"] = _rms_gain_grad(dmemn, mem, "dmem_gain")
    st["dxr"], st["dg_rg"], st["dw_a"], st["dw_x"], st["dvec"] = _rglru_bwd(
        seg["xr"], seg["g_rg"], st["h_rg"], dys[0], conv_w, conv_b, w_a, b_a, w_x, b_x, lam, st["T"])
    return st


def _backward_b(st, rel_bias, sinks):
    seg = st["seg"]
    dq_s, dg_swa, dkv, st["dsinks"], st["drel"] = _swa_bwd(seg["q_s"], seg["kv"], seg["g_swa"], st["o_swa"], st["dys"][1],
                                                           st["bucket"], rel_bias, sinks)
    st["dproj"] = jnp.concatenate([st["dxr"], st["dg_rg"], dq_s, dkv.astype(BF), dg_swa, st["dq_m"], st["dg_mem"], *st["dgl"]], axis=1)
    return st


def _dw_in_half(st, half, dep=None):
    S = st["h"].shape[0]
    return _matmul(st["h"], st["dproj"], "tn", D_MODEL // 2, D_IN, S, 512, 1792, S, BF, "dw_in%d" % half, a_moff=2 * half, dep=dep)


def _owner_blocks(a):
    return jnp.swapaxes(a.reshape((4, 2) + a.shape[1:]), 0, 1)


def _local_step(x, mem, target, pre_g, post_g, mem_g, w_in, conv_w, conv_b, w_a, b_a, w_x, b_x, lam, sinks, rel_bias,
                w_memkv, wbr, w_out):
    st = _forward_a(x, mem, pre_g, mem_g, w_in, conv_w, conv_b, w_a, b_a, w_x, b_x, lam, sinks, rel_bias)
    st = _forward_b(st, x, target, post_g, w_memkv, wbr, w_out)
    st = _backward_a(st, mem, w_memkv, wbr, w_out, conv_w, conv_b, w_a, b_a, w_x, b_x, lam)
    st = _backward_b(st, rel_bias, sinks)
    st["dw_in"] = jnp.concatenate([_dw_in_half(st, 0), _dw_in_half(st, 1)], axis=0)
    st["grad_x"], st["dpre"] = _dh_dx(st["dproj"], w_in, x, st["dy"], pre_g, st["tm"], 896)
    return st


def _pad_rows(a, rows):
    a = a.reshape(-1, 128) if a.shape[-1] % 128 == 0 else jnp.pad(a, ((0, 0), (0, 128 - a.shape[-1])))
    return jnp.pad(a, ((0, rows - a.shape[0]), (0, 0))) if a.shape[0] < rows else a


def kernel(x, mem, pre_norm_g, post_norm_g, mem_norm_g, w_in, conv_w, conv_b, w_rg_a, b_rg_a, w_rg_x, b_rg_x, lru_lambda, swa_sinks, rel_bias, w_mem_kv, w_br_rg, w_br_swa, w_br_mem, w_out, loss_target, m_pre_norm_g, m_post_norm_g, m_mem_norm_g, m_w_in, m_conv_w, m_conv_b, m_w_rg_a, m_b_rg_a, m_w_rg_x, m_b_rg_x, m_lru_lambda, m_swa_sinks, m_rel_bias, m_w_mem_kv, m_w_br_rg, m_w_br_swa, m_w_br_mem, m_w_out, v_pre_norm_g, v_post_norm_g, v_mem_norm_g, v_w_in, v_conv_w, v_conv_b, v_w_rg_a, v_b_rg_a, v_w_rg_x, v_b_rg_x, v_lru_lambda, v_swa_sinks, v_rel_bias, v_w_mem_kv, v_w_br_rg, v_w_br_swa, v_w_br_mem, v_w_out):
    cx, cy, cc = lax.axis_index("x"), lax.axis_index("y"), lax.axis_index("c")
    me = 4 * cx + 2 * cy + cc
    chip = 2 * cx + cy
    core = jnp.reshape(cc, (1,)).astype(jnp.int32)
    x0, mem0 = x[0], mem[0]
    w_a_b, w_x_b = w_rg_a[0].astype(BF), w_rg_x[0].astype(BF)

    def landing(own, slot, slots):
        return lax.dynamic_update_slice(lax.empty((slots,) + own.shape, own.dtype), own[None], (slot,) + (0,) * own.ndim)

    def scatter_start(parts, tag):
        got = _swap_with_sibling(parts, "scatter_%s_swap" % tag)
        sums = [_pair_sum(p, g, core, "scatter_%s_sum%d" % (tag, i)) for i, (p, g) in enumerate(zip(parts, got))]
        lands = [landing(lax.dynamic_index_in_dim(s, chip, 0, keepdims=False), chip, 4) for s in sums]
        return _exchange_start(sums, lands, _plan_scatter(len(sums)), "scatter_%s_start" % tag)

    g_in, g_cw = _all_gather([w_in[0].astype(BF), conv_w[0]], "gather_w_in")
    w_in_f = jnp.transpose(g_in, (1, 0, 2)).reshape(D_MODEL, D_IN)
    conv_w_f = jnp.transpose(g_cw, (1, 0, 2)).reshape(CONV_W, D_RNN)

    after_first = jnp.minimum(jnp.abs(g_cw[0, 0, 0]), 0.0).astype(BF)
    rest = [w.astype(BF) + after_first for w in (w_mem_kv[0], w_br_rg[0], w_br_swa[0], w_br_mem[0], w_out[0])]
    plan_g = _plan_gather(len(rest))
    g_send, g_recv, g_src, g_land, g_token = _exchange_start(rest, [landing(w, me, N_DEV) for w in rest], plan_g, "gather_rest_start")
    st = _forward_a(x0, mem0, pre_norm_g + g_token[0:1, 0:1], mem_norm_g, w_in_f, conv_w_f, conv_b, w_a_b, b_rg_a, w_x_b, b_rg_x,
                    lru_lambda, swa_sinks, rel_bias)
    g_land = _exchange_wait(g_send, g_recv, g_src, g_land, plan_g, st["y_swa"], "gather_rest_wait")
    g_land = _forward_to_sibling(g_land, "gather_rest_forward")
    w_memkv_f = g_land[0].reshape(D_MODEL, 2 * D_MEM)
    wbr = (g_land[1], g_land[2], g_land[3])
    w_out_f = g_land[4].reshape(D_MODEL, D_MODEL)

    st = _forward_b(st, x0, loss_target[0], post_norm_g, w_memkv_f, wbr, w_out_f)
    st = _backward_a(st, mem0, w_memkv_f, wbr, w_out_f, conv_w_f, conv_b, w_a_b, b_rg_a, w_x_b, b_rg_x, lru_lambda)
    parts_a = [st["dw_memkv"], st["dwbr"][0], st["dwbr"][1], st["dwbr"][2], st["dw_out"],
               _owner_blocks(st["dw_a"]), _owner_blocks(st["dw_x"])]
    plan_a = _plan_scatter(len(parts_a))
    a_send, a_recv, a_src, a_land, a_token = scatter_start(parts_a, "a")

    st = _backward_b(st, rel_bias, swa_sinks + a_token[0:1, 0:1])
    plan_b = _plan_scatter(1)
    halves, dep = [], None
    for half in range(2):
        dwh = _dw_in_half(st, half, dep)
        parts_b = [jnp.transpose(dwh.reshape(D_MODEL // 2, 4, 2, D_IN // N_DEV), (2, 1, 0, 3))]
        halves.append(scatter_start(parts_b, "b%d" % half))
        dep = halves[-1][4]
    grad_x, dpre = _dh_dx(st["dproj"], w_in_f, x0, st["dy"], pre_norm_g + dep[0:1, 0:1], st["tm"], 896)
    a_land = _exchange_wait(a_send, a_recv, a_src, a_land, plan_a, grad_x, "scatter_a_wait")
    g_wa_blk = _sum_parts(a_land[5], "sum_w_rg_a")
    g_wx_blk = _sum_parts(a_land[6], "sum_w_rg_x")
    big = [None]
    for j, (wt, mt, vt) in enumerate(((w_mem_kv, m_w_mem_kv, v_w_mem_kv), (w_br_rg, m_w_br_rg, v_w_br_rg),
                                      (w_br_swa, m_w_br_swa, v_w_br_swa), (w_br_mem, m_w_br_mem, v_w_br_mem), (w_out, m_w_out, v_w_out))):
        big.append([a[None] for a in _adamw(a_land[j], wt[0], mt[0], vt[0], "adamw_big%d" % (j + 1))])
    after, b_lands = big[5][1], []
    for half, (b_send, b_recv, b_src, b_land, _) in enumerate(halves):
        b_lands.append(_exchange_wait(b_send, b_recv, b_src, b_land, plan_b, after, "scatter_b%d_wait" % half)[0])
        after = b_lands[-1]
    big[0] = [a[None] for a in _adamw(b_lands, w_in[0], m_w_in[0], v_w_in[0], "adamw_big0")]
    links_free = jnp.minimum(jnp.abs(big[0][0][0, 0, 0]), 0.0)

    pack = jnp.concatenate([dpre.reshape(16, 128), st["dpost"].reshape(16, 128), st["dmem_g"].reshape(16, 128),
                            st["dvec"].reshape(64, 128), _pad_rows(st["dsinks"], 8), st["drel"], g_wa_blk, g_wx_blk], axis=0) + links_free
    gathered = _all_gather([pack], "gather_small")[0]
    gs = _sum_parts(gathered, "sum_small")
    g_pre, g_post, g_memg = gs[0:16].reshape(1, D_MODEL), gs[16:32].reshape(1, D_MODEL), gs[32:48].reshape(1, D_MODEL)
    gvec = gs[48:112].reshape(8, D_RNN)
    g_conv_w = lax.dynamic_slice(gvec[0:CONV_W], (0, me * RNN_BLOCK), (CONV_W, RNN_BLOCK))
    g_conv_b, g_b_a, g_b_x, g_lam = gvec[4:5], gvec[5:6], gvec[6:7], gvec[7:8]
    g_sinks = gs[112:113, :SWA_HEADS]
    g_rel = gs[120:152, :SWA_HEADS]
    g_w_a = gathered[:, 152:280]
    g_w_x = gathered[:, 280:408]

    def packed(ts):
        pre, post, memg, cb, ba, bx, lm, wa, wx, sk, rel, cw = ts
        return jnp.concatenate([pre.reshape(16, 128), post.reshape(16, 128), memg.reshape(16, 128), cb.reshape(8, 128),
                                ba.reshape(8, 128), bx.reshape(8, 128), lm.reshape(8, 128), wa.reshape(1024, 128),
                                wx.reshape(1024, 128), _pad_rows(sk.reshape(1, SWA_HEADS), 8), _pad_rows(rel, 32),
                                _pad_rows(cw.reshape(CONV_W, RNN_BLOCK), 8)], axis=0)

    def unpacked(a):
        return (a[0:16].reshape(1, D_MODEL), a[16:32].reshape(1, D_MODEL), a[32:48].reshape(1, D_MODEL), a[48:56].reshape(1, D_RNN),
                a[56:64].reshape(1, D_RNN), a[64:72].reshape(1, D_RNN), a[72:80].reshape(1, D_RNN),
                a[80:1104].reshape(1, RNN_BLOCKS, RNN_BLOCK, RNN_BLOCK), a[1104:2128].reshape(1, RNN_BLOCKS, RNN_BLOCK, RNN_BLOCK),
                a[2128:2129, :SWA_HEADS], a[2136:2168, :SWA_HEADS], a[2168:2172].reshape(1, CONV_W, RNN_BLOCK))

    g_small = (g_pre, g_post, g_memg, g_conv_b, g_b_a, g_b_x, g_lam, g_w_a, g_w_x, g_sinks, g_rel, g_conv_w)
    w_small = (pre_norm_g, post_norm_g, mem_norm_g, conv_b, b_rg_a, b_rg_x, lru_lambda, w_rg_a, w_rg_x, swa_sinks, rel_bias, conv_w)
    m_small = (m_pre_norm_g, m_post_norm_g, m_mem_norm_g, m_conv_b, m_b_rg_a, m_b_rg_x, m_lru_lambda, m_w_rg_a, m_w_rg_x, m_swa_sinks, m_rel_bias, m_conv_w)
    v_small = (v_pre_norm_g, v_post_norm_g, v_mem_norm_g, v_conv_b, v_b_rg_a, v_b_rg_x, v_lru_lambda, v_w_rg_a, v_w_rg_x, v_swa_sinks, v_rel_bias, v_conv_w)
    sm = [unpacked(a) for a in _adamw(packed(g_small)[None], packed(w_small), packed(m_small), packed(v_small), "adamw_small")]


    loss_total = lax.psum(st["loss"][0, 0], AXES)

    def leaves(k):
        s = sm[k]
        return [s[0], s[1], s[2], big[0][k], s[11], s[3], s[7], s[4], s[8], s[5], s[6], s[9], s[10],
                big[1][k], big[2][k], big[3][k], big[4][k], big[5][k]]

    return (loss_total, grad_x[None], *leaves(0), *leaves(1), *leaves(2), *leaves(3))
```

```python
import math

import jax
import jax.numpy as jnp
import numpy as np
from jax import lax
from jax.experimental import pallas as pl
from jax.experimental.pallas import tpu as pltpu

F32, BF = jnp.float32, jnp.bfloat16
MESH = pl.DeviceIdType.MESH
AXES = ("x", "y", "c")
N_DEV = 8

D_MODEL = 2048
D_RNN = 1024
RNN_BLOCKS = 8
RNN_BLOCK = 128
CONV_W = 4
LRU_C = 8.0
SWA_HEADS = 16
SWA_KV_HEADS = 2
SWA_HD = 64
WINDOW = 128
MEM_HEADS = 4
MEM_HD = 256
D_MEM = 1024
REL_BUCKETS = 32
REL_MAX_DIST = 128
EPS = 1e-6
NEG_INF = -1e30
D_IN = 12544
SEGMENTS = (("xr", 0, 1024, F32), ("g_rg", 1024, 1024, F32), ("q_s", 2048, 1024, BF), ("kv", 3072, 256, BF),
            ("g_swa", 3328, 1024, F32), ("q_m", 4352, 1024, BF), ("g_mem", 5376, 1024, F32), ("gl", 6400, 6144, F32))
SEG_TILE = 256

ADAM_LR, ADAM_B1, ADAM_B2, ADAM_EPS, ADAM_WD, ADAM_STEP = 0.001, 0.9, 0.999, 1e-08, 0.01, 10

NN = (((1,), (0,)), ((), ()))
NT = (((1,), (1,)), ((), ()))
TN = (((0,), (0,)), ((), ()))
MIB = 2 ** 20


def _dot(a, b, dn):
    return lax.dot_general(a, b, dn, preferred_element_type=F32)


def _params(sem, vmem_mib=48):
    return pltpu.CompilerParams(dimension_semantics=sem, vmem_limit_bytes=vmem_mib * MIB)


def _sigmoid(z):
    return 1.0 / (1.0 + jnp.exp(-z))


def _softplus(z):
    return jnp.maximum(z, 0.0) + jnp.log(1.0 + jnp.exp(-jnp.abs(z)))


def _expm1(z):
    p = z * (1.0 + z * (0.5 + z * (1.0 / 6 + z * (1.0 / 24 + z * (1.0 / 120 + z * (1.0 / 720 + z * (1.0 / 5040 + z / 40320)))))))
    return jnp.where(jnp.abs(z) < 0.3, p, jnp.exp(z) - 1.0)


def _flat(p):
    return 4 * p[0] + 2 * p[1] + p[2]


def _all_gather(arrs, name):
    n = len(arrs)

    def body(*refs):
        ins, outs = refs[:n], refs[n:2 * n]
        send_sems, recv_sems, local_sems = refs[2 * n:]
        x, y, c = lax.axis_index("x"), lax.axis_index("y"), lax.axis_index("c")
        me, sibling = (x, y, c), (x, y, 1 - c)
        chips = [(1 - x, y), (x, 1 - y), (1 - x, 1 - y)]

        def copy(a, k, block, to, src=None):
            dst = outs[a].at[_flat(block)]
            return pltpu.make_async_remote_copy(src_ref=dst if src is None else src, dst_ref=dst,
                                                send_sem=send_sems.at[a * 7 + k], recv_sem=recv_sems.at[a * 7 + k],
                                                device_id=to, device_id_type=MESH)

        mine = [pltpu.make_async_copy(ins[a], outs[a].at[_flat(me)], local_sems.at[a]) for a in range(n)]
        for cp in mine:
            cp.start()
        first = []
        for a in range(n):
            first += [copy(a, 1 + j, me, (*chip, c), src=ins[a]) for j, chip in enumerate(chips)]
            first.append(copy(a, 0, me, sibling, src=ins[a]))
        for cp in first:
            cp.start()
        passed = []
        for j, chip in enumerate(chips):
            for a in range(n):
                copy(a, 1 + j, (*chip, c), me).wait_recv()
                fw = copy(a, 4 + j, (*chip, c), sibling)
                fw.start()
                passed.append(fw)
        for a in range(n):
            copy(a, 0, sibling, me).wait_recv()
            for j, chip in enumerate(chips):
                copy(a, 4 + j, (*chip, 1 - c), me).wait_recv()
        for cp in first + passed:
            cp.wait_send()
        for cp in mine:
            cp.wait()

    any_spec = pl.BlockSpec(memory_space=pl.ANY)
    return pl.pallas_call(
        body, name=name,
        out_shape=[jax.ShapeDtypeStruct((N_DEV,) + a.shape, a.dtype) for a in arrs],
        in_specs=[any_spec] * n, out_specs=[any_spec] * n,
        scratch_shapes=[pltpu.SemaphoreType.DMA((7 * n,)), pltpu.SemaphoreType.DMA((7 * n,)), pltpu.SemaphoreType.DMA((n,))],
    )(*arrs)


def _chip_peers(x, y):
    return [(1 - x, y), (x, 1 - y), (1 - x, 1 - y)]


def _chip(p):
    return 2 * p[0] + p[1]


def _plan_gather(kinds):
    def plan(x, y, c):
        out = []
        for a, kind in enumerate(kinds):
            for peer in [(x, y, 1 - c)] + [(*ch, c) for ch in _chip_peers(x, y)]:
                out.append((a, None, (kind, _flat((x, y, c))), peer, (kind, _flat(peer))))
        return out
    return plan


def _slot(ref, where):
    kind, k = where
    if kind == "lead":
        return ref.at[k]
    return ref.at[:, pl.ds(pl.multiple_of(k * 256, 256), 256)]


def _plan_scatter(n):
    def plan(x, y, c):
        out = []
        for a in range(n):
            for ch in _chip_peers(x, y):
                out.append((a, _chip(ch), ("lead", _chip((x, y))), (*ch, c), ("lead", _chip(ch))))
        return out
    return plan


HBM_SPEC = pl.BlockSpec(memory_space=pltpu.HBM)
SEM_SPEC = pl.BlockSpec(memory_space=pltpu.SEMAPHORE)


def _in_hbm(a):
    return pltpu.with_memory_space_constraint(a, pltpu.HBM)


def _exchange_start(srcs, lands, plan, name):
    n = len(srcs)
    count = len(plan(0, 0, 0))

    def body(*refs):
        src_refs, land_refs = refs[:n], refs[n:2 * n]
        send_sems, recv_sems = refs[2 * n], refs[2 * n + 1]
        token = refs[-1]
        x, y, c = lax.axis_index("x"), lax.axis_index("y"), lax.axis_index("c")
        for k, (a, si, di, peer, _) in enumerate(plan(x, y, c)):
            src = src_refs[a] if si is None else src_refs[a].at[si]
            pltpu.make_async_remote_copy(src_ref=src, dst_ref=_slot(land_refs[a], di), send_sem=send_sems.at[k],
                                         recv_sem=recv_sems.at[k], device_id=peer, device_id_type=MESH).start()
        token[...] = jnp.zeros_like(token)

    out = pl.pallas_call(
        body, name=name,
        out_shape=(pltpu.SemaphoreType.DMA((count,)), pltpu.SemaphoreType.DMA((count,)),
                   *[pltpu.HBM(a.shape, a.dtype) for a in srcs], *[pltpu.HBM(a.shape, a.dtype) for a in lands],
                   jax.ShapeDtypeStruct((8, 128), F32)),
        in_specs=[HBM_SPEC] * (2 * n),
        out_specs=(SEM_SPEC, SEM_SPEC, *([HBM_SPEC] * (2 * n)), pl.BlockSpec(memory_space=pltpu.VMEM)),
        input_output_aliases={i: 2 + i for i in range(2 * n)},
        compiler_params=pltpu.CompilerParams(has_side_effects=pltpu.SideEffectType.DATAFLOW_SIDE_EFFECTING),
    )(*[_in_hbm(a) for a in srcs], *[_in_hbm(a) for a in lands])
    return out[0], out[1], list(out[2:2 + n]), list(out[2 + n:2 + 2 * n]), out[-1]


def _exchange_wait(send_sems, recv_sems, srcs, lands, plan, after, name):
    n = len(srcs)

    def body(*refs):
        src_refs, land_refs = refs[:n], refs[n:2 * n]
        send_sems, recv_sems = refs[2 * n], refs[2 * n + 1]
        x, y, c = lax.axis_index("x"), lax.axis_index("y"), lax.axis_index("c")
        for k, (a, si, _, peer, ri) in enumerate(plan(x, y, c)):
            src = src_refs[a] if si is None else src_refs[a].at[si]
            cp = pltpu.make_async_remote_copy(src_ref=src, dst_ref=_slot(land_refs[a], ri), send_sem=send_sems.at[k],
                                              recv_sem=recv_sems.at[k], device_id=peer, device_id_type=MESH)
            cp.wait_send()
            cp.wait_recv()

    out = pl.pallas_call(
        body, name=name,
        out_shape=(*[pltpu.HBM(a.shape, a.dtype) for a in srcs], *[pltpu.HBM(a.shape, a.dtype) for a in lands]),
        in_specs=[HBM_SPEC] * (2 * n) + [SEM_SPEC, SEM_SPEC, pl.BlockSpec(memory_space=pl.ANY)],
        out_specs=tuple([HBM_SPEC] * (2 * n)),
        input_output_aliases={i: i for i in range(2 * n)},
        compiler_params=pltpu.CompilerParams(has_side_effects=pltpu.SideEffectType.DATAFLOW_SIDE_EFFECTING),
    )(*srcs, *lands, send_sems, recv_sems, after)
    return list(out[n:2 * n])


def _forward_to_sibling(lands, kinds, name):
    n = len(lands)

    def body(*refs):
        in_refs, out_refs = refs[:n], refs[n:2 * n]
        send_sems, recv_sems = refs[2 * n:]
        x, y, c = lax.axis_index("x"), lax.axis_index("y"), lax.axis_index("c")
        sibling = (x, y, 1 - c)

        def copy(a, j, slot):
            return pltpu.make_async_remote_copy(src_ref=_slot(in_refs[a], (kinds[a], slot)), dst_ref=_slot(out_refs[a], (kinds[a], slot)),
                                                send_sem=send_sems.at[a * 3 + j], recv_sem=recv_sems.at[a * 3 + j],
                                                device_id=sibling, device_id_type=MESH)

        sends = [copy(a, j, _flat((*ch, c))) for a in range(n) for j, ch in enumerate(_chip_peers(x, y))]
        for cp in sends:
            cp.start()
        for a in range(n):
            for j, ch in enumerate(_chip_peers(x, y)):
                copy(a, j, _flat((*ch, 1 - c))).wait_recv()
        for cp in sends:
            cp.wait_send()

    any_spec = pl.BlockSpec(memory_space=pl.ANY)
    return pl.pallas_call(
        body, name=name, out_shape=[jax.ShapeDtypeStruct(a.shape, a.dtype) for a in lands],
        in_specs=[any_spec] * n, out_specs=[any_spec] * n, input_output_aliases={a: a for a in range(n)},
        scratch_shapes=[pltpu.SemaphoreType.DMA((3 * n,)), pltpu.SemaphoreType.DMA((3 * n,))],
    )(*lands)


def _swap_with_sibling(parts, name):
    n = len(parts)

    def body(*refs):
        in_refs, out_refs = refs[:n], refs[n:2 * n]
        send_sems, recv_sems = refs[2 * n:]
        x, y, c = lax.axis_index("x"), lax.axis_index("y"), lax.axis_index("c")
        sends = [pltpu.make_async_remote_copy(src_ref=in_refs[a].at[1 - c], dst_ref=out_refs[a], send_sem=send_sems.at[a],
                                              recv_sem=recv_sems.at[a], device_id=(x, y, 1 - c), device_id_type=MESH)
                 for a in range(n)]
        for cp in sends:
            cp.start()
        for cp in sends:
            cp.wait()

    any_spec = pl.BlockSpec(memory_space=pl.ANY)
    return pl.pallas_call(
        body, name=name, out_shape=[jax.ShapeDtypeStruct(a.shape[1:], a.dtype) for a in parts],
        in_specs=[any_spec] * n, out_specs=[any_spec] * n,
        scratch_shapes=[pltpu.SemaphoreType.DMA((n,)), pltpu.SemaphoreType.DMA((n,))],
    )(*parts)


def _pair_sum(parts, got, core, name):
    _, _, R, C = parts.shape
    tr = 256 if R % 256 == 0 else R

    def body(c_ref, p_ref, g_ref, o_ref):
        o_ref[...] = (p_ref[...].astype(F32) + g_ref[...].astype(F32)).astype(o_ref.dtype)

    return pl.pallas_call(
        body, name=name,
        grid_spec=pltpu.PrefetchScalarGridSpec(
            num_scalar_prefetch=1, grid=(4, R // tr),
            in_specs=[pl.BlockSpec((None, None, tr, C), lambda j, i, c_ref: (c_ref[0], j, i, 0)),
                      pl.BlockSpec((None, tr, C), lambda j, i, c_ref: (j, i, 0))],
            out_specs=pl.BlockSpec((None, tr, C), lambda j, i, c_ref: (j, i, 0))),
        out_shape=jax.ShapeDtypeStruct((4, R, C), parts.dtype),
        compiler_params=_params(("parallel", "parallel")),
    )(core, parts, got)


def _matmul(a, b, mode, M, N, K, tm, tn, tk, out_dtype, name, b_noff=0, a_moff=0, b_blocked=False, out_blocked=None, dep=None,
            vmem_mib=48):
    nm, nn, nk = M // tm, N // tn, K // tk
    if mode == "nn":
        a_spec = pl.BlockSpec((tm, tk), lambda j, i, k: (i, k))
        b_spec = pl.BlockSpec((tk, tn), lambda j, i, k: (k, j + b_noff))
        dn = NN
    elif mode == "nt":
        a_spec = pl.BlockSpec((tm, tk), lambda j, i, k: (i, k))
        if b_blocked:
            b_spec = pl.BlockSpec((None, tn, tk), lambda j, i, k: (k, j, 0))
        else:
            b_spec = pl.BlockSpec((tn, tk), lambda j, i, k: (j + b_noff, k))
        dn = NT
    else:
        a_spec = pl.BlockSpec((tk, tm), lambda j, i, k: (k, i + a_moff))
        b_spec = pl.BlockSpec((tk, tn), lambda j, i, k: (k, j + b_noff))
        dn = TN
    if out_blocked == "col":
        out_shape = jax.ShapeDtypeStruct((2, 4, M, tn), out_dtype)
        out_spec = pl.BlockSpec((None, None, tm, tn), lambda j, i, k: (j % 2, j // 2, i, 0))
    elif out_blocked == "row":
        out_shape = jax.ShapeDtypeStruct((2, 4, tm, N), out_dtype)
        out_spec = pl.BlockSpec((None, None, tm, tn), lambda j, i, k: (i % 2, i // 2, 0, j))
    elif out_blocked == "third":
        out_shape = jax.ShapeDtypeStruct((3, M, N // 3), out_dtype)
        out_spec = pl.BlockSpec((None, tm, tn), lambda j, i, k: (j // (nn // 3), i, j % (nn // 3)))
    else:
        out_shape = jax.ShapeDtypeStruct((M, N), out_dtype)
        out_spec = pl.BlockSpec((tm, tn), lambda j, i, k: (i, j))

    def body(a_ref, b_ref, *rest):
        o_ref, scratch = (rest[1], rest[2:]) if dep is not None else (rest[0], rest[1:])
        if nk == 1:
            o_ref[...] = _dot(a_ref[...], b_ref[...], dn).astype(out_dtype)
        else:
            acc_ref, = scratch
            k = pl.program_id(2)

            @pl.when(k == 0)
            def _():
                acc_ref[...] = jnp.zeros_like(acc_ref)

            acc_ref[...] += _dot(a_ref[...], b_ref[...], dn)

            @pl.when(k == nk - 1)
            def _():
                o_ref[...] = acc_ref[...].astype(out_dtype)

    return pl.pallas_call(
        body, name=name, grid=(nn, nm, nk),
        in_specs=[a_spec, b_spec] + ([] if dep is None else [pl.BlockSpec((8, 128), lambda j, i, k: (0, 0))]),
        out_specs=out_spec, out_shape=out_shape,
        scratch_shapes=[] if nk == 1 else [pltpu.VMEM((tm, tn), F32)],
        compiler_params=_params(("parallel", "parallel", "arbitrary"), vmem_mib),
    )(a, b, *([] if dep is None else [dep]))


def _rms_fwd(x, g, name):
    R, Dm = x.shape
    tr = min(R, 256)

    def body(x_ref, g_ref, h_ref):
        xv = x_ref[...]
        r = lax.rsqrt(jnp.mean(xv * xv, axis=-1, keepdims=True) + EPS)
        h_ref[...] = (xv * r * g_ref[...]).astype(BF)

    return pl.pallas_call(
        body, name=name, grid=(R // tr,),
        in_specs=[pl.BlockSpec((tr, Dm), lambda i: (i, 0)), pl.BlockSpec((1, Dm), lambda i: (0, 0))],
        out_specs=pl.BlockSpec((tr, Dm), lambda i: (i, 0)), out_shape=jax.ShapeDtypeStruct((R, Dm), BF),
        compiler_params=_params(("parallel",)),
    )(x, g)


def _rms_gain_grad(dn, x, name):
    R, Dm = x.shape

    def body(dn_ref, x_ref, o_ref):
        xv = x_ref[...]
        r = lax.rsqrt(jnp.mean(xv * xv, axis=-1, keepdims=True) + EPS)
        o_ref[...] = jnp.sum(dn_ref[...] * xv * r, axis=0, keepdims=True)

    return pl.pallas_call(
        body, name=name, out_shape=jax.ShapeDtypeStruct((1, Dm), F32),
        compiler_params=pltpu.CompilerParams(vmem_limit_bytes=32 * MIB),
    )(dn, x)


def _shift_down(v, k, head8, row, T):
    if k == 0:
        return v
    r = pltpu.roll(v, k, 0)
    hr = pltpu.roll(head8, k, 0)
    top = jnp.where(row[:8] < k, hr, r[:8])
    return jnp.concatenate([top, r[8:]], axis=0)


def _shift_up(v, k, tail8, row, T):
    if k == 0:
        return v
    r = pltpu.roll(v, T - k, 0)
    tr = pltpu.roll(tail8, 8 - k, 0)
    bot = jnp.where(row[:8] >= 8 - k, tr, r[T - 8:])
    return jnp.concatenate([r[:T - 8], bot], axis=0)


def _rglru_gates(u, head8, grow, row, T, cw_ref, cb_ref, wa_ref, ba_ref, wx_ref, bx_ref, lam_ref):
    us = [_shift_down(u, k, head8, row, T) for k in range(CONV_W)]
    acc = us[0] * cw_ref[0:1, :]
    for k in range(1, CONV_W):
        acc = acc + us[k] * cw_ref[k:k + 1, :]
    conv = cb_ref[...] + acc
    cbf = conv.astype(BF)
    r_ = _sigmoid(_dot(cbf, wa_ref[0], NN) + ba_ref[...])
    i_ = _sigmoid(_dot(cbf, wx_ref[0], NN) + bx_ref[...])
    sp = _softplus(-lam_ref[...])
    la = -LRU_C * r_ * sp
    a = jnp.exp(la)
    mult_raw = jnp.sqrt(-_expm1(2.0 * la))
    mult = jnp.where(grow == 0, 1.0, mult_raw)
    return us, conv, cbf, r_, i_, sp, a, mult_raw, mult


def _rglru_specs(T, nt, rev):
    tmap = (lambda n, t: (nt - 1 - t, n)) if rev else (lambda n, t: (t, n))
    hmap = ((lambda n, t: (jnp.maximum((nt - 1 - t) * (T // 8) - 1, 0), n)) if rev
            else (lambda n, t: (jnp.maximum(t * (T // 8) - 1, 0), n)))
    tile = pl.BlockSpec((T, RNN_BLOCK), tmap)
    halo = pl.BlockSpec((8, RNN_BLOCK), hmap)
    vec = pl.BlockSpec((1, RNN_BLOCK), lambda n, t: (0, n))
    cw = pl.BlockSpec((CONV_W, RNN_BLOCK), lambda n, t: (0, n))
    wblk = pl.BlockSpec((1, RNN_BLOCK, RNN_BLOCK), lambda n, t: (n, 0, 0))
    return tile, halo, vec, cw, wblk


def _rglru_fwd(xr, g, cw, cb, wa, ba, wx, bx, lam, T):
    S = xr.shape[0]
    nt = S // T

    def body(u_ref, uh_ref, g_ref, cw_ref, cb_ref, wa_ref, ba_ref, wx_ref, bx_ref, lam_ref, h_ref, y_ref, carry):
        t = pl.program_id(1)

        @pl.when(t == 0)
        def _():
            carry[...] = jnp.zeros_like(carry)

        row = lax.broadcasted_iota(jnp.int32, (T, RNN_BLOCK), 0)
        grow = row + t * T
        head8 = jnp.where(t > 0, uh_ref[...], 0.0)
        _, conv, _, _, i_, _, a, _, mult = _rglru_gates(u_ref[...], head8, grow, row, T, cw_ref, cb_ref, wa_ref, ba_ref,
                                                         wx_ref, bx_ref, lam_ref)
        b = mult * i_ * conv
        s = 1
        while s < T:
            keep = row >= s
            a_s = jnp.where(keep, pltpu.roll(a, s, 0), 1.0)
            b_s = jnp.where(keep, pltpu.roll(b, s, 0), 0.0)
            b = a * b_s + b
            a = a * a_s
            s *= 2
        h = b + a * carry[0:1, :]
        carry[...] = jnp.broadcast_to(h[T - 1:T, :], carry.shape)
        h_ref[...] = h
        gv = g_ref[...]
        y_ref[...] = (h * (gv * _sigmoid(gv))).astype(BF)

    tile, halo, vec, cwspec, wblk = _rglru_specs(T, nt, False)
    return pl.pallas_call(
        body, name="rglru_fwd", grid=(RNN_BLOCKS, nt),
        in_specs=[tile, halo, tile, cwspec, vec, wblk, vec, wblk, vec, vec],
        out_specs=[tile, tile],
        out_shape=[jax.ShapeDtypeStruct((S, D_RNN), F32), jax.ShapeDtypeStruct((S, D_RNN), BF)],
        scratch_shapes=[pltpu.VMEM((8, RNN_BLOCK), F32)],
        compiler_params=_params(("parallel", "arbitrary")),
    )(xr, xr, g, cw, cb, wa, ba, wx, bx, lam)


def _rglru_bwd(xr, g, h, dy, cw, cb, wa, ba, wx, bx, lam, T):
    S = xr.shape[0]
    nt = S // T

    def body(u_ref, uh_ref, g_ref, h_ref, hh_ref, dy_ref, cw_ref, cb_ref, wa_ref, ba_ref, wx_ref, bx_ref, lam_ref,
             du_ref, dg_ref, dwa_ref, dwx_ref, dvec_ref, c_dhh, c_a, c_dconv):
        t = pl.program_id(1)
        tt = nt - 1 - t

        @pl.when(t == 0)
        def _():
            c_dhh[...] = jnp.zeros_like(c_dhh)
            c_a[...] = jnp.zeros_like(c_a)
            c_dconv[...] = jnp.zeros_like(c_dconv)
            dwa_ref[...] = jnp.zeros_like(dwa_ref)
            dwx_ref[...] = jnp.zeros_like(dwx_ref)
            dvec_ref[...] = jnp.zeros_like(dvec_ref)

        row = lax.broadcasted_iota(jnp.int32, (T, RNN_BLOCK), 0)
        row8 = row[:8]
        grow = row + tt * T
        head8 = jnp.where(tt > 0, uh_ref[...], 0.0)
        us, conv, cbf, r_, i_, sp, a, mult_raw, mult = _rglru_gates(
            u_ref[...], head8, grow, row, T, cw_ref, cb_ref, wa_ref, ba_ref, wx_ref, bx_ref, lam_ref)
        hv = h_ref[...]
        hprev = _shift_down(hv, 1, jnp.where(tt > 0, hh_ref[...], 0.0), row, T)
        gv = g_ref[...]
        sg = _sigmoid(gv)
        dyv = dy_ref[...]
        dg_ref[...] = (dyv * hv * (sg * (1.0 + gv * (1.0 - sg)))).astype(BF)
        d = dyv * (gv * sg)
        A = _shift_up(a, 1, c_a[...], row, T)
        s = 1
        while s < T:
            keep = row < T - s
            A_s = jnp.where(keep, pltpu.roll(A, T - s, 0), 1.0)
            d_s = jnp.where(keep, pltpu.roll(d, T - s, 0), 0.0)
            d = A * d_s + d
            A = A * A_s
            s *= 2
        dhh = d + A * c_dhh[0:1, :]
        da = dhh * hprev
        dconv = dhh * mult * i_
        di = dhh * mult * conv
        dmult = dhh * i_ * conv
        dla = da * a - jnp.where(grow == 0, 0.0, dmult * (a * a) / mult_raw)
        dr = dla * (-LRU_C * sp)
        dsp = jnp.sum(dla * (-LRU_C * r_), axis=0, keepdims=True)
        dza = dr * r_ * (1.0 - r_)
        dzx = di * i_ * (1.0 - i_)
        dza_b, dzx_b = dza.astype(BF), dzx.astype(BF)
        dconv = dconv + _dot(dza_b, wa_ref[0], NT) + _dot(dzx_b, wx_ref[0], NT)
        dwa_ref[0] += _dot(cbf, dza_b, TN)
        dwx_ref[0] += _dot(cbf, dzx_b, TN)
        lam = lam_ref[...]
        rows = [jnp.sum(dconv * us[k], axis=0, keepdims=True) for k in range(CONV_W)]
        rows += [jnp.sum(dconv, axis=0, keepdims=True), jnp.sum(dza, axis=0, keepdims=True),
                 jnp.sum(dzx, axis=0, keepdims=True), dsp * (-_sigmoid(-lam))]
        upd = jnp.zeros((8, RNN_BLOCK), F32)
        for j, rv in enumerate(rows):
            upd = upd + jnp.where(row8 == j, rv, 0.0)
        dvec_ref[...] += upd
        tail8 = c_dconv[...]
        du = dconv * cw_ref[0:1, :]
        for k in range(1, CONV_W):
            du = du + _shift_up(dconv, k, tail8, row, T) * cw_ref[k:k + 1, :]
        du_ref[...] = du.astype(BF)
        c_dhh[...] = jnp.broadcast_to(dhh[0:1, :], c_dhh.shape)
        c_a[...] = jnp.broadcast_to(a[0:1, :], c_a.shape)
        c_dconv[...] = dconv[:8]

    tile, halo, vec, cwspec, wblk = _rglru_specs(T, nt, True)
    acc8 = pl.BlockSpec((8, RNN_BLOCK), lambda n, t: (0, n))
    return pl.pallas_call(
        body, name="rglru_bwd", grid=(RNN_BLOCKS, nt),
        in_specs=[tile, halo, tile, tile, halo, tile, cwspec, vec, wblk, vec, wblk, vec, vec],
        out_specs=[tile, tile, wblk, wblk, acc8],
        out_shape=[jax.ShapeDtypeStruct((S, D_RNN), BF), jax.ShapeDtypeStruct((S, D_RNN), BF),
                   jax.ShapeDtypeStruct((RNN_BLOCKS, RNN_BLOCK, RNN_BLOCK), F32),
                   jax.ShapeDtypeStruct((RNN_BLOCKS, RNN_BLOCK, RNN_BLOCK), F32),
                   jax.ShapeDtypeStruct((8, D_RNN), F32)],
        scratch_shapes=[pltpu.VMEM((8, RNN_BLOCK), F32)] * 3,
        compiler_params=_params(("parallel", "arbitrary")),
    )(xr, xr, g, h, h, dy, cw, cb, wa, ba, wx, bx, lam)


def _rel_bucket_map():
    qi = np.arange(WINDOW)[:, None]
    kj = np.arange(2 * WINDOW)[None, :]
    dist = jnp.asarray(qi + WINDOW - kj, jnp.int32)
    n = jnp.maximum(dist, 0)
    max_exact = REL_BUCKETS // 2
    ratio = jnp.log(jnp.maximum(n, 1).astype(F32) / max_exact) / math.log(REL_MAX_DIST / max_exact)
    large = jnp.minimum(max_exact + (ratio * (REL_BUCKETS - max_exact)).astype(jnp.int32), REL_BUCKETS - 1)
    bucket = jnp.where(n < max_exact, n, large).astype(jnp.int32)
    j = np.arange(WINDOW)[None, :]
    return jnp.where(jnp.asarray(j > qi), bucket[:, :WINDOW], bucket[:, WINDOW:])


def _swa_common(n, kv_ref, bucket_ref, relb_ref, bias_scr):
    @pl.when(n == 0)
    def _():
        bk = bucket_ref[...]
        for h in range(SWA_HEADS):
            acc = jnp.zeros((WINDOW, WINDOW), F32)
            for b in range(REL_BUCKETS):
                acc = acc + jnp.where(bk == b, relb_ref[b, h], 0.0)
            bias_scr[h] = acc

    prev0 = pl.multiple_of(jnp.maximum(n - 1, 0) * WINDOW, WINDOW)
    cur0 = pl.multiple_of(n * WINDOW, WINDOW)
    kk = jnp.concatenate([kv_ref[pl.ds(prev0, WINDOW), :], kv_ref[pl.ds(cur0, WINDOW), :]], axis=0).astype(F32)
    rowi = lax.broadcasted_iota(jnp.int32, (WINDOW, WINDOW), 0)
    col = lax.broadcasted_iota(jnp.int32, (WINDOW, WINDOW), 1)
    from_prev = col > rowi
    no_prev = jnp.where(n > 0, 0, 4 * WINDOW)
    valid = jnp.logical_or(col <= rowi, col > rowi + no_prev)
    return kk, from_prev, valid, prev0, cur0


def _fold(full, from_prev):
    return jnp.where(from_prev, full[:, :WINDOW], full[:, WINDOW:])


def _unfold(sq, from_prev):
    return jnp.concatenate([jnp.where(from_prev, sq, 0.0), jnp.where(from_prev, 0.0, sq)], axis=1)


def _half_pair(part, kvh):
    lo = lax.broadcasted_iota(jnp.int32, part.shape, 1) < SWA_HD
    if kvh == 0:
        pa = jnp.where(lo, part, 0.0)
        pb = pltpu.roll(pa, SWA_HD, 1)
    else:
        pb = jnp.where(lo, 0.0, part)
        pa = pltpu.roll(pb, SWA_HD, 1)
    return pa.astype(BF), pb.astype(BF)


def _swa_probs(q2, kx, bias, sink, from_prev, valid):
    lg = _fold(_dot(q2, kx, NT), from_prev) * (SWA_HD ** -0.5) + bias
    lg = jnp.where(valid, lg, NEG_INF)
    m = jnp.maximum(jnp.max(lg, axis=-1, keepdims=True), sink)
    e = jnp.exp(lg - m)
    es = jnp.exp(sink - m)
    den = jnp.sum(e, axis=-1, keepdims=True) + es
    return e / den, es / den


def _swa_fwd(q, kv, g, bucket, rel_bias, sinks):
    S = q.shape[0]
    nb = S // WINDOW

    def body(q_ref, kv_ref, g_ref, bucket_ref, relb_ref, sink_ref, o_ref, y_ref, bias_scr):
        n = pl.program_id(0)
        kk, from_prev, valid, _, _ = _swa_common(n, kv_ref, bucket_ref, relb_ref, bias_scr)
        for kvh in range(SWA_KV_HEADS):
            ka, kb = _half_pair(kk[:, :128], kvh)
            va, vb = _half_pair(kk[:, 128:], kvh)
            for p in range(4):
                c0 = kvh * 512 + p * 128
                h0 = kvh * 8 + 2 * p
                q2 = q_ref[:, c0:c0 + 128]
                p0, _ = _swa_probs(q2, ka, bias_scr[h0], sink_ref[0, h0], from_prev, valid)
                p1, _ = _swa_probs(q2, kb, bias_scr[h0 + 1], sink_ref[0, h0 + 1], from_prev, valid)
                o2 = _dot(_unfold(p0, from_prev).astype(BF), va, NN) + _dot(_unfold(p1, from_prev).astype(BF), vb, NN)
                o_ref[:, c0:c0 + 128] = o2
                gv = g_ref[:, c0:c0 + 128]
                y_ref[:, c0:c0 + 128] = (o2 * (gv * _sigmoid(gv))).astype(BF)

    blk = pl.BlockSpec((WINDOW, 1024), lambda n: (n, 0))
    smem = pl.BlockSpec(memory_space=pltpu.SMEM)
    return pl.pallas_call(
        body, name="swa_fwd", grid=(nb,),
        in_specs=[blk, pl.BlockSpec((S, 256), lambda n: (0, 0)), blk, pl.BlockSpec((WINDOW, WINDOW), lambda n: (0, 0)), smem, smem],
        out_specs=[blk, blk],
        out_shape=[jax.ShapeDtypeStruct((S, 1024), F32), jax.ShapeDtypeStruct((S, 1024), BF)],
        scratch_shapes=[pltpu.VMEM((SWA_HEADS, WINDOW, WINDOW), F32)],
        compiler_params=_params(("arbitrary",)),
    )(q, kv, g, bucket, rel_bias, sinks)


def _swa_bwd(q, kv, g, o, dy, bucket, rel_bias, sinks):
    S = q.shape[0]
    nb = S // WINDOW

    def body(q_ref, kv_ref, g_ref, o_ref, dy_ref, bucket_ref, relb_ref, sink_ref,
             dq_ref, dg_ref, dkv_ref, dsink_ref, drel_ref, bias_scr, dbias_scr, dsink_scr):
        n = pl.program_id(0)

        @pl.when(n == 0)
        def _():
            dbias_scr[...] = jnp.zeros_like(dbias_scr)
            dsink_scr[...] = jnp.zeros_like(dsink_scr)
            dkv_ref[...] = jnp.zeros_like(dkv_ref)

        kk, from_prev, valid, prev0, cur0 = _swa_common(n, kv_ref, bucket_ref, relb_ref, bias_scr)
        lane = lax.broadcasted_iota(jnp.int32, (WINDOW, 128), 1)
        lo256 = lax.broadcasted_iota(jnp.int32, (2 * WINDOW, 128), 1) < SWA_HD
        dks, dvs = [], []
        for kvh in range(SWA_KV_HEADS):
            ka, kb = _half_pair(kk[:, :128], kvh)
            va, vb = _half_pair(kk[:, 128:], kvh)
            dka = jnp.zeros((2 * WINDOW, 128), F32)
            dkb, dva, dvb = dka, dka, dka
            for p in range(4):
                c0 = kvh * 512 + p * 128
                h0 = kvh * 8 + 2 * p
                q2 = q_ref[:, c0:c0 + 128]
                gv = g_ref[:, c0:c0 + 128]
                sg = _sigmoid(gv)
                dyv = dy_ref[:, c0:c0 + 128]
                dg_ref[:, c0:c0 + 128] = (dyv * o_ref[:, c0:c0 + 128] * (sg * (1.0 + gv * (1.0 - sg)))).astype(BF)
                do2 = (dyv * (gv * sg)).astype(BF)
                dq2 = jnp.zeros((WINDOW, 128), F32)
                for half, (kx, vx) in enumerate(((ka, va), (kb, vb))):
                    hh = h0 + half
                    pr, ps = _swa_probs(q2, kx, bias_scr[hh], sink_ref[0, hh], from_prev, valid)
                    dp = _fold(_dot(do2, vx, NT), from_prev)
                    delta = jnp.sum(pr * dp, axis=-1, keepdims=True)
                    ds = pr * (dp - delta)
                    dbias_scr[hh] += ds
                    dsink_scr[...] += jnp.where(lane == hh, ps * delta, 0.0)
                    dsb = _unfold(ds * (SWA_HD ** -0.5), from_prev).astype(BF)
                    prb = _unfold(pr, from_prev).astype(BF)
                    dq2 = dq2 + _dot(dsb, kx, NN)
                    if half == 0:
                        dka = dka + _dot(dsb, q2, TN)
                        dva = dva + _dot(prb, do2, TN)
                    else:
                        dkb = dkb + _dot(dsb, q2, TN)
                        dvb = dvb + _dot(prb, do2, TN)
                dq_ref[:, c0:c0 + 128] = dq2.astype(BF)
            dks.append(jnp.where(lo256, dka, 0.0) + pltpu.roll(jnp.where(lo256, 0.0, dkb), SWA_HD, 1))
            dvs.append(jnp.where(lo256, dva, 0.0) + pltpu.roll(jnp.where(lo256, 0.0, dvb), SWA_HD, 1))
        dk = dks[0] + pltpu.roll(dks[1], SWA_HD, 1)
        dv = dvs[0] + pltpu.roll(dvs[1], SWA_HD, 1)
        dkv_ref[pl.ds(prev0, WINDOW), 0:128] += dk[:WINDOW]
        dkv_ref[pl.ds(prev0, WINDOW), 128:256] += dv[:WINDOW]
        dkv_ref[pl.ds(cur0, WINDOW), 0:128] += dk[WINDOW:]
        dkv_ref[pl.ds(cur0, WINDOW), 128:256] += dv[WINDOW:]

        @pl.when(n == nb - 1)
        def _():
            dsink_ref[...] = -jnp.sum(dsink_scr[...], axis=0, keepdims=True)
            bk = bucket_ref[...]
            r32 = lax.broadcasted_iota(jnp.int32, (REL_BUCKETS, 128), 0)
            l32 = lax.broadcasted_iota(jnp.int32, (REL_BUCKETS, 128), 1)
            acc = jnp.zeros((REL_BUCKETS, 128), F32)
            for b in range(REL_BUCKETS):
                mb = bk == b
                for h in range(SWA_HEADS):
                    t1 = jnp.sum(jnp.where(mb, dbias_scr[h], 0.0), axis=1, keepdims=True)
                    val = jnp.sum(t1, axis=0, keepdims=True)
                    acc = acc + jnp.where(jnp.logical_and(r32 == b, l32 == h), val, 0.0)
            drel_ref[...] = acc

    blk = pl.BlockSpec((WINDOW, 1024), lambda n: (n, 0))
    smem = pl.BlockSpec(memory_space=pltpu.SMEM)
    whole = lambda shape: pl.BlockSpec(shape, lambda n: (0, 0))
    return pl.pallas_call(
        body, name="swa_bwd", grid=(nb,),
        in_specs=[blk, whole((S, 256)), blk, blk, blk, whole((WINDOW, WINDOW)), smem, smem],
        out_specs=[blk, blk, whole((S, 256)), whole((1, 128)), whole((REL_BUCKETS, 128))],
        out_shape=[jax.ShapeDtypeStruct((S, 1024), BF), jax.ShapeDtypeStruct((S, 1024), BF),
                   jax.ShapeDtypeStruct((S, 256), F32), jax.ShapeDtypeStruct((1, 128), F32),
                   jax.ShapeDtypeStruct((REL_BUCKETS, 128), F32)],
        scratch_shapes=[pltpu.VMEM((SWA_HEADS, WINDOW, WINDOW), F32), pltpu.VMEM((SWA_HEADS, WINDOW, WINDOW), F32),
                        pltpu.VMEM((WINDOW, 128), F32)],
        compiler_params=_params(("arbitrary",)),
    )(q, kv, g, o, dy, bucket, rel_bias, sinks)


def _mem_probs(qh, mk):
    lg = _dot(qh, mk, NT) * (MEM_HD ** -0.5)
    e = jnp.exp(lg - jnp.max(lg, axis=-1, keepdims=True))
    return e / jnp.sum(e, axis=-1, keepdims=True)


def _mem_fwd(q, mkv, g):
    S = q.shape[0]
    M = mkv.shape[0]
    tq = 256

    def body(q_ref, mkv_ref, g_ref, o_ref, y_ref):
        for h in range(MEM_HEADS):
            c0 = h * MEM_HD
            pr = _mem_probs(q_ref[:, c0:c0 + MEM_HD], mkv_ref[:, c0:c0 + MEM_HD])
            o = _dot(pr.astype(BF), mkv_ref[:, D_MEM + c0:D_MEM + c0 + MEM_HD], NN)
            o_ref[:, c0:c0 + MEM_HD] = o
            gv = g_ref[:, c0:c0 + MEM_HD]
            y_ref[:, c0:c0 + MEM_HD] = (o * (gv * _sigmoid(gv))).astype(BF)

    blk = pl.BlockSpec((tq, D_MEM), lambda i: (i, 0))
    return pl.pallas_call(
        body, name="mem_fwd", grid=(S // tq,),
        in_specs=[blk, pl.BlockSpec((M, 2 * D_MEM), lambda i: (0, 0)), blk], out_specs=[blk, blk],
        out_shape=[jax.ShapeDtypeStruct((S, D_MEM), F32), jax.ShapeDtypeStruct((S, D_MEM), BF)],
        compiler_params=_params(("parallel",)),
    )(q, mkv, g)


def _mem_bwd(q, mkv, g, o, dy):
    S = q.shape[0]
    M = mkv.shape[0]
    tq = 256

    def body(q_ref, mkv_ref, g_ref, o_ref, dy_ref, dq_ref, dg_ref, dmkv_ref):
        @pl.when(pl.program_id(0) == 0)
        def _():
            dmkv_ref[...] = jnp.zeros_like(dmkv_ref)

        for h in range(MEM_HEADS):
            c0 = h * MEM_HD
            qh = q_ref[:, c0:c0 + MEM_HD]
            mk = mkv_ref[:, c0:c0 + MEM_HD]
            mv = mkv_ref[:, D_MEM + c0:D_MEM + c0 + MEM_HD]
            gv = g_ref[:, c0:c0 + MEM_HD]
            sg = _sigmoid(gv)
            dyv = dy_ref[:, c0:c0 + MEM_HD]
            dg_ref[:, c0:c0 + MEM_HD] = (dyv * o_ref[:, c0:c0 + MEM_HD] * (sg * (1.0 + gv * (1.0 - sg)))).astype(BF)
            do = (dyv * (gv * sg)).astype(BF)
            pr = _mem_probs(qh, mk)
            dp = _dot(do, mv, NT)
            ds = pr * (dp - jnp.sum(pr * dp, axis=-1, keepdims=True))
            dsb = (ds * (MEM_HD ** -0.5)).astype(BF)
            dq_ref[:, c0:c0 + MEM_HD] = _dot(dsb, mk, NN).astype(BF)
            dmkv_ref[:, c0:c0 + MEM_HD] += _dot(dsb, qh, TN)
            dmkv_ref[:, D_MEM + c0:D_MEM + c0 + MEM_HD] += _dot(pr.astype(BF), do, TN)

    blk = pl.BlockSpec((tq, D_MEM), lambda i: (i, 0))
    whole = pl.BlockSpec((M, 2 * D_MEM), lambda i: (0, 0))
    return pl.pallas_call(
        body, name="mem_bwd", grid=(S // tq,),
        in_specs=[blk, whole, blk, blk, blk], out_specs=[blk, blk, whole],
        out_shape=[jax.ShapeDtypeStruct((S, D_MEM), BF), jax.ShapeDtypeStruct((S, D_MEM), BF),
                   jax.ShapeDtypeStruct((M, 2 * D_MEM), F32)],
        compiler_params=_params(("arbitrary",)),
    )(q, mkv, g, o, dy)


MERGE_TN = 512


def _merge_specs(tm):
    ytile = pl.BlockSpec((tm, 1024), lambda i, j: (i, 0))
    wblk = pl.BlockSpec((1024, MERGE_TN), lambda i, j: (0, j))
    gls = [pl.BlockSpec((None, tm, MERGE_TN), (lambda i, j, br=br: (br, i, j))) for br in range(3)]
    otile = pl.BlockSpec((tm, MERGE_TN), lambda i, j: (i, j))
    return ytile, wblk, gls, otile


def _merge_fwd(ys, ws, gl, tm):
    S = gl.shape[1]

    def body(y0, y1, y2, w0, w1, w2, g0, g1, g2, o_ref):
        acc = None
        for y_ref, w_ref, g_ref in ((y0, w0, g0), (y1, w1, g1), (y2, w2, g2)):
            term = _sigmoid(g_ref[...]) * _dot(y_ref[...], w_ref[...], NN)
            acc = term if acc is None else acc + term
        o_ref[...] = acc.astype(BF)

    ytile, wblk, gls, otile = _merge_specs(tm)
    return pl.pallas_call(
        body, name="merge_fwd", grid=(S // tm, D_MODEL // MERGE_TN),
        in_specs=[ytile] * 3 + [wblk] * 3 + gls, out_specs=otile,
        out_shape=jax.ShapeDtypeStruct((S, D_MODEL), BF),
        compiler_params=_params(("parallel", "arbitrary")),
    )(*ys, *ws, gl, gl, gl)


def _merge_bwd(dout, w_out, ys, ws, gl, tm):
    S = gl.shape[1]

    def body(do_ref, wo_ref, y0, y1, y2, w0, w1, w2, g0, g1, g2, dg0, dg1, dg2, dp0, dp1, dp2):
        dm = _dot(do_ref[...], wo_ref[...], NT)
        for y_ref, w_ref, g_ref, dg_ref, dp_ref in ((y0, w0, g0, dg0, dp0), (y1, w1, g1, dg1, dp1), (y2, w2, g2, dg2, dp2)):
            gate = _sigmoid(g_ref[...])
            pv = _dot(y_ref[...], w_ref[...], NN)
            dg_ref[...] = (dm * pv * gate * (1.0 - gate)).astype(BF)
            dp_ref[...] = (dm * gate).astype(BF)

    ytile, wblk, gls, otile = _merge_specs(tm)
    out = jax.ShapeDtypeStruct((S, D_MODEL), BF)
    return pl.pallas_call(
        body, name="merge_bwd", grid=(S // tm, D_MODEL // MERGE_TN),
        in_specs=[pl.BlockSpec((tm, D_MODEL), lambda i, j: (i, 0)), pl.BlockSpec((MERGE_TN, D_MODEL), lambda i, j: (j, 0))]
        + [ytile] * 3 + [wblk] * 3 + gls,
        out_specs=[otile] * 6, out_shape=[out] * 6,
        compiler_params=_params(("parallel", "arbitrary")),
    )(dout, w_out, *ys, *ws, gl, gl, gl)


def _out_loss(merged, w_out, x, target, post_g, tm):
    S = x.shape[0]

    def body(m_ref, w_ref, x_ref, t_ref, g_ref, dout_ref, dy_ref, loss_ref, dpost_ref):
        @pl.when(pl.program_id(0) == 0)
        def _():
            loss_ref[...] = jnp.zeros_like(loss_ref)
            dpost_ref[...] = jnp.zeros_like(dpost_ref)

        out = _dot(m_ref[...], w_ref[...], NN)
        r = lax.rsqrt(jnp.mean(out * out, axis=-1, keepdims=True) + EPS)
        nrm = out * r
        gv = g_ref[...]
        err = (x_ref[...] + nrm * gv) - t_ref[...]
        sq = jnp.sum(jnp.sum(err * err, axis=1, keepdims=True), axis=0, keepdims=True)
        loss_ref[...] += sq * (0.5 / D_MODEL)
        dy = err * (1.0 / D_MODEL)
        dy_ref[...] = dy
        dpost_ref[...] += jnp.sum(dy * nrm, axis=0, keepdims=True)
        dn = dy * gv
        dout_ref[...] = (r * (dn - nrm * jnp.mean(dn * nrm, axis=-1, keepdims=True))).astype(BF)

    row = pl.BlockSpec((tm, D_MODEL), lambda i: (i, 0))
    return pl.pallas_call(
        body, name="out_loss", grid=(S // tm,),
        in_specs=[row, pl.BlockSpec((D_MODEL, D_MODEL), lambda i: (0, 0)), row, row, pl.BlockSpec((1, D_MODEL), lambda i: (0, 0))],
        out_specs=[row, row, pl.BlockSpec((8, 128), lambda i: (0, 0)), pl.BlockSpec((1, D_MODEL), lambda i: (0, 0))],
        out_shape=[jax.ShapeDtypeStruct((S, D_MODEL), BF), jax.ShapeDtypeStruct((S, D_MODEL), F32),
                   jax.ShapeDtypeStruct((8, 128), F32), jax.ShapeDtypeStruct((1, D_MODEL), F32)],
        compiler_params=_params(("arbitrary",)),
    )(merged, w_out, x, target, post_g)


def _dh_dx(dproj, w_in, x, dy, pre_g, tm, tk):
    S = x.shape[0]
    nk = D_IN // tk

    def body(dp_ref, w_ref, x_ref, dy_ref, g_ref, dx_ref, dpre_ref, acc_ref):
        i, k = pl.program_id(0), pl.program_id(1)

        @pl.when(jnp.logical_and(i == 0, k == 0))
        def _():
            dpre_ref[...] = jnp.zeros_like(dpre_ref)

        @pl.when(k == 0)
        def _():
            acc_ref[...] = jnp.zeros_like(acc_ref)

        acc_ref[...] += _dot(dp_ref[...], w_ref[...], NT)

        @pl.when(k == nk - 1)
        def _():
            dh = acc_ref[...]
            xv = x_ref[...]
            r = lax.rsqrt(jnp.mean(xv * xv, axis=-1, keepdims=True) + EPS)
            nrm = xv * r
            dpre_ref[...] += jnp.sum(dh * nrm, axis=0, keepdims=True)
            dn = dh * g_ref[...]
            dx_ref[...] = r * (dn - nrm * jnp.mean(dn * nrm, axis=-1, keepdims=True)) + dy_ref[...]

    row = pl.BlockSpec((tm, D_MODEL), lambda i, k: (i, 0))
    vec = pl.BlockSpec((1, D_MODEL), lambda i, k: (0, 0))
    return pl.pallas_call(
        body, name="dh_dx", grid=(S // tm, nk),
        in_specs=[pl.BlockSpec((tm, tk), lambda i, k: (i, k)), pl.BlockSpec((D_MODEL, tk), lambda i, k: (0, k)), row, row, vec],
        out_specs=[row, vec],
        out_shape=[jax.ShapeDtypeStruct((S, D_MODEL), F32), jax.ShapeDtypeStruct((1, D_MODEL), F32)],
        scratch_shapes=[pltpu.VMEM((tm, D_MODEL), F32)],
        compiler_params=_params(("arbitrary", "arbitrary"), 56),
    )(dproj, w_in, x, dy, pre_g)


def _sum_parts(parts, name):
    P, R, C = parts.shape
    tr = max(t for t in range(8, 513, 8) if R % t == 0)

    def body(p_ref, o_ref):
        acc = p_ref[0]
        for j in range(1, P):
            acc = acc + p_ref[j]
        o_ref[...] = acc

    return pl.pallas_call(
        body, name=name, grid=(R // tr,),
        in_specs=[pl.BlockSpec((P, tr, C), lambda i: (0, i, 0))], out_specs=pl.BlockSpec((tr, C), lambda i: (i, 0)),
        out_shape=jax.ShapeDtypeStruct((R, C), F32), compiler_params=_params(("parallel",)),
    )(parts)


def _adamw(parts, w, m, v, name):
    groups = list(parts) if isinstance(parts, (list, tuple)) else [parts]
    P, rows, C = groups[0].shape
    R = rows * len(groups)
    tr = 128 if rows % 128 == 0 else rows
    per = rows // tr
    c1 = 1.0 - ADAM_B1 ** ADAM_STEP
    c2 = 1.0 - ADAM_B2 ** ADAM_STEP

    def body(*refs):
        p_refs = refs[:len(groups)]
        w_ref, m_ref, v_ref, g_ref, d_ref, nm_ref, nv_ref = refs[len(groups):]
        g = None
        for q, p_ref in enumerate(p_refs):
            gq = p_ref[0].astype(F32)
            for j in range(1, P):
                gq = gq + p_ref[j].astype(F32)
            g = gq if g is None else jnp.where(pl.program_id(0) // per == q, gq, g)
        nm = ADAM_B1 * m_ref[...] + (1.0 - ADAM_B1) * g
        nv = ADAM_B2 * v_ref[...] + (1.0 - ADAM_B2) * (g * g)
        g_ref[...] = g
        nm_ref[...] = nm
        nv_ref[...] = nv
        d_ref[...] = -ADAM_LR * ((nm / c1) / (jnp.sqrt(nv / c2) + ADAM_EPS) + ADAM_WD * w_ref[...])

    tile = pl.BlockSpec((tr, C), lambda i: (i, 0))
    out = jax.ShapeDtypeStruct((R, C), F32)
    return pl.pallas_call(
        body, name=name, grid=(R // tr,),
        in_specs=[pl.BlockSpec((P, tr, C), (lambda i, q=q: (0, jnp.clip(i - q * per, 0, per - 1), 0))) for q in range(len(groups))]
        + [tile, tile, tile], out_specs=[tile] * 4, out_shape=[out] * 4,
        compiler_params=_params(("parallel",)),
    )(*groups, w, m, v)


def _forward_a(x, mem, pre_g, mem_g, w_in, conv_w, conv_b, w_a, b_a, w_x, b_x, lam, sinks, rel_bias):
    S = x.shape[0]
    st = dict(T=min(512, S // 2), tm=min(512, S), bucket=_rel_bucket_map())
    st["h"] = _rms_fwd(x, pre_g, "pre_norm")
    st["memn"] = _rms_fwd(mem, mem_g, "mem_norm")
    seg = {}
    for name, c0, width, dt in SEGMENTS:
        seg[name] = _matmul(st["h"], w_in, "nn", S, width, D_MODEL, S, SEG_TILE, D_MODEL, dt, "proj_" + name, b_noff=c0 // SEG_TILE,
                            out_blocked="third" if name == "gl" else None)
    st["seg"] = seg
    st["h_rg"], st["y_rg"] = _rglru_fwd(seg["xr"], seg["g_rg"], conv_w, conv_b, w_a, b_a, w_x, b_x, lam, st["T"])
    st["o_swa"], st["y_swa"] = _swa_fwd(seg["q_s"], seg["kv"], seg["g_swa"], st["bucket"], rel_bias, sinks)
    return st


def _forward_b(st, x, target, post_g, w_memkv, wbr, w_out):
    S = x.shape[0]
    M = st["memn"].shape[0]
    seg = st["seg"]
    st["mkv"] = _matmul(st["memn"], w_memkv, "nn", M, 2 * D_MEM, D_MODEL, M, 512, D_MODEL, BF, "mem_kv")
    st["o_mem"], st["y_mem"] = _mem_fwd(seg["q_m"], st["mkv"], seg["g_mem"])
    st["ys"] = (st["y_rg"], st["y_swa"], st["y_mem"])
    st["merged"] = _merge_fwd(st["ys"], wbr, seg["gl"], st["tm"])
    st["dout"], st["dy"], st["loss"], st["dpost"] = _out_loss(st["merged"], w_out, x, target, post_g, min(256, S))
    return st


def _backward_a(st, mem, w_memkv, wbr, w_out, conv_w, conv_b, w_a, b_a, w_x, b_x, lam):
    S = st["h"].shape[0]
    M = mem.shape[0]
    seg, ys, tm = st["seg"], st["ys"], st["tm"]
    st["dw_out"] = _matmul(st["merged"], st["dout"], "tn", D_MODEL, D_MODEL, S, 256, D_MODEL, S, BF, "dw_out", out_blocked="row")
    dgl0, dgl1, dgl2, dp0, dp1, dp2 = _merge_bwd(st["dout"], w_out, ys, wbr, seg["gl"], tm)
    st["dgl"] = (dgl0, dgl1, dgl2)
    dys, dwbr = [], []
    for i, dp in enumerate((dp0, dp1, dp2)):
        dys.append(_matmul(dp, wbr[i], "nt", S, 1024, D_MODEL, tm, 1024, D_MODEL, F32, "dy_br%d" % i))
        dwbr.append(_matmul(ys[i], dp, "tn", 1024, D_MODEL, S, 1024, 256, S, BF, "dw_br%d" % i, out_blocked="col"))
    st["dys"], st["dwbr"] = dys, dwbr
    st["dq_m"], st["dg_mem"], dmkv = _mem_bwd(seg["q_m"], st["mkv"], seg["g_mem"], st["o_mem"], dys[2])
    dmkv_b = dmkv.astype(BF)
    st["dw_memkv"] = _matmul(st["memn"], dmkv_b, "tn", D_MODEL, 2 * D_MEM, M, 256, 2 * D_MEM, M, BF, "dw_memkv", out_blocked="row")
    dmemn = _matmul(dmkv_b, w_memkv, "nt", M, D_MODEL, 2 * D_MEM, M, 512, 2 * D_MEM, F32, "dmemn")
    st["dmem_g"] = _rms_gain_grad(dmemn, mem, "dmem_gain")
    st["dxr"], st["dg_rg"], st["dw_a"], st["dw_x"], st["dvec"] = _rglru_bwd(
        seg["xr"], seg["g_rg"], st["h_rg"], dys[0], conv_w, conv_b, w_a, b_a, w_x, b_x, lam, st["T"])
    return st


def _backward_b(st, rel_bias, sinks):
    seg = st["seg"]
    dq_s, dg_swa, dkv, st["dsinks"], st["drel"] = _swa_bwd(seg["q_s"], seg["kv"], seg["g_swa"], st["o_swa"], st["dys"][1],
                                                           st["bucket"], rel_bias, sinks)
    st["dproj"] = jnp.concatenate([st["dxr"], st["dg_rg"], dq_s, dkv.astype(BF), dg_swa, st["dq_m"], st["dg_mem"], *st["dgl"]], axis=1)
    return st


def _dw_in_half(st, half, dep=None):
    S = st["h"].shape[0]
    return _matmul(st["h"], st["dproj"], "tn", D_MODEL // 2, D_IN, S, 512, 1792, S, BF, "dw_in%d" % half, a_moff=2 * half, dep=dep)


def _owner_blocks(a):
    return jnp.swapaxes(a.reshape((4, 2) + a.shape[1:]), 0, 1)


def _local_step(x, mem, target, pre_g, post_g, mem_g, w_in, conv_w, conv_b, w_a, b_a, w_x, b_x, lam, sinks, rel_bias,
                w_memkv, wbr, w_out):
    st = _forward_a(x, mem, pre_g, mem_g, w_in, conv_w, conv_b, w_a, b_a, w_x, b_x, lam, sinks, rel_bias)
    st = _forward_b(st, x, target, post_g, w_memkv, wbr, w_out)
    st = _backward_a(st, mem, w_memkv, wbr, w_out, conv_w, conv_b, w_a, b_a, w_x, b_x, lam)
    st = _backward_b(st, rel_bias, sinks)
    st["dw_in"] = jnp.concatenate([_dw_in_half(st, 0), _dw_in_half(st, 1)], axis=0)
    st["grad_x"], st["dpre"] = _dh_dx(st["dproj"], w_in, x, st["dy"], pre_g, st["tm"], 896)
    return st


def _pad_rows(a, rows):
    a = a.reshape(-1, 128) if a.shape[-1] % 128 == 0 else jnp.pad(a, ((0, 0), (0, 128 - a.shape[-1])))
    return jnp.pad(a, ((0, rows - a.shape[0]), (0, 0))) if a.shape[0] < rows else a


def kernel(x, mem, pre_norm_g, post_norm_g, mem_norm_g, w_in, conv_w, conv_b, w_rg_a, b_rg_a, w_rg_x, b_rg_x, lru_lambda, swa_sinks, rel_bias, w_mem_kv, w_br_rg, w_br_swa, w_br_mem, w_out, loss_target, m_pre_norm_g, m_post_norm_g, m_mem_norm_g, m_w_in, m_conv_w, m_conv_b, m_w_rg_a, m_b_rg_a, m_w_rg_x, m_b_rg_x, m_lru_lambda, m_swa_sinks, m_rel_bias, m_w_mem_kv, m_w_br_rg, m_w_br_swa, m_w_br_mem, m_w_out, v_pre_norm_g, v_post_norm_g, v_mem_norm_g, v_w_in, v_conv_w, v_conv_b, v_w_rg_a, v_b_rg_a, v_w_rg_x, v_b_rg_x, v_lru_lambda, v_swa_sinks, v_rel_bias, v_w_mem_kv, v_w_br_rg, v_w_br_swa, v_w_br_mem, v_w_out):
    cx, cy, cc = lax.axis_index("x"), lax.axis_index("y"), lax.axis_index("c")
    me = 4 * cx + 2 * cy + cc
    chip = 2 * cx + cy
    core = jnp.reshape(cc, (1,)).astype(jnp.int32)
    x0, mem0 = x[0], mem[0]
    w_a_b, w_x_b = w_rg_a[0].astype(BF), w_rg_x[0].astype(BF)

    def landing(own, slot, slots, kind="lead"):
        if kind == "cols":
            return lax.dynamic_update_slice(lax.empty((own.shape[0], slots * own.shape[1]), own.dtype), own, (0, slot * own.shape[1]))
        return lax.dynamic_update_slice(lax.empty((slots,) + own.shape, own.dtype), own[None], (slot,) + (0,) * own.ndim)

    def scatter_start(parts, tag):
        got = _swap_with_sibling(parts, "scatter_%s_swap" % tag)
        sums = [_pair_sum(p, g, core, "scatter_%s_sum%d" % (tag, i)) for i, (p, g) in enumerate(zip(parts, got))]
        lands = [landing(lax.dynamic_index_in_dim(s, chip, 0, keepdims=False), chip, 4) for s in sums]
        return _exchange_start(sums, lands, _plan_scatter(len(sums)), "scatter_%s_start" % tag)

    g_in, g_cw = _all_gather([w_in[0].astype(BF), conv_w[0]], "gather_w_in")
    w_in_f = jnp.transpose(g_in, (1, 0, 2)).reshape(D_MODEL, D_IN)
    conv_w_f = jnp.transpose(g_cw, (1, 0, 2)).reshape(CONV_W, D_RNN)

    after_first = jnp.minimum(jnp.abs(g_cw[0, 0, 0]), 0.0).astype(BF)
    rest = [w.astype(BF) + after_first for w in (w_mem_kv[0], w_br_rg[0], w_br_swa[0], w_br_mem[0], w_out[0])]
    kinds = ["lead", "cols", "cols", "cols", "lead"]
    plan_g = _plan_gather(kinds)
    g_send, g_recv, g_src, g_land, g_token = _exchange_start(rest, [landing(w, me, N_DEV, kd) for w, kd in zip(rest, kinds)], plan_g,
                                                             "gather_rest_start")
    st = _forward_a(x0, mem0, pre_norm_g + g_token[0:1, 0:1], mem_norm_g, w_in_f, conv_w_f, conv_b, w_a_b, b_rg_a, w_x_b, b_rg_x,
                    lru_lambda, swa_sinks, rel_bias)
    g_land = _exchange_wait(g_send, g_recv, g_src, g_land, plan_g, st["y_swa"], "gather_rest_wait")
    g_land = _forward_to_sibling(g_land, kinds, "gather_rest_forward")
    w_memkv_f = g_land[0].reshape(D_MODEL, 2 * D_MEM)
    wbr = (g_land[1], g_land[2], g_land[3])
    w_out_f = g_land[4].reshape(D_MODEL, D_MODEL)

    st = _forward_b(st, x0, loss_target[0], post_norm_g, w_memkv_f, wbr, w_out_f)
    st = _backward_a(st, mem0, w_memkv_f, wbr, w_out_f, conv_w_f, conv_b, w_a_b, b_rg_a, w_x_b, b_rg_x, lru_lambda)
    parts_a = [st["dw_memkv"], st["dwbr"][0], st["dwbr"][1], st["dwbr"][2], st["dw_out"],
               _owner_blocks(st["dw_a"]), _owner_blocks(st["dw_x"])]
    plan_a = _plan_scatter(len(parts_a))
    a_send, a_recv, a_src, a_land, a_token = scatter_start(parts_a, "a")

    st = _backward_b(st, rel_bias, swa_sinks + a_token[0:1, 0:1])
    plan_b = _plan_scatter(1)
    halves, dep = [], None
    for half in range(2):
        dwh = _dw_in_half(st, half, dep)
        parts_b = [jnp.transpose(dwh.reshape(D_MODEL // 2, 4, 2, D_IN // N_DEV), (2, 1, 0, 3))]
        halves.append(scatter_start(parts_b, "b%d" % half))
        dep = halves[-1][4]
    grad_x, dpre = _dh_dx(st["dproj"], w_in_f, x0, st["dy"], pre_norm_g + dep[0:1, 0:1], st["tm"], 896)
    a_land = _exchange_wait(a_send, a_recv, a_src, a_land, plan_a, grad_x, "scatter_a_wait")
    g_wa_blk = _sum_parts(a_land[5], "sum_w_rg_a")
    g_wx_blk = _sum_parts(a_land[6], "sum_w_rg_x")
    big = [None]
    for j, (wt, mt, vt) in enumerate(((w_mem_kv, m_w_mem_kv, v_w_mem_kv), (w_br_rg, m_w_br_rg, v_w_br_rg),
                                      (w_br_swa, m_w_br_swa, v_w_br_swa), (w_br_mem, m_w_br_mem, v_w_br_mem), (w_out, m_w_out, v_w_out))):
        big.append([a[None] for a in _adamw(a_land[j], wt[0], mt[0], vt[0], "adamw_big%d" % (j + 1))])
    after, b_lands = big[5][1], []
    for half, (b_send, b_recv, b_src, b_land, _) in enumerate(halves):
        b_lands.append(_exchange_wait(b_send, b_recv, b_src, b_land, plan_b, after, "scatter_b%d_wait" % half)[0])
        after = b_lands[-1]
    big[0] = [a[None] for a in _adamw(b_lands, w_in[0], m_w_in[0], v_w_in[0], "adamw_big0")]
    links_free = jnp.minimum(jnp.abs(big[0][0][0, 0, 0]), 0.0)

    pack = jnp.concatenate([dpre.reshape(16, 128), st["dpost"].reshape(16, 128), st["dmem_g"].reshape(16, 128),
                            st["dvec"].reshape(64, 128), _pad_rows(st["dsinks"], 8), st["drel"], g_wa_blk, g_wx_blk], axis=0) + links_free
    gathered = _all_gather([pack], "gather_small")[0]
    gs = _sum_parts(gathered, "sum_small")
    g_pre, g_post, g_memg = gs[0:16].reshape(1, D_MODEL), gs[16:32].reshape(1, D_MODEL), gs[32:48].reshape(1, D_MODEL)
    gvec = gs[48:112].reshape(8, D_RNN)
    g_conv_w = lax.dynamic_slice(gvec[0:CONV_W], (0, me * RNN_BLOCK), (CONV_W, RNN_BLOCK))
    g_conv_b, g_b_a, g_b_x, g_lam = gvec[4:5], gvec[5:6], gvec[6:7], gvec[7:8]
    g_sinks = gs[112:113, :SWA_HEADS]
    g_rel = gs[120:152, :SWA_HEADS]
    g_w_a = gathered[:, 152:280]
    g_w_x = gathered[:, 280:408]

    def packed(ts):
        pre, post, memg, cb, ba, bx, lm, wa, wx, sk, rel, cw = ts
        return jnp.concatenate([pre.reshape(16, 128), post.reshape(16, 128), memg.reshape(16, 128), cb.reshape(8, 128),
                                ba.reshape(8, 128), bx.reshape(8, 128), lm.reshape(8, 128), wa.reshape(1024, 128),
                                wx.reshape(1024, 128), _pad_rows(sk.reshape(1, SWA_HEADS), 8), _pad_rows(rel, 32),
                                _pad_rows(cw.reshape(CONV_W, RNN_BLOCK), 8)], axis=0)

    def unpacked(a):
        return (a[0:16].reshape(1, D_MODEL), a[16:32].reshape(1, D_MODEL), a[32:48].reshape(1, D_MODEL), a[48:56].reshape(1, D_RNN),
                a[56:64].reshape(1, D_RNN), a[64:72].reshape(1, D_RNN), a[72:80].reshape(1, D_RNN),
                a[80:1104].reshape(1, RNN_BLOCKS, RNN_BLOCK, RNN_BLOCK), a[1104:2128].reshape(1, RNN_BLOCKS, RNN_BLOCK, RNN_BLOCK),
                a[2128:2129, :SWA_HEADS], a[2136:2168, :SWA_HEADS], a[2168:2172].reshape(1, CONV_W, RNN_BLOCK))

    g_small = (g_pre, g_post, g_memg, g_conv_b, g_b_a, g_b_x, g_lam, g_w_a, g_w_x, g_sinks, g_rel, g_conv_w)
    w_small = (pre_norm_g, post_norm_g, mem_norm_g, conv_b, b_rg_a, b_rg_x, lru_lambda, w_rg_a, w_rg_x, swa_sinks, rel_bias, conv_w)
    m_small = (m_pre_norm_g, m_post_norm_g, m_mem_norm_g, m_conv_b, m_b_rg_a, m_b_rg_x, m_lru_lambda, m_w_rg_a, m_w_rg_x, m_swa_sinks, m_rel_bias, m_conv_w)
    v_small = (v_pre_norm_g, v_post_norm_g, v_mem_norm_g, v_conv_b, v_b_rg_a, v_b_rg_x, v_lru_lambda, v_w_rg_a, v_w_rg_x, v_swa_sinks, v_rel_bias, v_conv_w)
    sm = [unpacked(a) for a in _adamw(packed(g_small)[None], packed(w_small), packed(m_small), packed(v_small), "adamw_small")]


    loss_total = lax.psum(st["loss"][0, 0], AXES)

    def leaves(k):
        s = sm[k]
        return [s[0], s[1], s[2], big[0][k], s[11], s[3], s[7], s[4], s[8], s[5], s[6], s[9], s[10],
                big[1][k], big[2][k], big[3][k], big[4][k], big[5][k]]

    return (loss_total, grad_x[None], *leaves(0), *leaves(1), *leaves(2), *leaves(3))
```

```python
import math

import jax
import jax.numpy as jnp
import numpy as np
from jax import lax
from jax.experimental import pallas as pl
from jax.experimental.pallas import tpu as pltpu

F32, BF = jnp.float32, jnp.bfloat16
MESH = pl.DeviceIdType.MESH
AXES = ("x", "y", "c")
N_DEV = 8

D_MODEL = 2048
D_RNN = 1024
RNN_BLOCKS = 8
RNN_BLOCK = 128
CONV_W = 4
LRU_C = 8.0
SWA_HEADS = 16
SWA_KV_HEADS = 2
SWA_HD = 64
WINDOW = 128
MEM_HEADS = 4
MEM_HD = 256
D_MEM = 1024
REL_BUCKETS = 32
REL_MAX_DIST = 128
EPS = 1e-6
NEG_INF = -1e30
D_IN = 12544
SEGMENTS = (("xr", 0, 1024, F32), ("g_rg", 1024, 1024, F32), ("q_s", 2048, 1024, BF), ("kv", 3072, 256, BF),
            ("g_swa", 3328, 1024, F32), ("q_m", 4352, 1024, BF), ("g_mem", 5376, 1024, F32), ("gl", 6400, 6144, F32))
SEG_TILE = 256

ADAM_LR, ADAM_B1, ADAM_B2, ADAM_EPS, ADAM_WD, ADAM_STEP = 0.001, 0.9, 0.999, 1e-08, 0.01, 10

NN = (((1,), (0,)), ((), ()))
NT = (((1,), (1,)), ((), ()))
TN = (((0,), (0,)), ((), ()))
MIB = 2 ** 20


def _dot(a, b, dn):
    return lax.dot_general(a, b, dn, preferred_element_type=F32)


def _params(sem, vmem_mib=48):
    return pltpu.CompilerParams(dimension_semantics=sem, vmem_limit_bytes=vmem_mib * MIB)


def _sigmoid(z):
    return 1.0 / (1.0 + jnp.exp(-z))


def _softplus(z):
    return jnp.maximum(z, 0.0) + jnp.log(1.0 + jnp.exp(-jnp.abs(z)))


def _expm1(z):
    p = z * (1.0 + z * (0.5 + z * (1.0 / 6 + z * (1.0 / 24 + z * (1.0 / 120 + z * (1.0 / 720 + z * (1.0 / 5040 + z / 40320)))))))
    return jnp.where(jnp.abs(z) < 0.3, p, jnp.exp(z) - 1.0)


def _flat(p):
    return 4 * p[0] + 2 * p[1] + p[2]


def _all_gather(arrs, name):
    n = len(arrs)

    def body(*refs):
        ins, outs = refs[:n], refs[n:2 * n]
        send_sems, recv_sems, local_sems = refs[2 * n:]
        x, y, c = lax.axis_index("x"), lax.axis_index("y"), lax.axis_index("c")
        me, sibling = (x, y, c), (x, y, 1 - c)
        chips = [(1 - x, y), (x, 1 - y), (1 - x, 1 - y)]

        def copy(a, k, block, to, src=None):
            dst = outs[a].at[_flat(block)]
            return pltpu.make_async_remote_copy(src_ref=dst if src is None else src, dst_ref=dst,
                                                send_sem=send_sems.at[a * 7 + k], recv_sem=recv_sems.at[a * 7 + k],
                                                device_id=to, device_id_type=MESH)

        mine = [pltpu.make_async_copy(ins[a], outs[a].at[_flat(me)], local_sems.at[a]) for a in range(n)]
        for cp in mine:
            cp.start()
        first = []
        for a in range(n):
            first += [copy(a, 1 + j, me, (*chip, c), src=ins[a]) for j, chip in enumerate(chips)]
            first.append(copy(a, 0, me, sibling, src=ins[a]))
        for cp in first:
            cp.start()
        passed = []
        for j, chip in enumerate(chips):
            for a in range(n):
                copy(a, 1 + j, (*chip, c), me).wait_recv()
                fw = copy(a, 4 + j, (*chip, c), sibling)
                fw.start()
                passed.append(fw)
        for a in range(n):
            copy(a, 0, sibling, me).wait_recv()
            for j, chip in enumerate(chips):
                copy(a, 4 + j, (*chip, 1 - c), me).wait_recv()
        for cp in first + passed:
            cp.wait_send()
        for cp in mine:
            cp.wait()

    any_spec = pl.BlockSpec(memory_space=pl.ANY)
    return pl.pallas_call(
        body, name=name,
        out_shape=[jax.ShapeDtypeStruct((N_DEV,) + a.shape, a.dtype) for a in arrs],
        in_specs=[any_spec] * n, out_specs=[any_spec] * n,
        scratch_shapes=[pltpu.SemaphoreType.DMA((7 * n,)), pltpu.SemaphoreType.DMA((7 * n,)), pltpu.SemaphoreType.DMA((n,))],
    )(*arrs)


def _chip_peers(x, y):
    return [(1 - x, y), (x, 1 - y), (1 - x, 1 - y)]


def _chip(p):
    return 2 * p[0] + p[1]


def _plan_gather(kinds):
    def plan(x, y, c):
        out = []
        for a, kind in enumerate(kinds):
            for peer in [(x, y, 1 - c)] + [(*ch, c) for ch in _chip_peers(x, y)]:
                out.append((a, None, (kind, _flat((x, y, c))), peer, (kind, _flat(peer))))
        return out
    return plan


def _slot(ref, where):
    kind, k = where
    if kind == "lead":
        return ref.at[k]
    return ref.at[:, pl.ds(pl.multiple_of(k * 256, 256), 256)]


def _plan_scatter(n):
    def plan(x, y, c):
        out = []
        for a in range(n):
            for ch in _chip_peers(x, y):
                out.append((a, _chip(ch), ("lead", _chip((x, y))), (*ch, c), ("lead", _chip(ch))))
        return out
    return plan


HBM_SPEC = pl.BlockSpec(memory_space=pltpu.HBM)
SEM_SPEC = pl.BlockSpec(memory_space=pltpu.SEMAPHORE)


def _in_hbm(a):
    return pltpu.with_memory_space_constraint(a, pltpu.HBM)


def _exchange_start(srcs, lands, plan, name):
    n = len(srcs)
    count = len(plan(0, 0, 0))

    def body(*refs):
        src_refs, land_refs = refs[:n], refs[n:2 * n]
        send_sems, recv_sems = refs[2 * n], refs[2 * n + 1]
        token = refs[-1]
        x, y, c = lax.axis_index("x"), lax.axis_index("y"), lax.axis_index("c")
        for k, (a, si, di, peer, _) in enumerate(plan(x, y, c)):
            src = src_refs[a] if si is None else src_refs[a].at[si]
            pltpu.make_async_remote_copy(src_ref=src, dst_ref=_slot(land_refs[a], di), send_sem=send_sems.at[k],
                                         recv_sem=recv_sems.at[k], device_id=peer, device_id_type=MESH).start()
        token[...] = jnp.zeros_like(token)

    out = pl.pallas_call(
        body, name=name,
        out_shape=(pltpu.SemaphoreType.DMA((count,)), pltpu.SemaphoreType.DMA((count,)),
                   *[pltpu.HBM(a.shape, a.dtype) for a in srcs], *[pltpu.HBM(a.shape, a.dtype) for a in lands],
                   jax.ShapeDtypeStruct((8, 128), F32)),
        in_specs=[HBM_SPEC] * (2 * n),
        out_specs=(SEM_SPEC, SEM_SPEC, *([HBM_SPEC] * (2 * n)), pl.BlockSpec(memory_space=pltpu.VMEM)),
        input_output_aliases={i: 2 + i for i in range(2 * n)},
        compiler_params=pltpu.CompilerParams(has_side_effects=pltpu.SideEffectType.DATAFLOW_SIDE_EFFECTING),
    )(*[_in_hbm(a) for a in srcs], *[_in_hbm(a) for a in lands])
    return out[0], out[1], list(out[2:2 + n]), list(out[2 + n:2 + 2 * n]), out[-1]


def _exchange_wait(send_sems, recv_sems, srcs, lands, plan, after, name):
    n = len(srcs)

    def body(*refs):
        src_refs, land_refs = refs[:n], refs[n:2 * n]
        send_sems, recv_sems = refs[2 * n], refs[2 * n + 1]
        x, y, c = lax.axis_index("x"), lax.axis_index("y"), lax.axis_index("c")
        for k, (a, si, _, peer, ri) in enumerate(plan(x, y, c)):
            src = src_refs[a] if si is None else src_refs[a].at[si]
            cp = pltpu.make_async_remote_copy(src_ref=src, dst_ref=_slot(land_refs[a], ri), send_sem=send_sems.at[k],
                                              recv_sem=recv_sems.at[k], device_id=peer, device_id_type=MESH)
            cp.wait_send()
            cp.wait_recv()

    out = pl.pallas_call(
        body, name=name,
        out_shape=(*[pltpu.HBM(a.shape, a.dtype) for a in srcs], *[pltpu.HBM(a.shape, a.dtype) for a in lands]),
        in_specs=[HBM_SPEC] * (2 * n) + [SEM_SPEC, SEM_SPEC, pl.BlockSpec(memory_space=pl.ANY)],
        out_specs=tuple([HBM_SPEC] * (2 * n)),
        input_output_aliases={i: i for i in range(2 * n)},
        compiler_params=pltpu.CompilerParams(has_side_effects=pltpu.SideEffectType.DATAFLOW_SIDE_EFFECTING),
    )(*srcs, *lands, send_sems, recv_sems, after)
    return list(out[n:2 * n])


def _forward_to_sibling(lands, kinds, name):
    n = len(lands)

    def body(*refs):
        in_refs, out_refs = refs[:n], refs[n:2 * n]
        send_sems, recv_sems = refs[2 * n:]
        x, y, c = lax.axis_index("x"), lax.axis_index("y"), lax.axis_index("c")
        sibling = (x, y, 1 - c)

        def copy(a, j, slot):
            return pltpu.make_async_remote_copy(src_ref=_slot(in_refs[a], (kinds[a], slot)), dst_ref=_slot(out_refs[a], (kinds[a], slot)),
                                                send_sem=send_sems.at[a * 3 + j], recv_sem=recv_sems.at[a * 3 + j],
                                                device_id=sibling, device_id_type=MESH)

        sends = [copy(a, j, _flat((*ch, c))) for a in range(n) for j, ch in enumerate(_chip_peers(x, y))]
        for cp in sends:
            cp.start()
        for a in range(n):
            for j, ch in enumerate(_chip_peers(x, y)):
                copy(a, j, _flat((*ch, 1 - c))).wait_recv()
        for cp in sends:
            cp.wait_send()

    any_spec = pl.BlockSpec(memory_space=pl.ANY)
    return pl.pallas_call(
        body, name=name, out_shape=[jax.ShapeDtypeStruct(a.shape, a.dtype) for a in lands],
        in_specs=[any_spec] * n, out_specs=[any_spec] * n, input_output_aliases={a: a for a in range(n)},
        scratch_shapes=[pltpu.SemaphoreType.DMA((3 * n,)), pltpu.SemaphoreType.DMA((3 * n,))],
    )(*lands)


def _swap_with_sibling(parts, name):
    n = len(parts)

    def body(*refs):
        in_refs, out_refs = refs[:n], refs[n:2 * n]
        send_sems, recv_sems = refs[2 * n:]
        x, y, c = lax.axis_index("x"), lax.axis_index("y"), lax.axis_index("c")
        sends = [pltpu.make_async_remote_copy(src_ref=in_refs[a].at[1 - c], dst_ref=out_refs[a], send_sem=send_sems.at[a],
                                              recv_sem=recv_sems.at[a], device_id=(x, y, 1 - c), device_id_type=MESH)
                 for a in range(n)]
        for cp in sends:
            cp.start()
        for cp in sends:
            cp.wait()

    any_spec = pl.BlockSpec(memory_space=pl.ANY)
    return pl.pallas_call(
        body, name=name, out_shape=[jax.ShapeDtypeStruct(a.shape[1:], a.dtype) for a in parts],
        in_specs=[any_spec] * n, out_specs=[any_spec] * n,
        scratch_shapes=[pltpu.SemaphoreType.DMA((n,)), pltpu.SemaphoreType.DMA((n,))],
    )(*parts)


def _pair_sum(parts, got, core, name):
    _, _, R, C = parts.shape
    tr = 256 if R % 256 == 0 else R

    def body(c_ref, p_ref, g_ref, o_ref):
        o_ref[...] = (p_ref[...].astype(F32) + g_ref[...].astype(F32)).astype(o_ref.dtype)

    return pl.pallas_call(
        body, name=name,
        grid_spec=pltpu.PrefetchScalarGridSpec(
            num_scalar_prefetch=1, grid=(4, R // tr),
            in_specs=[pl.BlockSpec((None, None, tr, C), lambda j, i, c_ref: (c_ref[0], j, i, 0)),
                      pl.BlockSpec((None, tr, C), lambda j, i, c_ref: (j, i, 0))],
            out_specs=pl.BlockSpec((None, tr, C), lambda j, i, c_ref: (j, i, 0))),
        out_shape=jax.ShapeDtypeStruct((4, R, C), parts.dtype),
        compiler_params=_params(("parallel", "parallel")),
    )(core, parts, got)


def _matmul(a, b, mode, M, N, K, tm, tn, tk, out_dtype, name, b_noff=0, a_moff=0, b_blocked=False, out_blocked=None, dep=None,
            vmem_mib=48):
    nm, nn, nk = M // tm, N // tn, K // tk
    if mode == "nn":
        a_spec = pl.BlockSpec((tm, tk), lambda j, i, k: (i, k))
        b_spec = pl.BlockSpec((tk, tn), lambda j, i, k: (k, j + b_noff))
        dn = NN
    elif mode == "nt":
        a_spec = pl.BlockSpec((tm, tk), lambda j, i, k: (i, k))
        if b_blocked:
            b_spec = pl.BlockSpec((None, tn, tk), lambda j, i, k: (k, j, 0))
        else:
            b_spec = pl.BlockSpec((tn, tk), lambda j, i, k: (j + b_noff, k))
        dn = NT
    else:
        a_spec = pl.BlockSpec((tk, tm), lambda j, i, k: (k, i + a_moff))
        b_spec = pl.BlockSpec((tk, tn), lambda j, i, k: (k, j + b_noff))
        dn = TN
    if out_blocked == "col":
        out_shape = jax.ShapeDtypeStruct((2, 4, M, tn), out_dtype)
        out_spec = pl.BlockSpec((None, None, tm, tn), lambda j, i, k: (j % 2, j // 2, i, 0))
    elif out_blocked == "row":
        out_shape = jax.ShapeDtypeStruct((2, 4, tm, N), out_dtype)
        out_spec = pl.BlockSpec((None, None, tm, tn), lambda j, i, k: (i % 2, i // 2, 0, j))
    elif out_blocked == "third":
        out_shape = jax.ShapeDtypeStruct((3, M, N // 3), out_dtype)
        out_spec = pl.BlockSpec((None, tm, tn), lambda j, i, k: (j // (nn // 3), i, j % (nn // 3)))
    else:
        out_shape = jax.ShapeDtypeStruct((M, N), out_dtype)
        out_spec = pl.BlockSpec((tm, tn), lambda j, i, k: (i, j))

    def body(a_ref, b_ref, *rest):
        o_ref, scratch = (rest[1], rest[2:]) if dep is not None else (rest[0], rest[1:])
        if nk == 1:
            o_ref[...] = _dot(a_ref[...], b_ref[...], dn).astype(out_dtype)
        else:
            acc_ref, = scratch
            k = pl.program_id(2)

            @pl.when(k == 0)
            def _():
                acc_ref[...] = jnp.zeros_like(acc_ref)

            acc_ref[...] += _dot(a_ref[...], b_ref[...], dn)

            @pl.when(k == nk - 1)
            def _():
                o_ref[...] = acc_ref[...].astype(out_dtype)

    return pl.pallas_call(
        body, name=name, grid=(nn, nm, nk),
        in_specs=[a_spec, b_spec] + ([] if dep is None else [pl.BlockSpec((8, 128), lambda j, i, k: (0, 0))]),
        out_specs=out_spec, out_shape=out_shape,
        scratch_shapes=[] if nk == 1 else [pltpu.VMEM((tm, tn), F32)],
        compiler_params=_params(("parallel", "parallel", "arbitrary"), vmem_mib),
    )(a, b, *([] if dep is None else [dep]))


def _rms_fwd(x, g, name):
    R, Dm = x.shape
    tr = min(R, 256)

    def body(x_ref, g_ref, h_ref):
        xv = x_ref[...]
        r = lax.rsqrt(jnp.mean(xv * xv, axis=-1, keepdims=True) + EPS)
        h_ref[...] = (xv * r * g_ref[...]).astype(BF)

    return pl.pallas_call(
        body, name=name, grid=(R // tr,),
        in_specs=[pl.BlockSpec((tr, Dm), lambda i: (i, 0)), pl.BlockSpec((1, Dm), lambda i: (0, 0))],
        out_specs=pl.BlockSpec((tr, Dm), lambda i: (i, 0)), out_shape=jax.ShapeDtypeStruct((R, Dm), BF),
        compiler_params=_params(("parallel",)),
    )(x, g)


def _rms_gain_grad(dn, x, name):
    R, Dm = x.shape

    def body(dn_ref, x_ref, o_ref):
        xv = x_ref[...]
        r = lax.rsqrt(jnp.mean(xv * xv, axis=-1, keepdims=True) + EPS)
        o_ref[...] = jnp.sum(dn_ref[...] * xv * r, axis=0, keepdims=True)

    return pl.pallas_call(
        body, name=name, out_shape=jax.ShapeDtypeStruct((1, Dm), F32),
        compiler_params=pltpu.CompilerParams(vmem_limit_bytes=32 * MIB),
    )(dn, x)


def _shift_down(v, k, head8, row, T):
    if k == 0:
        return v
    r = pltpu.roll(v, k, 0)
    hr = pltpu.roll(head8, k, 0)
    top = jnp.where(row[:8] < k, hr, r[:8])
    return jnp.concatenate([top, r[8:]], axis=0)


def _shift_up(v, k, tail8, row, T):
    if k == 0:
        return v
    r = pltpu.roll(v, T - k, 0)
    tr = pltpu.roll(tail8, 8 - k, 0)
    bot = jnp.where(row[:8] >= 8 - k, tr, r[T - 8:])
    return jnp.concatenate([r[:T - 8], bot], axis=0)


def _rglru_gates(u, head8, grow, row, T, cw_ref, cb_ref, wa_ref, ba_ref, wx_ref, bx_ref, lam_ref):
    us = [_shift_down(u, k, head8, row, T) for k in range(CONV_W)]
    acc = us[0] * cw_ref[0:1, :]
    for k in range(1, CONV_W):
        acc = acc + us[k] * cw_ref[k:k + 1, :]
    conv = cb_ref[...] + acc
    cbf = conv.astype(BF)
    r_ = _sigmoid(_dot(cbf, wa_ref[0], NN) + ba_ref[...])
    i_ = _sigmoid(_dot(cbf, wx_ref[0], NN) + bx_ref[...])
    sp = _softplus(-lam_ref[...])
    la = -LRU_C * r_ * sp
    a = jnp.exp(la)
    mult_raw = jnp.sqrt(-_expm1(2.0 * la))
    mult = jnp.where(grow == 0, 1.0, mult_raw)
    return us, conv, cbf, r_, i_, sp, a, mult_raw, mult


def _rglru_specs(T, nt, rev):
    tmap = (lambda n, t: (nt - 1 - t, n)) if rev else (lambda n, t: (t, n))
    hmap = ((lambda n, t: (jnp.maximum((nt - 1 - t) * (T // 8) - 1, 0), n)) if rev
            else (lambda n, t: (jnp.maximum(t * (T // 8) - 1, 0), n)))
    tile = pl.BlockSpec((T, RNN_BLOCK), tmap)
    halo = pl.BlockSpec((8, RNN_BLOCK), hmap)
    vec = pl.BlockSpec((1, RNN_BLOCK), lambda n, t: (0, n))
    cw = pl.BlockSpec((CONV_W, RNN_BLOCK), lambda n, t: (0, n))
    wblk = pl.BlockSpec((1, RNN_BLOCK, RNN_BLOCK), lambda n, t: (n, 0, 0))
    return tile, halo, vec, cw, wblk


def _rglru_fwd(xr, g, cw, cb, wa, ba, wx, bx, lam, T):
    S = xr.shape[0]
    nt = S // T

    def body(u_ref, uh_ref, g_ref, cw_ref, cb_ref, wa_ref, ba_ref, wx_ref, bx_ref, lam_ref, h_ref, y_ref, carry):
        t = pl.program_id(1)

        @pl.when(t == 0)
        def _():
            carry[...] = jnp.zeros_like(carry)

        row = lax.broadcasted_iota(jnp.int32, (T, RNN_BLOCK), 0)
        grow = row + t * T
        head8 = jnp.where(t > 0, uh_ref[...], 0.0)
        _, conv, _, _, i_, _, a, _, mult = _rglru_gates(u_ref[...], head8, grow, row, T, cw_ref, cb_ref, wa_ref, ba_ref,
                                                         wx_ref, bx_ref, lam_ref)
        b = mult * i_ * conv
        s = 1
        while s < T:
            keep = row >= s
            a_s = jnp.where(keep, pltpu.roll(a, s, 0), 1.0)
            b_s = jnp.where(keep, pltpu.roll(b, s, 0), 0.0)
            b = a * b_s + b
            a = a * a_s
            s *= 2
        h = b + a * carry[0:1, :]
        carry[...] = jnp.broadcast_to(h[T - 1:T, :], carry.shape)
        h_ref[...] = h
        gv = g_ref[...]
        y_ref[...] = (h * (gv * _sigmoid(gv))).astype(BF)

    tile, halo, vec, cwspec, wblk = _rglru_specs(T, nt, False)
    return pl.pallas_call(
        body, name="rglru_fwd", grid=(RNN_BLOCKS, nt),
        in_specs=[tile, halo, tile, cwspec, vec, wblk, vec, wblk, vec, vec],
        out_specs=[tile, tile],
        out_shape=[jax.ShapeDtypeStruct((S, D_RNN), F32), jax.ShapeDtypeStruct((S, D_RNN), BF)],
        scratch_shapes=[pltpu.VMEM((8, RNN_BLOCK), F32)],
        compiler_params=_params(("parallel", "arbitrary")),
    )(xr, xr, g, cw, cb, wa, ba, wx, bx, lam)


def _rglru_bwd(xr, g, h, dy, cw, cb, wa, ba, wx, bx, lam, T):
    S = xr.shape[0]
    nt = S // T

    def body(u_ref, uh_ref, g_ref, h_ref, hh_ref, dy_ref, cw_ref, cb_ref, wa_ref, ba_ref, wx_ref, bx_ref, lam_ref,
             du_ref, dg_ref, dwa_ref, dwx_ref, dvec_ref, c_dhh, c_a, c_dconv):
        t = pl.program_id(1)
        tt = nt - 1 - t

        @pl.when(t == 0)
        def _():
            c_dhh[...] = jnp.zeros_like(c_dhh)
            c_a[...] = jnp.zeros_like(c_a)
            c_dconv[...] = jnp.zeros_like(c_dconv)
            dwa_ref[...] = jnp.zeros_like(dwa_ref)
            dwx_ref[...] = jnp.zeros_like(dwx_ref)
            dvec_ref[...] = jnp.zeros_like(dvec_ref)

        row = lax.broadcasted_iota(jnp.int32, (T, RNN_BLOCK), 0)
        row8 = row[:8]
        grow = row + tt * T
        head8 = jnp.where(tt > 0, uh_ref[...], 0.0)
        us, conv, cbf, r_, i_, sp, a, mult_raw, mult = _rglru_gates(
            u_ref[...], head8, grow, row, T, cw_ref, cb_ref, wa_ref, ba_ref, wx_ref, bx_ref, lam_ref)
        hv = h_ref[...]
        hprev = _shift_down(hv, 1, jnp.where(tt > 0, hh_ref[...], 0.0), row, T)
        gv = g_ref[...]
        sg = _sigmoid(gv)
        dyv = dy_ref[...]
        dg_ref[...] = (dyv * hv * (sg * (1.0 + gv * (1.0 - sg)))).astype(BF)
        d = dyv * (gv * sg)
        A = _shift_up(a, 1, c_a[...], row, T)
        s = 1
        while s < T:
            keep = row < T - s
            A_s = jnp.where(keep, pltpu.roll(A, T - s, 0), 1.0)
            d_s = jnp.where(keep, pltpu.roll(d, T - s, 0), 0.0)
            d = A * d_s + d
            A = A * A_s
            s *= 2
        dhh = d + A * c_dhh[0:1, :]
        da = dhh * hprev
        dconv = dhh * mult * i_
        di = dhh * mult * conv
        dmult = dhh * i_ * conv
        dla = da * a - jnp.where(grow == 0, 0.0, dmult * (a * a) / mult_raw)
        dr = dla * (-LRU_C * sp)
        dsp = jnp.sum(dla * (-LRU_C * r_), axis=0, keepdims=True)
        dza = dr * r_ * (1.0 - r_)
        dzx = di * i_ * (1.0 - i_)
        dza_b, dzx_b = dza.astype(BF), dzx.astype(BF)
        dconv = dconv + _dot(dza_b, wa_ref[0], NT) + _dot(dzx_b, wx_ref[0], NT)
        dwa_ref[0] += _dot(cbf, dza_b, TN)
        dwx_ref[0] += _dot(cbf, dzx_b, TN)
        lam = lam_ref[...]
        rows = [jnp.sum(dconv * us[k], axis=0, keepdims=True) for k in range(CONV_W)]
        rows += [jnp.sum(dconv, axis=0, keepdims=True), jnp.sum(dza, axis=0, keepdims=True),
                 jnp.sum(dzx, axis=0, keepdims=True), dsp * (-_sigmoid(-lam))]
        upd = jnp.zeros((8, RNN_BLOCK), F32)
        for j, rv in enumerate(rows):
            upd = upd + jnp.where(row8 == j, rv, 0.0)
        dvec_ref[...] += upd
        tail8 = c_dconv[...]
        du = dconv * cw_ref[0:1, :]
        for k in range(1, CONV_W):
            du = du + _shift_up(dconv, k, tail8, row, T) * cw_ref[k:k + 1, :]
        du_ref[...] = du.astype(BF)
        c_dhh[...] = jnp.broadcast_to(dhh[0:1, :], c_dhh.shape)
        c_a[...] = jnp.broadcast_to(a[0:1, :], c_a.shape)
        c_dconv[...] = dconv[:8]

    tile, halo, vec, cwspec, wblk = _rglru_specs(T, nt, True)
    acc8 = pl.BlockSpec((8, RNN_BLOCK), lambda n, t: (0, n))
    return pl.pallas_call(
        body, name="rglru_bwd", grid=(RNN_BLOCKS, nt),
        in_specs=[tile, halo, tile, tile, halo, tile, cwspec, vec, wblk, vec, wblk, vec, vec],
        out_specs=[tile, tile, wblk, wblk, acc8],
        out_shape=[jax.ShapeDtypeStruct((S, D_RNN), BF), jax.ShapeDtypeStruct((S, D_RNN), BF),
                   jax.ShapeDtypeStruct((RNN_BLOCKS, RNN_BLOCK, RNN_BLOCK), F32),
                   jax.ShapeDtypeStruct((RNN_BLOCKS, RNN_BLOCK, RNN_BLOCK), F32),
                   jax.ShapeDtypeStruct((8, D_RNN), F32)],
        scratch_shapes=[pltpu.VMEM((8, RNN_BLOCK), F32)] * 3,
        compiler_params=_params(("parallel", "arbitrary")),
    )(xr, xr, g, h, h, dy, cw, cb, wa, ba, wx, bx, lam)


def _rel_bucket_map():
    qi = np.arange(WINDOW)[:, None]
    kj = np.arange(2 * WINDOW)[None, :]
    dist = jnp.asarray(qi + WINDOW - kj, jnp.int32)
    n = jnp.maximum(dist, 0)
    max_exact = REL_BUCKETS // 2
    ratio = jnp.log(jnp.maximum(n, 1).astype(F32) / max_exact) / math.log(REL_MAX_DIST / max_exact)
    large = jnp.minimum(max_exact + (ratio * (REL_BUCKETS - max_exact)).astype(jnp.int32), REL_BUCKETS - 1)
    bucket = jnp.where(n < max_exact, n, large).astype(jnp.int32)
    j = np.arange(WINDOW)[None, :]
    return jnp.where(jnp.asarray(j > qi), bucket[:, :WINDOW], bucket[:, WINDOW:])


def _swa_common(n, kv_ref, bucket_ref, relb_ref, bias_scr):
    @pl.when(n == 0)
    def _():
        bk = bucket_ref[...]
        for h in range(SWA_HEADS):
            acc = jnp.zeros((WINDOW, WINDOW), F32)
            for b in range(REL_BUCKETS):
                acc = acc + jnp.where(bk == b, relb_ref[b, h], 0.0)
            bias_scr[h] = acc

    prev0 = pl.multiple_of(jnp.maximum(n - 1, 0) * WINDOW, WINDOW)
    cur0 = pl.multiple_of(n * WINDOW, WINDOW)
    kk = jnp.concatenate([kv_ref[pl.ds(prev0, WINDOW), :], kv_ref[pl.ds(cur0, WINDOW), :]], axis=0).astype(F32)
    rowi = lax.broadcasted_iota(jnp.int32, (WINDOW, WINDOW), 0)
    col = lax.broadcasted_iota(jnp.int32, (WINDOW, WINDOW), 1)
    from_prev = col > rowi
    return kk, from_prev, prev0, cur0


def _fold(full, from_prev):
    return jnp.where(from_prev, full[:, :WINDOW], full[:, WINDOW:])


def _unfold(sq, from_prev):
    return jnp.concatenate([jnp.where(from_prev, sq, 0.0), jnp.where(from_prev, 0.0, sq)], axis=1)


def _half_pair(part, kvh):
    lo = lax.broadcasted_iota(jnp.int32, part.shape, 1) < SWA_HD
    if kvh == 0:
        pa = jnp.where(lo, part, 0.0)
        pb = pltpu.roll(pa, SWA_HD, 1)
    else:
        pb = jnp.where(lo, 0.0, part)
        pa = pltpu.roll(pb, SWA_HD, 1)
    return pa.astype(BF), pb.astype(BF)


ALL_HEADS = SWA_HEADS * WINDOW


def _sink_column(sinks):
    return jnp.repeat(sinks.reshape(SWA_HEADS), WINDOW).reshape(ALL_HEADS, 1)


def _swa_operands(kk):
    return [(_half_pair(kk[:, :128], kvh), _half_pair(kk[:, 128:], kvh)) for kvh in range(SWA_KV_HEADS)]


def _swa_probs(n, q_ref, ops, bias_scr, sinkc_ref, from_prev):
    lgs = []
    for kvh in range(SWA_KV_HEADS):
        (ka, kb), _ = ops[kvh]
        for p in range(4):
            q2 = q_ref[:, kvh * 512 + p * 128:kvh * 512 + p * 128 + 128]
            lgs += [_fold(_dot(q2, ka, NT), from_prev), _fold(_dot(q2, kb, NT), from_prev)]
    lg = jnp.concatenate(lgs, axis=0) * (SWA_HD ** -0.5) + bias_scr[...].reshape(ALL_HEADS, WINDOW)
    rowi = jnp.bitwise_and(lax.broadcasted_iota(jnp.int32, (ALL_HEADS, WINDOW), 0), WINDOW - 1)
    col = lax.broadcasted_iota(jnp.int32, (ALL_HEADS, WINDOW), 1)
    no_prev = jnp.where(n > 0, 0, 4 * WINDOW)
    lg = jnp.where(jnp.logical_or(col <= rowi, col > rowi + no_prev), lg, NEG_INF)
    sink = sinkc_ref[...]
    m = jnp.maximum(jnp.max(lg, axis=-1, keepdims=True), sink)
    e = jnp.exp(lg - m)
    es = jnp.exp(sink - m)
    den = jnp.sum(e, axis=-1, keepdims=True) + es
    return e / den, es / den


def _swa_fwd(q, kv, g, bucket, rel_bias, sink_col):
    S = q.shape[0]
    nb = S // WINDOW

    def body(q_ref, kv_ref, g_ref, bucket_ref, relb_ref, sinkc_ref, o_ref, y_ref, bias_scr):
        n = pl.program_id(0)
        kk, from_prev, _, _ = _swa_common(n, kv_ref, bucket_ref, relb_ref, bias_scr)
        ops = _swa_operands(kk)
        pr, _ = _swa_probs(n, q_ref, ops, bias_scr, sinkc_ref, from_prev)
        for kvh in range(SWA_KV_HEADS):
            _, (va, vb) = ops[kvh]
            for p in range(4):
                c0 = kvh * 512 + p * 128
                r0 = (kvh * 8 + 2 * p) * WINDOW
                o2 = (_dot(_unfold(pr[r0:r0 + WINDOW], from_prev).astype(BF), va, NN)
                      + _dot(_unfold(pr[r0 + WINDOW:r0 + 2 * WINDOW], from_prev).astype(BF), vb, NN))
                o_ref[:, c0:c0 + 128] = o2
                gv = g_ref[:, c0:c0 + 128]
                y_ref[:, c0:c0 + 128] = (o2 * (gv * _sigmoid(gv))).astype(BF)

    blk = pl.BlockSpec((WINDOW, 1024), lambda n: (n, 0))
    smem = pl.BlockSpec(memory_space=pltpu.SMEM)
    sinkc = pl.BlockSpec((ALL_HEADS, 1), lambda n: (0, 0))
    return pl.pallas_call(
        body, name="swa_fwd", grid=(nb,),
        in_specs=[blk, pl.BlockSpec((S, 256), lambda n: (0, 0)), blk, pl.BlockSpec((WINDOW, WINDOW), lambda n: (0, 0)), smem, sinkc],
        out_specs=[blk, blk],
        out_shape=[jax.ShapeDtypeStruct((S, 1024), F32), jax.ShapeDtypeStruct((S, 1024), BF)],
        scratch_shapes=[pltpu.VMEM((SWA_HEADS, WINDOW, WINDOW), F32)],
        compiler_params=_params(("arbitrary",)),
    )(q, kv, g, bucket, rel_bias, sink_col)


def _swa_bwd(q, kv, g, o, dy, bucket, rel_bias, sink_col):
    S = q.shape[0]
    nb = S // WINDOW

    def body(q_ref, kv_ref, g_ref, o_ref, dy_ref, bucket_ref, relb_ref, sinkc_ref,
             dq_ref, dg_ref, dkv_ref, dsink_ref, drel_ref, bias_scr, dbias_scr, dsink_scr):
        n = pl.program_id(0)

        @pl.when(n == 0)
        def _():
            dbias_scr[...] = jnp.zeros_like(dbias_scr)
            dsink_scr[...] = jnp.zeros_like(dsink_scr)
            dkv_ref[...] = jnp.zeros_like(dkv_ref)

        kk, from_prev, prev0, cur0 = _swa_common(n, kv_ref, bucket_ref, relb_ref, bias_scr)
        ops = _swa_operands(kk)
        pr, ps = _swa_probs(n, q_ref, ops, bias_scr, sinkc_ref, from_prev)
        do2s, dps = [], []
        for kvh in range(SWA_KV_HEADS):
            _, (va, vb) = ops[kvh]
            for p in range(4):
                c0 = kvh * 512 + p * 128
                gv = g_ref[:, c0:c0 + 128]
                sg = _sigmoid(gv)
                dyv = dy_ref[:, c0:c0 + 128]
                dg_ref[:, c0:c0 + 128] = (dyv * o_ref[:, c0:c0 + 128] * (sg * (1.0 + gv * (1.0 - sg)))).astype(BF)
                do2 = (dyv * (gv * sg)).astype(BF)
                do2s.append(do2)
                dps += [_fold(_dot(do2, va, NT), from_prev), _fold(_dot(do2, vb, NT), from_prev)]
        dp = jnp.concatenate(dps, axis=0)
        delta = jnp.sum(pr * dp, axis=-1, keepdims=True)
        ds = pr * (dp - delta)
        dbias_scr[...] += ds.reshape(SWA_HEADS, WINDOW, WINDOW)
        dsink_scr[...] += ps * delta
        dsc = ds * (SWA_HD ** -0.5)
        lo256 = lax.broadcasted_iota(jnp.int32, (2 * WINDOW, 128), 1) < SWA_HD
        dks, dvs = [], []
        for kvh in range(SWA_KV_HEADS):
            (ka, kb), _ = ops[kvh]
            dka = jnp.zeros((2 * WINDOW, 128), F32)
            dkb, dva, dvb = dka, dka, dka
            for p in range(4):
                c0 = kvh * 512 + p * 128
                r0 = (kvh * 8 + 2 * p) * WINDOW
                q2 = q_ref[:, c0:c0 + 128]
                do2 = do2s[kvh * 4 + p]
                ds0 = _unfold(dsc[r0:r0 + WINDOW], from_prev).astype(BF)
                ds1 = _unfold(dsc[r0 + WINDOW:r0 + 2 * WINDOW], from_prev).astype(BF)
                dq_ref[:, c0:c0 + 128] = (_dot(ds0, ka, NN) + _dot(ds1, kb, NN)).astype(BF)
                dka = dka + _dot(ds0, q2, TN)
                dkb = dkb + _dot(ds1, q2, TN)
                dva = dva + _dot(_unfold(pr[r0:r0 + WINDOW], from_prev).astype(BF), do2, TN)
                dvb = dvb + _dot(_unfold(pr[r0 + WINDOW:r0 + 2 * WINDOW], from_prev).astype(BF), do2, TN)
            dks.append(jnp.where(lo256, dka, 0.0) + pltpu.roll(jnp.where(lo256, 0.0, dkb), SWA_HD, 1))
            dvs.append(jnp.where(lo256, dva, 0.0) + pltpu.roll(jnp.where(lo256, 0.0, dvb), SWA_HD, 1))
        dk = dks[0] + pltpu.roll(dks[1], SWA_HD, 1)
        dv = dvs[0] + pltpu.roll(dvs[1], SWA_HD, 1)
        dkv_ref[pl.ds(prev0, WINDOW), 0:128] += dk[:WINDOW]
        dkv_ref[pl.ds(prev0, WINDOW), 128:256] += dv[:WINDOW]
        dkv_ref[pl.ds(cur0, WINDOW), 0:128] += dk[WINDOW:]
        dkv_ref[pl.ds(cur0, WINDOW), 128:256] += dv[WINDOW:]

        @pl.when(n == nb - 1)
        def _():
            dsink_ref[...] = -jnp.sum(dsink_scr[...].reshape(SWA_HEADS, WINDOW, 1), axis=1)
            bk = bucket_ref[...]
            sums = []
            for b in range(REL_BUCKETS):
                sums.append(jnp.sum(jnp.where((bk == b)[None], dbias_scr[...], 0.0), axis=1))
            drel_ref[...] = jnp.sum(jnp.concatenate(sums, axis=0), axis=1, keepdims=True)

    blk = pl.BlockSpec((WINDOW, 1024), lambda n: (n, 0))
    smem = pl.BlockSpec(memory_space=pltpu.SMEM)
    whole = lambda shape: pl.BlockSpec(shape, lambda n: (0, 0))
    return pl.pallas_call(
        body, name="swa_bwd", grid=(nb,),
        in_specs=[blk, whole((S, 256)), blk, blk, blk, whole((WINDOW, WINDOW)), smem, whole((ALL_HEADS, 1))],
        out_specs=[blk, blk, whole((S, 256)), whole((SWA_HEADS, 1)), whole((REL_BUCKETS * SWA_HEADS, 1))],
        out_shape=[jax.ShapeDtypeStruct((S, 1024), BF), jax.ShapeDtypeStruct((S, 1024), BF),
                   jax.ShapeDtypeStruct((S, 256), F32), jax.ShapeDtypeStruct((SWA_HEADS, 1), F32),
                   jax.ShapeDtypeStruct((REL_BUCKETS * SWA_HEADS, 1), F32)],
        scratch_shapes=[pltpu.VMEM((SWA_HEADS, WINDOW, WINDOW), F32), pltpu.VMEM((SWA_HEADS, WINDOW, WINDOW), F32),
                        pltpu.VMEM((ALL_HEADS, 1), F32)],
        compiler_params=_params(("arbitrary",)),
    )(q, kv, g, o, dy, bucket, rel_bias, sink_col)


def _mem_probs(qh, mk):
    lg = _dot(qh, mk, NT) * (MEM_HD ** -0.5)
    e = jnp.exp(lg - jnp.max(lg, axis=-1, keepdims=True))
    return e / jnp.sum(e, axis=-1, keepdims=True)


def _mem_fwd(q, mkv, g):
    S = q.shape[0]
    M = mkv.shape[0]
    tq = 256

    def body(q_ref, mkv_ref, g_ref, o_ref, y_ref):
        for h in range(MEM_HEADS):
            c0 = h * MEM_HD
            pr = _mem_probs(q_ref[:, c0:c0 + MEM_HD], mkv_ref[:, c0:c0 + MEM_HD])
            o = _dot(pr.astype(BF), mkv_ref[:, D_MEM + c0:D_MEM + c0 + MEM_HD], NN)
            o_ref[:, c0:c0 + MEM_HD] = o
            gv = g_ref[:, c0:c0 + MEM_HD]
            y_ref[:, c0:c0 + MEM_HD] = (o * (gv * _sigmoid(gv))).astype(BF)

    blk = pl.BlockSpec((tq, D_MEM), lambda i: (i, 0))
    return pl.pallas_call(
        body, name="mem_fwd", grid=(S // tq,),
        in_specs=[blk, pl.BlockSpec((M, 2 * D_MEM), lambda i: (0, 0)), blk], out_specs=[blk, blk],
        out_shape=[jax.ShapeDtypeStruct((S, D_MEM), F32), jax.ShapeDtypeStruct((S, D_MEM), BF)],
        compiler_params=_params(("parallel",)),
    )(q, mkv, g)


def _mem_bwd(q, mkv, g, o, dy):
    S = q.shape[0]
    M = mkv.shape[0]
    tq = 256

    def body(q_ref, mkv_ref, g_ref, o_ref, dy_ref, dq_ref, dg_ref, dmkv_ref):
        @pl.when(pl.program_id(0) == 0)
        def _():
            dmkv_ref[...] = jnp.zeros_like(dmkv_ref)

        for h in range(MEM_HEADS):
            c0 = h * MEM_HD
            qh = q_ref[:, c0:c0 + MEM_HD]
            mk = mkv_ref[:, c0:c0 + MEM_HD]
            mv = mkv_ref[:, D_MEM + c0:D_MEM + c0 + MEM_HD]
            gv = g_ref[:, c0:c0 + MEM_HD]
            sg = _sigmoid(gv)
            dyv = dy_ref[:, c0:c0 + MEM_HD]
            dg_ref[:, c0:c0 + MEM_HD] = (dyv * o_ref[:, c0:c0 + MEM_HD] * (sg * (1.0 + gv * (1.0 - sg)))).astype(BF)
            do = (dyv * (gv * sg)).astype(BF)
            pr = _mem_probs(qh, mk)
            dp = _dot(do, mv, NT)
            ds = pr * (dp - jnp.sum(pr * dp, axis=-1, keepdims=True))
            dsb = (ds * (MEM_HD ** -0.5)).astype(BF)
            dq_ref[:, c0:c0 + MEM_HD] = _dot(dsb, mk, NN).astype(BF)
            dmkv_ref[:, c0:c0 + MEM_HD] += _dot(dsb, qh, TN)
            dmkv_ref[:, D_MEM + c0:D_MEM + c0 + MEM_HD] += _dot(pr.astype(BF), do, TN)

    blk = pl.BlockSpec((tq, D_MEM), lambda i: (i, 0))
    whole = pl.BlockSpec((M, 2 * D_MEM), lambda i: (0, 0))
    return pl.pallas_call(
        body, name="mem_bwd", grid=(S // tq,),
        in_specs=[blk, whole, blk, blk, blk], out_specs=[blk, blk, whole],
        out_shape=[jax.ShapeDtypeStruct((S, D_MEM), BF), jax.ShapeDtypeStruct((S, D_MEM), BF),
                   jax.ShapeDtypeStruct((M, 2 * D_MEM), F32)],
        compiler_params=_params(("arbitrary",)),
    )(q, mkv, g, o, dy)


MERGE_TN = 512


def _merge_specs(tm):
    ytile = pl.BlockSpec((tm, 1024), lambda i, j: (i, 0))
    wblk = pl.BlockSpec((1024, MERGE_TN), lambda i, j: (0, j))
    gls = [pl.BlockSpec((None, tm, MERGE_TN), (lambda i, j, br=br: (br, i, j))) for br in range(3)]
    otile = pl.BlockSpec((tm, MERGE_TN), lambda i, j: (i, j))
    return ytile, wblk, gls, otile


def _merge_fwd(ys, ws, gl, tm):
    S = gl.shape[1]

    def body(y0, y1, y2, w0, w1, w2, g0, g1, g2, o_ref):
        acc = None
        for y_ref, w_ref, g_ref in ((y0, w0, g0), (y1, w1, g1), (y2, w2, g2)):
            term = _sigmoid(g_ref[...]) * _dot(y_ref[...], w_ref[...], NN)
            acc = term if acc is None else acc + term
        o_ref[...] = acc.astype(BF)

    ytile, wblk, gls, otile = _merge_specs(tm)
    return pl.pallas_call(
        body, name="merge_fwd", grid=(S // tm, D_MODEL // MERGE_TN),
        in_specs=[ytile] * 3 + [wblk] * 3 + gls, out_specs=otile,
        out_shape=jax.ShapeDtypeStruct((S, D_MODEL), BF),
        compiler_params=_params(("parallel", "arbitrary")),
    )(*ys, *ws, gl, gl, gl)


def _merge_bwd(dout, w_out, ys, ws, gl, tm):
    S = gl.shape[1]

    def body(do_ref, wo_ref, y0, y1, y2, w0, w1, w2, g0, g1, g2, dg0, dg1, dg2, dp0, dp1, dp2):
        dm = _dot(do_ref[...], wo_ref[...], NT)
        for y_ref, w_ref, g_ref, dg_ref, dp_ref in ((y0, w0, g0, dg0, dp0), (y1, w1, g1, dg1, dp1), (y2, w2, g2, dg2, dp2)):
            gate = _sigmoid(g_ref[...])
            pv = _dot(y_ref[...], w_ref[...], NN)
            dg_ref[...] = (dm * pv * gate * (1.0 - gate)).astype(BF)
            dp_ref[...] = (dm * gate).astype(BF)

    ytile, wblk, gls, otile = _merge_specs(tm)
    out = jax.ShapeDtypeStruct((S, D_MODEL), BF)
    return pl.pallas_call(
        body, name="merge_bwd", grid=(S // tm, D_MODEL // MERGE_TN),
        in_specs=[pl.BlockSpec((tm, D_MODEL), lambda i, j: (i, 0)), pl.BlockSpec((MERGE_TN, D_MODEL), lambda i, j: (j, 0))]
        + [ytile] * 3 + [wblk] * 3 + gls,
        out_specs=[otile] * 6, out_shape=[out] * 6,
        compiler_params=_params(("parallel", "arbitrary")),
    )(dout, w_out, *ys, *ws, gl, gl, gl)


def _out_loss(merged, w_out, x, target, post_g, tm):
    S = x.shape[0]

    def body(m_ref, w_ref, x_ref, t_ref, g_ref, dout_ref, dy_ref, loss_ref, dpost_ref):
        @pl.when(pl.program_id(0) == 0)
        def _():
            loss_ref[...] = jnp.zeros_like(loss_ref)
            dpost_ref[...] = jnp.zeros_like(dpost_ref)

        out = _dot(m_ref[...], w_ref[...], NN)
        r = lax.rsqrt(jnp.mean(out * out, axis=-1, keepdims=True) + EPS)
        nrm = out * r
        gv = g_ref[...]
        err = (x_ref[...] + nrm * gv) - t_ref[...]
        sq = jnp.sum(jnp.sum(err * err, axis=1, keepdims=True), axis=0, keepdims=True)
        loss_ref[...] += sq * (0.5 / D_MODEL)
        dy = err * (1.0 / D_MODEL)
        dy_ref[...] = dy
        dpost_ref[...] += jnp.sum(dy * nrm, axis=0, keepdims=True)
        dn = dy * gv
        dout_ref[...] = (r * (dn - nrm * jnp.mean(dn * nrm, axis=-1, keepdims=True))).astype(BF)

    row = pl.BlockSpec((tm, D_MODEL), lambda i: (i, 0))
    return pl.pallas_call(
        body, name="out_loss", grid=(S // tm,),
        in_specs=[row, pl.BlockSpec((D_MODEL, D_MODEL), lambda i: (0, 0)), row, row, pl.BlockSpec((1, D_MODEL), lambda i: (0, 0))],
        out_specs=[row, row, pl.BlockSpec((8, 128), lambda i: (0, 0)), pl.BlockSpec((1, D_MODEL), lambda i: (0, 0))],
        out_shape=[jax.ShapeDtypeStruct((S, D_MODEL), BF), jax.ShapeDtypeStruct((S, D_MODEL), F32),
                   jax.ShapeDtypeStruct((8, 128), F32), jax.ShapeDtypeStruct((1, D_MODEL), F32)],
        compiler_params=_params(("arbitrary",)),
    )(merged, w_out, x, target, post_g)


def _dh_dx(dproj, w_in, x, dy, pre_g, tm, tk):
    S = x.shape[0]
    nk = D_IN // tk

    def body(dp_ref, w_ref, x_ref, dy_ref, g_ref, dx_ref, dpre_ref, acc_ref):
        i, k = pl.program_id(0), pl.program_id(1)

        @pl.when(jnp.logical_and(i == 0, k == 0))
        def _():
            dpre_ref[...] = jnp.zeros_like(dpre_ref)

        @pl.when(k == 0)
        def _():
            acc_ref[...] = jnp.zeros_like(acc_ref)

        acc_ref[...] += _dot(dp_ref[...], w_ref[...], NT)

        @pl.when(k == nk - 1)
        def _():
            dh = acc_ref[...]
            xv = x_ref[...]
            r = lax.rsqrt(jnp.mean(xv * xv, axis=-1, keepdims=True) + EPS)
            nrm = xv * r
            dpre_ref[...] += jnp.sum(dh * nrm, axis=0, keepdims=True)
            dn = dh * g_ref[...]
            dx_ref[...] = r * (dn - nrm * jnp.mean(dn * nrm, axis=-1, keepdims=True)) + dy_ref[...]

    row = pl.BlockSpec((tm, D_MODEL), lambda i, k: (i, 0))
    vec = pl.BlockSpec((1, D_MODEL), lambda i, k: (0, 0))
    return pl.pallas_call(
        body, name="dh_dx", grid=(S // tm, nk),
        in_specs=[pl.BlockSpec((tm, tk), lambda i, k: (i, k)), pl.BlockSpec((D_MODEL, tk), lambda i, k: (0, k)), row, row, vec],
        out_specs=[row, vec],
        out_shape=[jax.ShapeDtypeStruct((S, D_MODEL), F32), jax.ShapeDtypeStruct((1, D_MODEL), F32)],
        scratch_shapes=[pltpu.VMEM((tm, D_MODEL), F32)],
        compiler_params=_params(("arbitrary", "arbitrary"), 56),
    )(dproj, w_in, x, dy, pre_g)


def _sum_parts(parts, name):
    P, R, C = parts.shape
    tr = max(t for t in range(8, 513, 8) if R % t == 0)

    def body(p_ref, o_ref):
        acc = p_ref[0]
        for j in range(1, P):
            acc = acc + p_ref[j]
        o_ref[...] = acc

    return pl.pallas_call(
        body, name=name, grid=(R // tr,),
        in_specs=[pl.BlockSpec((P, tr, C), lambda i: (0, i, 0))], out_specs=pl.BlockSpec((tr, C), lambda i: (i, 0)),
        out_shape=jax.ShapeDtypeStruct((R, C), F32), compiler_params=_params(("parallel",)),
    )(parts)


def _adamw(parts, w, m, v, name):
    groups = list(parts) if isinstance(parts, (list, tuple)) else [parts]
    P, rows, C = groups[0].shape
    R = rows * len(groups)
    tr = 128 if rows % 128 == 0 else rows
    per = rows // tr
    c1 = 1.0 - ADAM_B1 ** ADAM_STEP
    c2 = 1.0 - ADAM_B2 ** ADAM_STEP

    def body(*refs):
        p_refs = refs[:len(groups)]
        w_ref, m_ref, v_ref, g_ref, d_ref, nm_ref, nv_ref = refs[len(groups):]
        g = None
        for q, p_ref in enumerate(p_refs):
            gq = p_ref[0].astype(F32)
            for j in range(1, P):
                gq = gq + p_ref[j].astype(F32)
            g = gq if g is None else jnp.where(pl.program_id(0) // per == q, gq, g)
        nm = ADAM_B1 * m_ref[...] + (1.0 - ADAM_B1) * g
        nv = ADAM_B2 * v_ref[...] + (1.0 - ADAM_B2) * (g * g)
        g_ref[...] = g
        nm_ref[...] = nm
        nv_ref[...] = nv
        d_ref[...] = -ADAM_LR * ((nm / c1) / (jnp.sqrt(nv / c2) + ADAM_EPS) + ADAM_WD * w_ref[...])

    tile = pl.BlockSpec((tr, C), lambda i: (i, 0))
    out = jax.ShapeDtypeStruct((R, C), F32)
    return pl.pallas_call(
        body, name=name, grid=(R // tr,),
        in_specs=[pl.BlockSpec((P, tr, C), (lambda i, q=q: (0, jnp.clip(i - q * per, 0, per - 1), 0))) for q in range(len(groups))]
        + [tile, tile, tile], out_specs=[tile] * 4, out_shape=[out] * 4,
        compiler_params=_params(("parallel",)),
    )(*groups, w, m, v)


def _forward_a(x, mem, pre_g, mem_g, w_in, conv_w, conv_b, w_a, b_a, w_x, b_x, lam, sinks, rel_bias):
    S = x.shape[0]
    st = dict(T=min(512, S // 2), tm=min(512, S), bucket=_rel_bucket_map())
    st["h"] = _rms_fwd(x, pre_g, "pre_norm")
    st["memn"] = _rms_fwd(mem, mem_g, "mem_norm")
    seg = {}
    for name, c0, width, dt in SEGMENTS:
        seg[name] = _matmul(st["h"], w_in, "nn", S, width, D_MODEL, S, SEG_TILE, D_MODEL, dt, "proj_" + name, b_noff=c0 // SEG_TILE,
                            out_blocked="third" if name == "gl" else None)
    st["seg"] = seg
    st["h_rg"], st["y_rg"] = _rglru_fwd(seg["xr"], seg["g_rg"], conv_w, conv_b, w_a, b_a, w_x, b_x, lam, st["T"])
    st["o_swa"], st["y_swa"] = _swa_fwd(seg["q_s"], seg["kv"], seg["g_swa"], st["bucket"], rel_bias, _sink_column(sinks))
    return st


def _forward_b(st, x, target, post_g, w_memkv, wbr, w_out):
    S = x.shape[0]
    M = st["memn"].shape[0]
    seg = st["seg"]
    st["mkv"] = _matmul(st["memn"], w_memkv, "nn", M, 2 * D_MEM, D_MODEL, M, 512, D_MODEL, BF, "mem_kv")
    st["o_mem"], st["y_mem"] = _mem_fwd(seg["q_m"], st["mkv"], seg["g_mem"])
    st["ys"] = (st["y_rg"], st["y_swa"], st["y_mem"])
    st["merged"] = _merge_fwd(st["ys"], wbr, seg["gl"], st["tm"])
    st["dout"], st["dy"], st["loss"], st["dpost"] = _out_loss(st["merged"], w_out, x, target, post_g, min(256, S))
    return st


def _backward_a(st, mem, w_memkv, wbr, w_out, conv_w, conv_b, w_a, b_a, w_x, b_x, lam):
    S = st["h"].shape[0]
    M = mem.shape[0]
    seg, ys, tm = st["seg"], st["ys"], st["tm"]
    st["dw_out"] = _matmul(st["merged"], st["dout"], "tn", D_MODEL, D_MODEL, S, 256, D_MODEL, S, BF, "dw_out", out_blocked="row")
    dgl0, dgl1, dgl2, dp0, dp1, dp2 = _merge_bwd(st["dout"], w_out, ys, wbr, seg["gl"], tm)
    st["dgl"] = (dgl0, dgl1, dgl2)
    dys, dwbr = [], []
    for i, dp in enumerate((dp0, dp1, dp2)):
        dys.append(_matmul(dp, wbr[i], "nt", S, 1024, D_MODEL, tm, 1024, D_MODEL, F32, "dy_br%d" % i))
        dwbr.append(_matmul(ys[i], dp, "tn", 1024, D_MODEL, S, 1024, 256, S, BF, "dw_br%d" % i, out_blocked="col"))
    st["dys"], st["dwbr"] = dys, dwbr
    st["dq_m"], st["dg_mem"], dmkv = _mem_bwd(seg["q_m"], st["mkv"], seg["g_mem"], st["o_mem"], dys[2])
    dmkv_b = dmkv.astype(BF)
    st["dw_memkv"] = _matmul(st["memn"], dmkv_b, "tn", D_MODEL, 2 * D_MEM, M, 256, 2 * D_MEM, M, BF, "dw_memkv", out_blocked="row")
    dmemn = _matmul(dmkv_b, w_memkv, "nt", M, D_MODEL, 2 * D_MEM, M, 512, 2 * D_MEM, F32, "dmemn")
    st["dmem_g"] = _rms_gain_grad(dmemn, mem, "dmem_gain")
    st["dxr"], st["dg_rg"], st["dw_a"], st["dw_x"], st["dvec"] = _rglru_bwd(
        seg["xr"], seg["g_rg"], st["h_rg"], dys[0], conv_w, conv_b, w_a, b_a, w_x, b_x, lam, st["T"])
    return st


def _backward_b(st, rel_bias, sinks):
    seg = st["seg"]
    dq_s, dg_swa, dkv, dsinks, drel = _swa_bwd(seg["q_s"], seg["kv"], seg["g_swa"], st["o_swa"], st["dys"][1],
                                               st["bucket"], rel_bias, _sink_column(sinks))
    st["dsinks"], st["drel"] = dsinks.reshape(1, SWA_HEADS), drel.reshape(REL_BUCKETS, SWA_HEADS)
    st["dproj"] = jnp.concatenate([st["dxr"], st["dg_rg"], dq_s, dkv.astype(BF), dg_swa, st["dq_m"], st["dg_mem"], *st["dgl"]], axis=1)
    return st


def _dw_in_half(st, half, dep=None):
    S = st["h"].shape[0]
    return _matmul(st["h"], st["dproj"], "tn", D_MODEL // 2, D_IN, S, 512, 1792, S, BF, "dw_in%d" % half, a_moff=2 * half, dep=dep)


def _owner_blocks(a):
    return jnp.swapaxes(a.reshape((4, 2) + a.shape[1:]), 0, 1)


def _local_step(x, mem, target, pre_g, post_g, mem_g, w_in, conv_w, conv_b, w_a, b_a, w_x, b_x, lam, sinks, rel_bias,
                w_memkv, wbr, w_out):
    st = _forward_a(x, mem, pre_g, mem_g, w_in, conv_w, conv_b, w_a, b_a, w_x, b_x, lam, sinks, rel_bias)
    st = _forward_b(st, x, target, post_g, w_memkv, wbr, w_out)
    st = _backward_a(st, mem, w_memkv, wbr, w_out, conv_w, conv_b, w_a, b_a, w_x, b_x, lam)
    st = _backward_b(st, rel_bias, sinks)
    st["dw_in"] = jnp.concatenate([_dw_in_half(st, 0), _dw_in_half(st, 1)], axis=0)
    st["grad_x"], st["dpre"] = _dh_dx(st["dproj"], w_in, x, st["dy"], pre_g, st["tm"], 896)
    return st


def _pad_rows(a, rows):
    a = a.reshape(-1, 128) if a.shape[-1] % 128 == 0 else jnp.pad(a, ((0, 0), (0, 128 - a.shape[-1])))
    return jnp.pad(a, ((0, rows - a.shape[0]), (0, 0))) if a.shape[0] < rows else a


def kernel(x, mem, pre_norm_g, post_norm_g, mem_norm_g, w_in, conv_w, conv_b, w_rg_a, b_rg_a, w_rg_x, b_rg_x, lru_lambda, swa_sinks, rel_bias, w_mem_kv, w_br_rg, w_br_swa, w_br_mem, w_out, loss_target, m_pre_norm_g, m_post_norm_g, m_mem_norm_g, m_w_in, m_conv_w, m_conv_b, m_w_rg_a, m_b_rg_a, m_w_rg_x, m_b_rg_x, m_lru_lambda, m_swa_sinks, m_rel_bias, m_w_mem_kv, m_w_br_rg, m_w_br_swa, m_w_br_mem, m_w_out, v_pre_norm_g, v_post_norm_g, v_mem_norm_g, v_w_in, v_conv_w, v_conv_b, v_w_rg_a, v_b_rg_a, v_w_rg_x, v_b_rg_x, v_lru_lambda, v_swa_sinks, v_rel_bias, v_w_mem_kv, v_w_br_rg, v_w_br_swa, v_w_br_mem, v_w_out):
    cx, cy, cc = lax.axis_index("x"), lax.axis_index("y"), lax.axis_index("c")
    me = 4 * cx + 2 * cy + cc
    chip = 2 * cx + cy
    core = jnp.reshape(cc, (1,)).astype(jnp.int32)
    x0, mem0 = x[0], mem[0]
    w_a_b, w_x_b = w_rg_a[0].astype(BF), w_rg_x[0].astype(BF)

    def landing(own, slot, slots, kind="lead"):
        if kind == "cols":
            return lax.dynamic_update_slice(lax.empty((own.shape[0], slots * own.shape[1]), own.dtype), own, (0, slot * own.shape[1]))
        return lax.dynamic_update_slice(lax.empty((slots,) + own.shape, own.dtype), own[None], (slot,) + (0,) * own.ndim)

    def scatter_start(parts, tag):
        got = _swap_with_sibling(parts, "scatter_%s_swap" % tag)
        sums = [_pair_sum(p, g, core, "scatter_%s_sum%d" % (tag, i)) for i, (p, g) in enumerate(zip(parts, got))]
        lands = [landing(lax.dynamic_index_in_dim(s, chip, 0, keepdims=False), chip, 4) for s in sums]
        return _exchange_start(sums, lands, _plan_scatter(len(sums)), "scatter_%s_start" % tag)

    g_in, g_cw = _all_gather([w_in[0].astype(BF), conv_w[0]], "gather_w_in")
    w_in_f = jnp.transpose(g_in, (1, 0, 2)).reshape(D_MODEL, D_IN)
    conv_w_f = jnp.transpose(g_cw, (1, 0, 2)).reshape(CONV_W, D_RNN)

    after_first = jnp.minimum(jnp.abs(g_cw[0, 0, 0]), 0.0).astype(BF)
    rest = [w.astype(BF) + after_first for w in (w_mem_kv[0], w_br_rg[0], w_br_swa[0], w_br_mem[0], w_out[0])]
    kinds = ["lead", "cols", "cols", "cols", "lead"]
    plan_g = _plan_gather(kinds)
    g_send, g_recv, g_src, g_land, g_token = _exchange_start(rest, [landing(w, me, N_DEV, kd) for w, kd in zip(rest, kinds)], plan_g,
                                                             "gather_rest_start")
    st = _forward_a(x0, mem0, pre_norm_g + g_token[0:1, 0:1], mem_norm_g, w_in_f, conv_w_f, conv_b, w_a_b, b_rg_a, w_x_b, b_rg_x,
                    lru_lambda, swa_sinks, rel_bias)
    g_land = _exchange_wait(g_send, g_recv, g_src, g_land, plan_g, st["y_swa"], "gather_rest_wait")
    g_land = _forward_to_sibling(g_land, kinds, "gather_rest_forward")
    w_memkv_f = g_land[0].reshape(D_MODEL, 2 * D_MEM)
    wbr = (g_land[1], g_land[2], g_land[3])
    w_out_f = g_land[4].reshape(D_MODEL, D_MODEL)

    st = _forward_b(st, x0, loss_target[0], post_norm_g, w_memkv_f, wbr, w_out_f)
    st = _backward_a(st, mem0, w_memkv_f, wbr, w_out_f, conv_w_f, conv_b, w_a_b, b_rg_a, w_x_b, b_rg_x, lru_lambda)
    parts_a = [st["dw_memkv"], st["dwbr"][0], st["dwbr"][1], st["dwbr"][2], st["dw_out"],
               _owner_blocks(st["dw_a"]), _owner_blocks(st["dw_x"])]
    plan_a = _plan_scatter(len(parts_a))
    a_send, a_recv, a_src, a_land, a_token = scatter_start(parts_a, "a")

    st = _backward_b(st, rel_bias, swa_sinks + a_token[0:1, 0:1])
    plan_b = _plan_scatter(1)
    halves, dep = [], None
    for half in range(2):
        dwh = _dw_in_half(st, half, dep)
        parts_b = [jnp.transpose(dwh.reshape(D_MODEL // 2, 4, 2, D_IN // N_DEV), (2, 1, 0, 3))]
        halves.append(scatter_start(parts_b, "b%d" % half))
        dep = halves[-1][4]
    grad_x, dpre = _dh_dx(st["dproj"], w_in_f, x0, st["dy"], pre_norm_g + dep[0:1, 0:1], st["tm"], 896)
    a_land = _exchange_wait(a_send, a_recv, a_src, a_land, plan_a, grad_x, "scatter_a_wait")
    g_wa_blk = _sum_parts(a_land[5], "sum_w_rg_a")
    g_wx_blk = _sum_parts(a_land[6], "sum_w_rg_x")
    big = [None]
    for j, (wt, mt, vt) in enumerate(((w_mem_kv, m_w_mem_kv, v_w_mem_kv), (w_br_rg, m_w_br_rg, v_w_br_rg),
                                      (w_br_swa, m_w_br_swa, v_w_br_swa), (w_br_mem, m_w_br_mem, v_w_br_mem), (w_out, m_w_out, v_w_out))):
        big.append([a[None] for a in _adamw(a_land[j], wt[0], mt[0], vt[0], "adamw_big%d" % (j + 1))])
    after, b_lands = big[5][1], []
    for half, (b_send, b_recv, b_src, b_land, _) in enumerate(halves):
        b_lands.append(_exchange_wait(b_send, b_recv, b_src, b_land, plan_b, after, "scatter_b%d_wait" % half)[0])
        after = b_lands[-1]
    big[0] = [a[None] for a in _adamw(b_lands, w_in[0], m_w_in[0], v_w_in[0], "adamw_big0")]
    links_free = jnp.minimum(jnp.abs(big[0][0][0, 0, 0]), 0.0)

    pack = jnp.concatenate([dpre.reshape(16, 128), st["dpost"].reshape(16, 128), st["dmem_g"].reshape(16, 128),
                            st["dvec"].reshape(64, 128), _pad_rows(st["dsinks"], 8), _pad_rows(st["drel"], 32), g_wa_blk, g_wx_blk], axis=0) + links_free
    gathered = _all_gather([pack], "gather_small")[0]
    gs = _sum_parts(gathered, "sum_small")
    g_pre, g_post, g_memg = gs[0:16].reshape(1, D_MODEL), gs[16:32].reshape(1, D_MODEL), gs[32:48].reshape(1, D_MODEL)
    gvec = gs[48:112].reshape(8, D_RNN)
    g_conv_w = lax.dynamic_slice(gvec[0:CONV_W], (0, me * RNN_BLOCK), (CONV_W, RNN_BLOCK))
    g_conv_b, g_b_a, g_b_x, g_lam = gvec[4:5], gvec[5:6], gvec[6:7], gvec[7:8]
    g_sinks = gs[112:113, :SWA_HEADS]
    g_rel = gs[120:152, :SWA_HEADS]
    g_w_a = gathered[:, 152:280]
    g_w_x = gathered[:, 280:408]

    def packed(ts):
        pre, post, memg, cb, ba, bx, lm, wa, wx, sk, rel, cw = ts
        return jnp.concatenate([pre.reshape(16, 128), post.reshape(16, 128), memg.reshape(16, 128), cb.reshape(8, 128),
                                ba.reshape(8, 128), bx.reshape(8, 128), lm.reshape(8, 128), wa.reshape(1024, 128),
                                wx.reshape(1024, 128), _pad_rows(sk.reshape(1, SWA_HEADS), 8), _pad_rows(rel, 32),
                                _pad_rows(cw.reshape(CONV_W, RNN_BLOCK), 8)], axis=0)

    def unpacked(a):
        return (a[0:16].reshape(1, D_MODEL), a[16:32].reshape(1, D_MODEL), a[32:48].reshape(1, D_MODEL), a[48:56].reshape(1, D_RNN),
                a[56:64].reshape(1, D_RNN), a[64:72].reshape(1, D_RNN), a[72:80].reshape(1, D_RNN),
                a[80:1104].reshape(1, RNN_BLOCKS, RNN_BLOCK, RNN_BLOCK), a[1104:2128].reshape(1, RNN_BLOCKS, RNN_BLOCK, RNN_BLOCK),
                a[2128:2129, :SWA_HEADS], a[2136:2168, :SWA_HEADS], a[2168:2172].reshape(1, CONV_W, RNN_BLOCK))

    g_small = (g_pre, g_post, g_memg, g_conv_b, g_b_a, g_b_x, g_lam, g_w_a, g_w_x, g_sinks, g_rel, g_conv_w)
    w_small = (pre_norm_g, post_norm_g, mem_norm_g, conv_b, b_rg_a, b_rg_x, lru_lambda, w_rg_a, w_rg_x, swa_sinks, rel_bias, conv_w)
    m_small = (m_pre_norm_g, m_post_norm_g, m_mem_norm_g, m_conv_b, m_b_rg_a, m_b_rg_x, m_lru_lambda, m_w_rg_a, m_w_rg_x, m_swa_sinks, m_rel_bias, m_conv_w)
    v_small = (v_pre_norm_g, v_post_norm_g, v_mem_norm_g, v_conv_b, v_b_rg_a, v_b_rg_x, v_lru_lambda, v_w_rg_a, v_w_rg_x, v_swa_sinks, v_rel_bias, v_conv_w)
    sm = [unpacked(a) for a in _adamw(packed(g_small)[None], packed(w_small), packed(m_small), packed(v_small), "adamw_small")]


    loss_total = lax.psum(st["loss"][0, 0], AXES)

    def leaves(k):
        s = sm[k]
        return [s[0], s[1], s[2], big[0][k], s[11], s[3], s[7], s[4], s[8], s[5], s[6], s[9], s[10],
                big[1][k], big[2][k], big[3][k], big[4][k], big[5][k]]

    return (loss_total, grad_x[None], *leaves(0), *leaves(1), *leaves(2), *leaves(3))
```

```python
import math

import jax
import jax.numpy as jnp
import numpy as np
from jax import lax
from jax.experimental import pallas as pl
from jax.experimental.pallas import tpu as pltpu

F32, BF = jnp.float32, jnp.bfloat16
MESH = pl.DeviceIdType.MESH
AXES = ("x", "y", "c")
N_DEV = 8

D_MODEL = 2048
D_RNN = 1024
RNN_BLOCKS = 8
RNN_BLOCK = 128
CONV_W = 4
LRU_C = 8.0
SWA_HEADS = 16
SWA_KV_HEADS = 2
SWA_HD = 64
WINDOW = 128
MEM_HEADS = 4
MEM_HD = 256
D_MEM = 1024
REL_BUCKETS = 32
REL_MAX_DIST = 128
EPS = 1e-6
NEG_INF = -1e30
D_IN = 12544
SEGMENTS = (("xr", 0, 1024, F32), ("g_rg", 1024, 1024, F32), ("q_s", 2048, 1024, BF), ("kv", 3072, 256, BF),
            ("g_swa", 3328, 1024, F32), ("q_m", 4352, 1024, BF), ("g_mem", 5376, 1024, F32), ("gl", 6400, 6144, F32))
SEG_TILE = 256

ADAM_LR, ADAM_B1, ADAM_B2, ADAM_EPS, ADAM_WD, ADAM_STEP = 0.001, 0.9, 0.999, 1e-08, 0.01, 10

NN = (((1,), (0,)), ((), ()))
NT = (((1,), (1,)), ((), ()))
TN = (((0,), (0,)), ((), ()))
MIB = 2 ** 20


def _dot(a, b, dn):
    return lax.dot_general(a, b, dn, preferred_element_type=F32)


def _params(sem, vmem_mib=48):
    return pltpu.CompilerParams(dimension_semantics=sem, vmem_limit_bytes=vmem_mib * MIB)


def _sigmoid(z):
    return 1.0 / (1.0 + jnp.exp(-z))


def _softplus(z):
    return jnp.maximum(z, 0.0) + jnp.log(1.0 + jnp.exp(-jnp.abs(z)))


def _expm1(z):
    p = z * (1.0 + z * (0.5 + z * (1.0 / 6 + z * (1.0 / 24 + z * (1.0 / 120 + z * (1.0 / 720 + z * (1.0 / 5040 + z / 40320)))))))
    return jnp.where(jnp.abs(z) < 0.3, p, jnp.exp(z) - 1.0)


def _flat(p):
    return 4 * p[0] + 2 * p[1] + p[2]


def _all_gather(arrs, name):
    n = len(arrs)

    def body(*refs):
        ins, outs = refs[:n], refs[n:2 * n]
        send_sems, recv_sems, local_sems = refs[2 * n:]
        x, y, c = lax.axis_index("x"), lax.axis_index("y"), lax.axis_index("c")
        me, sibling = (x, y, c), (x, y, 1 - c)
        chips = [(1 - x, y), (x, 1 - y), (1 - x, 1 - y)]

        def copy(a, k, block, to, src=None):
            dst = outs[a].at[_flat(block)]
            return pltpu.make_async_remote_copy(src_ref=dst if src is None else src, dst_ref=dst,
                                                send_sem=send_sems.at[a * 7 + k], recv_sem=recv_sems.at[a * 7 + k],
                                                device_id=to, device_id_type=MESH)

        mine = [pltpu.make_async_copy(ins[a], outs[a].at[_flat(me)], local_sems.at[a]) for a in range(n)]
        for cp in mine:
            cp.start()
        first = []
        for a in range(n):
            first += [copy(a, 1 + j, me, (*chip, c), src=ins[a]) for j, chip in enumerate(chips)]
            first.append(copy(a, 0, me, sibling, src=ins[a]))
        for cp in first:
            cp.start()
        passed = []
        for j, chip in enumerate(chips):
            for a in range(n):
                copy(a, 1 + j, (*chip, c), me).wait_recv()
                fw = copy(a, 4 + j, (*chip, c), sibling)
                fw.start()
                passed.append(fw)
        for a in range(n):
            copy(a, 0, sibling, me).wait_recv()
            for j, chip in enumerate(chips):
                copy(a, 4 + j, (*chip, 1 - c), me).wait_recv()
        for cp in first + passed:
            cp.wait_send()
        for cp in mine:
            cp.wait()

    any_spec = pl.BlockSpec(memory_space=pl.ANY)
    return pl.pallas_call(
        body, name=name,
        out_shape=[jax.ShapeDtypeStruct((N_DEV,) + a.shape, a.dtype) for a in arrs],
        in_specs=[any_spec] * n, out_specs=[any_spec] * n,
        scratch_shapes=[pltpu.SemaphoreType.DMA((7 * n,)), pltpu.SemaphoreType.DMA((7 * n,)), pltpu.SemaphoreType.DMA((n,))],
    )(*arrs)


def _chip_peers(x, y):
    return [(1 - x, y), (x, 1 - y), (1 - x, 1 - y)]


def _chip(p):
    return 2 * p[0] + p[1]


def _plan_gather(kinds):
    def plan(x, y, c):
        out = []
        for a, kind in enumerate(kinds):
            for peer in [(x, y, 1 - c)] + [(*ch, c) for ch in _chip_peers(x, y)]:
                out.append((a, None, (kind, _flat((x, y, c))), peer, (kind, _flat(peer))))
        return out
    return plan


def _slot(ref, where):
    kind, k = where
    if kind == "lead":
        return ref.at[k]
    return ref.at[:, pl.ds(pl.multiple_of(k * 256, 256), 256)]


def _plan_scatter(n):
    def plan(x, y, c):
        out = []
        for a in range(n):
            for ch in _chip_peers(x, y):
                out.append((a, _chip(ch), ("lead", _chip((x, y))), (*ch, c), ("lead", _chip(ch))))
        return out
    return plan


HBM_SPEC = pl.BlockSpec(memory_space=pltpu.HBM)
SEM_SPEC = pl.BlockSpec(memory_space=pltpu.SEMAPHORE)


def _in_hbm(a):
    return pltpu.with_memory_space_constraint(a, pltpu.HBM)


def _exchange_start(srcs, lands, plan, name):
    n = len(srcs)
    count = len(plan(0, 0, 0))

    def body(*refs):
        src_refs, land_refs = refs[:n], refs[n:2 * n]
        send_sems, recv_sems = refs[2 * n], refs[2 * n + 1]
        token = refs[-1]
        x, y, c = lax.axis_index("x"), lax.axis_index("y"), lax.axis_index("c")
        for k, (a, si, di, peer, _) in enumerate(plan(x, y, c)):
            src = src_refs[a] if si is None else src_refs[a].at[si]
            pltpu.make_async_remote_copy(src_ref=src, dst_ref=_slot(land_refs[a], di), send_sem=send_sems.at[k],
                                         recv_sem=recv_sems.at[k], device_id=peer, device_id_type=MESH).start()
        token[...] = jnp.zeros_like(token)

    out = pl.pallas_call(
        body, name=name,
        out_shape=(pltpu.SemaphoreType.DMA((count,)), pltpu.SemaphoreType.DMA((count,)),
                   *[pltpu.HBM(a.shape, a.dtype) for a in srcs], *[pltpu.HBM(a.shape, a.dtype) for a in lands],
                   jax.ShapeDtypeStruct((8, 128), F32)),
        in_specs=[HBM_SPEC] * (2 * n),
        out_specs=(SEM_SPEC, SEM_SPEC, *([HBM_SPEC] * (2 * n)), pl.BlockSpec(memory_space=pltpu.VMEM)),
        input_output_aliases={i: 2 + i for i in range(2 * n)},
        compiler_params=pltpu.CompilerParams(has_side_effects=pltpu.SideEffectType.DATAFLOW_SIDE_EFFECTING),
    )(*[_in_hbm(a) for a in srcs], *[_in_hbm(a) for a in lands])
    return out[0], out[1], list(out[2:2 + n]), list(out[2 + n:2 + 2 * n]), out[-1]


def _exchange_wait(send_sems, recv_sems, srcs, lands, plan, after, name):
    n = len(srcs)

    def body(*refs):
        src_refs, land_refs = refs[:n], refs[n:2 * n]
        send_sems, recv_sems = refs[2 * n], refs[2 * n + 1]
        x, y, c = lax.axis_index("x"), lax.axis_index("y"), lax.axis_index("c")
        for k, (a, si, _, peer, ri) in enumerate(plan(x, y, c)):
            src = src_refs[a] if si is None else src_refs[a].at[si]
            cp = pltpu.make_async_remote_copy(src_ref=src, dst_ref=_slot(land_refs[a], ri), send_sem=send_sems.at[k],
                                              recv_sem=recv_sems.at[k], device_id=peer, device_id_type=MESH)
            cp.wait_send()
            cp.wait_recv()

    out = pl.pallas_call(
        body, name=name,
        out_shape=(*[pltpu.HBM(a.shape, a.dtype) for a in srcs], *[pltpu.HBM(a.shape, a.dtype) for a in lands]),
        in_specs=[HBM_SPEC] * (2 * n) + [SEM_SPEC, SEM_SPEC, pl.BlockSpec(memory_space=pl.ANY)],
        out_specs=tuple([HBM_SPEC] * (2 * n)),
        input_output_aliases={i: i for i in range(2 * n)},
        compiler_params=pltpu.CompilerParams(has_side_effects=pltpu.SideEffectType.DATAFLOW_SIDE_EFFECTING),
    )(*srcs, *lands, send_sems, recv_sems, after)
    return list(out[n:2 * n])


def _forward_to_sibling(lands, kinds, name):
    n = len(lands)

    def body(*refs):
        in_refs, out_refs = refs[:n], refs[n:2 * n]
        send_sems, recv_sems = refs[2 * n:]
        x, y, c = lax.axis_index("x"), lax.axis_index("y"), lax.axis_index("c")
        sibling = (x, y, 1 - c)

        def copy(a, j, slot):
            return pltpu.make_async_remote_copy(src_ref=_slot(in_refs[a], (kinds[a], slot)), dst_ref=_slot(out_refs[a], (kinds[a], slot)),
                                                send_sem=send_sems.at[a * 3 + j], recv_sem=recv_sems.at[a * 3 + j],
                                                device_id=sibling, device_id_type=MESH)

        sends = [copy(a, j, _flat((*ch, c))) for a in range(n) for j, ch in enumerate(_chip_peers(x, y))]
        for cp in sends:
            cp.start()
        for a in range(n):
            for j, ch in enumerate(_chip_peers(x, y)):
                copy(a, j, _flat((*ch, 1 - c))).wait_recv()
        for cp in sends:
            cp.wait_send()

    any_spec = pl.BlockSpec(memory_space=pl.ANY)
    return pl.pallas_call(
        body, name=name, out_shape=[jax.ShapeDtypeStruct(a.shape, a.dtype) for a in lands],
        in_specs=[any_spec] * n, out_specs=[any_spec] * n, input_output_aliases={a: a for a in range(n)},
        scratch_shapes=[pltpu.SemaphoreType.DMA((3 * n,)), pltpu.SemaphoreType.DMA((3 * n,))],
    )(*lands)


def _swap_with_sibling(parts, name):
    n = len(parts)

    def body(*refs):
        in_refs, out_refs = refs[:n], refs[n:2 * n]
        send_sems, recv_sems = refs[2 * n:]
        x, y, c = lax.axis_index("x"), lax.axis_index("y"), lax.axis_index("c")
        sends = [pltpu.make_async_remote_copy(src_ref=in_refs[a].at[1 - c], dst_ref=out_refs[a], send_sem=send_sems.at[a],
                                              recv_sem=recv_sems.at[a], device_id=(x, y, 1 - c), device_id_type=MESH)
                 for a in range(n)]
        for cp in sends:
            cp.start()
        for cp in sends:
            cp.wait()

    any_spec = pl.BlockSpec(memory_space=pl.ANY)
    return pl.pallas_call(
        body, name=name, out_shape=[jax.ShapeDtypeStruct(a.shape[1:], a.dtype) for a in parts],
        in_specs=[any_spec] * n, out_specs=[any_spec] * n,
        scratch_shapes=[pltpu.SemaphoreType.DMA((n,)), pltpu.SemaphoreType.DMA((n,))],
    )(*parts)


def _pair_sum(parts, got, core, name):
    _, _, R, C = parts.shape
    tr = 256 if R % 256 == 0 else R

    def body(c_ref, p_ref, g_ref, o_ref):
        o_ref[...] = (p_ref[...].astype(F32) + g_ref[...].astype(F32)).astype(o_ref.dtype)

    return pl.pallas_call(
        body, name=name,
        grid_spec=pltpu.PrefetchScalarGridSpec(
            num_scalar_prefetch=1, grid=(4, R // tr),
            in_specs=[pl.BlockSpec((None, None, tr, C), lambda j, i, c_ref: (c_ref[0], j, i, 0)),
                      pl.BlockSpec((None, tr, C), lambda j, i, c_ref: (j, i, 0))],
            out_specs=pl.BlockSpec((None, tr, C), lambda j, i, c_ref: (j, i, 0))),
        out_shape=jax.ShapeDtypeStruct((4, R, C), parts.dtype),
        compiler_params=_params(("parallel", "parallel")),
    )(core, parts, got)


def _matmul(a, b, mode, M, N, K, tm, tn, tk, out_dtype, name, b_noff=0, a_moff=0, b_blocked=False, out_blocked=None, dep=None,
            vmem_mib=48):
    nm, nn, nk = M // tm, N // tn, K // tk
    if mode == "nn":
        a_spec = pl.BlockSpec((tm, tk), lambda j, i, k: (i, k))
        b_spec = pl.BlockSpec((tk, tn), lambda j, i, k: (k, j + b_noff))
        dn = NN
    elif mode == "nt":
        a_spec = pl.BlockSpec((tm, tk), lambda j, i, k: (i, k))
        if b_blocked:
            b_spec = pl.BlockSpec((None, tn, tk), lambda j, i, k: (k, j, 0))
        else:
            b_spec = pl.BlockSpec((tn, tk), lambda j, i, k: (j + b_noff, k))
        dn = NT
    else:
        a_spec = pl.BlockSpec((tk, tm), lambda j, i, k: (k, i + a_moff))
        b_spec = pl.BlockSpec((tk, tn), lambda j, i, k: (k, j + b_noff))
        dn = TN
    if out_blocked == "col":
        out_shape = jax.ShapeDtypeStruct((2, 4, M, tn), out_dtype)
        out_spec = pl.BlockSpec((None, None, tm, tn), lambda j, i, k: (j % 2, j // 2, i, 0))
    elif out_blocked == "row":
        out_shape = jax.ShapeDtypeStruct((2, 4, tm, N), out_dtype)
        out_spec = pl.BlockSpec((None, None, tm, tn), lambda j, i, k: (i % 2, i // 2, 0, j))
    elif out_blocked == "third":
        out_shape = jax.ShapeDtypeStruct((3, M, N // 3), out_dtype)
        out_spec = pl.BlockSpec((None, tm, tn), lambda j, i, k: (j // (nn // 3), i, j % (nn // 3)))
    else:
        out_shape = jax.ShapeDtypeStruct((M, N), out_dtype)
        out_spec = pl.BlockSpec((tm, tn), lambda j, i, k: (i, j))

    def body(a_ref, b_ref, *rest):
        o_ref, scratch = (rest[1], rest[2:]) if dep is not None else (rest[0], rest[1:])
        if nk == 1:
            o_ref[...] = _dot(a_ref[...], b_ref[...], dn).astype(out_dtype)
        else:
            acc_ref, = scratch
            k = pl.program_id(2)

            @pl.when(k == 0)
            def _():
                acc_ref[...] = jnp.zeros_like(acc_ref)

            acc_ref[...] += _dot(a_ref[...], b_ref[...], dn)

            @pl.when(k == nk - 1)
            def _():
                o_ref[...] = acc_ref[...].astype(out_dtype)

    return pl.pallas_call(
        body, name=name, grid=(nn, nm, nk),
        in_specs=[a_spec, b_spec] + ([] if dep is None else [pl.BlockSpec((8, 128), lambda j, i, k: (0, 0))]),
        out_specs=out_spec, out_shape=out_shape,
        scratch_shapes=[] if nk == 1 else [pltpu.VMEM((tm, tn), F32)],
        compiler_params=_params(("parallel", "parallel", "arbitrary"), vmem_mib),
    )(a, b, *([] if dep is None else [dep]))


def _rms_fwd(x, g, name):
    R, Dm = x.shape
    tr = min(R, 256)

    def body(x_ref, g_ref, h_ref):
        xv = x_ref[...]
        r = lax.rsqrt(jnp.mean(xv * xv, axis=-1, keepdims=True) + EPS)
        h_ref[...] = (xv * r * g_ref[...]).astype(BF)

    return pl.pallas_call(
        body, name=name, grid=(R // tr,),
        in_specs=[pl.BlockSpec((tr, Dm), lambda i: (i, 0)), pl.BlockSpec((1, Dm), lambda i: (0, 0))],
        out_specs=pl.BlockSpec((tr, Dm), lambda i: (i, 0)), out_shape=jax.ShapeDtypeStruct((R, Dm), BF),
        compiler_params=_params(("parallel",)),
    )(x, g)


def _rms_gain_grad(dn, x, name):
    R, Dm = x.shape

    def body(dn_ref, x_ref, o_ref):
        xv = x_ref[...]
        r = lax.rsqrt(jnp.mean(xv * xv, axis=-1, keepdims=True) + EPS)
        o_ref[...] = jnp.sum(dn_ref[...] * xv * r, axis=0, keepdims=True)

    return pl.pallas_call(
        body, name=name, out_shape=jax.ShapeDtypeStruct((1, Dm), F32),
        compiler_params=pltpu.CompilerParams(vmem_limit_bytes=32 * MIB),
    )(dn, x)


def _shift_down(v, k, head8, row, T):
    if k == 0:
        return v
    r = pltpu.roll(v, k, 0)
    hr = pltpu.roll(head8, k, 0)
    top = jnp.where(row[:8] < k, hr, r[:8])
    return jnp.concatenate([top, r[8:]], axis=0)


def _shift_up(v, k, tail8, row, T):
    if k == 0:
        return v
    r = pltpu.roll(v, T - k, 0)
    tr = pltpu.roll(tail8, 8 - k, 0)
    bot = jnp.where(row[:8] >= 8 - k, tr, r[T - 8:])
    return jnp.concatenate([r[:T - 8], bot], axis=0)


def _rglru_gates(u, head8, grow, row, T, cw_ref, cb_ref, wa_ref, ba_ref, wx_ref, bx_ref, lam_ref):
    us = [_shift_down(u, k, head8, row, T) for k in range(CONV_W)]
    acc = us[0] * cw_ref[0:1, :]
    for k in range(1, CONV_W):
        acc = acc + us[k] * cw_ref[k:k + 1, :]
    conv = cb_ref[...] + acc
    cbf = conv.astype(BF)
    r_ = _sigmoid(_dot(cbf, wa_ref[0], NN) + ba_ref[...])
    i_ = _sigmoid(_dot(cbf, wx_ref[0], NN) + bx_ref[...])
    sp = _softplus(-lam_ref[...])
    la = -LRU_C * r_ * sp
    a = jnp.exp(la)
    mult_raw = jnp.sqrt(-_expm1(2.0 * la))
    mult = jnp.where(grow == 0, 1.0, mult_raw)
    return us, conv, cbf, r_, i_, sp, a, mult_raw, mult


def _rglru_specs(T, nt, rev):
    tmap = (lambda n, t: (nt - 1 - t, n)) if rev else (lambda n, t: (t, n))
    hmap = ((lambda n, t: (jnp.maximum((nt - 1 - t) * (T // 8) - 1, 0), n)) if rev
            else (lambda n, t: (jnp.maximum(t * (T // 8) - 1, 0), n)))
    tile = pl.BlockSpec((T, RNN_BLOCK), tmap)
    halo = pl.BlockSpec((8, RNN_BLOCK), hmap)
    vec = pl.BlockSpec((1, RNN_BLOCK), lambda n, t: (0, n))
    cw = pl.BlockSpec((CONV_W, RNN_BLOCK), lambda n, t: (0, n))
    wblk = pl.BlockSpec((1, RNN_BLOCK, RNN_BLOCK), lambda n, t: (n, 0, 0))
    return tile, halo, vec, cw, wblk


def _rglru_fwd(xr, g, cw, cb, wa, ba, wx, bx, lam, T):
    S = xr.shape[0]
    nt = S // T

    def body(u_ref, uh_ref, g_ref, cw_ref, cb_ref, wa_ref, ba_ref, wx_ref, bx_ref, lam_ref, h_ref, y_ref, carry):
        t = pl.program_id(1)

        @pl.when(t == 0)
        def _():
            carry[...] = jnp.zeros_like(carry)

        row = lax.broadcasted_iota(jnp.int32, (T, RNN_BLOCK), 0)
        grow = row + t * T
        head8 = jnp.where(t > 0, uh_ref[...], 0.0)
        _, conv, _, _, i_, _, a, _, mult = _rglru_gates(u_ref[...], head8, grow, row, T, cw_ref, cb_ref, wa_ref, ba_ref,
                                                         wx_ref, bx_ref, lam_ref)
        b = mult * i_ * conv
        s = 1
        while s < T:
            keep = row >= s
            a_s = jnp.where(keep, pltpu.roll(a, s, 0), 1.0)
            b_s = jnp.where(keep, pltpu.roll(b, s, 0), 0.0)
            b = a * b_s + b
            a = a * a_s
            s *= 2
        h = b + a * carry[0:1, :]
        carry[...] = jnp.broadcast_to(h[T - 1:T, :], carry.shape)
        h_ref[...] = h
        gv = g_ref[...]
        y_ref[...] = (h * (gv * _sigmoid(gv))).astype(BF)

    tile, halo, vec, cwspec, wblk = _rglru_specs(T, nt, False)
    return pl.pallas_call(
        body, name="rglru_fwd", grid=(RNN_BLOCKS, nt),
        in_specs=[tile, halo, tile, cwspec, vec, wblk, vec, wblk, vec, vec],
        out_specs=[tile, tile],
        out_shape=[jax.ShapeDtypeStruct((S, D_RNN), F32), jax.ShapeDtypeStruct((S, D_RNN), BF)],
        scratch_shapes=[pltpu.VMEM((8, RNN_BLOCK), F32)],
        compiler_params=_params(("parallel", "arbitrary")),
    )(xr, xr, g, cw, cb, wa, ba, wx, bx, lam)


def _rglru_bwd(xr, g, h, dy, cw, cb, wa, ba, wx, bx, lam, T):
    S = xr.shape[0]
    nt = S // T

    def body(u_ref, uh_ref, g_ref, h_ref, hh_ref, dy_ref, cw_ref, cb_ref, wa_ref, ba_ref, wx_ref, bx_ref, lam_ref,
             du_ref, dg_ref, dwa_ref, dwx_ref, dvec_ref, c_dhh, c_a, c_dconv):
        t = pl.program_id(1)
        tt = nt - 1 - t

        @pl.when(t == 0)
        def _():
            c_dhh[...] = jnp.zeros_like(c_dhh)
            c_a[...] = jnp.zeros_like(c_a)
            c_dconv[...] = jnp.zeros_like(c_dconv)
            dwa_ref[...] = jnp.zeros_like(dwa_ref)
            dwx_ref[...] = jnp.zeros_like(dwx_ref)
            dvec_ref[...] = jnp.zeros_like(dvec_ref)

        row = lax.broadcasted_iota(jnp.int32, (T, RNN_BLOCK), 0)
        row8 = row[:8]
        grow = row + tt * T
        head8 = jnp.where(tt > 0, uh_ref[...], 0.0)
        us, conv, cbf, r_, i_, sp, a, mult_raw, mult = _rglru_gates(
            u_ref[...], head8, grow, row, T, cw_ref, cb_ref, wa_ref, ba_ref, wx_ref, bx_ref, lam_ref)
        hv = h_ref[...]
        hprev = _shift_down(hv, 1, jnp.where(tt > 0, hh_ref[...], 0.0), row, T)
        gv = g_ref[...]
        sg = _sigmoid(gv)
        dyv = dy_ref[...]
        dg_ref[...] = (dyv * hv * (sg * (1.0 + gv * (1.0 - sg)))).astype(BF)
        d = dyv * (gv * sg)
        A = _shift_up(a, 1, c_a[...], row, T)
        s = 1
        while s < T:
            keep = row < T - s
            A_s = jnp.where(keep, pltpu.roll(A, T - s, 0), 1.0)
            d_s = jnp.where(keep, pltpu.roll(d, T - s, 0), 0.0)
            d = A * d_s + d
            A = A * A_s
            s *= 2
        dhh = d + A * c_dhh[0:1, :]
        da = dhh * hprev
        dconv = dhh * mult * i_
        di = dhh * mult * conv
        dmult = dhh * i_ * conv
        dla = da * a - jnp.where(grow == 0, 0.0, dmult * (a * a) / mult_raw)
        dr = dla * (-LRU_C * sp)
        dsp = jnp.sum(dla * (-LRU_C * r_), axis=0, keepdims=True)
        dza = dr * r_ * (1.0 - r_)
        dzx = di * i_ * (1.0 - i_)
        dza_b, dzx_b = dza.astype(BF), dzx.astype(BF)
        dconv = dconv + _dot(dza_b, wa_ref[0], NT) + _dot(dzx_b, wx_ref[0], NT)
        dwa_ref[0] += _dot(cbf, dza_b, TN)
        dwx_ref[0] += _dot(cbf, dzx_b, TN)
        lam = lam_ref[...]
        rows = [jnp.sum(dconv * us[k], axis=0, keepdims=True) for k in range(CONV_W)]
        rows += [jnp.sum(dconv, axis=0, keepdims=True), jnp.sum(dza, axis=0, keepdims=True),
                 jnp.sum(dzx, axis=0, keepdims=True), dsp * (-_sigmoid(-lam))]
        upd = jnp.zeros((8, RNN_BLOCK), F32)
        for j, rv in enumerate(rows):
            upd = upd + jnp.where(row8 == j, rv, 0.0)
        dvec_ref[...] += upd
        tail8 = c_dconv[...]
        du = dconv * cw_ref[0:1, :]
        for k in range(1, CONV_W):
            du = du + _shift_up(dconv, k, tail8, row, T) * cw_ref[k:k + 1, :]
        du_ref[...] = du.astype(BF)
        c_dhh[...] = jnp.broadcast_to(dhh[0:1, :], c_dhh.shape)
        c_a[...] = jnp.broadcast_to(a[0:1, :], c_a.shape)
        c_dconv[...] = dconv[:8]

    tile, halo, vec, cwspec, wblk = _rglru_specs(T, nt, True)
    acc8 = pl.BlockSpec((8, RNN_BLOCK), lambda n, t: (0, n))
    return pl.pallas_call(
        body, name="rglru_bwd", grid=(RNN_BLOCKS, nt),
        in_specs=[tile, halo, tile, tile, halo, tile, cwspec, vec, wblk, vec, wblk, vec, vec],
        out_specs=[tile, tile, wblk, wblk, acc8],
        out_shape=[jax.ShapeDtypeStruct((S, D_RNN), BF), jax.ShapeDtypeStruct((S, D_RNN), BF),
                   jax.ShapeDtypeStruct((RNN_BLOCKS, RNN_BLOCK, RNN_BLOCK), F32),
                   jax.ShapeDtypeStruct((RNN_BLOCKS, RNN_BLOCK, RNN_BLOCK), F32),
                   jax.ShapeDtypeStruct((8, D_RNN), F32)],
        scratch_shapes=[pltpu.VMEM((8, RNN_BLOCK), F32)] * 3,
        compiler_params=_params(("parallel", "arbitrary")),
    )(xr, xr, g, h, h, dy, cw, cb, wa, ba, wx, bx, lam)


def _rel_bucket_map():
    qi = np.arange(WINDOW)[:, None]
    kj = np.arange(2 * WINDOW)[None, :]
    dist = jnp.asarray(qi + WINDOW - kj, jnp.int32)
    n = jnp.maximum(dist, 0)
    max_exact = REL_BUCKETS // 2
    ratio = jnp.log(jnp.maximum(n, 1).astype(F32) / max_exact) / math.log(REL_MAX_DIST / max_exact)
    large = jnp.minimum(max_exact + (ratio * (REL_BUCKETS - max_exact)).astype(jnp.int32), REL_BUCKETS - 1)
    bucket = jnp.where(n < max_exact, n, large).astype(jnp.int32)
    j = np.arange(WINDOW)[None, :]
    return jnp.where(jnp.asarray(j > qi), bucket[:, :WINDOW], bucket[:, WINDOW:])


def _swa_common(n, kv_ref, bucket_ref, relb_ref, bias_scr):
    @pl.when(n == 0)
    def _():
        bk = bucket_ref[...]
        for h in range(SWA_HEADS):
            acc = jnp.zeros((WINDOW, WINDOW), F32)
            for b in range(REL_BUCKETS):
                acc = acc + jnp.where(bk == b, relb_ref[b, h], 0.0)
            bias_scr[h] = acc

    prev0 = pl.multiple_of(jnp.maximum(n - 1, 0) * WINDOW, WINDOW)
    cur0 = pl.multiple_of(n * WINDOW, WINDOW)
    kk = jnp.concatenate([kv_ref[pl.ds(prev0, WINDOW), :], kv_ref[pl.ds(cur0, WINDOW), :]], axis=0).astype(F32)
    rowi = lax.broadcasted_iota(jnp.int32, (WINDOW, WINDOW), 0)
    col = lax.broadcasted_iota(jnp.int32, (WINDOW, WINDOW), 1)
    from_prev = col > rowi
    return kk, from_prev, prev0, cur0


def _fold(full, from_prev):
    return jnp.where(from_prev, full[:, :WINDOW], full[:, WINDOW:])


def _unfold(sq, from_prev):
    return jnp.concatenate([jnp.where(from_prev, sq, 0.0), jnp.where(from_prev, 0.0, sq)], axis=1)


def _half_pair(part, kvh):
    lo = lax.broadcasted_iota(jnp.int32, part.shape, 1) < SWA_HD
    if kvh == 0:
        pa = jnp.where(lo, part, 0.0)
        pb = pltpu.roll(pa, SWA_HD, 1)
    else:
        pb = jnp.where(lo, 0.0, part)
        pa = pltpu.roll(pb, SWA_HD, 1)
    return pa.astype(BF), pb.astype(BF)


ALL_HEADS = SWA_HEADS * WINDOW


def _sink_column(sinks):
    return jnp.repeat(sinks.reshape(SWA_HEADS), WINDOW).reshape(ALL_HEADS, 1)


def _swa_operands(kk):
    return [(_half_pair(kk[:, :128], kvh), _half_pair(kk[:, 128:], kvh)) for kvh in range(SWA_KV_HEADS)]


def _swa_probs(n, q_ref, ops, bias_scr, sinkc_ref, from_prev):
    lgs = []
    for kvh in range(SWA_KV_HEADS):
        (ka, kb), _ = ops[kvh]
        for p in range(4):
            q2 = q_ref[:, kvh * 512 + p * 128:kvh * 512 + p * 128 + 128]
            lgs += [_fold(_dot(q2, ka, NT), from_prev), _fold(_dot(q2, kb, NT), from_prev)]
    lg = jnp.concatenate(lgs, axis=0) * (SWA_HD ** -0.5) + bias_scr[...].reshape(ALL_HEADS, WINDOW)
    rowi = jnp.bitwise_and(lax.broadcasted_iota(jnp.int32, (ALL_HEADS, WINDOW), 0), WINDOW - 1)
    col = lax.broadcasted_iota(jnp.int32, (ALL_HEADS, WINDOW), 1)
    no_prev = jnp.where(n > 0, 0, 4 * WINDOW)
    lg = jnp.where(jnp.logical_or(col <= rowi, col > rowi + no_prev), lg, NEG_INF)
    sink = sinkc_ref[...]
    m = jnp.maximum(jnp.max(lg, axis=-1, keepdims=True), sink)
    e = jnp.exp(lg - m)
    es = jnp.exp(sink - m)
    den = jnp.sum(e, axis=-1, keepdims=True) + es
    return e / den, es / den


def _swa_fwd(q, kv, g, bucket, rel_bias, sink_col):
    S = q.shape[0]
    nb = S // WINDOW

    def body(q_ref, kv_ref, g_ref, bucket_ref, relb_ref, sinkc_ref, o_ref, y_ref, bias_scr):
        n = pl.program_id(0)
        kk, from_prev, _, _ = _swa_common(n, kv_ref, bucket_ref, relb_ref, bias_scr)
        ops = _swa_operands(kk)
        pr, _ = _swa_probs(n, q_ref, ops, bias_scr, sinkc_ref, from_prev)
        for kvh in range(SWA_KV_HEADS):
            _, (va, vb) = ops[kvh]
            for p in range(4):
                c0 = kvh * 512 + p * 128
                r0 = (kvh * 8 + 2 * p) * WINDOW
                o2 = (_dot(_unfold(pr[r0:r0 + WINDOW], from_prev).astype(BF), va, NN)
                      + _dot(_unfold(pr[r0 + WINDOW:r0 + 2 * WINDOW], from_prev).astype(BF), vb, NN))
                o_ref[:, c0:c0 + 128] = o2
                gv = g_ref[:, c0:c0 + 128]
                y_ref[:, c0:c0 + 128] = (o2 * (gv * _sigmoid(gv))).astype(BF)

    blk = pl.BlockSpec((WINDOW, 1024), lambda n: (n, 0))
    smem = pl.BlockSpec(memory_space=pltpu.SMEM)
    sinkc = pl.BlockSpec((ALL_HEADS, 1), lambda n: (0, 0))
    return pl.pallas_call(
        body, name="swa_fwd", grid=(nb,),
        in_specs=[blk, pl.BlockSpec((S, 256), lambda n: (0, 0)), blk, pl.BlockSpec((WINDOW, WINDOW), lambda n: (0, 0)), smem, sinkc],
        out_specs=[blk, blk],
        out_shape=[jax.ShapeDtypeStruct((S, 1024), F32), jax.ShapeDtypeStruct((S, 1024), BF)],
        scratch_shapes=[pltpu.VMEM((SWA_HEADS, WINDOW, WINDOW), F32)],
        compiler_params=_params(("arbitrary",)),
    )(q, kv, g, bucket, rel_bias, sink_col)


def _swa_bwd(q, kv, g, o, dy, bucket, rel_bias, sink_col):
    S = q.shape[0]
    nb = S // WINDOW

    def body(q_ref, kv_ref, g_ref, o_ref, dy_ref, bucket_ref, relb_ref, sinkc_ref,
             dq_ref, dg_ref, dkv_ref, dsink_ref, drel_ref, bias_scr, dbias_scr, dsink_scr):
        n = pl.program_id(0)

        @pl.when(n == 0)
        def _():
            dbias_scr[...] = jnp.zeros_like(dbias_scr)
            dsink_scr[...] = jnp.zeros_like(dsink_scr)
            dkv_ref[...] = jnp.zeros_like(dkv_ref)

        kk, from_prev, prev0, cur0 = _swa_common(n, kv_ref, bucket_ref, relb_ref, bias_scr)
        ops = _swa_operands(kk)
        pr, ps = _swa_probs(n, q_ref, ops, bias_scr, sinkc_ref, from_prev)
        do2s, dps = [], []
        for kvh in range(SWA_KV_HEADS):
            _, (va, vb) = ops[kvh]
            for p in range(4):
                c0 = kvh * 512 + p * 128
                gv = g_ref[:, c0:c0 + 128]
                sg = _sigmoid(gv)
                dyv = dy_ref[:, c0:c0 + 128]
                dg_ref[:, c0:c0 + 128] = (dyv * o_ref[:, c0:c0 + 128] * (sg * (1.0 + gv * (1.0 - sg)))).astype(BF)
                do2 = (dyv * (gv * sg)).astype(BF)
                do2s.append(do2)
                dps += [_fold(_dot(do2, va, NT), from_prev), _fold(_dot(do2, vb, NT), from_prev)]
        dp = jnp.concatenate(dps, axis=0)
        delta = jnp.sum(pr * dp, axis=-1, keepdims=True)
        ds = pr * (dp - delta)
        dbias_scr[...] += ds.reshape(SWA_HEADS, WINDOW, WINDOW)
        dsink_scr[...] += ps * delta
        dsc = ds * (SWA_HD ** -0.5)
        lo256 = lax.broadcasted_iota(jnp.int32, (2 * WINDOW, 128), 1) < SWA_HD
        dks, dvs = [], []
        for kvh in range(SWA_KV_HEADS):
            (ka, kb), _ = ops[kvh]
            dka = jnp.zeros((2 * WINDOW, 128), F32)
            dkb, dva, dvb = dka, dka, dka
            for p in range(4):
                c0 = kvh * 512 + p * 128
                r0 = (kvh * 8 + 2 * p) * WINDOW
                q2 = q_ref[:, c0:c0 + 128]
                do2 = do2s[kvh * 4 + p]
                ds0 = _unfold(dsc[r0:r0 + WINDOW], from_prev).astype(BF)
                ds1 = _unfold(dsc[r0 + WINDOW:r0 + 2 * WINDOW], from_prev).astype(BF)
                dq_ref[:, c0:c0 + 128] = (_dot(ds0, ka, NN) + _dot(ds1, kb, NN)).astype(BF)
                dka = dka + _dot(ds0, q2, TN)
                dkb = dkb + _dot(ds1, q2, TN)
                dva = dva + _dot(_unfold(pr[r0:r0 + WINDOW], from_prev).astype(BF), do2, TN)
                dvb = dvb + _dot(_unfold(pr[r0 + WINDOW:r0 + 2 * WINDOW], from_prev).astype(BF), do2, TN)
            dks.append(jnp.where(lo256, dka, 0.0) + pltpu.roll(jnp.where(lo256, 0.0, dkb), SWA_HD, 1))
            dvs.append(jnp.where(lo256, dva, 0.0) + pltpu.roll(jnp.where(lo256, 0.0, dvb), SWA_HD, 1))
        dk = dks[0] + pltpu.roll(dks[1], SWA_HD, 1)
        dv = dvs[0] + pltpu.roll(dvs[1], SWA_HD, 1)
        dkv_ref[pl.ds(prev0, WINDOW), 0:128] += dk[:WINDOW]
        dkv_ref[pl.ds(prev0, WINDOW), 128:256] += dv[:WINDOW]
        dkv_ref[pl.ds(cur0, WINDOW), 0:128] += dk[WINDOW:]
        dkv_ref[pl.ds(cur0, WINDOW), 128:256] += dv[WINDOW:]

        @pl.when(n == nb - 1)
        def _():
            dsink_ref[...] = -jnp.sum(dsink_scr[...].reshape(SWA_HEADS, WINDOW, 1), axis=1)
            bk = bucket_ref[...]
            sums = []
            for b in range(REL_BUCKETS):
                sums.append(jnp.sum(jnp.where((bk == b)[None], dbias_scr[...], 0.0), axis=1))
            drel_ref[...] = jnp.sum(jnp.concatenate(sums, axis=0), axis=1, keepdims=True)

    blk = pl.BlockSpec((WINDOW, 1024), lambda n: (n, 0))
    smem = pl.BlockSpec(memory_space=pltpu.SMEM)
    whole = lambda shape: pl.BlockSpec(shape, lambda n: (0, 0))
    return pl.pallas_call(
        body, name="swa_bwd", grid=(nb,),
        in_specs=[blk, whole((S, 256)), blk, blk, blk, whole((WINDOW, WINDOW)), smem, whole((ALL_HEADS, 1))],
        out_specs=[blk, blk, whole((S, 256)), whole((SWA_HEADS, 1)), whole((REL_BUCKETS * SWA_HEADS, 1))],
        out_shape=[jax.ShapeDtypeStruct((S, 1024), BF), jax.ShapeDtypeStruct((S, 1024), BF),
                   jax.ShapeDtypeStruct((S, 256), F32), jax.ShapeDtypeStruct((SWA_HEADS, 1), F32),
                   jax.ShapeDtypeStruct((REL_BUCKETS * SWA_HEADS, 1), F32)],
        scratch_shapes=[pltpu.VMEM((SWA_HEADS, WINDOW, WINDOW), F32), pltpu.VMEM((SWA_HEADS, WINDOW, WINDOW), F32),
                        pltpu.VMEM((ALL_HEADS, 1), F32)],
        compiler_params=_params(("arbitrary",)),
    )(q, kv, g, o, dy, bucket, rel_bias, sink_col)


def _mem_probs(qh, mk):
    lg = _dot(qh, mk, NT) * (MEM_HD ** -0.5)
    e = jnp.exp(lg - jnp.max(lg, axis=-1, keepdims=True))
    return e / jnp.sum(e, axis=-1, keepdims=True)


def _mem_fwd(q, mkv, g):
    S = q.shape[0]
    M = mkv.shape[0]
    tq = 256

    def body(q_ref, mkv_ref, g_ref, o_ref, y_ref):
        for h in range(MEM_HEADS):
            c0 = h * MEM_HD
            pr = _mem_probs(q_ref[:, c0:c0 + MEM_HD], mkv_ref[:, c0:c0 + MEM_HD])
            o = _dot(pr.astype(BF), mkv_ref[:, D_MEM + c0:D_MEM + c0 + MEM_HD], NN)
            o_ref[:, c0:c0 + MEM_HD] = o
            gv = g_ref[:, c0:c0 + MEM_HD]
            y_ref[:, c0:c0 + MEM_HD] = (o * (gv * _sigmoid(gv))).astype(BF)

    blk = pl.BlockSpec((tq, D_MEM), lambda i: (i, 0))
    return pl.pallas_call(
        body, name="mem_fwd", grid=(S // tq,),
        in_specs=[blk, pl.BlockSpec((M, 2 * D_MEM), lambda i: (0, 0)), blk], out_specs=[blk, blk],
        out_shape=[jax.ShapeDtypeStruct((S, D_MEM), F32), jax.ShapeDtypeStruct((S, D_MEM), BF)],
        compiler_params=_params(("parallel",)),
    )(q, mkv, g)


def _mem_bwd(q, mkv, g, o, dy):
    S = q.shape[0]
    M = mkv.shape[0]
    tq = 256

    def body(q_ref, mkv_ref, g_ref, o_ref, dy_ref, dq_ref, dg_ref, dmkv_ref):
        @pl.when(pl.program_id(0) == 0)
        def _():
            dmkv_ref[...] = jnp.zeros_like(dmkv_ref)

        for h in range(MEM_HEADS):
            c0 = h * MEM_HD
            qh = q_ref[:, c0:c0 + MEM_HD]
            mk = mkv_ref[:, c0:c0 + MEM_HD]
            mv = mkv_ref[:, D_MEM + c0:D_MEM + c0 + MEM_HD]
            gv = g_ref[:, c0:c0 + MEM_HD]
            sg = _sigmoid(gv)
            dyv = dy_ref[:, c0:c0 + MEM_HD]
            dg_ref[:, c0:c0 + MEM_HD] = (dyv * o_ref[:, c0:c0 + MEM_HD] * (sg * (1.0 + gv * (1.0 - sg)))).astype(BF)
            do = (dyv * (gv * sg)).astype(BF)
            pr = _mem_probs(qh, mk)
            dp = _dot(do, mv, NT)
            ds = pr * (dp - jnp.sum(pr * dp, axis=-1, keepdims=True))
            dsb = (ds * (MEM_HD ** -0.5)).astype(BF)
            dq_ref[:, c0:c0 + MEM_HD] = _dot(dsb, mk, NN).astype(BF)
            dmkv_ref[:, c0:c0 + MEM_HD] += _dot(dsb, qh, TN)
            dmkv_ref[:, D_MEM + c0:D_MEM + c0 + MEM_HD] += _dot(pr.astype(BF), do, TN)

    blk = pl.BlockSpec((tq, D_MEM), lambda i: (i, 0))
    whole = pl.BlockSpec((M, 2 * D_MEM), lambda i: (0, 0))
    return pl.pallas_call(
        body, name="mem_bwd", grid=(S // tq,),
        in_specs=[blk, whole, blk, blk, blk], out_specs=[blk, blk, whole],
        out_shape=[jax.ShapeDtypeStruct((S, D_MEM), BF), jax.ShapeDtypeStruct((S, D_MEM), BF),
                   jax.ShapeDtypeStruct((M, 2 * D_MEM), F32)],
        compiler_params=_params(("arbitrary",)),
    )(q, mkv, g, o, dy)


MERGE_TN = 512


def _merge_specs(tm):
    ytile = pl.BlockSpec((tm, 1024), lambda i, j: (i, 0))
    wblk = pl.BlockSpec((1024, MERGE_TN), lambda i, j: (0, j))
    gls = [pl.BlockSpec((None, tm, MERGE_TN), (lambda i, j, br=br: (br, i, j))) for br in range(3)]
    otile = pl.BlockSpec((tm, MERGE_TN), lambda i, j: (i, j))
    return ytile, wblk, gls, otile


def _merge_fwd(ys, ws, gl, tm):
    S = gl.shape[1]

    def body(y0, y1, y2, w0, w1, w2, g0, g1, g2, o_ref):
        acc = None
        for y_ref, w_ref, g_ref in ((y0, w0, g0), (y1, w1, g1), (y2, w2, g2)):
            term = _sigmoid(g_ref[...]) * _dot(y_ref[...], w_ref[...], NN)
            acc = term if acc is None else acc + term
        o_ref[...] = acc.astype(BF)

    ytile, wblk, gls, otile = _merge_specs(tm)
    return pl.pallas_call(
        body, name="merge_fwd", grid=(S // tm, D_MODEL // MERGE_TN),
        in_specs=[ytile] * 3 + [wblk] * 3 + gls, out_specs=otile,
        out_shape=jax.ShapeDtypeStruct((S, D_MODEL), BF),
        compiler_params=_params(("parallel", "arbitrary")),
    )(*ys, *ws, gl, gl, gl)


def _merge_bwd(dout, w_out, ys, ws, gl, tm):
    S = gl.shape[1]

    def body(do_ref, wo_ref, y0, y1, y2, w0, w1, w2, g0, g1, g2, dg0, dg1, dg2, dp0, dp1, dp2):
        dm = _dot(do_ref[...], wo_ref[...], NT)
        for y_ref, w_ref, g_ref, dg_ref, dp_ref in ((y0, w0, g0, dg0, dp0), (y1, w1, g1, dg1, dp1), (y2, w2, g2, dg2, dp2)):
            gate = _sigmoid(g_ref[...])
            pv = _dot(y_ref[...], w_ref[...], NN)
            dg_ref[...] = (dm * pv * gate * (1.0 - gate)).astype(BF)
            dp_ref[...] = (dm * gate).astype(BF)

    ytile, wblk, gls, otile = _merge_specs(tm)
    out = jax.ShapeDtypeStruct((S, D_MODEL), BF)
    return pl.pallas_call(
        body, name="merge_bwd", grid=(S // tm, D_MODEL // MERGE_TN),
        in_specs=[pl.BlockSpec((tm, D_MODEL), lambda i, j: (i, 0)), pl.BlockSpec((MERGE_TN, D_MODEL), lambda i, j: (j, 0))]
        + [ytile] * 3 + [wblk] * 3 + gls,
        out_specs=[otile] * 6, out_shape=[out] * 6,
        compiler_params=_params(("parallel", "arbitrary")),
    )(dout, w_out, *ys, *ws, gl, gl, gl)


def _out_loss(merged, w_out, x, target, post_g, tm):
    S = x.shape[0]

    def body(m_ref, w_ref, x_ref, t_ref, g_ref, dout_ref, dy_ref, loss_ref, dpost_ref):
        @pl.when(pl.program_id(0) == 0)
        def _():
            loss_ref[...] = jnp.zeros_like(loss_ref)
            dpost_ref[...] = jnp.zeros_like(dpost_ref)

        out = _dot(m_ref[...], w_ref[...], NN)
        r = lax.rsqrt(jnp.mean(out * out, axis=-1, keepdims=True) + EPS)
        nrm = out * r
        gv = g_ref[...]
        err = (x_ref[...] + nrm * gv) - t_ref[...]
        sq = jnp.sum(jnp.sum(err * err, axis=1, keepdims=True), axis=0, keepdims=True)
        loss_ref[...] += sq * (0.5 / D_MODEL)
        dy = err * (1.0 / D_MODEL)
        dy_ref[...] = dy
        dpost_ref[...] += jnp.sum(dy * nrm, axis=0, keepdims=True)
        dn = dy * gv
        dout_ref[...] = (r * (dn - nrm * jnp.mean(dn * nrm, axis=-1, keepdims=True))).astype(BF)

    row = pl.BlockSpec((tm, D_MODEL), lambda i: (i, 0))
    return pl.pallas_call(
        body, name="out_loss", grid=(S // tm,),
        in_specs=[row, pl.BlockSpec((D_MODEL, D_MODEL), lambda i: (0, 0)), row, row, pl.BlockSpec((1, D_MODEL), lambda i: (0, 0))],
        out_specs=[row, row, pl.BlockSpec((8, 128), lambda i: (0, 0)), pl.BlockSpec((1, D_MODEL), lambda i: (0, 0))],
        out_shape=[jax.ShapeDtypeStruct((S, D_MODEL), BF), jax.ShapeDtypeStruct((S, D_MODEL), F32),
                   jax.ShapeDtypeStruct((8, 128), F32), jax.ShapeDtypeStruct((1, D_MODEL), F32)],
        compiler_params=_params(("arbitrary",)),
    )(merged, w_out, x, target, post_g)


def _dh_dx(dproj, w_in, x, dy, pre_g, tm, tk):
    S = x.shape[0]
    nk = D_IN // tk

    def body(dp_ref, w_ref, x_ref, dy_ref, g_ref, dx_ref, dpre_ref, acc_ref):
        i, k = pl.program_id(0), pl.program_id(1)

        @pl.when(jnp.logical_and(i == 0, k == 0))
        def _():
            dpre_ref[...] = jnp.zeros_like(dpre_ref)

        @pl.when(k == 0)
        def _():
            acc_ref[...] = jnp.zeros_like(acc_ref)

        acc_ref[...] += _dot(dp_ref[...], w_ref[...], NN)

        @pl.when(k == nk - 1)
        def _():
            dh = acc_ref[...]
            xv = x_ref[...]
            r = lax.rsqrt(jnp.mean(xv * xv, axis=-1, keepdims=True) + EPS)
            nrm = xv * r
            dpre_ref[...] += jnp.sum(dh * nrm, axis=0, keepdims=True)
            dn = dh * g_ref[...]
            dx_ref[...] = r * (dn - nrm * jnp.mean(dn * nrm, axis=-1, keepdims=True)) + dy_ref[...]

    row = pl.BlockSpec((tm, D_MODEL), lambda i, k: (i, 0))
    vec = pl.BlockSpec((1, D_MODEL), lambda i, k: (0, 0))
    return pl.pallas_call(
        body, name="dh_dx", grid=(S // tm, nk),
        in_specs=[pl.BlockSpec((tm, tk), lambda i, k: (i, k)), pl.BlockSpec((tk, D_MODEL), lambda i, k: (k, 0)), row, row, vec],
        out_specs=[row, vec],
        out_shape=[jax.ShapeDtypeStruct((S, D_MODEL), F32), jax.ShapeDtypeStruct((1, D_MODEL), F32)],
        scratch_shapes=[pltpu.VMEM((tm, D_MODEL), F32)],
        compiler_params=_params(("arbitrary", "arbitrary"), 56),
    )(dproj, w_in, x, dy, pre_g)


def _sum_parts(parts, name):
    P, R, C = parts.shape
    tr = max(t for t in range(8, 513, 8) if R % t == 0)

    def body(p_ref, o_ref):
        acc = p_ref[0]
        for j in range(1, P):
            acc = acc + p_ref[j]
        o_ref[...] = acc

    return pl.pallas_call(
        body, name=name, grid=(R // tr,),
        in_specs=[pl.BlockSpec((P, tr, C), lambda i: (0, i, 0))], out_specs=pl.BlockSpec((tr, C), lambda i: (i, 0)),
        out_shape=jax.ShapeDtypeStruct((R, C), F32), compiler_params=_params(("parallel",)),
    )(parts)


def _adamw(parts, w, m, v, name):
    groups = list(parts) if isinstance(parts, (list, tuple)) else [parts]
    P, rows, C = groups[0].shape
    R = rows * len(groups)
    tr = 128 if rows % 128 == 0 else rows
    per = rows // tr
    c1 = 1.0 - ADAM_B1 ** ADAM_STEP
    c2 = 1.0 - ADAM_B2 ** ADAM_STEP

    def body(*refs):
        p_refs = refs[:len(groups)]
        w_ref, m_ref, v_ref, g_ref, d_ref, nm_ref, nv_ref = refs[len(groups):]
        g = None
        for q, p_ref in enumerate(p_refs):
            gq = p_ref[0].astype(F32)
            for j in range(1, P):
                gq = gq + p_ref[j].astype(F32)
            g = gq if g is None else jnp.where(pl.program_id(0) // per == q, gq, g)
        nm = ADAM_B1 * m_ref[...] + (1.0 - ADAM_B1) * g
        nv = ADAM_B2 * v_ref[...] + (1.0 - ADAM_B2) * (g * g)
        g_ref[...] = g
        nm_ref[...] = nm
        nv_ref[...] = nv
        d_ref[...] = -ADAM_LR * ((nm / c1) / (jnp.sqrt(nv / c2) + ADAM_EPS) + ADAM_WD * w_ref[...])

    tile = pl.BlockSpec((tr, C), lambda i: (i, 0))
    out = jax.ShapeDtypeStruct((R, C), F32)
    return pl.pallas_call(
        body, name=name, grid=(R // tr,),
        in_specs=[pl.BlockSpec((P, tr, C), (lambda i, q=q: (0, jnp.clip(i - q * per, 0, per - 1), 0))) for q in range(len(groups))]
        + [tile, tile, tile], out_specs=[tile] * 4, out_shape=[out] * 4,
        compiler_params=_params(("parallel",)),
    )(*groups, w, m, v)


def _forward_a(x, mem, pre_g, mem_g, w_in, conv_w, conv_b, w_a, b_a, w_x, b_x, lam, sinks, rel_bias):
    S = x.shape[0]
    st = dict(T=min(512, S // 2), tm=min(512, S), bucket=_rel_bucket_map())
    st["h"] = _rms_fwd(x, pre_g, "pre_norm")
    st["memn"] = _rms_fwd(mem, mem_g, "mem_norm")
    seg = {}
    for name, c0, width, dt in SEGMENTS:
        seg[name] = _matmul(st["h"], w_in, "nt", S, width, D_MODEL, S, SEG_TILE, D_MODEL, dt, "proj_" + name, b_noff=c0 // SEG_TILE,
                            out_blocked="third" if name == "gl" else None)
    st["seg"] = seg
    st["h_rg"], st["y_rg"] = _rglru_fwd(seg["xr"], seg["g_rg"], conv_w, conv_b, w_a, b_a, w_x, b_x, lam, st["T"])
    st["o_swa"], st["y_swa"] = _swa_fwd(seg["q_s"], seg["kv"], seg["g_swa"], st["bucket"], rel_bias, _sink_column(sinks))
    return st


def _forward_b(st, x, target, post_g, w_memkv, wbr, w_out):
    S = x.shape[0]
    M = st["memn"].shape[0]
    seg = st["seg"]
    st["mkv"] = _matmul(st["memn"], w_memkv, "nn", M, 2 * D_MEM, D_MODEL, M, 512, D_MODEL, BF, "mem_kv")
    st["o_mem"], st["y_mem"] = _mem_fwd(seg["q_m"], st["mkv"], seg["g_mem"])
    st["ys"] = (st["y_rg"], st["y_swa"], st["y_mem"])
    st["merged"] = _merge_fwd(st["ys"], wbr, seg["gl"], st["tm"])
    st["dout"], st["dy"], st["loss"], st["dpost"] = _out_loss(st["merged"], w_out, x, target, post_g, min(256, S))
    return st


def _backward_a(st, mem, w_memkv, wbr, w_out, conv_w, conv_b, w_a, b_a, w_x, b_x, lam):
    S = st["h"].shape[0]
    M = mem.shape[0]
    seg, ys, tm = st["seg"], st["ys"], st["tm"]
    st["dw_out"] = _matmul(st["merged"], st["dout"], "tn", D_MODEL, D_MODEL, S, 256, D_MODEL, S, BF, "dw_out", out_blocked="row")
    dgl0, dgl1, dgl2, dp0, dp1, dp2 = _merge_bwd(st["dout"], w_out, ys, wbr, seg["gl"], tm)
    st["dgl"] = (dgl0, dgl1, dgl2)
    dys, dwbr = [], []
    for i, dp in enumerate((dp0, dp1, dp2)):
        dys.append(_matmul(dp, wbr[i], "nt", S, 1024, D_MODEL, tm, 1024, D_MODEL, F32, "dy_br%d" % i))
        dwbr.append(_matmul(ys[i], dp, "tn", 1024, D_MODEL, S, 1024, 256, S, BF, "dw_br%d" % i, out_blocked="col"))
    st["dys"], st["dwbr"] = dys, dwbr
    st["dq_m"], st["dg_mem"], dmkv = _mem_bwd(seg["q_m"], st["mkv"], seg["g_mem"], st["o_mem"], dys[2])
    dmkv_b = dmkv.astype(BF)
    st["dw_memkv"] = _matmul(st["memn"], dmkv_b, "tn", D_MODEL, 2 * D_MEM, M, 256, 2 * D_MEM, M, BF, "dw_memkv", out_blocked="row")
    dmemn = _matmul(dmkv_b, w_memkv, "nt", M, D_MODEL, 2 * D_MEM, M, 512, 2 * D_MEM, F32, "dmemn")
    st["dmem_g"] = _rms_gain_grad(dmemn, mem, "dmem_gain")
    st["dxr"], st["dg_rg"], st["dw_a"], st["dw_x"], st["dvec"] = _rglru_bwd(
        seg["xr"], seg["g_rg"], st["h_rg"], dys[0], conv_w, conv_b, w_a, b_a, w_x, b_x, lam, st["T"])
    return st


def _backward_b(st, rel_bias, sinks):
    seg = st["seg"]
    dq_s, dg_swa, dkv, dsinks, drel = _swa_bwd(seg["q_s"], seg["kv"], seg["g_swa"], st["o_swa"], st["dys"][1],
                                               st["bucket"], rel_bias, _sink_column(sinks))
    st["dsinks"], st["drel"] = dsinks.reshape(1, SWA_HEADS), drel.reshape(REL_BUCKETS, SWA_HEADS)
    st["dproj"] = jnp.concatenate([st["dxr"], st["dg_rg"], dq_s, dkv.astype(BF), dg_swa, st["dq_m"], st["dg_mem"], *st["dgl"]], axis=1)
    return st


def _dw_in_half(st, half, dep=None):
    S = st["h"].shape[0]
    return _matmul(st["h"], st["dproj"], "tn", D_MODEL // 2, D_IN, S, 512, 1792, S, BF, "dw_in%d" % half, a_moff=2 * half, dep=dep)


def _owner_blocks(a):
    return jnp.swapaxes(a.reshape((4, 2) + a.shape[1:]), 0, 1)


def _local_step(x, mem, target, pre_g, post_g, mem_g, w_in, conv_w, conv_b, w_a, b_a, w_x, b_x, lam, sinks, rel_bias,
                w_memkv, wbr, w_out):
    st = _forward_a(x, mem, pre_g, mem_g, w_in, conv_w, conv_b, w_a, b_a, w_x, b_x, lam, sinks, rel_bias)
    st = _forward_b(st, x, target, post_g, w_memkv, wbr, w_out)
    st = _backward_a(st, mem, w_memkv, wbr, w_out, conv_w, conv_b, w_a, b_a, w_x, b_x, lam)
    st = _backward_b(st, rel_bias, sinks)
    st["dw_in"] = jnp.concatenate([_dw_in_half(st, 0), _dw_in_half(st, 1)], axis=0)
    st["grad_x"], st["dpre"] = _dh_dx(st["dproj"], w_in, x, st["dy"], pre_g, st["tm"], 896)
    return st


def _pad_rows(a, rows):
    a = a.reshape(-1, 128) if a.shape[-1] % 128 == 0 else jnp.pad(a, ((0, 0), (0, 128 - a.shape[-1])))
    return jnp.pad(a, ((0, rows - a.shape[0]), (0, 0))) if a.shape[0] < rows else a


def kernel(x, mem, pre_norm_g, post_norm_g, mem_norm_g, w_in, conv_w, conv_b, w_rg_a, b_rg_a, w_rg_x, b_rg_x, lru_lambda, swa_sinks, rel_bias, w_mem_kv, w_br_rg, w_br_swa, w_br_mem, w_out, loss_target, m_pre_norm_g, m_post_norm_g, m_mem_norm_g, m_w_in, m_conv_w, m_conv_b, m_w_rg_a, m_b_rg_a, m_w_rg_x, m_b_rg_x, m_lru_lambda, m_swa_sinks, m_rel_bias, m_w_mem_kv, m_w_br_rg, m_w_br_swa, m_w_br_mem, m_w_out, v_pre_norm_g, v_post_norm_g, v_mem_norm_g, v_w_in, v_conv_w, v_conv_b, v_w_rg_a, v_b_rg_a, v_w_rg_x, v_b_rg_x, v_lru_lambda, v_swa_sinks, v_rel_bias, v_w_mem_kv, v_w_br_rg, v_w_br_swa, v_w_br_mem, v_w_out):
    cx, cy, cc = lax.axis_index("x"), lax.axis_index("y"), lax.axis_index("c")
    me = 4 * cx + 2 * cy + cc
    chip = 2 * cx + cy
    core = jnp.reshape(cc, (1,)).astype(jnp.int32)
    x0, mem0 = x[0], mem[0]
    w_a_b, w_x_b = w_rg_a[0].astype(BF), w_rg_x[0].astype(BF)

    def landing(own, slot, slots, kind="lead"):
        if kind == "cols":
            return lax.dynamic_update_slice(lax.empty((own.shape[0], slots * own.shape[1]), own.dtype), own, (0, slot * own.shape[1]))
        return lax.dynamic_update_slice(lax.empty((slots,) + own.shape, own.dtype), own[None], (slot,) + (0,) * own.ndim)

    def scatter_start(parts, tag):
        got = _swap_with_sibling(parts, "scatter_%s_swap" % tag)
        sums = [_pair_sum(p, g, core, "scatter_%s_sum%d" % (tag, i)) for i, (p, g) in enumerate(zip(parts, got))]
        lands = [landing(lax.dynamic_index_in_dim(s, chip, 0, keepdims=False), chip, 4) for s in sums]
        return _exchange_start(sums, lands, _plan_scatter(len(sums)), "scatter_%s_start" % tag)

    g_in, g_cw = _all_gather([jnp.transpose(w_in[0]).astype(BF), conv_w[0]], "gather_w_in")
    w_in_f = g_in.reshape(D_IN, D_MODEL)
    conv_w_f = jnp.transpose(g_cw, (1, 0, 2)).reshape(CONV_W, D_RNN)

    after_first = jnp.minimum(jnp.abs(g_cw[0, 0, 0]), 0.0).astype(BF)
    rest = [w.astype(BF) + after_first for w in (w_mem_kv[0], w_br_rg[0], w_br_swa[0], w_br_mem[0], w_out[0])]
    kinds = ["lead", "cols", "cols", "cols", "lead"]
    plan_g = _plan_gather(kinds)
    g_send, g_recv, g_src, g_land, g_token = _exchange_start(rest, [landing(w, me, N_DEV, kd) for w, kd in zip(rest, kinds)], plan_g,
                                                             "gather_rest_start")
    st = _forward_a(x0, mem0, pre_norm_g + g_token[0:1, 0:1], mem_norm_g, w_in_f, conv_w_f, conv_b, w_a_b, b_rg_a, w_x_b, b_rg_x,
                    lru_lambda, swa_sinks, rel_bias)
    g_land = _exchange_wait(g_send, g_recv, g_src, g_land, plan_g, st["y_swa"], "gather_rest_wait")
    g_land = _forward_to_sibling(g_land, kinds, "gather_rest_forward")
    w_memkv_f = g_land[0].reshape(D_MODEL, 2 * D_MEM)
    wbr = (g_land[1], g_land[2], g_land[3])
    w_out_f = g_land[4].reshape(D_MODEL, D_MODEL)

    st = _forward_b(st, x0, loss_target[0], post_norm_g, w_memkv_f, wbr, w_out_f)
    st = _backward_a(st, mem0, w_memkv_f, wbr, w_out_f, conv_w_f, conv_b, w_a_b, b_rg_a, w_x_b, b_rg_x, lru_lambda)
    parts_a = [st["dw_memkv"], st["dwbr"][0], st["dwbr"][1], st["dwbr"][2], st["dw_out"],
               _owner_blocks(st["dw_a"]), _owner_blocks(st["dw_x"])]
    plan_a = _plan_scatter(len(parts_a))
    a_send, a_recv, a_src, a_land, a_token = scatter_start(parts_a, "a")

    st = _backward_b(st, rel_bias, swa_sinks + a_token[0:1, 0:1])
    plan_b = _plan_scatter(1)
    halves, dep = [], None
    for half in range(2):
        dwh = _dw_in_half(st, half, dep)
        parts_b = [jnp.transpose(dwh.reshape(D_MODEL // 2, 4, 2, D_IN // N_DEV), (2, 1, 0, 3))]
        halves.append(scatter_start(parts_b, "b%d" % half))
        dep = halves[-1][4]
    grad_x, dpre = _dh_dx(st["dproj"], w_in_f, x0, st["dy"], pre_norm_g + dep[0:1, 0:1], st["tm"], 896)
    a_land = _exchange_wait(a_send, a_recv, a_src, a_land, plan_a, grad_x, "scatter_a_wait")
    g_wa_blk = _sum_parts(a_land[5], "sum_w_rg_a")
    g_wx_blk = _sum_parts(a_land[6], "sum_w_rg_x")
    big = [None]
    for j, (wt, mt, vt) in enumerate(((w_mem_kv, m_w_mem_kv, v_w_mem_kv), (w_br_rg, m_w_br_rg, v_w_br_rg),
                                      (w_br_swa, m_w_br_swa, v_w_br_swa), (w_br_mem, m_w_br_mem, v_w_br_mem), (w_out, m_w_out, v_w_out))):
        big.append([a[None] for a in _adamw(a_land[j], wt[0], mt[0], vt[0], "adamw_big%d" % (j + 1))])
    after, b_lands = big[5][1], []
    for half, (b_send, b_recv, b_src, b_land, _) in enumerate(halves):
        b_lands.append(_exchange_wait(b_send, b_recv, b_src, b_land, plan_b, after, "scatter_b%d_wait" % half)[0])
        after = b_lands[-1]
    big[0] = [a[None] for a in _adamw(b_lands, w_in[0], m_w_in[0], v_w_in[0], "adamw_big0")]
    links_free = jnp.minimum(jnp.abs(big[0][0][0, 0, 0]), 0.0)

    pack = jnp.concatenate([dpre.reshape(16, 128), st["dpost"].reshape(16, 128), st["dmem_g"].reshape(16, 128),
                            st["dvec"].reshape(64, 128), _pad_rows(st["dsinks"], 8), _pad_rows(st["drel"], 32), g_wa_blk, g_wx_blk], axis=0) + links_free
    gathered = _all_gather([pack], "gather_small")[0]
    gs = _sum_parts(gathered, "sum_small")
    g_pre, g_post, g_memg = gs[0:16].reshape(1, D_MODEL), gs[16:32].reshape(1, D_MODEL), gs[32:48].reshape(1, D_MODEL)
    gvec = gs[48:112].reshape(8, D_RNN)
    g_conv_w = lax.dynamic_slice(gvec[0:CONV_W], (0, me * RNN_BLOCK), (CONV_W, RNN_BLOCK))
    g_conv_b, g_b_a, g_b_x, g_lam = gvec[4:5], gvec[5:6], gvec[6:7], gvec[7:8]
    g_sinks = gs[112:113, :SWA_HEADS]
    g_rel = gs[120:152, :SWA_HEADS]
    g_w_a = gathered[:, 152:280]
    g_w_x = gathered[:, 280:408]

    def packed(ts):
        pre, post, memg, cb, ba, bx, lm, wa, wx, sk, rel, cw = ts
        return jnp.concatenate([pre.reshape(16, 128), post.reshape(16, 128), memg.reshape(16, 128), cb.reshape(8, 128),
                                ba.reshape(8, 128), bx.reshape(8, 128), lm.reshape(8, 128), wa.reshape(1024, 128),
                                wx.reshape(1024, 128), _pad_rows(sk.reshape(1, SWA_HEADS), 8), _pad_rows(rel, 32),
                                _pad_rows(cw.reshape(CONV_W, RNN_BLOCK), 8)], axis=0)

    def unpacked(a):
        return (a[0:16].reshape(1, D_MODEL), a[16:32].reshape(1, D_MODEL), a[32:48].reshape(1, D_MODEL), a[48:56].reshape(1, D_RNN),
                a[56:64].reshape(1, D_RNN), a[64:72].reshape(1, D_RNN), a[72:80].reshape(1, D_RNN),
                a[80:1104].reshape(1, RNN_BLOCKS, RNN_BLOCK, RNN_BLOCK), a[1104:2128].reshape(1, RNN_BLOCKS, RNN_BLOCK, RNN_BLOCK),
                a[2128:2129, :SWA_HEADS], a[2136:2168, :SWA_HEADS], a[2168:2172].reshape(1, CONV_W, RNN_BLOCK))

    g_small = (g_pre, g_post, g_memg, g_conv_b, g_b_a, g_b_x, g_lam, g_w_a, g_w_x, g_sinks, g_rel, g_conv_w)
    w_small = (pre_norm_g, post_norm_g, mem_norm_g, conv_b, b_rg_a, b_rg_x, lru_lambda, w_rg_a, w_rg_x, swa_sinks, rel_bias, conv_w)
    m_small = (m_pre_norm_g, m_post_norm_g, m_mem_norm_g, m_conv_b, m_b_rg_a, m_b_rg_x, m_lru_lambda, m_w_rg_a, m_w_rg_x, m_swa_sinks, m_rel_bias, m_conv_w)
    v_small = (v_pre_norm_g, v_post_norm_g, v_mem_norm_g, v_conv_b, v_b_rg_a, v_b_rg_x, v_lru_lambda, v_w_rg_a, v_w_rg_x, v_swa_sinks, v_rel_bias, v_conv_w)
    sm = [unpacked(a) for a in _adamw(packed(g_small)[None], packed(w_small), packed(m_small), packed(v_small), "adamw_small")]


    loss_total = lax.psum(st["loss"][0, 0], AXES)

    def leaves(k):
        s = sm[k]
        return [s[0], s[1], s[2], big[0][k], s[11], s[3], s[7], s[4], s[8], s[5], s[6], s[9], s[10],
                big[1][k], big[2][k], big[3][k], big[4][k], big[5][k]]

    return (loss_total, grad_x[None], *leaves(0), *leaves(1), *leaves(2), *leaves(3))
```

```python
import math

import jax
import jax.numpy as jnp
import numpy as np
from jax import lax
from jax.experimental import pallas as pl
from jax.experimental.pallas import tpu as pltpu

F32, BF = jnp.float32, jnp.bfloat16
MESH = pl.DeviceIdType.MESH
AXES = ("x", "y", "c")
N_DEV = 8

D_MODEL = 2048
D_RNN = 1024
RNN_BLOCKS = 8
RNN_BLOCK = 128
CONV_W = 4
LRU_C = 8.0
SWA_HEADS = 16
SWA_KV_HEADS = 2
SWA_HD = 64
WINDOW = 128
MEM_HEADS = 4
MEM_HD = 256
D_MEM = 1024
REL_BUCKETS = 32
REL_MAX_DIST = 128
EPS = 1e-6
NEG_INF = -1e30
D_IN = 12544
SEGMENTS = (("xr", 0, 1024, F32), ("g_rg", 1024, 1024, F32), ("q_s", 2048, 1024, BF), ("kv", 3072, 256, BF),
            ("g_swa", 3328, 1024, F32), ("q_m", 4352, 1024, BF), ("g_mem", 5376, 1024, F32), ("gl", 6400, 6144, F32))
SEG_TILE = 256
K_HALF = D_MODEL // 2

ADAM_LR, ADAM_B1, ADAM_B2, ADAM_EPS, ADAM_WD, ADAM_STEP = 0.001, 0.9, 0.999, 1e-08, 0.01, 10

NN = (((1,), (0,)), ((), ()))
NT = (((1,), (1,)), ((), ()))
TN = (((0,), (0,)), ((), ()))
MIB = 2 ** 20


def _dot(a, b, dn):
    return lax.dot_general(a, b, dn, preferred_element_type=F32)


def _params(sem, vmem_mib=48):
    return pltpu.CompilerParams(dimension_semantics=sem, vmem_limit_bytes=vmem_mib * MIB)


def _sigmoid(z):
    return 1.0 / (1.0 + jnp.exp(-z))


def _softplus(z):
    return jnp.maximum(z, 0.0) + jnp.log(1.0 + jnp.exp(-jnp.abs(z)))


def _expm1(z):
    p = z * (1.0 + z * (0.5 + z * (1.0 / 6 + z * (1.0 / 24 + z * (1.0 / 120 + z * (1.0 / 720 + z * (1.0 / 5040 + z / 40320)))))))
    return jnp.where(jnp.abs(z) < 0.3, p, jnp.exp(z) - 1.0)


def _flat(p):
    return 4 * p[0] + 2 * p[1] + p[2]


def _all_gather(arrs, name):
    n = len(arrs)

    def body(*refs):
        ins, outs = refs[:n], refs[n:2 * n]
        send_sems, recv_sems, local_sems = refs[2 * n:]
        x, y, c = lax.axis_index("x"), lax.axis_index("y"), lax.axis_index("c")
        me, sibling = (x, y, c), (x, y, 1 - c)
        chips = [(1 - x, y), (x, 1 - y), (1 - x, 1 - y)]

        def copy(a, k, block, to, src=None):
            dst = outs[a].at[_flat(block)]
            return pltpu.make_async_remote_copy(src_ref=dst if src is None else src, dst_ref=dst,
                                                send_sem=send_sems.at[a * 7 + k], recv_sem=recv_sems.at[a * 7 + k],
                                                device_id=to, device_id_type=MESH)

        mine = [pltpu.make_async_copy(ins[a], outs[a].at[_flat(me)], local_sems.at[a]) for a in range(n)]
        for cp in mine:
            cp.start()
        first = []
        for a in range(n):
            first += [copy(a, 1 + j, me, (*chip, c), src=ins[a]) for j, chip in enumerate(chips)]
            first.append(copy(a, 0, me, sibling, src=ins[a]))
        for cp in first:
            cp.start()
        passed = []
        for j, chip in enumerate(chips):
            for a in range(n):
                copy(a, 1 + j, (*chip, c), me).wait_recv()
                fw = copy(a, 4 + j, (*chip, c), sibling)
                fw.start()
                passed.append(fw)
        for a in range(n):
            copy(a, 0, sibling, me).wait_recv()
            for j, chip in enumerate(chips):
                copy(a, 4 + j, (*chip, 1 - c), me).wait_recv()
        for cp in first + passed:
            cp.wait_send()
        for cp in mine:
            cp.wait()

    any_spec = pl.BlockSpec(memory_space=pl.ANY)
    return pl.pallas_call(
        body, name=name,
        out_shape=[jax.ShapeDtypeStruct((N_DEV,) + a.shape, a.dtype) for a in arrs],
        in_specs=[any_spec] * n, out_specs=[any_spec] * n,
        scratch_shapes=[pltpu.SemaphoreType.DMA((7 * n,)), pltpu.SemaphoreType.DMA((7 * n,)), pltpu.SemaphoreType.DMA((n,))],
    )(*arrs)


def _chip_peers(x, y):
    return [(1 - x, y), (x, 1 - y), (1 - x, 1 - y)]


def _chip(p):
    return 2 * p[0] + p[1]


def _plan_gather(kinds):
    def plan(x, y, c):
        out = []
        for a, kind in enumerate(kinds):
            for peer in [(x, y, 1 - c)] + [(*ch, c) for ch in _chip_peers(x, y)]:
                out.append((a, None, (kind, _flat((x, y, c))), peer, (kind, _flat(peer))))
        return out
    return plan


def _slot(ref, where):
    kind, k = where
    if kind == "lead":
        return ref.at[k]
    return ref.at[:, pl.ds(pl.multiple_of(k * 256, 256), 256)]


def _plan_scatter(n):
    def plan(x, y, c):
        out = []
        for a in range(n):
            for ch in _chip_peers(x, y):
                out.append((a, _chip(ch), ("lead", _chip((x, y))), (*ch, c), ("lead", _chip(ch))))
        return out
    return plan


HBM_SPEC = pl.BlockSpec(memory_space=pltpu.HBM)
SEM_SPEC = pl.BlockSpec(memory_space=pltpu.SEMAPHORE)


def _in_hbm(a):
    return pltpu.with_memory_space_constraint(a, pltpu.HBM)


def _exchange_start(srcs, lands, plan, name):
    n = len(srcs)
    count = len(plan(0, 0, 0))

    def body(*refs):
        src_refs, land_refs = refs[:n], refs[n:2 * n]
        send_sems, recv_sems = refs[2 * n], refs[2 * n + 1]
        token = refs[-1]
        x, y, c = lax.axis_index("x"), lax.axis_index("y"), lax.axis_index("c")
        for k, (a, si, di, peer, _) in enumerate(plan(x, y, c)):
            src = src_refs[a] if si is None else src_refs[a].at[si]
            pltpu.make_async_remote_copy(src_ref=src, dst_ref=_slot(land_refs[a], di), send_sem=send_sems.at[k],
                                         recv_sem=recv_sems.at[k], device_id=peer, device_id_type=MESH).start()
        token[...] = jnp.zeros_like(token)

    out = pl.pallas_call(
        body, name=name,
        out_shape=(pltpu.SemaphoreType.DMA((count,)), pltpu.SemaphoreType.DMA((count,)),
                   *[pltpu.HBM(a.shape, a.dtype) for a in srcs], *[pltpu.HBM(a.shape, a.dtype) for a in lands],
                   jax.ShapeDtypeStruct((8, 128), F32)),
        in_specs=[HBM_SPEC] * (2 * n),
        out_specs=(SEM_SPEC, SEM_SPEC, *([HBM_SPEC] * (2 * n)), pl.BlockSpec(memory_space=pltpu.VMEM)),
        input_output_aliases={i: 2 + i for i in range(2 * n)},
        compiler_params=pltpu.CompilerParams(has_side_effects=pltpu.SideEffectType.DATAFLOW_SIDE_EFFECTING),
    )(*[_in_hbm(a) for a in srcs], *[_in_hbm(a) for a in lands])
    return out[0], out[1], list(out[2:2 + n]), list(out[2 + n:2 + 2 * n]), out[-1]


def _exchange_wait(send_sems, recv_sems, srcs, lands, plan, after, name):
    n = len(srcs)

    def body(*refs):
        src_refs, land_refs = refs[:n], refs[n:2 * n]
        send_sems, recv_sems = refs[2 * n], refs[2 * n + 1]
        x, y, c = lax.axis_index("x"), lax.axis_index("y"), lax.axis_index("c")
        for k, (a, si, _, peer, ri) in enumerate(plan(x, y, c)):
            src = src_refs[a] if si is None else src_refs[a].at[si]
            cp = pltpu.make_async_remote_copy(src_ref=src, dst_ref=_slot(land_refs[a], ri), send_sem=send_sems.at[k],
                                              recv_sem=recv_sems.at[k], device_id=peer, device_id_type=MESH)
            cp.wait_send()
            cp.wait_recv()

    out = pl.pallas_call(
        body, name=name,
        out_shape=(*[pltpu.HBM(a.shape, a.dtype) for a in srcs], *[pltpu.HBM(a.shape, a.dtype) for a in lands]),
        in_specs=[HBM_SPEC] * (2 * n) + [SEM_SPEC, SEM_SPEC, pl.BlockSpec(memory_space=pl.ANY)],
        out_specs=tuple([HBM_SPEC] * (2 * n)),
        input_output_aliases={i: i for i in range(2 * n)},
        compiler_params=pltpu.CompilerParams(has_side_effects=pltpu.SideEffectType.DATAFLOW_SIDE_EFFECTING),
    )(*srcs, *lands, send_sems, recv_sems, after)
    return list(out[n:2 * n])


def _forward_to_sibling(lands, kinds, name):
    n = len(lands)

    def body(*refs):
        in_refs, out_refs = refs[:n], refs[n:2 * n]
        send_sems, recv_sems = refs[2 * n:]
        x, y, c = lax.axis_index("x"), lax.axis_index("y"), lax.axis_index("c")
        sibling = (x, y, 1 - c)

        def copy(a, j, slot):
            return pltpu.make_async_remote_copy(src_ref=_slot(in_refs[a], (kinds[a], slot)), dst_ref=_slot(out_refs[a], (kinds[a], slot)),
                                                send_sem=send_sems.at[a * 3 + j], recv_sem=recv_sems.at[a * 3 + j],
                                                device_id=sibling, device_id_type=MESH)

        sends = [copy(a, j, _flat((*ch, c))) for a in range(n) for j, ch in enumerate(_chip_peers(x, y))]
        for cp in sends:
            cp.start()
        for a in range(n):
            for j, ch in enumerate(_chip_peers(x, y)):
                copy(a, j, _flat((*ch, 1 - c))).wait_recv()
        for cp in sends:
            cp.wait_send()

    any_spec = pl.BlockSpec(memory_space=pl.ANY)
    return pl.pallas_call(
        body, name=name, out_shape=[jax.ShapeDtypeStruct(a.shape, a.dtype) for a in lands],
        in_specs=[any_spec] * n, out_specs=[any_spec] * n, input_output_aliases={a: a for a in range(n)},
        scratch_shapes=[pltpu.SemaphoreType.DMA((3 * n,)), pltpu.SemaphoreType.DMA((3 * n,))],
    )(*lands)


def _swap_with_sibling(parts, name):
    n = len(parts)

    def body(*refs):
        in_refs, out_refs = refs[:n], refs[n:2 * n]
        send_sems, recv_sems = refs[2 * n:]
        x, y, c = lax.axis_index("x"), lax.axis_index("y"), lax.axis_index("c")
        sends = [pltpu.make_async_remote_copy(src_ref=in_refs[a].at[1 - c], dst_ref=out_refs[a], send_sem=send_sems.at[a],
                                              recv_sem=recv_sems.at[a], device_id=(x, y, 1 - c), device_id_type=MESH)
                 for a in range(n)]
        for cp in sends:
            cp.start()
        for cp in sends:
            cp.wait()

    any_spec = pl.BlockSpec(memory_space=pl.ANY)
    return pl.pallas_call(
        body, name=name, out_shape=[jax.ShapeDtypeStruct(a.shape[1:], a.dtype) for a in parts],
        in_specs=[any_spec] * n, out_specs=[any_spec] * n,
        scratch_shapes=[pltpu.SemaphoreType.DMA((n,)), pltpu.SemaphoreType.DMA((n,))],
    )(*parts)


def _pair_sum(parts, got, core, name):
    _, _, R, C = parts.shape
    tr = 256 if R % 256 == 0 else R

    def body(c_ref, p_ref, g_ref, o_ref):
        o_ref[...] = (p_ref[...].astype(F32) + g_ref[...].astype(F32)).astype(o_ref.dtype)

    return pl.pallas_call(
        body, name=name,
        grid_spec=pltpu.PrefetchScalarGridSpec(
            num_scalar_prefetch=1, grid=(4, R // tr),
            in_specs=[pl.BlockSpec((None, None, tr, C), lambda j, i, c_ref: (c_ref[0], j, i, 0)),
                      pl.BlockSpec((None, tr, C), lambda j, i, c_ref: (j, i, 0))],
            out_specs=pl.BlockSpec((None, tr, C), lambda j, i, c_ref: (j, i, 0))),
        out_shape=jax.ShapeDtypeStruct((4, R, C), parts.dtype),
        compiler_params=_params(("parallel", "parallel")),
    )(core, parts, got)


def _matmul(a, b, mode, M, N, K, tm, tn, tk, out_dtype, name, b_noff=0, a_moff=0, a_koff=0, b_blocked=False, out_blocked=None,
            dep=None, addend=None, vmem_mib=48):
    nm, nn, nk = M // tm, N // tn, K // tk
    if mode == "nn":
        a_spec = pl.BlockSpec((tm, tk), lambda j, i, k: (i, k + a_koff))
        b_spec = pl.BlockSpec((tk, tn), lambda j, i, k: (k, j + b_noff))
        dn = NN
    elif mode == "nt":
        a_spec = pl.BlockSpec((tm, tk), lambda j, i, k: (i, k + a_koff))
        if b_blocked:
            b_spec = pl.BlockSpec((None, tn, tk), lambda j, i, k: (k, j, 0))
        else:
            b_spec = pl.BlockSpec((tn, tk), lambda j, i, k: (j + b_noff, k))
        dn = NT
    else:
        a_spec = pl.BlockSpec((tk, tm), lambda j, i, k: (k, i + a_moff))
        b_spec = pl.BlockSpec((tk, tn), lambda j, i, k: (k, j + b_noff))
        dn = TN
    if out_blocked == "col":
        out_shape = jax.ShapeDtypeStruct((2, 4, M, tn), out_dtype)
        out_spec = pl.BlockSpec((None, None, tm, tn), lambda j, i, k: (j % 2, j // 2, i, 0))
    elif out_blocked == "row":
        out_shape = jax.ShapeDtypeStruct((2, 4, tm, N), out_dtype)
        out_spec = pl.BlockSpec((None, None, tm, tn), lambda j, i, k: (i % 2, i // 2, 0, j))
    elif out_blocked == "third":
        out_shape = jax.ShapeDtypeStruct((3, M, N // 3), out_dtype)
        out_spec = pl.BlockSpec((None, tm, tn), lambda j, i, k: (j // (nn // 3), i, j % (nn // 3)))
    else:
        out_shape = jax.ShapeDtypeStruct((M, N), out_dtype)
        out_spec = pl.BlockSpec((tm, tn), lambda j, i, k: (i, j))

    n_extra = (addend is not None) + (dep is not None)

    def body(a_ref, b_ref, *rest):
        o_ref, scratch = rest[n_extra], rest[n_extra + 1:]
        if nk == 1:
            prod = _dot(a_ref[...], b_ref[...], dn)
            if addend is not None:
                prod = prod + rest[0][...].astype(F32)
            o_ref[...] = prod.astype(out_dtype)
        else:
            assert addend is None
            acc_ref, = scratch
            k = pl.program_id(2)

            @pl.when(k == 0)
            def _():
                acc_ref[...] = jnp.zeros_like(acc_ref)

            acc_ref[...] += _dot(a_ref[...], b_ref[...], dn)

            @pl.when(k == nk - 1)
            def _():
                o_ref[...] = acc_ref[...].astype(out_dtype)

    return pl.pallas_call(
        body, name=name, grid=(nn, nm, nk),
        in_specs=[a_spec, b_spec] + ([] if addend is None else [out_spec])
        + ([] if dep is None else [pl.BlockSpec((8, 128), lambda j, i, k: (0, 0))]),
        out_specs=out_spec, out_shape=out_shape,
        scratch_shapes=[] if nk == 1 else [pltpu.VMEM((tm, tn), F32)],
        compiler_params=_params(("parallel", "parallel", "arbitrary"), vmem_mib),
    )(a, b, *([] if addend is None else [addend]), *([] if dep is None else [dep]))


def _rms_fwd(x, g, name):
    R, Dm = x.shape
    tr = min(R, 256)

    def body(x_ref, g_ref, h_ref):
        xv = x_ref[...]
        r = lax.rsqrt(jnp.mean(xv * xv, axis=-1, keepdims=True) + EPS)
        h_ref[...] = (xv * r * g_ref[...]).astype(BF)

    return pl.pallas_call(
        body, name=name, grid=(R // tr,),
        in_specs=[pl.BlockSpec((tr, Dm), lambda i: (i, 0)), pl.BlockSpec((1, Dm), lambda i: (0, 0))],
        out_specs=pl.BlockSpec((tr, Dm), lambda i: (i, 0)), out_shape=jax.ShapeDtypeStruct((R, Dm), BF),
        compiler_params=_params(("parallel",)),
    )(x, g)


def _rms_gain_grad(dn, x, name):
    R, Dm = x.shape

    def body(dn_ref, x_ref, o_ref):
        xv = x_ref[...]
        r = lax.rsqrt(jnp.mean(xv * xv, axis=-1, keepdims=True) + EPS)
        o_ref[...] = jnp.sum(dn_ref[...] * xv * r, axis=0, keepdims=True)

    return pl.pallas_call(
        body, name=name, out_shape=jax.ShapeDtypeStruct((1, Dm), F32),
        compiler_params=pltpu.CompilerParams(vmem_limit_bytes=32 * MIB),
    )(dn, x)


def _shift_down(v, k, head8, row, T):
    if k == 0:
        return v
    r = pltpu.roll(v, k, 0)
    hr = pltpu.roll(head8, k, 0)
    top = jnp.where(row[:8] < k, hr, r[:8])
    return jnp.concatenate([top, r[8:]], axis=0)


def _shift_up(v, k, tail8, row, T):
    if k == 0:
        return v
    r = pltpu.roll(v, T - k, 0)
    tr = pltpu.roll(tail8, 8 - k, 0)
    bot = jnp.where(row[:8] >= 8 - k, tr, r[T - 8:])
    return jnp.concatenate([r[:T - 8], bot], axis=0)


def _rglru_gates(u, head8, grow, row, T, cw_ref, cb_ref, wa_ref, ba_ref, wx_ref, bx_ref, lam_ref):
    us = [_shift_down(u, k, head8, row, T) for k in range(CONV_W)]
    acc = us[0] * cw_ref[0:1, :]
    for k in range(1, CONV_W):
        acc = acc + us[k] * cw_ref[k:k + 1, :]
    conv = cb_ref[...] + acc
    cbf = conv.astype(BF)
    r_ = _sigmoid(_dot(cbf, wa_ref[0], NN) + ba_ref[...])
    i_ = _sigmoid(_dot(cbf, wx_ref[0], NN) + bx_ref[...])
    sp = _softplus(-lam_ref[...])
    la = -LRU_C * r_ * sp
    a = jnp.exp(la)
    mult_raw = jnp.sqrt(-_expm1(2.0 * la))
    mult = jnp.where(grow == 0, 1.0, mult_raw)
    return us, conv, cbf, r_, i_, sp, a, mult_raw, mult


def _rglru_specs(T, nt, rev):
    tmap = (lambda n, t: (nt - 1 - t, n)) if rev else (lambda n, t: (t, n))
    hmap = ((lambda n, t: (jnp.maximum((nt - 1 - t) * (T // 8) - 1, 0), n)) if rev
            else (lambda n, t: (jnp.maximum(t * (T // 8) - 1, 0), n)))
    tile = pl.BlockSpec((T, RNN_BLOCK), tmap)
    halo = pl.BlockSpec((8, RNN_BLOCK), hmap)
    vec = pl.BlockSpec((1, RNN_BLOCK), lambda n, t: (0, n))
    cw = pl.BlockSpec((CONV_W, RNN_BLOCK), lambda n, t: (0, n))
    wblk = pl.BlockSpec((1, RNN_BLOCK, RNN_BLOCK), lambda n, t: (n, 0, 0))
    return tile, halo, vec, cw, wblk


def _rglru_fwd(xr, g, cw, cb, wa, ba, wx, bx, lam, T):
    S = xr.shape[0]
    nt = S // T

    def body(u_ref, uh_ref, g_ref, cw_ref, cb_ref, wa_ref, ba_ref, wx_ref, bx_ref, lam_ref, h_ref, y_ref, carry):
        t = pl.program_id(1)

        @pl.when(t == 0)
        def _():
            carry[...] = jnp.zeros_like(carry)

        row = lax.broadcasted_iota(jnp.int32, (T, RNN_BLOCK), 0)
        grow = row + t * T
        head8 = jnp.where(t > 0, uh_ref[...], 0.0)
        _, conv, _, _, i_, _, a, _, mult = _rglru_gates(u_ref[...], head8, grow, row, T, cw_ref, cb_ref, wa_ref, ba_ref,
                                                         wx_ref, bx_ref, lam_ref)
        b = mult * i_ * conv
        s = 1
        while s < T:
            keep = row >= s
            a_s = jnp.where(keep, pltpu.roll(a, s, 0), 1.0)
            b_s = jnp.where(keep, pltpu.roll(b, s, 0), 0.0)
            b = a * b_s + b
            a = a * a_s
            s *= 2
        h = b + a * carry[0:1, :]
        carry[...] = jnp.broadcast_to(h[T - 1:T, :], carry.shape)
        h_ref[...] = h
        gv = g_ref[...]
        y_ref[...] = (h * (gv * _sigmoid(gv))).astype(BF)

    tile, halo, vec, cwspec, wblk = _rglru_specs(T, nt, False)
    return pl.pallas_call(
        body, name="rglru_fwd", grid=(RNN_BLOCKS, nt),
        in_specs=[tile, halo, tile, cwspec, vec, wblk, vec, wblk, vec, vec],
        out_specs=[tile, tile],
        out_shape=[jax.ShapeDtypeStruct((S, D_RNN), F32), jax.ShapeDtypeStruct((S, D_RNN), BF)],
        scratch_shapes=[pltpu.VMEM((8, RNN_BLOCK), F32)],
        compiler_params=_params(("parallel", "arbitrary")),
    )(xr, xr, g, cw, cb, wa, ba, wx, bx, lam)


def _rglru_bwd(xr, g, h, dy, cw, cb, wa, ba, wx, bx, lam, T):
    S = xr.shape[0]
    nt = S // T

    def body(u_ref, uh_ref, g_ref, h_ref, hh_ref, dy_ref, cw_ref, cb_ref, wa_ref, ba_ref, wx_ref, bx_ref, lam_ref,
             du_ref, dg_ref, dwa_ref, dwx_ref, dvec_ref, c_dhh, c_a, c_dconv):
        t = pl.program_id(1)
        tt = nt - 1 - t

        @pl.when(t == 0)
        def _():
            c_dhh[...] = jnp.zeros_like(c_dhh)
            c_a[...] = jnp.zeros_like(c_a)
            c_dconv[...] = jnp.zeros_like(c_dconv)
            dwa_ref[...] = jnp.zeros_like(dwa_ref)
            dwx_ref[...] = jnp.zeros_like(dwx_ref)
            dvec_ref[...] = jnp.zeros_like(dvec_ref)

        row = lax.broadcasted_iota(jnp.int32, (T, RNN_BLOCK), 0)
        row8 = row[:8]
        grow = row + tt * T
        head8 = jnp.where(tt > 0, uh_ref[...], 0.0)
        us, conv, cbf, r_, i_, sp, a, mult_raw, mult = _rglru_gates(
            u_ref[...], head8, grow, row, T, cw_ref, cb_ref, wa_ref, ba_ref, wx_ref, bx_ref, lam_ref)
        hv = h_ref[...]
        hprev = _shift_down(hv, 1, jnp.where(tt > 0, hh_ref[...], 0.0), row, T)
        gv = g_ref[...]
        sg = _sigmoid(gv)
        dyv = dy_ref[...]
        dg_ref[...] = (dyv * hv * (sg * (1.0 + gv * (1.0 - sg)))).astype(BF)
        d = dyv * (gv * sg)
        A = _shift_up(a, 1, c_a[...], row, T)
        s = 1
        while s < T:
            keep = row < T - s
            A_s = jnp.where(keep, pltpu.roll(A, T - s, 0), 1.0)
            d_s = jnp.where(keep, pltpu.roll(d, T - s, 0), 0.0)
            d = A * d_s + d
            A = A * A_s
            s *= 2
        dhh = d + A * c_dhh[0:1, :]
        da = dhh * hprev
        dconv = dhh * mult * i_
        di = dhh * mult * conv
        dmult = dhh * i_ * conv
        dla = da * a - jnp.where(grow == 0, 0.0, dmult * (a * a) / mult_raw)
        dr = dla * (-LRU_C * sp)
        dsp = jnp.sum(dla * (-LRU_C * r_), axis=0, keepdims=True)
        dza = dr * r_ * (1.0 - r_)
        dzx = di * i_ * (1.0 - i_)
        dza_b, dzx_b = dza.astype(BF), dzx.astype(BF)
        dconv = dconv + _dot(dza_b, wa_ref[0], NT) + _dot(dzx_b, wx_ref[0], NT)
        dwa_ref[0] += _dot(cbf, dza_b, TN)
        dwx_ref[0] += _dot(cbf, dzx_b, TN)
        lam = lam_ref[...]
        rows = [jnp.sum(dconv * us[k], axis=0, keepdims=True) for k in range(CONV_W)]
        rows += [jnp.sum(dconv, axis=0, keepdims=True), jnp.sum(dza, axis=0, keepdims=True),
                 jnp.sum(dzx, axis=0, keepdims=True), dsp * (-_sigmoid(-lam))]
        upd = jnp.zeros((8, RNN_BLOCK), F32)
        for j, rv in enumerate(rows):
            upd = upd + jnp.where(row8 == j, rv, 0.0)
        dvec_ref[...] += upd
        tail8 = c_dconv[...]
        du = dconv * cw_ref[0:1, :]
        for k in range(1, CONV_W):
            du = du + _shift_up(dconv, k, tail8, row, T) * cw_ref[k:k + 1, :]
        du_ref[...] = du.astype(BF)
        c_dhh[...] = jnp.broadcast_to(dhh[0:1, :], c_dhh.shape)
        c_a[...] = jnp.broadcast_to(a[0:1, :], c_a.shape)
        c_dconv[...] = dconv[:8]

    tile, halo, vec, cwspec, wblk = _rglru_specs(T, nt, True)
    acc8 = pl.BlockSpec((8, RNN_BLOCK), lambda n, t: (0, n))
    return pl.pallas_call(
        body, name="rglru_bwd", grid=(RNN_BLOCKS, nt),
        in_specs=[tile, halo, tile, tile, halo, tile, cwspec, vec, wblk, vec, wblk, vec, vec],
        out_specs=[tile, tile, wblk, wblk, acc8],
        out_shape=[jax.ShapeDtypeStruct((S, D_RNN), BF), jax.ShapeDtypeStruct((S, D_RNN), BF),
                   jax.ShapeDtypeStruct((RNN_BLOCKS, RNN_BLOCK, RNN_BLOCK), F32),
                   jax.ShapeDtypeStruct((RNN_BLOCKS, RNN_BLOCK, RNN_BLOCK), F32),
                   jax.ShapeDtypeStruct((8, D_RNN), F32)],
        scratch_shapes=[pltpu.VMEM((8, RNN_BLOCK), F32)] * 3,
        compiler_params=_params(("parallel", "arbitrary")),
    )(xr, xr, g, h, h, dy, cw, cb, wa, ba, wx, bx, lam)


def _rel_bucket_map():
    qi = np.arange(WINDOW)[:, None]
    kj = np.arange(2 * WINDOW)[None, :]
    dist = jnp.asarray(qi + WINDOW - kj, jnp.int32)
    n = jnp.maximum(dist, 0)
    max_exact = REL_BUCKETS // 2
    ratio = jnp.log(jnp.maximum(n, 1).astype(F32) / max_exact) / math.log(REL_MAX_DIST / max_exact)
    large = jnp.minimum(max_exact + (ratio * (REL_BUCKETS - max_exact)).astype(jnp.int32), REL_BUCKETS - 1)
    bucket = jnp.where(n < max_exact, n, large).astype(jnp.int32)
    j = np.arange(WINDOW)[None, :]
    return jnp.where(jnp.asarray(j > qi), bucket[:, :WINDOW], bucket[:, WINDOW:])


def _swa_common(n, kv_ref, bucket_ref, relb_ref, bias_scr):
    @pl.when(n == 0)
    def _():
        bk = bucket_ref[...]
        for h in range(SWA_HEADS):
            acc = jnp.zeros((WINDOW, WINDOW), F32)
            for b in range(REL_BUCKETS):
                acc = acc + jnp.where(bk == b, relb_ref[b, h], 0.0)
            bias_scr[h] = acc

    prev0 = pl.multiple_of(jnp.maximum(n - 1, 0) * WINDOW, WINDOW)
    cur0 = pl.multiple_of(n * WINDOW, WINDOW)
    kk = jnp.concatenate([kv_ref[pl.ds(prev0, WINDOW), :], kv_ref[pl.ds(cur0, WINDOW), :]], axis=0).astype(F32)
    rowi = lax.broadcasted_iota(jnp.int32, (WINDOW, WINDOW), 0)
    col = lax.broadcasted_iota(jnp.int32, (WINDOW, WINDOW), 1)
    from_prev = col > rowi
    return kk, from_prev, prev0, cur0


def _fold(full, from_prev):
    return jnp.where(from_prev, full[:, :WINDOW], full[:, WINDOW:])


def _unfold(sq, from_prev):
    return jnp.concatenate([jnp.where(from_prev, sq, 0.0), jnp.where(from_prev, 0.0, sq)], axis=1)


def _half_pair(part, kvh):
    lo = lax.broadcasted_iota(jnp.int32, part.shape, 1) < SWA_HD
    if kvh == 0:
        pa = jnp.where(lo, part, 0.0)
        pb = pltpu.roll(pa, SWA_HD, 1)
    else:
        pb = jnp.where(lo, 0.0, part)
        pa = pltpu.roll(pb, SWA_HD, 1)
    return pa.astype(BF), pb.astype(BF)


ALL_HEADS = SWA_HEADS * WINDOW


def _sink_column(sinks):
    return jnp.repeat(sinks.reshape(SWA_HEADS), WINDOW).reshape(ALL_HEADS, 1)


def _swa_operands(kk):
    return [(_half_pair(kk[:, :128], kvh), _half_pair(kk[:, 128:], kvh)) for kvh in range(SWA_KV_HEADS)]


def _swa_probs(n, q_ref, ops, bias_scr, sinkc_ref, from_prev):
    lgs = []
    for kvh in range(SWA_KV_HEADS):
        (ka, kb), _ = ops[kvh]
        for p in range(4):
            q2 = q_ref[:, kvh * 512 + p * 128:kvh * 512 + p * 128 + 128]
            lgs += [_fold(_dot(q2, ka, NT), from_prev), _fold(_dot(q2, kb, NT), from_prev)]
    lg = jnp.concatenate(lgs, axis=0) * (SWA_HD ** -0.5) + bias_scr[...].reshape(ALL_HEADS, WINDOW)
    rowi = jnp.bitwise_and(lax.broadcasted_iota(jnp.int32, (ALL_HEADS, WINDOW), 0), WINDOW - 1)
    col = lax.broadcasted_iota(jnp.int32, (ALL_HEADS, WINDOW), 1)
    no_prev = jnp.where(n > 0, 0, 4 * WINDOW)
    lg = jnp.where(jnp.logical_or(col <= rowi, col > rowi + no_prev), lg, NEG_INF)
    sink = sinkc_ref[...]
    m = jnp.maximum(jnp.max(lg, axis=-1, keepdims=True), sink)
    e = jnp.exp(lg - m)
    es = jnp.exp(sink - m)
    den = jnp.sum(e, axis=-1, keepdims=True) + es
    return e / den, es / den


def _swa_fwd(q, kv, g, bucket, rel_bias, sink_col):
    S = q.shape[0]
    nb = S // WINDOW

    def body(q_ref, kv_ref, g_ref, bucket_ref, relb_ref, sinkc_ref, o_ref, y_ref, bias_scr):
        n = pl.program_id(0)
        kk, from_prev, _, _ = _swa_common(n, kv_ref, bucket_ref, relb_ref, bias_scr)
        ops = _swa_operands(kk)
        pr, _ = _swa_probs(n, q_ref, ops, bias_scr, sinkc_ref, from_prev)
        for kvh in range(SWA_KV_HEADS):
            _, (va, vb) = ops[kvh]
            for p in range(4):
                c0 = kvh * 512 + p * 128
                r0 = (kvh * 8 + 2 * p) * WINDOW
                o2 = (_dot(_unfold(pr[r0:r0 + WINDOW], from_prev).astype(BF), va, NN)
                      + _dot(_unfold(pr[r0 + WINDOW:r0 + 2 * WINDOW], from_prev).astype(BF), vb, NN))
                o_ref[:, c0:c0 + 128] = o2
                gv = g_ref[:, c0:c0 + 128]
                y_ref[:, c0:c0 + 128] = (o2 * (gv * _sigmoid(gv))).astype(BF)

    blk = pl.BlockSpec((WINDOW, 1024), lambda n: (n, 0))
    smem = pl.BlockSpec(memory_space=pltpu.SMEM)
    sinkc = pl.BlockSpec((ALL_HEADS, 1), lambda n: (0, 0))
    return pl.pallas_call(
        body, name="swa_fwd", grid=(nb,),
        in_specs=[blk, pl.BlockSpec((S, 256), lambda n: (0, 0)), blk, pl.BlockSpec((WINDOW, WINDOW), lambda n: (0, 0)), smem, sinkc],
        out_specs=[blk, blk],
        out_shape=[jax.ShapeDtypeStruct((S, 1024), F32), jax.ShapeDtypeStruct((S, 1024), BF)],
        scratch_shapes=[pltpu.VMEM((SWA_HEADS, WINDOW, WINDOW), F32)],
        compiler_params=_params(("arbitrary",)),
    )(q, kv, g, bucket, rel_bias, sink_col)


def _swa_bwd(q, kv, g, o, dy, bucket, rel_bias, sink_col):
    S = q.shape[0]
    nb = S // WINDOW

    def body(q_ref, kv_ref, g_ref, o_ref, dy_ref, bucket_ref, relb_ref, sinkc_ref,
             dq_ref, dg_ref, dkv_ref, dsink_ref, drel_ref, bias_scr, dbias_scr, dsink_scr):
        n = pl.program_id(0)

        @pl.when(n == 0)
        def _():
            dbias_scr[...] = jnp.zeros_like(dbias_scr)
            dsink_scr[...] = jnp.zeros_like(dsink_scr)
            dkv_ref[...] = jnp.zeros_like(dkv_ref)

        kk, from_prev, prev0, cur0 = _swa_common(n, kv_ref, bucket_ref, relb_ref, bias_scr)
        ops = _swa_operands(kk)
        pr, ps = _swa_probs(n, q_ref, ops, bias_scr, sinkc_ref, from_prev)
        do2s, dps = [], []
        for kvh in range(SWA_KV_HEADS):
            _, (va, vb) = ops[kvh]
            for p in range(4):
                c0 = kvh * 512 + p * 128
                gv = g_ref[:, c0:c0 + 128]
                sg = _sigmoid(gv)
                dyv = dy_ref[:, c0:c0 + 128]
                dg_ref[:, c0:c0 + 128] = (dyv * o_ref[:, c0:c0 + 128] * (sg * (1.0 + gv * (1.0 - sg)))).astype(BF)
                do2 = (dyv * (gv * sg)).astype(BF)
                do2s.append(do2)
                dps += [_fold(_dot(do2, va, NT), from_prev), _fold(_dot(do2, vb, NT), from_prev)]
        dp = jnp.concatenate(dps, axis=0)
        delta = jnp.sum(pr * dp, axis=-1, keepdims=True)
        ds = pr * (dp - delta)
        dbias_scr[...] += ds.reshape(SWA_HEADS, WINDOW, WINDOW)
        dsink_scr[...] += ps * delta
        dsc = ds * (SWA_HD ** -0.5)
        lo256 = lax.broadcasted_iota(jnp.int32, (2 * WINDOW, 128), 1) < SWA_HD
        dks, dvs = [], []
        for kvh in range(SWA_KV_HEADS):
            (ka, kb), _ = ops[kvh]
            dka = jnp.zeros((2 * WINDOW, 128), F32)
            dkb, dva, dvb = dka, dka, dka
            for p in range(4):
                c0 = kvh * 512 + p * 128
                r0 = (kvh * 8 + 2 * p) * WINDOW
                q2 = q_ref[:, c0:c0 + 128]
                do2 = do2s[kvh * 4 + p]
                ds0 = _unfold(dsc[r0:r0 + WINDOW], from_prev).astype(BF)
                ds1 = _unfold(dsc[r0 + WINDOW:r0 + 2 * WINDOW], from_prev).astype(BF)
                dq_ref[:, c0:c0 + 128] = (_dot(ds0, ka, NN) + _dot(ds1, kb, NN)).astype(BF)
                dka = dka + _dot(ds0, q2, TN)
                dkb = dkb + _dot(ds1, q2, TN)
                dva = dva + _dot(_unfold(pr[r0:r0 + WINDOW], from_prev).astype(BF), do2, TN)
                dvb = dvb + _dot(_unfold(pr[r0 + WINDOW:r0 + 2 * WINDOW], from_prev).astype(BF), do2, TN)
            dks.append(jnp.where(lo256, dka, 0.0) + pltpu.roll(jnp.where(lo256, 0.0, dkb), SWA_HD, 1))
            dvs.append(jnp.where(lo256, dva, 0.0) + pltpu.roll(jnp.where(lo256, 0.0, dvb), SWA_HD, 1))
        dk = dks[0] + pltpu.roll(dks[1], SWA_HD, 1)
        dv = dvs[0] + pltpu.roll(dvs[1], SWA_HD, 1)
        dkv_ref[pl.ds(prev0, WINDOW), 0:128] += dk[:WINDOW]
        dkv_ref[pl.ds(prev0, WINDOW), 128:256] += dv[:WINDOW]
        dkv_ref[pl.ds(cur0, WINDOW), 0:128] += dk[WINDOW:]
        dkv_ref[pl.ds(cur0, WINDOW), 128:256] += dv[WINDOW:]

        @pl.when(n == nb - 1)
        def _():
            dsink_ref[...] = -jnp.sum(dsink_scr[...].reshape(SWA_HEADS, WINDOW, 1), axis=1)
            bk = bucket_ref[...]
            sums = []
            for b in range(REL_BUCKETS):
                sums.append(jnp.sum(jnp.where((bk == b)[None], dbias_scr[...], 0.0), axis=1))
            drel_ref[...] = jnp.sum(jnp.concatenate(sums, axis=0), axis=1, keepdims=True)

    blk = pl.BlockSpec((WINDOW, 1024), lambda n: (n, 0))
    smem = pl.BlockSpec(memory_space=pltpu.SMEM)
    whole = lambda shape: pl.BlockSpec(shape, lambda n: (0, 0))
    return pl.pallas_call(
        body, name="swa_bwd", grid=(nb,),
        in_specs=[blk, whole((S, 256)), blk, blk, blk, whole((WINDOW, WINDOW)), smem, whole((ALL_HEADS, 1))],
        out_specs=[blk, blk, whole((S, 256)), whole((SWA_HEADS, 1)), whole((REL_BUCKETS * SWA_HEADS, 1))],
        out_shape=[jax.ShapeDtypeStruct((S, 1024), BF), jax.ShapeDtypeStruct((S, 1024), BF),
                   jax.ShapeDtypeStruct((S, 256), F32), jax.ShapeDtypeStruct((SWA_HEADS, 1), F32),
                   jax.ShapeDtypeStruct((REL_BUCKETS * SWA_HEADS, 1), F32)],
        scratch_shapes=[pltpu.VMEM((SWA_HEADS, WINDOW, WINDOW), F32), pltpu.VMEM((SWA_HEADS, WINDOW, WINDOW), F32),
                        pltpu.VMEM((ALL_HEADS, 1), F32)],
        compiler_params=_params(("arbitrary",)),
    )(q, kv, g, o, dy, bucket, rel_bias, sink_col)


def _mem_probs(qh, mk):
    lg = _dot(qh, mk, NT) * (MEM_HD ** -0.5)
    e = jnp.exp(lg - jnp.max(lg, axis=-1, keepdims=True))
    return e / jnp.sum(e, axis=-1, keepdims=True)


def _mem_fwd(q, mkv, g):
    S = q.shape[0]
    M = mkv.shape[0]
    tq = 256

    def body(q_ref, mkv_ref, g_ref, o_ref, y_ref):
        for h in range(MEM_HEADS):
            c0 = h * MEM_HD
            pr = _mem_probs(q_ref[:, c0:c0 + MEM_HD], mkv_ref[:, c0:c0 + MEM_HD])
            o = _dot(pr.astype(BF), mkv_ref[:, D_MEM + c0:D_MEM + c0 + MEM_HD], NN)
            o_ref[:, c0:c0 + MEM_HD] = o
            gv = g_ref[:, c0:c0 + MEM_HD]
            y_ref[:, c0:c0 + MEM_HD] = (o * (gv * _sigmoid(gv))).astype(BF)

    blk = pl.BlockSpec((tq, D_MEM), lambda i: (i, 0))
    return pl.pallas_call(
        body, name="mem_fwd", grid=(S // tq,),
        in_specs=[blk, pl.BlockSpec((M, 2 * D_MEM), lambda i: (0, 0)), blk], out_specs=[blk, blk],
        out_shape=[jax.ShapeDtypeStruct((S, D_MEM), F32), jax.ShapeDtypeStruct((S, D_MEM), BF)],
        compiler_params=_params(("parallel",)),
    )(q, mkv, g)


def _mem_bwd(q, mkv, g, o, dy):
    S = q.shape[0]
    M = mkv.shape[0]
    tq = 256

    def body(q_ref, mkv_ref, g_ref, o_ref, dy_ref, dq_ref, dg_ref, dmkv_ref):
        @pl.when(pl.program_id(0) == 0)
        def _():
            dmkv_ref[...] = jnp.zeros_like(dmkv_ref)

        for h in range(MEM_HEADS):
            c0 = h * MEM_HD
            qh = q_ref[:, c0:c0 + MEM_HD]
            mk = mkv_ref[:, c0:c0 + MEM_HD]
            mv = mkv_ref[:, D_MEM + c0:D_MEM + c0 + MEM_HD]
            gv = g_ref[:, c0:c0 + MEM_HD]
            sg = _sigmoid(gv)
            dyv = dy_ref[:, c0:c0 + MEM_HD]
            dg_ref[:, c0:c0 + MEM_HD] = (dyv * o_ref[:, c0:c0 + MEM_HD] * (sg * (1.0 + gv * (1.0 - sg)))).astype(BF)
            do = (dyv * (gv * sg)).astype(BF)
            pr = _mem_probs(qh, mk)
            dp = _dot(do, mv, NT)
            ds = pr * (dp - jnp.sum(pr * dp, axis=-1, keepdims=True))
            dsb = (ds * (MEM_HD ** -0.5)).astype(BF)
            dq_ref[:, c0:c0 + MEM_HD] = _dot(dsb, mk, NN).astype(BF)
            dmkv_ref[:, c0:c0 + MEM_HD] += _dot(dsb, qh, TN)
            dmkv_ref[:, D_MEM + c0:D_MEM + c0 + MEM_HD] += _dot(pr.astype(BF), do, TN)

    blk = pl.BlockSpec((tq, D_MEM), lambda i: (i, 0))
    whole = pl.BlockSpec((M, 2 * D_MEM), lambda i: (0, 0))
    return pl.pallas_call(
        body, name="mem_bwd", grid=(S // tq,),
        in_specs=[blk, whole, blk, blk, blk], out_specs=[blk, blk, whole],
        out_shape=[jax.ShapeDtypeStruct((S, D_MEM), BF), jax.ShapeDtypeStruct((S, D_MEM), BF),
                   jax.ShapeDtypeStruct((M, 2 * D_MEM), F32)],
        compiler_params=_params(("arbitrary",)),
    )(q, mkv, g, o, dy)


MERGE_TN = 512


def _merge_specs(tm):
    ytile = pl.BlockSpec((tm, 1024), lambda i, j: (i, 0))
    wblk = pl.BlockSpec((1024, MERGE_TN), lambda i, j: (0, j))
    gls = [pl.BlockSpec((None, tm, MERGE_TN), (lambda i, j, br=br: (br, i, j))) for br in range(3)]
    otile = pl.BlockSpec((tm, MERGE_TN), lambda i, j: (i, j))
    return ytile, wblk, gls, otile


def _merge_fwd(ys, ws, gl, tm):
    S = gl.shape[1]

    def body(y0, y1, y2, w0, w1, w2, g0, g1, g2, o_ref):
        acc = None
        for y_ref, w_ref, g_ref in ((y0, w0, g0), (y1, w1, g1), (y2, w2, g2)):
            term = _sigmoid(g_ref[...]) * _dot(y_ref[...], w_ref[...], NN)
            acc = term if acc is None else acc + term
        o_ref[...] = acc.astype(BF)

    ytile, wblk, gls, otile = _merge_specs(tm)
    return pl.pallas_call(
        body, name="merge_fwd", grid=(S // tm, D_MODEL // MERGE_TN),
        in_specs=[ytile] * 3 + [wblk] * 3 + gls, out_specs=otile,
        out_shape=jax.ShapeDtypeStruct((S, D_MODEL), BF),
        compiler_params=_params(("parallel", "arbitrary")),
    )(*ys, *ws, gl, gl, gl)


def _merge_bwd(dout, w_out, ys, ws, gl, tm):
    S = gl.shape[1]

    def body(do_ref, wo_ref, y0, y1, y2, w0, w1, w2, g0, g1, g2, dg0, dg1, dg2, dp0, dp1, dp2):
        dm = _dot(do_ref[...], wo_ref[...], NT)
        for y_ref, w_ref, g_ref, dg_ref, dp_ref in ((y0, w0, g0, dg0, dp0), (y1, w1, g1, dg1, dp1), (y2, w2, g2, dg2, dp2)):
            gate = _sigmoid(g_ref[...])
            pv = _dot(y_ref[...], w_ref[...], NN)
            dg_ref[...] = (dm * pv * gate * (1.0 - gate)).astype(BF)
            dp_ref[...] = (dm * gate).astype(BF)

    ytile, wblk, gls, otile = _merge_specs(tm)
    out = jax.ShapeDtypeStruct((S, D_MODEL), BF)
    return pl.pallas_call(
        body, name="merge_bwd", grid=(S // tm, D_MODEL // MERGE_TN),
        in_specs=[pl.BlockSpec((tm, D_MODEL), lambda i, j: (i, 0)), pl.BlockSpec((MERGE_TN, D_MODEL), lambda i, j: (j, 0))]
        + [ytile] * 3 + [wblk] * 3 + gls,
        out_specs=[otile] * 6, out_shape=[out] * 6,
        compiler_params=_params(("parallel", "arbitrary")),
    )(dout, w_out, *ys, *ws, gl, gl, gl)


def _out_loss(merged, w_out, x, target, post_g, tm):
    S = x.shape[0]

    def body(m_ref, w_ref, x_ref, t_ref, g_ref, dout_ref, dy_ref, loss_ref, dpost_ref):
        @pl.when(pl.program_id(0) == 0)
        def _():
            loss_ref[...] = jnp.zeros_like(loss_ref)
            dpost_ref[...] = jnp.zeros_like(dpost_ref)

        out = _dot(m_ref[...], w_ref[...], NN)
        r = lax.rsqrt(jnp.mean(out * out, axis=-1, keepdims=True) + EPS)
        nrm = out * r
        gv = g_ref[...]
        err = (x_ref[...] + nrm * gv) - t_ref[...]
        sq = jnp.sum(jnp.sum(err * err, axis=1, keepdims=True), axis=0, keepdims=True)
        loss_ref[...] += sq * (0.5 / D_MODEL)
        dy = err * (1.0 / D_MODEL)
        dy_ref[...] = dy
        dpost_ref[...] += jnp.sum(dy * nrm, axis=0, keepdims=True)
        dn = dy * gv
        dout_ref[...] = (r * (dn - nrm * jnp.mean(dn * nrm, axis=-1, keepdims=True))).astype(BF)

    row = pl.BlockSpec((tm, D_MODEL), lambda i: (i, 0))
    return pl.pallas_call(
        body, name="out_loss", grid=(S // tm,),
        in_specs=[row, pl.BlockSpec((D_MODEL, D_MODEL), lambda i: (0, 0)), row, row, pl.BlockSpec((1, D_MODEL), lambda i: (0, 0))],
        out_specs=[row, row, pl.BlockSpec((8, 128), lambda i: (0, 0)), pl.BlockSpec((1, D_MODEL), lambda i: (0, 0))],
        out_shape=[jax.ShapeDtypeStruct((S, D_MODEL), BF), jax.ShapeDtypeStruct((S, D_MODEL), F32),
                   jax.ShapeDtypeStruct((8, 128), F32), jax.ShapeDtypeStruct((1, D_MODEL), F32)],
        compiler_params=_params(("arbitrary",)),
    )(merged, w_out, x, target, post_g)


def _dh_dx(dproj, w_lo, w_hi, x, dy, pre_g, tm, tk):
    S = x.shape[0]
    nk = D_IN // tk

    def body(dp_ref, wlo_ref, whi_ref, x_ref, dy_ref, g_ref, dx_ref, dpre_ref, acc_ref):
        i, k = pl.program_id(0), pl.program_id(1)

        @pl.when(jnp.logical_and(i == 0, k == 0))
        def _():
            dpre_ref[...] = jnp.zeros_like(dpre_ref)

        @pl.when(k == 0)
        def _():
            acc_ref[...] = jnp.zeros_like(acc_ref)

        dp = dp_ref[...]
        acc_ref[:, :K_HALF] += _dot(dp, wlo_ref[...], NN)
        acc_ref[:, K_HALF:] += _dot(dp, whi_ref[...], NN)

        @pl.when(k == nk - 1)
        def _():
            dh = acc_ref[...]
            xv = x_ref[...]
            r = lax.rsqrt(jnp.mean(xv * xv, axis=-1, keepdims=True) + EPS)
            nrm = xv * r
            dpre_ref[...] += jnp.sum(dh * nrm, axis=0, keepdims=True)
            dn = dh * g_ref[...]
            dx_ref[...] = r * (dn - nrm * jnp.mean(dn * nrm, axis=-1, keepdims=True)) + dy_ref[...]

    row = pl.BlockSpec((tm, D_MODEL), lambda i, k: (i, 0))
    vec = pl.BlockSpec((1, D_MODEL), lambda i, k: (0, 0))
    return pl.pallas_call(
        body, name="dh_dx", grid=(S // tm, nk),
        in_specs=[pl.BlockSpec((tm, tk), lambda i, k: (i, k)), pl.BlockSpec((tk, K_HALF), lambda i, k: (k, 0)),
                  pl.BlockSpec((tk, K_HALF), lambda i, k: (k, 0)), row, row, vec],
        out_specs=[row, vec],
        out_shape=[jax.ShapeDtypeStruct((S, D_MODEL), F32), jax.ShapeDtypeStruct((1, D_MODEL), F32)],
        scratch_shapes=[pltpu.VMEM((tm, D_MODEL), F32)],
        compiler_params=_params(("arbitrary", "arbitrary"), 56),
    )(dproj, w_lo, w_hi, x, dy, pre_g)


def _sum_parts(parts, name):
    P, R, C = parts.shape
    tr = max(t for t in range(8, 513, 8) if R % t == 0)

    def body(p_ref, o_ref):
        acc = p_ref[0]
        for j in range(1, P):
            acc = acc + p_ref[j]
        o_ref[...] = acc

    return pl.pallas_call(
        body, name=name, grid=(R // tr,),
        in_specs=[pl.BlockSpec((P, tr, C), lambda i: (0, i, 0))], out_specs=pl.BlockSpec((tr, C), lambda i: (i, 0)),
        out_shape=jax.ShapeDtypeStruct((R, C), F32), compiler_params=_params(("parallel",)),
    )(parts)


def _adamw(parts, w, m, v, name):
    groups = list(parts) if isinstance(parts, (list, tuple)) else [parts]
    P, rows, C = groups[0].shape
    R = rows * len(groups)
    tr = 128 if rows % 128 == 0 else rows
    per = rows // tr
    c1 = 1.0 - ADAM_B1 ** ADAM_STEP
    c2 = 1.0 - ADAM_B2 ** ADAM_STEP

    def body(*refs):
        p_refs = refs[:len(groups)]
        w_ref, m_ref, v_ref, g_ref, d_ref, nm_ref, nv_ref = refs[len(groups):]
        g = None
        for q, p_ref in enumerate(p_refs):
            gq = p_ref[0].astype(F32)
            for j in range(1, P):
                gq = gq + p_ref[j].astype(F32)
            g = gq if g is None else jnp.where(pl.program_id(0) // per == q, gq, g)
        nm = ADAM_B1 * m_ref[...] + (1.0 - ADAM_B1) * g
        nv = ADAM_B2 * v_ref[...] + (1.0 - ADAM_B2) * (g * g)
        g_ref[...] = g
        nm_ref[...] = nm
        nv_ref[...] = nv
        d_ref[...] = -ADAM_LR * ((nm / c1) / (jnp.sqrt(nv / c2) + ADAM_EPS) + ADAM_WD * w_ref[...])

    tile = pl.BlockSpec((tr, C), lambda i: (i, 0))
    out = jax.ShapeDtypeStruct((R, C), F32)
    return pl.pallas_call(
        body, name=name, grid=(R // tr,),
        in_specs=[pl.BlockSpec((P, tr, C), (lambda i, q=q: (0, jnp.clip(i - q * per, 0, per - 1), 0))) for q in range(len(groups))]
        + [tile, tile, tile], out_specs=[tile] * 4, out_shape=[out] * 4,
        compiler_params=_params(("parallel",)),
    )(*groups, w, m, v)


def _project(st, w_half, half):
    S = st["h"].shape[0]
    out = {}
    for name, c0, width, dt in SEGMENTS:
        out[name] = _matmul(st["h"], w_half, "nt", S, width, K_HALF, S, SEG_TILE, K_HALF, dt if half else F32,
                            "proj%d_%s" % (half, name), b_noff=c0 // SEG_TILE, a_koff=half,
                            out_blocked="third" if name == "gl" else None, addend=st["part"][name] if half else None)
    return out


def _forward_a1(x, mem, pre_g, mem_g, w_lo):
    S = x.shape[0]
    st = dict(T=min(512, S // 2), tm=min(512, S), bucket=_rel_bucket_map())
    st["h"] = _rms_fwd(x, pre_g, "pre_norm")
    st["memn"] = _rms_fwd(mem, mem_g, "mem_norm")
    st["part"] = _project(st, w_lo, 0)
    return st


def _forward_a2(st, w_hi, conv_w, conv_b, w_a, b_a, w_x, b_x, lam, sinks, rel_bias):
    seg = st["seg"] = _project(st, w_hi, 1)
    st["h_rg"], st["y_rg"] = _rglru_fwd(seg["xr"], seg["g_rg"], conv_w, conv_b, w_a, b_a, w_x, b_x, lam, st["T"])
    st["o_swa"], st["y_swa"] = _swa_fwd(seg["q_s"], seg["kv"], seg["g_swa"], st["bucket"], rel_bias, _sink_column(sinks))
    return st


def _forward_b(st, x, target, post_g, w_memkv, wbr, w_out):
    S = x.shape[0]
    M = st["memn"].shape[0]
    seg = st["seg"]
    st["mkv"] = _matmul(st["memn"], w_memkv, "nn", M, 2 * D_MEM, D_MODEL, M, 512, D_MODEL, BF, "mem_kv")
    st["o_mem"], st["y_mem"] = _mem_fwd(seg["q_m"], st["mkv"], seg["g_mem"])
    st["ys"] = (st["y_rg"], st["y_swa"], st["y_mem"])
    st["merged"] = _merge_fwd(st["ys"], wbr, seg["gl"], st["tm"])
    st["dout"], st["dy"], st["loss"], st["dpost"] = _out_loss(st["merged"], w_out, x, target, post_g, min(256, S))
    return st


def _backward_a(st, mem, w_memkv, wbr, w_out, conv_w, conv_b, w_a, b_a, w_x, b_x, lam):
    S = st["h"].shape[0]
    M = mem.shape[0]
    seg, ys, tm = st["seg"], st["ys"], st["tm"]
    st["dw_out"] = _matmul(st["merged"], st["dout"], "tn", D_MODEL, D_MODEL, S, 256, D_MODEL, S, BF, "dw_out", out_blocked="row")
    dgl0, dgl1, dgl2, dp0, dp1, dp2 = _merge_bwd(st["dout"], w_out, ys, wbr, seg["gl"], tm)
    st["dgl"] = (dgl0, dgl1, dgl2)
    dys, dwbr = [], []
    for i, dp in enumerate((dp0, dp1, dp2)):
        dys.append(_matmul(dp, wbr[i], "nt", S, 1024, D_MODEL, tm, 1024, D_MODEL, F32, "dy_br%d" % i))
        dwbr.append(_matmul(ys[i], dp, "tn", 1024, D_MODEL, S, 1024, 256, S, BF, "dw_br%d" % i, out_blocked="col"))
    st["dys"], st["dwbr"] = dys, dwbr
    st["dq_m"], st["dg_mem"], dmkv = _mem_bwd(seg["q_m"], st["mkv"], seg["g_mem"], st["o_mem"], dys[2])
    dmkv_b = dmkv.astype(BF)
    st["dw_memkv"] = _matmul(st["memn"], dmkv_b, "tn", D_MODEL, 2 * D_MEM, M, 256, 2 * D_MEM, M, BF, "dw_memkv", out_blocked="row")
    dmemn = _matmul(dmkv_b, w_memkv, "nt", M, D_MODEL, 2 * D_MEM, M, 512, 2 * D_MEM, F32, "dmemn")
    st["dmem_g"] = _rms_gain_grad(dmemn, mem, "dmem_gain")
    st["dxr"], st["dg_rg"], st["dw_a"], st["dw_x"], st["dvec"] = _rglru_bwd(
        seg["xr"], seg["g_rg"], st["h_rg"], dys[0], conv_w, conv_b, w_a, b_a, w_x, b_x, lam, st["T"])
    return st


def _backward_b(st, rel_bias, sinks):
    seg = st["seg"]
    dq_s, dg_swa, dkv, dsinks, drel = _swa_bwd(seg["q_s"], seg["kv"], seg["g_swa"], st["o_swa"], st["dys"][1],
                                               st["bucket"], rel_bias, _sink_column(sinks))
    st["dsinks"], st["drel"] = dsinks.reshape(1, SWA_HEADS), drel.reshape(REL_BUCKETS, SWA_HEADS)
    st["dproj"] = jnp.concatenate([st["dxr"], st["dg_rg"], dq_s, dkv.astype(BF), dg_swa, st["dq_m"], st["dg_mem"], *st["dgl"]], axis=1)
    return st


def _dw_in_half(st, half, dep=None):
    S = st["h"].shape[0]
    return _matmul(st["h"], st["dproj"], "tn", D_MODEL // 2, D_IN, S, 512, 1792, S, BF, "dw_in%d" % half, a_moff=2 * half, dep=dep)


def _owner_blocks(a):
    return jnp.swapaxes(a.reshape((4, 2) + a.shape[1:]), 0, 1)


def _local_step(x, mem, target, pre_g, post_g, mem_g, w_lo, w_hi, conv_w, conv_b, w_a, b_a, w_x, b_x, lam, sinks, rel_bias,
                w_memkv, wbr, w_out):
    st = _forward_a1(x, mem, pre_g, mem_g, w_lo)
    st = _forward_a2(st, w_hi, conv_w, conv_b, w_a, b_a, w_x, b_x, lam, sinks, rel_bias)
    st = _forward_b(st, x, target, post_g, w_memkv, wbr, w_out)
    st = _backward_a(st, mem, w_memkv, wbr, w_out, conv_w, conv_b, w_a, b_a, w_x, b_x, lam)
    st = _backward_b(st, rel_bias, sinks)
    st["dw_in"] = jnp.concatenate([_dw_in_half(st, 0), _dw_in_half(st, 1)], axis=0)
    st["grad_x"], st["dpre"] = _dh_dx(st["dproj"], w_lo, w_hi, x, st["dy"], pre_g, st["tm"], 896)
    return st


def _pad_rows(a, rows):
    a = a.reshape(-1, 128) if a.shape[-1] % 128 == 0 else jnp.pad(a, ((0, 0), (0, 128 - a.shape[-1])))
    return jnp.pad(a, ((0, rows - a.shape[0]), (0, 0))) if a.shape[0] < rows else a


def kernel(x, mem, pre_norm_g, post_norm_g, mem_norm_g, w_in, conv_w, conv_b, w_rg_a, b_rg_a, w_rg_x, b_rg_x, lru_lambda, swa_sinks, rel_bias, w_mem_kv, w_br_rg, w_br_swa, w_br_mem, w_out, loss_target, m_pre_norm_g, m_post_norm_g, m_mem_norm_g, m_w_in, m_conv_w, m_conv_b, m_w_rg_a, m_b_rg_a, m_w_rg_x, m_b_rg_x, m_lru_lambda, m_swa_sinks, m_rel_bias, m_w_mem_kv, m_w_br_rg, m_w_br_swa, m_w_br_mem, m_w_out, v_pre_norm_g, v_post_norm_g, v_mem_norm_g, v_w_in, v_conv_w, v_conv_b, v_w_rg_a, v_b_rg_a, v_w_rg_x, v_b_rg_x, v_lru_lambda, v_swa_sinks, v_rel_bias, v_w_mem_kv, v_w_br_rg, v_w_br_swa, v_w_br_mem, v_w_out):
    cx, cy, cc = lax.axis_index("x"), lax.axis_index("y"), lax.axis_index("c")
    me = 4 * cx + 2 * cy + cc
    chip = 2 * cx + cy
    core = jnp.reshape(cc, (1,)).astype(jnp.int32)
    x0, mem0 = x[0], mem[0]
    w_a_b, w_x_b = w_rg_a[0].astype(BF), w_rg_x[0].astype(BF)

    def landing(own, slot, slots, kind="lead"):
        if kind == "cols":
            return lax.dynamic_update_slice(lax.empty((own.shape[0], slots * own.shape[1]), own.dtype), own, (0, slot * own.shape[1]))
        return lax.dynamic_update_slice(lax.empty((slots,) + own.shape, own.dtype), own[None], (slot,) + (0,) * own.ndim)

    def scatter_start(parts, tag):
        got = _swap_with_sibling(parts, "scatter_%s_swap" % tag)
        sums = [_pair_sum(p, g, core, "scatter_%s_sum%d" % (tag, i)) for i, (p, g) in enumerate(zip(parts, got))]
        lands = [landing(lax.dynamic_index_in_dim(s, chip, 0, keepdims=False), chip, 4) for s in sums]
        return _exchange_start(sums, lands, _plan_scatter(len(sums)), "scatter_%s_start" % tag)

    def zero_after(a):
        return jnp.minimum(jnp.abs(a.reshape(-1)[0].astype(F32)), 0.0)

    w_t = jnp.transpose(w_in[0]).astype(BF)
    g_lo, g_cw = _all_gather([w_t[:, :K_HALF], conv_w[0]], "gather_w_in_lo")
    w_lo = g_lo.reshape(D_IN, K_HALF)
    conv_w_f = jnp.transpose(g_cw, (1, 0, 2)).reshape(CONV_W, D_RNN)
    hi = w_t[:, K_HALF:] + zero_after(g_cw).astype(BF)
    plan_h = _plan_gather(["lead"])
    h_send, h_recv, h_src, h_land, h_token = _exchange_start([hi], [landing(hi, me, N_DEV)], plan_h, "gather_w_in_hi_start")
    st = _forward_a1(x0, mem0, pre_norm_g + h_token[0:1, 0:1], mem_norm_g, w_lo)
    h_land = _exchange_wait(h_send, h_recv, h_src, h_land, plan_h, st["part"]["gl"], "gather_w_in_hi_wait")
    w_hi = _forward_to_sibling(h_land, ["lead"], "gather_w_in_hi_forward")[0].reshape(D_IN, K_HALF)

    after_hi = zero_after(w_hi).astype(BF)
    rest = [w.astype(BF) + after_hi for w in (w_mem_kv[0], w_br_rg[0], w_br_swa[0], w_br_mem[0], w_out[0])]
    kinds = ["lead", "cols", "cols", "cols", "lead"]
    plan_g = _plan_gather(kinds)
    g_send, g_recv, g_src, g_land, g_token = _exchange_start(rest, [landing(w, me, N_DEV, kd) for w, kd in zip(rest, kinds)], plan_g,
                                                             "gather_rest_start")
    st = _forward_a2(st, w_hi, conv_w_f, conv_b + g_token[0:1, 0:1], w_a_b, b_rg_a, w_x_b, b_rg_x, lru_lambda, swa_sinks, rel_bias)
    g_land = _exchange_wait(g_send, g_recv, g_src, g_land, plan_g, st["y_swa"], "gather_rest_wait")
    g_land = _forward_to_sibling(g_land, kinds, "gather_rest_forward")
    w_memkv_f = g_land[0].reshape(D_MODEL, 2 * D_MEM)
    wbr = (g_land[1], g_land[2], g_land[3])
    w_out_f = g_land[4].reshape(D_MODEL, D_MODEL)

    st = _forward_b(st, x0, loss_target[0], post_norm_g, w_memkv_f, wbr, w_out_f)
    st = _backward_a(st, mem0, w_memkv_f, wbr, w_out_f, conv_w_f, conv_b, w_a_b, b_rg_a, w_x_b, b_rg_x, lru_lambda)
    parts_a = [st["dw_memkv"], st["dwbr"][0], st["dwbr"][1], st["dwbr"][2], st["dw_out"],
               _owner_blocks(st["dw_a"]), _owner_blocks(st["dw_x"])]
    plan_a = _plan_scatter(len(parts_a))
    a_send, a_recv, a_src, a_land, a_token = scatter_start(parts_a, "a")

    st = _backward_b(st, rel_bias, swa_sinks + a_token[0:1, 0:1])
    plan_b = _plan_scatter(1)
    halves, dep = [], None
    for half in range(2):
        dwh = _dw_in_half(st, half, dep)
        parts_b = [jnp.transpose(dwh.reshape(D_MODEL // 2, 4, 2, D_IN // N_DEV), (2, 1, 0, 3))]
        halves.append(scatter_start(parts_b, "b%d" % half))
        dep = halves[-1][4]
    grad_x, dpre = _dh_dx(st["dproj"], w_lo, w_hi, x0, st["dy"], pre_norm_g + dep[0:1, 0:1], st["tm"], 896)
    a_land = _exchange_wait(a_send, a_recv, a_src, a_land, plan_a, grad_x, "scatter_a_wait")
    g_wa_blk = _sum_parts(a_land[5], "sum_w_rg_a")
    g_wx_blk = _sum_parts(a_land[6], "sum_w_rg_x")
    big = [None]
    for j, (wt, mt, vt) in enumerate(((w_mem_kv, m_w_mem_kv, v_w_mem_kv), (w_br_rg, m_w_br_rg, v_w_br_rg),
                                      (w_br_swa, m_w_br_swa, v_w_br_swa), (w_br_mem, m_w_br_mem, v_w_br_mem), (w_out, m_w_out, v_w_out))):
        big.append([a[None] for a in _adamw(a_land[j], wt[0], mt[0], vt[0], "adamw_big%d" % (j + 1))])
    after, b_lands = big[5][1], []
    for half, (b_send, b_recv, b_src, b_land, _) in enumerate(halves):
        b_lands.append(_exchange_wait(b_send, b_recv, b_src, b_land, plan_b, after, "scatter_b%d_wait" % half)[0])
        after = b_lands[-1]
    big[0] = [a[None] for a in _adamw(b_lands, w_in[0], m_w_in[0], v_w_in[0], "adamw_big0")]
    links_free = jnp.minimum(jnp.abs(big[0][0][0, 0, 0]), 0.0)

    pack = jnp.concatenate([dpre.reshape(16, 128), st["dpost"].reshape(16, 128), st["dmem_g"].reshape(16, 128),
                            st["dvec"].reshape(64, 128), _pad_rows(st["dsinks"], 8), _pad_rows(st["drel"], 32), g_wa_blk, g_wx_blk], axis=0) + links_free
    gathered = _all_gather([pack], "gather_small")[0]
    gs = _sum_parts(gathered, "sum_small")
    g_pre, g_post, g_memg = gs[0:16].reshape(1, D_MODEL), gs[16:32].reshape(1, D_MODEL), gs[32:48].reshape(1, D_MODEL)
    gvec = gs[48:112].reshape(8, D_RNN)
    g_conv_w = lax.dynamic_slice(gvec[0:CONV_W], (0, me * RNN_BLOCK), (CONV_W, RNN_BLOCK))
    g_conv_b, g_b_a, g_b_x, g_lam = gvec[4:5], gvec[5:6], gvec[6:7], gvec[7:8]
    g_sinks = gs[112:113, :SWA_HEADS]
    g_rel = gs[120:152, :SWA_HEADS]
    g_w_a = gathered[:, 152:280]
    g_w_x = gathered[:, 280:408]

    def packed(ts):
        pre, post, memg, cb, ba, bx, lm, wa, wx, sk, rel, cw = ts
        return jnp.concatenate([pre.reshape(16, 128), post.reshape(16, 128), memg.reshape(16, 128), cb.reshape(8, 128),
                                ba.reshape(8, 128), bx.reshape(8, 128), lm.reshape(8, 128), wa.reshape(1024, 128),
                                wx.reshape(1024, 128), _pad_rows(sk.reshape(1, SWA_HEADS), 8), _pad_rows(rel, 32),
                                _pad_rows(cw.reshape(CONV_W, RNN_BLOCK), 8)], axis=0)

    def unpacked(a):
        return (a[0:16].reshape(1, D_MODEL), a[16:32].reshape(1, D_MODEL), a[32:48].reshape(1, D_MODEL), a[48:56].reshape(1, D_RNN),
                a[56:64].reshape(1, D_RNN), a[64:72].reshape(1, D_RNN), a[72:80].reshape(1, D_RNN),
                a[80:1104].reshape(1, RNN_BLOCKS, RNN_BLOCK, RNN_BLOCK), a[1104:2128].reshape(1, RNN_BLOCKS, RNN_BLOCK, RNN_BLOCK),
                a[2128:2129, :SWA_HEADS], a[2136:2168, :SWA_HEADS], a[2168:2172].reshape(1, CONV_W, RNN_BLOCK))

    g_small = (g_pre, g_post, g_memg, g_conv_b, g_b_a, g_b_x, g_lam, g_w_a, g_w_x, g_sinks, g_rel, g_conv_w)
    w_small = (pre_norm_g, post_norm_g, mem_norm_g, conv_b, b_rg_a, b_rg_x, lru_lambda, w_rg_a, w_rg_x, swa_sinks, rel_bias, conv_w)
    m_small = (m_pre_norm_g, m_post_norm_g, m_mem_norm_g, m_conv_b, m_b_rg_a, m_b_rg_x, m_lru_lambda, m_w_rg_a, m_w_rg_x, m_swa_sinks, m_rel_bias, m_conv_w)
    v_small = (v_pre_norm_g, v_post_norm_g, v_mem_norm_g, v_conv_b, v_b_rg_a, v_b_rg_x, v_lru_lambda, v_w_rg_a, v_w_rg_x, v_swa_sinks, v_rel_bias, v_conv_w)
    sm = [unpacked(a) for a in _adamw(packed(g_small)[None], packed(w_small), packed(m_small), packed(v_small), "adamw_small")]


    loss_total = lax.psum(st["loss"][0, 0], AXES)

    def leaves(k):
        s = sm[k]
        return [s[0], s[1], s[2], big[0][k], s[11], s[3], s[7], s[4], s[8], s[5], s[6], s[9], s[10],
                big[1][k], big[2][k], big[3][k], big[4][k], big[5][k]]

    return (loss_total, grad_x[None], *leaves(0), *leaves(1), *leaves(2), *leaves(3))
```

```python
import math

import jax
import jax.numpy as jnp
import numpy as np
from jax import lax
from jax.experimental import pallas as pl
from jax.experimental.pallas import tpu as pltpu

F32, BF = jnp.float32, jnp.bfloat16
MESH = pl.DeviceIdType.MESH
AXES = ("x", "y", "c")
N_DEV = 8

D_MODEL = 2048
D_RNN = 1024
RNN_BLOCKS = 8
RNN_BLOCK = 128
CONV_W = 4
LRU_C = 8.0
SWA_HEADS = 16
SWA_KV_HEADS = 2
SWA_HD = 64
WINDOW = 128
MEM_HEADS = 4
MEM_HD = 256
D_MEM = 1024
REL_BUCKETS = 32
REL_MAX_DIST = 128
EPS = 1e-6
NEG_INF = -1e30
D_IN = 12544
SEGMENTS = (("xr", 0, 1024, F32), ("g_rg", 1024, 1024, F32), ("q_s", 2048, 1024, BF), ("kv", 3072, 256, BF),
            ("g_swa", 3328, 1024, F32), ("q_m", 4352, 1024, BF), ("g_mem", 5376, 1024, F32), ("gl", 6400, 6144, F32))
SEG_TILE = 256

ADAM_LR, ADAM_B1, ADAM_B2, ADAM_EPS, ADAM_WD, ADAM_STEP = 0.001, 0.9, 0.999, 1e-08, 0.01, 10

NN = (((1,), (0,)), ((), ()))
NT = (((1,), (1,)), ((), ()))
TN = (((0,), (0,)), ((), ()))
MIB = 2 ** 20


def _dot(a, b, dn):
    return lax.dot_general(a, b, dn, preferred_element_type=F32)


def _params(sem, vmem_mib=48):
    return pltpu.CompilerParams(dimension_semantics=sem, vmem_limit_bytes=vmem_mib * MIB)


def _sigmoid(z):
    return 1.0 / (1.0 + jnp.exp(-z))


def _softplus(z):
    return jnp.maximum(z, 0.0) + jnp.log(1.0 + jnp.exp(-jnp.abs(z)))


def _expm1(z):
    p = z * (1.0 + z * (0.5 + z * (1.0 / 6 + z * (1.0 / 24 + z * (1.0 / 120 + z * (1.0 / 720 + z * (1.0 / 5040 + z / 40320)))))))
    return jnp.where(jnp.abs(z) < 0.3, p, jnp.exp(z) - 1.0)


def _flat(p):
    return 4 * p[0] + 2 * p[1] + p[2]


def _all_gather(arrs, name):
    n = len(arrs)

    def body(*refs):
        ins, outs = refs[:n], refs[n:2 * n]
        send_sems, recv_sems, local_sems = refs[2 * n:]
        x, y, c = lax.axis_index("x"), lax.axis_index("y"), lax.axis_index("c")
        me, sibling = (x, y, c), (x, y, 1 - c)
        chips = [(1 - x, y), (x, 1 - y), (1 - x, 1 - y)]

        def copy(a, k, block, to, src=None):
            dst = outs[a].at[_flat(block)]
            return pltpu.make_async_remote_copy(src_ref=dst if src is None else src, dst_ref=dst,
                                                send_sem=send_sems.at[a * 7 + k], recv_sem=recv_sems.at[a * 7 + k],
                                                device_id=to, device_id_type=MESH)

        mine = [pltpu.make_async_copy(ins[a], outs[a].at[_flat(me)], local_sems.at[a]) for a in range(n)]
        for cp in mine:
            cp.start()
        first = []
        for a in range(n):
            first += [copy(a, 1 + j, me, (*chip, c), src=ins[a]) for j, chip in enumerate(chips)]
            first.append(copy(a, 0, me, sibling, src=ins[a]))
        for cp in first:
            cp.start()
        passed = []
        for j, chip in enumerate(chips):
            for a in range(n):
                copy(a, 1 + j, (*chip, c), me).wait_recv()
                fw = copy(a, 4 + j, (*chip, c), sibling)
                fw.start()
                passed.append(fw)
        for a in range(n):
            copy(a, 0, sibling, me).wait_recv()
            for j, chip in enumerate(chips):
                copy(a, 4 + j, (*chip, 1 - c), me).wait_recv()
        for cp in first + passed:
            cp.wait_send()
        for cp in mine:
            cp.wait()

    any_spec = pl.BlockSpec(memory_space=pl.ANY)
    return pl.pallas_call(
        body, name=name,
        out_shape=[jax.ShapeDtypeStruct((N_DEV,) + a.shape, a.dtype) for a in arrs],
        in_specs=[any_spec] * n, out_specs=[any_spec] * n,
        scratch_shapes=[pltpu.SemaphoreType.DMA((7 * n,)), pltpu.SemaphoreType.DMA((7 * n,)), pltpu.SemaphoreType.DMA((n,))],
    )(*arrs)


def _all_gather_relayed(arrs, relay, name):
    n = len(arrs)
    K = 9

    def body(*refs):
        ins, outs = refs[:n], refs[n:2 * n]
        send_sems, recv_sems, local_sems = refs[2 * n:]
        x, y, c = lax.axis_index("x"), lax.axis_index("y"), lax.axis_index("c")
        me, sib = (x, y, c), (x, y, 1 - c)
        xn, yn, dg = (1 - x, y, c), (x, 1 - y, c), (1 - x, 1 - y, c)

        def other(p):
            return (p[0], p[1], 1 - p[2])

        def rows(a, half):
            h = arrs[a].shape[0] // 2
            return pl.ds(half * h, h)

        def copy(a, k, block, to, half=None, src=None):
            dst = outs[a].at[_flat(block)]
            if half is not None:
                dst = dst.at[rows(a, half)]
            return pltpu.make_async_remote_copy(src_ref=dst if src is None else src, dst_ref=dst,
                                                send_sem=send_sems.at[a * K + k], recv_sem=recv_sems.at[a * K + k],
                                                device_id=to, device_id_type=MESH)

        mine = [pltpu.make_async_copy(ins[a], outs[a].at[_flat(me)], local_sems.at[a]) for a in range(n)]
        for cp in mine:
            cp.start()
        sends = []

        def start(cp):
            cp.start()
            sends.append(cp)

        for a in range(n):
            start(copy(a, 1, me, xn, src=ins[a]))
            start(copy(a, 2, me, yn, src=ins[a]))
            if not relay[a]:
                start(copy(a, 3, me, dg, src=ins[a]))
            start(copy(a, 0, me, sib, src=ins[a]))
        for a in range(n):
            copy(a, 1, xn, me).wait_recv()
            if relay[a]:
                start(copy(a, 3, xn, yn, half=0))
            start(copy(a, 5, xn, sib))
        for a in range(n):
            copy(a, 2, yn, me).wait_recv()
            if relay[a]:
                start(copy(a, 4, yn, xn, half=1))
            start(copy(a, 6, yn, sib))
        for a in range(n):
            if relay[a]:
                copy(a, 3, dg, me, half=0).wait_recv()
                start(copy(a, 7, dg, sib, half=0))
                copy(a, 4, dg, me, half=1).wait_recv()
                start(copy(a, 8, dg, sib, half=1))
            else:
                copy(a, 3, dg, me).wait_recv()
                start(copy(a, 7, dg, sib))
        for a in range(n):
            copy(a, 0, sib, me).wait_recv()
            copy(a, 5, other(xn), me).wait_recv()
            copy(a, 6, other(yn), me).wait_recv()
            if relay[a]:
                copy(a, 7, other(dg), me, half=0).wait_recv()
                copy(a, 8, other(dg), me, half=1).wait_recv()
            else:
                copy(a, 7, other(dg), me).wait_recv()
        for cp in sends:
            cp.wait_send()
        for cp in mine:
            cp.wait()

    any_spec = pl.BlockSpec(memory_space=pl.ANY)
    return pl.pallas_call(
        body, name=name,
        out_shape=[jax.ShapeDtypeStruct((N_DEV,) + a.shape, a.dtype) for a in arrs],
        in_specs=[any_spec] * n, out_specs=[any_spec] * n,
        scratch_shapes=[pltpu.SemaphoreType.DMA((K * n,)), pltpu.SemaphoreType.DMA((K * n,)), pltpu.SemaphoreType.DMA((n,))],
    )(*arrs)


def _chip_peers(x, y):
    return [(1 - x, y), (x, 1 - y), (1 - x, 1 - y)]


def _chip(p):
    return 2 * p[0] + p[1]


def _plan_gather(kinds):
    def plan(x, y, c):
        out = []
        for a, kind in enumerate(kinds):
            for peer in [(x, y, 1 - c)] + [(*ch, c) for ch in _chip_peers(x, y)]:
                out.append((a, None, (kind, _flat((x, y, c))), peer, (kind, _flat(peer))))
        return out
    return plan


def _plan_swap(n):
    def plan(x, y, c):
        return [(a, 1 - c, ("all", 0), (x, y, 1 - c), ("all", 0)) for a in range(n)]
    return plan


def _slot(ref, where):
    kind, k = where
    if kind == "all":
        return ref
    if kind == "lead":
        return ref.at[k]
    return ref.at[:, pl.ds(pl.multiple_of(k * 256, 256), 256)]


def _plan_scatter(n):
    def plan(x, y, c):
        out = []
        for a in range(n):
            for ch in _chip_peers(x, y):
                out.append((a, _chip(ch), ("lead", _chip((x, y))), (*ch, c), ("lead", _chip(ch))))
        return out
    return plan


HBM_SPEC = pl.BlockSpec(memory_space=pltpu.HBM)
SEM_SPEC = pl.BlockSpec(memory_space=pltpu.SEMAPHORE)


def _in_hbm(a):
    return pltpu.with_memory_space_constraint(a, pltpu.HBM)


def _exchange_start(srcs, lands, plan, name):
    n = len(srcs)
    count = len(plan(0, 0, 0))

    def body(*refs):
        src_refs, land_refs = refs[:n], refs[n:2 * n]
        send_sems, recv_sems = refs[2 * n], refs[2 * n + 1]
        token = refs[-1]
        x, y, c = lax.axis_index("x"), lax.axis_index("y"), lax.axis_index("c")
        for k, (a, si, di, peer, _) in enumerate(plan(x, y, c)):
            src = src_refs[a] if si is None else src_refs[a].at[si]
            pltpu.make_async_remote_copy(src_ref=src, dst_ref=_slot(land_refs[a], di), send_sem=send_sems.at[k],
                                         recv_sem=recv_sems.at[k], device_id=peer, device_id_type=MESH).start()
        token[...] = jnp.zeros_like(token)

    out = pl.pallas_call(
        body, name=name,
        out_shape=(pltpu.SemaphoreType.DMA((count,)), pltpu.SemaphoreType.DMA((count,)),
                   *[pltpu.HBM(a.shape, a.dtype) for a in srcs], *[pltpu.HBM(a.shape, a.dtype) for a in lands],
                   jax.ShapeDtypeStruct((8, 128), F32)),
        in_specs=[HBM_SPEC] * (2 * n),
        out_specs=(SEM_SPEC, SEM_SPEC, *([HBM_SPEC] * (2 * n)), pl.BlockSpec(memory_space=pltpu.VMEM)),
        input_output_aliases={i: 2 + i for i in range(2 * n)},
        compiler_params=pltpu.CompilerParams(has_side_effects=pltpu.SideEffectType.DATAFLOW_SIDE_EFFECTING),
    )(*[_in_hbm(a) for a in srcs], *[_in_hbm(a) for a in lands])
    return out[0], out[1], list(out[2:2 + n]), list(out[2 + n:2 + 2 * n]), out[-1]


def _exchange_wait(send_sems, recv_sems, srcs, lands, plan, after, name):
    n = len(srcs)

    def body(*refs):
        src_refs, land_refs = refs[:n], refs[n:2 * n]
        send_sems, recv_sems = refs[2 * n], refs[2 * n + 1]
        x, y, c = lax.axis_index("x"), lax.axis_index("y"), lax.axis_index("c")
        for k, (a, si, _, peer, ri) in enumerate(plan(x, y, c)):
            src = src_refs[a] if si is None else src_refs[a].at[si]
            cp = pltpu.make_async_remote_copy(src_ref=src, dst_ref=_slot(land_refs[a], ri), send_sem=send_sems.at[k],
                                              recv_sem=recv_sems.at[k], device_id=peer, device_id_type=MESH)
            cp.wait_send()
            cp.wait_recv()

    out = pl.pallas_call(
        body, name=name,
        out_shape=(*[pltpu.HBM(a.shape, a.dtype) for a in srcs], *[pltpu.HBM(a.shape, a.dtype) for a in lands]),
        in_specs=[HBM_SPEC] * (2 * n) + [SEM_SPEC, SEM_SPEC, pl.BlockSpec(memory_space=pl.ANY)],
        out_specs=tuple([HBM_SPEC] * (2 * n)),
        input_output_aliases={i: i for i in range(2 * n)},
        compiler_params=pltpu.CompilerParams(has_side_effects=pltpu.SideEffectType.DATAFLOW_SIDE_EFFECTING),
    )(*srcs, *lands, send_sems, recv_sems, after)
    return list(out[n:2 * n])


def _forward_to_sibling(lands, kinds, name):
    n = len(lands)

    def body(*refs):
        in_refs, out_refs = refs[:n], refs[n:2 * n]
        send_sems, recv_sems = refs[2 * n:]
        x, y, c = lax.axis_index("x"), lax.axis_index("y"), lax.axis_index("c")
        sibling = (x, y, 1 - c)

        def copy(a, j, slot):
            return pltpu.make_async_remote_copy(src_ref=_slot(in_refs[a], (kinds[a], slot)), dst_ref=_slot(out_refs[a], (kinds[a], slot)),
                                                send_sem=send_sems.at[a * 3 + j], recv_sem=recv_sems.at[a * 3 + j],
                                                device_id=sibling, device_id_type=MESH)

        sends = [copy(a, j, _flat((*ch, c))) for a in range(n) for j, ch in enumerate(_chip_peers(x, y))]
        for cp in sends:
            cp.start()
        for a in range(n):
            for j, ch in enumerate(_chip_peers(x, y)):
                copy(a, j, _flat((*ch, 1 - c))).wait_recv()
        for cp in sends:
            cp.wait_send()

    any_spec = pl.BlockSpec(memory_space=pl.ANY)
    return pl.pallas_call(
        body, name=name, out_shape=[jax.ShapeDtypeStruct(a.shape, a.dtype) for a in lands],
        in_specs=[any_spec] * n, out_specs=[any_spec] * n, input_output_aliases={a: a for a in range(n)},
        scratch_shapes=[pltpu.SemaphoreType.DMA((3 * n,)), pltpu.SemaphoreType.DMA((3 * n,))],
    )(*lands)


def _swap_with_sibling(parts, name):
    n = len(parts)

    def body(*refs):
        in_refs, out_refs = refs[:n], refs[n:2 * n]
        send_sems, recv_sems = refs[2 * n:]
        x, y, c = lax.axis_index("x"), lax.axis_index("y"), lax.axis_index("c")
        sends = [pltpu.make_async_remote_copy(src_ref=in_refs[a].at[1 - c], dst_ref=out_refs[a], send_sem=send_sems.at[a],
                                              recv_sem=recv_sems.at[a], device_id=(x, y, 1 - c), device_id_type=MESH)
                 for a in range(n)]
        for cp in sends:
            cp.start()
        for cp in sends:
            cp.wait()

    any_spec = pl.BlockSpec(memory_space=pl.ANY)
    return pl.pallas_call(
        body, name=name, out_shape=[jax.ShapeDtypeStruct(a.shape[1:], a.dtype) for a in parts],
        in_specs=[any_spec] * n, out_specs=[any_spec] * n,
        scratch_shapes=[pltpu.SemaphoreType.DMA((n,)), pltpu.SemaphoreType.DMA((n,))],
    )(*parts)


def _pair_sum(parts, got, core, name):
    _, _, R, C = parts.shape
    tr = 256 if R % 256 == 0 else R

    def body(c_ref, p_ref, g_ref, o_ref):
        o_ref[...] = (p_ref[...].astype(F32) + g_ref[...].astype(F32)).astype(o_ref.dtype)

    return pl.pallas_call(
        body, name=name,
        grid_spec=pltpu.PrefetchScalarGridSpec(
            num_scalar_prefetch=1, grid=(4, R // tr),
            in_specs=[pl.BlockSpec((None, None, tr, C), lambda j, i, c_ref: (c_ref[0], j, i, 0)),
                      pl.BlockSpec((None, tr, C), lambda j, i, c_ref: (j, i, 0))],
            out_specs=pl.BlockSpec((None, tr, C), lambda j, i, c_ref: (j, i, 0))),
        out_shape=jax.ShapeDtypeStruct((4, R, C), parts.dtype),
        compiler_params=_params(("parallel", "parallel")),
    )(core, parts, got)


def _matmul(a, b, mode, M, N, K, tm, tn, tk, out_dtype, name, b_noff=0, a_moff=0, a_koff=0, b_blocked=False, out_blocked=None,
            dep=None, addend=None, vmem_mib=48):
    nm, nn, nk = M // tm, N // tn, K // tk
    if mode == "nn":
        a_spec = pl.BlockSpec((tm, tk), lambda j, i, k: (i, k + a_koff))
        b_spec = pl.BlockSpec((tk, tn), lambda j, i, k: (k, j + b_noff))
        dn = NN
    elif mode == "nt":
        a_spec = pl.BlockSpec((tm, tk), lambda j, i, k: (i, k + a_koff))
        if b_blocked:
            b_spec = pl.BlockSpec((None, tn, tk), lambda j, i, k: (k, j, 0))
        else:
            b_spec = pl.BlockSpec((tn, tk), lambda j, i, k: (j + b_noff, k))
        dn = NT
    else:
        a_spec = pl.BlockSpec((tk, tm), lambda j, i, k: (k, i + a_moff))
        b_spec = pl.BlockSpec((tk, tn), lambda j, i, k: (k, j + b_noff))
        dn = TN
    if out_blocked == "col":
        out_shape = jax.ShapeDtypeStruct((2, 4, M, tn), out_dtype)
        out_spec = pl.BlockSpec((None, None, tm, tn), lambda j, i, k: (j % 2, j // 2, i, 0))
    elif out_blocked == "row":
        out_shape = jax.ShapeDtypeStruct((2, 4, tm, N), out_dtype)
        out_spec = pl.BlockSpec((None, None, tm, tn), lambda j, i, k: (i % 2, i // 2, 0, j))
    elif out_blocked == "third":
        out_shape = jax.ShapeDtypeStruct((3, M, N // 3), out_dtype)
        out_spec = pl.BlockSpec((None, tm, tn), lambda j, i, k: (j // (nn // 3), i, j % (nn // 3)))
    else:
        out_shape = jax.ShapeDtypeStruct((M, N), out_dtype)
        out_spec = pl.BlockSpec((tm, tn), lambda j, i, k: (i, j))

    n_extra = (addend is not None) + (dep is not None)

    def body(a_ref, b_ref, *rest):
        o_ref, scratch = rest[n_extra], rest[n_extra + 1:]
        if nk == 1:
            prod = _dot(a_ref[...], b_ref[...], dn)
            if addend is not None:
                prod = prod + rest[0][...].astype(F32)
            o_ref[...] = prod.astype(out_dtype)
        else:
            assert addend is None
            acc_ref, = scratch
            k = pl.program_id(2)

            @pl.when(k == 0)
            def _():
                acc_ref[...] = jnp.zeros_like(acc_ref)

            acc_ref[...] += _dot(a_ref[...], b_ref[...], dn)

            @pl.when(k == nk - 1)
            def _():
                o_ref[...] = acc_ref[...].astype(out_dtype)

    return pl.pallas_call(
        body, name=name, grid=(nn, nm, nk),
        in_specs=[a_spec, b_spec] + ([] if addend is None else [out_spec])
        + ([] if dep is None else [pl.BlockSpec((8, 128), lambda j, i, k: (0, 0))]),
        out_specs=out_spec, out_shape=out_shape,
        scratch_shapes=[] if nk == 1 else [pltpu.VMEM((tm, tn), F32)],
        compiler_params=_params(("parallel", "parallel", "arbitrary"), vmem_mib),
    )(a, b, *([] if addend is None else [addend]), *([] if dep is None else [dep]))


def _rms_fwd(x, g, name):
    R, Dm = x.shape
    tr = min(R, 256)

    def body(x_ref, g_ref, h_ref):
        xv = x_ref[...]
        r = lax.rsqrt(jnp.mean(xv * xv, axis=-1, keepdims=True) + EPS)
        h_ref[...] = (xv * r * g_ref[...]).astype(BF)

    return pl.pallas_call(
        body, name=name, grid=(R // tr,),
        in_specs=[pl.BlockSpec((tr, Dm), lambda i: (i, 0)), pl.BlockSpec((1, Dm), lambda i: (0, 0))],
        out_specs=pl.BlockSpec((tr, Dm), lambda i: (i, 0)), out_shape=jax.ShapeDtypeStruct((R, Dm), BF),
        compiler_params=_params(("parallel",)),
    )(x, g)


def _rms_gain_grad(dn, x, name):
    R, Dm = x.shape

    def body(dn_ref, x_ref, o_ref):
        xv = x_ref[...]
        r = lax.rsqrt(jnp.mean(xv * xv, axis=-1, keepdims=True) + EPS)
        o_ref[...] = jnp.sum(dn_ref[...] * xv * r, axis=0, keepdims=True)

    return pl.pallas_call(
        body, name=name, out_shape=jax.ShapeDtypeStruct((1, Dm), F32),
        compiler_params=pltpu.CompilerParams(vmem_limit_bytes=32 * MIB),
    )(dn, x)


def _shift_down(v, k, head8, row, T):
    if k == 0:
        return v
    r = pltpu.roll(v, k, 0)
    hr = pltpu.roll(head8, k, 0)
    top = jnp.where(row[:8] < k, hr, r[:8])
    return jnp.concatenate([top, r[8:]], axis=0)


def _shift_up(v, k, tail8, row, T):
    if k == 0:
        return v
    r = pltpu.roll(v, T - k, 0)
    tr = pltpu.roll(tail8, 8 - k, 0)
    bot = jnp.where(row[:8] >= 8 - k, tr, r[T - 8:])
    return jnp.concatenate([r[:T - 8], bot], axis=0)


def _rglru_gates(u, head8, grow, row, T, cw_ref, cb_ref, wa_ref, ba_ref, wx_ref, bx_ref, lam_ref):
    us = [_shift_down(u, k, head8, row, T) for k in range(CONV_W)]
    acc = us[0] * cw_ref[0:1, :]
    for k in range(1, CONV_W):
        acc = acc + us[k] * cw_ref[k:k + 1, :]
    conv = cb_ref[...] + acc
    cbf = conv.astype(BF)
    r_ = _sigmoid(_dot(cbf, wa_ref[0], NN) + ba_ref[...])
    i_ = _sigmoid(_dot(cbf, wx_ref[0], NN) + bx_ref[...])
    sp = _softplus(-lam_ref[...])
    la = -LRU_C * r_ * sp
    a = jnp.exp(la)
    mult_raw = jnp.sqrt(-_expm1(2.0 * la))
    mult = jnp.where(grow == 0, 1.0, mult_raw)
    return us, conv, cbf, r_, i_, sp, a, mult_raw, mult


def _rglru_specs(T, nt, rev):
    tmap = (lambda n, t: (nt - 1 - t, n)) if rev else (lambda n, t: (t, n))
    hmap = ((lambda n, t: (jnp.maximum((nt - 1 - t) * (T // 8) - 1, 0), n)) if rev
            else (lambda n, t: (jnp.maximum(t * (T // 8) - 1, 0), n)))
    tile = pl.BlockSpec((T, RNN_BLOCK), tmap)
    halo = pl.BlockSpec((8, RNN_BLOCK), hmap)
    vec = pl.BlockSpec((1, RNN_BLOCK), lambda n, t: (0, n))
    cw = pl.BlockSpec((CONV_W, RNN_BLOCK), lambda n, t: (0, n))
    wblk = pl.BlockSpec((1, RNN_BLOCK, RNN_BLOCK), lambda n, t: (n, 0, 0))
    return tile, halo, vec, cw, wblk


def _rglru_fwd(xr, g, cw, cb, wa, ba, wx, bx, lam, T):
    S = xr.shape[0]
    nt = S // T

    def body(u_ref, uh_ref, g_ref, cw_ref, cb_ref, wa_ref, ba_ref, wx_ref, bx_ref, lam_ref, h_ref, y_ref, carry):
        t = pl.program_id(1)

        @pl.when(t == 0)
        def _():
            carry[...] = jnp.zeros_like(carry)

        row = lax.broadcasted_iota(jnp.int32, (T, RNN_BLOCK), 0)
        grow = row + t * T
        head8 = jnp.where(t > 0, uh_ref[...], 0.0)
        _, conv, _, _, i_, _, a, _, mult = _rglru_gates(u_ref[...], head8, grow, row, T, cw_ref, cb_ref, wa_ref, ba_ref,
                                                         wx_ref, bx_ref, lam_ref)
        b = mult * i_ * conv
        s = 1
        while s < T:
            keep = row >= s
            a_s = jnp.where(keep, pltpu.roll(a, s, 0), 1.0)
            b_s = jnp.where(keep, pltpu.roll(b, s, 0), 0.0)
            b = a * b_s + b
            a = a * a_s
            s *= 2
        h = b + a * carry[0:1, :]
        carry[...] = jnp.broadcast_to(h[T - 1:T, :], carry.shape)
        h_ref[...] = h
        gv = g_ref[...]
        y_ref[...] = (h * (gv * _sigmoid(gv))).astype(BF)

    tile, halo, vec, cwspec, wblk = _rglru_specs(T, nt, False)
    return pl.pallas_call(
        body, name="rglru_fwd", grid=(RNN_BLOCKS, nt),
        in_specs=[tile, halo, tile, cwspec, vec, wblk, vec, wblk, vec, vec],
        out_specs=[tile, tile],
        out_shape=[jax.ShapeDtypeStruct((S, D_RNN), F32), jax.ShapeDtypeStruct((S, D_RNN), BF)],
        scratch_shapes=[pltpu.VMEM((8, RNN_BLOCK), F32)],
        compiler_params=_params(("parallel", "arbitrary")),
    )(xr, xr, g, cw, cb, wa, ba, wx, bx, lam)


def _rglru_bwd(xr, g, h, dy, cw, cb, wa, ba, wx, bx, lam, T):
    S = xr.shape[0]
    nt = S // T

    def body(u_ref, uh_ref, g_ref, h_ref, hh_ref, dy_ref, cw_ref, cb_ref, wa_ref, ba_ref, wx_ref, bx_ref, lam_ref,
             du_ref, dg_ref, dwa_ref, dwx_ref, dvec_ref, c_dhh, c_a, c_dconv):
        t = pl.program_id(1)
        tt = nt - 1 - t

        @pl.when(t == 0)
        def _():
            c_dhh[...] = jnp.zeros_like(c_dhh)
            c_a[...] = jnp.zeros_like(c_a)
            c_dconv[...] = jnp.zeros_like(c_dconv)
            dwa_ref[...] = jnp.zeros_like(dwa_ref)
            dwx_ref[...] = jnp.zeros_like(dwx_ref)
            dvec_ref[...] = jnp.zeros_like(dvec_ref)

        row = lax.broadcasted_iota(jnp.int32, (T, RNN_BLOCK), 0)
        row8 = row[:8]
        grow = row + tt * T
        head8 = jnp.where(tt > 0, uh_ref[...], 0.0)
        us, conv, cbf, r_, i_, sp, a, mult_raw, mult = _rglru_gates(
            u_ref[...], head8, grow, row, T, cw_ref, cb_ref, wa_ref, ba_ref, wx_ref, bx_ref, lam_ref)
        hv = h_ref[...]
        hprev = _shift_down(hv, 1, jnp.where(tt > 0, hh_ref[...], 0.0), row, T)
        gv = g_ref[...]
        sg = _sigmoid(gv)
        dyv = dy_ref[...]
        dg_ref[...] = (dyv * hv * (sg * (1.0 + gv * (1.0 - sg)))).astype(BF)
        d = dyv * (gv * sg)
        A = _shift_up(a, 1, c_a[...], row, T)
        s = 1
        while s < T:
            keep = row < T - s
            A_s = jnp.where(keep, pltpu.roll(A, T - s, 0), 1.0)
            d_s = jnp.where(keep, pltpu.roll(d, T - s, 0), 0.0)
            d = A * d_s + d
            A = A * A_s
            s *= 2
        dhh = d + A * c_dhh[0:1, :]
        da = dhh * hprev
        dconv = dhh * mult * i_
        di = dhh * mult * conv
        dmult = dhh * i_ * conv
        dla = da * a - jnp.where(grow == 0, 0.0, dmult * (a * a) / mult_raw)
        dr = dla * (-LRU_C * sp)
        dsp = jnp.sum(dla * (-LRU_C * r_), axis=0, keepdims=True)
        dza = dr * r_ * (1.0 - r_)
        dzx = di * i_ * (1.0 - i_)
        dza_b, dzx_b = dza.astype(BF), dzx.astype(BF)
        dconv = dconv + _dot(dza_b, wa_ref[0], NT) + _dot(dzx_b, wx_ref[0], NT)
        dwa_ref[0] += _dot(cbf, dza_b, TN)
        dwx_ref[0] += _dot(cbf, dzx_b, TN)
        lam = lam_ref[...]
        rows = [jnp.sum(dconv * us[k], axis=0, keepdims=True) for k in range(CONV_W)]
        rows += [jnp.sum(dconv, axis=0, keepdims=True), jnp.sum(dza, axis=0, keepdims=True),
                 jnp.sum(dzx, axis=0, keepdims=True), dsp * (-_sigmoid(-lam))]
        upd = jnp.zeros((8, RNN_BLOCK), F32)
        for j, rv in enumerate(rows):
            upd = upd + jnp.where(row8 == j, rv, 0.0)
        dvec_ref[...] += upd
        tail8 = c_dconv[...]
        du = dconv * cw_ref[0:1, :]
        for k in range(1, CONV_W):
            du = du + _shift_up(dconv, k, tail8, row, T) * cw_ref[k:k + 1, :]
        du_ref[...] = du.astype(BF)
        c_dhh[...] = jnp.broadcast_to(dhh[0:1, :], c_dhh.shape)
        c_a[...] = jnp.broadcast_to(a[0:1, :], c_a.shape)
        c_dconv[...] = dconv[:8]

    tile, halo, vec, cwspec, wblk = _rglru_specs(T, nt, True)
    acc8 = pl.BlockSpec((8, RNN_BLOCK), lambda n, t: (0, n))
    return pl.pallas_call(
        body, name="rglru_bwd", grid=(RNN_BLOCKS, nt),
        in_specs=[tile, halo, tile, tile, halo, tile, cwspec, vec, wblk, vec, wblk, vec, vec],
        out_specs=[tile, tile, wblk, wblk, acc8],
        out_shape=[jax.ShapeDtypeStruct((S, D_RNN), BF), jax.ShapeDtypeStruct((S, D_RNN), BF),
                   jax.ShapeDtypeStruct((RNN_BLOCKS, RNN_BLOCK, RNN_BLOCK), F32),
                   jax.ShapeDtypeStruct((RNN_BLOCKS, RNN_BLOCK, RNN_BLOCK), F32),
                   jax.ShapeDtypeStruct((8, D_RNN), F32)],
        scratch_shapes=[pltpu.VMEM((8, RNN_BLOCK), F32)] * 3,
        compiler_params=_params(("parallel", "arbitrary")),
    )(xr, xr, g, h, h, dy, cw, cb, wa, ba, wx, bx, lam)


def _rel_bucket_map():
    qi = np.arange(WINDOW)[:, None]
    kj = np.arange(2 * WINDOW)[None, :]
    dist = jnp.asarray(qi + WINDOW - kj, jnp.int32)
    n = jnp.maximum(dist, 0)
    max_exact = REL_BUCKETS // 2
    ratio = jnp.log(jnp.maximum(n, 1).astype(F32) / max_exact) / math.log(REL_MAX_DIST / max_exact)
    large = jnp.minimum(max_exact + (ratio * (REL_BUCKETS - max_exact)).astype(jnp.int32), REL_BUCKETS - 1)
    bucket = jnp.where(n < max_exact, n, large).astype(jnp.int32)
    j = np.arange(WINDOW)[None, :]
    return jnp.where(jnp.asarray(j > qi), bucket[:, :WINDOW], bucket[:, WINDOW:])


def _swa_common(n, kv_ref, bucket_ref, relb_ref, bias_scr):
    @pl.when(n == 0)
    def _():
        bk = bucket_ref[...]
        for h in range(SWA_HEADS):
            acc = jnp.zeros((WINDOW, WINDOW), F32)
            for b in range(REL_BUCKETS):
                acc = acc + jnp.where(bk == b, relb_ref[b, h], 0.0)
            bias_scr[h] = acc

    prev0 = pl.multiple_of(jnp.maximum(n - 1, 0) * WINDOW, WINDOW)
    cur0 = pl.multiple_of(n * WINDOW, WINDOW)
    kk = jnp.concatenate([kv_ref[pl.ds(prev0, WINDOW), :], kv_ref[pl.ds(cur0, WINDOW), :]], axis=0).astype(F32)
    rowi = lax.broadcasted_iota(jnp.int32, (WINDOW, WINDOW), 0)
    col = lax.broadcasted_iota(jnp.int32, (WINDOW, WINDOW), 1)
    from_prev = col > rowi
    return kk, from_prev, prev0, cur0


def _fold(full, from_prev):
    return jnp.where(from_prev, full[:, :WINDOW], full[:, WINDOW:])


def _unfold(sq, from_prev):
    return jnp.concatenate([jnp.where(from_prev, sq, 0.0), jnp.where(from_prev, 0.0, sq)], axis=1)


def _half_pair(part, kvh):
    lo = lax.broadcasted_iota(jnp.int32, part.shape, 1) < SWA_HD
    if kvh == 0:
        pa = jnp.where(lo, part, 0.0)
        pb = pltpu.roll(pa, SWA_HD, 1)
    else:
        pb = jnp.where(lo, 0.0, part)
        pa = pltpu.roll(pb, SWA_HD, 1)
    return pa.astype(BF), pb.astype(BF)


ALL_HEADS = SWA_HEADS * WINDOW


def _sink_column(sinks):
    return jnp.repeat(sinks.reshape(SWA_HEADS), WINDOW).reshape(ALL_HEADS, 1)


def _swa_operands(kk):
    return [(_half_pair(kk[:, :128], kvh), _half_pair(kk[:, 128:], kvh)) for kvh in range(SWA_KV_HEADS)]


def _swa_probs(n, q_ref, ops, bias_scr, sinkc_ref, from_prev):
    lgs = []
    for kvh in range(SWA_KV_HEADS):
        (ka, kb), _ = ops[kvh]
        for p in range(4):
            q2 = q_ref[:, kvh * 512 + p * 128:kvh * 512 + p * 128 + 128]
            lgs += [_fold(_dot(q2, ka, NT), from_prev), _fold(_dot(q2, kb, NT), from_prev)]
    lg = jnp.concatenate(lgs, axis=0) * (SWA_HD ** -0.5) + bias_scr[...].reshape(ALL_HEADS, WINDOW)
    rowi = jnp.bitwise_and(lax.broadcasted_iota(jnp.int32, (ALL_HEADS, WINDOW), 0), WINDOW - 1)
    col = lax.broadcasted_iota(jnp.int32, (ALL_HEADS, WINDOW), 1)
    no_prev = jnp.where(n > 0, 0, 4 * WINDOW)
    lg = jnp.where(jnp.logical_or(col <= rowi, col > rowi + no_prev), lg, NEG_INF)
    sink = sinkc_ref[...]
    m = jnp.maximum(jnp.max(lg, axis=-1, keepdims=True), sink)
    e = jnp.exp(lg - m)
    es = jnp.exp(sink - m)
    den = jnp.sum(e, axis=-1, keepdims=True) + es
    return e / den, es / den


def _swa_fwd(q, kv, g, bucket, rel_bias, sink_col):
    S = q.shape[0]
    nb = S // WINDOW

    def body(q_ref, kv_ref, g_ref, bucket_ref, relb_ref, sinkc_ref, o_ref, y_ref, bias_scr):
        n = pl.program_id(0)
        kk, from_prev, _, _ = _swa_common(n, kv_ref, bucket_ref, relb_ref, bias_scr)
        ops = _swa_operands(kk)
        pr, _ = _swa_probs(n, q_ref, ops, bias_scr, sinkc_ref, from_prev)
        for kvh in range(SWA_KV_HEADS):
            _, (va, vb) = ops[kvh]
            for p in range(4):
                c0 = kvh * 512 + p * 128
                r0 = (kvh * 8 + 2 * p) * WINDOW
                o2 = (_dot(_unfold(pr[r0:r0 + WINDOW], from_prev).astype(BF), va, NN)
                      + _dot(_unfold(pr[r0 + WINDOW:r0 + 2 * WINDOW], from_prev).astype(BF), vb, NN))
                o_ref[:, c0:c0 + 128] = o2
                gv = g_ref[:, c0:c0 + 128]
                y_ref[:, c0:c0 + 128] = (o2 * (gv * _sigmoid(gv))).astype(BF)

    blk = pl.BlockSpec((WINDOW, 1024), lambda n: (n, 0))
    smem = pl.BlockSpec(memory_space=pltpu.SMEM)
    sinkc = pl.BlockSpec((ALL_HEADS, 1), lambda n: (0, 0))
    return pl.pallas_call(
        body, name="swa_fwd", grid=(nb,),
        in_specs=[blk, pl.BlockSpec((S, 256), lambda n: (0, 0)), blk, pl.BlockSpec((WINDOW, WINDOW), lambda n: (0, 0)), smem, sinkc],
        out_specs=[blk, blk],
        out_shape=[jax.ShapeDtypeStruct((S, 1024), F32), jax.ShapeDtypeStruct((S, 1024), BF)],
        scratch_shapes=[pltpu.VMEM((SWA_HEADS, WINDOW, WINDOW), F32)],
        compiler_params=_params(("arbitrary",)),
    )(q, kv, g, bucket, rel_bias, sink_col)


def _swa_bwd(q, kv, g, o, dy, bucket, rel_bias, sink_col):
    S = q.shape[0]
    nb = S // WINDOW

    def body(q_ref, kv_ref, g_ref, o_ref, dy_ref, bucket_ref, relb_ref, sinkc_ref,
             dq_ref, dg_ref, dkv_ref, dsink_ref, drel_ref, bias_scr, dbias_scr, dsink_scr):
        n = pl.program_id(0)

        @pl.when(n == 0)
        def _():
            dbias_scr[...] = jnp.zeros_like(dbias_scr)
            dsink_scr[...] = jnp.zeros_like(dsink_scr)
            dkv_ref[...] = jnp.zeros_like(dkv_ref)

        kk, from_prev, prev0, cur0 = _swa_common(n, kv_ref, bucket_ref, relb_ref, bias_scr)
        ops = _swa_operands(kk)
        pr, ps = _swa_probs(n, q_ref, ops, bias_scr, sinkc_ref, from_prev)
        do2s, dps = [], []
        for kvh in range(SWA_KV_HEADS):
            _, (va, vb) = ops[kvh]
            for p in range(4):
                c0 = kvh * 512 + p * 128
                gv = g_ref[:, c0:c0 + 128]
                sg = _sigmoid(gv)
                dyv = dy_ref[:, c0:c0 + 128]
                dg_ref[:, c0:c0 + 128] = (dyv * o_ref[:, c0:c0 + 128] * (sg * (1.0 + gv * (1.0 - sg)))).astype(BF)
                do2 = (dyv * (gv * sg)).astype(BF)
                do2s.append(do2)
                dps += [_fold(_dot(do2, va, NT), from_prev), _fold(_dot(do2, vb, NT), from_prev)]
        dp = jnp.concatenate(dps, axis=0)
        delta = jnp.sum(pr * dp, axis=-1, keepdims=True)
        ds = pr * (dp - delta)
        dbias_scr[...] += ds.reshape(SWA_HEADS, WINDOW, WINDOW)
        dsink_scr[...] += ps * delta
        dsc = ds * (SWA_HD ** -0.5)
        lo256 = lax.broadcasted_iota(jnp.int32, (2 * WINDOW, 128), 1) < SWA_HD
        dks, dvs = [], []
        for kvh in range(SWA_KV_HEADS):
            (ka, kb), _ = ops[kvh]
            dka = jnp.zeros((2 * WINDOW, 128), F32)
            dkb, dva, dvb = dka, dka, dka
            for p in range(4):
                c0 = kvh * 512 + p * 128
                r0 = (kvh * 8 + 2 * p) * WINDOW
                q2 = q_ref[:, c0:c0 + 128]
                do2 = do2s[kvh * 4 + p]
                ds0 = _unfold(dsc[r0:r0 + WINDOW], from_prev).astype(BF)
                ds1 = _unfold(dsc[r0 + WINDOW:r0 + 2 * WINDOW], from_prev).astype(BF)
                dq_ref[:, c0:c0 + 128] = (_dot(ds0, ka, NN) + _dot(ds1, kb, NN)).astype(BF)
                dka = dka + _dot(ds0, q2, TN)
                dkb = dkb + _dot(ds1, q2, TN)
                dva = dva + _dot(_unfold(pr[r0:r0 + WINDOW], from_prev).astype(BF), do2, TN)
                dvb = dvb + _dot(_unfold(pr[r0 + WINDOW:r0 + 2 * WINDOW], from_prev).astype(BF), do2, TN)
            dks.append(jnp.where(lo256, dka, 0.0) + pltpu.roll(jnp.where(lo256, 0.0, dkb), SWA_HD, 1))
            dvs.append(jnp.where(lo256, dva, 0.0) + pltpu.roll(jnp.where(lo256, 0.0, dvb), SWA_HD, 1))
        dk = dks[0] + pltpu.roll(dks[1], SWA_HD, 1)
        dv = dvs[0] + pltpu.roll(dvs[1], SWA_HD, 1)
        dkv_ref[pl.ds(prev0, WINDOW), 0:128] += dk[:WINDOW]
        dkv_ref[pl.ds(prev0, WINDOW), 128:256] += dv[:WINDOW]
        dkv_ref[pl.ds(cur0, WINDOW), 0:128] += dk[WINDOW:]
        dkv_ref[pl.ds(cur0, WINDOW), 128:256] += dv[WINDOW:]

        @pl.when(n == nb - 1)
        def _():
            dsink_ref[...] = -jnp.sum(dsink_scr[...].reshape(SWA_HEADS, WINDOW, 1), axis=1)
            bk = bucket_ref[...]
            sums = []
            for b in range(REL_BUCKETS):
                sums.append(jnp.sum(jnp.where((bk == b)[None], dbias_scr[...], 0.0), axis=1))
            drel_ref[...] = jnp.sum(jnp.concatenate(sums, axis=0), axis=1, keepdims=True)

    blk = pl.BlockSpec((WINDOW, 1024), lambda n: (n, 0))
    smem = pl.BlockSpec(memory_space=pltpu.SMEM)
    whole = lambda shape: pl.BlockSpec(shape, lambda n: (0, 0))
    return pl.pallas_call(
        body, name="swa_bwd", grid=(nb,),
        in_specs=[blk, whole((S, 256)), blk, blk, blk, whole((WINDOW, WINDOW)), smem, whole((ALL_HEADS, 1))],
        out_specs=[blk, blk, whole((S, 256)), whole((SWA_HEADS, 1)), whole((REL_BUCKETS * SWA_HEADS, 1))],
        out_shape=[jax.ShapeDtypeStruct((S, 1024), BF), jax.ShapeDtypeStruct((S, 1024), BF),
                   jax.ShapeDtypeStruct((S, 256), F32), jax.ShapeDtypeStruct((SWA_HEADS, 1), F32),
                   jax.ShapeDtypeStruct((REL_BUCKETS * SWA_HEADS, 1), F32)],
        scratch_shapes=[pltpu.VMEM((SWA_HEADS, WINDOW, WINDOW), F32), pltpu.VMEM((SWA_HEADS, WINDOW, WINDOW), F32),
                        pltpu.VMEM((ALL_HEADS, 1), F32)],
        compiler_params=_params(("arbitrary",)),
    )(q, kv, g, o, dy, bucket, rel_bias, sink_col)


def _mem_probs(qh, mk):
    lg = _dot(qh, mk, NT) * (MEM_HD ** -0.5)
    e = jnp.exp(lg - jnp.max(lg, axis=-1, keepdims=True))
    return e / jnp.sum(e, axis=-1, keepdims=True)


def _mem_fwd(q, mkv, g):
    S = q.shape[0]
    M = mkv.shape[0]
    tq = 256

    def body(q_ref, mkv_ref, g_ref, o_ref, y_ref):
        for h in range(MEM_HEADS):
            c0 = h * MEM_HD
            pr = _mem_probs(q_ref[:, c0:c0 + MEM_HD], mkv_ref[:, c0:c0 + MEM_HD])
            o = _dot(pr.astype(BF), mkv_ref[:, D_MEM + c0:D_MEM + c0 + MEM_HD], NN)
            o_ref[:, c0:c0 + MEM_HD] = o
            gv = g_ref[:, c0:c0 + MEM_HD]
            y_ref[:, c0:c0 + MEM_HD] = (o * (gv * _sigmoid(gv))).astype(BF)

    blk = pl.BlockSpec((tq, D_MEM), lambda i: (i, 0))
    return pl.pallas_call(
        body, name="mem_fwd", grid=(S // tq,),
        in_specs=[blk, pl.BlockSpec((M, 2 * D_MEM), lambda i: (0, 0)), blk], out_specs=[blk, blk],
        out_shape=[jax.ShapeDtypeStruct((S, D_MEM), F32), jax.ShapeDtypeStruct((S, D_MEM), BF)],
        compiler_params=_params(("parallel",)),
    )(q, mkv, g)


def _mem_bwd(q, mkv, g, o, dy):
    S = q.shape[0]
    M = mkv.shape[0]
    tq = 256

    def body(q_ref, mkv_ref, g_ref, o_ref, dy_ref, dq_ref, dg_ref, dmkv_ref):
        @pl.when(pl.program_id(0) == 0)
        def _():
            dmkv_ref[...] = jnp.zeros_like(dmkv_ref)

        for h in range(MEM_HEADS):
            c0 = h * MEM_HD
            qh = q_ref[:, c0:c0 + MEM_HD]
            mk = mkv_ref[:, c0:c0 + MEM_HD]
            mv = mkv_ref[:, D_MEM + c0:D_MEM + c0 + MEM_HD]
            gv = g_ref[:, c0:c0 + MEM_HD]
            sg = _sigmoid(gv)
            dyv = dy_ref[:, c0:c0 + MEM_HD]
            dg_ref[:, c0:c0 + MEM_HD] = (dyv * o_ref[:, c0:c0 + MEM_HD] * (sg * (1.0 + gv * (1.0 - sg)))).astype(BF)
            do = (dyv * (gv * sg)).astype(BF)
            pr = _mem_probs(qh, mk)
            dp = _dot(do, mv, NT)
            ds = pr * (dp - jnp.sum(pr * dp, axis=-1, keepdims=True))
            dsb = (ds * (MEM_HD ** -0.5)).astype(BF)
            dq_ref[:, c0:c0 + MEM_HD] = _dot(dsb, mk, NN).astype(BF)
            dmkv_ref[:, c0:c0 + MEM_HD] += _dot(dsb, qh, TN)
            dmkv_ref[:, D_MEM + c0:D_MEM + c0 + MEM_HD] += _dot(pr.astype(BF), do, TN)

    blk = pl.BlockSpec((tq, D_MEM), lambda i: (i, 0))
    whole = pl.BlockSpec((M, 2 * D_MEM), lambda i: (0, 0))
    return pl.pallas_call(
        body, name="mem_bwd", grid=(S // tq,),
        in_specs=[blk, whole, blk, blk, blk], out_specs=[blk, blk, whole],
        out_shape=[jax.ShapeDtypeStruct((S, D_MEM), BF), jax.ShapeDtypeStruct((S, D_MEM), BF),
                   jax.ShapeDtypeStruct((M, 2 * D_MEM), F32)],
        compiler_params=_params(("arbitrary",)),
    )(q, mkv, g, o, dy)


MERGE_TN = 512


def _merge_specs(tm):
    ytile = pl.BlockSpec((tm, 1024), lambda i, j: (i, 0))
    wblk = pl.BlockSpec((1024, MERGE_TN), lambda i, j: (0, j))
    gls = [pl.BlockSpec((None, tm, MERGE_TN), (lambda i, j, br=br: (br, i, j))) for br in range(3)]
    otile = pl.BlockSpec((tm, MERGE_TN), lambda i, j: (i, j))
    return ytile, wblk, gls, otile


def _merge_fwd(ys, ws, gl, tm):
    S = gl.shape[1]

    def body(y0, y1, y2, w0, w1, w2, g0, g1, g2, o_ref):
        acc = None
        for y_ref, w_ref, g_ref in ((y0, w0, g0), (y1, w1, g1), (y2, w2, g2)):
            term = _sigmoid(g_ref[...]) * _dot(y_ref[...], w_ref[...], NN)
            acc = term if acc is None else acc + term
        o_ref[...] = acc.astype(BF)

    ytile, wblk, gls, otile = _merge_specs(tm)
    return pl.pallas_call(
        body, name="merge_fwd", grid=(S // tm, D_MODEL // MERGE_TN),
        in_specs=[ytile] * 3 + [wblk] * 3 + gls, out_specs=otile,
        out_shape=jax.ShapeDtypeStruct((S, D_MODEL), BF),
        compiler_params=_params(("parallel", "arbitrary")),
    )(*ys, *ws, gl, gl, gl)


def _merge_bwd(dout, w_out, ys, ws, gl, tm):
    S = gl.shape[1]

    def body(do_ref, wo_ref, y0, y1, y2, w0, w1, w2, g0, g1, g2, dg0, dg1, dg2, dp0, dp1, dp2):
        dm = _dot(do_ref[...], wo_ref[...], NT)
        for y_ref, w_ref, g_ref, dg_ref, dp_ref in ((y0, w0, g0, dg0, dp0), (y1, w1, g1, dg1, dp1), (y2, w2, g2, dg2, dp2)):
            gate = _sigmoid(g_ref[...])
            pv = _dot(y_ref[...], w_ref[...], NN)
            dg_ref[...] = (dm * pv * gate * (1.0 - gate)).astype(BF)
            dp_ref[...] = (dm * gate).astype(BF)

    ytile, wblk, gls, otile = _merge_specs(tm)
    out = jax.ShapeDtypeStruct((S, D_MODEL), BF)
    return pl.pallas_call(
        body, name="merge_bwd", grid=(S // tm, D_MODEL // MERGE_TN),
        in_specs=[pl.BlockSpec((tm, D_MODEL), lambda i, j: (i, 0)), pl.BlockSpec((MERGE_TN, D_MODEL), lambda i, j: (j, 0))]
        + [ytile] * 3 + [wblk] * 3 + gls,
        out_specs=[otile] * 6, out_shape=[out] * 6,
        compiler_params=_params(("parallel", "arbitrary")),
    )(dout, w_out, *ys, *ws, gl, gl, gl)


def _out_loss(merged, w_out, x, target, post_g, tm):
    S = x.shape[0]

    def body(m_ref, w_ref, x_ref, t_ref, g_ref, dout_ref, dy_ref, loss_ref, dpost_ref):
        @pl.when(pl.program_id(0) == 0)
        def _():
            loss_ref[...] = jnp.zeros_like(loss_ref)
            dpost_ref[...] = jnp.zeros_like(dpost_ref)

        out = _dot(m_ref[...], w_ref[...], NN)
        r = lax.rsqrt(jnp.mean(out * out, axis=-1, keepdims=True) + EPS)
        nrm = out * r
        gv = g_ref[...]
        err = (x_ref[...] + nrm * gv) - t_ref[...]
        sq = jnp.sum(jnp.sum(err * err, axis=1, keepdims=True), axis=0, keepdims=True)
        loss_ref[...] += sq * (0.5 / D_MODEL)
        dy = err * (1.0 / D_MODEL)
        dy_ref[...] = dy
        dpost_ref[...] += jnp.sum(dy * nrm, axis=0, keepdims=True)
        dn = dy * gv
        dout_ref[...] = (r * (dn - nrm * jnp.mean(dn * nrm, axis=-1, keepdims=True))).astype(BF)

    row = pl.BlockSpec((tm, D_MODEL), lambda i: (i, 0))
    return pl.pallas_call(
        body, name="out_loss", grid=(S // tm,),
        in_specs=[row, pl.BlockSpec((D_MODEL, D_MODEL), lambda i: (0, 0)), row, row, pl.BlockSpec((1, D_MODEL), lambda i: (0, 0))],
        out_specs=[row, row, pl.BlockSpec((8, 128), lambda i: (0, 0)), pl.BlockSpec((1, D_MODEL), lambda i: (0, 0))],
        out_shape=[jax.ShapeDtypeStruct((S, D_MODEL), BF), jax.ShapeDtypeStruct((S, D_MODEL), F32),
                   jax.ShapeDtypeStruct((8, 128), F32), jax.ShapeDtypeStruct((1, D_MODEL), F32)],
        compiler_params=_params(("arbitrary",)),
    )(merged, w_out, x, target, post_g)


def _dh_dx(dproj, w_in, x, dy, pre_g, tm, tk):
    S = x.shape[0]
    nk = D_IN // tk

    def body(dp_ref, w_ref, x_ref, dy_ref, g_ref, dx_ref, dpre_ref, acc_ref):
        i, k = pl.program_id(0), pl.program_id(1)

        @pl.when(jnp.logical_and(i == 0, k == 0))
        def _():
            dpre_ref[...] = jnp.zeros_like(dpre_ref)

        @pl.when(k == 0)
        def _():
            acc_ref[...] = jnp.zeros_like(acc_ref)

        acc_ref[...] += _dot(dp_ref[...], w_ref[...], NN)

        @pl.when(k == nk - 1)
        def _():
            dh = acc_ref[...]
            xv = x_ref[...]
            r = lax.rsqrt(jnp.mean(xv * xv, axis=-1, keepdims=True) + EPS)
            nrm = xv * r
            dpre_ref[...] += jnp.sum(dh * nrm, axis=0, keepdims=True)
            dn = dh * g_ref[...]
            dx_ref[...] = r * (dn - nrm * jnp.mean(dn * nrm, axis=-1, keepdims=True)) + dy_ref[...]

    row = pl.BlockSpec((tm, D_MODEL), lambda i, k: (i, 0))
    vec = pl.BlockSpec((1, D_MODEL), lambda i, k: (0, 0))
    return pl.pallas_call(
        body, name="dh_dx", grid=(S // tm, nk),
        in_specs=[pl.BlockSpec((tm, tk), lambda i, k: (i, k)), pl.BlockSpec((tk, D_MODEL), lambda i, k: (k, 0)), row, row, vec],
        out_specs=[row, vec],
        out_shape=[jax.ShapeDtypeStruct((S, D_MODEL), F32), jax.ShapeDtypeStruct((1, D_MODEL), F32)],
        scratch_shapes=[pltpu.VMEM((tm, D_MODEL), F32)],
        compiler_params=_params(("arbitrary", "arbitrary"), 56),
    )(dproj, w_in, x, dy, pre_g)


def _sum_parts(parts, name):
    P, R, C = parts.shape
    tr = max(t for t in range(8, 513, 8) if R % t == 0)

    def body(p_ref, o_ref):
        acc = p_ref[0]
        for j in range(1, P):
            acc = acc + p_ref[j]
        o_ref[...] = acc

    return pl.pallas_call(
        body, name=name, grid=(R // tr,),
        in_specs=[pl.BlockSpec((P, tr, C), lambda i: (0, i, 0))], out_specs=pl.BlockSpec((tr, C), lambda i: (i, 0)),
        out_shape=jax.ShapeDtypeStruct((R, C), F32), compiler_params=_params(("parallel",)),
    )(parts)


def _adamw(parts, w, m, v, name):
    groups = list(parts) if isinstance(parts, (list, tuple)) else [parts]
    P, rows, C = groups[0].shape
    R = rows * len(groups)
    tr = 128 if rows % 128 == 0 else rows
    per = rows // tr
    c1 = 1.0 - ADAM_B1 ** ADAM_STEP
    c2 = 1.0 - ADAM_B2 ** ADAM_STEP

    def body(*refs):
        p_refs = refs[:len(groups)]
        w_ref, m_ref, v_ref, g_ref, d_ref, nm_ref, nv_ref = refs[len(groups):]
        g = None
        for q, p_ref in enumerate(p_refs):
            gq = p_ref[0].astype(F32)
            for j in range(1, P):
                gq = gq + p_ref[j].astype(F32)
            g = gq if g is None else jnp.where(pl.program_id(0) // per == q, gq, g)
        nm = ADAM_B1 * m_ref[...] + (1.0 - ADAM_B1) * g
        nv = ADAM_B2 * v_ref[...] + (1.0 - ADAM_B2) * (g * g)
        g_ref[...] = g
        nm_ref[...] = nm
        nv_ref[...] = nv
        d_ref[...] = -ADAM_LR * ((nm / c1) / (jnp.sqrt(nv / c2) + ADAM_EPS) + ADAM_WD * w_ref[...])

    tile = pl.BlockSpec((tr, C), lambda i: (i, 0))
    out = jax.ShapeDtypeStruct((R, C), F32)
    return pl.pallas_call(
        body, name=name, grid=(R // tr,),
        in_specs=[pl.BlockSpec((P, tr, C), (lambda i, q=q: (0, jnp.clip(i - q * per, 0, per - 1), 0))) for q in range(len(groups))]
        + [tile, tile, tile], out_specs=[tile] * 4, out_shape=[out] * 4,
        compiler_params=_params(("parallel",)),
    )(*groups, w, m, v)


def _forward_a(x, mem, pre_g, mem_g, w_in, conv_w, conv_b, w_a, b_a, w_x, b_x, lam, sinks, rel_bias):
    S = x.shape[0]
    st = dict(T=min(512, S // 2), tm=min(512, S), bucket=_rel_bucket_map())
    st["h"] = _rms_fwd(x, pre_g, "pre_norm")
    st["memn"] = _rms_fwd(mem, mem_g, "mem_norm")
    seg = {}
    for name, c0, width, dt in SEGMENTS:
        seg[name] = _matmul(st["h"], w_in, "nt", S, width, D_MODEL, S, SEG_TILE, D_MODEL, dt, "proj_" + name, b_noff=c0 // SEG_TILE,
                            out_blocked="third" if name == "gl" else None)
    st["seg"] = seg
    st["h_rg"], st["y_rg"] = _rglru_fwd(seg["xr"], seg["g_rg"], conv_w, conv_b, w_a, b_a, w_x, b_x, lam, st["T"])
    st["o_swa"], st["y_swa"] = _swa_fwd(seg["q_s"], seg["kv"], seg["g_swa"], st["bucket"], rel_bias, _sink_column(sinks))
    return st


def _forward_b(st, x, target, post_g, w_memkv, wbr, w_out):
    S = x.shape[0]
    M = st["memn"].shape[0]
    seg = st["seg"]
    st["mkv"] = _matmul(st["memn"], w_memkv, "nn", M, 2 * D_MEM, D_MODEL, M, 512, D_MODEL, BF, "mem_kv")
    st["o_mem"], st["y_mem"] = _mem_fwd(seg["q_m"], st["mkv"], seg["g_mem"])
    st["ys"] = (st["y_rg"], st["y_swa"], st["y_mem"])
    st["merged"] = _merge_fwd(st["ys"], wbr, seg["gl"], st["tm"])
    st["dout"], st["dy"], st["loss"], st["dpost"] = _out_loss(st["merged"], w_out, x, target, post_g, min(256, S))
    return st


def _backward_a(st, mem, w_memkv, wbr, w_out, conv_w, conv_b, w_a, b_a, w_x, b_x, lam):
    S = st["h"].shape[0]
    M = mem.shape[0]
    seg, ys, tm = st["seg"], st["ys"], st["tm"]
    st["dw_out"] = _matmul(st["merged"], st["dout"], "tn", D_MODEL, D_MODEL, S, 256, D_MODEL, S, BF, "dw_out", out_blocked="row")
    dgl0, dgl1, dgl2, dp0, dp1, dp2 = _merge_bwd(st["dout"], w_out, ys, wbr, seg["gl"], tm)
    st["dgl"] = (dgl0, dgl1, dgl2)
    dys, dwbr = [], []
    for i, dp in enumerate((dp0, dp1, dp2)):
        dys.append(_matmul(dp, wbr[i], "nt", S, 1024, D_MODEL, tm, 1024, D_MODEL, F32, "dy_br%d" % i))
        dwbr.append(_matmul(ys[i], dp, "tn", 1024, D_MODEL, S, 1024, 256, S, BF, "dw_br%d" % i, out_blocked="col"))
    st["dys"], st["dwbr"] = dys, dwbr
    st["dq_m"], st["dg_mem"], dmkv = _mem_bwd(seg["q_m"], st["mkv"], seg["g_mem"], st["o_mem"], dys[2])
    dmkv_b = dmkv.astype(BF)
    st["dw_memkv"] = _matmul(st["memn"], dmkv_b, "tn", D_MODEL, 2 * D_MEM, M, 256, 2 * D_MEM, M, BF, "dw_memkv", out_blocked="row")
    dmemn = _matmul(dmkv_b, w_memkv, "nt", M, D_MODEL, 2 * D_MEM, M, 512, 2 * D_MEM, F32, "dmemn")
    st["dmem_g"] = _rms_gain_grad(dmemn, mem, "dmem_gain")
    st["dxr"], st["dg_rg"], st["dw_a"], st["dw_x"], st["dvec"] = _rglru_bwd(
        seg["xr"], seg["g_rg"], st["h_rg"], dys[0], conv_w, conv_b, w_a, b_a, w_x, b_x, lam, st["T"])
    return st


def _backward_b(st, rel_bias, sinks):
    seg = st["seg"]
    dq_s, dg_swa, dkv, dsinks, drel = _swa_bwd(seg["q_s"], seg["kv"], seg["g_swa"], st["o_swa"], st["dys"][1],
                                               st["bucket"], rel_bias, _sink_column(sinks))
    st["dsinks"], st["drel"] = dsinks.reshape(1, SWA_HEADS), drel.reshape(REL_BUCKETS, SWA_HEADS)
    st["dproj"] = jnp.concatenate([st["dxr"], st["dg_rg"], dq_s, dkv.astype(BF), dg_swa, st["dq_m"], st["dg_mem"], *st["dgl"]], axis=1)
    return st


def _dw_in_half(st, half, dep=None):
    S = st["h"].shape[0]
    return _matmul(st["h"], st["dproj"], "tn", D_MODEL // 2, D_IN, S, 512, 1792, S, BF, "dw_in%d" % half, a_moff=2 * half, dep=dep)


def _owner_blocks(a):
    return jnp.swapaxes(a.reshape((4, 2) + a.shape[1:]), 0, 1)


def _local_step(x, mem, target, pre_g, post_g, mem_g, w_in, conv_w, conv_b, w_a, b_a, w_x, b_x, lam, sinks, rel_bias,
                w_memkv, wbr, w_out):
    st = _forward_a(x, mem, pre_g, mem_g, w_in, conv_w, conv_b, w_a, b_a, w_x, b_x, lam, sinks, rel_bias)
    st = _forward_b(st, x, target, post_g, w_memkv, wbr, w_out)
    st = _backward_a(st, mem, w_memkv, wbr, w_out, conv_w, conv_b, w_a, b_a, w_x, b_x, lam)
    st = _backward_b(st, rel_bias, sinks)
    st["dw_in"] = jnp.concatenate([_dw_in_half(st, 0), _dw_in_half(st, 1)], axis=0)
    st["grad_x"], st["dpre"] = _dh_dx(st["dproj"], w_in, x, st["dy"], pre_g, st["tm"], 896)
    return st


def _pad_rows(a, rows):
    a = a.reshape(-1, 128) if a.shape[-1] % 128 == 0 else jnp.pad(a, ((0, 0), (0, 128 - a.shape[-1])))
    return jnp.pad(a, ((0, rows - a.shape[0]), (0, 0))) if a.shape[0] < rows else a


def kernel(x, mem, pre_norm_g, post_norm_g, mem_norm_g, w_in, conv_w, conv_b, w_rg_a, b_rg_a, w_rg_x, b_rg_x, lru_lambda, swa_sinks, rel_bias, w_mem_kv, w_br_rg, w_br_swa, w_br_mem, w_out, loss_target, m_pre_norm_g, m_post_norm_g, m_mem_norm_g, m_w_in, m_conv_w, m_conv_b, m_w_rg_a, m_b_rg_a, m_w_rg_x, m_b_rg_x, m_lru_lambda, m_swa_sinks, m_rel_bias, m_w_mem_kv, m_w_br_rg, m_w_br_swa, m_w_br_mem, m_w_out, v_pre_norm_g, v_post_norm_g, v_mem_norm_g, v_w_in, v_conv_w, v_conv_b, v_w_rg_a, v_b_rg_a, v_w_rg_x, v_b_rg_x, v_lru_lambda, v_swa_sinks, v_rel_bias, v_w_mem_kv, v_w_br_rg, v_w_br_swa, v_w_br_mem, v_w_out):
    cx, cy, cc = lax.axis_index("x"), lax.axis_index("y"), lax.axis_index("c")
    me = 4 * cx + 2 * cy + cc
    chip = 2 * cx + cy
    core = jnp.reshape(cc, (1,)).astype(jnp.int32)
    x0, mem0 = x[0], mem[0]
    w_a_b, w_x_b = w_rg_a[0].astype(BF), w_rg_x[0].astype(BF)

    def landing(own, slot, slots, kind="lead"):
        if kind == "cols":
            return lax.dynamic_update_slice(lax.empty((own.shape[0], slots * own.shape[1]), own.dtype), own, (0, slot * own.shape[1]))
        return lax.dynamic_update_slice(lax.empty((slots,) + own.shape, own.dtype), own[None], (slot,) + (0,) * own.ndim)

    def swap_start(parts, tag):
        return _exchange_start(parts, [lax.empty(p.shape[1:], p.dtype) for p in parts], _plan_swap(len(parts)), "swap_%s_start" % tag)

    def scatter_start(swap, after, tag):
        s_send, s_recv, parts, got, _ = swap
        got = _exchange_wait(s_send, s_recv, parts, got, _plan_swap(len(parts)), after, "swap_%s_wait" % tag)
        sums = [_pair_sum(p, g, core, "scatter_%s_sum%d" % (tag, i)) for i, (p, g) in enumerate(zip(parts, got))]
        lands = [landing(lax.dynamic_index_in_dim(s, chip, 0, keepdims=False), chip, 4) for s in sums]
        return _exchange_start(sums, lands, _plan_scatter(len(sums)), "scatter_%s_start" % tag)

    def zero_after(a):
        return jnp.minimum(jnp.abs(a.reshape(-1)[0].astype(F32)), 0.0)

    g_in, g_cw = _all_gather_relayed([jnp.transpose(w_in[0]).astype(BF), conv_w[0]], [True, False], "gather_w_in")
    w_in_f = g_in.reshape(D_IN, D_MODEL)
    conv_w_f = jnp.transpose(g_cw, (1, 0, 2)).reshape(CONV_W, D_RNN)

    after_first = zero_after(g_cw).astype(BF)
    rest = [w.astype(BF) + after_first for w in (w_mem_kv[0], w_br_rg[0], w_br_swa[0], w_br_mem[0], w_out[0])]
    kinds = ["lead", "cols", "cols", "cols", "lead"]
    plan_g = _plan_gather(kinds)
    g_send, g_recv, g_src, g_land, g_token = _exchange_start(rest, [landing(w, me, N_DEV, kd) for w, kd in zip(rest, kinds)], plan_g,
                                                             "gather_rest_start")
    st = _forward_a(x0, mem0, pre_norm_g + g_token[0:1, 0:1], mem_norm_g, w_in_f, conv_w_f, conv_b, w_a_b, b_rg_a, w_x_b, b_rg_x,
                    lru_lambda, swa_sinks, rel_bias)
    g_land = _exchange_wait(g_send, g_recv, g_src, g_land, plan_g, st["y_swa"], "gather_rest_wait")
    g_land = _forward_to_sibling(g_land, kinds, "gather_rest_forward")
    w_memkv_f = g_land[0].reshape(D_MODEL, 2 * D_MEM)
    wbr = (g_land[1], g_land[2], g_land[3])
    w_out_f = g_land[4].reshape(D_MODEL, D_MODEL)

    st = _forward_b(st, x0, loss_target[0], post_norm_g, w_memkv_f, wbr, w_out_f)
    st = _backward_a(st, mem0, w_memkv_f, wbr, w_out_f, conv_w_f, conv_b, w_a_b, b_rg_a, w_x_b, b_rg_x, lru_lambda)
    parts_a = [st["dw_memkv"], st["dwbr"][0], st["dwbr"][1], st["dwbr"][2], st["dw_out"],
               _owner_blocks(st["dw_a"]), _owner_blocks(st["dw_x"])]
    plan_a = _plan_scatter(len(parts_a))
    swap_a = swap_start(parts_a, "a")

    st = _backward_b(st, rel_bias, swa_sinks + swap_a[4][0:1, 0:1])
    a_send, a_recv, a_src, a_land, a_token = scatter_start(swap_a, st["dproj"], "a")
    plan_b = _plan_scatter(1)

    def dw_in_parts(half, dep):
        dwh = _dw_in_half(st, half, dep)
        return dwh, [jnp.transpose(dwh.reshape(D_MODEL // 2, 4, 2, D_IN // N_DEV), (2, 1, 0, 3))]

    dw0, parts_b0 = dw_in_parts(0, a_token)
    swap_b0 = swap_start(parts_b0, "b0")
    dw1, parts_b1 = dw_in_parts(1, swap_b0[4])
    halves = [scatter_start(swap_b0, dw1, "b0")]
    swap_b1 = swap_start(parts_b1, "b1")
    a_land = _exchange_wait(a_send, a_recv, a_src, a_land, plan_a, swap_b1[4], "scatter_a_wait")
    g_wa_blk = _sum_parts(a_land[5], "sum_w_rg_a")
    g_wx_blk = _sum_parts(a_land[6], "sum_w_rg_x")
    big = [None]
    for j, (wt, mt, vt) in enumerate(((w_mem_kv, m_w_mem_kv, v_w_mem_kv), (w_br_rg, m_w_br_rg, v_w_br_rg),
                                      (w_br_swa, m_w_br_swa, v_w_br_swa), (w_br_mem, m_w_br_mem, v_w_br_mem), (w_out, m_w_out, v_w_out))):
        big.append([a[None] for a in _adamw(a_land[j], wt[0], mt[0], vt[0], "adamw_big%d" % (j + 1))])
    halves.append(scatter_start(swap_b1, big[5][1], "b1"))
    grad_x, dpre = _dh_dx(st["dproj"], w_in_f, x0, st["dy"], pre_norm_g + halves[1][4][0:1, 0:1], st["tm"], 896)
    after, b_lands = grad_x, []
    for half, (b_send, b_recv, b_src, b_land, _) in enumerate(halves):
        b_lands.append(_exchange_wait(b_send, b_recv, b_src, b_land, plan_b, after, "scatter_b%d_wait" % half)[0])
        after = b_lands[-1]
    big[0] = [a[None] for a in _adamw(b_lands, w_in[0], m_w_in[0], v_w_in[0], "adamw_big0")]
    links_free = jnp.minimum(jnp.abs(big[0][0][0, 0, 0]), 0.0)

    pack = jnp.concatenate([dpre.reshape(16, 128), st["dpost"].reshape(16, 128), st["dmem_g"].reshape(16, 128),
                            st["dvec"].reshape(64, 128), _pad_rows(st["dsinks"], 8), _pad_rows(st["drel"], 32), g_wa_blk, g_wx_blk], axis=0) + links_free
    gathered = _all_gather([pack], "gather_small")[0]
    gs = _sum_parts(gathered, "sum_small")
    g_pre, g_post, g_memg = gs[0:16].reshape(1, D_MODEL), gs[16:32].reshape(1, D_MODEL), gs[32:48].reshape(1, D_MODEL)
    gvec = gs[48:112].reshape(8, D_RNN)
    g_conv_w = lax.dynamic_slice(gvec[0:CONV_W], (0, me * RNN_BLOCK), (CONV_W, RNN_BLOCK))
    g_conv_b, g_b_a, g_b_x, g_lam = gvec[4:5], gvec[5:6], gvec[6:7], gvec[7:8]
    g_sinks = gs[112:113, :SWA_HEADS]
    g_rel = gs[120:152, :SWA_HEADS]
    g_w_a = gathered[:, 152:280]
    g_w_x = gathered[:, 280:408]

    def packed(ts):
        pre, post, memg, cb, ba, bx, lm, wa, wx, sk, rel, cw = ts
        return jnp.concatenate([pre.reshape(16, 128), post.reshape(16, 128), memg.reshape(16, 128), cb.reshape(8, 128),
                                ba.reshape(8, 128), bx.reshape(8, 128), lm.reshape(8, 128), wa.reshape(1024, 128),
                                wx.reshape(1024, 128), _pad_rows(sk.reshape(1, SWA_HEADS), 8), _pad_rows(rel, 32),
                                _pad_rows(cw.reshape(CONV_W, RNN_BLOCK), 8)], axis=0)

    def unpacked(a):
        return (a[0:16].reshape(1, D_MODEL), a[16:32].reshape(1, D_MODEL), a[32:48].reshape(1, D_MODEL), a[48:56].reshape(1, D_RNN),
                a[56:64].reshape(1, D_RNN), a[64:72].reshape(1, D_RNN), a[72:80].reshape(1, D_RNN),
                a[80:1104].reshape(1, RNN_BLOCKS, RNN_BLOCK, RNN_BLOCK), a[1104:2128].reshape(1, RNN_BLOCKS, RNN_BLOCK, RNN_BLOCK),
                a[2128:2129, :SWA_HEADS], a[2136:2168, :SWA_HEADS], a[2168:2172].reshape(1, CONV_W, RNN_BLOCK))

    g_small = (g_pre, g_post, g_memg, g_conv_b, g_b_a, g_b_x, g_lam, g_w_a, g_w_x, g_sinks, g_rel, g_conv_w)
    w_small = (pre_norm_g, post_norm_g, mem_norm_g, conv_b, b_rg_a, b_rg_x, lru_lambda, w_rg_a, w_rg_x, swa_sinks, rel_bias, conv_w)
    m_small = (m_pre_norm_g, m_post_norm_g, m_mem_norm_g, m_conv_b, m_b_rg_a, m_b_rg_x, m_lru_lambda, m_w_rg_a, m_w_rg_x, m_swa_sinks, m_rel_bias, m_conv_w)
    v_small = (v_pre_norm_g, v_post_norm_g, v_mem_norm_g, v_conv_b, v_b_rg_a, v_b_rg_x, v_lru_lambda, v_w_rg_a, v_w_rg_x, v_swa_sinks, v_rel_bias, v_conv_w)
    sm = [unpacked(a) for a in _adamw(packed(g_small)[None], packed(w_small), packed(m_small), packed(v_small), "adamw_small")]


    loss_total = lax.psum(st["loss"][0, 0], AXES)

    def leaves(k):
        s = sm[k]
        return [s[0], s[1], s[2], big[0][k], s[11], s[3], s[7], s[4], s[8], s[5], s[6], s[9], s[10],
                big[1][k], big[2][k], big[3][k], big[4][k], big[5][k]]

    return (loss_total, grad_x[None], *leaves(0), *leaves(1), *leaves(2), *leaves(3))
```

```python
import math

import jax
import jax.numpy as jnp
import numpy as np
from jax import lax
from jax.experimental import pallas as pl
from jax.experimental.pallas import tpu as pltpu

F32, BF = jnp.float32, jnp.bfloat16
MESH = pl.DeviceIdType.MESH
AXES = ("x", "y", "c")
N_DEV = 8

D_MODEL = 2048
D_RNN = 1024
RNN_BLOCKS = 8
RNN_BLOCK = 128
CONV_W = 4
LRU_C = 8.0
SWA_HEADS = 16
SWA_KV_HEADS = 2
SWA_HD = 64
WINDOW = 128
MEM_HEADS = 4
MEM_HD = 256
D_MEM = 1024
REL_BUCKETS = 32
REL_MAX_DIST = 128
EPS = 1e-6
NEG_INF = -1e30
D_IN = 12544
SEGMENTS = (("xr", 0, 1024, F32), ("g_rg", 1024, 1024, F32), ("q_s", 2048, 1024, BF), ("kv", 3072, 256, BF),
            ("g_swa", 3328, 1024, F32), ("q_m", 4352, 1024, BF), ("g_mem", 5376, 1024, F32), ("gl", 6400, 6144, F32))
SEG_TILE = 256

ADAM_LR, ADAM_B1, ADAM_B2, ADAM_EPS, ADAM_WD, ADAM_STEP = 0.001, 0.9, 0.999, 1e-08, 0.01, 10

NN = (((1,), (0,)), ((), ()))
NT = (((1,), (1,)), ((), ()))
TN = (((0,), (0,)), ((), ()))
MIB = 2 ** 20


def _dot(a, b, dn):
    return lax.dot_general(a, b, dn, preferred_element_type=F32)


def _params(sem, vmem_mib=48):
    return pltpu.CompilerParams(dimension_semantics=sem, vmem_limit_bytes=vmem_mib * MIB)


def _sigmoid(z):
    return 1.0 / (1.0 + jnp.exp(-z))


def _softplus(z):
    return jnp.maximum(z, 0.0) + jnp.log(1.0 + jnp.exp(-jnp.abs(z)))


def _expm1(z):
    p = z * (1.0 + z * (0.5 + z * (1.0 / 6 + z * (1.0 / 24 + z * (1.0 / 120 + z * (1.0 / 720 + z * (1.0 / 5040 + z / 40320)))))))
    return jnp.where(jnp.abs(z) < 0.3, p, jnp.exp(z) - 1.0)


def _flat(p):
    return 4 * p[0] + 2 * p[1] + p[2]


def _all_gather(arrs, name):
    n = len(arrs)

    def body(*refs):
        ins, outs = refs[:n], refs[n:2 * n]
        send_sems, recv_sems, local_sems = refs[2 * n:]
        x, y, c = lax.axis_index("x"), lax.axis_index("y"), lax.axis_index("c")
        me, sibling = (x, y, c), (x, y, 1 - c)
        chips = [(1 - x, y), (x, 1 - y), (1 - x, 1 - y)]

        def copy(a, k, block, to, src=None):
            dst = outs[a].at[_flat(block)]
            return pltpu.make_async_remote_copy(src_ref=dst if src is None else src, dst_ref=dst,
                                                send_sem=send_sems.at[a * 7 + k], recv_sem=recv_sems.at[a * 7 + k],
                                                device_id=to, device_id_type=MESH)

        mine = [pltpu.make_async_copy(ins[a], outs[a].at[_flat(me)], local_sems.at[a]) for a in range(n)]
        for cp in mine:
            cp.start()
        first = []
        for a in range(n):
            first += [copy(a, 1 + j, me, (*chip, c), src=ins[a]) for j, chip in enumerate(chips)]
            first.append(copy(a, 0, me, sibling, src=ins[a]))
        for cp in first:
            cp.start()
        passed = []
        for j, chip in enumerate(chips):
            for a in range(n):
                copy(a, 1 + j, (*chip, c), me).wait_recv()
                fw = copy(a, 4 + j, (*chip, c), sibling)
                fw.start()
                passed.append(fw)
        for a in range(n):
            copy(a, 0, sibling, me).wait_recv()
            for j, chip in enumerate(chips):
                copy(a, 4 + j, (*chip, 1 - c), me).wait_recv()
        for cp in first + passed:
            cp.wait_send()
        for cp in mine:
            cp.wait()

    any_spec = pl.BlockSpec(memory_space=pl.ANY)
    return pl.pallas_call(
        body, name=name,
        out_shape=[jax.ShapeDtypeStruct((N_DEV,) + a.shape, a.dtype) for a in arrs],
        in_specs=[any_spec] * n, out_specs=[any_spec] * n,
        scratch_shapes=[pltpu.SemaphoreType.DMA((7 * n,)), pltpu.SemaphoreType.DMA((7 * n,)), pltpu.SemaphoreType.DMA((n,))],
    )(*arrs)


def _all_gather_relayed(arrs, relay, name):
    n = len(arrs)
    K = 9

    def body(*refs):
        ins, outs = refs[:n], refs[n:2 * n]
        send_sems, recv_sems, local_sems = refs[2 * n:]
        x, y, c = lax.axis_index("x"), lax.axis_index("y"), lax.axis_index("c")
        me, sib = (x, y, c), (x, y, 1 - c)
        xn, yn, dg = (1 - x, y, c), (x, 1 - y, c), (1 - x, 1 - y, c)

        def other(p):
            return (p[0], p[1], 1 - p[2])

        def rows(a, half):
            h = arrs[a].shape[0] // 2
            return pl.ds(half * h, h)

        def copy(a, k, block, to, half=None, src=None):
            dst = outs[a].at[_flat(block)]
            if half is not None:
                dst = dst.at[rows(a, half)]
            return pltpu.make_async_remote_copy(src_ref=dst if src is None else src, dst_ref=dst,
                                                send_sem=send_sems.at[a * K + k], recv_sem=recv_sems.at[a * K + k],
                                                device_id=to, device_id_type=MESH)

        mine = [pltpu.make_async_copy(ins[a], outs[a].at[_flat(me)], local_sems.at[a]) for a in range(n)]
        for cp in mine:
            cp.start()
        sends = []

        def start(cp):
            cp.start()
            sends.append(cp)

        for a in range(n):
            start(copy(a, 1, me, xn, src=ins[a]))
            start(copy(a, 2, me, yn, src=ins[a]))
            if not relay[a]:
                start(copy(a, 3, me, dg, src=ins[a]))
            start(copy(a, 0, me, sib, src=ins[a]))
        for a in range(n):
            copy(a, 1, xn, me).wait_recv()
            if relay[a]:
                start(copy(a, 3, xn, yn, half=0))
            start(copy(a, 5, xn, sib))
        for a in range(n):
            copy(a, 2, yn, me).wait_recv()
            if relay[a]:
                start(copy(a, 4, yn, xn, half=1))
            start(copy(a, 6, yn, sib))
        for a in range(n):
            if relay[a]:
                copy(a, 3, dg, me, half=0).wait_recv()
                start(copy(a, 7, dg, sib, half=0))
                copy(a, 4, dg, me, half=1).wait_recv()
                start(copy(a, 8, dg, sib, half=1))
            else:
                copy(a, 3, dg, me).wait_recv()
                start(copy(a, 7, dg, sib))
        for a in range(n):
            copy(a, 0, sib, me).wait_recv()
            copy(a, 5, other(xn), me).wait_recv()
            copy(a, 6, other(yn), me).wait_recv()
            if relay[a]:
                copy(a, 7, other(dg), me, half=0).wait_recv()
                copy(a, 8, other(dg), me, half=1).wait_recv()
            else:
                copy(a, 7, other(dg), me).wait_recv()
        for cp in sends:
            cp.wait_send()
        for cp in mine:
            cp.wait()

    any_spec = pl.BlockSpec(memory_space=pl.ANY)
    return pl.pallas_call(
        body, name=name,
        out_shape=[jax.ShapeDtypeStruct((N_DEV,) + a.shape, a.dtype) for a in arrs],
        in_specs=[any_spec] * n, out_specs=[any_spec] * n,
        scratch_shapes=[pltpu.SemaphoreType.DMA((K * n,)), pltpu.SemaphoreType.DMA((K * n,)), pltpu.SemaphoreType.DMA((n,))],
    )(*arrs)


def _chip_peers(x, y):
    return [(1 - x, y), (x, 1 - y), (1 - x, 1 - y)]


def _chip(p):
    return 2 * p[0] + p[1]


def _plan_gather(kinds):
    def plan(x, y, c):
        out = []
        for a, kind in enumerate(kinds):
            for peer in [(x, y, 1 - c)] + [(*ch, c) for ch in _chip_peers(x, y)]:
                out.append((a, None, (kind, _flat((x, y, c))), peer, (kind, _flat(peer))))
        return out
    return plan


def _plan_swap(n):
    def plan(x, y, c):
        return [(a, 1 - c, ("all", 0), (x, y, 1 - c), ("all", 0)) for a in range(n)]
    return plan


def _slot(ref, where):
    kind, k = where
    if kind == "all":
        return ref
    if kind == "lead":
        return ref.at[k]
    return ref.at[:, pl.ds(pl.multiple_of(k * 256, 256), 256)]


def _plan_scatter(n):
    def plan(x, y, c):
        out = []
        for a in range(n):
            for ch in _chip_peers(x, y):
                out.append((a, _chip(ch), ("lead", _chip((x, y))), (*ch, c), ("lead", _chip(ch))))
        return out
    return plan


HBM_SPEC = pl.BlockSpec(memory_space=pltpu.HBM)
SEM_SPEC = pl.BlockSpec(memory_space=pltpu.SEMAPHORE)


def _in_hbm(a):
    return pltpu.with_memory_space_constraint(a, pltpu.HBM)


def _exchange_start(srcs, lands, plan, name):
    n = len(srcs)
    count = len(plan(0, 0, 0))

    def body(*refs):
        src_refs, land_refs = refs[:n], refs[n:2 * n]
        send_sems, recv_sems = refs[2 * n], refs[2 * n + 1]
        token = refs[-1]
        x, y, c = lax.axis_index("x"), lax.axis_index("y"), lax.axis_index("c")
        for k, (a, si, di, peer, _) in enumerate(plan(x, y, c)):
            src = src_refs[a] if si is None else src_refs[a].at[si]
            pltpu.make_async_remote_copy(src_ref=src, dst_ref=_slot(land_refs[a], di), send_sem=send_sems.at[k],
                                         recv_sem=recv_sems.at[k], device_id=peer, device_id_type=MESH).start()
        token[...] = jnp.zeros_like(token)

    out = pl.pallas_call(
        body, name=name,
        out_shape=(pltpu.SemaphoreType.DMA((count,)), pltpu.SemaphoreType.DMA((count,)),
                   *[pltpu.HBM(a.shape, a.dtype) for a in lands], jax.ShapeDtypeStruct((8, 128), F32)),
        in_specs=[HBM_SPEC] * (2 * n),
        out_specs=(SEM_SPEC, SEM_SPEC, *([HBM_SPEC] * n), pl.BlockSpec(memory_space=pltpu.VMEM)),
        input_output_aliases={n + i: 2 + i for i in range(n)},
        compiler_params=pltpu.CompilerParams(has_side_effects=pltpu.SideEffectType.DATAFLOW_SIDE_EFFECTING),
    )(*[_in_hbm(a) for a in srcs], *[_in_hbm(a) for a in lands])
    return out[0], out[1], list(srcs), list(out[2:2 + n]), out[-1]


def _exchange_wait(send_sems, recv_sems, srcs, lands, plan, after, name):
    n = len(srcs)

    def body(*refs):
        src_refs, land_refs = refs[:n], refs[n:2 * n]
        send_sems, recv_sems = refs[2 * n], refs[2 * n + 1]
        x, y, c = lax.axis_index("x"), lax.axis_index("y"), lax.axis_index("c")
        for k, (a, si, _, peer, ri) in enumerate(plan(x, y, c)):
            src = src_refs[a] if si is None else src_refs[a].at[si]
            cp = pltpu.make_async_remote_copy(src_ref=src, dst_ref=_slot(land_refs[a], ri), send_sem=send_sems.at[k],
                                              recv_sem=recv_sems.at[k], device_id=peer, device_id_type=MESH)
            cp.wait_send()
            cp.wait_recv()

    out = pl.pallas_call(
        body, name=name,
        out_shape=tuple(pltpu.HBM(a.shape, a.dtype) for a in lands),
        in_specs=[HBM_SPEC] * (2 * n) + [SEM_SPEC, SEM_SPEC, pl.BlockSpec(memory_space=pl.ANY)],
        out_specs=tuple([HBM_SPEC] * n),
        input_output_aliases={n + i: i for i in range(n)},
        compiler_params=pltpu.CompilerParams(has_side_effects=pltpu.SideEffectType.DATAFLOW_SIDE_EFFECTING),
    )(*[_in_hbm(a) for a in srcs], *lands, send_sems, recv_sems, after)
    return list(out)


def _forward_to_sibling(lands, kinds, name):
    n = len(lands)

    def body(*refs):
        in_refs, out_refs = refs[:n], refs[n:2 * n]
        send_sems, recv_sems = refs[2 * n:]
        x, y, c = lax.axis_index("x"), lax.axis_index("y"), lax.axis_index("c")
        sibling = (x, y, 1 - c)

        def copy(a, j, slot):
            return pltpu.make_async_remote_copy(src_ref=_slot(in_refs[a], (kinds[a], slot)), dst_ref=_slot(out_refs[a], (kinds[a], slot)),
                                                send_sem=send_sems.at[a * 3 + j], recv_sem=recv_sems.at[a * 3 + j],
                                                device_id=sibling, device_id_type=MESH)

        sends = [copy(a, j, _flat((*ch, c))) for a in range(n) for j, ch in enumerate(_chip_peers(x, y))]
        for cp in sends:
            cp.start()
        for a in range(n):
            for j, ch in enumerate(_chip_peers(x, y)):
                copy(a, j, _flat((*ch, 1 - c))).wait_recv()
        for cp in sends:
            cp.wait_send()

    any_spec = pl.BlockSpec(memory_space=pl.ANY)
    return pl.pallas_call(
        body, name=name, out_shape=[jax.ShapeDtypeStruct(a.shape, a.dtype) for a in lands],
        in_specs=[any_spec] * n, out_specs=[any_spec] * n, input_output_aliases={a: a for a in range(n)},
        scratch_shapes=[pltpu.SemaphoreType.DMA((3 * n,)), pltpu.SemaphoreType.DMA((3 * n,))],
    )(*lands)


def _swap_with_sibling(parts, name):
    n = len(parts)

    def body(*refs):
        in_refs, out_refs = refs[:n], refs[n:2 * n]
        send_sems, recv_sems = refs[2 * n:]
        x, y, c = lax.axis_index("x"), lax.axis_index("y"), lax.axis_index("c")
        sends = [pltpu.make_async_remote_copy(src_ref=in_refs[a].at[1 - c], dst_ref=out_refs[a], send_sem=send_sems.at[a],
                                              recv_sem=recv_sems.at[a], device_id=(x, y, 1 - c), device_id_type=MESH)
                 for a in range(n)]
        for cp in sends:
            cp.start()
        for cp in sends:
            cp.wait()

    any_spec = pl.BlockSpec(memory_space=pl.ANY)
    return pl.pallas_call(
        body, name=name, out_shape=[jax.ShapeDtypeStruct(a.shape[1:], a.dtype) for a in parts],
        in_specs=[any_spec] * n, out_specs=[any_spec] * n,
        scratch_shapes=[pltpu.SemaphoreType.DMA((n,)), pltpu.SemaphoreType.DMA((n,))],
    )(*parts)


def _pair_sum(parts, got, core, name):
    _, _, R, C = parts.shape
    tr = 256 if R % 256 == 0 else R

    def body(c_ref, p_ref, g_ref, o_ref):
        o_ref[...] = (p_ref[...].astype(F32) + g_ref[...].astype(F32)).astype(o_ref.dtype)

    return pl.pallas_call(
        body, name=name,
        grid_spec=pltpu.PrefetchScalarGridSpec(
            num_scalar_prefetch=1, grid=(4, R // tr),
            in_specs=[pl.BlockSpec((None, None, tr, C), lambda j, i, c_ref: (c_ref[0], j, i, 0)),
                      pl.BlockSpec((None, tr, C), lambda j, i, c_ref: (j, i, 0))],
            out_specs=pl.BlockSpec((None, tr, C), lambda j, i, c_ref: (j, i, 0))),
        out_shape=jax.ShapeDtypeStruct((4, R, C), parts.dtype),
        compiler_params=_params(("parallel", "parallel")),
    )(core, parts, got)


def _matmul(a, b, mode, M, N, K, tm, tn, tk, out_dtype, name, b_noff=0, a_moff=0, a_koff=0, b_blocked=False, out_blocked=None,
            dep=None, addend=None, vmem_mib=48):
    nm, nn, nk = M // tm, N // tn, K // tk
    if mode == "nn":
        a_spec = pl.BlockSpec((tm, tk), lambda j, i, k: (i, k + a_koff))
        b_spec = pl.BlockSpec((tk, tn), lambda j, i, k: (k, j + b_noff))
        dn = NN
    elif mode == "nt":
        a_spec = pl.BlockSpec((tm, tk), lambda j, i, k: (i, k + a_koff))
        if b_blocked:
            b_spec = pl.BlockSpec((None, tn, tk), lambda j, i, k: (k, j, 0))
        else:
            b_spec = pl.BlockSpec((tn, tk), lambda j, i, k: (j + b_noff, k))
        dn = NT
    else:
        a_spec = pl.BlockSpec((tk, tm), lambda j, i, k: (k, i + a_moff))
        b_spec = pl.BlockSpec((tk, tn), lambda j, i, k: (k, j + b_noff))
        dn = TN
    if out_blocked == "col":
        out_shape = jax.ShapeDtypeStruct((2, 4, M, tn), out_dtype)
        out_spec = pl.BlockSpec((None, None, tm, tn), lambda j, i, k: (j % 2, j // 2, i, 0))
    elif out_blocked == "row":
        out_shape = jax.ShapeDtypeStruct((2, 4, tm, N), out_dtype)
        out_spec = pl.BlockSpec((None, None, tm, tn), lambda j, i, k: (i % 2, i // 2, 0, j))
    elif out_blocked == "third":
        out_shape = jax.ShapeDtypeStruct((3, M, N // 3), out_dtype)
        out_spec = pl.BlockSpec((None, tm, tn), lambda j, i, k: (j // (nn // 3), i, j % (nn // 3)))
    else:
        out_shape = jax.ShapeDtypeStruct((M, N), out_dtype)
        out_spec = pl.BlockSpec((tm, tn), lambda j, i, k: (i, j))

    n_extra = (addend is not None) + (dep is not None)

    def body(a_ref, b_ref, *rest):
        o_ref, scratch = rest[n_extra], rest[n_extra + 1:]
        if nk == 1:
            prod = _dot(a_ref[...], b_ref[...], dn)
            if addend is not None:
                prod = prod + rest[0][...].astype(F32)
            o_ref[...] = prod.astype(out_dtype)
        else:
            assert addend is None
            acc_ref, = scratch
            k = pl.program_id(2)

            @pl.when(k == 0)
            def _():
                acc_ref[...] = jnp.zeros_like(acc_ref)

            acc_ref[...] += _dot(a_ref[...], b_ref[...], dn)

            @pl.when(k == nk - 1)
            def _():
                o_ref[...] = acc_ref[...].astype(out_dtype)

    return pl.pallas_call(
        body, name=name, grid=(nn, nm, nk),
        in_specs=[a_spec, b_spec] + ([] if addend is None else [out_spec])
        + ([] if dep is None else [pl.BlockSpec((8, 128), lambda j, i, k: (0, 0))]),
        out_specs=out_spec, out_shape=out_shape,
        scratch_shapes=[] if nk == 1 else [pltpu.VMEM((tm, tn), F32)],
        compiler_params=_params(("parallel", "parallel", "arbitrary"), vmem_mib),
    )(a, b, *([] if addend is None else [addend]), *([] if dep is None else [dep]))


def _rms_fwd(x, g, name):
    R, Dm = x.shape
    tr = min(R, 256)

    def body(x_ref, g_ref, h_ref):
        xv = x_ref[...]
        r = lax.rsqrt(jnp.mean(xv * xv, axis=-1, keepdims=True) + EPS)
        h_ref[...] = (xv * r * g_ref[...]).astype(BF)

    return pl.pallas_call(
        body, name=name, grid=(R // tr,),
        in_specs=[pl.BlockSpec((tr, Dm), lambda i: (i, 0)), pl.BlockSpec((1, Dm), lambda i: (0, 0))],
        out_specs=pl.BlockSpec((tr, Dm), lambda i: (i, 0)), out_shape=jax.ShapeDtypeStruct((R, Dm), BF),
        compiler_params=_params(("parallel",)),
    )(x, g)


def _rms_gain_grad(dn, x, name):
    R, Dm = x.shape

    def body(dn_ref, x_ref, o_ref):
        xv = x_ref[...]
        r = lax.rsqrt(jnp.mean(xv * xv, axis=-1, keepdims=True) + EPS)
        o_ref[...] = jnp.sum(dn_ref[...] * xv * r, axis=0, keepdims=True)

    return pl.pallas_call(
        body, name=name, out_shape=jax.ShapeDtypeStruct((1, Dm), F32),
        compiler_params=pltpu.CompilerParams(vmem_limit_bytes=32 * MIB),
    )(dn, x)


def _shift_down(v, k, head8, row, T):
    if k == 0:
        return v
    r = pltpu.roll(v, k, 0)
    hr = pltpu.roll(head8, k, 0)
    top = jnp.where(row[:8] < k, hr, r[:8])
    return jnp.concatenate([top, r[8:]], axis=0)


def _shift_up(v, k, tail8, row, T):
    if k == 0:
        return v
    r = pltpu.roll(v, T - k, 0)
    tr = pltpu.roll(tail8, 8 - k, 0)
    bot = jnp.where(row[:8] >= 8 - k, tr, r[T - 8:])
    return jnp.concatenate([r[:T - 8], bot], axis=0)


def _rglru_gates(u, head8, grow, row, T, cw_ref, cb_ref, wa_ref, ba_ref, wx_ref, bx_ref, lam_ref):
    us = [_shift_down(u, k, head8, row, T) for k in range(CONV_W)]
    acc = us[0] * cw_ref[0:1, :]
    for k in range(1, CONV_W):
        acc = acc + us[k] * cw_ref[k:k + 1, :]
    conv = cb_ref[...] + acc
    cbf = conv.astype(BF)
    r_ = _sigmoid(_dot(cbf, wa_ref[0], NN) + ba_ref[...])
    i_ = _sigmoid(_dot(cbf, wx_ref[0], NN) + bx_ref[...])
    sp = _softplus(-lam_ref[...])
    la = -LRU_C * r_ * sp
    a = jnp.exp(la)
    mult_raw = jnp.sqrt(-_expm1(2.0 * la))
    mult = jnp.where(grow == 0, 1.0, mult_raw)
    return us, conv, cbf, r_, i_, sp, a, mult_raw, mult


def _rglru_specs(T, nt, rev):
    tmap = (lambda n, t: (nt - 1 - t, n)) if rev else (lambda n, t: (t, n))
    hmap = ((lambda n, t: (jnp.maximum((nt - 1 - t) * (T // 8) - 1, 0), n)) if rev
            else (lambda n, t: (jnp.maximum(t * (T // 8) - 1, 0), n)))
    tile = pl.BlockSpec((T, RNN_BLOCK), tmap)
    halo = pl.BlockSpec((8, RNN_BLOCK), hmap)
    vec = pl.BlockSpec((1, RNN_BLOCK), lambda n, t: (0, n))
    cw = pl.BlockSpec((CONV_W, RNN_BLOCK), lambda n, t: (0, n))
    wblk = pl.BlockSpec((1, RNN_BLOCK, RNN_BLOCK), lambda n, t: (n, 0, 0))
    return tile, halo, vec, cw, wblk


def _rglru_fwd(xr, g, cw, cb, wa, ba, wx, bx, lam, T):
    S = xr.shape[0]
    nt = S // T

    def body(u_ref, uh_ref, g_ref, cw_ref, cb_ref, wa_ref, ba_ref, wx_ref, bx_ref, lam_ref, h_ref, y_ref, carry):
        t = pl.program_id(1)

        @pl.when(t == 0)
        def _():
            carry[...] = jnp.zeros_like(carry)

        row = lax.broadcasted_iota(jnp.int32, (T, RNN_BLOCK), 0)
        grow = row + t * T
        head8 = jnp.where(t > 0, uh_ref[...], 0.0)
        _, conv, _, _, i_, _, a, _, mult = _rglru_gates(u_ref[...], head8, grow, row, T, cw_ref, cb_ref, wa_ref, ba_ref,
                                                         wx_ref, bx_ref, lam_ref)
        b = mult * i_ * conv
        s = 1
        while s < T:
            keep = row >= s
            a_s = jnp.where(keep, pltpu.roll(a, s, 0), 1.0)
            b_s = jnp.where(keep, pltpu.roll(b, s, 0), 0.0)
            b = a * b_s + b
            a = a * a_s
            s *= 2
        h = b + a * carry[0:1, :]
        carry[...] = jnp.broadcast_to(h[T - 1:T, :], carry.shape)
        h_ref[...] = h
        gv = g_ref[...]
        y_ref[...] = (h * (gv * _sigmoid(gv))).astype(BF)

    tile, halo, vec, cwspec, wblk = _rglru_specs(T, nt, False)
    return pl.pallas_call(
        body, name="rglru_fwd", grid=(RNN_BLOCKS, nt),
        in_specs=[tile, halo, tile, cwspec, vec, wblk, vec, wblk, vec, vec],
        out_specs=[tile, tile],
        out_shape=[jax.ShapeDtypeStruct((S, D_RNN), F32), jax.ShapeDtypeStruct((S, D_RNN), BF)],
        scratch_shapes=[pltpu.VMEM((8, RNN_BLOCK), F32)],
        compiler_params=_params(("parallel", "arbitrary")),
    )(xr, xr, g, cw, cb, wa, ba, wx, bx, lam)


def _rglru_bwd(xr, g, h, dy, cw, cb, wa, ba, wx, bx, lam, T):
    S = xr.shape[0]
    nt = S // T

    def body(u_ref, uh_ref, g_ref, h_ref, hh_ref, dy_ref, cw_ref, cb_ref, wa_ref, ba_ref, wx_ref, bx_ref, lam_ref,
             du_ref, dg_ref, dwa_ref, dwx_ref, dvec_ref, c_dhh, c_a, c_dconv):
        t = pl.program_id(1)
        tt = nt - 1 - t

        @pl.when(t == 0)
        def _():
            c_dhh[...] = jnp.zeros_like(c_dhh)
            c_a[...] = jnp.zeros_like(c_a)
            c_dconv[...] = jnp.zeros_like(c_dconv)
            dwa_ref[...] = jnp.zeros_like(dwa_ref)
            dwx_ref[...] = jnp.zeros_like(dwx_ref)
            dvec_ref[...] = jnp.zeros_like(dvec_ref)

        row = lax.broadcasted_iota(jnp.int32, (T, RNN_BLOCK), 0)
        row8 = row[:8]
        grow = row + tt * T
        head8 = jnp.where(tt > 0, uh_ref[...], 0.0)
        us, conv, cbf, r_, i_, sp, a, mult_raw, mult = _rglru_gates(
            u_ref[...], head8, grow, row, T, cw_ref, cb_ref, wa_ref, ba_ref, wx_ref, bx_ref, lam_ref)
        hv = h_ref[...]
        hprev = _shift_down(hv, 1, jnp.where(tt > 0, hh_ref[...], 0.0), row, T)
        gv = g_ref[...]
        sg = _sigmoid(gv)
        dyv = dy_ref[...]
        dg_ref[...] = (dyv * hv * (sg * (1.0 + gv * (1.0 - sg)))).astype(BF)
        d = dyv * (gv * sg)
        A = _shift_up(a, 1, c_a[...], row, T)
        s = 1
        while s < T:
            keep = row < T - s
            A_s = jnp.where(keep, pltpu.roll(A, T - s, 0), 1.0)
            d_s = jnp.where(keep, pltpu.roll(d, T - s, 0), 0.0)
            d = A * d_s + d
            A = A * A_s
            s *= 2
        dhh = d + A * c_dhh[0:1, :]
        da = dhh * hprev
        dconv = dhh * mult * i_
        di = dhh * mult * conv
        dmult = dhh * i_ * conv
        dla = da * a - jnp.where(grow == 0, 0.0, dmult * (a * a) / mult_raw)
        dr = dla * (-LRU_C * sp)
        dsp = jnp.sum(dla * (-LRU_C * r_), axis=0, keepdims=True)
        dza = dr * r_ * (1.0 - r_)
        dzx = di * i_ * (1.0 - i_)
        dza_b, dzx_b = dza.astype(BF), dzx.astype(BF)
        dconv = dconv + _dot(dza_b, wa_ref[0], NT) + _dot(dzx_b, wx_ref[0], NT)
        dwa_ref[0] += _dot(cbf, dza_b, TN)
        dwx_ref[0] += _dot(cbf, dzx_b, TN)
        lam = lam_ref[...]
        rows = [jnp.sum(dconv * us[k], axis=0, keepdims=True) for k in range(CONV_W)]
        rows += [jnp.sum(dconv, axis=0, keepdims=True), jnp.sum(dza, axis=0, keepdims=True),
                 jnp.sum(dzx, axis=0, keepdims=True), dsp * (-_sigmoid(-lam))]
        upd = jnp.zeros((8, RNN_BLOCK), F32)
        for j, rv in enumerate(rows):
            upd = upd + jnp.where(row8 == j, rv, 0.0)
        dvec_ref[...] += upd
        tail8 = c_dconv[...]
        du = dconv * cw_ref[0:1, :]
        for k in range(1, CONV_W):
            du = du + _shift_up(dconv, k, tail8, row, T) * cw_ref[k:k + 1, :]
        du_ref[...] = du.astype(BF)
        c_dhh[...] = jnp.broadcast_to(dhh[0:1, :], c_dhh.shape)
        c_a[...] = jnp.broadcast_to(a[0:1, :], c_a.shape)
        c_dconv[...] = dconv[:8]

    tile, halo, vec, cwspec, wblk = _rglru_specs(T, nt, True)
    acc8 = pl.BlockSpec((8, RNN_BLOCK), lambda n, t: (0, n))
    return pl.pallas_call(
        body, name="rglru_bwd", grid=(RNN_BLOCKS, nt),
        in_specs=[tile, halo, tile, tile, halo, tile, cwspec, vec, wblk, vec, wblk, vec, vec],
        out_specs=[tile, tile, wblk, wblk, acc8],
        out_shape=[jax.ShapeDtypeStruct((S, D_RNN), BF), jax.ShapeDtypeStruct((S, D_RNN), BF),
                   jax.ShapeDtypeStruct((RNN_BLOCKS, RNN_BLOCK, RNN_BLOCK), F32),
                   jax.ShapeDtypeStruct((RNN_BLOCKS, RNN_BLOCK, RNN_BLOCK), F32),
                   jax.ShapeDtypeStruct((8, D_RNN), F32)],
        scratch_shapes=[pltpu.VMEM((8, RNN_BLOCK), F32)] * 3,
        compiler_params=_params(("parallel", "arbitrary")),
    )(xr, xr, g, h, h, dy, cw, cb, wa, ba, wx, bx, lam)


def _rel_bucket_map():
    qi = np.arange(WINDOW)[:, None]
    kj = np.arange(2 * WINDOW)[None, :]
    dist = jnp.asarray(qi + WINDOW - kj, jnp.int32)
    n = jnp.maximum(dist, 0)
    max_exact = REL_BUCKETS // 2
    ratio = jnp.log(jnp.maximum(n, 1).astype(F32) / max_exact) / math.log(REL_MAX_DIST / max_exact)
    large = jnp.minimum(max_exact + (ratio * (REL_BUCKETS - max_exact)).astype(jnp.int32), REL_BUCKETS - 1)
    bucket = jnp.where(n < max_exact, n, large).astype(jnp.int32)
    j = np.arange(WINDOW)[None, :]
    return jnp.where(jnp.asarray(j > qi), bucket[:, :WINDOW], bucket[:, WINDOW:])


def _swa_common(n, kv_ref, bucket_ref, relb_ref, bias_scr):
    @pl.when(n == 0)
    def _():
        bk = bucket_ref[...]
        for h in range(SWA_HEADS):
            acc = jnp.zeros((WINDOW, WINDOW), F32)
            for b in range(REL_BUCKETS):
                acc = acc + jnp.where(bk == b, relb_ref[b, h], 0.0)
            bias_scr[h] = acc

    prev0 = pl.multiple_of(jnp.maximum(n - 1, 0) * WINDOW, WINDOW)
    cur0 = pl.multiple_of(n * WINDOW, WINDOW)
    kk = jnp.concatenate([kv_ref[pl.ds(prev0, WINDOW), :], kv_ref[pl.ds(cur0, WINDOW), :]], axis=0).astype(F32)
    rowi = lax.broadcasted_iota(jnp.int32, (WINDOW, WINDOW), 0)
    col = lax.broadcasted_iota(jnp.int32, (WINDOW, WINDOW), 1)
    from_prev = col > rowi
    return kk, from_prev, prev0, cur0


def _fold(full, from_prev):
    return jnp.where(from_prev, full[:, :WINDOW], full[:, WINDOW:])


def _unfold(sq, from_prev):
    return jnp.concatenate([jnp.where(from_prev, sq, 0.0), jnp.where(from_prev, 0.0, sq)], axis=1)


def _half_pair(part, kvh):
    lo = lax.broadcasted_iota(jnp.int32, part.shape, 1) < SWA_HD
    if kvh == 0:
        pa = jnp.where(lo, part, 0.0)
        pb = pltpu.roll(pa, SWA_HD, 1)
    else:
        pb = jnp.where(lo, 0.0, part)
        pa = pltpu.roll(pb, SWA_HD, 1)
    return pa.astype(BF), pb.astype(BF)


ALL_HEADS = SWA_HEADS * WINDOW


def _sink_column(sinks):
    return jnp.repeat(sinks.reshape(SWA_HEADS), WINDOW).reshape(ALL_HEADS, 1)


def _swa_operands(kk):
    return [(_half_pair(kk[:, :128], kvh), _half_pair(kk[:, 128:], kvh)) for kvh in range(SWA_KV_HEADS)]


def _swa_probs(n, q_ref, ops, bias_scr, sinkc_ref, from_prev):
    lgs = []
    for kvh in range(SWA_KV_HEADS):
        (ka, kb), _ = ops[kvh]
        for p in range(4):
            q2 = q_ref[:, kvh * 512 + p * 128:kvh * 512 + p * 128 + 128]
            lgs += [_fold(_dot(q2, ka, NT), from_prev), _fold(_dot(q2, kb, NT), from_prev)]
    lg = jnp.concatenate(lgs, axis=0) * (SWA_HD ** -0.5) + bias_scr[...].reshape(ALL_HEADS, WINDOW)
    rowi = jnp.bitwise_and(lax.broadcasted_iota(jnp.int32, (ALL_HEADS, WINDOW), 0), WINDOW - 1)
    col = lax.broadcasted_iota(jnp.int32, (ALL_HEADS, WINDOW), 1)
    no_prev = jnp.where(n > 0, 0, 4 * WINDOW)
    lg = jnp.where(jnp.logical_or(col <= rowi, col > rowi + no_prev), lg, NEG_INF)
    sink = sinkc_ref[...]
    m = jnp.maximum(jnp.max(lg, axis=-1, keepdims=True), sink)
    e = jnp.exp(lg - m)
    es = jnp.exp(sink - m)
    den = jnp.sum(e, axis=-1, keepdims=True) + es
    return e / den, es / den


def _swa_fwd(q, kv, g, bucket, rel_bias, sink_col):
    S = q.shape[0]
    nb = S // WINDOW

    def body(q_ref, kv_ref, g_ref, bucket_ref, relb_ref, sinkc_ref, o_ref, y_ref, bias_scr):
        n = pl.program_id(0)
        kk, from_prev, _, _ = _swa_common(n, kv_ref, bucket_ref, relb_ref, bias_scr)
        ops = _swa_operands(kk)
        pr, _ = _swa_probs(n, q_ref, ops, bias_scr, sinkc_ref, from_prev)
        for kvh in range(SWA_KV_HEADS):
            _, (va, vb) = ops[kvh]
            for p in range(4):
                c0 = kvh * 512 + p * 128
                r0 = (kvh * 8 + 2 * p) * WINDOW
                o2 = (_dot(_unfold(pr[r0:r0 + WINDOW], from_prev).astype(BF), va, NN)
                      + _dot(_unfold(pr[r0 + WINDOW:r0 + 2 * WINDOW], from_prev).astype(BF), vb, NN))
                o_ref[:, c0:c0 + 128] = o2
                gv = g_ref[:, c0:c0 + 128]
                y_ref[:, c0:c0 + 128] = (o2 * (gv * _sigmoid(gv))).astype(BF)

    blk = pl.BlockSpec((WINDOW, 1024), lambda n: (n, 0))
    smem = pl.BlockSpec(memory_space=pltpu.SMEM)
    sinkc = pl.BlockSpec((ALL_HEADS, 1), lambda n: (0, 0))
    return pl.pallas_call(
        body, name="swa_fwd", grid=(nb,),
        in_specs=[blk, pl.BlockSpec((S, 256), lambda n: (0, 0)), blk, pl.BlockSpec((WINDOW, WINDOW), lambda n: (0, 0)), smem, sinkc],
        out_specs=[blk, blk],
        out_shape=[jax.ShapeDtypeStruct((S, 1024), F32), jax.ShapeDtypeStruct((S, 1024), BF)],
        scratch_shapes=[pltpu.VMEM((SWA_HEADS, WINDOW, WINDOW), F32)],
        compiler_params=_params(("arbitrary",)),
    )(q, kv, g, bucket, rel_bias, sink_col)


def _swa_bwd(q, kv, g, o, dy, bucket, rel_bias, sink_col):
    S = q.shape[0]
    nb = S // WINDOW

    def body(q_ref, kv_ref, g_ref, o_ref, dy_ref, bucket_ref, relb_ref, sinkc_ref,
             dq_ref, dg_ref, dkv_ref, dsink_ref, drel_ref, bias_scr, dbias_scr, dsink_scr):
        n = pl.program_id(0)

        @pl.when(n == 0)
        def _():
            dbias_scr[...] = jnp.zeros_like(dbias_scr)
            dsink_scr[...] = jnp.zeros_like(dsink_scr)
            dkv_ref[...] = jnp.zeros_like(dkv_ref)

        kk, from_prev, prev0, cur0 = _swa_common(n, kv_ref, bucket_ref, relb_ref, bias_scr)
        ops = _swa_operands(kk)
        pr, ps = _swa_probs(n, q_ref, ops, bias_scr, sinkc_ref, from_prev)
        do2s, dps = [], []
        for kvh in range(SWA_KV_HEADS):
            _, (va, vb) = ops[kvh]
            for p in range(4):
                c0 = kvh * 512 + p * 128
                gv = g_ref[:, c0:c0 + 128]
                sg = _sigmoid(gv)
                dyv = dy_ref[:, c0:c0 + 128]
                dg_ref[:, c0:c0 + 128] = (dyv * o_ref[:, c0:c0 + 128] * (sg * (1.0 + gv * (1.0 - sg)))).astype(BF)
                do2 = (dyv * (gv * sg)).astype(BF)
                do2s.append(do2)
                dps += [_fold(_dot(do2, va, NT), from_prev), _fold(_dot(do2, vb, NT), from_prev)]
        dp = jnp.concatenate(dps, axis=0)
        delta = jnp.sum(pr * dp, axis=-1, keepdims=True)
        ds = pr * (dp - delta)
        dbias_scr[...] += ds.reshape(SWA_HEADS, WINDOW, WINDOW)
        dsink_scr[...] += ps * delta
        dsc = ds * (SWA_HD ** -0.5)
        lo256 = lax.broadcasted_iota(jnp.int32, (2 * WINDOW, 128), 1) < SWA_HD
        dks, dvs = [], []
        for kvh in range(SWA_KV_HEADS):
            (ka, kb), _ = ops[kvh]
            dka = jnp.zeros((2 * WINDOW, 128), F32)
            dkb, dva, dvb = dka, dka, dka
            for p in range(4):
                c0 = kvh * 512 + p * 128
                r0 = (kvh * 8 + 2 * p) * WINDOW
                q2 = q_ref[:, c0:c0 + 128]
                do2 = do2s[kvh * 4 + p]
                ds0 = _unfold(dsc[r0:r0 + WINDOW], from_prev).astype(BF)
                ds1 = _unfold(dsc[r0 + WINDOW:r0 + 2 * WINDOW], from_prev).astype(BF)
                dq_ref[:, c0:c0 + 128] = (_dot(ds0, ka, NN) + _dot(ds1, kb, NN)).astype(BF)
                dka = dka + _dot(ds0, q2, TN)
                dkb = dkb + _dot(ds1, q2, TN)
                dva = dva + _dot(_unfold(pr[r0:r0 + WINDOW], from_prev).astype(BF), do2, TN)
                dvb = dvb + _dot(_unfold(pr[r0 + WINDOW:r0 + 2 * WINDOW], from_prev).astype(BF), do2, TN)
            dks.append(jnp.where(lo256, dka, 0.0) + pltpu.roll(jnp.where(lo256, 0.0, dkb), SWA_HD, 1))
            dvs.append(jnp.where(lo256, dva, 0.0) + pltpu.roll(jnp.where(lo256, 0.0, dvb), SWA_HD, 1))
        dk = dks[0] + pltpu.roll(dks[1], SWA_HD, 1)
        dv = dvs[0] + pltpu.roll(dvs[1], SWA_HD, 1)
        dkv_ref[pl.ds(prev0, WINDOW), 0:128] += dk[:WINDOW]
        dkv_ref[pl.ds(prev0, WINDOW), 128:256] += dv[:WINDOW]
        dkv_ref[pl.ds(cur0, WINDOW), 0:128] += dk[WINDOW:]
        dkv_ref[pl.ds(cur0, WINDOW), 128:256] += dv[WINDOW:]

        @pl.when(n == nb - 1)
        def _():
            dsink_ref[...] = -jnp.sum(dsink_scr[...].reshape(SWA_HEADS, WINDOW, 1), axis=1)
            bk = bucket_ref[...]
            sums = []
            for b in range(REL_BUCKETS):
                sums.append(jnp.sum(jnp.where((bk == b)[None], dbias_scr[...], 0.0), axis=1))
            drel_ref[...] = jnp.sum(jnp.concatenate(sums, axis=0), axis=1, keepdims=True)

    blk = pl.BlockSpec((WINDOW, 1024), lambda n: (n, 0))
    smem = pl.BlockSpec(memory_space=pltpu.SMEM)
    whole = lambda shape: pl.BlockSpec(shape, lambda n: (0, 0))
    return pl.pallas_call(
        body, name="swa_bwd", grid=(nb,),
        in_specs=[blk, whole((S, 256)), blk, blk, blk, whole((WINDOW, WINDOW)), smem, whole((ALL_HEADS, 1))],
        out_specs=[blk, blk, whole((S, 256)), whole((SWA_HEADS, 1)), whole((REL_BUCKETS * SWA_HEADS, 1))],
        out_shape=[jax.ShapeDtypeStruct((S, 1024), BF), jax.ShapeDtypeStruct((S, 1024), BF),
                   jax.ShapeDtypeStruct((S, 256), F32), jax.ShapeDtypeStruct((SWA_HEADS, 1), F32),
                   jax.ShapeDtypeStruct((REL_BUCKETS * SWA_HEADS, 1), F32)],
        scratch_shapes=[pltpu.VMEM((SWA_HEADS, WINDOW, WINDOW), F32), pltpu.VMEM((SWA_HEADS, WINDOW, WINDOW), F32),
                        pltpu.VMEM((ALL_HEADS, 1), F32)],
        compiler_params=_params(("arbitrary",)),
    )(q, kv, g, o, dy, bucket, rel_bias, sink_col)


def _mem_probs(qh, mk):
    lg = _dot(qh, mk, NT) * (MEM_HD ** -0.5)
    e = jnp.exp(lg - jnp.max(lg, axis=-1, keepdims=True))
    return e / jnp.sum(e, axis=-1, keepdims=True)


def _mem_fwd(q, mkv, g):
    S = q.shape[0]
    M = mkv.shape[0]
    tq = 256

    def body(q_ref, mkv_ref, g_ref, o_ref, y_ref):
        for h in range(MEM_HEADS):
            c0 = h * MEM_HD
            pr = _mem_probs(q_ref[:, c0:c0 + MEM_HD], mkv_ref[:, c0:c0 + MEM_HD])
            o = _dot(pr.astype(BF), mkv_ref[:, D_MEM + c0:D_MEM + c0 + MEM_HD], NN)
            o_ref[:, c0:c0 + MEM_HD] = o
            gv = g_ref[:, c0:c0 + MEM_HD]
            y_ref[:, c0:c0 + MEM_HD] = (o * (gv * _sigmoid(gv))).astype(BF)

    blk = pl.BlockSpec((tq, D_MEM), lambda i: (i, 0))
    return pl.pallas_call(
        body, name="mem_fwd", grid=(S // tq,),
        in_specs=[blk, pl.BlockSpec((M, 2 * D_MEM), lambda i: (0, 0)), blk], out_specs=[blk, blk],
        out_shape=[jax.ShapeDtypeStruct((S, D_MEM), F32), jax.ShapeDtypeStruct((S, D_MEM), BF)],
        compiler_params=_params(("parallel",)),
    )(q, mkv, g)


def _mem_bwd(q, mkv, g, o, dy):
    S = q.shape[0]
    M = mkv.shape[0]
    tq = 256

    def body(q_ref, mkv_ref, g_ref, o_ref, dy_ref, dq_ref, dg_ref, dmkv_ref):
        @pl.when(pl.program_id(0) == 0)
        def _():
            dmkv_ref[...] = jnp.zeros_like(dmkv_ref)

        for h in range(MEM_HEADS):
            c0 = h * MEM_HD
            qh = q_ref[:, c0:c0 + MEM_HD]
            mk = mkv_ref[:, c0:c0 + MEM_HD]
            mv = mkv_ref[:, D_MEM + c0:D_MEM + c0 + MEM_HD]
            gv = g_ref[:, c0:c0 + MEM_HD]
            sg = _sigmoid(gv)
            dyv = dy_ref[:, c0:c0 + MEM_HD]
            dg_ref[:, c0:c0 + MEM_HD] = (dyv * o_ref[:, c0:c0 + MEM_HD] * (sg * (1.0 + gv * (1.0 - sg)))).astype(BF)
            do = (dyv * (gv * sg)).astype(BF)
            pr = _mem_probs(qh, mk)
            dp = _dot(do, mv, NT)
            ds = pr * (dp - jnp.sum(pr * dp, axis=-1, keepdims=True))
            dsb = (ds * (MEM_HD ** -0.5)).astype(BF)
            dq_ref[:, c0:c0 + MEM_HD] = _dot(dsb, mk, NN).astype(BF)
            dmkv_ref[:, c0:c0 + MEM_HD] += _dot(dsb, qh, TN)
            dmkv_ref[:, D_MEM + c0:D_MEM + c0 + MEM_HD] += _dot(pr.astype(BF), do, TN)

    blk = pl.BlockSpec((tq, D_MEM), lambda i: (i, 0))
    whole = pl.BlockSpec((M, 2 * D_MEM), lambda i: (0, 0))
    return pl.pallas_call(
        body, name="mem_bwd", grid=(S // tq,),
        in_specs=[blk, whole, blk, blk, blk], out_specs=[blk, blk, whole],
        out_shape=[jax.ShapeDtypeStruct((S, D_MEM), BF), jax.ShapeDtypeStruct((S, D_MEM), BF),
                   jax.ShapeDtypeStruct((M, 2 * D_MEM), F32)],
        compiler_params=_params(("arbitrary",)),
    )(q, mkv, g, o, dy)


MERGE_TN = 512


def _merge_specs(tm):
    ytile = pl.BlockSpec((tm, 1024), lambda i, j: (i, 0))
    wblk = pl.BlockSpec((1024, MERGE_TN), lambda i, j: (0, j))
    gls = [pl.BlockSpec((None, tm, MERGE_TN), (lambda i, j, br=br: (br, i, j))) for br in range(3)]
    otile = pl.BlockSpec((tm, MERGE_TN), lambda i, j: (i, j))
    return ytile, wblk, gls, otile


def _merge_fwd(ys, ws, gl, tm):
    S = gl.shape[1]

    def body(y0, y1, y2, w0, w1, w2, g0, g1, g2, o_ref):
        acc = None
        for y_ref, w_ref, g_ref in ((y0, w0, g0), (y1, w1, g1), (y2, w2, g2)):
            term = _sigmoid(g_ref[...]) * _dot(y_ref[...], w_ref[...], NN)
            acc = term if acc is None else acc + term
        o_ref[...] = acc.astype(BF)

    ytile, wblk, gls, otile = _merge_specs(tm)
    return pl.pallas_call(
        body, name="merge_fwd", grid=(S // tm, D_MODEL // MERGE_TN),
        in_specs=[ytile] * 3 + [wblk] * 3 + gls, out_specs=otile,
        out_shape=jax.ShapeDtypeStruct((S, D_MODEL), BF),
        compiler_params=_params(("parallel", "arbitrary")),
    )(*ys, *ws, gl, gl, gl)


def _merge_bwd(dout, w_out, ys, ws, gl, tm):
    S = gl.shape[1]

    def body(do_ref, wo_ref, y0, y1, y2, w0, w1, w2, g0, g1, g2, dg0, dg1, dg2, dp0, dp1, dp2):
        dm = _dot(do_ref[...], wo_ref[...], NT)
        for y_ref, w_ref, g_ref, dg_ref, dp_ref in ((y0, w0, g0, dg0, dp0), (y1, w1, g1, dg1, dp1), (y2, w2, g2, dg2, dp2)):
            gate = _sigmoid(g_ref[...])
            pv = _dot(y_ref[...], w_ref[...], NN)
            dg_ref[...] = (dm * pv * gate * (1.0 - gate)).astype(BF)
            dp_ref[...] = (dm * gate).astype(BF)

    ytile, wblk, gls, otile = _merge_specs(tm)
    out = jax.ShapeDtypeStruct((S, D_MODEL), BF)
    return pl.pallas_call(
        body, name="merge_bwd", grid=(S // tm, D_MODEL // MERGE_TN),
        in_specs=[pl.BlockSpec((tm, D_MODEL), lambda i, j: (i, 0)), pl.BlockSpec((MERGE_TN, D_MODEL), lambda i, j: (j, 0))]
        + [ytile] * 3 + [wblk] * 3 + gls,
        out_specs=[otile] * 6, out_shape=[out] * 6,
        compiler_params=_params(("parallel", "arbitrary")),
    )(dout, w_out, *ys, *ws, gl, gl, gl)


def _out_loss(merged, w_out, x, target, post_g, tm):
    S = x.shape[0]

    def body(m_ref, w_ref, x_ref, t_ref, g_ref, dout_ref, dy_ref, loss_ref, dpost_ref):
        @pl.when(pl.program_id(0) == 0)
        def _():
            loss_ref[...] = jnp.zeros_like(loss_ref)
            dpost_ref[...] = jnp.zeros_like(dpost_ref)

        out = _dot(m_ref[...], w_ref[...], NN)
        r = lax.rsqrt(jnp.mean(out * out, axis=-1, keepdims=True) + EPS)
        nrm = out * r
        gv = g_ref[...]
        err = (x_ref[...] + nrm * gv) - t_ref[...]
        sq = jnp.sum(jnp.sum(err * err, axis=1, keepdims=True), axis=0, keepdims=True)
        loss_ref[...] += sq * (0.5 / D_MODEL)
        dy = err * (1.0 / D_MODEL)
        dy_ref[...] = dy
        dpost_ref[...] += jnp.sum(dy * nrm, axis=0, keepdims=True)
        dn = dy * gv
        dout_ref[...] = (r * (dn - nrm * jnp.mean(dn * nrm, axis=-1, keepdims=True))).astype(BF)

    row = pl.BlockSpec((tm, D_MODEL), lambda i: (i, 0))
    return pl.pallas_call(
        body, name="out_loss", grid=(S // tm,),
        in_specs=[row, pl.BlockSpec((D_MODEL, D_MODEL), lambda i: (0, 0)), row, row, pl.BlockSpec((1, D_MODEL), lambda i: (0, 0))],
        out_specs=[row, row, pl.BlockSpec((8, 128), lambda i: (0, 0)), pl.BlockSpec((1, D_MODEL), lambda i: (0, 0))],
        out_shape=[jax.ShapeDtypeStruct((S, D_MODEL), BF), jax.ShapeDtypeStruct((S, D_MODEL), F32),
                   jax.ShapeDtypeStruct((8, 128), F32), jax.ShapeDtypeStruct((1, D_MODEL), F32)],
        compiler_params=_params(("arbitrary",)),
    )(merged, w_out, x, target, post_g)


def _dh_dx(dproj, w_in, x, dy, pre_g, tm, tk):
    S = x.shape[0]
    nk = D_IN // tk

    def body(dp_ref, w_ref, x_ref, dy_ref, g_ref, dx_ref, dpre_ref, acc_ref):
        i, k = pl.program_id(0), pl.program_id(1)

        @pl.when(jnp.logical_and(i == 0, k == 0))
        def _():
            dpre_ref[...] = jnp.zeros_like(dpre_ref)

        @pl.when(k == 0)
        def _():
            acc_ref[...] = jnp.zeros_like(acc_ref)

        acc_ref[...] += _dot(dp_ref[...], w_ref[...], NN)

        @pl.when(k == nk - 1)
        def _():
            dh = acc_ref[...]
            xv = x_ref[...]
            r = lax.rsqrt(jnp.mean(xv * xv, axis=-1, keepdims=True) + EPS)
            nrm = xv * r
            dpre_ref[...] += jnp.sum(dh * nrm, axis=0, keepdims=True)
            dn = dh * g_ref[...]
            dx_ref[...] = r * (dn - nrm * jnp.mean(dn * nrm, axis=-1, keepdims=True)) + dy_ref[...]

    row = pl.BlockSpec((tm, D_MODEL), lambda i, k: (i, 0))
    vec = pl.BlockSpec((1, D_MODEL), lambda i, k: (0, 0))
    return pl.pallas_call(
        body, name="dh_dx", grid=(S // tm, nk),
        in_specs=[pl.BlockSpec((tm, tk), lambda i, k: (i, k)), pl.BlockSpec((tk, D_MODEL), lambda i, k: (k, 0)), row, row, vec],
        out_specs=[row, vec],
        out_shape=[jax.ShapeDtypeStruct((S, D_MODEL), F32), jax.ShapeDtypeStruct((1, D_MODEL), F32)],
        scratch_shapes=[pltpu.VMEM((tm, D_MODEL), F32)],
        compiler_params=_params(("arbitrary", "arbitrary"), 56),
    )(dproj, w_in, x, dy, pre_g)


def _sum_parts(parts, name):
    P, R, C = parts.shape
    tr = max(t for t in range(8, 513, 8) if R % t == 0)

    def body(p_ref, o_ref):
        acc = p_ref[0]
        for j in range(1, P):
            acc = acc + p_ref[j]
        o_ref[...] = acc

    return pl.pallas_call(
        body, name=name, grid=(R // tr,),
        in_specs=[pl.BlockSpec((P, tr, C), lambda i: (0, i, 0))], out_specs=pl.BlockSpec((tr, C), lambda i: (i, 0)),
        out_shape=jax.ShapeDtypeStruct((R, C), F32), compiler_params=_params(("parallel",)),
    )(parts)


def _adamw(parts, w, m, v, name):
    groups = list(parts) if isinstance(parts, (list, tuple)) else [parts]
    P, rows, C = groups[0].shape
    R = rows * len(groups)
    tr = 128 if rows % 128 == 0 else rows
    per = rows // tr
    c1 = 1.0 - ADAM_B1 ** ADAM_STEP
    c2 = 1.0 - ADAM_B2 ** ADAM_STEP

    def body(*refs):
        p_refs = refs[:len(groups)]
        w_ref, m_ref, v_ref, g_ref, d_ref, nm_ref, nv_ref = refs[len(groups):]
        g = None
        for q, p_ref in enumerate(p_refs):
            gq = p_ref[0].astype(F32)
            for j in range(1, P):
                gq = gq + p_ref[j].astype(F32)
            g = gq if g is None else jnp.where(pl.program_id(0) // per == q, gq, g)
        nm = ADAM_B1 * m_ref[...] + (1.0 - ADAM_B1) * g
        nv = ADAM_B2 * v_ref[...] + (1.0 - ADAM_B2) * (g * g)
        g_ref[...] = g
        nm_ref[...] = nm
        nv_ref[...] = nv
        d_ref[...] = -ADAM_LR * ((nm / c1) / (jnp.sqrt(nv / c2) + ADAM_EPS) + ADAM_WD * w_ref[...])

    tile = pl.BlockSpec((tr, C), lambda i: (i, 0))
    out = jax.ShapeDtypeStruct((R, C), F32)
    return pl.pallas_call(
        body, name=name, grid=(R // tr,),
        in_specs=[pl.BlockSpec((P, tr, C), (lambda i, q=q: (0, jnp.clip(i - q * per, 0, per - 1), 0))) for q in range(len(groups))]
        + [tile, tile, tile], out_specs=[tile] * 4, out_shape=[out] * 4,
        compiler_params=_params(("parallel",)),
    )(*groups, w, m, v)


def _forward_a(x, mem, pre_g, mem_g, w_in, conv_w, conv_b, w_a, b_a, w_x, b_x, lam, sinks, rel_bias):
    S = x.shape[0]
    st = dict(T=min(512, S // 2), tm=min(512, S), bucket=_rel_bucket_map())
    st["h"] = _rms_fwd(x, pre_g, "pre_norm")
    st["memn"] = _rms_fwd(mem, mem_g, "mem_norm")
    seg = {}
    for name, c0, width, dt in SEGMENTS:
        seg[name] = _matmul(st["h"], w_in, "nt", S, width, D_MODEL, S, SEG_TILE, D_MODEL, dt, "proj_" + name, b_noff=c0 // SEG_TILE,
                            out_blocked="third" if name == "gl" else None)
    st["seg"] = seg
    st["h_rg"], st["y_rg"] = _rglru_fwd(seg["xr"], seg["g_rg"], conv_w, conv_b, w_a, b_a, w_x, b_x, lam, st["T"])
    st["o_swa"], st["y_swa"] = _swa_fwd(seg["q_s"], seg["kv"], seg["g_swa"], st["bucket"], rel_bias, _sink_column(sinks))
    return st


def _forward_b(st, x, target, post_g, w_memkv, wbr, w_out):
    S = x.shape[0]
    M = st["memn"].shape[0]
    seg = st["seg"]
    st["mkv"] = _matmul(st["memn"], w_memkv, "nn", M, 2 * D_MEM, D_MODEL, M, 512, D_MODEL, BF, "mem_kv")
    st["o_mem"], st["y_mem"] = _mem_fwd(seg["q_m"], st["mkv"], seg["g_mem"])
    st["ys"] = (st["y_rg"], st["y_swa"], st["y_mem"])
    st["merged"] = _merge_fwd(st["ys"], wbr, seg["gl"], st["tm"])
    st["dout"], st["dy"], st["loss"], st["dpost"] = _out_loss(st["merged"], w_out, x, target, post_g, min(256, S))
    return st


def _backward_a(st, mem, w_memkv, wbr, w_out, conv_w, conv_b, w_a, b_a, w_x, b_x, lam):
    S = st["h"].shape[0]
    M = mem.shape[0]
    seg, ys, tm = st["seg"], st["ys"], st["tm"]
    st["dw_out"] = _matmul(st["merged"], st["dout"], "tn", D_MODEL, D_MODEL, S, 256, D_MODEL, S, BF, "dw_out", out_blocked="row")
    dgl0, dgl1, dgl2, dp0, dp1, dp2 = _merge_bwd(st["dout"], w_out, ys, wbr, seg["gl"], tm)
    st["dgl"] = (dgl0, dgl1, dgl2)
    dys, dwbr = [], []
    for i, dp in enumerate((dp0, dp1, dp2)):
        dys.append(_matmul(dp, wbr[i], "nt", S, 1024, D_MODEL, tm, 1024, D_MODEL, F32, "dy_br%d" % i))
        dwbr.append(_matmul(ys[i], dp, "tn", 1024, D_MODEL, S, 1024, 256, S, BF, "dw_br%d" % i, out_blocked="col"))
    st["dys"], st["dwbr"] = dys, dwbr
    st["dq_m"], st["dg_mem"], dmkv = _mem_bwd(seg["q_m"], st["mkv"], seg["g_mem"], st["o_mem"], dys[2])
    dmkv_b = dmkv.astype(BF)
    st["dw_memkv"] = _matmul(st["memn"], dmkv_b, "tn", D_MODEL, 2 * D_MEM, M, 256, 2 * D_MEM, M, BF, "dw_memkv", out_blocked="row")
    dmemn = _matmul(dmkv_b, w_memkv, "nt", M, D_MODEL, 2 * D_MEM, M, 512, 2 * D_MEM, F32, "dmemn")
    st["dmem_g"] = _rms_gain_grad(dmemn, mem, "dmem_gain")
    st["dxr"], st["dg_rg"], st["dw_a"], st["dw_x"], st["dvec"] = _rglru_bwd(
        seg["xr"], seg["g_rg"], st["h_rg"], dys[0], conv_w, conv_b, w_a, b_a, w_x, b_x, lam, st["T"])
    return st


def _backward_b(st, rel_bias, sinks):
    seg = st["seg"]
    dq_s, dg_swa, dkv, dsinks, drel = _swa_bwd(seg["q_s"], seg["kv"], seg["g_swa"], st["o_swa"], st["dys"][1],
                                               st["bucket"], rel_bias, _sink_column(sinks))
    st["dsinks"], st["drel"] = dsinks.reshape(1, SWA_HEADS), drel.reshape(REL_BUCKETS, SWA_HEADS)
    st["dproj"] = jnp.concatenate([st["dxr"], st["dg_rg"], dq_s, dkv.astype(BF), dg_swa, st["dq_m"], st["dg_mem"], *st["dgl"]], axis=1)
    return st


def _dw_in_half(st, half, dep=None):
    S = st["h"].shape[0]
    return _matmul(st["h"], st["dproj"], "tn", D_MODEL // 2, D_IN, S, 512, 1792, S, BF, "dw_in%d" % half, a_moff=2 * half, dep=dep)


def _owner_blocks(a):
    return jnp.swapaxes(a.reshape((4, 2) + a.shape[1:]), 0, 1)


def _local_step(x, mem, target, pre_g, post_g, mem_g, w_in, conv_w, conv_b, w_a, b_a, w_x, b_x, lam, sinks, rel_bias,
                w_memkv, wbr, w_out):
    st = _forward_a(x, mem, pre_g, mem_g, w_in, conv_w, conv_b, w_a, b_a, w_x, b_x, lam, sinks, rel_bias)
    st = _forward_b(st, x, target, post_g, w_memkv, wbr, w_out)
    st = _backward_a(st, mem, w_memkv, wbr, w_out, conv_w, conv_b, w_a, b_a, w_x, b_x, lam)
    st = _backward_b(st, rel_bias, sinks)
    st["dw_in"] = jnp.concatenate([_dw_in_half(st, 0), _dw_in_half(st, 1)], axis=0)
    st["grad_x"], st["dpre"] = _dh_dx(st["dproj"], w_in, x, st["dy"], pre_g, st["tm"], 896)
    return st


def _pad_rows(a, rows):
    a = a.reshape(-1, 128) if a.shape[-1] % 128 == 0 else jnp.pad(a, ((0, 0), (0, 128 - a.shape[-1])))
    return jnp.pad(a, ((0, rows - a.shape[0]), (0, 0))) if a.shape[0] < rows else a


def kernel(x, mem, pre_norm_g, post_norm_g, mem_norm_g, w_in, conv_w, conv_b, w_rg_a, b_rg_a, w_rg_x, b_rg_x, lru_lambda, swa_sinks, rel_bias, w_mem_kv, w_br_rg, w_br_swa, w_br_mem, w_out, loss_target, m_pre_norm_g, m_post_norm_g, m_mem_norm_g, m_w_in, m_conv_w, m_conv_b, m_w_rg_a, m_b_rg_a, m_w_rg_x, m_b_rg_x, m_lru_lambda, m_swa_sinks, m_rel_bias, m_w_mem_kv, m_w_br_rg, m_w_br_swa, m_w_br_mem, m_w_out, v_pre_norm_g, v_post_norm_g, v_mem_norm_g, v_w_in, v_conv_w, v_conv_b, v_w_rg_a, v_b_rg_a, v_w_rg_x, v_b_rg_x, v_lru_lambda, v_swa_sinks, v_rel_bias, v_w_mem_kv, v_w_br_rg, v_w_br_swa, v_w_br_mem, v_w_out):
    cx, cy, cc = lax.axis_index("x"), lax.axis_index("y"), lax.axis_index("c")
    me = 4 * cx + 2 * cy + cc
    chip = 2 * cx + cy
    core = jnp.reshape(cc, (1,)).astype(jnp.int32)
    x0, mem0 = x[0], mem[0]
    w_a_b, w_x_b = w_rg_a[0].astype(BF), w_rg_x[0].astype(BF)

    def landing(own, slot, slots, kind="lead"):
        if kind == "cols":
            return lax.dynamic_update_slice(lax.empty((own.shape[0], slots * own.shape[1]), own.dtype), own, (0, slot * own.shape[1]))
        return lax.dynamic_update_slice(lax.empty((slots,) + own.shape, own.dtype), own[None], (slot,) + (0,) * own.ndim)

    def swap_start(parts, tag):
        return _exchange_start(parts, [lax.empty(p.shape[1:], p.dtype) for p in parts], _plan_swap(len(parts)), "swap_%s_start" % tag)

    def scatter_start(swap, after, tag):
        s_send, s_recv, parts, got, _ = swap
        got = _exchange_wait(s_send, s_recv, parts, got, _plan_swap(len(parts)), after, "swap_%s_wait" % tag)
        sums = [_pair_sum(p, g, core, "scatter_%s_sum%d" % (tag, i)) for i, (p, g) in enumerate(zip(parts, got))]
        lands = [landing(lax.dynamic_index_in_dim(s, chip, 0, keepdims=False), chip, 4) for s in sums]
        return _exchange_start(sums, lands, _plan_scatter(len(sums)), "scatter_%s_start" % tag)

    def zero_after(a):
        return jnp.minimum(jnp.abs(a.reshape(-1)[0].astype(F32)), 0.0)

    g_in, g_cw = _all_gather_relayed([jnp.transpose(w_in[0]).astype(BF), conv_w[0]], [True, False], "gather_w_in")
    w_in_f = g_in.reshape(D_IN, D_MODEL)
    conv_w_f = jnp.transpose(g_cw, (1, 0, 2)).reshape(CONV_W, D_RNN)

    after_first = zero_after(g_cw).astype(BF)
    rest = [w.astype(BF) + after_first for w in (w_mem_kv[0], w_br_rg[0], w_br_swa[0], w_br_mem[0], w_out[0])]
    kinds = ["lead", "cols", "cols", "cols", "lead"]
    plan_g = _plan_gather(kinds)
    g_send, g_recv, g_src, g_land, g_token = _exchange_start(rest, [landing(w, me, N_DEV, kd) for w, kd in zip(rest, kinds)], plan_g,
                                                             "gather_rest_start")
    st = _forward_a(x0, mem0, pre_norm_g + g_token[0:1, 0:1], mem_norm_g, w_in_f, conv_w_f, conv_b, w_a_b, b_rg_a, w_x_b, b_rg_x,
                    lru_lambda, swa_sinks, rel_bias)
    g_land = _exchange_wait(g_send, g_recv, g_src, g_land, plan_g, st["y_swa"], "gather_rest_wait")
    g_land = _forward_to_sibling(g_land, kinds, "gather_rest_forward")
    w_memkv_f = g_land[0].reshape(D_MODEL, 2 * D_MEM)
    wbr = (g_land[1], g_land[2], g_land[3])
    w_out_f = g_land[4].reshape(D_MODEL, D_MODEL)

    st = _forward_b(st, x0, loss_target[0], post_norm_g, w_memkv_f, wbr, w_out_f)
    st = _backward_a(st, mem0, w_memkv_f, wbr, w_out_f, conv_w_f, conv_b, w_a_b, b_rg_a, w_x_b, b_rg_x, lru_lambda)
    parts_a = [st["dw_memkv"], st["dwbr"][0], st["dwbr"][1], st["dwbr"][2], st["dw_out"],
               _owner_blocks(st["dw_a"]), _owner_blocks(st["dw_x"])]
    plan_a = _plan_scatter(len(parts_a))
    swap_a = swap_start(parts_a, "a")

    st = _backward_b(st, rel_bias, swa_sinks + swap_a[4][0:1, 0:1])
    a_send, a_recv, a_src, a_land, a_token = scatter_start(swap_a, st["dproj"], "a")
    plan_b = _plan_scatter(1)

    def dw_in_parts(half, dep):
        dwh = _dw_in_half(st, half, dep)
        return dwh, [jnp.transpose(dwh.reshape(D_MODEL // 2, 4, 2, D_IN // N_DEV), (2, 1, 0, 3))]

    dw0, parts_b0 = dw_in_parts(0, a_token)
    swap_b0 = swap_start(parts_b0, "b0")
    a_land = _exchange_wait(a_send, a_recv, a_src, a_land, plan_a, swap_b0[4], "scatter_a_wait")
    g_wa_blk = _sum_parts(a_land[5], "sum_w_rg_a")
    g_wx_blk = _sum_parts(a_land[6], "sum_w_rg_x")
    big = [None] * 6

    def adamw_big(j, wt, mt, vt):
        big[j] = [a[None] for a in _adamw(a_land[j - 1], wt[0], mt[0], vt[0], "adamw_big%d" % j)]

    adamw_big(1, w_mem_kv, m_w_mem_kv, v_w_mem_kv)
    adamw_big(2, w_br_rg, m_w_br_rg, v_w_br_rg)
    adamw_big(3, w_br_swa, m_w_br_swa, v_w_br_swa)
    halves = [scatter_start(swap_b0, big[3][1], "b0")]
    dw1, parts_b1 = dw_in_parts(1, halves[0][4])
    swap_b1 = swap_start(parts_b1, "b1")
    adamw_big(4, w_br_mem + swap_b1[4][0:1, 0:1], m_w_br_mem, v_w_br_mem)
    adamw_big(5, w_out, m_w_out, v_w_out)
    halves.append(scatter_start(swap_b1, big[5][1], "b1"))
    grad_x, dpre = _dh_dx(st["dproj"], w_in_f, x0, st["dy"], pre_norm_g + halves[1][4][0:1, 0:1], st["tm"], 896)
    after, b_lands = grad_x, []
    for half, (b_send, b_recv, b_src, b_land, _) in enumerate(halves):
        b_lands.append(_exchange_wait(b_send, b_recv, b_src, b_land, plan_b, after, "scatter_b%d_wait" % half)[0])
        after = b_lands[-1]
    big[0] = [a[None] for a in _adamw(b_lands, w_in[0], m_w_in[0], v_w_in[0], "adamw_big0")]
    links_free = jnp.minimum(jnp.abs(big[0][0][0, 0, 0]), 0.0)

    pack = jnp.concatenate([dpre.reshape(16, 128), st["dpost"].reshape(16, 128), st["dmem_g"].reshape(16, 128),
                            st["dvec"].reshape(64, 128), _pad_rows(st["dsinks"], 8), _pad_rows(st["drel"], 32), g_wa_blk, g_wx_blk], axis=0) + links_free
    gathered = _all_gather([pack], "gather_small")[0]
    gs = _sum_parts(gathered, "sum_small")
    g_pre, g_post, g_memg = gs[0:16].reshape(1, D_MODEL), gs[16:32].reshape(1, D_MODEL), gs[32:48].reshape(1, D_MODEL)
    gvec = gs[48:112].reshape(8, D_RNN)
    g_conv_w = lax.dynamic_slice(gvec[0:CONV_W], (0, me * RNN_BLOCK), (CONV_W, RNN_BLOCK))
    g_conv_b, g_b_a, g_b_x, g_lam = gvec[4:5], gvec[5:6], gvec[6:7], gvec[7:8]
    g_sinks = gs[112:113, :SWA_HEADS]
    g_rel = gs[120:152, :SWA_HEADS]
    g_w_a = gathered[:, 152:280]
    g_w_x = gathered[:, 280:408]

    def packed(ts):
        pre, post, memg, cb, ba, bx, lm, wa, wx, sk, rel, cw = ts
        return jnp.concatenate([pre.reshape(16, 128), post.reshape(16, 128), memg.reshape(16, 128), cb.reshape(8, 128),
                                ba.reshape(8, 128), bx.reshape(8, 128), lm.reshape(8, 128), wa.reshape(1024, 128),
                                wx.reshape(1024, 128), _pad_rows(sk.reshape(1, SWA_HEADS), 8), _pad_rows(rel, 32),
                                _pad_rows(cw.reshape(CONV_W, RNN_BLOCK), 8)], axis=0)

    def unpacked(a):
        return (a[0:16].reshape(1, D_MODEL), a[16:32].reshape(1, D_MODEL), a[32:48].reshape(1, D_MODEL), a[48:56].reshape(1, D_RNN),
                a[56:64].reshape(1, D_RNN), a[64:72].reshape(1, D_RNN), a[72:80].reshape(1, D_RNN),
                a[80:1104].reshape(1, RNN_BLOCKS, RNN_BLOCK, RNN_BLOCK), a[1104:2128].reshape(1, RNN_BLOCKS, RNN_BLOCK, RNN_BLOCK),
                a[2128:2129, :SWA_HEADS], a[2136:2168, :SWA_HEADS], a[2168:2172].reshape(1, CONV_W, RNN_BLOCK))

    g_small = (g_pre, g_post, g_memg, g_conv_b, g_b_a, g_b_x, g_lam, g_w_a, g_w_x, g_sinks, g_rel, g_conv_w)
    w_small = (pre_norm_g, post_norm_g, mem_norm_g, conv_b, b_rg_a, b_rg_x, lru_lambda, w_rg_a, w_rg_x, swa_sinks, rel_bias, conv_w)
    m_small = (m_pre_norm_g, m_post_norm_g, m_mem_norm_g, m_conv_b, m_b_rg_a, m_b_rg_x, m_lru_lambda, m_w_rg_a, m_w_rg_x, m_swa_sinks, m_rel_bias, m_conv_w)
    v_small = (v_pre_norm_g, v_post_norm_g, v_mem_norm_g, v_conv_b, v_b_rg_a, v_b_rg_x, v_lru_lambda, v_w_rg_a, v_w_rg_x, v_swa_sinks, v_rel_bias, v_conv_w)
    sm = [unpacked(a) for a in _adamw(packed(g_small)[None], packed(w_small), packed(m_small), packed(v_small), "adamw_small")]


    loss_total = lax.psum(st["loss"][0, 0], AXES)

    def leaves(k):
        s = sm[k]
        return [s[0], s[1], s[2], big[0][k], s[11], s[3], s[7], s[4], s[8], s[5], s[6], s[9], s[10],
                big[1][k], big[2][k], big[3][k], big[4][k], big[5][k]]

    return (loss_total, grad_x[None], *leaves(0), *leaves(1), *leaves(2), *leaves(3))
```

```python
import math

import jax
import jax.numpy as jnp
import numpy as np
from jax import lax
from jax.experimental import pallas as pl
from jax.experimental.pallas import tpu as pltpu

F32, BF = jnp.float32, jnp.bfloat16
MESH = pl.DeviceIdType.MESH
AXES = ("x", "y", "c")
N_DEV = 8

D_MODEL = 2048
D_RNN = 1024
RNN_BLOCKS = 8
RNN_BLOCK = 128
CONV_W = 4
LRU_C = 8.0
SWA_HEADS = 16
SWA_KV_HEADS = 2
SWA_HD = 64
WINDOW = 128
MEM_HEADS = 4
MEM_HD = 256
D_MEM = 1024
REL_BUCKETS = 32
REL_MAX_DIST = 128
EPS = 1e-6
NEG_INF = -1e30
D_IN = 12544
SEGMENTS = (("xr", 0, 1024, F32), ("g_rg", 1024, 1024, F32), ("q_s", 2048, 1024, BF), ("kv", 3072, 256, BF),
            ("g_swa", 3328, 1024, F32), ("q_m", 4352, 1024, BF), ("g_mem", 5376, 1024, F32), ("gl", 6400, 6144, F32))
SEG_TILE = 256

ADAM_LR, ADAM_B1, ADAM_B2, ADAM_EPS, ADAM_WD, ADAM_STEP = 0.001, 0.9, 0.999, 1e-08, 0.01, 10

NN = (((1,), (0,)), ((), ()))
NT = (((1,), (1,)), ((), ()))
TN = (((0,), (0,)), ((), ()))
MIB = 2 ** 20


def _dot(a, b, dn):
    return lax.dot_general(a, b, dn, preferred_element_type=F32)


def _params(sem, vmem_mib=48):
    return pltpu.CompilerParams(dimension_semantics=sem, vmem_limit_bytes=vmem_mib * MIB)


def _sigmoid(z):
    return 1.0 / (1.0 + jnp.exp(-z))


def _softplus(z):
    return jnp.maximum(z, 0.0) + jnp.log(1.0 + jnp.exp(-jnp.abs(z)))


def _expm1(z):
    p = z * (1.0 + z * (0.5 + z * (1.0 / 6 + z * (1.0 / 24 + z * (1.0 / 120 + z * (1.0 / 720 + z * (1.0 / 5040 + z / 40320)))))))
    return jnp.where(jnp.abs(z) < 0.3, p, jnp.exp(z) - 1.0)


def _flat(p):
    return 4 * p[0] + 2 * p[1] + p[2]


def _all_gather(arrs, name):
    n = len(arrs)

    def body(*refs):
        ins, outs = refs[:n], refs[n:2 * n]
        send_sems, recv_sems, local_sems = refs[2 * n:]
        x, y, c = lax.axis_index("x"), lax.axis_index("y"), lax.axis_index("c")
        me, sibling = (x, y, c), (x, y, 1 - c)
        chips = [(1 - x, y), (x, 1 - y), (1 - x, 1 - y)]

        def copy(a, k, block, to, src=None):
            dst = outs[a].at[_flat(block)]
            return pltpu.make_async_remote_copy(src_ref=dst if src is None else src, dst_ref=dst,
                                                send_sem=send_sems.at[a * 7 + k], recv_sem=recv_sems.at[a * 7 + k],
                                                device_id=to, device_id_type=MESH)

        mine = [pltpu.make_async_copy(ins[a], outs[a].at[_flat(me)], local_sems.at[a]) for a in range(n)]
        for cp in mine:
            cp.start()
        first = []
        for a in range(n):
            first += [copy(a, 1 + j, me, (*chip, c), src=ins[a]) for j, chip in enumerate(chips)]
            first.append(copy(a, 0, me, sibling, src=ins[a]))
        for cp in first:
            cp.start()
        passed = []
        for j, chip in enumerate(chips):
            for a in range(n):
                copy(a, 1 + j, (*chip, c), me).wait_recv()
                fw = copy(a, 4 + j, (*chip, c), sibling)
                fw.start()
                passed.append(fw)
        for a in range(n):
            copy(a, 0, sibling, me).wait_recv()
            for j, chip in enumerate(chips):
                copy(a, 4 + j, (*chip, 1 - c), me).wait_recv()
        for cp in first + passed:
            cp.wait_send()
        for cp in mine:
            cp.wait()

    any_spec = pl.BlockSpec(memory_space=pl.ANY)
    return pl.pallas_call(
        body, name=name,
        out_shape=[jax.ShapeDtypeStruct((N_DEV,) + a.shape, a.dtype) for a in arrs],
        in_specs=[any_spec] * n, out_specs=[any_spec] * n,
        scratch_shapes=[pltpu.SemaphoreType.DMA((7 * n,)), pltpu.SemaphoreType.DMA((7 * n,)), pltpu.SemaphoreType.DMA((n,))],
    )(*arrs)


def _all_gather_relayed(arrs, relay, name):
    n = len(arrs)
    K = 9

    def body(*refs):
        ins, outs = refs[:n], refs[n:2 * n]
        send_sems, recv_sems, local_sems = refs[2 * n:]
        x, y, c = lax.axis_index("x"), lax.axis_index("y"), lax.axis_index("c")
        me, sib = (x, y, c), (x, y, 1 - c)
        xn, yn, dg = (1 - x, y, c), (x, 1 - y, c), (1 - x, 1 - y, c)

        def other(p):
            return (p[0], p[1], 1 - p[2])

        def rows(a, half):
            h = arrs[a].shape[0] // 2
            return pl.ds(half * h, h)

        def copy(a, k, block, to, half=None, src=None):
            dst = outs[a].at[_flat(block)]
            if half is not None:
                dst = dst.at[rows(a, half)]
            return pltpu.make_async_remote_copy(src_ref=dst if src is None else src, dst_ref=dst,
                                                send_sem=send_sems.at[a * K + k], recv_sem=recv_sems.at[a * K + k],
                                                device_id=to, device_id_type=MESH)

        mine = [pltpu.make_async_copy(ins[a], outs[a].at[_flat(me)], local_sems.at[a]) for a in range(n)]
        for cp in mine:
            cp.start()
        sends = []

        def start(cp):
            cp.start()
            sends.append(cp)

        for a in range(n):
            start(copy(a, 1, me, xn, src=ins[a]))
            start(copy(a, 2, me, yn, src=ins[a]))
            if not relay[a]:
                start(copy(a, 3, me, dg, src=ins[a]))
            start(copy(a, 0, me, sib, src=ins[a]))
        for a in range(n):
            copy(a, 1, xn, me).wait_recv()
            if relay[a]:
                start(copy(a, 3, xn, yn, half=0))
            start(copy(a, 5, xn, sib))
        for a in range(n):
            copy(a, 2, yn, me).wait_recv()
            if relay[a]:
                start(copy(a, 4, yn, xn, half=1))
            start(copy(a, 6, yn, sib))
        for a in range(n):
            if relay[a]:
                copy(a, 3, dg, me, half=0).wait_recv()
                start(copy(a, 7, dg, sib, half=0))
                copy(a, 4, dg, me, half=1).wait_recv()
                start(copy(a, 8, dg, sib, half=1))
            else:
                copy(a, 3, dg, me).wait_recv()
                start(copy(a, 7, dg, sib))
        for a in range(n):
            copy(a, 0, sib, me).wait_recv()
            copy(a, 5, other(xn), me).wait_recv()
            copy(a, 6, other(yn), me).wait_recv()
            if relay[a]:
                copy(a, 7, other(dg), me, half=0).wait_recv()
                copy(a, 8, other(dg), me, half=1).wait_recv()
            else:
                copy(a, 7, other(dg), me).wait_recv()
        for cp in sends:
            cp.wait_send()
        for cp in mine:
            cp.wait()

    any_spec = pl.BlockSpec(memory_space=pl.ANY)
    return pl.pallas_call(
        body, name=name,
        out_shape=[jax.ShapeDtypeStruct((N_DEV,) + a.shape, a.dtype) for a in arrs],
        in_specs=[any_spec] * n, out_specs=[any_spec] * n,
        scratch_shapes=[pltpu.SemaphoreType.DMA((K * n,)), pltpu.SemaphoreType.DMA((K * n,)), pltpu.SemaphoreType.DMA((n,))],
    )(*arrs)


def _chip_peers(x, y):
    return [(1 - x, y), (x, 1 - y), (1 - x, 1 - y)]


def _chip(p):
    return 2 * p[0] + p[1]


def _plan_gather(kinds):
    def plan(x, y, c):
        out = []
        for a, kind in enumerate(kinds):
            for peer in [(x, y, 1 - c)] + [(*ch, c) for ch in _chip_peers(x, y)]:
                out.append((a, None, (kind, _flat((x, y, c))), peer, (kind, _flat(peer))))
        return out
    return plan


def _plan_swap(n):
    def plan(x, y, c):
        return [(a, 1 - c, ("all", 0), (x, y, 1 - c), ("all", 0)) for a in range(n)]
    return plan


def _slot(ref, where):
    kind, k = where
    if kind == "all":
        return ref
    if kind == "lead":
        return ref.at[k]
    return ref.at[:, pl.ds(pl.multiple_of(k * 256, 256), 256)]


def _plan_scatter(n):
    def plan(x, y, c):
        out = []
        for a in range(n):
            for ch in _chip_peers(x, y):
                out.append((a, _chip(ch), ("lead", _chip((x, y))), (*ch, c), ("lead", _chip(ch))))
        return out
    return plan


HBM_SPEC = pl.BlockSpec(memory_space=pltpu.HBM)
SEM_SPEC = pl.BlockSpec(memory_space=pltpu.SEMAPHORE)


def _in_hbm(a):
    return pltpu.with_memory_space_constraint(a, pltpu.HBM)


def _exchange_start(srcs, lands, plan, name):
    n = len(srcs)
    count = len(plan(0, 0, 0))

    def body(*refs):
        src_refs, land_refs = refs[:n], refs[n:2 * n]
        send_sems, recv_sems = refs[2 * n], refs[2 * n + 1]
        token = refs[-1]
        x, y, c = lax.axis_index("x"), lax.axis_index("y"), lax.axis_index("c")
        for k, (a, si, di, peer, _) in enumerate(plan(x, y, c)):
            src = src_refs[a] if si is None else src_refs[a].at[si]
            pltpu.make_async_remote_copy(src_ref=src, dst_ref=_slot(land_refs[a], di), send_sem=send_sems.at[k],
                                         recv_sem=recv_sems.at[k], device_id=peer, device_id_type=MESH).start()
        token[...] = jnp.zeros_like(token)

    out = pl.pallas_call(
        body, name=name,
        out_shape=(pltpu.SemaphoreType.DMA((count,)), pltpu.SemaphoreType.DMA((count,)),
                   *[pltpu.HBM(a.shape, a.dtype) for a in lands], jax.ShapeDtypeStruct((8, 128), F32)),
        in_specs=[HBM_SPEC] * (2 * n),
        out_specs=(SEM_SPEC, SEM_SPEC, *([HBM_SPEC] * n), pl.BlockSpec(memory_space=pltpu.VMEM)),
        input_output_aliases={n + i: 2 + i for i in range(n)},
        compiler_params=pltpu.CompilerParams(has_side_effects=pltpu.SideEffectType.DATAFLOW_SIDE_EFFECTING),
    )(*[_in_hbm(a) for a in srcs], *[_in_hbm(a) for a in lands])
    return out[0], out[1], list(srcs), list(out[2:2 + n]), out[-1]


def _exchange_wait(send_sems, recv_sems, srcs, lands, plan, after, name):
    n = len(srcs)

    def body(*refs):
        src_refs, land_refs = refs[:n], refs[n:2 * n]
        send_sems, recv_sems = refs[2 * n], refs[2 * n + 1]
        x, y, c = lax.axis_index("x"), lax.axis_index("y"), lax.axis_index("c")
        for k, (a, si, _, peer, ri) in enumerate(plan(x, y, c)):
            src = src_refs[a] if si is None else src_refs[a].at[si]
            cp = pltpu.make_async_remote_copy(src_ref=src, dst_ref=_slot(land_refs[a], ri), send_sem=send_sems.at[k],
                                              recv_sem=recv_sems.at[k], device_id=peer, device_id_type=MESH)
            cp.wait_send()
            cp.wait_recv()

    out = pl.pallas_call(
        body, name=name,
        out_shape=tuple(pltpu.HBM(a.shape, a.dtype) for a in lands),
        in_specs=[HBM_SPEC] * (2 * n) + [SEM_SPEC, SEM_SPEC, pl.BlockSpec(memory_space=pl.ANY)],
        out_specs=tuple([HBM_SPEC] * n),
        input_output_aliases={n + i: i for i in range(n)},
        compiler_params=pltpu.CompilerParams(has_side_effects=pltpu.SideEffectType.DATAFLOW_SIDE_EFFECTING),
    )(*[_in_hbm(a) for a in srcs], *lands, send_sems, recv_sems, after)
    return list(out)


def _forward_to_sibling(lands, kinds, name):
    n = len(lands)

    def body(*refs):
        in_refs, out_refs = refs[:n], refs[n:2 * n]
        send_sems, recv_sems = refs[2 * n:]
        x, y, c = lax.axis_index("x"), lax.axis_index("y"), lax.axis_index("c")
        sibling = (x, y, 1 - c)

        def copy(a, j, slot):
            return pltpu.make_async_remote_copy(src_ref=_slot(in_refs[a], (kinds[a], slot)), dst_ref=_slot(out_refs[a], (kinds[a], slot)),
                                                send_sem=send_sems.at[a * 3 + j], recv_sem=recv_sems.at[a * 3 + j],
                                                device_id=sibling, device_id_type=MESH)

        sends = [copy(a, j, _flat((*ch, c))) for a in range(n) for j, ch in enumerate(_chip_peers(x, y))]
        for cp in sends:
            cp.start()
        for a in range(n):
            for j, ch in enumerate(_chip_peers(x, y)):
                copy(a, j, _flat((*ch, 1 - c))).wait_recv()
        for cp in sends:
            cp.wait_send()

    any_spec = pl.BlockSpec(memory_space=pl.ANY)
    return pl.pallas_call(
        body, name=name, out_shape=[jax.ShapeDtypeStruct(a.shape, a.dtype) for a in lands],
        in_specs=[any_spec] * n, out_specs=[any_spec] * n, input_output_aliases={a: a for a in range(n)},
        scratch_shapes=[pltpu.SemaphoreType.DMA((3 * n,)), pltpu.SemaphoreType.DMA((3 * n,))],
    )(*lands)


def _swap_with_sibling(parts, name):
    n = len(parts)

    def body(*refs):
        in_refs, out_refs = refs[:n], refs[n:2 * n]
        send_sems, recv_sems = refs[2 * n:]
        x, y, c = lax.axis_index("x"), lax.axis_index("y"), lax.axis_index("c")
        sends = [pltpu.make_async_remote_copy(src_ref=in_refs[a].at[1 - c], dst_ref=out_refs[a], send_sem=send_sems.at[a],
                                              recv_sem=recv_sems.at[a], device_id=(x, y, 1 - c), device_id_type=MESH)
                 for a in range(n)]
        for cp in sends:
            cp.start()
        for cp in sends:
            cp.wait()

    any_spec = pl.BlockSpec(memory_space=pl.ANY)
    return pl.pallas_call(
        body, name=name, out_shape=[jax.ShapeDtypeStruct(a.shape[1:], a.dtype) for a in parts],
        in_specs=[any_spec] * n, out_specs=[any_spec] * n,
        scratch_shapes=[pltpu.SemaphoreType.DMA((n,)), pltpu.SemaphoreType.DMA((n,))],
    )(*parts)


def _pair_sum(parts, got, core, name):
    _, _, R, C = parts.shape
    tr = 256 if R % 256 == 0 else R

    def body(c_ref, p_ref, g_ref, o_ref):
        o_ref[...] = (p_ref[...].astype(F32) + g_ref[...].astype(F32)).astype(o_ref.dtype)

    return pl.pallas_call(
        body, name=name,
        grid_spec=pltpu.PrefetchScalarGridSpec(
            num_scalar_prefetch=1, grid=(4, R // tr),
            in_specs=[pl.BlockSpec((None, None, tr, C), lambda j, i, c_ref: (c_ref[0], j, i, 0)),
                      pl.BlockSpec((None, tr, C), lambda j, i, c_ref: (j, i, 0))],
            out_specs=pl.BlockSpec((None, tr, C), lambda j, i, c_ref: (j, i, 0))),
        out_shape=jax.ShapeDtypeStruct((4, R, C), parts.dtype),
        compiler_params=_params(("parallel", "parallel")),
    )(core, parts, got)


def _matmul(a, b, mode, M, N, K, tm, tn, tk, out_dtype, name, b_noff=0, a_moff=0, a_koff=0, b_blocked=False, out_blocked=None,
            dep=None, addend=None, vmem_mib=48):
    nm, nn, nk = M // tm, N // tn, K // tk
    if mode == "nn":
        a_spec = pl.BlockSpec((tm, tk), lambda j, i, k: (i, k + a_koff))
        b_spec = pl.BlockSpec((tk, tn), lambda j, i, k: (k, j + b_noff))
        dn = NN
    elif mode == "nt":
        a_spec = pl.BlockSpec((tm, tk), lambda j, i, k: (i, k + a_koff))
        if b_blocked:
            b_spec = pl.BlockSpec((None, tn, tk), lambda j, i, k: (k, j, 0))
        else:
            b_spec = pl.BlockSpec((tn, tk), lambda j, i, k: (j + b_noff, k))
        dn = NT
    else:
        a_spec = pl.BlockSpec((tk, tm), lambda j, i, k: (k, i + a_moff))
        if b_blocked:
            b_spec = pl.BlockSpec((None, tk, tn), lambda j, i, k: (j, k, 0))
        else:
            b_spec = pl.BlockSpec((tk, tn), lambda j, i, k: (k, j + b_noff))
        dn = TN
    if out_blocked == "col":
        out_shape = jax.ShapeDtypeStruct((2, 4, M, tn), out_dtype)
        out_spec = pl.BlockSpec((None, None, tm, tn), lambda j, i, k: (j % 2, j // 2, i, 0))
    elif out_blocked == "row":
        out_shape = jax.ShapeDtypeStruct((2, 4, tm, N), out_dtype)
        out_spec = pl.BlockSpec((None, None, tm, tn), lambda j, i, k: (i % 2, i // 2, 0, j))
    elif out_blocked == "third":
        out_shape = jax.ShapeDtypeStruct((3, M, N // 3), out_dtype)
        out_spec = pl.BlockSpec((None, tm, tn), lambda j, i, k: (j // (nn // 3), i, j % (nn // 3)))
    else:
        out_shape = jax.ShapeDtypeStruct((M, N), out_dtype)
        out_spec = pl.BlockSpec((tm, tn), lambda j, i, k: (i, j))

    n_extra = (addend is not None) + (dep is not None)

    def body(a_ref, b_ref, *rest):
        o_ref, scratch = rest[n_extra], rest[n_extra + 1:]
        if nk == 1:
            prod = _dot(a_ref[...], b_ref[...], dn)
            if addend is not None:
                prod = prod + rest[0][...].astype(F32)
            o_ref[...] = prod.astype(out_dtype)
        else:
            assert addend is None
            acc_ref, = scratch
            k = pl.program_id(2)

            @pl.when(k == 0)
            def _():
                acc_ref[...] = jnp.zeros_like(acc_ref)

            acc_ref[...] += _dot(a_ref[...], b_ref[...], dn)

            @pl.when(k == nk - 1)
            def _():
                o_ref[...] = acc_ref[...].astype(out_dtype)

    return pl.pallas_call(
        body, name=name, grid=(nn, nm, nk),
        in_specs=[a_spec, b_spec] + ([] if addend is None else [out_spec])
        + ([] if dep is None else [pl.BlockSpec((8, 128), lambda j, i, k: (0, 0))]),
        out_specs=out_spec, out_shape=out_shape,
        scratch_shapes=[] if nk == 1 else [pltpu.VMEM((tm, tn), F32)],
        compiler_params=_params(("parallel", "parallel", "arbitrary"), vmem_mib),
    )(a, b, *([] if addend is None else [addend]), *([] if dep is None else [dep]))


def _rms_fwd(x, g, name):
    R, Dm = x.shape
    tr = min(R, 256)

    def body(x_ref, g_ref, h_ref):
        xv = x_ref[...]
        r = lax.rsqrt(jnp.mean(xv * xv, axis=-1, keepdims=True) + EPS)
        h_ref[...] = (xv * r * g_ref[...]).astype(BF)

    return pl.pallas_call(
        body, name=name, grid=(R // tr,),
        in_specs=[pl.BlockSpec((tr, Dm), lambda i: (i, 0)), pl.BlockSpec((1, Dm), lambda i: (0, 0))],
        out_specs=pl.BlockSpec((tr, Dm), lambda i: (i, 0)), out_shape=jax.ShapeDtypeStruct((R, Dm), BF),
        compiler_params=_params(("parallel",)),
    )(x, g)


def _rms_gain_grad(dn, x, name):
    R, Dm = x.shape

    def body(dn_ref, x_ref, o_ref):
        xv = x_ref[...]
        r = lax.rsqrt(jnp.mean(xv * xv, axis=-1, keepdims=True) + EPS)
        o_ref[...] = jnp.sum(dn_ref[...] * xv * r, axis=0, keepdims=True)

    return pl.pallas_call(
        body, name=name, out_shape=jax.ShapeDtypeStruct((1, Dm), F32),
        compiler_params=pltpu.CompilerParams(vmem_limit_bytes=32 * MIB),
    )(dn, x)


def _shift_down(v, k, head8, row, T):
    if k == 0:
        return v
    r = pltpu.roll(v, k, 0)
    hr = pltpu.roll(head8, k, 0)
    top = jnp.where(row[:8] < k, hr, r[:8])
    return jnp.concatenate([top, r[8:]], axis=0)


def _shift_up(v, k, tail8, row, T):
    if k == 0:
        return v
    r = pltpu.roll(v, T - k, 0)
    tr = pltpu.roll(tail8, 8 - k, 0)
    bot = jnp.where(row[:8] >= 8 - k, tr, r[T - 8:])
    return jnp.concatenate([r[:T - 8], bot], axis=0)


def _rglru_gates(u, head8, grow, row, T, cw_ref, cb_ref, wa_ref, ba_ref, wx_ref, bx_ref, lam_ref):
    us = [_shift_down(u, k, head8, row, T) for k in range(CONV_W)]
    acc = us[0] * cw_ref[0:1, :]
    for k in range(1, CONV_W):
        acc = acc + us[k] * cw_ref[k:k + 1, :]
    conv = cb_ref[...] + acc
    cbf = conv.astype(BF)
    r_ = _sigmoid(_dot(cbf, wa_ref[0], NN) + ba_ref[...])
    i_ = _sigmoid(_dot(cbf, wx_ref[0], NN) + bx_ref[...])
    sp = _softplus(-lam_ref[...])
    la = -LRU_C * r_ * sp
    a = jnp.exp(la)
    mult_raw = jnp.sqrt(-_expm1(2.0 * la))
    mult = jnp.where(grow == 0, 1.0, mult_raw)
    return us, conv, cbf, r_, i_, sp, a, mult_raw, mult


def _rglru_specs(T, nt, rev):
    tmap = (lambda n, t: (nt - 1 - t, n)) if rev else (lambda n, t: (t, n))
    hmap = ((lambda n, t: (jnp.maximum((nt - 1 - t) * (T // 8) - 1, 0), n)) if rev
            else (lambda n, t: (jnp.maximum(t * (T // 8) - 1, 0), n)))
    tile = pl.BlockSpec((T, RNN_BLOCK), tmap)
    halo = pl.BlockSpec((8, RNN_BLOCK), hmap)
    vec = pl.BlockSpec((1, RNN_BLOCK), lambda n, t: (0, n))
    cw = pl.BlockSpec((CONV_W, RNN_BLOCK), lambda n, t: (0, n))
    wblk = pl.BlockSpec((1, RNN_BLOCK, RNN_BLOCK), lambda n, t: (n, 0, 0))
    return tile, halo, vec, cw, wblk


def _rglru_fwd(xr, g, cw, cb, wa, ba, wx, bx, lam, T):
    S = xr.shape[0]
    nt = S // T

    def body(u_ref, uh_ref, g_ref, cw_ref, cb_ref, wa_ref, ba_ref, wx_ref, bx_ref, lam_ref, h_ref, y_ref, carry):
        t = pl.program_id(1)

        @pl.when(t == 0)
        def _():
            carry[...] = jnp.zeros_like(carry)

        row = lax.broadcasted_iota(jnp.int32, (T, RNN_BLOCK), 0)
        grow = row + t * T
        head8 = jnp.where(t > 0, uh_ref[...], 0.0)
        _, conv, _, _, i_, _, a, _, mult = _rglru_gates(u_ref[...], head8, grow, row, T, cw_ref, cb_ref, wa_ref, ba_ref,
                                                         wx_ref, bx_ref, lam_ref)
        b = mult * i_ * conv
        s = 1
        while s < T:
            keep = row >= s
            a_s = jnp.where(keep, pltpu.roll(a, s, 0), 1.0)
            b_s = jnp.where(keep, pltpu.roll(b, s, 0), 0.0)
            b = a * b_s + b
            a = a * a_s
            s *= 2
        h = b + a * carry[0:1, :]
        carry[...] = jnp.broadcast_to(h[T - 1:T, :], carry.shape)
        h_ref[...] = h
        gv = g_ref[...]
        y_ref[...] = (h * (gv * _sigmoid(gv))).astype(BF)

    tile, halo, vec, cwspec, wblk = _rglru_specs(T, nt, False)
    return pl.pallas_call(
        body, name="rglru_fwd", grid=(RNN_BLOCKS, nt),
        in_specs=[tile, halo, tile, cwspec, vec, wblk, vec, wblk, vec, vec],
        out_specs=[tile, tile],
        out_shape=[jax.ShapeDtypeStruct((S, D_RNN), F32), jax.ShapeDtypeStruct((S, D_RNN), BF)],
        scratch_shapes=[pltpu.VMEM((8, RNN_BLOCK), F32)],
        compiler_params=_params(("parallel", "arbitrary")),
    )(xr, xr, g, cw, cb, wa, ba, wx, bx, lam)


def _rglru_bwd(xr, g, h, dy, cw, cb, wa, ba, wx, bx, lam, T):
    S = xr.shape[0]
    nt = S // T

    def body(u_ref, uh_ref, g_ref, h_ref, hh_ref, dy_ref, cw_ref, cb_ref, wa_ref, ba_ref, wx_ref, bx_ref, lam_ref,
             du_ref, dg_ref, dwa_ref, dwx_ref, dvec_ref, c_dhh, c_a, c_dconv):
        t = pl.program_id(1)
        tt = nt - 1 - t

        @pl.when(t == 0)
        def _():
            c_dhh[...] = jnp.zeros_like(c_dhh)
            c_a[...] = jnp.zeros_like(c_a)
            c_dconv[...] = jnp.zeros_like(c_dconv)
            dwa_ref[...] = jnp.zeros_like(dwa_ref)
            dwx_ref[...] = jnp.zeros_like(dwx_ref)
            dvec_ref[...] = jnp.zeros_like(dvec_ref)

        row = lax.broadcasted_iota(jnp.int32, (T, RNN_BLOCK), 0)
        row8 = row[:8]
        grow = row + tt * T
        head8 = jnp.where(tt > 0, uh_ref[...], 0.0)
        us, conv, cbf, r_, i_, sp, a, mult_raw, mult = _rglru_gates(
            u_ref[...], head8, grow, row, T, cw_ref, cb_ref, wa_ref, ba_ref, wx_ref, bx_ref, lam_ref)
        hv = h_ref[...]
        hprev = _shift_down(hv, 1, jnp.where(tt > 0, hh_ref[...], 0.0), row, T)
        gv = g_ref[...]
        sg = _sigmoid(gv)
        dyv = dy_ref[...]
        dg_ref[...] = (dyv * hv * (sg * (1.0 + gv * (1.0 - sg)))).astype(BF)
        d = dyv * (gv * sg)
        A = _shift_up(a, 1, c_a[...], row, T)
        s = 1
        while s < T:
            keep = row < T - s
            A_s = jnp.where(keep, pltpu.roll(A, T - s, 0), 1.0)
            d_s = jnp.where(keep, pltpu.roll(d, T - s, 0), 0.0)
            d = A * d_s + d
            A = A * A_s
            s *= 2
        dhh = d + A * c_dhh[0:1, :]
        da = dhh * hprev
        dconv = dhh * mult * i_
        di = dhh * mult * conv
        dmult = dhh * i_ * conv
        dla = da * a - jnp.where(grow == 0, 0.0, dmult * (a * a) / mult_raw)
        dr = dla * (-LRU_C * sp)
        dsp = jnp.sum(dla * (-LRU_C * r_), axis=0, keepdims=True)
        dza = dr * r_ * (1.0 - r_)
        dzx = di * i_ * (1.0 - i_)
        dza_b, dzx_b = dza.astype(BF), dzx.astype(BF)
        dconv = dconv + _dot(dza_b, wa_ref[0], NT) + _dot(dzx_b, wx_ref[0], NT)
        dwa_ref[0] += _dot(cbf, dza_b, TN)
        dwx_ref[0] += _dot(cbf, dzx_b, TN)
        lam = lam_ref[...]
        rows = [jnp.sum(dconv * us[k], axis=0, keepdims=True) for k in range(CONV_W)]
        rows += [jnp.sum(dconv, axis=0, keepdims=True), jnp.sum(dza, axis=0, keepdims=True),
                 jnp.sum(dzx, axis=0, keepdims=True), dsp * (-_sigmoid(-lam))]
        upd = jnp.zeros((8, RNN_BLOCK), F32)
        for j, rv in enumerate(rows):
            upd = upd + jnp.where(row8 == j, rv, 0.0)
        dvec_ref[...] += upd
        tail8 = c_dconv[...]
        du = dconv * cw_ref[0:1, :]
        for k in range(1, CONV_W):
            du = du + _shift_up(dconv, k, tail8, row, T) * cw_ref[k:k + 1, :]
        du_ref[...] = du.astype(BF)
        c_dhh[...] = jnp.broadcast_to(dhh[0:1, :], c_dhh.shape)
        c_a[...] = jnp.broadcast_to(a[0:1, :], c_a.shape)
        c_dconv[...] = dconv[:8]

    tile, halo, vec, cwspec, wblk = _rglru_specs(T, nt, True)
    acc8 = pl.BlockSpec((8, RNN_BLOCK), lambda n, t: (0, n))
    return pl.pallas_call(
        body, name="rglru_bwd", grid=(RNN_BLOCKS, nt),
        in_specs=[tile, halo, tile, tile, halo, tile, cwspec, vec, wblk, vec, wblk, vec, vec],
        out_specs=[tile, tile, wblk, wblk, acc8],
        out_shape=[jax.ShapeDtypeStruct((S, D_RNN), BF), jax.ShapeDtypeStruct((S, D_RNN), BF),
                   jax.ShapeDtypeStruct((RNN_BLOCKS, RNN_BLOCK, RNN_BLOCK), F32),
                   jax.ShapeDtypeStruct((RNN_BLOCKS, RNN_BLOCK, RNN_BLOCK), F32),
                   jax.ShapeDtypeStruct((8, D_RNN), F32)],
        scratch_shapes=[pltpu.VMEM((8, RNN_BLOCK), F32)] * 3,
        compiler_params=_params(("parallel", "arbitrary")),
    )(xr, xr, g, h, h, dy, cw, cb, wa, ba, wx, bx, lam)


def _rel_bucket_map():
    qi = np.arange(WINDOW)[:, None]
    kj = np.arange(2 * WINDOW)[None, :]
    dist = jnp.asarray(qi + WINDOW - kj, jnp.int32)
    n = jnp.maximum(dist, 0)
    max_exact = REL_BUCKETS // 2
    ratio = jnp.log(jnp.maximum(n, 1).astype(F32) / max_exact) / math.log(REL_MAX_DIST / max_exact)
    large = jnp.minimum(max_exact + (ratio * (REL_BUCKETS - max_exact)).astype(jnp.int32), REL_BUCKETS - 1)
    bucket = jnp.where(n < max_exact, n, large).astype(jnp.int32)
    j = np.arange(WINDOW)[None, :]
    return jnp.where(jnp.asarray(j > qi), bucket[:, :WINDOW], bucket[:, WINDOW:])


def _swa_common(n, kv_ref, bucket_ref, relb_ref, bias_scr):
    @pl.when(n == 0)
    def _():
        bk = bucket_ref[...]
        for h in range(SWA_HEADS):
            acc = jnp.zeros((WINDOW, WINDOW), F32)
            for b in range(REL_BUCKETS):
                acc = acc + jnp.where(bk == b, relb_ref[b, h], 0.0)
            bias_scr[h] = acc

    prev0 = pl.multiple_of(jnp.maximum(n - 1, 0) * WINDOW, WINDOW)
    cur0 = pl.multiple_of(n * WINDOW, WINDOW)
    kk = jnp.concatenate([kv_ref[pl.ds(prev0, WINDOW), :], kv_ref[pl.ds(cur0, WINDOW), :]], axis=0).astype(F32)
    rowi = lax.broadcasted_iota(jnp.int32, (WINDOW, WINDOW), 0)
    col = lax.broadcasted_iota(jnp.int32, (WINDOW, WINDOW), 1)
    from_prev = col > rowi
    return kk, from_prev, prev0, cur0


def _fold(full, from_prev):
    return jnp.where(from_prev, full[:, :WINDOW], full[:, WINDOW:])


def _unfold(sq, from_prev):
    return jnp.concatenate([jnp.where(from_prev, sq, 0.0), jnp.where(from_prev, 0.0, sq)], axis=1)


def _half_pair(part, kvh):
    lo = lax.broadcasted_iota(jnp.int32, part.shape, 1) < SWA_HD
    if kvh == 0:
        pa = jnp.where(lo, part, 0.0)
        pb = pltpu.roll(pa, SWA_HD, 1)
    else:
        pb = jnp.where(lo, 0.0, part)
        pa = pltpu.roll(pb, SWA_HD, 1)
    return pa.astype(BF), pb.astype(BF)


ALL_HEADS = SWA_HEADS * WINDOW


def _sink_column(sinks):
    return jnp.repeat(sinks.reshape(SWA_HEADS), WINDOW).reshape(ALL_HEADS, 1)


def _swa_operands(kk):
    return [(_half_pair(kk[:, :128], kvh), _half_pair(kk[:, 128:], kvh)) for kvh in range(SWA_KV_HEADS)]


def _swa_probs(n, q_ref, ops, bias_scr, sinkc_ref, from_prev):
    lgs = []
    for kvh in range(SWA_KV_HEADS):
        (ka, kb), _ = ops[kvh]
        for p in range(4):
            q2 = q_ref[:, kvh * 512 + p * 128:kvh * 512 + p * 128 + 128]
            lgs += [_fold(_dot(q2, ka, NT), from_prev), _fold(_dot(q2, kb, NT), from_prev)]
    lg = jnp.concatenate(lgs, axis=0) * (SWA_HD ** -0.5) + bias_scr[...].reshape(ALL_HEADS, WINDOW)
    rowi = jnp.bitwise_and(lax.broadcasted_iota(jnp.int32, (ALL_HEADS, WINDOW), 0), WINDOW - 1)
    col = lax.broadcasted_iota(jnp.int32, (ALL_HEADS, WINDOW), 1)
    no_prev = jnp.where(n > 0, 0, 4 * WINDOW)
    lg = jnp.where(jnp.logical_or(col <= rowi, col > rowi + no_prev), lg, NEG_INF)
    sink = sinkc_ref[...]
    m = jnp.maximum(jnp.max(lg, axis=-1, keepdims=True), sink)
    e = jnp.exp(lg - m)
    es = jnp.exp(sink - m)
    den = jnp.sum(e, axis=-1, keepdims=True) + es
    return e / den, es / den


def _swa_fwd(q, kv, g, bucket, rel_bias, sink_col):
    S = q.shape[0]
    nb = S // WINDOW

    def body(q_ref, kv_ref, g_ref, bucket_ref, relb_ref, sinkc_ref, o_ref, y_ref, bias_scr):
        n = pl.program_id(0)
        kk, from_prev, _, _ = _swa_common(n, kv_ref, bucket_ref, relb_ref, bias_scr)
        ops = _swa_operands(kk)
        pr, _ = _swa_probs(n, q_ref, ops, bias_scr, sinkc_ref, from_prev)
        for kvh in range(SWA_KV_HEADS):
            _, (va, vb) = ops[kvh]
            for p in range(4):
                c0 = kvh * 512 + p * 128
                r0 = (kvh * 8 + 2 * p) * WINDOW
                o2 = (_dot(_unfold(pr[r0:r0 + WINDOW], from_prev).astype(BF), va, NN)
                      + _dot(_unfold(pr[r0 + WINDOW:r0 + 2 * WINDOW], from_prev).astype(BF), vb, NN))
                o_ref[:, c0:c0 + 128] = o2
                gv = g_ref[:, c0:c0 + 128]
                y_ref[:, c0:c0 + 128] = (o2 * (gv * _sigmoid(gv))).astype(BF)

    blk = pl.BlockSpec((WINDOW, 1024), lambda n: (n, 0))
    smem = pl.BlockSpec(memory_space=pltpu.SMEM)
    sinkc = pl.BlockSpec((ALL_HEADS, 1), lambda n: (0, 0))
    return pl.pallas_call(
        body, name="swa_fwd", grid=(nb,),
        in_specs=[blk, pl.BlockSpec((S, 256), lambda n: (0, 0)), blk, pl.BlockSpec((WINDOW, WINDOW), lambda n: (0, 0)), smem, sinkc],
        out_specs=[blk, blk],
        out_shape=[jax.ShapeDtypeStruct((S, 1024), F32), jax.ShapeDtypeStruct((S, 1024), BF)],
        scratch_shapes=[pltpu.VMEM((SWA_HEADS, WINDOW, WINDOW), F32)],
        compiler_params=_params(("arbitrary",)),
    )(q, kv, g, bucket, rel_bias, sink_col)


def _swa_bwd(q, kv, g, o, dy, bucket, rel_bias, sink_col):
    S = q.shape[0]
    nb = S // WINDOW

    def body(q_ref, kv_ref, g_ref, o_ref, dy_ref, bucket_ref, relb_ref, sinkc_ref,
             dq_ref, dg_ref, dkv_ref, dsink_ref, drel_ref, bias_scr, dbias_scr, dsink_scr):
        n = pl.program_id(0)

        @pl.when(n == 0)
        def _():
            dbias_scr[...] = jnp.zeros_like(dbias_scr)
            dsink_scr[...] = jnp.zeros_like(dsink_scr)
            dkv_ref[...] = jnp.zeros_like(dkv_ref)

        kk, from_prev, prev0, cur0 = _swa_common(n, kv_ref, bucket_ref, relb_ref, bias_scr)
        ops = _swa_operands(kk)
        pr, ps = _swa_probs(n, q_ref, ops, bias_scr, sinkc_ref, from_prev)
        do2s, dps = [], []
        for kvh in range(SWA_KV_HEADS):
            _, (va, vb) = ops[kvh]
            for p in range(4):
                c0 = kvh * 512 + p * 128
                gv = g_ref[:, c0:c0 + 128]
                sg = _sigmoid(gv)
                dyv = dy_ref[:, c0:c0 + 128]
                dg_ref[:, c0:c0 + 128] = (dyv * o_ref[:, c0:c0 + 128] * (sg * (1.0 + gv * (1.0 - sg)))).astype(BF)
                do2 = (dyv * (gv * sg)).astype(BF)
                do2s.append(do2)
                dps += [_fold(_dot(do2, va, NT), from_prev), _fold(_dot(do2, vb, NT), from_prev)]
        dp = jnp.concatenate(dps, axis=0)
        delta = jnp.sum(pr * dp, axis=-1, keepdims=True)
        ds = pr * (dp - delta)
        dbias_scr[...] += ds.reshape(SWA_HEADS, WINDOW, WINDOW)
        dsink_scr[...] += ps * delta
        dsc = ds * (SWA_HD ** -0.5)
        lo256 = lax.broadcasted_iota(jnp.int32, (2 * WINDOW, 128), 1) < SWA_HD
        dks, dvs = [], []
        for kvh in range(SWA_KV_HEADS):
            (ka, kb), _ = ops[kvh]
            dka = jnp.zeros((2 * WINDOW, 128), F32)
            dkb, dva, dvb = dka, dka, dka
            for p in range(4):
                c0 = kvh * 512 + p * 128
                r0 = (kvh * 8 + 2 * p) * WINDOW
                q2 = q_ref[:, c0:c0 + 128]
                do2 = do2s[kvh * 4 + p]
                ds0 = _unfold(dsc[r0:r0 + WINDOW], from_prev).astype(BF)
                ds1 = _unfold(dsc[r0 + WINDOW:r0 + 2 * WINDOW], from_prev).astype(BF)
                dq_ref[:, c0:c0 + 128] = (_dot(ds0, ka, NN) + _dot(ds1, kb, NN)).astype(BF)
                dka = dka + _dot(ds0, q2, TN)
                dkb = dkb + _dot(ds1, q2, TN)
                dva = dva + _dot(_unfold(pr[r0:r0 + WINDOW], from_prev).astype(BF), do2, TN)
                dvb = dvb + _dot(_unfold(pr[r0 + WINDOW:r0 + 2 * WINDOW], from_prev).astype(BF), do2, TN)
            dks.append(jnp.where(lo256, dka, 0.0) + pltpu.roll(jnp.where(lo256, 0.0, dkb), SWA_HD, 1))
            dvs.append(jnp.where(lo256, dva, 0.0) + pltpu.roll(jnp.where(lo256, 0.0, dvb), SWA_HD, 1))
        dk = dks[0] + pltpu.roll(dks[1], SWA_HD, 1)
        dv = dvs[0] + pltpu.roll(dvs[1], SWA_HD, 1)
        dkv_ref[pl.ds(prev0, WINDOW), 0:128] += dk[:WINDOW]
        dkv_ref[pl.ds(prev0, WINDOW), 128:256] += dv[:WINDOW]
        dkv_ref[pl.ds(cur0, WINDOW), 0:128] += dk[WINDOW:]
        dkv_ref[pl.ds(cur0, WINDOW), 128:256] += dv[WINDOW:]

        @pl.when(n == nb - 1)
        def _():
            dsink_ref[...] = -jnp.sum(dsink_scr[...].reshape(SWA_HEADS, WINDOW, 1), axis=1)
            bk = bucket_ref[...]
            sums = []
            for b in range(REL_BUCKETS):
                sums.append(jnp.sum(jnp.where((bk == b)[None], dbias_scr[...], 0.0), axis=1))
            drel_ref[...] = jnp.sum(jnp.concatenate(sums, axis=0), axis=1, keepdims=True)

    blk = pl.BlockSpec((WINDOW, 1024), lambda n: (n, 0))
    smem = pl.BlockSpec(memory_space=pltpu.SMEM)
    whole = lambda shape: pl.BlockSpec(shape, lambda n: (0, 0))
    return pl.pallas_call(
        body, name="swa_bwd", grid=(nb,),
        in_specs=[blk, whole((S, 256)), blk, blk, blk, whole((WINDOW, WINDOW)), smem, whole((ALL_HEADS, 1))],
        out_specs=[blk, blk, whole((S, 256)), whole((SWA_HEADS, 1)), whole((REL_BUCKETS * SWA_HEADS, 1))],
        out_shape=[jax.ShapeDtypeStruct((S, 1024), BF), jax.ShapeDtypeStruct((S, 1024), BF),
                   jax.ShapeDtypeStruct((S, 256), F32), jax.ShapeDtypeStruct((SWA_HEADS, 1), F32),
                   jax.ShapeDtypeStruct((REL_BUCKETS * SWA_HEADS, 1), F32)],
        scratch_shapes=[pltpu.VMEM((SWA_HEADS, WINDOW, WINDOW), F32), pltpu.VMEM((SWA_HEADS, WINDOW, WINDOW), F32),
                        pltpu.VMEM((ALL_HEADS, 1), F32)],
        compiler_params=_params(("arbitrary",)),
    )(q, kv, g, o, dy, bucket, rel_bias, sink_col)


def _mem_probs(qh, mk):
    lg = _dot(qh, mk, NT) * (MEM_HD ** -0.5)
    e = jnp.exp(lg - jnp.max(lg, axis=-1, keepdims=True))
    return e / jnp.sum(e, axis=-1, keepdims=True)


def _mem_fwd(q, mkv, g):
    S = q.shape[0]
    M = mkv.shape[0]
    tq = 256

    def body(q_ref, mkv_ref, g_ref, o_ref, y_ref):
        for h in range(MEM_HEADS):
            c0 = h * MEM_HD
            pr = _mem_probs(q_ref[:, c0:c0 + MEM_HD], mkv_ref[:, c0:c0 + MEM_HD])
            o = _dot(pr.astype(BF), mkv_ref[:, D_MEM + c0:D_MEM + c0 + MEM_HD], NN)
            o_ref[:, c0:c0 + MEM_HD] = o
            gv = g_ref[:, c0:c0 + MEM_HD]
            y_ref[:, c0:c0 + MEM_HD] = (o * (gv * _sigmoid(gv))).astype(BF)

    blk = pl.BlockSpec((tq, D_MEM), lambda i: (i, 0))
    return pl.pallas_call(
        body, name="mem_fwd", grid=(S // tq,),
        in_specs=[blk, pl.BlockSpec((M, 2 * D_MEM), lambda i: (0, 0)), blk], out_specs=[blk, blk],
        out_shape=[jax.ShapeDtypeStruct((S, D_MEM), F32), jax.ShapeDtypeStruct((S, D_MEM), BF)],
        compiler_params=_params(("parallel",)),
    )(q, mkv, g)


def _mem_bwd(q, mkv, g, o, dy):
    S = q.shape[0]
    M = mkv.shape[0]
    tq = 256

    def body(q_ref, mkv_ref, g_ref, o_ref, dy_ref, dq_ref, dg_ref, dmkv_ref):
        @pl.when(pl.program_id(0) == 0)
        def _():
            dmkv_ref[...] = jnp.zeros_like(dmkv_ref)

        for h in range(MEM_HEADS):
            c0 = h * MEM_HD
            qh = q_ref[:, c0:c0 + MEM_HD]
            mk = mkv_ref[:, c0:c0 + MEM_HD]
            mv = mkv_ref[:, D_MEM + c0:D_MEM + c0 + MEM_HD]
            gv = g_ref[:, c0:c0 + MEM_HD]
            sg = _sigmoid(gv)
            dyv = dy_ref[:, c0:c0 + MEM_HD]
            dg_ref[:, c0:c0 + MEM_HD] = (dyv * o_ref[:, c0:c0 + MEM_HD] * (sg * (1.0 + gv * (1.0 - sg)))).astype(BF)
            do = (dyv * (gv * sg)).astype(BF)
            pr = _mem_probs(qh, mk)
            dp = _dot(do, mv, NT)
            ds = pr * (dp - jnp.sum(pr * dp, axis=-1, keepdims=True))
            dsb = (ds * (MEM_HD ** -0.5)).astype(BF)
            dq_ref[:, c0:c0 + MEM_HD] = _dot(dsb, mk, NN).astype(BF)
            dmkv_ref[:, c0:c0 + MEM_HD] += _dot(dsb, qh, TN)
            dmkv_ref[:, D_MEM + c0:D_MEM + c0 + MEM_HD] += _dot(pr.astype(BF), do, TN)

    blk = pl.BlockSpec((tq, D_MEM), lambda i: (i, 0))
    whole = pl.BlockSpec((M, 2 * D_MEM), lambda i: (0, 0))
    return pl.pallas_call(
        body, name="mem_bwd", grid=(S // tq,),
        in_specs=[blk, whole, blk, blk, blk], out_specs=[blk, blk, whole],
        out_shape=[jax.ShapeDtypeStruct((S, D_MEM), BF), jax.ShapeDtypeStruct((S, D_MEM), BF),
                   jax.ShapeDtypeStruct((M, 2 * D_MEM), F32)],
        compiler_params=_params(("arbitrary",)),
    )(q, mkv, g, o, dy)


MERGE_TN = 512


def _merge_specs(tm):
    ytile = pl.BlockSpec((tm, 1024), lambda i, j: (i, 0))
    wblk = pl.BlockSpec((1024, MERGE_TN), lambda i, j: (0, j))
    gls = [pl.BlockSpec((None, tm, MERGE_TN), (lambda i, j, br=br: (br, i, j))) for br in range(3)]
    otile = pl.BlockSpec((tm, MERGE_TN), lambda i, j: (i, j))
    return ytile, wblk, gls, otile


def _merge_fwd(ys, ws, gl, tm):
    S = gl.shape[1]

    def body(y0, y1, y2, w0, w1, w2, g0, g1, g2, o_ref):
        acc = None
        for y_ref, w_ref, g_ref in ((y0, w0, g0), (y1, w1, g1), (y2, w2, g2)):
            term = _sigmoid(g_ref[...]) * _dot(y_ref[...], w_ref[...], NN)
            acc = term if acc is None else acc + term
        o_ref[...] = acc.astype(BF)

    ytile, wblk, gls, otile = _merge_specs(tm)
    return pl.pallas_call(
        body, name="merge_fwd", grid=(S // tm, D_MODEL // MERGE_TN),
        in_specs=[ytile] * 3 + [wblk] * 3 + gls, out_specs=otile,
        out_shape=jax.ShapeDtypeStruct((S, D_MODEL), BF),
        compiler_params=_params(("parallel", "arbitrary")),
    )(*ys, *ws, gl, gl, gl)


def _merge_bwd(dout, w_out, ys, ws, gl, tm):
    S = gl.shape[1]

    def body(do_ref, wo_ref, y0, y1, y2, w0, w1, w2, g0, g1, g2, dg0, dg1, dg2, dp0, dp1, dp2):
        dm = _dot(do_ref[...], wo_ref[...], NT)
        for y_ref, w_ref, g_ref, dg_ref, dp_ref in ((y0, w0, g0, dg0, dp0), (y1, w1, g1, dg1, dp1), (y2, w2, g2, dg2, dp2)):
            gate = _sigmoid(g_ref[...])
            pv = _dot(y_ref[...], w_ref[...], NN)
            dg_ref[...] = (dm * pv * gate * (1.0 - gate)).astype(BF)
            dp_ref[...] = (dm * gate).astype(BF)

    ytile, wblk, gls, otile = _merge_specs(tm)
    out = jax.ShapeDtypeStruct((S, D_MODEL), BF)
    return pl.pallas_call(
        body, name="merge_bwd", grid=(S // tm, D_MODEL // MERGE_TN),
        in_specs=[pl.BlockSpec((tm, D_MODEL), lambda i, j: (i, 0)), pl.BlockSpec((MERGE_TN, D_MODEL), lambda i, j: (j, 0))]
        + [ytile] * 3 + [wblk] * 3 + gls,
        out_specs=[otile] * 6, out_shape=[out] * 6,
        compiler_params=_params(("parallel", "arbitrary")),
    )(dout, w_out, *ys, *ws, gl, gl, gl)


def _out_loss(merged, w_out, x, target, post_g, tm):
    S = x.shape[0]

    def body(m_ref, w_ref, x_ref, t_ref, g_ref, dout_ref, dy_ref, loss_ref, dpost_ref):
        @pl.when(pl.program_id(0) == 0)
        def _():
            loss_ref[...] = jnp.zeros_like(loss_ref)
            dpost_ref[...] = jnp.zeros_like(dpost_ref)

        out = _dot(m_ref[...], w_ref[...], NN)
        r = lax.rsqrt(jnp.mean(out * out, axis=-1, keepdims=True) + EPS)
        nrm = out * r
        gv = g_ref[...]
        err = (x_ref[...] + nrm * gv) - t_ref[...]
        sq = jnp.sum(jnp.sum(err * err, axis=1, keepdims=True), axis=0, keepdims=True)
        loss_ref[...] += sq * (0.5 / D_MODEL)
        dy = err * (1.0 / D_MODEL)
        dy_ref[...] = dy
        dpost_ref[...] += jnp.sum(dy * nrm, axis=0, keepdims=True)
        dn = dy * gv
        dout_ref[...] = (r * (dn - nrm * jnp.mean(dn * nrm, axis=-1, keepdims=True))).astype(BF)

    row = pl.BlockSpec((tm, D_MODEL), lambda i: (i, 0))
    return pl.pallas_call(
        body, name="out_loss", grid=(S // tm,),
        in_specs=[row, pl.BlockSpec((D_MODEL, D_MODEL), lambda i: (0, 0)), row, row, pl.BlockSpec((1, D_MODEL), lambda i: (0, 0))],
        out_specs=[row, row, pl.BlockSpec((8, 128), lambda i: (0, 0)), pl.BlockSpec((1, D_MODEL), lambda i: (0, 0))],
        out_shape=[jax.ShapeDtypeStruct((S, D_MODEL), BF), jax.ShapeDtypeStruct((S, D_MODEL), F32),
                   jax.ShapeDtypeStruct((8, 128), F32), jax.ShapeDtypeStruct((1, D_MODEL), F32)],
        compiler_params=_params(("arbitrary",)),
    )(merged, w_out, x, target, post_g)


def _dh_dx(dproj, w_in, x, dy, pre_g, tm):
    S = x.shape[0]
    nk, tk = dproj.shape[0], dproj.shape[2]

    def body(dp_ref, w_ref, x_ref, dy_ref, g_ref, dx_ref, dpre_ref, acc_ref):
        i, k = pl.program_id(0), pl.program_id(1)

        @pl.when(jnp.logical_and(i == 0, k == 0))
        def _():
            dpre_ref[...] = jnp.zeros_like(dpre_ref)

        @pl.when(k == 0)
        def _():
            acc_ref[...] = jnp.zeros_like(acc_ref)

        acc_ref[...] += _dot(dp_ref[...], w_ref[...], NN)

        @pl.when(k == nk - 1)
        def _():
            dh = acc_ref[...]
            xv = x_ref[...]
            r = lax.rsqrt(jnp.mean(xv * xv, axis=-1, keepdims=True) + EPS)
            nrm = xv * r
            dpre_ref[...] += jnp.sum(dh * nrm, axis=0, keepdims=True)
            dn = dh * g_ref[...]
            dx_ref[...] = r * (dn - nrm * jnp.mean(dn * nrm, axis=-1, keepdims=True)) + dy_ref[...]

    row = pl.BlockSpec((tm, D_MODEL), lambda i, k: (i, 0))
    vec = pl.BlockSpec((1, D_MODEL), lambda i, k: (0, 0))
    return pl.pallas_call(
        body, name="dh_dx", grid=(S // tm, nk),
        in_specs=[pl.BlockSpec((None, tm, tk), lambda i, k: (k, i, 0)), pl.BlockSpec((tk, D_MODEL), lambda i, k: (k, 0)), row, row, vec],
        out_specs=[row, vec],
        out_shape=[jax.ShapeDtypeStruct((S, D_MODEL), F32), jax.ShapeDtypeStruct((1, D_MODEL), F32)],
        scratch_shapes=[pltpu.VMEM((tm, D_MODEL), F32)],
        compiler_params=_params(("arbitrary", "arbitrary"), 56),
    )(dproj, w_in, x, dy, pre_g)


def _sum_parts(parts, name):
    P, R, C = parts.shape
    tr = max(t for t in range(8, 513, 8) if R % t == 0)

    def body(p_ref, o_ref):
        acc = p_ref[0]
        for j in range(1, P):
            acc = acc + p_ref[j]
        o_ref[...] = acc

    return pl.pallas_call(
        body, name=name, grid=(R // tr,),
        in_specs=[pl.BlockSpec((P, tr, C), lambda i: (0, i, 0))], out_specs=pl.BlockSpec((tr, C), lambda i: (i, 0)),
        out_shape=jax.ShapeDtypeStruct((R, C), F32), compiler_params=_params(("parallel",)),
    )(parts)


def _adamw(parts, w, m, v, name):
    groups = list(parts) if isinstance(parts, (list, tuple)) else [parts]
    P, rows, C = groups[0].shape
    R = rows * len(groups)
    tr = 128 if rows % 128 == 0 else rows
    per = rows // tr
    c1 = 1.0 - ADAM_B1 ** ADAM_STEP
    c2 = 1.0 - ADAM_B2 ** ADAM_STEP

    def body(*refs):
        p_refs = refs[:len(groups)]
        w_ref, m_ref, v_ref, g_ref, d_ref, nm_ref, nv_ref = refs[len(groups):]
        g = None
        for q, p_ref in enumerate(p_refs):
            gq = p_ref[0].astype(F32)
            for j in range(1, P):
                gq = gq + p_ref[j].astype(F32)
            g = gq if g is None else jnp.where(pl.program_id(0) // per == q, gq, g)
        nm = ADAM_B1 * m_ref[...] + (1.0 - ADAM_B1) * g
        nv = ADAM_B2 * v_ref[...] + (1.0 - ADAM_B2) * (g * g)
        g_ref[...] = g
        nm_ref[...] = nm
        nv_ref[...] = nv
        d_ref[...] = -ADAM_LR * ((nm / c1) / (jnp.sqrt(nv / c2) + ADAM_EPS) + ADAM_WD * w_ref[...])

    tile = pl.BlockSpec((tr, C), lambda i: (i, 0))
    out = jax.ShapeDtypeStruct((R, C), F32)
    return pl.pallas_call(
        body, name=name, grid=(R // tr,),
        in_specs=[pl.BlockSpec((P, tr, C), (lambda i, q=q: (0, jnp.clip(i - q * per, 0, per - 1), 0))) for q in range(len(groups))]
        + [tile, tile, tile], out_specs=[tile] * 4, out_shape=[out] * 4,
        compiler_params=_params(("parallel",)),
    )(*groups, w, m, v)


def _forward_a(x, mem, pre_g, mem_g, w_in, conv_w, conv_b, w_a, b_a, w_x, b_x, lam, sinks, rel_bias):
    S = x.shape[0]
    st = dict(T=min(512, S // 2), tm=min(512, S), bucket=_rel_bucket_map())
    st["h"] = _rms_fwd(x, pre_g, "pre_norm")
    st["memn"] = _rms_fwd(mem, mem_g, "mem_norm")
    seg = {}
    for name, c0, width, dt in SEGMENTS:
        seg[name] = _matmul(st["h"], w_in, "nt", S, width, D_MODEL, S, SEG_TILE, D_MODEL, dt, "proj_" + name, b_noff=c0 // SEG_TILE,
                            out_blocked="third" if name == "gl" else None)
    st["seg"] = seg
    st["h_rg"], st["y_rg"] = _rglru_fwd(seg["xr"], seg["g_rg"], conv_w, conv_b, w_a, b_a, w_x, b_x, lam, st["T"])
    st["o_swa"], st["y_swa"] = _swa_fwd(seg["q_s"], seg["kv"], seg["g_swa"], st["bucket"], rel_bias, _sink_column(sinks))
    return st


def _forward_b(st, x, target, post_g, w_memkv, wbr, w_out):
    S = x.shape[0]
    M = st["memn"].shape[0]
    seg = st["seg"]
    st["mkv"] = _matmul(st["memn"], w_memkv, "nn", M, 2 * D_MEM, D_MODEL, M, 512, D_MODEL, BF, "mem_kv")
    st["o_mem"], st["y_mem"] = _mem_fwd(seg["q_m"], st["mkv"], seg["g_mem"])
    st["ys"] = (st["y_rg"], st["y_swa"], st["y_mem"])
    st["merged"] = _merge_fwd(st["ys"], wbr, seg["gl"], st["tm"])
    st["dout"], st["dy"], st["loss"], st["dpost"] = _out_loss(st["merged"], w_out, x, target, post_g, min(256, S))
    return st


def _backward_a(st, mem, w_memkv, wbr, w_out, conv_w, conv_b, w_a, b_a, w_x, b_x, lam):
    S = st["h"].shape[0]
    M = mem.shape[0]
    seg, ys, tm = st["seg"], st["ys"], st["tm"]
    st["dw_out"] = _matmul(st["merged"], st["dout"], "tn", D_MODEL, D_MODEL, S, 256, D_MODEL, S, BF, "dw_out", out_blocked="row")
    dgl0, dgl1, dgl2, dp0, dp1, dp2 = _merge_bwd(st["dout"], w_out, ys, wbr, seg["gl"], tm)
    st["dgl"] = (dgl0, dgl1, dgl2)
    dys, dwbr = [], []
    for i, dp in enumerate((dp0, dp1, dp2)):
        dys.append(_matmul(dp, wbr[i], "nt", S, 1024, D_MODEL, tm, 1024, D_MODEL, F32, "dy_br%d" % i))
        dwbr.append(_matmul(ys[i], dp, "tn", 1024, D_MODEL, S, 1024, 256, S, BF, "dw_br%d" % i, out_blocked="col"))
    st["dys"], st["dwbr"] = dys, dwbr
    st["dq_m"], st["dg_mem"], dmkv = _mem_bwd(seg["q_m"], st["mkv"], seg["g_mem"], st["o_mem"], dys[2])
    dmkv_b = dmkv.astype(BF)
    st["dw_memkv"] = _matmul(st["memn"], dmkv_b, "tn", D_MODEL, 2 * D_MEM, M, 256, 2 * D_MEM, M, BF, "dw_memkv", out_blocked="row")
    dmemn = _matmul(dmkv_b, w_memkv, "nt", M, D_MODEL, 2 * D_MEM, M, 512, 2 * D_MEM, F32, "dmemn")
    st["dmem_g"] = _rms_gain_grad(dmemn, mem, "dmem_gain")
    st["dxr"], st["dg_rg"], st["dw_a"], st["dw_x"], st["dvec"] = _rglru_bwd(
        seg["xr"], seg["g_rg"], st["h_rg"], dys[0], conv_w, conv_b, w_a, b_a, w_x, b_x, lam, st["T"])
    return st


def _backward_b(st, rel_bias, sinks):
    seg = st["seg"]
    dq_s, dg_swa, dkv, dsinks, drel = _swa_bwd(seg["q_s"], seg["kv"], seg["g_swa"], st["o_swa"], st["dys"][1],
                                               st["bucket"], rel_bias, _sink_column(sinks))
    st["dsinks"], st["drel"] = dsinks.reshape(1, SWA_HEADS), drel.reshape(REL_BUCKETS, SWA_HEADS)
    dproj = jnp.concatenate([st["dxr"], st["dg_rg"], dq_s, dkv.astype(BF), dg_swa, st["dq_m"], st["dg_mem"], *st["dgl"]], axis=1)
    st["dproj"] = jnp.transpose(dproj.reshape(dproj.shape[0], N_DEV, D_IN // N_DEV), (1, 0, 2))
    return st


def _dw_in_half(st, half, dep=None):
    S = st["h"].shape[0]
    return _matmul(st["h"], st["dproj"], "tn", D_MODEL // 2, D_IN, S, 512, D_IN // N_DEV, S, BF, "dw_in%d" % half, a_moff=2 * half,
                   b_blocked=True, out_blocked="col", dep=dep)


def _owner_blocks(a):
    return jnp.swapaxes(a.reshape((4, 2) + a.shape[1:]), 0, 1)


def _local_step(x, mem, target, pre_g, post_g, mem_g, w_in, conv_w, conv_b, w_a, b_a, w_x, b_x, lam, sinks, rel_bias,
                w_memkv, wbr, w_out):
    st = _forward_a(x, mem, pre_g, mem_g, w_in, conv_w, conv_b, w_a, b_a, w_x, b_x, lam, sinks, rel_bias)
    st = _forward_b(st, x, target, post_g, w_memkv, wbr, w_out)
    st = _backward_a(st, mem, w_memkv, wbr, w_out, conv_w, conv_b, w_a, b_a, w_x, b_x, lam)
    st = _backward_b(st, rel_bias, sinks)
    st["dw_in"] = [_dw_in_half(st, 0), _dw_in_half(st, 1)]
    st["grad_x"], st["dpre"] = _dh_dx(st["dproj"], w_in, x, st["dy"], pre_g, st["tm"])
    return st


def _pad_rows(a, rows):
    a = a.reshape(-1, 128) if a.shape[-1] % 128 == 0 else jnp.pad(a, ((0, 0), (0, 128 - a.shape[-1])))
    return jnp.pad(a, ((0, rows - a.shape[0]), (0, 0))) if a.shape[0] < rows else a


def kernel(x, mem, pre_norm_g, post_norm_g, mem_norm_g, w_in, conv_w, conv_b, w_rg_a, b_rg_a, w_rg_x, b_rg_x, lru_lambda, swa_sinks, rel_bias, w_mem_kv, w_br_rg, w_br_swa, w_br_mem, w_out, loss_target, m_pre_norm_g, m_post_norm_g, m_mem_norm_g, m_w_in, m_conv_w, m_conv_b, m_w_rg_a, m_b_rg_a, m_w_rg_x, m_b_rg_x, m_lru_lambda, m_swa_sinks, m_rel_bias, m_w_mem_kv, m_w_br_rg, m_w_br_swa, m_w_br_mem, m_w_out, v_pre_norm_g, v_post_norm_g, v_mem_norm_g, v_w_in, v_conv_w, v_conv_b, v_w_rg_a, v_b_rg_a, v_w_rg_x, v_b_rg_x, v_lru_lambda, v_swa_sinks, v_rel_bias, v_w_mem_kv, v_w_br_rg, v_w_br_swa, v_w_br_mem, v_w_out):
    cx, cy, cc = lax.axis_index("x"), lax.axis_index("y"), lax.axis_index("c")
    me = 4 * cx + 2 * cy + cc
    chip = 2 * cx + cy
    core = jnp.reshape(cc, (1,)).astype(jnp.int32)
    x0, mem0 = x[0], mem[0]
    w_a_b, w_x_b = w_rg_a[0].astype(BF), w_rg_x[0].astype(BF)

    def landing(own, slot, slots, kind="lead"):
        if kind == "cols":
            return lax.dynamic_update_slice(lax.empty((own.shape[0], slots * own.shape[1]), own.dtype), own, (0, slot * own.shape[1]))
        return lax.dynamic_update_slice(lax.empty((slots,) + own.shape, own.dtype), own[None], (slot,) + (0,) * own.ndim)

    def swap_start(parts, tag):
        return _exchange_start(parts, [lax.empty(p.shape[1:], p.dtype) for p in parts], _plan_swap(len(parts)), "swap_%s_start" % tag)

    def scatter_start(swap, after, tag):
        s_send, s_recv, parts, got, _ = swap
        got = _exchange_wait(s_send, s_recv, parts, got, _plan_swap(len(parts)), after, "swap_%s_wait" % tag)
        sums = [_pair_sum(p, g, core, "scatter_%s_sum%d" % (tag, i)) for i, (p, g) in enumerate(zip(parts, got))]
        lands = [landing(lax.dynamic_index_in_dim(s, chip, 0, keepdims=False), chip, 4) for s in sums]
        return _exchange_start(sums, lands, _plan_scatter(len(sums)), "scatter_%s_start" % tag)

    def zero_after(a):
        return jnp.minimum(jnp.abs(a.reshape(-1)[0].astype(F32)), 0.0)

    g_in, g_cw = _all_gather_relayed([jnp.transpose(w_in[0]).astype(BF), conv_w[0]], [True, False], "gather_w_in")
    w_in_f = g_in.reshape(D_IN, D_MODEL)
    conv_w_f = jnp.transpose(g_cw, (1, 0, 2)).reshape(CONV_W, D_RNN)

    after_first = zero_after(g_cw).astype(BF)
    rest = [w.astype(BF) + after_first for w in (w_mem_kv[0], w_br_rg[0], w_br_swa[0], w_br_mem[0], w_out[0])]
    kinds = ["lead", "cols", "cols", "cols", "lead"]
    plan_g = _plan_gather(kinds)
    g_send, g_recv, g_src, g_land, g_token = _exchange_start(rest, [landing(w, me, N_DEV, kd) for w, kd in zip(rest, kinds)], plan_g,
                                                             "gather_rest_start")
    st = _forward_a(x0, mem0, pre_norm_g + g_token[0:1, 0:1], mem_norm_g, w_in_f, conv_w_f, conv_b, w_a_b, b_rg_a, w_x_b, b_rg_x,
                    lru_lambda, swa_sinks, rel_bias)
    g_land = _exchange_wait(g_send, g_recv, g_src, g_land, plan_g, st["y_swa"], "gather_rest_wait")
    g_land = _forward_to_sibling(g_land, kinds, "gather_rest_forward")
    w_memkv_f = g_land[0].reshape(D_MODEL, 2 * D_MEM)
    wbr = (g_land[1], g_land[2], g_land[3])
    w_out_f = g_land[4].reshape(D_MODEL, D_MODEL)

    st = _forward_b(st, x0, loss_target[0], post_norm_g, w_memkv_f, wbr, w_out_f)
    st = _backward_a(st, mem0, w_memkv_f, wbr, w_out_f, conv_w_f, conv_b, w_a_b, b_rg_a, w_x_b, b_rg_x, lru_lambda)
    parts_a = [st["dw_memkv"], st["dwbr"][0], st["dwbr"][1], st["dwbr"][2], st["dw_out"],
               _owner_blocks(st["dw_a"]), _owner_blocks(st["dw_x"])]
    plan_a = _plan_scatter(len(parts_a))
    swap_a = swap_start(parts_a, "a")

    st = _backward_b(st, rel_bias, swa_sinks + swap_a[4][0:1, 0:1])
    a_send, a_recv, a_src, a_land, a_token = scatter_start(swap_a, st["dproj"], "a")
    plan_b = _plan_scatter(1)

    def dw_in_parts(half, dep):
        dwh = _dw_in_half(st, half, dep)
        return dwh, [dwh]

    dw0, parts_b0 = dw_in_parts(0, a_token)
    swap_b0 = swap_start(parts_b0, "b0")
    a_land = _exchange_wait(a_send, a_recv, a_src, a_land, plan_a, swap_b0[4], "scatter_a_wait")
    g_wa_blk = _sum_parts(a_land[5], "sum_w_rg_a")
    g_wx_blk = _sum_parts(a_land[6], "sum_w_rg_x")
    big = [None] * 6

    def adamw_big(j, wt, mt, vt):
        big[j] = [a[None] for a in _adamw(a_land[j - 1], wt[0], mt[0], vt[0], "adamw_big%d" % j)]

    adamw_big(1, w_mem_kv, m_w_mem_kv, v_w_mem_kv)
    adamw_big(2, w_br_rg, m_w_br_rg, v_w_br_rg)
    adamw_big(3, w_br_swa, m_w_br_swa, v_w_br_swa)
    halves = [scatter_start(swap_b0, big[3][1], "b0")]
    dw1, parts_b1 = dw_in_parts(1, halves[0][4])
    swap_b1 = swap_start(parts_b1, "b1")
    adamw_big(4, w_br_mem + swap_b1[4][0:1, 0:1], m_w_br_mem, v_w_br_mem)
    adamw_big(5, w_out, m_w_out, v_w_out)
    halves.append(scatter_start(swap_b1, big[5][1], "b1"))
    grad_x, dpre = _dh_dx(st["dproj"], w_in_f, x0, st["dy"], pre_norm_g + halves[1][4][0:1, 0:1], st["tm"])
    after, b_lands = grad_x, []
    for half, (b_send, b_recv, b_src, b_land, _) in enumerate(halves):
        b_lands.append(_exchange_wait(b_send, b_recv, b_src, b_land, plan_b, after, "scatter_b%d_wait" % half)[0])
        after = b_lands[-1]
    big[0] = [a[None] for a in _adamw(b_lands, w_in[0], m_w_in[0], v_w_in[0], "adamw_big0")]
    links_free = jnp.minimum(jnp.abs(big[0][0][0, 0, 0]), 0.0)

    pack = jnp.concatenate([dpre.reshape(16, 128), st["dpost"].reshape(16, 128), st["dmem_g"].reshape(16, 128),
                            st["dvec"].reshape(64, 128), _pad_rows(st["dsinks"], 8), _pad_rows(st["drel"], 32), g_wa_blk, g_wx_blk], axis=0) + links_free
    gathered = _all_gather([pack], "gather_small")[0]
    gs = _sum_parts(gathered, "sum_small")
    g_pre, g_post, g_memg = gs[0:16].reshape(1, D_MODEL), gs[16:32].reshape(1, D_MODEL), gs[32:48].reshape(1, D_MODEL)
    gvec = gs[48:112].reshape(8, D_RNN)
    g_conv_w = lax.dynamic_slice(gvec[0:CONV_W], (0, me * RNN_BLOCK), (CONV_W, RNN_BLOCK))
    g_conv_b, g_b_a, g_b_x, g_lam = gvec[4:5], gvec[5:6], gvec[6:7], gvec[7:8]
    g_sinks = gs[112:113, :SWA_HEADS]
    g_rel = gs[120:152, :SWA_HEADS]
    g_w_a = gathered[:, 152:280]
    g_w_x = gathered[:, 280:408]

    def packed(ts):
        pre, post, memg, cb, ba, bx, lm, wa, wx, sk, rel, cw = ts
        return jnp.concatenate([pre.reshape(16, 128), post.reshape(16, 128), memg.reshape(16, 128), cb.reshape(8, 128),
                                ba.reshape(8, 128), bx.reshape(8, 128), lm.reshape(8, 128), wa.reshape(1024, 128),
                                wx.reshape(1024, 128), _pad_rows(sk.reshape(1, SWA_HEADS), 8), _pad_rows(rel, 32),
                                _pad_rows(cw.reshape(CONV_W, RNN_BLOCK), 8)], axis=0)

    def unpacked(a):
        return (a[0:16].reshape(1, D_MODEL), a[16:32].reshape(1, D_MODEL), a[32:48].reshape(1, D_MODEL), a[48:56].reshape(1, D_RNN),
                a[56:64].reshape(1, D_RNN), a[64:72].reshape(1, D_RNN), a[72:80].reshape(1, D_RNN),
                a[80:1104].reshape(1, RNN_BLOCKS, RNN_BLOCK, RNN_BLOCK), a[1104:2128].reshape(1, RNN_BLOCKS, RNN_BLOCK, RNN_BLOCK),
                a[2128:2129, :SWA_HEADS], a[2136:2168, :SWA_HEADS], a[2168:2172].reshape(1, CONV_W, RNN_BLOCK))

    g_small = (g_pre, g_post, g_memg, g_conv_b, g_b_a, g_b_x, g_lam, g_w_a, g_w_x, g_sinks, g_rel, g_conv_w)
    w_small = (pre_norm_g, post_norm_g, mem_norm_g, conv_b, b_rg_a, b_rg_x, lru_lambda, w_rg_a, w_rg_x, swa_sinks, rel_bias, conv_w)
    m_small = (m_pre_norm_g, m_post_norm_g, m_mem_norm_g, m_conv_b, m_b_rg_a, m_b_rg_x, m_lru_lambda, m_w_rg_a, m_w_rg_x, m_swa_sinks, m_rel_bias, m_conv_w)
    v_small = (v_pre_norm_g, v_post_norm_g, v_mem_norm_g, v_conv_b, v_b_rg_a, v_b_rg_x, v_lru_lambda, v_w_rg_a, v_w_rg_x, v_swa_sinks, v_rel_bias, v_conv_w)
    sm = [unpacked(a) for a in _adamw(packed(g_small)[None], packed(w_small), packed(m_small), packed(v_small), "adamw_small")]


    loss_total = lax.psum(st["loss"][0, 0], AXES)

    def leaves(k):
        s = sm[k]
        return [s[0], s[1], s[2], big[0][k], s[11], s[3], s[7], s[4], s[8], s[5], s[6], s[9], s[10],
                big[1][k], big[2][k], big[3][k], big[4][k], big[5][k]]

    return (loss_total, grad_x[None], *leaves(0), *leaves(1), *leaves(2), *leaves(3))
```

```python
import math

import jax
import jax.numpy as jnp
import numpy as np
from jax import lax
from jax.experimental import pallas as pl
from jax.experimental.pallas import tpu as pltpu

F32, BF = jnp.float32, jnp.bfloat16
MESH = pl.DeviceIdType.MESH
AXES = ("x", "y", "c")
N_DEV = 8

D_MODEL = 2048
D_RNN = 1024
RNN_BLOCKS = 8
RNN_BLOCK = 128
CONV_W = 4
LRU_C = 8.0
SWA_HEADS = 16
SWA_KV_HEADS = 2
SWA_HD = 64
WINDOW = 128
MEM_HEADS = 4
MEM_HD = 256
D_MEM = 1024
REL_BUCKETS = 32
REL_MAX_DIST = 128
EPS = 1e-6
NEG_INF = -1e30
D_IN = 12544
SEGMENTS = (("xr", 0, 1024, F32), ("g_rg", 1024, 1024, F32), ("q_s", 2048, 1024, BF), ("kv", 3072, 256, BF),
            ("g_swa", 3328, 1024, F32), ("q_m", 4352, 1024, BF), ("g_mem", 5376, 1024, F32), ("gl", 6400, 6144, F32))
SEG_TILE = 256

ADAM_LR, ADAM_B1, ADAM_B2, ADAM_EPS, ADAM_WD, ADAM_STEP = 0.001, 0.9, 0.999, 1e-08, 0.01, 10

NN = (((1,), (0,)), ((), ()))
NT = (((1,), (1,)), ((), ()))
TN = (((0,), (0,)), ((), ()))
MIB = 2 ** 20


def _dot(a, b, dn):
    return lax.dot_general(a, b, dn, preferred_element_type=F32)


def _params(sem, vmem_mib=48):
    return pltpu.CompilerParams(dimension_semantics=sem, vmem_limit_bytes=vmem_mib * MIB)


def _sigmoid(z):
    return 1.0 / (1.0 + jnp.exp(-z))


def _softplus(z):
    return jnp.maximum(z, 0.0) + jnp.log(1.0 + jnp.exp(-jnp.abs(z)))


def _expm1(z):
    p = z * (1.0 + z * (0.5 + z * (1.0 / 6 + z * (1.0 / 24 + z * (1.0 / 120 + z * (1.0 / 720 + z * (1.0 / 5040 + z / 40320)))))))
    return jnp.where(jnp.abs(z) < 0.3, p, jnp.exp(z) - 1.0)


def _flat(p):
    return 4 * p[0] + 2 * p[1] + p[2]


def _all_gather(arrs, name):
    n = len(arrs)

    def body(*refs):
        ins, outs = refs[:n], refs[n:2 * n]
        send_sems, recv_sems, local_sems = refs[2 * n:]
        x, y, c = lax.axis_index("x"), lax.axis_index("y"), lax.axis_index("c")
        me, sibling = (x, y, c), (x, y, 1 - c)
        chips = [(1 - x, y), (x, 1 - y), (1 - x, 1 - y)]

        def copy(a, k, block, to, src=None):
            dst = outs[a].at[_flat(block)]
            return pltpu.make_async_remote_copy(src_ref=dst if src is None else src, dst_ref=dst,
                                                send_sem=send_sems.at[a * 7 + k], recv_sem=recv_sems.at[a * 7 + k],
                                                device_id=to, device_id_type=MESH)

        mine = [pltpu.make_async_copy(ins[a], outs[a].at[_flat(me)], local_sems.at[a]) for a in range(n)]
        for cp in mine:
            cp.start()
        first = []
        for a in range(n):
            first += [copy(a, 1 + j, me, (*chip, c), src=ins[a]) for j, chip in enumerate(chips)]
            first.append(copy(a, 0, me, sibling, src=ins[a]))
        for cp in first:
            cp.start()
        passed = []
        for j, chip in enumerate(chips):
            for a in range(n):
                copy(a, 1 + j, (*chip, c), me).wait_recv()
                fw = copy(a, 4 + j, (*chip, c), sibling)
                fw.start()
                passed.append(fw)
        for a in range(n):
            copy(a, 0, sibling, me).wait_recv()
            for j, chip in enumerate(chips):
                copy(a, 4 + j, (*chip, 1 - c), me).wait_recv()
        for cp in first + passed:
            cp.wait_send()
        for cp in mine:
            cp.wait()

    any_spec = pl.BlockSpec(memory_space=pl.ANY)
    return pl.pallas_call(
        body, name=name,
        out_shape=[jax.ShapeDtypeStruct((N_DEV,) + a.shape, a.dtype) for a in arrs],
        in_specs=[any_spec] * n, out_specs=[any_spec] * n,
        scratch_shapes=[pltpu.SemaphoreType.DMA((7 * n,)), pltpu.SemaphoreType.DMA((7 * n,)), pltpu.SemaphoreType.DMA((n,))],
    )(*arrs)


def _all_gather_relayed(arrs, relay, name):
    n = len(arrs)
    K = 9

    def body(*refs):
        ins, outs = refs[:n], refs[n:2 * n]
        send_sems, recv_sems, local_sems = refs[2 * n:]
        x, y, c = lax.axis_index("x"), lax.axis_index("y"), lax.axis_index("c")
        me, sib = (x, y, c), (x, y, 1 - c)
        xn, yn, dg = (1 - x, y, c), (x, 1 - y, c), (1 - x, 1 - y, c)

        def other(p):
            return (p[0], p[1], 1 - p[2])

        def rows(a, half):
            h = arrs[a].shape[0] // 2
            return pl.ds(half * h, h)

        def copy(a, k, block, to, half=None, src=None):
            dst = outs[a].at[_flat(block)]
            if half is not None:
                dst = dst.at[rows(a, half)]
            return pltpu.make_async_remote_copy(src_ref=dst if src is None else src, dst_ref=dst,
                                                send_sem=send_sems.at[a * K + k], recv_sem=recv_sems.at[a * K + k],
                                                device_id=to, device_id_type=MESH)

        mine = [pltpu.make_async_copy(ins[a], outs[a].at[_flat(me)], local_sems.at[a]) for a in range(n)]
        for cp in mine:
            cp.start()
        sends = []

        def start(cp):
            cp.start()
            sends.append(cp)

        for a in range(n):
            start(copy(a, 1, me, xn, src=ins[a]))
            start(copy(a, 2, me, yn, src=ins[a]))
            if not relay[a]:
                start(copy(a, 3, me, dg, src=ins[a]))
            start(copy(a, 0, me, sib, src=ins[a]))
        for a in range(n):
            copy(a, 1, xn, me).wait_recv()
            if relay[a]:
                start(copy(a, 3, xn, yn, half=0))
            start(copy(a, 5, xn, sib))
        for a in range(n):
            copy(a, 2, yn, me).wait_recv()
            if relay[a]:
                start(copy(a, 4, yn, xn, half=1))
            start(copy(a, 6, yn, sib))
        for a in range(n):
            if relay[a]:
                copy(a, 3, dg, me, half=0).wait_recv()
                start(copy(a, 7, dg, sib, half=0))
                copy(a, 4, dg, me, half=1).wait_recv()
                start(copy(a, 8, dg, sib, half=1))
            else:
                copy(a, 3, dg, me).wait_recv()
                start(copy(a, 7, dg, sib))
        for a in range(n):
            copy(a, 0, sib, me).wait_recv()
            copy(a, 5, other(xn), me).wait_recv()
            copy(a, 6, other(yn), me).wait_recv()
            if relay[a]:
                copy(a, 7, other(dg), me, half=0).wait_recv()
                copy(a, 8, other(dg), me, half=1).wait_recv()
            else:
                copy(a, 7, other(dg), me).wait_recv()
        for cp in sends:
            cp.wait_send()
        for cp in mine:
            cp.wait()

    any_spec = pl.BlockSpec(memory_space=pl.ANY)
    return pl.pallas_call(
        body, name=name,
        out_shape=[jax.ShapeDtypeStruct((N_DEV,) + a.shape, a.dtype) for a in arrs],
        in_specs=[any_spec] * n, out_specs=[any_spec] * n,
        scratch_shapes=[pltpu.SemaphoreType.DMA((K * n,)), pltpu.SemaphoreType.DMA((K * n,)), pltpu.SemaphoreType.DMA((n,))],
    )(*arrs)


def _chip_peers(x, y):
    return [(1 - x, y), (x, 1 - y), (1 - x, 1 - y)]


def _chip(p):
    return 2 * p[0] + p[1]


def _plan_gather(kinds):
    def plan(x, y, c):
        out = []
        for a, kind in enumerate(kinds):
            for peer in [(x, y, 1 - c)] + [(*ch, c) for ch in _chip_peers(x, y)]:
                out.append((a, None, (kind, _flat((x, y, c))), peer, (kind, _flat(peer))))
        return out
    return plan


def _plan_swap(n):
    def plan(x, y, c):
        return [(a, 1 - c, ("all", 0), (x, y, 1 - c), ("all", 0)) for a in range(n)]
    return plan


def _slot(ref, where):
    kind, k = where
    if kind == "all":
        return ref
    if kind == "lead":
        return ref.at[k]
    return ref.at[:, pl.ds(pl.multiple_of(k * 256, 256), 256)]


def _plan_scatter(n):
    def plan(x, y, c):
        out = []
        for a in range(n):
            for ch in _chip_peers(x, y):
                out.append((a, _chip(ch), ("lead", _chip((x, y))), (*ch, c), ("lead", _chip(ch))))
        return out
    return plan


HBM_SPEC = pl.BlockSpec(memory_space=pltpu.HBM)
SEM_SPEC = pl.BlockSpec(memory_space=pltpu.SEMAPHORE)


def _in_hbm(a):
    return pltpu.with_memory_space_constraint(a, pltpu.HBM)


def _exchange_start(srcs, lands, plan, name):
    n = len(srcs)
    count = len(plan(0, 0, 0))

    def body(*refs):
        src_refs, land_refs = refs[:n], refs[n:2 * n]
        send_sems, recv_sems = refs[2 * n], refs[2 * n + 1]
        token = refs[-1]
        x, y, c = lax.axis_index("x"), lax.axis_index("y"), lax.axis_index("c")
        for k, (a, si, di, peer, _) in enumerate(plan(x, y, c)):
            src = src_refs[a] if si is None else src_refs[a].at[si]
            pltpu.make_async_remote_copy(src_ref=src, dst_ref=_slot(land_refs[a], di), send_sem=send_sems.at[k],
                                         recv_sem=recv_sems.at[k], device_id=peer, device_id_type=MESH).start()
        token[...] = jnp.zeros_like(token)

    out = pl.pallas_call(
        body, name=name,
        out_shape=(pltpu.SemaphoreType.DMA((count,)), pltpu.SemaphoreType.DMA((count,)),
                   *[pltpu.HBM(a.shape, a.dtype) for a in lands], jax.ShapeDtypeStruct((8, 128), F32)),
        in_specs=[HBM_SPEC] * (2 * n),
        out_specs=(SEM_SPEC, SEM_SPEC, *([HBM_SPEC] * n), pl.BlockSpec(memory_space=pltpu.VMEM)),
        input_output_aliases={n + i: 2 + i for i in range(n)},
        compiler_params=pltpu.CompilerParams(has_side_effects=pltpu.SideEffectType.DATAFLOW_SIDE_EFFECTING),
    )(*[_in_hbm(a) for a in srcs], *[_in_hbm(a) for a in lands])
    return out[0], out[1], list(srcs), list(out[2:2 + n]), out[-1]


def _exchange_wait(send_sems, recv_sems, srcs, lands, plan, after, name):
    n = len(srcs)

    def body(*refs):
        src_refs, land_refs = refs[:n], refs[n:2 * n]
        send_sems, recv_sems = refs[2 * n], refs[2 * n + 1]
        x, y, c = lax.axis_index("x"), lax.axis_index("y"), lax.axis_index("c")
        for k, (a, si, _, peer, ri) in enumerate(plan(x, y, c)):
            src = src_refs[a] if si is None else src_refs[a].at[si]
            cp = pltpu.make_async_remote_copy(src_ref=src, dst_ref=_slot(land_refs[a], ri), send_sem=send_sems.at[k],
                                              recv_sem=recv_sems.at[k], device_id=peer, device_id_type=MESH)
            cp.wait_send()
            cp.wait_recv()

    out = pl.pallas_call(
        body, name=name,
        out_shape=tuple(pltpu.HBM(a.shape, a.dtype) for a in lands),
        in_specs=[HBM_SPEC] * (2 * n) + [SEM_SPEC, SEM_SPEC, pl.BlockSpec(memory_space=pl.ANY)],
        out_specs=tuple([HBM_SPEC] * n),
        input_output_aliases={n + i: i for i in range(n)},
        compiler_params=pltpu.CompilerParams(has_side_effects=pltpu.SideEffectType.DATAFLOW_SIDE_EFFECTING),
    )(*[_in_hbm(a) for a in srcs], *lands, send_sems, recv_sems, after)
    return list(out)


def _forward_to_sibling(lands, kinds, name):
    n = len(lands)

    def body(*refs):
        in_refs, out_refs = refs[:n], refs[n:2 * n]
        send_sems, recv_sems = refs[2 * n:]
        x, y, c = lax.axis_index("x"), lax.axis_index("y"), lax.axis_index("c")
        sibling = (x, y, 1 - c)

        def copy(a, j, slot):
            return pltpu.make_async_remote_copy(src_ref=_slot(in_refs[a], (kinds[a], slot)), dst_ref=_slot(out_refs[a], (kinds[a], slot)),
                                                send_sem=send_sems.at[a * 3 + j], recv_sem=recv_sems.at[a * 3 + j],
                                                device_id=sibling, device_id_type=MESH)

        sends = [copy(a, j, _flat((*ch, c))) for a in range(n) for j, ch in enumerate(_chip_peers(x, y))]
        for cp in sends:
            cp.start()
        for a in range(n):
            for j, ch in enumerate(_chip_peers(x, y)):
                copy(a, j, _flat((*ch, 1 - c))).wait_recv()
        for cp in sends:
            cp.wait_send()

    any_spec = pl.BlockSpec(memory_space=pl.ANY)
    return pl.pallas_call(
        body, name=name, out_shape=[jax.ShapeDtypeStruct(a.shape, a.dtype) for a in lands],
        in_specs=[any_spec] * n, out_specs=[any_spec] * n, input_output_aliases={a: a for a in range(n)},
        scratch_shapes=[pltpu.SemaphoreType.DMA((3 * n,)), pltpu.SemaphoreType.DMA((3 * n,))],
    )(*lands)


def _swap_with_sibling(parts, name):
    n = len(parts)

    def body(*refs):
        in_refs, out_refs = refs[:n], refs[n:2 * n]
        send_sems, recv_sems = refs[2 * n:]
        x, y, c = lax.axis_index("x"), lax.axis_index("y"), lax.axis_index("c")
        sends = [pltpu.make_async_remote_copy(src_ref=in_refs[a].at[1 - c], dst_ref=out_refs[a], send_sem=send_sems.at[a],
                                              recv_sem=recv_sems.at[a], device_id=(x, y, 1 - c), device_id_type=MESH)
                 for a in range(n)]
        for cp in sends:
            cp.start()
        for cp in sends:
            cp.wait()

    any_spec = pl.BlockSpec(memory_space=pl.ANY)
    return pl.pallas_call(
        body, name=name, out_shape=[jax.ShapeDtypeStruct(a.shape[1:], a.dtype) for a in parts],
        in_specs=[any_spec] * n, out_specs=[any_spec] * n,
        scratch_shapes=[pltpu.SemaphoreType.DMA((n,)), pltpu.SemaphoreType.DMA((n,))],
    )(*parts)


def _pair_sum(parts, got, core, name):
    _, _, R, C = parts.shape
    tr = 256 if R % 256 == 0 else R

    def body(c_ref, p_ref, g_ref, o_ref):
        o_ref[...] = (p_ref[...].astype(F32) + g_ref[...].astype(F32)).astype(o_ref.dtype)

    return pl.pallas_call(
        body, name=name,
        grid_spec=pltpu.PrefetchScalarGridSpec(
            num_scalar_prefetch=1, grid=(4, R // tr),
            in_specs=[pl.BlockSpec((None, None, tr, C), lambda j, i, c_ref: (c_ref[0], j, i, 0)),
                      pl.BlockSpec((None, tr, C), lambda j, i, c_ref: (j, i, 0))],
            out_specs=pl.BlockSpec((None, tr, C), lambda j, i, c_ref: (j, i, 0))),
        out_shape=jax.ShapeDtypeStruct((4, R, C), parts.dtype),
        compiler_params=_params(("parallel", "parallel")),
    )(core, parts, got)


def _matmul(a, b, mode, M, N, K, tm, tn, tk, out_dtype, name, b_noff=0, a_moff=0, a_koff=0, b_blocked=False, out_blocked=None,
            dep=None, addend=None, vmem_mib=48):
    nm, nn, nk = M // tm, N // tn, K // tk
    if mode == "nn":
        a_spec = pl.BlockSpec((tm, tk), lambda j, i, k: (i, k + a_koff))
        b_spec = pl.BlockSpec((tk, tn), lambda j, i, k: (k, j + b_noff))
        dn = NN
    elif mode == "nt":
        a_spec = pl.BlockSpec((tm, tk), lambda j, i, k: (i, k + a_koff))
        if b_blocked:
            b_spec = pl.BlockSpec((None, tn, tk), lambda j, i, k: (k, j, 0))
        else:
            b_spec = pl.BlockSpec((tn, tk), lambda j, i, k: (j + b_noff, k))
        dn = NT
    else:
        a_spec = pl.BlockSpec((tk, tm), lambda j, i, k: (k, i + a_moff))
        if b_blocked:
            b_spec = pl.BlockSpec((None, tk, tn), lambda j, i, k: (j, k, 0))
        else:
            b_spec = pl.BlockSpec((tk, tn), lambda j, i, k: (k, j + b_noff))
        dn = TN
    if out_blocked == "col":
        out_shape = jax.ShapeDtypeStruct((2, 4, M, tn), out_dtype)
        out_spec = pl.BlockSpec((None, None, tm, tn), lambda j, i, k: (j % 2, j // 2, i, 0))
    elif out_blocked == "row":
        out_shape = jax.ShapeDtypeStruct((2, 4, tm, N), out_dtype)
        out_spec = pl.BlockSpec((None, None, tm, tn), lambda j, i, k: (i % 2, i // 2, 0, j))
    elif out_blocked == "third":
        out_shape = jax.ShapeDtypeStruct((3, M, N // 3), out_dtype)
        out_spec = pl.BlockSpec((None, tm, tn), lambda j, i, k: (j // (nn // 3), i, j % (nn // 3)))
    else:
        out_shape = jax.ShapeDtypeStruct((M, N), out_dtype)
        out_spec = pl.BlockSpec((tm, tn), lambda j, i, k: (i, j))

    n_extra = (addend is not None) + (dep is not None)

    def body(a_ref, b_ref, *rest):
        o_ref, scratch = rest[n_extra], rest[n_extra + 1:]
        if nk == 1:
            prod = _dot(a_ref[...], b_ref[...], dn)
            if addend is not None:
                prod = prod + rest[0][...].astype(F32)
            o_ref[...] = prod.astype(out_dtype)
        else:
            assert addend is None
            acc_ref, = scratch
            k = pl.program_id(2)

            @pl.when(k == 0)
            def _():
                acc_ref[...] = jnp.zeros_like(acc_ref)

            acc_ref[...] += _dot(a_ref[...], b_ref[...], dn)

            @pl.when(k == nk - 1)
            def _():
                o_ref[...] = acc_ref[...].astype(out_dtype)

    return pl.pallas_call(
        body, name=name, grid=(nn, nm, nk),
        in_specs=[a_spec, b_spec] + ([] if addend is None else [out_spec])
        + ([] if dep is None else [pl.BlockSpec((8, 128), lambda j, i, k: (0, 0))]),
        out_specs=out_spec, out_shape=out_shape,
        scratch_shapes=[] if nk == 1 else [pltpu.VMEM((tm, tn), F32)],
        compiler_params=_params(("parallel", "parallel", "arbitrary"), vmem_mib),
    )(a, b, *([] if addend is None else [addend]), *([] if dep is None else [dep]))


def _rms_fwd(x, g, name):
    R, Dm = x.shape
    tr = min(R, 256)

    def body(x_ref, g_ref, h_ref):
        xv = x_ref[...]
        r = lax.rsqrt(jnp.mean(xv * xv, axis=-1, keepdims=True) + EPS)
        h_ref[...] = (xv * r * g_ref[...]).astype(BF)

    return pl.pallas_call(
        body, name=name, grid=(R // tr,),
        in_specs=[pl.BlockSpec((tr, Dm), lambda i: (i, 0)), pl.BlockSpec((1, Dm), lambda i: (0, 0))],
        out_specs=pl.BlockSpec((tr, Dm), lambda i: (i, 0)), out_shape=jax.ShapeDtypeStruct((R, Dm), BF),
        compiler_params=_params(("parallel",)),
    )(x, g)


def _rms_gain_grad(dn, x, name):
    R, Dm = x.shape

    def body(dn_ref, x_ref, o_ref):
        xv = x_ref[...]
        r = lax.rsqrt(jnp.mean(xv * xv, axis=-1, keepdims=True) + EPS)
        o_ref[...] = jnp.sum(dn_ref[...] * xv * r, axis=0, keepdims=True)

    return pl.pallas_call(
        body, name=name, out_shape=jax.ShapeDtypeStruct((1, Dm), F32),
        compiler_params=pltpu.CompilerParams(vmem_limit_bytes=32 * MIB),
    )(dn, x)


def _shift_down(v, k, head8, row, T):
    if k == 0:
        return v
    r = pltpu.roll(v, k, 0)
    hr = pltpu.roll(head8, k, 0)
    top = jnp.where(row[:8] < k, hr, r[:8])
    return jnp.concatenate([top, r[8:]], axis=0)


def _shift_up(v, k, tail8, row, T):
    if k == 0:
        return v
    r = pltpu.roll(v, T - k, 0)
    tr = pltpu.roll(tail8, 8 - k, 0)
    bot = jnp.where(row[:8] >= 8 - k, tr, r[T - 8:])
    return jnp.concatenate([r[:T - 8], bot], axis=0)


def _rglru_gates(u, head8, grow, row, T, cw_ref, cb_ref, wa_ref, ba_ref, wx_ref, bx_ref, lam_ref):
    us = [_shift_down(u, k, head8, row, T) for k in range(CONV_W)]
    acc = us[0] * cw_ref[0:1, :]
    for k in range(1, CONV_W):
        acc = acc + us[k] * cw_ref[k:k + 1, :]
    conv = cb_ref[...] + acc
    cbf = conv.astype(BF)
    r_ = _sigmoid(_dot(cbf, wa_ref[0], NN) + ba_ref[...])
    i_ = _sigmoid(_dot(cbf, wx_ref[0], NN) + bx_ref[...])
    sp = _softplus(-lam_ref[...])
    la = -LRU_C * r_ * sp
    a = jnp.exp(la)
    mult_raw = jnp.sqrt(-_expm1(2.0 * la))
    mult = jnp.where(grow == 0, 1.0, mult_raw)
    return us, conv, cbf, r_, i_, sp, a, mult_raw, mult


def _rglru_specs(T, nt, rev):
    tmap = (lambda n, t: (nt - 1 - t, n)) if rev else (lambda n, t: (t, n))
    hmap = ((lambda n, t: (jnp.maximum((nt - 1 - t) * (T // 8) - 1, 0), n)) if rev
            else (lambda n, t: (jnp.maximum(t * (T // 8) - 1, 0), n)))
    tile = pl.BlockSpec((T, RNN_BLOCK), tmap)
    halo = pl.BlockSpec((8, RNN_BLOCK), hmap)
    vec = pl.BlockSpec((1, RNN_BLOCK), lambda n, t: (0, n))
    cw = pl.BlockSpec((CONV_W, RNN_BLOCK), lambda n, t: (0, n))
    wblk = pl.BlockSpec((1, RNN_BLOCK, RNN_BLOCK), lambda n, t: (n, 0, 0))
    return tile, halo, vec, cw, wblk


def _rglru_fwd(xr, g, cw, cb, wa, ba, wx, bx, lam, T):
    S = xr.shape[0]
    nt = S // T

    def body(u_ref, uh_ref, g_ref, cw_ref, cb_ref, wa_ref, ba_ref, wx_ref, bx_ref, lam_ref, h_ref, y_ref, carry):
        t = pl.program_id(1)

        @pl.when(t == 0)
        def _():
            carry[...] = jnp.zeros_like(carry)

        row = lax.broadcasted_iota(jnp.int32, (T, RNN_BLOCK), 0)
        grow = row + t * T
        head8 = jnp.where(t > 0, uh_ref[...], 0.0)
        _, conv, _, _, i_, _, a, _, mult = _rglru_gates(u_ref[...], head8, grow, row, T, cw_ref, cb_ref, wa_ref, ba_ref,
                                                         wx_ref, bx_ref, lam_ref)
        b = mult * i_ * conv
        s = 1
        while s < T:
            keep = row >= s
            a_s = jnp.where(keep, pltpu.roll(a, s, 0), 1.0)
            b_s = jnp.where(keep, pltpu.roll(b, s, 0), 0.0)
            b = a * b_s + b
            a = a * a_s
            s *= 2
        h = b + a * carry[0:1, :]
        carry[...] = jnp.broadcast_to(h[T - 1:T, :], carry.shape)
        h_ref[...] = h
        gv = g_ref[...]
        y_ref[...] = (h * (gv * _sigmoid(gv))).astype(BF)

    tile, halo, vec, cwspec, wblk = _rglru_specs(T, nt, False)
    return pl.pallas_call(
        body, name="rglru_fwd", grid=(RNN_BLOCKS, nt),
        in_specs=[tile, halo, tile, cwspec, vec, wblk, vec, wblk, vec, vec],
        out_specs=[tile, tile],
        out_shape=[jax.ShapeDtypeStruct((S, D_RNN), F32), jax.ShapeDtypeStruct((S, D_RNN), BF)],
        scratch_shapes=[pltpu.VMEM((8, RNN_BLOCK), F32)],
        compiler_params=_params(("parallel", "arbitrary")),
    )(xr, xr, g, cw, cb, wa, ba, wx, bx, lam)


def _rglru_bwd(xr, g, h, dy, cw, cb, wa, ba, wx, bx, lam, T):
    S = xr.shape[0]
    nt = S // T

    def body(u_ref, uh_ref, g_ref, h_ref, hh_ref, dy_ref, cw_ref, cb_ref, wa_ref, ba_ref, wx_ref, bx_ref, lam_ref,
             du_ref, dg_ref, dwa_ref, dwx_ref, dvec_ref, c_dhh, c_a, c_dconv):
        t = pl.program_id(1)
        tt = nt - 1 - t

        @pl.when(t == 0)
        def _():
            c_dhh[...] = jnp.zeros_like(c_dhh)
            c_a[...] = jnp.zeros_like(c_a)
            c_dconv[...] = jnp.zeros_like(c_dconv)
            dwa_ref[...] = jnp.zeros_like(dwa_ref)
            dwx_ref[...] = jnp.zeros_like(dwx_ref)
            dvec_ref[...] = jnp.zeros_like(dvec_ref)

        row = lax.broadcasted_iota(jnp.int32, (T, RNN_BLOCK), 0)
        row8 = row[:8]
        grow = row + tt * T
        head8 = jnp.where(tt > 0, uh_ref[...], 0.0)
        us, conv, cbf, r_, i_, sp, a, mult_raw, mult = _rglru_gates(
            u_ref[...], head8, grow, row, T, cw_ref, cb_ref, wa_ref, ba_ref, wx_ref, bx_ref, lam_ref)
        hv = h_ref[...]
        hprev = _shift_down(hv, 1, jnp.where(tt > 0, hh_ref[...], 0.0), row, T)
        gv = g_ref[...]
        sg = _sigmoid(gv)
        dyv = dy_ref[...]
        dg_ref[...] = (dyv * hv * (sg * (1.0 + gv * (1.0 - sg)))).astype(BF)
        d = dyv * (gv * sg)
        A = _shift_up(a, 1, c_a[...], row, T)
        s = 1
        while s < T:
            keep = row < T - s
            A_s = jnp.where(keep, pltpu.roll(A, T - s, 0), 1.0)
            d_s = jnp.where(keep, pltpu.roll(d, T - s, 0), 0.0)
            d = A * d_s + d
            A = A * A_s
            s *= 2
        dhh = d + A * c_dhh[0:1, :]
        da = dhh * hprev
        dconv = dhh * mult * i_
        di = dhh * mult * conv
        dmult = dhh * i_ * conv
        dla = da * a - jnp.where(grow == 0, 0.0, dmult * (a * a) / mult_raw)
        dr = dla * (-LRU_C * sp)
        dsp = jnp.sum(dla * (-LRU_C * r_), axis=0, keepdims=True)
        dza = dr * r_ * (1.0 - r_)
        dzx = di * i_ * (1.0 - i_)
        dza_b, dzx_b = dza.astype(BF), dzx.astype(BF)
        dconv = dconv + _dot(dza_b, wa_ref[0], NT) + _dot(dzx_b, wx_ref[0], NT)
        dwa_ref[0] += _dot(cbf, dza_b, TN)
        dwx_ref[0] += _dot(cbf, dzx_b, TN)
        lam = lam_ref[...]
        rows = [jnp.sum(dconv * us[k], axis=0, keepdims=True) for k in range(CONV_W)]
        rows += [jnp.sum(dconv, axis=0, keepdims=True), jnp.sum(dza, axis=0, keepdims=True),
                 jnp.sum(dzx, axis=0, keepdims=True), dsp * (-_sigmoid(-lam))]
        upd = jnp.zeros((8, RNN_BLOCK), F32)
        for j, rv in enumerate(rows):
            upd = upd + jnp.where(row8 == j, rv, 0.0)
        dvec_ref[...] += upd
        tail8 = c_dconv[...]
        du = dconv * cw_ref[0:1, :]
        for k in range(1, CONV_W):
            du = du + _shift_up(dconv, k, tail8, row, T) * cw_ref[k:k + 1, :]
        du_ref[...] = du.astype(BF)
        c_dhh[...] = jnp.broadcast_to(dhh[0:1, :], c_dhh.shape)
        c_a[...] = jnp.broadcast_to(a[0:1, :], c_a.shape)
        c_dconv[...] = dconv[:8]

    tile, halo, vec, cwspec, wblk = _rglru_specs(T, nt, True)
    acc8 = pl.BlockSpec((8, RNN_BLOCK), lambda n, t: (0, n))
    return pl.pallas_call(
        body, name="rglru_bwd", grid=(RNN_BLOCKS, nt),
        in_specs=[tile, halo, tile, tile, halo, tile, cwspec, vec, wblk, vec, wblk, vec, vec],
        out_specs=[tile, tile, wblk, wblk, acc8],
        out_shape=[jax.ShapeDtypeStruct((S, D_RNN), BF), jax.ShapeDtypeStruct((S, D_RNN), BF),
                   jax.ShapeDtypeStruct((RNN_BLOCKS, RNN_BLOCK, RNN_BLOCK), F32),
                   jax.ShapeDtypeStruct((RNN_BLOCKS, RNN_BLOCK, RNN_BLOCK), F32),
                   jax.ShapeDtypeStruct((8, D_RNN), F32)],
        scratch_shapes=[pltpu.VMEM((8, RNN_BLOCK), F32)] * 3,
        compiler_params=_params(("parallel", "arbitrary")),
    )(xr, xr, g, h, h, dy, cw, cb, wa, ba, wx, bx, lam)


def _rel_bucket_map():
    qi = np.arange(WINDOW)[:, None]
    kj = np.arange(2 * WINDOW)[None, :]
    dist = jnp.asarray(qi + WINDOW - kj, jnp.int32)
    n = jnp.maximum(dist, 0)
    max_exact = REL_BUCKETS // 2
    ratio = jnp.log(jnp.maximum(n, 1).astype(F32) / max_exact) / math.log(REL_MAX_DIST / max_exact)
    large = jnp.minimum(max_exact + (ratio * (REL_BUCKETS - max_exact)).astype(jnp.int32), REL_BUCKETS - 1)
    bucket = jnp.where(n < max_exact, n, large).astype(jnp.int32)
    j = np.arange(WINDOW)[None, :]
    return jnp.where(jnp.asarray(j > qi), bucket[:, :WINDOW], bucket[:, WINDOW:])


def _swa_common(n, kv_ref, bucket_ref, relb_ref, bias_scr):
    @pl.when(n == 0)
    def _():
        bk = bucket_ref[...]
        for h in range(SWA_HEADS):
            acc = jnp.zeros((WINDOW, WINDOW), F32)
            for b in range(REL_BUCKETS):
                acc = acc + jnp.where(bk == b, relb_ref[b, h], 0.0)
            bias_scr[h] = acc

    prev0 = pl.multiple_of(jnp.maximum(n - 1, 0) * WINDOW, WINDOW)
    cur0 = pl.multiple_of(n * WINDOW, WINDOW)
    kk = jnp.concatenate([kv_ref[pl.ds(prev0, WINDOW), :], kv_ref[pl.ds(cur0, WINDOW), :]], axis=0).astype(F32)
    rowi = lax.broadcasted_iota(jnp.int32, (WINDOW, WINDOW), 0)
    col = lax.broadcasted_iota(jnp.int32, (WINDOW, WINDOW), 1)
    from_prev = col > rowi
    return kk, from_prev, prev0, cur0


def _fold(full, from_prev):
    return jnp.where(from_prev, full[:, :WINDOW], full[:, WINDOW:])


def _unfold(sq, from_prev):
    return jnp.concatenate([jnp.where(from_prev, sq, 0.0), jnp.where(from_prev, 0.0, sq)], axis=1)


def _half_pair(part, kvh):
    lo = lax.broadcasted_iota(jnp.int32, part.shape, 1) < SWA_HD
    if kvh == 0:
        pa = jnp.where(lo, part, 0.0)
        pb = pltpu.roll(pa, SWA_HD, 1)
    else:
        pb = jnp.where(lo, 0.0, part)
        pa = pltpu.roll(pb, SWA_HD, 1)
    return pa.astype(BF), pb.astype(BF)


ALL_HEADS = SWA_HEADS * WINDOW


def _sink_column(sinks):
    return jnp.repeat(sinks.reshape(SWA_HEADS), WINDOW).reshape(ALL_HEADS, 1)


def _swa_operands(kk):
    return [(_half_pair(kk[:, :128], kvh), _half_pair(kk[:, 128:], kvh)) for kvh in range(SWA_KV_HEADS)]


def _swa_probs(n, q_ref, ops, bias_scr, sinkc_ref, from_prev):
    lgs = []
    for kvh in range(SWA_KV_HEADS):
        (ka, kb), _ = ops[kvh]
        for p in range(4):
            q2 = q_ref[:, kvh * 512 + p * 128:kvh * 512 + p * 128 + 128]
            lgs += [_fold(_dot(q2, ka, NT), from_prev), _fold(_dot(q2, kb, NT), from_prev)]
    lg = jnp.concatenate(lgs, axis=0) * (SWA_HD ** -0.5) + bias_scr[...].reshape(ALL_HEADS, WINDOW)
    rowi = jnp.bitwise_and(lax.broadcasted_iota(jnp.int32, (ALL_HEADS, WINDOW), 0), WINDOW - 1)
    col = lax.broadcasted_iota(jnp.int32, (ALL_HEADS, WINDOW), 1)
    no_prev = jnp.where(n > 0, 0, 4 * WINDOW)
    lg = jnp.where(jnp.logical_or(col <= rowi, col > rowi + no_prev), lg, NEG_INF)
    sink = sinkc_ref[...]
    m = jnp.maximum(jnp.max(lg, axis=-1, keepdims=True), sink)
    e = jnp.exp(lg - m)
    es = jnp.exp(sink - m)
    den = jnp.sum(e, axis=-1, keepdims=True) + es
    return e / den, es / den


def _swa_fwd(q, kv, g, bucket, rel_bias, sink_col):
    S = q.shape[0]
    nb = S // WINDOW

    def body(q_ref, kv_ref, g_ref, bucket_ref, relb_ref, sinkc_ref, o_ref, y_ref, bias_scr):
        n = pl.program_id(0)
        kk, from_prev, _, _ = _swa_common(n, kv_ref, bucket_ref, relb_ref, bias_scr)
        ops = _swa_operands(kk)
        pr, _ = _swa_probs(n, q_ref, ops, bias_scr, sinkc_ref, from_prev)
        for kvh in range(SWA_KV_HEADS):
            _, (va, vb) = ops[kvh]
            for p in range(4):
                c0 = kvh * 512 + p * 128
                r0 = (kvh * 8 + 2 * p) * WINDOW
                o2 = (_dot(_unfold(pr[r0:r0 + WINDOW], from_prev).astype(BF), va, NN)
                      + _dot(_unfold(pr[r0 + WINDOW:r0 + 2 * WINDOW], from_prev).astype(BF), vb, NN))
                o_ref[:, c0:c0 + 128] = o2
                gv = g_ref[:, c0:c0 + 128]
                y_ref[:, c0:c0 + 128] = (o2 * (gv * _sigmoid(gv))).astype(BF)

    blk = pl.BlockSpec((WINDOW, 1024), lambda n: (n, 0))
    smem = pl.BlockSpec(memory_space=pltpu.SMEM)
    sinkc = pl.BlockSpec((ALL_HEADS, 1), lambda n: (0, 0))
    return pl.pallas_call(
        body, name="swa_fwd", grid=(nb,),
        in_specs=[blk, pl.BlockSpec((S, 256), lambda n: (0, 0)), blk, pl.BlockSpec((WINDOW, WINDOW), lambda n: (0, 0)), smem, sinkc],
        out_specs=[blk, blk],
        out_shape=[jax.ShapeDtypeStruct((S, 1024), F32), jax.ShapeDtypeStruct((S, 1024), BF)],
        scratch_shapes=[pltpu.VMEM((SWA_HEADS, WINDOW, WINDOW), F32)],
        compiler_params=_params(("arbitrary",)),
    )(q, kv, g, bucket, rel_bias, sink_col)


def _swa_bwd(q, kv, g, o, dy, bucket, rel_bias, sink_col):
    S = q.shape[0]
    nb = S // WINDOW

    def body(q_ref, kv_ref, g_ref, o_ref, dy_ref, bucket_ref, relb_ref, sinkc_ref,
             dq_ref, dg_ref, dkv_ref, dsink_ref, drel_ref, bias_scr, dbias_scr, dsink_scr):
        n = pl.program_id(0)

        @pl.when(n == 0)
        def _():
            dbias_scr[...] = jnp.zeros_like(dbias_scr)
            dsink_scr[...] = jnp.zeros_like(dsink_scr)
            dkv_ref[...] = jnp.zeros_like(dkv_ref)

        kk, from_prev, prev0, cur0 = _swa_common(n, kv_ref, bucket_ref, relb_ref, bias_scr)
        ops = _swa_operands(kk)
        pr, ps = _swa_probs(n, q_ref, ops, bias_scr, sinkc_ref, from_prev)
        do2s, dps = [], []
        for kvh in range(SWA_KV_HEADS):
            _, (va, vb) = ops[kvh]
            for p in range(4):
                c0 = kvh * 512 + p * 128
                gv = g_ref[:, c0:c0 + 128]
                sg = _sigmoid(gv)
                dyv = dy_ref[:, c0:c0 + 128]
                dg_ref[:, c0:c0 + 128] = (dyv * o_ref[:, c0:c0 + 128] * (sg * (1.0 + gv * (1.0 - sg)))).astype(BF)
                do2 = (dyv * (gv * sg)).astype(BF)
                do2s.append(do2)
                dps += [_fold(_dot(do2, va, NT), from_prev), _fold(_dot(do2, vb, NT), from_prev)]
        dp = jnp.concatenate(dps, axis=0)
        delta = jnp.sum(pr * dp, axis=-1, keepdims=True)
        ds = pr * (dp - delta)
        dbias_scr[...] += ds.reshape(SWA_HEADS, WINDOW, WINDOW)
        dsink_scr[...] += ps * delta
        dsc = ds * (SWA_HD ** -0.5)
        lo256 = lax.broadcasted_iota(jnp.int32, (2 * WINDOW, 128), 1) < SWA_HD
        dks, dvs = [], []
        for kvh in range(SWA_KV_HEADS):
            (ka, kb), _ = ops[kvh]
            dka = jnp.zeros((2 * WINDOW, 128), F32)
            dkb, dva, dvb = dka, dka, dka
            for p in range(4):
                c0 = kvh * 512 + p * 128
                r0 = (kvh * 8 + 2 * p) * WINDOW
                q2 = q_ref[:, c0:c0 + 128]
                do2 = do2s[kvh * 4 + p]
                ds0 = _unfold(dsc[r0:r0 + WINDOW], from_prev).astype(BF)
                ds1 = _unfold(dsc[r0 + WINDOW:r0 + 2 * WINDOW], from_prev).astype(BF)
                dq_ref[:, c0:c0 + 128] = (_dot(ds0, ka, NN) + _dot(ds1, kb, NN)).astype(BF)
                dka = dka + _dot(ds0, q2, TN)
                dkb = dkb + _dot(ds1, q2, TN)
                dva = dva + _dot(_unfold(pr[r0:r0 + WINDOW], from_prev).astype(BF), do2, TN)
                dvb = dvb + _dot(_unfold(pr[r0 + WINDOW:r0 + 2 * WINDOW], from_prev).astype(BF), do2, TN)
            dks.append(jnp.where(lo256, dka, 0.0) + pltpu.roll(jnp.where(lo256, 0.0, dkb), SWA_HD, 1))
            dvs.append(jnp.where(lo256, dva, 0.0) + pltpu.roll(jnp.where(lo256, 0.0, dvb), SWA_HD, 1))
        dk = dks[0] + pltpu.roll(dks[1], SWA_HD, 1)
        dv = dvs[0] + pltpu.roll(dvs[1], SWA_HD, 1)
        dkv_ref[pl.ds(prev0, WINDOW), 0:128] += dk[:WINDOW]
        dkv_ref[pl.ds(prev0, WINDOW), 128:256] += dv[:WINDOW]
        dkv_ref[pl.ds(cur0, WINDOW), 0:128] += dk[WINDOW:]
        dkv_ref[pl.ds(cur0, WINDOW), 128:256] += dv[WINDOW:]

        @pl.when(n == nb - 1)
        def _():
            dsink_ref[...] = -jnp.sum(dsink_scr[...].reshape(SWA_HEADS, WINDOW, 1), axis=1)
            bk = bucket_ref[...]
            sums = []
            for b in range(REL_BUCKETS):
                sums.append(jnp.sum(jnp.where((bk == b)[None], dbias_scr[...], 0.0), axis=1))
            drel_ref[...] = jnp.sum(jnp.concatenate(sums, axis=0), axis=1, keepdims=True)

    blk = pl.BlockSpec((WINDOW, 1024), lambda n: (n, 0))
    smem = pl.BlockSpec(memory_space=pltpu.SMEM)
    whole = lambda shape: pl.BlockSpec(shape, lambda n: (0, 0))
    return pl.pallas_call(
        body, name="swa_bwd", grid=(nb,),
        in_specs=[blk, whole((S, 256)), blk, blk, blk, whole((WINDOW, WINDOW)), smem, whole((ALL_HEADS, 1))],
        out_specs=[blk, blk, whole((S, 256)), whole((SWA_HEADS, 1)), whole((REL_BUCKETS * SWA_HEADS, 1))],
        out_shape=[jax.ShapeDtypeStruct((S, 1024), BF), jax.ShapeDtypeStruct((S, 1024), BF),
                   jax.ShapeDtypeStruct((S, 256), F32), jax.ShapeDtypeStruct((SWA_HEADS, 1), F32),
                   jax.ShapeDtypeStruct((REL_BUCKETS * SWA_HEADS, 1), F32)],
        scratch_shapes=[pltpu.VMEM((SWA_HEADS, WINDOW, WINDOW), F32), pltpu.VMEM((SWA_HEADS, WINDOW, WINDOW), F32),
                        pltpu.VMEM((ALL_HEADS, 1), F32)],
        compiler_params=_params(("arbitrary",)),
    )(q, kv, g, o, dy, bucket, rel_bias, sink_col)


def _mem_probs(qh, mk):
    lg = _dot(qh, mk, NT) * (MEM_HD ** -0.5)
    e = jnp.exp(lg - jnp.max(lg, axis=-1, keepdims=True))
    return e / jnp.sum(e, axis=-1, keepdims=True)


def _mem_fwd(q, mkv, g):
    S = q.shape[0]
    M = mkv.shape[0]
    tq = 256

    def body(q_ref, mkv_ref, g_ref, o_ref, y_ref):
        for h in range(MEM_HEADS):
            c0 = h * MEM_HD
            pr = _mem_probs(q_ref[:, c0:c0 + MEM_HD], mkv_ref[:, c0:c0 + MEM_HD])
            o = _dot(pr.astype(BF), mkv_ref[:, D_MEM + c0:D_MEM + c0 + MEM_HD], NN)
            o_ref[:, c0:c0 + MEM_HD] = o
            gv = g_ref[:, c0:c0 + MEM_HD]
            y_ref[:, c0:c0 + MEM_HD] = (o * (gv * _sigmoid(gv))).astype(BF)

    blk = pl.BlockSpec((tq, D_MEM), lambda i: (i, 0))
    return pl.pallas_call(
        body, name="mem_fwd", grid=(S // tq,),
        in_specs=[blk, pl.BlockSpec((M, 2 * D_MEM), lambda i: (0, 0)), blk], out_specs=[blk, blk],
        out_shape=[jax.ShapeDtypeStruct((S, D_MEM), F32), jax.ShapeDtypeStruct((S, D_MEM), BF)],
        compiler_params=_params(("parallel",)),
    )(q, mkv, g)


def _mem_bwd(q, mkv, g, o, dy):
    S = q.shape[0]
    M = mkv.shape[0]
    tq = 256

    def body(q_ref, mkv_ref, g_ref, o_ref, dy_ref, dq_ref, dg_ref, dmkv_ref):
        @pl.when(pl.program_id(0) == 0)
        def _():
            dmkv_ref[...] = jnp.zeros_like(dmkv_ref)

        for h in range(MEM_HEADS):
            c0 = h * MEM_HD
            qh = q_ref[:, c0:c0 + MEM_HD]
            mk = mkv_ref[:, c0:c0 + MEM_HD]
            mv = mkv_ref[:, D_MEM + c0:D_MEM + c0 + MEM_HD]
            gv = g_ref[:, c0:c0 + MEM_HD]
            sg = _sigmoid(gv)
            dyv = dy_ref[:, c0:c0 + MEM_HD]
            dg_ref[:, c0:c0 + MEM_HD] = (dyv * o_ref[:, c0:c0 + MEM_HD] * (sg * (1.0 + gv * (1.0 - sg)))).astype(BF)
            do = (dyv * (gv * sg)).astype(BF)
            pr = _mem_probs(qh, mk)
            dp = _dot(do, mv, NT)
            ds = pr * (dp - jnp.sum(pr * dp, axis=-1, keepdims=True))
            dsb = (ds * (MEM_HD ** -0.5)).astype(BF)
            dq_ref[:, c0:c0 + MEM_HD] = _dot(dsb, mk, NN).astype(BF)
            dmkv_ref[:, c0:c0 + MEM_HD] += _dot(dsb, qh, TN)
            dmkv_ref[:, D_MEM + c0:D_MEM + c0 + MEM_HD] += _dot(pr.astype(BF), do, TN)

    blk = pl.BlockSpec((tq, D_MEM), lambda i: (i, 0))
    whole = pl.BlockSpec((M, 2 * D_MEM), lambda i: (0, 0))
    return pl.pallas_call(
        body, name="mem_bwd", grid=(S // tq,),
        in_specs=[blk, whole, blk, blk, blk], out_specs=[blk, blk, whole],
        out_shape=[jax.ShapeDtypeStruct((S, D_MEM), BF), jax.ShapeDtypeStruct((S, D_MEM), BF),
                   jax.ShapeDtypeStruct((M, 2 * D_MEM), F32)],
        compiler_params=_params(("arbitrary",)),
    )(q, mkv, g, o, dy)


MERGE_TN = 512


def _merge_specs(tm):
    ytile = pl.BlockSpec((tm, 1024), lambda i, j: (i, 0))
    wblk = pl.BlockSpec((1024, MERGE_TN), lambda i, j: (0, j))
    gls = [pl.BlockSpec((None, tm, MERGE_TN), (lambda i, j, br=br: (br, i, j))) for br in range(3)]
    otile = pl.BlockSpec((tm, MERGE_TN), lambda i, j: (i, j))
    return ytile, wblk, gls, otile


def _merge_fwd(ys, ws, gl, tm):
    S = gl.shape[1]

    def body(y0, y1, y2, w0, w1, w2, g0, g1, g2, o_ref):
        acc = None
        for y_ref, w_ref, g_ref in ((y0, w0, g0), (y1, w1, g1), (y2, w2, g2)):
            term = _sigmoid(g_ref[...]) * _dot(y_ref[...], w_ref[...], NN)
            acc = term if acc is None else acc + term
        o_ref[...] = acc.astype(BF)

    ytile, wblk, gls, otile = _merge_specs(tm)
    return pl.pallas_call(
        body, name="merge_fwd", grid=(S // tm, D_MODEL // MERGE_TN),
        in_specs=[ytile] * 3 + [wblk] * 3 + gls, out_specs=otile,
        out_shape=jax.ShapeDtypeStruct((S, D_MODEL), BF),
        compiler_params=_params(("parallel", "arbitrary")),
    )(*ys, *ws, gl, gl, gl)


def _merge_bwd(dout, w_out, ys, ws, gl, tm):
    S = gl.shape[1]

    def body(do_ref, wo_ref, y0, y1, y2, w0, w1, w2, g0, g1, g2, dg0, dg1, dg2, dp0, dp1, dp2):
        dm = _dot(do_ref[...], wo_ref[...], NT)
        for y_ref, w_ref, g_ref, dg_ref, dp_ref in ((y0, w0, g0, dg0, dp0), (y1, w1, g1, dg1, dp1), (y2, w2, g2, dg2, dp2)):
            gate = _sigmoid(g_ref[...])
            pv = _dot(y_ref[...], w_ref[...], NN)
            dg_ref[...] = (dm * pv * gate * (1.0 - gate)).astype(BF)
            dp_ref[...] = (dm * gate).astype(BF)

    ytile, wblk, gls, otile = _merge_specs(tm)
    out = jax.ShapeDtypeStruct((S, D_MODEL), BF)
    return pl.pallas_call(
        body, name="merge_bwd", grid=(S // tm, D_MODEL // MERGE_TN),
        in_specs=[pl.BlockSpec((tm, D_MODEL), lambda i, j: (i, 0)), pl.BlockSpec((MERGE_TN, D_MODEL), lambda i, j: (j, 0))]
        + [ytile] * 3 + [wblk] * 3 + gls,
        out_specs=[otile] * 6, out_shape=[out] * 6,
        compiler_params=_params(("parallel", "arbitrary")),
    )(dout, w_out, *ys, *ws, gl, gl, gl)


def _out_loss(merged, w_out, x, target, post_g, tm):
    S = x.shape[0]

    def body(m_ref, w_ref, x_ref, t_ref, g_ref, dout_ref, dy_ref, loss_ref, dpost_ref):
        @pl.when(pl.program_id(0) == 0)
        def _():
            loss_ref[...] = jnp.zeros_like(loss_ref)
            dpost_ref[...] = jnp.zeros_like(dpost_ref)

        out = _dot(m_ref[...], w_ref[...], NN)
        r = lax.rsqrt(jnp.mean(out * out, axis=-1, keepdims=True) + EPS)
        nrm = out * r
        gv = g_ref[...]
        err = (x_ref[...] + nrm * gv) - t_ref[...]
        sq = jnp.sum(jnp.sum(err * err, axis=1, keepdims=True), axis=0, keepdims=True)
        loss_ref[...] += sq * (0.5 / D_MODEL)
        dy = err * (1.0 / D_MODEL)
        dy_ref[...] = dy
        dpost_ref[...] += jnp.sum(dy * nrm, axis=0, keepdims=True)
        dn = dy * gv
        dout_ref[...] = (r * (dn - nrm * jnp.mean(dn * nrm, axis=-1, keepdims=True))).astype(BF)

    row = pl.BlockSpec((tm, D_MODEL), lambda i: (i, 0))
    return pl.pallas_call(
        body, name="out_loss", grid=(S // tm,),
        in_specs=[row, pl.BlockSpec((D_MODEL, D_MODEL), lambda i: (0, 0)), row, row, pl.BlockSpec((1, D_MODEL), lambda i: (0, 0))],
        out_specs=[row, row, pl.BlockSpec((8, 128), lambda i: (0, 0)), pl.BlockSpec((1, D_MODEL), lambda i: (0, 0))],
        out_shape=[jax.ShapeDtypeStruct((S, D_MODEL), BF), jax.ShapeDtypeStruct((S, D_MODEL), F32),
                   jax.ShapeDtypeStruct((8, 128), F32), jax.ShapeDtypeStruct((1, D_MODEL), F32)],
        compiler_params=_params(("arbitrary",)),
    )(merged, w_out, x, target, post_g)


def _dh_dx(dproj, w_in, x, dy, pre_g, tm):
    S = x.shape[0]
    nk, tk = dproj.shape[0], dproj.shape[2]

    def body(dp_ref, w_ref, x_ref, dy_ref, g_ref, dx_ref, dpre_ref, acc_ref):
        i, k = pl.program_id(0), pl.program_id(1)

        @pl.when(jnp.logical_and(i == 0, k == 0))
        def _():
            dpre_ref[...] = jnp.zeros_like(dpre_ref)

        @pl.when(k == 0)
        def _():
            acc_ref[...] = jnp.zeros_like(acc_ref)

        acc_ref[...] += _dot(dp_ref[...], w_ref[...], NN)

        @pl.when(k == nk - 1)
        def _():
            dh = acc_ref[...]
            xv = x_ref[...]
            r = lax.rsqrt(jnp.mean(xv * xv, axis=-1, keepdims=True) + EPS)
            nrm = xv * r
            dpre_ref[...] += jnp.sum(dh * nrm, axis=0, keepdims=True)
            dn = dh * g_ref[...]
            dx_ref[...] = r * (dn - nrm * jnp.mean(dn * nrm, axis=-1, keepdims=True)) + dy_ref[...]

    row = pl.BlockSpec((tm, D_MODEL), lambda i, k: (i, 0))
    vec = pl.BlockSpec((1, D_MODEL), lambda i, k: (0, 0))
    return pl.pallas_call(
        body, name="dh_dx", grid=(S // tm, nk),
        in_specs=[pl.BlockSpec((None, tm, tk), lambda i, k: (k, i, 0)), pl.BlockSpec((tk, D_MODEL), lambda i, k: (k, 0)), row, row, vec],
        out_specs=[row, vec],
        out_shape=[jax.ShapeDtypeStruct((S, D_MODEL), F32), jax.ShapeDtypeStruct((1, D_MODEL), F32)],
        scratch_shapes=[pltpu.VMEM((tm, D_MODEL), F32)],
        compiler_params=_params(("arbitrary", "arbitrary"), 56),
    )(dproj, w_in, x, dy, pre_g)


def _sum_parts(parts, name):
    P, R, C = parts.shape
    tr = max(t for t in range(8, 513, 8) if R % t == 0)

    def body(p_ref, o_ref):
        acc = p_ref[0]
        for j in range(1, P):
            acc = acc + p_ref[j]
        o_ref[...] = acc

    return pl.pallas_call(
        body, name=name, grid=(R // tr,),
        in_specs=[pl.BlockSpec((P, tr, C), lambda i: (0, i, 0))], out_specs=pl.BlockSpec((tr, C), lambda i: (i, 0)),
        out_shape=jax.ShapeDtypeStruct((R, C), F32), compiler_params=_params(("parallel",)),
    )(parts)


def _adamw(parts, w, m, v, name):
    groups = list(parts) if isinstance(parts, (list, tuple)) else [parts]
    P, rows, C = groups[0].shape
    R = rows * len(groups)
    tr = 128 if rows % 128 == 0 else rows
    per = rows // tr
    c1 = 1.0 - ADAM_B1 ** ADAM_STEP
    c2 = 1.0 - ADAM_B2 ** ADAM_STEP

    def body(*refs):
        p_refs = refs[:len(groups)]
        w_ref, m_ref, v_ref, g_ref, d_ref, nm_ref, nv_ref = refs[len(groups):]
        g = None
        for q, p_ref in enumerate(p_refs):
            gq = p_ref[0].astype(F32)
            for j in range(1, P):
                gq = gq + p_ref[j].astype(F32)
            g = gq if g is None else jnp.where(pl.program_id(0) // per == q, gq, g)
        nm = ADAM_B1 * m_ref[...] + (1.0 - ADAM_B1) * g
        nv = ADAM_B2 * v_ref[...] + (1.0 - ADAM_B2) * (g * g)
        g_ref[...] = g
        nm_ref[...] = nm
        nv_ref[...] = nv
        d_ref[...] = -ADAM_LR * ((nm / c1) / (jnp.sqrt(nv / c2) + ADAM_EPS) + ADAM_WD * w_ref[...])

    tile = pl.BlockSpec((tr, C), lambda i: (i, 0))
    out = jax.ShapeDtypeStruct((R, C), F32)
    return pl.pallas_call(
        body, name=name, grid=(R // tr,),
        in_specs=[pl.BlockSpec((P, tr, C), (lambda i, q=q: (0, jnp.clip(i - q * per, 0, per - 1), 0))) for q in range(len(groups))]
        + [tile, tile, tile], out_specs=[tile] * 4, out_shape=[out] * 4,
        compiler_params=_params(("parallel",)),
    )(*groups, w, m, v)


def _forward_a(x, mem, pre_g, mem_g, w_in, conv_w, conv_b, w_a, b_a, w_x, b_x, lam, sinks, rel_bias):
    S = x.shape[0]
    st = dict(T=min(512, S // 2), tm=min(512, S), bucket=_rel_bucket_map())
    st["h"] = _rms_fwd(x, pre_g, "pre_norm")
    st["memn"] = _rms_fwd(mem, mem_g, "mem_norm")
    seg = {}
    for name, c0, width, dt in SEGMENTS:
        seg[name] = _matmul(st["h"], w_in, "nt", S, width, D_MODEL, S, SEG_TILE, D_MODEL, dt, "proj_" + name, b_noff=c0 // SEG_TILE,
                            out_blocked="third" if name == "gl" else None)
    st["seg"] = seg
    st["h_rg"], st["y_rg"] = _rglru_fwd(seg["xr"], seg["g_rg"], conv_w, conv_b, w_a, b_a, w_x, b_x, lam, st["T"])
    st["o_swa"], st["y_swa"] = _swa_fwd(seg["q_s"], seg["kv"], seg["g_swa"], st["bucket"], rel_bias, _sink_column(sinks))
    return st


def _forward_b(st, x, target, post_g, w_memkv, wbr, w_out):
    S = x.shape[0]
    M = st["memn"].shape[0]
    seg = st["seg"]
    st["mkv"] = _matmul(st["memn"], w_memkv, "nn", M, 2 * D_MEM, D_MODEL, M, 512, D_MODEL, BF, "mem_kv")
    st["o_mem"], st["y_mem"] = _mem_fwd(seg["q_m"], st["mkv"], seg["g_mem"])
    st["ys"] = (st["y_rg"], st["y_swa"], st["y_mem"])
    st["merged"] = _merge_fwd(st["ys"], wbr, seg["gl"], st["tm"])
    st["dout"], st["dy"], st["loss"], st["dpost"] = _out_loss(st["merged"], w_out, x, target, post_g, min(256, S))
    return st


def _backward_a1(st, wbr, w_out):
    S = st["h"].shape[0]
    seg, ys, tm = st["seg"], st["ys"], st["tm"]
    st["dw_out"] = _matmul(st["merged"], st["dout"], "tn", D_MODEL, D_MODEL, S, 256, D_MODEL, S, BF, "dw_out", out_blocked="row")
    dgl0, dgl1, dgl2, dp0, dp1, dp2 = _merge_bwd(st["dout"], w_out, ys, wbr, seg["gl"], tm)
    st["dgl"] = (dgl0, dgl1, dgl2)
    dys, dwbr = [], []
    for i, dp in enumerate((dp0, dp1, dp2)):
        dys.append(_matmul(dp, wbr[i], "nt", S, 1024, D_MODEL, tm, 1024, D_MODEL, F32, "dy_br%d" % i))
        dwbr.append(_matmul(ys[i], dp, "tn", 1024, D_MODEL, S, 1024, 256, S, BF, "dw_br%d" % i, out_blocked="col"))
    st["dys"], st["dwbr"] = dys, dwbr
    return st


def _backward_a2(st, mem, w_memkv, conv_w, conv_b, w_a, b_a, w_x, b_x, lam):
    M = mem.shape[0]
    seg, dys = st["seg"], st["dys"]
    st["dq_m"], st["dg_mem"], dmkv = _mem_bwd(seg["q_m"], st["mkv"], seg["g_mem"], st["o_mem"], dys[2])
    dmkv_b = dmkv.astype(BF)
    st["dw_memkv"] = _matmul(st["memn"], dmkv_b, "tn", D_MODEL, 2 * D_MEM, M, 256, 2 * D_MEM, M, BF, "dw_memkv", out_blocked="row")
    dmemn = _matmul(dmkv_b, w_memkv, "nt", M, D_MODEL, 2 * D_MEM, M, 512, 2 * D_MEM, F32, "dmemn")
    st["dmem_g"] = _rms_gain_grad(dmemn, mem, "dmem_gain")
    st["dxr"], st["dg_rg"], st["dw_a"], st["dw_x"], st["dvec"] = _rglru_bwd(
        seg["xr"], seg["g_rg"], st["h_rg"], dys[0], conv_w, conv_b, w_a, b_a, w_x, b_x, lam, st["T"])
    return st


def _backward_b(st, rel_bias, sinks):
    seg = st["seg"]
    dq_s, dg_swa, dkv, dsinks, drel = _swa_bwd(seg["q_s"], seg["kv"], seg["g_swa"], st["o_swa"], st["dys"][1],
                                               st["bucket"], rel_bias, _sink_column(sinks))
    st["dsinks"], st["drel"] = dsinks.reshape(1, SWA_HEADS), drel.reshape(REL_BUCKETS, SWA_HEADS)
    dproj = jnp.concatenate([st["dxr"], st["dg_rg"], dq_s, dkv.astype(BF), dg_swa, st["dq_m"], st["dg_mem"], *st["dgl"]], axis=1)
    st["dproj"] = jnp.transpose(dproj.reshape(dproj.shape[0], N_DEV, D_IN // N_DEV), (1, 0, 2))
    return st


def _dw_in_half(st, half, dep=None):
    S = st["h"].shape[0]
    return _matmul(st["h"], st["dproj"], "tn", D_MODEL // 2, D_IN, S, 512, D_IN // N_DEV, S, BF, "dw_in%d" % half, a_moff=2 * half,
                   b_blocked=True, out_blocked="col", dep=dep)


def _owner_blocks(a):
    return jnp.swapaxes(a.reshape((4, 2) + a.shape[1:]), 0, 1)


def _local_step(x, mem, target, pre_g, post_g, mem_g, w_in, conv_w, conv_b, w_a, b_a, w_x, b_x, lam, sinks, rel_bias,
                w_memkv, wbr, w_out):
    st = _forward_a(x, mem, pre_g, mem_g, w_in, conv_w, conv_b, w_a, b_a, w_x, b_x, lam, sinks, rel_bias)
    st = _forward_b(st, x, target, post_g, w_memkv, wbr, w_out)
    st = _backward_a1(st, wbr, w_out)
    st = _backward_a2(st, mem, w_memkv, conv_w, conv_b, w_a, b_a, w_x, b_x, lam)
    st = _backward_b(st, rel_bias, sinks)
    st["dw_in"] = [_dw_in_half(st, 0), _dw_in_half(st, 1)]
    st["grad_x"], st["dpre"] = _dh_dx(st["dproj"], w_in, x, st["dy"], pre_g, st["tm"])
    return st


def _pad_rows(a, rows):
    a = a.reshape(-1, 128) if a.shape[-1] % 128 == 0 else jnp.pad(a, ((0, 0), (0, 128 - a.shape[-1])))
    return jnp.pad(a, ((0, rows - a.shape[0]), (0, 0))) if a.shape[0] < rows else a


def kernel(x, mem, pre_norm_g, post_norm_g, mem_norm_g, w_in, conv_w, conv_b, w_rg_a, b_rg_a, w_rg_x, b_rg_x, lru_lambda, swa_sinks, rel_bias, w_mem_kv, w_br_rg, w_br_swa, w_br_mem, w_out, loss_target, m_pre_norm_g, m_post_norm_g, m_mem_norm_g, m_w_in, m_conv_w, m_conv_b, m_w_rg_a, m_b_rg_a, m_w_rg_x, m_b_rg_x, m_lru_lambda, m_swa_sinks, m_rel_bias, m_w_mem_kv, m_w_br_rg, m_w_br_swa, m_w_br_mem, m_w_out, v_pre_norm_g, v_post_norm_g, v_mem_norm_g, v_w_in, v_conv_w, v_conv_b, v_w_rg_a, v_b_rg_a, v_w_rg_x, v_b_rg_x, v_lru_lambda, v_swa_sinks, v_rel_bias, v_w_mem_kv, v_w_br_rg, v_w_br_swa, v_w_br_mem, v_w_out):
    cx, cy, cc = lax.axis_index("x"), lax.axis_index("y"), lax.axis_index("c")
    me = 4 * cx + 2 * cy + cc
    chip = 2 * cx + cy
    core = jnp.reshape(cc, (1,)).astype(jnp.int32)
    x0, mem0 = x[0], mem[0]
    w_a_b, w_x_b = w_rg_a[0].astype(BF), w_rg_x[0].astype(BF)

    def landing(own, slot, slots, kind="lead"):
        if kind == "cols":
            return lax.dynamic_update_slice(lax.empty((own.shape[0], slots * own.shape[1]), own.dtype), own, (0, slot * own.shape[1]))
        return lax.dynamic_update_slice(lax.empty((slots,) + own.shape, own.dtype), own[None], (slot,) + (0,) * own.ndim)

    def swap_start(parts, tag):
        return _exchange_start(parts, [lax.empty(p.shape[1:], p.dtype) for p in parts], _plan_swap(len(parts)), "swap_%s_start" % tag)

    def scatter_start(swap, after, tag):
        s_send, s_recv, parts, got, _ = swap
        got = _exchange_wait(s_send, s_recv, parts, got, _plan_swap(len(parts)), after, "swap_%s_wait" % tag)
        sums = [_pair_sum(p, g, core, "scatter_%s_sum%d" % (tag, i)) for i, (p, g) in enumerate(zip(parts, got))]
        lands = [landing(lax.dynamic_index_in_dim(s, chip, 0, keepdims=False), chip, 4) for s in sums]
        return _exchange_start(sums, lands, _plan_scatter(len(sums)), "scatter_%s_start" % tag)

    def zero_after(a):
        return jnp.minimum(jnp.abs(a.reshape(-1)[0].astype(F32)), 0.0)

    g_in, g_cw = _all_gather_relayed([jnp.transpose(w_in[0]).astype(BF), conv_w[0]], [True, False], "gather_w_in")
    w_in_f = g_in.reshape(D_IN, D_MODEL)
    conv_w_f = jnp.transpose(g_cw, (1, 0, 2)).reshape(CONV_W, D_RNN)

    after_first = zero_after(g_cw).astype(BF)
    rest = [w.astype(BF) + after_first for w in (w_mem_kv[0], w_br_rg[0], w_br_swa[0], w_br_mem[0], w_out[0])]
    kinds = ["lead", "cols", "cols", "cols", "lead"]
    plan_g = _plan_gather(kinds)
    g_send, g_recv, g_src, g_land, g_token = _exchange_start(rest, [landing(w, me, N_DEV, kd) for w, kd in zip(rest, kinds)], plan_g,
                                                             "gather_rest_start")
    st = _forward_a(x0, mem0, pre_norm_g + g_token[0:1, 0:1], mem_norm_g, w_in_f, conv_w_f, conv_b, w_a_b, b_rg_a, w_x_b, b_rg_x,
                    lru_lambda, swa_sinks, rel_bias)
    g_land = _exchange_wait(g_send, g_recv, g_src, g_land, plan_g, st["y_swa"], "gather_rest_wait")
    g_land = _forward_to_sibling(g_land, kinds, "gather_rest_forward")
    w_memkv_f = g_land[0].reshape(D_MODEL, 2 * D_MEM)
    wbr = (g_land[1], g_land[2], g_land[3])
    w_out_f = g_land[4].reshape(D_MODEL, D_MODEL)

    st = _forward_b(st, x0, loss_target[0], post_norm_g, w_memkv_f, wbr, w_out_f)
    st = _backward_a1(st, wbr, w_out_f)
    parts_a = [st["dw_out"], st["dwbr"][0], st["dwbr"][1], st["dwbr"][2]]
    plan_a = _plan_scatter(len(parts_a))
    swap_a = swap_start(parts_a, "a")
    st = _backward_a2(st, mem0, w_memkv_f, conv_w_f, conv_b + swap_a[4][0:1, 0:1], w_a_b, b_rg_a, w_x_b, b_rg_x, lru_lambda)
    a_send, a_recv, a_src, a_land, a_token = scatter_start(swap_a, st["dxr"], "a")
    parts_c = [st["dw_memkv"], _owner_blocks(st["dw_a"]), _owner_blocks(st["dw_x"])]
    plan_c = _plan_scatter(len(parts_c))
    swap_c = swap_start(parts_c, "c")

    st = _backward_b(st, rel_bias, swa_sinks + swap_c[4][0:1, 0:1] + a_token[0:1, 0:1])
    c_send, c_recv, c_src, c_land, c_token = scatter_start(swap_c, st["dproj"], "c")
    plan_b = _plan_scatter(1)

    def dw_in_parts(half, dep):
        dwh = _dw_in_half(st, half, dep)
        return dwh, [dwh]

    dw0, parts_b0 = dw_in_parts(0, c_token)
    swap_b0 = swap_start(parts_b0, "b0")
    a_land = _exchange_wait(a_send, a_recv, a_src, a_land, plan_a, swap_b0[4], "scatter_a_wait")
    big = [None] * 6

    def adamw_big(j, land, wt, mt, vt):
        big[j] = [a[None] for a in _adamw(land, wt[0], mt[0], vt[0], "adamw_big%d" % j)]

    adamw_big(5, a_land[0], w_out, m_w_out, v_w_out)
    adamw_big(2, a_land[1], w_br_rg, m_w_br_rg, v_w_br_rg)
    adamw_big(3, a_land[2], w_br_swa, m_w_br_swa, v_w_br_swa)
    halves = [scatter_start(swap_b0, big[3][1], "b0")]
    dw1, parts_b1 = dw_in_parts(1, halves[0][4])
    swap_b1 = swap_start(parts_b1, "b1")
    c_land = _exchange_wait(c_send, c_recv, c_src, c_land, plan_c, swap_b1[4], "scatter_c_wait")
    g_wa_blk = _sum_parts(c_land[1], "sum_w_rg_a")
    g_wx_blk = _sum_parts(c_land[2], "sum_w_rg_x")
    adamw_big(4, a_land[3], w_br_mem, m_w_br_mem, v_w_br_mem)
    adamw_big(1, c_land[0], w_mem_kv, m_w_mem_kv, v_w_mem_kv)
    halves.append(scatter_start(swap_b1, big[1][1], "b1"))
    grad_x, dpre = _dh_dx(st["dproj"], w_in_f, x0, st["dy"], pre_norm_g + halves[1][4][0:1, 0:1], st["tm"])
    after, b_lands = grad_x, []
    for half, (b_send, b_recv, b_src, b_land, _) in enumerate(halves):
        b_lands.append(_exchange_wait(b_send, b_recv, b_src, b_land, plan_b, after, "scatter_b%d_wait" % half)[0])
        after = b_lands[-1]
    big[0] = [a[None] for a in _adamw(b_lands, w_in[0], m_w_in[0], v_w_in[0], "adamw_big0")]
    links_free = jnp.minimum(jnp.abs(big[0][0][0, 0, 0]), 0.0)

    pack = jnp.concatenate([dpre.reshape(16, 128), st["dpost"].reshape(16, 128), st["dmem_g"].reshape(16, 128),
                            st["dvec"].reshape(64, 128), _pad_rows(st["dsinks"], 8), _pad_rows(st["drel"], 32), g_wa_blk, g_wx_blk], axis=0) + links_free
    gathered = _all_gather([pack], "gather_small")[0]
    gs = _sum_parts(gathered, "sum_small")
    g_pre, g_post, g_memg = gs[0:16].reshape(1, D_MODEL), gs[16:32].reshape(1, D_MODEL), gs[32:48].reshape(1, D_MODEL)
    gvec = gs[48:112].reshape(8, D_RNN)
    g_conv_w = lax.dynamic_slice(gvec[0:CONV_W], (0, me * RNN_BLOCK), (CONV_W, RNN_BLOCK))
    g_conv_b, g_b_a, g_b_x, g_lam = gvec[4:5], gvec[5:6], gvec[6:7], gvec[7:8]
    g_sinks = gs[112:113, :SWA_HEADS]
    g_rel = gs[120:152, :SWA_HEADS]
    g_w_a = gathered[:, 152:280]
    g_w_x = gathered[:, 280:408]

    def packed(ts):
        pre, post, memg, cb, ba, bx, lm, wa, wx, sk, rel, cw = ts
        return jnp.concatenate([pre.reshape(16, 128), post.reshape(16, 128), memg.reshape(16, 128), cb.reshape(8, 128),
                                ba.reshape(8, 128), bx.reshape(8, 128), lm.reshape(8, 128), wa.reshape(1024, 128),
                                wx.reshape(1024, 128), _pad_rows(sk.reshape(1, SWA_HEADS), 8), _pad_rows(rel, 32),
                                _pad_rows(cw.reshape(CONV_W, RNN_BLOCK), 8)], axis=0)

    def unpacked(a):
        return (a[0:16].reshape(1, D_MODEL), a[16:32].reshape(1, D_MODEL), a[32:48].reshape(1, D_MODEL), a[48:56].reshape(1, D_RNN),
                a[56:64].reshape(1, D_RNN), a[64:72].reshape(1, D_RNN), a[72:80].reshape(1, D_RNN),
                a[80:1104].reshape(1, RNN_BLOCKS, RNN_BLOCK, RNN_BLOCK), a[1104:2128].reshape(1, RNN_BLOCKS, RNN_BLOCK, RNN_BLOCK),
                a[2128:2129, :SWA_HEADS], a[2136:2168, :SWA_HEADS], a[2168:2172].reshape(1, CONV_W, RNN_BLOCK))

    g_small = (g_pre, g_post, g_memg, g_conv_b, g_b_a, g_b_x, g_lam, g_w_a, g_w_x, g_sinks, g_rel, g_conv_w)
    w_small = (pre_norm_g, post_norm_g, mem_norm_g, conv_b, b_rg_a, b_rg_x, lru_lambda, w_rg_a, w_rg_x, swa_sinks, rel_bias, conv_w)
    m_small = (m_pre_norm_g, m_post_norm_g, m_mem_norm_g, m_conv_b, m_b_rg_a, m_b_rg_x, m_lru_lambda, m_w_rg_a, m_w_rg_x, m_swa_sinks, m_rel_bias, m_conv_w)
    v_small = (v_pre_norm_g, v_post_norm_g, v_mem_norm_g, v_conv_b, v_b_rg_a, v_b_rg_x, v_lru_lambda, v_w_rg_a, v_w_rg_x, v_swa_sinks, v_rel_bias, v_conv_w)
    sm = [unpacked(a) for a in _adamw(packed(g_small)[None], packed(w_small), packed(m_small), packed(v_small), "adamw_small")]


    loss_total = lax.psum(st["loss"][0, 0], AXES)

    def leaves(k):
        s = sm[k]
        return [s[0], s[1], s[2], big[0][k], s[11], s[3], s[7], s[4], s[8], s[5], s[6], s[9], s[10],
                big[1][k], big[2][k], big[3][k], big[4][k], big[5][k]]

    return (loss_total, grad_x[None], *leaves(0), *leaves(1), *leaves(2), *leaves(3))
```

```python
import math

import jax
import jax.numpy as jnp
import numpy as np
from jax import lax
from jax.experimental import pallas as pl
from jax.experimental.pallas import tpu as pltpu

F32, BF = jnp.float32, jnp.bfloat16
MESH = pl.DeviceIdType.MESH
AXES = ("x", "y", "c")
N_DEV = 8

D_MODEL = 2048
D_RNN = 1024
RNN_BLOCKS = 8
RNN_BLOCK = 128
CONV_W = 4
LRU_C = 8.0
SWA_HEADS = 16
SWA_KV_HEADS = 2
SWA_HD = 64
WINDOW = 128
MEM_HEADS = 4
MEM_HD = 256
D_MEM = 1024
REL_BUCKETS = 32
REL_MAX_DIST = 128
EPS = 1e-6
NEG_INF = -1e30
D_IN = 12544
SEGMENTS = (("xr", 0, 1024, F32), ("g_rg", 1024, 1024, F32), ("q_s", 2048, 1024, BF), ("kv", 3072, 256, BF),
            ("g_swa", 3328, 1024, F32), ("q_m", 4352, 1024, BF), ("g_mem", 5376, 1024, F32), ("gl", 6400, 6144, F32))
SEG_TILE = 256

ADAM_LR, ADAM_B1, ADAM_B2, ADAM_EPS, ADAM_WD, ADAM_STEP = 0.001, 0.9, 0.999, 1e-08, 0.01, 10

NN = (((1,), (0,)), ((), ()))
NT = (((1,), (1,)), ((), ()))
TN = (((0,), (0,)), ((), ()))
MIB = 2 ** 20


def _dot(a, b, dn):
    return lax.dot_general(a, b, dn, preferred_element_type=F32)


def _params(sem, vmem_mib=48):
    return pltpu.CompilerParams(dimension_semantics=sem, vmem_limit_bytes=vmem_mib * MIB)


def _sigmoid(z):
    return 1.0 / (1.0 + jnp.exp(-z))


def _softplus(z):
    return jnp.maximum(z, 0.0) + jnp.log(1.0 + jnp.exp(-jnp.abs(z)))


def _expm1(z):
    p = z * (1.0 + z * (0.5 + z * (1.0 / 6 + z * (1.0 / 24 + z * (1.0 / 120 + z * (1.0 / 720 + z * (1.0 / 5040 + z / 40320)))))))
    return jnp.where(jnp.abs(z) < 0.3, p, jnp.exp(z) - 1.0)


def _flat(p):
    return 4 * p[0] + 2 * p[1] + p[2]


def _all_gather(arrs, name):
    n = len(arrs)

    def body(*refs):
        ins, outs = refs[:n], refs[n:2 * n]
        send_sems, recv_sems, local_sems = refs[2 * n:]
        x, y, c = lax.axis_index("x"), lax.axis_index("y"), lax.axis_index("c")
        me, sibling = (x, y, c), (x, y, 1 - c)
        chips = [(1 - x, y), (x, 1 - y), (1 - x, 1 - y)]

        def copy(a, k, block, to, src=None):
            dst = outs[a].at[_flat(block)]
            return pltpu.make_async_remote_copy(src_ref=dst if src is None else src, dst_ref=dst,
                                                send_sem=send_sems.at[a * 7 + k], recv_sem=recv_sems.at[a * 7 + k],
                                                device_id=to, device_id_type=MESH)

        mine = [pltpu.make_async_copy(ins[a], outs[a].at[_flat(me)], local_sems.at[a]) for a in range(n)]
        for cp in mine:
            cp.start()
        first = []
        for a in range(n):
            first += [copy(a, 1 + j, me, (*chip, c), src=ins[a]) for j, chip in enumerate(chips)]
            first.append(copy(a, 0, me, sibling, src=ins[a]))
        for cp in first:
            cp.start()
        passed = []
        for j, chip in enumerate(chips):
            for a in range(n):
                copy(a, 1 + j, (*chip, c), me).wait_recv()
                fw = copy(a, 4 + j, (*chip, c), sibling)
                fw.start()
                passed.append(fw)
        for a in range(n):
            copy(a, 0, sibling, me).wait_recv()
            for j, chip in enumerate(chips):
                copy(a, 4 + j, (*chip, 1 - c), me).wait_recv()
        for cp in first + passed:
            cp.wait_send()
        for cp in mine:
            cp.wait()

    any_spec = pl.BlockSpec(memory_space=pl.ANY)
    return pl.pallas_call(
        body, name=name,
        out_shape=[jax.ShapeDtypeStruct((N_DEV,) + a.shape, a.dtype) for a in arrs],
        in_specs=[any_spec] * n, out_specs=[any_spec] * n,
        scratch_shapes=[pltpu.SemaphoreType.DMA((7 * n,)), pltpu.SemaphoreType.DMA((7 * n,)), pltpu.SemaphoreType.DMA((n,))],
    )(*arrs)


def _all_gather_relayed(arrs, relay, name):
    n = len(arrs)
    K = 9

    def body(*refs):
        ins, outs = refs[:n], refs[n:2 * n]
        send_sems, recv_sems, local_sems = refs[2 * n:]
        x, y, c = lax.axis_index("x"), lax.axis_index("y"), lax.axis_index("c")
        me, sib = (x, y, c), (x, y, 1 - c)
        xn, yn, dg = (1 - x, y, c), (x, 1 - y, c), (1 - x, 1 - y, c)

        def other(p):
            return (p[0], p[1], 1 - p[2])

        def rows(a, half):
            h = arrs[a].shape[0] // 2
            return pl.ds(half * h, h)

        def copy(a, k, block, to, half=None, src=None):
            dst = outs[a].at[_flat(block)]
            if half is not None:
                dst = dst.at[rows(a, half)]
            return pltpu.make_async_remote_copy(src_ref=dst if src is None else src, dst_ref=dst,
                                                send_sem=send_sems.at[a * K + k], recv_sem=recv_sems.at[a * K + k],
                                                device_id=to, device_id_type=MESH)

        mine = [pltpu.make_async_copy(ins[a], outs[a].at[_flat(me)], local_sems.at[a]) for a in range(n)]
        for cp in mine:
            cp.start()
        sends = []

        def start(cp):
            cp.start()
            sends.append(cp)

        for a in range(n):
            start(copy(a, 1, me, xn, src=ins[a]))
            start(copy(a, 2, me, yn, src=ins[a]))
            if not relay[a]:
                start(copy(a, 3, me, dg, src=ins[a]))
            start(copy(a, 0, me, sib, src=ins[a]))
        for a in range(n):
            copy(a, 1, xn, me).wait_recv()
            if relay[a]:
                start(copy(a, 3, xn, yn, half=0))
            start(copy(a, 5, xn, sib))
        for a in range(n):
            copy(a, 2, yn, me).wait_recv()
            if relay[a]:
                start(copy(a, 4, yn, xn, half=1))
            start(copy(a, 6, yn, sib))
        for a in range(n):
            if relay[a]:
                copy(a, 3, dg, me, half=0).wait_recv()
                start(copy(a, 7, dg, sib, half=0))
                copy(a, 4, dg, me, half=1).wait_recv()
                start(copy(a, 8, dg, sib, half=1))
            else:
                copy(a, 3, dg, me).wait_recv()
                start(copy(a, 7, dg, sib))
        for a in range(n):
            copy(a, 0, sib, me).wait_recv()
            copy(a, 5, other(xn), me).wait_recv()
            copy(a, 6, other(yn), me).wait_recv()
            if relay[a]:
                copy(a, 7, other(dg), me, half=0).wait_recv()
                copy(a, 8, other(dg), me, half=1).wait_recv()
            else:
                copy(a, 7, other(dg), me).wait_recv()
        for cp in sends:
            cp.wait_send()
        for cp in mine:
            cp.wait()

    any_spec = pl.BlockSpec(memory_space=pl.ANY)
    return pl.pallas_call(
        body, name=name,
        out_shape=[jax.ShapeDtypeStruct((N_DEV,) + a.shape, a.dtype) for a in arrs],
        in_specs=[any_spec] * n, out_specs=[any_spec] * n,
        scratch_shapes=[pltpu.SemaphoreType.DMA((K * n,)), pltpu.SemaphoreType.DMA((K * n,)), pltpu.SemaphoreType.DMA((n,))],
    )(*arrs)


def _chip_peers(x, y):
    return [(1 - x, y), (x, 1 - y), (1 - x, 1 - y)]


def _chip(p):
    return 2 * p[0] + p[1]


def _plan_gather(kinds):
    def plan(x, y, c):
        out = []
        for a, kind in enumerate(kinds):
            for peer in [(x, y, 1 - c)] + [(*ch, c) for ch in _chip_peers(x, y)]:
                out.append((a, None, (kind, _flat((x, y, c))), peer, (kind, _flat(peer))))
        return out
    return plan


def _plan_swap(n):
    def plan(x, y, c):
        return [(a, 1 - c, ("all", 0), (x, y, 1 - c), ("all", 0)) for a in range(n)]
    return plan


def _slot(ref, where):
    kind, k = where
    if kind == "all":
        return ref
    if kind == "lead":
        return ref.at[k]
    return ref.at[:, pl.ds(pl.multiple_of(k * 256, 256), 256)]


def _plan_scatter(n):
    def plan(x, y, c):
        out = []
        for a in range(n):
            for ch in _chip_peers(x, y):
                out.append((a, _chip(ch), ("lead", _chip((x, y))), (*ch, c), ("lead", _chip(ch))))
        return out
    return plan


HBM_SPEC = pl.BlockSpec(memory_space=pltpu.HBM)
SEM_SPEC = pl.BlockSpec(memory_space=pltpu.SEMAPHORE)


def _in_hbm(a):
    return pltpu.with_memory_space_constraint(a, pltpu.HBM)


def _exchange_start(srcs, lands, plan, name):
    n = len(srcs)
    count = len(plan(0, 0, 0))

    def body(*refs):
        src_refs, land_refs = refs[:n], refs[n:2 * n]
        send_sems, recv_sems = refs[2 * n], refs[2 * n + 1]
        token = refs[-1]
        x, y, c = lax.axis_index("x"), lax.axis_index("y"), lax.axis_index("c")
        for k, (a, si, di, peer, _) in enumerate(plan(x, y, c)):
            src = src_refs[a] if si is None else src_refs[a].at[si]
            pltpu.make_async_remote_copy(src_ref=src, dst_ref=_slot(land_refs[a], di), send_sem=send_sems.at[k],
                                         recv_sem=recv_sems.at[k], device_id=peer, device_id_type=MESH).start()
        token[...] = jnp.zeros_like(token)

    out = pl.pallas_call(
        body, name=name,
        out_shape=(pltpu.SemaphoreType.DMA((count,)), pltpu.SemaphoreType.DMA((count,)),
                   *[pltpu.HBM(a.shape, a.dtype) for a in lands], jax.ShapeDtypeStruct((8, 128), F32)),
        in_specs=[HBM_SPEC] * (2 * n),
        out_specs=(SEM_SPEC, SEM_SPEC, *([HBM_SPEC] * n), pl.BlockSpec(memory_space=pltpu.VMEM)),
        input_output_aliases={n + i: 2 + i for i in range(n)},
        compiler_params=pltpu.CompilerParams(has_side_effects=pltpu.SideEffectType.DATAFLOW_SIDE_EFFECTING),
    )(*[_in_hbm(a) for a in srcs], *[_in_hbm(a) for a in lands])
    return out[0], out[1], list(srcs), list(out[2:2 + n]), out[-1]


def _exchange_wait(send_sems, recv_sems, srcs, lands, plan, after, name):
    n = len(srcs)

    def body(*refs):
        src_refs, land_refs = refs[:n], refs[n:2 * n]
        send_sems, recv_sems = refs[2 * n], refs[2 * n + 1]
        x, y, c = lax.axis_index("x"), lax.axis_index("y"), lax.axis_index("c")
        for k, (a, si, _, peer, ri) in enumerate(plan(x, y, c)):
            src = src_refs[a] if si is None else src_refs[a].at[si]
            cp = pltpu.make_async_remote_copy(src_ref=src, dst_ref=_slot(land_refs[a], ri), send_sem=send_sems.at[k],
                                              recv_sem=recv_sems.at[k], device_id=peer, device_id_type=MESH)
            cp.wait_send()
            cp.wait_recv()

    out = pl.pallas_call(
        body, name=name,
        out_shape=tuple(pltpu.HBM(a.shape, a.dtype) for a in lands),
        in_specs=[HBM_SPEC] * (2 * n) + [SEM_SPEC, SEM_SPEC, pl.BlockSpec(memory_space=pl.ANY)],
        out_specs=tuple([HBM_SPEC] * n),
        input_output_aliases={n + i: i for i in range(n)},
        compiler_params=pltpu.CompilerParams(has_side_effects=pltpu.SideEffectType.DATAFLOW_SIDE_EFFECTING),
    )(*[_in_hbm(a) for a in srcs], *lands, send_sems, recv_sems, after)
    return list(out)


def _forward_to_sibling(lands, kinds, owns, name):
    n = len(lands)

    def body(*refs):
        in_refs, own_refs, out_refs = refs[:n], refs[n:2 * n], refs[2 * n:3 * n]
        send_sems, recv_sems, local_sems = refs[3 * n:]
        x, y, c = lax.axis_index("x"), lax.axis_index("y"), lax.axis_index("c")
        sibling = (x, y, 1 - c)
        mine = [pltpu.make_async_copy(own_refs[a], _slot(out_refs[a], (kinds[a], _flat((x, y, c)))), local_sems.at[a]) for a in range(n)]
        for cp in mine:
            cp.start()

        def copy(a, j, slot):
            return pltpu.make_async_remote_copy(src_ref=_slot(in_refs[a], (kinds[a], slot)), dst_ref=_slot(out_refs[a], (kinds[a], slot)),
                                                send_sem=send_sems.at[a * 3 + j], recv_sem=recv_sems.at[a * 3 + j],
                                                device_id=sibling, device_id_type=MESH)

        sends = [copy(a, j, _flat((*ch, c))) for a in range(n) for j, ch in enumerate(_chip_peers(x, y))]
        for cp in sends:
            cp.start()
        for a in range(n):
            for j, ch in enumerate(_chip_peers(x, y)):
                copy(a, j, _flat((*ch, 1 - c))).wait_recv()
        for cp in sends:
            cp.wait_send()
        for cp in mine:
            cp.wait()

    any_spec = pl.BlockSpec(memory_space=pl.ANY)
    return pl.pallas_call(
        body, name=name, out_shape=[jax.ShapeDtypeStruct(a.shape, a.dtype) for a in lands],
        in_specs=[any_spec] * (2 * n), out_specs=[any_spec] * n, input_output_aliases={a: a for a in range(n)},
        scratch_shapes=[pltpu.SemaphoreType.DMA((3 * n,)), pltpu.SemaphoreType.DMA((3 * n,)), pltpu.SemaphoreType.DMA((n,))],
    )(*lands, *owns)


def _swap_with_sibling(parts, name):
    n = len(parts)

    def body(*refs):
        in_refs, out_refs = refs[:n], refs[n:2 * n]
        send_sems, recv_sems = refs[2 * n:]
        x, y, c = lax.axis_index("x"), lax.axis_index("y"), lax.axis_index("c")
        sends = [pltpu.make_async_remote_copy(src_ref=in_refs[a].at[1 - c], dst_ref=out_refs[a], send_sem=send_sems.at[a],
                                              recv_sem=recv_sems.at[a], device_id=(x, y, 1 - c), device_id_type=MESH)
                 for a in range(n)]
        for cp in sends:
            cp.start()
        for cp in sends:
            cp.wait()

    any_spec = pl.BlockSpec(memory_space=pl.ANY)
    return pl.pallas_call(
        body, name=name, out_shape=[jax.ShapeDtypeStruct(a.shape[1:], a.dtype) for a in parts],
        in_specs=[any_spec] * n, out_specs=[any_spec] * n,
        scratch_shapes=[pltpu.SemaphoreType.DMA((n,)), pltpu.SemaphoreType.DMA((n,))],
    )(*parts)


def _pair_sum(parts, got, core, name):
    _, _, R, C = parts.shape
    tr = 256 if R % 256 == 0 else R

    def body(c_ref, p_ref, g_ref, o_ref):
        o_ref[...] = (p_ref[...].astype(F32) + g_ref[...].astype(F32)).astype(o_ref.dtype)

    return pl.pallas_call(
        body, name=name,
        grid_spec=pltpu.PrefetchScalarGridSpec(
            num_scalar_prefetch=1, grid=(4, R // tr),
            in_specs=[pl.BlockSpec((None, None, tr, C), lambda j, i, c_ref: (c_ref[0], j, i, 0)),
                      pl.BlockSpec((None, tr, C), lambda j, i, c_ref: (j, i, 0))],
            out_specs=pl.BlockSpec((None, tr, C), lambda j, i, c_ref: (j, i, 0))),
        out_shape=jax.ShapeDtypeStruct((4, R, C), parts.dtype),
        compiler_params=_params(("parallel", "parallel")),
    )(core, parts, got)


def _matmul(a, b, mode, M, N, K, tm, tn, tk, out_dtype, name, b_noff=0, a_moff=0, a_koff=0, b_blocked=False, out_blocked=None,
            dep=None, addend=None, vmem_mib=48):
    nm, nn, nk = M // tm, N // tn, K // tk
    if mode == "nn":
        a_spec = pl.BlockSpec((tm, tk), lambda j, i, k: (i, k + a_koff))
        b_spec = pl.BlockSpec((tk, tn), lambda j, i, k: (k, j + b_noff))
        dn = NN
    elif mode == "nt":
        a_spec = pl.BlockSpec((tm, tk), lambda j, i, k: (i, k + a_koff))
        if b_blocked:
            b_spec = pl.BlockSpec((None, tn, tk), lambda j, i, k: (k, j, 0))
        else:
            b_spec = pl.BlockSpec((tn, tk), lambda j, i, k: (j + b_noff, k))
        dn = NT
    else:
        a_spec = pl.BlockSpec((tk, tm), lambda j, i, k: (k, i + a_moff))
        if b_blocked:
            b_spec = pl.BlockSpec((None, tk, tn), lambda j, i, k: (j, k, 0))
        else:
            b_spec = pl.BlockSpec((tk, tn), lambda j, i, k: (k, j + b_noff))
        dn = TN
    if out_blocked == "col":
        out_shape = jax.ShapeDtypeStruct((2, 4, M, tn), out_dtype)
        out_spec = pl.BlockSpec((None, None, tm, tn), lambda j, i, k: (j % 2, j // 2, i, 0))
    elif out_blocked == "row":
        out_shape = jax.ShapeDtypeStruct((2, 4, tm, N), out_dtype)
        out_spec = pl.BlockSpec((None, None, tm, tn), lambda j, i, k: (i % 2, i // 2, 0, j))
    elif out_blocked == "third":
        out_shape = jax.ShapeDtypeStruct((3, M, N // 3), out_dtype)
        out_spec = pl.BlockSpec((None, tm, tn), lambda j, i, k: (j // (nn // 3), i, j % (nn // 3)))
    else:
        out_shape = jax.ShapeDtypeStruct((M, N), out_dtype)
        out_spec = pl.BlockSpec((tm, tn), lambda j, i, k: (i, j))

    n_extra = (addend is not None) + (dep is not None)

    def body(a_ref, b_ref, *rest):
        o_ref, scratch = rest[n_extra], rest[n_extra + 1:]
        if nk == 1:
            prod = _dot(a_ref[...], b_ref[...], dn)
            if addend is not None:
                prod = prod + rest[0][...].astype(F32)
            o_ref[...] = prod.astype(out_dtype)
        else:
            assert addend is None
            acc_ref, = scratch
            k = pl.program_id(2)

            @pl.when(k == 0)
            def _():
                acc_ref[...] = jnp.zeros_like(acc_ref)

            acc_ref[...] += _dot(a_ref[...], b_ref[...], dn)

            @pl.when(k == nk - 1)
            def _():
                o_ref[...] = acc_ref[...].astype(out_dtype)

    return pl.pallas_call(
        body, name=name, grid=(nn, nm, nk),
        in_specs=[a_spec, b_spec] + ([] if addend is None else [out_spec])
        + ([] if dep is None else [pl.BlockSpec((8, 128), lambda j, i, k: (0, 0))]),
        out_specs=out_spec, out_shape=out_shape,
        scratch_shapes=[] if nk == 1 else [pltpu.VMEM((tm, tn), F32)],
        compiler_params=_params(("parallel", "parallel", "arbitrary"), vmem_mib),
    )(a, b, *([] if addend is None else [addend]), *([] if dep is None else [dep]))


def _rms_fwd(x, g, name):
    R, Dm = x.shape
    tr = min(R, 256)

    def body(x_ref, g_ref, h_ref):
        xv = x_ref[...]
        r = lax.rsqrt(jnp.mean(xv * xv, axis=-1, keepdims=True) + EPS)
        h_ref[...] = (xv * r * g_ref[...]).astype(BF)

    return pl.pallas_call(
        body, name=name, grid=(R // tr,),
        in_specs=[pl.BlockSpec((tr, Dm), lambda i: (i, 0)), pl.BlockSpec((1, Dm), lambda i: (0, 0))],
        out_specs=pl.BlockSpec((tr, Dm), lambda i: (i, 0)), out_shape=jax.ShapeDtypeStruct((R, Dm), BF),
        compiler_params=_params(("parallel",)),
    )(x, g)


def _rms_gain_grad(dn, x, name):
    R, Dm = x.shape

    def body(dn_ref, x_ref, o_ref):
        xv = x_ref[...]
        r = lax.rsqrt(jnp.mean(xv * xv, axis=-1, keepdims=True) + EPS)
        o_ref[...] = jnp.sum(dn_ref[...] * xv * r, axis=0, keepdims=True)

    return pl.pallas_call(
        body, name=name, out_shape=jax.ShapeDtypeStruct((1, Dm), F32),
        compiler_params=pltpu.CompilerParams(vmem_limit_bytes=32 * MIB),
    )(dn, x)


def _shift_down(v, k, head8, row, T):
    if k == 0:
        return v
    r = pltpu.roll(v, k, 0)
    hr = pltpu.roll(head8, k, 0)
    top = jnp.where(row[:8] < k, hr, r[:8])
    return jnp.concatenate([top, r[8:]], axis=0)


def _shift_up(v, k, tail8, row, T):
    if k == 0:
        return v
    r = pltpu.roll(v, T - k, 0)
    tr = pltpu.roll(tail8, 8 - k, 0)
    bot = jnp.where(row[:8] >= 8 - k, tr, r[T - 8:])
    return jnp.concatenate([r[:T - 8], bot], axis=0)


def _rglru_gates(u, head8, grow, row, T, cw_ref, cb_ref, wa_ref, ba_ref, wx_ref, bx_ref, lam_ref):
    us = [_shift_down(u, k, head8, row, T) for k in range(CONV_W)]
    acc = us[0] * cw_ref[0:1, :]
    for k in range(1, CONV_W):
        acc = acc + us[k] * cw_ref[k:k + 1, :]
    conv = cb_ref[...] + acc
    cbf = conv.astype(BF)
    r_ = _sigmoid(_dot(cbf, wa_ref[0], NN) + ba_ref[...])
    i_ = _sigmoid(_dot(cbf, wx_ref[0], NN) + bx_ref[...])
    sp = _softplus(-lam_ref[...])
    la = -LRU_C * r_ * sp
    a = jnp.exp(la)
    mult_raw = jnp.sqrt(-_expm1(2.0 * la))
    mult = jnp.where(grow == 0, 1.0, mult_raw)
    return us, conv, cbf, r_, i_, sp, a, mult_raw, mult


def _rglru_specs(T, nt, rev):
    tmap = (lambda n, t: (nt - 1 - t, n)) if rev else (lambda n, t: (t, n))
    hmap = ((lambda n, t: (jnp.maximum((nt - 1 - t) * (T // 8) - 1, 0), n)) if rev
            else (lambda n, t: (jnp.maximum(t * (T // 8) - 1, 0), n)))
    tile = pl.BlockSpec((T, RNN_BLOCK), tmap)
    halo = pl.BlockSpec((8, RNN_BLOCK), hmap)
    vec = pl.BlockSpec((1, RNN_BLOCK), lambda n, t: (0, n))
    cw = pl.BlockSpec((CONV_W, RNN_BLOCK), lambda n, t: (0, n))
    wblk = pl.BlockSpec((1, RNN_BLOCK, RNN_BLOCK), lambda n, t: (n, 0, 0))
    return tile, halo, vec, cw, wblk


def _rglru_fwd(xr, g, cw, cb, wa, ba, wx, bx, lam, T):
    S = xr.shape[0]
    nt = S // T

    def body(u_ref, uh_ref, g_ref, cw_ref, cb_ref, wa_ref, ba_ref, wx_ref, bx_ref, lam_ref, h_ref, y_ref, carry):
        t = pl.program_id(1)

        @pl.when(t == 0)
        def _():
            carry[...] = jnp.zeros_like(carry)

        row = lax.broadcasted_iota(jnp.int32, (T, RNN_BLOCK), 0)
        grow = row + t * T
        head8 = jnp.where(t > 0, uh_ref[...], 0.0)
        _, conv, _, _, i_, _, a, _, mult = _rglru_gates(u_ref[...], head8, grow, row, T, cw_ref, cb_ref, wa_ref, ba_ref,
                                                         wx_ref, bx_ref, lam_ref)
        b = mult * i_ * conv
        s = 1
        while s < T:
            keep = row >= s
            a_s = jnp.where(keep, pltpu.roll(a, s, 0), 1.0)
            b_s = jnp.where(keep, pltpu.roll(b, s, 0), 0.0)
            b = a * b_s + b
            a = a * a_s
            s *= 2
        h = b + a * carry[0:1, :]
        carry[...] = jnp.broadcast_to(h[T - 1:T, :], carry.shape)
        h_ref[...] = h
        gv = g_ref[...]
        y_ref[...] = (h * (gv * _sigmoid(gv))).astype(BF)

    tile, halo, vec, cwspec, wblk = _rglru_specs(T, nt, False)
    return pl.pallas_call(
        body, name="rglru_fwd", grid=(RNN_BLOCKS, nt),
        in_specs=[tile, halo, tile, cwspec, vec, wblk, vec, wblk, vec, vec],
        out_specs=[tile, tile],
        out_shape=[jax.ShapeDtypeStruct((S, D_RNN), F32), jax.ShapeDtypeStruct((S, D_RNN), BF)],
        scratch_shapes=[pltpu.VMEM((8, RNN_BLOCK), F32)],
        compiler_params=_params(("parallel", "arbitrary")),
    )(xr, xr, g, cw, cb, wa, ba, wx, bx, lam)


def _rglru_bwd(xr, g, h, dy, cw, cb, wa, ba, wx, bx, lam, T):
    S = xr.shape[0]
    nt = S // T

    def body(u_ref, uh_ref, g_ref, h_ref, hh_ref, dy_ref, cw_ref, cb_ref, wa_ref, ba_ref, wx_ref, bx_ref, lam_ref,
             du_ref, dg_ref, dwa_ref, dwx_ref, dvec_ref, c_dhh, c_a, c_dconv):
        t = pl.program_id(1)
        tt = nt - 1 - t

        @pl.when(t == 0)
        def _():
            c_dhh[...] = jnp.zeros_like(c_dhh)
            c_a[...] = jnp.zeros_like(c_a)
            c_dconv[...] = jnp.zeros_like(c_dconv)
            dwa_ref[...] = jnp.zeros_like(dwa_ref)
            dwx_ref[...] = jnp.zeros_like(dwx_ref)
            dvec_ref[...] = jnp.zeros_like(dvec_ref)

        row = lax.broadcasted_iota(jnp.int32, (T, RNN_BLOCK), 0)
        row8 = row[:8]
        grow = row + tt * T
        head8 = jnp.where(tt > 0, uh_ref[...], 0.0)
        us, conv, cbf, r_, i_, sp, a, mult_raw, mult = _rglru_gates(
            u_ref[...], head8, grow, row, T, cw_ref, cb_ref, wa_ref, ba_ref, wx_ref, bx_ref, lam_ref)
        hv = h_ref[...]
        hprev = _shift_down(hv, 1, jnp.where(tt > 0, hh_ref[...], 0.0), row, T)
        gv = g_ref[...]
        sg = _sigmoid(gv)
        dyv = dy_ref[...]
        dg_ref[...] = (dyv * hv * (sg * (1.0 + gv * (1.0 - sg)))).astype(BF)
        d = dyv * (gv * sg)
        A = _shift_up(a, 1, c_a[...], row, T)
        s = 1
        while s < T:
            keep = row < T - s
            A_s = jnp.where(keep, pltpu.roll(A, T - s, 0), 1.0)
            d_s = jnp.where(keep, pltpu.roll(d, T - s, 0), 0.0)
            d = A * d_s + d
            A = A * A_s
            s *= 2
        dhh = d + A * c_dhh[0:1, :]
        da = dhh * hprev
        dconv = dhh * mult * i_
        di = dhh * mult * conv
        dmult = dhh * i_ * conv
        dla = da * a - jnp.where(grow == 0, 0.0, dmult * (a * a) / mult_raw)
        dr = dla * (-LRU_C * sp)
        dsp = jnp.sum(dla * (-LRU_C * r_), axis=0, keepdims=True)
        dza = dr * r_ * (1.0 - r_)
        dzx = di * i_ * (1.0 - i_)
        dza_b, dzx_b = dza.astype(BF), dzx.astype(BF)
        dconv = dconv + _dot(dza_b, wa_ref[0], NT) + _dot(dzx_b, wx_ref[0], NT)
        dwa_ref[0] += _dot(cbf, dza_b, TN)
        dwx_ref[0] += _dot(cbf, dzx_b, TN)
        lam = lam_ref[...]
        rows = [jnp.sum(dconv * us[k], axis=0, keepdims=True) for k in range(CONV_W)]
        rows += [jnp.sum(dconv, axis=0, keepdims=True), jnp.sum(dza, axis=0, keepdims=True),
                 jnp.sum(dzx, axis=0, keepdims=True), dsp * (-_sigmoid(-lam))]
        upd = jnp.zeros((8, RNN_BLOCK), F32)
        for j, rv in enumerate(rows):
            upd = upd + jnp.where(row8 == j, rv, 0.0)
        dvec_ref[...] += upd
        tail8 = c_dconv[...]
        du = dconv * cw_ref[0:1, :]
        for k in range(1, CONV_W):
            du = du + _shift_up(dconv, k, tail8, row, T) * cw_ref[k:k + 1, :]
        du_ref[...] = du.astype(BF)
        c_dhh[...] = jnp.broadcast_to(dhh[0:1, :], c_dhh.shape)
        c_a[...] = jnp.broadcast_to(a[0:1, :], c_a.shape)
        c_dconv[...] = dconv[:8]

    tile, halo, vec, cwspec, wblk = _rglru_specs(T, nt, True)
    acc8 = pl.BlockSpec((8, RNN_BLOCK), lambda n, t: (0, n))
    return pl.pallas_call(
        body, name="rglru_bwd", grid=(RNN_BLOCKS, nt),
        in_specs=[tile, halo, tile, tile, halo, tile, cwspec, vec, wblk, vec, wblk, vec, vec],
        out_specs=[tile, tile, wblk, wblk, acc8],
        out_shape=[jax.ShapeDtypeStruct((S, D_RNN), BF), jax.ShapeDtypeStruct((S, D_RNN), BF),
                   jax.ShapeDtypeStruct((RNN_BLOCKS, RNN_BLOCK, RNN_BLOCK), F32),
                   jax.ShapeDtypeStruct((RNN_BLOCKS, RNN_BLOCK, RNN_BLOCK), F32),
                   jax.ShapeDtypeStruct((8, D_RNN), F32)],
        scratch_shapes=[pltpu.VMEM((8, RNN_BLOCK), F32)] * 3,
        compiler_params=_params(("parallel", "arbitrary")),
    )(xr, xr, g, h, h, dy, cw, cb, wa, ba, wx, bx, lam)


def _rel_bucket_map():
    qi = np.arange(WINDOW)[:, None]
    kj = np.arange(2 * WINDOW)[None, :]
    dist = jnp.asarray(qi + WINDOW - kj, jnp.int32)
    n = jnp.maximum(dist, 0)
    max_exact = REL_BUCKETS // 2
    ratio = jnp.log(jnp.maximum(n, 1).astype(F32) / max_exact) / math.log(REL_MAX_DIST / max_exact)
    large = jnp.minimum(max_exact + (ratio * (REL_BUCKETS - max_exact)).astype(jnp.int32), REL_BUCKETS - 1)
    bucket = jnp.where(n < max_exact, n, large).astype(jnp.int32)
    j = np.arange(WINDOW)[None, :]
    return jnp.where(jnp.asarray(j > qi), bucket[:, :WINDOW], bucket[:, WINDOW:])


def _swa_common(n, kv_ref, bucket_ref, relb_ref, bias_scr):
    @pl.when(n == 0)
    def _():
        bk = bucket_ref[...]
        for h in range(SWA_HEADS):
            acc = jnp.zeros((WINDOW, WINDOW), F32)
            for b in range(REL_BUCKETS):
                acc = acc + jnp.where(bk == b, relb_ref[b, h], 0.0)
            bias_scr[h] = acc

    prev0 = pl.multiple_of(jnp.maximum(n - 1, 0) * WINDOW, WINDOW)
    cur0 = pl.multiple_of(n * WINDOW, WINDOW)
    kk = jnp.concatenate([kv_ref[pl.ds(prev0, WINDOW), :], kv_ref[pl.ds(cur0, WINDOW), :]], axis=0).astype(F32)
    rowi = lax.broadcasted_iota(jnp.int32, (WINDOW, WINDOW), 0)
    col = lax.broadcasted_iota(jnp.int32, (WINDOW, WINDOW), 1)
    from_prev = col > rowi
    return kk, from_prev, prev0, cur0


def _fold(full, from_prev):
    return jnp.where(from_prev, full[:, :WINDOW], full[:, WINDOW:])


def _unfold(sq, from_prev):
    return jnp.concatenate([jnp.where(from_prev, sq, 0.0), jnp.where(from_prev, 0.0, sq)], axis=1)


def _half_pair(part, kvh):
    lo = lax.broadcasted_iota(jnp.int32, part.shape, 1) < SWA_HD
    if kvh == 0:
        pa = jnp.where(lo, part, 0.0)
        pb = pltpu.roll(pa, SWA_HD, 1)
    else:
        pb = jnp.where(lo, 0.0, part)
        pa = pltpu.roll(pb, SWA_HD, 1)
    return pa.astype(BF), pb.astype(BF)


ALL_HEADS = SWA_HEADS * WINDOW


def _sink_column(sinks):
    return jnp.repeat(sinks.reshape(SWA_HEADS), WINDOW).reshape(ALL_HEADS, 1)


def _swa_operands(kk):
    return [(_half_pair(kk[:, :128], kvh), _half_pair(kk[:, 128:], kvh)) for kvh in range(SWA_KV_HEADS)]


def _swa_probs(n, q_ref, ops, bias_scr, sinkc_ref, from_prev):
    lgs = []
    for kvh in range(SWA_KV_HEADS):
        (ka, kb), _ = ops[kvh]
        for p in range(4):
            q2 = q_ref[:, kvh * 512 + p * 128:kvh * 512 + p * 128 + 128]
            lgs += [_fold(_dot(q2, ka, NT), from_prev), _fold(_dot(q2, kb, NT), from_prev)]
    lg = jnp.concatenate(lgs, axis=0) * (SWA_HD ** -0.5) + bias_scr[...].reshape(ALL_HEADS, WINDOW)
    rowi = jnp.bitwise_and(lax.broadcasted_iota(jnp.int32, (ALL_HEADS, WINDOW), 0), WINDOW - 1)
    col = lax.broadcasted_iota(jnp.int32, (ALL_HEADS, WINDOW), 1)
    no_prev = jnp.where(n > 0, 0, 4 * WINDOW)
    lg = jnp.where(jnp.logical_or(col <= rowi, col > rowi + no_prev), lg, NEG_INF)
    sink = sinkc_ref[...]
    m = jnp.maximum(jnp.max(lg, axis=-1, keepdims=True), sink)
    e = jnp.exp(lg - m)
    es = jnp.exp(sink - m)
    den = jnp.sum(e, axis=-1, keepdims=True) + es
    return e / den, es / den


def _swa_fwd(q, kv, g, bucket, rel_bias, sink_col):
    S = q.shape[0]
    nb = S // WINDOW

    def body(q_ref, kv_ref, g_ref, bucket_ref, relb_ref, sinkc_ref, o_ref, y_ref, bias_scr):
        n = pl.program_id(0)
        kk, from_prev, _, _ = _swa_common(n, kv_ref, bucket_ref, relb_ref, bias_scr)
        ops = _swa_operands(kk)
        pr, _ = _swa_probs(n, q_ref, ops, bias_scr, sinkc_ref, from_prev)
        for kvh in range(SWA_KV_HEADS):
            _, (va, vb) = ops[kvh]
            for p in range(4):
                c0 = kvh * 512 + p * 128
                r0 = (kvh * 8 + 2 * p) * WINDOW
                o2 = (_dot(_unfold(pr[r0:r0 + WINDOW], from_prev).astype(BF), va, NN)
                      + _dot(_unfold(pr[r0 + WINDOW:r0 + 2 * WINDOW], from_prev).astype(BF), vb, NN))
                o_ref[:, c0:c0 + 128] = o2
                gv = g_ref[:, c0:c0 + 128]
                y_ref[:, c0:c0 + 128] = (o2 * (gv * _sigmoid(gv))).astype(BF)

    blk = pl.BlockSpec((WINDOW, 1024), lambda n: (n, 0))
    smem = pl.BlockSpec(memory_space=pltpu.SMEM)
    sinkc = pl.BlockSpec((ALL_HEADS, 1), lambda n: (0, 0))
    return pl.pallas_call(
        body, name="swa_fwd", grid=(nb,),
        in_specs=[blk, pl.BlockSpec((S, 256), lambda n: (0, 0)), blk, pl.BlockSpec((WINDOW, WINDOW), lambda n: (0, 0)), smem, sinkc],
        out_specs=[blk, blk],
        out_shape=[jax.ShapeDtypeStruct((S, 1024), F32), jax.ShapeDtypeStruct((S, 1024), BF)],
        scratch_shapes=[pltpu.VMEM((SWA_HEADS, WINDOW, WINDOW), F32)],
        compiler_params=_params(("arbitrary",)),
    )(q, kv, g, bucket, rel_bias, sink_col)


def _swa_bwd(q, kv, g, o, dy, bucket, rel_bias, sink_col):
    S = q.shape[0]
    nb = S // WINDOW

    def body(q_ref, kv_ref, g_ref, o_ref, dy_ref, bucket_ref, relb_ref, sinkc_ref,
             dq_ref, dg_ref, dkv_ref, dsink_ref, drel_ref, bias_scr, dbias_scr, dsink_scr):
        n = pl.program_id(0)

        @pl.when(n == 0)
        def _():
            dbias_scr[...] = jnp.zeros_like(dbias_scr)
            dsink_scr[...] = jnp.zeros_like(dsink_scr)
            dkv_ref[...] = jnp.zeros_like(dkv_ref)

        kk, from_prev, prev0, cur0 = _swa_common(n, kv_ref, bucket_ref, relb_ref, bias_scr)
        ops = _swa_operands(kk)
        pr, ps = _swa_probs(n, q_ref, ops, bias_scr, sinkc_ref, from_prev)
        do2s, dps = [], []
        for kvh in range(SWA_KV_HEADS):
            _, (va, vb) = ops[kvh]
            for p in range(4):
                c0 = kvh * 512 + p * 128
                gv = g_ref[:, c0:c0 + 128]
                sg = _sigmoid(gv)
                dyv = dy_ref[:, c0:c0 + 128]
                dg_ref[:, c0:c0 + 128] = (dyv * o_ref[:, c0:c0 + 128] * (sg * (1.0 + gv * (1.0 - sg)))).astype(BF)
                do2 = (dyv * (gv * sg)).astype(BF)
                do2s.append(do2)
                dps += [_fold(_dot(do2, va, NT), from_prev), _fold(_dot(do2, vb, NT), from_prev)]
        dp = jnp.concatenate(dps, axis=0)
        delta = jnp.sum(pr * dp, axis=-1, keepdims=True)
        ds = pr * (dp - delta)
        dbias_scr[...] += ds.reshape(SWA_HEADS, WINDOW, WINDOW)
        dsink_scr[...] += ps * delta
        dsc = ds * (SWA_HD ** -0.5)
        lo256 = lax.broadcasted_iota(jnp.int32, (2 * WINDOW, 128), 1) < SWA_HD
        dks, dvs = [], []
        for kvh in range(SWA_KV_HEADS):
            (ka, kb), _ = ops[kvh]
            dka = jnp.zeros((2 * WINDOW, 128), F32)
            dkb, dva, dvb = dka, dka, dka
            for p in range(4):
                c0 = kvh * 512 + p * 128
                r0 = (kvh * 8 + 2 * p) * WINDOW
                q2 = q_ref[:, c0:c0 + 128]
                do2 = do2s[kvh * 4 + p]
                ds0 = _unfold(dsc[r0:r0 + WINDOW], from_prev).astype(BF)
                ds1 = _unfold(dsc[r0 + WINDOW:r0 + 2 * WINDOW], from_prev).astype(BF)
                dq_ref[:, c0:c0 + 128] = (_dot(ds0, ka, NN) + _dot(ds1, kb, NN)).astype(BF)
                dka = dka + _dot(ds0, q2, TN)
                dkb = dkb + _dot(ds1, q2, TN)
                dva = dva + _dot(_unfold(pr[r0:r0 + WINDOW], from_prev).astype(BF), do2, TN)
                dvb = dvb + _dot(_unfold(pr[r0 + WINDOW:r0 + 2 * WINDOW], from_prev).astype(BF), do2, TN)
            dks.append(jnp.where(lo256, dka, 0.0) + pltpu.roll(jnp.where(lo256, 0.0, dkb), SWA_HD, 1))
            dvs.append(jnp.where(lo256, dva, 0.0) + pltpu.roll(jnp.where(lo256, 0.0, dvb), SWA_HD, 1))
        dk = dks[0] + pltpu.roll(dks[1], SWA_HD, 1)
        dv = dvs[0] + pltpu.roll(dvs[1], SWA_HD, 1)
        dkv_ref[pl.ds(prev0, WINDOW), 0:128] += dk[:WINDOW]
        dkv_ref[pl.ds(prev0, WINDOW), 128:256] += dv[:WINDOW]
        dkv_ref[pl.ds(cur0, WINDOW), 0:128] += dk[WINDOW:]
        dkv_ref[pl.ds(cur0, WINDOW), 128:256] += dv[WINDOW:]

        @pl.when(n == nb - 1)
        def _():
            dsink_ref[...] = -jnp.sum(dsink_scr[...].reshape(SWA_HEADS, WINDOW, 1), axis=1)
            bk = bucket_ref[...]
            sums = []
            for b in range(REL_BUCKETS):
                sums.append(jnp.sum(jnp.where((bk == b)[None], dbias_scr[...], 0.0), axis=1))
            drel_ref[...] = jnp.sum(jnp.concatenate(sums, axis=0), axis=1, keepdims=True)

    blk = pl.BlockSpec((WINDOW, 1024), lambda n: (n, 0))
    smem = pl.BlockSpec(memory_space=pltpu.SMEM)
    whole = lambda shape: pl.BlockSpec(shape, lambda n: (0, 0))
    return pl.pallas_call(
        body, name="swa_bwd", grid=(nb,),
        in_specs=[blk, whole((S, 256)), blk, blk, blk, whole((WINDOW, WINDOW)), smem, whole((ALL_HEADS, 1))],
        out_specs=[blk, blk, whole((S, 256)), whole((SWA_HEADS, 1)), whole((REL_BUCKETS * SWA_HEADS, 1))],
        out_shape=[jax.ShapeDtypeStruct((S, 1024), BF), jax.ShapeDtypeStruct((S, 1024), BF),
                   jax.ShapeDtypeStruct((S, 256), F32), jax.ShapeDtypeStruct((SWA_HEADS, 1), F32),
                   jax.ShapeDtypeStruct((REL_BUCKETS * SWA_HEADS, 1), F32)],
        scratch_shapes=[pltpu.VMEM((SWA_HEADS, WINDOW, WINDOW), F32), pltpu.VMEM((SWA_HEADS, WINDOW, WINDOW), F32),
                        pltpu.VMEM((ALL_HEADS, 1), F32)],
        compiler_params=_params(("arbitrary",)),
    )(q, kv, g, o, dy, bucket, rel_bias, sink_col)


def _mem_probs(qh, mk):
    lg = _dot(qh, mk, NT) * (MEM_HD ** -0.5)
    e = jnp.exp(lg - jnp.max(lg, axis=-1, keepdims=True))
    return e / jnp.sum(e, axis=-1, keepdims=True)


def _mem_fwd(q, mkv, g):
    S = q.shape[0]
    M = mkv.shape[0]
    tq = 256

    def body(q_ref, mkv_ref, g_ref, o_ref, y_ref):
        for h in range(MEM_HEADS):
            c0 = h * MEM_HD
            pr = _mem_probs(q_ref[:, c0:c0 + MEM_HD], mkv_ref[:, c0:c0 + MEM_HD])
            o = _dot(pr.astype(BF), mkv_ref[:, D_MEM + c0:D_MEM + c0 + MEM_HD], NN)
            o_ref[:, c0:c0 + MEM_HD] = o
            gv = g_ref[:, c0:c0 + MEM_HD]
            y_ref[:, c0:c0 + MEM_HD] = (o * (gv * _sigmoid(gv))).astype(BF)

    blk = pl.BlockSpec((tq, D_MEM), lambda i: (i, 0))
    return pl.pallas_call(
        body, name="mem_fwd", grid=(S // tq,),
        in_specs=[blk, pl.BlockSpec((M, 2 * D_MEM), lambda i: (0, 0)), blk], out_specs=[blk, blk],
        out_shape=[jax.ShapeDtypeStruct((S, D_MEM), F32), jax.ShapeDtypeStruct((S, D_MEM), BF)],
        compiler_params=_params(("parallel",)),
    )(q, mkv, g)


def _mem_bwd(q, mkv, g, o, dy):
    S = q.shape[0]
    M = mkv.shape[0]
    tq = 256

    def body(q_ref, mkv_ref, g_ref, o_ref, dy_ref, dq_ref, dg_ref, dmkv_ref):
        @pl.when(pl.program_id(0) == 0)
        def _():
            dmkv_ref[...] = jnp.zeros_like(dmkv_ref)

        for h in range(MEM_HEADS):
            c0 = h * MEM_HD
            qh = q_ref[:, c0:c0 + MEM_HD]
            mk = mkv_ref[:, c0:c0 + MEM_HD]
            mv = mkv_ref[:, D_MEM + c0:D_MEM + c0 + MEM_HD]
            gv = g_ref[:, c0:c0 + MEM_HD]
            sg = _sigmoid(gv)
            dyv = dy_ref[:, c0:c0 + MEM_HD]
            dg_ref[:, c0:c0 + MEM_HD] = (dyv * o_ref[:, c0:c0 + MEM_HD] * (sg * (1.0 + gv * (1.0 - sg)))).astype(BF)
            do = (dyv * (gv * sg)).astype(BF)
            pr = _mem_probs(qh, mk)
            dp = _dot(do, mv, NT)
            ds = pr * (dp - jnp.sum(pr * dp, axis=-1, keepdims=True))
            dsb = (ds * (MEM_HD ** -0.5)).astype(BF)
            dq_ref[:, c0:c0 + MEM_HD] = _dot(dsb, mk, NN).astype(BF)
            dmkv_ref[:, c0:c0 + MEM_HD] += _dot(dsb, qh, TN)
            dmkv_ref[:, D_MEM + c0:D_MEM + c0 + MEM_HD] += _dot(pr.astype(BF), do, TN)

    blk = pl.BlockSpec((tq, D_MEM), lambda i: (i, 0))
    whole = pl.BlockSpec((M, 2 * D_MEM), lambda i: (0, 0))
    return pl.pallas_call(
        body, name="mem_bwd", grid=(S // tq,),
        in_specs=[blk, whole, blk, blk, blk], out_specs=[blk, blk, whole],
        out_shape=[jax.ShapeDtypeStruct((S, D_MEM), BF), jax.ShapeDtypeStruct((S, D_MEM), BF),
                   jax.ShapeDtypeStruct((M, 2 * D_MEM), F32)],
        compiler_params=_params(("arbitrary",)),
    )(q, mkv, g, o, dy)


MERGE_TN = 512


def _merge_specs(tm):
    ytile = pl.BlockSpec((tm, 1024), lambda i, j: (i, 0))
    wblk = pl.BlockSpec((1024, MERGE_TN), lambda i, j: (0, j))
    gls = [pl.BlockSpec((None, tm, MERGE_TN), (lambda i, j, br=br: (br, i, j))) for br in range(3)]
    otile = pl.BlockSpec((tm, MERGE_TN), lambda i, j: (i, j))
    return ytile, wblk, gls, otile


def _merge_fwd(ys, ws, gl, tm):
    S = gl.shape[1]

    def body(y0, y1, y2, w0, w1, w2, g0, g1, g2, o_ref):
        acc = None
        for y_ref, w_ref, g_ref in ((y0, w0, g0), (y1, w1, g1), (y2, w2, g2)):
            term = _sigmoid(g_ref[...]) * _dot(y_ref[...], w_ref[...], NN)
            acc = term if acc is None else acc + term
        o_ref[...] = acc.astype(BF)

    ytile, wblk, gls, otile = _merge_specs(tm)
    return pl.pallas_call(
        body, name="merge_fwd", grid=(S // tm, D_MODEL // MERGE_TN),
        in_specs=[ytile] * 3 + [wblk] * 3 + gls, out_specs=otile,
        out_shape=jax.ShapeDtypeStruct((S, D_MODEL), BF),
        compiler_params=_params(("parallel", "arbitrary")),
    )(*ys, *ws, gl, gl, gl)


def _merge_bwd(dout, w_out, ys, ws, gl, tm):
    S = gl.shape[1]

    def body(do_ref, wo_ref, y0, y1, y2, w0, w1, w2, g0, g1, g2, dg0, dg1, dg2, dp0, dp1, dp2):
        dm = _dot(do_ref[...], wo_ref[...], NT)
        for y_ref, w_ref, g_ref, dg_ref, dp_ref in ((y0, w0, g0, dg0, dp0), (y1, w1, g1, dg1, dp1), (y2, w2, g2, dg2, dp2)):
            gate = _sigmoid(g_ref[...])
            pv = _dot(y_ref[...], w_ref[...], NN)
            dg_ref[...] = (dm * pv * gate * (1.0 - gate)).astype(BF)
            dp_ref[...] = (dm * gate).astype(BF)

    ytile, wblk, gls, otile = _merge_specs(tm)
    out = jax.ShapeDtypeStruct((S, D_MODEL), BF)
    return pl.pallas_call(
        body, name="merge_bwd", grid=(S // tm, D_MODEL // MERGE_TN),
        in_specs=[pl.BlockSpec((tm, D_MODEL), lambda i, j: (i, 0)), pl.BlockSpec((MERGE_TN, D_MODEL), lambda i, j: (j, 0))]
        + [ytile] * 3 + [wblk] * 3 + gls,
        out_specs=[otile] * 6, out_shape=[out] * 6,
        compiler_params=_params(("parallel", "arbitrary")),
    )(dout, w_out, *ys, *ws, gl, gl, gl)


def _out_loss(merged, w_out, x, target, post_g, tm):
    S = x.shape[0]

    def body(m_ref, w_ref, x_ref, t_ref, g_ref, dout_ref, dy_ref, loss_ref, dpost_ref):
        @pl.when(pl.program_id(0) == 0)
        def _():
            loss_ref[...] = jnp.zeros_like(loss_ref)
            dpost_ref[...] = jnp.zeros_like(dpost_ref)

        out = _dot(m_ref[...], w_ref[...], NN)
        r = lax.rsqrt(jnp.mean(out * out, axis=-1, keepdims=True) + EPS)
        nrm = out * r
        gv = g_ref[...]
        err = (x_ref[...] + nrm * gv) - t_ref[...]
        sq = jnp.sum(jnp.sum(err * err, axis=1, keepdims=True), axis=0, keepdims=True)
        loss_ref[...] += sq * (0.5 / D_MODEL)
        dy = err * (1.0 / D_MODEL)
        dy_ref[...] = dy
        dpost_ref[...] += jnp.sum(dy * nrm, axis=0, keepdims=True)
        dn = dy * gv
        dout_ref[...] = (r * (dn - nrm * jnp.mean(dn * nrm, axis=-1, keepdims=True))).astype(BF)

    row = pl.BlockSpec((tm, D_MODEL), lambda i: (i, 0))
    return pl.pallas_call(
        body, name="out_loss", grid=(S // tm,),
        in_specs=[row, pl.BlockSpec((D_MODEL, D_MODEL), lambda i: (0, 0)), row, row, pl.BlockSpec((1, D_MODEL), lambda i: (0, 0))],
        out_specs=[row, row, pl.BlockSpec((8, 128), lambda i: (0, 0)), pl.BlockSpec((1, D_MODEL), lambda i: (0, 0))],
        out_shape=[jax.ShapeDtypeStruct((S, D_MODEL), BF), jax.ShapeDtypeStruct((S, D_MODEL), F32),
                   jax.ShapeDtypeStruct((8, 128), F32), jax.ShapeDtypeStruct((1, D_MODEL), F32)],
        compiler_params=_params(("arbitrary",)),
    )(merged, w_out, x, target, post_g)


def _dh_dx(dproj, w_in, x, dy, pre_g, tm):
    S = x.shape[0]
    nk, tk = dproj.shape[0], dproj.shape[2]

    def body(dp_ref, w_ref, x_ref, dy_ref, g_ref, dx_ref, dpre_ref, acc_ref):
        i, k = pl.program_id(0), pl.program_id(1)

        @pl.when(jnp.logical_and(i == 0, k == 0))
        def _():
            dpre_ref[...] = jnp.zeros_like(dpre_ref)

        @pl.when(k == 0)
        def _():
            acc_ref[...] = jnp.zeros_like(acc_ref)

        acc_ref[...] += _dot(dp_ref[...], w_ref[...], NN)

        @pl.when(k == nk - 1)
        def _():
            dh = acc_ref[...]
            xv = x_ref[...]
            r = lax.rsqrt(jnp.mean(xv * xv, axis=-1, keepdims=True) + EPS)
            nrm = xv * r
            dpre_ref[...] += jnp.sum(dh * nrm, axis=0, keepdims=True)
            dn = dh * g_ref[...]
            dx_ref[...] = r * (dn - nrm * jnp.mean(dn * nrm, axis=-1, keepdims=True)) + dy_ref[...]

    row = pl.BlockSpec((tm, D_MODEL), lambda i, k: (i, 0))
    vec = pl.BlockSpec((1, D_MODEL), lambda i, k: (0, 0))
    return pl.pallas_call(
        body, name="dh_dx", grid=(S // tm, nk),
        in_specs=[pl.BlockSpec((None, tm, tk), lambda i, k: (k, i, 0)), pl.BlockSpec((tk, D_MODEL), lambda i, k: (k, 0)), row, row, vec],
        out_specs=[row, vec],
        out_shape=[jax.ShapeDtypeStruct((S, D_MODEL), F32), jax.ShapeDtypeStruct((1, D_MODEL), F32)],
        scratch_shapes=[pltpu.VMEM((tm, D_MODEL), F32)],
        compiler_params=_params(("arbitrary", "arbitrary"), 56),
    )(dproj, w_in, x, dy, pre_g)


def _sum_parts(parts, name):
    P, R, C = parts.shape
    tr = max(t for t in range(8, 513, 8) if R % t == 0)

    def body(p_ref, o_ref):
        acc = p_ref[0]
        for j in range(1, P):
            acc = acc + p_ref[j]
        o_ref[...] = acc

    return pl.pallas_call(
        body, name=name, grid=(R // tr,),
        in_specs=[pl.BlockSpec((P, tr, C), lambda i: (0, i, 0))], out_specs=pl.BlockSpec((tr, C), lambda i: (i, 0)),
        out_shape=jax.ShapeDtypeStruct((R, C), F32), compiler_params=_params(("parallel",)),
    )(parts)


def _adamw(parts, w, m, v, name, own=None, chip=None):
    groups = list(parts) if isinstance(parts, (list, tuple)) else [parts]
    owns = [] if own is None else (list(own) if isinstance(own, (list, tuple)) else [own])
    P, rows, C = groups[0].shape
    R = rows * len(groups)
    tr = 128 if rows % 128 == 0 else rows
    per = rows // tr
    c1 = 1.0 - ADAM_B1 ** ADAM_STEP
    c2 = 1.0 - ADAM_B2 ** ADAM_STEP
    n_in = len(groups) * (4 if owns else 1)

    def body(*refs):
        if owns:
            refs = refs[1:]
        p_refs = refs[:n_in]
        w_ref, m_ref, v_ref, g_ref, d_ref, nm_ref, nv_ref = refs[n_in:]
        g = None
        for q in range(len(groups)):
            if owns:
                gq = p_refs[4 * q + 3][...].astype(F32)
                for j in range(3):
                    gq = gq + p_refs[4 * q + j][...].astype(F32)
            else:
                gq = p_refs[q][0].astype(F32)
                for j in range(1, P):
                    gq = gq + p_refs[q][j].astype(F32)
            g = gq if g is None else jnp.where(pl.program_id(0) // per == q, gq, g)
        nm = ADAM_B1 * m_ref[...] + (1.0 - ADAM_B1) * g
        nv = ADAM_B2 * v_ref[...] + (1.0 - ADAM_B2) * (g * g)
        g_ref[...] = g
        nm_ref[...] = nm
        nv_ref[...] = nv
        d_ref[...] = -ADAM_LR * ((nm / c1) / (jnp.sqrt(nv / c2) + ADAM_EPS) + ADAM_WD * w_ref[...])

    out = jax.ShapeDtypeStruct((R, C), F32)
    if not owns:
        tile = pl.BlockSpec((tr, C), lambda i: (i, 0))
        return pl.pallas_call(
            body, name=name, grid=(R // tr,),
            in_specs=[pl.BlockSpec((P, tr, C), (lambda i, q=q: (0, jnp.clip(i - q * per, 0, per - 1), 0))) for q in range(len(groups))]
            + [tile, tile, tile], out_specs=[tile] * 4, out_shape=[out] * 4,
            compiler_params=_params(("parallel",)),
        )(*groups, w, m, v)
    tile = pl.BlockSpec((tr, C), lambda i, c_ref: (i, 0))
    specs, operands = [], []
    for q in range(len(groups)):
        for k in range(3):
            specs.append(pl.BlockSpec((None, tr, C), (lambda i, c_ref, q=q, k=k: (k + (c_ref[0] <= k).astype(jnp.int32),
                                                                                  jnp.clip(i - q * per, 0, per - 1), 0))))
            operands.append(groups[q])
        specs.append(pl.BlockSpec((None, tr, C), (lambda i, c_ref, q=q: (c_ref[0], jnp.clip(i - q * per, 0, per - 1), 0))))
        operands.append(owns[q])
    return pl.pallas_call(
        body, name=name,
        grid_spec=pltpu.PrefetchScalarGridSpec(num_scalar_prefetch=1, grid=(R // tr,), in_specs=specs + [tile, tile, tile],
                                               out_specs=[tile] * 4),
        out_shape=[out] * 4, compiler_params=_params(("parallel",)),
    )(chip, *operands, w, m, v)


def _project(h, w_t):
    S = h.shape[0]
    n_tiles = D_IN // SEG_TILE
    ranges = [(c0 // SEG_TILE, (c0 + width) // SEG_TILE) for _, c0, width, _ in SEGMENTS]

    def body(h_ref, w_ref, *outs):
        j = pl.program_id(0)
        prod = _dot(h_ref[...], w_ref[...], NT)
        for (j0, j1), (_, _, _, dt), o_ref in zip(ranges, SEGMENTS, outs):
            @pl.when(jnp.logical_and(j >= j0, j < j1))
            def _(o_ref=o_ref, dt=dt):
                o_ref[...] = prod.astype(dt)

    out_shapes, out_specs = [], []
    for (j0, j1), (name, _, width, dt) in zip(ranges, SEGMENTS):
        if name == "gl":
            per = (j1 - j0) // 3
            out_shapes.append(jax.ShapeDtypeStruct((3, S, width // 3), dt))
            out_specs.append(pl.BlockSpec((None, S, SEG_TILE), (lambda j, j0=j0, j1=j1, per=per: (
                jnp.clip(j - j0, 0, j1 - j0 - 1) // per, 0, jnp.clip(j - j0, 0, j1 - j0 - 1) % per))))
        else:
            out_shapes.append(jax.ShapeDtypeStruct((S, width), dt))
            out_specs.append(pl.BlockSpec((S, SEG_TILE), (lambda j, j0=j0, j1=j1: (0, jnp.clip(j - j0, 0, j1 - j0 - 1)))))
    outs = pl.pallas_call(
        body, name="proj", grid=(n_tiles,),
        in_specs=[pl.BlockSpec((S, D_MODEL), lambda j: (0, 0)), pl.BlockSpec((SEG_TILE, D_MODEL), lambda j: (j, 0))],
        out_specs=out_specs, out_shape=out_shapes,
        compiler_params=_params(("arbitrary",), 56),
    )(h, w_t)
    return {name: o for (name, _, _, _), o in zip(SEGMENTS, outs)}


def _forward_a(x, mem, pre_g, mem_g, w_in, conv_w, conv_b, w_a, b_a, w_x, b_x, lam, sinks, rel_bias):
    S = x.shape[0]
    st = dict(T=min(512, S // 2), tm=min(512, S), bucket=_rel_bucket_map())
    st["h"] = _rms_fwd(x, pre_g, "pre_norm")
    st["memn"] = _rms_fwd(mem, mem_g, "mem_norm")
    seg = st["seg"] = _project(st["h"], w_in)
    st["h_rg"], st["y_rg"] = _rglru_fwd(seg["xr"], seg["g_rg"], conv_w, conv_b, w_a, b_a, w_x, b_x, lam, st["T"])
    st["o_swa"], st["y_swa"] = _swa_fwd(seg["q_s"], seg["kv"], seg["g_swa"], st["bucket"], rel_bias, _sink_column(sinks))
    return st


def _forward_b(st, x, target, post_g, w_memkv, wbr, w_out):
    S = x.shape[0]
    M = st["memn"].shape[0]
    seg = st["seg"]
    st["mkv"] = _matmul(st["memn"], w_memkv, "nn", M, 2 * D_MEM, D_MODEL, M, 512, D_MODEL, BF, "mem_kv")
    st["o_mem"], st["y_mem"] = _mem_fwd(seg["q_m"], st["mkv"], seg["g_mem"])
    st["ys"] = (st["y_rg"], st["y_swa"], st["y_mem"])
    st["merged"] = _merge_fwd(st["ys"], wbr, seg["gl"], st["tm"])
    st["dout"], st["dy"], st["loss"], st["dpost"] = _out_loss(st["merged"], w_out, x, target, post_g, min(256, S))
    return st


def _backward_a1(st, wbr, w_out):
    S = st["h"].shape[0]
    seg, ys, tm = st["seg"], st["ys"], st["tm"]
    st["dw_out"] = _matmul(st["merged"], st["dout"], "tn", D_MODEL, D_MODEL, S, 256, D_MODEL, S, BF, "dw_out", out_blocked="row")
    dgl0, dgl1, dgl2, dp0, dp1, dp2 = _merge_bwd(st["dout"], w_out, ys, wbr, seg["gl"], tm)
    st["dgl"] = (dgl0, dgl1, dgl2)
    dys, dwbr = [], []
    for i, dp in enumerate((dp0, dp1, dp2)):
        dys.append(_matmul(dp, wbr[i], "nt", S, 1024, D_MODEL, tm, 1024, D_MODEL, F32, "dy_br%d" % i))
        dwbr.append(_matmul(ys[i], dp, "tn", 1024, D_MODEL, S, 1024, 256, S, BF, "dw_br%d" % i, out_blocked="col"))
    st["dys"], st["dwbr"] = dys, dwbr
    return st


def _backward_a2(st, mem, w_memkv, conv_w, conv_b, w_a, b_a, w_x, b_x, lam):
    M = mem.shape[0]
    seg, dys = st["seg"], st["dys"]
    st["dq_m"], st["dg_mem"], dmkv = _mem_bwd(seg["q_m"], st["mkv"], seg["g_mem"], st["o_mem"], dys[2])
    dmkv_b = dmkv.astype(BF)
    st["dw_memkv"] = _matmul(st["memn"], dmkv_b, "tn", D_MODEL, 2 * D_MEM, M, 256, 2 * D_MEM, M, BF, "dw_memkv", out_blocked="row")
    dmemn = _matmul(dmkv_b, w_memkv, "nt", M, D_MODEL, 2 * D_MEM, M, 512, 2 * D_MEM, F32, "dmemn")
    st["dmem_g"] = _rms_gain_grad(dmemn, mem, "dmem_gain")
    st["dxr"], st["dg_rg"], st["dw_a"], st["dw_x"], st["dvec"] = _rglru_bwd(
        seg["xr"], seg["g_rg"], st["h_rg"], dys[0], conv_w, conv_b, w_a, b_a, w_x, b_x, lam, st["T"])
    return st


def _backward_b(st, rel_bias, sinks):
    seg = st["seg"]
    dq_s, dg_swa, dkv, dsinks, drel = _swa_bwd(seg["q_s"], seg["kv"], seg["g_swa"], st["o_swa"], st["dys"][1],
                                               st["bucket"], rel_bias, _sink_column(sinks))
    st["dsinks"], st["drel"] = dsinks.reshape(1, SWA_HEADS), drel.reshape(REL_BUCKETS, SWA_HEADS)
    dproj = jnp.concatenate([st["dxr"], st["dg_rg"], dq_s, dkv.astype(BF), dg_swa, st["dq_m"], st["dg_mem"], *st["dgl"]], axis=1)
    st["dproj"] = jnp.transpose(dproj.reshape(dproj.shape[0], N_DEV, D_IN // N_DEV), (1, 0, 2))
    return st


def _dw_in_half(st, half, dep=None):
    S = st["h"].shape[0]
    return _matmul(st["h"], st["dproj"], "tn", D_MODEL // 2, D_IN, S, 512, D_IN // N_DEV, S, BF, "dw_in%d" % half, a_moff=2 * half,
                   b_blocked=True, out_blocked="col", dep=dep)


def _owner_blocks(a):
    return jnp.swapaxes(a.reshape((4, 2) + a.shape[1:]), 0, 1)


def _local_step(x, mem, target, pre_g, post_g, mem_g, w_in, conv_w, conv_b, w_a, b_a, w_x, b_x, lam, sinks, rel_bias,
                w_memkv, wbr, w_out):
    st = _forward_a(x, mem, pre_g, mem_g, w_in, conv_w, conv_b, w_a, b_a, w_x, b_x, lam, sinks, rel_bias)
    st = _forward_b(st, x, target, post_g, w_memkv, wbr, w_out)
    st = _backward_a1(st, wbr, w_out)
    st = _backward_a2(st, mem, w_memkv, conv_w, conv_b, w_a, b_a, w_x, b_x, lam)
    st = _backward_b(st, rel_bias, sinks)
    st["dw_in"] = [_dw_in_half(st, 0), _dw_in_half(st, 1)]
    st["grad_x"], st["dpre"] = _dh_dx(st["dproj"], w_in, x, st["dy"], pre_g, st["tm"])
    return st


def _pad_rows(a, rows):
    a = a.reshape(-1, 128) if a.shape[-1] % 128 == 0 else jnp.pad(a, ((0, 0), (0, 128 - a.shape[-1])))
    return jnp.pad(a, ((0, rows - a.shape[0]), (0, 0))) if a.shape[0] < rows else a


def kernel(x, mem, pre_norm_g, post_norm_g, mem_norm_g, w_in, conv_w, conv_b, w_rg_a, b_rg_a, w_rg_x, b_rg_x, lru_lambda, swa_sinks, rel_bias, w_mem_kv, w_br_rg, w_br_swa, w_br_mem, w_out, loss_target, m_pre_norm_g, m_post_norm_g, m_mem_norm_g, m_w_in, m_conv_w, m_conv_b, m_w_rg_a, m_b_rg_a, m_w_rg_x, m_b_rg_x, m_lru_lambda, m_swa_sinks, m_rel_bias, m_w_mem_kv, m_w_br_rg, m_w_br_swa, m_w_br_mem, m_w_out, v_pre_norm_g, v_post_norm_g, v_mem_norm_g, v_w_in, v_conv_w, v_conv_b, v_w_rg_a, v_b_rg_a, v_w_rg_x, v_b_rg_x, v_lru_lambda, v_swa_sinks, v_rel_bias, v_w_mem_kv, v_w_br_rg, v_w_br_swa, v_w_br_mem, v_w_out):
    cx, cy, cc = lax.axis_index("x"), lax.axis_index("y"), lax.axis_index("c")
    me = 4 * cx + 2 * cy + cc
    chip = 2 * cx + cy
    core = jnp.reshape(cc, (1,)).astype(jnp.int32)
    x0, mem0 = x[0], mem[0]
    w_a_b, w_x_b = w_rg_a[0].astype(BF), w_rg_x[0].astype(BF)

    def landing(own, slot, slots):
        return lax.dynamic_update_slice(lax.empty((slots,) + own.shape, own.dtype), own[None], (slot,) + (0,) * own.ndim)

    def gather_zone(own, kind):
        shape = (own.shape[0], N_DEV * own.shape[1]) if kind == "cols" else (N_DEV,) + own.shape
        return lax.empty(shape, own.dtype)

    def swap_start(parts, tag):
        return _exchange_start(parts, [lax.empty(p.shape[1:], p.dtype) for p in parts], _plan_swap(len(parts)), "swap_%s_start" % tag)

    def scatter_start(swap, after, tag, prefill=()):
        s_send, s_recv, parts, got, _ = swap
        got = _exchange_wait(s_send, s_recv, parts, got, _plan_swap(len(parts)), after, "swap_%s_wait" % tag)
        sums = [_pair_sum(p, g, core, "scatter_%s_sum%d" % (tag, i)) for i, (p, g) in enumerate(zip(parts, got))]
        lands = [landing(lax.dynamic_index_in_dim(s, chip, 0, keepdims=False), chip, 4) if i in prefill
                 else lax.empty(s.shape, s.dtype) for i, s in enumerate(sums)]
        return _exchange_start(sums, lands, _plan_scatter(len(sums)), "scatter_%s_start" % tag)

    def zero_after(a):
        return jnp.minimum(jnp.abs(a.reshape(-1)[0].astype(F32)), 0.0)

    g_in, g_cw = _all_gather_relayed([jnp.transpose(w_in[0]).astype(BF), conv_w[0]], [True, False], "gather_w_in")
    w_in_f = g_in.reshape(D_IN, D_MODEL)
    conv_w_f = jnp.transpose(g_cw, (1, 0, 2)).reshape(CONV_W, D_RNN)

    after_first = zero_after(g_cw).astype(BF)
    rest = [w.astype(BF) + after_first for w in (w_mem_kv[0], w_br_rg[0], w_br_swa[0], w_br_mem[0], w_out[0])]
    kinds = ["lead", "cols", "cols", "cols", "lead"]
    plan_g = _plan_gather(kinds)
    g_send, g_recv, g_src, g_land, g_token = _exchange_start(rest, [gather_zone(w, kd) for w, kd in zip(rest, kinds)], plan_g,
                                                             "gather_rest_start")
    st = _forward_a(x0, mem0, pre_norm_g + g_token[0:1, 0:1], mem_norm_g, w_in_f, conv_w_f, conv_b, w_a_b, b_rg_a, w_x_b, b_rg_x,
                    lru_lambda, swa_sinks, rel_bias)
    g_land = _exchange_wait(g_send, g_recv, g_src, g_land, plan_g, st["y_swa"], "gather_rest_wait")
    g_land = _forward_to_sibling(g_land, kinds, rest, "gather_rest_forward")
    w_memkv_f = g_land[0].reshape(D_MODEL, 2 * D_MEM)
    wbr = (g_land[1], g_land[2], g_land[3])
    w_out_f = g_land[4].reshape(D_MODEL, D_MODEL)

    st = _forward_b(st, x0, loss_target[0], post_norm_g, w_memkv_f, wbr, w_out_f)
    st = _backward_a1(st, wbr, w_out_f)
    parts_a = [st["dw_out"], st["dwbr"][0], st["dwbr"][1], st["dwbr"][2]]
    plan_a = _plan_scatter(len(parts_a))
    swap_a = swap_start(parts_a, "a")
    st = _backward_a2(st, mem0, w_memkv_f, conv_w_f, conv_b + swap_a[4][0:1, 0:1], w_a_b, b_rg_a, w_x_b, b_rg_x, lru_lambda)
    a_send, a_recv, a_src, a_land, a_token = scatter_start(swap_a, st["dxr"], "a")
    parts_c = [st["dw_memkv"], _owner_blocks(st["dw_a"]), _owner_blocks(st["dw_x"])]
    plan_c = _plan_scatter(len(parts_c))
    swap_c = swap_start(parts_c, "c")

    st = _backward_b(st, rel_bias, swa_sinks + swap_c[4][0:1, 0:1] + a_token[0:1, 0:1])
    c_send, c_recv, c_src, c_land, c_token = scatter_start(swap_c, st["dproj"], "c", prefill=(1, 2))
    plan_b = _plan_scatter(1)

    def dw_in_parts(half, dep):
        dwh = _dw_in_half(st, half, dep)
        return dwh, [dwh]

    dw0, parts_b0 = dw_in_parts(0, c_token)
    swap_b0 = swap_start(parts_b0, "b0")
    a_land = _exchange_wait(a_send, a_recv, a_src, a_land, plan_a, swap_b0[4], "scatter_a_wait")
    big = [None] * 6

    chip1 = jnp.reshape(chip, (1,)).astype(jnp.int32)

    def adamw_big(j, land, own, wt, mt, vt):
        big[j] = [a[None] for a in _adamw(land, wt[0], mt[0], vt[0], "adamw_big%d" % j, own=own, chip=chip1)]

    adamw_big(5, a_land[0], a_src[0], w_out, m_w_out, v_w_out)
    adamw_big(2, a_land[1], a_src[1], w_br_rg, m_w_br_rg, v_w_br_rg)
    adamw_big(3, a_land[2], a_src[2], w_br_swa, m_w_br_swa, v_w_br_swa)
    halves = [scatter_start(swap_b0, big[3][1], "b0")]
    dw1, parts_b1 = dw_in_parts(1, halves[0][4])
    swap_b1 = swap_start(parts_b1, "b1")
    c_land = _exchange_wait(c_send, c_recv, c_src, c_land, plan_c, swap_b1[4], "scatter_c_wait")
    g_wa_blk = _sum_parts(c_land[1], "sum_w_rg_a")
    g_wx_blk = _sum_parts(c_land[2], "sum_w_rg_x")
    adamw_big(4, a_land[3], a_src[3], w_br_mem, m_w_br_mem, v_w_br_mem)
    adamw_big(1, c_land[0], c_src[0], w_mem_kv, m_w_mem_kv, v_w_mem_kv)
    halves.append(scatter_start(swap_b1, big[1][1], "b1"))
    grad_x, dpre = _dh_dx(st["dproj"], w_in_f, x0, st["dy"], pre_norm_g + halves[1][4][0:1, 0:1], st["tm"])
    after, b_lands, b_sums = grad_x, [], []
    for half, (b_send, b_recv, b_src, b_land, _) in enumerate(halves):
        b_lands.append(_exchange_wait(b_send, b_recv, b_src, b_land, plan_b, after, "scatter_b%d_wait" % half)[0])
        b_sums.append(b_src[0])
        after = b_lands[-1]
    big[0] = [a[None] for a in _adamw(b_lands, w_in[0], m_w_in[0], v_w_in[0], "adamw_big0", own=b_sums, chip=chip1)]
    links_free = jnp.minimum(jnp.abs(big[0][0][0, 0, 0]), 0.0)

    pack = jnp.concatenate([dpre.reshape(16, 128), st["dpost"].reshape(16, 128), st["dmem_g"].reshape(16, 128),
                            st["dvec"].reshape(64, 128), _pad_rows(st["dsinks"], 8), _pad_rows(st["drel"], 32), g_wa_blk, g_wx_blk], axis=0) + links_free
    gathered = _all_gather([pack], "gather_small")[0]
    gs = _sum_parts(gathered, "sum_small")
    g_pre, g_post, g_memg = gs[0:16].reshape(1, D_MODEL), gs[16:32].reshape(1, D_MODEL), gs[32:48].reshape(1, D_MODEL)
    gvec = gs[48:112].reshape(8, D_RNN)
    g_conv_w = lax.dynamic_slice(gvec[0:CONV_W], (0, me * RNN_BLOCK), (CONV_W, RNN_BLOCK))
    g_conv_b, g_b_a, g_b_x, g_lam = gvec[4:5], gvec[5:6], gvec[6:7], gvec[7:8]
    g_sinks = gs[112:113, :SWA_HEADS]
    g_rel = gs[120:152, :SWA_HEADS]
    g_w_a = gathered[:, 152:280]
    g_w_x = gathered[:, 280:408]

    def packed(ts):
        pre, post, memg, cb, ba, bx, lm, wa, wx, sk, rel, cw = ts
        return jnp.concatenate([pre.reshape(16, 128), post.reshape(16, 128), memg.reshape(16, 128), cb.reshape(8, 128),
                                ba.reshape(8, 128), bx.reshape(8, 128), lm.reshape(8, 128), wa.reshape(1024, 128),
                                wx.reshape(1024, 128), _pad_rows(sk.reshape(1, SWA_HEADS), 8), _pad_rows(rel, 32),
                                _pad_rows(cw.reshape(CONV_W, RNN_BLOCK), 8)], axis=0)

    def unpacked(a):
        return (a[0:16].reshape(1, D_MODEL), a[16:32].reshape(1, D_MODEL), a[32:48].reshape(1, D_MODEL), a[48:56].reshape(1, D_RNN),
                a[56:64].reshape(1, D_RNN), a[64:72].reshape(1, D_RNN), a[72:80].reshape(1, D_RNN),
                a[80:1104].reshape(1, RNN_BLOCKS, RNN_BLOCK, RNN_BLOCK), a[1104:2128].reshape(1, RNN_BLOCKS, RNN_BLOCK, RNN_BLOCK),
                a[2128:2129, :SWA_HEADS], a[2136:2168, :SWA_HEADS], a[2168:2172].reshape(1, CONV_W, RNN_BLOCK))

    g_small = (g_pre, g_post, g_memg, g_conv_b, g_b_a, g_b_x, g_lam, g_w_a, g_w_x, g_sinks, g_rel, g_conv_w)
    w_small = (pre_norm_g, post_norm_g, mem_norm_g, conv_b, b_rg_a, b_rg_x, lru_lambda, w_rg_a, w_rg_x, swa_sinks, rel_bias, conv_w)
    m_small = (m_pre_norm_g, m_post_norm_g, m_mem_norm_g, m_conv_b, m_b_rg_a, m_b_rg_x, m_lru_lambda, m_w_rg_a, m_w_rg_x, m_swa_sinks, m_rel_bias, m_conv_w)
    v_small = (v_pre_norm_g, v_post_norm_g, v_mem_norm_g, v_conv_b, v_b_rg_a, v_b_rg_x, v_lru_lambda, v_w_rg_a, v_w_rg_x, v_swa_sinks, v_rel_bias, v_conv_w)
    sm = [unpacked(a) for a in _adamw(packed(g_small)[None], packed(w_small), packed(m_small), packed(v_small), "adamw_small")]


    loss_total = lax.psum(st["loss"][0, 0], AXES)

    def leaves(k):
        s = sm[k]
        return [s[0], s[1], s[2], big[0][k], s[11], s[3], s[7], s[4], s[8], s[5], s[6], s[9], s[10],
                big[1][k], big[2][k], big[3][k], big[4][k], big[5][k]]

    return (loss_total, grad_x[None], *leaves(0), *leaves(1), *leaves(2), *leaves(3))
```

```python
import math

import jax
import jax.numpy as jnp
import numpy as np
from jax import lax
from jax.experimental import pallas as pl
from jax.experimental.pallas import tpu as pltpu

F32, BF = jnp.float32, jnp.bfloat16
MESH = pl.DeviceIdType.MESH
AXES = ("x", "y", "c")
N_DEV = 8

D_MODEL = 2048
D_RNN = 1024
RNN_BLOCKS = 8
RNN_BLOCK = 128
CONV_W = 4
LRU_C = 8.0
SWA_HEADS = 16
SWA_KV_HEADS = 2
SWA_HD = 64
WINDOW = 128
MEM_HEADS = 4
MEM_HD = 256
D_MEM = 1024
REL_BUCKETS = 32
REL_MAX_DIST = 128
EPS = 1e-6
NEG_INF = -1e30
D_IN = 12544
SEGMENTS = (("xr", 0, 1024, F32), ("g_rg", 1024, 1024, F32), ("q_s", 2048, 1024, BF), ("kv", 3072, 256, BF),
            ("g_swa", 3328, 1024, F32), ("q_m", 4352, 1024, BF), ("g_mem", 5376, 1024, F32), ("gl", 6400, 6144, F32))
SEG_TILE = 256

ADAM_LR, ADAM_B1, ADAM_B2, ADAM_EPS, ADAM_WD, ADAM_STEP = 0.001, 0.9, 0.999, 1e-08, 0.01, 10

NN = (((1,), (0,)), ((), ()))
NT = (((1,), (1,)), ((), ()))
TN = (((0,), (0,)), ((), ()))
MIB = 2 ** 20


def _dot(a, b, dn):
    return lax.dot_general(a, b, dn, preferred_element_type=F32)


def _params(sem, vmem_mib=48):
    return pltpu.CompilerParams(dimension_semantics=sem, vmem_limit_bytes=vmem_mib * MIB)


def _sigmoid(z):
    return 1.0 / (1.0 + jnp.exp(-z))


def _softplus(z):
    return jnp.maximum(z, 0.0) + jnp.log(1.0 + jnp.exp(-jnp.abs(z)))


def _expm1(z):
    p = z * (1.0 + z * (0.5 + z * (1.0 / 6 + z * (1.0 / 24 + z * (1.0 / 120 + z * (1.0 / 720 + z * (1.0 / 5040 + z / 40320)))))))
    return jnp.where(jnp.abs(z) < 0.3, p, jnp.exp(z) - 1.0)


def _flat(p):
    return 4 * p[0] + 2 * p[1] + p[2]


def _all_gather(arrs, name):
    n = len(arrs)

    def body(*refs):
        ins, outs = refs[:n], refs[n:2 * n]
        send_sems, recv_sems, local_sems = refs[2 * n:]
        x, y, c = lax.axis_index("x"), lax.axis_index("y"), lax.axis_index("c")
        me, sibling = (x, y, c), (x, y, 1 - c)
        chips = [(1 - x, y), (x, 1 - y), (1 - x, 1 - y)]

        def copy(a, k, block, to, src=None):
            dst = outs[a].at[_flat(block)]
            return pltpu.make_async_remote_copy(src_ref=dst if src is None else src, dst_ref=dst,
                                                send_sem=send_sems.at[a * 7 + k], recv_sem=recv_sems.at[a * 7 + k],
                                                device_id=to, device_id_type=MESH)

        mine = [pltpu.make_async_copy(ins[a], outs[a].at[_flat(me)], local_sems.at[a]) for a in range(n)]
        for cp in mine:
            cp.start()
        first = []
        for a in range(n):
            first += [copy(a, 1 + j, me, (*chip, c), src=ins[a]) for j, chip in enumerate(chips)]
            first.append(copy(a, 0, me, sibling, src=ins[a]))
        for cp in first:
            cp.start()
        passed = []
        for j, chip in enumerate(chips):
            for a in range(n):
                copy(a, 1 + j, (*chip, c), me).wait_recv()
                fw = copy(a, 4 + j, (*chip, c), sibling)
                fw.start()
                passed.append(fw)
        for a in range(n):
            copy(a, 0, sibling, me).wait_recv()
            for j, chip in enumerate(chips):
                copy(a, 4 + j, (*chip, 1 - c), me).wait_recv()
        for cp in first + passed:
            cp.wait_send()
        for cp in mine:
            cp.wait()

    any_spec = pl.BlockSpec(memory_space=pl.ANY)
    return pl.pallas_call(
        body, name=name,
        out_shape=[jax.ShapeDtypeStruct((N_DEV,) + a.shape, a.dtype) for a in arrs],
        in_specs=[any_spec] * n, out_specs=[any_spec] * n,
        scratch_shapes=[pltpu.SemaphoreType.DMA((7 * n,)), pltpu.SemaphoreType.DMA((7 * n,)), pltpu.SemaphoreType.DMA((n,))],
    )(*arrs)


def _all_gather_relayed(arrs, relay, name):
    n = len(arrs)
    K = 9

    def body(*refs):
        ins, outs = refs[:n], refs[n:2 * n]
        send_sems, recv_sems, local_sems = refs[2 * n:]
        x, y, c = lax.axis_index("x"), lax.axis_index("y"), lax.axis_index("c")
        me, sib = (x, y, c), (x, y, 1 - c)
        xn, yn, dg = (1 - x, y, c), (x, 1 - y, c), (1 - x, 1 - y, c)

        def other(p):
            return (p[0], p[1], 1 - p[2])

        def rows(a, half):
            h = arrs[a].shape[0] // 2
            return pl.ds(half * h, h)

        def copy(a, k, block, to, half=None, src=None):
            dst = outs[a].at[_flat(block)]
            if half is not None:
                dst = dst.at[rows(a, half)]
            return pltpu.make_async_remote_copy(src_ref=dst if src is None else src, dst_ref=dst,
                                                send_sem=send_sems.at[a * K + k], recv_sem=recv_sems.at[a * K + k],
                                                device_id=to, device_id_type=MESH)

        mine = [pltpu.make_async_copy(ins[a], outs[a].at[_flat(me)], local_sems.at[a]) for a in range(n)]
        for cp in mine:
            cp.start()
        sends = []

        def start(cp):
            cp.start()
            sends.append(cp)

        for a in range(n):
            start(copy(a, 1, me, xn, src=ins[a]))
            start(copy(a, 2, me, yn, src=ins[a]))
            if not relay[a]:
                start(copy(a, 3, me, dg, src=ins[a]))
            start(copy(a, 0, me, sib, src=ins[a]))
        for a in range(n):
            copy(a, 1, xn, me).wait_recv()
            if relay[a]:
                start(copy(a, 3, xn, yn, half=0))
            start(copy(a, 5, xn, sib))
        for a in range(n):
            copy(a, 2, yn, me).wait_recv()
            if relay[a]:
                start(copy(a, 4, yn, xn, half=1))
            start(copy(a, 6, yn, sib))
        for a in range(n):
            if relay[a]:
                copy(a, 3, dg, me, half=0).wait_recv()
                start(copy(a, 7, dg, sib, half=0))
                copy(a, 4, dg, me, half=1).wait_recv()
                start(copy(a, 8, dg, sib, half=1))
            else:
                copy(a, 3, dg, me).wait_recv()
                start(copy(a, 7, dg, sib))
        for a in range(n):
            copy(a, 0, sib, me).wait_recv()
            copy(a, 5, other(xn), me).wait_recv()
            copy(a, 6, other(yn), me).wait_recv()
            if relay[a]:
                copy(a, 7, other(dg), me, half=0).wait_recv()
                copy(a, 8, other(dg), me, half=1).wait_recv()
            else:
                copy(a, 7, other(dg), me).wait_recv()
        for cp in sends:
            cp.wait_send()
        for cp in mine:
            cp.wait()

    any_spec = pl.BlockSpec(memory_space=pl.ANY)
    return pl.pallas_call(
        body, name=name,
        out_shape=[jax.ShapeDtypeStruct((N_DEV,) + a.shape, a.dtype) for a in arrs],
        in_specs=[any_spec] * n, out_specs=[any_spec] * n,
        scratch_shapes=[pltpu.SemaphoreType.DMA((K * n,)), pltpu.SemaphoreType.DMA((K * n,)), pltpu.SemaphoreType.DMA((n,))],
    )(*arrs)


def _chip_peers(x, y):
    return [(1 - x, y), (x, 1 - y), (1 - x, 1 - y)]


def _chip(p):
    return 2 * p[0] + p[1]


def _plan_gather(kinds):
    def plan(x, y, c):
        out = []
        for a, kind in enumerate(kinds):
            for peer in [(x, y, 1 - c)] + [(*ch, c) for ch in _chip_peers(x, y)]:
                out.append((a, None, (kind, _flat((x, y, c))), peer, (kind, _flat(peer))))
        return out
    return plan


def _plan_swap(n):
    def plan(x, y, c):
        return [(a, 1 - c, ("all", 0), (x, y, 1 - c), ("all", 0)) for a in range(n)]
    return plan


def _slot(ref, where):
    kind, k = where
    if kind == "all":
        return ref
    if kind == "lead":
        return ref.at[k]
    return ref.at[:, pl.ds(pl.multiple_of(k * 256, 256), 256)]


def _plan_scatter(n):
    def plan(x, y, c):
        out = []
        for a in range(n):
            for ch in _chip_peers(x, y):
                out.append((a, _chip(ch), ("lead", _chip((x, y))), (*ch, c), ("lead", _chip(ch))))
        return out
    return plan


HBM_SPEC = pl.BlockSpec(memory_space=pltpu.HBM)
SEM_SPEC = pl.BlockSpec(memory_space=pltpu.SEMAPHORE)


def _in_hbm(a):
    return pltpu.with_memory_space_constraint(a, pltpu.HBM)


def _exchange_start(srcs, lands, plan, name):
    n = len(srcs)
    count = len(plan(0, 0, 0))

    def body(*refs):
        src_refs, land_refs = refs[:n], refs[n:2 * n]
        send_sems, recv_sems = refs[2 * n], refs[2 * n + 1]
        token = refs[-1]
        x, y, c = lax.axis_index("x"), lax.axis_index("y"), lax.axis_index("c")
        for k, (a, si, di, peer, _) in enumerate(plan(x, y, c)):
            src = src_refs[a] if si is None else src_refs[a].at[si]
            pltpu.make_async_remote_copy(src_ref=src, dst_ref=_slot(land_refs[a], di), send_sem=send_sems.at[k],
                                         recv_sem=recv_sems.at[k], device_id=peer, device_id_type=MESH).start()
        token[...] = jnp.zeros_like(token)

    out = pl.pallas_call(
        body, name=name,
        out_shape=(pltpu.SemaphoreType.DMA((count,)), pltpu.SemaphoreType.DMA((count,)),
                   *[pltpu.HBM(a.shape, a.dtype) for a in lands], jax.ShapeDtypeStruct((8, 128), F32)),
        in_specs=[HBM_SPEC] * (2 * n),
        out_specs=(SEM_SPEC, SEM_SPEC, *([HBM_SPEC] * n), pl.BlockSpec(memory_space=pltpu.VMEM)),
        input_output_aliases={n + i: 2 + i for i in range(n)},
        compiler_params=pltpu.CompilerParams(has_side_effects=pltpu.SideEffectType.DATAFLOW_SIDE_EFFECTING),
    )(*[_in_hbm(a) for a in srcs], *[_in_hbm(a) for a in lands])
    return out[0], out[1], list(srcs), list(out[2:2 + n]), out[-1]


def _exchange_wait(send_sems, recv_sems, srcs, lands, plan, after, name):
    n = len(srcs)

    def body(*refs):
        src_refs, land_refs = refs[:n], refs[n:2 * n]
        send_sems, recv_sems = refs[2 * n], refs[2 * n + 1]
        x, y, c = lax.axis_index("x"), lax.axis_index("y"), lax.axis_index("c")
        for k, (a, si, _, peer, ri) in enumerate(plan(x, y, c)):
            src = src_refs[a] if si is None else src_refs[a].at[si]
            cp = pltpu.make_async_remote_copy(src_ref=src, dst_ref=_slot(land_refs[a], ri), send_sem=send_sems.at[k],
                                              recv_sem=recv_sems.at[k], device_id=peer, device_id_type=MESH)
            cp.wait_send()
            cp.wait_recv()

    out = pl.pallas_call(
        body, name=name,
        out_shape=tuple(pltpu.HBM(a.shape, a.dtype) for a in lands),
        in_specs=[HBM_SPEC] * (2 * n) + [SEM_SPEC, SEM_SPEC, pl.BlockSpec(memory_space=pl.ANY)],
        out_specs=tuple([HBM_SPEC] * n),
        input_output_aliases={n + i: i for i in range(n)},
        compiler_params=pltpu.CompilerParams(has_side_effects=pltpu.SideEffectType.DATAFLOW_SIDE_EFFECTING),
    )(*[_in_hbm(a) for a in srcs], *lands, send_sems, recv_sems, after)
    return list(out)


def _forward_to_sibling(lands, kinds, owns, name):
    n = len(lands)

    def body(*refs):
        in_refs, own_refs, out_refs = refs[:n], refs[n:2 * n], refs[2 * n:3 * n]
        send_sems, recv_sems, local_sems = refs[3 * n:]
        x, y, c = lax.axis_index("x"), lax.axis_index("y"), lax.axis_index("c")
        sibling = (x, y, 1 - c)
        mine = [pltpu.make_async_copy(own_refs[a], _slot(out_refs[a], (kinds[a], _flat((x, y, c)))), local_sems.at[a])
                for a in range(n) if kinds[a] == "lead"]
        for cp in mine:
            cp.start()

        def copy(a, j, slot):
            return pltpu.make_async_remote_copy(src_ref=_slot(in_refs[a], (kinds[a], slot)), dst_ref=_slot(out_refs[a], (kinds[a], slot)),
                                                send_sem=send_sems.at[a * 3 + j], recv_sem=recv_sems.at[a * 3 + j],
                                                device_id=sibling, device_id_type=MESH)

        sends = [copy(a, j, _flat((*ch, c))) for a in range(n) for j, ch in enumerate(_chip_peers(x, y))]
        for cp in sends:
            cp.start()
        for a in range(n):
            for j, ch in enumerate(_chip_peers(x, y)):
                copy(a, j, _flat((*ch, 1 - c))).wait_recv()
        for cp in sends:
            cp.wait_send()
        for cp in mine:
            cp.wait()

    any_spec = pl.BlockSpec(memory_space=pl.ANY)
    return pl.pallas_call(
        body, name=name, out_shape=[jax.ShapeDtypeStruct(a.shape, a.dtype) for a in lands],
        in_specs=[any_spec] * (2 * n), out_specs=[any_spec] * n, input_output_aliases={a: a for a in range(n)},
        scratch_shapes=[pltpu.SemaphoreType.DMA((3 * n,)), pltpu.SemaphoreType.DMA((3 * n,)), pltpu.SemaphoreType.DMA((n,))],
    )(*lands, *owns)


def _swap_with_sibling(parts, name):
    n = len(parts)

    def body(*refs):
        in_refs, out_refs = refs[:n], refs[n:2 * n]
        send_sems, recv_sems = refs[2 * n:]
        x, y, c = lax.axis_index("x"), lax.axis_index("y"), lax.axis_index("c")
        sends = [pltpu.make_async_remote_copy(src_ref=in_refs[a].at[1 - c], dst_ref=out_refs[a], send_sem=send_sems.at[a],
                                              recv_sem=recv_sems.at[a], device_id=(x, y, 1 - c), device_id_type=MESH)
                 for a in range(n)]
        for cp in sends:
            cp.start()
        for cp in sends:
            cp.wait()

    any_spec = pl.BlockSpec(memory_space=pl.ANY)
    return pl.pallas_call(
        body, name=name, out_shape=[jax.ShapeDtypeStruct(a.shape[1:], a.dtype) for a in parts],
        in_specs=[any_spec] * n, out_specs=[any_spec] * n,
        scratch_shapes=[pltpu.SemaphoreType.DMA((n,)), pltpu.SemaphoreType.DMA((n,))],
    )(*parts)


def _pair_sum(parts, got, core, name):
    _, _, R, C = parts.shape
    tr = 256 if R % 256 == 0 else R

    def body(c_ref, p_ref, g_ref, o_ref):
        o_ref[...] = (p_ref[...].astype(F32) + g_ref[...].astype(F32)).astype(o_ref.dtype)

    return pl.pallas_call(
        body, name=name,
        grid_spec=pltpu.PrefetchScalarGridSpec(
            num_scalar_prefetch=1, grid=(4, R // tr),
            in_specs=[pl.BlockSpec((None, None, tr, C), lambda j, i, c_ref: (c_ref[0], j, i, 0)),
                      pl.BlockSpec((None, tr, C), lambda j, i, c_ref: (j, i, 0))],
            out_specs=pl.BlockSpec((None, tr, C), lambda j, i, c_ref: (j, i, 0))),
        out_shape=jax.ShapeDtypeStruct((4, R, C), parts.dtype),
        compiler_params=_params(("parallel", "parallel")),
    )(core, parts, got)


def _matmul(a, b, mode, M, N, K, tm, tn, tk, out_dtype, name, b_noff=0, a_moff=0, a_koff=0, b_blocked=False, out_blocked=None,
            dep=None, addend=None, vmem_mib=48):
    nm, nn, nk = M // tm, N // tn, K // tk
    if mode == "nn":
        a_spec = pl.BlockSpec((tm, tk), lambda j, i, k: (i, k + a_koff))
        b_spec = pl.BlockSpec((tk, tn), lambda j, i, k: (k, j + b_noff))
        dn = NN
    elif mode == "nt":
        a_spec = pl.BlockSpec((tm, tk), lambda j, i, k: (i, k + a_koff))
        if b_blocked:
            b_spec = pl.BlockSpec((None, tn, tk), lambda j, i, k: (k, j, 0))
        else:
            b_spec = pl.BlockSpec((tn, tk), lambda j, i, k: (j + b_noff, k))
        dn = NT
    else:
        a_spec = pl.BlockSpec((tk, tm), lambda j, i, k: (k, i + a_moff))
        if b_blocked:
            b_spec = pl.BlockSpec((None, tk, tn), lambda j, i, k: (j, k, 0))
        else:
            b_spec = pl.BlockSpec((tk, tn), lambda j, i, k: (k, j + b_noff))
        dn = TN
    if out_blocked == "col":
        out_shape = jax.ShapeDtypeStruct((2, 4, M, tn), out_dtype)
        out_spec = pl.BlockSpec((None, None, tm, tn), lambda j, i, k: (j % 2, j // 2, i, 0))
    elif out_blocked == "row":
        out_shape = jax.ShapeDtypeStruct((2, 4, tm, N), out_dtype)
        out_spec = pl.BlockSpec((None, None, tm, tn), lambda j, i, k: (i % 2, i // 2, 0, j))
    elif out_blocked == "third":
        out_shape = jax.ShapeDtypeStruct((3, M, N // 3), out_dtype)
        out_spec = pl.BlockSpec((None, tm, tn), lambda j, i, k: (j // (nn // 3), i, j % (nn // 3)))
    else:
        out_shape = jax.ShapeDtypeStruct((M, N), out_dtype)
        out_spec = pl.BlockSpec((tm, tn), lambda j, i, k: (i, j))

    n_extra = (addend is not None) + (dep is not None)

    def body(a_ref, b_ref, *rest):
        o_ref, scratch = rest[n_extra], rest[n_extra + 1:]
        if nk == 1:
            prod = _dot(a_ref[...], b_ref[...], dn)
            if addend is not None:
                prod = prod + rest[0][...].astype(F32)
            o_ref[...] = prod.astype(out_dtype)
        else:
            assert addend is None
            acc_ref, = scratch
            k = pl.program_id(2)

            @pl.when(k == 0)
            def _():
                acc_ref[...] = jnp.zeros_like(acc_ref)

            acc_ref[...] += _dot(a_ref[...], b_ref[...], dn)

            @pl.when(k == nk - 1)
            def _():
                o_ref[...] = acc_ref[...].astype(out_dtype)

    return pl.pallas_call(
        body, name=name, grid=(nn, nm, nk),
        in_specs=[a_spec, b_spec] + ([] if addend is None else [out_spec])
        + ([] if dep is None else [pl.BlockSpec((8, 128), lambda j, i, k: (0, 0))]),
        out_specs=out_spec, out_shape=out_shape,
        scratch_shapes=[] if nk == 1 else [pltpu.VMEM((tm, tn), F32)],
        compiler_params=_params(("parallel", "parallel", "arbitrary"), vmem_mib),
    )(a, b, *([] if addend is None else [addend]), *([] if dep is None else [dep]))


def _rms_fwd(x, g, name):
    R, Dm = x.shape
    tr = min(R, 256)

    def body(x_ref, g_ref, h_ref):
        xv = x_ref[...]
        r = lax.rsqrt(jnp.mean(xv * xv, axis=-1, keepdims=True) + EPS)
        h_ref[...] = (xv * r * g_ref[...]).astype(BF)

    return pl.pallas_call(
        body, name=name, grid=(R // tr,),
        in_specs=[pl.BlockSpec((tr, Dm), lambda i: (i, 0)), pl.BlockSpec((1, Dm), lambda i: (0, 0))],
        out_specs=pl.BlockSpec((tr, Dm), lambda i: (i, 0)), out_shape=jax.ShapeDtypeStruct((R, Dm), BF),
        compiler_params=_params(("parallel",)),
    )(x, g)


def _rms_gain_grad(dn, x, name):
    R, Dm = x.shape

    def body(dn_ref, x_ref, o_ref):
        xv = x_ref[...]
        r = lax.rsqrt(jnp.mean(xv * xv, axis=-1, keepdims=True) + EPS)
        o_ref[...] = jnp.sum(dn_ref[...] * xv * r, axis=0, keepdims=True)

    return pl.pallas_call(
        body, name=name, out_shape=jax.ShapeDtypeStruct((1, Dm), F32),
        compiler_params=pltpu.CompilerParams(vmem_limit_bytes=32 * MIB),
    )(dn, x)


def _shift_down(v, k, head8, row, T):
    if k == 0:
        return v
    r = pltpu.roll(v, k, 0)
    hr = pltpu.roll(head8, k, 0)
    top = jnp.where(row[:8] < k, hr, r[:8])
    return jnp.concatenate([top, r[8:]], axis=0)


def _shift_up(v, k, tail8, row, T):
    if k == 0:
        return v
    r = pltpu.roll(v, T - k, 0)
    tr = pltpu.roll(tail8, 8 - k, 0)
    bot = jnp.where(row[:8] >= 8 - k, tr, r[T - 8:])
    return jnp.concatenate([r[:T - 8], bot], axis=0)


def _rglru_gates(u, head8, grow, row, T, cw_ref, cb_ref, wa_ref, ba_ref, wx_ref, bx_ref, lam_ref):
    us = [_shift_down(u, k, head8, row, T) for k in range(CONV_W)]
    acc = us[0] * cw_ref[0:1, :]
    for k in range(1, CONV_W):
        acc = acc + us[k] * cw_ref[k:k + 1, :]
    conv = cb_ref[...] + acc
    cbf = conv.astype(BF)
    r_ = _sigmoid(_dot(cbf, wa_ref[0], NN) + ba_ref[...])
    i_ = _sigmoid(_dot(cbf, wx_ref[0], NN) + bx_ref[...])
    sp = _softplus(-lam_ref[...])
    la = -LRU_C * r_ * sp
    a = jnp.exp(la)
    mult_raw = jnp.sqrt(-_expm1(2.0 * la))
    mult = jnp.where(grow == 0, 1.0, mult_raw)
    return us, conv, cbf, r_, i_, sp, a, mult_raw, mult


def _rglru_specs(T, nt, rev):
    tmap = (lambda n, t: (nt - 1 - t, n)) if rev else (lambda n, t: (t, n))
    hmap = ((lambda n, t: (jnp.maximum((nt - 1 - t) * (T // 8) - 1, 0), n)) if rev
            else (lambda n, t: (jnp.maximum(t * (T // 8) - 1, 0), n)))
    tile = pl.BlockSpec((T, RNN_BLOCK), tmap)
    halo = pl.BlockSpec((8, RNN_BLOCK), hmap)
    vec = pl.BlockSpec((1, RNN_BLOCK), lambda n, t: (0, n))
    cw = pl.BlockSpec((CONV_W, RNN_BLOCK), lambda n, t: (0, n))
    wblk = pl.BlockSpec((1, RNN_BLOCK, RNN_BLOCK), lambda n, t: (n, 0, 0))
    return tile, halo, vec, cw, wblk


def _rglru_fwd(xr, g, cw, cb, wa, ba, wx, bx, lam, T):
    S = xr.shape[0]
    nt = S // T

    def body(u_ref, uh_ref, g_ref, cw_ref, cb_ref, wa_ref, ba_ref, wx_ref, bx_ref, lam_ref, h_ref, y_ref, carry):
        t = pl.program_id(1)

        @pl.when(t == 0)
        def _():
            carry[...] = jnp.zeros_like(carry)

        row = lax.broadcasted_iota(jnp.int32, (T, RNN_BLOCK), 0)
        grow = row + t * T
        head8 = jnp.where(t > 0, uh_ref[...], 0.0)
        _, conv, _, _, i_, _, a, _, mult = _rglru_gates(u_ref[...], head8, grow, row, T, cw_ref, cb_ref, wa_ref, ba_ref,
                                                         wx_ref, bx_ref, lam_ref)
        b = mult * i_ * conv
        s = 1
        while s < T:
            keep = row >= s
            a_s = jnp.where(keep, pltpu.roll(a, s, 0), 1.0)
            b_s = jnp.where(keep, pltpu.roll(b, s, 0), 0.0)
            b = a * b_s + b
            a = a * a_s
            s *= 2
        h = b + a * carry[0:1, :]
        carry[...] = jnp.broadcast_to(h[T - 1:T, :], carry.shape)
        h_ref[...] = h
        gv = g_ref[...]
        y_ref[...] = (h * (gv * _sigmoid(gv))).astype(BF)

    tile, halo, vec, cwspec, wblk = _rglru_specs(T, nt, False)
    return pl.pallas_call(
        body, name="rglru_fwd", grid=(RNN_BLOCKS, nt),
        in_specs=[tile, halo, tile, cwspec, vec, wblk, vec, wblk, vec, vec],
        out_specs=[tile, tile],
        out_shape=[jax.ShapeDtypeStruct((S, D_RNN), F32), jax.ShapeDtypeStruct((S, D_RNN), BF)],
        scratch_shapes=[pltpu.VMEM((8, RNN_BLOCK), F32)],
        compiler_params=_params(("parallel", "arbitrary")),
    )(xr, xr, g, cw, cb, wa, ba, wx, bx, lam)


def _rglru_bwd(xr, g, h, dy, cw, cb, wa, ba, wx, bx, lam, T):
    S = xr.shape[0]
    nt = S // T

    def body(u_ref, uh_ref, g_ref, h_ref, hh_ref, dy_ref, cw_ref, cb_ref, wa_ref, ba_ref, wx_ref, bx_ref, lam_ref,
             du_ref, dg_ref, dwa_ref, dwx_ref, dvec_ref, c_dhh, c_a, c_dconv):
        t = pl.program_id(1)
        tt = nt - 1 - t

        @pl.when(t == 0)
        def _():
            c_dhh[...] = jnp.zeros_like(c_dhh)
            c_a[...] = jnp.zeros_like(c_a)
            c_dconv[...] = jnp.zeros_like(c_dconv)
            dwa_ref[...] = jnp.zeros_like(dwa_ref)
            dwx_ref[...] = jnp.zeros_like(dwx_ref)
            dvec_ref[...] = jnp.zeros_like(dvec_ref)

        row = lax.broadcasted_iota(jnp.int32, (T, RNN_BLOCK), 0)
        row8 = row[:8]
        grow = row + tt * T
        head8 = jnp.where(tt > 0, uh_ref[...], 0.0)
        us, conv, cbf, r_, i_, sp, a, mult_raw, mult = _rglru_gates(
            u_ref[...], head8, grow, row, T, cw_ref, cb_ref, wa_ref, ba_ref, wx_ref, bx_ref, lam_ref)
        hv = h_ref[...]
        hprev = _shift_down(hv, 1, jnp.where(tt > 0, hh_ref[...], 0.0), row, T)
        gv = g_ref[...]
        sg = _sigmoid(gv)
        dyv = dy_ref[...]
        dg_ref[...] = (dyv * hv * (sg * (1.0 + gv * (1.0 - sg)))).astype(BF)
        d = dyv * (gv * sg)
        A = _shift_up(a, 1, c_a[...], row, T)
        s = 1
        while s < T:
            keep = row < T - s
            A_s = jnp.where(keep, pltpu.roll(A, T - s, 0), 1.0)
            d_s = jnp.where(keep, pltpu.roll(d, T - s, 0), 0.0)
            d = A * d_s + d
            A = A * A_s
            s *= 2
        dhh = d + A * c_dhh[0:1, :]
        da = dhh * hprev
        dconv = dhh * mult * i_
        di = dhh * mult * conv
        dmult = dhh * i_ * conv
        dla = da * a - jnp.where(grow == 0, 0.0, dmult * (a * a) / mult_raw)
        dr = dla * (-LRU_C * sp)
        dsp = jnp.sum(dla * (-LRU_C * r_), axis=0, keepdims=True)
        dza = dr * r_ * (1.0 - r_)
        dzx = di * i_ * (1.0 - i_)
        dza_b, dzx_b = dza.astype(BF), dzx.astype(BF)
        dconv = dconv + _dot(dza_b, wa_ref[0], NT) + _dot(dzx_b, wx_ref[0], NT)
        dwa_ref[0] += _dot(cbf, dza_b, TN)
        dwx_ref[0] += _dot(cbf, dzx_b, TN)
        lam = lam_ref[...]
        rows = [jnp.sum(dconv * us[k], axis=0, keepdims=True) for k in range(CONV_W)]
        rows += [jnp.sum(dconv, axis=0, keepdims=True), jnp.sum(dza, axis=0, keepdims=True),
                 jnp.sum(dzx, axis=0, keepdims=True), dsp * (-_sigmoid(-lam))]
        upd = jnp.zeros((8, RNN_BLOCK), F32)
        for j, rv in enumerate(rows):
            upd = upd + jnp.where(row8 == j, rv, 0.0)
        dvec_ref[...] += upd
        tail8 = c_dconv[...]
        du = dconv * cw_ref[0:1, :]
        for k in range(1, CONV_W):
            du = du + _shift_up(dconv, k, tail8, row, T) * cw_ref[k:k + 1, :]
        du_ref[...] = du.astype(BF)
        c_dhh[...] = jnp.broadcast_to(dhh[0:1, :], c_dhh.shape)
        c_a[...] = jnp.broadcast_to(a[0:1, :], c_a.shape)
        c_dconv[...] = dconv[:8]

    tile, halo, vec, cwspec, wblk = _rglru_specs(T, nt, True)
    acc8 = pl.BlockSpec((8, RNN_BLOCK), lambda n, t: (0, n))
    return pl.pallas_call(
        body, name="rglru_bwd", grid=(RNN_BLOCKS, nt),
        in_specs=[tile, halo, tile, tile, halo, tile, cwspec, vec, wblk, vec, wblk, vec, vec],
        out_specs=[tile, tile, wblk, wblk, acc8],
        out_shape=[jax.ShapeDtypeStruct((S, D_RNN), BF), jax.ShapeDtypeStruct((S, D_RNN), BF),
                   jax.ShapeDtypeStruct((RNN_BLOCKS, RNN_BLOCK, RNN_BLOCK), F32),
                   jax.ShapeDtypeStruct((RNN_BLOCKS, RNN_BLOCK, RNN_BLOCK), F32),
                   jax.ShapeDtypeStruct((8, D_RNN), F32)],
        scratch_shapes=[pltpu.VMEM((8, RNN_BLOCK), F32)] * 3,
        compiler_params=_params(("parallel", "arbitrary")),
    )(xr, xr, g, h, h, dy, cw, cb, wa, ba, wx, bx, lam)


def _rel_bucket_map():
    qi = np.arange(WINDOW)[:, None]
    kj = np.arange(2 * WINDOW)[None, :]
    dist = jnp.asarray(qi + WINDOW - kj, jnp.int32)
    n = jnp.maximum(dist, 0)
    max_exact = REL_BUCKETS // 2
    ratio = jnp.log(jnp.maximum(n, 1).astype(F32) / max_exact) / math.log(REL_MAX_DIST / max_exact)
    large = jnp.minimum(max_exact + (ratio * (REL_BUCKETS - max_exact)).astype(jnp.int32), REL_BUCKETS - 1)
    bucket = jnp.where(n < max_exact, n, large).astype(jnp.int32)
    j = np.arange(WINDOW)[None, :]
    return jnp.where(jnp.asarray(j > qi), bucket[:, :WINDOW], bucket[:, WINDOW:])


def _swa_common(n, kv_ref, bucket_ref, relb_ref, bias_scr):
    @pl.when(n == 0)
    def _():
        bk = bucket_ref[...]
        for h in range(SWA_HEADS):
            acc = jnp.zeros((WINDOW, WINDOW), F32)
            for b in range(REL_BUCKETS):
                acc = acc + jnp.where(bk == b, relb_ref[b, h], 0.0)
            bias_scr[h] = acc

    prev0 = pl.multiple_of(jnp.maximum(n - 1, 0) * WINDOW, WINDOW)
    cur0 = pl.multiple_of(n * WINDOW, WINDOW)
    kk = jnp.concatenate([kv_ref[pl.ds(prev0, WINDOW), :], kv_ref[pl.ds(cur0, WINDOW), :]], axis=0).astype(F32)
    rowi = lax.broadcasted_iota(jnp.int32, (WINDOW, WINDOW), 0)
    col = lax.broadcasted_iota(jnp.int32, (WINDOW, WINDOW), 1)
    from_prev = col > rowi
    return kk, from_prev, prev0, cur0


def _fold(full, from_prev):
    return jnp.where(from_prev, full[:, :WINDOW], full[:, WINDOW:])


def _unfold(sq, from_prev):
    return jnp.concatenate([jnp.where(from_prev, sq, 0.0), jnp.where(from_prev, 0.0, sq)], axis=1)


def _half_pair(part, kvh):
    lo = lax.broadcasted_iota(jnp.int32, part.shape, 1) < SWA_HD
    if kvh == 0:
        pa = jnp.where(lo, part, 0.0)
        pb = pltpu.roll(pa, SWA_HD, 1)
    else:
        pb = jnp.where(lo, 0.0, part)
        pa = pltpu.roll(pb, SWA_HD, 1)
    return pa.astype(BF), pb.astype(BF)


ALL_HEADS = SWA_HEADS * WINDOW


def _sink_column(sinks):
    return jnp.repeat(sinks.reshape(SWA_HEADS), WINDOW).reshape(ALL_HEADS, 1)


def _swa_operands(kk):
    return [(_half_pair(kk[:, :128], kvh), _half_pair(kk[:, 128:], kvh)) for kvh in range(SWA_KV_HEADS)]


def _swa_probs(n, q_ref, ops, bias_scr, sinkc_ref, from_prev):
    lgs = []
    for kvh in range(SWA_KV_HEADS):
        (ka, kb), _ = ops[kvh]
        for p in range(4):
            q2 = q_ref[:, kvh * 512 + p * 128:kvh * 512 + p * 128 + 128]
            lgs += [_fold(_dot(q2, ka, NT), from_prev), _fold(_dot(q2, kb, NT), from_prev)]
    lg = jnp.concatenate(lgs, axis=0) * (SWA_HD ** -0.5) + bias_scr[...].reshape(ALL_HEADS, WINDOW)
    rowi = jnp.bitwise_and(lax.broadcasted_iota(jnp.int32, (ALL_HEADS, WINDOW), 0), WINDOW - 1)
    col = lax.broadcasted_iota(jnp.int32, (ALL_HEADS, WINDOW), 1)
    no_prev = jnp.where(n > 0, 0, 4 * WINDOW)
    lg = jnp.where(jnp.logical_or(col <= rowi, col > rowi + no_prev), lg, NEG_INF)
    sink = sinkc_ref[...]
    m = jnp.maximum(jnp.max(lg, axis=-1, keepdims=True), sink)
    e = jnp.exp(lg - m)
    es = jnp.exp(sink - m)
    den = jnp.sum(e, axis=-1, keepdims=True) + es
    return e / den, es / den


def _swa_fwd(q, kv, g, bucket, rel_bias, sink_col):
    S = q.shape[0]
    nb = S // WINDOW

    def body(q_ref, kv_ref, g_ref, bucket_ref, relb_ref, sinkc_ref, o_ref, y_ref, bias_scr):
        n = pl.program_id(0)
        kk, from_prev, _, _ = _swa_common(n, kv_ref, bucket_ref, relb_ref, bias_scr)
        ops = _swa_operands(kk)
        pr, _ = _swa_probs(n, q_ref, ops, bias_scr, sinkc_ref, from_prev)
        for kvh in range(SWA_KV_HEADS):
            _, (va, vb) = ops[kvh]
            for p in range(4):
                c0 = kvh * 512 + p * 128
                r0 = (kvh * 8 + 2 * p) * WINDOW
                o2 = (_dot(_unfold(pr[r0:r0 + WINDOW], from_prev).astype(BF), va, NN)
                      + _dot(_unfold(pr[r0 + WINDOW:r0 + 2 * WINDOW], from_prev).astype(BF), vb, NN))
                o_ref[:, c0:c0 + 128] = o2
                gv = g_ref[:, c0:c0 + 128]
                y_ref[:, c0:c0 + 128] = (o2 * (gv * _sigmoid(gv))).astype(BF)

    blk = pl.BlockSpec((WINDOW, 1024), lambda n: (n, 0))
    smem = pl.BlockSpec(memory_space=pltpu.SMEM)
    sinkc = pl.BlockSpec((ALL_HEADS, 1), lambda n: (0, 0))
    return pl.pallas_call(
        body, name="swa_fwd", grid=(nb,),
        in_specs=[blk, pl.BlockSpec((S, 256), lambda n: (0, 0)), blk, pl.BlockSpec((WINDOW, WINDOW), lambda n: (0, 0)), smem, sinkc],
        out_specs=[blk, blk],
        out_shape=[jax.ShapeDtypeStruct((S, 1024), F32), jax.ShapeDtypeStruct((S, 1024), BF)],
        scratch_shapes=[pltpu.VMEM((SWA_HEADS, WINDOW, WINDOW), F32)],
        compiler_params=_params(("arbitrary",)),
    )(q, kv, g, bucket, rel_bias, sink_col)


def _swa_bwd(q, kv, g, o, dy, bucket, rel_bias, sink_col):
    S = q.shape[0]
    nb = S // WINDOW

    def body(q_ref, kv_ref, g_ref, o_ref, dy_ref, bucket_ref, relb_ref, sinkc_ref,
             dq_ref, dg_ref, dkv_ref, dsink_ref, drel_ref, bias_scr, dbias_scr, dsink_scr):
        n = pl.program_id(0)

        @pl.when(n == 0)
        def _():
            dbias_scr[...] = jnp.zeros_like(dbias_scr)
            dsink_scr[...] = jnp.zeros_like(dsink_scr)
            dkv_ref[...] = jnp.zeros_like(dkv_ref)

        kk, from_prev, prev0, cur0 = _swa_common(n, kv_ref, bucket_ref, relb_ref, bias_scr)
        ops = _swa_operands(kk)
        pr, ps = _swa_probs(n, q_ref, ops, bias_scr, sinkc_ref, from_prev)
        do2s, dps = [], []
        for kvh in range(SWA_KV_HEADS):
            _, (va, vb) = ops[kvh]
            for p in range(4):
                c0 = kvh * 512 + p * 128
                gv = g_ref[:, c0:c0 + 128]
                sg = _sigmoid(gv)
                dyv = dy_ref[:, c0:c0 + 128]
                dg_ref[:, c0:c0 + 128] = (dyv * o_ref[:, c0:c0 + 128] * (sg * (1.0 + gv * (1.0 - sg)))).astype(BF)
                do2 = (dyv * (gv * sg)).astype(BF)
                do2s.append(do2)
                dps += [_fold(_dot(do2, va, NT), from_prev), _fold(_dot(do2, vb, NT), from_prev)]
        dp = jnp.concatenate(dps, axis=0)
        delta = jnp.sum(pr * dp, axis=-1, keepdims=True)
        ds = pr * (dp - delta)
        dbias_scr[...] += ds.reshape(SWA_HEADS, WINDOW, WINDOW)
        dsink_scr[...] += ps * delta
        dsc = ds * (SWA_HD ** -0.5)
        lo256 = lax.broadcasted_iota(jnp.int32, (2 * WINDOW, 128), 1) < SWA_HD
        dks, dvs = [], []
        for kvh in range(SWA_KV_HEADS):
            (ka, kb), _ = ops[kvh]
            dka = jnp.zeros((2 * WINDOW, 128), F32)
            dkb, dva, dvb = dka, dka, dka
            for p in range(4):
                c0 = kvh * 512 + p * 128
                r0 = (kvh * 8 + 2 * p) * WINDOW
                q2 = q_ref[:, c0:c0 + 128]
                do2 = do2s[kvh * 4 + p]
                ds0 = _unfold(dsc[r0:r0 + WINDOW], from_prev).astype(BF)
                ds1 = _unfold(dsc[r0 + WINDOW:r0 + 2 * WINDOW], from_prev).astype(BF)
                dq_ref[:, c0:c0 + 128] = (_dot(ds0, ka, NN) + _dot(ds1, kb, NN)).astype(BF)
                dka = dka + _dot(ds0, q2, TN)
                dkb = dkb + _dot(ds1, q2, TN)
                dva = dva + _dot(_unfold(pr[r0:r0 + WINDOW], from_prev).astype(BF), do2, TN)
                dvb = dvb + _dot(_unfold(pr[r0 + WINDOW:r0 + 2 * WINDOW], from_prev).astype(BF), do2, TN)
            dks.append(jnp.where(lo256, dka, 0.0) + pltpu.roll(jnp.where(lo256, 0.0, dkb), SWA_HD, 1))
            dvs.append(jnp.where(lo256, dva, 0.0) + pltpu.roll(jnp.where(lo256, 0.0, dvb), SWA_HD, 1))
        dk = dks[0] + pltpu.roll(dks[1], SWA_HD, 1)
        dv = dvs[0] + pltpu.roll(dvs[1], SWA_HD, 1)
        dkv_ref[pl.ds(prev0, WINDOW), 0:128] += dk[:WINDOW]
        dkv_ref[pl.ds(prev0, WINDOW), 128:256] += dv[:WINDOW]
        dkv_ref[pl.ds(cur0, WINDOW), 0:128] += dk[WINDOW:]
        dkv_ref[pl.ds(cur0, WINDOW), 128:256] += dv[WINDOW:]

        @pl.when(n == nb - 1)
        def _():
            dsink_ref[...] = -jnp.sum(dsink_scr[...].reshape(SWA_HEADS, WINDOW, 1), axis=1)
            bk = bucket_ref[...]
            sums = []
            for b in range(REL_BUCKETS):
                sums.append(jnp.sum(jnp.where((bk == b)[None], dbias_scr[...], 0.0), axis=1))
            drel_ref[...] = jnp.sum(jnp.concatenate(sums, axis=0), axis=1, keepdims=True)

    blk = pl.BlockSpec((WINDOW, 1024), lambda n: (n, 0))
    smem = pl.BlockSpec(memory_space=pltpu.SMEM)
    whole = lambda shape: pl.BlockSpec(shape, lambda n: (0, 0))
    return pl.pallas_call(
        body, name="swa_bwd", grid=(nb,),
        in_specs=[blk, whole((S, 256)), blk, blk, blk, whole((WINDOW, WINDOW)), smem, whole((ALL_HEADS, 1))],
        out_specs=[blk, blk, whole((S, 256)), whole((SWA_HEADS, 1)), whole((REL_BUCKETS * SWA_HEADS, 1))],
        out_shape=[jax.ShapeDtypeStruct((S, 1024), BF), jax.ShapeDtypeStruct((S, 1024), BF),
                   jax.ShapeDtypeStruct((S, 256), F32), jax.ShapeDtypeStruct((SWA_HEADS, 1), F32),
                   jax.ShapeDtypeStruct((REL_BUCKETS * SWA_HEADS, 1), F32)],
        scratch_shapes=[pltpu.VMEM((SWA_HEADS, WINDOW, WINDOW), F32), pltpu.VMEM((SWA_HEADS, WINDOW, WINDOW), F32),
                        pltpu.VMEM((ALL_HEADS, 1), F32)],
        compiler_params=_params(("arbitrary",)),
    )(q, kv, g, o, dy, bucket, rel_bias, sink_col)


def _mem_probs(qh, mk):
    lg = _dot(qh, mk, NT) * (MEM_HD ** -0.5)
    e = jnp.exp(lg - jnp.max(lg, axis=-1, keepdims=True))
    return e / jnp.sum(e, axis=-1, keepdims=True)


def _mem_fwd(q, mkv, g):
    S = q.shape[0]
    M = mkv.shape[0]
    tq = 256

    def body(q_ref, mkv_ref, g_ref, o_ref, y_ref):
        for h in range(MEM_HEADS):
            c0 = h * MEM_HD
            pr = _mem_probs(q_ref[:, c0:c0 + MEM_HD], mkv_ref[:, c0:c0 + MEM_HD])
            o = _dot(pr.astype(BF), mkv_ref[:, D_MEM + c0:D_MEM + c0 + MEM_HD], NN)
            o_ref[:, c0:c0 + MEM_HD] = o
            gv = g_ref[:, c0:c0 + MEM_HD]
            y_ref[:, c0:c0 + MEM_HD] = (o * (gv * _sigmoid(gv))).astype(BF)

    blk = pl.BlockSpec((tq, D_MEM), lambda i: (i, 0))
    return pl.pallas_call(
        body, name="mem_fwd", grid=(S // tq,),
        in_specs=[blk, pl.BlockSpec((M, 2 * D_MEM), lambda i: (0, 0)), blk], out_specs=[blk, blk],
        out_shape=[jax.ShapeDtypeStruct((S, D_MEM), F32), jax.ShapeDtypeStruct((S, D_MEM), BF)],
        compiler_params=_params(("parallel",)),
    )(q, mkv, g)


def _mem_bwd(q, mkv, g, o, dy):
    S = q.shape[0]
    M = mkv.shape[0]
    tq = 256

    def body(q_ref, mkv_ref, g_ref, o_ref, dy_ref, dq_ref, dg_ref, dmkv_ref):
        @pl.when(pl.program_id(0) == 0)
        def _():
            dmkv_ref[...] = jnp.zeros_like(dmkv_ref)

        for h in range(MEM_HEADS):
            c0 = h * MEM_HD
            qh = q_ref[:, c0:c0 + MEM_HD]
            mk = mkv_ref[:, c0:c0 + MEM_HD]
            mv = mkv_ref[:, D_MEM + c0:D_MEM + c0 + MEM_HD]
            gv = g_ref[:, c0:c0 + MEM_HD]
            sg = _sigmoid(gv)
            dyv = dy_ref[:, c0:c0 + MEM_HD]
            dg_ref[:, c0:c0 + MEM_HD] = (dyv * o_ref[:, c0:c0 + MEM_HD] * (sg * (1.0 + gv * (1.0 - sg)))).astype(BF)
            do = (dyv * (gv * sg)).astype(BF)
            pr = _mem_probs(qh, mk)
            dp = _dot(do, mv, NT)
            ds = pr * (dp - jnp.sum(pr * dp, axis=-1, keepdims=True))
            dsb = (ds * (MEM_HD ** -0.5)).astype(BF)
            dq_ref[:, c0:c0 + MEM_HD] = _dot(dsb, mk, NN).astype(BF)
            dmkv_ref[:, c0:c0 + MEM_HD] += _dot(dsb, qh, TN)
            dmkv_ref[:, D_MEM + c0:D_MEM + c0 + MEM_HD] += _dot(pr.astype(BF), do, TN)

    blk = pl.BlockSpec((tq, D_MEM), lambda i: (i, 0))
    whole = pl.BlockSpec((M, 2 * D_MEM), lambda i: (0, 0))
    return pl.pallas_call(
        body, name="mem_bwd", grid=(S // tq,),
        in_specs=[blk, whole, blk, blk, blk], out_specs=[blk, blk, whole],
        out_shape=[jax.ShapeDtypeStruct((S, D_MEM), BF), jax.ShapeDtypeStruct((S, D_MEM), BF),
                   jax.ShapeDtypeStruct((M, 2 * D_MEM), F32)],
        compiler_params=_params(("arbitrary",)),
    )(q, mkv, g, o, dy)


MERGE_TN = 512


def _merge_specs(tm):
    ytile = pl.BlockSpec((tm, 1024), lambda i, j: (i, 0))
    wblk = pl.BlockSpec((1024, MERGE_TN), lambda i, j: (0, j))
    gls = [pl.BlockSpec((None, tm, MERGE_TN), (lambda i, j, br=br: (br, i, j))) for br in range(3)]
    otile = pl.BlockSpec((tm, MERGE_TN), lambda i, j: (i, j))
    return ytile, wblk, gls, otile


def _merge_fwd(ys, ws, gl, tm):
    S = gl.shape[1]

    def body(y0, y1, y2, w0, w1, w2, g0, g1, g2, o_ref):
        acc = None
        for y_ref, w_ref, g_ref in ((y0, w0, g0), (y1, w1, g1), (y2, w2, g2)):
            term = _sigmoid(g_ref[...]) * _dot(y_ref[...], w_ref[...], NN)
            acc = term if acc is None else acc + term
        o_ref[...] = acc.astype(BF)

    ytile, wblk, gls, otile = _merge_specs(tm)
    return pl.pallas_call(
        body, name="merge_fwd", grid=(S // tm, D_MODEL // MERGE_TN),
        in_specs=[ytile] * 3 + [wblk] * 3 + gls, out_specs=otile,
        out_shape=jax.ShapeDtypeStruct((S, D_MODEL), BF),
        compiler_params=_params(("parallel", "arbitrary")),
    )(*ys, *ws, gl, gl, gl)


def _merge_bwd(dout, w_out, ys, ws, gl, tm):
    S = gl.shape[1]

    def body(do_ref, wo_ref, y0, y1, y2, w0, w1, w2, g0, g1, g2, dg0, dg1, dg2, dp0, dp1, dp2):
        dm = _dot(do_ref[...], wo_ref[...], NT)
        for y_ref, w_ref, g_ref, dg_ref, dp_ref in ((y0, w0, g0, dg0, dp0), (y1, w1, g1, dg1, dp1), (y2, w2, g2, dg2, dp2)):
            gate = _sigmoid(g_ref[...])
            pv = _dot(y_ref[...], w_ref[...], NN)
            dg_ref[...] = (dm * pv * gate * (1.0 - gate)).astype(BF)
            dp_ref[...] = (dm * gate).astype(BF)

    ytile, wblk, gls, otile = _merge_specs(tm)
    out = jax.ShapeDtypeStruct((S, D_MODEL), BF)
    return pl.pallas_call(
        body, name="merge_bwd", grid=(S // tm, D_MODEL // MERGE_TN),
        in_specs=[pl.BlockSpec((tm, D_MODEL), lambda i, j: (i, 0)), pl.BlockSpec((MERGE_TN, D_MODEL), lambda i, j: (j, 0))]
        + [ytile] * 3 + [wblk] * 3 + gls,
        out_specs=[otile] * 6, out_shape=[out] * 6,
        compiler_params=_params(("parallel", "arbitrary")),
    )(dout, w_out, *ys, *ws, gl, gl, gl)


def _out_loss(merged, w_out, x, target, post_g, tm):
    S = x.shape[0]

    def body(m_ref, w_ref, x_ref, t_ref, g_ref, dout_ref, dy_ref, loss_ref, dpost_ref):
        @pl.when(pl.program_id(0) == 0)
        def _():
            loss_ref[...] = jnp.zeros_like(loss_ref)
            dpost_ref[...] = jnp.zeros_like(dpost_ref)

        out = _dot(m_ref[...], w_ref[...], NN)
        r = lax.rsqrt(jnp.mean(out * out, axis=-1, keepdims=True) + EPS)
        nrm = out * r
        gv = g_ref[...]
        err = (x_ref[...] + nrm * gv) - t_ref[...]
        sq = jnp.sum(jnp.sum(err * err, axis=1, keepdims=True), axis=0, keepdims=True)
        loss_ref[...] += sq * (0.5 / D_MODEL)
        dy = err * (1.0 / D_MODEL)
        dy_ref[...] = dy
        dpost_ref[...] += jnp.sum(dy * nrm, axis=0, keepdims=True)
        dn = dy * gv
        dout_ref[...] = (r * (dn - nrm * jnp.mean(dn * nrm, axis=-1, keepdims=True))).astype(BF)

    row = pl.BlockSpec((tm, D_MODEL), lambda i: (i, 0))
    return pl.pallas_call(
        body, name="out_loss", grid=(S // tm,),
        in_specs=[row, pl.BlockSpec((D_MODEL, D_MODEL), lambda i: (0, 0)), row, row, pl.BlockSpec((1, D_MODEL), lambda i: (0, 0))],
        out_specs=[row, row, pl.BlockSpec((8, 128), lambda i: (0, 0)), pl.BlockSpec((1, D_MODEL), lambda i: (0, 0))],
        out_shape=[jax.ShapeDtypeStruct((S, D_MODEL), BF), jax.ShapeDtypeStruct((S, D_MODEL), F32),
                   jax.ShapeDtypeStruct((8, 128), F32), jax.ShapeDtypeStruct((1, D_MODEL), F32)],
        compiler_params=_params(("arbitrary",)),
    )(merged, w_out, x, target, post_g)


def _dh_dx(dproj, w_in, x, dy, pre_g, tm):
    S = x.shape[0]
    nk, tk = dproj.shape[0], dproj.shape[2]

    def body(dp_ref, w_ref, x_ref, dy_ref, g_ref, dx_ref, dpre_ref, acc_ref):
        i, k = pl.program_id(0), pl.program_id(1)

        @pl.when(jnp.logical_and(i == 0, k == 0))
        def _():
            dpre_ref[...] = jnp.zeros_like(dpre_ref)

        @pl.when(k == 0)
        def _():
            acc_ref[...] = jnp.zeros_like(acc_ref)

        acc_ref[...] += _dot(dp_ref[...], w_ref[...], NN)

        @pl.when(k == nk - 1)
        def _():
            dh = acc_ref[...]
            xv = x_ref[...]
            r = lax.rsqrt(jnp.mean(xv * xv, axis=-1, keepdims=True) + EPS)
            nrm = xv * r
            dpre_ref[...] += jnp.sum(dh * nrm, axis=0, keepdims=True)
            dn = dh * g_ref[...]
            dx_ref[...] = r * (dn - nrm * jnp.mean(dn * nrm, axis=-1, keepdims=True)) + dy_ref[...]

    row = pl.BlockSpec((tm, D_MODEL), lambda i, k: (i, 0))
    vec = pl.BlockSpec((1, D_MODEL), lambda i, k: (0, 0))
    return pl.pallas_call(
        body, name="dh_dx", grid=(S // tm, nk),
        in_specs=[pl.BlockSpec((None, tm, tk), lambda i, k: (k, i, 0)), pl.BlockSpec((tk, D_MODEL), lambda i, k: (k, 0)), row, row, vec],
        out_specs=[row, vec],
        out_shape=[jax.ShapeDtypeStruct((S, D_MODEL), F32), jax.ShapeDtypeStruct((1, D_MODEL), F32)],
        scratch_shapes=[pltpu.VMEM((tm, D_MODEL), F32)],
        compiler_params=_params(("arbitrary", "arbitrary"), 56),
    )(dproj, w_in, x, dy, pre_g)


def _sum_parts(parts, name):
    P, R, C = parts.shape
    tr = max(t for t in range(8, 513, 8) if R % t == 0)

    def body(p_ref, o_ref):
        acc = p_ref[0]
        for j in range(1, P):
            acc = acc + p_ref[j]
        o_ref[...] = acc

    return pl.pallas_call(
        body, name=name, grid=(R // tr,),
        in_specs=[pl.BlockSpec((P, tr, C), lambda i: (0, i, 0))], out_specs=pl.BlockSpec((tr, C), lambda i: (i, 0)),
        out_shape=jax.ShapeDtypeStruct((R, C), F32), compiler_params=_params(("parallel",)),
    )(parts)


def _adamw(parts, w, m, v, name, own=None, chip=None):
    groups = list(parts) if isinstance(parts, (list, tuple)) else [parts]
    owns = [] if own is None else (list(own) if isinstance(own, (list, tuple)) else [own])
    P, rows, C = groups[0].shape
    R = rows * len(groups)
    tr = 128 if rows % 128 == 0 else rows
    per = rows // tr
    c1 = 1.0 - ADAM_B1 ** ADAM_STEP
    c2 = 1.0 - ADAM_B2 ** ADAM_STEP
    n_in = len(groups) * (4 if owns else 1)

    def body(*refs):
        if owns:
            refs = refs[1:]
        p_refs = refs[:n_in]
        w_ref, m_ref, v_ref, g_ref, d_ref, nm_ref, nv_ref = refs[n_in:]
        g = None
        for q in range(len(groups)):
            if owns:
                gq = p_refs[4 * q + 3][...].astype(F32)
                for j in range(3):
                    gq = gq + p_refs[4 * q + j][...].astype(F32)
            else:
                gq = p_refs[q][0].astype(F32)
                for j in range(1, P):
                    gq = gq + p_refs[q][j].astype(F32)
            g = gq if g is None else jnp.where(pl.program_id(0) // per == q, gq, g)
        nm = ADAM_B1 * m_ref[...] + (1.0 - ADAM_B1) * g
        nv = ADAM_B2 * v_ref[...] + (1.0 - ADAM_B2) * (g * g)
        g_ref[...] = g
        nm_ref[...] = nm
        nv_ref[...] = nv
        d_ref[...] = -ADAM_LR * ((nm / c1) / (jnp.sqrt(nv / c2) + ADAM_EPS) + ADAM_WD * w_ref[...])

    out = jax.ShapeDtypeStruct((R, C), F32)
    if not owns:
        tile = pl.BlockSpec((tr, C), lambda i: (i, 0))
        return pl.pallas_call(
            body, name=name, grid=(R // tr,),
            in_specs=[pl.BlockSpec((P, tr, C), (lambda i, q=q: (0, jnp.clip(i - q * per, 0, per - 1), 0))) for q in range(len(groups))]
            + [tile, tile, tile], out_specs=[tile] * 4, out_shape=[out] * 4,
            compiler_params=_params(("parallel",)),
        )(*groups, w, m, v)
    tile = pl.BlockSpec((tr, C), lambda i, c_ref: (i, 0))
    specs, operands = [], []
    for q in range(len(groups)):
        for k in range(3):
            specs.append(pl.BlockSpec((None, tr, C), (lambda i, c_ref, q=q, k=k: (k + (c_ref[0] <= k).astype(jnp.int32),
                                                                                  jnp.clip(i - q * per, 0, per - 1), 0))))
            operands.append(groups[q])
        specs.append(pl.BlockSpec((None, tr, C), (lambda i, c_ref, q=q: (c_ref[0], jnp.clip(i - q * per, 0, per - 1), 0))))
        operands.append(owns[q])
    return pl.pallas_call(
        body, name=name,
        grid_spec=pltpu.PrefetchScalarGridSpec(num_scalar_prefetch=1, grid=(R // tr,), in_specs=specs + [tile, tile, tile],
                                               out_specs=[tile] * 4),
        out_shape=[out] * 4, compiler_params=_params(("parallel",)),
    )(chip, *operands, w, m, v)


def _project(h, w_t):
    S = h.shape[0]
    n_tiles = D_IN // SEG_TILE
    ranges = [(c0 // SEG_TILE, (c0 + width) // SEG_TILE) for _, c0, width, _ in SEGMENTS]

    def body(h_ref, w_ref, *outs):
        j = pl.program_id(0)
        prod = _dot(h_ref[...], w_ref[...], NT)
        for (j0, j1), (_, _, _, dt), o_ref in zip(ranges, SEGMENTS, outs):
            @pl.when(jnp.logical_and(j >= j0, j < j1))
            def _(o_ref=o_ref, dt=dt):
                o_ref[...] = prod.astype(dt)

    out_shapes, out_specs = [], []
    for (j0, j1), (name, _, width, dt) in zip(ranges, SEGMENTS):
        if name == "gl":
            per = (j1 - j0) // 3
            out_shapes.append(jax.ShapeDtypeStruct((3, S, width // 3), dt))
            out_specs.append(pl.BlockSpec((None, S, SEG_TILE), (lambda j, j0=j0, j1=j1, per=per: (
                jnp.clip(j - j0, 0, j1 - j0 - 1) // per, 0, jnp.clip(j - j0, 0, j1 - j0 - 1) % per))))
        else:
            out_shapes.append(jax.ShapeDtypeStruct((S, width), dt))
            out_specs.append(pl.BlockSpec((S, SEG_TILE), (lambda j, j0=j0, j1=j1: (0, jnp.clip(j - j0, 0, j1 - j0 - 1)))))
    outs = pl.pallas_call(
        body, name="proj", grid=(n_tiles,),
        in_specs=[pl.BlockSpec((S, D_MODEL), lambda j: (0, 0)), pl.BlockSpec((SEG_TILE, D_MODEL), lambda j: (j, 0))],
        out_specs=out_specs, out_shape=out_shapes,
        compiler_params=_params(("arbitrary",), 56),
    )(h, w_t)
    return {name: o for (name, _, _, _), o in zip(SEGMENTS, outs)}


def _forward_a(x, mem, pre_g, mem_g, w_in, conv_w, conv_b, w_a, b_a, w_x, b_x, lam, sinks, rel_bias):
    S = x.shape[0]
    st = dict(T=min(512, S // 2), tm=min(512, S), bucket=_rel_bucket_map())
    st["h"] = _rms_fwd(x, pre_g, "pre_norm")
    st["memn"] = _rms_fwd(mem, mem_g, "mem_norm")
    seg = st["seg"] = _project(st["h"], w_in)
    st["h_rg"], st["y_rg"] = _rglru_fwd(seg["xr"], seg["g_rg"], conv_w, conv_b, w_a, b_a, w_x, b_x, lam, st["T"])
    st["o_swa"], st["y_swa"] = _swa_fwd(seg["q_s"], seg["kv"], seg["g_swa"], st["bucket"], rel_bias, _sink_column(sinks))
    return st


def _forward_b(st, x, target, post_g, w_memkv, wbr, w_out):
    S = x.shape[0]
    M = st["memn"].shape[0]
    seg = st["seg"]
    st["mkv"] = _matmul(st["memn"], w_memkv, "nn", M, 2 * D_MEM, D_MODEL, M, 512, D_MODEL, BF, "mem_kv")
    st["o_mem"], st["y_mem"] = _mem_fwd(seg["q_m"], st["mkv"], seg["g_mem"])
    st["ys"] = (st["y_rg"], st["y_swa"], st["y_mem"])
    st["merged"] = _merge_fwd(st["ys"], wbr, seg["gl"], st["tm"])
    st["dout"], st["dy"], st["loss"], st["dpost"] = _out_loss(st["merged"], w_out, x, target, post_g, min(256, S))
    return st


def _backward_a1(st, wbr, w_out):
    S = st["h"].shape[0]
    seg, ys, tm = st["seg"], st["ys"], st["tm"]
    st["dw_out"] = _matmul(st["merged"], st["dout"], "tn", D_MODEL, D_MODEL, S, 256, D_MODEL, S, BF, "dw_out", out_blocked="row")
    dgl0, dgl1, dgl2, dp0, dp1, dp2 = _merge_bwd(st["dout"], w_out, ys, wbr, seg["gl"], tm)
    st["dgl"] = (dgl0, dgl1, dgl2)
    dys, dwbr = [], []
    for i, dp in enumerate((dp0, dp1, dp2)):
        dys.append(_matmul(dp, wbr[i], "nt", S, 1024, D_MODEL, tm, 1024, D_MODEL, F32, "dy_br%d" % i))
        dwbr.append(_matmul(ys[i], dp, "tn", 1024, D_MODEL, S, 1024, 256, S, BF, "dw_br%d" % i, out_blocked="col"))
    st["dys"], st["dwbr"] = dys, dwbr
    return st


def _backward_a2(st, mem, w_memkv, conv_w, conv_b, w_a, b_a, w_x, b_x, lam):
    M = mem.shape[0]
    seg, dys = st["seg"], st["dys"]
    st["dq_m"], st["dg_mem"], dmkv = _mem_bwd(seg["q_m"], st["mkv"], seg["g_mem"], st["o_mem"], dys[2])
    dmkv_b = dmkv.astype(BF)
    st["dw_memkv"] = _matmul(st["memn"], dmkv_b, "tn", D_MODEL, 2 * D_MEM, M, 256, 2 * D_MEM, M, BF, "dw_memkv", out_blocked="row")
    dmemn = _matmul(dmkv_b, w_memkv, "nt", M, D_MODEL, 2 * D_MEM, M, 512, 2 * D_MEM, F32, "dmemn")
    st["dmem_g"] = _rms_gain_grad(dmemn, mem, "dmem_gain")
    st["dxr"], st["dg_rg"], st["dw_a"], st["dw_x"], st["dvec"] = _rglru_bwd(
        seg["xr"], seg["g_rg"], st["h_rg"], dys[0], conv_w, conv_b, w_a, b_a, w_x, b_x, lam, st["T"])
    return st


def _backward_b(st, rel_bias, sinks):
    seg = st["seg"]
    dq_s, dg_swa, dkv, dsinks, drel = _swa_bwd(seg["q_s"], seg["kv"], seg["g_swa"], st["o_swa"], st["dys"][1],
                                               st["bucket"], rel_bias, _sink_column(sinks))
    st["dsinks"], st["drel"] = dsinks.reshape(1, SWA_HEADS), drel.reshape(REL_BUCKETS, SWA_HEADS)
    dproj = jnp.concatenate([st["dxr"], st["dg_rg"], dq_s, dkv.astype(BF), dg_swa, st["dq_m"], st["dg_mem"], *st["dgl"]], axis=1)
    st["dproj"] = jnp.transpose(dproj.reshape(dproj.shape[0], N_DEV, D_IN // N_DEV), (1, 0, 2))
    return st


def _dw_in_half(st, half, dep=None):
    S = st["h"].shape[0]
    return _matmul(st["h"], st["dproj"], "tn", D_MODEL // 2, D_IN, S, 512, D_IN // N_DEV, S, BF, "dw_in%d" % half, a_moff=2 * half,
                   b_blocked=True, out_blocked="col", dep=dep)


def _owner_blocks(a):
    return jnp.swapaxes(a.reshape((4, 2) + a.shape[1:]), 0, 1)


def _local_step(x, mem, target, pre_g, post_g, mem_g, w_in, conv_w, conv_b, w_a, b_a, w_x, b_x, lam, sinks, rel_bias,
                w_memkv, wbr, w_out):
    st = _forward_a(x, mem, pre_g, mem_g, w_in, conv_w, conv_b, w_a, b_a, w_x, b_x, lam, sinks, rel_bias)
    st = _forward_b(st, x, target, post_g, w_memkv, wbr, w_out)
    st = _backward_a1(st, wbr, w_out)
    st = _backward_a2(st, mem, w_memkv, conv_w, conv_b, w_a, b_a, w_x, b_x, lam)
    st = _backward_b(st, rel_bias, sinks)
    st["dw_in"] = [_dw_in_half(st, 0), _dw_in_half(st, 1)]
    st["grad_x"], st["dpre"] = _dh_dx(st["dproj"], w_in, x, st["dy"], pre_g, st["tm"])
    return st


def _pad_rows(a, rows):
    a = a.reshape(-1, 128) if a.shape[-1] % 128 == 0 else jnp.pad(a, ((0, 0), (0, 128 - a.shape[-1])))
    return jnp.pad(a, ((0, rows - a.shape[0]), (0, 0))) if a.shape[0] < rows else a


def kernel(x, mem, pre_norm_g, post_norm_g, mem_norm_g, w_in, conv_w, conv_b, w_rg_a, b_rg_a, w_rg_x, b_rg_x, lru_lambda, swa_sinks, rel_bias, w_mem_kv, w_br_rg, w_br_swa, w_br_mem, w_out, loss_target, m_pre_norm_g, m_post_norm_g, m_mem_norm_g, m_w_in, m_conv_w, m_conv_b, m_w_rg_a, m_b_rg_a, m_w_rg_x, m_b_rg_x, m_lru_lambda, m_swa_sinks, m_rel_bias, m_w_mem_kv, m_w_br_rg, m_w_br_swa, m_w_br_mem, m_w_out, v_pre_norm_g, v_post_norm_g, v_mem_norm_g, v_w_in, v_conv_w, v_conv_b, v_w_rg_a, v_b_rg_a, v_w_rg_x, v_b_rg_x, v_lru_lambda, v_swa_sinks, v_rel_bias, v_w_mem_kv, v_w_br_rg, v_w_br_swa, v_w_br_mem, v_w_out):
    cx, cy, cc = lax.axis_index("x"), lax.axis_index("y"), lax.axis_index("c")
    me = 4 * cx + 2 * cy + cc
    chip = 2 * cx + cy
    core = jnp.reshape(cc, (1,)).astype(jnp.int32)
    x0, mem0 = x[0], mem[0]
    w_a_b, w_x_b = w_rg_a[0].astype(BF), w_rg_x[0].astype(BF)

    def landing(own, slot, slots):
        return lax.dynamic_update_slice(lax.empty((slots,) + own.shape, own.dtype), own[None], (slot,) + (0,) * own.ndim)

    def gather_zone(own, kind):
        if kind == "cols":
            return lax.dynamic_update_slice(lax.empty((own.shape[0], N_DEV * own.shape[1]), own.dtype), own, (0, me * own.shape[1]))
        return lax.empty((N_DEV,) + own.shape, own.dtype)

    def swap_start(parts, tag):
        return _exchange_start(parts, [lax.empty(p.shape[1:], p.dtype) for p in parts], _plan_swap(len(parts)), "swap_%s_start" % tag)

    def scatter_start(swap, after, tag, prefill=()):
        s_send, s_recv, parts, got, _ = swap
        got = _exchange_wait(s_send, s_recv, parts, got, _plan_swap(len(parts)), after, "swap_%s_wait" % tag)
        sums = [_pair_sum(p, g, core, "scatter_%s_sum%d" % (tag, i)) for i, (p, g) in enumerate(zip(parts, got))]
        lands = [landing(lax.dynamic_index_in_dim(s, chip, 0, keepdims=False), chip, 4) if i in prefill
                 else lax.empty(s.shape, s.dtype) for i, s in enumerate(sums)]
        return _exchange_start(sums, lands, _plan_scatter(len(sums)), "scatter_%s_start" % tag)

    def zero_after(a):
        return jnp.minimum(jnp.abs(a.reshape(-1)[0].astype(F32)), 0.0)

    g_in, g_cw = _all_gather_relayed([jnp.transpose(w_in[0]).astype(BF), conv_w[0]], [True, False], "gather_w_in")
    w_in_f = g_in.reshape(D_IN, D_MODEL)
    conv_w_f = jnp.transpose(g_cw, (1, 0, 2)).reshape(CONV_W, D_RNN)

    after_first = zero_after(g_cw).astype(BF)
    rest = [w.astype(BF) + after_first for w in (w_mem_kv[0], w_br_rg[0], w_br_swa[0], w_br_mem[0], w_out[0])]
    kinds = ["lead", "cols", "cols", "cols", "lead"]
    plan_g = _plan_gather(kinds)
    g_send, g_recv, g_src, g_land, g_token = _exchange_start(rest, [gather_zone(w, kd) for w, kd in zip(rest, kinds)], plan_g,
                                                             "gather_rest_start")
    st = _forward_a(x0, mem0, pre_norm_g + g_token[0:1, 0:1], mem_norm_g, w_in_f, conv_w_f, conv_b, w_a_b, b_rg_a, w_x_b, b_rg_x,
                    lru_lambda, swa_sinks, rel_bias)
    g_land = _exchange_wait(g_send, g_recv, g_src, g_land, plan_g, st["y_swa"], "gather_rest_wait")
    g_land = _forward_to_sibling(g_land, kinds, rest, "gather_rest_forward")
    w_memkv_f = g_land[0].reshape(D_MODEL, 2 * D_MEM)
    wbr = (g_land[1], g_land[2], g_land[3])
    w_out_f = g_land[4].reshape(D_MODEL, D_MODEL)

    st = _forward_b(st, x0, loss_target[0], post_norm_g, w_memkv_f, wbr, w_out_f)
    st = _backward_a1(st, wbr, w_out_f)
    parts_a = [st["dw_out"], st["dwbr"][0], st["dwbr"][1], st["dwbr"][2]]
    plan_a = _plan_scatter(len(parts_a))
    swap_a = swap_start(parts_a, "a")
    st = _backward_a2(st, mem0, w_memkv_f, conv_w_f, conv_b + swap_a[4][0:1, 0:1], w_a_b, b_rg_a, w_x_b, b_rg_x, lru_lambda)
    a_send, a_recv, a_src, a_land, a_token = scatter_start(swap_a, st["dxr"], "a")
    parts_c = [st["dw_memkv"], _owner_blocks(st["dw_a"]), _owner_blocks(st["dw_x"])]
    plan_c = _plan_scatter(len(parts_c))
    swap_c = swap_start(parts_c, "c")

    st = _backward_b(st, rel_bias, swa_sinks + swap_c[4][0:1, 0:1] + a_token[0:1, 0:1])
    c_send, c_recv, c_src, c_land, c_token = scatter_start(swap_c, st["dproj"], "c", prefill=(1, 2))
    plan_b = _plan_scatter(1)

    def dw_in_parts(half, dep):
        dwh = _dw_in_half(st, half, dep)
        return dwh, [dwh]

    dw0, parts_b0 = dw_in_parts(0, c_token)
    swap_b0 = swap_start(parts_b0, "b0")
    a_land = _exchange_wait(a_send, a_recv, a_src, a_land, plan_a, swap_b0[4], "scatter_a_wait")
    big = [None] * 6

    chip1 = jnp.reshape(chip, (1,)).astype(jnp.int32)

    def adamw_big(j, land, own, wt, mt, vt):
        big[j] = [a[None] for a in _adamw(land, wt[0], mt[0], vt[0], "adamw_big%d" % j, own=own, chip=chip1)]

    adamw_big(5, a_land[0], a_src[0], w_out, m_w_out, v_w_out)
    adamw_big(2, a_land[1], a_src[1], w_br_rg, m_w_br_rg, v_w_br_rg)
    adamw_big(3, a_land[2], a_src[2], w_br_swa, m_w_br_swa, v_w_br_swa)
    halves = [scatter_start(swap_b0, big[3][1], "b0")]
    dw1, parts_b1 = dw_in_parts(1, halves[0][4])
    swap_b1 = swap_start(parts_b1, "b1")
    c_land = _exchange_wait(c_send, c_recv, c_src, c_land, plan_c, swap_b1[4], "scatter_c_wait")
    g_wa_blk = _sum_parts(c_land[1], "sum_w_rg_a")
    g_wx_blk = _sum_parts(c_land[2], "sum_w_rg_x")
    adamw_big(4, a_land[3], a_src[3], w_br_mem, m_w_br_mem, v_w_br_mem)
    adamw_big(1, c_land[0], c_src[0], w_mem_kv, m_w_mem_kv, v_w_mem_kv)
    halves.append(scatter_start(swap_b1, big[1][1], "b1"))
    grad_x, dpre = _dh_dx(st["dproj"], w_in_f, x0, st["dy"], pre_norm_g + halves[1][4][0:1, 0:1], st["tm"])
    after, b_lands, b_sums = grad_x, [], []
    for half, (b_send, b_recv, b_src, b_land, _) in enumerate(halves):
        b_lands.append(_exchange_wait(b_send, b_recv, b_src, b_land, plan_b, after, "scatter_b%d_wait" % half)[0])
        b_sums.append(b_src[0])
        after = b_lands[-1]
    big[0] = [a[None] for a in _adamw(b_lands, w_in[0], m_w_in[0], v_w_in[0], "adamw_big0", own=b_sums, chip=chip1)]
    links_free = jnp.minimum(jnp.abs(big[0][0][0, 0, 0]), 0.0)

    pack = jnp.concatenate([dpre.reshape(16, 128), st["dpost"].reshape(16, 128), st["dmem_g"].reshape(16, 128),
                            st["dvec"].reshape(64, 128), _pad_rows(st["dsinks"], 8), _pad_rows(st["drel"], 32), g_wa_blk, g_wx_blk], axis=0) + links_free
    gathered = _all_gather([pack], "gather_small")[0]
    gs = _sum_parts(gathered, "sum_small")
    g_pre, g_post, g_memg = gs[0:16].reshape(1, D_MODEL), gs[16:32].reshape(1, D_MODEL), gs[32:48].reshape(1, D_MODEL)
    gvec = gs[48:112].reshape(8, D_RNN)
    g_conv_w = lax.dynamic_slice(gvec[0:CONV_W], (0, me * RNN_BLOCK), (CONV_W, RNN_BLOCK))
    g_conv_b, g_b_a, g_b_x, g_lam = gvec[4:5], gvec[5:6], gvec[6:7], gvec[7:8]
    g_sinks = gs[112:113, :SWA_HEADS]
    g_rel = gs[120:152, :SWA_HEADS]
    g_w_a = gathered[:, 152:280]
    g_w_x = gathered[:, 280:408]

    def packed(ts):
        pre, post, memg, cb, ba, bx, lm, wa, wx, sk, rel, cw = ts
        return jnp.concatenate([pre.reshape(16, 128), post.reshape(16, 128), memg.reshape(16, 128), cb.reshape(8, 128),
                                ba.reshape(8, 128), bx.reshape(8, 128), lm.reshape(8, 128), wa.reshape(1024, 128),
                                wx.reshape(1024, 128), _pad_rows(sk.reshape(1, SWA_HEADS), 8), _pad_rows(rel, 32),
                                _pad_rows(cw.reshape(CONV_W, RNN_BLOCK), 8)], axis=0)

    def unpacked(a):
        return (a[0:16].reshape(1, D_MODEL), a[16:32].reshape(1, D_MODEL), a[32:48].reshape(1, D_MODEL), a[48:56].reshape(1, D_RNN),
                a[56:64].reshape(1, D_RNN), a[64:72].reshape(1, D_RNN), a[72:80].reshape(1, D_RNN),
                a[80:1104].reshape(1, RNN_BLOCKS, RNN_BLOCK, RNN_BLOCK), a[1104:2128].reshape(1, RNN_BLOCKS, RNN_BLOCK, RNN_BLOCK),
                a[2128:2129, :SWA_HEADS], a[2136:2168, :SWA_HEADS], a[2168:2172].reshape(1, CONV_W, RNN_BLOCK))

    g_small = (g_pre, g_post, g_memg, g_conv_b, g_b_a, g_b_x, g_lam, g_w_a, g_w_x, g_sinks, g_rel, g_conv_w)
    w_small = (pre_norm_g, post_norm_g, mem_norm_g, conv_b, b_rg_a, b_rg_x, lru_lambda, w_rg_a, w_rg_x, swa_sinks, rel_bias, conv_w)
    m_small = (m_pre_norm_g, m_post_norm_g, m_mem_norm_g, m_conv_b, m_b_rg_a, m_b_rg_x, m_lru_lambda, m_w_rg_a, m_w_rg_x, m_swa_sinks, m_rel_bias, m_conv_w)
    v_small = (v_pre_norm_g, v_post_norm_g, v_mem_norm_g, v_conv_b, v_b_rg_a, v_b_rg_x, v_lru_lambda, v_w_rg_a, v_w_rg_x, v_swa_sinks, v_rel_bias, v_conv_w)
    sm = [unpacked(a) for a in _adamw(packed(g_small)[None], packed(w_small), packed(m_small), packed(v_small), "adamw_small")]


    loss_total = lax.psum(st["loss"][0, 0], AXES)

    def leaves(k):
        s = sm[k]
        return [s[0], s[1], s[2], big[0][k], s[11], s[3], s[7], s[4], s[8], s[5], s[6], s[9], s[10],
                big[1][k], big[2][k], big[3][k], big[4][k], big[5][k]]

    return (loss_total, grad_x[None], *leaves(0), *leaves(1), *leaves(2), *leaves(3))
```

```python
import math

import jax
import jax.numpy as jnp
import numpy as np
from jax import lax
from jax.experimental import pallas as pl
from jax.experimental.pallas import tpu as pltpu

F32, BF = jnp.float32, jnp.bfloat16
MESH = pl.DeviceIdType.MESH
AXES = ("x", "y", "c")
N_DEV = 8

D_MODEL = 2048
D_RNN = 1024
RNN_BLOCKS = 8
RNN_BLOCK = 128
CONV_W = 4
LRU_C = 8.0
SWA_HEADS = 16
SWA_KV_HEADS = 2
SWA_HD = 64
WINDOW = 128
MEM_HEADS = 4
MEM_HD = 256
D_MEM = 1024
REL_BUCKETS = 32
REL_MAX_DIST = 128
EPS = 1e-6
NEG_INF = -1e30
D_IN = 12544
SEGMENTS = (("xr", 0, 1024, F32), ("g_rg", 1024, 1024, F32), ("q_s", 2048, 1024, BF), ("kv", 3072, 256, BF),
            ("g_swa", 3328, 1024, F32), ("q_m", 4352, 1024, BF), ("g_mem", 5376, 1024, F32), ("gl", 6400, 6144, F32))
SEG_TILE = 256

ADAM_LR, ADAM_B1, ADAM_B2, ADAM_EPS, ADAM_WD, ADAM_STEP = 0.001, 0.9, 0.999, 1e-08, 0.01, 10

NN = (((1,), (0,)), ((), ()))
NT = (((1,), (1,)), ((), ()))
TN = (((0,), (0,)), ((), ()))
MIB = 2 ** 20


def _dot(a, b, dn):
    return lax.dot_general(a, b, dn, preferred_element_type=F32)


def _params(sem, vmem_mib=48):
    return pltpu.CompilerParams(dimension_semantics=sem, vmem_limit_bytes=vmem_mib * MIB)


def _sigmoid(z):
    return 1.0 / (1.0 + jnp.exp(-z))


def _softplus(z):
    return jnp.maximum(z, 0.0) + jnp.log(1.0 + jnp.exp(-jnp.abs(z)))


def _expm1(z):
    p = z * (1.0 + z * (0.5 + z * (1.0 / 6 + z * (1.0 / 24 + z * (1.0 / 120 + z * (1.0 / 720 + z * (1.0 / 5040 + z / 40320)))))))
    return jnp.where(jnp.abs(z) < 0.3, p, jnp.exp(z) - 1.0)


def _flat(p):
    return 4 * p[0] + 2 * p[1] + p[2]


def _all_gather(arrs, name):
    n = len(arrs)

    def body(*refs):
        ins, outs = refs[:n], refs[n:2 * n]
        send_sems, recv_sems, local_sems = refs[2 * n:]
        x, y, c = lax.axis_index("x"), lax.axis_index("y"), lax.axis_index("c")
        me, sibling = (x, y, c), (x, y, 1 - c)
        chips = [(1 - x, y), (x, 1 - y), (1 - x, 1 - y)]

        def copy(a, k, block, to, src=None):
            dst = outs[a].at[_flat(block)]
            return pltpu.make_async_remote_copy(src_ref=dst if src is None else src, dst_ref=dst,
                                                send_sem=send_sems.at[a * 7 + k], recv_sem=recv_sems.at[a * 7 + k],
                                                device_id=to, device_id_type=MESH)

        mine = [pltpu.make_async_copy(ins[a], outs[a].at[_flat(me)], local_sems.at[a]) for a in range(n)]
        for cp in mine:
            cp.start()
        first = []
        for a in range(n):
            first += [copy(a, 1 + j, me, (*chip, c), src=ins[a]) for j, chip in enumerate(chips)]
            first.append(copy(a, 0, me, sibling, src=ins[a]))
        for cp in first:
            cp.start()
        passed = []
        for j, chip in enumerate(chips):
            for a in range(n):
                copy(a, 1 + j, (*chip, c), me).wait_recv()
                fw = copy(a, 4 + j, (*chip, c), sibling)
                fw.start()
                passed.append(fw)
        for a in range(n):
            copy(a, 0, sibling, me).wait_recv()
            for j, chip in enumerate(chips):
                copy(a, 4 + j, (*chip, 1 - c), me).wait_recv()
        for cp in first + passed:
            cp.wait_send()
        for cp in mine:
            cp.wait()

    any_spec = pl.BlockSpec(memory_space=pl.ANY)
    return pl.pallas_call(
        body, name=name,
        out_shape=[jax.ShapeDtypeStruct((N_DEV,) + a.shape, a.dtype) for a in arrs],
        in_specs=[any_spec] * n, out_specs=[any_spec] * n,
        scratch_shapes=[pltpu.SemaphoreType.DMA((7 * n,)), pltpu.SemaphoreType.DMA((7 * n,)), pltpu.SemaphoreType.DMA((n,))],
    )(*arrs)


def _all_gather_relayed(arrs, relay, name):
    n = len(arrs)
    K = 9

    def body(*refs):
        ins, outs = refs[:n], refs[n:2 * n]
        send_sems, recv_sems, local_sems = refs[2 * n:]
        x, y, c = lax.axis_index("x"), lax.axis_index("y"), lax.axis_index("c")
        me, sib = (x, y, c), (x, y, 1 - c)
        xn, yn, dg = (1 - x, y, c), (x, 1 - y, c), (1 - x, 1 - y, c)

        def other(p):
            return (p[0], p[1], 1 - p[2])

        def rows(a, half):
            h = arrs[a].shape[0] // 2
            return pl.ds(half * h, h)

        def copy(a, k, block, to, half=None, src=None):
            dst = outs[a].at[_flat(block)]
            if half is not None:
                dst = dst.at[rows(a, half)]
            return pltpu.make_async_remote_copy(src_ref=dst if src is None else src, dst_ref=dst,
                                                send_sem=send_sems.at[a * K + k], recv_sem=recv_sems.at[a * K + k],
                                                device_id=to, device_id_type=MESH)

        mine = [pltpu.make_async_copy(ins[a], outs[a].at[_flat(me)], local_sems.at[a]) for a in range(n)]
        for cp in mine:
            cp.start()
        sends = []

        def start(cp):
            cp.start()
            sends.append(cp)

        for a in range(n):
            start(copy(a, 1, me, xn, src=ins[a]))
            start(copy(a, 2, me, yn, src=ins[a]))
            if not relay[a]:
                start(copy(a, 3, me, dg, src=ins[a]))
            start(copy(a, 0, me, sib, src=ins[a]))
        for a in range(n):
            copy(a, 1, xn, me).wait_recv()
            if relay[a]:
                start(copy(a, 3, xn, yn, half=0))
            start(copy(a, 5, xn, sib))
        for a in range(n):
            copy(a, 2, yn, me).wait_recv()
            if relay[a]:
                start(copy(a, 4, yn, xn, half=1))
            start(copy(a, 6, yn, sib))
        for a in range(n):
            if relay[a]:
                copy(a, 3, dg, me, half=0).wait_recv()
                start(copy(a, 7, dg, sib, half=0))
                copy(a, 4, dg, me, half=1).wait_recv()
                start(copy(a, 8, dg, sib, half=1))
            else:
                copy(a, 3, dg, me).wait_recv()
                start(copy(a, 7, dg, sib))
        for a in range(n):
            copy(a, 0, sib, me).wait_recv()
            copy(a, 5, other(xn), me).wait_recv()
            copy(a, 6, other(yn), me).wait_recv()
            if relay[a]:
                copy(a, 7, other(dg), me, half=0).wait_recv()
                copy(a, 8, other(dg), me, half=1).wait_recv()
            else:
                copy(a, 7, other(dg), me).wait_recv()
        for cp in sends:
            cp.wait_send()
        for cp in mine:
            cp.wait()

    any_spec = pl.BlockSpec(memory_space=pl.ANY)
    return pl.pallas_call(
        body, name=name,
        out_shape=[jax.ShapeDtypeStruct((N_DEV,) + a.shape, a.dtype) for a in arrs],
        in_specs=[any_spec] * n, out_specs=[any_spec] * n,
        scratch_shapes=[pltpu.SemaphoreType.DMA((K * n,)), pltpu.SemaphoreType.DMA((K * n,)), pltpu.SemaphoreType.DMA((n,))],
    )(*arrs)


def _chip_peers(x, y):
    return [(1 - x, y), (x, 1 - y), (1 - x, 1 - y)]


def _chip(p):
    return 2 * p[0] + p[1]


def _plan_gather(kinds):
    def plan(x, y, c):
        out = []
        for a, kind in enumerate(kinds):
            for peer in [(x, y, 1 - c)] + [(*ch, c) for ch in _chip_peers(x, y)]:
                out.append((a, None, (kind, _flat((x, y, c))), peer, (kind, _flat(peer))))
        return out
    return plan


def _plan_swap(n):
    def plan(x, y, c):
        return [(a, 1 - c, ("all", 0), (x, y, 1 - c), ("all", 0)) for a in range(n)]
    return plan


def _slot(ref, where):
    kind, k = where
    if kind == "all":
        return ref
    if kind == "lead":
        return ref.at[k]
    return ref.at[:, pl.ds(pl.multiple_of(k * 256, 256), 256)]


def _plan_scatter(n):
    def plan(x, y, c):
        out = []
        for a in range(n):
            for ch in _chip_peers(x, y):
                out.append((a, _chip(ch), ("lead", _chip((x, y))), (*ch, c), ("lead", _chip(ch))))
        return out
    return plan


HBM_SPEC = pl.BlockSpec(memory_space=pltpu.HBM)
SEM_SPEC = pl.BlockSpec(memory_space=pltpu.SEMAPHORE)


def _in_hbm(a):
    return pltpu.with_memory_space_constraint(a, pltpu.HBM)


def _exchange_start(srcs, lands, plan, name):
    n = len(srcs)
    count = len(plan(0, 0, 0))

    def body(*refs):
        src_refs, land_refs = refs[:n], refs[n:2 * n]
        send_sems, recv_sems = refs[2 * n], refs[2 * n + 1]
        token = refs[-1]
        x, y, c = lax.axis_index("x"), lax.axis_index("y"), lax.axis_index("c")
        for k, (a, si, di, peer, _) in enumerate(plan(x, y, c)):
            src = src_refs[a] if si is None else src_refs[a].at[si]
            pltpu.make_async_remote_copy(src_ref=src, dst_ref=_slot(land_refs[a], di), send_sem=send_sems.at[k],
                                         recv_sem=recv_sems.at[k], device_id=peer, device_id_type=MESH).start()
        token[...] = jnp.zeros_like(token)

    out = pl.pallas_call(
        body, name=name,
        out_shape=(pltpu.SemaphoreType.DMA((count,)), pltpu.SemaphoreType.DMA((count,)),
                   *[pltpu.HBM(a.shape, a.dtype) for a in lands], jax.ShapeDtypeStruct((8, 128), F32)),
        in_specs=[HBM_SPEC] * (2 * n),
        out_specs=(SEM_SPEC, SEM_SPEC, *([HBM_SPEC] * n), pl.BlockSpec(memory_space=pltpu.VMEM)),
        input_output_aliases={n + i: 2 + i for i in range(n)},
        compiler_params=pltpu.CompilerParams(has_side_effects=pltpu.SideEffectType.DATAFLOW_SIDE_EFFECTING),
    )(*[_in_hbm(a) for a in srcs], *[_in_hbm(a) for a in lands])
    return out[0], out[1], list(srcs), list(out[2:2 + n]), out[-1]


def _exchange_wait(send_sems, recv_sems, srcs, lands, plan, after, name):
    n = len(srcs)

    def body(*refs):
        src_refs, land_refs = refs[:n], refs[n:2 * n]
        send_sems, recv_sems = refs[2 * n], refs[2 * n + 1]
        x, y, c = lax.axis_index("x"), lax.axis_index("y"), lax.axis_index("c")
        for k, (a, si, _, peer, ri) in enumerate(plan(x, y, c)):
            src = src_refs[a] if si is None else src_refs[a].at[si]
            cp = pltpu.make_async_remote_copy(src_ref=src, dst_ref=_slot(land_refs[a], ri), send_sem=send_sems.at[k],
                                              recv_sem=recv_sems.at[k], device_id=peer, device_id_type=MESH)
            cp.wait_send()
            cp.wait_recv()

    out = pl.pallas_call(
        body, name=name,
        out_shape=tuple(pltpu.HBM(a.shape, a.dtype) for a in lands),
        in_specs=[HBM_SPEC] * (2 * n) + [SEM_SPEC, SEM_SPEC, pl.BlockSpec(memory_space=pl.ANY)],
        out_specs=tuple([HBM_SPEC] * n),
        input_output_aliases={n + i: i for i in range(n)},
        compiler_params=pltpu.CompilerParams(has_side_effects=pltpu.SideEffectType.DATAFLOW_SIDE_EFFECTING),
    )(*[_in_hbm(a) for a in srcs], *lands, send_sems, recv_sems, after)
    return list(out)


def _forward_to_sibling(lands, kinds, owns, name):
    n = len(lands)

    def body(*refs):
        in_refs, own_refs, out_refs = refs[:n], refs[n:2 * n], refs[2 * n:3 * n]
        send_sems, recv_sems, local_sems = refs[3 * n:]
        x, y, c = lax.axis_index("x"), lax.axis_index("y"), lax.axis_index("c")
        sibling = (x, y, 1 - c)
        mine = [pltpu.make_async_copy(own_refs[a], _slot(out_refs[a], (kinds[a], _flat((x, y, c)))), local_sems.at[a])
                for a in range(n) if kinds[a] == "lead"]
        for cp in mine:
            cp.start()

        def copy(a, j, slot):
            return pltpu.make_async_remote_copy(src_ref=_slot(in_refs[a], (kinds[a], slot)), dst_ref=_slot(out_refs[a], (kinds[a], slot)),
                                                send_sem=send_sems.at[a * 3 + j], recv_sem=recv_sems.at[a * 3 + j],
                                                device_id=sibling, device_id_type=MESH)

        sends = [copy(a, j, _flat((*ch, c))) for a in range(n) for j, ch in enumerate(_chip_peers(x, y))]
        for cp in sends:
            cp.start()
        for a in range(n):
            for j, ch in enumerate(_chip_peers(x, y)):
                copy(a, j, _flat((*ch, 1 - c))).wait_recv()
        for cp in sends:
            cp.wait_send()
        for cp in mine:
            cp.wait()

    any_spec = pl.BlockSpec(memory_space=pl.ANY)
    return pl.pallas_call(
        body, name=name, out_shape=[jax.ShapeDtypeStruct(a.shape, a.dtype) for a in lands],
        in_specs=[any_spec] * (2 * n), out_specs=[any_spec] * n, input_output_aliases={a: a for a in range(n)},
        scratch_shapes=[pltpu.SemaphoreType.DMA((3 * n,)), pltpu.SemaphoreType.DMA((3 * n,)), pltpu.SemaphoreType.DMA((n,))],
    )(*lands, *owns)


def _swap_with_sibling(parts, name):
    n = len(parts)

    def body(*refs):
        in_refs, out_refs = refs[:n], refs[n:2 * n]
        send_sems, recv_sems = refs[2 * n:]
        x, y, c = lax.axis_index("x"), lax.axis_index("y"), lax.axis_index("c")
        sends = [pltpu.make_async_remote_copy(src_ref=in_refs[a].at[1 - c], dst_ref=out_refs[a], send_sem=send_sems.at[a],
                                              recv_sem=recv_sems.at[a], device_id=(x, y, 1 - c), device_id_type=MESH)
                 for a in range(n)]
        for cp in sends:
            cp.start()
        for cp in sends:
            cp.wait()

    any_spec = pl.BlockSpec(memory_space=pl.ANY)
    return pl.pallas_call(
        body, name=name, out_shape=[jax.ShapeDtypeStruct(a.shape[1:], a.dtype) for a in parts],
        in_specs=[any_spec] * n, out_specs=[any_spec] * n,
        scratch_shapes=[pltpu.SemaphoreType.DMA((n,)), pltpu.SemaphoreType.DMA((n,))],
    )(*parts)


def _pair_sum(parts, got, core, name):
    _, _, R, C = parts.shape
    tr = 256 if R % 256 == 0 else R

    def body(c_ref, p_ref, g_ref, o_ref):
        o_ref[...] = (p_ref[...].astype(F32) + g_ref[...].astype(F32)).astype(o_ref.dtype)

    return pl.pallas_call(
        body, name=name,
        grid_spec=pltpu.PrefetchScalarGridSpec(
            num_scalar_prefetch=1, grid=(4, R // tr),
            in_specs=[pl.BlockSpec((None, None, tr, C), lambda j, i, c_ref: (c_ref[0], j, i, 0)),
                      pl.BlockSpec((None, tr, C), lambda j, i, c_ref: (j, i, 0))],
            out_specs=pl.BlockSpec((None, tr, C), lambda j, i, c_ref: (j, i, 0))),
        out_shape=jax.ShapeDtypeStruct((4, R, C), parts.dtype),
        compiler_params=_params(("parallel", "parallel")),
    )(core, parts, got)


def _matmul(a, b, mode, M, N, K, tm, tn, tk, out_dtype, name, b_noff=0, a_moff=0, a_koff=0, b_blocked=False, out_blocked=None,
            dep=None, addend=None, vmem_mib=48):
    nm, nn, nk = M // tm, N // tn, K // tk
    if mode == "nn":
        a_spec = pl.BlockSpec((tm, tk), lambda j, i, k: (i, k + a_koff))
        b_spec = pl.BlockSpec((tk, tn), lambda j, i, k: (k, j + b_noff))
        dn = NN
    elif mode == "nt":
        a_spec = pl.BlockSpec((tm, tk), lambda j, i, k: (i, k + a_koff))
        if b_blocked:
            b_spec = pl.BlockSpec((None, tn, tk), lambda j, i, k: (k, j, 0))
        else:
            b_spec = pl.BlockSpec((tn, tk), lambda j, i, k: (j + b_noff, k))
        dn = NT
    else:
        a_spec = pl.BlockSpec((tk, tm), lambda j, i, k: (k, i + a_moff))
        if b_blocked:
            b_spec = pl.BlockSpec((None, tk, tn), lambda j, i, k: (j, k, 0))
        else:
            b_spec = pl.BlockSpec((tk, tn), lambda j, i, k: (k, j + b_noff))
        dn = TN
    if out_blocked == "col":
        out_shape = jax.ShapeDtypeStruct((2, 4, M, tn), out_dtype)
        out_spec = pl.BlockSpec((None, None, tm, tn), lambda j, i, k: (j % 2, j // 2, i, 0))
    elif out_blocked == "row":
        out_shape = jax.ShapeDtypeStruct((2, 4, tm, N), out_dtype)
        out_spec = pl.BlockSpec((None, None, tm, tn), lambda j, i, k: (i % 2, i // 2, 0, j))
    elif out_blocked == "third":
        out_shape = jax.ShapeDtypeStruct((3, M, N // 3), out_dtype)
        out_spec = pl.BlockSpec((None, tm, tn), lambda j, i, k: (j // (nn // 3), i, j % (nn // 3)))
    else:
        out_shape = jax.ShapeDtypeStruct((M, N), out_dtype)
        out_spec = pl.BlockSpec((tm, tn), lambda j, i, k: (i, j))

    n_extra = (addend is not None) + (dep is not None)

    def body(a_ref, b_ref, *rest):
        o_ref, scratch = rest[n_extra], rest[n_extra + 1:]
        if nk == 1:
            prod = _dot(a_ref[...], b_ref[...], dn)
            if addend is not None:
                prod = prod + rest[0][...].astype(F32)
            o_ref[...] = prod.astype(out_dtype)
        else:
            assert addend is None
            acc_ref, = scratch
            k = pl.program_id(2)

            @pl.when(k == 0)
            def _():
                acc_ref[...] = jnp.zeros_like(acc_ref)

            acc_ref[...] += _dot(a_ref[...], b_ref[...], dn)

            @pl.when(k == nk - 1)
            def _():
                o_ref[...] = acc_ref[...].astype(out_dtype)

    return pl.pallas_call(
        body, name=name, grid=(nn, nm, nk),
        in_specs=[a_spec, b_spec] + ([] if addend is None else [out_spec])
        + ([] if dep is None else [pl.BlockSpec((8, 128), lambda j, i, k: (0, 0))]),
        out_specs=out_spec, out_shape=out_shape,
        scratch_shapes=[] if nk == 1 else [pltpu.VMEM((tm, tn), F32)],
        compiler_params=_params(("parallel", "parallel", "arbitrary"), vmem_mib),
    )(a, b, *([] if addend is None else [addend]), *([] if dep is None else [dep]))


def _rms_fwd(x, g, name):
    R, Dm = x.shape
    tr = min(R, 256)

    def body(x_ref, g_ref, h_ref):
        xv = x_ref[...]
        r = lax.rsqrt(jnp.mean(xv * xv, axis=-1, keepdims=True) + EPS)
        h_ref[...] = (xv * r * g_ref[...]).astype(BF)

    return pl.pallas_call(
        body, name=name, grid=(R // tr,),
        in_specs=[pl.BlockSpec((tr, Dm), lambda i: (i, 0)), pl.BlockSpec((1, Dm), lambda i: (0, 0))],
        out_specs=pl.BlockSpec((tr, Dm), lambda i: (i, 0)), out_shape=jax.ShapeDtypeStruct((R, Dm), BF),
        compiler_params=_params(("parallel",)),
    )(x, g)


def _rms_gain_grad(dn, x, name):
    R, Dm = x.shape

    def body(dn_ref, x_ref, o_ref):
        xv = x_ref[...]
        r = lax.rsqrt(jnp.mean(xv * xv, axis=-1, keepdims=True) + EPS)
        o_ref[...] = jnp.sum(dn_ref[...] * xv * r, axis=0, keepdims=True)

    return pl.pallas_call(
        body, name=name, out_shape=jax.ShapeDtypeStruct((1, Dm), F32),
        compiler_params=pltpu.CompilerParams(vmem_limit_bytes=32 * MIB),
    )(dn, x)


def _shift_down(v, k, head8, row, T):
    if k == 0:
        return v
    r = pltpu.roll(v, k, 0)
    hr = pltpu.roll(head8, k, 0)
    top = jnp.where(row[:8] < k, hr, r[:8])
    return jnp.concatenate([top, r[8:]], axis=0)


def _shift_up(v, k, tail8, row, T):
    if k == 0:
        return v
    r = pltpu.roll(v, T - k, 0)
    tr = pltpu.roll(tail8, 8 - k, 0)
    bot = jnp.where(row[:8] >= 8 - k, tr, r[T - 8:])
    return jnp.concatenate([r[:T - 8], bot], axis=0)


def _rglru_gates(u, head8, grow, row, T, cw_ref, cb_ref, wa_ref, ba_ref, wx_ref, bx_ref, lam_ref):
    us = [_shift_down(u, k, head8, row, T) for k in range(CONV_W)]
    acc = us[0] * cw_ref[0:1, :]
    for k in range(1, CONV_W):
        acc = acc + us[k] * cw_ref[k:k + 1, :]
    conv = cb_ref[...] + acc
    cbf = conv.astype(BF)
    r_ = _sigmoid(_dot(cbf, wa_ref[0], NN) + ba_ref[...])
    i_ = _sigmoid(_dot(cbf, wx_ref[0], NN) + bx_ref[...])
    sp = _softplus(-lam_ref[...])
    la = -LRU_C * r_ * sp
    a = jnp.exp(la)
    mult_raw = jnp.sqrt(-_expm1(2.0 * la))
    mult = jnp.where(grow == 0, 1.0, mult_raw)
    return us, conv, cbf, r_, i_, sp, a, mult_raw, mult


def _rglru_specs(T, nt, rev):
    tmap = (lambda n, t: (nt - 1 - t, n)) if rev else (lambda n, t: (t, n))
    hmap = ((lambda n, t: (jnp.maximum((nt - 1 - t) * (T // 8) - 1, 0), n)) if rev
            else (lambda n, t: (jnp.maximum(t * (T // 8) - 1, 0), n)))
    tile = pl.BlockSpec((T, RNN_BLOCK), tmap)
    halo = pl.BlockSpec((8, RNN_BLOCK), hmap)
    vec = pl.BlockSpec((1, RNN_BLOCK), lambda n, t: (0, n))
    cw = pl.BlockSpec((CONV_W, RNN_BLOCK), lambda n, t: (0, n))
    wblk = pl.BlockSpec((1, RNN_BLOCK, RNN_BLOCK), lambda n, t: (n, 0, 0))
    return tile, halo, vec, cw, wblk


def _rglru_fwd(xr, g, cw, cb, wa, ba, wx, bx, lam, T):
    S = xr.shape[0]
    nt = S // T

    def body(u_ref, uh_ref, g_ref, cw_ref, cb_ref, wa_ref, ba_ref, wx_ref, bx_ref, lam_ref, h_ref, y_ref, carry):
        t = pl.program_id(1)

        @pl.when(t == 0)
        def _():
            carry[...] = jnp.zeros_like(carry)

        row = lax.broadcasted_iota(jnp.int32, (T, RNN_BLOCK), 0)
        grow = row + t * T
        head8 = jnp.where(t > 0, uh_ref[...], 0.0)
        _, conv, _, _, i_, _, a, _, mult = _rglru_gates(u_ref[...], head8, grow, row, T, cw_ref, cb_ref, wa_ref, ba_ref,
                                                         wx_ref, bx_ref, lam_ref)
        b = mult * i_ * conv
        s = 1
        while s < T:
            keep = row >= s
            a_s = jnp.where(keep, pltpu.roll(a, s, 0), 1.0)
            b_s = jnp.where(keep, pltpu.roll(b, s, 0), 0.0)
            b = a * b_s + b
            a = a * a_s
            s *= 2
        h = b + a * carry[0:1, :]
        carry[...] = jnp.broadcast_to(h[T - 1:T, :], carry.shape)
        h_ref[...] = h
        gv = g_ref[...]
        y_ref[...] = (h * (gv * _sigmoid(gv))).astype(BF)

    tile, halo, vec, cwspec, wblk = _rglru_specs(T, nt, False)
    return pl.pallas_call(
        body, name="rglru_fwd", grid=(RNN_BLOCKS, nt),
        in_specs=[tile, halo, tile, cwspec, vec, wblk, vec, wblk, vec, vec],
        out_specs=[tile, tile],
        out_shape=[jax.ShapeDtypeStruct((S, D_RNN), F32), jax.ShapeDtypeStruct((S, D_RNN), BF)],
        scratch_shapes=[pltpu.VMEM((8, RNN_BLOCK), F32)],
        compiler_params=_params(("parallel", "arbitrary")),
    )(xr, xr, g, cw, cb, wa, ba, wx, bx, lam)


def _rglru_bwd(xr, g, h, dy, cw, cb, wa, ba, wx, bx, lam, T):
    S = xr.shape[0]
    nt = S // T

    def body(u_ref, uh_ref, g_ref, h_ref, hh_ref, dy_ref, cw_ref, cb_ref, wa_ref, ba_ref, wx_ref, bx_ref, lam_ref,
             du_ref, dg_ref, dwa_ref, dwx_ref, dvec_ref, c_dhh, c_a, c_dconv):
        t = pl.program_id(1)
        tt = nt - 1 - t

        @pl.when(t == 0)
        def _():
            c_dhh[...] = jnp.zeros_like(c_dhh)
            c_a[...] = jnp.zeros_like(c_a)
            c_dconv[...] = jnp.zeros_like(c_dconv)
            dwa_ref[...] = jnp.zeros_like(dwa_ref)
            dwx_ref[...] = jnp.zeros_like(dwx_ref)
            dvec_ref[...] = jnp.zeros_like(dvec_ref)

        row = lax.broadcasted_iota(jnp.int32, (T, RNN_BLOCK), 0)
        row8 = row[:8]
        grow = row + tt * T
        head8 = jnp.where(tt > 0, uh_ref[...], 0.0)
        us, conv, cbf, r_, i_, sp, a, mult_raw, mult = _rglru_gates(
            u_ref[...], head8, grow, row, T, cw_ref, cb_ref, wa_ref, ba_ref, wx_ref, bx_ref, lam_ref)
        hv = h_ref[...]
        hprev = _shift_down(hv, 1, jnp.where(tt > 0, hh_ref[...], 0.0), row, T)
        gv = g_ref[...]
        sg = _sigmoid(gv)
        dyv = dy_ref[...]
        dg_ref[...] = (dyv * hv * (sg * (1.0 + gv * (1.0 - sg)))).astype(BF)
        d = dyv * (gv * sg)
        A = _shift_up(a, 1, c_a[...], row, T)
        s = 1
        while s < T:
            keep = row < T - s
            A_s = jnp.where(keep, pltpu.roll(A, T - s, 0), 1.0)
            d_s = jnp.where(keep, pltpu.roll(d, T - s, 0), 0.0)
            d = A * d_s + d
            A = A * A_s
            s *= 2
        dhh = d + A * c_dhh[0:1, :]
        da = dhh * hprev
        dconv = dhh * mult * i_
        di = dhh * mult * conv
        dmult = dhh * i_ * conv
        dla = da * a - jnp.where(grow == 0, 0.0, dmult * (a * a) / mult_raw)
        dr = dla * (-LRU_C * sp)
        dsp = jnp.sum(dla * (-LRU_C * r_), axis=0, keepdims=True)
        dza = dr * r_ * (1.0 - r_)
        dzx = di * i_ * (1.0 - i_)
        dza_b, dzx_b = dza.astype(BF), dzx.astype(BF)
        dconv = dconv + _dot(dza_b, wa_ref[0], NT) + _dot(dzx_b, wx_ref[0], NT)
        dwa_ref[0] += _dot(cbf, dza_b, TN)
        dwx_ref[0] += _dot(cbf, dzx_b, TN)
        lam = lam_ref[...]
        rows = [jnp.sum(dconv * us[k], axis=0, keepdims=True) for k in range(CONV_W)]
        rows += [jnp.sum(dconv, axis=0, keepdims=True), jnp.sum(dza, axis=0, keepdims=True),
                 jnp.sum(dzx, axis=0, keepdims=True), dsp * (-_sigmoid(-lam))]
        upd = jnp.zeros((8, RNN_BLOCK), F32)
        for j, rv in enumerate(rows):
            upd = upd + jnp.where(row8 == j, rv, 0.0)
        dvec_ref[...] += upd
        tail8 = c_dconv[...]
        du = dconv * cw_ref[0:1, :]
        for k in range(1, CONV_W):
            du = du + _shift_up(dconv, k, tail8, row, T) * cw_ref[k:k + 1, :]
        du_ref[...] = du.astype(BF)
        c_dhh[...] = jnp.broadcast_to(dhh[0:1, :], c_dhh.shape)
        c_a[...] = jnp.broadcast_to(a[0:1, :], c_a.shape)
        c_dconv[...] = dconv[:8]

    tile, halo, vec, cwspec, wblk = _rglru_specs(T, nt, True)
    acc8 = pl.BlockSpec((8, RNN_BLOCK), lambda n, t: (0, n))
    return pl.pallas_call(
        body, name="rglru_bwd", grid=(RNN_BLOCKS, nt),
        in_specs=[tile, halo, tile, tile, halo, tile, cwspec, vec, wblk, vec, wblk, vec, vec],
        out_specs=[tile, tile, wblk, wblk, acc8],
        out_shape=[jax.ShapeDtypeStruct((S, D_RNN), BF), jax.ShapeDtypeStruct((S, D_RNN), BF),
                   jax.ShapeDtypeStruct((RNN_BLOCKS, RNN_BLOCK, RNN_BLOCK), F32),
                   jax.ShapeDtypeStruct((RNN_BLOCKS, RNN_BLOCK, RNN_BLOCK), F32),
                   jax.ShapeDtypeStruct((8, D_RNN), F32)],
        scratch_shapes=[pltpu.VMEM((8, RNN_BLOCK), F32)] * 3,
        compiler_params=_params(("parallel", "arbitrary")),
    )(xr, xr, g, h, h, dy, cw, cb, wa, ba, wx, bx, lam)


def _rel_bucket_map():
    qi = np.arange(WINDOW)[:, None]
    kj = np.arange(2 * WINDOW)[None, :]
    dist = jnp.asarray(qi + WINDOW - kj, jnp.int32)
    n = jnp.maximum(dist, 0)
    max_exact = REL_BUCKETS // 2
    ratio = jnp.log(jnp.maximum(n, 1).astype(F32) / max_exact) / math.log(REL_MAX_DIST / max_exact)
    large = jnp.minimum(max_exact + (ratio * (REL_BUCKETS - max_exact)).astype(jnp.int32), REL_BUCKETS - 1)
    bucket = jnp.where(n < max_exact, n, large).astype(jnp.int32)
    j = np.arange(WINDOW)[None, :]
    return jnp.where(jnp.asarray(j > qi), bucket[:, :WINDOW], bucket[:, WINDOW:])


def _swa_common(n, kv_ref, bucket_ref, relb_ref, bias_scr):
    @pl.when(n == 0)
    def _():
        bk = bucket_ref[...]
        for h in range(SWA_HEADS):
            acc = jnp.zeros((WINDOW, WINDOW), F32)
            for b in range(REL_BUCKETS):
                acc = acc + jnp.where(bk == b, relb_ref[b, h], 0.0)
            bias_scr[h] = acc

    prev0 = pl.multiple_of(jnp.maximum(n - 1, 0) * WINDOW, WINDOW)
    cur0 = pl.multiple_of(n * WINDOW, WINDOW)
    kk = jnp.concatenate([kv_ref[pl.ds(prev0, WINDOW), :], kv_ref[pl.ds(cur0, WINDOW), :]], axis=0).astype(F32)
    rowi = lax.broadcasted_iota(jnp.int32, (WINDOW, WINDOW), 0)
    col = lax.broadcasted_iota(jnp.int32, (WINDOW, WINDOW), 1)
    from_prev = col > rowi
    return kk, from_prev, prev0, cur0


def _fold(full, from_prev):
    return jnp.where(from_prev, full[:, :WINDOW], full[:, WINDOW:])


def _unfold(sq, from_prev):
    return jnp.concatenate([jnp.where(from_prev, sq, 0.0), jnp.where(from_prev, 0.0, sq)], axis=1)


def _half_pair(part, kvh):
    lo = lax.broadcasted_iota(jnp.int32, part.shape, 1) < SWA_HD
    if kvh == 0:
        pa = jnp.where(lo, part, 0.0)
        pb = pltpu.roll(pa, SWA_HD, 1)
    else:
        pb = jnp.where(lo, 0.0, part)
        pa = pltpu.roll(pb, SWA_HD, 1)
    return pa.astype(BF), pb.astype(BF)


ALL_HEADS = SWA_HEADS * WINDOW


def _sink_column(sinks):
    return jnp.repeat(sinks.reshape(SWA_HEADS), WINDOW).reshape(ALL_HEADS, 1)


def _swa_operands(kk):
    return [(_half_pair(kk[:, :128], kvh), _half_pair(kk[:, 128:], kvh)) for kvh in range(SWA_KV_HEADS)]


def _swa_probs(n, q_ref, ops, bias_scr, sinkc_ref, from_prev):
    lgs = []
    for kvh in range(SWA_KV_HEADS):
        (ka, kb), _ = ops[kvh]
        for p in range(4):
            q2 = q_ref[:, kvh * 512 + p * 128:kvh * 512 + p * 128 + 128]
            lgs += [_fold(_dot(q2, ka, NT), from_prev), _fold(_dot(q2, kb, NT), from_prev)]
    lg = jnp.concatenate(lgs, axis=0) * (SWA_HD ** -0.5) + bias_scr[...].reshape(ALL_HEADS, WINDOW)
    rowi = jnp.bitwise_and(lax.broadcasted_iota(jnp.int32, (ALL_HEADS, WINDOW), 0), WINDOW - 1)
    col = lax.broadcasted_iota(jnp.int32, (ALL_HEADS, WINDOW), 1)
    no_prev = jnp.where(n > 0, 0, 4 * WINDOW)
    lg = jnp.where(jnp.logical_or(col <= rowi, col > rowi + no_prev), lg, NEG_INF)
    sink = sinkc_ref[...]
    m = jnp.maximum(jnp.max(lg, axis=-1, keepdims=True), sink)
    e = jnp.exp(lg - m)
    es = jnp.exp(sink - m)
    den = jnp.sum(e, axis=-1, keepdims=True) + es
    return e / den, es / den


def _swa_fwd(q, kv, g, bucket, rel_bias, sink_col):
    S = q.shape[0]
    nb = S // WINDOW

    def body(q_ref, kv_ref, g_ref, bucket_ref, relb_ref, sinkc_ref, o_ref, y_ref, bias_scr):
        n = pl.program_id(0)
        kk, from_prev, _, _ = _swa_common(n, kv_ref, bucket_ref, relb_ref, bias_scr)
        ops = _swa_operands(kk)
        pr, _ = _swa_probs(n, q_ref, ops, bias_scr, sinkc_ref, from_prev)
        for kvh in range(SWA_KV_HEADS):
            _, (va, vb) = ops[kvh]
            for p in range(4):
                c0 = kvh * 512 + p * 128
                r0 = (kvh * 8 + 2 * p) * WINDOW
                o2 = (_dot(_unfold(pr[r0:r0 + WINDOW], from_prev).astype(BF), va, NN)
                      + _dot(_unfold(pr[r0 + WINDOW:r0 + 2 * WINDOW], from_prev).astype(BF), vb, NN))
                o_ref[:, c0:c0 + 128] = o2
                gv = g_ref[:, c0:c0 + 128]
                y_ref[:, c0:c0 + 128] = (o2 * (gv * _sigmoid(gv))).astype(BF)

    blk = pl.BlockSpec((WINDOW, 1024), lambda n: (n, 0))
    smem = pl.BlockSpec(memory_space=pltpu.SMEM)
    sinkc = pl.BlockSpec((ALL_HEADS, 1), lambda n: (0, 0))
    return pl.pallas_call(
        body, name="swa_fwd", grid=(nb,),
        in_specs=[blk, pl.BlockSpec((S, 256), lambda n: (0, 0)), blk, pl.BlockSpec((WINDOW, WINDOW), lambda n: (0, 0)), smem, sinkc],
        out_specs=[blk, blk],
        out_shape=[jax.ShapeDtypeStruct((S, 1024), F32), jax.ShapeDtypeStruct((S, 1024), BF)],
        scratch_shapes=[pltpu.VMEM((SWA_HEADS, WINDOW, WINDOW), F32)],
        compiler_params=_params(("arbitrary",)),
    )(q, kv, g, bucket, rel_bias, sink_col)


def _swa_bwd(q, kv, g, o, dy, bucket, rel_bias, sink_col):
    S = q.shape[0]
    nb = S // WINDOW

    def body(q_ref, kv_ref, g_ref, o_ref, dy_ref, bucket_ref, relb_ref, sinkc_ref,
             dq_ref, dg_ref, dkv_ref, dsink_ref, drel_ref, bias_scr, dbias_scr, dsink_scr):
        n = pl.program_id(0)

        @pl.when(n == 0)
        def _():
            dbias_scr[...] = jnp.zeros_like(dbias_scr)
            dsink_scr[...] = jnp.zeros_like(dsink_scr)
            dkv_ref[...] = jnp.zeros_like(dkv_ref)

        kk, from_prev, prev0, cur0 = _swa_common(n, kv_ref, bucket_ref, relb_ref, bias_scr)
        ops = _swa_operands(kk)
        pr, ps = _swa_probs(n, q_ref, ops, bias_scr, sinkc_ref, from_prev)
        do2s, dps = [], []
        for kvh in range(SWA_KV_HEADS):
            _, (va, vb) = ops[kvh]
            for p in range(4):
                c0 = kvh * 512 + p * 128
                gv = g_ref[:, c0:c0 + 128]
                sg = _sigmoid(gv)
                dyv = dy_ref[:, c0:c0 + 128]
                dg_ref[:, c0:c0 + 128] = (dyv * o_ref[:, c0:c0 + 128] * (sg * (1.0 + gv * (1.0 - sg)))).astype(BF)
                do2 = (dyv * (gv * sg)).astype(BF)
                do2s.append(do2)
                dps += [_fold(_dot(do2, va, NT), from_prev), _fold(_dot(do2, vb, NT), from_prev)]
        dp = jnp.concatenate(dps, axis=0)
        delta = jnp.sum(pr * dp, axis=-1, keepdims=True)
        ds = pr * (dp - delta)
        dbias_scr[...] += ds.reshape(SWA_HEADS, WINDOW, WINDOW)
        dsink_scr[...] += ps * delta
        dsc = ds * (SWA_HD ** -0.5)
        lo256 = lax.broadcasted_iota(jnp.int32, (2 * WINDOW, 128), 1) < SWA_HD
        dks, dvs = [], []
        for kvh in range(SWA_KV_HEADS):
            (ka, kb), _ = ops[kvh]
            dka = jnp.zeros((2 * WINDOW, 128), F32)
            dkb, dva, dvb = dka, dka, dka
            for p in range(4):
                c0 = kvh * 512 + p * 128
                r0 = (kvh * 8 + 2 * p) * WINDOW
                q2 = q_ref[:, c0:c0 + 128]
                do2 = do2s[kvh * 4 + p]
                ds0 = _unfold(dsc[r0:r0 + WINDOW], from_prev).astype(BF)
                ds1 = _unfold(dsc[r0 + WINDOW:r0 + 2 * WINDOW], from_prev).astype(BF)
                dq_ref[:, c0:c0 + 128] = (_dot(ds0, ka, NN) + _dot(ds1, kb, NN)).astype(BF)
                dka = dka + _dot(ds0, q2, TN)
                dkb = dkb + _dot(ds1, q2, TN)
                dva = dva + _dot(_unfold(pr[r0:r0 + WINDOW], from_prev).astype(BF), do2, TN)
                dvb = dvb + _dot(_unfold(pr[r0 + WINDOW:r0 + 2 * WINDOW], from_prev).astype(BF), do2, TN)
            dks.append(jnp.where(lo256, dka, 0.0) + pltpu.roll(jnp.where(lo256, 0.0, dkb), SWA_HD, 1))
            dvs.append(jnp.where(lo256, dva, 0.0) + pltpu.roll(jnp.where(lo256, 0.0, dvb), SWA_HD, 1))
        dk = dks[0] + pltpu.roll(dks[1], SWA_HD, 1)
        dv = dvs[0] + pltpu.roll(dvs[1], SWA_HD, 1)
        dkv_ref[pl.ds(prev0, WINDOW), 0:128] += dk[:WINDOW]
        dkv_ref[pl.ds(prev0, WINDOW), 128:256] += dv[:WINDOW]
        dkv_ref[pl.ds(cur0, WINDOW), 0:128] += dk[WINDOW:]
        dkv_ref[pl.ds(cur0, WINDOW), 128:256] += dv[WINDOW:]

        @pl.when(n == nb - 1)
        def _():
            dsink_ref[...] = -jnp.sum(dsink_scr[...].reshape(SWA_HEADS, WINDOW, 1), axis=1)
            bk = bucket_ref[...]
            sums = []
            for b in range(REL_BUCKETS):
                sums.append(jnp.sum(jnp.where((bk == b)[None], dbias_scr[...], 0.0), axis=1))
            drel_ref[...] = jnp.sum(jnp.concatenate(sums, axis=0), axis=1, keepdims=True)

    blk = pl.BlockSpec((WINDOW, 1024), lambda n: (n, 0))
    smem = pl.BlockSpec(memory_space=pltpu.SMEM)
    whole = lambda shape: pl.BlockSpec(shape, lambda n: (0, 0))
    return pl.pallas_call(
        body, name="swa_bwd", grid=(nb,),
        in_specs=[blk, whole((S, 256)), blk, blk, blk, whole((WINDOW, WINDOW)), smem, whole((ALL_HEADS, 1))],
        out_specs=[blk, blk, whole((S, 256)), whole((SWA_HEADS, 1)), whole((REL_BUCKETS * SWA_HEADS, 1))],
        out_shape=[jax.ShapeDtypeStruct((S, 1024), BF), jax.ShapeDtypeStruct((S, 1024), BF),
                   jax.ShapeDtypeStruct((S, 256), F32), jax.ShapeDtypeStruct((SWA_HEADS, 1), F32),
                   jax.ShapeDtypeStruct((REL_BUCKETS * SWA_HEADS, 1), F32)],
        scratch_shapes=[pltpu.VMEM((SWA_HEADS, WINDOW, WINDOW), F32), pltpu.VMEM((SWA_HEADS, WINDOW, WINDOW), F32),
                        pltpu.VMEM((ALL_HEADS, 1), F32)],
        compiler_params=_params(("arbitrary",)),
    )(q, kv, g, o, dy, bucket, rel_bias, sink_col)


def _mem_probs(qh, mk):
    lg = _dot(qh, mk, NT) * (MEM_HD ** -0.5)
    e = jnp.exp(lg - jnp.max(lg, axis=-1, keepdims=True))
    return e / jnp.sum(e, axis=-1, keepdims=True)


def _mem_fwd(q, mkv, g):
    S = q.shape[0]
    M = mkv.shape[0]
    tq = 256

    def body(q_ref, mkv_ref, g_ref, o_ref, y_ref):
        for h in range(MEM_HEADS):
            c0 = h * MEM_HD
            pr = _mem_probs(q_ref[:, c0:c0 + MEM_HD], mkv_ref[:, c0:c0 + MEM_HD])
            o = _dot(pr.astype(BF), mkv_ref[:, D_MEM + c0:D_MEM + c0 + MEM_HD], NN)
            o_ref[:, c0:c0 + MEM_HD] = o
            gv = g_ref[:, c0:c0 + MEM_HD]
            y_ref[:, c0:c0 + MEM_HD] = (o * (gv * _sigmoid(gv))).astype(BF)

    blk = pl.BlockSpec((tq, D_MEM), lambda i: (i, 0))
    return pl.pallas_call(
        body, name="mem_fwd", grid=(S // tq,),
        in_specs=[blk, pl.BlockSpec((M, 2 * D_MEM), lambda i: (0, 0)), blk], out_specs=[blk, blk],
        out_shape=[jax.ShapeDtypeStruct((S, D_MEM), F32), jax.ShapeDtypeStruct((S, D_MEM), BF)],
        compiler_params=_params(("parallel",)),
    )(q, mkv, g)


def _mem_bwd(q, mkv, g, o, dy):
    S = q.shape[0]
    M = mkv.shape[0]
    tq = 256

    def body(q_ref, mkv_ref, g_ref, o_ref, dy_ref, dq_ref, dg_ref, dmkv_ref):
        @pl.when(pl.program_id(0) == 0)
        def _():
            dmkv_ref[...] = jnp.zeros_like(dmkv_ref)

        for h in range(MEM_HEADS):
            c0 = h * MEM_HD
            qh = q_ref[:, c0:c0 + MEM_HD]
            mk = mkv_ref[:, c0:c0 + MEM_HD]
            mv = mkv_ref[:, D_MEM + c0:D_MEM + c0 + MEM_HD]
            gv = g_ref[:, c0:c0 + MEM_HD]
            sg = _sigmoid(gv)
            dyv = dy_ref[:, c0:c0 + MEM_HD]
            dg_ref[:, c0:c0 + MEM_HD] = (dyv * o_ref[:, c0:c0 + MEM_HD] * (sg * (1.0 + gv * (1.0 - sg)))).astype(BF)
            do = (dyv * (gv * sg)).astype(BF)
            pr = _mem_probs(qh, mk)
            dp = _dot(do, mv, NT)
            ds = pr * (dp - jnp.sum(pr * dp, axis=-1, keepdims=True))
            dsb = (ds * (MEM_HD ** -0.5)).astype(BF)
            dq_ref[:, c0:c0 + MEM_HD] = _dot(dsb, mk, NN).astype(BF)
            dmkv_ref[:, c0:c0 + MEM_HD] += _dot(dsb, qh, TN)
            dmkv_ref[:, D_MEM + c0:D_MEM + c0 + MEM_HD] += _dot(pr.astype(BF), do, TN)

    blk = pl.BlockSpec((tq, D_MEM), lambda i: (i, 0))
    whole = pl.BlockSpec((M, 2 * D_MEM), lambda i: (0, 0))
    return pl.pallas_call(
        body, name="mem_bwd", grid=(S // tq,),
        in_specs=[blk, whole, blk, blk, blk], out_specs=[blk, blk, whole],
        out_shape=[jax.ShapeDtypeStruct((S, D_MEM), BF), jax.ShapeDtypeStruct((S, D_MEM), BF),
                   jax.ShapeDtypeStruct((M, 2 * D_MEM), F32)],
        compiler_params=_params(("arbitrary",)),
    )(q, mkv, g, o, dy)


MERGE_TN = 512


def _merge_specs(tm):
    ytile = pl.BlockSpec((tm, 1024), lambda i, j: (i, 0))
    wblk = pl.BlockSpec((MERGE_TN, 1024), lambda i, j: (j, 0))
    gls = [pl.BlockSpec((None, tm, MERGE_TN), (lambda i, j, br=br: (br, i, j))) for br in range(3)]
    otile = pl.BlockSpec((tm, MERGE_TN), lambda i, j: (i, j))
    return ytile, wblk, gls, otile


def _merge_fwd(ys, ws, gl, tm):
    S = gl.shape[1]

    def body(y0, y1, y2, w0, w1, w2, g0, g1, g2, o_ref):
        acc = None
        for y_ref, w_ref, g_ref in ((y0, w0, g0), (y1, w1, g1), (y2, w2, g2)):
            term = _sigmoid(g_ref[...]) * _dot(y_ref[...], w_ref[...], NT)
            acc = term if acc is None else acc + term
        o_ref[...] = acc.astype(BF)

    ytile, wblk, gls, otile = _merge_specs(tm)
    return pl.pallas_call(
        body, name="merge_fwd", grid=(S // tm, D_MODEL // MERGE_TN),
        in_specs=[ytile] * 3 + [wblk] * 3 + gls, out_specs=otile,
        out_shape=jax.ShapeDtypeStruct((S, D_MODEL), BF),
        compiler_params=_params(("parallel", "arbitrary")),
    )(*ys, *ws, gl, gl, gl)


def _merge_bwd(dout, w_out, ys, ws, gl, tm):
    S = gl.shape[1]

    def body(do_ref, wo_ref, y0, y1, y2, w0, w1, w2, g0, g1, g2, dg0, dg1, dg2, dp0, dp1, dp2):
        dm = _dot(do_ref[...], wo_ref[...], NT)
        for y_ref, w_ref, g_ref, dg_ref, dp_ref in ((y0, w0, g0, dg0, dp0), (y1, w1, g1, dg1, dp1), (y2, w2, g2, dg2, dp2)):
            gate = _sigmoid(g_ref[...])
            pv = _dot(y_ref[...], w_ref[...], NT)
            dg_ref[...] = (dm * pv * gate * (1.0 - gate)).astype(BF)
            dp_ref[...] = (dm * gate).astype(BF)

    ytile, wblk, gls, otile = _merge_specs(tm)
    out = jax.ShapeDtypeStruct((S, D_MODEL), BF)
    return pl.pallas_call(
        body, name="merge_bwd", grid=(S // tm, D_MODEL // MERGE_TN),
        in_specs=[pl.BlockSpec((tm, D_MODEL), lambda i, j: (i, 0)), pl.BlockSpec((MERGE_TN, D_MODEL), lambda i, j: (j, 0))]
        + [ytile] * 3 + [wblk] * 3 + gls,
        out_specs=[otile] * 6, out_shape=[out] * 6,
        compiler_params=_params(("parallel", "arbitrary")),
    )(dout, w_out, *ys, *ws, gl, gl, gl)


def _out_loss(merged, w_out, x, target, post_g, tm):
    S = x.shape[0]

    def body(m_ref, w_ref, x_ref, t_ref, g_ref, dout_ref, dy_ref, loss_ref, dpost_ref):
        @pl.when(pl.program_id(0) == 0)
        def _():
            loss_ref[...] = jnp.zeros_like(loss_ref)
            dpost_ref[...] = jnp.zeros_like(dpost_ref)

        out = _dot(m_ref[...], w_ref[...], NN)
        r = lax.rsqrt(jnp.mean(out * out, axis=-1, keepdims=True) + EPS)
        nrm = out * r
        gv = g_ref[...]
        err = (x_ref[...] + nrm * gv) - t_ref[...]
        sq = jnp.sum(jnp.sum(err * err, axis=1, keepdims=True), axis=0, keepdims=True)
        loss_ref[...] += sq * (0.5 / D_MODEL)
        dy = err * (1.0 / D_MODEL)
        dy_ref[...] = dy
        dpost_ref[...] += jnp.sum(dy * nrm, axis=0, keepdims=True)
        dn = dy * gv
        dout_ref[...] = (r * (dn - nrm * jnp.mean(dn * nrm, axis=-1, keepdims=True))).astype(BF)

    row = pl.BlockSpec((tm, D_MODEL), lambda i: (i, 0))
    return pl.pallas_call(
        body, name="out_loss", grid=(S // tm,),
        in_specs=[row, pl.BlockSpec((D_MODEL, D_MODEL), lambda i: (0, 0)), row, row, pl.BlockSpec((1, D_MODEL), lambda i: (0, 0))],
        out_specs=[row, row, pl.BlockSpec((8, 128), lambda i: (0, 0)), pl.BlockSpec((1, D_MODEL), lambda i: (0, 0))],
        out_shape=[jax.ShapeDtypeStruct((S, D_MODEL), BF), jax.ShapeDtypeStruct((S, D_MODEL), F32),
                   jax.ShapeDtypeStruct((8, 128), F32), jax.ShapeDtypeStruct((1, D_MODEL), F32)],
        compiler_params=_params(("arbitrary",)),
    )(merged, w_out, x, target, post_g)


def _dh_dx(dproj, w_in, x, dy, pre_g, tm):
    S = x.shape[0]
    nk, tk = dproj.shape[0], dproj.shape[2]

    def body(dp_ref, w_ref, x_ref, dy_ref, g_ref, dx_ref, dpre_ref, acc_ref):
        i, k = pl.program_id(0), pl.program_id(1)

        @pl.when(jnp.logical_and(i == 0, k == 0))
        def _():
            dpre_ref[...] = jnp.zeros_like(dpre_ref)

        @pl.when(k == 0)
        def _():
            acc_ref[...] = jnp.zeros_like(acc_ref)

        acc_ref[...] += _dot(dp_ref[...], w_ref[...], NN)

        @pl.when(k == nk - 1)
        def _():
            dh = acc_ref[...]
            xv = x_ref[...]
            r = lax.rsqrt(jnp.mean(xv * xv, axis=-1, keepdims=True) + EPS)
            nrm = xv * r
            dpre_ref[...] += jnp.sum(dh * nrm, axis=0, keepdims=True)
            dn = dh * g_ref[...]
            dx_ref[...] = r * (dn - nrm * jnp.mean(dn * nrm, axis=-1, keepdims=True)) + dy_ref[...]

    row = pl.BlockSpec((tm, D_MODEL), lambda i, k: (i, 0))
    vec = pl.BlockSpec((1, D_MODEL), lambda i, k: (0, 0))
    return pl.pallas_call(
        body, name="dh_dx", grid=(S // tm, nk),
        in_specs=[pl.BlockSpec((None, tm, tk), lambda i, k: (k, i, 0)), pl.BlockSpec((tk, D_MODEL), lambda i, k: (k, 0)), row, row, vec],
        out_specs=[row, vec],
        out_shape=[jax.ShapeDtypeStruct((S, D_MODEL), F32), jax.ShapeDtypeStruct((1, D_MODEL), F32)],
        scratch_shapes=[pltpu.VMEM((tm, D_MODEL), F32)],
        compiler_params=_params(("arbitrary", "arbitrary"), 56),
    )(dproj, w_in, x, dy, pre_g)


def _sum_parts(parts, name):
    P, R, C = parts.shape
    tr = max(t for t in range(8, 513, 8) if R % t == 0)

    def body(p_ref, o_ref):
        acc = p_ref[0]
        for j in range(1, P):
            acc = acc + p_ref[j]
        o_ref[...] = acc

    return pl.pallas_call(
        body, name=name, grid=(R // tr,),
        in_specs=[pl.BlockSpec((P, tr, C), lambda i: (0, i, 0))], out_specs=pl.BlockSpec((tr, C), lambda i: (i, 0)),
        out_shape=jax.ShapeDtypeStruct((R, C), F32), compiler_params=_params(("parallel",)),
    )(parts)


def _adamw(parts, w, m, v, name, own=None, chip=None):
    groups = list(parts) if isinstance(parts, (list, tuple)) else [parts]
    owns = [] if own is None else (list(own) if isinstance(own, (list, tuple)) else [own])
    P, rows, C = groups[0].shape
    R = rows * len(groups)
    tr = 128 if rows % 128 == 0 else rows
    per = rows // tr
    c1 = 1.0 - ADAM_B1 ** ADAM_STEP
    c2 = 1.0 - ADAM_B2 ** ADAM_STEP
    n_in = len(groups) * (4 if owns else 1)

    def body(*refs):
        if owns:
            refs = refs[1:]
        p_refs = refs[:n_in]
        w_ref, m_ref, v_ref, g_ref, d_ref, nm_ref, nv_ref = refs[n_in:]
        g = None
        for q in range(len(groups)):
            if owns:
                gq = p_refs[4 * q + 3][...].astype(F32)
                for j in range(3):
                    gq = gq + p_refs[4 * q + j][...].astype(F32)
            else:
                gq = p_refs[q][0].astype(F32)
                for j in range(1, P):
                    gq = gq + p_refs[q][j].astype(F32)
            g = gq if g is None else jnp.where(pl.program_id(0) // per == q, gq, g)
        nm = ADAM_B1 * m_ref[...] + (1.0 - ADAM_B1) * g
        nv = ADAM_B2 * v_ref[...] + (1.0 - ADAM_B2) * (g * g)
        g_ref[...] = g
        nm_ref[...] = nm
        nv_ref[...] = nv
        d_ref[...] = -ADAM_LR * ((nm / c1) / (jnp.sqrt(nv / c2) + ADAM_EPS) + ADAM_WD * w_ref[...])

    out = jax.ShapeDtypeStruct((R, C), F32)
    if not owns:
        tile = pl.BlockSpec((tr, C), lambda i: (i, 0))
        return pl.pallas_call(
            body, name=name, grid=(R // tr,),
            in_specs=[pl.BlockSpec((P, tr, C), (lambda i, q=q: (0, jnp.clip(i - q * per, 0, per - 1), 0))) for q in range(len(groups))]
            + [tile, tile, tile], out_specs=[tile] * 4, out_shape=[out] * 4,
            compiler_params=_params(("parallel",)),
        )(*groups, w, m, v)
    tile = pl.BlockSpec((tr, C), lambda i, c_ref: (i, 0))
    specs, operands = [], []
    for q in range(len(groups)):
        for k in range(3):
            specs.append(pl.BlockSpec((None, tr, C), (lambda i, c_ref, q=q, k=k: (k + (c_ref[0] <= k).astype(jnp.int32),
                                                                                  jnp.clip(i - q * per, 0, per - 1), 0))))
            operands.append(groups[q])
        specs.append(pl.BlockSpec((None, tr, C), (lambda i, c_ref, q=q: (c_ref[0], jnp.clip(i - q * per, 0, per - 1), 0))))
        operands.append(owns[q])
    return pl.pallas_call(
        body, name=name,
        grid_spec=pltpu.PrefetchScalarGridSpec(num_scalar_prefetch=1, grid=(R // tr,), in_specs=specs + [tile, tile, tile],
                                               out_specs=[tile] * 4),
        out_shape=[out] * 4, compiler_params=_params(("parallel",)),
    )(chip, *operands, w, m, v)


def _project(h, w_t):
    S = h.shape[0]
    n_tiles = D_IN // SEG_TILE
    ranges = [(c0 // SEG_TILE, (c0 + width) // SEG_TILE) for _, c0, width, _ in SEGMENTS]

    def body(h_ref, w_ref, *outs):
        j = pl.program_id(0)
        prod = _dot(h_ref[...], w_ref[...], NT)
        for (j0, j1), (_, _, _, dt), o_ref in zip(ranges, SEGMENTS, outs):
            @pl.when(jnp.logical_and(j >= j0, j < j1))
            def _(o_ref=o_ref, dt=dt):
                o_ref[...] = prod.astype(dt)

    out_shapes, out_specs = [], []
    for (j0, j1), (name, _, width, dt) in zip(ranges, SEGMENTS):
        if name == "gl":
            per = (j1 - j0) // 3
            out_shapes.append(jax.ShapeDtypeStruct((3, S, width // 3), dt))
            out_specs.append(pl.BlockSpec((None, S, SEG_TILE), (lambda j, j0=j0, j1=j1, per=per: (
                jnp.clip(j - j0, 0, j1 - j0 - 1) // per, 0, jnp.clip(j - j0, 0, j1 - j0 - 1) % per))))
        else:
            out_shapes.append(jax.ShapeDtypeStruct((S, width), dt))
            out_specs.append(pl.BlockSpec((S, SEG_TILE), (lambda j, j0=j0, j1=j1: (0, jnp.clip(j - j0, 0, j1 - j0 - 1)))))
    outs = pl.pallas_call(
        body, name="proj", grid=(n_tiles,),
        in_specs=[pl.BlockSpec((S, D_MODEL), lambda j: (0, 0)), pl.BlockSpec((SEG_TILE, D_MODEL), lambda j: (j, 0))],
        out_specs=out_specs, out_shape=out_shapes,
        compiler_params=_params(("arbitrary",), 56),
    )(h, w_t)
    return {name: o for (name, _, _, _), o in zip(SEGMENTS, outs)}


def _forward_a(x, mem, pre_g, mem_g, w_in, conv_w, conv_b, w_a, b_a, w_x, b_x, lam, sinks, rel_bias):
    S = x.shape[0]
    st = dict(T=min(512, S // 2), tm=min(512, S), bucket=_rel_bucket_map())
    st["h"] = _rms_fwd(x, pre_g, "pre_norm")
    st["memn"] = _rms_fwd(mem, mem_g, "mem_norm")
    seg = st["seg"] = _project(st["h"], w_in)
    st["h_rg"], st["y_rg"] = _rglru_fwd(seg["xr"], seg["g_rg"], conv_w, conv_b, w_a, b_a, w_x, b_x, lam, st["T"])
    st["o_swa"], st["y_swa"] = _swa_fwd(seg["q_s"], seg["kv"], seg["g_swa"], st["bucket"], rel_bias, _sink_column(sinks))
    return st


def _forward_b(st, x, target, post_g, w_memkv, wbr, w_out):
    S = x.shape[0]
    M = st["memn"].shape[0]
    seg = st["seg"]
    st["mkv"] = _matmul(st["memn"], w_memkv, "nn", M, 2 * D_MEM, D_MODEL, M, 512, D_MODEL, BF, "mem_kv")
    st["o_mem"], st["y_mem"] = _mem_fwd(seg["q_m"], st["mkv"], seg["g_mem"])
    st["ys"] = (st["y_rg"], st["y_swa"], st["y_mem"])
    st["merged"] = _merge_fwd(st["ys"], wbr, seg["gl"], st["tm"])
    st["dout"], st["dy"], st["loss"], st["dpost"] = _out_loss(st["merged"], w_out, x, target, post_g, min(256, S))
    return st


def _backward_a1(st, wbr, w_out):
    S = st["h"].shape[0]
    seg, ys, tm = st["seg"], st["ys"], st["tm"]
    st["dw_out"] = _matmul(st["merged"], st["dout"], "tn", D_MODEL, D_MODEL, S, 256, D_MODEL, S, BF, "dw_out", out_blocked="row")
    dgl0, dgl1, dgl2, dp0, dp1, dp2 = _merge_bwd(st["dout"], w_out, ys, wbr, seg["gl"], tm)
    st["dgl"] = (dgl0, dgl1, dgl2)
    dys, dwbr = [], []
    for i, dp in enumerate((dp0, dp1, dp2)):
        dys.append(_matmul(dp, wbr[i], "nn", S, 1024, D_MODEL, tm, 1024, D_MODEL, F32, "dy_br%d" % i))
        dwbr.append(_matmul(ys[i], dp, "tn", 1024, D_MODEL, S, 1024, 256, S, BF, "dw_br%d" % i, out_blocked="col"))
    st["dys"], st["dwbr"] = dys, dwbr
    return st


def _backward_a2(st, mem, w_memkv, conv_w, conv_b, w_a, b_a, w_x, b_x, lam):
    M = mem.shape[0]
    seg, dys = st["seg"], st["dys"]
    st["dq_m"], st["dg_mem"], dmkv = _mem_bwd(seg["q_m"], st["mkv"], seg["g_mem"], st["o_mem"], dys[2])
    dmkv_b = dmkv.astype(BF)
    st["dw_memkv"] = _matmul(st["memn"], dmkv_b, "tn", D_MODEL, 2 * D_MEM, M, 256, 2 * D_MEM, M, BF, "dw_memkv", out_blocked="row")
    dmemn = _matmul(dmkv_b, w_memkv, "nt", M, D_MODEL, 2 * D_MEM, M, 512, 2 * D_MEM, F32, "dmemn")
    st["dmem_g"] = _rms_gain_grad(dmemn, mem, "dmem_gain")
    st["dxr"], st["dg_rg"], st["dw_a"], st["dw_x"], st["dvec"] = _rglru_bwd(
        seg["xr"], seg["g_rg"], st["h_rg"], dys[0], conv_w, conv_b, w_a, b_a, w_x, b_x, lam, st["T"])
    return st


def _backward_b(st, rel_bias, sinks):
    seg = st["seg"]
    dq_s, dg_swa, dkv, dsinks, drel = _swa_bwd(seg["q_s"], seg["kv"], seg["g_swa"], st["o_swa"], st["dys"][1],
                                               st["bucket"], rel_bias, _sink_column(sinks))
    st["dsinks"], st["drel"] = dsinks.reshape(1, SWA_HEADS), drel.reshape(REL_BUCKETS, SWA_HEADS)
    dproj = jnp.concatenate([st["dxr"], st["dg_rg"], dq_s, dkv.astype(BF), dg_swa, st["dq_m"], st["dg_mem"], *st["dgl"]], axis=1)
    st["dproj"] = jnp.transpose(dproj.reshape(dproj.shape[0], N_DEV, D_IN // N_DEV), (1, 0, 2))
    return st


def _dw_in_half(st, half, dep=None):
    S = st["h"].shape[0]
    return _matmul(st["h"], st["dproj"], "tn", D_MODEL // 2, D_IN, S, 512, D_IN // N_DEV, S, BF, "dw_in%d" % half, a_moff=2 * half,
                   b_blocked=True, out_blocked="col", dep=dep)


def _owner_blocks(a):
    return jnp.swapaxes(a.reshape((4, 2) + a.shape[1:]), 0, 1)


def _local_step(x, mem, target, pre_g, post_g, mem_g, w_in, conv_w, conv_b, w_a, b_a, w_x, b_x, lam, sinks, rel_bias,
                w_memkv, wbr, w_out):
    st = _forward_a(x, mem, pre_g, mem_g, w_in, conv_w, conv_b, w_a, b_a, w_x, b_x, lam, sinks, rel_bias)
    st = _forward_b(st, x, target, post_g, w_memkv, wbr, w_out)
    st = _backward_a1(st, wbr, w_out)
    st = _backward_a2(st, mem, w_memkv, conv_w, conv_b, w_a, b_a, w_x, b_x, lam)
    st = _backward_b(st, rel_bias, sinks)
    st["dw_in"] = [_dw_in_half(st, 0), _dw_in_half(st, 1)]
    st["grad_x"], st["dpre"] = _dh_dx(st["dproj"], w_in, x, st["dy"], pre_g, st["tm"])
    return st


def _pad_rows(a, rows):
    a = a.reshape(-1, 128) if a.shape[-1] % 128 == 0 else jnp.pad(a, ((0, 0), (0, 128 - a.shape[-1])))
    return jnp.pad(a, ((0, rows - a.shape[0]), (0, 0))) if a.shape[0] < rows else a


def kernel(x, mem, pre_norm_g, post_norm_g, mem_norm_g, w_in, conv_w, conv_b, w_rg_a, b_rg_a, w_rg_x, b_rg_x, lru_lambda, swa_sinks, rel_bias, w_mem_kv, w_br_rg, w_br_swa, w_br_mem, w_out, loss_target, m_pre_norm_g, m_post_norm_g, m_mem_norm_g, m_w_in, m_conv_w, m_conv_b, m_w_rg_a, m_b_rg_a, m_w_rg_x, m_b_rg_x, m_lru_lambda, m_swa_sinks, m_rel_bias, m_w_mem_kv, m_w_br_rg, m_w_br_swa, m_w_br_mem, m_w_out, v_pre_norm_g, v_post_norm_g, v_mem_norm_g, v_w_in, v_conv_w, v_conv_b, v_w_rg_a, v_b_rg_a, v_w_rg_x, v_b_rg_x, v_lru_lambda, v_swa_sinks, v_rel_bias, v_w_mem_kv, v_w_br_rg, v_w_br_swa, v_w_br_mem, v_w_out):
    cx, cy, cc = lax.axis_index("x"), lax.axis_index("y"), lax.axis_index("c")
    me = 4 * cx + 2 * cy + cc
    chip = 2 * cx + cy
    core = jnp.reshape(cc, (1,)).astype(jnp.int32)
    x0, mem0 = x[0], mem[0]
    w_a_b, w_x_b = w_rg_a[0].astype(BF), w_rg_x[0].astype(BF)

    def landing(own, slot, slots):
        return lax.dynamic_update_slice(lax.empty((slots,) + own.shape, own.dtype), own[None], (slot,) + (0,) * own.ndim)

    def gather_zone(own, kind):
        if kind == "cols":
            return lax.dynamic_update_slice(lax.empty((own.shape[0], N_DEV * own.shape[1]), own.dtype), own, (0, me * own.shape[1]))
        return lax.empty((N_DEV,) + own.shape, own.dtype)

    def swap_start(parts, tag):
        return _exchange_start(parts, [lax.empty(p.shape[1:], p.dtype) for p in parts], _plan_swap(len(parts)), "swap_%s_start" % tag)

    def scatter_start(swap, after, tag, prefill=()):
        s_send, s_recv, parts, got, _ = swap
        got = _exchange_wait(s_send, s_recv, parts, got, _plan_swap(len(parts)), after, "swap_%s_wait" % tag)
        sums = [_pair_sum(p, g, core, "scatter_%s_sum%d" % (tag, i)) for i, (p, g) in enumerate(zip(parts, got))]
        lands = [landing(lax.dynamic_index_in_dim(s, chip, 0, keepdims=False), chip, 4) if i in prefill
                 else lax.empty(s.shape, s.dtype) for i, s in enumerate(sums)]
        return _exchange_start(sums, lands, _plan_scatter(len(sums)), "scatter_%s_start" % tag)

    def zero_after(a):
        return jnp.minimum(jnp.abs(a.reshape(-1)[0].astype(F32)), 0.0)

    g_in, g_cw = _all_gather_relayed([jnp.transpose(w_in[0]).astype(BF), conv_w[0]], [True, False], "gather_w_in")
    w_in_f = g_in.reshape(D_IN, D_MODEL)
    conv_w_f = jnp.transpose(g_cw, (1, 0, 2)).reshape(CONV_W, D_RNN)

    after_first = zero_after(g_cw).astype(BF)
    rest = [w.astype(BF) + after_first for w in (w_mem_kv[0], jnp.transpose(w_br_rg[0]), jnp.transpose(w_br_swa[0]),
                                                 jnp.transpose(w_br_mem[0]), w_out[0])]
    kinds = ["lead"] * len(rest)
    plan_g = _plan_gather(kinds)
    g_send, g_recv, g_src, g_land, g_token = _exchange_start(rest, [gather_zone(w, kd) for w, kd in zip(rest, kinds)], plan_g,
                                                             "gather_rest_start")
    st = _forward_a(x0, mem0, pre_norm_g + g_token[0:1, 0:1], mem_norm_g, w_in_f, conv_w_f, conv_b, w_a_b, b_rg_a, w_x_b, b_rg_x,
                    lru_lambda, swa_sinks, rel_bias)
    g_land = _exchange_wait(g_send, g_recv, g_src, g_land, plan_g, st["y_swa"], "gather_rest_wait")
    g_land = _forward_to_sibling(g_land, kinds, rest, "gather_rest_forward")
    w_memkv_f = g_land[0].reshape(D_MODEL, 2 * D_MEM)
    wbr = tuple(g_land[i].reshape(D_MODEL, D_RNN) for i in (1, 2, 3))
    w_out_f = g_land[4].reshape(D_MODEL, D_MODEL)

    st = _forward_b(st, x0, loss_target[0], post_norm_g, w_memkv_f, wbr, w_out_f)
    st = _backward_a1(st, wbr, w_out_f)
    parts_a = [st["dw_out"], st["dwbr"][0], st["dwbr"][1], st["dwbr"][2]]
    plan_a = _plan_scatter(len(parts_a))
    swap_a = swap_start(parts_a, "a")
    st = _backward_a2(st, mem0, w_memkv_f, conv_w_f, conv_b + swap_a[4][0:1, 0:1], w_a_b, b_rg_a, w_x_b, b_rg_x, lru_lambda)
    a_send, a_recv, a_src, a_land, a_token = scatter_start(swap_a, st["dxr"], "a")
    parts_c = [st["dw_memkv"], _owner_blocks(st["dw_a"]), _owner_blocks(st["dw_x"])]
    plan_c = _plan_scatter(len(parts_c))
    swap_c = swap_start(parts_c, "c")

    st = _backward_b(st, rel_bias, swa_sinks + swap_c[4][0:1, 0:1] + a_token[0:1, 0:1])
    c_send, c_recv, c_src, c_land, c_token = scatter_start(swap_c, st["dproj"], "c", prefill=(1, 2))
    plan_b = _plan_scatter(1)

    def dw_in_parts(half, dep):
        dwh = _dw_in_half(st, half, dep)
        return dwh, [dwh]

    dw0, parts_b0 = dw_in_parts(0, c_token)
    swap_b0 = swap_start(parts_b0, "b0")
    a_land = _exchange_wait(a_send, a_recv, a_src, a_land, plan_a, swap_b0[4], "scatter_a_wait")
    big = [None] * 6

    chip1 = jnp.reshape(chip, (1,)).astype(jnp.int32)

    def adamw_big(j, land, own, wt, mt, vt):
        big[j] = [a[None] for a in _adamw(land, wt[0], mt[0], vt[0], "adamw_big%d" % j, own=own, chip=chip1)]

    adamw_big(5, a_land[0], a_src[0], w_out, m_w_out, v_w_out)
    adamw_big(2, a_land[1], a_src[1], w_br_rg, m_w_br_rg, v_w_br_rg)
    adamw_big(3, a_land[2], a_src[2], w_br_swa, m_w_br_swa, v_w_br_swa)
    halves = [scatter_start(swap_b0, big[3][1], "b0")]
    dw1, parts_b1 = dw_in_parts(1, halves[0][4])
    swap_b1 = swap_start(parts_b1, "b1")
    c_land = _exchange_wait(c_send, c_recv, c_src, c_land, plan_c, swap_b1[4], "scatter_c_wait")
    g_wa_blk = _sum_parts(c_land[1], "sum_w_rg_a")
    g_wx_blk = _sum_parts(c_land[2], "sum_w_rg_x")
    adamw_big(4, a_land[3], a_src[3], w_br_mem, m_w_br_mem, v_w_br_mem)
    adamw_big(1, c_land[0], c_src[0], w_mem_kv, m_w_mem_kv, v_w_mem_kv)
    halves.append(scatter_start(swap_b1, big[1][1], "b1"))
    grad_x, dpre = _dh_dx(st["dproj"], w_in_f, x0, st["dy"], pre_norm_g + halves[1][4][0:1, 0:1], st["tm"])
    after, b_lands, b_sums = grad_x, [], []
    for half, (b_send, b_recv, b_src, b_land, _) in enumerate(halves):
        b_lands.append(_exchange_wait(b_send, b_recv, b_src, b_land, plan_b, after, "scatter_b%d_wait" % half)[0])
        b_sums.append(b_src[0])
        after = b_lands[-1]
    big[0] = [a[None] for a in _adamw(b_lands, w_in[0], m_w_in[0], v_w_in[0], "adamw_big0", own=b_sums, chip=chip1)]
    links_free = jnp.minimum(jnp.abs(big[0][0][0, 0, 0]), 0.0)

    pack = jnp.concatenate([dpre.reshape(16, 128), st["dpost"].reshape(16, 128), st["dmem_g"].reshape(16, 128),
                            st["dvec"].reshape(64, 128), _pad_rows(st["dsinks"], 8), _pad_rows(st["drel"], 32), g_wa_blk, g_wx_blk], axis=0) + links_free
    gathered = _all_gather([pack], "gather_small")[0]
    gs = _sum_parts(gathered, "sum_small")
    g_pre, g_post, g_memg = gs[0:16].reshape(1, D_MODEL), gs[16:32].reshape(1, D_MODEL), gs[32:48].reshape(1, D_MODEL)
    gvec = gs[48:112].reshape(8, D_RNN)
    g_conv_w = lax.dynamic_slice(gvec[0:CONV_W], (0, me * RNN_BLOCK), (CONV_W, RNN_BLOCK))
    g_conv_b, g_b_a, g_b_x, g_lam = gvec[4:5], gvec[5:6], gvec[6:7], gvec[7:8]
    g_sinks = gs[112:113, :SWA_HEADS]
    g_rel = gs[120:152, :SWA_HEADS]
    g_w_a = gathered[:, 152:280]
    g_w_x = gathered[:, 280:408]

    def packed(ts):
        pre, post, memg, cb, ba, bx, lm, wa, wx, sk, rel, cw = ts
        return jnp.concatenate([pre.reshape(16, 128), post.reshape(16, 128), memg.reshape(16, 128), cb.reshape(8, 128),
                                ba.reshape(8, 128), bx.reshape(8, 128), lm.reshape(8, 128), wa.reshape(1024, 128),
                                wx.reshape(1024, 128), _pad_rows(sk.reshape(1, SWA_HEADS), 8), _pad_rows(rel, 32),
                                _pad_rows(cw.reshape(CONV_W, RNN_BLOCK), 8)], axis=0)

    def unpacked(a):
        return (a[0:16].reshape(1, D_MODEL), a[16:32].reshape(1, D_MODEL), a[32:48].reshape(1, D_MODEL), a[48:56].reshape(1, D_RNN),
                a[56:64].reshape(1, D_RNN), a[64:72].reshape(1, D_RNN), a[72:80].reshape(1, D_RNN),
                a[80:1104].reshape(1, RNN_BLOCKS, RNN_BLOCK, RNN_BLOCK), a[1104:2128].reshape(1, RNN_BLOCKS, RNN_BLOCK, RNN_BLOCK),
                a[2128:2129, :SWA_HEADS], a[2136:2168, :SWA_HEADS], a[2168:2172].reshape(1, CONV_W, RNN_BLOCK))

    g_small = (g_pre, g_post, g_memg, g_conv_b, g_b_a, g_b_x, g_lam, g_w_a, g_w_x, g_sinks, g_rel, g_conv_w)
    w_small = (pre_norm_g, post_norm_g, mem_norm_g, conv_b, b_rg_a, b_rg_x, lru_lambda, w_rg_a, w_rg_x, swa_sinks, rel_bias, conv_w)
    m_small = (m_pre_norm_g, m_post_norm_g, m_mem_norm_g, m_conv_b, m_b_rg_a, m_b_rg_x, m_lru_lambda, m_w_rg_a, m_w_rg_x, m_swa_sinks, m_rel_bias, m_conv_w)
    v_small = (v_pre_norm_g, v_post_norm_g, v_mem_norm_g, v_conv_b, v_b_rg_a, v_b_rg_x, v_lru_lambda, v_w_rg_a, v_w_rg_x, v_swa_sinks, v_rel_bias, v_conv_w)
    sm = [unpacked(a) for a in _adamw(packed(g_small)[None], packed(w_small), packed(m_small), packed(v_small), "adamw_small")]


    loss_total = lax.psum(st["loss"][0, 0], AXES)

    def leaves(k):
        s = sm[k]
        return [s[0], s[1], s[2], big[0][k], s[11], s[3], s[7], s[4], s[8], s[5], s[6], s[9], s[10],
                big[1][k], big[2][k], big[3][k], big[4][k], big[5][k]]

    return (loss_total, grad_x[None], *leaves(0), *leaves(1), *leaves(2), *leaves(3))
```

```python
import math

import jax
import jax.numpy as jnp
import numpy as np
from jax import lax
from jax.experimental import pallas as pl
from jax.experimental.pallas import tpu as pltpu

F32, BF = jnp.float32, jnp.bfloat16
MESH = pl.DeviceIdType.MESH
AXES = ("x", "y", "c")
N_DEV = 8

D_MODEL = 2048
D_RNN = 1024
RNN_BLOCKS = 8
RNN_BLOCK = 128
CONV_W = 4
LRU_C = 8.0
SWA_HEADS = 16
SWA_KV_HEADS = 2
SWA_HD = 64
WINDOW = 128
MEM_HEADS = 4
MEM_HD = 256
D_MEM = 1024
REL_BUCKETS = 32
REL_MAX_DIST = 128
EPS = 1e-6
NEG_INF = -1e30
D_IN = 12544
SEGMENTS = (("xr", 0, 1024, F32), ("g_rg", 1024, 1024, F32), ("q_s", 2048, 1024, BF), ("kv", 3072, 256, BF),
            ("g_swa", 3328, 1024, F32), ("q_m", 4352, 1024, BF), ("g_mem", 5376, 1024, F32), ("gl", 6400, 6144, F32))
SEG_TILE = 256

ADAM_LR, ADAM_B1, ADAM_B2, ADAM_EPS, ADAM_WD, ADAM_STEP = 0.001, 0.9, 0.999, 1e-08, 0.01, 10

NN = (((1,), (0,)), ((), ()))
NT = (((1,), (1,)), ((), ()))
TN = (((0,), (0,)), ((), ()))
MIB = 2 ** 20


def _dot(a, b, dn):
    return lax.dot_general(a, b, dn, preferred_element_type=F32)


def _params(sem, vmem_mib=48):
    return pltpu.CompilerParams(dimension_semantics=sem, vmem_limit_bytes=vmem_mib * MIB)


def _sigmoid(z):
    return 1.0 / (1.0 + jnp.exp(-z))


def _softplus(z):
    return jnp.maximum(z, 0.0) + jnp.log(1.0 + jnp.exp(-jnp.abs(z)))


def _expm1(z):
    p = z * (1.0 + z * (0.5 + z * (1.0 / 6 + z * (1.0 / 24 + z * (1.0 / 120 + z * (1.0 / 720 + z * (1.0 / 5040 + z / 40320)))))))
    return jnp.where(jnp.abs(z) < 0.3, p, jnp.exp(z) - 1.0)


def _flat(p):
    return 4 * p[0] + 2 * p[1] + p[2]


def _all_gather(arrs, name):
    n = len(arrs)

    def body(*refs):
        ins, outs = refs[:n], refs[n:2 * n]
        send_sems, recv_sems, local_sems = refs[2 * n:]
        x, y, c = lax.axis_index("x"), lax.axis_index("y"), lax.axis_index("c")
        me, sibling = (x, y, c), (x, y, 1 - c)
        chips = [(1 - x, y), (x, 1 - y), (1 - x, 1 - y)]

        def copy(a, k, block, to, src=None):
            dst = outs[a].at[_flat(block)]
            return pltpu.make_async_remote_copy(src_ref=dst if src is None else src, dst_ref=dst,
                                                send_sem=send_sems.at[a * 7 + k], recv_sem=recv_sems.at[a * 7 + k],
                                                device_id=to, device_id_type=MESH)

        mine = [pltpu.make_async_copy(ins[a], outs[a].at[_flat(me)], local_sems.at[a]) for a in range(n)]
        for cp in mine:
            cp.start()
        first = []
        for a in range(n):
            first += [copy(a, 1 + j, me, (*chip, c), src=ins[a]) for j, chip in enumerate(chips)]
            first.append(copy(a, 0, me, sibling, src=ins[a]))
        for cp in first:
            cp.start()
        passed = []
        for j, chip in enumerate(chips):
            for a in range(n):
                copy(a, 1 + j, (*chip, c), me).wait_recv()
                fw = copy(a, 4 + j, (*chip, c), sibling)
                fw.start()
                passed.append(fw)
        for a in range(n):
            copy(a, 0, sibling, me).wait_recv()
            for j, chip in enumerate(chips):
                copy(a, 4 + j, (*chip, 1 - c), me).wait_recv()
        for cp in first + passed:
            cp.wait_send()
        for cp in mine:
            cp.wait()

    any_spec = pl.BlockSpec(memory_space=pl.ANY)
    return pl.pallas_call(
        body, name=name,
        out_shape=[jax.ShapeDtypeStruct((N_DEV,) + a.shape, a.dtype) for a in arrs],
        in_specs=[any_spec] * n, out_specs=[any_spec] * n,
        scratch_shapes=[pltpu.SemaphoreType.DMA((7 * n,)), pltpu.SemaphoreType.DMA((7 * n,)), pltpu.SemaphoreType.DMA((n,))],
    )(*arrs)


def _all_gather_relayed(arrs, relay, name):
    n = len(arrs)
    K = 9

    def body(*refs):
        ins, outs = refs[:n], refs[n:2 * n]
        send_sems, recv_sems, local_sems = refs[2 * n:]
        x, y, c = lax.axis_index("x"), lax.axis_index("y"), lax.axis_index("c")
        me, sib = (x, y, c), (x, y, 1 - c)
        xn, yn, dg = (1 - x, y, c), (x, 1 - y, c), (1 - x, 1 - y, c)

        def other(p):
            return (p[0], p[1], 1 - p[2])

        def rows(a, half):
            h = arrs[a].shape[0] // 2
            return pl.ds(half * h, h)

        def copy(a, k, block, to, half=None, src=None):
            dst = outs[a].at[_flat(block)]
            if half is not None:
                dst = dst.at[rows(a, half)]
            return pltpu.make_async_remote_copy(src_ref=dst if src is None else src, dst_ref=dst,
                                                send_sem=send_sems.at[a * K + k], recv_sem=recv_sems.at[a * K + k],
                                                device_id=to, device_id_type=MESH)

        mine = [pltpu.make_async_copy(ins[a], outs[a].at[_flat(me)], local_sems.at[a]) for a in range(n)]
        for cp in mine:
            cp.start()
        sends = []

        def start(cp):
            cp.start()
            sends.append(cp)

        for a in range(n):
            start(copy(a, 1, me, xn, src=ins[a]))
            start(copy(a, 2, me, yn, src=ins[a]))
            if not relay[a]:
                start(copy(a, 3, me, dg, src=ins[a]))
            start(copy(a, 0, me, sib, src=ins[a]))
        for a in range(n):
            copy(a, 1, xn, me).wait_recv()
            if relay[a]:
                start(copy(a, 3, xn, yn, half=0))
            start(copy(a, 5, xn, sib))
        for a in range(n):
            copy(a, 2, yn, me).wait_recv()
            if relay[a]:
                start(copy(a, 4, yn, xn, half=1))
            start(copy(a, 6, yn, sib))
        for a in range(n):
            if relay[a]:
                copy(a, 3, dg, me, half=0).wait_recv()
                start(copy(a, 7, dg, sib, half=0))
                copy(a, 4, dg, me, half=1).wait_recv()
                start(copy(a, 8, dg, sib, half=1))
            else:
                copy(a, 3, dg, me).wait_recv()
                start(copy(a, 7, dg, sib))
        for a in range(n):
            copy(a, 0, sib, me).wait_recv()
            copy(a, 5, other(xn), me).wait_recv()
            copy(a, 6, other(yn), me).wait_recv()
            if relay[a]:
                copy(a, 7, other(dg), me, half=0).wait_recv()
                copy(a, 8, other(dg), me, half=1).wait_recv()
            else:
                copy(a, 7, other(dg), me).wait_recv()
        for cp in sends:
            cp.wait_send()
        for cp in mine:
            cp.wait()

    any_spec = pl.BlockSpec(memory_space=pl.ANY)
    return pl.pallas_call(
        body, name=name,
        out_shape=[jax.ShapeDtypeStruct((N_DEV,) + a.shape, a.dtype) for a in arrs],
        in_specs=[any_spec] * n, out_specs=[any_spec] * n,
        scratch_shapes=[pltpu.SemaphoreType.DMA((K * n,)), pltpu.SemaphoreType.DMA((K * n,)), pltpu.SemaphoreType.DMA((n,))],
    )(*arrs)


def _chip_peers(x, y):
    return [(1 - x, y), (x, 1 - y), (1 - x, 1 - y)]


def _chip(p):
    return 2 * p[0] + p[1]


def _plan_gather(kinds):
    def plan(x, y, c):
        out = []
        for a, kind in enumerate(kinds):
            for peer in [(x, y, 1 - c)] + [(*ch, c) for ch in _chip_peers(x, y)]:
                out.append((a, None, (kind, _flat((x, y, c))), peer, (kind, _flat(peer))))
        return out
    return plan


def _plan_swap(n):
    def plan(x, y, c):
        return [(a, 1 - c, ("all", 0), (x, y, 1 - c), ("all", 0)) for a in range(n)]
    return plan


def _slot(ref, where):
    kind, k = where
    if kind == "all":
        return ref
    if kind == "lead":
        return ref.at[k]
    return ref.at[:, pl.ds(pl.multiple_of(k * 256, 256), 256)]


def _plan_scatter(n):
    def plan(x, y, c):
        out = []
        for a in range(n):
            for ch in _chip_peers(x, y):
                out.append((a, _chip(ch), ("lead", _chip((x, y))), (*ch, c), ("lead", _chip(ch))))
        return out
    return plan


HBM_SPEC = pl.BlockSpec(memory_space=pltpu.HBM)
SEM_SPEC = pl.BlockSpec(memory_space=pltpu.SEMAPHORE)


def _in_hbm(a):
    return pltpu.with_memory_space_constraint(a, pltpu.HBM)


def _exchange_start(srcs, lands, plan, name):
    n = len(srcs)
    count = len(plan(0, 0, 0))

    def body(*refs):
        src_refs, land_refs = refs[:n], refs[n:2 * n]
        send_sems, recv_sems = refs[2 * n], refs[2 * n + 1]
        token = refs[-1]
        x, y, c = lax.axis_index("x"), lax.axis_index("y"), lax.axis_index("c")
        for k, (a, si, di, peer, _) in enumerate(plan(x, y, c)):
            src = src_refs[a] if si is None else src_refs[a].at[si]
            pltpu.make_async_remote_copy(src_ref=src, dst_ref=_slot(land_refs[a], di), send_sem=send_sems.at[k],
                                         recv_sem=recv_sems.at[k], device_id=peer, device_id_type=MESH).start()
        token[...] = jnp.zeros_like(token)

    out = pl.pallas_call(
        body, name=name,
        out_shape=(pltpu.SemaphoreType.DMA((count,)), pltpu.SemaphoreType.DMA((count,)),
                   *[pltpu.HBM(a.shape, a.dtype) for a in lands], jax.ShapeDtypeStruct((8, 128), F32)),
        in_specs=[HBM_SPEC] * (2 * n),
        out_specs=(SEM_SPEC, SEM_SPEC, *([HBM_SPEC] * n), pl.BlockSpec(memory_space=pltpu.VMEM)),
        input_output_aliases={n + i: 2 + i for i in range(n)},
        compiler_params=pltpu.CompilerParams(has_side_effects=pltpu.SideEffectType.DATAFLOW_SIDE_EFFECTING),
    )(*[_in_hbm(a) for a in srcs], *[_in_hbm(a) for a in lands])
    return out[0], out[1], list(srcs), list(out[2:2 + n]), out[-1]


def _exchange_wait(send_sems, recv_sems, srcs, lands, plan, after, name):
    n = len(srcs)

    def body(*refs):
        src_refs, land_refs = refs[:n], refs[n:2 * n]
        send_sems, recv_sems = refs[2 * n], refs[2 * n + 1]
        x, y, c = lax.axis_index("x"), lax.axis_index("y"), lax.axis_index("c")
        for k, (a, si, _, peer, ri) in enumerate(plan(x, y, c)):
            src = src_refs[a] if si is None else src_refs[a].at[si]
            cp = pltpu.make_async_remote_copy(src_ref=src, dst_ref=_slot(land_refs[a], ri), send_sem=send_sems.at[k],
                                              recv_sem=recv_sems.at[k], device_id=peer, device_id_type=MESH)
            cp.wait_send()
            cp.wait_recv()

    out = pl.pallas_call(
        body, name=name,
        out_shape=tuple(pltpu.HBM(a.shape, a.dtype) for a in lands),
        in_specs=[HBM_SPEC] * (2 * n) + [SEM_SPEC, SEM_SPEC, pl.BlockSpec(memory_space=pl.ANY)],
        out_specs=tuple([HBM_SPEC] * n),
        input_output_aliases={n + i: i for i in range(n)},
        compiler_params=pltpu.CompilerParams(has_side_effects=pltpu.SideEffectType.DATAFLOW_SIDE_EFFECTING),
    )(*[_in_hbm(a) for a in srcs], *lands, send_sems, recv_sems, after)
    return list(out)


def _forward_to_sibling(lands, kinds, name):
    n = len(lands)

    def body(*refs):
        in_refs, out_refs = refs[:n], refs[n:2 * n]
        send_sems, recv_sems = refs[2 * n:]
        x, y, c = lax.axis_index("x"), lax.axis_index("y"), lax.axis_index("c")
        sibling = (x, y, 1 - c)

        def copy(a, j, slot):
            return pltpu.make_async_remote_copy(src_ref=_slot(in_refs[a], (kinds[a], slot)), dst_ref=_slot(out_refs[a], (kinds[a], slot)),
                                                send_sem=send_sems.at[a * 3 + j], recv_sem=recv_sems.at[a * 3 + j],
                                                device_id=sibling, device_id_type=MESH)

        sends = [copy(a, j, _flat((*ch, c))) for a in range(n) for j, ch in enumerate(_chip_peers(x, y))]
        for cp in sends:
            cp.start()
        for a in range(n):
            for j, ch in enumerate(_chip_peers(x, y)):
                copy(a, j, _flat((*ch, 1 - c))).wait_recv()
        for cp in sends:
            cp.wait_send()

    any_spec = pl.BlockSpec(memory_space=pl.ANY)
    return pl.pallas_call(
        body, name=name, out_shape=[jax.ShapeDtypeStruct(a.shape, a.dtype) for a in lands],
        in_specs=[any_spec] * n, out_specs=[any_spec] * n, input_output_aliases={a: a for a in range(n)},
        scratch_shapes=[pltpu.SemaphoreType.DMA((3 * n,)), pltpu.SemaphoreType.DMA((3 * n,))],
    )(*lands)


def _place_own(zones, owns, slot, name):
    n = len(zones)

    def body(slot_ref, *refs):
        for a in range(n):
            refs[2 * n + a][...] = refs[a][...]

    return pl.pallas_call(
        body, name=name,
        grid_spec=pltpu.PrefetchScalarGridSpec(
            num_scalar_prefetch=1, grid=(1,),
            in_specs=[pl.BlockSpec(o.shape, lambda i, s_ref: (0, 0)) for o in owns] + [pl.BlockSpec(memory_space=pl.ANY)] * n,
            out_specs=[pl.BlockSpec((None,) + o.shape, lambda i, s_ref: (s_ref[0], 0, 0)) for o in owns]),
        out_shape=[jax.ShapeDtypeStruct(z.shape, z.dtype) for z in zones],
        input_output_aliases={1 + n + a: a for a in range(n)},
        compiler_params=_params(("arbitrary",)),
    )(slot, *owns, *zones)


def _swap_with_sibling(parts, name):
    n = len(parts)

    def body(*refs):
        in_refs, out_refs = refs[:n], refs[n:2 * n]
        send_sems, recv_sems = refs[2 * n:]
        x, y, c = lax.axis_index("x"), lax.axis_index("y"), lax.axis_index("c")
        sends = [pltpu.make_async_remote_copy(src_ref=in_refs[a].at[1 - c], dst_ref=out_refs[a], send_sem=send_sems.at[a],
                                              recv_sem=recv_sems.at[a], device_id=(x, y, 1 - c), device_id_type=MESH)
                 for a in range(n)]
        for cp in sends:
            cp.start()
        for cp in sends:
            cp.wait()

    any_spec = pl.BlockSpec(memory_space=pl.ANY)
    return pl.pallas_call(
        body, name=name, out_shape=[jax.ShapeDtypeStruct(a.shape[1:], a.dtype) for a in parts],
        in_specs=[any_spec] * n, out_specs=[any_spec] * n,
        scratch_shapes=[pltpu.SemaphoreType.DMA((n,)), pltpu.SemaphoreType.DMA((n,))],
    )(*parts)


def _pair_sum(parts, got, core, name):
    _, _, R, C = parts.shape
    tr = 256 if R % 256 == 0 else R

    def body(c_ref, p_ref, g_ref, o_ref):
        o_ref[...] = (p_ref[...].astype(F32) + g_ref[...].astype(F32)).astype(o_ref.dtype)

    return pl.pallas_call(
        body, name=name,
        grid_spec=pltpu.PrefetchScalarGridSpec(
            num_scalar_prefetch=1, grid=(4, R // tr),
            in_specs=[pl.BlockSpec((None, None, tr, C), lambda j, i, c_ref: (c_ref[0], j, i, 0)),
                      pl.BlockSpec((None, tr, C), lambda j, i, c_ref: (j, i, 0))],
            out_specs=pl.BlockSpec((None, tr, C), lambda j, i, c_ref: (j, i, 0))),
        out_shape=jax.ShapeDtypeStruct((4, R, C), parts.dtype),
        compiler_params=_params(("parallel", "parallel")),
    )(core, parts, got)


def _matmul(a, b, mode, M, N, K, tm, tn, tk, out_dtype, name, b_noff=0, a_moff=0, a_koff=0, b_blocked=False, out_blocked=None,
            dep=None, addend=None, vmem_mib=48):
    nm, nn, nk = M // tm, N // tn, K // tk
    if mode == "nn":
        a_spec = pl.BlockSpec((tm, tk), lambda j, i, k: (i, k + a_koff))
        b_spec = pl.BlockSpec((tk, tn), lambda j, i, k: (k, j + b_noff))
        dn = NN
    elif mode == "nt":
        a_spec = pl.BlockSpec((tm, tk), lambda j, i, k: (i, k + a_koff))
        if b_blocked:
            b_spec = pl.BlockSpec((None, tn, tk), lambda j, i, k: (k, j, 0))
        else:
            b_spec = pl.BlockSpec((tn, tk), lambda j, i, k: (j + b_noff, k))
        dn = NT
    else:
        a_spec = pl.BlockSpec((tk, tm), lambda j, i, k: (k, i + a_moff))
        if b_blocked:
            b_spec = pl.BlockSpec((None, tk, tn), lambda j, i, k: (j, k, 0))
        else:
            b_spec = pl.BlockSpec((tk, tn), lambda j, i, k: (k, j + b_noff))
        dn = TN
    if out_blocked == "col":
        out_shape = jax.ShapeDtypeStruct((2, 4, M, tn), out_dtype)
        out_spec = pl.BlockSpec((None, None, tm, tn), lambda j, i, k: (j % 2, j // 2, i, 0))
    elif out_blocked == "row":
        out_shape = jax.ShapeDtypeStruct((2, 4, tm, N), out_dtype)
        out_spec = pl.BlockSpec((None, None, tm, tn), lambda j, i, k: (i % 2, i // 2, 0, j))
    elif out_blocked == "third":
        out_shape = jax.ShapeDtypeStruct((3, M, N // 3), out_dtype)
        out_spec = pl.BlockSpec((None, tm, tn), lambda j, i, k: (j // (nn // 3), i, j % (nn // 3)))
    else:
        out_shape = jax.ShapeDtypeStruct((M, N), out_dtype)
        out_spec = pl.BlockSpec((tm, tn), lambda j, i, k: (i, j))

    n_extra = (addend is not None) + (dep is not None)

    def body(a_ref, b_ref, *rest):
        o_ref, scratch = rest[n_extra], rest[n_extra + 1:]
        if nk == 1:
            prod = _dot(a_ref[...], b_ref[...], dn)
            if addend is not None:
                prod = prod + rest[0][...].astype(F32)
            o_ref[...] = prod.astype(out_dtype)
        else:
            assert addend is None
            acc_ref, = scratch
            k = pl.program_id(2)

            @pl.when(k == 0)
            def _():
                acc_ref[...] = jnp.zeros_like(acc_ref)

            acc_ref[...] += _dot(a_ref[...], b_ref[...], dn)

            @pl.when(k == nk - 1)
            def _():
                o_ref[...] = acc_ref[...].astype(out_dtype)

    return pl.pallas_call(
        body, name=name, grid=(nn, nm, nk),
        in_specs=[a_spec, b_spec] + ([] if addend is None else [out_spec])
        + ([] if dep is None else [pl.BlockSpec((8, 128), lambda j, i, k: (0, 0))]),
        out_specs=out_spec, out_shape=out_shape,
        scratch_shapes=[] if nk == 1 else [pltpu.VMEM((tm, tn), F32)],
        compiler_params=_params(("parallel", "parallel", "arbitrary"), vmem_mib),
    )(a, b, *([] if addend is None else [addend]), *([] if dep is None else [dep]))


def _rms_fwd(x, g, name):
    R, Dm = x.shape
    tr = min(R, 256)

    def body(x_ref, g_ref, h_ref):
        xv = x_ref[...]
        r = lax.rsqrt(jnp.mean(xv * xv, axis=-1, keepdims=True) + EPS)
        h_ref[...] = (xv * r * g_ref[...]).astype(BF)

    return pl.pallas_call(
        body, name=name, grid=(R // tr,),
        in_specs=[pl.BlockSpec((tr, Dm), lambda i: (i, 0)), pl.BlockSpec((1, Dm), lambda i: (0, 0))],
        out_specs=pl.BlockSpec((tr, Dm), lambda i: (i, 0)), out_shape=jax.ShapeDtypeStruct((R, Dm), BF),
        compiler_params=_params(("parallel",)),
    )(x, g)


def _rms_gain_grad(dn, x, name):
    R, Dm = x.shape

    def body(dn_ref, x_ref, o_ref):
        xv = x_ref[...]
        r = lax.rsqrt(jnp.mean(xv * xv, axis=-1, keepdims=True) + EPS)
        o_ref[...] = jnp.sum(dn_ref[...] * xv * r, axis=0, keepdims=True)

    return pl.pallas_call(
        body, name=name, out_shape=jax.ShapeDtypeStruct((1, Dm), F32),
        compiler_params=pltpu.CompilerParams(vmem_limit_bytes=32 * MIB),
    )(dn, x)


def _shift_down(v, k, head8, row, T):
    if k == 0:
        return v
    r = pltpu.roll(v, k, 0)
    hr = pltpu.roll(head8, k, 0)
    top = jnp.where(row[:8] < k, hr, r[:8])
    return jnp.concatenate([top, r[8:]], axis=0)


def _shift_up(v, k, tail8, row, T):
    if k == 0:
        return v
    r = pltpu.roll(v, T - k, 0)
    tr = pltpu.roll(tail8, 8 - k, 0)
    bot = jnp.where(row[:8] >= 8 - k, tr, r[T - 8:])
    return jnp.concatenate([r[:T - 8], bot], axis=0)


def _rglru_gates(u, head8, grow, row, T, cw_ref, cb_ref, wa_ref, ba_ref, wx_ref, bx_ref, lam_ref):
    us = [_shift_down(u, k, head8, row, T) for k in range(CONV_W)]
    acc = us[0] * cw_ref[0:1, :]
    for k in range(1, CONV_W):
        acc = acc + us[k] * cw_ref[k:k + 1, :]
    conv = cb_ref[...] + acc
    cbf = conv.astype(BF)
    r_ = _sigmoid(_dot(cbf, wa_ref[0], NN) + ba_ref[...])
    i_ = _sigmoid(_dot(cbf, wx_ref[0], NN) + bx_ref[...])
    sp = _softplus(-lam_ref[...])
    la = -LRU_C * r_ * sp
    a = jnp.exp(la)
    mult_raw = jnp.sqrt(-_expm1(2.0 * la))
    mult = jnp.where(grow == 0, 1.0, mult_raw)
    return us, conv, cbf, r_, i_, sp, a, mult_raw, mult


def _rglru_specs(T, nt, rev):
    tmap = (lambda n, t: (nt - 1 - t, n)) if rev else (lambda n, t: (t, n))
    hmap = ((lambda n, t: (jnp.maximum((nt - 1 - t) * (T // 8) - 1, 0), n)) if rev
            else (lambda n, t: (jnp.maximum(t * (T // 8) - 1, 0), n)))
    tile = pl.BlockSpec((T, RNN_BLOCK), tmap)
    halo = pl.BlockSpec((8, RNN_BLOCK), hmap)
    vec = pl.BlockSpec((1, RNN_BLOCK), lambda n, t: (0, n))
    cw = pl.BlockSpec((CONV_W, RNN_BLOCK), lambda n, t: (0, n))
    wblk = pl.BlockSpec((1, RNN_BLOCK, RNN_BLOCK), lambda n, t: (n, 0, 0))
    return tile, halo, vec, cw, wblk


def _rglru_fwd(xr, g, cw, cb, wa, ba, wx, bx, lam, T):
    S = xr.shape[0]
    nt = S // T

    def body(u_ref, uh_ref, g_ref, cw_ref, cb_ref, wa_ref, ba_ref, wx_ref, bx_ref, lam_ref, h_ref, y_ref, carry):
        t = pl.program_id(1)

        @pl.when(t == 0)
        def _():
            carry[...] = jnp.zeros_like(carry)

        row = lax.broadcasted_iota(jnp.int32, (T, RNN_BLOCK), 0)
        grow = row + t * T
        head8 = jnp.where(t > 0, uh_ref[...], 0.0)
        _, conv, _, _, i_, _, a, _, mult = _rglru_gates(u_ref[...], head8, grow, row, T, cw_ref, cb_ref, wa_ref, ba_ref,
                                                         wx_ref, bx_ref, lam_ref)
        b = mult * i_ * conv
        s = 1
        while s < T:
            keep = row >= s
            a_s = jnp.where(keep, pltpu.roll(a, s, 0), 1.0)
            b_s = jnp.where(keep, pltpu.roll(b, s, 0), 0.0)
            b = a * b_s + b
            a = a * a_s
            s *= 2
        h = b + a * carry[0:1, :]
        carry[...] = jnp.broadcast_to(h[T - 1:T, :], carry.shape)
        h_ref[...] = h
        gv = g_ref[...]
        y_ref[...] = (h * (gv * _sigmoid(gv))).astype(BF)

    tile, halo, vec, cwspec, wblk = _rglru_specs(T, nt, False)
    return pl.pallas_call(
        body, name="rglru_fwd", grid=(RNN_BLOCKS, nt),
        in_specs=[tile, halo, tile, cwspec, vec, wblk, vec, wblk, vec, vec],
        out_specs=[tile, tile],
        out_shape=[jax.ShapeDtypeStruct((S, D_RNN), F32), jax.ShapeDtypeStruct((S, D_RNN), BF)],
        scratch_shapes=[pltpu.VMEM((8, RNN_BLOCK), F32)],
        compiler_params=_params(("parallel", "arbitrary")),
    )(xr, xr, g, cw, cb, wa, ba, wx, bx, lam)


def _rglru_bwd(xr, g, h, dy, cw, cb, wa, ba, wx, bx, lam, T):
    S = xr.shape[0]
    nt = S // T

    def body(u_ref, uh_ref, g_ref, h_ref, hh_ref, dy_ref, cw_ref, cb_ref, wa_ref, ba_ref, wx_ref, bx_ref, lam_ref,
             du_ref, dg_ref, dwa_ref, dwx_ref, dvec_ref, c_dhh, c_a, c_dconv):
        t = pl.program_id(1)
        tt = nt - 1 - t

        @pl.when(t == 0)
        def _():
            c_dhh[...] = jnp.zeros_like(c_dhh)
            c_a[...] = jnp.zeros_like(c_a)
            c_dconv[...] = jnp.zeros_like(c_dconv)
            dwa_ref[...] = jnp.zeros_like(dwa_ref)
            dwx_ref[...] = jnp.zeros_like(dwx_ref)
            dvec_ref[...] = jnp.zeros_like(dvec_ref)

        row = lax.broadcasted_iota(jnp.int32, (T, RNN_BLOCK), 0)
        row8 = row[:8]
        grow = row + tt * T
        head8 = jnp.where(tt > 0, uh_ref[...], 0.0)
        us, conv, cbf, r_, i_, sp, a, mult_raw, mult = _rglru_gates(
            u_ref[...], head8, grow, row, T, cw_ref, cb_ref, wa_ref, ba_ref, wx_ref, bx_ref, lam_ref)
        hv = h_ref[...]
        hprev = _shift_down(hv, 1, jnp.where(tt > 0, hh_ref[...], 0.0), row, T)
        gv = g_ref[...]
        sg = _sigmoid(gv)
        dyv = dy_ref[...]
        dg_ref[...] = (dyv * hv * (sg * (1.0 + gv * (1.0 - sg)))).astype(BF)
        d = dyv * (gv * sg)
        A = _shift_up(a, 1, c_a[...], row, T)
        s = 1
        while s < T:
            keep = row < T - s
            A_s = jnp.where(keep, pltpu.roll(A, T - s, 0), 1.0)
            d_s = jnp.where(keep, pltpu.roll(d, T - s, 0), 0.0)
            d = A * d_s + d
            A = A * A_s
            s *= 2
        dhh = d + A * c_dhh[0:1, :]
        da = dhh * hprev
        dconv = dhh * mult * i_
        di = dhh * mult * conv
        dmult = dhh * i_ * conv
        dla = da * a - jnp.where(grow == 0, 0.0, dmult * (a * a) / mult_raw)
        dr = dla * (-LRU_C * sp)
        dsp = jnp.sum(dla * (-LRU_C * r_), axis=0, keepdims=True)
        dza = dr * r_ * (1.0 - r_)
        dzx = di * i_ * (1.0 - i_)
        dza_b, dzx_b = dza.astype(BF), dzx.astype(BF)
        dconv = dconv + _dot(dza_b, wa_ref[0], NT) + _dot(dzx_b, wx_ref[0], NT)
        dwa_ref[0] += _dot(cbf, dza_b, TN)
        dwx_ref[0] += _dot(cbf, dzx_b, TN)
        lam = lam_ref[...]
        rows = [jnp.sum(dconv * us[k], axis=0, keepdims=True) for k in range(CONV_W)]
        rows += [jnp.sum(dconv, axis=0, keepdims=True), jnp.sum(dza, axis=0, keepdims=True),
                 jnp.sum(dzx, axis=0, keepdims=True), dsp * (-_sigmoid(-lam))]
        upd = jnp.zeros((8, RNN_BLOCK), F32)
        for j, rv in enumerate(rows):
            upd = upd + jnp.where(row8 == j, rv, 0.0)
        dvec_ref[...] += upd
        tail8 = c_dconv[...]
        du = dconv * cw_ref[0:1, :]
        for k in range(1, CONV_W):
            du = du + _shift_up(dconv, k, tail8, row, T) * cw_ref[k:k + 1, :]
        du_ref[...] = du.astype(BF)
        c_dhh[...] = jnp.broadcast_to(dhh[0:1, :], c_dhh.shape)
        c_a[...] = jnp.broadcast_to(a[0:1, :], c_a.shape)
        c_dconv[...] = dconv[:8]

    tile, halo, vec, cwspec, wblk = _rglru_specs(T, nt, True)
    acc8 = pl.BlockSpec((8, RNN_BLOCK), lambda n, t: (0, n))
    return pl.pallas_call(
        body, name="rglru_bwd", grid=(RNN_BLOCKS, nt),
        in_specs=[tile, halo, tile, tile, halo, tile, cwspec, vec, wblk, vec, wblk, vec, vec],
        out_specs=[tile, tile, wblk, wblk, acc8],
        out_shape=[jax.ShapeDtypeStruct((S, D_RNN), BF), jax.ShapeDtypeStruct((S, D_RNN), BF),
                   jax.ShapeDtypeStruct((RNN_BLOCKS, RNN_BLOCK, RNN_BLOCK), F32),
                   jax.ShapeDtypeStruct((RNN_BLOCKS, RNN_BLOCK, RNN_BLOCK), F32),
                   jax.ShapeDtypeStruct((8, D_RNN), F32)],
        scratch_shapes=[pltpu.VMEM((8, RNN_BLOCK), F32)] * 3,
        compiler_params=_params(("parallel", "arbitrary")),
    )(xr, xr, g, h, h, dy, cw, cb, wa, ba, wx, bx, lam)


def _rel_bucket_map():
    qi = np.arange(WINDOW)[:, None]
    kj = np.arange(2 * WINDOW)[None, :]
    dist = jnp.asarray(qi + WINDOW - kj, jnp.int32)
    n = jnp.maximum(dist, 0)
    max_exact = REL_BUCKETS // 2
    ratio = jnp.log(jnp.maximum(n, 1).astype(F32) / max_exact) / math.log(REL_MAX_DIST / max_exact)
    large = jnp.minimum(max_exact + (ratio * (REL_BUCKETS - max_exact)).astype(jnp.int32), REL_BUCKETS - 1)
    bucket = jnp.where(n < max_exact, n, large).astype(jnp.int32)
    j = np.arange(WINDOW)[None, :]
    return jnp.where(jnp.asarray(j > qi), bucket[:, :WINDOW], bucket[:, WINDOW:])


def _swa_common(n, kv_ref, bucket_ref, relb_ref, bias_scr):
    @pl.when(n == 0)
    def _():
        bk = bucket_ref[...]
        for h in range(SWA_HEADS):
            acc = jnp.zeros((WINDOW, WINDOW), F32)
            for b in range(REL_BUCKETS):
                acc = acc + jnp.where(bk == b, relb_ref[b, h], 0.0)
            bias_scr[h] = acc

    prev0 = pl.multiple_of(jnp.maximum(n - 1, 0) * WINDOW, WINDOW)
    cur0 = pl.multiple_of(n * WINDOW, WINDOW)
    kk = jnp.concatenate([kv_ref[pl.ds(prev0, WINDOW), :], kv_ref[pl.ds(cur0, WINDOW), :]], axis=0).astype(F32)
    rowi = lax.broadcasted_iota(jnp.int32, (WINDOW, WINDOW), 0)
    col = lax.broadcasted_iota(jnp.int32, (WINDOW, WINDOW), 1)
    from_prev = col > rowi
    return kk, from_prev, prev0, cur0


def _fold(full, from_prev):
    return jnp.where(from_prev, full[:, :WINDOW], full[:, WINDOW:])


def _unfold(sq, from_prev):
    return jnp.concatenate([jnp.where(from_prev, sq, 0.0), jnp.where(from_prev, 0.0, sq)], axis=1)


def _half_pair(part, kvh):
    lo = lax.broadcasted_iota(jnp.int32, part.shape, 1) < SWA_HD
    if kvh == 0:
        pa = jnp.where(lo, part, 0.0)
        pb = pltpu.roll(pa, SWA_HD, 1)
    else:
        pb = jnp.where(lo, 0.0, part)
        pa = pltpu.roll(pb, SWA_HD, 1)
    return pa.astype(BF), pb.astype(BF)


ALL_HEADS = SWA_HEADS * WINDOW


def _sink_column(sinks):
    return jnp.repeat(sinks.reshape(SWA_HEADS), WINDOW).reshape(ALL_HEADS, 1)


def _swa_operands(kk):
    return [(_half_pair(kk[:, :128], kvh), _half_pair(kk[:, 128:], kvh)) for kvh in range(SWA_KV_HEADS)]


def _swa_probs(n, q_ref, ops, bias_scr, sinkc_ref, from_prev):
    lgs = []
    for kvh in range(SWA_KV_HEADS):
        (ka, kb), _ = ops[kvh]
        for p in range(4):
            q2 = q_ref[:, kvh * 512 + p * 128:kvh * 512 + p * 128 + 128]
            lgs += [_fold(_dot(q2, ka, NT), from_prev), _fold(_dot(q2, kb, NT), from_prev)]
    lg = jnp.concatenate(lgs, axis=0) * (SWA_HD ** -0.5) + bias_scr[...].reshape(ALL_HEADS, WINDOW)
    rowi = jnp.bitwise_and(lax.broadcasted_iota(jnp.int32, (ALL_HEADS, WINDOW), 0), WINDOW - 1)
    col = lax.broadcasted_iota(jnp.int32, (ALL_HEADS, WINDOW), 1)
    no_prev = jnp.where(n > 0, 0, 4 * WINDOW)
    lg = jnp.where(jnp.logical_or(col <= rowi, col > rowi + no_prev), lg, NEG_INF)
    sink = sinkc_ref[...]
    m = jnp.maximum(jnp.max(lg, axis=-1, keepdims=True), sink)
    e = jnp.exp(lg - m)
    es = jnp.exp(sink - m)
    den = jnp.sum(e, axis=-1, keepdims=True) + es
    return e / den, es / den


def _swa_fwd(q, kv, g, bucket, rel_bias, sink_col):
    S = q.shape[0]
    nb = S // WINDOW

    def body(q_ref, kv_ref, g_ref, bucket_ref, relb_ref, sinkc_ref, o_ref, y_ref, bias_scr):
        n = pl.program_id(0)
        kk, from_prev, _, _ = _swa_common(n, kv_ref, bucket_ref, relb_ref, bias_scr)
        ops = _swa_operands(kk)
        pr, _ = _swa_probs(n, q_ref, ops, bias_scr, sinkc_ref, from_prev)
        for kvh in range(SWA_KV_HEADS):
            _, (va, vb) = ops[kvh]
            for p in range(4):
                c0 = kvh * 512 + p * 128
                r0 = (kvh * 8 + 2 * p) * WINDOW
                o2 = (_dot(_unfold(pr[r0:r0 + WINDOW], from_prev).astype(BF), va, NN)
                      + _dot(_unfold(pr[r0 + WINDOW:r0 + 2 * WINDOW], from_prev).astype(BF), vb, NN))
                o_ref[:, c0:c0 + 128] = o2
                gv = g_ref[:, c0:c0 + 128]
                y_ref[:, c0:c0 + 128] = (o2 * (gv * _sigmoid(gv))).astype(BF)

    blk = pl.BlockSpec((WINDOW, 1024), lambda n: (n, 0))
    smem = pl.BlockSpec(memory_space=pltpu.SMEM)
    sinkc = pl.BlockSpec((ALL_HEADS, 1), lambda n: (0, 0))
    return pl.pallas_call(
        body, name="swa_fwd", grid=(nb,),
        in_specs=[blk, pl.BlockSpec((S, 256), lambda n: (0, 0)), blk, pl.BlockSpec((WINDOW, WINDOW), lambda n: (0, 0)), smem, sinkc],
        out_specs=[blk, blk],
        out_shape=[jax.ShapeDtypeStruct((S, 1024), F32), jax.ShapeDtypeStruct((S, 1024), BF)],
        scratch_shapes=[pltpu.VMEM((SWA_HEADS, WINDOW, WINDOW), F32)],
        compiler_params=_params(("arbitrary",)),
    )(q, kv, g, bucket, rel_bias, sink_col)


def _swa_bwd(q, kv, g, o, dy, bucket, rel_bias, sink_col):
    S = q.shape[0]
    nb = S // WINDOW

    def body(q_ref, kv_ref, g_ref, o_ref, dy_ref, bucket_ref, relb_ref, sinkc_ref,
             dq_ref, dg_ref, dkv_ref, dsink_ref, drel_ref, bias_scr, dbias_scr, dsink_scr):
        n = pl.program_id(0)

        @pl.when(n == 0)
        def _():
            dbias_scr[...] = jnp.zeros_like(dbias_scr)
            dsink_scr[...] = jnp.zeros_like(dsink_scr)
            dkv_ref[...] = jnp.zeros_like(dkv_ref)

        kk, from_prev, prev0, cur0 = _swa_common(n, kv_ref, bucket_ref, relb_ref, bias_scr)
        ops = _swa_operands(kk)
        pr, ps = _swa_probs(n, q_ref, ops, bias_scr, sinkc_ref, from_prev)
        do2s, dps = [], []
        for kvh in range(SWA_KV_HEADS):
            _, (va, vb) = ops[kvh]
            for p in range(4):
                c0 = kvh * 512 + p * 128
                gv = g_ref[:, c0:c0 + 128]
                sg = _sigmoid(gv)
                dyv = dy_ref[:, c0:c0 + 128]
                dg_ref[:, c0:c0 + 128] = (dyv * o_ref[:, c0:c0 + 128] * (sg * (1.0 + gv * (1.0 - sg)))).astype(BF)
                do2 = (dyv * (gv * sg)).astype(BF)
                do2s.append(do2)
                dps += [_fold(_dot(do2, va, NT), from_prev), _fold(_dot(do2, vb, NT), from_prev)]
        dp = jnp.concatenate(dps, axis=0)
        delta = jnp.sum(pr * dp, axis=-1, keepdims=True)
        ds = pr * (dp - delta)
        dbias_scr[...] += ds.reshape(SWA_HEADS, WINDOW, WINDOW)
        dsink_scr[...] += ps * delta
        dsc = ds * (SWA_HD ** -0.5)
        lo256 = lax.broadcasted_iota(jnp.int32, (2 * WINDOW, 128), 1) < SWA_HD
        dks, dvs = [], []
        for kvh in range(SWA_KV_HEADS):
            (ka, kb), _ = ops[kvh]
            dka = jnp.zeros((2 * WINDOW, 128), F32)
            dkb, dva, dvb = dka, dka, dka
            for p in range(4):
                c0 = kvh * 512 + p * 128
                r0 = (kvh * 8 + 2 * p) * WINDOW
                q2 = q_ref[:, c0:c0 + 128]
                do2 = do2s[kvh * 4 + p]
                ds0 = _unfold(dsc[r0:r0 + WINDOW], from_prev).astype(BF)
                ds1 = _unfold(dsc[r0 + WINDOW:r0 + 2 * WINDOW], from_prev).astype(BF)
                dq_ref[:, c0:c0 + 128] = (_dot(ds0, ka, NN) + _dot(ds1, kb, NN)).astype(BF)
                dka = dka + _dot(ds0, q2, TN)
                dkb = dkb + _dot(ds1, q2, TN)
                dva = dva + _dot(_unfold(pr[r0:r0 + WINDOW], from_prev).astype(BF), do2, TN)
                dvb = dvb + _dot(_unfold(pr[r0 + WINDOW:r0 + 2 * WINDOW], from_prev).astype(BF), do2, TN)
            dks.append(jnp.where(lo256, dka, 0.0) + pltpu.roll(jnp.where(lo256, 0.0, dkb), SWA_HD, 1))
            dvs.append(jnp.where(lo256, dva, 0.0) + pltpu.roll(jnp.where(lo256, 0.0, dvb), SWA_HD, 1))
        dk = dks[0] + pltpu.roll(dks[1], SWA_HD, 1)
        dv = dvs[0] + pltpu.roll(dvs[1], SWA_HD, 1)
        dkv_ref[pl.ds(prev0, WINDOW), 0:128] += dk[:WINDOW]
        dkv_ref[pl.ds(prev0, WINDOW), 128:256] += dv[:WINDOW]
        dkv_ref[pl.ds(cur0, WINDOW), 0:128] += dk[WINDOW:]
        dkv_ref[pl.ds(cur0, WINDOW), 128:256] += dv[WINDOW:]

        @pl.when(n == nb - 1)
        def _():
            dsink_ref[...] = -jnp.sum(dsink_scr[...].reshape(SWA_HEADS, WINDOW, 1), axis=1)
            bk = bucket_ref[...]
            sums = []
            for b in range(REL_BUCKETS):
                sums.append(jnp.sum(jnp.where((bk == b)[None], dbias_scr[...], 0.0), axis=1))
            drel_ref[...] = jnp.sum(jnp.concatenate(sums, axis=0), axis=1, keepdims=True)

    blk = pl.BlockSpec((WINDOW, 1024), lambda n: (n, 0))
    smem = pl.BlockSpec(memory_space=pltpu.SMEM)
    whole = lambda shape: pl.BlockSpec(shape, lambda n: (0, 0))
    return pl.pallas_call(
        body, name="swa_bwd", grid=(nb,),
        in_specs=[blk, whole((S, 256)), blk, blk, blk, whole((WINDOW, WINDOW)), smem, whole((ALL_HEADS, 1))],
        out_specs=[blk, blk, whole((S, 256)), whole((SWA_HEADS, 1)), whole((REL_BUCKETS * SWA_HEADS, 1))],
        out_shape=[jax.ShapeDtypeStruct((S, 1024), BF), jax.ShapeDtypeStruct((S, 1024), BF),
                   jax.ShapeDtypeStruct((S, 256), F32), jax.ShapeDtypeStruct((SWA_HEADS, 1), F32),
                   jax.ShapeDtypeStruct((REL_BUCKETS * SWA_HEADS, 1), F32)],
        scratch_shapes=[pltpu.VMEM((SWA_HEADS, WINDOW, WINDOW), F32), pltpu.VMEM((SWA_HEADS, WINDOW, WINDOW), F32),
                        pltpu.VMEM((ALL_HEADS, 1), F32)],
        compiler_params=_params(("arbitrary",)),
    )(q, kv, g, o, dy, bucket, rel_bias, sink_col)


def _mem_probs(qh, mk):
    lg = _dot(qh, mk, NT) * (MEM_HD ** -0.5)
    e = jnp.exp(lg - jnp.max(lg, axis=-1, keepdims=True))
    return e / jnp.sum(e, axis=-1, keepdims=True)


def _mem_fwd(q, mkv, g):
    S = q.shape[0]
    M = mkv.shape[0]
    tq = 256

    def body(q_ref, mkv_ref, g_ref, o_ref, y_ref):
        for h in range(MEM_HEADS):
            c0 = h * MEM_HD
            pr = _mem_probs(q_ref[:, c0:c0 + MEM_HD], mkv_ref[:, c0:c0 + MEM_HD])
            o = _dot(pr.astype(BF), mkv_ref[:, D_MEM + c0:D_MEM + c0 + MEM_HD], NN)
            o_ref[:, c0:c0 + MEM_HD] = o
            gv = g_ref[:, c0:c0 + MEM_HD]
            y_ref[:, c0:c0 + MEM_HD] = (o * (gv * _sigmoid(gv))).astype(BF)

    blk = pl.BlockSpec((tq, D_MEM), lambda i: (i, 0))
    return pl.pallas_call(
        body, name="mem_fwd", grid=(S // tq,),
        in_specs=[blk, pl.BlockSpec((M, 2 * D_MEM), lambda i: (0, 0)), blk], out_specs=[blk, blk],
        out_shape=[jax.ShapeDtypeStruct((S, D_MEM), F32), jax.ShapeDtypeStruct((S, D_MEM), BF)],
        compiler_params=_params(("parallel",)),
    )(q, mkv, g)


def _mem_bwd(q, mkv, g, o, dy):
    S = q.shape[0]
    M = mkv.shape[0]
    tq = 256

    def body(q_ref, mkv_ref, g_ref, o_ref, dy_ref, dq_ref, dg_ref, dmkv_ref):
        @pl.when(pl.program_id(0) == 0)
        def _():
            dmkv_ref[...] = jnp.zeros_like(dmkv_ref)

        for h in range(MEM_HEADS):
            c0 = h * MEM_HD
            qh = q_ref[:, c0:c0 + MEM_HD]
            mk = mkv_ref[:, c0:c0 + MEM_HD]
            mv = mkv_ref[:, D_MEM + c0:D_MEM + c0 + MEM_HD]
            gv = g_ref[:, c0:c0 + MEM_HD]
            sg = _sigmoid(gv)
            dyv = dy_ref[:, c0:c0 + MEM_HD]
            dg_ref[:, c0:c0 + MEM_HD] = (dyv * o_ref[:, c0:c0 + MEM_HD] * (sg * (1.0 + gv * (1.0 - sg)))).astype(BF)
            do = (dyv * (gv * sg)).astype(BF)
            pr = _mem_probs(qh, mk)
            dp = _dot(do, mv, NT)
            ds = pr * (dp - jnp.sum(pr * dp, axis=-1, keepdims=True))
            dsb = (ds * (MEM_HD ** -0.5)).astype(BF)
            dq_ref[:, c0:c0 + MEM_HD] = _dot(dsb, mk, NN).astype(BF)
            dmkv_ref[:, c0:c0 + MEM_HD] += _dot(dsb, qh, TN)
            dmkv_ref[:, D_MEM + c0:D_MEM + c0 + MEM_HD] += _dot(pr.astype(BF), do, TN)

    blk = pl.BlockSpec((tq, D_MEM), lambda i: (i, 0))
    whole = pl.BlockSpec((M, 2 * D_MEM), lambda i: (0, 0))
    return pl.pallas_call(
        body, name="mem_bwd", grid=(S // tq,),
        in_specs=[blk, whole, blk, blk, blk], out_specs=[blk, blk, whole],
        out_shape=[jax.ShapeDtypeStruct((S, D_MEM), BF), jax.ShapeDtypeStruct((S, D_MEM), BF),
                   jax.ShapeDtypeStruct((M, 2 * D_MEM), F32)],
        compiler_params=_params(("arbitrary",)),
    )(q, mkv, g, o, dy)


MERGE_TN = 512


def _merge_specs(tm):
    ytile = pl.BlockSpec((tm, 1024), lambda i, j: (i, 0))
    wblk = pl.BlockSpec((MERGE_TN, 1024), lambda i, j: (j, 0))
    gls = [pl.BlockSpec((None, tm, MERGE_TN), (lambda i, j, br=br: (br, i, j))) for br in range(3)]
    otile = pl.BlockSpec((tm, MERGE_TN), lambda i, j: (i, j))
    return ytile, wblk, gls, otile


def _merge_fwd(ys, ws, gl, tm):
    S = gl.shape[1]

    def body(y0, y1, y2, w0, w1, w2, g0, g1, g2, o_ref):
        acc = None
        for y_ref, w_ref, g_ref in ((y0, w0, g0), (y1, w1, g1), (y2, w2, g2)):
            term = _sigmoid(g_ref[...]) * _dot(y_ref[...], w_ref[...], NT)
            acc = term if acc is None else acc + term
        o_ref[...] = acc.astype(BF)

    ytile, wblk, gls, otile = _merge_specs(tm)
    return pl.pallas_call(
        body, name="merge_fwd", grid=(S // tm, D_MODEL // MERGE_TN),
        in_specs=[ytile] * 3 + [wblk] * 3 + gls, out_specs=otile,
        out_shape=jax.ShapeDtypeStruct((S, D_MODEL), BF),
        compiler_params=_params(("parallel", "arbitrary")),
    )(*ys, *ws, gl, gl, gl)


def _merge_bwd(dout, w_out, ys, ws, gl, tm):
    S = gl.shape[1]

    def body(do_ref, wo_ref, y0, y1, y2, w0, w1, w2, g0, g1, g2, dg0, dg1, dg2, dp0, dp1, dp2):
        dm = _dot(do_ref[...], wo_ref[...], NT)
        for y_ref, w_ref, g_ref, dg_ref, dp_ref in ((y0, w0, g0, dg0, dp0), (y1, w1, g1, dg1, dp1), (y2, w2, g2, dg2, dp2)):
            gate = _sigmoid(g_ref[...])
            pv = _dot(y_ref[...], w_ref[...], NT)
            dg_ref[...] = (dm * pv * gate * (1.0 - gate)).astype(BF)
            dp_ref[...] = (dm * gate).astype(BF)

    ytile, wblk, gls, otile = _merge_specs(tm)
    out = jax.ShapeDtypeStruct((S, D_MODEL), BF)
    return pl.pallas_call(
        body, name="merge_bwd", grid=(S // tm, D_MODEL // MERGE_TN),
        in_specs=[pl.BlockSpec((tm, D_MODEL), lambda i, j: (i, 0)), pl.BlockSpec((MERGE_TN, D_MODEL), lambda i, j: (j, 0))]
        + [ytile] * 3 + [wblk] * 3 + gls,
        out_specs=[otile] * 6, out_shape=[out] * 6,
        compiler_params=_params(("parallel", "arbitrary")),
    )(dout, w_out, *ys, *ws, gl, gl, gl)


def _out_loss(merged, w_out, x, target, post_g, tm):
    S = x.shape[0]

    def body(m_ref, w_ref, x_ref, t_ref, g_ref, dout_ref, dy_ref, loss_ref, dpost_ref):
        @pl.when(pl.program_id(0) == 0)
        def _():
            loss_ref[...] = jnp.zeros_like(loss_ref)
            dpost_ref[...] = jnp.zeros_like(dpost_ref)

        out = _dot(m_ref[...], w_ref[...], NN)
        r = lax.rsqrt(jnp.mean(out * out, axis=-1, keepdims=True) + EPS)
        nrm = out * r
        gv = g_ref[...]
        err = (x_ref[...] + nrm * gv) - t_ref[...]
        sq = jnp.sum(jnp.sum(err * err, axis=1, keepdims=True), axis=0, keepdims=True)
        loss_ref[...] += sq * (0.5 / D_MODEL)
        dy = err * (1.0 / D_MODEL)
        dy_ref[...] = dy
        dpost_ref[...] += jnp.sum(dy * nrm, axis=0, keepdims=True)
        dn = dy * gv
        dout_ref[...] = (r * (dn - nrm * jnp.mean(dn * nrm, axis=-1, keepdims=True))).astype(BF)

    row = pl.BlockSpec((tm, D_MODEL), lambda i: (i, 0))
    return pl.pallas_call(
        body, name="out_loss", grid=(S // tm,),
        in_specs=[row, pl.BlockSpec((D_MODEL, D_MODEL), lambda i: (0, 0)), row, row, pl.BlockSpec((1, D_MODEL), lambda i: (0, 0))],
        out_specs=[row, row, pl.BlockSpec((8, 128), lambda i: (0, 0)), pl.BlockSpec((1, D_MODEL), lambda i: (0, 0))],
        out_shape=[jax.ShapeDtypeStruct((S, D_MODEL), BF), jax.ShapeDtypeStruct((S, D_MODEL), F32),
                   jax.ShapeDtypeStruct((8, 128), F32), jax.ShapeDtypeStruct((1, D_MODEL), F32)],
        compiler_params=_params(("arbitrary",)),
    )(merged, w_out, x, target, post_g)


def _dh_dx(dproj, w_in, x, dy, pre_g, tm):
    S = x.shape[0]
    nk, tk = dproj.shape[0], dproj.shape[2]

    def body(dp_ref, w_ref, x_ref, dy_ref, g_ref, dx_ref, dpre_ref, acc_ref):
        i, k = pl.program_id(0), pl.program_id(1)

        @pl.when(jnp.logical_and(i == 0, k == 0))
        def _():
            dpre_ref[...] = jnp.zeros_like(dpre_ref)

        @pl.when(k == 0)
        def _():
            acc_ref[...] = jnp.zeros_like(acc_ref)

        acc_ref[...] += _dot(dp_ref[...], w_ref[...], NN)

        @pl.when(k == nk - 1)
        def _():
            dh = acc_ref[...]
            xv = x_ref[...]
            r = lax.rsqrt(jnp.mean(xv * xv, axis=-1, keepdims=True) + EPS)
            nrm = xv * r
            dpre_ref[...] += jnp.sum(dh * nrm, axis=0, keepdims=True)
            dn = dh * g_ref[...]
            dx_ref[...] = r * (dn - nrm * jnp.mean(dn * nrm, axis=-1, keepdims=True)) + dy_ref[...]

    row = pl.BlockSpec((tm, D_MODEL), lambda i, k: (i, 0))
    vec = pl.BlockSpec((1, D_MODEL), lambda i, k: (0, 0))
    return pl.pallas_call(
        body, name="dh_dx", grid=(S // tm, nk),
        in_specs=[pl.BlockSpec((None, tm, tk), lambda i, k: (k, i, 0)), pl.BlockSpec((tk, D_MODEL), lambda i, k: (k, 0)), row, row, vec],
        out_specs=[row, vec],
        out_shape=[jax.ShapeDtypeStruct((S, D_MODEL), F32), jax.ShapeDtypeStruct((1, D_MODEL), F32)],
        scratch_shapes=[pltpu.VMEM((tm, D_MODEL), F32)],
        compiler_params=_params(("arbitrary", "arbitrary"), 56),
    )(dproj, w_in, x, dy, pre_g)


def _sum_parts(parts, name):
    P, R, C = parts.shape
    tr = max(t for t in range(8, 513, 8) if R % t == 0)

    def body(p_ref, o_ref):
        acc = p_ref[0]
        for j in range(1, P):
            acc = acc + p_ref[j]
        o_ref[...] = acc

    return pl.pallas_call(
        body, name=name, grid=(R // tr,),
        in_specs=[pl.BlockSpec((P, tr, C), lambda i: (0, i, 0))], out_specs=pl.BlockSpec((tr, C), lambda i: (i, 0)),
        out_shape=jax.ShapeDtypeStruct((R, C), F32), compiler_params=_params(("parallel",)),
    )(parts)


def _adamw(parts, w, m, v, name, own=None, chip=None):
    groups = list(parts) if isinstance(parts, (list, tuple)) else [parts]
    owns = [] if own is None else (list(own) if isinstance(own, (list, tuple)) else [own])
    P, rows, C = groups[0].shape
    R = rows * len(groups)
    tr = 128 if rows % 128 == 0 else rows
    per = rows // tr
    c1 = 1.0 - ADAM_B1 ** ADAM_STEP
    c2 = 1.0 - ADAM_B2 ** ADAM_STEP
    n_in = len(groups) * (4 if owns else 1)

    def body(*refs):
        if owns:
            refs = refs[1:]
        p_refs = refs[:n_in]
        w_ref, m_ref, v_ref, g_ref, d_ref, nm_ref, nv_ref = refs[n_in:]
        g = None
        for q in range(len(groups)):
            if owns:
                gq = p_refs[4 * q + 3][...].astype(F32)
                for j in range(3):
                    gq = gq + p_refs[4 * q + j][...].astype(F32)
            else:
                gq = p_refs[q][0].astype(F32)
                for j in range(1, P):
                    gq = gq + p_refs[q][j].astype(F32)
            g = gq if g is None else jnp.where(pl.program_id(0) // per == q, gq, g)
        nm = ADAM_B1 * m_ref[...] + (1.0 - ADAM_B1) * g
        nv = ADAM_B2 * v_ref[...] + (1.0 - ADAM_B2) * (g * g)
        g_ref[...] = g
        nm_ref[...] = nm
        nv_ref[...] = nv
        d_ref[...] = -ADAM_LR * ((nm / c1) / (jnp.sqrt(nv / c2) + ADAM_EPS) + ADAM_WD * w_ref[...])

    out = jax.ShapeDtypeStruct((R, C), F32)
    if not owns:
        tile = pl.BlockSpec((tr, C), lambda i: (i, 0))
        return pl.pallas_call(
            body, name=name, grid=(R // tr,),
            in_specs=[pl.BlockSpec((P, tr, C), (lambda i, q=q: (0, jnp.clip(i - q * per, 0, per - 1), 0))) for q in range(len(groups))]
            + [tile, tile, tile], out_specs=[tile] * 4, out_shape=[out] * 4,
            compiler_params=_params(("parallel",)),
        )(*groups, w, m, v)
    tile = pl.BlockSpec((tr, C), lambda i, c_ref: (i, 0))
    specs, operands = [], []
    for q in range(len(groups)):
        for k in range(3):
            specs.append(pl.BlockSpec((None, tr, C), (lambda i, c_ref, q=q, k=k: (k + (c_ref[0] <= k).astype(jnp.int32),
                                                                                  jnp.clip(i - q * per, 0, per - 1), 0))))
            operands.append(groups[q])
        specs.append(pl.BlockSpec((None, tr, C), (lambda i, c_ref, q=q: (c_ref[0], jnp.clip(i - q * per, 0, per - 1), 0))))
        operands.append(owns[q])
    return pl.pallas_call(
        body, name=name,
        grid_spec=pltpu.PrefetchScalarGridSpec(num_scalar_prefetch=1, grid=(R // tr,), in_specs=specs + [tile, tile, tile],
                                               out_specs=[tile] * 4),
        out_shape=[out] * 4, compiler_params=_params(("parallel",)),
    )(chip, *operands, w, m, v)


def _project(h, w_t):
    S = h.shape[0]
    n_tiles = D_IN // SEG_TILE
    ranges = [(c0 // SEG_TILE, (c0 + width) // SEG_TILE) for _, c0, width, _ in SEGMENTS]

    def body(h_ref, w_ref, *outs):
        j = pl.program_id(0)
        prod = _dot(h_ref[...], w_ref[...], NT)
        for (j0, j1), (_, _, _, dt), o_ref in zip(ranges, SEGMENTS, outs):
            @pl.when(jnp.logical_and(j >= j0, j < j1))
            def _(o_ref=o_ref, dt=dt):
                o_ref[...] = prod.astype(dt)

    out_shapes, out_specs = [], []
    for (j0, j1), (name, _, width, dt) in zip(ranges, SEGMENTS):
        if name == "gl":
            per = (j1 - j0) // 3
            out_shapes.append(jax.ShapeDtypeStruct((3, S, width // 3), dt))
            out_specs.append(pl.BlockSpec((None, S, SEG_TILE), (lambda j, j0=j0, j1=j1, per=per: (
                jnp.clip(j - j0, 0, j1 - j0 - 1) // per, 0, jnp.clip(j - j0, 0, j1 - j0 - 1) % per))))
        else:
            out_shapes.append(jax.ShapeDtypeStruct((S, width), dt))
            out_specs.append(pl.BlockSpec((S, SEG_TILE), (lambda j, j0=j0, j1=j1: (0, jnp.clip(j - j0, 0, j1 - j0 - 1)))))
    outs = pl.pallas_call(
        body, name="proj", grid=(n_tiles,),
        in_specs=[pl.BlockSpec((S, D_MODEL), lambda j: (0, 0)), pl.BlockSpec((SEG_TILE, D_MODEL), lambda j: (j, 0))],
        out_specs=out_specs, out_shape=out_shapes,
        compiler_params=_params(("arbitrary",), 56),
    )(h, w_t)
    return {name: o for (name, _, _, _), o in zip(SEGMENTS, outs)}


def _forward_a(x, mem, pre_g, mem_g, w_in, conv_w, conv_b, w_a, b_a, w_x, b_x, lam, sinks, rel_bias):
    S = x.shape[0]
    st = dict(T=min(512, S // 2), tm=min(512, S), bucket=_rel_bucket_map())
    st["h"] = _rms_fwd(x, pre_g, "pre_norm")
    st["memn"] = _rms_fwd(mem, mem_g, "mem_norm")
    seg = st["seg"] = _project(st["h"], w_in)
    st["h_rg"], st["y_rg"] = _rglru_fwd(seg["xr"], seg["g_rg"], conv_w, conv_b, w_a, b_a, w_x, b_x, lam, st["T"])
    st["o_swa"], st["y_swa"] = _swa_fwd(seg["q_s"], seg["kv"], seg["g_swa"], st["bucket"], rel_bias, _sink_column(sinks))
    return st


def _forward_b(st, x, target, post_g, w_memkv, wbr, w_out):
    S = x.shape[0]
    M = st["memn"].shape[0]
    seg = st["seg"]
    st["mkv"] = _matmul(st["memn"], w_memkv, "nn", M, 2 * D_MEM, D_MODEL, M, 512, D_MODEL, BF, "mem_kv")
    st["o_mem"], st["y_mem"] = _mem_fwd(seg["q_m"], st["mkv"], seg["g_mem"])
    st["ys"] = (st["y_rg"], st["y_swa"], st["y_mem"])
    st["merged"] = _merge_fwd(st["ys"], wbr, seg["gl"], st["tm"])
    st["dout"], st["dy"], st["loss"], st["dpost"] = _out_loss(st["merged"], w_out, x, target, post_g, min(256, S))
    return st


def _backward_a1(st, wbr, w_out):
    S = st["h"].shape[0]
    seg, ys, tm = st["seg"], st["ys"], st["tm"]
    st["dw_out"] = _matmul(st["merged"], st["dout"], "tn", D_MODEL, D_MODEL, S, 256, D_MODEL, S, BF, "dw_out", out_blocked="row")
    dgl0, dgl1, dgl2, dp0, dp1, dp2 = _merge_bwd(st["dout"], w_out, ys, wbr, seg["gl"], tm)
    st["dgl"] = (dgl0, dgl1, dgl2)
    dys, dwbr = [], []
    for i, dp in enumerate((dp0, dp1, dp2)):
        dys.append(_matmul(dp, wbr[i], "nn", S, 1024, D_MODEL, tm, 1024, D_MODEL, F32, "dy_br%d" % i))
        dwbr.append(_matmul(ys[i], dp, "tn", 1024, D_MODEL, S, 1024, 256, S, BF, "dw_br%d" % i, out_blocked="col"))
    st["dys"], st["dwbr"] = dys, dwbr
    return st


def _backward_a2(st, mem, w_memkv, conv_w, conv_b, w_a, b_a, w_x, b_x, lam):
    M = mem.shape[0]
    seg, dys = st["seg"], st["dys"]
    st["dq_m"], st["dg_mem"], dmkv = _mem_bwd(seg["q_m"], st["mkv"], seg["g_mem"], st["o_mem"], dys[2])
    dmkv_b = dmkv.astype(BF)
    st["dw_memkv"] = _matmul(st["memn"], dmkv_b, "tn", D_MODEL, 2 * D_MEM, M, 256, 2 * D_MEM, M, BF, "dw_memkv", out_blocked="row")
    dmemn = _matmul(dmkv_b, w_memkv, "nt", M, D_MODEL, 2 * D_MEM, M, 512, 2 * D_MEM, F32, "dmemn")
    st["dmem_g"] = _rms_gain_grad(dmemn, mem, "dmem_gain")
    st["dxr"], st["dg_rg"], st["dw_a"], st["dw_x"], st["dvec"] = _rglru_bwd(
        seg["xr"], seg["g_rg"], st["h_rg"], dys[0], conv_w, conv_b, w_a, b_a, w_x, b_x, lam, st["T"])
    return st


def _backward_b(st, rel_bias, sinks):
    seg = st["seg"]
    dq_s, dg_swa, dkv, dsinks, drel = _swa_bwd(seg["q_s"], seg["kv"], seg["g_swa"], st["o_swa"], st["dys"][1],
                                               st["bucket"], rel_bias, _sink_column(sinks))
    st["dsinks"], st["drel"] = dsinks.reshape(1, SWA_HEADS), drel.reshape(REL_BUCKETS, SWA_HEADS)
    dproj = jnp.concatenate([st["dxr"], st["dg_rg"], dq_s, dkv.astype(BF), dg_swa, st["dq_m"], st["dg_mem"], *st["dgl"]], axis=1)
    st["dproj"] = jnp.transpose(dproj.reshape(dproj.shape[0], N_DEV, D_IN // N_DEV), (1, 0, 2))
    return st


def _dw_in_half(st, half, dep=None):
    S = st["h"].shape[0]
    return _matmul(st["h"], st["dproj"], "tn", D_MODEL // 2, D_IN, S, 512, D_IN // N_DEV, S, BF, "dw_in%d" % half, a_moff=2 * half,
                   b_blocked=True, out_blocked="col", dep=dep)


def _owner_blocks(a):
    return jnp.swapaxes(a.reshape((4, 2) + a.shape[1:]), 0, 1)


def _local_step(x, mem, target, pre_g, post_g, mem_g, w_in, conv_w, conv_b, w_a, b_a, w_x, b_x, lam, sinks, rel_bias,
                w_memkv, wbr, w_out):
    st = _forward_a(x, mem, pre_g, mem_g, w_in, conv_w, conv_b, w_a, b_a, w_x, b_x, lam, sinks, rel_bias)
    st = _forward_b(st, x, target, post_g, w_memkv, wbr, w_out)
    st = _backward_a1(st, wbr, w_out)
    st = _backward_a2(st, mem, w_memkv, conv_w, conv_b, w_a, b_a, w_x, b_x, lam)
    st = _backward_b(st, rel_bias, sinks)
    st["dw_in"] = [_dw_in_half(st, 0), _dw_in_half(st, 1)]
    st["grad_x"], st["dpre"] = _dh_dx(st["dproj"], w_in, x, st["dy"], pre_g, st["tm"])
    return st


def _pad_rows(a, rows):
    a = a.reshape(-1, 128) if a.shape[-1] % 128 == 0 else jnp.pad(a, ((0, 0), (0, 128 - a.shape[-1])))
    return jnp.pad(a, ((0, rows - a.shape[0]), (0, 0))) if a.shape[0] < rows else a


def kernel(x, mem, pre_norm_g, post_norm_g, mem_norm_g, w_in, conv_w, conv_b, w_rg_a, b_rg_a, w_rg_x, b_rg_x, lru_lambda, swa_sinks, rel_bias, w_mem_kv, w_br_rg, w_br_swa, w_br_mem, w_out, loss_target, m_pre_norm_g, m_post_norm_g, m_mem_norm_g, m_w_in, m_conv_w, m_conv_b, m_w_rg_a, m_b_rg_a, m_w_rg_x, m_b_rg_x, m_lru_lambda, m_swa_sinks, m_rel_bias, m_w_mem_kv, m_w_br_rg, m_w_br_swa, m_w_br_mem, m_w_out, v_pre_norm_g, v_post_norm_g, v_mem_norm_g, v_w_in, v_conv_w, v_conv_b, v_w_rg_a, v_b_rg_a, v_w_rg_x, v_b_rg_x, v_lru_lambda, v_swa_sinks, v_rel_bias, v_w_mem_kv, v_w_br_rg, v_w_br_swa, v_w_br_mem, v_w_out):
    cx, cy, cc = lax.axis_index("x"), lax.axis_index("y"), lax.axis_index("c")
    me = 4 * cx + 2 * cy + cc
    chip = 2 * cx + cy
    core = jnp.reshape(cc, (1,)).astype(jnp.int32)
    x0, mem0 = x[0], mem[0]
    w_a_b, w_x_b = w_rg_a[0].astype(BF), w_rg_x[0].astype(BF)

    def landing(own, slot, slots):
        return lax.dynamic_update_slice(lax.empty((slots,) + own.shape, own.dtype), own[None], (slot,) + (0,) * own.ndim)


    def swap_start(parts, tag):
        return _exchange_start(parts, [lax.empty(p.shape[1:], p.dtype) for p in parts], _plan_swap(len(parts)), "swap_%s_start" % tag)

    def scatter_start(swap, after, tag, prefill=()):
        s_send, s_recv, parts, got, _ = swap
        got = _exchange_wait(s_send, s_recv, parts, got, _plan_swap(len(parts)), after, "swap_%s_wait" % tag)
        sums = [_pair_sum(p, g, core, "scatter_%s_sum%d" % (tag, i)) for i, (p, g) in enumerate(zip(parts, got))]
        lands = [landing(lax.dynamic_index_in_dim(s, chip, 0, keepdims=False), chip, 4) if i in prefill
                 else lax.empty(s.shape, s.dtype) for i, s in enumerate(sums)]
        return _exchange_start(sums, lands, _plan_scatter(len(sums)), "scatter_%s_start" % tag)

    def zero_after(a):
        return jnp.minimum(jnp.abs(a.reshape(-1)[0].astype(F32)), 0.0)

    g_in, g_cw = _all_gather_relayed([jnp.transpose(w_in[0]).astype(BF), conv_w[0]], [True, False], "gather_w_in")
    w_in_f = g_in.reshape(D_IN, D_MODEL)
    conv_w_f = jnp.transpose(g_cw, (1, 0, 2)).reshape(CONV_W, D_RNN)

    after_first = zero_after(g_cw).astype(BF)
    rest = [w.astype(BF) + after_first for w in (w_mem_kv[0], jnp.transpose(w_br_rg[0]), jnp.transpose(w_br_swa[0]),
                                                 jnp.transpose(w_br_mem[0]), w_out[0])]
    kinds = ["lead"] * len(rest)
    plan_g = _plan_gather(kinds)
    zones = _place_own([lax.empty((N_DEV,) + w.shape, w.dtype) for w in rest], rest, jnp.reshape(me, (1,)).astype(jnp.int32), "gather_rest_own")
    g_send, g_recv, g_src, g_land, g_token = _exchange_start(rest, zones, plan_g, "gather_rest_start")
    st = _forward_a(x0, mem0, pre_norm_g + g_token[0:1, 0:1], mem_norm_g, w_in_f, conv_w_f, conv_b, w_a_b, b_rg_a, w_x_b, b_rg_x,
                    lru_lambda, swa_sinks, rel_bias)
    g_land = _exchange_wait(g_send, g_recv, g_src, g_land, plan_g, st["y_swa"], "gather_rest_wait")
    g_land = _forward_to_sibling(g_land, kinds, "gather_rest_forward")
    w_memkv_f = g_land[0].reshape(D_MODEL, 2 * D_MEM)
    wbr = tuple(g_land[i].reshape(D_MODEL, D_RNN) for i in (1, 2, 3))
    w_out_f = g_land[4].reshape(D_MODEL, D_MODEL)

    st = _forward_b(st, x0, loss_target[0], post_norm_g, w_memkv_f, wbr, w_out_f)
    st = _backward_a1(st, wbr, w_out_f)
    parts_a = [st["dw_out"], st["dwbr"][0], st["dwbr"][1], st["dwbr"][2]]
    plan_a = _plan_scatter(len(parts_a))
    swap_a = swap_start(parts_a, "a")
    st = _backward_a2(st, mem0, w_memkv_f, conv_w_f, conv_b + swap_a[4][0:1, 0:1], w_a_b, b_rg_a, w_x_b, b_rg_x, lru_lambda)
    a_send, a_recv, a_src, a_land, a_token = scatter_start(swap_a, st["dxr"], "a")
    parts_c = [st["dw_memkv"], _owner_blocks(st["dw_a"]), _owner_blocks(st["dw_x"])]
    plan_c = _plan_scatter(len(parts_c))
    swap_c = swap_start(parts_c, "c")

    st = _backward_b(st, rel_bias, swa_sinks + swap_c[4][0:1, 0:1] + a_token[0:1, 0:1])
    c_send, c_recv, c_src, c_land, c_token = scatter_start(swap_c, st["dproj"], "c", prefill=(1, 2))
    plan_b = _plan_scatter(1)

    def dw_in_parts(half, dep):
        dwh = _dw_in_half(st, half, dep)
        return dwh, [dwh]

    dw0, parts_b0 = dw_in_parts(0, c_token)
    swap_b0 = swap_start(parts_b0, "b0")
    a_land = _exchange_wait(a_send, a_recv, a_src, a_land, plan_a, swap_b0[4], "scatter_a_wait")
    big = [None] * 6

    chip1 = jnp.reshape(chip, (1,)).astype(jnp.int32)

    def adamw_big(j, land, own, wt, mt, vt):
        big[j] = [a[None] for a in _adamw(land, wt[0], mt[0], vt[0], "adamw_big%d" % j, own=own, chip=chip1)]

    adamw_big(5, a_land[0], a_src[0], w_out, m_w_out, v_w_out)
    adamw_big(2, a_land[1], a_src[1], w_br_rg, m_w_br_rg, v_w_br_rg)
    adamw_big(3, a_land[2], a_src[2], w_br_swa, m_w_br_swa, v_w_br_swa)
    halves = [scatter_start(swap_b0, big[3][1], "b0")]
    dw1, parts_b1 = dw_in_parts(1, halves[0][4])
    swap_b1 = swap_start(parts_b1, "b1")
    c_land = _exchange_wait(c_send, c_recv, c_src, c_land, plan_c, swap_b1[4], "scatter_c_wait")
    g_wa_blk = _sum_parts(c_land[1], "sum_w_rg_a")
    g_wx_blk = _sum_parts(c_land[2], "sum_w_rg_x")
    adamw_big(4, a_land[3], a_src[3], w_br_mem, m_w_br_mem, v_w_br_mem)
    adamw_big(1, c_land[0], c_src[0], w_mem_kv, m_w_mem_kv, v_w_mem_kv)
    halves.append(scatter_start(swap_b1, big[1][1], "b1"))
    grad_x, dpre = _dh_dx(st["dproj"], w_in_f, x0, st["dy"], pre_norm_g + halves[1][4][0:1, 0:1], st["tm"])
    after, b_lands, b_sums = grad_x, [], []
    for half, (b_send, b_recv, b_src, b_land, _) in enumerate(halves):
        b_lands.append(_exchange_wait(b_send, b_recv, b_src, b_land, plan_b, after, "scatter_b%d_wait" % half)[0])
        b_sums.append(b_src[0])
        after = b_lands[-1]
    big[0] = [a[None] for a in _adamw(b_lands, w_in[0], m_w_in[0], v_w_in[0], "adamw_big0", own=b_sums, chip=chip1)]
    links_free = jnp.minimum(jnp.abs(big[0][0][0, 0, 0]), 0.0)

    pack = jnp.concatenate([dpre.reshape(16, 128), st["dpost"].reshape(16, 128), st["dmem_g"].reshape(16, 128),
                            st["dvec"].reshape(64, 128), _pad_rows(st["dsinks"], 8), _pad_rows(st["drel"], 32), g_wa_blk, g_wx_blk], axis=0) + links_free
    gathered = _all_gather([pack], "gather_small")[0]
    gs = _sum_parts(gathered, "sum_small")
    g_pre, g_post, g_memg = gs[0:16].reshape(1, D_MODEL), gs[16:32].reshape(1, D_MODEL), gs[32:48].reshape(1, D_MODEL)
    gvec = gs[48:112].reshape(8, D_RNN)
    g_conv_w = lax.dynamic_slice(gvec[0:CONV_W], (0, me * RNN_BLOCK), (CONV_W, RNN_BLOCK))
    g_conv_b, g_b_a, g_b_x, g_lam = gvec[4:5], gvec[5:6], gvec[6:7], gvec[7:8]
    g_sinks = gs[112:113, :SWA_HEADS]
    g_rel = gs[120:152, :SWA_HEADS]
    g_w_a = gathered[:, 152:280]
    g_w_x = gathered[:, 280:408]

    def packed(ts):
        pre, post, memg, cb, ba, bx, lm, wa, wx, sk, rel, cw = ts
        return jnp.concatenate([pre.reshape(16, 128), post.reshape(16, 128), memg.reshape(16, 128), cb.reshape(8, 128),
                                ba.reshape(8, 128), bx.reshape(8, 128), lm.reshape(8, 128), wa.reshape(1024, 128),
                                wx.reshape(1024, 128), _pad_rows(sk.reshape(1, SWA_HEADS), 8), _pad_rows(rel, 32),
                                _pad_rows(cw.reshape(CONV_W, RNN_BLOCK), 8)], axis=0)

    def unpacked(a):
        return (a[0:16].reshape(1, D_MODEL), a[16:32].reshape(1, D_MODEL), a[32:48].reshape(1, D_MODEL), a[48:56].reshape(1, D_RNN),
                a[56:64].reshape(1, D_RNN), a[64:72].reshape(1, D_RNN), a[72:80].reshape(1, D_RNN),
                a[80:1104].reshape(1, RNN_BLOCKS, RNN_BLOCK, RNN_BLOCK), a[1104:2128].reshape(1, RNN_BLOCKS, RNN_BLOCK, RNN_BLOCK),
                a[2128:2129, :SWA_HEADS], a[2136:2168, :SWA_HEADS], a[2168:2172].reshape(1, CONV_W, RNN_BLOCK))

    g_small = (g_pre, g_post, g_memg, g_conv_b, g_b_a, g_b_x, g_lam, g_w_a, g_w_x, g_sinks, g_rel, g_conv_w)
    w_small = (pre_norm_g, post_norm_g, mem_norm_g, conv_b, b_rg_a, b_rg_x, lru_lambda, w_rg_a, w_rg_x, swa_sinks, rel_bias, conv_w)
    m_small = (m_pre_norm_g, m_post_norm_g, m_mem_norm_g, m_conv_b, m_b_rg_a, m_b_rg_x, m_lru_lambda, m_w_rg_a, m_w_rg_x, m_swa_sinks, m_rel_bias, m_conv_w)
    v_small = (v_pre_norm_g, v_post_norm_g, v_mem_norm_g, v_conv_b, v_b_rg_a, v_b_rg_x, v_lru_lambda, v_w_rg_a, v_w_rg_x, v_swa_sinks, v_rel_bias, v_conv_w)
    sm = [unpacked(a) for a in _adamw(packed(g_small)[None], packed(w_small), packed(m_small), packed(v_small), "adamw_small")]


    loss_total = lax.psum(st["loss"][0, 0], AXES)

    def leaves(k):
        s = sm[k]
        return [s[0], s[1], s[2], big[0][k], s[11], s[3], s[7], s[4], s[8], s[5], s[6], s[9], s[10],
                big[1][k], big[2][k], big[3][k], big[4][k], big[5][k]]

    return (loss_total, grad_x[None], *leaves(0), *leaves(1), *leaves(2), *leaves(3))
```

```python
import math

import jax
import jax.numpy as jnp
import numpy as np
from jax import lax
from jax.experimental import pallas as pl
from jax.experimental.pallas import tpu as pltpu

F32, BF = jnp.float32, jnp.bfloat16
MESH = pl.DeviceIdType.MESH
AXES = ("x", "y", "c")
N_DEV = 8

D_MODEL = 2048
D_RNN = 1024
RNN_BLOCKS = 8
RNN_BLOCK = 128
CONV_W = 4
LRU_C = 8.0
SWA_HEADS = 16
SWA_KV_HEADS = 2
SWA_HD = 64
WINDOW = 128
MEM_HEADS = 4
MEM_HD = 256
D_MEM = 1024
REL_BUCKETS = 32
REL_MAX_DIST = 128
EPS = 1e-6
NEG_INF = -1e30
D_IN = 12544
SEGMENTS = (("xr", 0, 1024, F32), ("g_rg", 1024, 1024, F32), ("q_s", 2048, 1024, BF), ("kv", 3072, 256, BF),
            ("g_swa", 3328, 1024, F32), ("q_m", 4352, 1024, BF), ("g_mem", 5376, 1024, F32), ("gl", 6400, 6144, F32))
SEG_TILE = 256

ADAM_LR, ADAM_B1, ADAM_B2, ADAM_EPS, ADAM_WD, ADAM_STEP = 0.001, 0.9, 0.999, 1e-08, 0.01, 10

NN = (((1,), (0,)), ((), ()))
NT = (((1,), (1,)), ((), ()))
TN = (((0,), (0,)), ((), ()))
MIB = 2 ** 20


def _dot(a, b, dn):
    return lax.dot_general(a, b, dn, preferred_element_type=F32)


def _params(sem, vmem_mib=48):
    return pltpu.CompilerParams(dimension_semantics=sem, vmem_limit_bytes=vmem_mib * MIB)


def _sigmoid(z):
    return 1.0 / (1.0 + jnp.exp(-z))


def _softplus(z):
    return jnp.maximum(z, 0.0) + jnp.log(1.0 + jnp.exp(-jnp.abs(z)))


def _expm1(z):
    p = z * (1.0 + z * (0.5 + z * (1.0 / 6 + z * (1.0 / 24 + z * (1.0 / 120 + z * (1.0 / 720 + z * (1.0 / 5040 + z / 40320)))))))
    return jnp.where(jnp.abs(z) < 0.3, p, jnp.exp(z) - 1.0)


def _flat(p):
    return 4 * p[0] + 2 * p[1] + p[2]


def _all_gather(arrs, name):
    n = len(arrs)

    def body(*refs):
        ins, outs = refs[:n], refs[n:2 * n]
        send_sems, recv_sems, local_sems = refs[2 * n:]
        x, y, c = lax.axis_index("x"), lax.axis_index("y"), lax.axis_index("c")
        me, sibling = (x, y, c), (x, y, 1 - c)
        chips = [(1 - x, y), (x, 1 - y), (1 - x, 1 - y)]

        def copy(a, k, block, to, src=None):
            dst = outs[a].at[_flat(block)]
            return pltpu.make_async_remote_copy(src_ref=dst if src is None else src, dst_ref=dst,
                                                send_sem=send_sems.at[a * 7 + k], recv_sem=recv_sems.at[a * 7 + k],
                                                device_id=to, device_id_type=MESH)

        mine = [pltpu.make_async_copy(ins[a], outs[a].at[_flat(me)], local_sems.at[a]) for a in range(n)]
        for cp in mine:
            cp.start()
        first = []
        for a in range(n):
            first += [copy(a, 1 + j, me, (*chip, c), src=ins[a]) for j, chip in enumerate(chips)]
            first.append(copy(a, 0, me, sibling, src=ins[a]))
        for cp in first:
            cp.start()
        passed = []
        for j, chip in enumerate(chips):
            for a in range(n):
                copy(a, 1 + j, (*chip, c), me).wait_recv()
                fw = copy(a, 4 + j, (*chip, c), sibling)
                fw.start()
                passed.append(fw)
        for a in range(n):
            copy(a, 0, sibling, me).wait_recv()
            for j, chip in enumerate(chips):
                copy(a, 4 + j, (*chip, 1 - c), me).wait_recv()
        for cp in first + passed:
            cp.wait_send()
        for cp in mine:
            cp.wait()

    any_spec = pl.BlockSpec(memory_space=pl.ANY)
    return pl.pallas_call(
        body, name=name,
        out_shape=[jax.ShapeDtypeStruct((N_DEV,) + a.shape, a.dtype) for a in arrs],
        in_specs=[any_spec] * n, out_specs=[any_spec] * n,
        scratch_shapes=[pltpu.SemaphoreType.DMA((7 * n,)), pltpu.SemaphoreType.DMA((7 * n,)), pltpu.SemaphoreType.DMA((n,))],
    )(*arrs)


def _all_gather_relayed(arrs, relay, name):
    n = len(arrs)
    K = 9

    def body(*refs):
        ins, outs = refs[:n], refs[n:2 * n]
        send_sems, recv_sems, local_sems = refs[2 * n:]
        x, y, c = lax.axis_index("x"), lax.axis_index("y"), lax.axis_index("c")
        me, sib = (x, y, c), (x, y, 1 - c)
        xn, yn, dg = (1 - x, y, c), (x, 1 - y, c), (1 - x, 1 - y, c)

        def other(p):
            return (p[0], p[1], 1 - p[2])

        def rows(a, half):
            h = arrs[a].shape[0] // 2
            return pl.ds(half * h, h)

        def copy(a, k, block, to, half=None, src=None):
            dst = outs[a].at[_flat(block)]
            if half is not None:
                dst = dst.at[rows(a, half)]
            return pltpu.make_async_remote_copy(src_ref=dst if src is None else src, dst_ref=dst,
                                                send_sem=send_sems.at[a * K + k], recv_sem=recv_sems.at[a * K + k],
                                                device_id=to, device_id_type=MESH)

        mine = [pltpu.make_async_copy(ins[a], outs[a].at[_flat(me)], local_sems.at[a]) for a in range(n)]
        for cp in mine:
            cp.start()
        sends = []

        def start(cp):
            cp.start()
            sends.append(cp)

        for a in range(n):
            start(copy(a, 1, me, xn, src=ins[a]))
            start(copy(a, 2, me, yn, src=ins[a]))
            if not relay[a]:
                start(copy(a, 3, me, dg, src=ins[a]))
            start(copy(a, 0, me, sib, src=ins[a]))
        for a in range(n):
            copy(a, 1, xn, me).wait_recv()
            if relay[a]:
                start(copy(a, 3, xn, yn, half=0))
            start(copy(a, 5, xn, sib))
        for a in range(n):
            copy(a, 2, yn, me).wait_recv()
            if relay[a]:
                start(copy(a, 4, yn, xn, half=1))
            start(copy(a, 6, yn, sib))
        for a in range(n):
            if relay[a]:
                copy(a, 3, dg, me, half=0).wait_recv()
                start(copy(a, 7, dg, sib, half=0))
                copy(a, 4, dg, me, half=1).wait_recv()
                start(copy(a, 8, dg, sib, half=1))
            else:
                copy(a, 3, dg, me).wait_recv()
                start(copy(a, 7, dg, sib))
        for a in range(n):
            copy(a, 0, sib, me).wait_recv()
            copy(a, 5, other(xn), me).wait_recv()
            copy(a, 6, other(yn), me).wait_recv()
            if relay[a]:
                copy(a, 7, other(dg), me, half=0).wait_recv()
                copy(a, 8, other(dg), me, half=1).wait_recv()
            else:
                copy(a, 7, other(dg), me).wait_recv()
        for cp in sends:
            cp.wait_send()
        for cp in mine:
            cp.wait()

    any_spec = pl.BlockSpec(memory_space=pl.ANY)
    return pl.pallas_call(
        body, name=name,
        out_shape=[jax.ShapeDtypeStruct((N_DEV,) + a.shape, a.dtype) for a in arrs],
        in_specs=[any_spec] * n, out_specs=[any_spec] * n,
        scratch_shapes=[pltpu.SemaphoreType.DMA((K * n,)), pltpu.SemaphoreType.DMA((K * n,)), pltpu.SemaphoreType.DMA((n,))],
    )(*arrs)


def _chip_peers(x, y):
    return [(1 - x, y), (x, 1 - y), (1 - x, 1 - y)]


def _chip(p):
    return 2 * p[0] + p[1]


def _plan_gather(kinds):
    def plan(x, y, c):
        out = []
        for a, kind in enumerate(kinds):
            for peer in [(x, y, 1 - c)] + [(*ch, c) for ch in _chip_peers(x, y)]:
                out.append((a, None, (kind, _flat((x, y, c))), peer, (kind, _flat(peer))))
        return out
    return plan


def _plan_swap(n):
    def plan(x, y, c):
        return [(a, 1 - c, ("all", 0), (x, y, 1 - c), ("all", 0)) for a in range(n)]
    return plan


def _slot(ref, where):
    kind, k = where
    if kind == "all":
        return ref
    if kind == "lead":
        return ref.at[k]
    return ref.at[:, pl.ds(pl.multiple_of(k * 256, 256), 256)]


def _plan_scatter(n):
    def plan(x, y, c):
        out = []
        for a in range(n):
            for ch in _chip_peers(x, y):
                out.append((a, _chip(ch), ("lead", _chip((x, y))), (*ch, c), ("lead", _chip(ch))))
        return out
    return plan


HBM_SPEC = pl.BlockSpec(memory_space=pltpu.HBM)
SEM_SPEC = pl.BlockSpec(memory_space=pltpu.SEMAPHORE)


def _in_hbm(a):
    return pltpu.with_memory_space_constraint(a, pltpu.HBM)


def _exchange_start(srcs, lands, plan, name):
    n = len(srcs)
    count = len(plan(0, 0, 0))

    def body(*refs):
        src_refs, land_refs = refs[:n], refs[n:2 * n]
        send_sems, recv_sems = refs[2 * n], refs[2 * n + 1]
        token = refs[-1]
        x, y, c = lax.axis_index("x"), lax.axis_index("y"), lax.axis_index("c")
        for k, (a, si, di, peer, _) in enumerate(plan(x, y, c)):
            src = src_refs[a] if si is None else src_refs[a].at[si]
            pltpu.make_async_remote_copy(src_ref=src, dst_ref=_slot(land_refs[a], di), send_sem=send_sems.at[k],
                                         recv_sem=recv_sems.at[k], device_id=peer, device_id_type=MESH).start()
        token[...] = jnp.zeros_like(token)

    out = pl.pallas_call(
        body, name=name,
        out_shape=(pltpu.SemaphoreType.DMA((count,)), pltpu.SemaphoreType.DMA((count,)),
                   *[pltpu.HBM(a.shape, a.dtype) for a in lands], jax.ShapeDtypeStruct((8, 128), F32)),
        in_specs=[HBM_SPEC] * (2 * n),
        out_specs=(SEM_SPEC, SEM_SPEC, *([HBM_SPEC] * n), pl.BlockSpec(memory_space=pltpu.VMEM)),
        input_output_aliases={n + i: 2 + i for i in range(n)},
        compiler_params=pltpu.CompilerParams(has_side_effects=pltpu.SideEffectType.DATAFLOW_SIDE_EFFECTING),
    )(*[_in_hbm(a) for a in srcs], *[_in_hbm(a) for a in lands])
    return out[0], out[1], list(srcs), list(out[2:2 + n]), out[-1]


def _exchange_wait(send_sems, recv_sems, srcs, lands, plan, after, name):
    n = len(srcs)

    def body(*refs):
        src_refs, land_refs = refs[:n], refs[n:2 * n]
        send_sems, recv_sems = refs[2 * n], refs[2 * n + 1]
        x, y, c = lax.axis_index("x"), lax.axis_index("y"), lax.axis_index("c")
        for k, (a, si, _, peer, ri) in enumerate(plan(x, y, c)):
            src = src_refs[a] if si is None else src_refs[a].at[si]
            cp = pltpu.make_async_remote_copy(src_ref=src, dst_ref=_slot(land_refs[a], ri), send_sem=send_sems.at[k],
                                              recv_sem=recv_sems.at[k], device_id=peer, device_id_type=MESH)
            cp.wait_send()
            cp.wait_recv()

    out = pl.pallas_call(
        body, name=name,
        out_shape=tuple(pltpu.HBM(a.shape, a.dtype) for a in lands),
        in_specs=[HBM_SPEC] * (2 * n) + [SEM_SPEC, SEM_SPEC, pl.BlockSpec(memory_space=pl.ANY)],
        out_specs=tuple([HBM_SPEC] * n),
        input_output_aliases={n + i: i for i in range(n)},
        compiler_params=pltpu.CompilerParams(has_side_effects=pltpu.SideEffectType.DATAFLOW_SIDE_EFFECTING),
    )(*[_in_hbm(a) for a in srcs], *lands, send_sems, recv_sems, after)
    return list(out)


def _forward_to_sibling(lands, kinds, name):
    n = len(lands)

    def body(*refs):
        in_refs, out_refs = refs[:n], refs[n:2 * n]
        send_sems, recv_sems = refs[2 * n:]
        x, y, c = lax.axis_index("x"), lax.axis_index("y"), lax.axis_index("c")
        sibling = (x, y, 1 - c)

        def copy(a, j, slot):
            return pltpu.make_async_remote_copy(src_ref=_slot(in_refs[a], (kinds[a], slot)), dst_ref=_slot(out_refs[a], (kinds[a], slot)),
                                                send_sem=send_sems.at[a * 3 + j], recv_sem=recv_sems.at[a * 3 + j],
                                                device_id=sibling, device_id_type=MESH)

        sends = [copy(a, j, _flat((*ch, c))) for a in range(n) for j, ch in enumerate(_chip_peers(x, y))]
        for cp in sends:
            cp.start()
        for a in range(n):
            for j, ch in enumerate(_chip_peers(x, y)):
                copy(a, j, _flat((*ch, 1 - c))).wait_recv()
        for cp in sends:
            cp.wait_send()

    any_spec = pl.BlockSpec(memory_space=pl.ANY)
    return pl.pallas_call(
        body, name=name, out_shape=[jax.ShapeDtypeStruct(a.shape, a.dtype) for a in lands],
        in_specs=[any_spec] * n, out_specs=[any_spec] * n, input_output_aliases={a: a for a in range(n)},
        scratch_shapes=[pltpu.SemaphoreType.DMA((3 * n,)), pltpu.SemaphoreType.DMA((3 * n,))],
    )(*lands)


def _place_own(zones, owns, slot, name):
    n = len(zones)

    def body(slot_ref, *refs):
        for a in range(n):
            refs[2 * n + a][...] = refs[a][...]

    return pl.pallas_call(
        body, name=name,
        grid_spec=pltpu.PrefetchScalarGridSpec(
            num_scalar_prefetch=1, grid=(1,),
            in_specs=[pl.BlockSpec(o.shape, lambda i, s_ref: (0, 0)) for o in owns] + [pl.BlockSpec(memory_space=pl.ANY)] * n,
            out_specs=[pl.BlockSpec((None,) + o.shape, lambda i, s_ref: (s_ref[0], 0, 0)) for o in owns]),
        out_shape=[jax.ShapeDtypeStruct(z.shape, z.dtype) for z in zones],
        input_output_aliases={1 + n + a: a for a in range(n)},
        compiler_params=_params(("arbitrary",)),
    )(slot, *owns, *zones)


def _swap_with_sibling(parts, name):
    n = len(parts)

    def body(*refs):
        in_refs, out_refs = refs[:n], refs[n:2 * n]
        send_sems, recv_sems = refs[2 * n:]
        x, y, c = lax.axis_index("x"), lax.axis_index("y"), lax.axis_index("c")
        sends = [pltpu.make_async_remote_copy(src_ref=in_refs[a].at[1 - c], dst_ref=out_refs[a], send_sem=send_sems.at[a],
                                              recv_sem=recv_sems.at[a], device_id=(x, y, 1 - c), device_id_type=MESH)
                 for a in range(n)]
        for cp in sends:
            cp.start()
        for cp in sends:
            cp.wait()

    any_spec = pl.BlockSpec(memory_space=pl.ANY)
    return pl.pallas_call(
        body, name=name, out_shape=[jax.ShapeDtypeStruct(a.shape[1:], a.dtype) for a in parts],
        in_specs=[any_spec] * n, out_specs=[any_spec] * n,
        scratch_shapes=[pltpu.SemaphoreType.DMA((n,)), pltpu.SemaphoreType.DMA((n,))],
    )(*parts)


def _pair_sum(parts, got, core, name):
    _, _, R, C = parts.shape
    tr = 256 if R % 256 == 0 else R

    def body(c_ref, p_ref, g_ref, o_ref):
        o_ref[...] = (p_ref[...].astype(F32) + g_ref[...].astype(F32)).astype(o_ref.dtype)

    return pl.pallas_call(
        body, name=name,
        grid_spec=pltpu.PrefetchScalarGridSpec(
            num_scalar_prefetch=1, grid=(4, R // tr),
            in_specs=[pl.BlockSpec((None, None, tr, C), lambda j, i, c_ref: (c_ref[0], j, i, 0)),
                      pl.BlockSpec((None, tr, C), lambda j, i, c_ref: (j, i, 0))],
            out_specs=pl.BlockSpec((None, tr, C), lambda j, i, c_ref: (j, i, 0))),
        out_shape=jax.ShapeDtypeStruct((4, R, C), parts.dtype),
        compiler_params=_params(("parallel", "parallel")),
    )(core, parts, got)


def _matmul(a, b, mode, M, N, K, tm, tn, tk, out_dtype, name, b_noff=0, a_moff=0, a_koff=0, b_blocked=False, out_blocked=None,
            dep=None, addend=None, vmem_mib=48):
    nm, nn, nk = M // tm, N // tn, K // tk
    if mode == "nn":
        a_spec = pl.BlockSpec((tm, tk), lambda j, i, k: (i, k + a_koff))
        b_spec = pl.BlockSpec((tk, tn), lambda j, i, k: (k, j + b_noff))
        dn = NN
    elif mode == "nt":
        a_spec = pl.BlockSpec((tm, tk), lambda j, i, k: (i, k + a_koff))
        if b_blocked:
            b_spec = pl.BlockSpec((None, tn, tk), lambda j, i, k: (k, j, 0))
        else:
            b_spec = pl.BlockSpec((tn, tk), lambda j, i, k: (j + b_noff, k))
        dn = NT
    else:
        a_spec = pl.BlockSpec((tk, tm), lambda j, i, k: (k, i + a_moff))
        if b_blocked:
            b_spec = pl.BlockSpec((None, tk, tn), lambda j, i, k: (j, k, 0))
        else:
            b_spec = pl.BlockSpec((tk, tn), lambda j, i, k: (k, j + b_noff))
        dn = TN
    if out_blocked == "col":
        out_shape = jax.ShapeDtypeStruct((2, 4, M, tn), out_dtype)
        out_spec = pl.BlockSpec((None, None, tm, tn), lambda j, i, k: (j % 2, j // 2, i, 0))
    elif out_blocked == "row":
        out_shape = jax.ShapeDtypeStruct((2, 4, tm, N), out_dtype)
        out_spec = pl.BlockSpec((None, None, tm, tn), lambda j, i, k: (i % 2, i // 2, 0, j))
    elif out_blocked == "third":
        out_shape = jax.ShapeDtypeStruct((3, M, N // 3), out_dtype)
        out_spec = pl.BlockSpec((None, tm, tn), lambda j, i, k: (j // (nn // 3), i, j % (nn // 3)))
    else:
        out_shape = jax.ShapeDtypeStruct((M, N), out_dtype)
        out_spec = pl.BlockSpec((tm, tn), lambda j, i, k: (i, j))

    n_extra = (addend is not None) + (dep is not None)

    def body(a_ref, b_ref, *rest):
        o_ref, scratch = rest[n_extra], rest[n_extra + 1:]
        if nk == 1:
            prod = _dot(a_ref[...], b_ref[...], dn)
            if addend is not None:
                prod = prod + rest[0][...].astype(F32)
            o_ref[...] = prod.astype(out_dtype)
        else:
            assert addend is None
            acc_ref, = scratch
            k = pl.program_id(2)

            @pl.when(k == 0)
            def _():
                acc_ref[...] = jnp.zeros_like(acc_ref)

            acc_ref[...] += _dot(a_ref[...], b_ref[...], dn)

            @pl.when(k == nk - 1)
            def _():
                o_ref[...] = acc_ref[...].astype(out_dtype)

    return pl.pallas_call(
        body, name=name, grid=(nn, nm, nk),
        in_specs=[a_spec, b_spec] + ([] if addend is None else [out_spec])
        + ([] if dep is None else [pl.BlockSpec((8, 128), lambda j, i, k: (0, 0))]),
        out_specs=out_spec, out_shape=out_shape,
        scratch_shapes=[] if nk == 1 else [pltpu.VMEM((tm, tn), F32)],
        compiler_params=_params(("parallel", "parallel", "arbitrary"), vmem_mib),
    )(a, b, *([] if addend is None else [addend]), *([] if dep is None else [dep]))


def _rms_fwd(x, g, name):
    R, Dm = x.shape
    tr = min(R, 256)

    def body(x_ref, g_ref, h_ref):
        xv = x_ref[...]
        r = lax.rsqrt(jnp.mean(xv * xv, axis=-1, keepdims=True) + EPS)
        h_ref[...] = (xv * r * g_ref[...]).astype(BF)

    return pl.pallas_call(
        body, name=name, grid=(R // tr,),
        in_specs=[pl.BlockSpec((tr, Dm), lambda i: (i, 0)), pl.BlockSpec((1, Dm), lambda i: (0, 0))],
        out_specs=pl.BlockSpec((tr, Dm), lambda i: (i, 0)), out_shape=jax.ShapeDtypeStruct((R, Dm), BF),
        compiler_params=_params(("parallel",)),
    )(x, g)


def _rms_gain_grad(dn, x, name):
    R, Dm = x.shape

    def body(dn_ref, x_ref, o_ref):
        xv = x_ref[...]
        r = lax.rsqrt(jnp.mean(xv * xv, axis=-1, keepdims=True) + EPS)
        o_ref[...] = jnp.sum(dn_ref[...] * xv * r, axis=0, keepdims=True)

    return pl.pallas_call(
        body, name=name, out_shape=jax.ShapeDtypeStruct((1, Dm), F32),
        compiler_params=pltpu.CompilerParams(vmem_limit_bytes=32 * MIB),
    )(dn, x)


def _shift_down(v, k, head8, row, T):
    if k == 0:
        return v
    r = pltpu.roll(v, k, 0)
    hr = pltpu.roll(head8, k, 0)
    top = jnp.where(row[:8] < k, hr, r[:8])
    return jnp.concatenate([top, r[8:]], axis=0)


def _shift_up(v, k, tail8, row, T):
    if k == 0:
        return v
    r = pltpu.roll(v, T - k, 0)
    tr = pltpu.roll(tail8, 8 - k, 0)
    bot = jnp.where(row[:8] >= 8 - k, tr, r[T - 8:])
    return jnp.concatenate([r[:T - 8], bot], axis=0)


def _rglru_gates(u, head8, grow, row, T, cw_ref, cb_ref, wa_ref, ba_ref, wx_ref, bx_ref, lam_ref):
    us = [_shift_down(u, k, head8, row, T) for k in range(CONV_W)]
    acc = us[0] * cw_ref[0:1, :]
    for k in range(1, CONV_W):
        acc = acc + us[k] * cw_ref[k:k + 1, :]
    conv = cb_ref[...] + acc
    cbf = conv.astype(BF)
    r_ = _sigmoid(_dot(cbf, wa_ref[0], NN) + ba_ref[...])
    i_ = _sigmoid(_dot(cbf, wx_ref[0], NN) + bx_ref[...])
    sp = _softplus(-lam_ref[...])
    la = -LRU_C * r_ * sp
    a = jnp.exp(la)
    mult_raw = jnp.sqrt(-_expm1(2.0 * la))
    mult = jnp.where(grow == 0, 1.0, mult_raw)
    return us, conv, cbf, r_, i_, sp, a, mult_raw, mult


def _rglru_specs(T, nt, rev):
    tmap = (lambda n, t: (nt - 1 - t, n)) if rev else (lambda n, t: (t, n))
    hmap = ((lambda n, t: (jnp.maximum((nt - 1 - t) * (T // 8) - 1, 0), n)) if rev
            else (lambda n, t: (jnp.maximum(t * (T // 8) - 1, 0), n)))
    tile = pl.BlockSpec((T, RNN_BLOCK), tmap)
    halo = pl.BlockSpec((8, RNN_BLOCK), hmap)
    vec = pl.BlockSpec((1, RNN_BLOCK), lambda n, t: (0, n))
    cw = pl.BlockSpec((CONV_W, RNN_BLOCK), lambda n, t: (0, n))
    wblk = pl.BlockSpec((1, RNN_BLOCK, RNN_BLOCK), lambda n, t: (n, 0, 0))
    return tile, halo, vec, cw, wblk


def _rglru_fwd(xr, g, cw, cb, wa, ba, wx, bx, lam, T):
    S = xr.shape[0]
    nt = S // T

    def body(u_ref, uh_ref, g_ref, cw_ref, cb_ref, wa_ref, ba_ref, wx_ref, bx_ref, lam_ref, h_ref, y_ref, carry):
        t = pl.program_id(1)

        @pl.when(t == 0)
        def _():
            carry[...] = jnp.zeros_like(carry)

        row = lax.broadcasted_iota(jnp.int32, (T, RNN_BLOCK), 0)
        grow = row + t * T
        head8 = jnp.where(t > 0, uh_ref[...], 0.0)
        _, conv, _, _, i_, _, a, _, mult = _rglru_gates(u_ref[...], head8, grow, row, T, cw_ref, cb_ref, wa_ref, ba_ref,
                                                         wx_ref, bx_ref, lam_ref)
        b = mult * i_ * conv
        s = 1
        while s < T:
            keep = row >= s
            a_s = jnp.where(keep, pltpu.roll(a, s, 0), 1.0)
            b_s = jnp.where(keep, pltpu.roll(b, s, 0), 0.0)
            b = a * b_s + b
            a = a * a_s
            s *= 2
        h = b + a * carry[0:1, :]
        carry[...] = jnp.broadcast_to(h[T - 1:T, :], carry.shape)
        h_ref[...] = h
        gv = g_ref[...]
        y_ref[...] = (h * (gv * _sigmoid(gv))).astype(BF)

    tile, halo, vec, cwspec, wblk = _rglru_specs(T, nt, False)
    return pl.pallas_call(
        body, name="rglru_fwd", grid=(RNN_BLOCKS, nt),
        in_specs=[tile, halo, tile, cwspec, vec, wblk, vec, wblk, vec, vec],
        out_specs=[tile, tile],
        out_shape=[jax.ShapeDtypeStruct((S, D_RNN), F32), jax.ShapeDtypeStruct((S, D_RNN), BF)],
        scratch_shapes=[pltpu.VMEM((8, RNN_BLOCK), F32)],
        compiler_params=_params(("parallel", "arbitrary")),
    )(xr, xr, g, cw, cb, wa, ba, wx, bx, lam)


def _rglru_bwd(xr, g, h, dy, cw, cb, wa, ba, wx, bx, lam, T):
    S = xr.shape[0]
    nt = S // T

    def body(u_ref, uh_ref, g_ref, h_ref, hh_ref, dy_ref, cw_ref, cb_ref, wa_ref, ba_ref, wx_ref, bx_ref, lam_ref,
             du_ref, dg_ref, dwa_ref, dwx_ref, dvec_ref, c_dhh, c_a, c_dconv):
        t = pl.program_id(1)
        tt = nt - 1 - t

        @pl.when(t == 0)
        def _():
            c_dhh[...] = jnp.zeros_like(c_dhh)
            c_a[...] = jnp.zeros_like(c_a)
            c_dconv[...] = jnp.zeros_like(c_dconv)
            dwa_ref[...] = jnp.zeros_like(dwa_ref)
            dwx_ref[...] = jnp.zeros_like(dwx_ref)
            dvec_ref[...] = jnp.zeros_like(dvec_ref)

        row = lax.broadcasted_iota(jnp.int32, (T, RNN_BLOCK), 0)
        row8 = row[:8]
        grow = row + tt * T
        head8 = jnp.where(tt > 0, uh_ref[...], 0.0)
        us, conv, cbf, r_, i_, sp, a, mult_raw, mult = _rglru_gates(
            u_ref[...], head8, grow, row, T, cw_ref, cb_ref, wa_ref, ba_ref, wx_ref, bx_ref, lam_ref)
        hv = h_ref[...]
        hprev = _shift_down(hv, 1, jnp.where(tt > 0, hh_ref[...], 0.0), row, T)
        gv = g_ref[...]
        sg = _sigmoid(gv)
        dyv = dy_ref[...]
        dg_ref[...] = (dyv * hv * (sg * (1.0 + gv * (1.0 - sg)))).astype(BF)
        d = dyv * (gv * sg)
        A = _shift_up(a, 1, c_a[...], row, T)
        s = 1
        while s < T:
            keep = row < T - s
            A_s = jnp.where(keep, pltpu.roll(A, T - s, 0), 1.0)
            d_s = jnp.where(keep, pltpu.roll(d, T - s, 0), 0.0)
            d = A * d_s + d
            A = A * A_s
            s *= 2
        dhh = d + A * c_dhh[0:1, :]
        da = dhh * hprev
        dconv = dhh * mult * i_
        di = dhh * mult * conv
        dmult = dhh * i_ * conv
        dla = da * a - jnp.where(grow == 0, 0.0, dmult * (a * a) / mult_raw)
        dr = dla * (-LRU_C * sp)
        dsp = jnp.sum(dla * (-LRU_C * r_), axis=0, keepdims=True)
        dza = dr * r_ * (1.0 - r_)
        dzx = di * i_ * (1.0 - i_)
        dza_b, dzx_b = dza.astype(BF), dzx.astype(BF)
        dconv = dconv + _dot(dza_b, wa_ref[0], NT) + _dot(dzx_b, wx_ref[0], NT)
        dwa_ref[0] += _dot(cbf, dza_b, TN)
        dwx_ref[0] += _dot(cbf, dzx_b, TN)
        lam = lam_ref[...]
        rows = [jnp.sum(dconv * us[k], axis=0, keepdims=True) for k in range(CONV_W)]
        rows += [jnp.sum(dconv, axis=0, keepdims=True), jnp.sum(dza, axis=0, keepdims=True),
                 jnp.sum(dzx, axis=0, keepdims=True), dsp * (-_sigmoid(-lam))]
        upd = jnp.zeros((8, RNN_BLOCK), F32)
        for j, rv in enumerate(rows):
            upd = upd + jnp.where(row8 == j, rv, 0.0)
        dvec_ref[...] += upd
        tail8 = c_dconv[...]
        du = dconv * cw_ref[0:1, :]
        for k in range(1, CONV_W):
            du = du + _shift_up(dconv, k, tail8, row, T) * cw_ref[k:k + 1, :]
        du_ref[...] = du.astype(BF)
        c_dhh[...] = jnp.broadcast_to(dhh[0:1, :], c_dhh.shape)
        c_a[...] = jnp.broadcast_to(a[0:1, :], c_a.shape)
        c_dconv[...] = dconv[:8]

    tile, halo, vec, cwspec, wblk = _rglru_specs(T, nt, True)
    acc8 = pl.BlockSpec((8, RNN_BLOCK), lambda n, t: (0, n))
    return pl.pallas_call(
        body, name="rglru_bwd", grid=(RNN_BLOCKS, nt),
        in_specs=[tile, halo, tile, tile, halo, tile, cwspec, vec, wblk, vec, wblk, vec, vec],
        out_specs=[tile, tile, wblk, wblk, acc8],
        out_shape=[jax.ShapeDtypeStruct((S, D_RNN), BF), jax.ShapeDtypeStruct((S, D_RNN), BF),
                   jax.ShapeDtypeStruct((RNN_BLOCKS, RNN_BLOCK, RNN_BLOCK), F32),
                   jax.ShapeDtypeStruct((RNN_BLOCKS, RNN_BLOCK, RNN_BLOCK), F32),
                   jax.ShapeDtypeStruct((8, D_RNN), F32)],
        scratch_shapes=[pltpu.VMEM((8, RNN_BLOCK), F32)] * 3,
        compiler_params=_params(("parallel", "arbitrary")),
    )(xr, xr, g, h, h, dy, cw, cb, wa, ba, wx, bx, lam)


def _rel_bucket_map():
    qi = np.arange(WINDOW)[:, None]
    kj = np.arange(2 * WINDOW)[None, :]
    dist = jnp.asarray(qi + WINDOW - kj, jnp.int32)
    n = jnp.maximum(dist, 0)
    max_exact = REL_BUCKETS // 2
    ratio = jnp.log(jnp.maximum(n, 1).astype(F32) / max_exact) / math.log(REL_MAX_DIST / max_exact)
    large = jnp.minimum(max_exact + (ratio * (REL_BUCKETS - max_exact)).astype(jnp.int32), REL_BUCKETS - 1)
    bucket = jnp.where(n < max_exact, n, large).astype(jnp.int32)
    j = np.arange(WINDOW)[None, :]
    return jnp.where(jnp.asarray(j > qi), bucket[:, :WINDOW], bucket[:, WINDOW:])


def _swa_common(n, kv_ref, bucket_ref, relb_ref, bias_scr):
    @pl.when(n == 0)
    def _():
        bk = bucket_ref[...]
        for h in range(SWA_HEADS):
            acc = jnp.zeros((WINDOW, WINDOW), F32)
            for b in range(REL_BUCKETS):
                acc = acc + jnp.where(bk == b, relb_ref[b, h], 0.0)
            bias_scr[h] = acc

    prev0 = pl.multiple_of(jnp.maximum(n - 1, 0) * WINDOW, WINDOW)
    cur0 = pl.multiple_of(n * WINDOW, WINDOW)
    kk = jnp.concatenate([kv_ref[pl.ds(prev0, WINDOW), :], kv_ref[pl.ds(cur0, WINDOW), :]], axis=0).astype(F32)
    rowi = lax.broadcasted_iota(jnp.int32, (WINDOW, WINDOW), 0)
    col = lax.broadcasted_iota(jnp.int32, (WINDOW, WINDOW), 1)
    from_prev = col > rowi
    return kk, from_prev, prev0, cur0


def _fold(full, from_prev):
    return jnp.where(from_prev, full[:, :WINDOW], full[:, WINDOW:])


def _unfold(sq, from_prev):
    return jnp.concatenate([jnp.where(from_prev, sq, 0.0), jnp.where(from_prev, 0.0, sq)], axis=1)


def _half_pair(part, kvh):
    lo = lax.broadcasted_iota(jnp.int32, part.shape, 1) < SWA_HD
    if kvh == 0:
        pa = jnp.where(lo, part, 0.0)
        pb = pltpu.roll(pa, SWA_HD, 1)
    else:
        pb = jnp.where(lo, 0.0, part)
        pa = pltpu.roll(pb, SWA_HD, 1)
    return pa.astype(BF), pb.astype(BF)


ALL_HEADS = SWA_HEADS * WINDOW


def _sink_column(sinks):
    return jnp.repeat(sinks.reshape(SWA_HEADS), WINDOW).reshape(ALL_HEADS, 1)


def _swa_operands(kk):
    return [(_half_pair(kk[:, :128], kvh), _half_pair(kk[:, 128:], kvh)) for kvh in range(SWA_KV_HEADS)]


def _swa_probs(n, q_ref, ops, bias_scr, sinkc_ref, from_prev):
    lgs = []
    for kvh in range(SWA_KV_HEADS):
        (ka, kb), _ = ops[kvh]
        for p in range(4):
            q2 = q_ref[:, kvh * 512 + p * 128:kvh * 512 + p * 128 + 128]
            lgs += [_fold(_dot(q2, ka, NT), from_prev), _fold(_dot(q2, kb, NT), from_prev)]
    lg = jnp.concatenate(lgs, axis=0) * (SWA_HD ** -0.5) + bias_scr[...].reshape(ALL_HEADS, WINDOW)
    rowi = jnp.bitwise_and(lax.broadcasted_iota(jnp.int32, (ALL_HEADS, WINDOW), 0), WINDOW - 1)
    col = lax.broadcasted_iota(jnp.int32, (ALL_HEADS, WINDOW), 1)
    no_prev = jnp.where(n > 0, 0, 4 * WINDOW)
    lg = jnp.where(jnp.logical_or(col <= rowi, col > rowi + no_prev), lg, NEG_INF)
    sink = sinkc_ref[...]
    m = jnp.maximum(jnp.max(lg, axis=-1, keepdims=True), sink)
    e = jnp.exp(lg - m)
    es = jnp.exp(sink - m)
    den = jnp.sum(e, axis=-1, keepdims=True) + es
    return e / den, es / den


def _swa_fwd(q, kv, g, bucket, rel_bias, sink_col):
    S = q.shape[0]
    nb = S // WINDOW

    def body(q_ref, kv_ref, g_ref, bucket_ref, relb_ref, sinkc_ref, o_ref, y_ref, bias_scr):
        n = pl.program_id(0)
        kk, from_prev, _, _ = _swa_common(n, kv_ref, bucket_ref, relb_ref, bias_scr)
        ops = _swa_operands(kk)
        pr, _ = _swa_probs(n, q_ref, ops, bias_scr, sinkc_ref, from_prev)
        for kvh in range(SWA_KV_HEADS):
            _, (va, vb) = ops[kvh]
            for p in range(4):
                c0 = kvh * 512 + p * 128
                r0 = (kvh * 8 + 2 * p) * WINDOW
                o2 = (_dot(_unfold(pr[r0:r0 + WINDOW], from_prev).astype(BF), va, NN)
                      + _dot(_unfold(pr[r0 + WINDOW:r0 + 2 * WINDOW], from_prev).astype(BF), vb, NN))
                o_ref[:, c0:c0 + 128] = o2
                gv = g_ref[:, c0:c0 + 128]
                y_ref[:, c0:c0 + 128] = (o2 * (gv * _sigmoid(gv))).astype(BF)

    blk = pl.BlockSpec((WINDOW, 1024), lambda n: (n, 0))
    smem = pl.BlockSpec(memory_space=pltpu.SMEM)
    sinkc = pl.BlockSpec((ALL_HEADS, 1), lambda n: (0, 0))
    return pl.pallas_call(
        body, name="swa_fwd", grid=(nb,),
        in_specs=[blk, pl.BlockSpec((S, 256), lambda n: (0, 0)), blk, pl.BlockSpec((WINDOW, WINDOW), lambda n: (0, 0)), smem, sinkc],
        out_specs=[blk, blk],
        out_shape=[jax.ShapeDtypeStruct((S, 1024), F32), jax.ShapeDtypeStruct((S, 1024), BF)],
        scratch_shapes=[pltpu.VMEM((SWA_HEADS, WINDOW, WINDOW), F32)],
        compiler_params=_params(("arbitrary",)),
    )(q, kv, g, bucket, rel_bias, sink_col)


def _swa_bwd(q, kv, g, o, dy, bucket, rel_bias, sink_col):
    S = q.shape[0]
    nb = S // WINDOW

    def body(q_ref, kv_ref, g_ref, o_ref, dy_ref, bucket_ref, relb_ref, sinkc_ref,
             dq_ref, dg_ref, dkv_ref, dsink_ref, drel_ref, bias_scr, dbias_scr, dsink_scr):
        n = pl.program_id(0)

        @pl.when(n == 0)
        def _():
            dbias_scr[...] = jnp.zeros_like(dbias_scr)
            dsink_scr[...] = jnp.zeros_like(dsink_scr)
            dkv_ref[...] = jnp.zeros_like(dkv_ref)

        kk, from_prev, prev0, cur0 = _swa_common(n, kv_ref, bucket_ref, relb_ref, bias_scr)
        ops = _swa_operands(kk)
        pr, ps = _swa_probs(n, q_ref, ops, bias_scr, sinkc_ref, from_prev)
        do2s, dps = [], []
        for kvh in range(SWA_KV_HEADS):
            _, (va, vb) = ops[kvh]
            for p in range(4):
                c0 = kvh * 512 + p * 128
                gv = g_ref[:, c0:c0 + 128]
                sg = _sigmoid(gv)
                dyv = dy_ref[:, c0:c0 + 128]
                dg_ref[:, c0:c0 + 128] = (dyv * o_ref[:, c0:c0 + 128] * (sg * (1.0 + gv * (1.0 - sg)))).astype(BF)
                do2 = (dyv * (gv * sg)).astype(BF)
                do2s.append(do2)
                dps += [_fold(_dot(do2, va, NT), from_prev), _fold(_dot(do2, vb, NT), from_prev)]
        dp = jnp.concatenate(dps, axis=0)
        delta = jnp.sum(pr * dp, axis=-1, keepdims=True)
        ds = pr * (dp - delta)
        dbias_scr[...] += ds.reshape(SWA_HEADS, WINDOW, WINDOW)
        dsink_scr[...] += ps * delta
        dsc = ds * (SWA_HD ** -0.5)
        lo256 = lax.broadcasted_iota(jnp.int32, (2 * WINDOW, 128), 1) < SWA_HD
        dks, dvs = [], []
        for kvh in range(SWA_KV_HEADS):
            (ka, kb), _ = ops[kvh]
            dka = jnp.zeros((2 * WINDOW, 128), F32)
            dkb, dva, dvb = dka, dka, dka
            for p in range(4):
                c0 = kvh * 512 + p * 128
                r0 = (kvh * 8 + 2 * p) * WINDOW
                q2 = q_ref[:, c0:c0 + 128]
                do2 = do2s[kvh * 4 + p]
                ds0 = _unfold(dsc[r0:r0 + WINDOW], from_prev).astype(BF)
                ds1 = _unfold(dsc[r0 + WINDOW:r0 + 2 * WINDOW], from_prev).astype(BF)
                dq_ref[:, c0:c0 + 128] = (_dot(ds0, ka, NN) + _dot(ds1, kb, NN)).astype(BF)
                dka = dka + _dot(ds0, q2, TN)
                dkb = dkb + _dot(ds1, q2, TN)
                dva = dva + _dot(_unfold(pr[r0:r0 + WINDOW], from_prev).astype(BF), do2, TN)
                dvb = dvb + _dot(_unfold(pr[r0 + WINDOW:r0 + 2 * WINDOW], from_prev).astype(BF), do2, TN)
            dks.append(jnp.where(lo256, dka, 0.0) + pltpu.roll(jnp.where(lo256, 0.0, dkb), SWA_HD, 1))
            dvs.append(jnp.where(lo256, dva, 0.0) + pltpu.roll(jnp.where(lo256, 0.0, dvb), SWA_HD, 1))
        dk = dks[0] + pltpu.roll(dks[1], SWA_HD, 1)
        dv = dvs[0] + pltpu.roll(dvs[1], SWA_HD, 1)
        dkv_ref[pl.ds(prev0, WINDOW), 0:128] += dk[:WINDOW]
        dkv_ref[pl.ds(prev0, WINDOW), 128:256] += dv[:WINDOW]
        dkv_ref[pl.ds(cur0, WINDOW), 0:128] += dk[WINDOW:]
        dkv_ref[pl.ds(cur0, WINDOW), 128:256] += dv[WINDOW:]

        @pl.when(n == nb - 1)
        def _():
            dsink_ref[...] = -jnp.sum(dsink_scr[...].reshape(SWA_HEADS, WINDOW, 1), axis=1)
            bk = bucket_ref[...]
            sums = []
            for b in range(REL_BUCKETS):
                sums.append(jnp.sum(jnp.where((bk == b)[None], dbias_scr[...], 0.0), axis=1))
            drel_ref[...] = jnp.sum(jnp.concatenate(sums, axis=0), axis=1, keepdims=True)

    blk = pl.BlockSpec((WINDOW, 1024), lambda n: (n, 0))
    smem = pl.BlockSpec(memory_space=pltpu.SMEM)
    whole = lambda shape: pl.BlockSpec(shape, lambda n: (0, 0))
    return pl.pallas_call(
        body, name="swa_bwd", grid=(nb,),
        in_specs=[blk, whole((S, 256)), blk, blk, blk, whole((WINDOW, WINDOW)), smem, whole((ALL_HEADS, 1))],
        out_specs=[blk, blk, whole((S, 256)), whole((SWA_HEADS, 1)), whole((REL_BUCKETS * SWA_HEADS, 1))],
        out_shape=[jax.ShapeDtypeStruct((S, 1024), BF), jax.ShapeDtypeStruct((S, 1024), BF),
                   jax.ShapeDtypeStruct((S, 256), F32), jax.ShapeDtypeStruct((SWA_HEADS, 1), F32),
                   jax.ShapeDtypeStruct((REL_BUCKETS * SWA_HEADS, 1), F32)],
        scratch_shapes=[pltpu.VMEM((SWA_HEADS, WINDOW, WINDOW), F32), pltpu.VMEM((SWA_HEADS, WINDOW, WINDOW), F32),
                        pltpu.VMEM((ALL_HEADS, 1), F32)],
        compiler_params=_params(("arbitrary",)),
    )(q, kv, g, o, dy, bucket, rel_bias, sink_col)


def _mem_probs(qh, mk):
    lg = _dot(qh, mk, NT) * (MEM_HD ** -0.5)
    e = jnp.exp(lg - jnp.max(lg, axis=-1, keepdims=True))
    return e / jnp.sum(e, axis=-1, keepdims=True)


def _mem_fwd(q, mkv, g):
    S = q.shape[0]
    M = mkv.shape[0]
    tq = 256

    def body(q_ref, mkv_ref, g_ref, o_ref, y_ref):
        for h in range(MEM_HEADS):
            c0 = h * MEM_HD
            pr = _mem_probs(q_ref[:, c0:c0 + MEM_HD], mkv_ref[:, c0:c0 + MEM_HD])
            o = _dot(pr.astype(BF), mkv_ref[:, D_MEM + c0:D_MEM + c0 + MEM_HD], NN)
            o_ref[:, c0:c0 + MEM_HD] = o
            gv = g_ref[:, c0:c0 + MEM_HD]
            y_ref[:, c0:c0 + MEM_HD] = (o * (gv * _sigmoid(gv))).astype(BF)

    blk = pl.BlockSpec((tq, D_MEM), lambda i: (i, 0))
    return pl.pallas_call(
        body, name="mem_fwd", grid=(S // tq,),
        in_specs=[blk, pl.BlockSpec((M, 2 * D_MEM), lambda i: (0, 0)), blk], out_specs=[blk, blk],
        out_shape=[jax.ShapeDtypeStruct((S, D_MEM), F32), jax.ShapeDtypeStruct((S, D_MEM), BF)],
        compiler_params=_params(("parallel",)),
    )(q, mkv, g)


def _mem_bwd(q, mkv, g, o, dy):
    S = q.shape[0]
    M = mkv.shape[0]
    tq = 256

    def body(q_ref, mkv_ref, g_ref, o_ref, dy_ref, dq_ref, dg_ref, dmkv_ref):
        @pl.when(pl.program_id(0) == 0)
        def _():
            dmkv_ref[...] = jnp.zeros_like(dmkv_ref)

        for h in range(MEM_HEADS):
            c0 = h * MEM_HD
            qh = q_ref[:, c0:c0 + MEM_HD]
            mk = mkv_ref[:, c0:c0 + MEM_HD]
            mv = mkv_ref[:, D_MEM + c0:D_MEM + c0 + MEM_HD]
            gv = g_ref[:, c0:c0 + MEM_HD]
            sg = _sigmoid(gv)
            dyv = dy_ref[:, c0:c0 + MEM_HD]
            dg_ref[:, c0:c0 + MEM_HD] = (dyv * o_ref[:, c0:c0 + MEM_HD] * (sg * (1.0 + gv * (1.0 - sg)))).astype(BF)
            do = (dyv * (gv * sg)).astype(BF)
            pr = _mem_probs(qh, mk)
            dp = _dot(do, mv, NT)
            ds = pr * (dp - jnp.sum(pr * dp, axis=-1, keepdims=True))
            dsb = (ds * (MEM_HD ** -0.5)).astype(BF)
            dq_ref[:, c0:c0 + MEM_HD] = _dot(dsb, mk, NN).astype(BF)
            dmkv_ref[:, c0:c0 + MEM_HD] += _dot(dsb, qh, TN)
            dmkv_ref[:, D_MEM + c0:D_MEM + c0 + MEM_HD] += _dot(pr.astype(BF), do, TN)

    blk = pl.BlockSpec((tq, D_MEM), lambda i: (i, 0))
    whole = pl.BlockSpec((M, 2 * D_MEM), lambda i: (0, 0))
    return pl.pallas_call(
        body, name="mem_bwd", grid=(S // tq,),
        in_specs=[blk, whole, blk, blk, blk], out_specs=[blk, blk, whole],
        out_shape=[jax.ShapeDtypeStruct((S, D_MEM), BF), jax.ShapeDtypeStruct((S, D_MEM), BF),
                   jax.ShapeDtypeStruct((M, 2 * D_MEM), F32)],
        compiler_params=_params(("arbitrary",)),
    )(q, mkv, g, o, dy)


MERGE_TN = 512


def _merge_specs(tm):
    ytile = pl.BlockSpec((tm, 1024), lambda i, j: (i, 0))
    wblk = pl.BlockSpec((MERGE_TN, 1024), lambda i, j: (j, 0))
    gls = [pl.BlockSpec((None, tm, MERGE_TN), (lambda i, j, br=br: (br, i, j))) for br in range(3)]
    otile = pl.BlockSpec((tm, MERGE_TN), lambda i, j: (i, j))
    return ytile, wblk, gls, otile


def _merge_fwd(ys, ws, gl, tm):
    S = gl.shape[1]

    def body(y0, y1, y2, w0, w1, w2, g0, g1, g2, o_ref):
        acc = None
        for y_ref, w_ref, g_ref in ((y0, w0, g0), (y1, w1, g1), (y2, w2, g2)):
            term = _sigmoid(g_ref[...]) * _dot(y_ref[...], w_ref[...], NT)
            acc = term if acc is None else acc + term
        o_ref[...] = acc.astype(BF)

    ytile, wblk, gls, otile = _merge_specs(tm)
    return pl.pallas_call(
        body, name="merge_fwd", grid=(S // tm, D_MODEL // MERGE_TN),
        in_specs=[ytile] * 3 + [wblk] * 3 + gls, out_specs=otile,
        out_shape=jax.ShapeDtypeStruct((S, D_MODEL), BF),
        compiler_params=_params(("parallel", "arbitrary")),
    )(*ys, *ws, gl, gl, gl)


def _merge_bwd(dout, w_out, ys, ws, gl, tm):
    S = gl.shape[1]

    def body(do_ref, wo_ref, y0, y1, y2, w0, w1, w2, g0, g1, g2, dg0, dg1, dg2, dp0, dp1, dp2):
        dm = _dot(do_ref[...], wo_ref[...], NT)
        for y_ref, w_ref, g_ref, dg_ref, dp_ref in ((y0, w0, g0, dg0, dp0), (y1, w1, g1, dg1, dp1), (y2, w2, g2, dg2, dp2)):
            gate = _sigmoid(g_ref[...])
            pv = _dot(y_ref[...], w_ref[...], NT)
            dg_ref[...] = (dm * pv * gate * (1.0 - gate)).astype(BF)
            dp_ref[...] = (dm * gate).astype(BF)

    ytile, wblk, gls, otile = _merge_specs(tm)
    out = jax.ShapeDtypeStruct((S, D_MODEL), BF)
    return pl.pallas_call(
        body, name="merge_bwd", grid=(S // tm, D_MODEL // MERGE_TN),
        in_specs=[pl.BlockSpec((tm, D_MODEL), lambda i, j: (i, 0)), pl.BlockSpec((MERGE_TN, D_MODEL), lambda i, j: (j, 0))]
        + [ytile] * 3 + [wblk] * 3 + gls,
        out_specs=[otile] * 6, out_shape=[out] * 6,
        compiler_params=_params(("parallel", "arbitrary")),
    )(dout, w_out, *ys, *ws, gl, gl, gl)


def _out_loss(merged, w_out, x, target, post_g, tm):
    S = x.shape[0]

    def body(m_ref, w_ref, x_ref, t_ref, g_ref, dout_ref, dy_ref, loss_ref, dpost_ref):
        @pl.when(pl.program_id(0) == 0)
        def _():
            loss_ref[...] = jnp.zeros_like(loss_ref)
            dpost_ref[...] = jnp.zeros_like(dpost_ref)

        out = _dot(m_ref[...], w_ref[...], NN)
        r = lax.rsqrt(jnp.mean(out * out, axis=-1, keepdims=True) + EPS)
        nrm = out * r
        gv = g_ref[...]
        err = (x_ref[...] + nrm * gv) - t_ref[...]
        sq = jnp.sum(jnp.sum(err * err, axis=1, keepdims=True), axis=0, keepdims=True)
        loss_ref[...] += sq * (0.5 / D_MODEL)
        dy = err * (1.0 / D_MODEL)
        dy_ref[...] = dy
        dpost_ref[...] += jnp.sum(dy * nrm, axis=0, keepdims=True)
        dn = dy * gv
        dout_ref[...] = (r * (dn - nrm * jnp.mean(dn * nrm, axis=-1, keepdims=True))).astype(BF)

    row = pl.BlockSpec((tm, D_MODEL), lambda i: (i, 0))
    return pl.pallas_call(
        body, name="out_loss", grid=(S // tm,),
        in_specs=[row, pl.BlockSpec((D_MODEL, D_MODEL), lambda i: (0, 0)), row, row, pl.BlockSpec((1, D_MODEL), lambda i: (0, 0))],
        out_specs=[row, row, pl.BlockSpec((8, 128), lambda i: (0, 0)), pl.BlockSpec((1, D_MODEL), lambda i: (0, 0))],
        out_shape=[jax.ShapeDtypeStruct((S, D_MODEL), BF), jax.ShapeDtypeStruct((S, D_MODEL), F32),
                   jax.ShapeDtypeStruct((8, 128), F32), jax.ShapeDtypeStruct((1, D_MODEL), F32)],
        compiler_params=_params(("arbitrary",)),
    )(merged, w_out, x, target, post_g)


def _dh_dx(dproj, w_in, x, dy, pre_g, tm):
    S = x.shape[0]
    nk, tk = dproj.shape[0], dproj.shape[2]

    def body(dp_ref, w_ref, x_ref, dy_ref, g_ref, dx_ref, dpre_ref, acc_ref):
        i, k = pl.program_id(0), pl.program_id(1)

        @pl.when(jnp.logical_and(i == 0, k == 0))
        def _():
            dpre_ref[...] = jnp.zeros_like(dpre_ref)

        @pl.when(k == 0)
        def _():
            acc_ref[...] = jnp.zeros_like(acc_ref)

        acc_ref[...] += _dot(dp_ref[...], w_ref[...], NN)

        @pl.when(k == nk - 1)
        def _():
            dh = acc_ref[...]
            xv = x_ref[...]
            r = lax.rsqrt(jnp.mean(xv * xv, axis=-1, keepdims=True) + EPS)
            nrm = xv * r
            dpre_ref[...] += jnp.sum(dh * nrm, axis=0, keepdims=True)
            dn = dh * g_ref[...]
            dx_ref[...] = r * (dn - nrm * jnp.mean(dn * nrm, axis=-1, keepdims=True)) + dy_ref[...]

    row = pl.BlockSpec((tm, D_MODEL), lambda i, k: (i, 0))
    vec = pl.BlockSpec((1, D_MODEL), lambda i, k: (0, 0))
    return pl.pallas_call(
        body, name="dh_dx", grid=(S // tm, nk),
        in_specs=[pl.BlockSpec((None, tm, tk), lambda i, k: (k, i, 0)), pl.BlockSpec((tk, D_MODEL), lambda i, k: (k, 0)), row, row, vec],
        out_specs=[row, vec],
        out_shape=[jax.ShapeDtypeStruct((S, D_MODEL), F32), jax.ShapeDtypeStruct((1, D_MODEL), F32)],
        scratch_shapes=[pltpu.VMEM((tm, D_MODEL), F32)],
        compiler_params=_params(("arbitrary", "arbitrary"), 56),
    )(dproj, w_in, x, dy, pre_g)


def _sum_parts(parts, name):
    P, R, C = parts.shape
    tr = max(t for t in range(8, 513, 8) if R % t == 0)

    def body(p_ref, o_ref):
        acc = p_ref[0]
        for j in range(1, P):
            acc = acc + p_ref[j]
        o_ref[...] = acc

    return pl.pallas_call(
        body, name=name, grid=(R // tr,),
        in_specs=[pl.BlockSpec((P, tr, C), lambda i: (0, i, 0))], out_specs=pl.BlockSpec((tr, C), lambda i: (i, 0)),
        out_shape=jax.ShapeDtypeStruct((R, C), F32), compiler_params=_params(("parallel",)),
    )(parts)


def _adamw(parts, w, m, v, name, own=None, chip=None):
    groups = list(parts) if isinstance(parts, (list, tuple)) else [parts]
    owns = [] if own is None else (list(own) if isinstance(own, (list, tuple)) else [own])
    P, rows, C = groups[0].shape
    R = rows * len(groups)
    tr = 128 if rows % 128 == 0 else rows
    per = rows // tr
    c1 = 1.0 - ADAM_B1 ** ADAM_STEP
    c2 = 1.0 - ADAM_B2 ** ADAM_STEP
    n_in = len(groups) * (4 if owns else 1)

    def body(*refs):
        if owns:
            refs = refs[1:]
        p_refs = refs[:n_in]
        w_ref, m_ref, v_ref, g_ref, d_ref, nm_ref, nv_ref = refs[n_in:]
        g = None
        for q in range(len(groups)):
            if owns:
                gq = p_refs[4 * q + 3][...].astype(F32)
                for j in range(3):
                    gq = gq + p_refs[4 * q + j][...].astype(F32)
            else:
                gq = p_refs[q][0].astype(F32)
                for j in range(1, P):
                    gq = gq + p_refs[q][j].astype(F32)
            g = gq if g is None else jnp.where(pl.program_id(0) // per == q, gq, g)
        nm = ADAM_B1 * m_ref[...] + (1.0 - ADAM_B1) * g
        nv = ADAM_B2 * v_ref[...] + (1.0 - ADAM_B2) * (g * g)
        g_ref[...] = g
        nm_ref[...] = nm
        nv_ref[...] = nv
        d_ref[...] = -ADAM_LR * ((nm / c1) / (jnp.sqrt(nv / c2) + ADAM_EPS) + ADAM_WD * w_ref[...])

    out = jax.ShapeDtypeStruct((R, C), F32)
    if not owns:
        tile = pl.BlockSpec((tr, C), lambda i: (i, 0))
        return pl.pallas_call(
            body, name=name, grid=(R // tr,),
            in_specs=[pl.BlockSpec((P, tr, C), (lambda i, q=q: (0, jnp.clip(i - q * per, 0, per - 1), 0))) for q in range(len(groups))]
            + [tile, tile, tile], out_specs=[tile] * 4, out_shape=[out] * 4,
            compiler_params=_params(("parallel",)),
        )(*groups, w, m, v)
    tile = pl.BlockSpec((tr, C), lambda i, c_ref: (i, 0))
    specs, operands = [], []
    for q in range(len(groups)):
        for k in range(3):
            specs.append(pl.BlockSpec((None, tr, C), (lambda i, c_ref, q=q, k=k: (k + (c_ref[0] <= k).astype(jnp.int32),
                                                                                  jnp.clip(i - q * per, 0, per - 1), 0))))
            operands.append(groups[q])
        specs.append(pl.BlockSpec((None, tr, C), (lambda i, c_ref, q=q: (c_ref[0], jnp.clip(i - q * per, 0, per - 1), 0))))
        operands.append(owns[q])
    return pl.pallas_call(
        body, name=name,
        grid_spec=pltpu.PrefetchScalarGridSpec(num_scalar_prefetch=1, grid=(R // tr,), in_specs=specs + [tile, tile, tile],
                                               out_specs=[tile] * 4),
        out_shape=[out] * 4, compiler_params=_params(("parallel",)),
    )(chip, *operands, w, m, v)


def _adamw_small(gs, ws, ms, vs):
    n = len(ws)
    c1 = 1.0 - ADAM_B1 ** ADAM_STEP
    c2 = 1.0 - ADAM_B2 ** ADAM_STEP

    def flat2(a):
        return a.reshape(-1, a.shape[-1])

    def body(*refs):
        ins, outs = refs[:4 * n], refs[4 * n:]
        for a in range(n):
            g, w, m, v = (ins[k * n + a][...] for k in range(4))
            nm = ADAM_B1 * m + (1.0 - ADAM_B1) * g
            nv = ADAM_B2 * v + (1.0 - ADAM_B2) * (g * g)
            outs[a][...] = g
            outs[n + a][...] = -ADAM_LR * ((nm / c1) / (jnp.sqrt(nv / c2) + ADAM_EPS) + ADAM_WD * w)
            outs[2 * n + a][...] = nm
            outs[3 * n + a][...] = nv

    shapes = [flat2(w).shape for w in ws]
    out = pl.pallas_call(
        body, name="adamw_small", out_shape=[jax.ShapeDtypeStruct(sh, F32) for sh in shapes] * 4,
        compiler_params=pltpu.CompilerParams(vmem_limit_bytes=32 * MIB),
    )(*[g.reshape(sh) for g, sh in zip(gs, shapes)], *[flat2(a) for a in (*ws, *ms, *vs)])
    return [[out[k * n + a].reshape(ws[a].shape) for a in range(n)] for k in range(4)]


def _project(h, w_t):
    S = h.shape[0]
    n_tiles = D_IN // SEG_TILE
    ranges = [(c0 // SEG_TILE, (c0 + width) // SEG_TILE) for _, c0, width, _ in SEGMENTS]

    def body(h_ref, w_ref, *outs):
        j = pl.program_id(0)
        prod = _dot(h_ref[...], w_ref[...], NT)
        for (j0, j1), (_, _, _, dt), o_ref in zip(ranges, SEGMENTS, outs):
            @pl.when(jnp.logical_and(j >= j0, j < j1))
            def _(o_ref=o_ref, dt=dt):
                o_ref[...] = prod.astype(dt)

    out_shapes, out_specs = [], []
    for (j0, j1), (name, _, width, dt) in zip(ranges, SEGMENTS):
        if name == "gl":
            per = (j1 - j0) // 3
            out_shapes.append(jax.ShapeDtypeStruct((3, S, width // 3), dt))
            out_specs.append(pl.BlockSpec((None, S, SEG_TILE), (lambda j, j0=j0, j1=j1, per=per: (
                jnp.clip(j - j0, 0, j1 - j0 - 1) // per, 0, jnp.clip(j - j0, 0, j1 - j0 - 1) % per))))
        else:
            out_shapes.append(jax.ShapeDtypeStruct((S, width), dt))
            out_specs.append(pl.BlockSpec((S, SEG_TILE), (lambda j, j0=j0, j1=j1: (0, jnp.clip(j - j0, 0, j1 - j0 - 1)))))
    outs = pl.pallas_call(
        body, name="proj", grid=(n_tiles,),
        in_specs=[pl.BlockSpec((S, D_MODEL), lambda j: (0, 0)), pl.BlockSpec((SEG_TILE, D_MODEL), lambda j: (j, 0))],
        out_specs=out_specs, out_shape=out_shapes,
        compiler_params=_params(("arbitrary",), 56),
    )(h, w_t)
    return {name: o for (name, _, _, _), o in zip(SEGMENTS, outs)}


def _forward_a(x, mem, pre_g, mem_g, w_in, conv_w, conv_b, w_a, b_a, w_x, b_x, lam, sinks, rel_bias):
    S = x.shape[0]
    st = dict(T=min(512, S // 2), tm=min(512, S), bucket=_rel_bucket_map())
    st["h"] = _rms_fwd(x, pre_g, "pre_norm")
    st["memn"] = _rms_fwd(mem, mem_g, "mem_norm")
    seg = st["seg"] = _project(st["h"], w_in)
    st["h_rg"], st["y_rg"] = _rglru_fwd(seg["xr"], seg["g_rg"], conv_w, conv_b, w_a, b_a, w_x, b_x, lam, st["T"])
    st["o_swa"], st["y_swa"] = _swa_fwd(seg["q_s"], seg["kv"], seg["g_swa"], st["bucket"], rel_bias, _sink_column(sinks))
    return st


def _forward_b(st, x, target, post_g, w_memkv, wbr, w_out):
    S = x.shape[0]
    M = st["memn"].shape[0]
    seg = st["seg"]
    st["mkv"] = _matmul(st["memn"], w_memkv, "nn", M, 2 * D_MEM, D_MODEL, M, 512, D_MODEL, BF, "mem_kv")
    st["o_mem"], st["y_mem"] = _mem_fwd(seg["q_m"], st["mkv"], seg["g_mem"])
    st["ys"] = (st["y_rg"], st["y_swa"], st["y_mem"])
    st["merged"] = _merge_fwd(st["ys"], wbr, seg["gl"], st["tm"])
    st["dout"], st["dy"], st["loss"], st["dpost"] = _out_loss(st["merged"], w_out, x, target, post_g, min(256, S))
    return st


def _backward_a1(st, wbr, w_out):
    S = st["h"].shape[0]
    seg, ys, tm = st["seg"], st["ys"], st["tm"]
    st["dw_out"] = _matmul(st["merged"], st["dout"], "tn", D_MODEL, D_MODEL, S, 256, D_MODEL, S, BF, "dw_out", out_blocked="row")
    dgl0, dgl1, dgl2, dp0, dp1, dp2 = _merge_bwd(st["dout"], w_out, ys, wbr, seg["gl"], tm)
    st["dgl"] = (dgl0, dgl1, dgl2)
    dys, dwbr = [], []
    for i, dp in enumerate((dp0, dp1, dp2)):
        dys.append(_matmul(dp, wbr[i], "nn", S, 1024, D_MODEL, tm, 1024, D_MODEL, F32, "dy_br%d" % i))
        dwbr.append(_matmul(ys[i], dp, "tn", 1024, D_MODEL, S, 1024, 256, S, BF, "dw_br%d" % i, out_blocked="col"))
    st["dys"], st["dwbr"] = dys, dwbr
    return st


def _backward_a2(st, mem, w_memkv, conv_w, conv_b, w_a, b_a, w_x, b_x, lam):
    M = mem.shape[0]
    seg, dys = st["seg"], st["dys"]
    st["dq_m"], st["dg_mem"], dmkv = _mem_bwd(seg["q_m"], st["mkv"], seg["g_mem"], st["o_mem"], dys[2])
    dmkv_b = dmkv.astype(BF)
    st["dw_memkv"] = _matmul(st["memn"], dmkv_b, "tn", D_MODEL, 2 * D_MEM, M, 256, 2 * D_MEM, M, BF, "dw_memkv", out_blocked="row")
    dmemn = _matmul(dmkv_b, w_memkv, "nt", M, D_MODEL, 2 * D_MEM, M, 512, 2 * D_MEM, F32, "dmemn")
    st["dmem_g"] = _rms_gain_grad(dmemn, mem, "dmem_gain")
    st["dxr"], st["dg_rg"], st["dw_a"], st["dw_x"], st["dvec"] = _rglru_bwd(
        seg["xr"], seg["g_rg"], st["h_rg"], dys[0], conv_w, conv_b, w_a, b_a, w_x, b_x, lam, st["T"])
    return st


def _backward_b(st, rel_bias, sinks):
    seg = st["seg"]
    dq_s, dg_swa, dkv, dsinks, drel = _swa_bwd(seg["q_s"], seg["kv"], seg["g_swa"], st["o_swa"], st["dys"][1],
                                               st["bucket"], rel_bias, _sink_column(sinks))
    st["dsinks"], st["drel"] = dsinks.reshape(1, SWA_HEADS), drel.reshape(REL_BUCKETS, SWA_HEADS)
    dproj = jnp.concatenate([st["dxr"], st["dg_rg"], dq_s, dkv.astype(BF), dg_swa, st["dq_m"], st["dg_mem"], *st["dgl"]], axis=1)
    st["dproj"] = jnp.transpose(dproj.reshape(dproj.shape[0], N_DEV, D_IN // N_DEV), (1, 0, 2))
    return st


def _dw_in_half(st, half, dep=None):
    S = st["h"].shape[0]
    return _matmul(st["h"], st["dproj"], "tn", D_MODEL // 2, D_IN, S, 512, D_IN // N_DEV, S, BF, "dw_in%d" % half, a_moff=2 * half,
                   b_blocked=True, out_blocked="col", dep=dep)


def _owner_blocks(a):
    return jnp.swapaxes(a.reshape((4, 2) + a.shape[1:]), 0, 1)


def _local_step(x, mem, target, pre_g, post_g, mem_g, w_in, conv_w, conv_b, w_a, b_a, w_x, b_x, lam, sinks, rel_bias,
                w_memkv, wbr, w_out):
    st = _forward_a(x, mem, pre_g, mem_g, w_in, conv_w, conv_b, w_a, b_a, w_x, b_x, lam, sinks, rel_bias)
    st = _forward_b(st, x, target, post_g, w_memkv, wbr, w_out)
    st = _backward_a1(st, wbr, w_out)
    st = _backward_a2(st, mem, w_memkv, conv_w, conv_b, w_a, b_a, w_x, b_x, lam)
    st = _backward_b(st, rel_bias, sinks)
    st["dw_in"] = [_dw_in_half(st, 0), _dw_in_half(st, 1)]
    st["grad_x"], st["dpre"] = _dh_dx(st["dproj"], w_in, x, st["dy"], pre_g, st["tm"])
    return st


def _pad_rows(a, rows):
    a = a.reshape(-1, 128) if a.shape[-1] % 128 == 0 else jnp.pad(a, ((0, 0), (0, 128 - a.shape[-1])))
    return jnp.pad(a, ((0, rows - a.shape[0]), (0, 0))) if a.shape[0] < rows else a


def kernel(x, mem, pre_norm_g, post_norm_g, mem_norm_g, w_in, conv_w, conv_b, w_rg_a, b_rg_a, w_rg_x, b_rg_x, lru_lambda, swa_sinks, rel_bias, w_mem_kv, w_br_rg, w_br_swa, w_br_mem, w_out, loss_target, m_pre_norm_g, m_post_norm_g, m_mem_norm_g, m_w_in, m_conv_w, m_conv_b, m_w_rg_a, m_b_rg_a, m_w_rg_x, m_b_rg_x, m_lru_lambda, m_swa_sinks, m_rel_bias, m_w_mem_kv, m_w_br_rg, m_w_br_swa, m_w_br_mem, m_w_out, v_pre_norm_g, v_post_norm_g, v_mem_norm_g, v_w_in, v_conv_w, v_conv_b, v_w_rg_a, v_b_rg_a, v_w_rg_x, v_b_rg_x, v_lru_lambda, v_swa_sinks, v_rel_bias, v_w_mem_kv, v_w_br_rg, v_w_br_swa, v_w_br_mem, v_w_out):
    cx, cy, cc = lax.axis_index("x"), lax.axis_index("y"), lax.axis_index("c")
    me = 4 * cx + 2 * cy + cc
    chip = 2 * cx + cy
    core = jnp.reshape(cc, (1,)).astype(jnp.int32)
    x0, mem0 = x[0], mem[0]
    w_a_b, w_x_b = w_rg_a[0].astype(BF), w_rg_x[0].astype(BF)

    def landing(own, slot, slots):
        return lax.dynamic_update_slice(lax.empty((slots,) + own.shape, own.dtype), own[None], (slot,) + (0,) * own.ndim)


    def swap_start(parts, tag):
        return _exchange_start(parts, [lax.empty(p.shape[1:], p.dtype) for p in parts], _plan_swap(len(parts)), "swap_%s_start" % tag)

    def scatter_start(swap, after, tag, prefill=()):
        s_send, s_recv, parts, got, _ = swap
        got = _exchange_wait(s_send, s_recv, parts, got, _plan_swap(len(parts)), after, "swap_%s_wait" % tag)
        sums = [_pair_sum(p, g, core, "scatter_%s_sum%d" % (tag, i)) for i, (p, g) in enumerate(zip(parts, got))]
        lands = [landing(lax.dynamic_index_in_dim(s, chip, 0, keepdims=False), chip, 4) if i in prefill
                 else lax.empty(s.shape, s.dtype) for i, s in enumerate(sums)]
        return _exchange_start(sums, lands, _plan_scatter(len(sums)), "scatter_%s_start" % tag)

    def zero_after(a):
        return jnp.minimum(jnp.abs(a.reshape(-1)[0].astype(F32)), 0.0)

    g_in, g_cw = _all_gather_relayed([jnp.transpose(w_in[0]).astype(BF), conv_w[0]], [True, False], "gather_w_in")
    w_in_f = g_in.reshape(D_IN, D_MODEL)
    conv_w_f = jnp.transpose(g_cw, (1, 0, 2)).reshape(CONV_W, D_RNN)

    after_first = zero_after(g_cw).astype(BF)
    rest = [w.astype(BF) + after_first for w in (w_mem_kv[0], jnp.transpose(w_br_rg[0]), jnp.transpose(w_br_swa[0]),
                                                 jnp.transpose(w_br_mem[0]), w_out[0])]
    kinds = ["lead"] * len(rest)
    plan_g = _plan_gather(kinds)
    zones = _place_own([lax.empty((N_DEV,) + w.shape, w.dtype) for w in rest], rest, jnp.reshape(me, (1,)).astype(jnp.int32), "gather_rest_own")
    g_send, g_recv, g_src, g_land, g_token = _exchange_start(rest, zones, plan_g, "gather_rest_start")
    st = _forward_a(x0, mem0, pre_norm_g + g_token[0:1, 0:1], mem_norm_g, w_in_f, conv_w_f, conv_b, w_a_b, b_rg_a, w_x_b, b_rg_x,
                    lru_lambda, swa_sinks, rel_bias)
    g_land = _exchange_wait(g_send, g_recv, g_src, g_land, plan_g, st["y_swa"], "gather_rest_wait")
    g_land = _forward_to_sibling(g_land, kinds, "gather_rest_forward")
    w_memkv_f = g_land[0].reshape(D_MODEL, 2 * D_MEM)
    wbr = tuple(g_land[i].reshape(D_MODEL, D_RNN) for i in (1, 2, 3))
    w_out_f = g_land[4].reshape(D_MODEL, D_MODEL)

    st = _forward_b(st, x0, loss_target[0], post_norm_g, w_memkv_f, wbr, w_out_f)
    st = _backward_a1(st, wbr, w_out_f)
    parts_a = [st["dw_out"], st["dwbr"][0], st["dwbr"][1], st["dwbr"][2]]
    plan_a = _plan_scatter(len(parts_a))
    swap_a = swap_start(parts_a, "a")
    st = _backward_a2(st, mem0, w_memkv_f, conv_w_f, conv_b + swap_a[4][0:1, 0:1], w_a_b, b_rg_a, w_x_b, b_rg_x, lru_lambda)
    a_send, a_recv, a_src, a_land, a_token = scatter_start(swap_a, st["dxr"], "a")
    parts_c = [st["dw_memkv"], _owner_blocks(st["dw_a"]), _owner_blocks(st["dw_x"])]
    plan_c = _plan_scatter(len(parts_c))
    swap_c = swap_start(parts_c, "c")

    st = _backward_b(st, rel_bias, swa_sinks + swap_c[4][0:1, 0:1] + a_token[0:1, 0:1])
    c_send, c_recv, c_src, c_land, c_token = scatter_start(swap_c, st["dproj"], "c", prefill=(1, 2))
    plan_b = _plan_scatter(1)

    def dw_in_parts(half, dep):
        dwh = _dw_in_half(st, half, dep)
        return dwh, [dwh]

    dw0, parts_b0 = dw_in_parts(0, c_token)
    swap_b0 = swap_start(parts_b0, "b0")
    a_land = _exchange_wait(a_send, a_recv, a_src, a_land, plan_a, swap_b0[4], "scatter_a_wait")
    big = [None] * 6

    chip1 = jnp.reshape(chip, (1,)).astype(jnp.int32)

    def adamw_big(j, land, own, wt, mt, vt):
        big[j] = [a[None] for a in _adamw(land, wt[0], mt[0], vt[0], "adamw_big%d" % j, own=own, chip=chip1)]

    adamw_big(5, a_land[0], a_src[0], w_out, m_w_out, v_w_out)
    adamw_big(2, a_land[1], a_src[1], w_br_rg, m_w_br_rg, v_w_br_rg)
    adamw_big(3, a_land[2], a_src[2], w_br_swa, m_w_br_swa, v_w_br_swa)
    halves = [scatter_start(swap_b0, big[3][1], "b0")]
    dw1, parts_b1 = dw_in_parts(1, halves[0][4])
    swap_b1 = swap_start(parts_b1, "b1")
    c_land = _exchange_wait(c_send, c_recv, c_src, c_land, plan_c, swap_b1[4], "scatter_c_wait")
    g_wa_blk = _sum_parts(c_land[1], "sum_w_rg_a")
    g_wx_blk = _sum_parts(c_land[2], "sum_w_rg_x")
    adamw_big(4, a_land[3], a_src[3], w_br_mem, m_w_br_mem, v_w_br_mem)
    adamw_big(1, c_land[0], c_src[0], w_mem_kv, m_w_mem_kv, v_w_mem_kv)
    halves.append(scatter_start(swap_b1, big[1][1], "b1"))
    grad_x, dpre = _dh_dx(st["dproj"], w_in_f, x0, st["dy"], pre_norm_g + halves[1][4][0:1, 0:1], st["tm"])
    after, b_lands, b_sums = grad_x, [], []
    for half, (b_send, b_recv, b_src, b_land, _) in enumerate(halves):
        b_lands.append(_exchange_wait(b_send, b_recv, b_src, b_land, plan_b, after, "scatter_b%d_wait" % half)[0])
        b_sums.append(b_src[0])
        after = b_lands[-1]
    big[0] = [a[None] for a in _adamw(b_lands, w_in[0], m_w_in[0], v_w_in[0], "adamw_big0", own=b_sums, chip=chip1)]
    links_free = jnp.minimum(jnp.abs(big[0][0][0, 0, 0]), 0.0)

    pack = jnp.concatenate([dpre.reshape(16, 128), st["dpost"].reshape(16, 128), st["dmem_g"].reshape(16, 128),
                            st["dvec"].reshape(64, 128), _pad_rows(st["dsinks"], 8), _pad_rows(st["drel"], 32), g_wa_blk, g_wx_blk], axis=0) + links_free
    gathered = _all_gather([pack], "gather_small")[0]
    gs = _sum_parts(gathered, "sum_small")
    g_pre, g_post, g_memg = gs[0:16].reshape(1, D_MODEL), gs[16:32].reshape(1, D_MODEL), gs[32:48].reshape(1, D_MODEL)
    gvec = gs[48:112].reshape(8, D_RNN)
    g_conv_w = lax.dynamic_slice(gvec[0:CONV_W], (0, me * RNN_BLOCK), (CONV_W, RNN_BLOCK))
    g_conv_b, g_b_a, g_b_x, g_lam = gvec[4:5], gvec[5:6], gvec[6:7], gvec[7:8]
    g_sinks = gs[112:113, :SWA_HEADS]
    g_rel = gs[120:152, :SWA_HEADS]
    g_w_a = gathered[:, 152:280]
    g_w_x = gathered[:, 280:408]

    g_small = (g_pre, g_post, g_memg, g_conv_b, g_b_a, g_b_x, g_lam, g_w_a, g_w_x, g_sinks, g_rel, g_conv_w)
    w_small = (pre_norm_g, post_norm_g, mem_norm_g, conv_b, b_rg_a, b_rg_x, lru_lambda, w_rg_a, w_rg_x, swa_sinks, rel_bias, conv_w)
    m_small = (m_pre_norm_g, m_post_norm_g, m_mem_norm_g, m_conv_b, m_b_rg_a, m_b_rg_x, m_lru_lambda, m_w_rg_a, m_w_rg_x, m_swa_sinks, m_rel_bias, m_conv_w)
    v_small = (v_pre_norm_g, v_post_norm_g, v_mem_norm_g, v_conv_b, v_b_rg_a, v_b_rg_x, v_lru_lambda, v_w_rg_a, v_w_rg_x, v_swa_sinks, v_rel_bias, v_conv_w)
    sm = _adamw_small(g_small, w_small, m_small, v_small)

    loss_total = lax.psum(st["loss"][0, 0], AXES)

    def leaves(k):
        s = sm[k]
        return [s[0], s[1], s[2], big[0][k], s[11], s[3], s[7], s[4], s[8], s[5], s[6], s[9], s[10],
                big[1][k], big[2][k], big[3][k], big[4][k], big[5][k]]

    return (loss_total, grad_x[None], *leaves(0), *leaves(1), *leaves(2), *leaves(3))
```

```python
import math

import jax
import jax.numpy as jnp
import numpy as np
from jax import lax
from jax.experimental import pallas as pl
from jax.experimental.pallas import tpu as pltpu

F32, BF = jnp.float32, jnp.bfloat16
MESH = pl.DeviceIdType.MESH
AXES = ("x", "y", "c")
N_DEV = 8

D_MODEL = 2048
D_RNN = 1024
RNN_BLOCKS = 8
RNN_BLOCK = 128
CONV_W = 4
LRU_C = 8.0
SWA_HEADS = 16
SWA_KV_HEADS = 2
SWA_HD = 64
WINDOW = 128
MEM_HEADS = 4
MEM_HD = 256
D_MEM = 1024
REL_BUCKETS = 32
REL_MAX_DIST = 128
EPS = 1e-6
NEG_INF = -1e30
D_IN = 12544
SEGMENTS = (("xr", 0, 1024, F32), ("g_rg", 1024, 1024, F32), ("q_s", 2048, 1024, BF), ("kv", 3072, 256, BF),
            ("g_swa", 3328, 1024, F32), ("q_m", 4352, 1024, BF), ("g_mem", 5376, 1024, F32), ("gl", 6400, 6144, F32))
SEG_TILE = 256

ADAM_LR, ADAM_B1, ADAM_B2, ADAM_EPS, ADAM_WD, ADAM_STEP = 0.001, 0.9, 0.999, 1e-08, 0.01, 10

NN = (((1,), (0,)), ((), ()))
NT = (((1,), (1,)), ((), ()))
TN = (((0,), (0,)), ((), ()))
MIB = 2 ** 20


def _dot(a, b, dn):
    return lax.dot_general(a, b, dn, preferred_element_type=F32)


VMEM_LIMIT_MIB = 60


def _params(sem=None):
    return pltpu.CompilerParams(dimension_semantics=sem, vmem_limit_bytes=VMEM_LIMIT_MIB * MIB)


def _sigmoid(z):
    return 1.0 / (1.0 + jnp.exp(-z))


def _softplus(z):
    return jnp.maximum(z, 0.0) + jnp.log(1.0 + jnp.exp(-jnp.abs(z)))


def _expm1(z):
    p = z * (1.0 + z * (0.5 + z * (1.0 / 6 + z * (1.0 / 24 + z * (1.0 / 120 + z * (1.0 / 720 + z * (1.0 / 5040 + z / 40320)))))))
    return jnp.where(jnp.abs(z) < 0.3, p, jnp.exp(z) - 1.0)


def _flat(p):
    return 4 * p[0] + 2 * p[1] + p[2]


def _all_gather(arrs, name):
    n = len(arrs)

    def body(*refs):
        ins, outs = refs[:n], refs[n:2 * n]
        send_sems, recv_sems, local_sems = refs[2 * n:]
        x, y, c = lax.axis_index("x"), lax.axis_index("y"), lax.axis_index("c")
        me, sibling = (x, y, c), (x, y, 1 - c)
        chips = [(1 - x, y), (x, 1 - y), (1 - x, 1 - y)]

        def copy(a, k, block, to, src=None):
            dst = outs[a].at[_flat(block)]
            return pltpu.make_async_remote_copy(src_ref=dst if src is None else src, dst_ref=dst,
                                                send_sem=send_sems.at[a * 7 + k], recv_sem=recv_sems.at[a * 7 + k],
                                                device_id=to, device_id_type=MESH)

        mine = [pltpu.make_async_copy(ins[a], outs[a].at[_flat(me)], local_sems.at[a]) for a in range(n)]
        for cp in mine:
            cp.start()
        first = []
        for a in range(n):
            first += [copy(a, 1 + j, me, (*chip, c), src=ins[a]) for j, chip in enumerate(chips)]
            first.append(copy(a, 0, me, sibling, src=ins[a]))
        for cp in first:
            cp.start()
        passed = []
        for j, chip in enumerate(chips):
            for a in range(n):
                copy(a, 1 + j, (*chip, c), me).wait_recv()
                fw = copy(a, 4 + j, (*chip, c), sibling)
                fw.start()
                passed.append(fw)
        for a in range(n):
            copy(a, 0, sibling, me).wait_recv()
            for j, chip in enumerate(chips):
                copy(a, 4 + j, (*chip, 1 - c), me).wait_recv()
        for cp in first + passed:
            cp.wait_send()
        for cp in mine:
            cp.wait()

    any_spec = pl.BlockSpec(memory_space=pl.ANY)
    return pl.pallas_call(
        body, name=name,
        out_shape=[jax.ShapeDtypeStruct((N_DEV,) + a.shape, a.dtype) for a in arrs],
        in_specs=[any_spec] * n, out_specs=[any_spec] * n,
        scratch_shapes=[pltpu.SemaphoreType.DMA((7 * n,)), pltpu.SemaphoreType.DMA((7 * n,)), pltpu.SemaphoreType.DMA((n,))],
    )(*arrs)


def _all_gather_relayed(arrs, relay, name):
    n = len(arrs)
    K = 9

    def body(*refs):
        ins, outs = refs[:n], refs[n:2 * n]
        send_sems, recv_sems, local_sems = refs[2 * n:]
        x, y, c = lax.axis_index("x"), lax.axis_index("y"), lax.axis_index("c")
        me, sib = (x, y, c), (x, y, 1 - c)
        xn, yn, dg = (1 - x, y, c), (x, 1 - y, c), (1 - x, 1 - y, c)

        def other(p):
            return (p[0], p[1], 1 - p[2])

        def rows(a, half):
            h = arrs[a].shape[0] // 2
            return pl.ds(half * h, h)

        def copy(a, k, block, to, half=None, src=None):
            dst = outs[a].at[_flat(block)]
            if half is not None:
                dst = dst.at[rows(a, half)]
            return pltpu.make_async_remote_copy(src_ref=dst if src is None else src, dst_ref=dst,
                                                send_sem=send_sems.at[a * K + k], recv_sem=recv_sems.at[a * K + k],
                                                device_id=to, device_id_type=MESH)

        mine = [pltpu.make_async_copy(ins[a], outs[a].at[_flat(me)], local_sems.at[a]) for a in range(n)]
        for cp in mine:
            cp.start()
        sends = []

        def start(cp):
            cp.start()
            sends.append(cp)

        for a in range(n):
            start(copy(a, 1, me, xn, src=ins[a]))
            start(copy(a, 2, me, yn, src=ins[a]))
            if not relay[a]:
                start(copy(a, 3, me, dg, src=ins[a]))
            start(copy(a, 0, me, sib, src=ins[a]))
        for a in range(n):
            copy(a, 1, xn, me).wait_recv()
            if relay[a]:
                start(copy(a, 3, xn, yn, half=0))
            start(copy(a, 5, xn, sib))
        for a in range(n):
            copy(a, 2, yn, me).wait_recv()
            if relay[a]:
                start(copy(a, 4, yn, xn, half=1))
            start(copy(a, 6, yn, sib))
        for a in range(n):
            if relay[a]:
                copy(a, 3, dg, me, half=0).wait_recv()
                start(copy(a, 7, dg, sib, half=0))
                copy(a, 4, dg, me, half=1).wait_recv()
                start(copy(a, 8, dg, sib, half=1))
            else:
                copy(a, 3, dg, me).wait_recv()
                start(copy(a, 7, dg, sib))
        for a in range(n):
            copy(a, 0, sib, me).wait_recv()
            copy(a, 5, other(xn), me).wait_recv()
            copy(a, 6, other(yn), me).wait_recv()
            if relay[a]:
                copy(a, 7, other(dg), me, half=0).wait_recv()
                copy(a, 8, other(dg), me, half=1).wait_recv()
            else:
                copy(a, 7, other(dg), me).wait_recv()
        for cp in sends:
            cp.wait_send()
        for cp in mine:
            cp.wait()

    any_spec = pl.BlockSpec(memory_space=pl.ANY)
    return pl.pallas_call(
        body, name=name,
        out_shape=[jax.ShapeDtypeStruct((N_DEV,) + a.shape, a.dtype) for a in arrs],
        in_specs=[any_spec] * n, out_specs=[any_spec] * n,
        scratch_shapes=[pltpu.SemaphoreType.DMA((K * n,)), pltpu.SemaphoreType.DMA((K * n,)), pltpu.SemaphoreType.DMA((n,))],
    )(*arrs)


def _chip_peers(x, y):
    return [(1 - x, y), (x, 1 - y), (1 - x, 1 - y)]


def _chip(p):
    return 2 * p[0] + p[1]


def _plan_gather(kinds):
    def plan(x, y, c):
        out = []
        for a, kind in enumerate(kinds):
            for peer in [(x, y, 1 - c)] + [(*ch, c) for ch in _chip_peers(x, y)]:
                out.append((a, None, (kind, _flat((x, y, c))), peer, (kind, _flat(peer))))
        return out
    return plan


def _plan_swap(n):
    def plan(x, y, c):
        return [(a, 1 - c, ("all", 0), (x, y, 1 - c), ("all", 0)) for a in range(n)]
    return plan


def _slot(ref, where):
    kind, k = where
    if kind == "all":
        return ref
    if kind == "lead":
        return ref.at[k]
    return ref.at[:, pl.ds(pl.multiple_of(k * 256, 256), 256)]


def _plan_scatter(n):
    def plan(x, y, c):
        out = []
        for a in range(n):
            for ch in _chip_peers(x, y):
                out.append((a, _chip(ch), ("lead", _chip((x, y))), (*ch, c), ("lead", _chip(ch))))
        return out
    return plan


HBM_SPEC = pl.BlockSpec(memory_space=pltpu.HBM)
SEM_SPEC = pl.BlockSpec(memory_space=pltpu.SEMAPHORE)


def _in_hbm(a):
    return pltpu.with_memory_space_constraint(a, pltpu.HBM)


def _exchange_start(srcs, lands, plan, name):
    n = len(srcs)
    count = len(plan(0, 0, 0))

    def body(*refs):
        src_refs, land_refs = refs[:n], refs[n:2 * n]
        send_sems, recv_sems = refs[2 * n], refs[2 * n + 1]
        token = refs[-1]
        x, y, c = lax.axis_index("x"), lax.axis_index("y"), lax.axis_index("c")
        for k, (a, si, di, peer, _) in enumerate(plan(x, y, c)):
            src = src_refs[a] if si is None else src_refs[a].at[si]
            pltpu.make_async_remote_copy(src_ref=src, dst_ref=_slot(land_refs[a], di), send_sem=send_sems.at[k],
                                         recv_sem=recv_sems.at[k], device_id=peer, device_id_type=MESH).start()
        token[...] = jnp.zeros_like(token)

    out = pl.pallas_call(
        body, name=name,
        out_shape=(pltpu.SemaphoreType.DMA((count,)), pltpu.SemaphoreType.DMA((count,)),
                   *[pltpu.HBM(a.shape, a.dtype) for a in lands], jax.ShapeDtypeStruct((8, 128), F32)),
        in_specs=[HBM_SPEC] * (2 * n),
        out_specs=(SEM_SPEC, SEM_SPEC, *([HBM_SPEC] * n), pl.BlockSpec(memory_space=pltpu.VMEM)),
        input_output_aliases={n + i: 2 + i for i in range(n)},
        compiler_params=pltpu.CompilerParams(has_side_effects=pltpu.SideEffectType.DATAFLOW_SIDE_EFFECTING),
    )(*[_in_hbm(a) for a in srcs], *[_in_hbm(a) for a in lands])
    return out[0], out[1], list(srcs), list(out[2:2 + n]), out[-1]


def _exchange_wait(send_sems, recv_sems, srcs, lands, plan, after, name):
    n = len(srcs)

    def body(*refs):
        src_refs, land_refs = refs[:n], refs[n:2 * n]
        send_sems, recv_sems = refs[2 * n], refs[2 * n + 1]
        x, y, c = lax.axis_index("x"), lax.axis_index("y"), lax.axis_index("c")
        for k, (a, si, _, peer, ri) in enumerate(plan(x, y, c)):
            src = src_refs[a] if si is None else src_refs[a].at[si]
            cp = pltpu.make_async_remote_copy(src_ref=src, dst_ref=_slot(land_refs[a], ri), send_sem=send_sems.at[k],
                                              recv_sem=recv_sems.at[k], device_id=peer, device_id_type=MESH)
            cp.wait_send()
            cp.wait_recv()

    out = pl.pallas_call(
        body, name=name,
        out_shape=tuple(pltpu.HBM(a.shape, a.dtype) for a in lands),
        in_specs=[HBM_SPEC] * (2 * n) + [SEM_SPEC, SEM_SPEC, pl.BlockSpec(memory_space=pl.ANY)],
        out_specs=tuple([HBM_SPEC] * n),
        input_output_aliases={n + i: i for i in range(n)},
        compiler_params=pltpu.CompilerParams(has_side_effects=pltpu.SideEffectType.DATAFLOW_SIDE_EFFECTING),
    )(*[_in_hbm(a) for a in srcs], *lands, send_sems, recv_sems, after)
    return list(out)


def _forward_to_sibling(lands, kinds, name):
    n = len(lands)

    def body(*refs):
        in_refs, out_refs = refs[:n], refs[n:2 * n]
        send_sems, recv_sems = refs[2 * n:]
        x, y, c = lax.axis_index("x"), lax.axis_index("y"), lax.axis_index("c")
        sibling = (x, y, 1 - c)

        def copy(a, j, slot):
            return pltpu.make_async_remote_copy(src_ref=_slot(in_refs[a], (kinds[a], slot)), dst_ref=_slot(out_refs[a], (kinds[a], slot)),
                                                send_sem=send_sems.at[a * 3 + j], recv_sem=recv_sems.at[a * 3 + j],
                                                device_id=sibling, device_id_type=MESH)

        sends = [copy(a, j, _flat((*ch, c))) for a in range(n) for j, ch in enumerate(_chip_peers(x, y))]
        for cp in sends:
            cp.start()
        for a in range(n):
            for j, ch in enumerate(_chip_peers(x, y)):
                copy(a, j, _flat((*ch, 1 - c))).wait_recv()
        for cp in sends:
            cp.wait_send()

    any_spec = pl.BlockSpec(memory_space=pl.ANY)
    return pl.pallas_call(
        body, name=name, out_shape=[jax.ShapeDtypeStruct(a.shape, a.dtype) for a in lands],
        in_specs=[any_spec] * n, out_specs=[any_spec] * n, input_output_aliases={a: a for a in range(n)},
        scratch_shapes=[pltpu.SemaphoreType.DMA((3 * n,)), pltpu.SemaphoreType.DMA((3 * n,))],
    )(*lands)


def _place_own(zones, owns, slot, name):
    n = len(zones)

    def body(slot_ref, *refs):
        for a in range(n):
            refs[2 * n + a][...] = refs[a][...]

    return pl.pallas_call(
        body, name=name,
        grid_spec=pltpu.PrefetchScalarGridSpec(
            num_scalar_prefetch=1, grid=(1,),
            in_specs=[pl.BlockSpec(o.shape, lambda i, s_ref: (0, 0)) for o in owns] + [pl.BlockSpec(memory_space=pl.ANY)] * n,
            out_specs=[pl.BlockSpec((None,) + o.shape, lambda i, s_ref: (s_ref[0], 0, 0)) for o in owns]),
        out_shape=[jax.ShapeDtypeStruct(z.shape, z.dtype) for z in zones],
        input_output_aliases={1 + n + a: a for a in range(n)},
        compiler_params=_params(("arbitrary",)),
    )(slot, *owns, *zones)


def _swap_with_sibling(parts, name):
    n = len(parts)

    def body(*refs):
        in_refs, out_refs = refs[:n], refs[n:2 * n]
        send_sems, recv_sems = refs[2 * n:]
        x, y, c = lax.axis_index("x"), lax.axis_index("y"), lax.axis_index("c")
        sends = [pltpu.make_async_remote_copy(src_ref=in_refs[a].at[1 - c], dst_ref=out_refs[a], send_sem=send_sems.at[a],
                                              recv_sem=recv_sems.at[a], device_id=(x, y, 1 - c), device_id_type=MESH)
                 for a in range(n)]
        for cp in sends:
            cp.start()
        for cp in sends:
            cp.wait()

    any_spec = pl.BlockSpec(memory_space=pl.ANY)
    return pl.pallas_call(
        body, name=name, out_shape=[jax.ShapeDtypeStruct(a.shape[1:], a.dtype) for a in parts],
        in_specs=[any_spec] * n, out_specs=[any_spec] * n,
        scratch_shapes=[pltpu.SemaphoreType.DMA((n,)), pltpu.SemaphoreType.DMA((n,))],
    )(*parts)


def _pair_sum(parts, got, core, name):
    _, _, R, C = parts.shape
    tr = 256 if R % 256 == 0 else R

    def body(c_ref, p_ref, g_ref, o_ref):
        o_ref[...] = (p_ref[...].astype(F32) + g_ref[...].astype(F32)).astype(o_ref.dtype)

    return pl.pallas_call(
        body, name=name,
        grid_spec=pltpu.PrefetchScalarGridSpec(
            num_scalar_prefetch=1, grid=(4, R // tr),
            in_specs=[pl.BlockSpec((None, None, tr, C), lambda j, i, c_ref: (c_ref[0], j, i, 0)),
                      pl.BlockSpec((None, tr, C), lambda j, i, c_ref: (j, i, 0))],
            out_specs=pl.BlockSpec((None, tr, C), lambda j, i, c_ref: (j, i, 0))),
        out_shape=jax.ShapeDtypeStruct((4, R, C), parts.dtype),
        compiler_params=_params(("parallel", "parallel")),
    )(core, parts, got)


def _matmul(a, b, mode, M, N, K, tm, tn, tk, out_dtype, name, b_noff=0, a_moff=0, a_koff=0, b_blocked=False, out_blocked=None,
            dep=None, addend=None):
    nm, nn, nk = M // tm, N // tn, K // tk
    if mode == "nn":
        a_spec = pl.BlockSpec((tm, tk), lambda j, i, k: (i, k + a_koff))
        b_spec = pl.BlockSpec((tk, tn), lambda j, i, k: (k, j + b_noff))
        dn = NN
    elif mode == "nt":
        a_spec = pl.BlockSpec((tm, tk), lambda j, i, k: (i, k + a_koff))
        if b_blocked:
            b_spec = pl.BlockSpec((None, tn, tk), lambda j, i, k: (k, j, 0))
        else:
            b_spec = pl.BlockSpec((tn, tk), lambda j, i, k: (j + b_noff, k))
        dn = NT
    else:
        a_spec = pl.BlockSpec((tk, tm), lambda j, i, k: (k, i + a_moff))
        if b_blocked:
            b_spec = pl.BlockSpec((None, tk, tn), lambda j, i, k: (j, k, 0))
        else:
            b_spec = pl.BlockSpec((tk, tn), lambda j, i, k: (k, j + b_noff))
        dn = TN
    if out_blocked == "col":
        out_shape = jax.ShapeDtypeStruct((2, 4, M, tn), out_dtype)
        out_spec = pl.BlockSpec((None, None, tm, tn), lambda j, i, k: (j % 2, j // 2, i, 0))
    elif out_blocked == "row":
        out_shape = jax.ShapeDtypeStruct((2, 4, tm, N), out_dtype)
        out_spec = pl.BlockSpec((None, None, tm, tn), lambda j, i, k: (i % 2, i // 2, 0, j))
    elif out_blocked == "third":
        out_shape = jax.ShapeDtypeStruct((3, M, N // 3), out_dtype)
        out_spec = pl.BlockSpec((None, tm, tn), lambda j, i, k: (j // (nn // 3), i, j % (nn // 3)))
    else:
        out_shape = jax.ShapeDtypeStruct((M, N), out_dtype)
        out_spec = pl.BlockSpec((tm, tn), lambda j, i, k: (i, j))

    n_extra = (addend is not None) + (dep is not None)

    def body(a_ref, b_ref, *rest):
        o_ref, scratch = rest[n_extra], rest[n_extra + 1:]
        if nk == 1:
            prod = _dot(a_ref[...], b_ref[...], dn)
            if addend is not None:
                prod = prod + rest[0][...].astype(F32)
            o_ref[...] = prod.astype(out_dtype)
        else:
            assert addend is None
            acc_ref, = scratch
            k = pl.program_id(2)

            @pl.when(k == 0)
            def _():
                acc_ref[...] = jnp.zeros_like(acc_ref)

            acc_ref[...] += _dot(a_ref[...], b_ref[...], dn)

            @pl.when(k == nk - 1)
            def _():
                o_ref[...] = acc_ref[...].astype(out_dtype)

    return pl.pallas_call(
        body, name=name, grid=(nn, nm, nk),
        in_specs=[a_spec, b_spec] + ([] if addend is None else [out_spec])
        + ([] if dep is None else [pl.BlockSpec((8, 128), lambda j, i, k: (0, 0))]),
        out_specs=out_spec, out_shape=out_shape,
        scratch_shapes=[] if nk == 1 else [pltpu.VMEM((tm, tn), F32)],
        compiler_params=_params(("parallel", "parallel", "arbitrary")),
    )(a, b, *([] if addend is None else [addend]), *([] if dep is None else [dep]))


def _rms_fwd(x, g, name):
    R, Dm = x.shape
    tr = min(R, 256)

    def body(x_ref, g_ref, h_ref):
        xv = x_ref[...]
        r = lax.rsqrt(jnp.mean(xv * xv, axis=-1, keepdims=True) + EPS)
        h_ref[...] = (xv * r * g_ref[...]).astype(BF)

    return pl.pallas_call(
        body, name=name, grid=(R // tr,),
        in_specs=[pl.BlockSpec((tr, Dm), lambda i: (i, 0)), pl.BlockSpec((1, Dm), lambda i: (0, 0))],
        out_specs=pl.BlockSpec((tr, Dm), lambda i: (i, 0)), out_shape=jax.ShapeDtypeStruct((R, Dm), BF),
        compiler_params=_params(("parallel",)),
    )(x, g)


def _rms_gain_grad(dn, x, name):
    R, Dm = x.shape

    def body(dn_ref, x_ref, o_ref):
        xv = x_ref[...]
        r = lax.rsqrt(jnp.mean(xv * xv, axis=-1, keepdims=True) + EPS)
        o_ref[...] = jnp.sum(dn_ref[...] * xv * r, axis=0, keepdims=True)

    return pl.pallas_call(
        body, name=name, out_shape=jax.ShapeDtypeStruct((1, Dm), F32),
        compiler_params=_params(),
    )(dn, x)


def _shift_down(v, k, head8, row, T):
    if k == 0:
        return v
    r = pltpu.roll(v, k, 0)
    hr = pltpu.roll(head8, k, 0)
    top = jnp.where(row[:8] < k, hr, r[:8])
    return jnp.concatenate([top, r[8:]], axis=0)


def _shift_up(v, k, tail8, row, T):
    if k == 0:
        return v
    r = pltpu.roll(v, T - k, 0)
    tr = pltpu.roll(tail8, 8 - k, 0)
    bot = jnp.where(row[:8] >= 8 - k, tr, r[T - 8:])
    return jnp.concatenate([r[:T - 8], bot], axis=0)


def _rglru_gates(u, head8, grow, row, T, cw_ref, cb_ref, wa_ref, ba_ref, wx_ref, bx_ref, lam_ref):
    us = [_shift_down(u, k, head8, row, T) for k in range(CONV_W)]
    acc = us[0] * cw_ref[0:1, :]
    for k in range(1, CONV_W):
        acc = acc + us[k] * cw_ref[k:k + 1, :]
    conv = cb_ref[...] + acc
    cbf = conv.astype(BF)
    r_ = _sigmoid(_dot(cbf, wa_ref[0], NN) + ba_ref[...])
    i_ = _sigmoid(_dot(cbf, wx_ref[0], NN) + bx_ref[...])
    sp = _softplus(-lam_ref[...])
    la = -LRU_C * r_ * sp
    a = jnp.exp(la)
    mult_raw = jnp.sqrt(-_expm1(2.0 * la))
    mult = jnp.where(grow == 0, 1.0, mult_raw)
    return us, conv, cbf, r_, i_, sp, a, mult_raw, mult


def _rglru_specs(T, nt, rev):
    tmap = (lambda n, t: (nt - 1 - t, n)) if rev else (lambda n, t: (t, n))
    hmap = ((lambda n, t: (jnp.maximum((nt - 1 - t) * (T // 8) - 1, 0), n)) if rev
            else (lambda n, t: (jnp.maximum(t * (T // 8) - 1, 0), n)))
    tile = pl.BlockSpec((T, RNN_BLOCK), tmap)
    halo = pl.BlockSpec((8, RNN_BLOCK), hmap)
    vec = pl.BlockSpec((1, RNN_BLOCK), lambda n, t: (0, n))
    cw = pl.BlockSpec((CONV_W, RNN_BLOCK), lambda n, t: (0, n))
    wblk = pl.BlockSpec((1, RNN_BLOCK, RNN_BLOCK), lambda n, t: (n, 0, 0))
    return tile, halo, vec, cw, wblk


def _rglru_fwd(xr, g, cw, cb, wa, ba, wx, bx, lam, T):
    S = xr.shape[0]
    nt = S // T

    def body(u_ref, uh_ref, g_ref, cw_ref, cb_ref, wa_ref, ba_ref, wx_ref, bx_ref, lam_ref, h_ref, y_ref, carry):
        t = pl.program_id(1)

        @pl.when(t == 0)
        def _():
            carry[...] = jnp.zeros_like(carry)

        row = lax.broadcasted_iota(jnp.int32, (T, RNN_BLOCK), 0)
        grow = row + t * T
        head8 = jnp.where(t > 0, uh_ref[...], 0.0)
        _, conv, _, _, i_, _, a, _, mult = _rglru_gates(u_ref[...], head8, grow, row, T, cw_ref, cb_ref, wa_ref, ba_ref,
                                                         wx_ref, bx_ref, lam_ref)
        b = mult * i_ * conv
        s = 1
        while s < T:
            keep = row >= s
            a_s = jnp.where(keep, pltpu.roll(a, s, 0), 1.0)
            b_s = jnp.where(keep, pltpu.roll(b, s, 0), 0.0)
            b = a * b_s + b
            a = a * a_s
            s *= 2
        h = b + a * carry[0:1, :]
        carry[...] = jnp.broadcast_to(h[T - 1:T, :], carry.shape)
        h_ref[...] = h
        gv = g_ref[...]
        y_ref[...] = (h * (gv * _sigmoid(gv))).astype(BF)

    tile, halo, vec, cwspec, wblk = _rglru_specs(T, nt, False)
    return pl.pallas_call(
        body, name="rglru_fwd", grid=(RNN_BLOCKS, nt),
        in_specs=[tile, halo, tile, cwspec, vec, wblk, vec, wblk, vec, vec],
        out_specs=[tile, tile],
        out_shape=[jax.ShapeDtypeStruct((S, D_RNN), F32), jax.ShapeDtypeStruct((S, D_RNN), BF)],
        scratch_shapes=[pltpu.VMEM((8, RNN_BLOCK), F32)],
        compiler_params=_params(("parallel", "arbitrary")),
    )(xr, xr, g, cw, cb, wa, ba, wx, bx, lam)


def _rglru_bwd(xr, g, h, dy, cw, cb, wa, ba, wx, bx, lam, T):
    S = xr.shape[0]
    nt = S // T

    def body(u_ref, uh_ref, g_ref, h_ref, hh_ref, dy_ref, cw_ref, cb_ref, wa_ref, ba_ref, wx_ref, bx_ref, lam_ref,
             du_ref, dg_ref, dwa_ref, dwx_ref, dvec_ref, c_dhh, c_a, c_dconv):
        t = pl.program_id(1)
        tt = nt - 1 - t

        @pl.when(t == 0)
        def _():
            c_dhh[...] = jnp.zeros_like(c_dhh)
            c_a[...] = jnp.zeros_like(c_a)
            c_dconv[...] = jnp.zeros_like(c_dconv)
            dwa_ref[...] = jnp.zeros_like(dwa_ref)
            dwx_ref[...] = jnp.zeros_like(dwx_ref)
            dvec_ref[...] = jnp.zeros_like(dvec_ref)

        row = lax.broadcasted_iota(jnp.int32, (T, RNN_BLOCK), 0)
        row8 = row[:8]
        grow = row + tt * T
        head8 = jnp.where(tt > 0, uh_ref[...], 0.0)
        us, conv, cbf, r_, i_, sp, a, mult_raw, mult = _rglru_gates(
            u_ref[...], head8, grow, row, T, cw_ref, cb_ref, wa_ref, ba_ref, wx_ref, bx_ref, lam_ref)
        hv = h_ref[...]
        hprev = _shift_down(hv, 1, jnp.where(tt > 0, hh_ref[...], 0.0), row, T)
        gv = g_ref[...]
        sg = _sigmoid(gv)
        dyv = dy_ref[...]
        dg_ref[...] = (dyv * hv * (sg * (1.0 + gv * (1.0 - sg)))).astype(BF)
        d = dyv * (gv * sg)
        A = _shift_up(a, 1, c_a[...], row, T)
        s = 1
        while s < T:
            keep = row < T - s
            A_s = jnp.where(keep, pltpu.roll(A, T - s, 0), 1.0)
            d_s = jnp.where(keep, pltpu.roll(d, T - s, 0), 0.0)
            d = A * d_s + d
            A = A * A_s
            s *= 2
        dhh = d + A * c_dhh[0:1, :]
        da = dhh * hprev
        dconv = dhh * mult * i_
        di = dhh * mult * conv
        dmult = dhh * i_ * conv
        dla = da * a - jnp.where(grow == 0, 0.0, dmult * (a * a) / mult_raw)
        dr = dla * (-LRU_C * sp)
        dsp = jnp.sum(dla * (-LRU_C * r_), axis=0, keepdims=True)
        dza = dr * r_ * (1.0 - r_)
        dzx = di * i_ * (1.0 - i_)
        dza_b, dzx_b = dza.astype(BF), dzx.astype(BF)
        dconv = dconv + _dot(dza_b, wa_ref[0], NT) + _dot(dzx_b, wx_ref[0], NT)
        dwa_ref[0] += _dot(cbf, dza_b, TN)
        dwx_ref[0] += _dot(cbf, dzx_b, TN)
        lam = lam_ref[...]
        rows = [jnp.sum(dconv * us[k], axis=0, keepdims=True) for k in range(CONV_W)]
        rows += [jnp.sum(dconv, axis=0, keepdims=True), jnp.sum(dza, axis=0, keepdims=True),
                 jnp.sum(dzx, axis=0, keepdims=True), dsp * (-_sigmoid(-lam))]
        upd = jnp.zeros((8, RNN_BLOCK), F32)
        for j, rv in enumerate(rows):
            upd = upd + jnp.where(row8 == j, rv, 0.0)
        dvec_ref[...] += upd
        tail8 = c_dconv[...]
        du = dconv * cw_ref[0:1, :]
        for k in range(1, CONV_W):
            du = du + _shift_up(dconv, k, tail8, row, T) * cw_ref[k:k + 1, :]
        du_ref[...] = du.astype(BF)
        c_dhh[...] = jnp.broadcast_to(dhh[0:1, :], c_dhh.shape)
        c_a[...] = jnp.broadcast_to(a[0:1, :], c_a.shape)
        c_dconv[...] = dconv[:8]

    tile, halo, vec, cwspec, wblk = _rglru_specs(T, nt, True)
    acc8 = pl.BlockSpec((8, RNN_BLOCK), lambda n, t: (0, n))
    return pl.pallas_call(
        body, name="rglru_bwd", grid=(RNN_BLOCKS, nt),
        in_specs=[tile, halo, tile, tile, halo, tile, cwspec, vec, wblk, vec, wblk, vec, vec],
        out_specs=[tile, tile, wblk, wblk, acc8],
        out_shape=[jax.ShapeDtypeStruct((S, D_RNN), BF), jax.ShapeDtypeStruct((S, D_RNN), BF),
                   jax.ShapeDtypeStruct((RNN_BLOCKS, RNN_BLOCK, RNN_BLOCK), F32),
                   jax.ShapeDtypeStruct((RNN_BLOCKS, RNN_BLOCK, RNN_BLOCK), F32),
                   jax.ShapeDtypeStruct((8, D_RNN), F32)],
        scratch_shapes=[pltpu.VMEM((8, RNN_BLOCK), F32)] * 3,
        compiler_params=_params(("parallel", "arbitrary")),
    )(xr, xr, g, h, h, dy, cw, cb, wa, ba, wx, bx, lam)


def _rel_bucket_map():
    qi = np.arange(WINDOW)[:, None]
    kj = np.arange(2 * WINDOW)[None, :]
    dist = jnp.asarray(qi + WINDOW - kj, jnp.int32)
    n = jnp.maximum(dist, 0)
    max_exact = REL_BUCKETS // 2
    ratio = jnp.log(jnp.maximum(n, 1).astype(F32) / max_exact) / math.log(REL_MAX_DIST / max_exact)
    large = jnp.minimum(max_exact + (ratio * (REL_BUCKETS - max_exact)).astype(jnp.int32), REL_BUCKETS - 1)
    bucket = jnp.where(n < max_exact, n, large).astype(jnp.int32)
    j = np.arange(WINDOW)[None, :]
    return jnp.where(jnp.asarray(j > qi), bucket[:, :WINDOW], bucket[:, WINDOW:])


def _swa_common(n, kv_ref, bucket_ref, relb_ref, bias_scr):
    @pl.when(n == 0)
    def _():
        bk = bucket_ref[...]
        for h in range(SWA_HEADS):
            acc = jnp.zeros((WINDOW, WINDOW), F32)
            for b in range(REL_BUCKETS):
                acc = acc + jnp.where(bk == b, relb_ref[b, h], 0.0)
            bias_scr[h] = acc

    prev0 = pl.multiple_of(jnp.maximum(n - 1, 0) * WINDOW, WINDOW)
    cur0 = pl.multiple_of(n * WINDOW, WINDOW)
    kk = jnp.concatenate([kv_ref[pl.ds(prev0, WINDOW), :], kv_ref[pl.ds(cur0, WINDOW), :]], axis=0).astype(F32)
    rowi = lax.broadcasted_iota(jnp.int32, (WINDOW, WINDOW), 0)
    col = lax.broadcasted_iota(jnp.int32, (WINDOW, WINDOW), 1)
    from_prev = col > rowi
    return kk, from_prev, prev0, cur0


def _fold(full, from_prev):
    return jnp.where(from_prev, full[:, :WINDOW], full[:, WINDOW:])


def _unfold(sq, from_prev):
    return jnp.concatenate([jnp.where(from_prev, sq, 0.0), jnp.where(from_prev, 0.0, sq)], axis=1)


def _half_pair(part, kvh):
    lo = lax.broadcasted_iota(jnp.int32, part.shape, 1) < SWA_HD
    if kvh == 0:
        pa = jnp.where(lo, part, 0.0)
        pb = pltpu.roll(pa, SWA_HD, 1)
    else:
        pb = jnp.where(lo, 0.0, part)
        pa = pltpu.roll(pb, SWA_HD, 1)
    return pa.astype(BF), pb.astype(BF)


ALL_HEADS = SWA_HEADS * WINDOW


def _sink_column(sinks):
    return jnp.repeat(sinks.reshape(SWA_HEADS), WINDOW).reshape(ALL_HEADS, 1)


def _swa_operands(kk):
    return [(_half_pair(kk[:, :128], kvh), _half_pair(kk[:, 128:], kvh)) for kvh in range(SWA_KV_HEADS)]


def _swa_probs(n, q_ref, ops, bias_scr, sinkc_ref, from_prev):
    lgs = []
    for kvh in range(SWA_KV_HEADS):
        (ka, kb), _ = ops[kvh]
        for p in range(4):
            q2 = q_ref[:, kvh * 512 + p * 128:kvh * 512 + p * 128 + 128]
            lgs += [_fold(_dot(q2, ka, NT), from_prev), _fold(_dot(q2, kb, NT), from_prev)]
    lg = jnp.concatenate(lgs, axis=0) * (SWA_HD ** -0.5) + bias_scr[...].reshape(ALL_HEADS, WINDOW)
    rowi = jnp.bitwise_and(lax.broadcasted_iota(jnp.int32, (ALL_HEADS, WINDOW), 0), WINDOW - 1)
    col = lax.broadcasted_iota(jnp.int32, (ALL_HEADS, WINDOW), 1)
    no_prev = jnp.where(n > 0, 0, 4 * WINDOW)
    lg = jnp.where(jnp.logical_or(col <= rowi, col > rowi + no_prev), lg, NEG_INF)
    sink = sinkc_ref[...]
    m = jnp.maximum(jnp.max(lg, axis=-1, keepdims=True), sink)
    e = jnp.exp(lg - m)
    es = jnp.exp(sink - m)
    den = jnp.sum(e, axis=-1, keepdims=True) + es
    return e / den, es / den


def _swa_fwd(q, kv, g, bucket, rel_bias, sink_col):
    S = q.shape[0]
    nb = S // WINDOW

    def body(q_ref, kv_ref, g_ref, bucket_ref, relb_ref, sinkc_ref, o_ref, y_ref, bias_scr):
        n = pl.program_id(0)
        kk, from_prev, _, _ = _swa_common(n, kv_ref, bucket_ref, relb_ref, bias_scr)
        ops = _swa_operands(kk)
        pr, _ = _swa_probs(n, q_ref, ops, bias_scr, sinkc_ref, from_prev)
        for kvh in range(SWA_KV_HEADS):
            _, (va, vb) = ops[kvh]
            for p in range(4):
                c0 = kvh * 512 + p * 128
                r0 = (kvh * 8 + 2 * p) * WINDOW
                o2 = (_dot(_unfold(pr[r0:r0 + WINDOW], from_prev).astype(BF), va, NN)
                      + _dot(_unfold(pr[r0 + WINDOW:r0 + 2 * WINDOW], from_prev).astype(BF), vb, NN))
                o_ref[:, c0:c0 + 128] = o2
                gv = g_ref[:, c0:c0 + 128]
                y_ref[:, c0:c0 + 128] = (o2 * (gv * _sigmoid(gv))).astype(BF)

    blk = pl.BlockSpec((WINDOW, 1024), lambda n: (n, 0))
    smem = pl.BlockSpec(memory_space=pltpu.SMEM)
    sinkc = pl.BlockSpec((ALL_HEADS, 1), lambda n: (0, 0))
    return pl.pallas_call(
        body, name="swa_fwd", grid=(nb,),
        in_specs=[blk, pl.BlockSpec((S, 256), lambda n: (0, 0)), blk, pl.BlockSpec((WINDOW, WINDOW), lambda n: (0, 0)), smem, sinkc],
        out_specs=[blk, blk],
        out_shape=[jax.ShapeDtypeStruct((S, 1024), F32), jax.ShapeDtypeStruct((S, 1024), BF)],
        scratch_shapes=[pltpu.VMEM((SWA_HEADS, WINDOW, WINDOW), F32)],
        compiler_params=_params(("arbitrary",)),
    )(q, kv, g, bucket, rel_bias, sink_col)


def _swa_bwd(q, kv, g, o, dy, bucket, rel_bias, sink_col):
    S = q.shape[0]
    nb = S // WINDOW

    def body(q_ref, kv_ref, g_ref, o_ref, dy_ref, bucket_ref, relb_ref, sinkc_ref,
             dq_ref, dg_ref, dkv_ref, dsink_ref, drel_ref, bias_scr, dbias_scr, dsink_scr):
        n = pl.program_id(0)

        @pl.when(n == 0)
        def _():
            dbias_scr[...] = jnp.zeros_like(dbias_scr)
            dsink_scr[...] = jnp.zeros_like(dsink_scr)
            dkv_ref[...] = jnp.zeros_like(dkv_ref)

        kk, from_prev, prev0, cur0 = _swa_common(n, kv_ref, bucket_ref, relb_ref, bias_scr)
        ops = _swa_operands(kk)
        pr, ps = _swa_probs(n, q_ref, ops, bias_scr, sinkc_ref, from_prev)
        do2s, dps = [], []
        for kvh in range(SWA_KV_HEADS):
            _, (va, vb) = ops[kvh]
            for p in range(4):
                c0 = kvh * 512 + p * 128
                gv = g_ref[:, c0:c0 + 128]
                sg = _sigmoid(gv)
                dyv = dy_ref[:, c0:c0 + 128]
                dg_ref[:, c0:c0 + 128] = (dyv * o_ref[:, c0:c0 + 128] * (sg * (1.0 + gv * (1.0 - sg)))).astype(BF)
                do2 = (dyv * (gv * sg)).astype(BF)
                do2s.append(do2)
                dps += [_fold(_dot(do2, va, NT), from_prev), _fold(_dot(do2, vb, NT), from_prev)]
        dp = jnp.concatenate(dps, axis=0)
        delta = jnp.sum(pr * dp, axis=-1, keepdims=True)
        ds = pr * (dp - delta)
        dbias_scr[...] += ds.reshape(SWA_HEADS, WINDOW, WINDOW)
        dsink_scr[...] += ps * delta
        dsc = ds * (SWA_HD ** -0.5)
        lo256 = lax.broadcasted_iota(jnp.int32, (2 * WINDOW, 128), 1) < SWA_HD
        dks, dvs = [], []
        for kvh in range(SWA_KV_HEADS):
            (ka, kb), _ = ops[kvh]
            dka = jnp.zeros((2 * WINDOW, 128), F32)
            dkb, dva, dvb = dka, dka, dka
            for p in range(4):
                c0 = kvh * 512 + p * 128
                r0 = (kvh * 8 + 2 * p) * WINDOW
                q2 = q_ref[:, c0:c0 + 128]
                do2 = do2s[kvh * 4 + p]
                ds0 = _unfold(dsc[r0:r0 + WINDOW], from_prev).astype(BF)
                ds1 = _unfold(dsc[r0 + WINDOW:r0 + 2 * WINDOW], from_prev).astype(BF)
                dq_ref[:, c0:c0 + 128] = (_dot(ds0, ka, NN) + _dot(ds1, kb, NN)).astype(BF)
                dka = dka + _dot(ds0, q2, TN)
                dkb = dkb + _dot(ds1, q2, TN)
                dva = dva + _dot(_unfold(pr[r0:r0 + WINDOW], from_prev).astype(BF), do2, TN)
                dvb = dvb + _dot(_unfold(pr[r0 + WINDOW:r0 + 2 * WINDOW], from_prev).astype(BF), do2, TN)
            dks.append(jnp.where(lo256, dka, 0.0) + pltpu.roll(jnp.where(lo256, 0.0, dkb), SWA_HD, 1))
            dvs.append(jnp.where(lo256, dva, 0.0) + pltpu.roll(jnp.where(lo256, 0.0, dvb), SWA_HD, 1))
        dk = dks[0] + pltpu.roll(dks[1], SWA_HD, 1)
        dv = dvs[0] + pltpu.roll(dvs[1], SWA_HD, 1)
        dkv_ref[pl.ds(prev0, WINDOW), 0:128] += dk[:WINDOW]
        dkv_ref[pl.ds(prev0, WINDOW), 128:256] += dv[:WINDOW]
        dkv_ref[pl.ds(cur0, WINDOW), 0:128] += dk[WINDOW:]
        dkv_ref[pl.ds(cur0, WINDOW), 128:256] += dv[WINDOW:]

        @pl.when(n == nb - 1)
        def _():
            dsink_ref[...] = -jnp.sum(dsink_scr[...].reshape(SWA_HEADS, WINDOW, 1), axis=1)
            bk = bucket_ref[...]
            sums = []
            for b in range(REL_BUCKETS):
                sums.append(jnp.sum(jnp.where((bk == b)[None], dbias_scr[...], 0.0), axis=1))
            drel_ref[...] = jnp.sum(jnp.concatenate(sums, axis=0), axis=1, keepdims=True)

    blk = pl.BlockSpec((WINDOW, 1024), lambda n: (n, 0))
    smem = pl.BlockSpec(memory_space=pltpu.SMEM)
    whole = lambda shape: pl.BlockSpec(shape, lambda n: (0, 0))
    return pl.pallas_call(
        body, name="swa_bwd", grid=(nb,),
        in_specs=[blk, whole((S, 256)), blk, blk, blk, whole((WINDOW, WINDOW)), smem, whole((ALL_HEADS, 1))],
        out_specs=[blk, blk, whole((S, 256)), whole((SWA_HEADS, 1)), whole((REL_BUCKETS * SWA_HEADS, 1))],
        out_shape=[jax.ShapeDtypeStruct((S, 1024), BF), jax.ShapeDtypeStruct((S, 1024), BF),
                   jax.ShapeDtypeStruct((S, 256), F32), jax.ShapeDtypeStruct((SWA_HEADS, 1), F32),
                   jax.ShapeDtypeStruct((REL_BUCKETS * SWA_HEADS, 1), F32)],
        scratch_shapes=[pltpu.VMEM((SWA_HEADS, WINDOW, WINDOW), F32), pltpu.VMEM((SWA_HEADS, WINDOW, WINDOW), F32),
                        pltpu.VMEM((ALL_HEADS, 1), F32)],
        compiler_params=_params(("arbitrary",)),
    )(q, kv, g, o, dy, bucket, rel_bias, sink_col)


def _mem_probs(qh, mk):
    lg = _dot(qh, mk, NT) * (MEM_HD ** -0.5)
    e = jnp.exp(lg - jnp.max(lg, axis=-1, keepdims=True))
    return e / jnp.sum(e, axis=-1, keepdims=True)


def _mem_fwd(q, mkv, g):
    S = q.shape[0]
    M = mkv.shape[0]
    tq = 256

    def body(q_ref, mkv_ref, g_ref, o_ref, y_ref):
        for h in range(MEM_HEADS):
            c0 = h * MEM_HD
            pr = _mem_probs(q_ref[:, c0:c0 + MEM_HD], mkv_ref[:, c0:c0 + MEM_HD])
            o = _dot(pr.astype(BF), mkv_ref[:, D_MEM + c0:D_MEM + c0 + MEM_HD], NN)
            o_ref[:, c0:c0 + MEM_HD] = o
            gv = g_ref[:, c0:c0 + MEM_HD]
            y_ref[:, c0:c0 + MEM_HD] = (o * (gv * _sigmoid(gv))).astype(BF)

    blk = pl.BlockSpec((tq, D_MEM), lambda i: (i, 0))
    return pl.pallas_call(
        body, name="mem_fwd", grid=(S // tq,),
        in_specs=[blk, pl.BlockSpec((M, 2 * D_MEM), lambda i: (0, 0)), blk], out_specs=[blk, blk],
        out_shape=[jax.ShapeDtypeStruct((S, D_MEM), F32), jax.ShapeDtypeStruct((S, D_MEM), BF)],
        compiler_params=_params(("parallel",)),
    )(q, mkv, g)


def _mem_bwd(q, mkv, g, o, dy):
    S = q.shape[0]
    M = mkv.shape[0]
    tq = 256

    def body(q_ref, mkv_ref, g_ref, o_ref, dy_ref, dq_ref, dg_ref, dmkv_ref):
        @pl.when(pl.program_id(0) == 0)
        def _():
            dmkv_ref[...] = jnp.zeros_like(dmkv_ref)

        for h in range(MEM_HEADS):
            c0 = h * MEM_HD
            qh = q_ref[:, c0:c0 + MEM_HD]
            mk = mkv_ref[:, c0:c0 + MEM_HD]
            mv = mkv_ref[:, D_MEM + c0:D_MEM + c0 + MEM_HD]
            gv = g_ref[:, c0:c0 + MEM_HD]
            sg = _sigmoid(gv)
            dyv = dy_ref[:, c0:c0 + MEM_HD]
            dg_ref[:, c0:c0 + MEM_HD] = (dyv * o_ref[:, c0:c0 + MEM_HD] * (sg * (1.0 + gv * (1.0 - sg)))).astype(BF)
            do = (dyv * (gv * sg)).astype(BF)
            pr = _mem_probs(qh, mk)
            dp = _dot(do, mv, NT)
            ds = pr * (dp - jnp.sum(pr * dp, axis=-1, keepdims=True))
            dsb = (ds * (MEM_HD ** -0.5)).astype(BF)
            dq_ref[:, c0:c0 + MEM_HD] = _dot(dsb, mk, NN).astype(BF)
            dmkv_ref[:, c0:c0 + MEM_HD] += _dot(dsb, qh, TN)
            dmkv_ref[:, D_MEM + c0:D_MEM + c0 + MEM_HD] += _dot(pr.astype(BF), do, TN)

    blk = pl.BlockSpec((tq, D_MEM), lambda i: (i, 0))
    whole = pl.BlockSpec((M, 2 * D_MEM), lambda i: (0, 0))
    return pl.pallas_call(
        body, name="mem_bwd", grid=(S // tq,),
        in_specs=[blk, whole, blk, blk, blk], out_specs=[blk, blk, whole],
        out_shape=[jax.ShapeDtypeStruct((S, D_MEM), BF), jax.ShapeDtypeStruct((S, D_MEM), BF),
                   jax.ShapeDtypeStruct((M, 2 * D_MEM), F32)],
        compiler_params=_params(("arbitrary",)),
    )(q, mkv, g, o, dy)


MERGE_TN = 512


def _merge_specs(tm):
    ytile = pl.BlockSpec((tm, 1024), lambda i, j: (i, 0))
    wblk = pl.BlockSpec((MERGE_TN, 1024), lambda i, j: (j, 0))
    gls = [pl.BlockSpec((None, tm, MERGE_TN), (lambda i, j, br=br: (br, i, j))) for br in range(3)]
    otile = pl.BlockSpec((tm, MERGE_TN), lambda i, j: (i, j))
    return ytile, wblk, gls, otile


def _merge_fwd(ys, ws, gl, tm):
    S = gl.shape[1]

    def body(y0, y1, y2, w0, w1, w2, g0, g1, g2, o_ref):
        acc = None
        for y_ref, w_ref, g_ref in ((y0, w0, g0), (y1, w1, g1), (y2, w2, g2)):
            term = _sigmoid(g_ref[...]) * _dot(y_ref[...], w_ref[...], NT)
            acc = term if acc is None else acc + term
        o_ref[...] = acc.astype(BF)

    ytile, wblk, gls, otile = _merge_specs(tm)
    return pl.pallas_call(
        body, name="merge_fwd", grid=(S // tm, D_MODEL // MERGE_TN),
        in_specs=[ytile] * 3 + [wblk] * 3 + gls, out_specs=otile,
        out_shape=jax.ShapeDtypeStruct((S, D_MODEL), BF),
        compiler_params=_params(("parallel", "arbitrary")),
    )(*ys, *ws, gl, gl, gl)


def _merge_bwd(dout, w_out, ys, ws, gl, tm):
    S = gl.shape[1]

    def body(do_ref, wo_ref, y0, y1, y2, w0, w1, w2, g0, g1, g2, dg0, dg1, dg2, dp0, dp1, dp2):
        dm = _dot(do_ref[...], wo_ref[...], NT)
        for y_ref, w_ref, g_ref, dg_ref, dp_ref in ((y0, w0, g0, dg0, dp0), (y1, w1, g1, dg1, dp1), (y2, w2, g2, dg2, dp2)):
            gate = _sigmoid(g_ref[...])
            pv = _dot(y_ref[...], w_ref[...], NT)
            dg_ref[...] = (dm * pv * gate * (1.0 - gate)).astype(BF)
            dp_ref[...] = (dm * gate).astype(BF)

    ytile, wblk, gls, otile = _merge_specs(tm)
    out = jax.ShapeDtypeStruct((S, D_MODEL), BF)
    return pl.pallas_call(
        body, name="merge_bwd", grid=(S // tm, D_MODEL // MERGE_TN),
        in_specs=[pl.BlockSpec((tm, D_MODEL), lambda i, j: (i, 0)), pl.BlockSpec((MERGE_TN, D_MODEL), lambda i, j: (j, 0))]
        + [ytile] * 3 + [wblk] * 3 + gls,
        out_specs=[otile] * 6, out_shape=[out] * 6,
        compiler_params=_params(("parallel", "arbitrary")),
    )(dout, w_out, *ys, *ws, gl, gl, gl)


def _out_loss(merged, w_out, x, target, post_g, tm):
    S = x.shape[0]

    def body(m_ref, w_ref, x_ref, t_ref, g_ref, dout_ref, dy_ref, loss_ref, dpost_ref):
        @pl.when(pl.program_id(0) == 0)
        def _():
            loss_ref[...] = jnp.zeros_like(loss_ref)
            dpost_ref[...] = jnp.zeros_like(dpost_ref)

        out = _dot(m_ref[...], w_ref[...], NN)
        r = lax.rsqrt(jnp.mean(out * out, axis=-1, keepdims=True) + EPS)
        nrm = out * r
        gv = g_ref[...]
        err = (x_ref[...] + nrm * gv) - t_ref[...]
        sq = jnp.sum(jnp.sum(err * err, axis=1, keepdims=True), axis=0, keepdims=True)
        loss_ref[...] += sq * (0.5 / D_MODEL)
        dy = err * (1.0 / D_MODEL)
        dy_ref[...] = dy
        dpost_ref[...] += jnp.sum(dy * nrm, axis=0, keepdims=True)
        dn = dy * gv
        dout_ref[...] = (r * (dn - nrm * jnp.mean(dn * nrm, axis=-1, keepdims=True))).astype(BF)

    row = pl.BlockSpec((tm, D_MODEL), lambda i: (i, 0))
    return pl.pallas_call(
        body, name="out_loss", grid=(S // tm,),
        in_specs=[row, pl.BlockSpec((D_MODEL, D_MODEL), lambda i: (0, 0)), row, row, pl.BlockSpec((1, D_MODEL), lambda i: (0, 0))],
        out_specs=[row, row, pl.BlockSpec((8, 128), lambda i: (0, 0)), pl.BlockSpec((1, D_MODEL), lambda i: (0, 0))],
        out_shape=[jax.ShapeDtypeStruct((S, D_MODEL), BF), jax.ShapeDtypeStruct((S, D_MODEL), F32),
                   jax.ShapeDtypeStruct((8, 128), F32), jax.ShapeDtypeStruct((1, D_MODEL), F32)],
        compiler_params=_params(("arbitrary",)),
    )(merged, w_out, x, target, post_g)


def _dh_dx(dproj, w_in, x, dy, pre_g, tm):
    S = x.shape[0]
    nk, tk = dproj.shape[0], dproj.shape[2]

    def body(dp_ref, w_ref, x_ref, dy_ref, g_ref, dx_ref, dpre_ref, acc_ref):
        i, k = pl.program_id(0), pl.program_id(1)

        @pl.when(jnp.logical_and(i == 0, k == 0))
        def _():
            dpre_ref[...] = jnp.zeros_like(dpre_ref)

        @pl.when(k == 0)
        def _():
            acc_ref[...] = jnp.zeros_like(acc_ref)

        acc_ref[...] += _dot(dp_ref[...], w_ref[...], NN)

        @pl.when(k == nk - 1)
        def _():
            dh = acc_ref[...]
            xv = x_ref[...]
            r = lax.rsqrt(jnp.mean(xv * xv, axis=-1, keepdims=True) + EPS)
            nrm = xv * r
            dpre_ref[...] += jnp.sum(dh * nrm, axis=0, keepdims=True)
            dn = dh * g_ref[...]
            dx_ref[...] = r * (dn - nrm * jnp.mean(dn * nrm, axis=-1, keepdims=True)) + dy_ref[...]

    row = pl.BlockSpec((tm, D_MODEL), lambda i, k: (i, 0))
    vec = pl.BlockSpec((1, D_MODEL), lambda i, k: (0, 0))
    return pl.pallas_call(
        body, name="dh_dx", grid=(S // tm, nk),
        in_specs=[pl.BlockSpec((None, tm, tk), lambda i, k: (k, i, 0)), pl.BlockSpec((tk, D_MODEL), lambda i, k: (k, 0)), row, row, vec],
        out_specs=[row, vec],
        out_shape=[jax.ShapeDtypeStruct((S, D_MODEL), F32), jax.ShapeDtypeStruct((1, D_MODEL), F32)],
        scratch_shapes=[pltpu.VMEM((tm, D_MODEL), F32)],
        compiler_params=_params(("arbitrary", "arbitrary")),
    )(dproj, w_in, x, dy, pre_g)


def _sum_parts(parts, name):
    P, R, C = parts.shape
    tr = max(t for t in range(8, 513, 8) if R % t == 0)

    def body(p_ref, o_ref):
        acc = p_ref[0]
        for j in range(1, P):
            acc = acc + p_ref[j]
        o_ref[...] = acc

    return pl.pallas_call(
        body, name=name, grid=(R // tr,),
        in_specs=[pl.BlockSpec((P, tr, C), lambda i: (0, i, 0))], out_specs=pl.BlockSpec((tr, C), lambda i: (i, 0)),
        out_shape=jax.ShapeDtypeStruct((R, C), F32), compiler_params=_params(("parallel",)),
    )(parts)


def _adamw(parts, w, m, v, name, own=None, chip=None):
    groups = list(parts) if isinstance(parts, (list, tuple)) else [parts]
    owns = [] if own is None else (list(own) if isinstance(own, (list, tuple)) else [own])
    P, rows, C = groups[0].shape
    R = rows * len(groups)
    tr = 128 if rows % 128 == 0 else rows
    per = rows // tr
    c1 = 1.0 - ADAM_B1 ** ADAM_STEP
    c2 = 1.0 - ADAM_B2 ** ADAM_STEP
    n_in = len(groups) * (4 if owns else 1)

    def body(*refs):
        if owns:
            refs = refs[1:]
        p_refs = refs[:n_in]
        w_ref, m_ref, v_ref, g_ref, d_ref, nm_ref, nv_ref = refs[n_in:]
        g = None
        for q in range(len(groups)):
            if owns:
                gq = p_refs[4 * q + 3][...].astype(F32)
                for j in range(3):
                    gq = gq + p_refs[4 * q + j][...].astype(F32)
            else:
                gq = p_refs[q][0].astype(F32)
                for j in range(1, P):
                    gq = gq + p_refs[q][j].astype(F32)
            g = gq if g is None else jnp.where(pl.program_id(0) // per == q, gq, g)
        nm = ADAM_B1 * m_ref[...] + (1.0 - ADAM_B1) * g
        nv = ADAM_B2 * v_ref[...] + (1.0 - ADAM_B2) * (g * g)
        g_ref[...] = g
        nm_ref[...] = nm
        nv_ref[...] = nv
        d_ref[...] = -ADAM_LR * ((nm / c1) / (jnp.sqrt(nv / c2) + ADAM_EPS) + ADAM_WD * w_ref[...])

    out = jax.ShapeDtypeStruct((R, C), F32)
    if not owns:
        tile = pl.BlockSpec((tr, C), lambda i: (i, 0))
        return pl.pallas_call(
            body, name=name, grid=(R // tr,),
            in_specs=[pl.BlockSpec((P, tr, C), (lambda i, q=q: (0, jnp.clip(i - q * per, 0, per - 1), 0))) for q in range(len(groups))]
            + [tile, tile, tile], out_specs=[tile] * 4, out_shape=[out] * 4,
            compiler_params=_params(("parallel",)),
        )(*groups, w, m, v)
    tile = pl.BlockSpec((tr, C), lambda i, c_ref: (i, 0))
    specs, operands = [], []
    for q in range(len(groups)):
        for k in range(3):
            specs.append(pl.BlockSpec((None, tr, C), (lambda i, c_ref, q=q, k=k: (k + (c_ref[0] <= k).astype(jnp.int32),
                                                                                  jnp.clip(i - q * per, 0, per - 1), 0))))
            operands.append(groups[q])
        specs.append(pl.BlockSpec((None, tr, C), (lambda i, c_ref, q=q: (c_ref[0], jnp.clip(i - q * per, 0, per - 1), 0))))
        operands.append(owns[q])
    return pl.pallas_call(
        body, name=name,
        grid_spec=pltpu.PrefetchScalarGridSpec(num_scalar_prefetch=1, grid=(R // tr,), in_specs=specs + [tile, tile, tile],
                                               out_specs=[tile] * 4),
        out_shape=[out] * 4, compiler_params=_params(("parallel",)),
    )(chip, *operands, w, m, v)


def _adamw_small(gs, ws, ms, vs):
    n = len(ws)
    c1 = 1.0 - ADAM_B1 ** ADAM_STEP
    c2 = 1.0 - ADAM_B2 ** ADAM_STEP

    def flat2(a):
        return a.reshape(-1, a.shape[-1])

    def body(*refs):
        ins, outs = refs[:4 * n], refs[4 * n:]
        for a in range(n):
            g, w, m, v = (ins[k * n + a][...] for k in range(4))
            nm = ADAM_B1 * m + (1.0 - ADAM_B1) * g
            nv = ADAM_B2 * v + (1.0 - ADAM_B2) * (g * g)
            outs[a][...] = g
            outs[n + a][...] = -ADAM_LR * ((nm / c1) / (jnp.sqrt(nv / c2) + ADAM_EPS) + ADAM_WD * w)
            outs[2 * n + a][...] = nm
            outs[3 * n + a][...] = nv

    shapes = [flat2(w).shape for w in ws]
    out = pl.pallas_call(
        body, name="adamw_small", out_shape=[jax.ShapeDtypeStruct(sh, F32) for sh in shapes] * 4,
        compiler_params=_params(),
    )(*[g.reshape(sh) for g, sh in zip(gs, shapes)], *[flat2(a) for a in (*ws, *ms, *vs)])
    return [[out[k * n + a].reshape(ws[a].shape) for a in range(n)] for k in range(4)]


def _project(h, w_t):
    S = h.shape[0]
    n_tiles = D_IN // SEG_TILE
    ranges = [(c0 // SEG_TILE, (c0 + width) // SEG_TILE) for _, c0, width, _ in SEGMENTS]

    def body(h_ref, w_ref, *outs):
        j = pl.program_id(0)
        prod = _dot(h_ref[...], w_ref[...], NT)
        for (j0, j1), (_, _, _, dt), o_ref in zip(ranges, SEGMENTS, outs):
            @pl.when(jnp.logical_and(j >= j0, j < j1))
            def _(o_ref=o_ref, dt=dt):
                o_ref[...] = prod.astype(dt)

    out_shapes, out_specs = [], []
    for (j0, j1), (name, _, width, dt) in zip(ranges, SEGMENTS):
        if name == "gl":
            per = (j1 - j0) // 3
            out_shapes.append(jax.ShapeDtypeStruct((3, S, width // 3), dt))
            out_specs.append(pl.BlockSpec((None, S, SEG_TILE), (lambda j, j0=j0, j1=j1, per=per: (
                jnp.clip(j - j0, 0, j1 - j0 - 1) // per, 0, jnp.clip(j - j0, 0, j1 - j0 - 1) % per))))
        else:
            out_shapes.append(jax.ShapeDtypeStruct((S, width), dt))
            out_specs.append(pl.BlockSpec((S, SEG_TILE), (lambda j, j0=j0, j1=j1: (0, jnp.clip(j - j0, 0, j1 - j0 - 1)))))
    outs = pl.pallas_call(
        body, name="proj", grid=(n_tiles,),
        in_specs=[pl.BlockSpec((S, D_MODEL), lambda j: (0, 0)), pl.BlockSpec((SEG_TILE, D_MODEL), lambda j: (j, 0))],
        out_specs=out_specs, out_shape=out_shapes,
        compiler_params=_params(("arbitrary",)),
    )(h, w_t)
    return {name: o for (name, _, _, _), o in zip(SEGMENTS, outs)}


def _forward_a(x, mem, pre_g, mem_g, w_in, conv_w, conv_b, w_a, b_a, w_x, b_x, lam, sinks, rel_bias):
    S = x.shape[0]
    st = dict(T=min(512, S // 2), tm=min(512, S), bucket=_rel_bucket_map())
    st["h"] = _rms_fwd(x, pre_g, "pre_norm")
    st["memn"] = _rms_fwd(mem, mem_g, "mem_norm")
    seg = st["seg"] = _project(st["h"], w_in)
    st["h_rg"], st["y_rg"] = _rglru_fwd(seg["xr"], seg["g_rg"], conv_w, conv_b, w_a, b_a, w_x, b_x, lam, st["T"])
    st["o_swa"], st["y_swa"] = _swa_fwd(seg["q_s"], seg["kv"], seg["g_swa"], st["bucket"], rel_bias, _sink_column(sinks))
    return st


def _forward_b(st, x, target, post_g, w_memkv, wbr, w_out):
    S = x.shape[0]
    M = st["memn"].shape[0]
    seg = st["seg"]
    st["mkv"] = _matmul(st["memn"], w_memkv, "nn", M, 2 * D_MEM, D_MODEL, M, 512, D_MODEL, BF, "mem_kv")
    st["o_mem"], st["y_mem"] = _mem_fwd(seg["q_m"], st["mkv"], seg["g_mem"])
    st["ys"] = (st["y_rg"], st["y_swa"], st["y_mem"])
    st["merged"] = _merge_fwd(st["ys"], wbr, seg["gl"], st["tm"])
    st["dout"], st["dy"], st["loss"], st["dpost"] = _out_loss(st["merged"], w_out, x, target, post_g, min(256, S))
    return st


def _backward_a1(st, wbr, w_out):
    S = st["h"].shape[0]
    seg, ys, tm = st["seg"], st["ys"], st["tm"]
    st["dw_out"] = _matmul(st["merged"], st["dout"], "tn", D_MODEL, D_MODEL, S, 256, D_MODEL, S, BF, "dw_out", out_blocked="row")
    dgl0, dgl1, dgl2, dp0, dp1, dp2 = _merge_bwd(st["dout"], w_out, ys, wbr, seg["gl"], tm)
    st["dgl"] = (dgl0, dgl1, dgl2)
    dys, dwbr = [], []
    for i, dp in enumerate((dp0, dp1, dp2)):
        dys.append(_matmul(dp, wbr[i], "nn", S, 1024, D_MODEL, tm, 1024, D_MODEL, F32, "dy_br%d" % i))
        dwbr.append(_matmul(ys[i], dp, "tn", 1024, D_MODEL, S, 1024, 256, S, BF, "dw_br%d" % i, out_blocked="col"))
    st["dys"], st["dwbr"] = dys, dwbr
    return st


def _backward_a2(st, mem, w_memkv, conv_w, conv_b, w_a, b_a, w_x, b_x, lam):
    M = mem.shape[0]
    seg, dys = st["seg"], st["dys"]
    st["dq_m"], st["dg_mem"], dmkv = _mem_bwd(seg["q_m"], st["mkv"], seg["g_mem"], st["o_mem"], dys[2])
    dmkv_b = dmkv.astype(BF)
    st["dw_memkv"] = _matmul(st["memn"], dmkv_b, "tn", D_MODEL, 2 * D_MEM, M, 256, 2 * D_MEM, M, BF, "dw_memkv", out_blocked="row")
    dmemn = _matmul(dmkv_b, w_memkv, "nt", M, D_MODEL, 2 * D_MEM, M, 512, 2 * D_MEM, F32, "dmemn")
    st["dmem_g"] = _rms_gain_grad(dmemn, mem, "dmem_gain")
    st["dxr"], st["dg_rg"], st["dw_a"], st["dw_x"], st["dvec"] = _rglru_bwd(
        seg["xr"], seg["g_rg"], st["h_rg"], dys[0], conv_w, conv_b, w_a, b_a, w_x, b_x, lam, st["T"])
    return st


def _backward_b(st, rel_bias, sinks):
    seg = st["seg"]
    dq_s, dg_swa, dkv, dsinks, drel = _swa_bwd(seg["q_s"], seg["kv"], seg["g_swa"], st["o_swa"], st["dys"][1],
                                               st["bucket"], rel_bias, _sink_column(sinks))
    st["dsinks"], st["drel"] = dsinks.reshape(1, SWA_HEADS), drel.reshape(REL_BUCKETS, SWA_HEADS)
    dproj = jnp.concatenate([st["dxr"], st["dg_rg"], dq_s, dkv.astype(BF), dg_swa, st["dq_m"], st["dg_mem"], *st["dgl"]], axis=1)
    st["dproj"] = jnp.transpose(dproj.reshape(dproj.shape[0], N_DEV, D_IN // N_DEV), (1, 0, 2))
    return st


def _dw_in_half(st, half, dep=None):
    S = st["h"].shape[0]
    return _matmul(st["h"], st["dproj"], "tn", D_MODEL // 2, D_IN, S, 512, D_IN // N_DEV, S, BF, "dw_in%d" % half, a_moff=2 * half,
                   b_blocked=True, out_blocked="col", dep=dep)


def _owner_blocks(a):
    return jnp.swapaxes(a.reshape((4, 2) + a.shape[1:]), 0, 1)


def _local_step(x, mem, target, pre_g, post_g, mem_g, w_in, conv_w, conv_b, w_a, b_a, w_x, b_x, lam, sinks, rel_bias,
                w_memkv, wbr, w_out):
    st = _forward_a(x, mem, pre_g, mem_g, w_in, conv_w, conv_b, w_a, b_a, w_x, b_x, lam, sinks, rel_bias)
    st = _forward_b(st, x, target, post_g, w_memkv, wbr, w_out)
    st = _backward_a1(st, wbr, w_out)
    st = _backward_a2(st, mem, w_memkv, conv_w, conv_b, w_a, b_a, w_x, b_x, lam)
    st = _backward_b(st, rel_bias, sinks)
    st["dw_in"] = [_dw_in_half(st, 0), _dw_in_half(st, 1)]
    st["grad_x"], st["dpre"] = _dh_dx(st["dproj"], w_in, x, st["dy"], pre_g, st["tm"])
    return st


def _pad_rows(a, rows):
    a = a.reshape(-1, 128) if a.shape[-1] % 128 == 0 else jnp.pad(a, ((0, 0), (0, 128 - a.shape[-1])))
    return jnp.pad(a, ((0, rows - a.shape[0]), (0, 0))) if a.shape[0] < rows else a


def kernel(x, mem, pre_norm_g, post_norm_g, mem_norm_g, w_in, conv_w, conv_b, w_rg_a, b_rg_a, w_rg_x, b_rg_x, lru_lambda, swa_sinks, rel_bias, w_mem_kv, w_br_rg, w_br_swa, w_br_mem, w_out, loss_target, m_pre_norm_g, m_post_norm_g, m_mem_norm_g, m_w_in, m_conv_w, m_conv_b, m_w_rg_a, m_b_rg_a, m_w_rg_x, m_b_rg_x, m_lru_lambda, m_swa_sinks, m_rel_bias, m_w_mem_kv, m_w_br_rg, m_w_br_swa, m_w_br_mem, m_w_out, v_pre_norm_g, v_post_norm_g, v_mem_norm_g, v_w_in, v_conv_w, v_conv_b, v_w_rg_a, v_b_rg_a, v_w_rg_x, v_b_rg_x, v_lru_lambda, v_swa_sinks, v_rel_bias, v_w_mem_kv, v_w_br_rg, v_w_br_swa, v_w_br_mem, v_w_out):
    cx, cy, cc = lax.axis_index("x"), lax.axis_index("y"), lax.axis_index("c")
    me = 4 * cx + 2 * cy + cc
    chip = 2 * cx + cy
    core = jnp.reshape(cc, (1,)).astype(jnp.int32)
    x0, mem0 = x[0], mem[0]
    w_a_b, w_x_b = w_rg_a[0].astype(BF), w_rg_x[0].astype(BF)

    def landing(own, slot, slots):
        return lax.dynamic_update_slice(lax.empty((slots,) + own.shape, own.dtype), own[None], (slot,) + (0,) * own.ndim)


    def swap_start(parts, tag):
        return _exchange_start(parts, [lax.empty(p.shape[1:], p.dtype) for p in parts], _plan_swap(len(parts)), "swap_%s_start" % tag)

    def scatter_start(swap, after, tag, prefill=()):
        s_send, s_recv, parts, got, _ = swap
        got = _exchange_wait(s_send, s_recv, parts, got, _plan_swap(len(parts)), after, "swap_%s_wait" % tag)
        sums = [_pair_sum(p, g, core, "scatter_%s_sum%d" % (tag, i)) for i, (p, g) in enumerate(zip(parts, got))]
        lands = [landing(lax.dynamic_index_in_dim(s, chip, 0, keepdims=False), chip, 4) if i in prefill
                 else lax.empty(s.shape, s.dtype) for i, s in enumerate(sums)]
        return _exchange_start(sums, lands, _plan_scatter(len(sums)), "scatter_%s_start" % tag)

    def zero_after(a):
        return jnp.minimum(jnp.abs(a.reshape(-1)[0].astype(F32)), 0.0)

    g_in, g_cw = _all_gather_relayed([jnp.transpose(w_in[0]).astype(BF), conv_w[0]], [True, False], "gather_w_in")
    w_in_f = g_in.reshape(D_IN, D_MODEL)
    conv_w_f = jnp.transpose(g_cw, (1, 0, 2)).reshape(CONV_W, D_RNN)

    after_first = zero_after(g_cw).astype(BF)
    rest = [w.astype(BF) + after_first for w in (w_mem_kv[0], jnp.transpose(w_br_rg[0]), jnp.transpose(w_br_swa[0]),
                                                 jnp.transpose(w_br_mem[0]), w_out[0])]
    kinds = ["lead"] * len(rest)
    plan_g = _plan_gather(kinds)
    zones = _place_own([lax.empty((N_DEV,) + w.shape, w.dtype) for w in rest], rest, jnp.reshape(me, (1,)).astype(jnp.int32), "gather_rest_own")
    g_send, g_recv, g_src, g_land, g_token = _exchange_start(rest, zones, plan_g, "gather_rest_start")
    st = _forward_a(x0, mem0, pre_norm_g + g_token[0:1, 0:1], mem_norm_g, w_in_f, conv_w_f, conv_b, w_a_b, b_rg_a, w_x_b, b_rg_x,
                    lru_lambda, swa_sinks, rel_bias)
    g_land = _exchange_wait(g_send, g_recv, g_src, g_land, plan_g, st["y_swa"], "gather_rest_wait")
    g_land = _forward_to_sibling(g_land, kinds, "gather_rest_forward")
    w_memkv_f = g_land[0].reshape(D_MODEL, 2 * D_MEM)
    wbr = tuple(g_land[i].reshape(D_MODEL, D_RNN) for i in (1, 2, 3))
    w_out_f = g_land[4].reshape(D_MODEL, D_MODEL)

    st = _forward_b(st, x0, loss_target[0], post_norm_g, w_memkv_f, wbr, w_out_f)
    st = _backward_a1(st, wbr, w_out_f)
    parts_a = [st["dw_out"], st["dwbr"][0], st["dwbr"][1], st["dwbr"][2]]
    plan_a = _plan_scatter(len(parts_a))
    swap_a = swap_start(parts_a, "a")
    st = _backward_a2(st, mem0, w_memkv_f, conv_w_f, conv_b + swap_a[4][0:1, 0:1], w_a_b, b_rg_a, w_x_b, b_rg_x, lru_lambda)
    a_send, a_recv, a_src, a_land, a_token = scatter_start(swap_a, st["dxr"], "a")
    parts_c = [st["dw_memkv"], _owner_blocks(st["dw_a"]), _owner_blocks(st["dw_x"])]
    plan_c = _plan_scatter(len(parts_c))
    swap_c = swap_start(parts_c, "c")

    st = _backward_b(st, rel_bias, swa_sinks + swap_c[4][0:1, 0:1] + a_token[0:1, 0:1])
    c_send, c_recv, c_src, c_land, c_token = scatter_start(swap_c, st["dproj"], "c", prefill=(1, 2))
    plan_b = _plan_scatter(1)

    def dw_in_parts(half, dep):
        dwh = _dw_in_half(st, half, dep)
        return dwh, [dwh]

    dw0, parts_b0 = dw_in_parts(0, c_token)
    swap_b0 = swap_start(parts_b0, "b0")
    a_land = _exchange_wait(a_send, a_recv, a_src, a_land, plan_a, swap_b0[4], "scatter_a_wait")
    big = [None] * 6

    chip1 = jnp.reshape(chip, (1,)).astype(jnp.int32)

    def adamw_big(j, land, own, wt, mt, vt):
        big[j] = [a[None] for a in _adamw(land, wt[0], mt[0], vt[0], "adamw_big%d" % j, own=own, chip=chip1)]

    adamw_big(5, a_land[0], a_src[0], w_out, m_w_out, v_w_out)
    adamw_big(2, a_land[1], a_src[1], w_br_rg, m_w_br_rg, v_w_br_rg)
    adamw_big(3, a_land[2], a_src[2], w_br_swa, m_w_br_swa, v_w_br_swa)
    halves = [scatter_start(swap_b0, big[3][1], "b0")]
    dw1, parts_b1 = dw_in_parts(1, halves[0][4])
    swap_b1 = swap_start(parts_b1, "b1")
    c_land = _exchange_wait(c_send, c_recv, c_src, c_land, plan_c, swap_b1[4], "scatter_c_wait")
    g_wa_blk = _sum_parts(c_land[1], "sum_w_rg_a")
    g_wx_blk = _sum_parts(c_land[2], "sum_w_rg_x")
    adamw_big(4, a_land[3], a_src[3], w_br_mem, m_w_br_mem, v_w_br_mem)
    adamw_big(1, c_land[0], c_src[0], w_mem_kv, m_w_mem_kv, v_w_mem_kv)
    halves.append(scatter_start(swap_b1, big[1][1], "b1"))
    grad_x, dpre = _dh_dx(st["dproj"], w_in_f, x0, st["dy"], pre_norm_g + halves[1][4][0:1, 0:1], st["tm"])
    after, b_lands, b_sums = grad_x, [], []
    for half, (b_send, b_recv, b_src, b_land, _) in enumerate(halves):
        b_lands.append(_exchange_wait(b_send, b_recv, b_src, b_land, plan_b, after, "scatter_b%d_wait" % half)[0])
        b_sums.append(b_src[0])
        after = b_lands[-1]
    big[0] = [a[None] for a in _adamw(b_lands, w_in[0], m_w_in[0], v_w_in[0], "adamw_big0", own=b_sums, chip=chip1)]
    links_free = jnp.minimum(jnp.abs(big[0][0][0, 0, 0]), 0.0)

    pack = jnp.concatenate([dpre.reshape(16, 128), st["dpost"].reshape(16, 128), st["dmem_g"].reshape(16, 128),
                            st["dvec"].reshape(64, 128), _pad_rows(st["dsinks"], 8), _pad_rows(st["drel"], 32), g_wa_blk, g_wx_blk], axis=0) + links_free
    gathered = _all_gather([pack], "gather_small")[0]
    gs = _sum_parts(gathered, "sum_small")
    g_pre, g_post, g_memg = gs[0:16].reshape(1, D_MODEL), gs[16:32].reshape(1, D_MODEL), gs[32:48].reshape(1, D_MODEL)
    gvec = gs[48:112].reshape(8, D_RNN)
    g_conv_w = lax.dynamic_slice(gvec[0:CONV_W], (0, me * RNN_BLOCK), (CONV_W, RNN_BLOCK))
    g_conv_b, g_b_a, g_b_x, g_lam = gvec[4:5], gvec[5:6], gvec[6:7], gvec[7:8]
    g_sinks = gs[112:113, :SWA_HEADS]
    g_rel = gs[120:152, :SWA_HEADS]
    g_w_a = gathered[:, 152:280]
    g_w_x = gathered[:, 280:408]

    g_small = (g_pre, g_post, g_memg, g_conv_b, g_b_a, g_b_x, g_lam, g_w_a, g_w_x, g_sinks, g_rel, g_conv_w)
    w_small = (pre_norm_g, post_norm_g, mem_norm_g, conv_b, b_rg_a, b_rg_x, lru_lambda, w_rg_a, w_rg_x, swa_sinks, rel_bias, conv_w)
    m_small = (m_pre_norm_g, m_post_norm_g, m_mem_norm_g, m_conv_b, m_b_rg_a, m_b_rg_x, m_lru_lambda, m_w_rg_a, m_w_rg_x, m_swa_sinks, m_rel_bias, m_conv_w)
    v_small = (v_pre_norm_g, v_post_norm_g, v_mem_norm_g, v_conv_b, v_b_rg_a, v_b_rg_x, v_lru_lambda, v_w_rg_a, v_w_rg_x, v_swa_sinks, v_rel_bias, v_conv_w)
    sm = _adamw_small(g_small, w_small, m_small, v_small)

    loss_total = lax.psum(st["loss"][0, 0], AXES)

    def leaves(k):
        s = sm[k]
        return [s[0], s[1], s[2], big[0][k], s[11], s[3], s[7], s[4], s[8], s[5], s[6], s[9], s[10],
                big[1][k], big[2][k], big[3][k], big[4][k], big[5][k]]

    return (loss_total, grad_x[None], *leaves(0), *leaves(1), *leaves(2), *leaves(3))
```

```python
import math

import jax
import jax.numpy as jnp
import numpy as np
from jax import lax
from jax.experimental import pallas as pl
from jax.experimental.pallas import tpu as pltpu

F32, BF = jnp.float32, jnp.bfloat16
MESH = pl.DeviceIdType.MESH
AXES = ("x", "y", "c")
N_DEV = 8

D_MODEL = 2048
D_RNN = 1024
RNN_BLOCKS = 8
RNN_BLOCK = 128
CONV_W = 4
LRU_C = 8.0
SWA_HEADS = 16
SWA_KV_HEADS = 2
SWA_HD = 64
WINDOW = 128
MEM_HEADS = 4
MEM_HD = 256
D_MEM = 1024
REL_BUCKETS = 32
REL_MAX_DIST = 128
EPS = 1e-6
NEG_INF = -1e30
D_IN = 12544
SEGMENTS = (("xr", 0, 1024, F32), ("g_rg", 1024, 1024, F32), ("q_s", 2048, 1024, BF), ("kv", 3072, 256, BF),
            ("g_swa", 3328, 1024, F32), ("q_m", 4352, 1024, BF), ("g_mem", 5376, 1024, F32), ("gl", 6400, 6144, F32))
SEG_TILE = 256

ADAM_LR, ADAM_B1, ADAM_B2, ADAM_EPS, ADAM_WD, ADAM_STEP = 0.001, 0.9, 0.999, 1e-08, 0.01, 10

NN = (((1,), (0,)), ((), ()))
NT = (((1,), (1,)), ((), ()))
TN = (((0,), (0,)), ((), ()))
MIB = 2 ** 20


def _dot(a, b, dn):
    return lax.dot_general(a, b, dn, preferred_element_type=F32)


VMEM_LIMIT_MIB = 48
VMEM_LIMIT_LARGE_MIB = 56


def _params(sem=None, large=False):
    return pltpu.CompilerParams(dimension_semantics=sem, vmem_limit_bytes=(VMEM_LIMIT_LARGE_MIB if large else VMEM_LIMIT_MIB) * MIB)


def _sigmoid(z):
    return 1.0 / (1.0 + jnp.exp(-z))


def _softplus(z):
    return jnp.maximum(z, 0.0) + jnp.log(1.0 + jnp.exp(-jnp.abs(z)))


def _expm1(z):
    p = z * (1.0 + z * (0.5 + z * (1.0 / 6 + z * (1.0 / 24 + z * (1.0 / 120 + z * (1.0 / 720 + z * (1.0 / 5040 + z / 40320)))))))
    return jnp.where(jnp.abs(z) < 0.3, p, jnp.exp(z) - 1.0)


def _flat(p):
    return 4 * p[0] + 2 * p[1] + p[2]


def _all_gather(arrs, name):
    n = len(arrs)

    def body(*refs):
        ins, outs = refs[:n], refs[n:2 * n]
        send_sems, recv_sems, local_sems = refs[2 * n:]
        x, y, c = lax.axis_index("x"), lax.axis_index("y"), lax.axis_index("c")
        me, sibling = (x, y, c), (x, y, 1 - c)
        chips = [(1 - x, y), (x, 1 - y), (1 - x, 1 - y)]

        def copy(a, k, block, to, src=None):
            dst = outs[a].at[_flat(block)]
            return pltpu.make_async_remote_copy(src_ref=dst if src is None else src, dst_ref=dst,
                                                send_sem=send_sems.at[a * 7 + k], recv_sem=recv_sems.at[a * 7 + k],
                                                device_id=to, device_id_type=MESH)

        mine = [pltpu.make_async_copy(ins[a], outs[a].at[_flat(me)], local_sems.at[a]) for a in range(n)]
        for cp in mine:
            cp.start()
        first = []
        for a in range(n):
            first += [copy(a, 1 + j, me, (*chip, c), src=ins[a]) for j, chip in enumerate(chips)]
            first.append(copy(a, 0, me, sibling, src=ins[a]))
        for cp in first:
            cp.start()
        passed = []
        for j, chip in enumerate(chips):
            for a in range(n):
                copy(a, 1 + j, (*chip, c), me).wait_recv()
                fw = copy(a, 4 + j, (*chip, c), sibling)
                fw.start()
                passed.append(fw)
        for a in range(n):
            copy(a, 0, sibling, me).wait_recv()
            for j, chip in enumerate(chips):
                copy(a, 4 + j, (*chip, 1 - c), me).wait_recv()
        for cp in first + passed:
            cp.wait_send()
        for cp in mine:
            cp.wait()

    any_spec = pl.BlockSpec(memory_space=pl.ANY)
    return pl.pallas_call(
        body, name=name,
        out_shape=[jax.ShapeDtypeStruct((N_DEV,) + a.shape, a.dtype) for a in arrs],
        in_specs=[any_spec] * n, out_specs=[any_spec] * n,
        scratch_shapes=[pltpu.SemaphoreType.DMA((7 * n,)), pltpu.SemaphoreType.DMA((7 * n,)), pltpu.SemaphoreType.DMA((n,))],
    )(*arrs)


def _all_gather_relayed(arrs, relay, name):
    n = len(arrs)
    K = 9

    def body(*refs):
        ins, outs = refs[:n], refs[n:2 * n]
        send_sems, recv_sems, local_sems = refs[2 * n:]
        x, y, c = lax.axis_index("x"), lax.axis_index("y"), lax.axis_index("c")
        me, sib = (x, y, c), (x, y, 1 - c)
        xn, yn, dg = (1 - x, y, c), (x, 1 - y, c), (1 - x, 1 - y, c)

        def other(p):
            return (p[0], p[1], 1 - p[2])

        def rows(a, half):
            h = arrs[a].shape[0] // 2
            return pl.ds(half * h, h)

        def copy(a, k, block, to, half=None, src=None):
            dst = outs[a].at[_flat(block)]
            if half is not None:
                dst = dst.at[rows(a, half)]
            return pltpu.make_async_remote_copy(src_ref=dst if src is None else src, dst_ref=dst,
                                                send_sem=send_sems.at[a * K + k], recv_sem=recv_sems.at[a * K + k],
                                                device_id=to, device_id_type=MESH)

        mine = [pltpu.make_async_copy(ins[a], outs[a].at[_flat(me)], local_sems.at[a]) for a in range(n)]
        for cp in mine:
            cp.start()
        sends = []

        def start(cp):
            cp.start()
            sends.append(cp)

        for a in range(n):
            start(copy(a, 1, me, xn, src=ins[a]))
            start(copy(a, 2, me, yn, src=ins[a]))
            if not relay[a]:
                start(copy(a, 3, me, dg, src=ins[a]))
            start(copy(a, 0, me, sib, src=ins[a]))
        for a in range(n):
            copy(a, 1, xn, me).wait_recv()
            if relay[a]:
                start(copy(a, 3, xn, yn, half=0))
            start(copy(a, 5, xn, sib))
        for a in range(n):
            copy(a, 2, yn, me).wait_recv()
            if relay[a]:
                start(copy(a, 4, yn, xn, half=1))
            start(copy(a, 6, yn, sib))
        for a in range(n):
            if relay[a]:
                copy(a, 3, dg, me, half=0).wait_recv()
                start(copy(a, 7, dg, sib, half=0))
                copy(a, 4, dg, me, half=1).wait_recv()
                start(copy(a, 8, dg, sib, half=1))
            else:
                copy(a, 3, dg, me).wait_recv()
                start(copy(a, 7, dg, sib))
        for a in range(n):
            copy(a, 0, sib, me).wait_recv()
            copy(a, 5, other(xn), me).wait_recv()
            copy(a, 6, other(yn), me).wait_recv()
            if relay[a]:
                copy(a, 7, other(dg), me, half=0).wait_recv()
                copy(a, 8, other(dg), me, half=1).wait_recv()
            else:
                copy(a, 7, other(dg), me).wait_recv()
        for cp in sends:
            cp.wait_send()
        for cp in mine:
            cp.wait()

    any_spec = pl.BlockSpec(memory_space=pl.ANY)
    return pl.pallas_call(
        body, name=name,
        out_shape=[jax.ShapeDtypeStruct((N_DEV,) + a.shape, a.dtype) for a in arrs],
        in_specs=[any_spec] * n, out_specs=[any_spec] * n,
        scratch_shapes=[pltpu.SemaphoreType.DMA((K * n,)), pltpu.SemaphoreType.DMA((K * n,)), pltpu.SemaphoreType.DMA((n,))],
    )(*arrs)


def _chip_peers(x, y):
    return [(1 - x, y), (x, 1 - y), (1 - x, 1 - y)]


def _chip(p):
    return 2 * p[0] + p[1]


def _plan_gather(kinds):
    def plan(x, y, c):
        out = []
        for a, kind in enumerate(kinds):
            for peer in [(x, y, 1 - c)] + [(*ch, c) for ch in _chip_peers(x, y)]:
                out.append((a, None, (kind, _flat((x, y, c))), peer, (kind, _flat(peer))))
        return out
    return plan


def _plan_swap(n):
    def plan(x, y, c):
        return [(a, 1 - c, ("all", 0), (x, y, 1 - c), ("all", 0)) for a in range(n)]
    return plan


def _slot(ref, where):
    kind, k = where
    if kind == "all":
        return ref
    if kind == "lead":
        return ref.at[k]
    return ref.at[:, pl.ds(pl.multiple_of(k * 256, 256), 256)]


def _plan_scatter(n):
    def plan(x, y, c):
        out = []
        for a in range(n):
            for ch in _chip_peers(x, y):
                out.append((a, _chip(ch), ("lead", _chip((x, y))), (*ch, c), ("lead", _chip(ch))))
        return out
    return plan


HBM_SPEC = pl.BlockSpec(memory_space=pltpu.HBM)
SEM_SPEC = pl.BlockSpec(memory_space=pltpu.SEMAPHORE)


def _in_hbm(a):
    return pltpu.with_memory_space_constraint(a, pltpu.HBM)


def _exchange_start(srcs, lands, plan, name):
    n = len(srcs)
    count = len(plan(0, 0, 0))

    def body(*refs):
        src_refs, land_refs = refs[:n], refs[n:2 * n]
        send_sems, recv_sems = refs[2 * n], refs[2 * n + 1]
        token = refs[-1]
        x, y, c = lax.axis_index("x"), lax.axis_index("y"), lax.axis_index("c")
        for k, (a, si, di, peer, _) in enumerate(plan(x, y, c)):
            src = src_refs[a] if si is None else src_refs[a].at[si]
            pltpu.make_async_remote_copy(src_ref=src, dst_ref=_slot(land_refs[a], di), send_sem=send_sems.at[k],
                                         recv_sem=recv_sems.at[k], device_id=peer, device_id_type=MESH).start()
        token[...] = jnp.zeros_like(token)

    out = pl.pallas_call(
        body, name=name,
        out_shape=(pltpu.SemaphoreType.DMA((count,)), pltpu.SemaphoreType.DMA((count,)),
                   *[pltpu.HBM(a.shape, a.dtype) for a in lands], jax.ShapeDtypeStruct((8, 128), F32)),
        in_specs=[HBM_SPEC] * (2 * n),
        out_specs=(SEM_SPEC, SEM_SPEC, *([HBM_SPEC] * n), pl.BlockSpec(memory_space=pltpu.VMEM)),
        input_output_aliases={n + i: 2 + i for i in range(n)},
        compiler_params=pltpu.CompilerParams(has_side_effects=pltpu.SideEffectType.DATAFLOW_SIDE_EFFECTING),
    )(*[_in_hbm(a) for a in srcs], *[_in_hbm(a) for a in lands])
    return out[0], out[1], list(srcs), list(out[2:2 + n]), out[-1]


def _exchange_wait(send_sems, recv_sems, srcs, lands, plan, after, name):
    n = len(srcs)

    def body(*refs):
        src_refs, land_refs = refs[:n], refs[n:2 * n]
        send_sems, recv_sems = refs[2 * n], refs[2 * n + 1]
        x, y, c = lax.axis_index("x"), lax.axis_index("y"), lax.axis_index("c")
        for k, (a, si, _, peer, ri) in enumerate(plan(x, y, c)):
            src = src_refs[a] if si is None else src_refs[a].at[si]
            cp = pltpu.make_async_remote_copy(src_ref=src, dst_ref=_slot(land_refs[a], ri), send_sem=send_sems.at[k],
                                              recv_sem=recv_sems.at[k], device_id=peer, device_id_type=MESH)
            cp.wait_send()
            cp.wait_recv()

    out = pl.pallas_call(
        body, name=name,
        out_shape=tuple(pltpu.HBM(a.shape, a.dtype) for a in lands),
        in_specs=[HBM_SPEC] * (2 * n) + [SEM_SPEC, SEM_SPEC, pl.BlockSpec(memory_space=pl.ANY)],
        out_specs=tuple([HBM_SPEC] * n),
        input_output_aliases={n + i: i for i in range(n)},
        compiler_params=pltpu.CompilerParams(has_side_effects=pltpu.SideEffectType.DATAFLOW_SIDE_EFFECTING),
    )(*[_in_hbm(a) for a in srcs], *lands, send_sems, recv_sems, after)
    return list(out)


def _forward_to_sibling(lands, kinds, name):
    n = len(lands)

    def body(*refs):
        in_refs, out_refs = refs[:n], refs[n:2 * n]
        send_sems, recv_sems = refs[2 * n:]
        x, y, c = lax.axis_index("x"), lax.axis_index("y"), lax.axis_index("c")
        sibling = (x, y, 1 - c)

        def copy(a, j, slot):
            return pltpu.make_async_remote_copy(src_ref=_slot(in_refs[a], (kinds[a], slot)), dst_ref=_slot(out_refs[a], (kinds[a], slot)),
                                                send_sem=send_sems.at[a * 3 + j], recv_sem=recv_sems.at[a * 3 + j],
                                                device_id=sibling, device_id_type=MESH)

        sends = [copy(a, j, _flat((*ch, c))) for a in range(n) for j, ch in enumerate(_chip_peers(x, y))]
        for cp in sends:
            cp.start()
        for a in range(n):
            for j, ch in enumerate(_chip_peers(x, y)):
                copy(a, j, _flat((*ch, 1 - c))).wait_recv()
        for cp in sends:
            cp.wait_send()

    any_spec = pl.BlockSpec(memory_space=pl.ANY)
    return pl.pallas_call(
        body, name=name, out_shape=[jax.ShapeDtypeStruct(a.shape, a.dtype) for a in lands],
        in_specs=[any_spec] * n, out_specs=[any_spec] * n, input_output_aliases={a: a for a in range(n)},
        scratch_shapes=[pltpu.SemaphoreType.DMA((3 * n,)), pltpu.SemaphoreType.DMA((3 * n,))],
    )(*lands)


def _place_own(zones, owns, slot, name):
    n = len(zones)

    def body(slot_ref, *refs):
        for a in range(n):
            refs[2 * n + a][...] = refs[a][...]

    return pl.pallas_call(
        body, name=name,
        grid_spec=pltpu.PrefetchScalarGridSpec(
            num_scalar_prefetch=1, grid=(1,),
            in_specs=[pl.BlockSpec(o.shape, lambda i, s_ref: (0, 0)) for o in owns] + [pl.BlockSpec(memory_space=pl.ANY)] * n,
            out_specs=[pl.BlockSpec((None,) + o.shape, lambda i, s_ref: (s_ref[0], 0, 0)) for o in owns]),
        out_shape=[jax.ShapeDtypeStruct(z.shape, z.dtype) for z in zones],
        input_output_aliases={1 + n + a: a for a in range(n)},
        compiler_params=_params(("arbitrary",)),
    )(slot, *owns, *zones)


def _swap_with_sibling(parts, name):
    n = len(parts)

    def body(*refs):
        in_refs, out_refs = refs[:n], refs[n:2 * n]
        send_sems, recv_sems = refs[2 * n:]
        x, y, c = lax.axis_index("x"), lax.axis_index("y"), lax.axis_index("c")
        sends = [pltpu.make_async_remote_copy(src_ref=in_refs[a].at[1 - c], dst_ref=out_refs[a], send_sem=send_sems.at[a],
                                              recv_sem=recv_sems.at[a], device_id=(x, y, 1 - c), device_id_type=MESH)
                 for a in range(n)]
        for cp in sends:
            cp.start()
        for cp in sends:
            cp.wait()

    any_spec = pl.BlockSpec(memory_space=pl.ANY)
    return pl.pallas_call(
        body, name=name, out_shape=[jax.ShapeDtypeStruct(a.shape[1:], a.dtype) for a in parts],
        in_specs=[any_spec] * n, out_specs=[any_spec] * n,
        scratch_shapes=[pltpu.SemaphoreType.DMA((n,)), pltpu.SemaphoreType.DMA((n,))],
    )(*parts)


def _pair_sum(parts, got, core, name):
    _, _, R, C = parts.shape
    tr = 256 if R % 256 == 0 else R

    def body(c_ref, p_ref, g_ref, o_ref):
        o_ref[...] = (p_ref[...].astype(F32) + g_ref[...].astype(F32)).astype(o_ref.dtype)

    return pl.pallas_call(
        body, name=name,
        grid_spec=pltpu.PrefetchScalarGridSpec(
            num_scalar_prefetch=1, grid=(4, R // tr),
            in_specs=[pl.BlockSpec((None, None, tr, C), lambda j, i, c_ref: (c_ref[0], j, i, 0)),
                      pl.BlockSpec((None, tr, C), lambda j, i, c_ref: (j, i, 0))],
            out_specs=pl.BlockSpec((None, tr, C), lambda j, i, c_ref: (j, i, 0))),
        out_shape=jax.ShapeDtypeStruct((4, R, C), parts.dtype),
        compiler_params=_params(("parallel", "parallel")),
    )(core, parts, got)


def _matmul(a, b, mode, M, N, K, tm, tn, tk, out_dtype, name, b_noff=0, a_moff=0, a_koff=0, b_blocked=False, out_blocked=None,
            dep=None, addend=None):
    nm, nn, nk = M // tm, N // tn, K // tk
    if mode == "nn":
        a_spec = pl.BlockSpec((tm, tk), lambda j, i, k: (i, k + a_koff))
        b_spec = pl.BlockSpec((tk, tn), lambda j, i, k: (k, j + b_noff))
        dn = NN
    elif mode == "nt":
        a_spec = pl.BlockSpec((tm, tk), lambda j, i, k: (i, k + a_koff))
        if b_blocked:
            b_spec = pl.BlockSpec((None, tn, tk), lambda j, i, k: (k, j, 0))
        else:
            b_spec = pl.BlockSpec((tn, tk), lambda j, i, k: (j + b_noff, k))
        dn = NT
    else:
        a_spec = pl.BlockSpec((tk, tm), lambda j, i, k: (k, i + a_moff))
        if b_blocked:
            b_spec = pl.BlockSpec((None, tk, tn), lambda j, i, k: (j, k, 0))
        else:
            b_spec = pl.BlockSpec((tk, tn), lambda j, i, k: (k, j + b_noff))
        dn = TN
    if out_blocked == "col":
        out_shape = jax.ShapeDtypeStruct((2, 4, M, tn), out_dtype)
        out_spec = pl.BlockSpec((None, None, tm, tn), lambda j, i, k: (j % 2, j // 2, i, 0))
    elif out_blocked == "row":
        out_shape = jax.ShapeDtypeStruct((2, 4, tm, N), out_dtype)
        out_spec = pl.BlockSpec((None, None, tm, tn), lambda j, i, k: (i % 2, i // 2, 0, j))
    elif out_blocked == "third":
        out_shape = jax.ShapeDtypeStruct((3, M, N // 3), out_dtype)
        out_spec = pl.BlockSpec((None, tm, tn), lambda j, i, k: (j // (nn // 3), i, j % (nn // 3)))
    else:
        out_shape = jax.ShapeDtypeStruct((M, N), out_dtype)
        out_spec = pl.BlockSpec((tm, tn), lambda j, i, k: (i, j))

    n_extra = (addend is not None) + (dep is not None)

    def body(a_ref, b_ref, *rest):
        o_ref, scratch = rest[n_extra], rest[n_extra + 1:]
        if nk == 1:
            prod = _dot(a_ref[...], b_ref[...], dn)
            if addend is not None:
                prod = prod + rest[0][...].astype(F32)
            o_ref[...] = prod.astype(out_dtype)
        else:
            assert addend is None
            acc_ref, = scratch
            k = pl.program_id(2)

            @pl.when(k == 0)
            def _():
                acc_ref[...] = jnp.zeros_like(acc_ref)

            acc_ref[...] += _dot(a_ref[...], b_ref[...], dn)

            @pl.when(k == nk - 1)
            def _():
                o_ref[...] = acc_ref[...].astype(out_dtype)

    return pl.pallas_call(
        body, name=name, grid=(nn, nm, nk),
        in_specs=[a_spec, b_spec] + ([] if addend is None else [out_spec])
        + ([] if dep is None else [pl.BlockSpec((8, 128), lambda j, i, k: (0, 0))]),
        out_specs=out_spec, out_shape=out_shape,
        scratch_shapes=[] if nk == 1 else [pltpu.VMEM((tm, tn), F32)],
        compiler_params=_params(("parallel", "parallel", "arbitrary")),
    )(a, b, *([] if addend is None else [addend]), *([] if dep is None else [dep]))


def _rms_fwd(x, g, name):
    R, Dm = x.shape
    tr = min(R, 256)

    def body(x_ref, g_ref, h_ref):
        xv = x_ref[...]
        r = lax.rsqrt(jnp.mean(xv * xv, axis=-1, keepdims=True) + EPS)
        h_ref[...] = (xv * r * g_ref[...]).astype(BF)

    return pl.pallas_call(
        body, name=name, grid=(R // tr,),
        in_specs=[pl.BlockSpec((tr, Dm), lambda i: (i, 0)), pl.BlockSpec((1, Dm), lambda i: (0, 0))],
        out_specs=pl.BlockSpec((tr, Dm), lambda i: (i, 0)), out_shape=jax.ShapeDtypeStruct((R, Dm), BF),
        compiler_params=_params(("parallel",)),
    )(x, g)


def _rms_gain_grad(dn, x, name):
    R, Dm = x.shape

    def body(dn_ref, x_ref, o_ref):
        xv = x_ref[...]
        r = lax.rsqrt(jnp.mean(xv * xv, axis=-1, keepdims=True) + EPS)
        o_ref[...] = jnp.sum(dn_ref[...] * xv * r, axis=0, keepdims=True)

    return pl.pallas_call(
        body, name=name, out_shape=jax.ShapeDtypeStruct((1, Dm), F32),
        compiler_params=_params(),
    )(dn, x)


def _shift_down(v, k, head8, row, T):
    if k == 0:
        return v
    r = pltpu.roll(v, k, 0)
    hr = pltpu.roll(head8, k, 0)
    top = jnp.where(row[:8] < k, hr, r[:8])
    return jnp.concatenate([top, r[8:]], axis=0)


def _shift_up(v, k, tail8, row, T):
    if k == 0:
        return v
    r = pltpu.roll(v, T - k, 0)
    tr = pltpu.roll(tail8, 8 - k, 0)
    bot = jnp.where(row[:8] >= 8 - k, tr, r[T - 8:])
    return jnp.concatenate([r[:T - 8], bot], axis=0)


def _rglru_gates(u, head8, grow, row, T, cw_ref, cb_ref, wa_ref, ba_ref, wx_ref, bx_ref, lam_ref):
    us = [_shift_down(u, k, head8, row, T) for k in range(CONV_W)]
    acc = us[0] * cw_ref[0:1, :]
    for k in range(1, CONV_W):
        acc = acc + us[k] * cw_ref[k:k + 1, :]
    conv = cb_ref[...] + acc
    cbf = conv.astype(BF)
    r_ = _sigmoid(_dot(cbf, wa_ref[0], NN) + ba_ref[...])
    i_ = _sigmoid(_dot(cbf, wx_ref[0], NN) + bx_ref[...])
    sp = _softplus(-lam_ref[...])
    la = -LRU_C * r_ * sp
    a = jnp.exp(la)
    mult_raw = jnp.sqrt(-_expm1(2.0 * la))
    mult = jnp.where(grow == 0, 1.0, mult_raw)
    return us, conv, cbf, r_, i_, sp, a, mult_raw, mult


def _rglru_specs(T, nt, rev):
    tmap = (lambda n, t: (nt - 1 - t, n)) if rev else (lambda n, t: (t, n))
    hmap = ((lambda n, t: (jnp.maximum((nt - 1 - t) * (T // 8) - 1, 0), n)) if rev
            else (lambda n, t: (jnp.maximum(t * (T // 8) - 1, 0), n)))
    tile = pl.BlockSpec((T, RNN_BLOCK), tmap)
    halo = pl.BlockSpec((8, RNN_BLOCK), hmap)
    vec = pl.BlockSpec((1, RNN_BLOCK), lambda n, t: (0, n))
    cw = pl.BlockSpec((CONV_W, RNN_BLOCK), lambda n, t: (0, n))
    wblk = pl.BlockSpec((1, RNN_BLOCK, RNN_BLOCK), lambda n, t: (n, 0, 0))
    return tile, halo, vec, cw, wblk


def _rglru_fwd(xr, g, cw, cb, wa, ba, wx, bx, lam, T):
    S = xr.shape[0]
    nt = S // T

    def body(u_ref, uh_ref, g_ref, cw_ref, cb_ref, wa_ref, ba_ref, wx_ref, bx_ref, lam_ref, h_ref, y_ref, carry):
        t = pl.program_id(1)

        @pl.when(t == 0)
        def _():
            carry[...] = jnp.zeros_like(carry)

        row = lax.broadcasted_iota(jnp.int32, (T, RNN_BLOCK), 0)
        grow = row + t * T
        head8 = jnp.where(t > 0, uh_ref[...], 0.0)
        _, conv, _, _, i_, _, a, _, mult = _rglru_gates(u_ref[...], head8, grow, row, T, cw_ref, cb_ref, wa_ref, ba_ref,
                                                         wx_ref, bx_ref, lam_ref)
        b = mult * i_ * conv
        s = 1
        while s < T:
            keep = row >= s
            a_s = jnp.where(keep, pltpu.roll(a, s, 0), 1.0)
            b_s = jnp.where(keep, pltpu.roll(b, s, 0), 0.0)
            b = a * b_s + b
            a = a * a_s
            s *= 2
        h = b + a * carry[0:1, :]
        carry[...] = jnp.broadcast_to(h[T - 1:T, :], carry.shape)
        h_ref[...] = h
        gv = g_ref[...]
        y_ref[...] = (h * (gv * _sigmoid(gv))).astype(BF)

    tile, halo, vec, cwspec, wblk = _rglru_specs(T, nt, False)
    return pl.pallas_call(
        body, name="rglru_fwd", grid=(RNN_BLOCKS, nt),
        in_specs=[tile, halo, tile, cwspec, vec, wblk, vec, wblk, vec, vec],
        out_specs=[tile, tile],
        out_shape=[jax.ShapeDtypeStruct((S, D_RNN), F32), jax.ShapeDtypeStruct((S, D_RNN), BF)],
        scratch_shapes=[pltpu.VMEM((8, RNN_BLOCK), F32)],
        compiler_params=_params(("parallel", "arbitrary")),
    )(xr, xr, g, cw, cb, wa, ba, wx, bx, lam)


def _rglru_bwd(xr, g, h, dy, cw, cb, wa, ba, wx, bx, lam, T):
    S = xr.shape[0]
    nt = S // T

    def body(u_ref, uh_ref, g_ref, h_ref, hh_ref, dy_ref, cw_ref, cb_ref, wa_ref, ba_ref, wx_ref, bx_ref, lam_ref,
             du_ref, dg_ref, dwa_ref, dwx_ref, dvec_ref, c_dhh, c_a, c_dconv):
        t = pl.program_id(1)
        tt = nt - 1 - t

        @pl.when(t == 0)
        def _():
            c_dhh[...] = jnp.zeros_like(c_dhh)
            c_a[...] = jnp.zeros_like(c_a)
            c_dconv[...] = jnp.zeros_like(c_dconv)
            dwa_ref[...] = jnp.zeros_like(dwa_ref)
            dwx_ref[...] = jnp.zeros_like(dwx_ref)
            dvec_ref[...] = jnp.zeros_like(dvec_ref)

        row = lax.broadcasted_iota(jnp.int32, (T, RNN_BLOCK), 0)
        row8 = row[:8]
        grow = row + tt * T
        head8 = jnp.where(tt > 0, uh_ref[...], 0.0)
        us, conv, cbf, r_, i_, sp, a, mult_raw, mult = _rglru_gates(
            u_ref[...], head8, grow, row, T, cw_ref, cb_ref, wa_ref, ba_ref, wx_ref, bx_ref, lam_ref)
        hv = h_ref[...]
        hprev = _shift_down(hv, 1, jnp.where(tt > 0, hh_ref[...], 0.0), row, T)
        gv = g_ref[...]
        sg = _sigmoid(gv)
        dyv = dy_ref[...]
        dg_ref[...] = (dyv * hv * (sg * (1.0 + gv * (1.0 - sg)))).astype(BF)
        d = dyv * (gv * sg)
        A = _shift_up(a, 1, c_a[...], row, T)
        s = 1
        while s < T:
            keep = row < T - s
            A_s = jnp.where(keep, pltpu.roll(A, T - s, 0), 1.0)
            d_s = jnp.where(keep, pltpu.roll(d, T - s, 0), 0.0)
            d = A * d_s + d
            A = A * A_s
            s *= 2
        dhh = d + A * c_dhh[0:1, :]
        da = dhh * hprev
        dconv = dhh * mult * i_
        di = dhh * mult * conv
        dmult = dhh * i_ * conv
        dla = da * a - jnp.where(grow == 0, 0.0, dmult * (a * a) / mult_raw)
        dr = dla * (-LRU_C * sp)
        dsp = jnp.sum(dla * (-LRU_C * r_), axis=0, keepdims=True)
        dza = dr * r_ * (1.0 - r_)
        dzx = di * i_ * (1.0 - i_)
        dza_b, dzx_b = dza.astype(BF), dzx.astype(BF)
        dconv = dconv + _dot(dza_b, wa_ref[0], NT) + _dot(dzx_b, wx_ref[0], NT)
        dwa_ref[0] += _dot(cbf, dza_b, TN)
        dwx_ref[0] += _dot(cbf, dzx_b, TN)
        lam = lam_ref[...]
        rows = [jnp.sum(dconv * us[k], axis=0, keepdims=True) for k in range(CONV_W)]
        rows += [jnp.sum(dconv, axis=0, keepdims=True), jnp.sum(dza, axis=0, keepdims=True),
                 jnp.sum(dzx, axis=0, keepdims=True), dsp * (-_sigmoid(-lam))]
        upd = jnp.zeros((8, RNN_BLOCK), F32)
        for j, rv in enumerate(rows):
            upd = upd + jnp.where(row8 == j, rv, 0.0)
        dvec_ref[...] += upd
        tail8 = c_dconv[...]
        du = dconv * cw_ref[0:1, :]
        for k in range(1, CONV_W):
            du = du + _shift_up(dconv, k, tail8, row, T) * cw_ref[k:k + 1, :]
        du_ref[...] = du.astype(BF)
        c_dhh[...] = jnp.broadcast_to(dhh[0:1, :], c_dhh.shape)
        c_a[...] = jnp.broadcast_to(a[0:1, :], c_a.shape)
        c_dconv[...] = dconv[:8]

    tile, halo, vec, cwspec, wblk = _rglru_specs(T, nt, True)
    acc8 = pl.BlockSpec((8, RNN_BLOCK), lambda n, t: (0, n))
    return pl.pallas_call(
        body, name="rglru_bwd", grid=(RNN_BLOCKS, nt),
        in_specs=[tile, halo, tile, tile, halo, tile, cwspec, vec, wblk, vec, wblk, vec, vec],
        out_specs=[tile, tile, wblk, wblk, acc8],
        out_shape=[jax.ShapeDtypeStruct((S, D_RNN), BF), jax.ShapeDtypeStruct((S, D_RNN), BF),
                   jax.ShapeDtypeStruct((RNN_BLOCKS, RNN_BLOCK, RNN_BLOCK), F32),
                   jax.ShapeDtypeStruct((RNN_BLOCKS, RNN_BLOCK, RNN_BLOCK), F32),
                   jax.ShapeDtypeStruct((8, D_RNN), F32)],
        scratch_shapes=[pltpu.VMEM((8, RNN_BLOCK), F32)] * 3,
        compiler_params=_params(("parallel", "arbitrary")),
    )(xr, xr, g, h, h, dy, cw, cb, wa, ba, wx, bx, lam)


def _rel_bucket_map():
    qi = np.arange(WINDOW)[:, None]
    kj = np.arange(2 * WINDOW)[None, :]
    dist = jnp.asarray(qi + WINDOW - kj, jnp.int32)
    n = jnp.maximum(dist, 0)
    max_exact = REL_BUCKETS // 2
    ratio = jnp.log(jnp.maximum(n, 1).astype(F32) / max_exact) / math.log(REL_MAX_DIST / max_exact)
    large = jnp.minimum(max_exact + (ratio * (REL_BUCKETS - max_exact)).astype(jnp.int32), REL_BUCKETS - 1)
    bucket = jnp.where(n < max_exact, n, large).astype(jnp.int32)
    j = np.arange(WINDOW)[None, :]
    return jnp.where(jnp.asarray(j > qi), bucket[:, :WINDOW], bucket[:, WINDOW:])


def _swa_common(n, kv_ref, bucket_ref, relb_ref, bias_scr):
    @pl.when(n == 0)
    def _():
        bk = bucket_ref[...]
        for h in range(SWA_HEADS):
            acc = jnp.zeros((WINDOW, WINDOW), F32)
            for b in range(REL_BUCKETS):
                acc = acc + jnp.where(bk == b, relb_ref[b, h], 0.0)
            bias_scr[h] = acc

    prev0 = pl.multiple_of(jnp.maximum(n - 1, 0) * WINDOW, WINDOW)
    cur0 = pl.multiple_of(n * WINDOW, WINDOW)
    kk = jnp.concatenate([kv_ref[pl.ds(prev0, WINDOW), :], kv_ref[pl.ds(cur0, WINDOW), :]], axis=0).astype(F32)
    rowi = lax.broadcasted_iota(jnp.int32, (WINDOW, WINDOW), 0)
    col = lax.broadcasted_iota(jnp.int32, (WINDOW, WINDOW), 1)
    from_prev = col > rowi
    return kk, from_prev, prev0, cur0


def _fold(full, from_prev):
    return jnp.where(from_prev, full[:, :WINDOW], full[:, WINDOW:])


def _unfold(sq, from_prev):
    return jnp.concatenate([jnp.where(from_prev, sq, 0.0), jnp.where(from_prev, 0.0, sq)], axis=1)


def _half_pair(part, kvh):
    lo = lax.broadcasted_iota(jnp.int32, part.shape, 1) < SWA_HD
    if kvh == 0:
        pa = jnp.where(lo, part, 0.0)
        pb = pltpu.roll(pa, SWA_HD, 1)
    else:
        pb = jnp.where(lo, 0.0, part)
        pa = pltpu.roll(pb, SWA_HD, 1)
    return pa.astype(BF), pb.astype(BF)


ALL_HEADS = SWA_HEADS * WINDOW


def _sink_column(sinks):
    return jnp.repeat(sinks.reshape(SWA_HEADS), WINDOW).reshape(ALL_HEADS, 1)


def _swa_operands(kk):
    return [(_half_pair(kk[:, :128], kvh), _half_pair(kk[:, 128:], kvh)) for kvh in range(SWA_KV_HEADS)]


def _swa_probs(n, q_ref, ops, bias_scr, sinkc_ref, from_prev):
    lgs = []
    for kvh in range(SWA_KV_HEADS):
        (ka, kb), _ = ops[kvh]
        for p in range(4):
            q2 = q_ref[:, kvh * 512 + p * 128:kvh * 512 + p * 128 + 128]
            lgs += [_fold(_dot(q2, ka, NT), from_prev), _fold(_dot(q2, kb, NT), from_prev)]
    lg = jnp.concatenate(lgs, axis=0) * (SWA_HD ** -0.5) + bias_scr[...].reshape(ALL_HEADS, WINDOW)
    rowi = jnp.bitwise_and(lax.broadcasted_iota(jnp.int32, (ALL_HEADS, WINDOW), 0), WINDOW - 1)
    col = lax.broadcasted_iota(jnp.int32, (ALL_HEADS, WINDOW), 1)
    no_prev = jnp.where(n > 0, 0, 4 * WINDOW)
    lg = jnp.where(jnp.logical_or(col <= rowi, col > rowi + no_prev), lg, NEG_INF)
    sink = sinkc_ref[...]
    m = jnp.maximum(jnp.max(lg, axis=-1, keepdims=True), sink)
    e = jnp.exp(lg - m)
    es = jnp.exp(sink - m)
    den = jnp.sum(e, axis=-1, keepdims=True) + es
    return e / den, es / den


def _swa_fwd(q, kv, g, bucket, rel_bias, sink_col):
    S = q.shape[0]
    nb = S // WINDOW

    def body(q_ref, kv_ref, g_ref, bucket_ref, relb_ref, sinkc_ref, o_ref, y_ref, bias_scr):
        n = pl.program_id(0)
        kk, from_prev, _, _ = _swa_common(n, kv_ref, bucket_ref, relb_ref, bias_scr)
        ops = _swa_operands(kk)
        pr, _ = _swa_probs(n, q_ref, ops, bias_scr, sinkc_ref, from_prev)
        for kvh in range(SWA_KV_HEADS):
            _, (va, vb) = ops[kvh]
            for p in range(4):
                c0 = kvh * 512 + p * 128
                r0 = (kvh * 8 + 2 * p) * WINDOW
                o2 = (_dot(_unfold(pr[r0:r0 + WINDOW], from_prev).astype(BF), va, NN)
                      + _dot(_unfold(pr[r0 + WINDOW:r0 + 2 * WINDOW], from_prev).astype(BF), vb, NN))
                o_ref[:, c0:c0 + 128] = o2
                gv = g_ref[:, c0:c0 + 128]
                y_ref[:, c0:c0 + 128] = (o2 * (gv * _sigmoid(gv))).astype(BF)

    blk = pl.BlockSpec((WINDOW, 1024), lambda n: (n, 0))
    smem = pl.BlockSpec(memory_space=pltpu.SMEM)
    sinkc = pl.BlockSpec((ALL_HEADS, 1), lambda n: (0, 0))
    return pl.pallas_call(
        body, name="swa_fwd", grid=(nb,),
        in_specs=[blk, pl.BlockSpec((S, 256), lambda n: (0, 0)), blk, pl.BlockSpec((WINDOW, WINDOW), lambda n: (0, 0)), smem, sinkc],
        out_specs=[blk, blk],
        out_shape=[jax.ShapeDtypeStruct((S, 1024), F32), jax.ShapeDtypeStruct((S, 1024), BF)],
        scratch_shapes=[pltpu.VMEM((SWA_HEADS, WINDOW, WINDOW), F32)],
        compiler_params=_params(("arbitrary",)),
    )(q, kv, g, bucket, rel_bias, sink_col)


def _swa_bwd(q, kv, g, o, dy, bucket, rel_bias, sink_col):
    S = q.shape[0]
    nb = S // WINDOW

    def body(q_ref, kv_ref, g_ref, o_ref, dy_ref, bucket_ref, relb_ref, sinkc_ref,
             dq_ref, dg_ref, dkv_ref, dsink_ref, drel_ref, bias_scr, dbias_scr, dsink_scr):
        n = pl.program_id(0)

        @pl.when(n == 0)
        def _():
            dbias_scr[...] = jnp.zeros_like(dbias_scr)
            dsink_scr[...] = jnp.zeros_like(dsink_scr)
            dkv_ref[...] = jnp.zeros_like(dkv_ref)

        kk, from_prev, prev0, cur0 = _swa_common(n, kv_ref, bucket_ref, relb_ref, bias_scr)
        ops = _swa_operands(kk)
        pr, ps = _swa_probs(n, q_ref, ops, bias_scr, sinkc_ref, from_prev)
        do2s, dps = [], []
        for kvh in range(SWA_KV_HEADS):
            _, (va, vb) = ops[kvh]
            for p in range(4):
                c0 = kvh * 512 + p * 128
                gv = g_ref[:, c0:c0 + 128]
                sg = _sigmoid(gv)
                dyv = dy_ref[:, c0:c0 + 128]
                dg_ref[:, c0:c0 + 128] = (dyv * o_ref[:, c0:c0 + 128] * (sg * (1.0 + gv * (1.0 - sg)))).astype(BF)
                do2 = (dyv * (gv * sg)).astype(BF)
                do2s.append(do2)
                dps += [_fold(_dot(do2, va, NT), from_prev), _fold(_dot(do2, vb, NT), from_prev)]
        dp = jnp.concatenate(dps, axis=0)
        delta = jnp.sum(pr * dp, axis=-1, keepdims=True)
        ds = pr * (dp - delta)
        dbias_scr[...] += ds.reshape(SWA_HEADS, WINDOW, WINDOW)
        dsink_scr[...] += ps * delta
        dsc = ds * (SWA_HD ** -0.5)
        lo256 = lax.broadcasted_iota(jnp.int32, (2 * WINDOW, 128), 1) < SWA_HD
        dks, dvs = [], []
        for kvh in range(SWA_KV_HEADS):
            (ka, kb), _ = ops[kvh]
            dka = jnp.zeros((2 * WINDOW, 128), F32)
            dkb, dva, dvb = dka, dka, dka
            for p in range(4):
                c0 = kvh * 512 + p * 128
                r0 = (kvh * 8 + 2 * p) * WINDOW
                q2 = q_ref[:, c0:c0 + 128]
                do2 = do2s[kvh * 4 + p]
                ds0 = _unfold(dsc[r0:r0 + WINDOW], from_prev).astype(BF)
                ds1 = _unfold(dsc[r0 + WINDOW:r0 + 2 * WINDOW], from_prev).astype(BF)
                dq_ref[:, c0:c0 + 128] = (_dot(ds0, ka, NN) + _dot(ds1, kb, NN)).astype(BF)
                dka = dka + _dot(ds0, q2, TN)
                dkb = dkb + _dot(ds1, q2, TN)
                dva = dva + _dot(_unfold(pr[r0:r0 + WINDOW], from_prev).astype(BF), do2, TN)
                dvb = dvb + _dot(_unfold(pr[r0 + WINDOW:r0 + 2 * WINDOW], from_prev).astype(BF), do2, TN)
            dks.append(jnp.where(lo256, dka, 0.0) + pltpu.roll(jnp.where(lo256, 0.0, dkb), SWA_HD, 1))
            dvs.append(jnp.where(lo256, dva, 0.0) + pltpu.roll(jnp.where(lo256, 0.0, dvb), SWA_HD, 1))
        dk = dks[0] + pltpu.roll(dks[1], SWA_HD, 1)
        dv = dvs[0] + pltpu.roll(dvs[1], SWA_HD, 1)
        dkv_ref[pl.ds(prev0, WINDOW), 0:128] += dk[:WINDOW]
        dkv_ref[pl.ds(prev0, WINDOW), 128:256] += dv[:WINDOW]
        dkv_ref[pl.ds(cur0, WINDOW), 0:128] += dk[WINDOW:]
        dkv_ref[pl.ds(cur0, WINDOW), 128:256] += dv[WINDOW:]

        @pl.when(n == nb - 1)
        def _():
            dsink_ref[...] = -jnp.sum(dsink_scr[...].reshape(SWA_HEADS, WINDOW, 1), axis=1)
            bk = bucket_ref[...]
            sums = []
            for b in range(REL_BUCKETS):
                sums.append(jnp.sum(jnp.where((bk == b)[None], dbias_scr[...], 0.0), axis=1))
            drel_ref[...] = jnp.sum(jnp.concatenate(sums, axis=0), axis=1, keepdims=True)

    blk = pl.BlockSpec((WINDOW, 1024), lambda n: (n, 0))
    smem = pl.BlockSpec(memory_space=pltpu.SMEM)
    whole = lambda shape: pl.BlockSpec(shape, lambda n: (0, 0))
    return pl.pallas_call(
        body, name="swa_bwd", grid=(nb,),
        in_specs=[blk, whole((S, 256)), blk, blk, blk, whole((WINDOW, WINDOW)), smem, whole((ALL_HEADS, 1))],
        out_specs=[blk, blk, whole((S, 256)), whole((SWA_HEADS, 1)), whole((REL_BUCKETS * SWA_HEADS, 1))],
        out_shape=[jax.ShapeDtypeStruct((S, 1024), BF), jax.ShapeDtypeStruct((S, 1024), BF),
                   jax.ShapeDtypeStruct((S, 256), F32), jax.ShapeDtypeStruct((SWA_HEADS, 1), F32),
                   jax.ShapeDtypeStruct((REL_BUCKETS * SWA_HEADS, 1), F32)],
        scratch_shapes=[pltpu.VMEM((SWA_HEADS, WINDOW, WINDOW), F32), pltpu.VMEM((SWA_HEADS, WINDOW, WINDOW), F32),
                        pltpu.VMEM((ALL_HEADS, 1), F32)],
        compiler_params=_params(("arbitrary",)),
    )(q, kv, g, o, dy, bucket, rel_bias, sink_col)


def _mem_probs(qh, mk):
    lg = _dot(qh, mk, NT) * (MEM_HD ** -0.5)
    e = jnp.exp(lg - jnp.max(lg, axis=-1, keepdims=True))
    return e / jnp.sum(e, axis=-1, keepdims=True)


def _mem_fwd(q, mkv, g):
    S = q.shape[0]
    M = mkv.shape[0]
    tq = 256

    def body(q_ref, mkv_ref, g_ref, o_ref, y_ref):
        for h in range(MEM_HEADS):
            c0 = h * MEM_HD
            pr = _mem_probs(q_ref[:, c0:c0 + MEM_HD], mkv_ref[:, c0:c0 + MEM_HD])
            o = _dot(pr.astype(BF), mkv_ref[:, D_MEM + c0:D_MEM + c0 + MEM_HD], NN)
            o_ref[:, c0:c0 + MEM_HD] = o
            gv = g_ref[:, c0:c0 + MEM_HD]
            y_ref[:, c0:c0 + MEM_HD] = (o * (gv * _sigmoid(gv))).astype(BF)

    blk = pl.BlockSpec((tq, D_MEM), lambda i: (i, 0))
    return pl.pallas_call(
        body, name="mem_fwd", grid=(S // tq,),
        in_specs=[blk, pl.BlockSpec((M, 2 * D_MEM), lambda i: (0, 0)), blk], out_specs=[blk, blk],
        out_shape=[jax.ShapeDtypeStruct((S, D_MEM), F32), jax.ShapeDtypeStruct((S, D_MEM), BF)],
        compiler_params=_params(("parallel",)),
    )(q, mkv, g)


def _mem_bwd(q, mkv, g, o, dy):
    S = q.shape[0]
    M = mkv.shape[0]
    tq = 256

    def body(q_ref, mkv_ref, g_ref, o_ref, dy_ref, dq_ref, dg_ref, dmkv_ref):
        @pl.when(pl.program_id(0) == 0)
        def _():
            dmkv_ref[...] = jnp.zeros_like(dmkv_ref)

        for h in range(MEM_HEADS):
            c0 = h * MEM_HD
            qh = q_ref[:, c0:c0 + MEM_HD]
            mk = mkv_ref[:, c0:c0 + MEM_HD]
            mv = mkv_ref[:, D_MEM + c0:D_MEM + c0 + MEM_HD]
            gv = g_ref[:, c0:c0 + MEM_HD]
            sg = _sigmoid(gv)
            dyv = dy_ref[:, c0:c0 + MEM_HD]
            dg_ref[:, c0:c0 + MEM_HD] = (dyv * o_ref[:, c0:c0 + MEM_HD] * (sg * (1.0 + gv * (1.0 - sg)))).astype(BF)
            do = (dyv * (gv * sg)).astype(BF)
            pr = _mem_probs(qh, mk)
            dp = _dot(do, mv, NT)
            ds = pr * (dp - jnp.sum(pr * dp, axis=-1, keepdims=True))
            dsb = (ds * (MEM_HD ** -0.5)).astype(BF)
            dq_ref[:, c0:c0 + MEM_HD] = _dot(dsb, mk, NN).astype(BF)
            dmkv_ref[:, c0:c0 + MEM_HD] += _dot(dsb, qh, TN)
            dmkv_ref[:, D_MEM + c0:D_MEM + c0 + MEM_HD] += _dot(pr.astype(BF), do, TN)

    blk = pl.BlockSpec((tq, D_MEM), lambda i: (i, 0))
    whole = pl.BlockSpec((M, 2 * D_MEM), lambda i: (0, 0))
    return pl.pallas_call(
        body, name="mem_bwd", grid=(S // tq,),
        in_specs=[blk, whole, blk, blk, blk], out_specs=[blk, blk, whole],
        out_shape=[jax.ShapeDtypeStruct((S, D_MEM), BF), jax.ShapeDtypeStruct((S, D_MEM), BF),
                   jax.ShapeDtypeStruct((M, 2 * D_MEM), F32)],
        compiler_params=_params(("arbitrary",)),
    )(q, mkv, g, o, dy)


MERGE_TN = 512


def _merge_specs(tm):
    ytile = pl.BlockSpec((tm, 1024), lambda i, j: (i, 0))
    wblk = pl.BlockSpec((MERGE_TN, 1024), lambda i, j: (j, 0))
    gls = [pl.BlockSpec((None, tm, MERGE_TN), (lambda i, j, br=br: (br, i, j))) for br in range(3)]
    otile = pl.BlockSpec((tm, MERGE_TN), lambda i, j: (i, j))
    return ytile, wblk, gls, otile


def _merge_fwd(ys, ws, gl, tm):
    S = gl.shape[1]

    def body(y0, y1, y2, w0, w1, w2, g0, g1, g2, o_ref):
        acc = None
        for y_ref, w_ref, g_ref in ((y0, w0, g0), (y1, w1, g1), (y2, w2, g2)):
            term = _sigmoid(g_ref[...]) * _dot(y_ref[...], w_ref[...], NT)
            acc = term if acc is None else acc + term
        o_ref[...] = acc.astype(BF)

    ytile, wblk, gls, otile = _merge_specs(tm)
    return pl.pallas_call(
        body, name="merge_fwd", grid=(S // tm, D_MODEL // MERGE_TN),
        in_specs=[ytile] * 3 + [wblk] * 3 + gls, out_specs=otile,
        out_shape=jax.ShapeDtypeStruct((S, D_MODEL), BF),
        compiler_params=_params(("parallel", "arbitrary")),
    )(*ys, *ws, gl, gl, gl)


def _merge_bwd(dout, w_out, ys, ws, gl, tm):
    S = gl.shape[1]

    def body(do_ref, wo_ref, y0, y1, y2, w0, w1, w2, g0, g1, g2, dg0, dg1, dg2, dp0, dp1, dp2):
        dm = _dot(do_ref[...], wo_ref[...], NT)
        for y_ref, w_ref, g_ref, dg_ref, dp_ref in ((y0, w0, g0, dg0, dp0), (y1, w1, g1, dg1, dp1), (y2, w2, g2, dg2, dp2)):
            gate = _sigmoid(g_ref[...])
            pv = _dot(y_ref[...], w_ref[...], NT)
            dg_ref[...] = (dm * pv * gate * (1.0 - gate)).astype(BF)
            dp_ref[...] = (dm * gate).astype(BF)

    ytile, wblk, gls, otile = _merge_specs(tm)
    out = jax.ShapeDtypeStruct((S, D_MODEL), BF)
    return pl.pallas_call(
        body, name="merge_bwd", grid=(S // tm, D_MODEL // MERGE_TN),
        in_specs=[pl.BlockSpec((tm, D_MODEL), lambda i, j: (i, 0)), pl.BlockSpec((MERGE_TN, D_MODEL), lambda i, j: (j, 0))]
        + [ytile] * 3 + [wblk] * 3 + gls,
        out_specs=[otile] * 6, out_shape=[out] * 6,
        compiler_params=_params(("parallel", "arbitrary")),
    )(dout, w_out, *ys, *ws, gl, gl, gl)


def _out_loss(merged, w_out, x, target, post_g, tm):
    S = x.shape[0]

    def body(m_ref, w_ref, x_ref, t_ref, g_ref, dout_ref, dy_ref, loss_ref, dpost_ref):
        @pl.when(pl.program_id(0) == 0)
        def _():
            loss_ref[...] = jnp.zeros_like(loss_ref)
            dpost_ref[...] = jnp.zeros_like(dpost_ref)

        out = _dot(m_ref[...], w_ref[...], NN)
        r = lax.rsqrt(jnp.mean(out * out, axis=-1, keepdims=True) + EPS)
        nrm = out * r
        gv = g_ref[...]
        err = (x_ref[...] + nrm * gv) - t_ref[...]
        sq = jnp.sum(jnp.sum(err * err, axis=1, keepdims=True), axis=0, keepdims=True)
        loss_ref[...] += sq * (0.5 / D_MODEL)
        dy = err * (1.0 / D_MODEL)
        dy_ref[...] = dy
        dpost_ref[...] += jnp.sum(dy * nrm, axis=0, keepdims=True)
        dn = dy * gv
        dout_ref[...] = (r * (dn - nrm * jnp.mean(dn * nrm, axis=-1, keepdims=True))).astype(BF)

    row = pl.BlockSpec((tm, D_MODEL), lambda i: (i, 0))
    return pl.pallas_call(
        body, name="out_loss", grid=(S // tm,),
        in_specs=[row, pl.BlockSpec((D_MODEL, D_MODEL), lambda i: (0, 0)), row, row, pl.BlockSpec((1, D_MODEL), lambda i: (0, 0))],
        out_specs=[row, row, pl.BlockSpec((8, 128), lambda i: (0, 0)), pl.BlockSpec((1, D_MODEL), lambda i: (0, 0))],
        out_shape=[jax.ShapeDtypeStruct((S, D_MODEL), BF), jax.ShapeDtypeStruct((S, D_MODEL), F32),
                   jax.ShapeDtypeStruct((8, 128), F32), jax.ShapeDtypeStruct((1, D_MODEL), F32)],
        compiler_params=_params(("arbitrary",)),
    )(merged, w_out, x, target, post_g)


def _dh_dx(dproj, w_in, x, dy, pre_g, tm):
    S = x.shape[0]
    nk, tk = dproj.shape[0], dproj.shape[2]

    def body(dp_ref, w_ref, x_ref, dy_ref, g_ref, dx_ref, dpre_ref, acc_ref):
        i, k = pl.program_id(0), pl.program_id(1)

        @pl.when(jnp.logical_and(i == 0, k == 0))
        def _():
            dpre_ref[...] = jnp.zeros_like(dpre_ref)

        @pl.when(k == 0)
        def _():
            acc_ref[...] = jnp.zeros_like(acc_ref)

        acc_ref[...] += _dot(dp_ref[...], w_ref[...], NN)

        @pl.when(k == nk - 1)
        def _():
            dh = acc_ref[...]
            xv = x_ref[...]
            r = lax.rsqrt(jnp.mean(xv * xv, axis=-1, keepdims=True) + EPS)
            nrm = xv * r
            dpre_ref[...] += jnp.sum(dh * nrm, axis=0, keepdims=True)
            dn = dh * g_ref[...]
            dx_ref[...] = r * (dn - nrm * jnp.mean(dn * nrm, axis=-1, keepdims=True)) + dy_ref[...]

    row = pl.BlockSpec((tm, D_MODEL), lambda i, k: (i, 0))
    vec = pl.BlockSpec((1, D_MODEL), lambda i, k: (0, 0))
    return pl.pallas_call(
        body, name="dh_dx", grid=(S // tm, nk),
        in_specs=[pl.BlockSpec((None, tm, tk), lambda i, k: (k, i, 0)), pl.BlockSpec((tk, D_MODEL), lambda i, k: (k, 0)), row, row, vec],
        out_specs=[row, vec],
        out_shape=[jax.ShapeDtypeStruct((S, D_MODEL), F32), jax.ShapeDtypeStruct((1, D_MODEL), F32)],
        scratch_shapes=[pltpu.VMEM((tm, D_MODEL), F32)],
        compiler_params=_params(("arbitrary", "arbitrary"), large=True),
    )(dproj, w_in, x, dy, pre_g)


def _sum_parts(parts, name):
    P, R, C = parts.shape
    tr = max(t for t in range(8, 513, 8) if R % t == 0)

    def body(p_ref, o_ref):
        acc = p_ref[0]
        for j in range(1, P):
            acc = acc + p_ref[j]
        o_ref[...] = acc

    return pl.pallas_call(
        body, name=name, grid=(R // tr,),
        in_specs=[pl.BlockSpec((P, tr, C), lambda i: (0, i, 0))], out_specs=pl.BlockSpec((tr, C), lambda i: (i, 0)),
        out_shape=jax.ShapeDtypeStruct((R, C), F32), compiler_params=_params(("parallel",)),
    )(parts)


def _adamw(lands, sums, chip, w, m, v, name):
    lands = list(lands) if isinstance(lands, (list, tuple)) else [lands]
    sums = list(sums) if isinstance(sums, (list, tuple)) else [sums]
    _, rows, C = lands[0].shape
    R = rows * len(lands)
    tr = 128 if rows % 128 == 0 else rows
    per = rows // tr
    c1 = 1.0 - ADAM_B1 ** ADAM_STEP
    c2 = 1.0 - ADAM_B2 ** ADAM_STEP

    def body(chip_ref, *refs):
        p_refs = refs[:4 * len(lands)]
        w_ref, m_ref, v_ref, g_ref, d_ref, nm_ref, nv_ref = refs[4 * len(lands):]
        g = None
        for q in range(len(lands)):
            gq = p_refs[4 * q + 3][...].astype(F32)
            for j in range(3):
                gq = gq + p_refs[4 * q + j][...].astype(F32)
            g = gq if g is None else jnp.where(pl.program_id(0) // per == q, gq, g)
        nm = ADAM_B1 * m_ref[...] + (1.0 - ADAM_B1) * g
        nv = ADAM_B2 * v_ref[...] + (1.0 - ADAM_B2) * (g * g)
        g_ref[...] = g
        nm_ref[...] = nm
        nv_ref[...] = nv
        d_ref[...] = -ADAM_LR * ((nm / c1) / (jnp.sqrt(nv / c2) + ADAM_EPS) + ADAM_WD * w_ref[...])

    tile = pl.BlockSpec((None, tr, C), lambda i, c_ref: (0, i, 0))
    specs, operands = [], []
    for q in range(len(lands)):
        for k in range(3):
            specs.append(pl.BlockSpec((None, tr, C), (lambda i, c_ref, q=q, k=k: (k + (c_ref[0] <= k).astype(jnp.int32),
                                                                                  jnp.clip(i - q * per, 0, per - 1), 0))))
            operands.append(lands[q])
        specs.append(pl.BlockSpec((None, tr, C), (lambda i, c_ref, q=q: (c_ref[0], jnp.clip(i - q * per, 0, per - 1), 0))))
        operands.append(sums[q])
    return pl.pallas_call(
        body, name=name,
        grid_spec=pltpu.PrefetchScalarGridSpec(num_scalar_prefetch=1, grid=(R // tr,), in_specs=specs + [tile, tile, tile],
                                               out_specs=[tile] * 4),
        out_shape=[jax.ShapeDtypeStruct((1, R, C), F32)] * 4, compiler_params=_params(("parallel",)),
    )(chip, *operands, w, m, v)


def _adamw_small(gs, ws, ms, vs):
    n = len(ws)
    c1 = 1.0 - ADAM_B1 ** ADAM_STEP
    c2 = 1.0 - ADAM_B2 ** ADAM_STEP

    def flat2(a):
        return a.reshape(-1, a.shape[-1])

    def body(*refs):
        ins, outs = refs[:4 * n], refs[4 * n:]
        for a in range(n):
            g, w, m, v = (ins[k * n + a][...] for k in range(4))
            nm = ADAM_B1 * m + (1.0 - ADAM_B1) * g
            nv = ADAM_B2 * v + (1.0 - ADAM_B2) * (g * g)
            outs[a][...] = g
            outs[n + a][...] = -ADAM_LR * ((nm / c1) / (jnp.sqrt(nv / c2) + ADAM_EPS) + ADAM_WD * w)
            outs[2 * n + a][...] = nm
            outs[3 * n + a][...] = nv

    shapes = [flat2(w).shape for w in ws]
    out = pl.pallas_call(
        body, name="adamw_small", out_shape=[jax.ShapeDtypeStruct(sh, F32) for sh in shapes] * 4,
        compiler_params=_params(),
    )(*[g.reshape(sh) for g, sh in zip(gs, shapes)], *[flat2(a) for a in (*ws, *ms, *vs)])
    return [[out[k * n + a].reshape(ws[a].shape) for a in range(n)] for k in range(4)]


def _project(h, w_t):
    S = h.shape[0]
    n_tiles = D_IN // SEG_TILE
    ranges = [(c0 // SEG_TILE, (c0 + width) // SEG_TILE) for _, c0, width, _ in SEGMENTS]

    def body(h_ref, w_ref, *outs):
        j = pl.program_id(0)
        prod = _dot(h_ref[...], w_ref[...], NT)
        for (j0, j1), (_, _, _, dt), o_ref in zip(ranges, SEGMENTS, outs):
            @pl.when(jnp.logical_and(j >= j0, j < j1))
            def _(o_ref=o_ref, dt=dt):
                o_ref[...] = prod.astype(dt)

    out_shapes, out_specs = [], []
    for (j0, j1), (name, _, width, dt) in zip(ranges, SEGMENTS):
        if name == "gl":
            per = (j1 - j0) // 3
            out_shapes.append(jax.ShapeDtypeStruct((3, S, width // 3), dt))
            out_specs.append(pl.BlockSpec((None, S, SEG_TILE), (lambda j, j0=j0, j1=j1, per=per: (
                jnp.clip(j - j0, 0, j1 - j0 - 1) // per, 0, jnp.clip(j - j0, 0, j1 - j0 - 1) % per))))
        else:
            out_shapes.append(jax.ShapeDtypeStruct((S, width), dt))
            out_specs.append(pl.BlockSpec((S, SEG_TILE), (lambda j, j0=j0, j1=j1: (0, jnp.clip(j - j0, 0, j1 - j0 - 1)))))
    outs = pl.pallas_call(
        body, name="proj", grid=(n_tiles,),
        in_specs=[pl.BlockSpec((S, D_MODEL), lambda j: (0, 0)), pl.BlockSpec((SEG_TILE, D_MODEL), lambda j: (j, 0))],
        out_specs=out_specs, out_shape=out_shapes,
        compiler_params=_params(("arbitrary",), large=True),
    )(h, w_t)
    return {name: o for (name, _, _, _), o in zip(SEGMENTS, outs)}


def _forward_a(x, mem, pre_g, mem_g, w_in, conv_w, conv_b, w_a, b_a, w_x, b_x, lam, sinks, rel_bias):
    S = x.shape[0]
    st = dict(T=min(512, S // 2), tm=min(512, S), bucket=_rel_bucket_map())
    st["h"] = _rms_fwd(x, pre_g, "pre_norm")
    st["memn"] = _rms_fwd(mem, mem_g, "mem_norm")
    seg = st["seg"] = _project(st["h"], w_in)
    st["h_rg"], st["y_rg"] = _rglru_fwd(seg["xr"], seg["g_rg"], conv_w, conv_b, w_a, b_a, w_x, b_x, lam, st["T"])
    st["o_swa"], st["y_swa"] = _swa_fwd(seg["q_s"], seg["kv"], seg["g_swa"], st["bucket"], rel_bias, _sink_column(sinks))
    return st


def _forward_b(st, x, target, post_g, w_memkv, wbr, w_out):
    S = x.shape[0]
    M = st["memn"].shape[0]
    seg = st["seg"]
    st["mkv"] = _matmul(st["memn"], w_memkv, "nn", M, 2 * D_MEM, D_MODEL, M, 512, D_MODEL, BF, "mem_kv")
    st["o_mem"], st["y_mem"] = _mem_fwd(seg["q_m"], st["mkv"], seg["g_mem"])
    st["ys"] = (st["y_rg"], st["y_swa"], st["y_mem"])
    st["merged"] = _merge_fwd(st["ys"], wbr, seg["gl"], st["tm"])
    st["dout"], st["dy"], st["loss"], st["dpost"] = _out_loss(st["merged"], w_out, x, target, post_g, min(256, S))
    return st


def _backward_a1(st, wbr, w_out):
    S = st["h"].shape[0]
    seg, ys, tm = st["seg"], st["ys"], st["tm"]
    st["dw_out"] = _matmul(st["merged"], st["dout"], "tn", D_MODEL, D_MODEL, S, 256, D_MODEL, S, BF, "dw_out", out_blocked="row")
    dgl0, dgl1, dgl2, dp0, dp1, dp2 = _merge_bwd(st["dout"], w_out, ys, wbr, seg["gl"], tm)
    st["dgl"] = (dgl0, dgl1, dgl2)
    dys, dwbr = [], []
    for i, dp in enumerate((dp0, dp1, dp2)):
        dys.append(_matmul(dp, wbr[i], "nn", S, 1024, D_MODEL, tm, 1024, D_MODEL, F32, "dy_br%d" % i))
        dwbr.append(_matmul(ys[i], dp, "tn", 1024, D_MODEL, S, 1024, 256, S, BF, "dw_br%d" % i, out_blocked="col"))
    st["dys"], st["dwbr"] = dys, dwbr
    return st


def _backward_a2(st, mem, w_memkv, conv_w, conv_b, w_a, b_a, w_x, b_x, lam):
    M = mem.shape[0]
    seg, dys = st["seg"], st["dys"]
    st["dq_m"], st["dg_mem"], dmkv = _mem_bwd(seg["q_m"], st["mkv"], seg["g_mem"], st["o_mem"], dys[2])
    dmkv_b = dmkv.astype(BF)
    st["dw_memkv"] = _matmul(st["memn"], dmkv_b, "tn", D_MODEL, 2 * D_MEM, M, 256, 2 * D_MEM, M, BF, "dw_memkv", out_blocked="row")
    dmemn = _matmul(dmkv_b, w_memkv, "nt", M, D_MODEL, 2 * D_MEM, M, 512, 2 * D_MEM, F32, "dmemn")
    st["dmem_g"] = _rms_gain_grad(dmemn, mem, "dmem_gain")
    st["dxr"], st["dg_rg"], st["dw_a"], st["dw_x"], st["dvec"] = _rglru_bwd(
        seg["xr"], seg["g_rg"], st["h_rg"], dys[0], conv_w, conv_b, w_a, b_a, w_x, b_x, lam, st["T"])
    return st


def _backward_b(st, rel_bias, sinks):
    seg = st["seg"]
    dq_s, dg_swa, dkv, dsinks, drel = _swa_bwd(seg["q_s"], seg["kv"], seg["g_swa"], st["o_swa"], st["dys"][1],
                                               st["bucket"], rel_bias, _sink_column(sinks))
    st["dsinks"], st["drel"] = dsinks.reshape(1, SWA_HEADS), drel.reshape(REL_BUCKETS, SWA_HEADS)
    dproj = jnp.concatenate([st["dxr"], st["dg_rg"], dq_s, dkv.astype(BF), dg_swa, st["dq_m"], st["dg_mem"], *st["dgl"]], axis=1)
    st["dproj"] = jnp.transpose(dproj.reshape(dproj.shape[0], N_DEV, D_IN // N_DEV), (1, 0, 2))
    return st


def _dw_in_half(st, half, dep=None):
    S = st["h"].shape[0]
    return _matmul(st["h"], st["dproj"], "tn", D_MODEL // 2, D_IN, S, 512, D_IN // N_DEV, S, BF, "dw_in%d" % half, a_moff=2 * half,
                   b_blocked=True, out_blocked="col", dep=dep)


def _owner_blocks(a):
    return jnp.swapaxes(a.reshape((4, 2) + a.shape[1:]), 0, 1)


def _local_step(x, mem, target, pre_g, post_g, mem_g, w_in, conv_w, conv_b, w_a, b_a, w_x, b_x, lam, sinks, rel_bias,
                w_memkv, wbr, w_out):
    st = _forward_a(x, mem, pre_g, mem_g, w_in, conv_w, conv_b, w_a, b_a, w_x, b_x, lam, sinks, rel_bias)
    st = _forward_b(st, x, target, post_g, w_memkv, wbr, w_out)
    st = _backward_a1(st, wbr, w_out)
    st = _backward_a2(st, mem, w_memkv, conv_w, conv_b, w_a, b_a, w_x, b_x, lam)
    st = _backward_b(st, rel_bias, sinks)
    st["dw_in"] = [_dw_in_half(st, 0), _dw_in_half(st, 1)]
    st["grad_x"], st["dpre"] = _dh_dx(st["dproj"], w_in, x, st["dy"], pre_g, st["tm"])
    return st


def _pad_rows(a, rows):
    a = a.reshape(-1, 128) if a.shape[-1] % 128 == 0 else jnp.pad(a, ((0, 0), (0, 128 - a.shape[-1])))
    return jnp.pad(a, ((0, rows - a.shape[0]), (0, 0))) if a.shape[0] < rows else a


def kernel(x, mem, pre_norm_g, post_norm_g, mem_norm_g, w_in, conv_w, conv_b, w_rg_a, b_rg_a, w_rg_x, b_rg_x, lru_lambda, swa_sinks, rel_bias, w_mem_kv, w_br_rg, w_br_swa, w_br_mem, w_out, loss_target, m_pre_norm_g, m_post_norm_g, m_mem_norm_g, m_w_in, m_conv_w, m_conv_b, m_w_rg_a, m_b_rg_a, m_w_rg_x, m_b_rg_x, m_lru_lambda, m_swa_sinks, m_rel_bias, m_w_mem_kv, m_w_br_rg, m_w_br_swa, m_w_br_mem, m_w_out, v_pre_norm_g, v_post_norm_g, v_mem_norm_g, v_w_in, v_conv_w, v_conv_b, v_w_rg_a, v_b_rg_a, v_w_rg_x, v_b_rg_x, v_lru_lambda, v_swa_sinks, v_rel_bias, v_w_mem_kv, v_w_br_rg, v_w_br_swa, v_w_br_mem, v_w_out):
    cx, cy, cc = lax.axis_index("x"), lax.axis_index("y"), lax.axis_index("c")
    me = 4 * cx + 2 * cy + cc
    chip = 2 * cx + cy
    core = jnp.reshape(cc, (1,)).astype(jnp.int32)
    x0, mem0 = x[0], mem[0]
    w_a_b, w_x_b = w_rg_a[0].astype(BF), w_rg_x[0].astype(BF)

    def landing(own, slot, slots):
        return lax.dynamic_update_slice(lax.empty((slots,) + own.shape, own.dtype), own[None], (slot,) + (0,) * own.ndim)


    def swap_start(parts, tag):
        return _exchange_start(parts, [lax.empty(p.shape[1:], p.dtype) for p in parts], _plan_swap(len(parts)), "swap_%s_start" % tag)

    def scatter_start(swap, after, tag, prefill=()):
        s_send, s_recv, parts, got, _ = swap
        got = _exchange_wait(s_send, s_recv, parts, got, _plan_swap(len(parts)), after, "swap_%s_wait" % tag)
        sums = [_pair_sum(p, g, core, "scatter_%s_sum%d" % (tag, i)) for i, (p, g) in enumerate(zip(parts, got))]
        lands = [landing(lax.dynamic_index_in_dim(s, chip, 0, keepdims=False), chip, 4) if i in prefill
                 else lax.empty(s.shape, s.dtype) for i, s in enumerate(sums)]
        return _exchange_start(sums, lands, _plan_scatter(len(sums)), "scatter_%s_start" % tag)

    def zero_after(a):
        return jnp.minimum(jnp.abs(a.reshape(-1)[0].astype(F32)), 0.0)

    g_in, g_cw = _all_gather_relayed([jnp.transpose(w_in[0]).astype(BF), conv_w[0]], [True, False], "gather_w_in")
    w_in_f = g_in.reshape(D_IN, D_MODEL)
    conv_w_f = jnp.transpose(g_cw, (1, 0, 2)).reshape(CONV_W, D_RNN)

    after_first = zero_after(g_cw).astype(BF)
    rest = [w.astype(BF) + after_first for w in (w_mem_kv[0], jnp.transpose(w_br_rg[0]), jnp.transpose(w_br_swa[0]),
                                                 jnp.transpose(w_br_mem[0]), w_out[0])]
    kinds = ["lead"] * len(rest)
    plan_g = _plan_gather(kinds)
    zones = _place_own([lax.empty((N_DEV,) + w.shape, w.dtype) for w in rest], rest, jnp.reshape(me, (1,)).astype(jnp.int32), "gather_rest_own")
    g_send, g_recv, g_src, g_land, g_token = _exchange_start(rest, zones, plan_g, "gather_rest_start")
    st = _forward_a(x0, mem0, pre_norm_g + g_token[0:1, 0:1], mem_norm_g, w_in_f, conv_w_f, conv_b, w_a_b, b_rg_a, w_x_b, b_rg_x,
                    lru_lambda, swa_sinks, rel_bias)
    g_land = _exchange_wait(g_send, g_recv, g_src, g_land, plan_g, st["y_swa"], "gather_rest_wait")
    g_land = _forward_to_sibling(g_land, kinds, "gather_rest_forward")
    w_memkv_f = g_land[0].reshape(D_MODEL, 2 * D_MEM)
    wbr = tuple(g_land[i].reshape(D_MODEL, D_RNN) for i in (1, 2, 3))
    w_out_f = g_land[4].reshape(D_MODEL, D_MODEL)

    st = _forward_b(st, x0, loss_target[0], post_norm_g, w_memkv_f, wbr, w_out_f)
    st = _backward_a1(st, wbr, w_out_f)
    parts_a = [st["dw_out"], st["dwbr"][0], st["dwbr"][1], st["dwbr"][2]]
    plan_a = _plan_scatter(len(parts_a))
    swap_a = swap_start(parts_a, "a")
    st = _backward_a2(st, mem0, w_memkv_f, conv_w_f, conv_b + swap_a[4][0:1, 0:1], w_a_b, b_rg_a, w_x_b, b_rg_x, lru_lambda)
    a_send, a_recv, a_src, a_land, a_token = scatter_start(swap_a, st["dxr"], "a")
    parts_c = [st["dw_memkv"], _owner_blocks(st["dw_a"]), _owner_blocks(st["dw_x"])]
    plan_c = _plan_scatter(len(parts_c))
    swap_c = swap_start(parts_c, "c")

    st = _backward_b(st, rel_bias, swa_sinks + swap_c[4][0:1, 0:1] + a_token[0:1, 0:1])
    c_send, c_recv, c_src, c_land, c_token = scatter_start(swap_c, st["dproj"], "c", prefill=(1, 2))
    plan_b = _plan_scatter(1)

    def dw_in_parts(half, dep):
        dwh = _dw_in_half(st, half, dep)
        return dwh, [dwh]

    dw0, parts_b0 = dw_in_parts(0, c_token)
    swap_b0 = swap_start(parts_b0, "b0")
    a_land = _exchange_wait(a_send, a_recv, a_src, a_land, plan_a, swap_b0[4], "scatter_a_wait")
    big = [None] * 6

    chip1 = jnp.reshape(chip, (1,)).astype(jnp.int32)

    def adamw_big(j, land, own, wt, mt, vt):
        big[j] = _adamw(land, own, chip1, wt, mt, vt, "adamw_big%d" % j)

    adamw_big(5, a_land[0], a_src[0], w_out, m_w_out, v_w_out)
    adamw_big(2, a_land[1], a_src[1], w_br_rg, m_w_br_rg, v_w_br_rg)
    adamw_big(3, a_land[2], a_src[2], w_br_swa, m_w_br_swa, v_w_br_swa)
    halves = [scatter_start(swap_b0, big[3][1], "b0")]
    dw1, parts_b1 = dw_in_parts(1, halves[0][4])
    swap_b1 = swap_start(parts_b1, "b1")
    c_land = _exchange_wait(c_send, c_recv, c_src, c_land, plan_c, swap_b1[4], "scatter_c_wait")
    g_wa_blk = _sum_parts(c_land[1], "sum_w_rg_a")
    g_wx_blk = _sum_parts(c_land[2], "sum_w_rg_x")
    adamw_big(4, a_land[3], a_src[3], w_br_mem, m_w_br_mem, v_w_br_mem)
    adamw_big(1, c_land[0], c_src[0], w_mem_kv, m_w_mem_kv, v_w_mem_kv)
    halves.append(scatter_start(swap_b1, big[1][1], "b1"))
    grad_x, dpre = _dh_dx(st["dproj"], w_in_f, x0, st["dy"], pre_norm_g + halves[1][4][0:1, 0:1], st["tm"])
    after, b_lands, b_sums = grad_x, [], []
    for half, (b_send, b_recv, b_src, b_land, _) in enumerate(halves):
        b_lands.append(_exchange_wait(b_send, b_recv, b_src, b_land, plan_b, after, "scatter_b%d_wait" % half)[0])
        b_sums.append(b_src[0])
        after = b_lands[-1]
    big[0] = _adamw(b_lands, b_sums, chip1, w_in, m_w_in, v_w_in, "adamw_big0")
    links_free = jnp.minimum(jnp.abs(big[0][0][0, 0, 0]), 0.0)

    pack = jnp.concatenate([dpre.reshape(16, 128), st["dpost"].reshape(16, 128), st["dmem_g"].reshape(16, 128),
                            st["dvec"].reshape(64, 128), _pad_rows(st["dsinks"], 8), _pad_rows(st["drel"], 32), g_wa_blk, g_wx_blk], axis=0) + links_free
    gathered = _all_gather([pack], "gather_small")[0]
    gs = _sum_parts(gathered, "sum_small")
    g_pre, g_post, g_memg = gs[0:16].reshape(1, D_MODEL), gs[16:32].reshape(1, D_MODEL), gs[32:48].reshape(1, D_MODEL)
    gvec = gs[48:112].reshape(8, D_RNN)
    g_conv_w = lax.dynamic_slice(gvec[0:CONV_W], (0, me * RNN_BLOCK), (CONV_W, RNN_BLOCK))
    g_conv_b, g_b_a, g_b_x, g_lam = gvec[4:5], gvec[5:6], gvec[6:7], gvec[7:8]
    g_sinks = gs[112:113, :SWA_HEADS]
    g_rel = gs[120:152, :SWA_HEADS]
    g_w_a = gathered[:, 152:280]
    g_w_x = gathered[:, 280:408]

    g_small = (g_pre, g_post, g_memg, g_conv_b, g_b_a, g_b_x, g_lam, g_w_a, g_w_x, g_sinks, g_rel, g_conv_w)
    w_small = (pre_norm_g, post_norm_g, mem_norm_g, conv_b, b_rg_a, b_rg_x, lru_lambda, w_rg_a, w_rg_x, swa_sinks, rel_bias, conv_w)
    m_small = (m_pre_norm_g, m_post_norm_g, m_mem_norm_g, m_conv_b, m_b_rg_a, m_b_rg_x, m_lru_lambda, m_w_rg_a, m_w_rg_x, m_swa_sinks, m_rel_bias, m_conv_w)
    v_small = (v_pre_norm_g, v_post_norm_g, v_mem_norm_g, v_conv_b, v_b_rg_a, v_b_rg_x, v_lru_lambda, v_w_rg_a, v_w_rg_x, v_swa_sinks, v_rel_bias, v_conv_w)
    sm = _adamw_small(g_small, w_small, m_small, v_small)

    loss_total = lax.psum(st["loss"][0, 0], AXES)

    def leaves(k):
        s = sm[k]
        return [s[0], s[1], s[2], big[0][k], s[11], s[3], s[7], s[4], s[8], s[5], s[6], s[9], s[10],
                big[1][k], big[2][k], big[3][k], big[4][k], big[5][k]]

    return (loss_total, grad_x[None], *leaves(0), *leaves(1), *leaves(2), *leaves(3))
```

```python
import math

import jax
import jax.numpy as jnp
import numpy as np
from jax import lax
from jax.experimental import pallas as pl
from jax.experimental.pallas import tpu as pltpu

F32, BF = jnp.float32, jnp.bfloat16
MESH = pl.DeviceIdType.MESH
AXES = ("x", "y", "c")
N_DEV = 8

D_MODEL = 2048
D_RNN = 1024
RNN_BLOCKS = 8
RNN_BLOCK = 128
CONV_W = 4
LRU_C = 8.0
SWA_HEADS = 16
SWA_KV_HEADS = 2
SWA_HD = 64
WINDOW = 128
MEM_HEADS = 4
MEM_HD = 256
D_MEM = 1024
REL_BUCKETS = 32
REL_MAX_DIST = 128
EPS = 1e-6
NEG_INF = -1e30
D_IN = 12544
SEGMENTS = (("xr", 0, 1024, F32), ("g_rg", 1024, 1024, F32), ("q_s", 2048, 1024, BF), ("kv", 3072, 256, BF),
            ("g_swa", 3328, 1024, F32), ("q_m", 4352, 1024, BF), ("g_mem", 5376, 1024, F32), ("gl", 6400, 6144, F32))
SEG_TILE = 256

ADAM_LR, ADAM_B1, ADAM_B2, ADAM_EPS, ADAM_WD, ADAM_STEP = 0.001, 0.9, 0.999, 1e-08, 0.01, 10

NN = (((1,), (0,)), ((), ()))
NT = (((1,), (1,)), ((), ()))
TN = (((0,), (0,)), ((), ()))
MIB = 2 ** 20


def _dot(a, b, dn):
    return lax.dot_general(a, b, dn, preferred_element_type=F32)


VMEM_LIMIT_MIB = 48
VMEM_LIMIT_LARGE_MIB = 56


def _params(sem=None, large=False):
    return pltpu.CompilerParams(dimension_semantics=sem, vmem_limit_bytes=(VMEM_LIMIT_LARGE_MIB if large else VMEM_LIMIT_MIB) * MIB)


def _sigmoid(z):
    return 1.0 / (1.0 + jnp.exp(-z))


def _softplus(z):
    return jnp.maximum(z, 0.0) + jnp.log(1.0 + jnp.exp(-jnp.abs(z)))


def _expm1(z):
    p = z * (1.0 + z * (0.5 + z * (1.0 / 6 + z * (1.0 / 24 + z * (1.0 / 120 + z * (1.0 / 720 + z * (1.0 / 5040 + z / 40320)))))))
    return jnp.where(jnp.abs(z) < 0.3, p, jnp.exp(z) - 1.0)


def _flat(p):
    return 4 * p[0] + 2 * p[1] + p[2]


def _all_gather(arrs, name):
    n = len(arrs)

    def body(*refs):
        ins, outs = refs[:n], refs[n:2 * n]
        send_sems, recv_sems, local_sems = refs[2 * n:]
        x, y, c = lax.axis_index("x"), lax.axis_index("y"), lax.axis_index("c")
        me, sibling = (x, y, c), (x, y, 1 - c)
        chips = [(1 - x, y), (x, 1 - y), (1 - x, 1 - y)]

        def copy(a, k, block, to, src=None):
            dst = outs[a].at[_flat(block)]
            return pltpu.make_async_remote_copy(src_ref=dst if src is None else src, dst_ref=dst,
                                                send_sem=send_sems.at[a * 7 + k], recv_sem=recv_sems.at[a * 7 + k],
                                                device_id=to, device_id_type=MESH)

        mine = [pltpu.make_async_copy(ins[a], outs[a].at[_flat(me)], local_sems.at[a]) for a in range(n)]
        for cp in mine:
            cp.start()
        first = []
        for a in range(n):
            first += [copy(a, 1 + j, me, (*chip, c), src=ins[a]) for j, chip in enumerate(chips)]
            first.append(copy(a, 0, me, sibling, src=ins[a]))
        for cp in first:
            cp.start()
        passed = []
        for j, chip in enumerate(chips):
            for a in range(n):
                copy(a, 1 + j, (*chip, c), me).wait_recv()
                fw = copy(a, 4 + j, (*chip, c), sibling)
                fw.start()
                passed.append(fw)
        for a in range(n):
            copy(a, 0, sibling, me).wait_recv()
            for j, chip in enumerate(chips):
                copy(a, 4 + j, (*chip, 1 - c), me).wait_recv()
        for cp in first + passed:
            cp.wait_send()
        for cp in mine:
            cp.wait()

    any_spec = pl.BlockSpec(memory_space=pl.ANY)
    return pl.pallas_call(
        body, name=name,
        out_shape=[jax.ShapeDtypeStruct((N_DEV,) + a.shape, a.dtype) for a in arrs],
        in_specs=[any_spec] * n, out_specs=[any_spec] * n,
        scratch_shapes=[pltpu.SemaphoreType.DMA((7 * n,)), pltpu.SemaphoreType.DMA((7 * n,)), pltpu.SemaphoreType.DMA((n,))],
    )(*arrs)


def _all_gather_relayed(arrs, relay, name):
    n = len(arrs)
    K = 9

    def body(*refs):
        ins, outs = refs[:n], refs[n:2 * n]
        send_sems, recv_sems, local_sems = refs[2 * n:]
        x, y, c = lax.axis_index("x"), lax.axis_index("y"), lax.axis_index("c")
        me, sib = (x, y, c), (x, y, 1 - c)
        xn, yn, dg = (1 - x, y, c), (x, 1 - y, c), (1 - x, 1 - y, c)

        def other(p):
            return (p[0], p[1], 1 - p[2])

        def rows(a, half):
            h = arrs[a].shape[0] // 2
            return pl.ds(half * h, h)

        def copy(a, k, block, to, half=None, src=None):
            dst = outs[a].at[_flat(block)]
            if half is not None:
                dst = dst.at[rows(a, half)]
            return pltpu.make_async_remote_copy(src_ref=dst if src is None else src, dst_ref=dst,
                                                send_sem=send_sems.at[a * K + k], recv_sem=recv_sems.at[a * K + k],
                                                device_id=to, device_id_type=MESH)

        mine = [pltpu.make_async_copy(ins[a], outs[a].at[_flat(me)], local_sems.at[a]) for a in range(n)]
        for cp in mine:
            cp.start()
        sends = []

        def start(cp):
            cp.start()
            sends.append(cp)

        for a in range(n):
            start(copy(a, 1, me, xn, src=ins[a]))
            start(copy(a, 2, me, yn, src=ins[a]))
            if not relay[a]:
                start(copy(a, 3, me, dg, src=ins[a]))
            start(copy(a, 0, me, sib, src=ins[a]))
        for a in range(n):
            copy(a, 1, xn, me).wait_recv()
            if relay[a]:
                start(copy(a, 3, xn, yn, half=0))
            start(copy(a, 5, xn, sib))
        for a in range(n):
            copy(a, 2, yn, me).wait_recv()
            if relay[a]:
                start(copy(a, 4, yn, xn, half=1))
            start(copy(a, 6, yn, sib))
        for a in range(n):
            if relay[a]:
                copy(a, 3, dg, me, half=0).wait_recv()
                start(copy(a, 7, dg, sib, half=0))
                copy(a, 4, dg, me, half=1).wait_recv()
                start(copy(a, 8, dg, sib, half=1))
            else:
                copy(a, 3, dg, me).wait_recv()
                start(copy(a, 7, dg, sib))
        for a in range(n):
            copy(a, 0, sib, me).wait_recv()
            copy(a, 5, other(xn), me).wait_recv()
            copy(a, 6, other(yn), me).wait_recv()
            if relay[a]:
                copy(a, 7, other(dg), me, half=0).wait_recv()
                copy(a, 8, other(dg), me, half=1).wait_recv()
            else:
                copy(a, 7, other(dg), me).wait_recv()
        for cp in sends:
            cp.wait_send()
        for cp in mine:
            cp.wait()

    any_spec = pl.BlockSpec(memory_space=pl.ANY)
    return pl.pallas_call(
        body, name=name,
        out_shape=[jax.ShapeDtypeStruct((N_DEV,) + a.shape, a.dtype) for a in arrs],
        in_specs=[any_spec] * n, out_specs=[any_spec] * n,
        scratch_shapes=[pltpu.SemaphoreType.DMA((K * n,)), pltpu.SemaphoreType.DMA((K * n,)), pltpu.SemaphoreType.DMA((n,))],
    )(*arrs)


def _chip_peers(x, y):
    return [(1 - x, y), (x, 1 - y), (1 - x, 1 - y)]


def _chip(p):
    return 2 * p[0] + p[1]


def _plan_gather(kinds):
    def plan(x, y, c):
        out = []
        for a, kind in enumerate(kinds):
            for peer in [(x, y, 1 - c)] + [(*ch, c) for ch in _chip_peers(x, y)]:
                out.append((a, None, (kind, _flat((x, y, c))), peer, (kind, _flat(peer))))
        return out
    return plan


def _plan_swap(n):
    def plan(x, y, c):
        return [(a, 1 - c, ("all", 0), (x, y, 1 - c), ("all", 0)) for a in range(n)]
    return plan


def _slot(ref, where):
    kind, k = where
    if kind == "all":
        return ref
    if kind == "lead":
        return ref.at[k]
    return ref.at[:, pl.ds(pl.multiple_of(k * 256, 256), 256)]


def _plan_scatter(n):
    def plan(x, y, c):
        out = []
        for a in range(n):
            for ch in _chip_peers(x, y):
                out.append((a, _chip(ch), ("lead", _chip((x, y))), (*ch, c), ("lead", _chip(ch))))
        return out
    return plan


HBM_SPEC = pl.BlockSpec(memory_space=pltpu.HBM)
SEM_SPEC = pl.BlockSpec(memory_space=pltpu.SEMAPHORE)


def _in_hbm(a):
    return pltpu.with_memory_space_constraint(a, pltpu.HBM)


def _exchange_start(srcs, lands, plan, name):
    n = len(srcs)
    count = len(plan(0, 0, 0))

    def body(*refs):
        src_refs, land_refs = refs[:n], refs[n:2 * n]
        send_sems, recv_sems = refs[2 * n], refs[2 * n + 1]
        token = refs[-1]
        x, y, c = lax.axis_index("x"), lax.axis_index("y"), lax.axis_index("c")
        for k, (a, si, di, peer, _) in enumerate(plan(x, y, c)):
            src = src_refs[a] if si is None else src_refs[a].at[si]
            pltpu.make_async_remote_copy(src_ref=src, dst_ref=_slot(land_refs[a], di), send_sem=send_sems.at[k],
                                         recv_sem=recv_sems.at[k], device_id=peer, device_id_type=MESH).start()
        token[...] = jnp.zeros_like(token)

    out = pl.pallas_call(
        body, name=name,
        out_shape=(pltpu.SemaphoreType.DMA((count,)), pltpu.SemaphoreType.DMA((count,)),
                   *[pltpu.HBM(a.shape, a.dtype) for a in lands], jax.ShapeDtypeStruct((8, 128), F32)),
        in_specs=[HBM_SPEC] * (2 * n),
        out_specs=(SEM_SPEC, SEM_SPEC, *([HBM_SPEC] * n), pl.BlockSpec(memory_space=pltpu.VMEM)),
        input_output_aliases={n + i: 2 + i for i in range(n)},
        compiler_params=pltpu.CompilerParams(has_side_effects=pltpu.SideEffectType.DATAFLOW_SIDE_EFFECTING),
    )(*[_in_hbm(a) for a in srcs], *[_in_hbm(a) for a in lands])
    return out[0], out[1], list(srcs), list(out[2:2 + n]), out[-1]


def _exchange_wait(send_sems, recv_sems, srcs, lands, plan, after, name):
    n = len(srcs)

    def body(*refs):
        src_refs, land_refs = refs[:n], refs[n:2 * n]
        send_sems, recv_sems = refs[2 * n], refs[2 * n + 1]
        x, y, c = lax.axis_index("x"), lax.axis_index("y"), lax.axis_index("c")
        for k, (a, si, _, peer, ri) in enumerate(plan(x, y, c)):
            src = src_refs[a] if si is None else src_refs[a].at[si]
            cp = pltpu.make_async_remote_copy(src_ref=src, dst_ref=_slot(land_refs[a], ri), send_sem=send_sems.at[k],
                                              recv_sem=recv_sems.at[k], device_id=peer, device_id_type=MESH)
            cp.wait_send()
            cp.wait_recv()

    out = pl.pallas_call(
        body, name=name,
        out_shape=tuple(pltpu.HBM(a.shape, a.dtype) for a in lands),
        in_specs=[HBM_SPEC] * (2 * n) + [SEM_SPEC, SEM_SPEC, pl.BlockSpec(memory_space=pl.ANY)],
        out_specs=tuple([HBM_SPEC] * n),
        input_output_aliases={n + i: i for i in range(n)},
        compiler_params=pltpu.CompilerParams(has_side_effects=pltpu.SideEffectType.DATAFLOW_SIDE_EFFECTING),
    )(*[_in_hbm(a) for a in srcs], *lands, send_sems, recv_sems, after)
    return list(out)


def _forward_to_sibling(lands, kinds, name):
    n = len(lands)

    def body(*refs):
        in_refs, out_refs = refs[:n], refs[n:2 * n]
        send_sems, recv_sems = refs[2 * n:]
        x, y, c = lax.axis_index("x"), lax.axis_index("y"), lax.axis_index("c")
        sibling = (x, y, 1 - c)

        def copy(a, j, slot):
            return pltpu.make_async_remote_copy(src_ref=_slot(in_refs[a], (kinds[a], slot)), dst_ref=_slot(out_refs[a], (kinds[a], slot)),
                                                send_sem=send_sems.at[a * 3 + j], recv_sem=recv_sems.at[a * 3 + j],
                                                device_id=sibling, device_id_type=MESH)

        sends = [copy(a, j, _flat((*ch, c))) for a in range(n) for j, ch in enumerate(_chip_peers(x, y))]
        for cp in sends:
            cp.start()
        for a in range(n):
            for j, ch in enumerate(_chip_peers(x, y)):
                copy(a, j, _flat((*ch, 1 - c))).wait_recv()
        for cp in sends:
            cp.wait_send()

    any_spec = pl.BlockSpec(memory_space=pl.ANY)
    return pl.pallas_call(
        body, name=name, out_shape=[jax.ShapeDtypeStruct(a.shape, a.dtype) for a in lands],
        in_specs=[any_spec] * n, out_specs=[any_spec] * n, input_output_aliases={a: a for a in range(n)},
        scratch_shapes=[pltpu.SemaphoreType.DMA((3 * n,)), pltpu.SemaphoreType.DMA((3 * n,))],
    )(*lands)


def _place_own(zones, owns, slot, name):
    n = len(zones)

    def body(slot_ref, *refs):
        for a in range(n):
            refs[2 * n + a][...] = refs[a][...]

    return pl.pallas_call(
        body, name=name,
        grid_spec=pltpu.PrefetchScalarGridSpec(
            num_scalar_prefetch=1, grid=(1,),
            in_specs=[pl.BlockSpec(o.shape, lambda i, s_ref: (0, 0)) for o in owns] + [pl.BlockSpec(memory_space=pl.ANY)] * n,
            out_specs=[pl.BlockSpec((None,) + o.shape, lambda i, s_ref: (s_ref[0], 0, 0)) for o in owns]),
        out_shape=[jax.ShapeDtypeStruct(z.shape, z.dtype) for z in zones],
        input_output_aliases={1 + n + a: a for a in range(n)},
        compiler_params=_params(("arbitrary",)),
    )(slot, *owns, *zones)


def _swap_with_sibling(parts, name):
    n = len(parts)

    def body(*refs):
        in_refs, out_refs = refs[:n], refs[n:2 * n]
        send_sems, recv_sems = refs[2 * n:]
        x, y, c = lax.axis_index("x"), lax.axis_index("y"), lax.axis_index("c")
        sends = [pltpu.make_async_remote_copy(src_ref=in_refs[a].at[1 - c], dst_ref=out_refs[a], send_sem=send_sems.at[a],
                                              recv_sem=recv_sems.at[a], device_id=(x, y, 1 - c), device_id_type=MESH)
                 for a in range(n)]
        for cp in sends:
            cp.start()
        for cp in sends:
            cp.wait()

    any_spec = pl.BlockSpec(memory_space=pl.ANY)
    return pl.pallas_call(
        body, name=name, out_shape=[jax.ShapeDtypeStruct(a.shape[1:], a.dtype) for a in parts],
        in_specs=[any_spec] * n, out_specs=[any_spec] * n,
        scratch_shapes=[pltpu.SemaphoreType.DMA((n,)), pltpu.SemaphoreType.DMA((n,))],
    )(*parts)


def _pair_sum(parts, got, core, name):
    _, _, R, C = parts.shape
    tr = 256 if R % 256 == 0 else R

    def body(c_ref, p_ref, g_ref, o_ref):
        o_ref[...] = (p_ref[...].astype(F32) + g_ref[...].astype(F32)).astype(o_ref.dtype)

    return pl.pallas_call(
        body, name=name,
        grid_spec=pltpu.PrefetchScalarGridSpec(
            num_scalar_prefetch=1, grid=(4, R // tr),
            in_specs=[pl.BlockSpec((None, None, tr, C), lambda j, i, c_ref: (c_ref[0], j, i, 0)),
                      pl.BlockSpec((None, tr, C), lambda j, i, c_ref: (j, i, 0))],
            out_specs=pl.BlockSpec((None, tr, C), lambda j, i, c_ref: (j, i, 0))),
        out_shape=jax.ShapeDtypeStruct((4, R, C), parts.dtype),
        compiler_params=_params(("parallel", "parallel")),
    )(core, parts, got)


def _matmul(a, b, mode, M, N, K, tm, tn, tk, out_dtype, name, b_noff=0, a_moff=0, a_koff=0, a_blocked=False, b_blocked=False,
            out_blocked=None, dep=None, addend=None):
    nm, nn, nk = M // tm, N // tn, K // tk
    if mode == "nn":
        a_spec = pl.BlockSpec((tm, tk), lambda j, i, k: (i, k + a_koff))
        b_spec = pl.BlockSpec((tk, tn), lambda j, i, k: (k, j + b_noff))
        dn = NN
    elif mode == "nt":
        a_spec = pl.BlockSpec((tm, tk), lambda j, i, k: (i, k + a_koff))
        if b_blocked:
            b_spec = pl.BlockSpec((None, tn, tk), lambda j, i, k: (k, j, 0))
        else:
            b_spec = pl.BlockSpec((tn, tk), lambda j, i, k: (j + b_noff, k))
        dn = NT
    else:
        if a_blocked:
            a_spec = pl.BlockSpec((None, tk, tm), lambda j, i, k: (i, k, 0))
        else:
            a_spec = pl.BlockSpec((tk, tm), lambda j, i, k: (k, i + a_moff))
        if b_blocked:
            b_spec = pl.BlockSpec((None, tk, tn), lambda j, i, k: (j, k, 0))
        else:
            b_spec = pl.BlockSpec((tk, tn), lambda j, i, k: (k, j + b_noff))
        dn = TN
    if out_blocked == "col":
        out_shape = jax.ShapeDtypeStruct((2, 4, M, tn), out_dtype)
        out_spec = pl.BlockSpec((None, None, tm, tn), lambda j, i, k: (j % 2, j // 2, i, 0))
    elif out_blocked == "row":
        out_shape = jax.ShapeDtypeStruct((2, 4, tm, N), out_dtype)
        out_spec = pl.BlockSpec((None, None, tm, tn), (lambda j, i, k: (i % 2, i // 2, 0, j)))
    elif out_blocked == "third":
        out_shape = jax.ShapeDtypeStruct((3, M, N // 3), out_dtype)
        out_spec = pl.BlockSpec((None, tm, tn), lambda j, i, k: (j // (nn // 3), i, j % (nn // 3)))
    else:
        out_shape = jax.ShapeDtypeStruct((M, N), out_dtype)
        out_spec = pl.BlockSpec((tm, tn), lambda j, i, k: (i, j))

    n_extra = (addend is not None) + (dep is not None)

    def body(a_ref, b_ref, *rest):
        o_ref, scratch = rest[n_extra], rest[n_extra + 1:]
        if nk == 1:
            prod = _dot(a_ref[...], b_ref[...], dn)
            if addend is not None:
                prod = prod + rest[0][...].astype(F32)
            o_ref[...] = prod.astype(out_dtype)
        else:
            assert addend is None
            acc_ref, = scratch
            k = pl.program_id(2)

            @pl.when(k == 0)
            def _():
                acc_ref[...] = jnp.zeros_like(acc_ref)

            acc_ref[...] += _dot(a_ref[...], b_ref[...], dn)

            @pl.when(k == nk - 1)
            def _():
                o_ref[...] = acc_ref[...].astype(out_dtype)

    return pl.pallas_call(
        body, name=name, grid=(nn, nm, nk),
        in_specs=[a_spec, b_spec] + ([] if addend is None else [out_spec])
        + ([] if dep is None else [pl.BlockSpec((8, 128), lambda j, i, k: (0, 0))]),
        out_specs=out_spec, out_shape=out_shape,
        scratch_shapes=[] if nk == 1 else [pltpu.VMEM((tm, tn), F32)],
        compiler_params=_params(("parallel", "parallel", "arbitrary")),
    )(a, b, *([] if addend is None else [addend]), *([] if dep is None else [dep]))


def _rms_fwd(x, g, name):
    R, Dm = x.shape
    tr = min(R, 256)

    def body(x_ref, g_ref, h_ref):
        xv = x_ref[...]
        r = lax.rsqrt(jnp.mean(xv * xv, axis=-1, keepdims=True) + EPS)
        h_ref[...] = (xv * r * g_ref[...]).astype(BF)

    return pl.pallas_call(
        body, name=name, grid=(R // tr,),
        in_specs=[pl.BlockSpec((tr, Dm), lambda i: (i, 0)), pl.BlockSpec((1, Dm), lambda i: (0, 0))],
        out_specs=pl.BlockSpec((tr, Dm), lambda i: (i, 0)), out_shape=jax.ShapeDtypeStruct((R, Dm), BF),
        compiler_params=_params(("parallel",)),
    )(x, g)


def _rms_gain_grad(dn, x, name):
    R, Dm = x.shape

    def body(dn_ref, x_ref, o_ref):
        xv = x_ref[...]
        r = lax.rsqrt(jnp.mean(xv * xv, axis=-1, keepdims=True) + EPS)
        o_ref[...] = jnp.sum(dn_ref[...] * xv * r, axis=0, keepdims=True)

    return pl.pallas_call(
        body, name=name, out_shape=jax.ShapeDtypeStruct((1, Dm), F32),
        compiler_params=_params(),
    )(dn, x)


def _shift_down(v, k, head8, row, T):
    if k == 0:
        return v
    r = pltpu.roll(v, k, 0)
    hr = pltpu.roll(head8, k, 0)
    top = jnp.where(row[:8] < k, hr, r[:8])
    return jnp.concatenate([top, r[8:]], axis=0)


def _shift_up(v, k, tail8, row, T):
    if k == 0:
        return v
    r = pltpu.roll(v, T - k, 0)
    tr = pltpu.roll(tail8, 8 - k, 0)
    bot = jnp.where(row[:8] >= 8 - k, tr, r[T - 8:])
    return jnp.concatenate([r[:T - 8], bot], axis=0)


def _rglru_gates(u, head8, grow, row, T, cw_ref, cb_ref, wa_ref, ba_ref, wx_ref, bx_ref, lam_ref):
    us = [_shift_down(u, k, head8, row, T) for k in range(CONV_W)]
    acc = us[0] * cw_ref[0:1, :]
    for k in range(1, CONV_W):
        acc = acc + us[k] * cw_ref[k:k + 1, :]
    conv = cb_ref[...] + acc
    cbf = conv.astype(BF)
    r_ = _sigmoid(_dot(cbf, wa_ref[0], NN) + ba_ref[...])
    i_ = _sigmoid(_dot(cbf, wx_ref[0], NN) + bx_ref[...])
    sp = _softplus(-lam_ref[...])
    la = -LRU_C * r_ * sp
    a = jnp.exp(la)
    mult_raw = jnp.sqrt(-_expm1(2.0 * la))
    mult = jnp.where(grow == 0, 1.0, mult_raw)
    return us, conv, cbf, r_, i_, sp, a, mult_raw, mult


def _rglru_specs(T, nt, rev):
    tmap = (lambda n, t: (nt - 1 - t, n)) if rev else (lambda n, t: (t, n))
    hmap = ((lambda n, t: (jnp.maximum((nt - 1 - t) * (T // 8) - 1, 0), n)) if rev
            else (lambda n, t: (jnp.maximum(t * (T // 8) - 1, 0), n)))
    tile = pl.BlockSpec((T, RNN_BLOCK), tmap)
    halo = pl.BlockSpec((8, RNN_BLOCK), hmap)
    vec = pl.BlockSpec((1, RNN_BLOCK), lambda n, t: (0, n))
    cw = pl.BlockSpec((CONV_W, RNN_BLOCK), lambda n, t: (0, n))
    wblk = pl.BlockSpec((1, RNN_BLOCK, RNN_BLOCK), lambda n, t: (n, 0, 0))
    return tile, halo, vec, cw, wblk


def _rglru_fwd(xr, g, cw, cb, wa, ba, wx, bx, lam, T):
    S = xr.shape[0]
    nt = S // T

    def body(u_ref, uh_ref, g_ref, cw_ref, cb_ref, wa_ref, ba_ref, wx_ref, bx_ref, lam_ref, h_ref, y_ref, carry):
        t = pl.program_id(1)

        @pl.when(t == 0)
        def _():
            carry[...] = jnp.zeros_like(carry)

        row = lax.broadcasted_iota(jnp.int32, (T, RNN_BLOCK), 0)
        grow = row + t * T
        head8 = jnp.where(t > 0, uh_ref[...], 0.0)
        _, conv, _, _, i_, _, a, _, mult = _rglru_gates(u_ref[...], head8, grow, row, T, cw_ref, cb_ref, wa_ref, ba_ref,
                                                         wx_ref, bx_ref, lam_ref)
        b = mult * i_ * conv
        s = 1
        while s < T:
            keep = row >= s
            a_s = jnp.where(keep, pltpu.roll(a, s, 0), 1.0)
            b_s = jnp.where(keep, pltpu.roll(b, s, 0), 0.0)
            b = a * b_s + b
            a = a * a_s
            s *= 2
        h = b + a * carry[0:1, :]
        carry[...] = jnp.broadcast_to(h[T - 1:T, :], carry.shape)
        h_ref[...] = h
        gv = g_ref[...]
        y_ref[...] = (h * (gv * _sigmoid(gv))).astype(BF)

    tile, halo, vec, cwspec, wblk = _rglru_specs(T, nt, False)
    return pl.pallas_call(
        body, name="rglru_fwd", grid=(RNN_BLOCKS, nt),
        in_specs=[tile, halo, tile, cwspec, vec, wblk, vec, wblk, vec, vec],
        out_specs=[tile, tile],
        out_shape=[jax.ShapeDtypeStruct((S, D_RNN), F32), jax.ShapeDtypeStruct((S, D_RNN), BF)],
        scratch_shapes=[pltpu.VMEM((8, RNN_BLOCK), F32)],
        compiler_params=_params(("parallel", "arbitrary")),
    )(xr, xr, g, cw, cb, wa, ba, wx, bx, lam)


def _rglru_bwd(xr, g, h, dy, cw, cb, wa, ba, wx, bx, lam, T):
    S = xr.shape[0]
    nt = S // T

    def body(u_ref, uh_ref, g_ref, h_ref, hh_ref, dy_ref, cw_ref, cb_ref, wa_ref, ba_ref, wx_ref, bx_ref, lam_ref,
             du_ref, dg_ref, dwa_ref, dwx_ref, dvec_ref, c_dhh, c_a, c_dconv):
        t = pl.program_id(1)
        tt = nt - 1 - t

        @pl.when(t == 0)
        def _():
            c_dhh[...] = jnp.zeros_like(c_dhh)
            c_a[...] = jnp.zeros_like(c_a)
            c_dconv[...] = jnp.zeros_like(c_dconv)
            dwa_ref[...] = jnp.zeros_like(dwa_ref)
            dwx_ref[...] = jnp.zeros_like(dwx_ref)
            dvec_ref[...] = jnp.zeros_like(dvec_ref)

        row = lax.broadcasted_iota(jnp.int32, (T, RNN_BLOCK), 0)
        row8 = row[:8]
        grow = row + tt * T
        head8 = jnp.where(tt > 0, uh_ref[...], 0.0)
        us, conv, cbf, r_, i_, sp, a, mult_raw, mult = _rglru_gates(
            u_ref[...], head8, grow, row, T, cw_ref, cb_ref, wa_ref, ba_ref, wx_ref, bx_ref, lam_ref)
        hv = h_ref[...]
        hprev = _shift_down(hv, 1, jnp.where(tt > 0, hh_ref[...], 0.0), row, T)
        gv = g_ref[...]
        sg = _sigmoid(gv)
        dyv = dy_ref[...]
        dg_ref[...] = (dyv * hv * (sg * (1.0 + gv * (1.0 - sg)))).astype(BF)
        d = dyv * (gv * sg)
        A = _shift_up(a, 1, c_a[...], row, T)
        s = 1
        while s < T:
            keep = row < T - s
            A_s = jnp.where(keep, pltpu.roll(A, T - s, 0), 1.0)
            d_s = jnp.where(keep, pltpu.roll(d, T - s, 0), 0.0)
            d = A * d_s + d
            A = A * A_s
            s *= 2
        dhh = d + A * c_dhh[0:1, :]
        da = dhh * hprev
        dconv = dhh * mult * i_
        di = dhh * mult * conv
        dmult = dhh * i_ * conv
        dla = da * a - jnp.where(grow == 0, 0.0, dmult * (a * a) / mult_raw)
        dr = dla * (-LRU_C * sp)
        dsp = jnp.sum(dla * (-LRU_C * r_), axis=0, keepdims=True)
        dza = dr * r_ * (1.0 - r_)
        dzx = di * i_ * (1.0 - i_)
        dza_b, dzx_b = dza.astype(BF), dzx.astype(BF)
        dconv = dconv + _dot(dza_b, wa_ref[0], NT) + _dot(dzx_b, wx_ref[0], NT)
        dwa_ref[0] += _dot(cbf, dza_b, TN)
        dwx_ref[0] += _dot(cbf, dzx_b, TN)
        lam = lam_ref[...]
        rows = [jnp.sum(dconv * us[k], axis=0, keepdims=True) for k in range(CONV_W)]
        rows += [jnp.sum(dconv, axis=0, keepdims=True), jnp.sum(dza, axis=0, keepdims=True),
                 jnp.sum(dzx, axis=0, keepdims=True), dsp * (-_sigmoid(-lam))]
        upd = jnp.zeros((8, RNN_BLOCK), F32)
        for j, rv in enumerate(rows):
            upd = upd + jnp.where(row8 == j, rv, 0.0)
        dvec_ref[...] += upd
        tail8 = c_dconv[...]
        du = dconv * cw_ref[0:1, :]
        for k in range(1, CONV_W):
            du = du + _shift_up(dconv, k, tail8, row, T) * cw_ref[k:k + 1, :]
        du_ref[...] = du.astype(BF)
        c_dhh[...] = jnp.broadcast_to(dhh[0:1, :], c_dhh.shape)
        c_a[...] = jnp.broadcast_to(a[0:1, :], c_a.shape)
        c_dconv[...] = dconv[:8]

    tile, halo, vec, cwspec, wblk = _rglru_specs(T, nt, True)
    acc8 = pl.BlockSpec((8, RNN_BLOCK), lambda n, t: (0, n))
    return pl.pallas_call(
        body, name="rglru_bwd", grid=(RNN_BLOCKS, nt),
        in_specs=[tile, halo, tile, tile, halo, tile, cwspec, vec, wblk, vec, wblk, vec, vec],
        out_specs=[tile, tile, wblk, wblk, acc8],
        out_shape=[jax.ShapeDtypeStruct((S, D_RNN), BF), jax.ShapeDtypeStruct((S, D_RNN), BF),
                   jax.ShapeDtypeStruct((RNN_BLOCKS, RNN_BLOCK, RNN_BLOCK), F32),
                   jax.ShapeDtypeStruct((RNN_BLOCKS, RNN_BLOCK, RNN_BLOCK), F32),
                   jax.ShapeDtypeStruct((8, D_RNN), F32)],
        scratch_shapes=[pltpu.VMEM((8, RNN_BLOCK), F32)] * 3,
        compiler_params=_params(("parallel", "arbitrary")),
    )(xr, xr, g, h, h, dy, cw, cb, wa, ba, wx, bx, lam)


def _rel_bucket_map():
    qi = np.arange(WINDOW)[:, None]
    kj = np.arange(2 * WINDOW)[None, :]
    dist = jnp.asarray(qi + WINDOW - kj, jnp.int32)
    n = jnp.maximum(dist, 0)
    max_exact = REL_BUCKETS // 2
    ratio = jnp.log(jnp.maximum(n, 1).astype(F32) / max_exact) / math.log(REL_MAX_DIST / max_exact)
    large = jnp.minimum(max_exact + (ratio * (REL_BUCKETS - max_exact)).astype(jnp.int32), REL_BUCKETS - 1)
    bucket = jnp.where(n < max_exact, n, large).astype(jnp.int32)
    j = np.arange(WINDOW)[None, :]
    return jnp.where(jnp.asarray(j > qi), bucket[:, :WINDOW], bucket[:, WINDOW:])


def _swa_common(n, kv_ref, bucket_ref, relb_ref, bias_scr):
    @pl.when(n == 0)
    def _():
        bk = bucket_ref[...]
        for h in range(SWA_HEADS):
            acc = jnp.zeros((WINDOW, WINDOW), F32)
            for b in range(REL_BUCKETS):
                acc = acc + jnp.where(bk == b, relb_ref[b, h], 0.0)
            bias_scr[h] = acc

    prev0 = pl.multiple_of(jnp.maximum(n - 1, 0) * WINDOW, WINDOW)
    cur0 = pl.multiple_of(n * WINDOW, WINDOW)
    kk = jnp.concatenate([kv_ref[pl.ds(prev0, WINDOW), :], kv_ref[pl.ds(cur0, WINDOW), :]], axis=0).astype(F32)
    rowi = lax.broadcasted_iota(jnp.int32, (WINDOW, WINDOW), 0)
    col = lax.broadcasted_iota(jnp.int32, (WINDOW, WINDOW), 1)
    from_prev = col > rowi
    return kk, from_prev, prev0, cur0


def _fold(full, from_prev):
    return jnp.where(from_prev, full[:, :WINDOW], full[:, WINDOW:])


def _unfold(sq, from_prev):
    return jnp.concatenate([jnp.where(from_prev, sq, 0.0), jnp.where(from_prev, 0.0, sq)], axis=1)


def _half_pair(part, kvh):
    lo = lax.broadcasted_iota(jnp.int32, part.shape, 1) < SWA_HD
    if kvh == 0:
        pa = jnp.where(lo, part, 0.0)
        pb = pltpu.roll(pa, SWA_HD, 1)
    else:
        pb = jnp.where(lo, 0.0, part)
        pa = pltpu.roll(pb, SWA_HD, 1)
    return pa.astype(BF), pb.astype(BF)


ALL_HEADS = SWA_HEADS * WINDOW


def _sink_column(sinks):
    return jnp.repeat(sinks.reshape(SWA_HEADS), WINDOW).reshape(ALL_HEADS, 1)


def _swa_operands(kk):
    return [(_half_pair(kk[:, :128], kvh), _half_pair(kk[:, 128:], kvh)) for kvh in range(SWA_KV_HEADS)]


def _swa_probs(n, q_ref, ops, bias_scr, sinkc_ref, from_prev):
    lgs = []
    for kvh in range(SWA_KV_HEADS):
        (ka, kb), _ = ops[kvh]
        for p in range(4):
            q2 = q_ref[:, kvh * 512 + p * 128:kvh * 512 + p * 128 + 128]
            lgs += [_fold(_dot(q2, ka, NT), from_prev), _fold(_dot(q2, kb, NT), from_prev)]
    lg = jnp.concatenate(lgs, axis=0) * (SWA_HD ** -0.5) + bias_scr[...].reshape(ALL_HEADS, WINDOW)
    rowi = jnp.bitwise_and(lax.broadcasted_iota(jnp.int32, (ALL_HEADS, WINDOW), 0), WINDOW - 1)
    col = lax.broadcasted_iota(jnp.int32, (ALL_HEADS, WINDOW), 1)
    no_prev = jnp.where(n > 0, 0, 4 * WINDOW)
    lg = jnp.where(jnp.logical_or(col <= rowi, col > rowi + no_prev), lg, NEG_INF)
    sink = sinkc_ref[...]
    m = jnp.maximum(jnp.max(lg, axis=-1, keepdims=True), sink)
    e = jnp.exp(lg - m)
    es = jnp.exp(sink - m)
    den = jnp.sum(e, axis=-1, keepdims=True) + es
    return e / den, es / den


def _swa_fwd(q, kv, g, bucket, rel_bias, sink_col):
    S = q.shape[0]
    nb = S // WINDOW

    def body(q_ref, kv_ref, g_ref, bucket_ref, relb_ref, sinkc_ref, o_ref, y_ref, bias_scr):
        n = pl.program_id(0)
        kk, from_prev, _, _ = _swa_common(n, kv_ref, bucket_ref, relb_ref, bias_scr)
        ops = _swa_operands(kk)
        pr, _ = _swa_probs(n, q_ref, ops, bias_scr, sinkc_ref, from_prev)
        for kvh in range(SWA_KV_HEADS):
            _, (va, vb) = ops[kvh]
            for p in range(4):
                c0 = kvh * 512 + p * 128
                r0 = (kvh * 8 + 2 * p) * WINDOW
                o2 = (_dot(_unfold(pr[r0:r0 + WINDOW], from_prev).astype(BF), va, NN)
                      + _dot(_unfold(pr[r0 + WINDOW:r0 + 2 * WINDOW], from_prev).astype(BF), vb, NN))
                o_ref[:, c0:c0 + 128] = o2
                gv = g_ref[:, c0:c0 + 128]
                y_ref[:, c0:c0 + 128] = (o2 * (gv * _sigmoid(gv))).astype(BF)

    blk = pl.BlockSpec((WINDOW, 1024), lambda n: (n, 0))
    smem = pl.BlockSpec(memory_space=pltpu.SMEM)
    sinkc = pl.BlockSpec((ALL_HEADS, 1), lambda n: (0, 0))
    return pl.pallas_call(
        body, name="swa_fwd", grid=(nb,),
        in_specs=[blk, pl.BlockSpec((S, 256), lambda n: (0, 0)), blk, pl.BlockSpec((WINDOW, WINDOW), lambda n: (0, 0)), smem, sinkc],
        out_specs=[blk, blk],
        out_shape=[jax.ShapeDtypeStruct((S, 1024), F32), jax.ShapeDtypeStruct((S, 1024), BF)],
        scratch_shapes=[pltpu.VMEM((SWA_HEADS, WINDOW, WINDOW), F32)],
        compiler_params=_params(("arbitrary",)),
    )(q, kv, g, bucket, rel_bias, sink_col)


def _swa_bwd(q, kv, g, o, dy, bucket, rel_bias, sink_col):
    S = q.shape[0]
    nb = S // WINDOW

    def body(q_ref, kv_ref, g_ref, o_ref, dy_ref, bucket_ref, relb_ref, sinkc_ref,
             dq_ref, dg_ref, dkv_ref, dsink_ref, drel_ref, bias_scr, dbias_scr, dsink_scr):
        n = pl.program_id(0)

        @pl.when(n == 0)
        def _():
            dbias_scr[...] = jnp.zeros_like(dbias_scr)
            dsink_scr[...] = jnp.zeros_like(dsink_scr)
            dkv_ref[...] = jnp.zeros_like(dkv_ref)

        kk, from_prev, prev0, cur0 = _swa_common(n, kv_ref, bucket_ref, relb_ref, bias_scr)
        ops = _swa_operands(kk)
        pr, ps = _swa_probs(n, q_ref, ops, bias_scr, sinkc_ref, from_prev)
        do2s, dps = [], []
        for kvh in range(SWA_KV_HEADS):
            _, (va, vb) = ops[kvh]
            for p in range(4):
                c0 = kvh * 512 + p * 128
                gv = g_ref[:, c0:c0 + 128]
                sg = _sigmoid(gv)
                dyv = dy_ref[:, c0:c0 + 128]
                dg_ref[:, c0:c0 + 128] = (dyv * o_ref[:, c0:c0 + 128] * (sg * (1.0 + gv * (1.0 - sg)))).astype(BF)
                do2 = (dyv * (gv * sg)).astype(BF)
                do2s.append(do2)
                dps += [_fold(_dot(do2, va, NT), from_prev), _fold(_dot(do2, vb, NT), from_prev)]
        dp = jnp.concatenate(dps, axis=0)
        delta = jnp.sum(pr * dp, axis=-1, keepdims=True)
        ds = pr * (dp - delta)
        dbias_scr[...] += ds.reshape(SWA_HEADS, WINDOW, WINDOW)
        dsink_scr[...] += ps * delta
        dsc = ds * (SWA_HD ** -0.5)
        lo256 = lax.broadcasted_iota(jnp.int32, (2 * WINDOW, 128), 1) < SWA_HD
        dks, dvs = [], []
        for kvh in range(SWA_KV_HEADS):
            (ka, kb), _ = ops[kvh]
            dka = jnp.zeros((2 * WINDOW, 128), F32)
            dkb, dva, dvb = dka, dka, dka
            for p in range(4):
                c0 = kvh * 512 + p * 128
                r0 = (kvh * 8 + 2 * p) * WINDOW
                q2 = q_ref[:, c0:c0 + 128]
                do2 = do2s[kvh * 4 + p]
                ds0 = _unfold(dsc[r0:r0 + WINDOW], from_prev).astype(BF)
                ds1 = _unfold(dsc[r0 + WINDOW:r0 + 2 * WINDOW], from_prev).astype(BF)
                dq_ref[:, c0:c0 + 128] = (_dot(ds0, ka, NN) + _dot(ds1, kb, NN)).astype(BF)
                dka = dka + _dot(ds0, q2, TN)
                dkb = dkb + _dot(ds1, q2, TN)
                dva = dva + _dot(_unfold(pr[r0:r0 + WINDOW], from_prev).astype(BF), do2, TN)
                dvb = dvb + _dot(_unfold(pr[r0 + WINDOW:r0 + 2 * WINDOW], from_prev).astype(BF), do2, TN)
            dks.append(jnp.where(lo256, dka, 0.0) + pltpu.roll(jnp.where(lo256, 0.0, dkb), SWA_HD, 1))
            dvs.append(jnp.where(lo256, dva, 0.0) + pltpu.roll(jnp.where(lo256, 0.0, dvb), SWA_HD, 1))
        dk = dks[0] + pltpu.roll(dks[1], SWA_HD, 1)
        dv = dvs[0] + pltpu.roll(dvs[1], SWA_HD, 1)
        dkv_ref[pl.ds(prev0, WINDOW), 0:128] += dk[:WINDOW]
        dkv_ref[pl.ds(prev0, WINDOW), 128:256] += dv[:WINDOW]
        dkv_ref[pl.ds(cur0, WINDOW), 0:128] += dk[WINDOW:]
        dkv_ref[pl.ds(cur0, WINDOW), 128:256] += dv[WINDOW:]

        @pl.when(n == nb - 1)
        def _():
            dsink_ref[...] = -jnp.sum(dsink_scr[...].reshape(SWA_HEADS, WINDOW, 1), axis=1)
            bk = bucket_ref[...]
            sums = []
            for b in range(REL_BUCKETS):
                sums.append(jnp.sum(jnp.where((bk == b)[None], dbias_scr[...], 0.0), axis=1))
            drel_ref[...] = jnp.sum(jnp.concatenate(sums, axis=0), axis=1, keepdims=True)

    blk = pl.BlockSpec((WINDOW, 1024), lambda n: (n, 0))
    smem = pl.BlockSpec(memory_space=pltpu.SMEM)
    whole = lambda shape: pl.BlockSpec(shape, lambda n: (0, 0))
    return pl.pallas_call(
        body, name="swa_bwd", grid=(nb,),
        in_specs=[blk, whole((S, 256)), blk, blk, blk, whole((WINDOW, WINDOW)), smem, whole((ALL_HEADS, 1))],
        out_specs=[blk, blk, whole((S, 256)), whole((SWA_HEADS, 1)), whole((REL_BUCKETS * SWA_HEADS, 1))],
        out_shape=[jax.ShapeDtypeStruct((S, 1024), BF), jax.ShapeDtypeStruct((S, 1024), BF),
                   jax.ShapeDtypeStruct((S, 256), F32), jax.ShapeDtypeStruct((SWA_HEADS, 1), F32),
                   jax.ShapeDtypeStruct((REL_BUCKETS * SWA_HEADS, 1), F32)],
        scratch_shapes=[pltpu.VMEM((SWA_HEADS, WINDOW, WINDOW), F32), pltpu.VMEM((SWA_HEADS, WINDOW, WINDOW), F32),
                        pltpu.VMEM((ALL_HEADS, 1), F32)],
        compiler_params=_params(("arbitrary",)),
    )(q, kv, g, o, dy, bucket, rel_bias, sink_col)


def _mem_probs(qh, mk):
    lg = _dot(qh, mk, NT) * (MEM_HD ** -0.5)
    e = jnp.exp(lg - jnp.max(lg, axis=-1, keepdims=True))
    return e / jnp.sum(e, axis=-1, keepdims=True)


def _mem_fwd(q, mkv, g):
    S = q.shape[0]
    M = mkv.shape[0]
    tq = 256

    def body(q_ref, mkv_ref, g_ref, o_ref, y_ref):
        for h in range(MEM_HEADS):
            c0 = h * MEM_HD
            pr = _mem_probs(q_ref[:, c0:c0 + MEM_HD], mkv_ref[:, c0:c0 + MEM_HD])
            o = _dot(pr.astype(BF), mkv_ref[:, D_MEM + c0:D_MEM + c0 + MEM_HD], NN)
            o_ref[:, c0:c0 + MEM_HD] = o
            gv = g_ref[:, c0:c0 + MEM_HD]
            y_ref[:, c0:c0 + MEM_HD] = (o * (gv * _sigmoid(gv))).astype(BF)

    blk = pl.BlockSpec((tq, D_MEM), lambda i: (i, 0))
    return pl.pallas_call(
        body, name="mem_fwd", grid=(S // tq,),
        in_specs=[blk, pl.BlockSpec((M, 2 * D_MEM), lambda i: (0, 0)), blk], out_specs=[blk, blk],
        out_shape=[jax.ShapeDtypeStruct((S, D_MEM), F32), jax.ShapeDtypeStruct((S, D_MEM), BF)],
        compiler_params=_params(("parallel",)),
    )(q, mkv, g)


def _mem_bwd(q, mkv, g, o, dy):
    S = q.shape[0]
    M = mkv.shape[0]
    tq = 256

    def body(q_ref, mkv_ref, g_ref, o_ref, dy_ref, dq_ref, dg_ref, dmkv_ref):
        @pl.when(pl.program_id(0) == 0)
        def _():
            dmkv_ref[...] = jnp.zeros_like(dmkv_ref)

        for h in range(MEM_HEADS):
            c0 = h * MEM_HD
            qh = q_ref[:, c0:c0 + MEM_HD]
            mk = mkv_ref[:, c0:c0 + MEM_HD]
            mv = mkv_ref[:, D_MEM + c0:D_MEM + c0 + MEM_HD]
            gv = g_ref[:, c0:c0 + MEM_HD]
            sg = _sigmoid(gv)
            dyv = dy_ref[:, c0:c0 + MEM_HD]
            dg_ref[:, c0:c0 + MEM_HD] = (dyv * o_ref[:, c0:c0 + MEM_HD] * (sg * (1.0 + gv * (1.0 - sg)))).astype(BF)
            do = (dyv * (gv * sg)).astype(BF)
            pr = _mem_probs(qh, mk)
            dp = _dot(do, mv, NT)
            ds = pr * (dp - jnp.sum(pr * dp, axis=-1, keepdims=True))
            dsb = (ds * (MEM_HD ** -0.5)).astype(BF)
            dq_ref[:, c0:c0 + MEM_HD] = _dot(dsb, mk, NN).astype(BF)
            dmkv_ref[:, c0:c0 + MEM_HD] += _dot(dsb, qh, TN)
            dmkv_ref[:, D_MEM + c0:D_MEM + c0 + MEM_HD] += _dot(pr.astype(BF), do, TN)

    blk = pl.BlockSpec((tq, D_MEM), lambda i: (i, 0))
    whole = pl.BlockSpec((M, 2 * D_MEM), lambda i: (0, 0))
    return pl.pallas_call(
        body, name="mem_bwd", grid=(S // tq,),
        in_specs=[blk, whole, blk, blk, blk], out_specs=[blk, blk, whole],
        out_shape=[jax.ShapeDtypeStruct((S, D_MEM), BF), jax.ShapeDtypeStruct((S, D_MEM), BF),
                   jax.ShapeDtypeStruct((M, 2 * D_MEM), F32)],
        compiler_params=_params(("arbitrary",)),
    )(q, mkv, g, o, dy)


MERGE_TN = 512


def _merge_specs(tm):
    ytile = pl.BlockSpec((tm, 1024), lambda i, j: (i, 0))
    wblk = pl.BlockSpec((MERGE_TN, 1024), lambda i, j: (j, 0))
    gls = [pl.BlockSpec((None, tm, MERGE_TN), (lambda i, j, br=br: (br, i, j))) for br in range(3)]
    otile = pl.BlockSpec((tm, MERGE_TN), lambda i, j: (i, j))
    return ytile, wblk, gls, otile


def _merge_fwd(ys, ws, gl, tm):
    S = gl.shape[1]

    def body(y0, y1, y2, w0, w1, w2, g0, g1, g2, o_ref):
        acc = None
        for y_ref, w_ref, g_ref in ((y0, w0, g0), (y1, w1, g1), (y2, w2, g2)):
            term = _sigmoid(g_ref[...]) * _dot(y_ref[...], w_ref[...], NT)
            acc = term if acc is None else acc + term
        o_ref[...] = acc.astype(BF)

    ytile, wblk, gls, otile = _merge_specs(tm)
    return pl.pallas_call(
        body, name="merge_fwd", grid=(S // tm, D_MODEL // MERGE_TN),
        in_specs=[ytile] * 3 + [wblk] * 3 + gls, out_specs=otile,
        out_shape=jax.ShapeDtypeStruct((S, D_MODEL), BF),
        compiler_params=_params(("parallel", "arbitrary")),
    )(*ys, *ws, gl, gl, gl)


def _merge_bwd(dout, w_out, ys, ws, gl, tm):
    S = gl.shape[1]

    def body(do_ref, wo_ref, y0, y1, y2, w0, w1, w2, g0, g1, g2, dg0, dg1, dg2, dp0, dp1, dp2):
        dm = _dot(do_ref[...], wo_ref[...], NT)
        for y_ref, w_ref, g_ref, dg_ref, dp_ref in ((y0, w0, g0, dg0, dp0), (y1, w1, g1, dg1, dp1), (y2, w2, g2, dg2, dp2)):
            gate = _sigmoid(g_ref[...])
            pv = _dot(y_ref[...], w_ref[...], NT)
            dg_ref[...] = (dm * pv * gate * (1.0 - gate)).astype(BF)
            dp_ref[...] = (dm * gate).astype(BF)

    ytile, wblk, gls, otile = _merge_specs(tm)
    out = jax.ShapeDtypeStruct((S, D_MODEL), BF)
    return pl.pallas_call(
        body, name="merge_bwd", grid=(S // tm, D_MODEL // MERGE_TN),
        in_specs=[pl.BlockSpec((tm, D_MODEL), lambda i, j: (i, 0)), pl.BlockSpec((MERGE_TN, D_MODEL), lambda i, j: (j, 0))]
        + [ytile] * 3 + [wblk] * 3 + gls,
        out_specs=[otile] * 6, out_shape=[out] * 6,
        compiler_params=_params(("parallel", "arbitrary")),
    )(dout, w_out, *ys, *ws, gl, gl, gl)


def _out_loss(merged, w_out, x, target, post_g, tm):
    S = x.shape[0]

    def body(m_ref, w_ref, x_ref, t_ref, g_ref, dout_ref, dy_ref, loss_ref, dpost_ref):
        @pl.when(pl.program_id(0) == 0)
        def _():
            loss_ref[...] = jnp.zeros_like(loss_ref)
            dpost_ref[...] = jnp.zeros_like(dpost_ref)

        out = _dot(m_ref[...], w_ref[...], NN)
        r = lax.rsqrt(jnp.mean(out * out, axis=-1, keepdims=True) + EPS)
        nrm = out * r
        gv = g_ref[...]
        err = (x_ref[...] + nrm * gv) - t_ref[...]
        sq = jnp.sum(jnp.sum(err * err, axis=1, keepdims=True), axis=0, keepdims=True)
        loss_ref[...] += sq * (0.5 / D_MODEL)
        dy = err * (1.0 / D_MODEL)
        dy_ref[...] = dy
        dpost_ref[...] += jnp.sum(dy * nrm, axis=0, keepdims=True)
        dn = dy * gv
        dout_ref[...] = (r * (dn - nrm * jnp.mean(dn * nrm, axis=-1, keepdims=True))).astype(BF)

    row = pl.BlockSpec((tm, D_MODEL), lambda i: (i, 0))
    return pl.pallas_call(
        body, name="out_loss", grid=(S // tm,),
        in_specs=[row, pl.BlockSpec((D_MODEL, D_MODEL), lambda i: (0, 0)), row, row, pl.BlockSpec((1, D_MODEL), lambda i: (0, 0))],
        out_specs=[row, row, pl.BlockSpec((8, 128), lambda i: (0, 0)), pl.BlockSpec((1, D_MODEL), lambda i: (0, 0))],
        out_shape=[jax.ShapeDtypeStruct((S, D_MODEL), BF), jax.ShapeDtypeStruct((S, D_MODEL), F32),
                   jax.ShapeDtypeStruct((8, 128), F32), jax.ShapeDtypeStruct((1, D_MODEL), F32)],
        compiler_params=_params(("arbitrary",)),
    )(merged, w_out, x, target, post_g)


def _dh_dx(dproj, w_in, x, dy, pre_g, tm):
    S = x.shape[0]
    nk, tk = dproj.shape[0], dproj.shape[2]

    def body(dp_ref, w_ref, x_ref, dy_ref, g_ref, dx_ref, dpre_ref, acc_ref):
        i, k = pl.program_id(0), pl.program_id(1)

        @pl.when(jnp.logical_and(i == 0, k == 0))
        def _():
            dpre_ref[...] = jnp.zeros_like(dpre_ref)

        @pl.when(k == 0)
        def _():
            acc_ref[...] = jnp.zeros_like(acc_ref)

        acc_ref[...] += _dot(dp_ref[...], w_ref[...], NN)

        @pl.when(k == nk - 1)
        def _():
            dh = acc_ref[...]
            xv = x_ref[...]
            r = lax.rsqrt(jnp.mean(xv * xv, axis=-1, keepdims=True) + EPS)
            nrm = xv * r
            dpre_ref[...] += jnp.sum(dh * nrm, axis=0, keepdims=True)
            dn = dh * g_ref[...]
            dx_ref[...] = r * (dn - nrm * jnp.mean(dn * nrm, axis=-1, keepdims=True)) + dy_ref[...]

    row = pl.BlockSpec((tm, D_MODEL), lambda i, k: (i, 0))
    vec = pl.BlockSpec((1, D_MODEL), lambda i, k: (0, 0))
    return pl.pallas_call(
        body, name="dh_dx", grid=(S // tm, nk),
        in_specs=[pl.BlockSpec((None, tm, tk), lambda i, k: (k, i, 0)), pl.BlockSpec((tk, D_MODEL), lambda i, k: (k, 0)), row, row, vec],
        out_specs=[row, vec],
        out_shape=[jax.ShapeDtypeStruct((S, D_MODEL), F32), jax.ShapeDtypeStruct((1, D_MODEL), F32)],
        scratch_shapes=[pltpu.VMEM((tm, D_MODEL), F32)],
        compiler_params=_params(("arbitrary", "arbitrary"), large=True),
    )(dproj, w_in, x, dy, pre_g)


def _sum_parts(parts, name):
    P, R, C = parts.shape
    tr = max(t for t in range(8, 513, 8) if R % t == 0)

    def body(p_ref, o_ref):
        acc = p_ref[0]
        for j in range(1, P):
            acc = acc + p_ref[j]
        o_ref[...] = acc

    return pl.pallas_call(
        body, name=name, grid=(R // tr,),
        in_specs=[pl.BlockSpec((P, tr, C), lambda i: (0, i, 0))], out_specs=pl.BlockSpec((tr, C), lambda i: (i, 0)),
        out_shape=jax.ShapeDtypeStruct((R, C), F32), compiler_params=_params(("parallel",)),
    )(parts)


def _adamw(lands, sums, chip, w, m, v, name):
    lands = list(lands) if isinstance(lands, (list, tuple)) else [lands]
    sums = list(sums) if isinstance(sums, (list, tuple)) else [sums]
    _, R, cols = lands[0].shape
    C = cols * len(lands)
    tr = max(t for t in range(16, 257, 16) if R % t == 0)
    c1 = 1.0 - ADAM_B1 ** ADAM_STEP
    c2 = 1.0 - ADAM_B2 ** ADAM_STEP

    def body(chip_ref, *refs):
        p_refs = refs[:4 * len(lands)]
        w_ref, m_ref, v_ref, g_ref, d_ref, nm_ref, nv_ref = refs[4 * len(lands):]
        gs = []
        for q in range(len(lands)):
            gq = p_refs[4 * q + 3][...].astype(F32)
            for j in range(3):
                gq = gq + p_refs[4 * q + j][...].astype(F32)
            gs.append(gq)
        g = gs[0] if len(gs) == 1 else jnp.concatenate(gs, axis=1)
        nm = ADAM_B1 * m_ref[...] + (1.0 - ADAM_B1) * g
        nv = ADAM_B2 * v_ref[...] + (1.0 - ADAM_B2) * (g * g)
        g_ref[...] = g
        nm_ref[...] = nm
        nv_ref[...] = nv
        d_ref[...] = -ADAM_LR * ((nm / c1) / (jnp.sqrt(nv / c2) + ADAM_EPS) + ADAM_WD * w_ref[...])

    tile = pl.BlockSpec((None, tr, C), lambda i, c_ref: (0, i, 0))
    specs, operands = [], []
    for q in range(len(lands)):
        for k in range(3):
            specs.append(pl.BlockSpec((None, tr, cols), (lambda i, c_ref, k=k: (k + (c_ref[0] <= k).astype(jnp.int32), i, 0))))
            operands.append(lands[q])
        specs.append(pl.BlockSpec((None, tr, cols), (lambda i, c_ref: (c_ref[0], i, 0))))
        operands.append(sums[q])
    return pl.pallas_call(
        body, name=name,
        grid_spec=pltpu.PrefetchScalarGridSpec(num_scalar_prefetch=1, grid=(R // tr,), in_specs=specs + [tile, tile, tile],
                                               out_specs=[tile] * 4),
        out_shape=[jax.ShapeDtypeStruct((1, R, C), F32)] * 4, compiler_params=_params(("parallel",)),
    )(chip, *operands, w, m, v)


def _adamw_small(gs, ws, ms, vs):
    n = len(ws)
    c1 = 1.0 - ADAM_B1 ** ADAM_STEP
    c2 = 1.0 - ADAM_B2 ** ADAM_STEP

    def flat2(a):
        return a.reshape(-1, a.shape[-1])

    def body(*refs):
        ins, outs = refs[:4 * n], refs[4 * n:]
        for a in range(n):
            g, w, m, v = (ins[k * n + a][...] for k in range(4))
            nm = ADAM_B1 * m + (1.0 - ADAM_B1) * g
            nv = ADAM_B2 * v + (1.0 - ADAM_B2) * (g * g)
            outs[a][...] = g
            outs[n + a][...] = -ADAM_LR * ((nm / c1) / (jnp.sqrt(nv / c2) + ADAM_EPS) + ADAM_WD * w)
            outs[2 * n + a][...] = nm
            outs[3 * n + a][...] = nv

    shapes = [flat2(w).shape for w in ws]
    out = pl.pallas_call(
        body, name="adamw_small", out_shape=[jax.ShapeDtypeStruct(sh, F32) for sh in shapes] * 4,
        compiler_params=_params(),
    )(*[g.reshape(sh) for g, sh in zip(gs, shapes)], *[flat2(a) for a in (*ws, *ms, *vs)])
    return [[out[k * n + a].reshape(ws[a].shape) for a in range(n)] for k in range(4)]


def _project(h, w_t):
    S = h.shape[0]
    n_tiles = D_IN // SEG_TILE
    ranges = [(c0 // SEG_TILE, (c0 + width) // SEG_TILE) for _, c0, width, _ in SEGMENTS]

    def body(h_ref, w_ref, *outs):
        j = pl.program_id(0)
        prod = _dot(h_ref[...], w_ref[...], NT)
        for (j0, j1), (_, _, _, dt), o_ref in zip(ranges, SEGMENTS, outs):
            @pl.when(jnp.logical_and(j >= j0, j < j1))
            def _(o_ref=o_ref, dt=dt):
                o_ref[...] = prod.astype(dt)

    out_shapes, out_specs = [], []
    for (j0, j1), (name, _, width, dt) in zip(ranges, SEGMENTS):
        if name == "gl":
            per = (j1 - j0) // 3
            out_shapes.append(jax.ShapeDtypeStruct((3, S, width // 3), dt))
            out_specs.append(pl.BlockSpec((None, S, SEG_TILE), (lambda j, j0=j0, j1=j1, per=per: (
                jnp.clip(j - j0, 0, j1 - j0 - 1) // per, 0, jnp.clip(j - j0, 0, j1 - j0 - 1) % per))))
        else:
            out_shapes.append(jax.ShapeDtypeStruct((S, width), dt))
            out_specs.append(pl.BlockSpec((S, SEG_TILE), (lambda j, j0=j0, j1=j1: (0, jnp.clip(j - j0, 0, j1 - j0 - 1)))))
    outs = pl.pallas_call(
        body, name="proj", grid=(n_tiles,),
        in_specs=[pl.BlockSpec((S, D_MODEL), lambda j: (0, 0)), pl.BlockSpec((SEG_TILE, D_MODEL), lambda j: (j, 0))],
        out_specs=out_specs, out_shape=out_shapes,
        compiler_params=_params(("arbitrary",), large=True),
    )(h, w_t)
    return {name: o for (name, _, _, _), o in zip(SEGMENTS, outs)}


def _forward_a(x, mem, pre_g, mem_g, w_in, conv_w, conv_b, w_a, b_a, w_x, b_x, lam, sinks, rel_bias):
    S = x.shape[0]
    st = dict(T=min(512, S // 2), tm=min(512, S), bucket=_rel_bucket_map())
    st["h"] = _rms_fwd(x, pre_g, "pre_norm")
    st["memn"] = _rms_fwd(mem, mem_g, "mem_norm")
    seg = st["seg"] = _project(st["h"], w_in)
    st["h_rg"], st["y_rg"] = _rglru_fwd(seg["xr"], seg["g_rg"], conv_w, conv_b, w_a, b_a, w_x, b_x, lam, st["T"])
    st["o_swa"], st["y_swa"] = _swa_fwd(seg["q_s"], seg["kv"], seg["g_swa"], st["bucket"], rel_bias, _sink_column(sinks))
    return st


def _forward_b(st, x, target, post_g, w_memkv, wbr, w_out):
    S = x.shape[0]
    M = st["memn"].shape[0]
    seg = st["seg"]
    st["mkv"] = _matmul(st["memn"], w_memkv, "nn", M, 2 * D_MEM, D_MODEL, M, 512, D_MODEL, BF, "mem_kv")
    st["o_mem"], st["y_mem"] = _mem_fwd(seg["q_m"], st["mkv"], seg["g_mem"])
    st["ys"] = (st["y_rg"], st["y_swa"], st["y_mem"])
    st["merged"] = _merge_fwd(st["ys"], wbr, seg["gl"], st["tm"])
    st["dout"], st["dy"], st["loss"], st["dpost"] = _out_loss(st["merged"], w_out, x, target, post_g, min(256, S))
    return st


def _backward_a1(st, wbr, w_out):
    S = st["h"].shape[0]
    seg, ys, tm = st["seg"], st["ys"], st["tm"]
    st["dw_out"] = _matmul(st["merged"], st["dout"], "tn", D_MODEL, D_MODEL, S, 256, D_MODEL, S, BF, "dw_out", out_blocked="row")
    dgl0, dgl1, dgl2, dp0, dp1, dp2 = _merge_bwd(st["dout"], w_out, ys, wbr, seg["gl"], tm)
    st["dgl"] = (dgl0, dgl1, dgl2)
    dys, dwbr = [], []
    for i, dp in enumerate((dp0, dp1, dp2)):
        dys.append(_matmul(dp, wbr[i], "nn", S, 1024, D_MODEL, tm, 1024, D_MODEL, F32, "dy_br%d" % i))
        dwbr.append(_matmul(ys[i], dp, "tn", 1024, D_MODEL, S, 1024, 256, S, BF, "dw_br%d" % i, out_blocked="col"))
    st["dys"], st["dwbr"] = dys, dwbr
    return st


def _backward_a2(st, mem, w_memkv, conv_w, conv_b, w_a, b_a, w_x, b_x, lam):
    M = mem.shape[0]
    seg, dys = st["seg"], st["dys"]
    st["dq_m"], st["dg_mem"], dmkv = _mem_bwd(seg["q_m"], st["mkv"], seg["g_mem"], st["o_mem"], dys[2])
    dmkv_b = dmkv.astype(BF)
    st["dw_memkv"] = _matmul(st["memn"], dmkv_b, "tn", D_MODEL, 2 * D_MEM, M, 256, 2 * D_MEM, M, BF, "dw_memkv", out_blocked="row")
    dmemn = _matmul(dmkv_b, w_memkv, "nt", M, D_MODEL, 2 * D_MEM, M, 512, 2 * D_MEM, F32, "dmemn")
    st["dmem_g"] = _rms_gain_grad(dmemn, mem, "dmem_gain")
    st["dxr"], st["dg_rg"], st["dw_a"], st["dw_x"], st["dvec"] = _rglru_bwd(
        seg["xr"], seg["g_rg"], st["h_rg"], dys[0], conv_w, conv_b, w_a, b_a, w_x, b_x, lam, st["T"])
    return st


def _backward_b(st, rel_bias, sinks):
    seg = st["seg"]
    dq_s, dg_swa, dkv, dsinks, drel = _swa_bwd(seg["q_s"], seg["kv"], seg["g_swa"], st["o_swa"], st["dys"][1],
                                               st["bucket"], rel_bias, _sink_column(sinks))
    st["dsinks"], st["drel"] = dsinks.reshape(1, SWA_HEADS), drel.reshape(REL_BUCKETS, SWA_HEADS)
    dproj = jnp.concatenate([st["dxr"], st["dg_rg"], dq_s, dkv.astype(BF), dg_swa, st["dq_m"], st["dg_mem"], *st["dgl"]], axis=1)
    st["dproj"] = jnp.transpose(dproj.reshape(dproj.shape[0], N_DEV, D_IN // N_DEV), (1, 0, 2))
    return st


def _dw_in_half(st, half, dep=None):
    S = st["h"].shape[0]
    return _matmul(st["dproj"], st["h"], "tn", D_IN, D_MODEL // 2, S, D_IN // N_DEV, 512, S, BF, "dw_in%d" % half, b_noff=2 * half,
                   a_blocked=True, out_blocked="row", dep=dep)


def _owner_blocks(a):
    return jnp.swapaxes(a.reshape((4, 2) + a.shape[1:]), 0, 1)


def _local_step(x, mem, target, pre_g, post_g, mem_g, w_in, conv_w, conv_b, w_a, b_a, w_x, b_x, lam, sinks, rel_bias,
                w_memkv, wbr, w_out):
    st = _forward_a(x, mem, pre_g, mem_g, w_in, conv_w, conv_b, w_a, b_a, w_x, b_x, lam, sinks, rel_bias)
    st = _forward_b(st, x, target, post_g, w_memkv, wbr, w_out)
    st = _backward_a1(st, wbr, w_out)
    st = _backward_a2(st, mem, w_memkv, conv_w, conv_b, w_a, b_a, w_x, b_x, lam)
    st = _backward_b(st, rel_bias, sinks)
    st["dw_in"] = [_dw_in_half(st, 0), _dw_in_half(st, 1)]
    st["grad_x"], st["dpre"] = _dh_dx(st["dproj"], w_in, x, st["dy"], pre_g, st["tm"])
    return st


def _pad_rows(a, rows):
    a = a.reshape(-1, 128) if a.shape[-1] % 128 == 0 else jnp.pad(a, ((0, 0), (0, 128 - a.shape[-1])))
    return jnp.pad(a, ((0, rows - a.shape[0]), (0, 0))) if a.shape[0] < rows else a


def kernel(x, mem, pre_norm_g, post_norm_g, mem_norm_g, w_in, conv_w, conv_b, w_rg_a, b_rg_a, w_rg_x, b_rg_x, lru_lambda, swa_sinks, rel_bias, w_mem_kv, w_br_rg, w_br_swa, w_br_mem, w_out, loss_target, m_pre_norm_g, m_post_norm_g, m_mem_norm_g, m_w_in, m_conv_w, m_conv_b, m_w_rg_a, m_b_rg_a, m_w_rg_x, m_b_rg_x, m_lru_lambda, m_swa_sinks, m_rel_bias, m_w_mem_kv, m_w_br_rg, m_w_br_swa, m_w_br_mem, m_w_out, v_pre_norm_g, v_post_norm_g, v_mem_norm_g, v_w_in, v_conv_w, v_conv_b, v_w_rg_a, v_b_rg_a, v_w_rg_x, v_b_rg_x, v_lru_lambda, v_swa_sinks, v_rel_bias, v_w_mem_kv, v_w_br_rg, v_w_br_swa, v_w_br_mem, v_w_out):
    cx, cy, cc = lax.axis_index("x"), lax.axis_index("y"), lax.axis_index("c")
    me = 4 * cx + 2 * cy + cc
    chip = 2 * cx + cy
    core = jnp.reshape(cc, (1,)).astype(jnp.int32)
    x0, mem0 = x[0], mem[0]
    w_a_b, w_x_b = w_rg_a[0].astype(BF), w_rg_x[0].astype(BF)

    def landing(own, slot, slots):
        return lax.dynamic_update_slice(lax.empty((slots,) + own.shape, own.dtype), own[None], (slot,) + (0,) * own.ndim)


    def swap_start(parts, tag):
        return _exchange_start(parts, [lax.empty(p.shape[1:], p.dtype) for p in parts], _plan_swap(len(parts)), "swap_%s_start" % tag)

    def scatter_start(swap, after, tag, prefill=()):
        s_send, s_recv, parts, got, _ = swap
        got = _exchange_wait(s_send, s_recv, parts, got, _plan_swap(len(parts)), after, "swap_%s_wait" % tag)
        sums = [_pair_sum(p, g, core, "scatter_%s_sum%d" % (tag, i)) for i, (p, g) in enumerate(zip(parts, got))]
        lands = [landing(lax.dynamic_index_in_dim(s, chip, 0, keepdims=False), chip, 4) if i in prefill
                 else lax.empty(s.shape, s.dtype) for i, s in enumerate(sums)]
        return _exchange_start(sums, lands, _plan_scatter(len(sums)), "scatter_%s_start" % tag)

    def zero_after(a):
        return jnp.minimum(jnp.abs(a.reshape(-1)[0].astype(F32)), 0.0)

    g_in, g_cw = _all_gather_relayed([jnp.transpose(w_in[0]).astype(BF), conv_w[0]], [True, False], "gather_w_in")
    w_in_f = g_in.reshape(D_IN, D_MODEL)
    conv_w_f = jnp.transpose(g_cw, (1, 0, 2)).reshape(CONV_W, D_RNN)

    after_first = zero_after(g_cw).astype(BF)
    rest = [w.astype(BF) + after_first for w in (w_mem_kv[0], jnp.transpose(w_br_rg[0]), jnp.transpose(w_br_swa[0]),
                                                 jnp.transpose(w_br_mem[0]), w_out[0])]
    kinds = ["lead"] * len(rest)
    plan_g = _plan_gather(kinds)
    zones = _place_own([lax.empty((N_DEV,) + w.shape, w.dtype) for w in rest], rest, jnp.reshape(me, (1,)).astype(jnp.int32), "gather_rest_own")
    g_send, g_recv, g_src, g_land, g_token = _exchange_start(rest, zones, plan_g, "gather_rest_start")
    st = _forward_a(x0, mem0, pre_norm_g + g_token[0:1, 0:1], mem_norm_g, w_in_f, conv_w_f, conv_b, w_a_b, b_rg_a, w_x_b, b_rg_x,
                    lru_lambda, swa_sinks, rel_bias)
    g_land = _exchange_wait(g_send, g_recv, g_src, g_land, plan_g, st["y_swa"], "gather_rest_wait")
    g_land = _forward_to_sibling(g_land, kinds, "gather_rest_forward")
    w_memkv_f = g_land[0].reshape(D_MODEL, 2 * D_MEM)
    wbr = tuple(g_land[i].reshape(D_MODEL, D_RNN) for i in (1, 2, 3))
    w_out_f = g_land[4].reshape(D_MODEL, D_MODEL)

    st = _forward_b(st, x0, loss_target[0], post_norm_g, w_memkv_f, wbr, w_out_f)
    st = _backward_a1(st, wbr, w_out_f)
    parts_a = [st["dw_out"], st["dwbr"][0], st["dwbr"][1], st["dwbr"][2]]
    plan_a = _plan_scatter(len(parts_a))
    swap_a = swap_start(parts_a, "a")
    st = _backward_a2(st, mem0, w_memkv_f, conv_w_f, conv_b + swap_a[4][0:1, 0:1], w_a_b, b_rg_a, w_x_b, b_rg_x, lru_lambda)
    a_send, a_recv, a_src, a_land, a_token = scatter_start(swap_a, st["dxr"], "a")
    parts_c = [st["dw_memkv"], _owner_blocks(st["dw_a"]), _owner_blocks(st["dw_x"])]
    plan_c = _plan_scatter(len(parts_c))
    swap_c = swap_start(parts_c, "c")

    st = _backward_b(st, rel_bias, swa_sinks + swap_c[4][0:1, 0:1] + a_token[0:1, 0:1])
    c_send, c_recv, c_src, c_land, c_token = scatter_start(swap_c, st["dproj"], "c", prefill=(1, 2))
    plan_b = _plan_scatter(1)

    def dw_in_parts(half, dep):
        dwh = _dw_in_half(st, half, dep)
        return dwh, [dwh]

    dw0, parts_b0 = dw_in_parts(0, c_token)
    swap_b0 = swap_start(parts_b0, "b0")
    a_land = _exchange_wait(a_send, a_recv, a_src, a_land, plan_a, swap_b0[4], "scatter_a_wait")
    big = [None] * 6

    chip1 = jnp.reshape(chip, (1,)).astype(jnp.int32)

    def adamw_big(j, land, own, wt, mt, vt):
        big[j] = _adamw(land, own, chip1, wt, mt, vt, "adamw_big%d" % j)

    adamw_big(5, a_land[0], a_src[0], w_out, m_w_out, v_w_out)
    adamw_big(2, a_land[1], a_src[1], w_br_rg, m_w_br_rg, v_w_br_rg)
    adamw_big(3, a_land[2], a_src[2], w_br_swa, m_w_br_swa, v_w_br_swa)
    halves = [scatter_start(swap_b0, big[3][1], "b0")]
    dw1, parts_b1 = dw_in_parts(1, halves[0][4])
    swap_b1 = swap_start(parts_b1, "b1")
    c_land = _exchange_wait(c_send, c_recv, c_src, c_land, plan_c, swap_b1[4], "scatter_c_wait")
    g_wa_blk = _sum_parts(c_land[1], "sum_w_rg_a")
    g_wx_blk = _sum_parts(c_land[2], "sum_w_rg_x")
    adamw_big(4, a_land[3], a_src[3], w_br_mem, m_w_br_mem, v_w_br_mem)
    adamw_big(1, c_land[0], c_src[0], w_mem_kv, m_w_mem_kv, v_w_mem_kv)
    halves.append(scatter_start(swap_b1, big[1][1], "b1"))
    grad_x, dpre = _dh_dx(st["dproj"], w_in_f, x0, st["dy"], pre_norm_g + halves[1][4][0:1, 0:1], st["tm"])
    after, b_lands, b_sums = grad_x, [], []
    for half, (b_send, b_recv, b_src, b_land, _) in enumerate(halves):
        b_lands.append(_exchange_wait(b_send, b_recv, b_src, b_land, plan_b, after, "scatter_b%d_wait" % half)[0])
        b_sums.append(b_src[0])
        after = b_lands[-1]
    swap_last = lambda a: jnp.transpose(a, (0, 2, 1))
    big[0] = [swap_last(a) for a in _adamw(b_lands, b_sums, chip1, swap_last(w_in), swap_last(m_w_in), swap_last(v_w_in), "adamw_big0")]
    links_free = jnp.minimum(jnp.abs(big[0][0][0, 0, 0]), 0.0)

    pack = jnp.concatenate([dpre.reshape(16, 128), st["dpost"].reshape(16, 128), st["dmem_g"].reshape(16, 128),
                            st["dvec"].reshape(64, 128), _pad_rows(st["dsinks"], 8), _pad_rows(st["drel"], 32), g_wa_blk, g_wx_blk], axis=0) + links_free
    gathered = _all_gather([pack], "gather_small")[0]
    gs = _sum_parts(gathered, "sum_small")
    g_pre, g_post, g_memg = gs[0:16].reshape(1, D_MODEL), gs[16:32].reshape(1, D_MODEL), gs[32:48].reshape(1, D_MODEL)
    gvec = gs[48:112].reshape(8, D_RNN)
    g_conv_w = lax.dynamic_slice(gvec[0:CONV_W], (0, me * RNN_BLOCK), (CONV_W, RNN_BLOCK))
    g_conv_b, g_b_a, g_b_x, g_lam = gvec[4:5], gvec[5:6], gvec[6:7], gvec[7:8]
    g_sinks = gs[112:113, :SWA_HEADS]
    g_rel = gs[120:152, :SWA_HEADS]
    g_w_a = gathered[:, 152:280]
    g_w_x = gathered[:, 280:408]

    g_small = (g_pre, g_post, g_memg, g_conv_b, g_b_a, g_b_x, g_lam, g_w_a, g_w_x, g_sinks, g_rel, g_conv_w)
    w_small = (pre_norm_g, post_norm_g, mem_norm_g, conv_b, b_rg_a, b_rg_x, lru_lambda, w_rg_a, w_rg_x, swa_sinks, rel_bias, conv_w)
    m_small = (m_pre_norm_g, m_post_norm_g, m_mem_norm_g, m_conv_b, m_b_rg_a, m_b_rg_x, m_lru_lambda, m_w_rg_a, m_w_rg_x, m_swa_sinks, m_rel_bias, m_conv_w)
    v_small = (v_pre_norm_g, v_post_norm_g, v_mem_norm_g, v_conv_b, v_b_rg_a, v_b_rg_x, v_lru_lambda, v_w_rg_a, v_w_rg_x, v_swa_sinks, v_rel_bias, v_conv_w)
    sm = _adamw_small(g_small, w_small, m_small, v_small)

    loss_total = lax.psum(st["loss"][0, 0], AXES)

    def leaves(k):
        s = sm[k]
        return [s[0], s[1], s[2], big[0][k], s[11], s[3], s[7], s[4], s[8], s[5], s[6], s[9], s[10],
                big[1][k], big[2][k], big[3][k], big[4][k], big[5][k]]

    return (loss_total, grad_x[None], *leaves(0), *leaves(1), *leaves(2), *leaves(3))
```

```python
import math

import jax
import jax.numpy as jnp
import numpy as np
from jax import lax
from jax.experimental import pallas as pl
from jax.experimental.pallas import tpu as pltpu

F32, BF = jnp.float32, jnp.bfloat16
MESH = pl.DeviceIdType.MESH
AXES = ("x", "y", "c")
N_DEV = 8

D_MODEL = 2048
D_RNN = 1024
RNN_BLOCKS = 8
RNN_BLOCK = 128
CONV_W = 4
LRU_C = 8.0
SWA_HEADS = 16
SWA_KV_HEADS = 2
SWA_HD = 64
WINDOW = 128
MEM_HEADS = 4
MEM_HD = 256
D_MEM = 1024
REL_BUCKETS = 32
REL_MAX_DIST = 128
EPS = 1e-6
NEG_INF = -1e30
D_IN = 12544
SEGMENTS = (("xr", 0, 1024, F32), ("g_rg", 1024, 1024, F32), ("q_s", 2048, 1024, BF), ("kv", 3072, 256, BF),
            ("g_swa", 3328, 1024, F32), ("q_m", 4352, 1024, BF), ("g_mem", 5376, 1024, F32), ("gl", 6400, 6144, F32))
SEG_TILE = 256

ADAM_LR, ADAM_B1, ADAM_B2, ADAM_EPS, ADAM_WD, ADAM_STEP = 0.001, 0.9, 0.999, 1e-08, 0.01, 10

NN = (((1,), (0,)), ((), ()))
NT = (((1,), (1,)), ((), ()))
TN = (((0,), (0,)), ((), ()))
MIB = 2 ** 20


def _dot(a, b, dn):
    return lax.dot_general(a, b, dn, preferred_element_type=F32)


VMEM_LIMIT_MIB = 48
VMEM_LIMIT_LARGE_MIB = 56


def _params(sem=None, large=False):
    return pltpu.CompilerParams(dimension_semantics=sem, vmem_limit_bytes=(VMEM_LIMIT_LARGE_MIB if large else VMEM_LIMIT_MIB) * MIB)


def _sigmoid(z):
    return 1.0 / (1.0 + jnp.exp(-z))


def _softplus(z):
    return jnp.maximum(z, 0.0) + jnp.log(1.0 + jnp.exp(-jnp.abs(z)))


def _expm1(z):
    p = z * (1.0 + z * (0.5 + z * (1.0 / 6 + z * (1.0 / 24 + z * (1.0 / 120 + z * (1.0 / 720 + z * (1.0 / 5040 + z / 40320)))))))
    return jnp.where(jnp.abs(z) < 0.3, p, jnp.exp(z) - 1.0)


def _flat(p):
    return 4 * p[0] + 2 * p[1] + p[2]


def _all_gather(arrs, name):
    n = len(arrs)

    def body(*refs):
        ins, outs = refs[:n], refs[n:2 * n]
        send_sems, recv_sems, local_sems = refs[2 * n:]
        x, y, c = lax.axis_index("x"), lax.axis_index("y"), lax.axis_index("c")
        me, sibling = (x, y, c), (x, y, 1 - c)
        chips = [(1 - x, y), (x, 1 - y), (1 - x, 1 - y)]

        def copy(a, k, block, to, src=None):
            dst = outs[a].at[_flat(block)]
            return pltpu.make_async_remote_copy(src_ref=dst if src is None else src, dst_ref=dst,
                                                send_sem=send_sems.at[a * 7 + k], recv_sem=recv_sems.at[a * 7 + k],
                                                device_id=to, device_id_type=MESH)

        mine = [pltpu.make_async_copy(ins[a], outs[a].at[_flat(me)], local_sems.at[a]) for a in range(n)]
        for cp in mine:
            cp.start()
        first = []
        for a in range(n):
            first += [copy(a, 1 + j, me, (*chip, c), src=ins[a]) for j, chip in enumerate(chips)]
            first.append(copy(a, 0, me, sibling, src=ins[a]))
        for cp in first:
            cp.start()
        passed = []
        for j, chip in enumerate(chips):
            for a in range(n):
                copy(a, 1 + j, (*chip, c), me).wait_recv()
                fw = copy(a, 4 + j, (*chip, c), sibling)
                fw.start()
                passed.append(fw)
        for a in range(n):
            copy(a, 0, sibling, me).wait_recv()
            for j, chip in enumerate(chips):
                copy(a, 4 + j, (*chip, 1 - c), me).wait_recv()
        for cp in first + passed:
            cp.wait_send()
        for cp in mine:
            cp.wait()

    any_spec = pl.BlockSpec(memory_space=pl.ANY)
    return pl.pallas_call(
        body, name=name,
        out_shape=[jax.ShapeDtypeStruct((N_DEV,) + a.shape, a.dtype) for a in arrs],
        in_specs=[any_spec] * n, out_specs=[any_spec] * n,
        scratch_shapes=[pltpu.SemaphoreType.DMA((7 * n,)), pltpu.SemaphoreType.DMA((7 * n,)), pltpu.SemaphoreType.DMA((n,))],
    )(*arrs)


def _all_gather_relayed(arrs, relay, name):
    n = len(arrs)
    K = 9

    def body(*refs):
        ins, outs = refs[:n], refs[n:2 * n]
        send_sems, recv_sems, local_sems = refs[2 * n:]
        x, y, c = lax.axis_index("x"), lax.axis_index("y"), lax.axis_index("c")
        me, sib = (x, y, c), (x, y, 1 - c)
        xn, yn, dg = (1 - x, y, c), (x, 1 - y, c), (1 - x, 1 - y, c)

        def other(p):
            return (p[0], p[1], 1 - p[2])

        def rows(a, half):
            h = arrs[a].shape[0] // 2
            return pl.ds(half * h, h)

        def copy(a, k, block, to, half=None, src=None):
            dst = outs[a].at[_flat(block)]
            if half is not None:
                dst = dst.at[rows(a, half)]
            return pltpu.make_async_remote_copy(src_ref=dst if src is None else src, dst_ref=dst,
                                                send_sem=send_sems.at[a * K + k], recv_sem=recv_sems.at[a * K + k],
                                                device_id=to, device_id_type=MESH)

        mine = [pltpu.make_async_copy(ins[a], outs[a].at[_flat(me)], local_sems.at[a]) for a in range(n)]
        for cp in mine:
            cp.start()
        sends = []

        def start(cp):
            cp.start()
            sends.append(cp)

        for a in range(n):
            start(copy(a, 1, me, xn, src=ins[a]))
            start(copy(a, 2, me, yn, src=ins[a]))
            if not relay[a]:
                start(copy(a, 3, me, dg, src=ins[a]))
            start(copy(a, 0, me, sib, src=ins[a]))
        for a in range(n):
            copy(a, 1, xn, me).wait_recv()
            if relay[a]:
                start(copy(a, 3, xn, yn, half=0))
            start(copy(a, 5, xn, sib))
        for a in range(n):
            copy(a, 2, yn, me).wait_recv()
            if relay[a]:
                start(copy(a, 4, yn, xn, half=1))
            start(copy(a, 6, yn, sib))
        for a in range(n):
            if relay[a]:
                copy(a, 3, dg, me, half=0).wait_recv()
                start(copy(a, 7, dg, sib, half=0))
                copy(a, 4, dg, me, half=1).wait_recv()
                start(copy(a, 8, dg, sib, half=1))
            else:
                copy(a, 3, dg, me).wait_recv()
                start(copy(a, 7, dg, sib))
        for a in range(n):
            copy(a, 0, sib, me).wait_recv()
            copy(a, 5, other(xn), me).wait_recv()
            copy(a, 6, other(yn), me).wait_recv()
            if relay[a]:
                copy(a, 7, other(dg), me, half=0).wait_recv()
                copy(a, 8, other(dg), me, half=1).wait_recv()
            else:
                copy(a, 7, other(dg), me).wait_recv()
        for cp in sends:
            cp.wait_send()
        for cp in mine:
            cp.wait()

    any_spec = pl.BlockSpec(memory_space=pl.ANY)
    return pl.pallas_call(
        body, name=name,
        out_shape=[jax.ShapeDtypeStruct((N_DEV,) + a.shape, a.dtype) for a in arrs],
        in_specs=[any_spec] * n, out_specs=[any_spec] * n,
        scratch_shapes=[pltpu.SemaphoreType.DMA((K * n,)), pltpu.SemaphoreType.DMA((K * n,)), pltpu.SemaphoreType.DMA((n,))],
    )(*arrs)


def _chip_peers(x, y):
    return [(1 - x, y), (x, 1 - y), (1 - x, 1 - y)]


def _chip(p):
    return 2 * p[0] + p[1]


def _plan_gather(kinds):
    def plan(x, y, c):
        out = []
        for a, kind in enumerate(kinds):
            for peer in [(x, y, 1 - c)] + [(*ch, c) for ch in _chip_peers(x, y)]:
                out.append((a, None, (kind, _flat((x, y, c))), peer, (kind, _flat(peer))))
        return out
    return plan


def _plan_everyone(n):
    def plan(x, y, c):
        out = []
        for a in range(n):
            for r in range(1, N_DEV):
                peer = (1 - x if r & 4 else x, 1 - y if r & 2 else y, 1 - c if r & 1 else c)
                out.append((a, None, ("lead", _flat((x, y, c))), peer, ("lead", _flat(peer))))
        return out
    return plan


def _plan_swap(n):
    def plan(x, y, c):
        return [(a, 1 - c, ("all", 0), (x, y, 1 - c), ("all", 0)) for a in range(n)]
    return plan


def _slot(ref, where):
    kind, k = where
    if kind == "all":
        return ref
    if kind == "lead":
        return ref.at[k]
    return ref.at[:, pl.ds(pl.multiple_of(k * 256, 256), 256)]


def _plan_scatter(n):
    def plan(x, y, c):
        out = []
        for a in range(n):
            for ch in _chip_peers(x, y):
                out.append((a, _chip(ch), ("lead", _chip((x, y))), (*ch, c), ("lead", _chip(ch))))
        return out
    return plan


HBM_SPEC = pl.BlockSpec(memory_space=pltpu.HBM)
SEM_SPEC = pl.BlockSpec(memory_space=pltpu.SEMAPHORE)


def _in_hbm(a):
    return pltpu.with_memory_space_constraint(a, pltpu.HBM)


def _exchange_start(srcs, lands, plan, name):
    n = len(srcs)
    count = len(plan(0, 0, 0))

    def body(*refs):
        src_refs, land_refs = refs[:n], refs[n:2 * n]
        send_sems, recv_sems = refs[2 * n], refs[2 * n + 1]
        token = refs[-1]
        x, y, c = lax.axis_index("x"), lax.axis_index("y"), lax.axis_index("c")
        for k, (a, si, di, peer, _) in enumerate(plan(x, y, c)):
            src = src_refs[a] if si is None else src_refs[a].at[si]
            pltpu.make_async_remote_copy(src_ref=src, dst_ref=_slot(land_refs[a], di), send_sem=send_sems.at[k],
                                         recv_sem=recv_sems.at[k], device_id=peer, device_id_type=MESH).start()
        token[...] = jnp.zeros_like(token)

    out = pl.pallas_call(
        body, name=name,
        out_shape=(pltpu.SemaphoreType.DMA((count,)), pltpu.SemaphoreType.DMA((count,)),
                   *[pltpu.HBM(a.shape, a.dtype) for a in lands], jax.ShapeDtypeStruct((8, 128), F32)),
        in_specs=[HBM_SPEC] * (2 * n),
        out_specs=(SEM_SPEC, SEM_SPEC, *([HBM_SPEC] * n), pl.BlockSpec(memory_space=pltpu.VMEM)),
        input_output_aliases={n + i: 2 + i for i in range(n)},
        compiler_params=pltpu.CompilerParams(has_side_effects=pltpu.SideEffectType.DATAFLOW_SIDE_EFFECTING),
    )(*[_in_hbm(a) for a in srcs], *[_in_hbm(a) for a in lands])
    return out[0], out[1], list(srcs), list(out[2:2 + n]), out[-1]


def _exchange_wait(send_sems, recv_sems, srcs, lands, plan, after, name):
    n = len(srcs)

    def body(*refs):
        src_refs, land_refs = refs[:n], refs[n:2 * n]
        send_sems, recv_sems = refs[2 * n], refs[2 * n + 1]
        x, y, c = lax.axis_index("x"), lax.axis_index("y"), lax.axis_index("c")
        for k, (a, si, _, peer, ri) in enumerate(plan(x, y, c)):
            src = src_refs[a] if si is None else src_refs[a].at[si]
            cp = pltpu.make_async_remote_copy(src_ref=src, dst_ref=_slot(land_refs[a], ri), send_sem=send_sems.at[k],
                                              recv_sem=recv_sems.at[k], device_id=peer, device_id_type=MESH)
            cp.wait_send()
            cp.wait_recv()

    out = pl.pallas_call(
        body, name=name,
        out_shape=tuple(pltpu.HBM(a.shape, a.dtype) for a in lands),
        in_specs=[HBM_SPEC] * (2 * n) + [SEM_SPEC, SEM_SPEC, pl.BlockSpec(memory_space=pl.ANY)],
        out_specs=tuple([HBM_SPEC] * n),
        input_output_aliases={n + i: i for i in range(n)},
        compiler_params=pltpu.CompilerParams(has_side_effects=pltpu.SideEffectType.DATAFLOW_SIDE_EFFECTING),
    )(*[_in_hbm(a) for a in srcs], *lands, send_sems, recv_sems, after)
    return list(out)


def _forward_to_sibling(lands, kinds, name):
    n = len(lands)

    def body(*refs):
        in_refs, out_refs = refs[:n], refs[n:2 * n]
        send_sems, recv_sems = refs[2 * n:]
        x, y, c = lax.axis_index("x"), lax.axis_index("y"), lax.axis_index("c")
        sibling = (x, y, 1 - c)

        def copy(a, j, slot):
            return pltpu.make_async_remote_copy(src_ref=_slot(in_refs[a], (kinds[a], slot)), dst_ref=_slot(out_refs[a], (kinds[a], slot)),
                                                send_sem=send_sems.at[a * 3 + j], recv_sem=recv_sems.at[a * 3 + j],
                                                device_id=sibling, device_id_type=MESH)

        sends = [copy(a, j, _flat((*ch, c))) for a in range(n) for j, ch in enumerate(_chip_peers(x, y))]
        for cp in sends:
            cp.start()
        for a in range(n):
            for j, ch in enumerate(_chip_peers(x, y)):
                copy(a, j, _flat((*ch, 1 - c))).wait_recv()
        for cp in sends:
            cp.wait_send()

    any_spec = pl.BlockSpec(memory_space=pl.ANY)
    return pl.pallas_call(
        body, name=name, out_shape=[jax.ShapeDtypeStruct(a.shape, a.dtype) for a in lands],
        in_specs=[any_spec] * n, out_specs=[any_spec] * n, input_output_aliases={a: a for a in range(n)},
        scratch_shapes=[pltpu.SemaphoreType.DMA((3 * n,)), pltpu.SemaphoreType.DMA((3 * n,))],
    )(*lands)


def _place_own(zones, owns, slot, name):
    n = len(zones)

    def body(slot_ref, *refs):
        for a in range(n):
            refs[2 * n + a][...] = refs[a][...]

    return pl.pallas_call(
        body, name=name,
        grid_spec=pltpu.PrefetchScalarGridSpec(
            num_scalar_prefetch=1, grid=(1,),
            in_specs=[pl.BlockSpec(o.shape, lambda i, s_ref: (0, 0)) for o in owns] + [pl.BlockSpec(memory_space=pl.ANY)] * n,
            out_specs=[pl.BlockSpec((None,) + o.shape, lambda i, s_ref: (s_ref[0], 0, 0)) for o in owns]),
        out_shape=[jax.ShapeDtypeStruct(z.shape, z.dtype) for z in zones],
        input_output_aliases={1 + n + a: a for a in range(n)},
        compiler_params=_params(("arbitrary",)),
    )(slot, *owns, *zones)


def _swap_with_sibling(parts, name):
    n = len(parts)

    def body(*refs):
        in_refs, out_refs = refs[:n], refs[n:2 * n]
        send_sems, recv_sems = refs[2 * n:]
        x, y, c = lax.axis_index("x"), lax.axis_index("y"), lax.axis_index("c")
        sends = [pltpu.make_async_remote_copy(src_ref=in_refs[a].at[1 - c], dst_ref=out_refs[a], send_sem=send_sems.at[a],
                                              recv_sem=recv_sems.at[a], device_id=(x, y, 1 - c), device_id_type=MESH)
                 for a in range(n)]
        for cp in sends:
            cp.start()
        for cp in sends:
            cp.wait()

    any_spec = pl.BlockSpec(memory_space=pl.ANY)
    return pl.pallas_call(
        body, name=name, out_shape=[jax.ShapeDtypeStruct(a.shape[1:], a.dtype) for a in parts],
        in_specs=[any_spec] * n, out_specs=[any_spec] * n,
        scratch_shapes=[pltpu.SemaphoreType.DMA((n,)), pltpu.SemaphoreType.DMA((n,))],
    )(*parts)


def _pair_sum(parts, got, core, name):
    _, _, R, C = parts.shape
    tr = 256 if R % 256 == 0 else R

    def body(c_ref, p_ref, g_ref, o_ref):
        o_ref[...] = (p_ref[...].astype(F32) + g_ref[...].astype(F32)).astype(o_ref.dtype)

    return pl.pallas_call(
        body, name=name,
        grid_spec=pltpu.PrefetchScalarGridSpec(
            num_scalar_prefetch=1, grid=(4, R // tr),
            in_specs=[pl.BlockSpec((None, None, tr, C), lambda j, i, c_ref: (c_ref[0], j, i, 0)),
                      pl.BlockSpec((None, tr, C), lambda j, i, c_ref: (j, i, 0))],
            out_specs=pl.BlockSpec((None, tr, C), lambda j, i, c_ref: (j, i, 0))),
        out_shape=jax.ShapeDtypeStruct((4, R, C), parts.dtype),
        compiler_params=_params(("parallel", "parallel")),
    )(core, parts, got)


def _matmul(a, b, mode, M, N, K, tm, tn, tk, out_dtype, name, b_noff=0, a_moff=0, a_koff=0, a_blocked=False, b_blocked=False,
            out_blocked=None, dep=None, addend=None):
    nm, nn, nk = M // tm, N // tn, K // tk
    if mode == "nn":
        a_spec = pl.BlockSpec((tm, tk), lambda j, i, k: (i, k + a_koff))
        b_spec = pl.BlockSpec((tk, tn), lambda j, i, k: (k, j + b_noff))
        dn = NN
    elif mode == "nt":
        a_spec = pl.BlockSpec((tm, tk), lambda j, i, k: (i, k + a_koff))
        if b_blocked:
            b_spec = pl.BlockSpec((None, tn, tk), lambda j, i, k: (k, j, 0))
        else:
            b_spec = pl.BlockSpec((tn, tk), lambda j, i, k: (j + b_noff, k))
        dn = NT
    else:
        if a_blocked:
            a_spec = pl.BlockSpec((None, tk, tm), lambda j, i, k: (i, k, 0))
        else:
            a_spec = pl.BlockSpec((tk, tm), lambda j, i, k: (k, i + a_moff))
        if b_blocked:
            b_spec = pl.BlockSpec((None, tk, tn), lambda j, i, k: (j, k, 0))
        else:
            b_spec = pl.BlockSpec((tk, tn), lambda j, i, k: (k, j + b_noff))
        dn = TN
    if out_blocked == "col":
        out_shape = jax.ShapeDtypeStruct((2, 4, M, tn), out_dtype)
        out_spec = pl.BlockSpec((None, None, tm, tn), lambda j, i, k: (j % 2, j // 2, i, 0))
    elif out_blocked == "row":
        out_shape = jax.ShapeDtypeStruct((2, 4, tm, N), out_dtype)
        out_spec = pl.BlockSpec((None, None, tm, tn), (lambda j, i, k: (i % 2, i // 2, 0, j)))
    elif out_blocked == "third":
        out_shape = jax.ShapeDtypeStruct((3, M, N // 3), out_dtype)
        out_spec = pl.BlockSpec((None, tm, tn), lambda j, i, k: (j // (nn // 3), i, j % (nn // 3)))
    else:
        out_shape = jax.ShapeDtypeStruct((M, N), out_dtype)
        out_spec = pl.BlockSpec((tm, tn), lambda j, i, k: (i, j))

    n_extra = (addend is not None) + (dep is not None)

    def body(a_ref, b_ref, *rest):
        o_ref, scratch = rest[n_extra], rest[n_extra + 1:]
        if nk == 1:
            prod = _dot(a_ref[...], b_ref[...], dn)
            if addend is not None:
                prod = prod + rest[0][...].astype(F32)
            o_ref[...] = prod.astype(out_dtype)
        else:
            assert addend is None
            acc_ref, = scratch
            k = pl.program_id(2)

            @pl.when(k == 0)
            def _():
                acc_ref[...] = jnp.zeros_like(acc_ref)

            acc_ref[...] += _dot(a_ref[...], b_ref[...], dn)

            @pl.when(k == nk - 1)
            def _():
                o_ref[...] = acc_ref[...].astype(out_dtype)

    return pl.pallas_call(
        body, name=name, grid=(nn, nm, nk),
        in_specs=[a_spec, b_spec] + ([] if addend is None else [out_spec])
        + ([] if dep is None else [pl.BlockSpec((8, 128), lambda j, i, k: (0, 0))]),
        out_specs=out_spec, out_shape=out_shape,
        scratch_shapes=[] if nk == 1 else [pltpu.VMEM((tm, tn), F32)],
        compiler_params=_params(("parallel", "parallel", "arbitrary")),
    )(a, b, *([] if addend is None else [addend]), *([] if dep is None else [dep]))


def _rms_fwd(x, g, name):
    R, Dm = x.shape
    tr = min(R, 256)

    def body(x_ref, g_ref, h_ref):
        xv = x_ref[...]
        r = lax.rsqrt(jnp.mean(xv * xv, axis=-1, keepdims=True) + EPS)
        h_ref[...] = (xv * r * g_ref[...]).astype(BF)

    return pl.pallas_call(
        body, name=name, grid=(R // tr,),
        in_specs=[pl.BlockSpec((tr, Dm), lambda i: (i, 0)), pl.BlockSpec((1, Dm), lambda i: (0, 0))],
        out_specs=pl.BlockSpec((tr, Dm), lambda i: (i, 0)), out_shape=jax.ShapeDtypeStruct((R, Dm), BF),
        compiler_params=_params(("parallel",)),
    )(x, g)


def _rms_gain_grad(dn, x, name):
    R, Dm = x.shape

    def body(dn_ref, x_ref, o_ref):
        xv = x_ref[...]
        r = lax.rsqrt(jnp.mean(xv * xv, axis=-1, keepdims=True) + EPS)
        o_ref[...] = jnp.sum(dn_ref[...] * xv * r, axis=0, keepdims=True)

    return pl.pallas_call(
        body, name=name, out_shape=jax.ShapeDtypeStruct((1, Dm), F32),
        compiler_params=_params(),
    )(dn, x)


def _shift_down(v, k, head8, row, T):
    if k == 0:
        return v
    r = pltpu.roll(v, k, 0)
    hr = pltpu.roll(head8, k, 0)
    top = jnp.where(row[:8] < k, hr, r[:8])
    return jnp.concatenate([top, r[8:]], axis=0)


def _shift_up(v, k, tail8, row, T):
    if k == 0:
        return v
    r = pltpu.roll(v, T - k, 0)
    tr = pltpu.roll(tail8, 8 - k, 0)
    bot = jnp.where(row[:8] >= 8 - k, tr, r[T - 8:])
    return jnp.concatenate([r[:T - 8], bot], axis=0)


def _rglru_gates(u, head8, grow, row, T, cw_ref, cb_ref, wa_ref, ba_ref, wx_ref, bx_ref, lam_ref):
    us = [_shift_down(u, k, head8, row, T) for k in range(CONV_W)]
    acc = us[0] * cw_ref[0:1, :]
    for k in range(1, CONV_W):
        acc = acc + us[k] * cw_ref[k:k + 1, :]
    conv = cb_ref[...] + acc
    cbf = conv.astype(BF)
    r_ = _sigmoid(_dot(cbf, wa_ref[0], NN) + ba_ref[...])
    i_ = _sigmoid(_dot(cbf, wx_ref[0], NN) + bx_ref[...])
    sp = _softplus(-lam_ref[...])
    la = -LRU_C * r_ * sp
    a = jnp.exp(la)
    mult_raw = jnp.sqrt(-_expm1(2.0 * la))
    mult = jnp.where(grow == 0, 1.0, mult_raw)
    return us, conv, cbf, r_, i_, sp, a, mult_raw, mult


def _rglru_specs(T, nt, rev):
    tmap = (lambda n, t: (nt - 1 - t, n)) if rev else (lambda n, t: (t, n))
    hmap = ((lambda n, t: (jnp.maximum((nt - 1 - t) * (T // 8) - 1, 0), n)) if rev
            else (lambda n, t: (jnp.maximum(t * (T // 8) - 1, 0), n)))
    tile = pl.BlockSpec((T, RNN_BLOCK), tmap)
    halo = pl.BlockSpec((8, RNN_BLOCK), hmap)
    vec = pl.BlockSpec((1, RNN_BLOCK), lambda n, t: (0, n))
    cw = pl.BlockSpec((CONV_W, RNN_BLOCK), lambda n, t: (0, n))
    wblk = pl.BlockSpec((1, RNN_BLOCK, RNN_BLOCK), lambda n, t: (n, 0, 0))
    return tile, halo, vec, cw, wblk


def _rglru_fwd(xr, g, cw, cb, wa, ba, wx, bx, lam, T):
    S = xr.shape[0]
    nt = S // T

    def body(u_ref, uh_ref, g_ref, cw_ref, cb_ref, wa_ref, ba_ref, wx_ref, bx_ref, lam_ref, h_ref, y_ref, carry):
        t = pl.program_id(1)

        @pl.when(t == 0)
        def _():
            carry[...] = jnp.zeros_like(carry)

        row = lax.broadcasted_iota(jnp.int32, (T, RNN_BLOCK), 0)
        grow = row + t * T
        head8 = jnp.where(t > 0, uh_ref[...], 0.0)
        _, conv, _, _, i_, _, a, _, mult = _rglru_gates(u_ref[...], head8, grow, row, T, cw_ref, cb_ref, wa_ref, ba_ref,
                                                         wx_ref, bx_ref, lam_ref)
        b = mult * i_ * conv
        s = 1
        while s < T:
            keep = row >= s
            a_s = jnp.where(keep, pltpu.roll(a, s, 0), 1.0)
            b_s = jnp.where(keep, pltpu.roll(b, s, 0), 0.0)
            b = a * b_s + b
            a = a * a_s
            s *= 2
        h = b + a * carry[0:1, :]
        carry[...] = jnp.broadcast_to(h[T - 1:T, :], carry.shape)
        h_ref[...] = h
        gv = g_ref[...]
        y_ref[...] = (h * (gv * _sigmoid(gv))).astype(BF)

    tile, halo, vec, cwspec, wblk = _rglru_specs(T, nt, False)
    return pl.pallas_call(
        body, name="rglru_fwd", grid=(RNN_BLOCKS, nt),
        in_specs=[tile, halo, tile, cwspec, vec, wblk, vec, wblk, vec, vec],
        out_specs=[tile, tile],
        out_shape=[jax.ShapeDtypeStruct((S, D_RNN), F32), jax.ShapeDtypeStruct((S, D_RNN), BF)],
        scratch_shapes=[pltpu.VMEM((8, RNN_BLOCK), F32)],
        compiler_params=_params(("parallel", "arbitrary")),
    )(xr, xr, g, cw, cb, wa, ba, wx, bx, lam)


def _rglru_bwd(xr, g, h, dy, cw, cb, wa, ba, wx, bx, lam, T):
    S = xr.shape[0]
    nt = S // T

    def body(u_ref, uh_ref, g_ref, h_ref, hh_ref, dy_ref, cw_ref, cb_ref, wa_ref, ba_ref, wx_ref, bx_ref, lam_ref,
             du_ref, dg_ref, dwa_ref, dwx_ref, dvec_ref, c_dhh, c_a, c_dconv):
        t = pl.program_id(1)
        tt = nt - 1 - t

        @pl.when(t == 0)
        def _():
            c_dhh[...] = jnp.zeros_like(c_dhh)
            c_a[...] = jnp.zeros_like(c_a)
            c_dconv[...] = jnp.zeros_like(c_dconv)
            dwa_ref[...] = jnp.zeros_like(dwa_ref)
            dwx_ref[...] = jnp.zeros_like(dwx_ref)
            dvec_ref[...] = jnp.zeros_like(dvec_ref)

        row = lax.broadcasted_iota(jnp.int32, (T, RNN_BLOCK), 0)
        row8 = row[:8]
        grow = row + tt * T
        head8 = jnp.where(tt > 0, uh_ref[...], 0.0)
        us, conv, cbf, r_, i_, sp, a, mult_raw, mult = _rglru_gates(
            u_ref[...], head8, grow, row, T, cw_ref, cb_ref, wa_ref, ba_ref, wx_ref, bx_ref, lam_ref)
        hv = h_ref[...]
        hprev = _shift_down(hv, 1, jnp.where(tt > 0, hh_ref[...], 0.0), row, T)
        gv = g_ref[...]
        sg = _sigmoid(gv)
        dyv = dy_ref[...]
        dg_ref[...] = (dyv * hv * (sg * (1.0 + gv * (1.0 - sg)))).astype(BF)
        d = dyv * (gv * sg)
        A = _shift_up(a, 1, c_a[...], row, T)
        s = 1
        while s < T:
            keep = row < T - s
            A_s = jnp.where(keep, pltpu.roll(A, T - s, 0), 1.0)
            d_s = jnp.where(keep, pltpu.roll(d, T - s, 0), 0.0)
            d = A * d_s + d
            A = A * A_s
            s *= 2
        dhh = d + A * c_dhh[0:1, :]
        da = dhh * hprev
        dconv = dhh * mult * i_
        di = dhh * mult * conv
        dmult = dhh * i_ * conv
        dla = da * a - jnp.where(grow == 0, 0.0, dmult * (a * a) / mult_raw)
        dr = dla * (-LRU_C * sp)
        dsp = jnp.sum(dla * (-LRU_C * r_), axis=0, keepdims=True)
        dza = dr * r_ * (1.0 - r_)
        dzx = di * i_ * (1.0 - i_)
        dza_b, dzx_b = dza.astype(BF), dzx.astype(BF)
        dconv = dconv + _dot(dza_b, wa_ref[0], NT) + _dot(dzx_b, wx_ref[0], NT)
        dwa_ref[0] += _dot(cbf, dza_b, TN)
        dwx_ref[0] += _dot(cbf, dzx_b, TN)
        lam = lam_ref[...]
        rows = [jnp.sum(dconv * us[k], axis=0, keepdims=True) for k in range(CONV_W)]
        rows += [jnp.sum(dconv, axis=0, keepdims=True), jnp.sum(dza, axis=0, keepdims=True),
                 jnp.sum(dzx, axis=0, keepdims=True), dsp * (-_sigmoid(-lam))]
        upd = jnp.zeros((8, RNN_BLOCK), F32)
        for j, rv in enumerate(rows):
            upd = upd + jnp.where(row8 == j, rv, 0.0)
        dvec_ref[...] += upd
        tail8 = c_dconv[...]
        du = dconv * cw_ref[0:1, :]
        for k in range(1, CONV_W):
            du = du + _shift_up(dconv, k, tail8, row, T) * cw_ref[k:k + 1, :]
        du_ref[...] = du.astype(BF)
        c_dhh[...] = jnp.broadcast_to(dhh[0:1, :], c_dhh.shape)
        c_a[...] = jnp.broadcast_to(a[0:1, :], c_a.shape)
        c_dconv[...] = dconv[:8]

    tile, halo, vec, cwspec, wblk = _rglru_specs(T, nt, True)
    acc8 = pl.BlockSpec((8, RNN_BLOCK), lambda n, t: (0, n))
    return pl.pallas_call(
        body, name="rglru_bwd", grid=(RNN_BLOCKS, nt),
        in_specs=[tile, halo, tile, tile, halo, tile, cwspec, vec, wblk, vec, wblk, vec, vec],
        out_specs=[tile, tile, wblk, wblk, acc8],
        out_shape=[jax.ShapeDtypeStruct((S, D_RNN), BF), jax.ShapeDtypeStruct((S, D_RNN), BF),
                   jax.ShapeDtypeStruct((RNN_BLOCKS, RNN_BLOCK, RNN_BLOCK), F32),
                   jax.ShapeDtypeStruct((RNN_BLOCKS, RNN_BLOCK, RNN_BLOCK), F32),
                   jax.ShapeDtypeStruct((8, D_RNN), F32)],
        scratch_shapes=[pltpu.VMEM((8, RNN_BLOCK), F32)] * 3,
        compiler_params=_params(("parallel", "arbitrary")),
    )(xr, xr, g, h, h, dy, cw, cb, wa, ba, wx, bx, lam)


def _rel_bucket_map():
    qi = np.arange(WINDOW)[:, None]
    kj = np.arange(2 * WINDOW)[None, :]
    dist = jnp.asarray(qi + WINDOW - kj, jnp.int32)
    n = jnp.maximum(dist, 0)
    max_exact = REL_BUCKETS // 2
    ratio = jnp.log(jnp.maximum(n, 1).astype(F32) / max_exact) / math.log(REL_MAX_DIST / max_exact)
    large = jnp.minimum(max_exact + (ratio * (REL_BUCKETS - max_exact)).astype(jnp.int32), REL_BUCKETS - 1)
    bucket = jnp.where(n < max_exact, n, large).astype(jnp.int32)
    j = np.arange(WINDOW)[None, :]
    return jnp.where(jnp.asarray(j > qi), bucket[:, :WINDOW], bucket[:, WINDOW:])


def _swa_common(n, kv_ref, bucket_ref, relb_ref, bias_scr):
    @pl.when(n == 0)
    def _():
        bk = bucket_ref[...]
        for h in range(SWA_HEADS):
            acc = jnp.zeros((WINDOW, WINDOW), F32)
            for b in range(REL_BUCKETS):
                acc = acc + jnp.where(bk == b, relb_ref[b, h], 0.0)
            bias_scr[h] = acc

    prev0 = pl.multiple_of(jnp.maximum(n - 1, 0) * WINDOW, WINDOW)
    cur0 = pl.multiple_of(n * WINDOW, WINDOW)
    kk = jnp.concatenate([kv_ref[pl.ds(prev0, WINDOW), :], kv_ref[pl.ds(cur0, WINDOW), :]], axis=0).astype(F32)
    rowi = lax.broadcasted_iota(jnp.int32, (WINDOW, WINDOW), 0)
    col = lax.broadcasted_iota(jnp.int32, (WINDOW, WINDOW), 1)
    from_prev = col > rowi
    return kk, from_prev, prev0, cur0


def _fold(full, from_prev):
    return jnp.where(from_prev, full[:, :WINDOW], full[:, WINDOW:])


def _unfold(sq, from_prev):
    return jnp.concatenate([jnp.where(from_prev, sq, 0.0), jnp.where(from_prev, 0.0, sq)], axis=1)


def _half_pair(part, kvh):
    lo = lax.broadcasted_iota(jnp.int32, part.shape, 1) < SWA_HD
    if kvh == 0:
        pa = jnp.where(lo, part, 0.0)
        pb = pltpu.roll(pa, SWA_HD, 1)
    else:
        pb = jnp.where(lo, 0.0, part)
        pa = pltpu.roll(pb, SWA_HD, 1)
    return pa.astype(BF), pb.astype(BF)


ALL_HEADS = SWA_HEADS * WINDOW


def _sink_column(sinks):
    return jnp.repeat(sinks.reshape(SWA_HEADS), WINDOW).reshape(ALL_HEADS, 1)


def _swa_operands(kk):
    return [(_half_pair(kk[:, :128], kvh), _half_pair(kk[:, 128:], kvh)) for kvh in range(SWA_KV_HEADS)]


def _swa_probs(n, q_ref, ops, bias_scr, sinkc_ref, from_prev):
    lgs = []
    for kvh in range(SWA_KV_HEADS):
        (ka, kb), _ = ops[kvh]
        for p in range(4):
            q2 = q_ref[:, kvh * 512 + p * 128:kvh * 512 + p * 128 + 128]
            lgs += [_fold(_dot(q2, ka, NT), from_prev), _fold(_dot(q2, kb, NT), from_prev)]
    lg = jnp.concatenate(lgs, axis=0) * (SWA_HD ** -0.5) + bias_scr[...].reshape(ALL_HEADS, WINDOW)
    rowi = jnp.bitwise_and(lax.broadcasted_iota(jnp.int32, (ALL_HEADS, WINDOW), 0), WINDOW - 1)
    col = lax.broadcasted_iota(jnp.int32, (ALL_HEADS, WINDOW), 1)
    no_prev = jnp.where(n > 0, 0, 4 * WINDOW)
    lg = jnp.where(jnp.logical_or(col <= rowi, col > rowi + no_prev), lg, NEG_INF)
    sink = sinkc_ref[...]
    m = jnp.maximum(jnp.max(lg, axis=-1, keepdims=True), sink)
    e = jnp.exp(lg - m)
    es = jnp.exp(sink - m)
    den = jnp.sum(e, axis=-1, keepdims=True) + es
    return e / den, es / den


def _swa_fwd(q, kv, g, bucket, rel_bias, sink_col):
    S = q.shape[0]
    nb = S // WINDOW

    def body(q_ref, kv_ref, g_ref, bucket_ref, relb_ref, sinkc_ref, o_ref, y_ref, bias_scr):
        n = pl.program_id(0)
        kk, from_prev, _, _ = _swa_common(n, kv_ref, bucket_ref, relb_ref, bias_scr)
        ops = _swa_operands(kk)
        pr, _ = _swa_probs(n, q_ref, ops, bias_scr, sinkc_ref, from_prev)
        for kvh in range(SWA_KV_HEADS):
            _, (va, vb) = ops[kvh]
            for p in range(4):
                c0 = kvh * 512 + p * 128
                r0 = (kvh * 8 + 2 * p) * WINDOW
                o2 = (_dot(_unfold(pr[r0:r0 + WINDOW], from_prev).astype(BF), va, NN)
                      + _dot(_unfold(pr[r0 + WINDOW:r0 + 2 * WINDOW], from_prev).astype(BF), vb, NN))
                o_ref[:, c0:c0 + 128] = o2
                gv = g_ref[:, c0:c0 + 128]
                y_ref[:, c0:c0 + 128] = (o2 * (gv * _sigmoid(gv))).astype(BF)

    blk = pl.BlockSpec((WINDOW, 1024), lambda n: (n, 0))
    smem = pl.BlockSpec(memory_space=pltpu.SMEM)
    sinkc = pl.BlockSpec((ALL_HEADS, 1), lambda n: (0, 0))
    return pl.pallas_call(
        body, name="swa_fwd", grid=(nb,),
        in_specs=[blk, pl.BlockSpec((S, 256), lambda n: (0, 0)), blk, pl.BlockSpec((WINDOW, WINDOW), lambda n: (0, 0)), smem, sinkc],
        out_specs=[blk, blk],
        out_shape=[jax.ShapeDtypeStruct((S, 1024), F32), jax.ShapeDtypeStruct((S, 1024), BF)],
        scratch_shapes=[pltpu.VMEM((SWA_HEADS, WINDOW, WINDOW), F32)],
        compiler_params=_params(("arbitrary",)),
    )(q, kv, g, bucket, rel_bias, sink_col)


def _swa_bwd(q, kv, g, o, dy, bucket, rel_bias, sink_col):
    S = q.shape[0]
    nb = S // WINDOW

    def body(q_ref, kv_ref, g_ref, o_ref, dy_ref, bucket_ref, relb_ref, sinkc_ref,
             dq_ref, dg_ref, dkv_ref, dsink_ref, drel_ref, bias_scr, dbias_scr, dsink_scr):
        n = pl.program_id(0)

        @pl.when(n == 0)
        def _():
            dbias_scr[...] = jnp.zeros_like(dbias_scr)
            dsink_scr[...] = jnp.zeros_like(dsink_scr)
            dkv_ref[...] = jnp.zeros_like(dkv_ref)

        kk, from_prev, prev0, cur0 = _swa_common(n, kv_ref, bucket_ref, relb_ref, bias_scr)
        ops = _swa_operands(kk)
        pr, ps = _swa_probs(n, q_ref, ops, bias_scr, sinkc_ref, from_prev)
        do2s, dps = [], []
        for kvh in range(SWA_KV_HEADS):
            _, (va, vb) = ops[kvh]
            for p in range(4):
                c0 = kvh * 512 + p * 128
                gv = g_ref[:, c0:c0 + 128]
                sg = _sigmoid(gv)
                dyv = dy_ref[:, c0:c0 + 128]
                dg_ref[:, c0:c0 + 128] = (dyv * o_ref[:, c0:c0 + 128] * (sg * (1.0 + gv * (1.0 - sg)))).astype(BF)
                do2 = (dyv * (gv * sg)).astype(BF)
                do2s.append(do2)
                dps += [_fold(_dot(do2, va, NT), from_prev), _fold(_dot(do2, vb, NT), from_prev)]
        dp = jnp.concatenate(dps, axis=0)
        delta = jnp.sum(pr * dp, axis=-1, keepdims=True)
        ds = pr * (dp - delta)
        dbias_scr[...] += ds.reshape(SWA_HEADS, WINDOW, WINDOW)
        dsink_scr[...] += ps * delta
        dsc = ds * (SWA_HD ** -0.5)
        lo256 = lax.broadcasted_iota(jnp.int32, (2 * WINDOW, 128), 1) < SWA_HD
        dks, dvs = [], []
        for kvh in range(SWA_KV_HEADS):
            (ka, kb), _ = ops[kvh]
            dka = jnp.zeros((2 * WINDOW, 128), F32)
            dkb, dva, dvb = dka, dka, dka
            for p in range(4):
                c0 = kvh * 512 + p * 128
                r0 = (kvh * 8 + 2 * p) * WINDOW
                q2 = q_ref[:, c0:c0 + 128]
                do2 = do2s[kvh * 4 + p]
                ds0 = _unfold(dsc[r0:r0 + WINDOW], from_prev).astype(BF)
                ds1 = _unfold(dsc[r0 + WINDOW:r0 + 2 * WINDOW], from_prev).astype(BF)
                dq_ref[:, c0:c0 + 128] = (_dot(ds0, ka, NN) + _dot(ds1, kb, NN)).astype(BF)
                dka = dka + _dot(ds0, q2, TN)
                dkb = dkb + _dot(ds1, q2, TN)
                dva = dva + _dot(_unfold(pr[r0:r0 + WINDOW], from_prev).astype(BF), do2, TN)
                dvb = dvb + _dot(_unfold(pr[r0 + WINDOW:r0 + 2 * WINDOW], from_prev).astype(BF), do2, TN)
            dks.append(jnp.where(lo256, dka, 0.0) + pltpu.roll(jnp.where(lo256, 0.0, dkb), SWA_HD, 1))
            dvs.append(jnp.where(lo256, dva, 0.0) + pltpu.roll(jnp.where(lo256, 0.0, dvb), SWA_HD, 1))
        dk = dks[0] + pltpu.roll(dks[1], SWA_HD, 1)
        dv = dvs[0] + pltpu.roll(dvs[1], SWA_HD, 1)
        dkv_ref[pl.ds(prev0, WINDOW), 0:128] += dk[:WINDOW]
        dkv_ref[pl.ds(prev0, WINDOW), 128:256] += dv[:WINDOW]
        dkv_ref[pl.ds(cur0, WINDOW), 0:128] += dk[WINDOW:]
        dkv_ref[pl.ds(cur0, WINDOW), 128:256] += dv[WINDOW:]

        @pl.when(n == nb - 1)
        def _():
            dsink_ref[...] = -jnp.sum(dsink_scr[...].reshape(SWA_HEADS, WINDOW, 1), axis=1)
            bk = bucket_ref[...]
            sums = []
            for b in range(REL_BUCKETS):
                sums.append(jnp.sum(jnp.where((bk == b)[None], dbias_scr[...], 0.0), axis=1))
            drel_ref[...] = jnp.sum(jnp.concatenate(sums, axis=0), axis=1, keepdims=True)

    blk = pl.BlockSpec((WINDOW, 1024), lambda n: (n, 0))
    smem = pl.BlockSpec(memory_space=pltpu.SMEM)
    whole = lambda shape: pl.BlockSpec(shape, lambda n: (0, 0))
    return pl.pallas_call(
        body, name="swa_bwd", grid=(nb,),
        in_specs=[blk, whole((S, 256)), blk, blk, blk, whole((WINDOW, WINDOW)), smem, whole((ALL_HEADS, 1))],
        out_specs=[blk, blk, whole((S, 256)), whole((SWA_HEADS, 1)), whole((REL_BUCKETS * SWA_HEADS, 1))],
        out_shape=[jax.ShapeDtypeStruct((S, 1024), BF), jax.ShapeDtypeStruct((S, 1024), BF),
                   jax.ShapeDtypeStruct((S, 256), F32), jax.ShapeDtypeStruct((SWA_HEADS, 1), F32),
                   jax.ShapeDtypeStruct((REL_BUCKETS * SWA_HEADS, 1), F32)],
        scratch_shapes=[pltpu.VMEM((SWA_HEADS, WINDOW, WINDOW), F32), pltpu.VMEM((SWA_HEADS, WINDOW, WINDOW), F32),
                        pltpu.VMEM((ALL_HEADS, 1), F32)],
        compiler_params=_params(("arbitrary",)),
    )(q, kv, g, o, dy, bucket, rel_bias, sink_col)


def _mem_probs(qh, mk):
    lg = _dot(qh, mk, NT) * (MEM_HD ** -0.5)
    e = jnp.exp(lg - jnp.max(lg, axis=-1, keepdims=True))
    return e / jnp.sum(e, axis=-1, keepdims=True)


def _mem_fwd(q, mkv, g):
    S = q.shape[0]
    M = mkv.shape[0]
    tq = 256

    def body(q_ref, mkv_ref, g_ref, o_ref, y_ref):
        for h in range(MEM_HEADS):
            c0 = h * MEM_HD
            pr = _mem_probs(q_ref[:, c0:c0 + MEM_HD], mkv_ref[:, c0:c0 + MEM_HD])
            o = _dot(pr.astype(BF), mkv_ref[:, D_MEM + c0:D_MEM + c0 + MEM_HD], NN)
            o_ref[:, c0:c0 + MEM_HD] = o
            gv = g_ref[:, c0:c0 + MEM_HD]
            y_ref[:, c0:c0 + MEM_HD] = (o * (gv * _sigmoid(gv))).astype(BF)

    blk = pl.BlockSpec((tq, D_MEM), lambda i: (i, 0))
    return pl.pallas_call(
        body, name="mem_fwd", grid=(S // tq,),
        in_specs=[blk, pl.BlockSpec((M, 2 * D_MEM), lambda i: (0, 0)), blk], out_specs=[blk, blk],
        out_shape=[jax.ShapeDtypeStruct((S, D_MEM), F32), jax.ShapeDtypeStruct((S, D_MEM), BF)],
        compiler_params=_params(("parallel",)),
    )(q, mkv, g)


def _mem_bwd(q, mkv, g, o, dy):
    S = q.shape[0]
    M = mkv.shape[0]
    tq = 256

    def body(q_ref, mkv_ref, g_ref, o_ref, dy_ref, dq_ref, dg_ref, dmkv_ref):
        @pl.when(pl.program_id(0) == 0)
        def _():
            dmkv_ref[...] = jnp.zeros_like(dmkv_ref)

        for h in range(MEM_HEADS):
            c0 = h * MEM_HD
            qh = q_ref[:, c0:c0 + MEM_HD]
            mk = mkv_ref[:, c0:c0 + MEM_HD]
            mv = mkv_ref[:, D_MEM + c0:D_MEM + c0 + MEM_HD]
            gv = g_ref[:, c0:c0 + MEM_HD]
            sg = _sigmoid(gv)
            dyv = dy_ref[:, c0:c0 + MEM_HD]
            dg_ref[:, c0:c0 + MEM_HD] = (dyv * o_ref[:, c0:c0 + MEM_HD] * (sg * (1.0 + gv * (1.0 - sg)))).astype(BF)
            do = (dyv * (gv * sg)).astype(BF)
            pr = _mem_probs(qh, mk)
            dp = _dot(do, mv, NT)
            ds = pr * (dp - jnp.sum(pr * dp, axis=-1, keepdims=True))
            dsb = (ds * (MEM_HD ** -0.5)).astype(BF)
            dq_ref[:, c0:c0 + MEM_HD] = _dot(dsb, mk, NN).astype(BF)
            dmkv_ref[:, c0:c0 + MEM_HD] += _dot(dsb, qh, TN)
            dmkv_ref[:, D_MEM + c0:D_MEM + c0 + MEM_HD] += _dot(pr.astype(BF), do, TN)

    blk = pl.BlockSpec((tq, D_MEM), lambda i: (i, 0))
    whole = pl.BlockSpec((M, 2 * D_MEM), lambda i: (0, 0))
    return pl.pallas_call(
        body, name="mem_bwd", grid=(S // tq,),
        in_specs=[blk, whole, blk, blk, blk], out_specs=[blk, blk, whole],
        out_shape=[jax.ShapeDtypeStruct((S, D_MEM), BF), jax.ShapeDtypeStruct((S, D_MEM), BF),
                   jax.ShapeDtypeStruct((M, 2 * D_MEM), F32)],
        compiler_params=_params(("arbitrary",)),
    )(q, mkv, g, o, dy)


MERGE_TN = 512


def _merge_specs(tm):
    ytile = pl.BlockSpec((tm, 1024), lambda i, j: (i, 0))
    wblk = pl.BlockSpec((MERGE_TN, 1024), lambda i, j: (j, 0))
    gls = [pl.BlockSpec((None, tm, MERGE_TN), (lambda i, j, br=br: (br, i, j))) for br in range(3)]
    otile = pl.BlockSpec((tm, MERGE_TN), lambda i, j: (i, j))
    return ytile, wblk, gls, otile


def _merge_fwd(ys, ws, gl, tm):
    S = gl.shape[1]

    def body(y0, y1, y2, w0, w1, w2, g0, g1, g2, o_ref):
        acc = None
        for y_ref, w_ref, g_ref in ((y0, w0, g0), (y1, w1, g1), (y2, w2, g2)):
            term = _sigmoid(g_ref[...]) * _dot(y_ref[...], w_ref[...], NT)
            acc = term if acc is None else acc + term
        o_ref[...] = acc.astype(BF)

    ytile, wblk, gls, otile = _merge_specs(tm)
    return pl.pallas_call(
        body, name="merge_fwd", grid=(S // tm, D_MODEL // MERGE_TN),
        in_specs=[ytile] * 3 + [wblk] * 3 + gls, out_specs=otile,
        out_shape=jax.ShapeDtypeStruct((S, D_MODEL), BF),
        compiler_params=_params(("parallel", "arbitrary")),
    )(*ys, *ws, gl, gl, gl)


def _merge_bwd(dout, w_out, ys, ws, gl, tm):
    S = gl.shape[1]

    def body(do_ref, wo_ref, y0, y1, y2, w0, w1, w2, g0, g1, g2, dg0, dg1, dg2, dp0, dp1, dp2):
        dm = _dot(do_ref[...], wo_ref[...], NT)
        for y_ref, w_ref, g_ref, dg_ref, dp_ref in ((y0, w0, g0, dg0, dp0), (y1, w1, g1, dg1, dp1), (y2, w2, g2, dg2, dp2)):
            gate = _sigmoid(g_ref[...])
            pv = _dot(y_ref[...], w_ref[...], NT)
            dg_ref[...] = (dm * pv * gate * (1.0 - gate)).astype(BF)
            dp_ref[...] = (dm * gate).astype(BF)

    ytile, wblk, gls, otile = _merge_specs(tm)
    out = jax.ShapeDtypeStruct((S, D_MODEL), BF)
    return pl.pallas_call(
        body, name="merge_bwd", grid=(S // tm, D_MODEL // MERGE_TN),
        in_specs=[pl.BlockSpec((tm, D_MODEL), lambda i, j: (i, 0)), pl.BlockSpec((MERGE_TN, D_MODEL), lambda i, j: (j, 0))]
        + [ytile] * 3 + [wblk] * 3 + gls,
        out_specs=[otile] * 6, out_shape=[out] * 6,
        compiler_params=_params(("parallel", "arbitrary")),
    )(dout, w_out, *ys, *ws, gl, gl, gl)


def _out_loss(merged, w_out, x, target, post_g, tm):
    S = x.shape[0]

    def body(m_ref, w_ref, x_ref, t_ref, g_ref, dout_ref, dy_ref, loss_ref, dpost_ref):
        @pl.when(pl.program_id(0) == 0)
        def _():
            loss_ref[...] = jnp.zeros_like(loss_ref)
            dpost_ref[...] = jnp.zeros_like(dpost_ref)

        out = _dot(m_ref[...], w_ref[...], NN)
        r = lax.rsqrt(jnp.mean(out * out, axis=-1, keepdims=True) + EPS)
        nrm = out * r
        gv = g_ref[...]
        err = (x_ref[...] + nrm * gv) - t_ref[...]
        sq = jnp.sum(jnp.sum(err * err, axis=1, keepdims=True), axis=0, keepdims=True)
        loss_ref[...] += sq * (0.5 / D_MODEL)
        dy = err * (1.0 / D_MODEL)
        dy_ref[...] = dy
        dpost_ref[...] += jnp.sum(dy * nrm, axis=0, keepdims=True)
        dn = dy * gv
        dout_ref[...] = (r * (dn - nrm * jnp.mean(dn * nrm, axis=-1, keepdims=True))).astype(BF)

    row = pl.BlockSpec((tm, D_MODEL), lambda i: (i, 0))
    return pl.pallas_call(
        body, name="out_loss", grid=(S // tm,),
        in_specs=[row, pl.BlockSpec((D_MODEL, D_MODEL), lambda i: (0, 0)), row, row, pl.BlockSpec((1, D_MODEL), lambda i: (0, 0))],
        out_specs=[row, row, pl.BlockSpec((8, 128), lambda i: (0, 0)), pl.BlockSpec((1, D_MODEL), lambda i: (0, 0))],
        out_shape=[jax.ShapeDtypeStruct((S, D_MODEL), BF), jax.ShapeDtypeStruct((S, D_MODEL), F32),
                   jax.ShapeDtypeStruct((8, 128), F32), jax.ShapeDtypeStruct((1, D_MODEL), F32)],
        compiler_params=_params(("arbitrary",)),
    )(merged, w_out, x, target, post_g)


def _dh_dx(dproj, w_in, x, dy, pre_g, tm):
    S = x.shape[0]
    nk, tk = dproj.shape[0], dproj.shape[2]

    def body(dp_ref, w_ref, x_ref, dy_ref, g_ref, dx_ref, dpre_ref, acc_ref):
        i, k = pl.program_id(0), pl.program_id(1)

        @pl.when(jnp.logical_and(i == 0, k == 0))
        def _():
            dpre_ref[...] = jnp.zeros_like(dpre_ref)

        @pl.when(k == 0)
        def _():
            acc_ref[...] = jnp.zeros_like(acc_ref)

        acc_ref[...] += _dot(dp_ref[...], w_ref[...], NN)

        @pl.when(k == nk - 1)
        def _():
            dh = acc_ref[...]
            xv = x_ref[...]
            r = lax.rsqrt(jnp.mean(xv * xv, axis=-1, keepdims=True) + EPS)
            nrm = xv * r
            dpre_ref[...] += jnp.sum(dh * nrm, axis=0, keepdims=True)
            dn = dh * g_ref[...]
            dx_ref[...] = r * (dn - nrm * jnp.mean(dn * nrm, axis=-1, keepdims=True)) + dy_ref[...]

    row = pl.BlockSpec((tm, D_MODEL), lambda i, k: (i, 0))
    vec = pl.BlockSpec((1, D_MODEL), lambda i, k: (0, 0))
    return pl.pallas_call(
        body, name="dh_dx", grid=(S // tm, nk),
        in_specs=[pl.BlockSpec((None, tm, tk), lambda i, k: (k, i, 0)), pl.BlockSpec((tk, D_MODEL), lambda i, k: (k, 0)), row, row, vec],
        out_specs=[row, vec],
        out_shape=[jax.ShapeDtypeStruct((S, D_MODEL), F32), jax.ShapeDtypeStruct((1, D_MODEL), F32)],
        scratch_shapes=[pltpu.VMEM((tm, D_MODEL), F32)],
        compiler_params=_params(("arbitrary", "arbitrary"), large=True),
    )(dproj, w_in, x, dy, pre_g)


def _sum_parts(parts, name):
    P, R, C = parts.shape
    tr = max(t for t in range(8, 513, 8) if R % t == 0)

    def body(p_ref, o_ref):
        acc = p_ref[0]
        for j in range(1, P):
            acc = acc + p_ref[j]
        o_ref[...] = acc

    return pl.pallas_call(
        body, name=name, grid=(R // tr,),
        in_specs=[pl.BlockSpec((P, tr, C), lambda i: (0, i, 0))], out_specs=pl.BlockSpec((tr, C), lambda i: (i, 0)),
        out_shape=jax.ShapeDtypeStruct((R, C), F32), compiler_params=_params(("parallel",)),
    )(parts)


def _adamw(lands, sums, chip, w, m, v, name):
    lands = list(lands) if isinstance(lands, (list, tuple)) else [lands]
    sums = list(sums) if isinstance(sums, (list, tuple)) else [sums]
    _, R, cols = lands[0].shape
    C = cols * len(lands)
    tr = max(t for t in range(16, 257, 16) if R % t == 0)
    c1 = 1.0 - ADAM_B1 ** ADAM_STEP
    c2 = 1.0 - ADAM_B2 ** ADAM_STEP

    def body(chip_ref, *refs):
        p_refs = refs[:4 * len(lands)]
        w_ref, m_ref, v_ref, g_ref, d_ref, nm_ref, nv_ref = refs[4 * len(lands):]
        gs = []
        for q in range(len(lands)):
            gq = p_refs[4 * q + 3][...].astype(F32)
            for j in range(3):
                gq = gq + p_refs[4 * q + j][...].astype(F32)
            gs.append(gq)
        g = gs[0] if len(gs) == 1 else jnp.concatenate(gs, axis=1)
        nm = ADAM_B1 * m_ref[...] + (1.0 - ADAM_B1) * g
        nv = ADAM_B2 * v_ref[...] + (1.0 - ADAM_B2) * (g * g)
        g_ref[...] = g
        nm_ref[...] = nm
        nv_ref[...] = nv
        d_ref[...] = -ADAM_LR * ((nm / c1) / (jnp.sqrt(nv / c2) + ADAM_EPS) + ADAM_WD * w_ref[...])

    tile = pl.BlockSpec((None, tr, C), lambda i, c_ref: (0, i, 0))
    specs, operands = [], []
    for q in range(len(lands)):
        for k in range(3):
            specs.append(pl.BlockSpec((None, tr, cols), (lambda i, c_ref, k=k: (k + (c_ref[0] <= k).astype(jnp.int32), i, 0))))
            operands.append(lands[q])
        specs.append(pl.BlockSpec((None, tr, cols), (lambda i, c_ref: (c_ref[0], i, 0))))
        operands.append(sums[q])
    return pl.pallas_call(
        body, name=name,
        grid_spec=pltpu.PrefetchScalarGridSpec(num_scalar_prefetch=1, grid=(R // tr,), in_specs=specs + [tile, tile, tile],
                                               out_specs=[tile] * 4),
        out_shape=[jax.ShapeDtypeStruct((1, R, C), F32)] * 4, compiler_params=_params(("parallel",)),
    )(chip, *operands, w, m, v)


def _adamw_small(gs, ws, ms, vs):
    n = len(ws)
    c1 = 1.0 - ADAM_B1 ** ADAM_STEP
    c2 = 1.0 - ADAM_B2 ** ADAM_STEP

    def flat2(a):
        return a.reshape(-1, a.shape[-1])

    def body(*refs):
        ins, outs = refs[:4 * n], refs[4 * n:]
        for a in range(n):
            g, w, m, v = (ins[k * n + a][...] for k in range(4))
            nm = ADAM_B1 * m + (1.0 - ADAM_B1) * g
            nv = ADAM_B2 * v + (1.0 - ADAM_B2) * (g * g)
            outs[a][...] = g
            outs[n + a][...] = -ADAM_LR * ((nm / c1) / (jnp.sqrt(nv / c2) + ADAM_EPS) + ADAM_WD * w)
            outs[2 * n + a][...] = nm
            outs[3 * n + a][...] = nv

    shapes = [flat2(w).shape for w in ws]
    out = pl.pallas_call(
        body, name="adamw_small", out_shape=[jax.ShapeDtypeStruct(sh, F32) for sh in shapes] * 4,
        compiler_params=_params(),
    )(*[g.reshape(sh) for g, sh in zip(gs, shapes)], *[flat2(a) for a in (*ws, *ms, *vs)])
    return [[out[k * n + a].reshape(ws[a].shape) for a in range(n)] for k in range(4)]


def _project(h, w_t):
    S = h.shape[0]
    n_tiles = D_IN // SEG_TILE
    ranges = [(c0 // SEG_TILE, (c0 + width) // SEG_TILE) for _, c0, width, _ in SEGMENTS]

    def body(h_ref, w_ref, *outs):
        j = pl.program_id(0)
        prod = _dot(h_ref[...], w_ref[...], NT)
        for (j0, j1), (_, _, _, dt), o_ref in zip(ranges, SEGMENTS, outs):
            @pl.when(jnp.logical_and(j >= j0, j < j1))
            def _(o_ref=o_ref, dt=dt):
                o_ref[...] = prod.astype(dt)

    out_shapes, out_specs = [], []
    for (j0, j1), (name, _, width, dt) in zip(ranges, SEGMENTS):
        if name == "gl":
            per = (j1 - j0) // 3
            out_shapes.append(jax.ShapeDtypeStruct((3, S, width // 3), dt))
            out_specs.append(pl.BlockSpec((None, S, SEG_TILE), (lambda j, j0=j0, j1=j1, per=per: (
                jnp.clip(j - j0, 0, j1 - j0 - 1) // per, 0, jnp.clip(j - j0, 0, j1 - j0 - 1) % per))))
        else:
            out_shapes.append(jax.ShapeDtypeStruct((S, width), dt))
            out_specs.append(pl.BlockSpec((S, SEG_TILE), (lambda j, j0=j0, j1=j1: (0, jnp.clip(j - j0, 0, j1 - j0 - 1)))))
    outs = pl.pallas_call(
        body, name="proj", grid=(n_tiles,),
        in_specs=[pl.BlockSpec((S, D_MODEL), lambda j: (0, 0)), pl.BlockSpec((SEG_TILE, D_MODEL), lambda j: (j, 0))],
        out_specs=out_specs, out_shape=out_shapes,
        compiler_params=_params(("arbitrary",), large=True),
    )(h, w_t)
    return {name: o for (name, _, _, _), o in zip(SEGMENTS, outs)}


def _forward_a(x, mem, pre_g, mem_g, w_in, conv_w, conv_b, w_a, b_a, w_x, b_x, lam, sinks, rel_bias):
    S = x.shape[0]
    st = dict(T=min(512, S // 2), tm=min(512, S), bucket=_rel_bucket_map())
    st["h"] = _rms_fwd(x, pre_g, "pre_norm")
    st["memn"] = _rms_fwd(mem, mem_g, "mem_norm")
    seg = st["seg"] = _project(st["h"], w_in)
    st["h_rg"], st["y_rg"] = _rglru_fwd(seg["xr"], seg["g_rg"], conv_w, conv_b, w_a, b_a, w_x, b_x, lam, st["T"])
    st["o_swa"], st["y_swa"] = _swa_fwd(seg["q_s"], seg["kv"], seg["g_swa"], st["bucket"], rel_bias, _sink_column(sinks))
    return st


def _forward_b(st, x, target, post_g, w_memkv, wbr, w_out):
    S = x.shape[0]
    M = st["memn"].shape[0]
    seg = st["seg"]
    st["mkv"] = _matmul(st["memn"], w_memkv, "nn", M, 2 * D_MEM, D_MODEL, M, 512, D_MODEL, BF, "mem_kv")
    st["o_mem"], st["y_mem"] = _mem_fwd(seg["q_m"], st["mkv"], seg["g_mem"])
    st["ys"] = (st["y_rg"], st["y_swa"], st["y_mem"])
    st["merged"] = _merge_fwd(st["ys"], wbr, seg["gl"], st["tm"])
    st["dout"], st["dy"], st["loss"], st["dpost"] = _out_loss(st["merged"], w_out, x, target, post_g, min(256, S))
    return st


def _backward_a1(st, wbr, w_out):
    S = st["h"].shape[0]
    seg, ys, tm = st["seg"], st["ys"], st["tm"]
    st["dw_out"] = _matmul(st["merged"], st["dout"], "tn", D_MODEL, D_MODEL, S, 256, D_MODEL, S, BF, "dw_out", out_blocked="row")
    dgl0, dgl1, dgl2, dp0, dp1, dp2 = _merge_bwd(st["dout"], w_out, ys, wbr, seg["gl"], tm)
    st["dgl"] = (dgl0, dgl1, dgl2)
    dys, dwbr = [], []
    for i, dp in enumerate((dp0, dp1, dp2)):
        dys.append(_matmul(dp, wbr[i], "nn", S, 1024, D_MODEL, tm, 1024, D_MODEL, F32, "dy_br%d" % i))
        dwbr.append(_matmul(ys[i], dp, "tn", 1024, D_MODEL, S, 1024, 256, S, BF, "dw_br%d" % i, out_blocked="col"))
    st["dys"], st["dwbr"] = dys, dwbr
    return st


def _backward_a2(st, mem, w_memkv, conv_w, conv_b, w_a, b_a, w_x, b_x, lam):
    M = mem.shape[0]
    seg, dys = st["seg"], st["dys"]
    st["dq_m"], st["dg_mem"], dmkv = _mem_bwd(seg["q_m"], st["mkv"], seg["g_mem"], st["o_mem"], dys[2])
    dmkv_b = dmkv.astype(BF)
    st["dw_memkv"] = _matmul(st["memn"], dmkv_b, "tn", D_MODEL, 2 * D_MEM, M, 256, 2 * D_MEM, M, BF, "dw_memkv", out_blocked="row")
    dmemn = _matmul(dmkv_b, w_memkv, "nt", M, D_MODEL, 2 * D_MEM, M, 512, 2 * D_MEM, F32, "dmemn")
    st["dmem_g"] = _rms_gain_grad(dmemn, mem, "dmem_gain")
    st["dxr"], st["dg_rg"], st["dw_a"], st["dw_x"], st["dvec"] = _rglru_bwd(
        seg["xr"], seg["g_rg"], st["h_rg"], dys[0], conv_w, conv_b, w_a, b_a, w_x, b_x, lam, st["T"])
    return st


def _backward_b(st, rel_bias, sinks):
    seg = st["seg"]
    dq_s, dg_swa, dkv, dsinks, drel = _swa_bwd(seg["q_s"], seg["kv"], seg["g_swa"], st["o_swa"], st["dys"][1],
                                               st["bucket"], rel_bias, _sink_column(sinks))
    st["dsinks"], st["drel"] = dsinks.reshape(1, SWA_HEADS), drel.reshape(REL_BUCKETS, SWA_HEADS)
    dproj = st["dproj_plain"] = jnp.concatenate([st["dxr"], st["dg_rg"], dq_s, dkv.astype(BF), dg_swa, st["dq_m"], st["dg_mem"],
                                                 *st["dgl"]], axis=1)
    st["dproj"] = jnp.transpose(dproj.reshape(dproj.shape[0], N_DEV, D_IN // N_DEV), (1, 0, 2))
    return st


def _dw_in_half(st, half, dep=None):
    S = st["h"].shape[0]
    return _matmul(st["dproj"], st["h"], "tn", D_IN, D_MODEL // 2, S, D_IN // N_DEV, 512, S, BF, "dw_in%d" % half, b_noff=2 * half,
                   a_blocked=True, out_blocked="row", dep=dep)


def _owner_blocks(a):
    return jnp.swapaxes(a.reshape((4, 2) + a.shape[1:]), 0, 1)


def _local_step(x, mem, target, pre_g, post_g, mem_g, w_in, conv_w, conv_b, w_a, b_a, w_x, b_x, lam, sinks, rel_bias,
                w_memkv, wbr, w_out):
    st = _forward_a(x, mem, pre_g, mem_g, w_in, conv_w, conv_b, w_a, b_a, w_x, b_x, lam, sinks, rel_bias)
    st = _forward_b(st, x, target, post_g, w_memkv, wbr, w_out)
    st = _backward_a1(st, wbr, w_out)
    st = _backward_a2(st, mem, w_memkv, conv_w, conv_b, w_a, b_a, w_x, b_x, lam)
    st = _backward_b(st, rel_bias, sinks)
    st["dw_in"] = [_dw_in_half(st, 0), _dw_in_half(st, 1)]
    st["grad_x"], st["dpre"] = _dh_dx(st["dproj"], w_in, x, st["dy"], pre_g, st["tm"])
    return st


def _pad_rows(a, rows):
    a = a.reshape(-1, 128) if a.shape[-1] % 128 == 0 else jnp.pad(a, ((0, 0), (0, 128 - a.shape[-1])))
    return jnp.pad(a, ((0, rows - a.shape[0]), (0, 0))) if a.shape[0] < rows else a


def kernel(x, mem, pre_norm_g, post_norm_g, mem_norm_g, w_in, conv_w, conv_b, w_rg_a, b_rg_a, w_rg_x, b_rg_x, lru_lambda, swa_sinks, rel_bias, w_mem_kv, w_br_rg, w_br_swa, w_br_mem, w_out, loss_target, m_pre_norm_g, m_post_norm_g, m_mem_norm_g, m_w_in, m_conv_w, m_conv_b, m_w_rg_a, m_b_rg_a, m_w_rg_x, m_b_rg_x, m_lru_lambda, m_swa_sinks, m_rel_bias, m_w_mem_kv, m_w_br_rg, m_w_br_swa, m_w_br_mem, m_w_out, v_pre_norm_g, v_post_norm_g, v_mem_norm_g, v_w_in, v_conv_w, v_conv_b, v_w_rg_a, v_b_rg_a, v_w_rg_x, v_b_rg_x, v_lru_lambda, v_swa_sinks, v_rel_bias, v_w_mem_kv, v_w_br_rg, v_w_br_swa, v_w_br_mem, v_w_out):
    cx, cy, cc = lax.axis_index("x"), lax.axis_index("y"), lax.axis_index("c")
    me = 4 * cx + 2 * cy + cc
    chip = 2 * cx + cy
    core = jnp.reshape(cc, (1,)).astype(jnp.int32)
    x0, mem0 = x[0], mem[0]
    w_a_b, w_x_b = w_rg_a[0].astype(BF), w_rg_x[0].astype(BF)

    def landing(own, slot, slots):
        return lax.dynamic_update_slice(lax.empty((slots,) + own.shape, own.dtype), own[None], (slot,) + (0,) * own.ndim)


    def swap_start(parts, tag):
        return _exchange_start(parts, [lax.empty(p.shape[1:], p.dtype) for p in parts], _plan_swap(len(parts)), "swap_%s_start" % tag)

    def scatter_start(swap, after, tag, prefill=()):
        s_send, s_recv, parts, got, _ = swap
        got = _exchange_wait(s_send, s_recv, parts, got, _plan_swap(len(parts)), after, "swap_%s_wait" % tag)
        sums = [_pair_sum(p, g, core, "scatter_%s_sum%d" % (tag, i)) for i, (p, g) in enumerate(zip(parts, got))]
        lands = [landing(lax.dynamic_index_in_dim(s, chip, 0, keepdims=False), chip, 4) if i in prefill
                 else lax.empty(s.shape, s.dtype) for i, s in enumerate(sums)]
        return _exchange_start(sums, lands, _plan_scatter(len(sums)), "scatter_%s_start" % tag)

    def zero_after(a):
        return jnp.minimum(jnp.abs(a.reshape(-1)[0].astype(F32)), 0.0)

    g_in, g_cw = _all_gather_relayed([jnp.transpose(w_in[0]).astype(BF), conv_w[0]], [True, False], "gather_w_in")
    w_in_f = g_in.reshape(D_IN, D_MODEL)
    conv_w_f = jnp.transpose(g_cw, (1, 0, 2)).reshape(CONV_W, D_RNN)

    after_first = zero_after(g_cw).astype(BF)
    rest = [w.astype(BF) + after_first for w in (w_mem_kv[0], jnp.transpose(w_br_rg[0]), jnp.transpose(w_br_swa[0]),
                                                 jnp.transpose(w_br_mem[0]), w_out[0])]
    kinds = ["lead"] * len(rest)
    plan_g = _plan_gather(kinds)
    zones = _place_own([lax.empty((N_DEV,) + w.shape, w.dtype) for w in rest], rest, jnp.reshape(me, (1,)).astype(jnp.int32), "gather_rest_own")
    g_send, g_recv, g_src, g_land, g_token = _exchange_start(rest, zones, plan_g, "gather_rest_start")
    st = _forward_a(x0, mem0, pre_norm_g + g_token[0:1, 0:1], mem_norm_g, w_in_f, conv_w_f, conv_b, w_a_b, b_rg_a, w_x_b, b_rg_x,
                    lru_lambda, swa_sinks, rel_bias)
    g_land = _exchange_wait(g_send, g_recv, g_src, g_land, plan_g, st["y_swa"], "gather_rest_wait")
    g_land = _forward_to_sibling(g_land, kinds, "gather_rest_forward")
    w_memkv_f = g_land[0].reshape(D_MODEL, 2 * D_MEM)
    wbr = tuple(g_land[i].reshape(D_MODEL, D_RNN) for i in (1, 2, 3))
    w_out_f = g_land[4].reshape(D_MODEL, D_MODEL)

    st = _forward_b(st, x0, loss_target[0], post_norm_g, w_memkv_f, wbr, w_out_f)
    st = _backward_a1(st, wbr, w_out_f)
    parts_a = [st["dw_out"], st["dwbr"][0], st["dwbr"][1], st["dwbr"][2]]
    plan_a = _plan_scatter(len(parts_a))
    swap_a = swap_start(parts_a, "a")
    st = _backward_a2(st, mem0, w_memkv_f, conv_w_f, conv_b + swap_a[4][0:1, 0:1], w_a_b, b_rg_a, w_x_b, b_rg_x, lru_lambda)
    a_send, a_recv, a_src, a_land, a_token = scatter_start(swap_a, st["dxr"], "a")
    parts_c = [st["dw_memkv"], _owner_blocks(st["dw_a"]), _owner_blocks(st["dw_x"])]
    plan_c = _plan_scatter(len(parts_c))
    swap_c = swap_start(parts_c, "c")

    st = _backward_b(st, rel_bias, swa_sinks + swap_c[4][0:1, 0:1] + a_token[0:1, 0:1])
    c_send, c_recv, c_src, c_land, c_token = scatter_start(swap_c, st["dproj_plain"], "c", prefill=(1, 2))
    plan_b = _plan_scatter(1)

    def dw_in_parts(half, dep):
        dwh = _dw_in_half(st, half, dep)
        return dwh, [dwh]

    dw0, parts_b0 = dw_in_parts(0, c_token)
    swap_b0 = swap_start(parts_b0, "b0")
    a_land = _exchange_wait(a_send, a_recv, a_src, a_land, plan_a, swap_b0[4], "scatter_a_wait")
    big = [None] * 6

    chip1 = jnp.reshape(chip, (1,)).astype(jnp.int32)

    def adamw_big(j, land, own, wt, mt, vt):
        big[j] = _adamw(land, own, chip1, wt, mt, vt, "adamw_big%d" % j)

    adamw_big(5, a_land[0], a_src[0], w_out, m_w_out, v_w_out)
    adamw_big(2, a_land[1], a_src[1], w_br_rg, m_w_br_rg, v_w_br_rg)
    adamw_big(3, a_land[2], a_src[2], w_br_swa, m_w_br_swa, v_w_br_swa)
    halves = [scatter_start(swap_b0, big[3][1], "b0")]
    dw1, parts_b1 = dw_in_parts(1, halves[0][4])
    swap_b1 = swap_start(parts_b1, "b1")
    c_land = _exchange_wait(c_send, c_recv, c_src, c_land, plan_c, swap_b1[4], "scatter_c_wait")
    g_wa_blk = _sum_parts(c_land[1], "sum_w_rg_a")
    g_wx_blk = _sum_parts(c_land[2], "sum_w_rg_x")
    adamw_big(4, a_land[3], a_src[3], w_br_mem, m_w_br_mem, v_w_br_mem)
    adamw_big(1, c_land[0], c_src[0], w_mem_kv, m_w_mem_kv, v_w_mem_kv)
    halves.append(scatter_start(swap_b1, big[1][1], "b1"))
    grad_x, dpre = _dh_dx(st["dproj"], w_in_f, x0, st["dy"], pre_norm_g + halves[1][4][0:1, 0:1], st["tm"])
    pack = jnp.concatenate([dpre.reshape(16, 128), st["dpost"].reshape(16, 128), st["dmem_g"].reshape(16, 128),
                            st["dvec"].reshape(64, 128), _pad_rows(st["dsinks"], 8), _pad_rows(st["drel"], 32), g_wa_blk, g_wx_blk,
                            st["loss"]], axis=0)
    plan_s = _plan_everyone(1)
    s_send, s_recv, s_src, s_land, s_token = _exchange_start([pack], [landing(pack, me, N_DEV)], plan_s, "gather_small_start")
    after, b_lands, b_sums = s_token, [], []
    for half, (b_send, b_recv, b_src, b_land, _) in enumerate(halves):
        b_lands.append(_exchange_wait(b_send, b_recv, b_src, b_land, plan_b, after, "scatter_b%d_wait" % half)[0])
        b_sums.append(b_src[0])
        after = b_lands[-1]
    swap_last = lambda a: jnp.transpose(a, (0, 2, 1))
    big[0] = [swap_last(a) for a in _adamw(b_lands, b_sums, chip1, swap_last(w_in), swap_last(m_w_in), swap_last(v_w_in), "adamw_big0")]
    gathered = _exchange_wait(s_send, s_recv, s_src, s_land, plan_s, big[0][1], "gather_small_wait")[0]
    gs = _sum_parts(gathered, "sum_small")
    loss_total = gs[408, 0]
    g_pre, g_post, g_memg = gs[0:16].reshape(1, D_MODEL), gs[16:32].reshape(1, D_MODEL), gs[32:48].reshape(1, D_MODEL)
    gvec = gs[48:112].reshape(8, D_RNN)
    g_conv_w = lax.dynamic_slice(gvec[0:CONV_W], (0, me * RNN_BLOCK), (CONV_W, RNN_BLOCK))
    g_conv_b, g_b_a, g_b_x, g_lam = gvec[4:5], gvec[5:6], gvec[6:7], gvec[7:8]
    g_sinks = gs[112:113, :SWA_HEADS]
    g_rel = gs[120:152, :SWA_HEADS]
    g_w_a = gathered[:, 152:280]
    g_w_x = gathered[:, 280:408]

    g_small = (g_pre, g_post, g_memg, g_conv_b, g_b_a, g_b_x, g_lam, g_w_a, g_w_x, g_sinks, g_rel, g_conv_w)
    w_small = (pre_norm_g, post_norm_g, mem_norm_g, conv_b, b_rg_a, b_rg_x, lru_lambda, w_rg_a, w_rg_x, swa_sinks, rel_bias, conv_w)
    m_small = (m_pre_norm_g, m_post_norm_g, m_mem_norm_g, m_conv_b, m_b_rg_a, m_b_rg_x, m_lru_lambda, m_w_rg_a, m_w_rg_x, m_swa_sinks, m_rel_bias, m_conv_w)
    v_small = (v_pre_norm_g, v_post_norm_g, v_mem_norm_g, v_conv_b, v_b_rg_a, v_b_rg_x, v_lru_lambda, v_w_rg_a, v_w_rg_x, v_swa_sinks, v_rel_bias, v_conv_w)
    sm = _adamw_small(g_small, w_small, m_small, v_small)


    def leaves(k):
        s = sm[k]
        return [s[0], s[1], s[2], big[0][k], s[11], s[3], s[7], s[4], s[8], s[5], s[6], s[9], s[10],
                big[1][k], big[2][k], big[3][k], big[4][k], big[5][k]]

    return (loss_total, grad_x[None], *leaves(0), *leaves(1), *leaves(2), *leaves(3))
```

```python
import math

import jax
import jax.numpy as jnp
import numpy as np
from jax import lax
from jax.experimental import pallas as pl
from jax.experimental.pallas import tpu as pltpu

F32, BF = jnp.float32, jnp.bfloat16
MESH = pl.DeviceIdType.MESH
N_DEV = 8

D_MODEL = 2048
D_RNN = 1024
RNN_BLOCKS = 8
RNN_BLOCK = 128
CONV_W = 4
LRU_C = 8.0
SWA_HEADS = 16
SWA_KV_HEADS = 2
SWA_HD = 64
WINDOW = 128
MEM_HEADS = 4
MEM_HD = 256
D_MEM = 1024
REL_BUCKETS = 32
REL_MAX_DIST = 128
EPS = 1e-6
NEG_INF = -1e30
D_IN = 12544
SEGMENTS = (("xr", 0, 1024, F32), ("g_rg", 1024, 1024, F32), ("q_s", 2048, 1024, BF), ("kv", 3072, 256, BF),
            ("g_swa", 3328, 1024, F32), ("q_m", 4352, 1024, BF), ("g_mem", 5376, 1024, F32), ("gl", 6400, 6144, F32))
SEG_TILE = 256

ADAM_LR, ADAM_B1, ADAM_B2, ADAM_EPS, ADAM_WD, ADAM_STEP = 0.001, 0.9, 0.999, 1e-08, 0.01, 10

NN = (((1,), (0,)), ((), ()))
NT = (((1,), (1,)), ((), ()))
TN = (((0,), (0,)), ((), ()))
MIB = 2 ** 20


def _dot(a, b, dn):
    return lax.dot_general(a, b, dn, preferred_element_type=F32)


VMEM_LIMIT_MIB = 48
VMEM_LIMIT_LARGE_MIB = 56


def _params(sem=None, large=False):
    return pltpu.CompilerParams(dimension_semantics=sem, vmem_limit_bytes=(VMEM_LIMIT_LARGE_MIB if large else VMEM_LIMIT_MIB) * MIB)


def _sigmoid(z):
    return 1.0 / (1.0 + jnp.exp(-z))


def _softplus(z):
    return jnp.maximum(z, 0.0) + jnp.log(1.0 + jnp.exp(-jnp.abs(z)))


def _expm1(z):
    p = z * (1.0 + z * (0.5 + z * (1.0 / 6 + z * (1.0 / 24 + z * (1.0 / 120 + z * (1.0 / 720 + z * (1.0 / 5040 + z / 40320)))))))
    return jnp.where(jnp.abs(z) < 0.3, p, jnp.exp(z) - 1.0)


def _flat(p):
    return 4 * p[0] + 2 * p[1] + p[2]


def _all_gather_relayed(arrs, relay, name):
    n = len(arrs)
    K = 9

    def body(*refs):
        ins, outs = refs[:n], refs[n:2 * n]
        send_sems, recv_sems, local_sems = refs[2 * n:]
        x, y, c = lax.axis_index("x"), lax.axis_index("y"), lax.axis_index("c")
        me, sib = (x, y, c), (x, y, 1 - c)
        xn, yn, dg = (1 - x, y, c), (x, 1 - y, c), (1 - x, 1 - y, c)

        def other(p):
            return (p[0], p[1], 1 - p[2])

        def rows(a, half):
            h = arrs[a].shape[0] // 2
            return pl.ds(half * h, h)

        def copy(a, k, block, to, half=None, src=None):
            dst = outs[a].at[_flat(block)]
            if half is not None:
                dst = dst.at[rows(a, half)]
            return pltpu.make_async_remote_copy(src_ref=dst if src is None else src, dst_ref=dst,
                                                send_sem=send_sems.at[a * K + k], recv_sem=recv_sems.at[a * K + k],
                                                device_id=to, device_id_type=MESH)

        mine = [pltpu.make_async_copy(ins[a], outs[a].at[_flat(me)], local_sems.at[a]) for a in range(n)]
        for cp in mine:
            cp.start()
        sends = []

        def start(cp):
            cp.start()
            sends.append(cp)

        for a in range(n):
            start(copy(a, 1, me, xn, src=ins[a]))
            start(copy(a, 2, me, yn, src=ins[a]))
            if not relay[a]:
                start(copy(a, 3, me, dg, src=ins[a]))
            start(copy(a, 0, me, sib, src=ins[a]))
        for a in range(n):
            copy(a, 1, xn, me).wait_recv()
            if relay[a]:
                start(copy(a, 3, xn, yn, half=0))
            start(copy(a, 5, xn, sib))
        for a in range(n):
            copy(a, 2, yn, me).wait_recv()
            if relay[a]:
                start(copy(a, 4, yn, xn, half=1))
            start(copy(a, 6, yn, sib))
        for a in range(n):
            if relay[a]:
                copy(a, 3, dg, me, half=0).wait_recv()
                start(copy(a, 7, dg, sib, half=0))
                copy(a, 4, dg, me, half=1).wait_recv()
                start(copy(a, 8, dg, sib, half=1))
            else:
                copy(a, 3, dg, me).wait_recv()
                start(copy(a, 7, dg, sib))
        for a in range(n):
            copy(a, 0, sib, me).wait_recv()
            copy(a, 5, other(xn), me).wait_recv()
            copy(a, 6, other(yn), me).wait_recv()
            if relay[a]:
                copy(a, 7, other(dg), me, half=0).wait_recv()
                copy(a, 8, other(dg), me, half=1).wait_recv()
            else:
                copy(a, 7, other(dg), me).wait_recv()
        for cp in sends:
            cp.wait_send()
        for cp in mine:
            cp.wait()

    any_spec = pl.BlockSpec(memory_space=pl.ANY)
    return pl.pallas_call(
        body, name=name,
        out_shape=[jax.ShapeDtypeStruct((N_DEV,) + a.shape, a.dtype) for a in arrs],
        in_specs=[any_spec] * n, out_specs=[any_spec] * n,
        scratch_shapes=[pltpu.SemaphoreType.DMA((K * n,)), pltpu.SemaphoreType.DMA((K * n,)), pltpu.SemaphoreType.DMA((n,))],
    )(*arrs)


def _chip_peers(x, y):
    return [(1 - x, y), (x, 1 - y), (1 - x, 1 - y)]


def _chip(p):
    return 2 * p[0] + p[1]


def _plan_gather(kinds):
    def plan(x, y, c):
        out = []
        for a, kind in enumerate(kinds):
            for peer in [(x, y, 1 - c)] + [(*ch, c) for ch in _chip_peers(x, y)]:
                out.append((a, None, (kind, _flat((x, y, c))), peer, (kind, _flat(peer))))
        return out
    return plan


def _plan_everyone(n):
    def plan(x, y, c):
        out = []
        for a in range(n):
            for r in range(1, N_DEV):
                peer = (1 - x if r & 4 else x, 1 - y if r & 2 else y, 1 - c if r & 1 else c)
                out.append((a, None, ("lead", _flat((x, y, c))), peer, ("lead", _flat(peer))))
        return out
    return plan


def _plan_swap(n):
    def plan(x, y, c):
        return [(a, 1 - c, ("all", 0), (x, y, 1 - c), ("all", 0)) for a in range(n)]
    return plan


def _slot(ref, where):
    kind, k = where
    if kind == "all":
        return ref
    if kind == "lead":
        return ref.at[k]
    return ref.at[:, pl.ds(pl.multiple_of(k * 256, 256), 256)]


def _plan_scatter(n):
    def plan(x, y, c):
        out = []
        for a in range(n):
            for ch in _chip_peers(x, y):
                out.append((a, _chip(ch), ("lead", _chip((x, y))), (*ch, c), ("lead", _chip(ch))))
        return out
    return plan


HBM_SPEC = pl.BlockSpec(memory_space=pltpu.HBM)
SEM_SPEC = pl.BlockSpec(memory_space=pltpu.SEMAPHORE)


def _in_hbm(a):
    return pltpu.with_memory_space_constraint(a, pltpu.HBM)


def _exchange_start(srcs, lands, plan, name):
    n = len(srcs)
    count = len(plan(0, 0, 0))

    def body(*refs):
        src_refs, land_refs = refs[:n], refs[n:2 * n]
        send_sems, recv_sems = refs[2 * n], refs[2 * n + 1]
        token = refs[-1]
        x, y, c = lax.axis_index("x"), lax.axis_index("y"), lax.axis_index("c")
        for k, (a, si, di, peer, _) in enumerate(plan(x, y, c)):
            src = src_refs[a] if si is None else src_refs[a].at[si]
            pltpu.make_async_remote_copy(src_ref=src, dst_ref=_slot(land_refs[a], di), send_sem=send_sems.at[k],
                                         recv_sem=recv_sems.at[k], device_id=peer, device_id_type=MESH).start()
        token[...] = jnp.zeros_like(token)

    out = pl.pallas_call(
        body, name=name,
        out_shape=(pltpu.SemaphoreType.DMA((count,)), pltpu.SemaphoreType.DMA((count,)),
                   *[pltpu.HBM(a.shape, a.dtype) for a in lands], jax.ShapeDtypeStruct((8, 128), F32)),
        in_specs=[HBM_SPEC] * (2 * n),
        out_specs=(SEM_SPEC, SEM_SPEC, *([HBM_SPEC] * n), pl.BlockSpec(memory_space=pltpu.VMEM)),
        input_output_aliases={n + i: 2 + i for i in range(n)},
        compiler_params=pltpu.CompilerParams(has_side_effects=pltpu.SideEffectType.DATAFLOW_SIDE_EFFECTING),
    )(*[_in_hbm(a) for a in srcs], *[_in_hbm(a) for a in lands])
    return out[0], out[1], list(srcs), list(out[2:2 + n]), out[-1]


def _exchange_wait(send_sems, recv_sems, srcs, lands, plan, after, name):
    n = len(srcs)

    def body(*refs):
        src_refs, land_refs = refs[:n], refs[n:2 * n]
        send_sems, recv_sems = refs[2 * n], refs[2 * n + 1]
        x, y, c = lax.axis_index("x"), lax.axis_index("y"), lax.axis_index("c")
        for k, (a, si, _, peer, ri) in enumerate(plan(x, y, c)):
            src = src_refs[a] if si is None else src_refs[a].at[si]
            cp = pltpu.make_async_remote_copy(src_ref=src, dst_ref=_slot(land_refs[a], ri), send_sem=send_sems.at[k],
                                              recv_sem=recv_sems.at[k], device_id=peer, device_id_type=MESH)
            cp.wait_send()
            cp.wait_recv()

    out = pl.pallas_call(
        body, name=name,
        out_shape=tuple(pltpu.HBM(a.shape, a.dtype) for a in lands),
        in_specs=[HBM_SPEC] * (2 * n) + [SEM_SPEC, SEM_SPEC, pl.BlockSpec(memory_space=pl.ANY)],
        out_specs=tuple([HBM_SPEC] * n),
        input_output_aliases={n + i: i for i in range(n)},
        compiler_params=pltpu.CompilerParams(has_side_effects=pltpu.SideEffectType.DATAFLOW_SIDE_EFFECTING),
    )(*[_in_hbm(a) for a in srcs], *lands, send_sems, recv_sems, after)
    return list(out)


def _forward_to_sibling(lands, kinds, name):
    n = len(lands)

    def body(*refs):
        in_refs, out_refs = refs[:n], refs[n:2 * n]
        send_sems, recv_sems = refs[2 * n:]
        x, y, c = lax.axis_index("x"), lax.axis_index("y"), lax.axis_index("c")
        sibling = (x, y, 1 - c)

        def copy(a, j, slot):
            return pltpu.make_async_remote_copy(src_ref=_slot(in_refs[a], (kinds[a], slot)), dst_ref=_slot(out_refs[a], (kinds[a], slot)),
                                                send_sem=send_sems.at[a * 3 + j], recv_sem=recv_sems.at[a * 3 + j],
                                                device_id=sibling, device_id_type=MESH)

        sends = [copy(a, j, _flat((*ch, c))) for a in range(n) for j, ch in enumerate(_chip_peers(x, y))]
        for cp in sends:
            cp.start()
        for a in range(n):
            for j, ch in enumerate(_chip_peers(x, y)):
                copy(a, j, _flat((*ch, 1 - c))).wait_recv()
        for cp in sends:
            cp.wait_send()

    any_spec = pl.BlockSpec(memory_space=pl.ANY)
    return pl.pallas_call(
        body, name=name, out_shape=[jax.ShapeDtypeStruct(a.shape, a.dtype) for a in lands],
        in_specs=[any_spec] * n, out_specs=[any_spec] * n, input_output_aliases={a: a for a in range(n)},
        scratch_shapes=[pltpu.SemaphoreType.DMA((3 * n,)), pltpu.SemaphoreType.DMA((3 * n,))],
    )(*lands)


def _place_own(zones, owns, slot, name):
    n = len(zones)

    def body(slot_ref, *refs):
        for a in range(n):
            refs[2 * n + a][...] = refs[a][...]

    return pl.pallas_call(
        body, name=name,
        grid_spec=pltpu.PrefetchScalarGridSpec(
            num_scalar_prefetch=1, grid=(1,),
            in_specs=[pl.BlockSpec(o.shape, lambda i, s_ref: (0, 0)) for o in owns] + [pl.BlockSpec(memory_space=pl.ANY)] * n,
            out_specs=[pl.BlockSpec((None,) + o.shape, lambda i, s_ref: (s_ref[0], 0, 0)) for o in owns]),
        out_shape=[jax.ShapeDtypeStruct(z.shape, z.dtype) for z in zones],
        input_output_aliases={1 + n + a: a for a in range(n)},
        compiler_params=_params(("arbitrary",)),
    )(slot, *owns, *zones)


def _pair_sum(parts, got, core, name):
    _, _, R, C = parts.shape
    tr = 256 if R % 256 == 0 else R

    def body(c_ref, p_ref, g_ref, o_ref):
        o_ref[...] = (p_ref[...].astype(F32) + g_ref[...].astype(F32)).astype(o_ref.dtype)

    return pl.pallas_call(
        body, name=name,
        grid_spec=pltpu.PrefetchScalarGridSpec(
            num_scalar_prefetch=1, grid=(4, R // tr),
            in_specs=[pl.BlockSpec((None, None, tr, C), lambda j, i, c_ref: (c_ref[0], j, i, 0)),
                      pl.BlockSpec((None, tr, C), lambda j, i, c_ref: (j, i, 0))],
            out_specs=pl.BlockSpec((None, tr, C), lambda j, i, c_ref: (j, i, 0))),
        out_shape=jax.ShapeDtypeStruct((4, R, C), parts.dtype),
        compiler_params=_params(("parallel", "parallel")),
    )(core, parts, got)


def _matmul(a, b, mode, M, N, K, tm, tn, tk, out_dtype, name, b_noff=0, a_moff=0, a_blocked=False, b_blocked=False,
            out_blocked=None, dep=None):
    nm, nn, nk = M // tm, N // tn, K // tk
    if mode == "nn":
        a_spec = pl.BlockSpec((tm, tk), lambda j, i, k: (i, k))
        b_spec = pl.BlockSpec((tk, tn), lambda j, i, k: (k, j + b_noff))
        dn = NN
    elif mode == "nt":
        a_spec = pl.BlockSpec((tm, tk), lambda j, i, k: (i, k))
        if b_blocked:
            b_spec = pl.BlockSpec((None, tn, tk), lambda j, i, k: (k, j, 0))
        else:
            b_spec = pl.BlockSpec((tn, tk), lambda j, i, k: (j + b_noff, k))
        dn = NT
    else:
        if a_blocked:
            a_spec = pl.BlockSpec((None, tk, tm), lambda j, i, k: (i, k, 0))
        else:
            a_spec = pl.BlockSpec((tk, tm), lambda j, i, k: (k, i + a_moff))
        if b_blocked:
            b_spec = pl.BlockSpec((None, tk, tn), lambda j, i, k: (j, k, 0))
        else:
            b_spec = pl.BlockSpec((tk, tn), lambda j, i, k: (k, j + b_noff))
        dn = TN
    if out_blocked == "col":
        out_shape = jax.ShapeDtypeStruct((2, 4, M, tn), out_dtype)
        out_spec = pl.BlockSpec((None, None, tm, tn), lambda j, i, k: (j % 2, j // 2, i, 0))
    elif out_blocked == "row":
        out_shape = jax.ShapeDtypeStruct((2, 4, tm, N), out_dtype)
        out_spec = pl.BlockSpec((None, None, tm, tn), (lambda j, i, k: (i % 2, i // 2, 0, j)))
    elif out_blocked == "third":
        out_shape = jax.ShapeDtypeStruct((3, M, N // 3), out_dtype)
        out_spec = pl.BlockSpec((None, tm, tn), lambda j, i, k: (j // (nn // 3), i, j % (nn // 3)))
    else:
        out_shape = jax.ShapeDtypeStruct((M, N), out_dtype)
        out_spec = pl.BlockSpec((tm, tn), lambda j, i, k: (i, j))

    n_extra = int(dep is not None)

    def body(a_ref, b_ref, *rest):
        o_ref, scratch = rest[n_extra], rest[n_extra + 1:]
        if nk == 1:
            o_ref[...] = _dot(a_ref[...], b_ref[...], dn).astype(out_dtype)
        else:
            acc_ref, = scratch
            k = pl.program_id(2)

            @pl.when(k == 0)
            def _():
                acc_ref[...] = jnp.zeros_like(acc_ref)

            acc_ref[...] += _dot(a_ref[...], b_ref[...], dn)

            @pl.when(k == nk - 1)
            def _():
                o_ref[...] = acc_ref[...].astype(out_dtype)

    return pl.pallas_call(
        body, name=name, grid=(nn, nm, nk),
        in_specs=[a_spec, b_spec] + ([] if dep is None else [pl.BlockSpec((8, 128), lambda j, i, k: (0, 0))]),
        out_specs=out_spec, out_shape=out_shape,
        scratch_shapes=[] if nk == 1 else [pltpu.VMEM((tm, tn), F32)],
        compiler_params=_params(("parallel", "parallel", "arbitrary")),
    )(a, b, *([] if dep is None else [dep]))


def _rms_fwd(x, g, name):
    R, Dm = x.shape
    tr = min(R, 256)

    def body(x_ref, g_ref, h_ref):
        xv = x_ref[...]
        r = lax.rsqrt(jnp.mean(xv * xv, axis=-1, keepdims=True) + EPS)
        h_ref[...] = (xv * r * g_ref[...]).astype(BF)

    return pl.pallas_call(
        body, name=name, grid=(R // tr,),
        in_specs=[pl.BlockSpec((tr, Dm), lambda i: (i, 0)), pl.BlockSpec((1, Dm), lambda i: (0, 0))],
        out_specs=pl.BlockSpec((tr, Dm), lambda i: (i, 0)), out_shape=jax.ShapeDtypeStruct((R, Dm), BF),
        compiler_params=_params(("parallel",)),
    )(x, g)


def _rms_gain_grad(dn, x, name):
    R, Dm = x.shape

    def body(dn_ref, x_ref, o_ref):
        xv = x_ref[...]
        r = lax.rsqrt(jnp.mean(xv * xv, axis=-1, keepdims=True) + EPS)
        o_ref[...] = jnp.sum(dn_ref[...] * xv * r, axis=0, keepdims=True)

    return pl.pallas_call(
        body, name=name, out_shape=jax.ShapeDtypeStruct((1, Dm), F32),
        compiler_params=_params(),
    )(dn, x)


def _shift_down(v, k, head8, row, T):
    if k == 0:
        return v
    r = pltpu.roll(v, k, 0)
    hr = pltpu.roll(head8, k, 0)
    top = jnp.where(row[:8] < k, hr, r[:8])
    return jnp.concatenate([top, r[8:]], axis=0)


def _shift_up(v, k, tail8, row, T):
    if k == 0:
        return v
    r = pltpu.roll(v, T - k, 0)
    tr = pltpu.roll(tail8, 8 - k, 0)
    bot = jnp.where(row[:8] >= 8 - k, tr, r[T - 8:])
    return jnp.concatenate([r[:T - 8], bot], axis=0)


def _rglru_gates(u, head8, grow, row, T, cw_ref, cb_ref, wa_ref, ba_ref, wx_ref, bx_ref, lam_ref):
    us = [_shift_down(u, k, head8, row, T) for k in range(CONV_W)]
    acc = us[0] * cw_ref[0:1, :]
    for k in range(1, CONV_W):
        acc = acc + us[k] * cw_ref[k:k + 1, :]
    conv = cb_ref[...] + acc
    cbf = conv.astype(BF)
    r_ = _sigmoid(_dot(cbf, wa_ref[0], NN) + ba_ref[...])
    i_ = _sigmoid(_dot(cbf, wx_ref[0], NN) + bx_ref[...])
    sp = _softplus(-lam_ref[...])
    la = -LRU_C * r_ * sp
    a = jnp.exp(la)
    mult_raw = jnp.sqrt(-_expm1(2.0 * la))
    mult = jnp.where(grow == 0, 1.0, mult_raw)
    return us, conv, cbf, r_, i_, sp, a, mult_raw, mult


def _rglru_specs(T, nt, rev):
    tmap = (lambda n, t: (nt - 1 - t, n)) if rev else (lambda n, t: (t, n))
    hmap = ((lambda n, t: (jnp.maximum((nt - 1 - t) * (T // 8) - 1, 0), n)) if rev
            else (lambda n, t: (jnp.maximum(t * (T // 8) - 1, 0), n)))
    tile = pl.BlockSpec((T, RNN_BLOCK), tmap)
    halo = pl.BlockSpec((8, RNN_BLOCK), hmap)
    vec = pl.BlockSpec((1, RNN_BLOCK), lambda n, t: (0, n))
    cw = pl.BlockSpec((CONV_W, RNN_BLOCK), lambda n, t: (0, n))
    wblk = pl.BlockSpec((1, RNN_BLOCK, RNN_BLOCK), lambda n, t: (n, 0, 0))
    return tile, halo, vec, cw, wblk


def _rglru_fwd(xr, g, cw, cb, wa, ba, wx, bx, lam, T):
    S = xr.shape[0]
    nt = S // T

    def body(u_ref, uh_ref, g_ref, cw_ref, cb_ref, wa_ref, ba_ref, wx_ref, bx_ref, lam_ref, h_ref, y_ref, carry):
        t = pl.program_id(1)

        @pl.when(t == 0)
        def _():
            carry[...] = jnp.zeros_like(carry)

        row = lax.broadcasted_iota(jnp.int32, (T, RNN_BLOCK), 0)
        grow = row + t * T
        head8 = jnp.where(t > 0, uh_ref[...], 0.0)
        _, conv, _, _, i_, _, a, _, mult = _rglru_gates(u_ref[...], head8, grow, row, T, cw_ref, cb_ref, wa_ref, ba_ref,
                                                         wx_ref, bx_ref, lam_ref)
        b = mult * i_ * conv
        s = 1
        while s < T:
            keep = row >= s
            a_s = jnp.where(keep, pltpu.roll(a, s, 0), 1.0)
            b_s = jnp.where(keep, pltpu.roll(b, s, 0), 0.0)
            b = a * b_s + b
            a = a * a_s
            s *= 2
        h = b + a * carry[0:1, :]
        carry[...] = jnp.broadcast_to(h[T - 1:T, :], carry.shape)
        h_ref[...] = h
        gv = g_ref[...]
        y_ref[...] = (h * (gv * _sigmoid(gv))).astype(BF)

    tile, halo, vec, cwspec, wblk = _rglru_specs(T, nt, False)
    return pl.pallas_call(
        body, name="rglru_fwd", grid=(RNN_BLOCKS, nt),
        in_specs=[tile, halo, tile, cwspec, vec, wblk, vec, wblk, vec, vec],
        out_specs=[tile, tile],
        out_shape=[jax.ShapeDtypeStruct((S, D_RNN), F32), jax.ShapeDtypeStruct((S, D_RNN), BF)],
        scratch_shapes=[pltpu.VMEM((8, RNN_BLOCK), F32)],
        compiler_params=_params(("parallel", "arbitrary")),
    )(xr, xr, g, cw, cb, wa, ba, wx, bx, lam)


def _rglru_bwd(xr, g, h, dy, cw, cb, wa, ba, wx, bx, lam, T):
    S = xr.shape[0]
    nt = S // T

    def body(u_ref, uh_ref, g_ref, h_ref, hh_ref, dy_ref, cw_ref, cb_ref, wa_ref, ba_ref, wx_ref, bx_ref, lam_ref,
             du_ref, dg_ref, dwa_ref, dwx_ref, dvec_ref, c_dhh, c_a, c_dconv):
        t = pl.program_id(1)
        tt = nt - 1 - t

        @pl.when(t == 0)
        def _():
            c_dhh[...] = jnp.zeros_like(c_dhh)
            c_a[...] = jnp.zeros_like(c_a)
            c_dconv[...] = jnp.zeros_like(c_dconv)
            dwa_ref[...] = jnp.zeros_like(dwa_ref)
            dwx_ref[...] = jnp.zeros_like(dwx_ref)
            dvec_ref[...] = jnp.zeros_like(dvec_ref)

        row = lax.broadcasted_iota(jnp.int32, (T, RNN_BLOCK), 0)
        row8 = row[:8]
        grow = row + tt * T
        head8 = jnp.where(tt > 0, uh_ref[...], 0.0)
        us, conv, cbf, r_, i_, sp, a, mult_raw, mult = _rglru_gates(
            u_ref[...], head8, grow, row, T, cw_ref, cb_ref, wa_ref, ba_ref, wx_ref, bx_ref, lam_ref)
        hv = h_ref[...]
        hprev = _shift_down(hv, 1, jnp.where(tt > 0, hh_ref[...], 0.0), row, T)
        gv = g_ref[...]
        sg = _sigmoid(gv)
        dyv = dy_ref[...]
        dg_ref[...] = (dyv * hv * (sg * (1.0 + gv * (1.0 - sg)))).astype(BF)
        d = dyv * (gv * sg)
        A = _shift_up(a, 1, c_a[...], row, T)
        s = 1
        while s < T:
            keep = row < T - s
            A_s = jnp.where(keep, pltpu.roll(A, T - s, 0), 1.0)
            d_s = jnp.where(keep, pltpu.roll(d, T - s, 0), 0.0)
            d = A * d_s + d
            A = A * A_s
            s *= 2
        dhh = d + A * c_dhh[0:1, :]
        da = dhh * hprev
        dconv = dhh * mult * i_
        di = dhh * mult * conv
        dmult = dhh * i_ * conv
        dla = da * a - jnp.where(grow == 0, 0.0, dmult * (a * a) / mult_raw)
        dr = dla * (-LRU_C * sp)
        dsp = jnp.sum(dla * (-LRU_C * r_), axis=0, keepdims=True)
        dza = dr * r_ * (1.0 - r_)
        dzx = di * i_ * (1.0 - i_)
        dza_b, dzx_b = dza.astype(BF), dzx.astype(BF)
        dconv = dconv + _dot(dza_b, wa_ref[0], NT) + _dot(dzx_b, wx_ref[0], NT)
        dwa_ref[0] += _dot(cbf, dza_b, TN)
        dwx_ref[0] += _dot(cbf, dzx_b, TN)
        lam = lam_ref[...]
        rows = [jnp.sum(dconv * us[k], axis=0, keepdims=True) for k in range(CONV_W)]
        rows += [jnp.sum(dconv, axis=0, keepdims=True), jnp.sum(dza, axis=0, keepdims=True),
                 jnp.sum(dzx, axis=0, keepdims=True), dsp * (-_sigmoid(-lam))]
        upd = jnp.zeros((8, RNN_BLOCK), F32)
        for j, rv in enumerate(rows):
            upd = upd + jnp.where(row8 == j, rv, 0.0)
        dvec_ref[...] += upd
        tail8 = c_dconv[...]
        du = dconv * cw_ref[0:1, :]
        for k in range(1, CONV_W):
            du = du + _shift_up(dconv, k, tail8, row, T) * cw_ref[k:k + 1, :]
        du_ref[...] = du.astype(BF)
        c_dhh[...] = jnp.broadcast_to(dhh[0:1, :], c_dhh.shape)
        c_a[...] = jnp.broadcast_to(a[0:1, :], c_a.shape)
        c_dconv[...] = dconv[:8]

    tile, halo, vec, cwspec, wblk = _rglru_specs(T, nt, True)
    acc8 = pl.BlockSpec((8, RNN_BLOCK), lambda n, t: (0, n))
    return pl.pallas_call(
        body, name="rglru_bwd", grid=(RNN_BLOCKS, nt),
        in_specs=[tile, halo, tile, tile, halo, tile, cwspec, vec, wblk, vec, wblk, vec, vec],
        out_specs=[tile, tile, wblk, wblk, acc8],
        out_shape=[jax.ShapeDtypeStruct((S, D_RNN), BF), jax.ShapeDtypeStruct((S, D_RNN), BF),
                   jax.ShapeDtypeStruct((RNN_BLOCKS, RNN_BLOCK, RNN_BLOCK), F32),
                   jax.ShapeDtypeStruct((RNN_BLOCKS, RNN_BLOCK, RNN_BLOCK), F32),
                   jax.ShapeDtypeStruct((8, D_RNN), F32)],
        scratch_shapes=[pltpu.VMEM((8, RNN_BLOCK), F32)] * 3,
        compiler_params=_params(("parallel", "arbitrary")),
    )(xr, xr, g, h, h, dy, cw, cb, wa, ba, wx, bx, lam)


def _rel_bucket_map():
    qi = np.arange(WINDOW)[:, None]
    kj = np.arange(2 * WINDOW)[None, :]
    dist = jnp.asarray(qi + WINDOW - kj, jnp.int32)
    n = jnp.maximum(dist, 0)
    max_exact = REL_BUCKETS // 2
    ratio = jnp.log(jnp.maximum(n, 1).astype(F32) / max_exact) / math.log(REL_MAX_DIST / max_exact)
    large = jnp.minimum(max_exact + (ratio * (REL_BUCKETS - max_exact)).astype(jnp.int32), REL_BUCKETS - 1)
    bucket = jnp.where(n < max_exact, n, large).astype(jnp.int32)
    j = np.arange(WINDOW)[None, :]
    return jnp.where(jnp.asarray(j > qi), bucket[:, :WINDOW], bucket[:, WINDOW:])


def _swa_common(n, kv_ref, bucket_ref, relb_ref, bias_scr):
    @pl.when(n == 0)
    def _():
        bk = bucket_ref[...]
        for h in range(SWA_HEADS):
            acc = jnp.zeros((WINDOW, WINDOW), F32)
            for b in range(REL_BUCKETS):
                acc = acc + jnp.where(bk == b, relb_ref[b, h], 0.0)
            bias_scr[h] = acc

    prev0 = pl.multiple_of(jnp.maximum(n - 1, 0) * WINDOW, WINDOW)
    cur0 = pl.multiple_of(n * WINDOW, WINDOW)
    kk = jnp.concatenate([kv_ref[pl.ds(prev0, WINDOW), :], kv_ref[pl.ds(cur0, WINDOW), :]], axis=0).astype(F32)
    rowi = lax.broadcasted_iota(jnp.int32, (WINDOW, WINDOW), 0)
    col = lax.broadcasted_iota(jnp.int32, (WINDOW, WINDOW), 1)
    from_prev = col > rowi
    return kk, from_prev, prev0, cur0


def _fold(full, from_prev):
    return jnp.where(from_prev, full[:, :WINDOW], full[:, WINDOW:])


def _unfold(sq, from_prev):
    return jnp.concatenate([jnp.where(from_prev, sq, 0.0), jnp.where(from_prev, 0.0, sq)], axis=1)


def _half_pair(part, kvh):
    lo = lax.broadcasted_iota(jnp.int32, part.shape, 1) < SWA_HD
    if kvh == 0:
        pa = jnp.where(lo, part, 0.0)
        pb = pltpu.roll(pa, SWA_HD, 1)
    else:
        pb = jnp.where(lo, 0.0, part)
        pa = pltpu.roll(pb, SWA_HD, 1)
    return pa.astype(BF), pb.astype(BF)


ALL_HEADS = SWA_HEADS * WINDOW


def _sink_column(sinks):
    return jnp.repeat(sinks.reshape(SWA_HEADS), WINDOW).reshape(ALL_HEADS, 1)


def _swa_operands(kk):
    return [(_half_pair(kk[:, :128], kvh), _half_pair(kk[:, 128:], kvh)) for kvh in range(SWA_KV_HEADS)]


def _swa_probs(n, q_ref, ops, bias_scr, sinkc_ref, from_prev):
    lgs = []
    for kvh in range(SWA_KV_HEADS):
        (ka, kb), _ = ops[kvh]
        for p in range(4):
            q2 = q_ref[:, kvh * 512 + p * 128:kvh * 512 + p * 128 + 128]
            lgs += [_fold(_dot(q2, ka, NT), from_prev), _fold(_dot(q2, kb, NT), from_prev)]
    lg = jnp.concatenate(lgs, axis=0) * (SWA_HD ** -0.5) + bias_scr[...].reshape(ALL_HEADS, WINDOW)
    rowi = jnp.bitwise_and(lax.broadcasted_iota(jnp.int32, (ALL_HEADS, WINDOW), 0), WINDOW - 1)
    col = lax.broadcasted_iota(jnp.int32, (ALL_HEADS, WINDOW), 1)
    no_prev = jnp.where(n > 0, 0, 4 * WINDOW)
    lg = jnp.where(jnp.logical_or(col <= rowi, col > rowi + no_prev), lg, NEG_INF)
    sink = sinkc_ref[...]
    m = jnp.maximum(jnp.max(lg, axis=-1, keepdims=True), sink)
    e = jnp.exp(lg - m)
    es = jnp.exp(sink - m)
    den = jnp.sum(e, axis=-1, keepdims=True) + es
    return e / den, es / den


def _swa_fwd(q, kv, g, bucket, rel_bias, sink_col):
    S = q.shape[0]
    nb = S // WINDOW

    def body(q_ref, kv_ref, g_ref, bucket_ref, relb_ref, sinkc_ref, o_ref, y_ref, bias_scr):
        n = pl.program_id(0)
        kk, from_prev, _, _ = _swa_common(n, kv_ref, bucket_ref, relb_ref, bias_scr)
        ops = _swa_operands(kk)
        pr, _ = _swa_probs(n, q_ref, ops, bias_scr, sinkc_ref, from_prev)
        for kvh in range(SWA_KV_HEADS):
            _, (va, vb) = ops[kvh]
            for p in range(4):
                c0 = kvh * 512 + p * 128
                r0 = (kvh * 8 + 2 * p) * WINDOW
                o2 = (_dot(_unfold(pr[r0:r0 + WINDOW], from_prev).astype(BF), va, NN)
                      + _dot(_unfold(pr[r0 + WINDOW:r0 + 2 * WINDOW], from_prev).astype(BF), vb, NN))
                o_ref[:, c0:c0 + 128] = o2
                gv = g_ref[:, c0:c0 + 128]
                y_ref[:, c0:c0 + 128] = (o2 * (gv * _sigmoid(gv))).astype(BF)

    blk = pl.BlockSpec((WINDOW, 1024), lambda n: (n, 0))
    smem = pl.BlockSpec(memory_space=pltpu.SMEM)
    sinkc = pl.BlockSpec((ALL_HEADS, 1), lambda n: (0, 0))
    return pl.pallas_call(
        body, name="swa_fwd", grid=(nb,),
        in_specs=[blk, pl.BlockSpec((S, 256), lambda n: (0, 0)), blk, pl.BlockSpec((WINDOW, WINDOW), lambda n: (0, 0)), smem, sinkc],
        out_specs=[blk, blk],
        out_shape=[jax.ShapeDtypeStruct((S, 1024), F32), jax.ShapeDtypeStruct((S, 1024), BF)],
        scratch_shapes=[pltpu.VMEM((SWA_HEADS, WINDOW, WINDOW), F32)],
        compiler_params=_params(("arbitrary",)),
    )(q, kv, g, bucket, rel_bias, sink_col)


def _swa_bwd(q, kv, g, o, dy, bucket, rel_bias, sink_col):
    S = q.shape[0]
    nb = S // WINDOW

    def body(q_ref, kv_ref, g_ref, o_ref, dy_ref, bucket_ref, relb_ref, sinkc_ref,
             dq_ref, dg_ref, dkv_ref, dsink_ref, drel_ref, bias_scr, dbias_scr, dsink_scr):
        n = pl.program_id(0)

        @pl.when(n == 0)
        def _():
            dbias_scr[...] = jnp.zeros_like(dbias_scr)
            dsink_scr[...] = jnp.zeros_like(dsink_scr)
            dkv_ref[...] = jnp.zeros_like(dkv_ref)

        kk, from_prev, prev0, cur0 = _swa_common(n, kv_ref, bucket_ref, relb_ref, bias_scr)
        ops = _swa_operands(kk)
        pr, ps = _swa_probs(n, q_ref, ops, bias_scr, sinkc_ref, from_prev)
        do2s, dps = [], []
        for kvh in range(SWA_KV_HEADS):
            _, (va, vb) = ops[kvh]
            for p in range(4):
                c0 = kvh * 512 + p * 128
                gv = g_ref[:, c0:c0 + 128]
                sg = _sigmoid(gv)
                dyv = dy_ref[:, c0:c0 + 128]
                dg_ref[:, c0:c0 + 128] = (dyv * o_ref[:, c0:c0 + 128] * (sg * (1.0 + gv * (1.0 - sg)))).astype(BF)
                do2 = (dyv * (gv * sg)).astype(BF)
                do2s.append(do2)
                dps += [_fold(_dot(do2, va, NT), from_prev), _fold(_dot(do2, vb, NT), from_prev)]
        dp = jnp.concatenate(dps, axis=0)
        delta = jnp.sum(pr * dp, axis=-1, keepdims=True)
        ds = pr * (dp - delta)
        dbias_scr[...] += ds.reshape(SWA_HEADS, WINDOW, WINDOW)
        dsink_scr[...] += ps * delta
        dsc = ds * (SWA_HD ** -0.5)
        lo256 = lax.broadcasted_iota(jnp.int32, (2 * WINDOW, 128), 1) < SWA_HD
        dks, dvs = [], []
        for kvh in range(SWA_KV_HEADS):
            (ka, kb), _ = ops[kvh]
            dka = jnp.zeros((2 * WINDOW, 128), F32)
            dkb, dva, dvb = dka, dka, dka
            for p in range(4):
                c0 = kvh * 512 + p * 128
                r0 = (kvh * 8 + 2 * p) * WINDOW
                q2 = q_ref[:, c0:c0 + 128]
                do2 = do2s[kvh * 4 + p]
                ds0 = _unfold(dsc[r0:r0 + WINDOW], from_prev).astype(BF)
                ds1 = _unfold(dsc[r0 + WINDOW:r0 + 2 * WINDOW], from_prev).astype(BF)
                dq_ref[:, c0:c0 + 128] = (_dot(ds0, ka, NN) + _dot(ds1, kb, NN)).astype(BF)
                dka = dka + _dot(ds0, q2, TN)
                dkb = dkb + _dot(ds1, q2, TN)
                dva = dva + _dot(_unfold(pr[r0:r0 + WINDOW], from_prev).astype(BF), do2, TN)
                dvb = dvb + _dot(_unfold(pr[r0 + WINDOW:r0 + 2 * WINDOW], from_prev).astype(BF), do2, TN)
            dks.append(jnp.where(lo256, dka, 0.0) + pltpu.roll(jnp.where(lo256, 0.0, dkb), SWA_HD, 1))
            dvs.append(jnp.where(lo256, dva, 0.0) + pltpu.roll(jnp.where(lo256, 0.0, dvb), SWA_HD, 1))
        dk = dks[0] + pltpu.roll(dks[1], SWA_HD, 1)
        dv = dvs[0] + pltpu.roll(dvs[1], SWA_HD, 1)
        dkv_ref[pl.ds(prev0, WINDOW), 0:128] += dk[:WINDOW]
        dkv_ref[pl.ds(prev0, WINDOW), 128:256] += dv[:WINDOW]
        dkv_ref[pl.ds(cur0, WINDOW), 0:128] += dk[WINDOW:]
        dkv_ref[pl.ds(cur0, WINDOW), 128:256] += dv[WINDOW:]

        @pl.when(n == nb - 1)
        def _():
            dsink_ref[...] = -jnp.sum(dsink_scr[...].reshape(SWA_HEADS, WINDOW, 1), axis=1)
            bk = bucket_ref[...]
            sums = []
            for b in range(REL_BUCKETS):
                sums.append(jnp.sum(jnp.where((bk == b)[None], dbias_scr[...], 0.0), axis=1))
            drel_ref[...] = jnp.sum(jnp.concatenate(sums, axis=0), axis=1, keepdims=True)

    blk = pl.BlockSpec((WINDOW, 1024), lambda n: (n, 0))
    smem = pl.BlockSpec(memory_space=pltpu.SMEM)
    whole = lambda shape: pl.BlockSpec(shape, lambda n: (0, 0))
    return pl.pallas_call(
        body, name="swa_bwd", grid=(nb,),
        in_specs=[blk, whole((S, 256)), blk, blk, blk, whole((WINDOW, WINDOW)), smem, whole((ALL_HEADS, 1))],
        out_specs=[blk, blk, whole((S, 256)), whole((SWA_HEADS, 1)), whole((REL_BUCKETS * SWA_HEADS, 1))],
        out_shape=[jax.ShapeDtypeStruct((S, 1024), BF), jax.ShapeDtypeStruct((S, 1024), BF),
                   jax.ShapeDtypeStruct((S, 256), F32), jax.ShapeDtypeStruct((SWA_HEADS, 1), F32),
                   jax.ShapeDtypeStruct((REL_BUCKETS * SWA_HEADS, 1), F32)],
        scratch_shapes=[pltpu.VMEM((SWA_HEADS, WINDOW, WINDOW), F32), pltpu.VMEM((SWA_HEADS, WINDOW, WINDOW), F32),
                        pltpu.VMEM((ALL_HEADS, 1), F32)],
        compiler_params=_params(("arbitrary",)),
    )(q, kv, g, o, dy, bucket, rel_bias, sink_col)


def _mem_probs(qh, mk):
    lg = _dot(qh, mk, NT) * (MEM_HD ** -0.5)
    e = jnp.exp(lg - jnp.max(lg, axis=-1, keepdims=True))
    return e / jnp.sum(e, axis=-1, keepdims=True)


def _mem_fwd(q, mkv, g):
    S = q.shape[0]
    M = mkv.shape[0]
    tq = 256

    def body(q_ref, mkv_ref, g_ref, o_ref, y_ref):
        for h in range(MEM_HEADS):
            c0 = h * MEM_HD
            pr = _mem_probs(q_ref[:, c0:c0 + MEM_HD], mkv_ref[:, c0:c0 + MEM_HD])
            o = _dot(pr.astype(BF), mkv_ref[:, D_MEM + c0:D_MEM + c0 + MEM_HD], NN)
            o_ref[:, c0:c0 + MEM_HD] = o
            gv = g_ref[:, c0:c0 + MEM_HD]
            y_ref[:, c0:c0 + MEM_HD] = (o * (gv * _sigmoid(gv))).astype(BF)

    blk = pl.BlockSpec((tq, D_MEM), lambda i: (i, 0))
    return pl.pallas_call(
        body, name="mem_fwd", grid=(S // tq,),
        in_specs=[blk, pl.BlockSpec((M, 2 * D_MEM), lambda i: (0, 0)), blk], out_specs=[blk, blk],
        out_shape=[jax.ShapeDtypeStruct((S, D_MEM), F32), jax.ShapeDtypeStruct((S, D_MEM), BF)],
        compiler_params=_params(("parallel",)),
    )(q, mkv, g)


def _mem_bwd(q, mkv, g, o, dy):
    S = q.shape[0]
    M = mkv.shape[0]
    tq = 256

    def body(q_ref, mkv_ref, g_ref, o_ref, dy_ref, dq_ref, dg_ref, dmkv_ref):
        @pl.when(pl.program_id(0) == 0)
        def _():
            dmkv_ref[...] = jnp.zeros_like(dmkv_ref)

        for h in range(MEM_HEADS):
            c0 = h * MEM_HD
            qh = q_ref[:, c0:c0 + MEM_HD]
            mk = mkv_ref[:, c0:c0 + MEM_HD]
            mv = mkv_ref[:, D_MEM + c0:D_MEM + c0 + MEM_HD]
            gv = g_ref[:, c0:c0 + MEM_HD]
            sg = _sigmoid(gv)
            dyv = dy_ref[:, c0:c0 + MEM_HD]
            dg_ref[:, c0:c0 + MEM_HD] = (dyv * o_ref[:, c0:c0 + MEM_HD] * (sg * (1.0 + gv * (1.0 - sg)))).astype(BF)
            do = (dyv * (gv * sg)).astype(BF)
            pr = _mem_probs(qh, mk)
            dp = _dot(do, mv, NT)
            ds = pr * (dp - jnp.sum(pr * dp, axis=-1, keepdims=True))
            dsb = (ds * (MEM_HD ** -0.5)).astype(BF)
            dq_ref[:, c0:c0 + MEM_HD] = _dot(dsb, mk, NN).astype(BF)
            dmkv_ref[:, c0:c0 + MEM_HD] += _dot(dsb, qh, TN)
            dmkv_ref[:, D_MEM + c0:D_MEM + c0 + MEM_HD] += _dot(pr.astype(BF), do, TN)

    blk = pl.BlockSpec((tq, D_MEM), lambda i: (i, 0))
    whole = pl.BlockSpec((M, 2 * D_MEM), lambda i: (0, 0))
    return pl.pallas_call(
        body, name="mem_bwd", grid=(S // tq,),
        in_specs=[blk, whole, blk, blk, blk], out_specs=[blk, blk, whole],
        out_shape=[jax.ShapeDtypeStruct((S, D_MEM), BF), jax.ShapeDtypeStruct((S, D_MEM), BF),
                   jax.ShapeDtypeStruct((M, 2 * D_MEM), F32)],
        compiler_params=_params(("arbitrary",)),
    )(q, mkv, g, o, dy)


MERGE_TN = 512


def _merge_specs(tm):
    ytile = pl.BlockSpec((tm, 1024), lambda i, j: (i, 0))
    wblk = pl.BlockSpec((MERGE_TN, 1024), lambda i, j: (j, 0))
    gls = [pl.BlockSpec((None, tm, MERGE_TN), (lambda i, j, br=br: (br, i, j))) for br in range(3)]
    otile = pl.BlockSpec((tm, MERGE_TN), lambda i, j: (i, j))
    return ytile, wblk, gls, otile


def _merge_fwd(ys, ws, gl, tm):
    S = gl.shape[1]

    def body(y0, y1, y2, w0, w1, w2, g0, g1, g2, o_ref):
        acc = None
        for y_ref, w_ref, g_ref in ((y0, w0, g0), (y1, w1, g1), (y2, w2, g2)):
            term = _sigmoid(g_ref[...]) * _dot(y_ref[...], w_ref[...], NT)
            acc = term if acc is None else acc + term
        o_ref[...] = acc.astype(BF)

    ytile, wblk, gls, otile = _merge_specs(tm)
    return pl.pallas_call(
        body, name="merge_fwd", grid=(S // tm, D_MODEL // MERGE_TN),
        in_specs=[ytile] * 3 + [wblk] * 3 + gls, out_specs=otile,
        out_shape=jax.ShapeDtypeStruct((S, D_MODEL), BF),
        compiler_params=_params(("parallel", "arbitrary")),
    )(*ys, *ws, gl, gl, gl)


def _merge_bwd(dout, w_out, ys, ws, gl, tm):
    S = gl.shape[1]

    def body(do_ref, wo_ref, y0, y1, y2, w0, w1, w2, g0, g1, g2, dg0, dg1, dg2, dp0, dp1, dp2):
        dm = _dot(do_ref[...], wo_ref[...], NT)
        for y_ref, w_ref, g_ref, dg_ref, dp_ref in ((y0, w0, g0, dg0, dp0), (y1, w1, g1, dg1, dp1), (y2, w2, g2, dg2, dp2)):
            gate = _sigmoid(g_ref[...])
            pv = _dot(y_ref[...], w_ref[...], NT)
            dg_ref[...] = (dm * pv * gate * (1.0 - gate)).astype(BF)
            dp_ref[...] = (dm * gate).astype(BF)

    ytile, wblk, gls, otile = _merge_specs(tm)
    out = jax.ShapeDtypeStruct((S, D_MODEL), BF)
    return pl.pallas_call(
        body, name="merge_bwd", grid=(S // tm, D_MODEL // MERGE_TN),
        in_specs=[pl.BlockSpec((tm, D_MODEL), lambda i, j: (i, 0)), pl.BlockSpec((MERGE_TN, D_MODEL), lambda i, j: (j, 0))]
        + [ytile] * 3 + [wblk] * 3 + gls,
        out_specs=[otile] * 6, out_shape=[out] * 6,
        compiler_params=_params(("parallel", "arbitrary")),
    )(dout, w_out, *ys, *ws, gl, gl, gl)


def _out_loss(merged, w_out, x, target, post_g, tm):
    S = x.shape[0]

    def body(m_ref, w_ref, x_ref, t_ref, g_ref, dout_ref, dy_ref, loss_ref, dpost_ref):
        @pl.when(pl.program_id(0) == 0)
        def _():
            loss_ref[...] = jnp.zeros_like(loss_ref)
            dpost_ref[...] = jnp.zeros_like(dpost_ref)

        out = _dot(m_ref[...], w_ref[...], NN)
        r = lax.rsqrt(jnp.mean(out * out, axis=-1, keepdims=True) + EPS)
        nrm = out * r
        gv = g_ref[...]
        err = (x_ref[...] + nrm * gv) - t_ref[...]
        sq = jnp.sum(jnp.sum(err * err, axis=1, keepdims=True), axis=0, keepdims=True)
        loss_ref[...] += sq * (0.5 / D_MODEL)
        dy = err * (1.0 / D_MODEL)
        dy_ref[...] = dy
        dpost_ref[...] += jnp.sum(dy * nrm, axis=0, keepdims=True)
        dn = dy * gv
        dout_ref[...] = (r * (dn - nrm * jnp.mean(dn * nrm, axis=-1, keepdims=True))).astype(BF)

    row = pl.BlockSpec((tm, D_MODEL), lambda i: (i, 0))
    return pl.pallas_call(
        body, name="out_loss", grid=(S // tm,),
        in_specs=[row, pl.BlockSpec((D_MODEL, D_MODEL), lambda i: (0, 0)), row, row, pl.BlockSpec((1, D_MODEL), lambda i: (0, 0))],
        out_specs=[row, row, pl.BlockSpec((8, 128), lambda i: (0, 0)), pl.BlockSpec((1, D_MODEL), lambda i: (0, 0))],
        out_shape=[jax.ShapeDtypeStruct((S, D_MODEL), BF), jax.ShapeDtypeStruct((S, D_MODEL), F32),
                   jax.ShapeDtypeStruct((8, 128), F32), jax.ShapeDtypeStruct((1, D_MODEL), F32)],
        compiler_params=_params(("arbitrary",)),
    )(merged, w_out, x, target, post_g)


def _dh_dx(dproj, w_in, x, dy, pre_g, tm):
    S = x.shape[0]
    nk, tk = dproj.shape[0], dproj.shape[2]

    def body(dp_ref, w_ref, x_ref, dy_ref, g_ref, dx_ref, dpre_ref, acc_ref):
        i, k = pl.program_id(0), pl.program_id(1)

        @pl.when(jnp.logical_and(i == 0, k == 0))
        def _():
            dpre_ref[...] = jnp.zeros_like(dpre_ref)

        @pl.when(k == 0)
        def _():
            acc_ref[...] = jnp.zeros_like(acc_ref)

        acc_ref[...] += _dot(dp_ref[...], w_ref[...], NN)

        @pl.when(k == nk - 1)
        def _():
            dh = acc_ref[...]
            xv = x_ref[...]
            r = lax.rsqrt(jnp.mean(xv * xv, axis=-1, keepdims=True) + EPS)
            nrm = xv * r
            dpre_ref[...] += jnp.sum(dh * nrm, axis=0, keepdims=True)
            dn = dh * g_ref[...]
            dx_ref[...] = r * (dn - nrm * jnp.mean(dn * nrm, axis=-1, keepdims=True)) + dy_ref[...]

    row = pl.BlockSpec((tm, D_MODEL), lambda i, k: (i, 0))
    vec = pl.BlockSpec((1, D_MODEL), lambda i, k: (0, 0))
    return pl.pallas_call(
        body, name="dh_dx", grid=(S // tm, nk),
        in_specs=[pl.BlockSpec((None, tm, tk), lambda i, k: (k, i, 0)), pl.BlockSpec((tk, D_MODEL), lambda i, k: (k, 0)), row, row, vec],
        out_specs=[row, vec],
        out_shape=[jax.ShapeDtypeStruct((S, D_MODEL), F32), jax.ShapeDtypeStruct((1, D_MODEL), F32)],
        scratch_shapes=[pltpu.VMEM((tm, D_MODEL), F32)],
        compiler_params=_params(("arbitrary", "arbitrary"), large=True),
    )(dproj, w_in, x, dy, pre_g)


def _sum_parts(parts, name):
    P, R, C = parts.shape
    tr = max(t for t in range(8, 513, 8) if R % t == 0)

    def body(p_ref, o_ref):
        acc = p_ref[0]
        for j in range(1, P):
            acc = acc + p_ref[j]
        o_ref[...] = acc

    return pl.pallas_call(
        body, name=name, grid=(R // tr,),
        in_specs=[pl.BlockSpec((P, tr, C), lambda i: (0, i, 0))], out_specs=pl.BlockSpec((tr, C), lambda i: (i, 0)),
        out_shape=jax.ShapeDtypeStruct((R, C), F32), compiler_params=_params(("parallel",)),
    )(parts)


def _adamw(lands, sums, chip, w, m, v, name):
    lands = list(lands) if isinstance(lands, (list, tuple)) else [lands]
    sums = list(sums) if isinstance(sums, (list, tuple)) else [sums]
    _, R, cols = lands[0].shape
    C = cols * len(lands)
    tr = max(t for t in range(16, 257, 16) if R % t == 0)
    c1 = 1.0 - ADAM_B1 ** ADAM_STEP
    c2 = 1.0 - ADAM_B2 ** ADAM_STEP

    def body(chip_ref, *refs):
        p_refs = refs[:4 * len(lands)]
        w_ref, m_ref, v_ref, g_ref, d_ref, nm_ref, nv_ref = refs[4 * len(lands):]
        gs = []
        for q in range(len(lands)):
            gq = p_refs[4 * q + 3][...].astype(F32)
            for j in range(3):
                gq = gq + p_refs[4 * q + j][...].astype(F32)
            gs.append(gq)
        g = gs[0] if len(gs) == 1 else jnp.concatenate(gs, axis=1)
        nm = ADAM_B1 * m_ref[...] + (1.0 - ADAM_B1) * g
        nv = ADAM_B2 * v_ref[...] + (1.0 - ADAM_B2) * (g * g)
        g_ref[...] = g
        nm_ref[...] = nm
        nv_ref[...] = nv
        d_ref[...] = -ADAM_LR * ((nm / c1) / (jnp.sqrt(nv / c2) + ADAM_EPS) + ADAM_WD * w_ref[...])

    tile = pl.BlockSpec((None, tr, C), lambda i, c_ref: (0, i, 0))
    specs, operands = [], []
    for q in range(len(lands)):
        for k in range(3):
            specs.append(pl.BlockSpec((None, tr, cols), (lambda i, c_ref, k=k: (k + (c_ref[0] <= k).astype(jnp.int32), i, 0))))
            operands.append(lands[q])
        specs.append(pl.BlockSpec((None, tr, cols), (lambda i, c_ref: (c_ref[0], i, 0))))
        operands.append(sums[q])
    return pl.pallas_call(
        body, name=name,
        grid_spec=pltpu.PrefetchScalarGridSpec(num_scalar_prefetch=1, grid=(R // tr,), in_specs=specs + [tile, tile, tile],
                                               out_specs=[tile] * 4),
        out_shape=[jax.ShapeDtypeStruct((1, R, C), F32)] * 4, compiler_params=_params(("parallel",)),
    )(chip, *operands, w, m, v)


def _adamw_small(gs, ws, ms, vs):
    n = len(ws)
    c1 = 1.0 - ADAM_B1 ** ADAM_STEP
    c2 = 1.0 - ADAM_B2 ** ADAM_STEP

    def flat2(a):
        return a.reshape(-1, a.shape[-1])

    def body(*refs):
        ins, outs = refs[:4 * n], refs[4 * n:]
        for a in range(n):
            g, w, m, v = (ins[k * n + a][...] for k in range(4))
            nm = ADAM_B1 * m + (1.0 - ADAM_B1) * g
            nv = ADAM_B2 * v + (1.0 - ADAM_B2) * (g * g)
            outs[a][...] = g
            outs[n + a][...] = -ADAM_LR * ((nm / c1) / (jnp.sqrt(nv / c2) + ADAM_EPS) + ADAM_WD * w)
            outs[2 * n + a][...] = nm
            outs[3 * n + a][...] = nv

    shapes = [flat2(w).shape for w in ws]
    out = pl.pallas_call(
        body, name="adamw_small", out_shape=[jax.ShapeDtypeStruct(sh, F32) for sh in shapes] * 4,
        compiler_params=_params(),
    )(*[g.reshape(sh) for g, sh in zip(gs, shapes)], *[flat2(a) for a in (*ws, *ms, *vs)])
    return [[out[k * n + a].reshape(ws[a].shape) for a in range(n)] for k in range(4)]


def _project(h, w_t):
    S = h.shape[0]
    n_tiles = D_IN // SEG_TILE
    ranges = [(c0 // SEG_TILE, (c0 + width) // SEG_TILE) for _, c0, width, _ in SEGMENTS]

    def body(h_ref, w_ref, *outs):
        j = pl.program_id(0)
        prod = _dot(h_ref[...], w_ref[...], NT)
        for (j0, j1), (_, _, _, dt), o_ref in zip(ranges, SEGMENTS, outs):
            @pl.when(jnp.logical_and(j >= j0, j < j1))
            def _(o_ref=o_ref, dt=dt):
                o_ref[...] = prod.astype(dt)

    out_shapes, out_specs = [], []
    for (j0, j1), (name, _, width, dt) in zip(ranges, SEGMENTS):
        if name == "gl":
            per = (j1 - j0) // 3
            out_shapes.append(jax.ShapeDtypeStruct((3, S, width // 3), dt))
            out_specs.append(pl.BlockSpec((None, S, SEG_TILE), (lambda j, j0=j0, j1=j1, per=per: (
                jnp.clip(j - j0, 0, j1 - j0 - 1) // per, 0, jnp.clip(j - j0, 0, j1 - j0 - 1) % per))))
        else:
            out_shapes.append(jax.ShapeDtypeStruct((S, width), dt))
            out_specs.append(pl.BlockSpec((S, SEG_TILE), (lambda j, j0=j0, j1=j1: (0, jnp.clip(j - j0, 0, j1 - j0 - 1)))))
    outs = pl.pallas_call(
        body, name="proj", grid=(n_tiles,),
        in_specs=[pl.BlockSpec((S, D_MODEL), lambda j: (0, 0)), pl.BlockSpec((SEG_TILE, D_MODEL), lambda j: (j, 0))],
        out_specs=out_specs, out_shape=out_shapes,
        compiler_params=_params(("arbitrary",), large=True),
    )(h, w_t)
    return {name: o for (name, _, _, _), o in zip(SEGMENTS, outs)}


def _forward_a(x, mem, pre_g, mem_g, w_in, conv_w, conv_b, w_a, b_a, w_x, b_x, lam, sinks, rel_bias):
    S = x.shape[0]
    st = dict(T=min(512, S // 2), tm=min(512, S), bucket=_rel_bucket_map())
    st["h"] = _rms_fwd(x, pre_g, "pre_norm")
    st["memn"] = _rms_fwd(mem, mem_g, "mem_norm")
    seg = st["seg"] = _project(st["h"], w_in)
    st["h_rg"], st["y_rg"] = _rglru_fwd(seg["xr"], seg["g_rg"], conv_w, conv_b, w_a, b_a, w_x, b_x, lam, st["T"])
    st["o_swa"], st["y_swa"] = _swa_fwd(seg["q_s"], seg["kv"], seg["g_swa"], st["bucket"], rel_bias, _sink_column(sinks))
    return st


def _forward_b(st, x, target, post_g, w_memkv, wbr, w_out):
    S = x.shape[0]
    M = st["memn"].shape[0]
    seg = st["seg"]
    st["mkv"] = _matmul(st["memn"], w_memkv, "nn", M, 2 * D_MEM, D_MODEL, M, 512, D_MODEL, BF, "mem_kv")
    st["o_mem"], st["y_mem"] = _mem_fwd(seg["q_m"], st["mkv"], seg["g_mem"])
    st["ys"] = (st["y_rg"], st["y_swa"], st["y_mem"])
    st["merged"] = _merge_fwd(st["ys"], wbr, seg["gl"], st["tm"])
    st["dout"], st["dy"], st["loss"], st["dpost"] = _out_loss(st["merged"], w_out, x, target, post_g, min(256, S))
    return st


def _backward_a1(st, wbr, w_out):
    S = st["h"].shape[0]
    seg, ys, tm = st["seg"], st["ys"], st["tm"]
    st["dw_out"] = _matmul(st["merged"], st["dout"], "tn", D_MODEL, D_MODEL, S, 256, D_MODEL, S, BF, "dw_out", out_blocked="row")
    dgl0, dgl1, dgl2, dp0, dp1, dp2 = _merge_bwd(st["dout"], w_out, ys, wbr, seg["gl"], tm)
    st["dgl"] = (dgl0, dgl1, dgl2)
    dys, dwbr = [], []
    for i, dp in enumerate((dp0, dp1, dp2)):
        dys.append(_matmul(dp, wbr[i], "nn", S, 1024, D_MODEL, tm, 1024, D_MODEL, F32, "dy_br%d" % i))
        dwbr.append(_matmul(ys[i], dp, "tn", 1024, D_MODEL, S, 1024, 256, S, BF, "dw_br%d" % i, out_blocked="col"))
    st["dys"], st["dwbr"] = dys, dwbr
    return st


def _backward_a2(st, mem, w_memkv, conv_w, conv_b, w_a, b_a, w_x, b_x, lam):
    M = mem.shape[0]
    seg, dys = st["seg"], st["dys"]
    st["dq_m"], st["dg_mem"], dmkv = _mem_bwd(seg["q_m"], st["mkv"], seg["g_mem"], st["o_mem"], dys[2])
    dmkv_b = dmkv.astype(BF)
    st["dw_memkv"] = _matmul(st["memn"], dmkv_b, "tn", D_MODEL, 2 * D_MEM, M, 256, 2 * D_MEM, M, BF, "dw_memkv", out_blocked="row")
    dmemn = _matmul(dmkv_b, w_memkv, "nt", M, D_MODEL, 2 * D_MEM, M, 512, 2 * D_MEM, F32, "dmemn")
    st["dmem_g"] = _rms_gain_grad(dmemn, mem, "dmem_gain")
    st["dxr"], st["dg_rg"], st["dw_a"], st["dw_x"], st["dvec"] = _rglru_bwd(
        seg["xr"], seg["g_rg"], st["h_rg"], dys[0], conv_w, conv_b, w_a, b_a, w_x, b_x, lam, st["T"])
    return st


def _backward_b(st, rel_bias, sinks):
    seg = st["seg"]
    dq_s, dg_swa, dkv, dsinks, drel = _swa_bwd(seg["q_s"], seg["kv"], seg["g_swa"], st["o_swa"], st["dys"][1],
                                               st["bucket"], rel_bias, _sink_column(sinks))
    st["dsinks"], st["drel"] = dsinks.reshape(1, SWA_HEADS), drel.reshape(REL_BUCKETS, SWA_HEADS)
    dproj = jnp.concatenate([st["dxr"], st["dg_rg"], dq_s, dkv.astype(BF), dg_swa, st["dq_m"], st["dg_mem"], *st["dgl"]], axis=1)
    st["dproj"] = jnp.transpose(dproj.reshape(dproj.shape[0], N_DEV, D_IN // N_DEV), (1, 0, 2))
    return st


def _dw_in_half(st, half, dep=None):
    S = st["h"].shape[0]
    return _matmul(st["dproj"], st["h"], "tn", D_IN, D_MODEL // 2, S, D_IN // N_DEV, 512, S, BF, "dw_in%d" % half, b_noff=2 * half,
                   a_blocked=True, out_blocked="row", dep=dep)


def _owner_blocks(a):
    return jnp.swapaxes(a.reshape((4, 2) + a.shape[1:]), 0, 1)


def _local_step(x, mem, target, pre_g, post_g, mem_g, w_in, conv_w, conv_b, w_a, b_a, w_x, b_x, lam, sinks, rel_bias,
                w_memkv, wbr, w_out):
    st = _forward_a(x, mem, pre_g, mem_g, w_in, conv_w, conv_b, w_a, b_a, w_x, b_x, lam, sinks, rel_bias)
    st = _forward_b(st, x, target, post_g, w_memkv, wbr, w_out)
    st = _backward_a1(st, wbr, w_out)
    st = _backward_a2(st, mem, w_memkv, conv_w, conv_b, w_a, b_a, w_x, b_x, lam)
    st = _backward_b(st, rel_bias, sinks)
    st["dw_in"] = [_dw_in_half(st, 0), _dw_in_half(st, 1)]
    st["grad_x"], st["dpre"] = _dh_dx(st["dproj"], w_in, x, st["dy"], pre_g, st["tm"])
    return st


def _pad_rows(a, rows):
    a = a.reshape(-1, 128) if a.shape[-1] % 128 == 0 else jnp.pad(a, ((0, 0), (0, 128 - a.shape[-1])))
    return jnp.pad(a, ((0, rows - a.shape[0]), (0, 0))) if a.shape[0] < rows else a


def kernel(x, mem, pre_norm_g, post_norm_g, mem_norm_g, w_in, conv_w, conv_b, w_rg_a, b_rg_a, w_rg_x, b_rg_x, lru_lambda, swa_sinks, rel_bias, w_mem_kv, w_br_rg, w_br_swa, w_br_mem, w_out, loss_target, m_pre_norm_g, m_post_norm_g, m_mem_norm_g, m_w_in, m_conv_w, m_conv_b, m_w_rg_a, m_b_rg_a, m_w_rg_x, m_b_rg_x, m_lru_lambda, m_swa_sinks, m_rel_bias, m_w_mem_kv, m_w_br_rg, m_w_br_swa, m_w_br_mem, m_w_out, v_pre_norm_g, v_post_norm_g, v_mem_norm_g, v_w_in, v_conv_w, v_conv_b, v_w_rg_a, v_b_rg_a, v_w_rg_x, v_b_rg_x, v_lru_lambda, v_swa_sinks, v_rel_bias, v_w_mem_kv, v_w_br_rg, v_w_br_swa, v_w_br_mem, v_w_out):
    cx, cy, cc = lax.axis_index("x"), lax.axis_index("y"), lax.axis_index("c")
    me = 4 * cx + 2 * cy + cc
    chip = 2 * cx + cy
    core = jnp.reshape(cc, (1,)).astype(jnp.int32)
    x0, mem0 = x[0], mem[0]
    w_a_b, w_x_b = w_rg_a[0].astype(BF), w_rg_x[0].astype(BF)

    def landing(own, slot, slots):
        return lax.dynamic_update_slice(lax.empty((slots,) + own.shape, own.dtype), own[None], (slot,) + (0,) * own.ndim)


    def swap_start(parts, tag):
        return _exchange_start(parts, [lax.empty(p.shape[1:], p.dtype) for p in parts], _plan_swap(len(parts)), "swap_%s_start" % tag)

    def scatter_start(swap, after, tag, prefill=()):
        s_send, s_recv, parts, got, _ = swap
        got = _exchange_wait(s_send, s_recv, parts, got, _plan_swap(len(parts)), after, "swap_%s_wait" % tag)
        sums = [_pair_sum(p, g, core, "scatter_%s_sum%d" % (tag, i)) for i, (p, g) in enumerate(zip(parts, got))]
        lands = [landing(lax.dynamic_index_in_dim(s, chip, 0, keepdims=False), chip, 4) if i in prefill
                 else lax.empty(s.shape, s.dtype) for i, s in enumerate(sums)]
        return _exchange_start(sums, lands, _plan_scatter(len(sums)), "scatter_%s_start" % tag)

    def zero_after(a):
        return jnp.minimum(jnp.abs(a.reshape(-1)[0].astype(F32)), 0.0)

    g_in, g_cw = _all_gather_relayed([jnp.transpose(w_in[0]).astype(BF), conv_w[0]], [True, False], "gather_w_in")
    w_in_f = g_in.reshape(D_IN, D_MODEL)
    conv_w_f = jnp.transpose(g_cw, (1, 0, 2)).reshape(CONV_W, D_RNN)

    after_first = zero_after(g_cw).astype(BF)
    rest = [w.astype(BF) + after_first for w in (w_mem_kv[0], jnp.transpose(w_br_rg[0]), jnp.transpose(w_br_swa[0]),
                                                 jnp.transpose(w_br_mem[0]), w_out[0])]
    kinds = ["lead"] * len(rest)
    plan_g = _plan_gather(kinds)
    zones = _place_own([lax.empty((N_DEV,) + w.shape, w.dtype) for w in rest], rest, jnp.reshape(me, (1,)).astype(jnp.int32), "gather_rest_own")
    g_send, g_recv, g_src, g_land, g_token = _exchange_start(rest, zones, plan_g, "gather_rest_start")
    st = _forward_a(x0, mem0, pre_norm_g + g_token[0:1, 0:1], mem_norm_g, w_in_f, conv_w_f, conv_b, w_a_b, b_rg_a, w_x_b, b_rg_x,
                    lru_lambda, swa_sinks, rel_bias)
    g_land = _exchange_wait(g_send, g_recv, g_src, g_land, plan_g, st["y_swa"], "gather_rest_wait")
    g_land = _forward_to_sibling(g_land, kinds, "gather_rest_forward")
    w_memkv_f = g_land[0].reshape(D_MODEL, 2 * D_MEM)
    wbr = tuple(g_land[i].reshape(D_MODEL, D_RNN) for i in (1, 2, 3))
    w_out_f = g_land[4].reshape(D_MODEL, D_MODEL)

    st = _forward_b(st, x0, loss_target[0], post_norm_g, w_memkv_f, wbr, w_out_f)
    st = _backward_a1(st, wbr, w_out_f)
    parts_a = [st["dw_out"], st["dwbr"][0], st["dwbr"][1], st["dwbr"][2]]
    plan_a = _plan_scatter(len(parts_a))
    swap_a = swap_start(parts_a, "a")
    st = _backward_a2(st, mem0, w_memkv_f, conv_w_f, conv_b + swap_a[4][0:1, 0:1], w_a_b, b_rg_a, w_x_b, b_rg_x, lru_lambda)
    a_send, a_recv, a_src, a_land, a_token = scatter_start(swap_a, st["dxr"], "a")
    parts_c = [st["dw_memkv"], _owner_blocks(st["dw_a"]), _owner_blocks(st["dw_x"])]
    plan_c = _plan_scatter(len(parts_c))
    swap_c = swap_start(parts_c, "c")

    st = _backward_b(st, rel_bias, swa_sinks + swap_c[4][0:1, 0:1] + a_token[0:1, 0:1])
    c_send, c_recv, c_src, c_land, c_token = scatter_start(swap_c, st["dsinks"], "c", prefill=(1, 2))
    plan_b = _plan_scatter(1)

    def dw_in_parts(half, dep):
        dwh = _dw_in_half(st, half, dep)
        return dwh, [dwh]

    dw0, parts_b0 = dw_in_parts(0, c_token)
    swap_b0 = swap_start(parts_b0, "b0")
    a_land = _exchange_wait(a_send, a_recv, a_src, a_land, plan_a, swap_b0[4], "scatter_a_wait")
    big = [None] * 6

    chip1 = jnp.reshape(chip, (1,)).astype(jnp.int32)

    def adamw_big(j, land, own, wt, mt, vt):
        big[j] = _adamw(land, own, chip1, wt, mt, vt, "adamw_big%d" % j)

    adamw_big(5, a_land[0], a_src[0], w_out, m_w_out, v_w_out)
    adamw_big(2, a_land[1], a_src[1], w_br_rg, m_w_br_rg, v_w_br_rg)
    adamw_big(3, a_land[2], a_src[2], w_br_swa, m_w_br_swa, v_w_br_swa)
    halves = [scatter_start(swap_b0, big[3][1], "b0")]
    dw1, parts_b1 = dw_in_parts(1, halves[0][4])
    swap_b1 = swap_start(parts_b1, "b1")
    c_land = _exchange_wait(c_send, c_recv, c_src, c_land, plan_c, swap_b1[4], "scatter_c_wait")
    g_wa_blk = _sum_parts(c_land[1], "sum_w_rg_a")
    g_wx_blk = _sum_parts(c_land[2], "sum_w_rg_x")
    adamw_big(4, a_land[3], a_src[3], w_br_mem, m_w_br_mem, v_w_br_mem)
    adamw_big(1, c_land[0], c_src[0], w_mem_kv, m_w_mem_kv, v_w_mem_kv)
    halves.append(scatter_start(swap_b1, big[1][1], "b1"))
    grad_x, dpre = _dh_dx(st["dproj"], w_in_f, x0, st["dy"], pre_norm_g + halves[1][4][0:1, 0:1], st["tm"])
    pack = jnp.concatenate([dpre.reshape(16, 128), st["dpost"].reshape(16, 128), st["dmem_g"].reshape(16, 128),
                            st["dvec"].reshape(64, 128), _pad_rows(st["dsinks"], 8), _pad_rows(st["drel"], 32), g_wa_blk, g_wx_blk,
                            st["loss"]], axis=0)
    plan_s = _plan_everyone(1)
    s_send, s_recv, s_src, s_land, s_token = _exchange_start([pack], [landing(pack, me, N_DEV)], plan_s, "gather_small_start")
    after, b_lands, b_sums = s_token, [], []
    for half, (b_send, b_recv, b_src, b_land, _) in enumerate(halves):
        b_lands.append(_exchange_wait(b_send, b_recv, b_src, b_land, plan_b, after, "scatter_b%d_wait" % half)[0])
        b_sums.append(b_src[0])
        after = b_lands[-1]
    swap_last = lambda a: jnp.transpose(a, (0, 2, 1))
    big[0] = [swap_last(a) for a in _adamw(b_lands, b_sums, chip1, swap_last(w_in), swap_last(m_w_in), swap_last(v_w_in), "adamw_big0")]
    gathered = _exchange_wait(s_send, s_recv, s_src, s_land, plan_s, big[0][1], "gather_small_wait")[0]
    gs = _sum_parts(gathered, "sum_small")
    loss_total = gs[408, 0]
    g_pre, g_post, g_memg = gs[0:16].reshape(1, D_MODEL), gs[16:32].reshape(1, D_MODEL), gs[32:48].reshape(1, D_MODEL)
    gvec = gs[48:112].reshape(8, D_RNN)
    g_conv_w = lax.dynamic_slice(gvec[0:CONV_W], (0, me * RNN_BLOCK), (CONV_W, RNN_BLOCK))
    g_conv_b, g_b_a, g_b_x, g_lam = gvec[4:5], gvec[5:6], gvec[6:7], gvec[7:8]
    g_sinks = gs[112:113, :SWA_HEADS]
    g_rel = gs[120:152, :SWA_HEADS]
    g_w_a = gathered[:, 152:280]
    g_w_x = gathered[:, 280:408]

    g_small = (g_pre, g_post, g_memg, g_conv_b, g_b_a, g_b_x, g_lam, g_w_a, g_w_x, g_sinks, g_rel, g_conv_w)
    w_small = (pre_norm_g, post_norm_g, mem_norm_g, conv_b, b_rg_a, b_rg_x, lru_lambda, w_rg_a, w_rg_x, swa_sinks, rel_bias, conv_w)
    m_small = (m_pre_norm_g, m_post_norm_g, m_mem_norm_g, m_conv_b, m_b_rg_a, m_b_rg_x, m_lru_lambda, m_w_rg_a, m_w_rg_x, m_swa_sinks, m_rel_bias, m_conv_w)
    v_small = (v_pre_norm_g, v_post_norm_g, v_mem_norm_g, v_conv_b, v_b_rg_a, v_b_rg_x, v_lru_lambda, v_w_rg_a, v_w_rg_x, v_swa_sinks, v_rel_bias, v_conv_w)
    sm = _adamw_small(g_small, w_small, m_small, v_small)


    def leaves(k):
        s = sm[k]
        return [s[0], s[1], s[2], big[0][k], s[11], s[3], s[7], s[4], s[8], s[5], s[6], s[9], s[10],
                big[1][k], big[2][k], big[3][k], big[4][k], big[5][k]]

    return (loss_total, grad_x[None], *leaves(0), *leaves(1), *leaves(2), *leaves(3))
```

```python
import math

import jax
import jax.numpy as jnp
import numpy as np
from jax import lax
from jax.experimental import pallas as pl
from jax.experimental.pallas import tpu as pltpu

F32, BF = jnp.float32, jnp.bfloat16
MESH = pl.DeviceIdType.MESH
N_DEV = 8

D_MODEL = 2048
D_RNN = 1024
RNN_BLOCKS = 8
RNN_BLOCK = 128
CONV_W = 4
LRU_C = 8.0
SWA_HEADS = 16
SWA_KV_HEADS = 2
SWA_HD = 64
WINDOW = 128
MEM_HEADS = 4
MEM_HD = 256
D_MEM = 1024
REL_BUCKETS = 32
REL_MAX_DIST = 128
EPS = 1e-6
NEG_INF = -1e30
D_IN = 12544
SEGMENTS = (("xr", 0, 1024, F32), ("g_rg", 1024, 1024, F32), ("q_s", 2048, 1024, BF), ("kv", 3072, 256, BF),
            ("g_swa", 3328, 1024, F32), ("q_m", 4352, 1024, BF), ("g_mem", 5376, 1024, F32), ("gl", 6400, 6144, F32))
SEG_TILE = 256

ADAM_LR, ADAM_B1, ADAM_B2, ADAM_EPS, ADAM_WD, ADAM_STEP = 0.001, 0.9, 0.999, 1e-08, 0.01, 10

NN = (((1,), (0,)), ((), ()))
NT = (((1,), (1,)), ((), ()))
TN = (((0,), (0,)), ((), ()))
MIB = 2 ** 20


def _dot(a, b, dn):
    return lax.dot_general(a, b, dn, preferred_element_type=F32)


VMEM_LIMIT_MIB = 48
VMEM_LIMIT_LARGE_MIB = 56


def _params(sem=None, large=False):
    return pltpu.CompilerParams(dimension_semantics=sem, vmem_limit_bytes=(VMEM_LIMIT_LARGE_MIB if large else VMEM_LIMIT_MIB) * MIB)


def _sigmoid(z):
    return 1.0 / (1.0 + jnp.exp(-z))


def _softplus(z):
    return jnp.maximum(z, 0.0) + jnp.log(1.0 + jnp.exp(-jnp.abs(z)))


def _expm1(z):
    p = z * (1.0 + z * (0.5 + z * (1.0 / 6 + z * (1.0 / 24 + z * (1.0 / 120 + z * (1.0 / 720 + z * (1.0 / 5040 + z / 40320)))))))
    return jnp.where(jnp.abs(z) < 0.3, p, jnp.exp(z) - 1.0)


def _flat(p):
    return 4 * p[0] + 2 * p[1] + p[2]


def _all_gather_relayed(arrs, relay, name):
    n = len(arrs)
    K = 9

    def body(*refs):
        ins, outs = refs[:n], refs[n:2 * n]
        send_sems, recv_sems, local_sems = refs[2 * n:]
        x, y, c = lax.axis_index("x"), lax.axis_index("y"), lax.axis_index("c")
        me, sib = (x, y, c), (x, y, 1 - c)
        xn, yn, dg = (1 - x, y, c), (x, 1 - y, c), (1 - x, 1 - y, c)

        def other(p):
            return (p[0], p[1], 1 - p[2])

        def rows(a, half):
            h = arrs[a].shape[0] // 2
            return pl.ds(half * h, h)

        def copy(a, k, block, to, half=None, src=None):
            dst = outs[a].at[_flat(block)]
            if half is not None:
                dst = dst.at[rows(a, half)]
            return pltpu.make_async_remote_copy(src_ref=dst if src is None else src, dst_ref=dst,
                                                send_sem=send_sems.at[a * K + k], recv_sem=recv_sems.at[a * K + k],
                                                device_id=to, device_id_type=MESH)

        mine = [pltpu.make_async_copy(ins[a], outs[a].at[_flat(me)], local_sems.at[a]) for a in range(n)]
        for cp in mine:
            cp.start()
        sends = []

        def start(cp):
            cp.start()
            sends.append(cp)

        for a in range(n):
            start(copy(a, 1, me, xn, src=ins[a]))
            start(copy(a, 2, me, yn, src=ins[a]))
            if not relay[a]:
                start(copy(a, 3, me, dg, src=ins[a]))
            start(copy(a, 0, me, sib, src=ins[a]))
        for a in range(n):
            copy(a, 1, xn, me).wait_recv()
            if relay[a]:
                start(copy(a, 3, xn, yn, half=0))
            start(copy(a, 5, xn, sib))
        for a in range(n):
            copy(a, 2, yn, me).wait_recv()
            if relay[a]:
                start(copy(a, 4, yn, xn, half=1))
            start(copy(a, 6, yn, sib))
        for a in range(n):
            if relay[a]:
                copy(a, 3, dg, me, half=0).wait_recv()
                start(copy(a, 7, dg, sib, half=0))
                copy(a, 4, dg, me, half=1).wait_recv()
                start(copy(a, 8, dg, sib, half=1))
            else:
                copy(a, 3, dg, me).wait_recv()
                start(copy(a, 7, dg, sib))
        for a in range(n):
            copy(a, 0, sib, me).wait_recv()
            copy(a, 5, other(xn), me).wait_recv()
            copy(a, 6, other(yn), me).wait_recv()
            if relay[a]:
                copy(a, 7, other(dg), me, half=0).wait_recv()
                copy(a, 8, other(dg), me, half=1).wait_recv()
            else:
                copy(a, 7, other(dg), me).wait_recv()
        for cp in sends:
            cp.wait_send()
        for cp in mine:
            cp.wait()

    any_spec = pl.BlockSpec(memory_space=pl.ANY)
    return pl.pallas_call(
        body, name=name,
        out_shape=[jax.ShapeDtypeStruct((N_DEV,) + a.shape, a.dtype) for a in arrs],
        in_specs=[any_spec] * n, out_specs=[any_spec] * n,
        scratch_shapes=[pltpu.SemaphoreType.DMA((K * n,)), pltpu.SemaphoreType.DMA((K * n,)), pltpu.SemaphoreType.DMA((n,))],
    )(*arrs)


def _chip_peers(x, y):
    return [(1 - x, y), (x, 1 - y), (1 - x, 1 - y)]


def _chip(p):
    return 2 * p[0] + p[1]


def _plan_gather(kinds):
    def plan(x, y, c):
        out = []
        for a, kind in enumerate(kinds):
            for peer in [(x, y, 1 - c)] + [(*ch, c) for ch in _chip_peers(x, y)]:
                out.append((a, None, (kind, _flat((x, y, c))), peer, (kind, _flat(peer))))
        return out
    return plan


def _plan_everyone(n):
    def plan(x, y, c):
        out = []
        for a in range(n):
            for r in range(1, N_DEV):
                peer = (1 - x if r & 4 else x, 1 - y if r & 2 else y, 1 - c if r & 1 else c)
                out.append((a, None, ("lead", _flat((x, y, c))), peer, ("lead", _flat(peer))))
        return out
    return plan


def _plan_swap(n):
    def plan(x, y, c):
        return [(a, 1 - c, ("all", 0), (x, y, 1 - c), ("all", 0)) for a in range(n)]
    return plan


def _slot(ref, where):
    kind, k = where
    if kind == "all":
        return ref
    if kind == "lead":
        return ref.at[k]
    return ref.at[:, pl.ds(pl.multiple_of(k * 256, 256), 256)]


def _plan_scatter(n):
    def plan(x, y, c):
        out = []
        for a in range(n):
            for ch in _chip_peers(x, y):
                out.append((a, _chip(ch), ("lead", _chip((x, y))), (*ch, c), ("lead", _chip(ch))))
        return out
    return plan


HBM_SPEC = pl.BlockSpec(memory_space=pltpu.HBM)
SEM_SPEC = pl.BlockSpec(memory_space=pltpu.SEMAPHORE)


def _in_hbm(a):
    return pltpu.with_memory_space_constraint(a, pltpu.HBM)


def _exchange_start(srcs, lands, plan, name):
    n = len(srcs)
    count = len(plan(0, 0, 0))

    def body(*refs):
        src_refs, land_refs = refs[:n], refs[n:2 * n]
        send_sems, recv_sems = refs[2 * n], refs[2 * n + 1]
        token = refs[-1]
        x, y, c = lax.axis_index("x"), lax.axis_index("y"), lax.axis_index("c")
        for k, (a, si, di, peer, _) in enumerate(plan(x, y, c)):
            src = src_refs[a] if si is None else src_refs[a].at[si]
            pltpu.make_async_remote_copy(src_ref=src, dst_ref=_slot(land_refs[a], di), send_sem=send_sems.at[k],
                                         recv_sem=recv_sems.at[k], device_id=peer, device_id_type=MESH).start()
        token[...] = jnp.zeros_like(token)

    out = pl.pallas_call(
        body, name=name,
        out_shape=(pltpu.SemaphoreType.DMA((count,)), pltpu.SemaphoreType.DMA((count,)),
                   *[pltpu.HBM(a.shape, a.dtype) for a in lands], jax.ShapeDtypeStruct((8, 128), F32)),
        in_specs=[HBM_SPEC] * (2 * n),
        out_specs=(SEM_SPEC, SEM_SPEC, *([HBM_SPEC] * n), pl.BlockSpec(memory_space=pltpu.VMEM)),
        input_output_aliases={n + i: 2 + i for i in range(n)},
        compiler_params=pltpu.CompilerParams(has_side_effects=pltpu.SideEffectType.DATAFLOW_SIDE_EFFECTING),
    )(*[_in_hbm(a) for a in srcs], *[_in_hbm(a) for a in lands])
    return out[0], out[1], list(srcs), list(out[2:2 + n]), out[-1]


def _exchange_wait(send_sems, recv_sems, srcs, lands, plan, after, name):
    n = len(srcs)

    def body(*refs):
        src_refs, land_refs = refs[:n], refs[n:2 * n]
        send_sems, recv_sems = refs[2 * n], refs[2 * n + 1]
        x, y, c = lax.axis_index("x"), lax.axis_index("y"), lax.axis_index("c")
        for k, (a, si, _, peer, ri) in enumerate(plan(x, y, c)):
            src = src_refs[a] if si is None else src_refs[a].at[si]
            cp = pltpu.make_async_remote_copy(src_ref=src, dst_ref=_slot(land_refs[a], ri), send_sem=send_sems.at[k],
                                              recv_sem=recv_sems.at[k], device_id=peer, device_id_type=MESH)
            cp.wait_send()
            cp.wait_recv()

    out = pl.pallas_call(
        body, name=name,
        out_shape=tuple(pltpu.HBM(a.shape, a.dtype) for a in lands),
        in_specs=[HBM_SPEC] * (2 * n) + [SEM_SPEC, SEM_SPEC, pl.BlockSpec(memory_space=pl.ANY)],
        out_specs=tuple([HBM_SPEC] * n),
        input_output_aliases={n + i: i for i in range(n)},
        compiler_params=pltpu.CompilerParams(has_side_effects=pltpu.SideEffectType.DATAFLOW_SIDE_EFFECTING),
    )(*[_in_hbm(a) for a in srcs], *lands, send_sems, recv_sems, after)
    return list(out)


def _forward_to_sibling(lands, kinds, name):
    n = len(lands)

    def body(*refs):
        in_refs, out_refs = refs[:n], refs[n:2 * n]
        send_sems, recv_sems = refs[2 * n:]
        x, y, c = lax.axis_index("x"), lax.axis_index("y"), lax.axis_index("c")
        sibling = (x, y, 1 - c)

        def copy(a, j, slot):
            return pltpu.make_async_remote_copy(src_ref=_slot(in_refs[a], (kinds[a], slot)), dst_ref=_slot(out_refs[a], (kinds[a], slot)),
                                                send_sem=send_sems.at[a * 3 + j], recv_sem=recv_sems.at[a * 3 + j],
                                                device_id=sibling, device_id_type=MESH)

        sends = [copy(a, j, _flat((*ch, c))) for a in range(n) for j, ch in enumerate(_chip_peers(x, y))]
        for cp in sends:
            cp.start()
        for a in range(n):
            for j, ch in enumerate(_chip_peers(x, y)):
                copy(a, j, _flat((*ch, 1 - c))).wait_recv()
        for cp in sends:
            cp.wait_send()

    any_spec = pl.BlockSpec(memory_space=pl.ANY)
    return pl.pallas_call(
        body, name=name, out_shape=[jax.ShapeDtypeStruct(a.shape, a.dtype) for a in lands],
        in_specs=[any_spec] * n, out_specs=[any_spec] * n, input_output_aliases={a: a for a in range(n)},
        scratch_shapes=[pltpu.SemaphoreType.DMA((3 * n,)), pltpu.SemaphoreType.DMA((3 * n,))],
    )(*lands)


def _place_own(zones, owns, slot, name):
    n = len(zones)

    def body(slot_ref, *refs):
        for a in range(n):
            refs[2 * n + a][...] = refs[a][...]

    return pl.pallas_call(
        body, name=name,
        grid_spec=pltpu.PrefetchScalarGridSpec(
            num_scalar_prefetch=1, grid=(1,),
            in_specs=[pl.BlockSpec(o.shape, lambda i, s_ref: (0, 0)) for o in owns] + [pl.BlockSpec(memory_space=pl.ANY)] * n,
            out_specs=[pl.BlockSpec((None,) + o.shape, lambda i, s_ref: (s_ref[0], 0, 0)) for o in owns]),
        out_shape=[jax.ShapeDtypeStruct(z.shape, z.dtype) for z in zones],
        input_output_aliases={1 + n + a: a for a in range(n)},
        compiler_params=_params(("arbitrary",)),
    )(slot, *owns, *zones)


def _pair_sum(parts, got, core, name):
    _, _, R, C = parts.shape
    tr = 256 if R % 256 == 0 else R

    def body(c_ref, p_ref, g_ref, o_ref):
        o_ref[...] = (p_ref[...].astype(F32) + g_ref[...].astype(F32)).astype(o_ref.dtype)

    return pl.pallas_call(
        body, name=name,
        grid_spec=pltpu.PrefetchScalarGridSpec(
            num_scalar_prefetch=1, grid=(4, R // tr),
            in_specs=[pl.BlockSpec((None, None, tr, C), lambda j, i, c_ref: (c_ref[0], j, i, 0)),
                      pl.BlockSpec((None, tr, C), lambda j, i, c_ref: (j, i, 0))],
            out_specs=pl.BlockSpec((None, tr, C), lambda j, i, c_ref: (j, i, 0))),
        out_shape=jax.ShapeDtypeStruct((4, R, C), parts.dtype),
        compiler_params=_params(("parallel", "parallel")),
    )(core, parts, got)


def _matmul(a, b, mode, M, N, K, tm, tn, tk, out_dtype, name, b_noff=0, a_moff=0, a_blocked=False, b_blocked=False,
            out_blocked=None, dep=None):
    nm, nn, nk = M // tm, N // tn, K // tk
    if mode == "nn":
        a_spec = pl.BlockSpec((tm, tk), lambda j, i, k: (i, k))
        b_spec = pl.BlockSpec((tk, tn), lambda j, i, k: (k, j + b_noff))
        dn = NN
    elif mode == "nt":
        a_spec = pl.BlockSpec((tm, tk), lambda j, i, k: (i, k))
        if b_blocked:
            b_spec = pl.BlockSpec((None, tn, tk), lambda j, i, k: (k, j, 0))
        else:
            b_spec = pl.BlockSpec((tn, tk), lambda j, i, k: (j + b_noff, k))
        dn = NT
    else:
        if a_blocked:
            a_spec = pl.BlockSpec((None, tk, tm), lambda j, i, k: (i, k, 0))
        else:
            a_spec = pl.BlockSpec((tk, tm), lambda j, i, k: (k, i + a_moff))
        if b_blocked:
            b_spec = pl.BlockSpec((None, tk, tn), lambda j, i, k: (j, k, 0))
        else:
            b_spec = pl.BlockSpec((tk, tn), lambda j, i, k: (k, j + b_noff))
        dn = TN
    if out_blocked == "col":
        out_shape = jax.ShapeDtypeStruct((2, 4, M, tn), out_dtype)
        out_spec = pl.BlockSpec((None, None, tm, tn), lambda j, i, k: (j % 2, j // 2, i, 0))
    elif out_blocked == "row":
        out_shape = jax.ShapeDtypeStruct((2, 4, tm, N), out_dtype)
        out_spec = pl.BlockSpec((None, None, tm, tn), (lambda j, i, k: (i % 2, i // 2, 0, j)))
    elif out_blocked == "third":
        out_shape = jax.ShapeDtypeStruct((3, M, N // 3), out_dtype)
        out_spec = pl.BlockSpec((None, tm, tn), lambda j, i, k: (j // (nn // 3), i, j % (nn // 3)))
    else:
        out_shape = jax.ShapeDtypeStruct((M, N), out_dtype)
        out_spec = pl.BlockSpec((tm, tn), lambda j, i, k: (i, j))

    n_extra = int(dep is not None)

    def body(a_ref, b_ref, *rest):
        o_ref, scratch = rest[n_extra], rest[n_extra + 1:]
        if nk == 1:
            o_ref[...] = _dot(a_ref[...], b_ref[...], dn).astype(out_dtype)
        else:
            acc_ref, = scratch
            k = pl.program_id(2)

            @pl.when(k == 0)
            def _():
                acc_ref[...] = jnp.zeros_like(acc_ref)

            acc_ref[...] += _dot(a_ref[...], b_ref[...], dn)

            @pl.when(k == nk - 1)
            def _():
                o_ref[...] = acc_ref[...].astype(out_dtype)

    return pl.pallas_call(
        body, name=name, grid=(nn, nm, nk),
        in_specs=[a_spec, b_spec] + ([] if dep is None else [pl.BlockSpec((8, 128), lambda j, i, k: (0, 0))]),
        out_specs=out_spec, out_shape=out_shape,
        scratch_shapes=[] if nk == 1 else [pltpu.VMEM((tm, tn), F32)],
        compiler_params=_params(("parallel", "parallel", "arbitrary")),
    )(a, b, *([] if dep is None else [dep]))


def _rms_fwd(x, g, name):
    R, Dm = x.shape
    tr = min(R, 256)

    def body(x_ref, g_ref, h_ref):
        xv = x_ref[...]
        r = lax.rsqrt(jnp.mean(xv * xv, axis=-1, keepdims=True) + EPS)
        h_ref[...] = (xv * r * g_ref[...]).astype(BF)

    return pl.pallas_call(
        body, name=name, grid=(R // tr,),
        in_specs=[pl.BlockSpec((tr, Dm), lambda i: (i, 0)), pl.BlockSpec((1, Dm), lambda i: (0, 0))],
        out_specs=pl.BlockSpec((tr, Dm), lambda i: (i, 0)), out_shape=jax.ShapeDtypeStruct((R, Dm), BF),
        compiler_params=_params(("parallel",)),
    )(x, g)


def _rms_gain_grad(dn, x, name):
    R, Dm = x.shape

    def body(dn_ref, x_ref, o_ref):
        xv = x_ref[...]
        r = lax.rsqrt(jnp.mean(xv * xv, axis=-1, keepdims=True) + EPS)
        o_ref[...] = jnp.sum(dn_ref[...] * xv * r, axis=0, keepdims=True)

    return pl.pallas_call(
        body, name=name, out_shape=jax.ShapeDtypeStruct((1, Dm), F32),
        compiler_params=_params(),
    )(dn, x)


def _shift_down(v, k, head8, row, T):
    if k == 0:
        return v
    r = pltpu.roll(v, k, 0)
    hr = pltpu.roll(head8, k, 0)
    top = jnp.where(row[:8] < k, hr, r[:8])
    return jnp.concatenate([top, r[8:]], axis=0)


def _shift_up(v, k, tail8, row, T):
    if k == 0:
        return v
    r = pltpu.roll(v, T - k, 0)
    tr = pltpu.roll(tail8, 8 - k, 0)
    bot = jnp.where(row[:8] >= 8 - k, tr, r[T - 8:])
    return jnp.concatenate([r[:T - 8], bot], axis=0)


def _rglru_gates(u, head8, grow, row, T, cw_ref, cb_ref, wa_ref, ba_ref, wx_ref, bx_ref, lam_ref):
    us = [_shift_down(u, k, head8, row, T) for k in range(CONV_W)]
    acc = us[0] * cw_ref[0:1, :]
    for k in range(1, CONV_W):
        acc = acc + us[k] * cw_ref[k:k + 1, :]
    conv = cb_ref[...] + acc
    cbf = conv.astype(BF)
    r_ = _sigmoid(_dot(cbf, wa_ref[0], NN) + ba_ref[...])
    i_ = _sigmoid(_dot(cbf, wx_ref[0], NN) + bx_ref[...])
    sp = _softplus(-lam_ref[...])
    la = -LRU_C * r_ * sp
    a = jnp.exp(la)
    mult_raw = jnp.sqrt(-_expm1(2.0 * la))
    mult = jnp.where(grow == 0, 1.0, mult_raw)
    return us, conv, cbf, r_, i_, sp, a, mult_raw, mult


def _rglru_specs(T, nt, rev):
    tmap = (lambda n, t: (nt - 1 - t, n)) if rev else (lambda n, t: (t, n))
    hmap = ((lambda n, t: (jnp.maximum((nt - 1 - t) * (T // 8) - 1, 0), n)) if rev
            else (lambda n, t: (jnp.maximum(t * (T // 8) - 1, 0), n)))
    tile = pl.BlockSpec((T, RNN_BLOCK), tmap)
    halo = pl.BlockSpec((8, RNN_BLOCK), hmap)
    vec = pl.BlockSpec((1, RNN_BLOCK), lambda n, t: (0, n))
    cw = pl.BlockSpec((CONV_W, RNN_BLOCK), lambda n, t: (0, n))
    wblk = pl.BlockSpec((1, RNN_BLOCK, RNN_BLOCK), lambda n, t: (n, 0, 0))
    return tile, halo, vec, cw, wblk


def _rglru_fwd(xr, g, cw, cb, wa, ba, wx, bx, lam, T):
    S = xr.shape[0]
    nt = S // T

    def body(u_ref, uh_ref, g_ref, cw_ref, cb_ref, wa_ref, ba_ref, wx_ref, bx_ref, lam_ref, h_ref, y_ref, carry):
        t = pl.program_id(1)

        @pl.when(t == 0)
        def _():
            carry[...] = jnp.zeros_like(carry)

        row = lax.broadcasted_iota(jnp.int32, (T, RNN_BLOCK), 0)
        grow = row + t * T
        head8 = jnp.where(t > 0, uh_ref[...], 0.0)
        _, conv, _, _, i_, _, a, _, mult = _rglru_gates(u_ref[...], head8, grow, row, T, cw_ref, cb_ref, wa_ref, ba_ref,
                                                         wx_ref, bx_ref, lam_ref)
        b = mult * i_ * conv
        s = 1
        while s < T:
            keep = row >= s
            a_s = jnp.where(keep, pltpu.roll(a, s, 0), 1.0)
            b_s = jnp.where(keep, pltpu.roll(b, s, 0), 0.0)
            b = a * b_s + b
            a = a * a_s
            s *= 2
        h = b + a * carry[0:1, :]
        carry[...] = jnp.broadcast_to(h[T - 1:T, :], carry.shape)
        h_ref[...] = h
        gv = g_ref[...]
        y_ref[...] = (h * (gv * _sigmoid(gv))).astype(BF)

    tile, halo, vec, cwspec, wblk = _rglru_specs(T, nt, False)
    return pl.pallas_call(
        body, name="rglru_fwd", grid=(RNN_BLOCKS, nt),
        in_specs=[tile, halo, tile, cwspec, vec, wblk, vec, wblk, vec, vec],
        out_specs=[tile, tile],
        out_shape=[jax.ShapeDtypeStruct((S, D_RNN), F32), jax.ShapeDtypeStruct((S, D_RNN), BF)],
        scratch_shapes=[pltpu.VMEM((8, RNN_BLOCK), F32)],
        compiler_params=_params(("parallel", "arbitrary")),
    )(xr, xr, g, cw, cb, wa, ba, wx, bx, lam)


def _rglru_bwd(xr, g, h, dy, cw, cb, wa, ba, wx, bx, lam, T):
    S = xr.shape[0]
    nt = S // T

    def body(u_ref, uh_ref, g_ref, h_ref, hh_ref, dy_ref, cw_ref, cb_ref, wa_ref, ba_ref, wx_ref, bx_ref, lam_ref,
             du_ref, dg_ref, dwa_ref, dwx_ref, dvec_ref, c_dhh, c_a, c_dconv):
        t = pl.program_id(1)
        tt = nt - 1 - t

        @pl.when(t == 0)
        def _():
            c_dhh[...] = jnp.zeros_like(c_dhh)
            c_a[...] = jnp.zeros_like(c_a)
            c_dconv[...] = jnp.zeros_like(c_dconv)
            dwa_ref[...] = jnp.zeros_like(dwa_ref)
            dwx_ref[...] = jnp.zeros_like(dwx_ref)
            dvec_ref[...] = jnp.zeros_like(dvec_ref)

        row = lax.broadcasted_iota(jnp.int32, (T, RNN_BLOCK), 0)
        row8 = row[:8]
        grow = row + tt * T
        head8 = jnp.where(tt > 0, uh_ref[...], 0.0)
        us, conv, cbf, r_, i_, sp, a, mult_raw, mult = _rglru_gates(
            u_ref[...], head8, grow, row, T, cw_ref, cb_ref, wa_ref, ba_ref, wx_ref, bx_ref, lam_ref)
        hv = h_ref[...]
        hprev = _shift_down(hv, 1, jnp.where(tt > 0, hh_ref[...], 0.0), row, T)
        gv = g_ref[...]
        sg = _sigmoid(gv)
        dyv = dy_ref[...]
        dg_ref[...] = (dyv * hv * (sg * (1.0 + gv * (1.0 - sg)))).astype(BF)
        d = dyv * (gv * sg)
        A = _shift_up(a, 1, c_a[...], row, T)
        s = 1
        while s < T:
            keep = row < T - s
            A_s = jnp.where(keep, pltpu.roll(A, T - s, 0), 1.0)
            d_s = jnp.where(keep, pltpu.roll(d, T - s, 0), 0.0)
            d = A * d_s + d
            A = A * A_s
            s *= 2
        dhh = d + A * c_dhh[0:1, :]
        da = dhh * hprev
        dconv = dhh * mult * i_
        di = dhh * mult * conv
        dmult = dhh * i_ * conv
        dla = da * a - jnp.where(grow == 0, 0.0, dmult * (a * a) / mult_raw)
        dr = dla * (-LRU_C * sp)
        dsp = jnp.sum(dla * (-LRU_C * r_), axis=0, keepdims=True)
        dza = dr * r_ * (1.0 - r_)
        dzx = di * i_ * (1.0 - i_)
        dza_b, dzx_b = dza.astype(BF), dzx.astype(BF)
        dconv = dconv + _dot(dza_b, wa_ref[0], NT) + _dot(dzx_b, wx_ref[0], NT)
        dwa_ref[0] += _dot(cbf, dza_b, TN)
        dwx_ref[0] += _dot(cbf, dzx_b, TN)
        lam = lam_ref[...]
        rows = [jnp.sum(dconv * us[k], axis=0, keepdims=True) for k in range(CONV_W)]
        rows += [jnp.sum(dconv, axis=0, keepdims=True), jnp.sum(dza, axis=0, keepdims=True),
                 jnp.sum(dzx, axis=0, keepdims=True), dsp * (-_sigmoid(-lam))]
        upd = jnp.zeros((8, RNN_BLOCK), F32)
        for j, rv in enumerate(rows):
            upd = upd + jnp.where(row8 == j, rv, 0.0)
        dvec_ref[...] += upd
        tail8 = c_dconv[...]
        du = dconv * cw_ref[0:1, :]
        for k in range(1, CONV_W):
            du = du + _shift_up(dconv, k, tail8, row, T) * cw_ref[k:k + 1, :]
        du_ref[...] = du.astype(BF)
        c_dhh[...] = jnp.broadcast_to(dhh[0:1, :], c_dhh.shape)
        c_a[...] = jnp.broadcast_to(a[0:1, :], c_a.shape)
        c_dconv[...] = dconv[:8]

    tile, halo, vec, cwspec, wblk = _rglru_specs(T, nt, True)
    acc8 = pl.BlockSpec((8, RNN_BLOCK), lambda n, t: (0, n))
    return pl.pallas_call(
        body, name="rglru_bwd", grid=(RNN_BLOCKS, nt),
        in_specs=[tile, halo, tile, tile, halo, tile, cwspec, vec, wblk, vec, wblk, vec, vec],
        out_specs=[tile, tile, wblk, wblk, acc8],
        out_shape=[jax.ShapeDtypeStruct((S, D_RNN), BF), jax.ShapeDtypeStruct((S, D_RNN), BF),
                   jax.ShapeDtypeStruct((RNN_BLOCKS, RNN_BLOCK, RNN_BLOCK), F32),
                   jax.ShapeDtypeStruct((RNN_BLOCKS, RNN_BLOCK, RNN_BLOCK), F32),
                   jax.ShapeDtypeStruct((8, D_RNN), F32)],
        scratch_shapes=[pltpu.VMEM((8, RNN_BLOCK), F32)] * 3,
        compiler_params=_params(("parallel", "arbitrary")),
    )(xr, xr, g, h, h, dy, cw, cb, wa, ba, wx, bx, lam)


def _rel_bucket_map():
    qi = np.arange(WINDOW)[:, None]
    kj = np.arange(2 * WINDOW)[None, :]
    dist = jnp.asarray(qi + WINDOW - kj, jnp.int32)
    n = jnp.maximum(dist, 0)
    max_exact = REL_BUCKETS // 2
    ratio = jnp.log(jnp.maximum(n, 1).astype(F32) / max_exact) / math.log(REL_MAX_DIST / max_exact)
    large = jnp.minimum(max_exact + (ratio * (REL_BUCKETS - max_exact)).astype(jnp.int32), REL_BUCKETS - 1)
    bucket = jnp.where(n < max_exact, n, large).astype(jnp.int32)
    j = np.arange(WINDOW)[None, :]
    return jnp.where(jnp.asarray(j > qi), bucket[:, :WINDOW], bucket[:, WINDOW:])


def _swa_common(n, kv_ref, bucket_ref, relb_ref, bias_scr):
    @pl.when(n == 0)
    def _():
        bk = bucket_ref[...]
        for h in range(SWA_HEADS):
            acc = jnp.zeros((WINDOW, WINDOW), F32)
            for b in range(REL_BUCKETS):
                acc = acc + jnp.where(bk == b, relb_ref[b, h], 0.0)
            bias_scr[h] = acc

    prev0 = pl.multiple_of(jnp.maximum(n - 1, 0) * WINDOW, WINDOW)
    cur0 = pl.multiple_of(n * WINDOW, WINDOW)
    kk = jnp.concatenate([kv_ref[pl.ds(prev0, WINDOW), :], kv_ref[pl.ds(cur0, WINDOW), :]], axis=0).astype(F32)
    rowi = lax.broadcasted_iota(jnp.int32, (WINDOW, WINDOW), 0)
    col = lax.broadcasted_iota(jnp.int32, (WINDOW, WINDOW), 1)
    from_prev = col > rowi
    return kk, from_prev, prev0, cur0


def _fold(full, from_prev):
    return jnp.where(from_prev, full[:, :WINDOW], full[:, WINDOW:])


def _unfold(sq, from_prev):
    return jnp.concatenate([jnp.where(from_prev, sq, 0.0), jnp.where(from_prev, 0.0, sq)], axis=1)


def _half_pair(part, kvh):
    lo = lax.broadcasted_iota(jnp.int32, part.shape, 1) < SWA_HD
    if kvh == 0:
        pa = jnp.where(lo, part, 0.0)
        pb = pltpu.roll(pa, SWA_HD, 1)
    else:
        pb = jnp.where(lo, 0.0, part)
        pa = pltpu.roll(pb, SWA_HD, 1)
    return pa.astype(BF), pb.astype(BF)


ALL_HEADS = SWA_HEADS * WINDOW


def _sink_column(sinks):
    return jnp.repeat(sinks.reshape(SWA_HEADS), WINDOW).reshape(ALL_HEADS, 1)


def _swa_operands(kk):
    return [(_half_pair(kk[:, :128], kvh), _half_pair(kk[:, 128:], kvh)) for kvh in range(SWA_KV_HEADS)]


def _swa_probs(n, q_ref, ops, bias_scr, sinkc_ref, from_prev):
    lgs = []
    for kvh in range(SWA_KV_HEADS):
        (ka, kb), _ = ops[kvh]
        for p in range(4):
            q2 = q_ref[:, kvh * 512 + p * 128:kvh * 512 + p * 128 + 128]
            lgs += [_fold(_dot(q2, ka, NT), from_prev), _fold(_dot(q2, kb, NT), from_prev)]
    lg = jnp.concatenate(lgs, axis=0) * (SWA_HD ** -0.5) + bias_scr[...].reshape(ALL_HEADS, WINDOW)
    rowi = jnp.bitwise_and(lax.broadcasted_iota(jnp.int32, (ALL_HEADS, WINDOW), 0), WINDOW - 1)
    col = lax.broadcasted_iota(jnp.int32, (ALL_HEADS, WINDOW), 1)
    no_prev = jnp.where(n > 0, 0, 4 * WINDOW)
    lg = jnp.where(jnp.logical_or(col <= rowi, col > rowi + no_prev), lg, NEG_INF)
    sink = sinkc_ref[...]
    m = jnp.maximum(jnp.max(lg, axis=-1, keepdims=True), sink)
    e = jnp.exp(lg - m)
    es = jnp.exp(sink - m)
    den = jnp.sum(e, axis=-1, keepdims=True) + es
    return e / den, es / den


def _swa_fwd(q, kv, g, bucket, rel_bias, sink_col):
    S = q.shape[0]
    nb = S // WINDOW

    def body(q_ref, kv_ref, g_ref, bucket_ref, relb_ref, sinkc_ref, o_ref, y_ref, bias_scr):
        n = pl.program_id(0)
        kk, from_prev, _, _ = _swa_common(n, kv_ref, bucket_ref, relb_ref, bias_scr)
        ops = _swa_operands(kk)
        pr, _ = _swa_probs(n, q_ref, ops, bias_scr, sinkc_ref, from_prev)
        for kvh in range(SWA_KV_HEADS):
            _, (va, vb) = ops[kvh]
            for p in range(4):
                c0 = kvh * 512 + p * 128
                r0 = (kvh * 8 + 2 * p) * WINDOW
                o2 = (_dot(_unfold(pr[r0:r0 + WINDOW], from_prev).astype(BF), va, NN)
                      + _dot(_unfold(pr[r0 + WINDOW:r0 + 2 * WINDOW], from_prev).astype(BF), vb, NN))
                o_ref[:, c0:c0 + 128] = o2
                gv = g_ref[:, c0:c0 + 128]
                y_ref[:, c0:c0 + 128] = (o2 * (gv * _sigmoid(gv))).astype(BF)

    blk = pl.BlockSpec((WINDOW, 1024), lambda n: (n, 0))
    smem = pl.BlockSpec(memory_space=pltpu.SMEM)
    sinkc = pl.BlockSpec((ALL_HEADS, 1), lambda n: (0, 0))
    return pl.pallas_call(
        body, name="swa_fwd", grid=(nb,),
        in_specs=[blk, pl.BlockSpec((S, 256), lambda n: (0, 0)), blk, pl.BlockSpec((WINDOW, WINDOW), lambda n: (0, 0)), smem, sinkc],
        out_specs=[blk, blk],
        out_shape=[jax.ShapeDtypeStruct((S, 1024), F32), jax.ShapeDtypeStruct((S, 1024), BF)],
        scratch_shapes=[pltpu.VMEM((SWA_HEADS, WINDOW, WINDOW), F32)],
        compiler_params=_params(("arbitrary",)),
    )(q, kv, g, bucket, rel_bias, sink_col)


def _swa_bwd(q, kv, g, o, dy, bucket, rel_bias, sink_col):
    S = q.shape[0]
    nb = S // WINDOW

    def body(q_ref, kv_ref, g_ref, o_ref, dy_ref, bucket_ref, relb_ref, sinkc_ref,
             dq_ref, dg_ref, dkv_ref, dsink_ref, drel_ref, bias_scr, dbias_scr, dsink_scr):
        n = pl.program_id(0)

        @pl.when(n == 0)
        def _():
            dbias_scr[...] = jnp.zeros_like(dbias_scr)
            dsink_scr[...] = jnp.zeros_like(dsink_scr)
            dkv_ref[...] = jnp.zeros_like(dkv_ref)

        kk, from_prev, prev0, cur0 = _swa_common(n, kv_ref, bucket_ref, relb_ref, bias_scr)
        ops = _swa_operands(kk)
        pr, ps = _swa_probs(n, q_ref, ops, bias_scr, sinkc_ref, from_prev)
        do2s, dps = [], []
        for kvh in range(SWA_KV_HEADS):
            _, (va, vb) = ops[kvh]
            for p in range(4):
                c0 = kvh * 512 + p * 128
                gv = g_ref[:, c0:c0 + 128]
                sg = _sigmoid(gv)
                dyv = dy_ref[:, c0:c0 + 128]
                dg_ref[:, c0:c0 + 128] = (dyv * o_ref[:, c0:c0 + 128] * (sg * (1.0 + gv * (1.0 - sg)))).astype(BF)
                do2 = (dyv * (gv * sg)).astype(BF)
                do2s.append(do2)
                dps += [_fold(_dot(do2, va, NT), from_prev), _fold(_dot(do2, vb, NT), from_prev)]
        dp = jnp.concatenate(dps, axis=0)
        delta = jnp.sum(pr * dp, axis=-1, keepdims=True)
        ds = pr * (dp - delta)
        dbias_scr[...] += ds.reshape(SWA_HEADS, WINDOW, WINDOW)
        dsink_scr[...] += ps * delta
        dsc = ds * (SWA_HD ** -0.5)
        lo256 = lax.broadcasted_iota(jnp.int32, (2 * WINDOW, 128), 1) < SWA_HD
        dks, dvs = [], []
        for kvh in range(SWA_KV_HEADS):
            (ka, kb), _ = ops[kvh]
            dka = jnp.zeros((2 * WINDOW, 128), F32)
            dkb, dva, dvb = dka, dka, dka
            for p in range(4):
                c0 = kvh * 512 + p * 128
                r0 = (kvh * 8 + 2 * p) * WINDOW
                q2 = q_ref[:, c0:c0 + 128]
                do2 = do2s[kvh * 4 + p]
                ds0 = _unfold(dsc[r0:r0 + WINDOW], from_prev).astype(BF)
                ds1 = _unfold(dsc[r0 + WINDOW:r0 + 2 * WINDOW], from_prev).astype(BF)
                dq_ref[:, c0:c0 + 128] = (_dot(ds0, ka, NN) + _dot(ds1, kb, NN)).astype(BF)
                dka = dka + _dot(ds0, q2, TN)
                dkb = dkb + _dot(ds1, q2, TN)
                dva = dva + _dot(_unfold(pr[r0:r0 + WINDOW], from_prev).astype(BF), do2, TN)
                dvb = dvb + _dot(_unfold(pr[r0 + WINDOW:r0 + 2 * WINDOW], from_prev).astype(BF), do2, TN)
            dks.append(jnp.where(lo256, dka, 0.0) + pltpu.roll(jnp.where(lo256, 0.0, dkb), SWA_HD, 1))
            dvs.append(jnp.where(lo256, dva, 0.0) + pltpu.roll(jnp.where(lo256, 0.0, dvb), SWA_HD, 1))
        dk = dks[0] + pltpu.roll(dks[1], SWA_HD, 1)
        dv = dvs[0] + pltpu.roll(dvs[1], SWA_HD, 1)
        dkv_ref[pl.ds(prev0, WINDOW), 0:128] += dk[:WINDOW]
        dkv_ref[pl.ds(prev0, WINDOW), 128:256] += dv[:WINDOW]
        dkv_ref[pl.ds(cur0, WINDOW), 0:128] += dk[WINDOW:]
        dkv_ref[pl.ds(cur0, WINDOW), 128:256] += dv[WINDOW:]

        @pl.when(n == nb - 1)
        def _():
            dsink_ref[...] = -jnp.sum(dsink_scr[...].reshape(SWA_HEADS, WINDOW, 1), axis=1)
            bk = bucket_ref[...]
            sums = []
            for b in range(REL_BUCKETS):
                sums.append(jnp.sum(jnp.where((bk == b)[None], dbias_scr[...], 0.0), axis=1))
            drel_ref[...] = jnp.sum(jnp.concatenate(sums, axis=0), axis=1, keepdims=True)

    blk = pl.BlockSpec((WINDOW, 1024), lambda n: (n, 0))
    smem = pl.BlockSpec(memory_space=pltpu.SMEM)
    whole = lambda shape: pl.BlockSpec(shape, lambda n: (0, 0))
    return pl.pallas_call(
        body, name="swa_bwd", grid=(nb,),
        in_specs=[blk, whole((S, 256)), blk, blk, blk, whole((WINDOW, WINDOW)), smem, whole((ALL_HEADS, 1))],
        out_specs=[blk, blk, whole((S, 256)), whole((SWA_HEADS, 1)), whole((REL_BUCKETS * SWA_HEADS, 1))],
        out_shape=[jax.ShapeDtypeStruct((S, 1024), BF), jax.ShapeDtypeStruct((S, 1024), BF),
                   jax.ShapeDtypeStruct((S, 256), F32), jax.ShapeDtypeStruct((SWA_HEADS, 1), F32),
                   jax.ShapeDtypeStruct((REL_BUCKETS * SWA_HEADS, 1), F32)],
        scratch_shapes=[pltpu.VMEM((SWA_HEADS, WINDOW, WINDOW), F32), pltpu.VMEM((SWA_HEADS, WINDOW, WINDOW), F32),
                        pltpu.VMEM((ALL_HEADS, 1), F32)],
        compiler_params=_params(("arbitrary",)),
    )(q, kv, g, o, dy, bucket, rel_bias, sink_col)


def _mem_probs(qh, mk):
    lg = _dot(qh, mk, NT) * (MEM_HD ** -0.5)
    e = jnp.exp(lg - jnp.max(lg, axis=-1, keepdims=True))
    return e / jnp.sum(e, axis=-1, keepdims=True)


def _mem_fwd(q, mkv, g):
    S = q.shape[0]
    M = mkv.shape[0]
    tq = 256

    def body(q_ref, mkv_ref, g_ref, o_ref, y_ref):
        for h in range(MEM_HEADS):
            c0 = h * MEM_HD
            pr = _mem_probs(q_ref[:, c0:c0 + MEM_HD], mkv_ref[:, c0:c0 + MEM_HD])
            o = _dot(pr.astype(BF), mkv_ref[:, D_MEM + c0:D_MEM + c0 + MEM_HD], NN)
            o_ref[:, c0:c0 + MEM_HD] = o
            gv = g_ref[:, c0:c0 + MEM_HD]
            y_ref[:, c0:c0 + MEM_HD] = (o * (gv * _sigmoid(gv))).astype(BF)

    blk = pl.BlockSpec((tq, D_MEM), lambda i: (i, 0))
    return pl.pallas_call(
        body, name="mem_fwd", grid=(S // tq,),
        in_specs=[blk, pl.BlockSpec((M, 2 * D_MEM), lambda i: (0, 0)), blk], out_specs=[blk, blk],
        out_shape=[jax.ShapeDtypeStruct((S, D_MEM), F32), jax.ShapeDtypeStruct((S, D_MEM), BF)],
        compiler_params=_params(("parallel",)),
    )(q, mkv, g)


def _mem_bwd(q, mkv, g, o, dy):
    S = q.shape[0]
    M = mkv.shape[0]
    tq = 256

    def body(q_ref, mkv_ref, g_ref, o_ref, dy_ref, dq_ref, dg_ref, dmkv_ref):
        @pl.when(pl.program_id(0) == 0)
        def _():
            dmkv_ref[...] = jnp.zeros_like(dmkv_ref)

        for h in range(MEM_HEADS):
            c0 = h * MEM_HD
            qh = q_ref[:, c0:c0 + MEM_HD]
            mk = mkv_ref[:, c0:c0 + MEM_HD]
            mv = mkv_ref[:, D_MEM + c0:D_MEM + c0 + MEM_HD]
            gv = g_ref[:, c0:c0 + MEM_HD]
            sg = _sigmoid(gv)
            dyv = dy_ref[:, c0:c0 + MEM_HD]
            dg_ref[:, c0:c0 + MEM_HD] = (dyv * o_ref[:, c0:c0 + MEM_HD] * (sg * (1.0 + gv * (1.0 - sg)))).astype(BF)
            do = (dyv * (gv * sg)).astype(BF)
            pr = _mem_probs(qh, mk)
            dp = _dot(do, mv, NT)
            ds = pr * (dp - jnp.sum(pr * dp, axis=-1, keepdims=True))
            dsb = (ds * (MEM_HD ** -0.5)).astype(BF)
            dq_ref[:, c0:c0 + MEM_HD] = _dot(dsb, mk, NN).astype(BF)
            dmkv_ref[:, c0:c0 + MEM_HD] += _dot(dsb, qh, TN)
            dmkv_ref[:, D_MEM + c0:D_MEM + c0 + MEM_HD] += _dot(pr.astype(BF), do, TN)

    blk = pl.BlockSpec((tq, D_MEM), lambda i: (i, 0))
    whole = pl.BlockSpec((M, 2 * D_MEM), lambda i: (0, 0))
    return pl.pallas_call(
        body, name="mem_bwd", grid=(S // tq,),
        in_specs=[blk, whole, blk, blk, blk], out_specs=[blk, blk, whole],
        out_shape=[jax.ShapeDtypeStruct((S, D_MEM), BF), jax.ShapeDtypeStruct((S, D_MEM), BF),
                   jax.ShapeDtypeStruct((M, 2 * D_MEM), F32)],
        compiler_params=_params(("arbitrary",)),
    )(q, mkv, g, o, dy)


MERGE_TN = 512


def _merge_specs(tm):
    ytile = pl.BlockSpec((tm, 1024), lambda i, j: (i, 0))
    wblk = pl.BlockSpec((MERGE_TN, 1024), lambda i, j: (j, 0))
    gls = [pl.BlockSpec((None, tm, MERGE_TN), (lambda i, j, br=br: (br, i, j))) for br in range(3)]
    otile = pl.BlockSpec((tm, MERGE_TN), lambda i, j: (i, j))
    return ytile, wblk, gls, otile


def _merge_fwd(ys, ws, gl, tm):
    S = gl.shape[1]

    def body(y0, y1, y2, w0, w1, w2, g0, g1, g2, o_ref):
        acc = None
        for y_ref, w_ref, g_ref in ((y0, w0, g0), (y1, w1, g1), (y2, w2, g2)):
            term = _sigmoid(g_ref[...]) * _dot(y_ref[...], w_ref[...], NT)
            acc = term if acc is None else acc + term
        o_ref[...] = acc.astype(BF)

    ytile, wblk, gls, otile = _merge_specs(tm)
    return pl.pallas_call(
        body, name="merge_fwd", grid=(S // tm, D_MODEL // MERGE_TN),
        in_specs=[ytile] * 3 + [wblk] * 3 + gls, out_specs=otile,
        out_shape=jax.ShapeDtypeStruct((S, D_MODEL), BF),
        compiler_params=_params(("parallel", "arbitrary")),
    )(*ys, *ws, gl, gl, gl)


def _merge_bwd(dout, w_out, ys, ws, gl, tm):
    S = gl.shape[1]

    def body(do_ref, wo_ref, y0, y1, y2, w0, w1, w2, g0, g1, g2, dg0, dg1, dg2, dp0, dp1, dp2):
        dm = _dot(do_ref[...], wo_ref[...], NT)
        for y_ref, w_ref, g_ref, dg_ref, dp_ref in ((y0, w0, g0, dg0, dp0), (y1, w1, g1, dg1, dp1), (y2, w2, g2, dg2, dp2)):
            gate = _sigmoid(g_ref[...])
            pv = _dot(y_ref[...], w_ref[...], NT)
            dg_ref[...] = (dm * pv * gate * (1.0 - gate)).astype(BF)
            dp_ref[...] = (dm * gate).astype(BF)

    ytile, wblk, gls, otile = _merge_specs(tm)
    out = jax.ShapeDtypeStruct((S, D_MODEL), BF)
    return pl.pallas_call(
        body, name="merge_bwd", grid=(S // tm, D_MODEL // MERGE_TN),
        in_specs=[pl.BlockSpec((tm, D_MODEL), lambda i, j: (i, 0)), pl.BlockSpec((MERGE_TN, D_MODEL), lambda i, j: (j, 0))]
        + [ytile] * 3 + [wblk] * 3 + gls,
        out_specs=[otile] * 6, out_shape=[out] * 6,
        compiler_params=_params(("parallel", "arbitrary")),
    )(dout, w_out, *ys, *ws, gl, gl, gl)


def _out_loss(merged, w_out, x, target, post_g, tm):
    S = x.shape[0]

    def body(m_ref, w_ref, x_ref, t_ref, g_ref, dout_ref, dy_ref, loss_ref, dpost_ref):
        @pl.when(pl.program_id(0) == 0)
        def _():
            loss_ref[...] = jnp.zeros_like(loss_ref)
            dpost_ref[...] = jnp.zeros_like(dpost_ref)

        out = _dot(m_ref[...], w_ref[...], NN)
        r = lax.rsqrt(jnp.mean(out * out, axis=-1, keepdims=True) + EPS)
        nrm = out * r
        gv = g_ref[...]
        err = (x_ref[...] + nrm * gv) - t_ref[...]
        sq = jnp.sum(jnp.sum(err * err, axis=1, keepdims=True), axis=0, keepdims=True)
        loss_ref[...] += sq * (0.5 / D_MODEL)
        dy = err * (1.0 / D_MODEL)
        dy_ref[...] = dy
        dpost_ref[...] += jnp.sum(dy * nrm, axis=0, keepdims=True)
        dn = dy * gv
        dout_ref[...] = (r * (dn - nrm * jnp.mean(dn * nrm, axis=-1, keepdims=True))).astype(BF)

    row = pl.BlockSpec((tm, D_MODEL), lambda i: (i, 0))
    return pl.pallas_call(
        body, name="out_loss", grid=(S // tm,),
        in_specs=[row, pl.BlockSpec((D_MODEL, D_MODEL), lambda i: (0, 0)), row, row, pl.BlockSpec((1, D_MODEL), lambda i: (0, 0))],
        out_specs=[row, row, pl.BlockSpec((8, 128), lambda i: (0, 0)), pl.BlockSpec((1, D_MODEL), lambda i: (0, 0))],
        out_shape=[jax.ShapeDtypeStruct((S, D_MODEL), BF), jax.ShapeDtypeStruct((S, D_MODEL), F32),
                   jax.ShapeDtypeStruct((8, 128), F32), jax.ShapeDtypeStruct((1, D_MODEL), F32)],
        compiler_params=_params(("arbitrary",)),
    )(merged, w_out, x, target, post_g)


def _dh_dx(dproj, w_in, x, dy, pre_g, tm):
    S = x.shape[0]
    nk, tk = dproj.shape[0], dproj.shape[2]

    def body(dp_ref, w_ref, x_ref, dy_ref, g_ref, dx_ref, dpre_ref, acc_ref):
        i, k = pl.program_id(0), pl.program_id(1)

        @pl.when(jnp.logical_and(i == 0, k == 0))
        def _():
            dpre_ref[...] = jnp.zeros_like(dpre_ref)

        @pl.when(k == 0)
        def _():
            acc_ref[...] = jnp.zeros_like(acc_ref)

        acc_ref[...] += _dot(dp_ref[...], w_ref[...], NN)

        @pl.when(k == nk - 1)
        def _():
            dh = acc_ref[...]
            xv = x_ref[...]
            r = lax.rsqrt(jnp.mean(xv * xv, axis=-1, keepdims=True) + EPS)
            nrm = xv * r
            dpre_ref[...] += jnp.sum(dh * nrm, axis=0, keepdims=True)
            dn = dh * g_ref[...]
            dx_ref[...] = r * (dn - nrm * jnp.mean(dn * nrm, axis=-1, keepdims=True)) + dy_ref[...]

    row = pl.BlockSpec((tm, D_MODEL), lambda i, k: (i, 0))
    vec = pl.BlockSpec((1, D_MODEL), lambda i, k: (0, 0))
    return pl.pallas_call(
        body, name="dh_dx", grid=(S // tm, nk),
        in_specs=[pl.BlockSpec((None, tm, tk), lambda i, k: (k, i, 0)), pl.BlockSpec((tk, D_MODEL), lambda i, k: (k, 0)), row, row, vec],
        out_specs=[row, vec],
        out_shape=[jax.ShapeDtypeStruct((S, D_MODEL), F32), jax.ShapeDtypeStruct((1, D_MODEL), F32)],
        scratch_shapes=[pltpu.VMEM((tm, D_MODEL), F32)],
        compiler_params=_params(("arbitrary", "arbitrary"), large=True),
    )(dproj, w_in, x, dy, pre_g)


def _sum_parts(parts, name):
    P, R, C = parts.shape
    tr = max(t for t in range(8, 513, 8) if R % t == 0)

    def body(p_ref, o_ref):
        acc = p_ref[0]
        for j in range(1, P):
            acc = acc + p_ref[j]
        o_ref[...] = acc

    return pl.pallas_call(
        body, name=name, grid=(R // tr,),
        in_specs=[pl.BlockSpec((P, tr, C), lambda i: (0, i, 0))], out_specs=pl.BlockSpec((tr, C), lambda i: (i, 0)),
        out_shape=jax.ShapeDtypeStruct((R, C), F32), compiler_params=_params(("parallel",)),
    )(parts)


def _adamw(lands, sums, chip, w, m, v, name):
    lands = list(lands) if isinstance(lands, (list, tuple)) else [lands]
    sums = list(sums) if isinstance(sums, (list, tuple)) else [sums]
    _, R, cols = lands[0].shape
    C = cols * len(lands)
    tr = max(t for t in range(16, 257, 16) if R % t == 0)
    c1 = 1.0 - ADAM_B1 ** ADAM_STEP
    c2 = 1.0 - ADAM_B2 ** ADAM_STEP

    def body(chip_ref, *refs):
        p_refs = refs[:4 * len(lands)]
        w_ref, m_ref, v_ref, g_ref, d_ref, nm_ref, nv_ref = refs[4 * len(lands):]
        gs = []
        for q in range(len(lands)):
            gq = p_refs[4 * q + 3][...].astype(F32)
            for j in range(3):
                gq = gq + p_refs[4 * q + j][...].astype(F32)
            gs.append(gq)
        g = gs[0] if len(gs) == 1 else jnp.concatenate(gs, axis=1)
        nm = ADAM_B1 * m_ref[...] + (1.0 - ADAM_B1) * g
        nv = ADAM_B2 * v_ref[...] + (1.0 - ADAM_B2) * (g * g)
        g_ref[...] = g
        nm_ref[...] = nm
        nv_ref[...] = nv
        d_ref[...] = -ADAM_LR * ((nm / c1) / (jnp.sqrt(nv / c2) + ADAM_EPS) + ADAM_WD * w_ref[...])

    tile = pl.BlockSpec((None, tr, C), lambda i, c_ref: (0, i, 0))
    specs, operands = [], []
    for q in range(len(lands)):
        for k in range(3):
            specs.append(pl.BlockSpec((None, tr, cols), (lambda i, c_ref, k=k: (k + (c_ref[0] <= k).astype(jnp.int32), i, 0))))
            operands.append(lands[q])
        specs.append(pl.BlockSpec((None, tr, cols), (lambda i, c_ref: (c_ref[0], i, 0))))
        operands.append(sums[q])
    return pl.pallas_call(
        body, name=name,
        grid_spec=pltpu.PrefetchScalarGridSpec(num_scalar_prefetch=1, grid=(R // tr,), in_specs=specs + [tile, tile, tile],
                                               out_specs=[tile] * 4),
        out_shape=[jax.ShapeDtypeStruct((1, R, C), F32)] * 4, compiler_params=_params(("parallel",)),
    )(chip, *operands, w, m, v)


def _adamw_small(gs, ws, ms, vs):
    n = len(ws)
    c1 = 1.0 - ADAM_B1 ** ADAM_STEP
    c2 = 1.0 - ADAM_B2 ** ADAM_STEP

    def flat2(a):
        return a.reshape(-1, a.shape[-1])

    def body(*refs):
        ins, outs = refs[:4 * n], refs[4 * n:]
        for a in range(n):
            g, w, m, v = (ins[k * n + a][...] for k in range(4))
            nm = ADAM_B1 * m + (1.0 - ADAM_B1) * g
            nv = ADAM_B2 * v + (1.0 - ADAM_B2) * (g * g)
            outs[a][...] = g
            outs[n + a][...] = -ADAM_LR * ((nm / c1) / (jnp.sqrt(nv / c2) + ADAM_EPS) + ADAM_WD * w)
            outs[2 * n + a][...] = nm
            outs[3 * n + a][...] = nv

    shapes = [flat2(w).shape for w in ws]
    out = pl.pallas_call(
        body, name="adamw_small", out_shape=[jax.ShapeDtypeStruct(sh, F32) for sh in shapes] * 4,
        compiler_params=_params(),
    )(*[g.reshape(sh) for g, sh in zip(gs, shapes)], *[flat2(a) for a in (*ws, *ms, *vs)])
    return [[out[k * n + a].reshape(ws[a].shape) for a in range(n)] for k in range(4)]


def _project(h, w_t):
    S = h.shape[0]
    n_tiles = D_IN // SEG_TILE
    ranges = [(c0 // SEG_TILE, (c0 + width) // SEG_TILE) for _, c0, width, _ in SEGMENTS]

    def body(h_ref, w_ref, *outs):
        j = pl.program_id(0)
        prod = _dot(h_ref[...], w_ref[...], NT)
        for (j0, j1), (_, _, _, dt), o_ref in zip(ranges, SEGMENTS, outs):
            @pl.when(jnp.logical_and(j >= j0, j < j1))
            def _(o_ref=o_ref, dt=dt):
                o_ref[...] = prod.astype(dt)

    out_shapes, out_specs = [], []
    for (j0, j1), (name, _, width, dt) in zip(ranges, SEGMENTS):
        if name == "gl":
            per = (j1 - j0) // 3
            out_shapes.append(jax.ShapeDtypeStruct((3, S, width // 3), dt))
            out_specs.append(pl.BlockSpec((None, S, SEG_TILE), (lambda j, j0=j0, j1=j1, per=per: (
                jnp.clip(j - j0, 0, j1 - j0 - 1) // per, 0, jnp.clip(j - j0, 0, j1 - j0 - 1) % per))))
        else:
            out_shapes.append(jax.ShapeDtypeStruct((S, width), dt))
            out_specs.append(pl.BlockSpec((S, SEG_TILE), (lambda j, j0=j0, j1=j1: (0, jnp.clip(j - j0, 0, j1 - j0 - 1)))))
    outs = pl.pallas_call(
        body, name="proj", grid=(n_tiles,),
        in_specs=[pl.BlockSpec((S, D_MODEL), lambda j: (0, 0)), pl.BlockSpec((SEG_TILE, D_MODEL), lambda j: (j, 0))],
        out_specs=out_specs, out_shape=out_shapes,
        compiler_params=_params(("arbitrary",), large=True),
    )(h, w_t)
    return {name: o for (name, _, _, _), o in zip(SEGMENTS, outs)}


def _forward_a(x, mem, pre_g, mem_g, w_in, conv_w, conv_b, w_a, b_a, w_x, b_x, lam, sinks, rel_bias):
    S = x.shape[0]
    st = dict(T=min(512, S // 2), tm=min(512, S), bucket=_rel_bucket_map())
    st["h"] = _rms_fwd(x, pre_g, "pre_norm")
    st["memn"] = _rms_fwd(mem, mem_g, "mem_norm")
    seg = st["seg"] = _project(st["h"], w_in)
    st["h_rg"], st["y_rg"] = _rglru_fwd(seg["xr"], seg["g_rg"], conv_w, conv_b, w_a, b_a, w_x, b_x, lam, st["T"])
    st["o_swa"], st["y_swa"] = _swa_fwd(seg["q_s"], seg["kv"], seg["g_swa"], st["bucket"], rel_bias, _sink_column(sinks))
    return st


def _forward_b(st, x, target, post_g, w_memkv, wbr, w_out):
    S = x.shape[0]
    M = st["memn"].shape[0]
    seg = st["seg"]
    st["mkv"] = _matmul(st["memn"], w_memkv, "nn", M, 2 * D_MEM, D_MODEL, M, 512, D_MODEL, BF, "mem_kv")
    st["o_mem"], st["y_mem"] = _mem_fwd(seg["q_m"], st["mkv"], seg["g_mem"])
    st["ys"] = (st["y_rg"], st["y_swa"], st["y_mem"])
    st["merged"] = _merge_fwd(st["ys"], wbr, seg["gl"], st["tm"])
    st["dout"], st["dy"], st["loss"], st["dpost"] = _out_loss(st["merged"], w_out, x, target, post_g, min(256, S))
    return st


def _backward_a1(st, wbr, w_out):
    S = st["h"].shape[0]
    seg, ys, tm = st["seg"], st["ys"], st["tm"]
    st["dw_out"] = _matmul(st["merged"], st["dout"], "tn", D_MODEL, D_MODEL, S, 256, D_MODEL, S, BF, "dw_out", out_blocked="row")
    dgl0, dgl1, dgl2, dp0, dp1, dp2 = _merge_bwd(st["dout"], w_out, ys, wbr, seg["gl"], tm)
    st["dgl"] = (dgl0, dgl1, dgl2)
    dys, dwbr = [], []
    for i, dp in enumerate((dp0, dp1, dp2)):
        dys.append(_matmul(dp, wbr[i], "nn", S, 1024, D_MODEL, tm, 1024, D_MODEL, F32, "dy_br%d" % i))
        dwbr.append(_matmul(ys[i], dp, "tn", 1024, D_MODEL, S, 1024, 256, S, BF, "dw_br%d" % i, out_blocked="col"))
    st["dys"], st["dwbr"] = dys, dwbr
    return st


def _backward_a2(st, mem, w_memkv, conv_w, conv_b, w_a, b_a, w_x, b_x, lam):
    M = mem.shape[0]
    seg, dys = st["seg"], st["dys"]
    st["dq_m"], st["dg_mem"], dmkv = _mem_bwd(seg["q_m"], st["mkv"], seg["g_mem"], st["o_mem"], dys[2])
    dmkv_b = dmkv.astype(BF)
    st["dw_memkv"] = _matmul(st["memn"], dmkv_b, "tn", D_MODEL, 2 * D_MEM, M, 256, 2 * D_MEM, M, BF, "dw_memkv", out_blocked="row")
    dmemn = _matmul(dmkv_b, w_memkv, "nt", M, D_MODEL, 2 * D_MEM, M, 512, 2 * D_MEM, F32, "dmemn")
    st["dmem_g"] = _rms_gain_grad(dmemn, mem, "dmem_gain")
    st["dxr"], st["dg_rg"], st["dw_a"], st["dw_x"], st["dvec"] = _rglru_bwd(
        seg["xr"], seg["g_rg"], st["h_rg"], dys[0], conv_w, conv_b, w_a, b_a, w_x, b_x, lam, st["T"])
    return st


def _backward_b(st, rel_bias, sinks):
    seg = st["seg"]
    dq_s, dg_swa, dkv, dsinks, drel = _swa_bwd(seg["q_s"], seg["kv"], seg["g_swa"], st["o_swa"], st["dys"][1],
                                               st["bucket"], rel_bias, _sink_column(sinks))
    st["dsinks"], st["drel"] = dsinks.reshape(1, SWA_HEADS), drel.reshape(REL_BUCKETS, SWA_HEADS)
    dproj = jnp.concatenate([st["dxr"], st["dg_rg"], dq_s, dkv.astype(BF), dg_swa, st["dq_m"], st["dg_mem"], *st["dgl"]], axis=1)
    st["dproj"] = jnp.transpose(dproj.reshape(dproj.shape[0], N_DEV, D_IN // N_DEV), (1, 0, 2))
    return st


def _dw_in_half(st, half, dep=None):
    S = st["h"].shape[0]
    return _matmul(st["dproj"], st["h"], "tn", D_IN, D_MODEL // 2, S, D_IN // N_DEV, 512, S, BF, "dw_in%d" % half, b_noff=2 * half,
                   a_blocked=True, out_blocked="row", dep=dep)


def _owner_blocks(a):
    return jnp.swapaxes(a.reshape((4, 2) + a.shape[1:]), 0, 1)


def _local_step(x, mem, target, pre_g, post_g, mem_g, w_in, conv_w, conv_b, w_a, b_a, w_x, b_x, lam, sinks, rel_bias,
                w_memkv, wbr, w_out):
    st = _forward_a(x, mem, pre_g, mem_g, w_in, conv_w, conv_b, w_a, b_a, w_x, b_x, lam, sinks, rel_bias)
    st = _forward_b(st, x, target, post_g, w_memkv, wbr, w_out)
    st = _backward_a1(st, wbr, w_out)
    st = _backward_a2(st, mem, w_memkv, conv_w, conv_b, w_a, b_a, w_x, b_x, lam)
    st = _backward_b(st, rel_bias, sinks)
    st["dw_in"] = [_dw_in_half(st, 0), _dw_in_half(st, 1)]
    st["grad_x"], st["dpre"] = _dh_dx(st["dproj"], w_in, x, st["dy"], pre_g, st["tm"])
    return st


def _pad_rows(a, rows):
    a = a.reshape(-1, 128) if a.shape[-1] % 128 == 0 else jnp.pad(a, ((0, 0), (0, 128 - a.shape[-1])))
    return jnp.pad(a, ((0, rows - a.shape[0]), (0, 0))) if a.shape[0] < rows else a


def kernel(x, mem, pre_norm_g, post_norm_g, mem_norm_g, w_in, conv_w, conv_b, w_rg_a, b_rg_a, w_rg_x, b_rg_x, lru_lambda, swa_sinks, rel_bias, w_mem_kv, w_br_rg, w_br_swa, w_br_mem, w_out, loss_target, m_pre_norm_g, m_post_norm_g, m_mem_norm_g, m_w_in, m_conv_w, m_conv_b, m_w_rg_a, m_b_rg_a, m_w_rg_x, m_b_rg_x, m_lru_lambda, m_swa_sinks, m_rel_bias, m_w_mem_kv, m_w_br_rg, m_w_br_swa, m_w_br_mem, m_w_out, v_pre_norm_g, v_post_norm_g, v_mem_norm_g, v_w_in, v_conv_w, v_conv_b, v_w_rg_a, v_b_rg_a, v_w_rg_x, v_b_rg_x, v_lru_lambda, v_swa_sinks, v_rel_bias, v_w_mem_kv, v_w_br_rg, v_w_br_swa, v_w_br_mem, v_w_out):
    cx, cy, cc = lax.axis_index("x"), lax.axis_index("y"), lax.axis_index("c")
    me = 4 * cx + 2 * cy + cc
    chip = 2 * cx + cy
    core = jnp.reshape(cc, (1,)).astype(jnp.int32)
    x0, mem0 = x[0], mem[0]
    w_a_b, w_x_b = w_rg_a[0].astype(BF), w_rg_x[0].astype(BF)

    def landing(own, slot, slots):
        return lax.dynamic_update_slice(lax.empty((slots,) + own.shape, own.dtype), own[None], (slot,) + (0,) * own.ndim)


    def swap_start(parts, tag):
        return _exchange_start(parts, [lax.empty(p.shape[1:], p.dtype) for p in parts], _plan_swap(len(parts)), "swap_%s_start" % tag)

    def scatter_start(swap, after, tag, prefill=()):
        s_send, s_recv, parts, got, _ = swap
        got = _exchange_wait(s_send, s_recv, parts, got, _plan_swap(len(parts)), after, "swap_%s_wait" % tag)
        sums = [_pair_sum(p, g, core, "scatter_%s_sum%d" % (tag, i)) for i, (p, g) in enumerate(zip(parts, got))]
        lands = [landing(lax.dynamic_index_in_dim(s, chip, 0, keepdims=False), chip, 4) if i in prefill
                 else lax.empty(s.shape, s.dtype) for i, s in enumerate(sums)]
        return _exchange_start(sums, lands, _plan_scatter(len(sums)), "scatter_%s_start" % tag)

    def corner(a):
        return a.reshape(-1, a.shape[-1])[:8, :128]

    def zero_after(a):
        return jnp.minimum(jnp.abs(a.reshape(-1)[0].astype(F32)), 0.0)

    g_in, g_cw = _all_gather_relayed([jnp.transpose(w_in[0]).astype(BF), conv_w[0]], [True, False], "gather_w_in")
    w_in_f = g_in.reshape(D_IN, D_MODEL)
    conv_w_f = jnp.transpose(g_cw, (1, 0, 2)).reshape(CONV_W, D_RNN)

    after_first = zero_after(g_cw).astype(BF)
    rest = [w.astype(BF) + after_first for w in (w_mem_kv[0], jnp.transpose(w_br_rg[0]), jnp.transpose(w_br_swa[0]),
                                                 jnp.transpose(w_br_mem[0]), w_out[0])]
    kinds = ["lead"] * len(rest)
    plan_g = _plan_gather(kinds)
    zones = _place_own([lax.empty((N_DEV,) + w.shape, w.dtype) for w in rest], rest, jnp.reshape(me, (1,)).astype(jnp.int32), "gather_rest_own")
    g_send, g_recv, g_src, g_land, g_token = _exchange_start(rest, zones, plan_g, "gather_rest_start")
    st = _forward_a(x0, mem0, pre_norm_g + g_token[0:1, 0:1], mem_norm_g, w_in_f, conv_w_f, conv_b, w_a_b, b_rg_a, w_x_b, b_rg_x,
                    lru_lambda, swa_sinks, rel_bias)
    g_land = _exchange_wait(g_send, g_recv, g_src, g_land, plan_g, st["y_swa"], "gather_rest_wait")
    g_land = _forward_to_sibling(g_land, kinds, "gather_rest_forward")
    w_memkv_f = g_land[0].reshape(D_MODEL, 2 * D_MEM)
    wbr = tuple(g_land[i].reshape(D_MODEL, D_RNN) for i in (1, 2, 3))
    w_out_f = g_land[4].reshape(D_MODEL, D_MODEL)

    st = _forward_b(st, x0, loss_target[0], post_norm_g, w_memkv_f, wbr, w_out_f)
    st = _backward_a1(st, wbr, w_out_f)
    parts_a = [st["dw_out"], st["dwbr"][0], st["dwbr"][1], st["dwbr"][2]]
    plan_a = _plan_scatter(len(parts_a))
    swap_a = swap_start(parts_a, "a")
    st = _backward_a2(st, mem0, w_memkv_f, conv_w_f, conv_b + swap_a[4][0:1, 0:1], w_a_b, b_rg_a, w_x_b, b_rg_x, lru_lambda)
    a_send, a_recv, a_src, a_land, a_token = scatter_start(swap_a, st["dxr"], "a")
    parts_c = [st["dw_memkv"], _owner_blocks(st["dw_a"]), _owner_blocks(st["dw_x"])]
    plan_c = _plan_scatter(len(parts_c))
    swap_c = swap_start(parts_c, "c")

    st = _backward_b(st, rel_bias, swa_sinks + swap_c[4][0:1, 0:1] + a_token[0:1, 0:1])
    c_send, c_recv, c_src, c_land, c_token = scatter_start(swap_c, st["dsinks"], "c", prefill=(1, 2))
    plan_b = _plan_scatter(1)

    def dw_in_parts(half, dep):
        dwh = _dw_in_half(st, half, dep)
        return dwh, [dwh]

    dw0, parts_b0 = dw_in_parts(0, c_token)
    swap_b0 = swap_start(parts_b0, "b0")
    a_land = _exchange_wait(a_send, a_recv, a_src, a_land, plan_a, swap_b0[4], "scatter_a_wait")
    big = [None] * 6

    chip1 = jnp.reshape(chip, (1,)).astype(jnp.int32)

    def adamw_big(j, land, own, wt, mt, vt):
        big[j] = _adamw(land, own, chip1, wt, mt, vt, "adamw_big%d" % j)

    adamw_big(5, a_land[0], a_src[0], w_out, m_w_out, v_w_out)
    adamw_big(2, a_land[1], a_src[1], w_br_rg, m_w_br_rg, v_w_br_rg)
    adamw_big(3, a_land[2], a_src[2], w_br_swa, m_w_br_swa, v_w_br_swa)
    halves = [scatter_start(swap_b0, corner(big[3][1]), "b0")]
    dw1, parts_b1 = dw_in_parts(1, halves[0][4])
    swap_b1 = swap_start(parts_b1, "b1")
    c_land = _exchange_wait(c_send, c_recv, c_src, c_land, plan_c, swap_b1[4], "scatter_c_wait")
    g_wa_blk = _sum_parts(c_land[1], "sum_w_rg_a")
    g_wx_blk = _sum_parts(c_land[2], "sum_w_rg_x")
    adamw_big(4, a_land[3], a_src[3], w_br_mem, m_w_br_mem, v_w_br_mem)
    adamw_big(1, c_land[0], c_src[0], w_mem_kv, m_w_mem_kv, v_w_mem_kv)
    halves.append(scatter_start(swap_b1, corner(big[1][1]), "b1"))
    grad_x, dpre = _dh_dx(st["dproj"], w_in_f, x0, st["dy"], pre_norm_g + halves[1][4][0:1, 0:1], st["tm"])
    pack = jnp.concatenate([dpre.reshape(16, 128), st["dpost"].reshape(16, 128), st["dmem_g"].reshape(16, 128),
                            st["dvec"].reshape(64, 128), _pad_rows(st["dsinks"], 8), _pad_rows(st["drel"], 32), g_wa_blk, g_wx_blk,
                            st["loss"]], axis=0)
    plan_s = _plan_everyone(1)
    s_send, s_recv, s_src, s_land, s_token = _exchange_start([pack], [landing(pack, me, N_DEV)], plan_s, "gather_small_start")
    after, b_lands, b_sums = s_token, [], []
    for half, (b_send, b_recv, b_src, b_land, _) in enumerate(halves):
        b_lands.append(_exchange_wait(b_send, b_recv, b_src, b_land, plan_b, after, "scatter_b%d_wait" % half)[0])
        b_sums.append(b_src[0])
        after = b_lands[-1]
    swap_last = lambda a: jnp.transpose(a, (0, 2, 1))
    big0_t = _adamw(b_lands, b_sums, chip1, swap_last(w_in), swap_last(m_w_in), swap_last(v_w_in), "adamw_big0")
    big[0] = [swap_last(a) for a in big0_t]
    gathered = _exchange_wait(s_send, s_recv, s_src, s_land, plan_s, corner(big0_t[1]), "gather_small_wait")[0]
    gs = _sum_parts(gathered, "sum_small")
    loss_total = gs[408, 0]
    g_pre, g_post, g_memg = gs[0:16].reshape(1, D_MODEL), gs[16:32].reshape(1, D_MODEL), gs[32:48].reshape(1, D_MODEL)
    gvec = gs[48:112].reshape(8, D_RNN)
    g_conv_w = lax.dynamic_slice(gvec[0:CONV_W], (0, me * RNN_BLOCK), (CONV_W, RNN_BLOCK))
    g_conv_b, g_b_a, g_b_x, g_lam = gvec[4:5], gvec[5:6], gvec[6:7], gvec[7:8]
    g_sinks = gs[112:113, :SWA_HEADS]
    g_rel = gs[120:152, :SWA_HEADS]
    g_w_a = gathered[:, 152:280]
    g_w_x = gathered[:, 280:408]

    g_small = (g_pre, g_post, g_memg, g_conv_b, g_b_a, g_b_x, g_lam, g_w_a, g_w_x, g_sinks, g_rel, g_conv_w)
    w_small = (pre_norm_g, post_norm_g, mem_norm_g, conv_b, b_rg_a, b_rg_x, lru_lambda, w_rg_a, w_rg_x, swa_sinks, rel_bias, conv_w)
    m_small = (m_pre_norm_g, m_post_norm_g, m_mem_norm_g, m_conv_b, m_b_rg_a, m_b_rg_x, m_lru_lambda, m_w_rg_a, m_w_rg_x, m_swa_sinks, m_rel_bias, m_conv_w)
    v_small = (v_pre_norm_g, v_post_norm_g, v_mem_norm_g, v_conv_b, v_b_rg_a, v_b_rg_x, v_lru_lambda, v_w_rg_a, v_w_rg_x, v_swa_sinks, v_rel_bias, v_conv_w)
    sm = _adamw_small(g_small, w_small, m_small, v_small)


    def leaves(k):
        s = sm[k]
        return [s[0], s[1], s[2], big[0][k], s[11], s[3], s[7], s[4], s[8], s[5], s[6], s[9], s[10],
                big[1][k], big[2][k], big[3][k], big[4][k], big[5][k]]

    return (loss_total, grad_x[None], *leaves(0), *leaves(1), *leaves(2), *leaves(3))
```

```python
import math

import jax
import jax.numpy as jnp
import numpy as np
from jax import lax
from jax.experimental import pallas as pl
from jax.experimental.pallas import tpu as pltpu

F32, BF = jnp.float32, jnp.bfloat16
MESH = pl.DeviceIdType.MESH
N_DEV = 8

D_MODEL = 2048
D_RNN = 1024
RNN_BLOCKS = 8
RNN_BLOCK = 128
CONV_W = 4
LRU_C = 8.0
SWA_HEADS = 16
SWA_KV_HEADS = 2
SWA_HD = 64
WINDOW = 128
MEM_HEADS = 4
MEM_HD = 256
D_MEM = 1024
REL_BUCKETS = 32
REL_MAX_DIST = 128
EPS = 1e-6
NEG_INF = -1e30
D_IN = 12544
SEGMENTS = (("xr", 0, 1024, F32), ("g_rg", 1024, 1024, F32), ("q_s", 2048, 1024, BF), ("kv", 3072, 256, BF),
            ("g_swa", 3328, 1024, F32), ("q_m", 4352, 1024, BF), ("g_mem", 5376, 1024, F32), ("gl", 6400, 6144, F32))
SEG_TILE = 256

ADAM_LR, ADAM_B1, ADAM_B2, ADAM_EPS, ADAM_WD, ADAM_STEP = 0.001, 0.9, 0.999, 1e-08, 0.01, 10

NN = (((1,), (0,)), ((), ()))
NT = (((1,), (1,)), ((), ()))
TN = (((0,), (0,)), ((), ()))
MIB = 2 ** 20


def _dot(a, b, dn):
    return lax.dot_general(a, b, dn, preferred_element_type=F32)


VMEM_LIMIT_MIB = 48
VMEM_LIMIT_LARGE_MIB = 56


def _params(sem=None, large=False):
    return pltpu.CompilerParams(dimension_semantics=sem, vmem_limit_bytes=(VMEM_LIMIT_LARGE_MIB if large else VMEM_LIMIT_MIB) * MIB)


def _sigmoid(z):
    return 1.0 / (1.0 + jnp.exp(-z))


def _softplus(z):
    return jnp.maximum(z, 0.0) + jnp.log(1.0 + jnp.exp(-jnp.abs(z)))


def _expm1(z):
    p = z * (1.0 + z * (0.5 + z * (1.0 / 6 + z * (1.0 / 24 + z * (1.0 / 120 + z * (1.0 / 720 + z * (1.0 / 5040 + z / 40320)))))))
    return jnp.where(jnp.abs(z) < 0.3, p, jnp.exp(z) - 1.0)


def _flat(p):
    return 4 * p[0] + 2 * p[1] + p[2]


def _all_gather_relayed(arrs, relay, name):
    n = len(arrs)
    K = 9

    def body(*refs):
        ins, outs = refs[:n], refs[n:2 * n]
        send_sems, recv_sems, local_sems = refs[2 * n:]
        x, y, c = lax.axis_index("x"), lax.axis_index("y"), lax.axis_index("c")
        me, sib = (x, y, c), (x, y, 1 - c)
        xn, yn, dg = (1 - x, y, c), (x, 1 - y, c), (1 - x, 1 - y, c)

        def other(p):
            return (p[0], p[1], 1 - p[2])

        def rows(a, half):
            h = arrs[a].shape[0] // 2
            return pl.ds(half * h, h)

        def copy(a, k, block, to, half=None, src=None):
            dst = outs[a].at[_flat(block)]
            if half is not None:
                dst = dst.at[rows(a, half)]
            return pltpu.make_async_remote_copy(src_ref=dst if src is None else src, dst_ref=dst,
                                                send_sem=send_sems.at[a * K + k], recv_sem=recv_sems.at[a * K + k],
                                                device_id=to, device_id_type=MESH)

        mine = [pltpu.make_async_copy(ins[a], outs[a].at[_flat(me)], local_sems.at[a]) for a in range(n)]
        for cp in mine:
            cp.start()
        sends = []

        def start(cp):
            cp.start()
            sends.append(cp)

        for a in range(n):
            start(copy(a, 1, me, xn, src=ins[a]))
            start(copy(a, 2, me, yn, src=ins[a]))
            if not relay[a]:
                start(copy(a, 3, me, dg, src=ins[a]))
            start(copy(a, 0, me, sib, src=ins[a]))
        for a in range(n):
            copy(a, 1, xn, me).wait_recv()
            if relay[a]:
                start(copy(a, 3, xn, yn, half=0))
            start(copy(a, 5, xn, sib))
        for a in range(n):
            copy(a, 2, yn, me).wait_recv()
            if relay[a]:
                start(copy(a, 4, yn, xn, half=1))
            start(copy(a, 6, yn, sib))
        for a in range(n):
            if relay[a]:
                copy(a, 3, dg, me, half=0).wait_recv()
                start(copy(a, 7, dg, sib, half=0))
                copy(a, 4, dg, me, half=1).wait_recv()
                start(copy(a, 8, dg, sib, half=1))
            else:
                copy(a, 3, dg, me).wait_recv()
                start(copy(a, 7, dg, sib))
        for a in range(n):
            copy(a, 0, sib, me).wait_recv()
            copy(a, 5, other(xn), me).wait_recv()
            copy(a, 6, other(yn), me).wait_recv()
            if relay[a]:
                copy(a, 7, other(dg), me, half=0).wait_recv()
                copy(a, 8, other(dg), me, half=1).wait_recv()
            else:
                copy(a, 7, other(dg), me).wait_recv()
        for cp in sends:
            cp.wait_send()
        for cp in mine:
            cp.wait()

    any_spec = pl.BlockSpec(memory_space=pl.ANY)
    return pl.pallas_call(
        body, name=name,
        out_shape=[jax.ShapeDtypeStruct((N_DEV,) + a.shape, a.dtype) for a in arrs],
        in_specs=[any_spec] * n, out_specs=[any_spec] * n,
        scratch_shapes=[pltpu.SemaphoreType.DMA((K * n,)), pltpu.SemaphoreType.DMA((K * n,)), pltpu.SemaphoreType.DMA((n,))],
    )(*arrs)


def _chip_peers(x, y):
    return [(1 - x, y), (x, 1 - y), (1 - x, 1 - y)]


def _chip(p):
    return 2 * p[0] + p[1]


def _plan_gather(kinds):
    def plan(x, y, c):
        out = []
        for a, kind in enumerate(kinds):
            for peer in [(x, y, 1 - c)] + [(*ch, c) for ch in _chip_peers(x, y)]:
                out.append((a, None, (kind, _flat((x, y, c))), peer, (kind, _flat(peer))))
        return out
    return plan


def _plan_everyone(n):
    def plan(x, y, c):
        out = []
        for a in range(n):
            for r in range(1, N_DEV):
                peer = (1 - x if r & 4 else x, 1 - y if r & 2 else y, 1 - c if r & 1 else c)
                out.append((a, None, ("lead", _flat((x, y, c))), peer, ("lead", _flat(peer))))
        return out
    return plan


def _plan_swap(ndims):
    def plan(x, y, c):
        out = []
        for a, nd in enumerate(ndims):
            if nd == 4:
                out.append((a, 1 - c, ("all", 0), (x, y, 1 - c), ("all", 0)))
            else:
                out += [(a, 2 * j + 1 - c, ("lead", j), (x, y, 1 - c), ("lead", j)) for j in range(4)]
        return out
    return plan


def _slot(ref, where):
    kind, k = where
    if kind == "all":
        return ref
    if kind == "lead":
        return ref.at[k]
    return ref.at[:, pl.ds(pl.multiple_of(k * 256, 256), 256)]


def _plan_scatter(n):
    def plan(x, y, c):
        out = []
        for a in range(n):
            for ch in _chip_peers(x, y):
                out.append((a, _chip(ch), ("lead", _chip((x, y))), (*ch, c), ("lead", _chip(ch))))
        return out
    return plan


HBM_SPEC = pl.BlockSpec(memory_space=pltpu.HBM)
SEM_SPEC = pl.BlockSpec(memory_space=pltpu.SEMAPHORE)


def _in_hbm(a):
    return pltpu.with_memory_space_constraint(a, pltpu.HBM)


def _exchange_start(srcs, lands, plan, name):
    n = len(srcs)
    count = len(plan(0, 0, 0))

    def body(*refs):
        src_refs, land_refs = refs[:n], refs[n:2 * n]
        send_sems, recv_sems = refs[2 * n], refs[2 * n + 1]
        token = refs[-1]
        x, y, c = lax.axis_index("x"), lax.axis_index("y"), lax.axis_index("c")
        for k, (a, si, di, peer, _) in enumerate(plan(x, y, c)):
            src = src_refs[a] if si is None else src_refs[a].at[si]
            pltpu.make_async_remote_copy(src_ref=src, dst_ref=_slot(land_refs[a], di), send_sem=send_sems.at[k],
                                         recv_sem=recv_sems.at[k], device_id=peer, device_id_type=MESH).start()
        token[...] = jnp.zeros_like(token)

    out = pl.pallas_call(
        body, name=name,
        out_shape=(pltpu.SemaphoreType.DMA((count,)), pltpu.SemaphoreType.DMA((count,)),
                   *[pltpu.HBM(a.shape, a.dtype) for a in lands], jax.ShapeDtypeStruct((8, 128), F32)),
        in_specs=[HBM_SPEC] * (2 * n),
        out_specs=(SEM_SPEC, SEM_SPEC, *([HBM_SPEC] * n), pl.BlockSpec(memory_space=pltpu.VMEM)),
        input_output_aliases={n + i: 2 + i for i in range(n)},
        compiler_params=pltpu.CompilerParams(has_side_effects=pltpu.SideEffectType.DATAFLOW_SIDE_EFFECTING),
    )(*[_in_hbm(a) for a in srcs], *[_in_hbm(a) for a in lands])
    return out[0], out[1], list(srcs), list(out[2:2 + n]), out[-1]


def _exchange_wait(send_sems, recv_sems, srcs, lands, plan, after, name):
    n = len(srcs)

    def body(*refs):
        src_refs, land_refs = refs[:n], refs[n:2 * n]
        send_sems, recv_sems = refs[2 * n], refs[2 * n + 1]
        x, y, c = lax.axis_index("x"), lax.axis_index("y"), lax.axis_index("c")
        for k, (a, si, _, peer, ri) in enumerate(plan(x, y, c)):
            src = src_refs[a] if si is None else src_refs[a].at[si]
            cp = pltpu.make_async_remote_copy(src_ref=src, dst_ref=_slot(land_refs[a], ri), send_sem=send_sems.at[k],
                                              recv_sem=recv_sems.at[k], device_id=peer, device_id_type=MESH)
            cp.wait_send()
            cp.wait_recv()

    out = pl.pallas_call(
        body, name=name,
        out_shape=tuple(pltpu.HBM(a.shape, a.dtype) for a in lands),
        in_specs=[HBM_SPEC] * (2 * n) + [SEM_SPEC, SEM_SPEC, pl.BlockSpec(memory_space=pl.ANY)],
        out_specs=tuple([HBM_SPEC] * n),
        input_output_aliases={n + i: i for i in range(n)},
        compiler_params=pltpu.CompilerParams(has_side_effects=pltpu.SideEffectType.DATAFLOW_SIDE_EFFECTING),
    )(*[_in_hbm(a) for a in srcs], *lands, send_sems, recv_sems, after)
    return list(out)


def _forward_to_sibling(lands, kinds, name):
    n = len(lands)

    def body(*refs):
        in_refs, out_refs = refs[:n], refs[n:2 * n]
        send_sems, recv_sems = refs[2 * n:]
        x, y, c = lax.axis_index("x"), lax.axis_index("y"), lax.axis_index("c")
        sibling = (x, y, 1 - c)

        def copy(a, j, slot):
            return pltpu.make_async_remote_copy(src_ref=_slot(in_refs[a], (kinds[a], slot)), dst_ref=_slot(out_refs[a], (kinds[a], slot)),
                                                send_sem=send_sems.at[a * 3 + j], recv_sem=recv_sems.at[a * 3 + j],
                                                device_id=sibling, device_id_type=MESH)

        sends = [copy(a, j, _flat((*ch, c))) for a in range(n) for j, ch in enumerate(_chip_peers(x, y))]
        for cp in sends:
            cp.start()
        for a in range(n):
            for j, ch in enumerate(_chip_peers(x, y)):
                copy(a, j, _flat((*ch, 1 - c))).wait_recv()
        for cp in sends:
            cp.wait_send()

    any_spec = pl.BlockSpec(memory_space=pl.ANY)
    return pl.pallas_call(
        body, name=name, out_shape=[jax.ShapeDtypeStruct(a.shape, a.dtype) for a in lands],
        in_specs=[any_spec] * n, out_specs=[any_spec] * n, input_output_aliases={a: a for a in range(n)},
        scratch_shapes=[pltpu.SemaphoreType.DMA((3 * n,)), pltpu.SemaphoreType.DMA((3 * n,))],
    )(*lands)


def _place_own(zones, owns, slot, name):
    n = len(zones)

    def body(slot_ref, *refs):
        for a in range(n):
            refs[2 * n + a][...] = refs[a][...]

    return pl.pallas_call(
        body, name=name,
        grid_spec=pltpu.PrefetchScalarGridSpec(
            num_scalar_prefetch=1, grid=(1,),
            in_specs=[pl.BlockSpec(o.shape, lambda i, s_ref: (0, 0)) for o in owns] + [pl.BlockSpec(memory_space=pl.ANY)] * n,
            out_specs=[pl.BlockSpec((None,) + o.shape, lambda i, s_ref: (s_ref[0], 0, 0)) for o in owns]),
        out_shape=[jax.ShapeDtypeStruct(z.shape, z.dtype) for z in zones],
        input_output_aliases={1 + n + a: a for a in range(n)},
        compiler_params=_params(("arbitrary",)),
    )(slot, *owns, *zones)


def _pair_sum(parts, got, core, name):
    R, C = parts.shape[-2:]
    tr = 256 if R % 256 == 0 else R
    mine = (pl.BlockSpec((None, None, tr, C), lambda j, i, c_ref: (c_ref[0], j, i, 0)) if parts.ndim == 4
            else pl.BlockSpec((None, tr, C), lambda j, i, c_ref: (2 * j + c_ref[0], i, 0)))

    def body(c_ref, p_ref, g_ref, o_ref):
        o_ref[...] = (p_ref[...].astype(F32) + g_ref[...].astype(F32)).astype(o_ref.dtype)

    return pl.pallas_call(
        body, name=name,
        grid_spec=pltpu.PrefetchScalarGridSpec(
            num_scalar_prefetch=1, grid=(4, R // tr),
            in_specs=[mine, pl.BlockSpec((None, tr, C), lambda j, i, c_ref: (j, i, 0))],
            out_specs=pl.BlockSpec((None, tr, C), lambda j, i, c_ref: (j, i, 0))),
        out_shape=jax.ShapeDtypeStruct((4, R, C), parts.dtype),
        compiler_params=_params(("parallel", "parallel")),
    )(core, parts, got)


def _matmul(a, b, mode, M, N, K, tm, tn, tk, out_dtype, name, b_noff=0, a_moff=0, a_blocked=False, b_blocked=False,
            out_blocked=None, dep=None):
    nm, nn, nk = M // tm, N // tn, K // tk
    if mode == "nn":
        a_spec = pl.BlockSpec((tm, tk), lambda j, i, k: (i, k))
        b_spec = pl.BlockSpec((tk, tn), lambda j, i, k: (k, j + b_noff))
        dn = NN
    elif mode == "nt":
        a_spec = pl.BlockSpec((tm, tk), lambda j, i, k: (i, k))
        if b_blocked:
            b_spec = pl.BlockSpec((None, tn, tk), lambda j, i, k: (k, j, 0))
        else:
            b_spec = pl.BlockSpec((tn, tk), lambda j, i, k: (j + b_noff, k))
        dn = NT
    else:
        if a_blocked:
            a_spec = pl.BlockSpec((None, tk, tm), lambda j, i, k: (i, k, 0))
        else:
            a_spec = pl.BlockSpec((tk, tm), lambda j, i, k: (k, i + a_moff))
        if b_blocked:
            b_spec = pl.BlockSpec((None, tk, tn), lambda j, i, k: (j, k, 0))
        else:
            b_spec = pl.BlockSpec((tk, tn), lambda j, i, k: (k, j + b_noff))
        dn = TN
    if out_blocked == "col":
        out_shape = jax.ShapeDtypeStruct((2, 4, M, tn), out_dtype)
        out_spec = pl.BlockSpec((None, None, tm, tn), lambda j, i, k: (j % 2, j // 2, i, 0))
    elif out_blocked == "row":
        out_shape = jax.ShapeDtypeStruct((2, 4, tm, N), out_dtype)
        out_spec = pl.BlockSpec((None, None, tm, tn), (lambda j, i, k: (i % 2, i // 2, 0, j)))
    elif out_blocked == "third":
        out_shape = jax.ShapeDtypeStruct((3, M, N // 3), out_dtype)
        out_spec = pl.BlockSpec((None, tm, tn), lambda j, i, k: (j // (nn // 3), i, j % (nn // 3)))
    else:
        out_shape = jax.ShapeDtypeStruct((M, N), out_dtype)
        out_spec = pl.BlockSpec((tm, tn), lambda j, i, k: (i, j))

    n_extra = int(dep is not None)

    def body(a_ref, b_ref, *rest):
        o_ref, scratch = rest[n_extra], rest[n_extra + 1:]
        if nk == 1:
            o_ref[...] = _dot(a_ref[...], b_ref[...], dn).astype(out_dtype)
        else:
            acc_ref, = scratch
            k = pl.program_id(2)

            @pl.when(k == 0)
            def _():
                acc_ref[...] = jnp.zeros_like(acc_ref)

            acc_ref[...] += _dot(a_ref[...], b_ref[...], dn)

            @pl.when(k == nk - 1)
            def _():
                o_ref[...] = acc_ref[...].astype(out_dtype)

    return pl.pallas_call(
        body, name=name, grid=(nn, nm, nk),
        in_specs=[a_spec, b_spec] + ([] if dep is None else [pl.BlockSpec((8, 128), lambda j, i, k: (0, 0))]),
        out_specs=out_spec, out_shape=out_shape,
        scratch_shapes=[] if nk == 1 else [pltpu.VMEM((tm, tn), F32)],
        compiler_params=_params(("parallel", "parallel", "arbitrary")),
    )(a, b, *([] if dep is None else [dep]))


def _rms_fwd(x, g, name):
    R, Dm = x.shape
    tr = min(R, 256)

    def body(x_ref, g_ref, h_ref):
        xv = x_ref[...]
        r = lax.rsqrt(jnp.mean(xv * xv, axis=-1, keepdims=True) + EPS)
        h_ref[...] = (xv * r * g_ref[...]).astype(BF)

    return pl.pallas_call(
        body, name=name, grid=(R // tr,),
        in_specs=[pl.BlockSpec((tr, Dm), lambda i: (i, 0)), pl.BlockSpec((1, Dm), lambda i: (0, 0))],
        out_specs=pl.BlockSpec((tr, Dm), lambda i: (i, 0)), out_shape=jax.ShapeDtypeStruct((R, Dm), BF),
        compiler_params=_params(("parallel",)),
    )(x, g)


def _rms_gain_grad(dn, x, name):
    R, Dm = x.shape

    def body(dn_ref, x_ref, o_ref):
        xv = x_ref[...]
        r = lax.rsqrt(jnp.mean(xv * xv, axis=-1, keepdims=True) + EPS)
        o_ref[...] = jnp.sum(dn_ref[...] * xv * r, axis=0, keepdims=True)

    return pl.pallas_call(
        body, name=name, out_shape=jax.ShapeDtypeStruct((1, Dm), F32),
        compiler_params=_params(),
    )(dn, x)


def _shift_down(v, k, head8, row, T):
    if k == 0:
        return v
    r = pltpu.roll(v, k, 0)
    hr = pltpu.roll(head8, k, 0)
    top = jnp.where(row[:8] < k, hr, r[:8])
    return jnp.concatenate([top, r[8:]], axis=0)


def _shift_up(v, k, tail8, row, T):
    if k == 0:
        return v
    r = pltpu.roll(v, T - k, 0)
    tr = pltpu.roll(tail8, 8 - k, 0)
    bot = jnp.where(row[:8] >= 8 - k, tr, r[T - 8:])
    return jnp.concatenate([r[:T - 8], bot], axis=0)


def _rglru_gates(u, head8, grow, row, T, cw_ref, cb_ref, wa_ref, ba_ref, wx_ref, bx_ref, lam_ref):
    us = [_shift_down(u, k, head8, row, T) for k in range(CONV_W)]
    acc = us[0] * cw_ref[0:1, :]
    for k in range(1, CONV_W):
        acc = acc + us[k] * cw_ref[k:k + 1, :]
    conv = cb_ref[...] + acc
    cbf = conv.astype(BF)
    r_ = _sigmoid(_dot(cbf, wa_ref[0], NN) + ba_ref[...])
    i_ = _sigmoid(_dot(cbf, wx_ref[0], NN) + bx_ref[...])
    sp = _softplus(-lam_ref[...])
    la = -LRU_C * r_ * sp
    a = jnp.exp(la)
    mult_raw = jnp.sqrt(-_expm1(2.0 * la))
    mult = jnp.where(grow == 0, 1.0, mult_raw)
    return us, conv, cbf, r_, i_, sp, a, mult_raw, mult


def _rglru_specs(T, nt, rev):
    tmap = (lambda n, t: (nt - 1 - t, n)) if rev else (lambda n, t: (t, n))
    hmap = ((lambda n, t: (jnp.maximum((nt - 1 - t) * (T // 8) - 1, 0), n)) if rev
            else (lambda n, t: (jnp.maximum(t * (T // 8) - 1, 0), n)))
    tile = pl.BlockSpec((T, RNN_BLOCK), tmap)
    halo = pl.BlockSpec((8, RNN_BLOCK), hmap)
    vec = pl.BlockSpec((1, RNN_BLOCK), lambda n, t: (0, n))
    cw = pl.BlockSpec((CONV_W, RNN_BLOCK), lambda n, t: (0, n))
    wblk = pl.BlockSpec((1, RNN_BLOCK, RNN_BLOCK), lambda n, t: (n, 0, 0))
    return tile, halo, vec, cw, wblk


def _rglru_fwd(xr, g, cw, cb, wa, ba, wx, bx, lam, T):
    S = xr.shape[0]
    nt = S // T

    def body(u_ref, uh_ref, g_ref, cw_ref, cb_ref, wa_ref, ba_ref, wx_ref, bx_ref, lam_ref, h_ref, y_ref, carry):
        t = pl.program_id(1)

        @pl.when(t == 0)
        def _():
            carry[...] = jnp.zeros_like(carry)

        row = lax.broadcasted_iota(jnp.int32, (T, RNN_BLOCK), 0)
        grow = row + t * T
        head8 = jnp.where(t > 0, uh_ref[...], 0.0)
        _, conv, _, _, i_, _, a, _, mult = _rglru_gates(u_ref[...], head8, grow, row, T, cw_ref, cb_ref, wa_ref, ba_ref,
                                                         wx_ref, bx_ref, lam_ref)
        b = mult * i_ * conv
        s = 1
        while s < T:
            keep = row >= s
            a_s = jnp.where(keep, pltpu.roll(a, s, 0), 1.0)
            b_s = jnp.where(keep, pltpu.roll(b, s, 0), 0.0)
            b = a * b_s + b
            a = a * a_s
            s *= 2
        h = b + a * carry[0:1, :]
        carry[...] = jnp.broadcast_to(h[T - 1:T, :], carry.shape)
        h_ref[...] = h
        gv = g_ref[...]
        y_ref[...] = (h * (gv * _sigmoid(gv))).astype(BF)

    tile, halo, vec, cwspec, wblk = _rglru_specs(T, nt, False)
    return pl.pallas_call(
        body, name="rglru_fwd", grid=(RNN_BLOCKS, nt),
        in_specs=[tile, halo, tile, cwspec, vec, wblk, vec, wblk, vec, vec],
        out_specs=[tile, tile],
        out_shape=[jax.ShapeDtypeStruct((S, D_RNN), F32), jax.ShapeDtypeStruct((S, D_RNN), BF)],
        scratch_shapes=[pltpu.VMEM((8, RNN_BLOCK), F32)],
        compiler_params=_params(("parallel", "arbitrary")),
    )(xr, xr, g, cw, cb, wa, ba, wx, bx, lam)


def _rglru_bwd(xr, g, h, dy, cw, cb, wa, ba, wx, bx, lam, T):
    S = xr.shape[0]
    nt = S // T

    def body(u_ref, uh_ref, g_ref, h_ref, hh_ref, dy_ref, cw_ref, cb_ref, wa_ref, ba_ref, wx_ref, bx_ref, lam_ref,
             du_ref, dg_ref, dwa_ref, dwx_ref, dvec_ref, c_dhh, c_a, c_dconv):
        t = pl.program_id(1)
        tt = nt - 1 - t

        @pl.when(t == 0)
        def _():
            c_dhh[...] = jnp.zeros_like(c_dhh)
            c_a[...] = jnp.zeros_like(c_a)
            c_dconv[...] = jnp.zeros_like(c_dconv)
            dwa_ref[...] = jnp.zeros_like(dwa_ref)
            dwx_ref[...] = jnp.zeros_like(dwx_ref)
            dvec_ref[...] = jnp.zeros_like(dvec_ref)

        row = lax.broadcasted_iota(jnp.int32, (T, RNN_BLOCK), 0)
        row8 = row[:8]
        grow = row + tt * T
        head8 = jnp.where(tt > 0, uh_ref[...], 0.0)
        us, conv, cbf, r_, i_, sp, a, mult_raw, mult = _rglru_gates(
            u_ref[...], head8, grow, row, T, cw_ref, cb_ref, wa_ref, ba_ref, wx_ref, bx_ref, lam_ref)
        hv = h_ref[...]
        hprev = _shift_down(hv, 1, jnp.where(tt > 0, hh_ref[...], 0.0), row, T)
        gv = g_ref[...]
        sg = _sigmoid(gv)
        dyv = dy_ref[...]
        dg_ref[...] = (dyv * hv * (sg * (1.0 + gv * (1.0 - sg)))).astype(BF)
        d = dyv * (gv * sg)
        A = _shift_up(a, 1, c_a[...], row, T)
        s = 1
        while s < T:
            keep = row < T - s
            A_s = jnp.where(keep, pltpu.roll(A, T - s, 0), 1.0)
            d_s = jnp.where(keep, pltpu.roll(d, T - s, 0), 0.0)
            d = A * d_s + d
            A = A * A_s
            s *= 2
        dhh = d + A * c_dhh[0:1, :]
        da = dhh * hprev
        dconv = dhh * mult * i_
        di = dhh * mult * conv
        dmult = dhh * i_ * conv
        dla = da * a - jnp.where(grow == 0, 0.0, dmult * (a * a) / mult_raw)
        dr = dla * (-LRU_C * sp)
        dsp = jnp.sum(dla * (-LRU_C * r_), axis=0, keepdims=True)
        dza = dr * r_ * (1.0 - r_)
        dzx = di * i_ * (1.0 - i_)
        dza_b, dzx_b = dza.astype(BF), dzx.astype(BF)
        dconv = dconv + _dot(dza_b, wa_ref[0], NT) + _dot(dzx_b, wx_ref[0], NT)
        dwa_ref[0] += _dot(cbf, dza_b, TN)
        dwx_ref[0] += _dot(cbf, dzx_b, TN)
        lam = lam_ref[...]
        rows = [jnp.sum(dconv * us[k], axis=0, keepdims=True) for k in range(CONV_W)]
        rows += [jnp.sum(dconv, axis=0, keepdims=True), jnp.sum(dza, axis=0, keepdims=True),
                 jnp.sum(dzx, axis=0, keepdims=True), dsp * (-_sigmoid(-lam))]
        upd = jnp.zeros((8, RNN_BLOCK), F32)
        for j, rv in enumerate(rows):
            upd = upd + jnp.where(row8 == j, rv, 0.0)
        dvec_ref[...] += upd
        tail8 = c_dconv[...]
        du = dconv * cw_ref[0:1, :]
        for k in range(1, CONV_W):
            du = du + _shift_up(dconv, k, tail8, row, T) * cw_ref[k:k + 1, :]
        du_ref[...] = du.astype(BF)
        c_dhh[...] = jnp.broadcast_to(dhh[0:1, :], c_dhh.shape)
        c_a[...] = jnp.broadcast_to(a[0:1, :], c_a.shape)
        c_dconv[...] = dconv[:8]

    tile, halo, vec, cwspec, wblk = _rglru_specs(T, nt, True)
    acc8 = pl.BlockSpec((8, RNN_BLOCK), lambda n, t: (0, n))
    return pl.pallas_call(
        body, name="rglru_bwd", grid=(RNN_BLOCKS, nt),
        in_specs=[tile, halo, tile, tile, halo, tile, cwspec, vec, wblk, vec, wblk, vec, vec],
        out_specs=[tile, tile, wblk, wblk, acc8],
        out_shape=[jax.ShapeDtypeStruct((S, D_RNN), BF), jax.ShapeDtypeStruct((S, D_RNN), BF),
                   jax.ShapeDtypeStruct((RNN_BLOCKS, RNN_BLOCK, RNN_BLOCK), F32),
                   jax.ShapeDtypeStruct((RNN_BLOCKS, RNN_BLOCK, RNN_BLOCK), F32),
                   jax.ShapeDtypeStruct((8, D_RNN), F32)],
        scratch_shapes=[pltpu.VMEM((8, RNN_BLOCK), F32)] * 3,
        compiler_params=_params(("parallel", "arbitrary")),
    )(xr, xr, g, h, h, dy, cw, cb, wa, ba, wx, bx, lam)


def _rel_bucket_map():
    qi = np.arange(WINDOW)[:, None]
    kj = np.arange(2 * WINDOW)[None, :]
    dist = jnp.asarray(qi + WINDOW - kj, jnp.int32)
    n = jnp.maximum(dist, 0)
    max_exact = REL_BUCKETS // 2
    ratio = jnp.log(jnp.maximum(n, 1).astype(F32) / max_exact) / math.log(REL_MAX_DIST / max_exact)
    large = jnp.minimum(max_exact + (ratio * (REL_BUCKETS - max_exact)).astype(jnp.int32), REL_BUCKETS - 1)
    bucket = jnp.where(n < max_exact, n, large).astype(jnp.int32)
    j = np.arange(WINDOW)[None, :]
    return jnp.where(jnp.asarray(j > qi), bucket[:, :WINDOW], bucket[:, WINDOW:])


def _swa_common(n, kv_ref, bucket_ref, relb_ref, bias_scr):
    @pl.when(n == 0)
    def _():
        bk = bucket_ref[...]
        for h in range(SWA_HEADS):
            acc = jnp.zeros((WINDOW, WINDOW), F32)
            for b in range(REL_BUCKETS):
                acc = acc + jnp.where(bk == b, relb_ref[b, h], 0.0)
            bias_scr[h] = acc

    prev0 = pl.multiple_of(jnp.maximum(n - 1, 0) * WINDOW, WINDOW)
    cur0 = pl.multiple_of(n * WINDOW, WINDOW)
    kk = jnp.concatenate([kv_ref[pl.ds(prev0, WINDOW), :], kv_ref[pl.ds(cur0, WINDOW), :]], axis=0).astype(F32)
    rowi = lax.broadcasted_iota(jnp.int32, (WINDOW, WINDOW), 0)
    col = lax.broadcasted_iota(jnp.int32, (WINDOW, WINDOW), 1)
    from_prev = col > rowi
    return kk, from_prev, prev0, cur0


def _fold(full, from_prev):
    return jnp.where(from_prev, full[:, :WINDOW], full[:, WINDOW:])


def _unfold(sq, from_prev):
    return jnp.concatenate([jnp.where(from_prev, sq, 0.0), jnp.where(from_prev, 0.0, sq)], axis=1)


def _half_pair(part, kvh):
    lo = lax.broadcasted_iota(jnp.int32, part.shape, 1) < SWA_HD
    if kvh == 0:
        pa = jnp.where(lo, part, 0.0)
        pb = pltpu.roll(pa, SWA_HD, 1)
    else:
        pb = jnp.where(lo, 0.0, part)
        pa = pltpu.roll(pb, SWA_HD, 1)
    return pa.astype(BF), pb.astype(BF)


ALL_HEADS = SWA_HEADS * WINDOW


def _sink_column(sinks):
    return jnp.repeat(sinks.reshape(SWA_HEADS), WINDOW).reshape(ALL_HEADS, 1)


def _swa_operands(kk):
    return [(_half_pair(kk[:, :128], kvh), _half_pair(kk[:, 128:], kvh)) for kvh in range(SWA_KV_HEADS)]


def _swa_probs(n, q_ref, ops, bias_scr, sinkc_ref, from_prev):
    lgs = []
    for kvh in range(SWA_KV_HEADS):
        (ka, kb), _ = ops[kvh]
        for p in range(4):
            q2 = q_ref[:, kvh * 512 + p * 128:kvh * 512 + p * 128 + 128]
            lgs += [_fold(_dot(q2, ka, NT), from_prev), _fold(_dot(q2, kb, NT), from_prev)]
    lg = jnp.concatenate(lgs, axis=0) * (SWA_HD ** -0.5) + bias_scr[...].reshape(ALL_HEADS, WINDOW)
    rowi = jnp.bitwise_and(lax.broadcasted_iota(jnp.int32, (ALL_HEADS, WINDOW), 0), WINDOW - 1)
    col = lax.broadcasted_iota(jnp.int32, (ALL_HEADS, WINDOW), 1)
    no_prev = jnp.where(n > 0, 0, 4 * WINDOW)
    lg = jnp.where(jnp.logical_or(col <= rowi, col > rowi + no_prev), lg, NEG_INF)
    sink = sinkc_ref[...]
    m = jnp.maximum(jnp.max(lg, axis=-1, keepdims=True), sink)
    e = jnp.exp(lg - m)
    es = jnp.exp(sink - m)
    den = jnp.sum(e, axis=-1, keepdims=True) + es
    return e / den, es / den


def _swa_fwd(q, kv, g, bucket, rel_bias, sink_col):
    S = q.shape[0]
    nb = S // WINDOW

    def body(q_ref, kv_ref, g_ref, bucket_ref, relb_ref, sinkc_ref, o_ref, y_ref, bias_scr):
        n = pl.program_id(0)
        kk, from_prev, _, _ = _swa_common(n, kv_ref, bucket_ref, relb_ref, bias_scr)
        ops = _swa_operands(kk)
        pr, _ = _swa_probs(n, q_ref, ops, bias_scr, sinkc_ref, from_prev)
        for kvh in range(SWA_KV_HEADS):
            _, (va, vb) = ops[kvh]
            for p in range(4):
                c0 = kvh * 512 + p * 128
                r0 = (kvh * 8 + 2 * p) * WINDOW
                o2 = (_dot(_unfold(pr[r0:r0 + WINDOW], from_prev).astype(BF), va, NN)
                      + _dot(_unfold(pr[r0 + WINDOW:r0 + 2 * WINDOW], from_prev).astype(BF), vb, NN))
                o_ref[:, c0:c0 + 128] = o2
                gv = g_ref[:, c0:c0 + 128]
                y_ref[:, c0:c0 + 128] = (o2 * (gv * _sigmoid(gv))).astype(BF)

    blk = pl.BlockSpec((WINDOW, 1024), lambda n: (n, 0))
    smem = pl.BlockSpec(memory_space=pltpu.SMEM)
    sinkc = pl.BlockSpec((ALL_HEADS, 1), lambda n: (0, 0))
    return pl.pallas_call(
        body, name="swa_fwd", grid=(nb,),
        in_specs=[blk, pl.BlockSpec((S, 256), lambda n: (0, 0)), blk, pl.BlockSpec((WINDOW, WINDOW), lambda n: (0, 0)), smem, sinkc],
        out_specs=[blk, blk],
        out_shape=[jax.ShapeDtypeStruct((S, 1024), F32), jax.ShapeDtypeStruct((S, 1024), BF)],
        scratch_shapes=[pltpu.VMEM((SWA_HEADS, WINDOW, WINDOW), F32)],
        compiler_params=_params(("arbitrary",)),
    )(q, kv, g, bucket, rel_bias, sink_col)


def _swa_bwd(q, kv, g, o, dy, bucket, rel_bias, sink_col):
    S = q.shape[0]
    nb = S // WINDOW

    def body(q_ref, kv_ref, g_ref, o_ref, dy_ref, bucket_ref, relb_ref, sinkc_ref,
             dq_ref, dg_ref, dkv_ref, dsink_ref, drel_ref, bias_scr, dbias_scr, dsink_scr):
        n = pl.program_id(0)

        @pl.when(n == 0)
        def _():
            dbias_scr[...] = jnp.zeros_like(dbias_scr)
            dsink_scr[...] = jnp.zeros_like(dsink_scr)
            dkv_ref[...] = jnp.zeros_like(dkv_ref)

        kk, from_prev, prev0, cur0 = _swa_common(n, kv_ref, bucket_ref, relb_ref, bias_scr)
        ops = _swa_operands(kk)
        pr, ps = _swa_probs(n, q_ref, ops, bias_scr, sinkc_ref, from_prev)
        do2s, dps = [], []
        for kvh in range(SWA_KV_HEADS):
            _, (va, vb) = ops[kvh]
            for p in range(4):
                c0 = kvh * 512 + p * 128
                gv = g_ref[:, c0:c0 + 128]
                sg = _sigmoid(gv)
                dyv = dy_ref[:, c0:c0 + 128]
                dg_ref[:, c0:c0 + 128] = (dyv * o_ref[:, c0:c0 + 128] * (sg * (1.0 + gv * (1.0 - sg)))).astype(BF)
                do2 = (dyv * (gv * sg)).astype(BF)
                do2s.append(do2)
                dps += [_fold(_dot(do2, va, NT), from_prev), _fold(_dot(do2, vb, NT), from_prev)]
        dp = jnp.concatenate(dps, axis=0)
        delta = jnp.sum(pr * dp, axis=-1, keepdims=True)
        ds = pr * (dp - delta)
        dbias_scr[...] += ds.reshape(SWA_HEADS, WINDOW, WINDOW)
        dsink_scr[...] += ps * delta
        dsc = ds * (SWA_HD ** -0.5)
        lo256 = lax.broadcasted_iota(jnp.int32, (2 * WINDOW, 128), 1) < SWA_HD
        dks, dvs = [], []
        for kvh in range(SWA_KV_HEADS):
            (ka, kb), _ = ops[kvh]
            dka = jnp.zeros((2 * WINDOW, 128), F32)
            dkb, dva, dvb = dka, dka, dka
            for p in range(4):
                c0 = kvh * 512 + p * 128
                r0 = (kvh * 8 + 2 * p) * WINDOW
                q2 = q_ref[:, c0:c0 + 128]
                do2 = do2s[kvh * 4 + p]
                ds0 = _unfold(dsc[r0:r0 + WINDOW], from_prev).astype(BF)
                ds1 = _unfold(dsc[r0 + WINDOW:r0 + 2 * WINDOW], from_prev).astype(BF)
                dq_ref[:, c0:c0 + 128] = (_dot(ds0, ka, NN) + _dot(ds1, kb, NN)).astype(BF)
                dka = dka + _dot(ds0, q2, TN)
                dkb = dkb + _dot(ds1, q2, TN)
                dva = dva + _dot(_unfold(pr[r0:r0 + WINDOW], from_prev).astype(BF), do2, TN)
                dvb = dvb + _dot(_unfold(pr[r0 + WINDOW:r0 + 2 * WINDOW], from_prev).astype(BF), do2, TN)
            dks.append(jnp.where(lo256, dka, 0.0) + pltpu.roll(jnp.where(lo256, 0.0, dkb), SWA_HD, 1))
            dvs.append(jnp.where(lo256, dva, 0.0) + pltpu.roll(jnp.where(lo256, 0.0, dvb), SWA_HD, 1))
        dk = dks[0] + pltpu.roll(dks[1], SWA_HD, 1)
        dv = dvs[0] + pltpu.roll(dvs[1], SWA_HD, 1)
        dkv_ref[pl.ds(prev0, WINDOW), 0:128] += dk[:WINDOW]
        dkv_ref[pl.ds(prev0, WINDOW), 128:256] += dv[:WINDOW]
        dkv_ref[pl.ds(cur0, WINDOW), 0:128] += dk[WINDOW:]
        dkv_ref[pl.ds(cur0, WINDOW), 128:256] += dv[WINDOW:]

        @pl.when(n == nb - 1)
        def _():
            dsink_ref[...] = -jnp.sum(dsink_scr[...].reshape(SWA_HEADS, WINDOW, 1), axis=1)
            bk = bucket_ref[...]
            sums = []
            for b in range(REL_BUCKETS):
                sums.append(jnp.sum(jnp.where((bk == b)[None], dbias_scr[...], 0.0), axis=1))
            drel_ref[...] = jnp.sum(jnp.concatenate(sums, axis=0), axis=1, keepdims=True)

    blk = pl.BlockSpec((WINDOW, 1024), lambda n: (n, 0))
    smem = pl.BlockSpec(memory_space=pltpu.SMEM)
    whole = lambda shape: pl.BlockSpec(shape, lambda n: (0, 0))
    return pl.pallas_call(
        body, name="swa_bwd", grid=(nb,),
        in_specs=[blk, whole((S, 256)), blk, blk, blk, whole((WINDOW, WINDOW)), smem, whole((ALL_HEADS, 1))],
        out_specs=[blk, blk, whole((S, 256)), whole((SWA_HEADS, 1)), whole((REL_BUCKETS * SWA_HEADS, 1))],
        out_shape=[jax.ShapeDtypeStruct((S, 1024), BF), jax.ShapeDtypeStruct((S, 1024), BF),
                   jax.ShapeDtypeStruct((S, 256), F32), jax.ShapeDtypeStruct((SWA_HEADS, 1), F32),
                   jax.ShapeDtypeStruct((REL_BUCKETS * SWA_HEADS, 1), F32)],
        scratch_shapes=[pltpu.VMEM((SWA_HEADS, WINDOW, WINDOW), F32), pltpu.VMEM((SWA_HEADS, WINDOW, WINDOW), F32),
                        pltpu.VMEM((ALL_HEADS, 1), F32)],
        compiler_params=_params(("arbitrary",)),
    )(q, kv, g, o, dy, bucket, rel_bias, sink_col)


def _mem_probs(qh, mk):
    lg = _dot(qh, mk, NT) * (MEM_HD ** -0.5)
    e = jnp.exp(lg - jnp.max(lg, axis=-1, keepdims=True))
    return e / jnp.sum(e, axis=-1, keepdims=True)


def _mem_fwd(q, mkv, g):
    S = q.shape[0]
    M = mkv.shape[0]
    tq = 256

    def body(q_ref, mkv_ref, g_ref, o_ref, y_ref):
        for h in range(MEM_HEADS):
            c0 = h * MEM_HD
            pr = _mem_probs(q_ref[:, c0:c0 + MEM_HD], mkv_ref[:, c0:c0 + MEM_HD])
            o = _dot(pr.astype(BF), mkv_ref[:, D_MEM + c0:D_MEM + c0 + MEM_HD], NN)
            o_ref[:, c0:c0 + MEM_HD] = o
            gv = g_ref[:, c0:c0 + MEM_HD]
            y_ref[:, c0:c0 + MEM_HD] = (o * (gv * _sigmoid(gv))).astype(BF)

    blk = pl.BlockSpec((tq, D_MEM), lambda i: (i, 0))
    return pl.pallas_call(
        body, name="mem_fwd", grid=(S // tq,),
        in_specs=[blk, pl.BlockSpec((M, 2 * D_MEM), lambda i: (0, 0)), blk], out_specs=[blk, blk],
        out_shape=[jax.ShapeDtypeStruct((S, D_MEM), F32), jax.ShapeDtypeStruct((S, D_MEM), BF)],
        compiler_params=_params(("parallel",)),
    )(q, mkv, g)


def _mem_bwd(q, mkv, g, o, dy):
    S = q.shape[0]
    M = mkv.shape[0]
    tq = 256

    def body(q_ref, mkv_ref, g_ref, o_ref, dy_ref, dq_ref, dg_ref, dmkv_ref):
        @pl.when(pl.program_id(0) == 0)
        def _():
            dmkv_ref[...] = jnp.zeros_like(dmkv_ref)

        for h in range(MEM_HEADS):
            c0 = h * MEM_HD
            qh = q_ref[:, c0:c0 + MEM_HD]
            mk = mkv_ref[:, c0:c0 + MEM_HD]
            mv = mkv_ref[:, D_MEM + c0:D_MEM + c0 + MEM_HD]
            gv = g_ref[:, c0:c0 + MEM_HD]
            sg = _sigmoid(gv)
            dyv = dy_ref[:, c0:c0 + MEM_HD]
            dg_ref[:, c0:c0 + MEM_HD] = (dyv * o_ref[:, c0:c0 + MEM_HD] * (sg * (1.0 + gv * (1.0 - sg)))).astype(BF)
            do = (dyv * (gv * sg)).astype(BF)
            pr = _mem_probs(qh, mk)
            dp = _dot(do, mv, NT)
            ds = pr * (dp - jnp.sum(pr * dp, axis=-1, keepdims=True))
            dsb = (ds * (MEM_HD ** -0.5)).astype(BF)
            dq_ref[:, c0:c0 + MEM_HD] = _dot(dsb, mk, NN).astype(BF)
            dmkv_ref[:, c0:c0 + MEM_HD] += _dot(dsb, qh, TN)
            dmkv_ref[:, D_MEM + c0:D_MEM + c0 + MEM_HD] += _dot(pr.astype(BF), do, TN)

    blk = pl.BlockSpec((tq, D_MEM), lambda i: (i, 0))
    whole = pl.BlockSpec((M, 2 * D_MEM), lambda i: (0, 0))
    return pl.pallas_call(
        body, name="mem_bwd", grid=(S // tq,),
        in_specs=[blk, whole, blk, blk, blk], out_specs=[blk, blk, whole],
        out_shape=[jax.ShapeDtypeStruct((S, D_MEM), BF), jax.ShapeDtypeStruct((S, D_MEM), BF),
                   jax.ShapeDtypeStruct((M, 2 * D_MEM), F32)],
        compiler_params=_params(("arbitrary",)),
    )(q, mkv, g, o, dy)


MERGE_TN = 512


def _merge_specs(tm):
    ytile = pl.BlockSpec((tm, 1024), lambda i, j: (i, 0))
    wblk = pl.BlockSpec((MERGE_TN, 1024), lambda i, j: (j, 0))
    gls = [pl.BlockSpec((None, tm, MERGE_TN), (lambda i, j, br=br: (br, i, j))) for br in range(3)]
    otile = pl.BlockSpec((tm, MERGE_TN), lambda i, j: (i, j))
    return ytile, wblk, gls, otile


def _merge_fwd(ys, ws, gl, tm):
    S = gl.shape[1]

    def body(y0, y1, y2, w0, w1, w2, g0, g1, g2, o_ref):
        acc = None
        for y_ref, w_ref, g_ref in ((y0, w0, g0), (y1, w1, g1), (y2, w2, g2)):
            term = _sigmoid(g_ref[...]) * _dot(y_ref[...], w_ref[...], NT)
            acc = term if acc is None else acc + term
        o_ref[...] = acc.astype(BF)

    ytile, wblk, gls, otile = _merge_specs(tm)
    return pl.pallas_call(
        body, name="merge_fwd", grid=(S // tm, D_MODEL // MERGE_TN),
        in_specs=[ytile] * 3 + [wblk] * 3 + gls, out_specs=otile,
        out_shape=jax.ShapeDtypeStruct((S, D_MODEL), BF),
        compiler_params=_params(("parallel", "arbitrary")),
    )(*ys, *ws, gl, gl, gl)


def _merge_bwd(dout, w_out, ys, ws, gl, tm):
    S = gl.shape[1]

    def body(do_ref, wo_ref, y0, y1, y2, w0, w1, w2, g0, g1, g2, dg0, dg1, dg2, dp0, dp1, dp2):
        dm = _dot(do_ref[...], wo_ref[...], NT)
        for y_ref, w_ref, g_ref, dg_ref, dp_ref in ((y0, w0, g0, dg0, dp0), (y1, w1, g1, dg1, dp1), (y2, w2, g2, dg2, dp2)):
            gate = _sigmoid(g_ref[...])
            pv = _dot(y_ref[...], w_ref[...], NT)
            dg_ref[...] = (dm * pv * gate * (1.0 - gate)).astype(BF)
            dp_ref[...] = (dm * gate).astype(BF)

    ytile, wblk, gls, otile = _merge_specs(tm)
    out = jax.ShapeDtypeStruct((S, D_MODEL), BF)
    return pl.pallas_call(
        body, name="merge_bwd", grid=(S // tm, D_MODEL // MERGE_TN),
        in_specs=[pl.BlockSpec((tm, D_MODEL), lambda i, j: (i, 0)), pl.BlockSpec((MERGE_TN, D_MODEL), lambda i, j: (j, 0))]
        + [ytile] * 3 + [wblk] * 3 + gls,
        out_specs=[otile] * 6, out_shape=[out] * 6,
        compiler_params=_params(("parallel", "arbitrary")),
    )(dout, w_out, *ys, *ws, gl, gl, gl)


def _out_loss(merged, w_out, x, target, post_g, tm):
    S = x.shape[0]

    def body(m_ref, w_ref, x_ref, t_ref, g_ref, dout_ref, dy_ref, loss_ref, dpost_ref):
        @pl.when(pl.program_id(0) == 0)
        def _():
            loss_ref[...] = jnp.zeros_like(loss_ref)
            dpost_ref[...] = jnp.zeros_like(dpost_ref)

        out = _dot(m_ref[...], w_ref[...], NN)
        r = lax.rsqrt(jnp.mean(out * out, axis=-1, keepdims=True) + EPS)
        nrm = out * r
        gv = g_ref[...]
        err = (x_ref[...] + nrm * gv) - t_ref[...]
        sq = jnp.sum(jnp.sum(err * err, axis=1, keepdims=True), axis=0, keepdims=True)
        loss_ref[...] += sq * (0.5 / D_MODEL)
        dy = err * (1.0 / D_MODEL)
        dy_ref[...] = dy
        dpost_ref[...] += jnp.sum(dy * nrm, axis=0, keepdims=True)
        dn = dy * gv
        dout_ref[...] = (r * (dn - nrm * jnp.mean(dn * nrm, axis=-1, keepdims=True))).astype(BF)

    row = pl.BlockSpec((tm, D_MODEL), lambda i: (i, 0))
    return pl.pallas_call(
        body, name="out_loss", grid=(S // tm,),
        in_specs=[row, pl.BlockSpec((D_MODEL, D_MODEL), lambda i: (0, 0)), row, row, pl.BlockSpec((1, D_MODEL), lambda i: (0, 0))],
        out_specs=[row, row, pl.BlockSpec((8, 128), lambda i: (0, 0)), pl.BlockSpec((1, D_MODEL), lambda i: (0, 0))],
        out_shape=[jax.ShapeDtypeStruct((S, D_MODEL), BF), jax.ShapeDtypeStruct((S, D_MODEL), F32),
                   jax.ShapeDtypeStruct((8, 128), F32), jax.ShapeDtypeStruct((1, D_MODEL), F32)],
        compiler_params=_params(("arbitrary",)),
    )(merged, w_out, x, target, post_g)


def _dh_dx(dproj, w_in, x, dy, pre_g, tm, tk):
    S = x.shape[0]
    nk = D_IN // tk

    def body(dp_ref, w_ref, x_ref, dy_ref, g_ref, dx_ref, dpre_ref, acc_ref):
        i, k = pl.program_id(0), pl.program_id(1)

        @pl.when(jnp.logical_and(i == 0, k == 0))
        def _():
            dpre_ref[...] = jnp.zeros_like(dpre_ref)

        @pl.when(k == 0)
        def _():
            acc_ref[...] = jnp.zeros_like(acc_ref)

        acc_ref[...] += _dot(dp_ref[...], w_ref[...], NN)

        @pl.when(k == nk - 1)
        def _():
            dh = acc_ref[...]
            xv = x_ref[...]
            r = lax.rsqrt(jnp.mean(xv * xv, axis=-1, keepdims=True) + EPS)
            nrm = xv * r
            dpre_ref[...] += jnp.sum(dh * nrm, axis=0, keepdims=True)
            dn = dh * g_ref[...]
            dx_ref[...] = r * (dn - nrm * jnp.mean(dn * nrm, axis=-1, keepdims=True)) + dy_ref[...]

    row = pl.BlockSpec((tm, D_MODEL), lambda i, k: (i, 0))
    vec = pl.BlockSpec((1, D_MODEL), lambda i, k: (0, 0))
    return pl.pallas_call(
        body, name="dh_dx", grid=(S // tm, nk),
        in_specs=[pl.BlockSpec((tm, tk), lambda i, k: (i, k)), pl.BlockSpec((tk, D_MODEL), lambda i, k: (k, 0)), row, row, vec],
        out_specs=[row, vec],
        out_shape=[jax.ShapeDtypeStruct((S, D_MODEL), F32), jax.ShapeDtypeStruct((1, D_MODEL), F32)],
        scratch_shapes=[pltpu.VMEM((tm, D_MODEL), F32)],
        compiler_params=_params(("arbitrary", "arbitrary"), large=True),
    )(dproj, w_in, x, dy, pre_g)


def _sum_parts(parts, name):
    P, R, C = parts.shape
    tr = max(t for t in range(8, 513, 8) if R % t == 0)

    def body(p_ref, o_ref):
        acc = p_ref[0]
        for j in range(1, P):
            acc = acc + p_ref[j]
        o_ref[...] = acc

    return pl.pallas_call(
        body, name=name, grid=(R // tr,),
        in_specs=[pl.BlockSpec((P, tr, C), lambda i: (0, i, 0))], out_specs=pl.BlockSpec((tr, C), lambda i: (i, 0)),
        out_shape=jax.ShapeDtypeStruct((R, C), F32), compiler_params=_params(("parallel",)),
    )(parts)


def _adamw(lands, sums, chip, w, m, v, name):
    lands = list(lands) if isinstance(lands, (list, tuple)) else [lands]
    sums = list(sums) if isinstance(sums, (list, tuple)) else [sums]
    _, R, cols = lands[0].shape
    C = cols * len(lands)
    tr = max(t for t in range(16, 257, 16) if R % t == 0)
    c1 = 1.0 - ADAM_B1 ** ADAM_STEP
    c2 = 1.0 - ADAM_B2 ** ADAM_STEP

    def body(chip_ref, *refs):
        p_refs = refs[:4 * len(lands)]
        w_ref, m_ref, v_ref, g_ref, d_ref, nm_ref, nv_ref = refs[4 * len(lands):]
        gs = []
        for q in range(len(lands)):
            gq = p_refs[4 * q + 3][...].astype(F32)
            for j in range(3):
                gq = gq + p_refs[4 * q + j][...].astype(F32)
            gs.append(gq)
        g = gs[0] if len(gs) == 1 else jnp.concatenate(gs, axis=1)
        nm = ADAM_B1 * m_ref[...] + (1.0 - ADAM_B1) * g
        nv = ADAM_B2 * v_ref[...] + (1.0 - ADAM_B2) * (g * g)
        g_ref[...] = g
        nm_ref[...] = nm
        nv_ref[...] = nv
        d_ref[...] = -ADAM_LR * ((nm / c1) / (jnp.sqrt(nv / c2) + ADAM_EPS) + ADAM_WD * w_ref[...])

    tile = pl.BlockSpec((None, tr, C), lambda i, c_ref: (0, i, 0))
    specs, operands = [], []
    for q in range(len(lands)):
        for k in range(3):
            specs.append(pl.BlockSpec((None, tr, cols), (lambda i, c_ref, k=k: (k + (c_ref[0] <= k).astype(jnp.int32), i, 0))))
            operands.append(lands[q])
        specs.append(pl.BlockSpec((None, tr, cols), (lambda i, c_ref: (c_ref[0], i, 0))))
        operands.append(sums[q])
    return pl.pallas_call(
        body, name=name,
        grid_spec=pltpu.PrefetchScalarGridSpec(num_scalar_prefetch=1, grid=(R // tr,), in_specs=specs + [tile, tile, tile],
                                               out_specs=[tile] * 4),
        out_shape=[jax.ShapeDtypeStruct((1, R, C), F32)] * 4, compiler_params=_params(("parallel",)),
    )(chip, *operands, w, m, v)


def _adamw_small(gs, ws, ms, vs):
    n = len(ws)
    c1 = 1.0 - ADAM_B1 ** ADAM_STEP
    c2 = 1.0 - ADAM_B2 ** ADAM_STEP

    def flat2(a):
        return a.reshape(-1, a.shape[-1])

    def body(*refs):
        ins, outs = refs[:4 * n], refs[4 * n:]
        for a in range(n):
            g, w, m, v = (ins[k * n + a][...] for k in range(4))
            nm = ADAM_B1 * m + (1.0 - ADAM_B1) * g
            nv = ADAM_B2 * v + (1.0 - ADAM_B2) * (g * g)
            outs[a][...] = g
            outs[n + a][...] = -ADAM_LR * ((nm / c1) / (jnp.sqrt(nv / c2) + ADAM_EPS) + ADAM_WD * w)
            outs[2 * n + a][...] = nm
            outs[3 * n + a][...] = nv

    shapes = [flat2(w).shape for w in ws]
    out = pl.pallas_call(
        body, name="adamw_small", out_shape=[jax.ShapeDtypeStruct(sh, F32) for sh in shapes] * 4,
        compiler_params=_params(),
    )(*[g.reshape(sh) for g, sh in zip(gs, shapes)], *[flat2(a) for a in (*ws, *ms, *vs)])
    return [[out[k * n + a].reshape(ws[a].shape) for a in range(n)] for k in range(4)]


def _project(h, w_t):
    S = h.shape[0]
    n_tiles = D_IN // SEG_TILE
    ranges = [(c0 // SEG_TILE, (c0 + width) // SEG_TILE) for _, c0, width, _ in SEGMENTS]

    def body(h_ref, w_ref, *outs):
        j = pl.program_id(0)
        prod = _dot(h_ref[...], w_ref[...], NT)
        for (j0, j1), (_, _, _, dt), o_ref in zip(ranges, SEGMENTS, outs):
            @pl.when(jnp.logical_and(j >= j0, j < j1))
            def _(o_ref=o_ref, dt=dt):
                o_ref[...] = prod.astype(dt)

    out_shapes, out_specs = [], []
    for (j0, j1), (name, _, width, dt) in zip(ranges, SEGMENTS):
        if name == "gl":
            per = (j1 - j0) // 3
            out_shapes.append(jax.ShapeDtypeStruct((3, S, width // 3), dt))
            out_specs.append(pl.BlockSpec((None, S, SEG_TILE), (lambda j, j0=j0, j1=j1, per=per: (
                jnp.clip(j - j0, 0, j1 - j0 - 1) // per, 0, jnp.clip(j - j0, 0, j1 - j0 - 1) % per))))
        else:
            out_shapes.append(jax.ShapeDtypeStruct((S, width), dt))
            out_specs.append(pl.BlockSpec((S, SEG_TILE), (lambda j, j0=j0, j1=j1: (0, jnp.clip(j - j0, 0, j1 - j0 - 1)))))
    outs = pl.pallas_call(
        body, name="proj", grid=(n_tiles,),
        in_specs=[pl.BlockSpec((S, D_MODEL), lambda j: (0, 0)), pl.BlockSpec((SEG_TILE, D_MODEL), lambda j: (j, 0))],
        out_specs=out_specs, out_shape=out_shapes,
        compiler_params=_params(("arbitrary",), large=True),
    )(h, w_t)
    return {name: o for (name, _, _, _), o in zip(SEGMENTS, outs)}


def _forward_a(x, mem, pre_g, mem_g, w_in, conv_w, conv_b, w_a, b_a, w_x, b_x, lam, sinks, rel_bias):
    S = x.shape[0]
    st = dict(T=min(512, S // 2), tm=min(512, S), bucket=_rel_bucket_map())
    st["h"] = _rms_fwd(x, pre_g, "pre_norm")
    st["memn"] = _rms_fwd(mem, mem_g, "mem_norm")
    seg = st["seg"] = _project(st["h"], w_in)
    st["h_rg"], st["y_rg"] = _rglru_fwd(seg["xr"], seg["g_rg"], conv_w, conv_b, w_a, b_a, w_x, b_x, lam, st["T"])
    st["o_swa"], st["y_swa"] = _swa_fwd(seg["q_s"], seg["kv"], seg["g_swa"], st["bucket"], rel_bias, _sink_column(sinks))
    return st


def _forward_b(st, x, target, post_g, w_memkv, wbr, w_out):
    S = x.shape[0]
    M = st["memn"].shape[0]
    seg = st["seg"]
    st["mkv"] = _matmul(st["memn"], w_memkv, "nn", M, 2 * D_MEM, D_MODEL, M, 512, D_MODEL, BF, "mem_kv")
    st["o_mem"], st["y_mem"] = _mem_fwd(seg["q_m"], st["mkv"], seg["g_mem"])
    st["ys"] = (st["y_rg"], st["y_swa"], st["y_mem"])
    st["merged"] = _merge_fwd(st["ys"], wbr, seg["gl"], st["tm"])
    st["dout"], st["dy"], st["loss"], st["dpost"] = _out_loss(st["merged"], w_out, x, target, post_g, min(256, S))
    return st


def _backward_a1(st, wbr, w_out):
    S = st["h"].shape[0]
    seg, ys, tm = st["seg"], st["ys"], st["tm"]
    st["dw_out"] = _matmul(st["merged"], st["dout"], "tn", D_MODEL, D_MODEL, S, 256, D_MODEL, S, BF, "dw_out", out_blocked="row")
    dgl0, dgl1, dgl2, dp0, dp1, dp2 = _merge_bwd(st["dout"], w_out, ys, wbr, seg["gl"], tm)
    st["dgl"] = (dgl0, dgl1, dgl2)
    dys, dwbr = [], []
    for i, dp in enumerate((dp0, dp1, dp2)):
        dys.append(_matmul(dp, wbr[i], "nn", S, 1024, D_MODEL, tm, 1024, D_MODEL, F32, "dy_br%d" % i))
        dwbr.append(_matmul(ys[i], dp, "tn", 1024, D_MODEL, S, 1024, 256, S, BF, "dw_br%d" % i, out_blocked="col"))
    st["dys"], st["dwbr"] = dys, dwbr
    return st


def _backward_a2(st, mem, w_memkv, conv_w, conv_b, w_a, b_a, w_x, b_x, lam):
    M = mem.shape[0]
    seg, dys = st["seg"], st["dys"]
    st["dq_m"], st["dg_mem"], dmkv = _mem_bwd(seg["q_m"], st["mkv"], seg["g_mem"], st["o_mem"], dys[2])
    dmkv_b = dmkv.astype(BF)
    st["dw_memkv"] = _matmul(st["memn"], dmkv_b, "tn", D_MODEL, 2 * D_MEM, M, 256, 2 * D_MEM, M, BF, "dw_memkv", out_blocked="row")
    dmemn = _matmul(dmkv_b, w_memkv, "nt", M, D_MODEL, 2 * D_MEM, M, 512, 2 * D_MEM, F32, "dmemn")
    st["dmem_g"] = _rms_gain_grad(dmemn, mem, "dmem_gain")
    st["dxr"], st["dg_rg"], st["dw_a"], st["dw_x"], st["dvec"] = _rglru_bwd(
        seg["xr"], seg["g_rg"], st["h_rg"], dys[0], conv_w, conv_b, w_a, b_a, w_x, b_x, lam, st["T"])
    return st


def _backward_b(st, rel_bias, sinks):
    seg = st["seg"]
    dq_s, dg_swa, dkv, dsinks, drel = _swa_bwd(seg["q_s"], seg["kv"], seg["g_swa"], st["o_swa"], st["dys"][1],
                                               st["bucket"], rel_bias, _sink_column(sinks))
    st["dsinks"], st["drel"] = dsinks.reshape(1, SWA_HEADS), drel.reshape(REL_BUCKETS, SWA_HEADS)
    st["dproj"] = jnp.concatenate([st["dxr"], st["dg_rg"], dq_s, dkv.astype(BF), dg_swa, st["dq_m"], st["dg_mem"], *st["dgl"]], axis=1)
    return st


def _dw_in_half(st, half, dep=None):
    S = st["h"].shape[0]
    dw = _matmul(st["dproj"], st["h"], "tn", D_IN, D_MODEL // 2, S, 896, 512, S, BF, "dw_in%d" % half, b_noff=2 * half, dep=dep)
    return dw.reshape(N_DEV, D_IN // N_DEV, D_MODEL // 2)


def _owner_blocks(a):
    return jnp.swapaxes(a.reshape((4, 2) + a.shape[1:]), 0, 1)


def _local_step(x, mem, target, pre_g, post_g, mem_g, w_in, conv_w, conv_b, w_a, b_a, w_x, b_x, lam, sinks, rel_bias,
                w_memkv, wbr, w_out):
    st = _forward_a(x, mem, pre_g, mem_g, w_in, conv_w, conv_b, w_a, b_a, w_x, b_x, lam, sinks, rel_bias)
    st = _forward_b(st, x, target, post_g, w_memkv, wbr, w_out)
    st = _backward_a1(st, wbr, w_out)
    st = _backward_a2(st, mem, w_memkv, conv_w, conv_b, w_a, b_a, w_x, b_x, lam)
    st = _backward_b(st, rel_bias, sinks)
    st["dw_in"] = [_dw_in_half(st, 0), _dw_in_half(st, 1)]
    st["grad_x"], st["dpre"] = _dh_dx(st["dproj"], w_in, x, st["dy"], pre_g, st["tm"], 896)
    return st


def _pad_rows(a, rows):
    a = a.reshape(-1, 128) if a.shape[-1] % 128 == 0 else jnp.pad(a, ((0, 0), (0, 128 - a.shape[-1])))
    return jnp.pad(a, ((0, rows - a.shape[0]), (0, 0))) if a.shape[0] < rows else a


def kernel(x, mem, pre_norm_g, post_norm_g, mem_norm_g, w_in, conv_w, conv_b, w_rg_a, b_rg_a, w_rg_x, b_rg_x, lru_lambda, swa_sinks, rel_bias, w_mem_kv, w_br_rg, w_br_swa, w_br_mem, w_out, loss_target, m_pre_norm_g, m_post_norm_g, m_mem_norm_g, m_w_in, m_conv_w, m_conv_b, m_w_rg_a, m_b_rg_a, m_w_rg_x, m_b_rg_x, m_lru_lambda, m_swa_sinks, m_rel_bias, m_w_mem_kv, m_w_br_rg, m_w_br_swa, m_w_br_mem, m_w_out, v_pre_norm_g, v_post_norm_g, v_mem_norm_g, v_w_in, v_conv_w, v_conv_b, v_w_rg_a, v_b_rg_a, v_w_rg_x, v_b_rg_x, v_lru_lambda, v_swa_sinks, v_rel_bias, v_w_mem_kv, v_w_br_rg, v_w_br_swa, v_w_br_mem, v_w_out):
    cx, cy, cc = lax.axis_index("x"), lax.axis_index("y"), lax.axis_index("c")
    me = 4 * cx + 2 * cy + cc
    chip = 2 * cx + cy
    core = jnp.reshape(cc, (1,)).astype(jnp.int32)
    x0, mem0 = x[0], mem[0]
    w_a_b, w_x_b = w_rg_a[0].astype(BF), w_rg_x[0].astype(BF)

    def landing(own, slot, slots):
        return lax.dynamic_update_slice(lax.empty((slots,) + own.shape, own.dtype), own[None], (slot,) + (0,) * own.ndim)


    def swap_start(parts, tag):
        return _exchange_start(parts, [lax.empty((4,) + p.shape[-2:], p.dtype) for p in parts], _plan_swap([p.ndim for p in parts]),
                               "swap_%s_start" % tag)

    def scatter_start(swap, after, tag, prefill=()):
        s_send, s_recv, parts, got, _ = swap
        got = _exchange_wait(s_send, s_recv, parts, got, _plan_swap([p.ndim for p in parts]), after, "swap_%s_wait" % tag)
        sums = [_pair_sum(p, g, core, "scatter_%s_sum%d" % (tag, i)) for i, (p, g) in enumerate(zip(parts, got))]
        lands = [landing(lax.dynamic_index_in_dim(s, chip, 0, keepdims=False), chip, 4) if i in prefill
                 else lax.empty(s.shape, s.dtype) for i, s in enumerate(sums)]
        return _exchange_start(sums, lands, _plan_scatter(len(sums)), "scatter_%s_start" % tag)

    def corner(a):
        return a.reshape(-1, a.shape[-1])[:8, :128]

    def zero_after(a):
        return jnp.minimum(jnp.abs(a.reshape(-1)[0].astype(F32)), 0.0)

    g_in, g_cw = _all_gather_relayed([jnp.transpose(w_in[0]).astype(BF), conv_w[0]], [True, False], "gather_w_in")
    w_in_f = g_in.reshape(D_IN, D_MODEL)
    conv_w_f = jnp.transpose(g_cw, (1, 0, 2)).reshape(CONV_W, D_RNN)

    after_first = zero_after(g_cw).astype(BF)
    rest = [w.astype(BF) + after_first for w in (w_mem_kv[0], jnp.transpose(w_br_rg[0]), jnp.transpose(w_br_swa[0]),
                                                 jnp.transpose(w_br_mem[0]), w_out[0])]
    kinds = ["lead"] * len(rest)
    plan_g = _plan_gather(kinds)
    zones = _place_own([lax.empty((N_DEV,) + w.shape, w.dtype) for w in rest], rest, jnp.reshape(me, (1,)).astype(jnp.int32), "gather_rest_own")
    g_send, g_recv, g_src, g_land, g_token = _exchange_start(rest, zones, plan_g, "gather_rest_start")
    st = _forward_a(x0, mem0, pre_norm_g + g_token[0:1, 0:1], mem_norm_g, w_in_f, conv_w_f, conv_b, w_a_b, b_rg_a, w_x_b, b_rg_x,
                    lru_lambda, swa_sinks, rel_bias)
    g_land = _exchange_wait(g_send, g_recv, g_src, g_land, plan_g, st["y_swa"], "gather_rest_wait")
    g_land = _forward_to_sibling(g_land, kinds, "gather_rest_forward")
    w_memkv_f = g_land[0].reshape(D_MODEL, 2 * D_MEM)
    wbr = tuple(g_land[i].reshape(D_MODEL, D_RNN) for i in (1, 2, 3))
    w_out_f = g_land[4].reshape(D_MODEL, D_MODEL)

    st = _forward_b(st, x0, loss_target[0], post_norm_g, w_memkv_f, wbr, w_out_f)
    st = _backward_a1(st, wbr, w_out_f)
    parts_a = [st["dw_out"], st["dwbr"][0], st["dwbr"][1], st["dwbr"][2]]
    plan_a = _plan_scatter(len(parts_a))
    swap_a = swap_start(parts_a, "a")
    st = _backward_a2(st, mem0, w_memkv_f, conv_w_f, conv_b + swap_a[4][0:1, 0:1], w_a_b, b_rg_a, w_x_b, b_rg_x, lru_lambda)
    a_send, a_recv, a_src, a_land, a_token = scatter_start(swap_a, st["dxr"], "a")
    parts_c = [st["dw_memkv"], _owner_blocks(st["dw_a"]), _owner_blocks(st["dw_x"])]
    plan_c = _plan_scatter(len(parts_c))
    swap_c = swap_start(parts_c, "c")

    st = _backward_b(st, rel_bias, swa_sinks + swap_c[4][0:1, 0:1] + a_token[0:1, 0:1])
    c_send, c_recv, c_src, c_land, c_token = scatter_start(swap_c, st["dsinks"], "c", prefill=(1, 2))
    plan_b = _plan_scatter(1)

    def dw_in_parts(half, dep):
        dwh = _dw_in_half(st, half, dep)
        return dwh, [dwh]

    dw0, parts_b0 = dw_in_parts(0, c_token)
    swap_b0 = swap_start(parts_b0, "b0")
    a_land = _exchange_wait(a_send, a_recv, a_src, a_land, plan_a, swap_b0[4], "scatter_a_wait")
    big = [None] * 6

    chip1 = jnp.reshape(chip, (1,)).astype(jnp.int32)

    def adamw_big(j, land, own, wt, mt, vt):
        big[j] = _adamw(land, own, chip1, wt, mt, vt, "adamw_big%d" % j)

    adamw_big(5, a_land[0], a_src[0], w_out, m_w_out, v_w_out)
    adamw_big(2, a_land[1], a_src[1], w_br_rg, m_w_br_rg, v_w_br_rg)
    adamw_big(3, a_land[2], a_src[2], w_br_swa, m_w_br_swa, v_w_br_swa)
    halves = [scatter_start(swap_b0, corner(big[3][1]), "b0")]
    dw1, parts_b1 = dw_in_parts(1, halves[0][4])
    swap_b1 = swap_start(parts_b1, "b1")
    c_land = _exchange_wait(c_send, c_recv, c_src, c_land, plan_c, swap_b1[4], "scatter_c_wait")
    g_wa_blk = _sum_parts(c_land[1], "sum_w_rg_a")
    g_wx_blk = _sum_parts(c_land[2], "sum_w_rg_x")
    adamw_big(4, a_land[3], a_src[3], w_br_mem, m_w_br_mem, v_w_br_mem)
    adamw_big(1, c_land[0], c_src[0], w_mem_kv, m_w_mem_kv, v_w_mem_kv)
    halves.append(scatter_start(swap_b1, corner(big[1][1]), "b1"))
    grad_x, dpre = _dh_dx(st["dproj"], w_in_f, x0, st["dy"], pre_norm_g + halves[1][4][0:1, 0:1], st["tm"], 896)
    pack = jnp.concatenate([dpre.reshape(16, 128), st["dpost"].reshape(16, 128), st["dmem_g"].reshape(16, 128),
                            st["dvec"].reshape(64, 128), _pad_rows(st["dsinks"], 8), _pad_rows(st["drel"], 32), g_wa_blk, g_wx_blk,
                            st["loss"]], axis=0)
    plan_s = _plan_everyone(1)
    s_send, s_recv, s_src, s_land, s_token = _exchange_start([pack], [landing(pack, me, N_DEV)], plan_s, "gather_small_start")
    after, b_lands, b_sums = s_token, [], []
    for half, (b_send, b_recv, b_src, b_land, _) in enumerate(halves):
        b_lands.append(_exchange_wait(b_send, b_recv, b_src, b_land, plan_b, after, "scatter_b%d_wait" % half)[0])
        b_sums.append(b_src[0])
        after = b_lands[-1]
    swap_last = lambda a: jnp.transpose(a, (0, 2, 1))
    big0_t = _adamw(b_lands, b_sums, chip1, swap_last(w_in), swap_last(m_w_in), swap_last(v_w_in), "adamw_big0")
    big[0] = [swap_last(a) for a in big0_t]
    gathered = _exchange_wait(s_send, s_recv, s_src, s_land, plan_s, corner(big0_t[1]), "gather_small_wait")[0]
    gs = _sum_parts(gathered, "sum_small")
    loss_total = gs[408, 0]
    g_pre, g_post, g_memg = gs[0:16].reshape(1, D_MODEL), gs[16:32].reshape(1, D_MODEL), gs[32:48].reshape(1, D_MODEL)
    gvec = gs[48:112].reshape(8, D_RNN)
    g_conv_w = lax.dynamic_slice(gvec[0:CONV_W], (0, me * RNN_BLOCK), (CONV_W, RNN_BLOCK))
    g_conv_b, g_b_a, g_b_x, g_lam = gvec[4:5], gvec[5:6], gvec[6:7], gvec[7:8]
    g_sinks = gs[112:113, :SWA_HEADS]
    g_rel = gs[120:152, :SWA_HEADS]
    g_w_a = gathered[:, 152:280]
    g_w_x = gathered[:, 280:408]

    g_small = (g_pre, g_post, g_memg, g_conv_b, g_b_a, g_b_x, g_lam, g_w_a, g_w_x, g_sinks, g_rel, g_conv_w)
    w_small = (pre_norm_g, post_norm_g, mem_norm_g, conv_b, b_rg_a, b_rg_x, lru_lambda, w_rg_a, w_rg_x, swa_sinks, rel_bias, conv_w)
    m_small = (m_pre_norm_g, m_post_norm_g, m_mem_norm_g, m_conv_b, m_b_rg_a, m_b_rg_x, m_lru_lambda, m_w_rg_a, m_w_rg_x, m_swa_sinks, m_rel_bias, m_conv_w)
    v_small = (v_pre_norm_g, v_post_norm_g, v_mem_norm_g, v_conv_b, v_b_rg_a, v_b_rg_x, v_lru_lambda, v_w_rg_a, v_w_rg_x, v_swa_sinks, v_rel_bias, v_conv_w)
    sm = _adamw_small(g_small, w_small, m_small, v_small)


    def leaves(k):
        s = sm[k]
        return [s[0], s[1], s[2], big[0][k], s[11], s[3], s[7], s[4], s[8], s[5], s[6], s[9], s[10],
                big[1][k], big[2][k], big[3][k], big[4][k], big[5][k]]

    return (loss_total, grad_x[None], *leaves(0), *leaves(1), *leaves(2), *leaves(3))
```

```python
import math

import jax
import jax.numpy as jnp
import numpy as np
from jax import lax
from jax.experimental import pallas as pl
from jax.experimental.pallas import tpu as pltpu

F32, BF = jnp.float32, jnp.bfloat16
MESH = pl.DeviceIdType.MESH
N_DEV = 8

D_MODEL = 2048
D_RNN = 1024
RNN_BLOCKS = 8
RNN_BLOCK = 128
CONV_W = 4
LRU_C = 8.0
SWA_HEADS = 16
SWA_KV_HEADS = 2
SWA_HD = 64
WINDOW = 128
MEM_HEADS = 4
MEM_HD = 256
D_MEM = 1024
REL_BUCKETS = 32
REL_MAX_DIST = 128
EPS = 1e-6
NEG_INF = -1e30
D_IN = 12544
SEGMENTS = (("xr", 0, 1024, F32), ("g_rg", 1024, 1024, F32), ("q_s", 2048, 1024, BF), ("kv", 3072, 256, BF),
            ("g_swa", 3328, 1024, F32), ("q_m", 4352, 1024, BF), ("g_mem", 5376, 1024, F32), ("gl", 6400, 6144, F32))
SEG_TILE = 256
D_IN_TILE = 7 * SEG_TILE

ADAM_LR, ADAM_B1, ADAM_B2, ADAM_EPS, ADAM_WD, ADAM_STEP = 0.001, 0.9, 0.999, 1e-08, 0.01, 10

NN = (((1,), (0,)), ((), ()))
NT = (((1,), (1,)), ((), ()))
TN = (((0,), (0,)), ((), ()))
MIB = 2 ** 20


def _dot(a, b, dn):
    return lax.dot_general(a, b, dn, preferred_element_type=F32)


VMEM_LIMIT_MIB = 48
VMEM_LIMIT_LARGE_MIB = 56


def _params(sem=None, large=False):
    return pltpu.CompilerParams(dimension_semantics=sem, vmem_limit_bytes=(VMEM_LIMIT_LARGE_MIB if large else VMEM_LIMIT_MIB) * MIB)


def _sigmoid(z):
    return 1.0 / (1.0 + jnp.exp(-z))


def _softplus(z):
    return jnp.maximum(z, 0.0) + jnp.log(1.0 + jnp.exp(-jnp.abs(z)))


def _expm1(z):
    p = z * (1.0 + z * (0.5 + z * (1.0 / 6 + z * (1.0 / 24 + z * (1.0 / 120 + z * (1.0 / 720 + z * (1.0 / 5040 + z / 40320)))))))
    return jnp.where(jnp.abs(z) < 0.3, p, jnp.exp(z) - 1.0)


def _flat(p):
    return 4 * p[0] + 2 * p[1] + p[2]


def _all_gather_relayed(arrs, relay, name):
    n = len(arrs)
    K = 9

    def body(*refs):
        ins, outs = refs[:n], refs[n:2 * n]
        send_sems, recv_sems, local_sems = refs[2 * n:]
        x, y, c = lax.axis_index("x"), lax.axis_index("y"), lax.axis_index("c")
        me, sib = (x, y, c), (x, y, 1 - c)
        xn, yn, dg = (1 - x, y, c), (x, 1 - y, c), (1 - x, 1 - y, c)

        def other(p):
            return (p[0], p[1], 1 - p[2])

        def rows(a, half):
            h = arrs[a].shape[0] // 2
            return pl.ds(half * h, h)

        def copy(a, k, block, to, half=None, src=None):
            dst = outs[a].at[_flat(block)]
            if half is not None:
                dst = dst.at[rows(a, half)]
            return pltpu.make_async_remote_copy(src_ref=dst if src is None else src, dst_ref=dst,
                                                send_sem=send_sems.at[a * K + k], recv_sem=recv_sems.at[a * K + k],
                                                device_id=to, device_id_type=MESH)

        mine = [pltpu.make_async_copy(ins[a], outs[a].at[_flat(me)], local_sems.at[a]) for a in range(n)]
        for cp in mine:
            cp.start()
        sends = []

        def start(cp):
            cp.start()
            sends.append(cp)

        for a in range(n):
            start(copy(a, 1, me, xn, src=ins[a]))
            start(copy(a, 2, me, yn, src=ins[a]))
            if not relay[a]:
                start(copy(a, 3, me, dg, src=ins[a]))
            start(copy(a, 0, me, sib, src=ins[a]))
        for a in range(n):
            copy(a, 1, xn, me).wait_recv()
            if relay[a]:
                start(copy(a, 3, xn, yn, half=0))
            start(copy(a, 5, xn, sib))
        for a in range(n):
            copy(a, 2, yn, me).wait_recv()
            if relay[a]:
                start(copy(a, 4, yn, xn, half=1))
            start(copy(a, 6, yn, sib))
        for a in range(n):
            if relay[a]:
                copy(a, 3, dg, me, half=0).wait_recv()
                start(copy(a, 7, dg, sib, half=0))
                copy(a, 4, dg, me, half=1).wait_recv()
                start(copy(a, 8, dg, sib, half=1))
            else:
                copy(a, 3, dg, me).wait_recv()
                start(copy(a, 7, dg, sib))
        for a in range(n):
            copy(a, 0, sib, me).wait_recv()
            copy(a, 5, other(xn), me).wait_recv()
            copy(a, 6, other(yn), me).wait_recv()
            if relay[a]:
                copy(a, 7, other(dg), me, half=0).wait_recv()
                copy(a, 8, other(dg), me, half=1).wait_recv()
            else:
                copy(a, 7, other(dg), me).wait_recv()
        for cp in sends:
            cp.wait_send()
        for cp in mine:
            cp.wait()

    any_spec = pl.BlockSpec(memory_space=pl.ANY)
    return pl.pallas_call(
        body, name=name,
        out_shape=[jax.ShapeDtypeStruct((N_DEV,) + a.shape, a.dtype) for a in arrs],
        in_specs=[any_spec] * n, out_specs=[any_spec] * n,
        scratch_shapes=[pltpu.SemaphoreType.DMA((K * n,)), pltpu.SemaphoreType.DMA((K * n,)), pltpu.SemaphoreType.DMA((n,))],
    )(*arrs)


def _chip_peers(x, y):
    return [(1 - x, y), (x, 1 - y), (1 - x, 1 - y)]


def _chip(p):
    return 2 * p[0] + p[1]


def _plan_gather(n):
    def plan(x, y, c):
        out = []
        for a in range(n):
            for peer in [(x, y, 1 - c)] + [(*ch, c) for ch in _chip_peers(x, y)]:
                out.append((a, None, ("lead", _flat((x, y, c))), peer, ("lead", _flat(peer))))
        return out
    return plan


def _plan_everyone(n):
    def plan(x, y, c):
        out = []
        for a in range(n):
            for r in range(1, N_DEV):
                peer = (1 - x if r & 4 else x, 1 - y if r & 2 else y, 1 - c if r & 1 else c)
                out.append((a, None, ("lead", _flat((x, y, c))), peer, ("lead", _flat(peer))))
        return out
    return plan


def _plan_swap(ndims):
    def plan(x, y, c):
        out = []
        for a, nd in enumerate(ndims):
            if nd == 4:
                out.append((a, 1 - c, ("all", 0), (x, y, 1 - c), ("all", 0)))
            else:
                out += [(a, 2 * j + 1 - c, ("lead", j), (x, y, 1 - c), ("lead", j)) for j in range(4)]
        return out
    return plan


def _slot(ref, where):
    kind, k = where
    return ref if kind == "all" else ref.at[k]


def _plan_scatter(n):
    def plan(x, y, c):
        out = []
        for a in range(n):
            for ch in _chip_peers(x, y):
                out.append((a, _chip(ch), ("lead", _chip((x, y))), (*ch, c), ("lead", _chip(ch))))
        return out
    return plan


HBM_SPEC = pl.BlockSpec(memory_space=pltpu.HBM)
SEM_SPEC = pl.BlockSpec(memory_space=pltpu.SEMAPHORE)


def _in_hbm(a):
    return pltpu.with_memory_space_constraint(a, pltpu.HBM)


def _exchange_start(srcs, lands, plan, name):
    n = len(srcs)
    count = len(plan(0, 0, 0))

    def body(*refs):
        src_refs, land_refs = refs[:n], refs[n:2 * n]
        send_sems, recv_sems = refs[2 * n], refs[2 * n + 1]
        token = refs[-1]
        x, y, c = lax.axis_index("x"), lax.axis_index("y"), lax.axis_index("c")
        for k, (a, si, di, peer, _) in enumerate(plan(x, y, c)):
            src = src_refs[a] if si is None else src_refs[a].at[si]
            pltpu.make_async_remote_copy(src_ref=src, dst_ref=_slot(land_refs[a], di), send_sem=send_sems.at[k],
                                         recv_sem=recv_sems.at[k], device_id=peer, device_id_type=MESH).start()
        token[...] = jnp.zeros_like(token)

    out = pl.pallas_call(
        body, name=name,
        out_shape=(pltpu.SemaphoreType.DMA((count,)), pltpu.SemaphoreType.DMA((count,)),
                   *[pltpu.HBM(a.shape, a.dtype) for a in lands], jax.ShapeDtypeStruct((8, 128), F32)),
        in_specs=[HBM_SPEC] * (2 * n),
        out_specs=(SEM_SPEC, SEM_SPEC, *([HBM_SPEC] * n), pl.BlockSpec(memory_space=pltpu.VMEM)),
        input_output_aliases={n + i: 2 + i for i in range(n)},
        compiler_params=pltpu.CompilerParams(has_side_effects=pltpu.SideEffectType.DATAFLOW_SIDE_EFFECTING),
    )(*[_in_hbm(a) for a in srcs], *[_in_hbm(a) for a in lands])
    return out[0], out[1], list(srcs), list(out[2:2 + n]), out[-1]


def _exchange_wait(send_sems, recv_sems, srcs, lands, plan, after, name):
    n = len(srcs)

    def body(*refs):
        src_refs, land_refs = refs[:n], refs[n:2 * n]
        send_sems, recv_sems = refs[2 * n], refs[2 * n + 1]
        x, y, c = lax.axis_index("x"), lax.axis_index("y"), lax.axis_index("c")
        for k, (a, si, _, peer, ri) in enumerate(plan(x, y, c)):
            src = src_refs[a] if si is None else src_refs[a].at[si]
            cp = pltpu.make_async_remote_copy(src_ref=src, dst_ref=_slot(land_refs[a], ri), send_sem=send_sems.at[k],
                                              recv_sem=recv_sems.at[k], device_id=peer, device_id_type=MESH)
            cp.wait_send()
            cp.wait_recv()

    out = pl.pallas_call(
        body, name=name,
        out_shape=tuple(pltpu.HBM(a.shape, a.dtype) for a in lands),
        in_specs=[HBM_SPEC] * (2 * n) + [SEM_SPEC, SEM_SPEC, pl.BlockSpec(memory_space=pl.ANY)],
        out_specs=tuple([HBM_SPEC] * n),
        input_output_aliases={n + i: i for i in range(n)},
        compiler_params=pltpu.CompilerParams(has_side_effects=pltpu.SideEffectType.DATAFLOW_SIDE_EFFECTING),
    )(*[_in_hbm(a) for a in srcs], *lands, send_sems, recv_sems, after)
    return list(out)


def _forward_to_sibling(lands, name):
    n = len(lands)

    def body(*refs):
        in_refs, out_refs = refs[:n], refs[n:2 * n]
        send_sems, recv_sems = refs[2 * n:]
        x, y, c = lax.axis_index("x"), lax.axis_index("y"), lax.axis_index("c")
        sibling = (x, y, 1 - c)

        def copy(a, j, slot):
            return pltpu.make_async_remote_copy(src_ref=in_refs[a].at[slot], dst_ref=out_refs[a].at[slot],
                                                send_sem=send_sems.at[a * 3 + j], recv_sem=recv_sems.at[a * 3 + j],
                                                device_id=sibling, device_id_type=MESH)

        sends = [copy(a, j, _flat((*ch, c))) for a in range(n) for j, ch in enumerate(_chip_peers(x, y))]
        for cp in sends:
            cp.start()
        for a in range(n):
            for j, ch in enumerate(_chip_peers(x, y)):
                copy(a, j, _flat((*ch, 1 - c))).wait_recv()
        for cp in sends:
            cp.wait_send()

    any_spec = pl.BlockSpec(memory_space=pl.ANY)
    return pl.pallas_call(
        body, name=name, out_shape=[jax.ShapeDtypeStruct(a.shape, a.dtype) for a in lands],
        in_specs=[any_spec] * n, out_specs=[any_spec] * n, input_output_aliases={a: a for a in range(n)},
        scratch_shapes=[pltpu.SemaphoreType.DMA((3 * n,)), pltpu.SemaphoreType.DMA((3 * n,))],
    )(*lands)


def _place_own(zones, owns, slot, name):
    n = len(zones)

    def body(slot_ref, *refs):
        for a in range(n):
            refs[2 * n + a][...] = refs[a][...]

    return pl.pallas_call(
        body, name=name,
        grid_spec=pltpu.PrefetchScalarGridSpec(
            num_scalar_prefetch=1, grid=(1,),
            in_specs=[pl.BlockSpec(o.shape, lambda i, s_ref: (0, 0)) for o in owns] + [pl.BlockSpec(memory_space=pl.ANY)] * n,
            out_specs=[pl.BlockSpec((None,) + o.shape, lambda i, s_ref: (s_ref[0], 0, 0)) for o in owns]),
        out_shape=[jax.ShapeDtypeStruct(z.shape, z.dtype) for z in zones],
        input_output_aliases={1 + n + a: a for a in range(n)},
        compiler_params=_params(("arbitrary",)),
    )(slot, *owns, *zones)


def _pair_sum(parts, got, core, name):
    R, C = parts.shape[-2:]
    tr = 256 if R % 256 == 0 else R
    mine = (pl.BlockSpec((None, None, tr, C), lambda j, i, c_ref: (c_ref[0], j, i, 0)) if parts.ndim == 4
            else pl.BlockSpec((None, tr, C), lambda j, i, c_ref: (2 * j + c_ref[0], i, 0)))

    def body(c_ref, p_ref, g_ref, o_ref):
        o_ref[...] = (p_ref[...].astype(F32) + g_ref[...].astype(F32)).astype(o_ref.dtype)

    return pl.pallas_call(
        body, name=name,
        grid_spec=pltpu.PrefetchScalarGridSpec(
            num_scalar_prefetch=1, grid=(4, R // tr),
            in_specs=[mine, pl.BlockSpec((None, tr, C), lambda j, i, c_ref: (j, i, 0))],
            out_specs=pl.BlockSpec((None, tr, C), lambda j, i, c_ref: (j, i, 0))),
        out_shape=jax.ShapeDtypeStruct((4, R, C), parts.dtype),
        compiler_params=_params(("parallel", "parallel")),
    )(core, parts, got)


def _matmul(a, b, mode, M, N, K, tm, tn, tk, out_dtype, name, b_noff=0, out_blocked=None, dep=None):
    nm, nn, nk = M // tm, N // tn, K // tk
    if mode == "nn":
        a_spec = pl.BlockSpec((tm, tk), lambda j, i, k: (i, k))
        b_spec = pl.BlockSpec((tk, tn), lambda j, i, k: (k, j + b_noff))
        dn = NN
    elif mode == "nt":
        a_spec = pl.BlockSpec((tm, tk), lambda j, i, k: (i, k))
        b_spec = pl.BlockSpec((tn, tk), lambda j, i, k: (j + b_noff, k))
        dn = NT
    else:
        a_spec = pl.BlockSpec((tk, tm), lambda j, i, k: (k, i))
        b_spec = pl.BlockSpec((tk, tn), lambda j, i, k: (k, j + b_noff))
        dn = TN
    if out_blocked == "col":
        out_shape = jax.ShapeDtypeStruct((2, 4, M, tn), out_dtype)
        out_spec = pl.BlockSpec((None, None, tm, tn), lambda j, i, k: (j % 2, j // 2, i, 0))
    elif out_blocked == "row":
        out_shape = jax.ShapeDtypeStruct((2, 4, tm, N), out_dtype)
        out_spec = pl.BlockSpec((None, None, tm, tn), lambda j, i, k: (i % 2, i // 2, 0, j))
    else:
        out_shape = jax.ShapeDtypeStruct((M, N), out_dtype)
        out_spec = pl.BlockSpec((tm, tn), lambda j, i, k: (i, j))

    n_extra = int(dep is not None)

    def body(a_ref, b_ref, *rest):
        o_ref, scratch = rest[n_extra], rest[n_extra + 1:]
        if nk == 1:
            o_ref[...] = _dot(a_ref[...], b_ref[...], dn).astype(out_dtype)
        else:
            acc_ref, = scratch
            k = pl.program_id(2)

            @pl.when(k == 0)
            def _():
                acc_ref[...] = jnp.zeros_like(acc_ref)

            acc_ref[...] += _dot(a_ref[...], b_ref[...], dn)

            @pl.when(k == nk - 1)
            def _():
                o_ref[...] = acc_ref[...].astype(out_dtype)

    return pl.pallas_call(
        body, name=name, grid=(nn, nm, nk),
        in_specs=[a_spec, b_spec] + ([] if dep is None else [pl.BlockSpec((8, 128), lambda j, i, k: (0, 0))]),
        out_specs=out_spec, out_shape=out_shape,
        scratch_shapes=[] if nk == 1 else [pltpu.VMEM((tm, tn), F32)],
        compiler_params=_params(("parallel", "parallel", "arbitrary")),
    )(a, b, *([] if dep is None else [dep]))


def _rms_fwd(x, g, name):
    R, Dm = x.shape
    tr = min(R, 256)

    def body(x_ref, g_ref, h_ref):
        xv = x_ref[...]
        r = lax.rsqrt(jnp.mean(xv * xv, axis=-1, keepdims=True) + EPS)
        h_ref[...] = (xv * r * g_ref[...]).astype(BF)

    return pl.pallas_call(
        body, name=name, grid=(R // tr,),
        in_specs=[pl.BlockSpec((tr, Dm), lambda i: (i, 0)), pl.BlockSpec((1, Dm), lambda i: (0, 0))],
        out_specs=pl.BlockSpec((tr, Dm), lambda i: (i, 0)), out_shape=jax.ShapeDtypeStruct((R, Dm), BF),
        compiler_params=_params(("parallel",)),
    )(x, g)


def _rms_gain_grad(dn, x, name):
    R, Dm = x.shape

    def body(dn_ref, x_ref, o_ref):
        xv = x_ref[...]
        r = lax.rsqrt(jnp.mean(xv * xv, axis=-1, keepdims=True) + EPS)
        o_ref[...] = jnp.sum(dn_ref[...] * xv * r, axis=0, keepdims=True)

    return pl.pallas_call(
        body, name=name, out_shape=jax.ShapeDtypeStruct((1, Dm), F32),
        compiler_params=_params(),
    )(dn, x)


def _shift_down(v, k, head8, row, T):
    if k == 0:
        return v
    r = pltpu.roll(v, k, 0)
    hr = pltpu.roll(head8, k, 0)
    top = jnp.where(row[:8] < k, hr, r[:8])
    return jnp.concatenate([top, r[8:]], axis=0)


def _shift_up(v, k, tail8, row, T):
    if k == 0:
        return v
    r = pltpu.roll(v, T - k, 0)
    tr = pltpu.roll(tail8, 8 - k, 0)
    bot = jnp.where(row[:8] >= 8 - k, tr, r[T - 8:])
    return jnp.concatenate([r[:T - 8], bot], axis=0)


def _rglru_gates(u, head8, grow, row, T, cw_ref, cb_ref, wa_ref, ba_ref, wx_ref, bx_ref, lam_ref):
    us = [_shift_down(u, k, head8, row, T) for k in range(CONV_W)]
    acc = us[0] * cw_ref[0:1, :]
    for k in range(1, CONV_W):
        acc = acc + us[k] * cw_ref[k:k + 1, :]
    conv = cb_ref[...] + acc
    cbf = conv.astype(BF)
    r_ = _sigmoid(_dot(cbf, wa_ref[0], NN) + ba_ref[...])
    i_ = _sigmoid(_dot(cbf, wx_ref[0], NN) + bx_ref[...])
    sp = _softplus(-lam_ref[...])
    la = -LRU_C * r_ * sp
    a = jnp.exp(la)
    mult_raw = jnp.sqrt(-_expm1(2.0 * la))
    mult = jnp.where(grow == 0, 1.0, mult_raw)
    return us, conv, cbf, r_, i_, sp, a, mult_raw, mult


def _rglru_specs(T, nt, rev):
    tmap = (lambda n, t: (nt - 1 - t, n)) if rev else (lambda n, t: (t, n))
    hmap = ((lambda n, t: (jnp.maximum((nt - 1 - t) * (T // 8) - 1, 0), n)) if rev
            else (lambda n, t: (jnp.maximum(t * (T // 8) - 1, 0), n)))
    tile = pl.BlockSpec((T, RNN_BLOCK), tmap)
    halo = pl.BlockSpec((8, RNN_BLOCK), hmap)
    vec = pl.BlockSpec((1, RNN_BLOCK), lambda n, t: (0, n))
    cw = pl.BlockSpec((CONV_W, RNN_BLOCK), lambda n, t: (0, n))
    wblk = pl.BlockSpec((1, RNN_BLOCK, RNN_BLOCK), lambda n, t: (n, 0, 0))
    return tile, halo, vec, cw, wblk


def _rglru_fwd(xr, g, cw, cb, wa, ba, wx, bx, lam, T):
    S = xr.shape[0]
    nt = S // T

    def body(u_ref, uh_ref, g_ref, cw_ref, cb_ref, wa_ref, ba_ref, wx_ref, bx_ref, lam_ref, h_ref, y_ref, carry):
        t = pl.program_id(1)

        @pl.when(t == 0)
        def _():
            carry[...] = jnp.zeros_like(carry)

        row = lax.broadcasted_iota(jnp.int32, (T, RNN_BLOCK), 0)
        grow = row + t * T
        head8 = jnp.where(t > 0, uh_ref[...], 0.0)
        _, conv, _, _, i_, _, a, _, mult = _rglru_gates(u_ref[...], head8, grow, row, T, cw_ref, cb_ref, wa_ref, ba_ref,
                                                         wx_ref, bx_ref, lam_ref)
        b = mult * i_ * conv
        s = 1
        while s < T:
            keep = row >= s
            a_s = jnp.where(keep, pltpu.roll(a, s, 0), 1.0)
            b_s = jnp.where(keep, pltpu.roll(b, s, 0), 0.0)
            b = a * b_s + b
            a = a * a_s
            s *= 2
        h = b + a * carry[0:1, :]
        carry[...] = jnp.broadcast_to(h[T - 1:T, :], carry.shape)
        h_ref[...] = h
        gv = g_ref[...]
        y_ref[...] = (h * (gv * _sigmoid(gv))).astype(BF)

    tile, halo, vec, cwspec, wblk = _rglru_specs(T, nt, False)
    return pl.pallas_call(
        body, name="rglru_fwd", grid=(RNN_BLOCKS, nt),
        in_specs=[tile, halo, tile, cwspec, vec, wblk, vec, wblk, vec, vec],
        out_specs=[tile, tile],
        out_shape=[jax.ShapeDtypeStruct((S, D_RNN), F32), jax.ShapeDtypeStruct((S, D_RNN), BF)],
        scratch_shapes=[pltpu.VMEM((8, RNN_BLOCK), F32)],
        compiler_params=_params(("parallel", "arbitrary")),
    )(xr, xr, g, cw, cb, wa, ba, wx, bx, lam)


def _rglru_bwd(xr, g, h, dy, cw, cb, wa, ba, wx, bx, lam, T):
    S = xr.shape[0]
    nt = S // T

    def body(u_ref, uh_ref, g_ref, h_ref, hh_ref, dy_ref, cw_ref, cb_ref, wa_ref, ba_ref, wx_ref, bx_ref, lam_ref,
             du_ref, dg_ref, dwa_ref, dwx_ref, dvec_ref, c_dhh, c_a, c_dconv):
        t = pl.program_id(1)
        tt = nt - 1 - t

        @pl.when(t == 0)
        def _():
            c_dhh[...] = jnp.zeros_like(c_dhh)
            c_a[...] = jnp.zeros_like(c_a)
            c_dconv[...] = jnp.zeros_like(c_dconv)
            dwa_ref[...] = jnp.zeros_like(dwa_ref)
            dwx_ref[...] = jnp.zeros_like(dwx_ref)
            dvec_ref[...] = jnp.zeros_like(dvec_ref)

        row = lax.broadcasted_iota(jnp.int32, (T, RNN_BLOCK), 0)
        row8 = row[:8]
        grow = row + tt * T
        head8 = jnp.where(tt > 0, uh_ref[...], 0.0)
        us, conv, cbf, r_, i_, sp, a, mult_raw, mult = _rglru_gates(
            u_ref[...], head8, grow, row, T, cw_ref, cb_ref, wa_ref, ba_ref, wx_ref, bx_ref, lam_ref)
        hv = h_ref[...]
        hprev = _shift_down(hv, 1, jnp.where(tt > 0, hh_ref[...], 0.0), row, T)
        gv = g_ref[...]
        sg = _sigmoid(gv)
        dyv = dy_ref[...]
        dg_ref[...] = (dyv * hv * (sg * (1.0 + gv * (1.0 - sg)))).astype(BF)
        d = dyv * (gv * sg)
        A = _shift_up(a, 1, c_a[...], row, T)
        s = 1
        while s < T:
            keep = row < T - s
            A_s = jnp.where(keep, pltpu.roll(A, T - s, 0), 1.0)
            d_s = jnp.where(keep, pltpu.roll(d, T - s, 0), 0.0)
            d = A * d_s + d
            A = A * A_s
            s *= 2
        dhh = d + A * c_dhh[0:1, :]
        da = dhh * hprev
        dconv = dhh * mult * i_
        di = dhh * mult * conv
        dmult = dhh * i_ * conv
        dla = da * a - jnp.where(grow == 0, 0.0, dmult * (a * a) / mult_raw)
        dr = dla * (-LRU_C * sp)
        dsp = jnp.sum(dla * (-LRU_C * r_), axis=0, keepdims=True)
        dza = dr * r_ * (1.0 - r_)
        dzx = di * i_ * (1.0 - i_)
        dza_b, dzx_b = dza.astype(BF), dzx.astype(BF)
        dconv = dconv + _dot(dza_b, wa_ref[0], NT) + _dot(dzx_b, wx_ref[0], NT)
        dwa_ref[0] += _dot(cbf, dza_b, TN)
        dwx_ref[0] += _dot(cbf, dzx_b, TN)
        lam = lam_ref[...]
        rows = [jnp.sum(dconv * us[k], axis=0, keepdims=True) for k in range(CONV_W)]
        rows += [jnp.sum(dconv, axis=0, keepdims=True), jnp.sum(dza, axis=0, keepdims=True),
                 jnp.sum(dzx, axis=0, keepdims=True), dsp * (-_sigmoid(-lam))]
        upd = jnp.zeros((8, RNN_BLOCK), F32)
        for j, rv in enumerate(rows):
            upd = upd + jnp.where(row8 == j, rv, 0.0)
        dvec_ref[...] += upd
        tail8 = c_dconv[...]
        du = dconv * cw_ref[0:1, :]
        for k in range(1, CONV_W):
            du = du + _shift_up(dconv, k, tail8, row, T) * cw_ref[k:k + 1, :]
        du_ref[...] = du.astype(BF)
        c_dhh[...] = jnp.broadcast_to(dhh[0:1, :], c_dhh.shape)
        c_a[...] = jnp.broadcast_to(a[0:1, :], c_a.shape)
        c_dconv[...] = dconv[:8]

    tile, halo, vec, cwspec, wblk = _rglru_specs(T, nt, True)
    acc8 = pl.BlockSpec((8, RNN_BLOCK), lambda n, t: (0, n))
    return pl.pallas_call(
        body, name="rglru_bwd", grid=(RNN_BLOCKS, nt),
        in_specs=[tile, halo, tile, tile, halo, tile, cwspec, vec, wblk, vec, wblk, vec, vec],
        out_specs=[tile, tile, wblk, wblk, acc8],
        out_shape=[jax.ShapeDtypeStruct((S, D_RNN), BF), jax.ShapeDtypeStruct((S, D_RNN), BF),
                   jax.ShapeDtypeStruct((RNN_BLOCKS, RNN_BLOCK, RNN_BLOCK), F32),
                   jax.ShapeDtypeStruct((RNN_BLOCKS, RNN_BLOCK, RNN_BLOCK), F32),
                   jax.ShapeDtypeStruct((8, D_RNN), F32)],
        scratch_shapes=[pltpu.VMEM((8, RNN_BLOCK), F32)] * 3,
        compiler_params=_params(("parallel", "arbitrary")),
    )(xr, xr, g, h, h, dy, cw, cb, wa, ba, wx, bx, lam)


def _rel_bucket_map():
    qi = np.arange(WINDOW)[:, None]
    kj = np.arange(2 * WINDOW)[None, :]
    dist = jnp.asarray(qi + WINDOW - kj, jnp.int32)
    n = jnp.maximum(dist, 0)
    max_exact = REL_BUCKETS // 2
    ratio = jnp.log(jnp.maximum(n, 1).astype(F32) / max_exact) / math.log(REL_MAX_DIST / max_exact)
    large = jnp.minimum(max_exact + (ratio * (REL_BUCKETS - max_exact)).astype(jnp.int32), REL_BUCKETS - 1)
    bucket = jnp.where(n < max_exact, n, large).astype(jnp.int32)
    j = np.arange(WINDOW)[None, :]
    return jnp.where(jnp.asarray(j > qi), bucket[:, :WINDOW], bucket[:, WINDOW:])


def _swa_common(n, kv_ref, bucket_ref, relb_ref, bias_scr):
    @pl.when(n == 0)
    def _():
        bk = bucket_ref[...]
        for h in range(SWA_HEADS):
            acc = jnp.zeros((WINDOW, WINDOW), F32)
            for b in range(REL_BUCKETS):
                acc = acc + jnp.where(bk == b, relb_ref[b, h], 0.0)
            bias_scr[h] = acc

    prev0 = pl.multiple_of(jnp.maximum(n - 1, 0) * WINDOW, WINDOW)
    cur0 = pl.multiple_of(n * WINDOW, WINDOW)
    kk = jnp.concatenate([kv_ref[pl.ds(prev0, WINDOW), :], kv_ref[pl.ds(cur0, WINDOW), :]], axis=0).astype(F32)
    rowi = lax.broadcasted_iota(jnp.int32, (WINDOW, WINDOW), 0)
    col = lax.broadcasted_iota(jnp.int32, (WINDOW, WINDOW), 1)
    from_prev = col > rowi
    return kk, from_prev, prev0, cur0


def _fold(full, from_prev):
    return jnp.where(from_prev, full[:, :WINDOW], full[:, WINDOW:])


def _unfold(sq, from_prev):
    return jnp.concatenate([jnp.where(from_prev, sq, 0.0), jnp.where(from_prev, 0.0, sq)], axis=1)


def _half_pair(part, kvh):
    lo = lax.broadcasted_iota(jnp.int32, part.shape, 1) < SWA_HD
    if kvh == 0:
        pa = jnp.where(lo, part, 0.0)
        pb = pltpu.roll(pa, SWA_HD, 1)
    else:
        pb = jnp.where(lo, 0.0, part)
        pa = pltpu.roll(pb, SWA_HD, 1)
    return pa.astype(BF), pb.astype(BF)


ALL_HEADS = SWA_HEADS * WINDOW


def _sink_column(sinks):
    return jnp.repeat(sinks.reshape(SWA_HEADS), WINDOW).reshape(ALL_HEADS, 1)


def _swa_operands(kk):
    return [(_half_pair(kk[:, :128], kvh), _half_pair(kk[:, 128:], kvh)) for kvh in range(SWA_KV_HEADS)]


def _swa_probs(n, q_ref, ops, bias_scr, sinkc_ref, from_prev):
    lgs = []
    for kvh in range(SWA_KV_HEADS):
        (ka, kb), _ = ops[kvh]
        for p in range(4):
            q2 = q_ref[:, kvh * 512 + p * 128:kvh * 512 + p * 128 + 128]
            lgs += [_fold(_dot(q2, ka, NT), from_prev), _fold(_dot(q2, kb, NT), from_prev)]
    lg = jnp.concatenate(lgs, axis=0) * (SWA_HD ** -0.5) + bias_scr[...].reshape(ALL_HEADS, WINDOW)
    rowi = jnp.bitwise_and(lax.broadcasted_iota(jnp.int32, (ALL_HEADS, WINDOW), 0), WINDOW - 1)
    col = lax.broadcasted_iota(jnp.int32, (ALL_HEADS, WINDOW), 1)
    no_prev = jnp.where(n > 0, 0, 4 * WINDOW)
    lg = jnp.where(jnp.logical_or(col <= rowi, col > rowi + no_prev), lg, NEG_INF)
    sink = sinkc_ref[...]
    m = jnp.maximum(jnp.max(lg, axis=-1, keepdims=True), sink)
    e = jnp.exp(lg - m)
    es = jnp.exp(sink - m)
    den = jnp.sum(e, axis=-1, keepdims=True) + es
    return e / den, es / den


def _swa_fwd(q, kv, g, bucket, rel_bias, sink_col):
    S = q.shape[0]
    nb = S // WINDOW

    def body(q_ref, kv_ref, g_ref, bucket_ref, relb_ref, sinkc_ref, o_ref, y_ref, bias_scr):
        n = pl.program_id(0)
        kk, from_prev, _, _ = _swa_common(n, kv_ref, bucket_ref, relb_ref, bias_scr)
        ops = _swa_operands(kk)
        pr, _ = _swa_probs(n, q_ref, ops, bias_scr, sinkc_ref, from_prev)
        for kvh in range(SWA_KV_HEADS):
            _, (va, vb) = ops[kvh]
            for p in range(4):
                c0 = kvh * 512 + p * 128
                r0 = (kvh * 8 + 2 * p) * WINDOW
                o2 = (_dot(_unfold(pr[r0:r0 + WINDOW], from_prev).astype(BF), va, NN)
                      + _dot(_unfold(pr[r0 + WINDOW:r0 + 2 * WINDOW], from_prev).astype(BF), vb, NN))
                o_ref[:, c0:c0 + 128] = o2
                gv = g_ref[:, c0:c0 + 128]
                y_ref[:, c0:c0 + 128] = (o2 * (gv * _sigmoid(gv))).astype(BF)

    blk = pl.BlockSpec((WINDOW, 1024), lambda n: (n, 0))
    smem = pl.BlockSpec(memory_space=pltpu.SMEM)
    sinkc = pl.BlockSpec((ALL_HEADS, 1), lambda n: (0, 0))
    return pl.pallas_call(
        body, name="swa_fwd", grid=(nb,),
        in_specs=[blk, pl.BlockSpec((S, 256), lambda n: (0, 0)), blk, pl.BlockSpec((WINDOW, WINDOW), lambda n: (0, 0)), smem, sinkc],
        out_specs=[blk, blk],
        out_shape=[jax.ShapeDtypeStruct((S, 1024), F32), jax.ShapeDtypeStruct((S, 1024), BF)],
        scratch_shapes=[pltpu.VMEM((SWA_HEADS, WINDOW, WINDOW), F32)],
        compiler_params=_params(("arbitrary",)),
    )(q, kv, g, bucket, rel_bias, sink_col)


def _swa_bwd(q, kv, g, o, dy, bucket, rel_bias, sink_col):
    S = q.shape[0]
    nb = S // WINDOW

    def body(q_ref, kv_ref, g_ref, o_ref, dy_ref, bucket_ref, relb_ref, sinkc_ref,
             dq_ref, dg_ref, dkv_ref, dsink_ref, drel_ref, bias_scr, dbias_scr, dsink_scr):
        n = pl.program_id(0)

        @pl.when(n == 0)
        def _():
            dbias_scr[...] = jnp.zeros_like(dbias_scr)
            dsink_scr[...] = jnp.zeros_like(dsink_scr)
            dkv_ref[...] = jnp.zeros_like(dkv_ref)

        kk, from_prev, prev0, cur0 = _swa_common(n, kv_ref, bucket_ref, relb_ref, bias_scr)
        ops = _swa_operands(kk)
        pr, ps = _swa_probs(n, q_ref, ops, bias_scr, sinkc_ref, from_prev)
        do2s, dps = [], []
        for kvh in range(SWA_KV_HEADS):
            _, (va, vb) = ops[kvh]
            for p in range(4):
                c0 = kvh * 512 + p * 128
                gv = g_ref[:, c0:c0 + 128]
                sg = _sigmoid(gv)
                dyv = dy_ref[:, c0:c0 + 128]
                dg_ref[:, c0:c0 + 128] = (dyv * o_ref[:, c0:c0 + 128] * (sg * (1.0 + gv * (1.0 - sg)))).astype(BF)
                do2 = (dyv * (gv * sg)).astype(BF)
                do2s.append(do2)
                dps += [_fold(_dot(do2, va, NT), from_prev), _fold(_dot(do2, vb, NT), from_prev)]
        dp = jnp.concatenate(dps, axis=0)
        delta = jnp.sum(pr * dp, axis=-1, keepdims=True)
        ds = pr * (dp - delta)
        dbias_scr[...] += ds.reshape(SWA_HEADS, WINDOW, WINDOW)
        dsink_scr[...] += ps * delta
        dsc = ds * (SWA_HD ** -0.5)
        lo256 = lax.broadcasted_iota(jnp.int32, (2 * WINDOW, 128), 1) < SWA_HD
        dks, dvs = [], []
        for kvh in range(SWA_KV_HEADS):
            (ka, kb), _ = ops[kvh]
            dka = jnp.zeros((2 * WINDOW, 128), F32)
            dkb, dva, dvb = dka, dka, dka
            for p in range(4):
                c0 = kvh * 512 + p * 128
                r0 = (kvh * 8 + 2 * p) * WINDOW
                q2 = q_ref[:, c0:c0 + 128]
                do2 = do2s[kvh * 4 + p]
                ds0 = _unfold(dsc[r0:r0 + WINDOW], from_prev).astype(BF)
                ds1 = _unfold(dsc[r0 + WINDOW:r0 + 2 * WINDOW], from_prev).astype(BF)
                dq_ref[:, c0:c0 + 128] = (_dot(ds0, ka, NN) + _dot(ds1, kb, NN)).astype(BF)
                dka = dka + _dot(ds0, q2, TN)
                dkb = dkb + _dot(ds1, q2, TN)
                dva = dva + _dot(_unfold(pr[r0:r0 + WINDOW], from_prev).astype(BF), do2, TN)
                dvb = dvb + _dot(_unfold(pr[r0 + WINDOW:r0 + 2 * WINDOW], from_prev).astype(BF), do2, TN)
            dks.append(jnp.where(lo256, dka, 0.0) + pltpu.roll(jnp.where(lo256, 0.0, dkb), SWA_HD, 1))
            dvs.append(jnp.where(lo256, dva, 0.0) + pltpu.roll(jnp.where(lo256, 0.0, dvb), SWA_HD, 1))
        dk = dks[0] + pltpu.roll(dks[1], SWA_HD, 1)
        dv = dvs[0] + pltpu.roll(dvs[1], SWA_HD, 1)
        dkv_ref[pl.ds(prev0, WINDOW), 0:128] += dk[:WINDOW]
        dkv_ref[pl.ds(prev0, WINDOW), 128:256] += dv[:WINDOW]
        dkv_ref[pl.ds(cur0, WINDOW), 0:128] += dk[WINDOW:]
        dkv_ref[pl.ds(cur0, WINDOW), 128:256] += dv[WINDOW:]

        @pl.when(n == nb - 1)
        def _():
            dsink_ref[...] = -jnp.sum(dsink_scr[...].reshape(SWA_HEADS, WINDOW, 1), axis=1)
            bk = bucket_ref[...]
            sums = []
            for b in range(REL_BUCKETS):
                sums.append(jnp.sum(jnp.where((bk == b)[None], dbias_scr[...], 0.0), axis=1))
            drel_ref[...] = jnp.sum(jnp.concatenate(sums, axis=0), axis=1, keepdims=True)

    blk = pl.BlockSpec((WINDOW, 1024), lambda n: (n, 0))
    smem = pl.BlockSpec(memory_space=pltpu.SMEM)
    whole = lambda shape: pl.BlockSpec(shape, lambda n: (0, 0))
    return pl.pallas_call(
        body, name="swa_bwd", grid=(nb,),
        in_specs=[blk, whole((S, 256)), blk, blk, blk, whole((WINDOW, WINDOW)), smem, whole((ALL_HEADS, 1))],
        out_specs=[blk, blk, whole((S, 256)), whole((SWA_HEADS, 1)), whole((REL_BUCKETS * SWA_HEADS, 1))],
        out_shape=[jax.ShapeDtypeStruct((S, 1024), BF), jax.ShapeDtypeStruct((S, 1024), BF),
                   jax.ShapeDtypeStruct((S, 256), F32), jax.ShapeDtypeStruct((SWA_HEADS, 1), F32),
                   jax.ShapeDtypeStruct((REL_BUCKETS * SWA_HEADS, 1), F32)],
        scratch_shapes=[pltpu.VMEM((SWA_HEADS, WINDOW, WINDOW), F32), pltpu.VMEM((SWA_HEADS, WINDOW, WINDOW), F32),
                        pltpu.VMEM((ALL_HEADS, 1), F32)],
        compiler_params=_params(("arbitrary",)),
    )(q, kv, g, o, dy, bucket, rel_bias, sink_col)


def _mem_probs(qh, mk):
    lg = _dot(qh, mk, NT) * (MEM_HD ** -0.5)
    e = jnp.exp(lg - jnp.max(lg, axis=-1, keepdims=True))
    return e / jnp.sum(e, axis=-1, keepdims=True)


def _mem_fwd(q, mkv, g):
    S = q.shape[0]
    M = mkv.shape[0]
    tq = 256

    def body(q_ref, mkv_ref, g_ref, o_ref, y_ref):
        for h in range(MEM_HEADS):
            c0 = h * MEM_HD
            pr = _mem_probs(q_ref[:, c0:c0 + MEM_HD], mkv_ref[:, c0:c0 + MEM_HD])
            o = _dot(pr.astype(BF), mkv_ref[:, D_MEM + c0:D_MEM + c0 + MEM_HD], NN)
            o_ref[:, c0:c0 + MEM_HD] = o
            gv = g_ref[:, c0:c0 + MEM_HD]
            y_ref[:, c0:c0 + MEM_HD] = (o * (gv * _sigmoid(gv))).astype(BF)

    blk = pl.BlockSpec((tq, D_MEM), lambda i: (i, 0))
    return pl.pallas_call(
        body, name="mem_fwd", grid=(S // tq,),
        in_specs=[blk, pl.BlockSpec((M, 2 * D_MEM), lambda i: (0, 0)), blk], out_specs=[blk, blk],
        out_shape=[jax.ShapeDtypeStruct((S, D_MEM), F32), jax.ShapeDtypeStruct((S, D_MEM), BF)],
        compiler_params=_params(("parallel",)),
    )(q, mkv, g)


def _mem_bwd(q, mkv, g, o, dy):
    S = q.shape[0]
    M = mkv.shape[0]
    tq = 256

    def body(q_ref, mkv_ref, g_ref, o_ref, dy_ref, dq_ref, dg_ref, dmkv_ref):
        @pl.when(pl.program_id(0) == 0)
        def _():
            dmkv_ref[...] = jnp.zeros_like(dmkv_ref)

        for h in range(MEM_HEADS):
            c0 = h * MEM_HD
            qh = q_ref[:, c0:c0 + MEM_HD]
            mk = mkv_ref[:, c0:c0 + MEM_HD]
            mv = mkv_ref[:, D_MEM + c0:D_MEM + c0 + MEM_HD]
            gv = g_ref[:, c0:c0 + MEM_HD]
            sg = _sigmoid(gv)
            dyv = dy_ref[:, c0:c0 + MEM_HD]
            dg_ref[:, c0:c0 + MEM_HD] = (dyv * o_ref[:, c0:c0 + MEM_HD] * (sg * (1.0 + gv * (1.0 - sg)))).astype(BF)
            do = (dyv * (gv * sg)).astype(BF)
            pr = _mem_probs(qh, mk)
            dp = _dot(do, mv, NT)
            ds = pr * (dp - jnp.sum(pr * dp, axis=-1, keepdims=True))
            dsb = (ds * (MEM_HD ** -0.5)).astype(BF)
            dq_ref[:, c0:c0 + MEM_HD] = _dot(dsb, mk, NN).astype(BF)
            dmkv_ref[:, c0:c0 + MEM_HD] += _dot(dsb, qh, TN)
            dmkv_ref[:, D_MEM + c0:D_MEM + c0 + MEM_HD] += _dot(pr.astype(BF), do, TN)

    blk = pl.BlockSpec((tq, D_MEM), lambda i: (i, 0))
    whole = pl.BlockSpec((M, 2 * D_MEM), lambda i: (0, 0))
    return pl.pallas_call(
        body, name="mem_bwd", grid=(S // tq,),
        in_specs=[blk, whole, blk, blk, blk], out_specs=[blk, blk, whole],
        out_shape=[jax.ShapeDtypeStruct((S, D_MEM), BF), jax.ShapeDtypeStruct((S, D_MEM), BF),
                   jax.ShapeDtypeStruct((M, 2 * D_MEM), F32)],
        compiler_params=_params(("arbitrary",)),
    )(q, mkv, g, o, dy)


MERGE_TN = 512


def _merge_specs(tm):
    ytile = pl.BlockSpec((tm, 1024), lambda i, j: (i, 0))
    wblk = pl.BlockSpec((MERGE_TN, 1024), lambda i, j: (j, 0))
    gls = [pl.BlockSpec((None, tm, MERGE_TN), (lambda i, j, br=br: (br, i, j))) for br in range(3)]
    otile = pl.BlockSpec((tm, MERGE_TN), lambda i, j: (i, j))
    return ytile, wblk, gls, otile


def _merge_fwd(ys, ws, gl, tm):
    S = gl.shape[1]

    def body(y0, y1, y2, w0, w1, w2, g0, g1, g2, o_ref):
        acc = None
        for y_ref, w_ref, g_ref in ((y0, w0, g0), (y1, w1, g1), (y2, w2, g2)):
            term = _sigmoid(g_ref[...]) * _dot(y_ref[...], w_ref[...], NT)
            acc = term if acc is None else acc + term
        o_ref[...] = acc.astype(BF)

    ytile, wblk, gls, otile = _merge_specs(tm)
    return pl.pallas_call(
        body, name="merge_fwd", grid=(S // tm, D_MODEL // MERGE_TN),
        in_specs=[ytile] * 3 + [wblk] * 3 + gls, out_specs=otile,
        out_shape=jax.ShapeDtypeStruct((S, D_MODEL), BF),
        compiler_params=_params(("parallel", "arbitrary")),
    )(*ys, *ws, gl, gl, gl)


def _merge_bwd(dout, w_out, ys, ws, gl, tm):
    S = gl.shape[1]

    def body(do_ref, wo_ref, y0, y1, y2, w0, w1, w2, g0, g1, g2, dg0, dg1, dg2, dp0, dp1, dp2):
        dm = _dot(do_ref[...], wo_ref[...], NT)
        for y_ref, w_ref, g_ref, dg_ref, dp_ref in ((y0, w0, g0, dg0, dp0), (y1, w1, g1, dg1, dp1), (y2, w2, g2, dg2, dp2)):
            gate = _sigmoid(g_ref[...])
            pv = _dot(y_ref[...], w_ref[...], NT)
            dg_ref[...] = (dm * pv * gate * (1.0 - gate)).astype(BF)
            dp_ref[...] = (dm * gate).astype(BF)

    ytile, wblk, gls, otile = _merge_specs(tm)
    out = jax.ShapeDtypeStruct((S, D_MODEL), BF)
    return pl.pallas_call(
        body, name="merge_bwd", grid=(S // tm, D_MODEL // MERGE_TN),
        in_specs=[pl.BlockSpec((tm, D_MODEL), lambda i, j: (i, 0)), pl.BlockSpec((MERGE_TN, D_MODEL), lambda i, j: (j, 0))]
        + [ytile] * 3 + [wblk] * 3 + gls,
        out_specs=[otile] * 6, out_shape=[out] * 6,
        compiler_params=_params(("parallel", "arbitrary")),
    )(dout, w_out, *ys, *ws, gl, gl, gl)


def _out_loss(merged, w_out, x, target, post_g, tm):
    S = x.shape[0]

    def body(m_ref, w_ref, x_ref, t_ref, g_ref, dout_ref, dy_ref, loss_ref, dpost_ref):
        @pl.when(pl.program_id(0) == 0)
        def _():
            loss_ref[...] = jnp.zeros_like(loss_ref)
            dpost_ref[...] = jnp.zeros_like(dpost_ref)

        out = _dot(m_ref[...], w_ref[...], NN)
        r = lax.rsqrt(jnp.mean(out * out, axis=-1, keepdims=True) + EPS)
        nrm = out * r
        gv = g_ref[...]
        err = (x_ref[...] + nrm * gv) - t_ref[...]
        sq = jnp.sum(jnp.sum(err * err, axis=1, keepdims=True), axis=0, keepdims=True)
        loss_ref[...] += sq * (0.5 / D_MODEL)
        dy = err * (1.0 / D_MODEL)
        dy_ref[...] = dy
        dpost_ref[...] += jnp.sum(dy * nrm, axis=0, keepdims=True)
        dn = dy * gv
        dout_ref[...] = (r * (dn - nrm * jnp.mean(dn * nrm, axis=-1, keepdims=True))).astype(BF)

    row = pl.BlockSpec((tm, D_MODEL), lambda i: (i, 0))
    return pl.pallas_call(
        body, name="out_loss", grid=(S // tm,),
        in_specs=[row, pl.BlockSpec((D_MODEL, D_MODEL), lambda i: (0, 0)), row, row, pl.BlockSpec((1, D_MODEL), lambda i: (0, 0))],
        out_specs=[row, row, pl.BlockSpec((8, 128), lambda i: (0, 0)), pl.BlockSpec((1, D_MODEL), lambda i: (0, 0))],
        out_shape=[jax.ShapeDtypeStruct((S, D_MODEL), BF), jax.ShapeDtypeStruct((S, D_MODEL), F32),
                   jax.ShapeDtypeStruct((8, 128), F32), jax.ShapeDtypeStruct((1, D_MODEL), F32)],
        compiler_params=_params(("arbitrary",)),
    )(merged, w_out, x, target, post_g)


DH_DX_CHUNK = 64


def _dh_dx(dproj, w_in, x, dy, pre_g, tm, tk):
    S = x.shape[0]
    nk = D_IN // tk

    def body(dp_ref, w_ref, x_ref, dy_ref, g_ref, dx_ref, dpre_ref, acc_ref):
        i, k = pl.program_id(0), pl.program_id(1)

        @pl.when(jnp.logical_and(i == 0, k == 0))
        def _():
            dpre_ref[...] = jnp.zeros_like(dpre_ref)

        @pl.when(k == 0)
        def _():
            acc_ref[...] = jnp.zeros_like(acc_ref)

        acc_ref[...] += _dot(dp_ref[...], w_ref[...], NN)

        @pl.when(k == nk - 1)
        def _():
            def chunk(c, carry):
                rows = pl.ds(pl.multiple_of(c * DH_DX_CHUNK, DH_DX_CHUNK), DH_DX_CHUNK)
                dh = acc_ref[rows, :]
                xv = x_ref[rows, :]
                r = lax.rsqrt(jnp.mean(xv * xv, axis=-1, keepdims=True) + EPS)
                nrm = xv * r
                dpre_ref[...] += jnp.sum(dh * nrm, axis=0, keepdims=True)
                dn = dh * g_ref[...]
                dx_ref[rows, :] = r * (dn - nrm * jnp.mean(dn * nrm, axis=-1, keepdims=True)) + dy_ref[rows, :]
                return carry
            lax.fori_loop(0, tm // DH_DX_CHUNK, chunk, 0)

    row = pl.BlockSpec((tm, D_MODEL), lambda i, k: (i, 0))
    vec = pl.BlockSpec((1, D_MODEL), lambda i, k: (0, 0))
    return pl.pallas_call(
        body, name="dh_dx", grid=(S // tm, nk),
        in_specs=[pl.BlockSpec((tm, tk), lambda i, k: (i, k)), pl.BlockSpec((tk, D_MODEL), lambda i, k: (k, 0)), row, row, vec],
        out_specs=[row, vec],
        out_shape=[jax.ShapeDtypeStruct((S, D_MODEL), F32), jax.ShapeDtypeStruct((1, D_MODEL), F32)],
        scratch_shapes=[pltpu.VMEM((tm, D_MODEL), F32)],
        compiler_params=_params(("arbitrary", "arbitrary"), large=True),
    )(dproj, w_in, x, dy, pre_g)


def _sum_parts(parts, name):
    P, R, C = parts.shape
    tr = max(t for t in range(8, 513, 8) if R % t == 0)

    def body(p_ref, o_ref):
        acc = p_ref[0]
        for j in range(1, P):
            acc = acc + p_ref[j]
        o_ref[...] = acc

    return pl.pallas_call(
        body, name=name, grid=(R // tr,),
        in_specs=[pl.BlockSpec((P, tr, C), lambda i: (0, i, 0))], out_specs=pl.BlockSpec((tr, C), lambda i: (i, 0)),
        out_shape=jax.ShapeDtypeStruct((R, C), F32), compiler_params=_params(("parallel",)),
    )(parts)


def _adamw(lands, sums, chip, w, m, v, name):
    lands = list(lands) if isinstance(lands, (list, tuple)) else [lands]
    sums = list(sums) if isinstance(sums, (list, tuple)) else [sums]
    _, R, cols = lands[0].shape
    C = cols * len(lands)
    tr = max(t for t in range(16, 257, 16) if R % t == 0)
    c1 = 1.0 - ADAM_B1 ** ADAM_STEP
    c2 = 1.0 - ADAM_B2 ** ADAM_STEP

    def body(chip_ref, *refs):
        p_refs = refs[:4 * len(lands)]
        w_ref, m_ref, v_ref, g_ref, d_ref, nm_ref, nv_ref = refs[4 * len(lands):]
        gs = []
        for q in range(len(lands)):
            gq = p_refs[4 * q + 3][...].astype(F32)
            for j in range(3):
                gq = gq + p_refs[4 * q + j][...].astype(F32)
            gs.append(gq)
        g = gs[0] if len(gs) == 1 else jnp.concatenate(gs, axis=1)
        nm = ADAM_B1 * m_ref[...] + (1.0 - ADAM_B1) * g
        nv = ADAM_B2 * v_ref[...] + (1.0 - ADAM_B2) * (g * g)
        g_ref[...] = g
        nm_ref[...] = nm
        nv_ref[...] = nv
        d_ref[...] = -ADAM_LR * ((nm / c1) / (jnp.sqrt(nv / c2) + ADAM_EPS) + ADAM_WD * w_ref[...])

    tile = pl.BlockSpec((None, tr, C), lambda i, c_ref: (0, i, 0))
    specs, operands = [], []
    for q in range(len(lands)):
        for k in range(3):
            specs.append(pl.BlockSpec((None, tr, cols), (lambda i, c_ref, k=k: (k + (c_ref[0] <= k).astype(jnp.int32), i, 0))))
            operands.append(lands[q])
        specs.append(pl.BlockSpec((None, tr, cols), (lambda i, c_ref: (c_ref[0], i, 0))))
        operands.append(sums[q])
    return pl.pallas_call(
        body, name=name,
        grid_spec=pltpu.PrefetchScalarGridSpec(num_scalar_prefetch=1, grid=(R // tr,), in_specs=specs + [tile, tile, tile],
                                               out_specs=[tile] * 4),
        out_shape=[jax.ShapeDtypeStruct((1, R, C), F32)] * 4, compiler_params=_params(("parallel",)),
    )(chip, *operands, w, m, v)


def _adamw_small(gs, ws, ms, vs):
    n = len(ws)
    c1 = 1.0 - ADAM_B1 ** ADAM_STEP
    c2 = 1.0 - ADAM_B2 ** ADAM_STEP

    def flat2(a):
        return a.reshape(-1, a.shape[-1])

    def body(*refs):
        ins, outs = refs[:4 * n], refs[4 * n:]
        for a in range(n):
            g, w, m, v = (ins[k * n + a][...] for k in range(4))
            nm = ADAM_B1 * m + (1.0 - ADAM_B1) * g
            nv = ADAM_B2 * v + (1.0 - ADAM_B2) * (g * g)
            outs[a][...] = g
            outs[n + a][...] = -ADAM_LR * ((nm / c1) / (jnp.sqrt(nv / c2) + ADAM_EPS) + ADAM_WD * w)
            outs[2 * n + a][...] = nm
            outs[3 * n + a][...] = nv

    shapes = [flat2(w).shape for w in ws]
    out = pl.pallas_call(
        body, name="adamw_small", out_shape=[jax.ShapeDtypeStruct(sh, F32) for sh in shapes] * 4,
        compiler_params=_params(),
    )(*[g.reshape(sh) for g, sh in zip(gs, shapes)], *[flat2(a) for a in (*ws, *ms, *vs)])
    return [[out[k * n + a].reshape(ws[a].shape) for a in range(n)] for k in range(4)]


PROJ_ROWS = 512


def _project(h, w_t):
    S = h.shape[0]
    n_tiles = D_IN // SEG_TILE
    ranges = [(c0 // SEG_TILE, (c0 + width) // SEG_TILE) for _, c0, width, _ in SEGMENTS]

    def body(h_ref, w_ref, *outs):
        j = pl.program_id(0)
        for (j0, j1), (_, _, _, dt), o_ref in zip(ranges, SEGMENTS, outs):
            @pl.when(jnp.logical_and(j >= j0, j < j1))
            def _(o_ref=o_ref, dt=dt):
                for c in range(S // PROJ_ROWS):
                    rows = pl.ds(c * PROJ_ROWS, PROJ_ROWS)
                    o_ref[rows, :] = _dot(h_ref[rows, :], w_ref[...], NT).astype(dt)

    out_shapes, out_specs = [], []
    for (j0, j1), (name, _, width, dt) in zip(ranges, SEGMENTS):
        if name == "gl":
            per = (j1 - j0) // 3
            out_shapes.append(jax.ShapeDtypeStruct((3, S, width // 3), dt))
            out_specs.append(pl.BlockSpec((None, S, SEG_TILE), (lambda j, j0=j0, j1=j1, per=per: (
                jnp.clip(j - j0, 0, j1 - j0 - 1) // per, 0, jnp.clip(j - j0, 0, j1 - j0 - 1) % per))))
        else:
            out_shapes.append(jax.ShapeDtypeStruct((S, width), dt))
            out_specs.append(pl.BlockSpec((S, SEG_TILE), (lambda j, j0=j0, j1=j1: (0, jnp.clip(j - j0, 0, j1 - j0 - 1)))))
    outs = pl.pallas_call(
        body, name="proj", grid=(n_tiles,),
        in_specs=[pl.BlockSpec((S, D_MODEL), lambda j: (0, 0)), pl.BlockSpec((SEG_TILE, D_MODEL), lambda j: (j, 0))],
        out_specs=out_specs, out_shape=out_shapes,
        compiler_params=_params(("arbitrary",), large=True),
    )(h, w_t)
    return {name: o for (name, _, _, _), o in zip(SEGMENTS, outs)}


def _forward_a(x, mem, pre_g, mem_g, w_in, conv_w, conv_b, w_a, b_a, w_x, b_x, lam, sinks, rel_bias):
    S = x.shape[0]
    st = dict(T=min(512, S // 2), tm=min(512, S), bucket=_rel_bucket_map())
    st["h"] = _rms_fwd(x, pre_g, "pre_norm")
    st["memn"] = _rms_fwd(mem, mem_g, "mem_norm")
    seg = st["seg"] = _project(st["h"], w_in)
    st["h_rg"], st["y_rg"] = _rglru_fwd(seg["xr"], seg["g_rg"], conv_w, conv_b, w_a, b_a, w_x, b_x, lam, st["T"])
    st["o_swa"], st["y_swa"] = _swa_fwd(seg["q_s"], seg["kv"], seg["g_swa"], st["bucket"], rel_bias, _sink_column(sinks))
    return st


def _forward_b(st, x, target, post_g, w_memkv, wbr, w_out):
    S = x.shape[0]
    M = st["memn"].shape[0]
    seg = st["seg"]
    st["mkv"] = _matmul(st["memn"], w_memkv, "nn", M, 2 * D_MEM, D_MODEL, M, 512, D_MODEL, BF, "mem_kv")
    st["o_mem"], st["y_mem"] = _mem_fwd(seg["q_m"], st["mkv"], seg["g_mem"])
    st["ys"] = (st["y_rg"], st["y_swa"], st["y_mem"])
    st["merged"] = _merge_fwd(st["ys"], wbr, seg["gl"], st["tm"])
    st["dout"], st["dy"], st["loss"], st["dpost"] = _out_loss(st["merged"], w_out, x, target, post_g, min(256, S))
    return st


def _backward_a1(st, wbr, w_out):
    S = st["h"].shape[0]
    seg, ys, tm = st["seg"], st["ys"], st["tm"]
    st["dw_out"] = _matmul(st["merged"], st["dout"], "tn", D_MODEL, D_MODEL, S, 256, D_MODEL, S, BF, "dw_out", out_blocked="row")
    dgl0, dgl1, dgl2, dp0, dp1, dp2 = _merge_bwd(st["dout"], w_out, ys, wbr, seg["gl"], tm)
    st["dgl"] = (dgl0, dgl1, dgl2)
    dys, dwbr = [], []
    for i, dp in enumerate((dp0, dp1, dp2)):
        dys.append(_matmul(dp, wbr[i], "nn", S, 1024, D_MODEL, tm, 1024, D_MODEL, F32, "dy_br%d" % i))
        dwbr.append(_matmul(ys[i], dp, "tn", 1024, D_MODEL, S, 1024, 256, S, BF, "dw_br%d" % i, out_blocked="col"))
    st["dys"], st["dwbr"] = dys, dwbr
    return st


def _backward_a2(st, mem, w_memkv, conv_w, conv_b, w_a, b_a, w_x, b_x, lam):
    M = mem.shape[0]
    seg, dys = st["seg"], st["dys"]
    st["dq_m"], st["dg_mem"], dmkv = _mem_bwd(seg["q_m"], st["mkv"], seg["g_mem"], st["o_mem"], dys[2])
    dmkv_b = dmkv.astype(BF)
    st["dw_memkv"] = _matmul(st["memn"], dmkv_b, "tn", D_MODEL, 2 * D_MEM, M, 256, 2 * D_MEM, M, BF, "dw_memkv", out_blocked="row")
    dmemn = _matmul(dmkv_b, w_memkv, "nt", M, D_MODEL, 2 * D_MEM, M, 512, 2 * D_MEM, F32, "dmemn")
    st["dmem_g"] = _rms_gain_grad(dmemn, mem, "dmem_gain")
    st["dxr"], st["dg_rg"], st["dw_a"], st["dw_x"], st["dvec"] = _rglru_bwd(
        seg["xr"], seg["g_rg"], st["h_rg"], dys[0], conv_w, conv_b, w_a, b_a, w_x, b_x, lam, st["T"])
    return st


def _backward_b(st, rel_bias, sinks):
    seg = st["seg"]
    dq_s, dg_swa, dkv, dsinks, drel = _swa_bwd(seg["q_s"], seg["kv"], seg["g_swa"], st["o_swa"], st["dys"][1],
                                               st["bucket"], rel_bias, _sink_column(sinks))
    st["dsinks"], st["drel"] = dsinks.reshape(1, SWA_HEADS), drel.reshape(REL_BUCKETS, SWA_HEADS)
    st["dproj"] = jnp.concatenate([st["dxr"], st["dg_rg"], dq_s, dkv.astype(BF), dg_swa, st["dq_m"], st["dg_mem"], *st["dgl"]], axis=1)
    return st


def _dw_in_half(st, half, dep=None):
    S = st["h"].shape[0]
    dw = _matmul(st["dproj"], st["h"], "tn", D_IN, D_MODEL // 2, S, D_IN_TILE, D_MODEL // 2, S, BF, "dw_in%d" % half, b_noff=half, dep=dep)
    return dw.reshape(N_DEV, D_IN // N_DEV, D_MODEL // 2)


def _owner_blocks(a):
    return jnp.swapaxes(a.reshape((4, 2) + a.shape[1:]), 0, 1)


def _pad_rows(a, rows):
    a = a.reshape(-1, 128) if a.shape[-1] % 128 == 0 else jnp.pad(a, ((0, 0), (0, 128 - a.shape[-1])))
    return jnp.pad(a, ((0, rows - a.shape[0]), (0, 0))) if a.shape[0] < rows else a


def kernel(x, mem, pre_norm_g, post_norm_g, mem_norm_g, w_in, conv_w, conv_b, w_rg_a, b_rg_a, w_rg_x, b_rg_x, lru_lambda, swa_sinks, rel_bias, w_mem_kv, w_br_rg, w_br_swa, w_br_mem, w_out, loss_target, m_pre_norm_g, m_post_norm_g, m_mem_norm_g, m_w_in, m_conv_w, m_conv_b, m_w_rg_a, m_b_rg_a, m_w_rg_x, m_b_rg_x, m_lru_lambda, m_swa_sinks, m_rel_bias, m_w_mem_kv, m_w_br_rg, m_w_br_swa, m_w_br_mem, m_w_out, v_pre_norm_g, v_post_norm_g, v_mem_norm_g, v_w_in, v_conv_w, v_conv_b, v_w_rg_a, v_b_rg_a, v_w_rg_x, v_b_rg_x, v_lru_lambda, v_swa_sinks, v_rel_bias, v_w_mem_kv, v_w_br_rg, v_w_br_swa, v_w_br_mem, v_w_out):
    cx, cy, cc = lax.axis_index("x"), lax.axis_index("y"), lax.axis_index("c")
    me = 4 * cx + 2 * cy + cc
    chip = 2 * cx + cy
    core = jnp.reshape(cc, (1,)).astype(jnp.int32)
    x0, mem0 = x[0], mem[0]
    w_a_b, w_x_b = w_rg_a[0].astype(BF), w_rg_x[0].astype(BF)

    def landing(own, slot, slots):
        return lax.dynamic_update_slice(lax.empty((slots,) + own.shape, own.dtype), own[None], (slot,) + (0,) * own.ndim)


    def swap_start(parts, tag):
        return _exchange_start(parts, [lax.empty((4,) + p.shape[-2:], p.dtype) for p in parts], _plan_swap([p.ndim for p in parts]),
                               "swap_%s_start" % tag)

    def scatter_start(swap, after, tag, prefill=()):
        s_send, s_recv, parts, got, _ = swap
        got = _exchange_wait(s_send, s_recv, parts, got, _plan_swap([p.ndim for p in parts]), after, "swap_%s_wait" % tag)
        sums = [_pair_sum(p, g, core, "scatter_%s_sum%d" % (tag, i)) for i, (p, g) in enumerate(zip(parts, got))]
        lands = [landing(lax.dynamic_index_in_dim(s, chip, 0, keepdims=False), chip, 4) if i in prefill
                 else lax.empty(s.shape, s.dtype) for i, s in enumerate(sums)]
        return _exchange_start(sums, lands, _plan_scatter(len(sums)), "scatter_%s_start" % tag)

    def corner(a):
        return a.reshape(-1, a.shape[-1])[:8, :128]

    def zero_after(a):
        return jnp.minimum(jnp.abs(a.reshape(-1)[0].astype(F32)), 0.0)

    g_in, g_cw = _all_gather_relayed([jnp.transpose(w_in[0]).astype(BF), conv_w[0]], [True, False], "gather_w_in")
    w_in_f = g_in.reshape(D_IN, D_MODEL)
    conv_w_f = jnp.transpose(g_cw, (1, 0, 2)).reshape(CONV_W, D_RNN)

    after_first = zero_after(g_cw).astype(BF)
    rest = [w.astype(BF) + after_first for w in (w_mem_kv[0], jnp.transpose(w_br_rg[0]), jnp.transpose(w_br_swa[0]),
                                                 jnp.transpose(w_br_mem[0]), w_out[0])]
    plan_g = _plan_gather(len(rest))
    zones = _place_own([lax.empty((N_DEV,) + w.shape, w.dtype) for w in rest], rest, jnp.reshape(me, (1,)).astype(jnp.int32), "gather_rest_own")
    g_send, g_recv, g_src, g_land, g_token = _exchange_start(rest, zones, plan_g, "gather_rest_start")
    st = _forward_a(x0, mem0, pre_norm_g + g_token[0:1, 0:1], mem_norm_g, w_in_f, conv_w_f, conv_b, w_a_b, b_rg_a, w_x_b, b_rg_x,
                    lru_lambda, swa_sinks, rel_bias)
    g_land = _exchange_wait(g_send, g_recv, g_src, g_land, plan_g, st["y_swa"], "gather_rest_wait")
    g_land = _forward_to_sibling(g_land, "gather_rest_forward")
    w_memkv_f = g_land[0].reshape(D_MODEL, 2 * D_MEM)
    wbr = tuple(g_land[i].reshape(D_MODEL, D_RNN) for i in (1, 2, 3))
    w_out_f = g_land[4].reshape(D_MODEL, D_MODEL)

    st = _forward_b(st, x0, loss_target[0], post_norm_g, w_memkv_f, wbr, w_out_f)
    st = _backward_a1(st, wbr, w_out_f)
    parts_a = [st["dw_out"], st["dwbr"][0], st["dwbr"][1], st["dwbr"][2]]
    plan_a = _plan_scatter(len(parts_a))
    swap_a = swap_start(parts_a, "a")
    st = _backward_a2(st, mem0, w_memkv_f, conv_w_f, conv_b + swap_a[4][0:1, 0:1], w_a_b, b_rg_a, w_x_b, b_rg_x, lru_lambda)
    a_send, a_recv, a_src, a_land, a_token = scatter_start(swap_a, st["dxr"], "a")
    parts_c = [st["dw_memkv"], _owner_blocks(st["dw_a"]), _owner_blocks(st["dw_x"])]
    plan_c = _plan_scatter(len(parts_c))
    swap_c = swap_start(parts_c, "c")

    st = _backward_b(st, rel_bias, swa_sinks + swap_c[4][0:1, 0:1] + a_token[0:1, 0:1])
    c_send, c_recv, c_src, c_land, c_token = scatter_start(swap_c, st["dsinks"], "c", prefill=(1, 2))
    plan_b = _plan_scatter(1)

    def dw_in_parts(half, dep):
        dwh = _dw_in_half(st, half, dep)
        return dwh, [dwh]

    dw0, parts_b0 = dw_in_parts(0, c_token)
    swap_b0 = swap_start(parts_b0, "b0")
    a_land = _exchange_wait(a_send, a_recv, a_src, a_land, plan_a, swap_b0[4], "scatter_a_wait")
    big = [None] * 6

    chip1 = jnp.reshape(chip, (1,)).astype(jnp.int32)

    def adamw_big(j, land, own, wt, mt, vt):
        big[j] = _adamw(land, own, chip1, wt, mt, vt, "adamw_big%d" % j)

    adamw_big(5, a_land[0], a_src[0], w_out, m_w_out, v_w_out)
    adamw_big(2, a_land[1], a_src[1], w_br_rg, m_w_br_rg, v_w_br_rg)
    adamw_big(3, a_land[2], a_src[2], w_br_swa, m_w_br_swa, v_w_br_swa)
    halves = [scatter_start(swap_b0, corner(big[3][1]), "b0")]
    dw1, parts_b1 = dw_in_parts(1, halves[0][4])
    swap_b1 = swap_start(parts_b1, "b1")
    c_land = _exchange_wait(c_send, c_recv, c_src, c_land, plan_c, swap_b1[4], "scatter_c_wait")
    g_wa_blk = _sum_parts(c_land[1], "sum_w_rg_a")
    g_wx_blk = _sum_parts(c_land[2], "sum_w_rg_x")
    adamw_big(4, a_land[3], a_src[3], w_br_mem, m_w_br_mem, v_w_br_mem)
    adamw_big(1, c_land[0], c_src[0], w_mem_kv, m_w_mem_kv, v_w_mem_kv)
    halves.append(scatter_start(swap_b1, corner(big[1][1]), "b1"))
    grad_x, dpre = _dh_dx(st["dproj"], w_in_f, x0, st["dy"], pre_norm_g + halves[1][4][0:1, 0:1], st["tm"], D_IN_TILE)
    pack = jnp.concatenate([dpre.reshape(16, 128), st["dpost"].reshape(16, 128), st["dmem_g"].reshape(16, 128),
                            st["dvec"].reshape(64, 128), _pad_rows(st["dsinks"], 8), _pad_rows(st["drel"], 32), g_wa_blk, g_wx_blk,
                            st["loss"]], axis=0)
    plan_s = _plan_everyone(1)
    s_send, s_recv, s_src, s_land, s_token = _exchange_start([pack], [landing(pack, me, N_DEV)], plan_s, "gather_small_start")
    after, b_lands, b_sums = s_token, [], []
    for half, (b_send, b_recv, b_src, b_land, _) in enumerate(halves):
        b_lands.append(_exchange_wait(b_send, b_recv, b_src, b_land, plan_b, after, "scatter_b%d_wait" % half)[0])
        b_sums.append(b_src[0])
        after = b_lands[-1]
    swap_last = lambda a: jnp.transpose(a, (0, 2, 1))
    big0_t = _adamw(b_lands, b_sums, chip1, swap_last(w_in), swap_last(m_w_in), swap_last(v_w_in), "adamw_big0")
    big[0] = [swap_last(a) for a in big0_t]
    gathered = _exchange_wait(s_send, s_recv, s_src, s_land, plan_s, corner(big0_t[1]), "gather_small_wait")[0]
    gs = _sum_parts(gathered, "sum_small")
    loss_total = gs[408, 0]
    g_pre, g_post, g_memg = gs[0:16].reshape(1, D_MODEL), gs[16:32].reshape(1, D_MODEL), gs[32:48].reshape(1, D_MODEL)
    gvec = gs[48:112].reshape(8, D_RNN)
    g_conv_w = lax.dynamic_slice(gvec[0:CONV_W], (0, me * RNN_BLOCK), (CONV_W, RNN_BLOCK))
    g_conv_b, g_b_a, g_b_x, g_lam = gvec[4:5], gvec[5:6], gvec[6:7], gvec[7:8]
    g_sinks = gs[112:113, :SWA_HEADS]
    g_rel = gs[120:152, :SWA_HEADS]
    g_w_a = gathered[:, 152:280]
    g_w_x = gathered[:, 280:408]

    g_small = (g_pre, g_post, g_memg, g_conv_b, g_b_a, g_b_x, g_lam, g_w_a, g_w_x, g_sinks, g_rel, g_conv_w)
    w_small = (pre_norm_g, post_norm_g, mem_norm_g, conv_b, b_rg_a, b_rg_x, lru_lambda, w_rg_a, w_rg_x, swa_sinks, rel_bias, conv_w)
    m_small = (m_pre_norm_g, m_post_norm_g, m_mem_norm_g, m_conv_b, m_b_rg_a, m_b_rg_x, m_lru_lambda, m_w_rg_a, m_w_rg_x, m_swa_sinks, m_rel_bias, m_conv_w)
    v_small = (v_pre_norm_g, v_post_norm_g, v_mem_norm_g, v_conv_b, v_b_rg_a, v_b_rg_x, v_lru_lambda, v_w_rg_a, v_w_rg_x, v_swa_sinks, v_rel_bias, v_conv_w)
    sm = _adamw_small(g_small, w_small, m_small, v_small)


    def leaves(k):
        s = sm[k]
        return [s[0], s[1], s[2], big[0][k], s[11], s[3], s[7], s[4], s[8], s[5], s[6], s[9], s[10],
                big[1][k], big[2][k], big[3][k], big[4][k], big[5][k]]

    return (loss_total, grad_x[None], *leaves(0), *leaves(1), *leaves(2), *leaves(3))
```

```python
import math

import jax
import jax.numpy as jnp
import numpy as np
from jax import lax
from jax.experimental import pallas as pl
from jax.experimental.pallas import tpu as pltpu

F32, BF = jnp.float32, jnp.bfloat16
MESH = pl.DeviceIdType.MESH
N_DEV = 8

D_MODEL = 2048
D_RNN = 1024
RNN_BLOCKS = 8
RNN_BLOCK = 128
CONV_W = 4
LRU_C = 8.0
SWA_HEADS = 16
SWA_KV_HEADS = 2
SWA_HD = 64
WINDOW = 128
MEM_HEADS = 4
MEM_HD = 256
D_MEM = 1024
REL_BUCKETS = 32
REL_MAX_DIST = 128
EPS = 1e-6
NEG_INF = -1e30
D_IN = 12544
SEGMENTS = (("xr", 0, 1024, F32), ("g_rg", 1024, 1024, F32), ("q_s", 2048, 1024, BF), ("kv", 3072, 256, BF),
            ("g_swa", 3328, 1024, F32), ("q_m", 4352, 1024, BF), ("g_mem", 5376, 1024, F32), ("gl", 6400, 6144, F32))
SEG_TILE = 256
D_IN_TILE = 7 * SEG_TILE

ADAM_LR, ADAM_B1, ADAM_B2, ADAM_EPS, ADAM_WD, ADAM_STEP = 0.001, 0.9, 0.999, 1e-08, 0.01, 10

NN = (((1,), (0,)), ((), ()))
NT = (((1,), (1,)), ((), ()))
TN = (((0,), (0,)), ((), ()))
MIB = 2 ** 20


def _dot(a, b, dn):
    return lax.dot_general(a, b, dn, preferred_element_type=F32)


VMEM_LIMIT_MIB = 48
VMEM_LIMIT_LARGE_MIB = 56


def _params(sem=None, large=False):
    return pltpu.CompilerParams(dimension_semantics=sem, vmem_limit_bytes=(VMEM_LIMIT_LARGE_MIB if large else VMEM_LIMIT_MIB) * MIB)


def _sigmoid(z):
    return 1.0 / (1.0 + jnp.exp(-z))


def _softplus(z):
    return jnp.maximum(z, 0.0) + jnp.log(1.0 + jnp.exp(-jnp.abs(z)))


def _expm1(z):
    p = z * (1.0 + z * (0.5 + z * (1.0 / 6 + z * (1.0 / 24 + z * (1.0 / 120 + z * (1.0 / 720 + z * (1.0 / 5040 + z / 40320)))))))
    return jnp.where(jnp.abs(z) < 0.3, p, jnp.exp(z) - 1.0)


def _flat(p):
    return 4 * p[0] + 2 * p[1] + p[2]


def _all_gather_relayed(arrs, relay, name):
    n = len(arrs)
    K = 9

    def body(*refs):
        ins, outs = refs[:n], refs[n:2 * n]
        send_sems, recv_sems, local_sems = refs[2 * n:]
        x, y, c = lax.axis_index("x"), lax.axis_index("y"), lax.axis_index("c")
        me, sib = (x, y, c), (x, y, 1 - c)
        xn, yn, dg = (1 - x, y, c), (x, 1 - y, c), (1 - x, 1 - y, c)

        def other(p):
            return (p[0], p[1], 1 - p[2])

        def rows(a, half):
            h = arrs[a].shape[0] // 2
            return pl.ds(half * h, h)

        def copy(a, k, block, to, half=None, src=None):
            dst = outs[a].at[_flat(block)]
            if half is not None:
                dst = dst.at[rows(a, half)]
            return pltpu.make_async_remote_copy(src_ref=dst if src is None else src, dst_ref=dst,
                                                send_sem=send_sems.at[a * K + k], recv_sem=recv_sems.at[a * K + k],
                                                device_id=to, device_id_type=MESH)

        mine = [pltpu.make_async_copy(ins[a], outs[a].at[_flat(me)], local_sems.at[a]) for a in range(n)]
        for cp in mine:
            cp.start()
        sends = []

        def start(cp):
            cp.start()
            sends.append(cp)

        for a in range(n):
            start(copy(a, 1, me, xn, src=ins[a]))
            start(copy(a, 2, me, yn, src=ins[a]))
            if not relay[a]:
                start(copy(a, 3, me, dg, src=ins[a]))
            start(copy(a, 0, me, sib, src=ins[a]))
        for a in range(n):
            copy(a, 1, xn, me).wait_recv()
            if relay[a]:
                start(copy(a, 3, xn, yn, half=0))
            start(copy(a, 5, xn, sib))
        for a in range(n):
            copy(a, 2, yn, me).wait_recv()
            if relay[a]:
                start(copy(a, 4, yn, xn, half=1))
            start(copy(a, 6, yn, sib))
        for a in range(n):
            if relay[a]:
                copy(a, 3, dg, me, half=0).wait_recv()
                start(copy(a, 7, dg, sib, half=0))
                copy(a, 4, dg, me, half=1).wait_recv()
                start(copy(a, 8, dg, sib, half=1))
            else:
                copy(a, 3, dg, me).wait_recv()
                start(copy(a, 7, dg, sib))
        for a in range(n):
            copy(a, 0, sib, me).wait_recv()
            copy(a, 5, other(xn), me).wait_recv()
            copy(a, 6, other(yn), me).wait_recv()
            if relay[a]:
                copy(a, 7, other(dg), me, half=0).wait_recv()
                copy(a, 8, other(dg), me, half=1).wait_recv()
            else:
                copy(a, 7, other(dg), me).wait_recv()
        for cp in sends:
            cp.wait_send()
        for cp in mine:
            cp.wait()

    any_spec = pl.BlockSpec(memory_space=pl.ANY)
    return pl.pallas_call(
        body, name=name,
        out_shape=[jax.ShapeDtypeStruct((N_DEV,) + a.shape, a.dtype) for a in arrs],
        in_specs=[any_spec] * n, out_specs=[any_spec] * n,
        scratch_shapes=[pltpu.SemaphoreType.DMA((K * n,)), pltpu.SemaphoreType.DMA((K * n,)), pltpu.SemaphoreType.DMA((n,))],
    )(*arrs)


def _chip_peers(x, y):
    return [(1 - x, y), (x, 1 - y), (1 - x, 1 - y)]


def _chip(p):
    return 2 * p[0] + p[1]


def _plan_gather(n):
    def plan(x, y, c):
        out = []
        for a in range(n):
            for peer in [(x, y, 1 - c)] + [(*ch, c) for ch in _chip_peers(x, y)]:
                out.append((a, None, ("lead", _flat((x, y, c))), peer, ("lead", _flat(peer))))
        return out
    return plan


def _plan_everyone(n):
    def plan(x, y, c):
        out = []
        for a in range(n):
            for r in range(1, N_DEV):
                peer = (1 - x if r & 4 else x, 1 - y if r & 2 else y, 1 - c if r & 1 else c)
                out.append((a, None, ("lead", _flat((x, y, c))), peer, ("lead", _flat(peer))))
        return out
    return plan


def _plan_swap(ndims):
    def plan(x, y, c):
        out = []
        for a, nd in enumerate(ndims):
            if nd == 4:
                out.append((a, 1 - c, ("all", 0), (x, y, 1 - c), ("all", 0)))
            else:
                out += [(a, 2 * j + 1 - c, ("lead", j), (x, y, 1 - c), ("lead", j)) for j in range(4)]
        return out
    return plan


def _slot(ref, where):
    kind, k = where
    return ref if kind == "all" else ref.at[k]


def _plan_scatter(n):
    def plan(x, y, c):
        out = []
        for a in range(n):
            for ch in _chip_peers(x, y):
                out.append((a, _chip(ch), ("lead", _chip((x, y))), (*ch, c), ("lead", _chip(ch))))
        return out
    return plan


HBM_SPEC = pl.BlockSpec(memory_space=pltpu.HBM)
SEM_SPEC = pl.BlockSpec(memory_space=pltpu.SEMAPHORE)


def _in_hbm(a):
    return pltpu.with_memory_space_constraint(a, pltpu.HBM)


def _exchange_start(srcs, lands, plan, name):
    n = len(srcs)
    count = len(plan(0, 0, 0))

    def body(*refs):
        src_refs, land_refs = refs[:n], refs[n:2 * n]
        send_sems, recv_sems = refs[2 * n], refs[2 * n + 1]
        token = refs[-1]
        x, y, c = lax.axis_index("x"), lax.axis_index("y"), lax.axis_index("c")
        for k, (a, si, di, peer, _) in enumerate(plan(x, y, c)):
            src = src_refs[a] if si is None else src_refs[a].at[si]
            pltpu.make_async_remote_copy(src_ref=src, dst_ref=_slot(land_refs[a], di), send_sem=send_sems.at[k],
                                         recv_sem=recv_sems.at[k], device_id=peer, device_id_type=MESH).start()
        token[...] = jnp.zeros_like(token)

    out = pl.pallas_call(
        body, name=name,
        out_shape=(pltpu.SemaphoreType.DMA((count,)), pltpu.SemaphoreType.DMA((count,)),
                   *[pltpu.HBM(a.shape, a.dtype) for a in lands], jax.ShapeDtypeStruct((8, 128), F32)),
        in_specs=[HBM_SPEC] * (2 * n),
        out_specs=(SEM_SPEC, SEM_SPEC, *([HBM_SPEC] * n), pl.BlockSpec(memory_space=pltpu.VMEM)),
        input_output_aliases={n + i: 2 + i for i in range(n)},
        compiler_params=pltpu.CompilerParams(has_side_effects=pltpu.SideEffectType.DATAFLOW_SIDE_EFFECTING),
    )(*[_in_hbm(a) for a in srcs], *[_in_hbm(a) for a in lands])
    return out[0], out[1], list(srcs), list(out[2:2 + n]), out[-1]


def _exchange_wait(send_sems, recv_sems, srcs, lands, plan, after, name):
    n = len(srcs)

    def body(*refs):
        src_refs, land_refs = refs[:n], refs[n:2 * n]
        send_sems, recv_sems = refs[2 * n], refs[2 * n + 1]
        x, y, c = lax.axis_index("x"), lax.axis_index("y"), lax.axis_index("c")
        for k, (a, si, _, peer, ri) in enumerate(plan(x, y, c)):
            src = src_refs[a] if si is None else src_refs[a].at[si]
            cp = pltpu.make_async_remote_copy(src_ref=src, dst_ref=_slot(land_refs[a], ri), send_sem=send_sems.at[k],
                                              recv_sem=recv_sems.at[k], device_id=peer, device_id_type=MESH)
            cp.wait_send()
            cp.wait_recv()

    out = pl.pallas_call(
        body, name=name,
        out_shape=tuple(pltpu.HBM(a.shape, a.dtype) for a in lands),
        in_specs=[HBM_SPEC] * (2 * n) + [SEM_SPEC, SEM_SPEC, pl.BlockSpec(memory_space=pl.ANY)],
        out_specs=tuple([HBM_SPEC] * n),
        input_output_aliases={n + i: i for i in range(n)},
        compiler_params=pltpu.CompilerParams(has_side_effects=pltpu.SideEffectType.DATAFLOW_SIDE_EFFECTING),
    )(*[_in_hbm(a) for a in srcs], *lands, send_sems, recv_sems, after)
    return list(out)


def _forward_to_sibling(lands, name):
    n = len(lands)

    def body(*refs):
        in_refs, out_refs = refs[:n], refs[n:2 * n]
        send_sems, recv_sems = refs[2 * n:]
        x, y, c = lax.axis_index("x"), lax.axis_index("y"), lax.axis_index("c")
        sibling = (x, y, 1 - c)

        def copy(a, j, slot):
            return pltpu.make_async_remote_copy(src_ref=in_refs[a].at[slot], dst_ref=out_refs[a].at[slot],
                                                send_sem=send_sems.at[a * 3 + j], recv_sem=recv_sems.at[a * 3 + j],
                                                device_id=sibling, device_id_type=MESH)

        sends = [copy(a, j, _flat((*ch, c))) for a in range(n) for j, ch in enumerate(_chip_peers(x, y))]
        for cp in sends:
            cp.start()
        for a in range(n):
            for j, ch in enumerate(_chip_peers(x, y)):
                copy(a, j, _flat((*ch, 1 - c))).wait_recv()
        for cp in sends:
            cp.wait_send()

    any_spec = pl.BlockSpec(memory_space=pl.ANY)
    return pl.pallas_call(
        body, name=name, out_shape=[jax.ShapeDtypeStruct(a.shape, a.dtype) for a in lands],
        in_specs=[any_spec] * n, out_specs=[any_spec] * n, input_output_aliases={a: a for a in range(n)},
        scratch_shapes=[pltpu.SemaphoreType.DMA((3 * n,)), pltpu.SemaphoreType.DMA((3 * n,))],
    )(*lands)


def _place_own(zones, owns, slot, name):
    n = len(zones)

    def body(slot_ref, *refs):
        for a in range(n):
            refs[2 * n + a][...] = refs[a][...]

    return pl.pallas_call(
        body, name=name,
        grid_spec=pltpu.PrefetchScalarGridSpec(
            num_scalar_prefetch=1, grid=(1,),
            in_specs=[pl.BlockSpec(o.shape, lambda i, s_ref: (0, 0)) for o in owns] + [pl.BlockSpec(memory_space=pl.ANY)] * n,
            out_specs=[pl.BlockSpec((None,) + o.shape, lambda i, s_ref: (s_ref[0], 0, 0)) for o in owns]),
        out_shape=[jax.ShapeDtypeStruct(z.shape, z.dtype) for z in zones],
        input_output_aliases={1 + n + a: a for a in range(n)},
        compiler_params=_params(("arbitrary",)),
    )(slot, *owns, *zones)


def _pair_sum(parts, got, core, name):
    R, C = parts.shape[-2:]
    tr = 256 if R % 256 == 0 else R
    mine = (pl.BlockSpec((None, None, tr, C), lambda j, i, c_ref: (c_ref[0], j, i, 0)) if parts.ndim == 4
            else pl.BlockSpec((None, tr, C), lambda j, i, c_ref: (2 * j + c_ref[0], i, 0)))

    def body(c_ref, p_ref, g_ref, o_ref):
        o_ref[...] = (p_ref[...].astype(F32) + g_ref[...].astype(F32)).astype(o_ref.dtype)

    return pl.pallas_call(
        body, name=name,
        grid_spec=pltpu.PrefetchScalarGridSpec(
            num_scalar_prefetch=1, grid=(4, R // tr),
            in_specs=[mine, pl.BlockSpec((None, tr, C), lambda j, i, c_ref: (j, i, 0))],
            out_specs=pl.BlockSpec((None, tr, C), lambda j, i, c_ref: (j, i, 0))),
        out_shape=jax.ShapeDtypeStruct((4, R, C), parts.dtype),
        compiler_params=_params(("parallel", "parallel")),
    )(core, parts, got)


def _matmul(a, b, mode, M, N, K, tm, tn, tk, out_dtype, name, b_noff=0, out_blocked=None, dep=None):
    nm, nn, nk = M // tm, N // tn, K // tk
    if mode == "nn":
        a_spec = pl.BlockSpec((tm, tk), lambda j, i, k: (i, k))
        b_spec = pl.BlockSpec((tk, tn), lambda j, i, k: (k, j + b_noff))
        dn = NN
    elif mode == "nt":
        a_spec = pl.BlockSpec((tm, tk), lambda j, i, k: (i, k))
        b_spec = pl.BlockSpec((tn, tk), lambda j, i, k: (j + b_noff, k))
        dn = NT
    else:
        a_spec = pl.BlockSpec((tk, tm), lambda j, i, k: (k, i))
        b_spec = pl.BlockSpec((tk, tn), lambda j, i, k: (k, j + b_noff))
        dn = TN
    if out_blocked == "col":
        out_shape = jax.ShapeDtypeStruct((2, 4, M, tn), out_dtype)
        out_spec = pl.BlockSpec((None, None, tm, tn), lambda j, i, k: (j % 2, j // 2, i, 0))
    elif out_blocked == "row":
        out_shape = jax.ShapeDtypeStruct((2, 4, tm, N), out_dtype)
        out_spec = pl.BlockSpec((None, None, tm, tn), lambda j, i, k: (i % 2, i // 2, 0, j))
    else:
        out_shape = jax.ShapeDtypeStruct((M, N), out_dtype)
        out_spec = pl.BlockSpec((tm, tn), lambda j, i, k: (i, j))

    n_extra = int(dep is not None)

    def body(a_ref, b_ref, *rest):
        o_ref, scratch = rest[n_extra], rest[n_extra + 1:]
        if nk == 1:
            o_ref[...] = _dot(a_ref[...], b_ref[...], dn).astype(out_dtype)
        else:
            acc_ref, = scratch
            k = pl.program_id(2)

            @pl.when(k == 0)
            def _():
                acc_ref[...] = jnp.zeros_like(acc_ref)

            acc_ref[...] += _dot(a_ref[...], b_ref[...], dn)

            @pl.when(k == nk - 1)
            def _():
                o_ref[...] = acc_ref[...].astype(out_dtype)

    return pl.pallas_call(
        body, name=name, grid=(nn, nm, nk),
        in_specs=[a_spec, b_spec] + ([] if dep is None else [pl.BlockSpec((8, 128), lambda j, i, k: (0, 0))]),
        out_specs=out_spec, out_shape=out_shape,
        scratch_shapes=[] if nk == 1 else [pltpu.VMEM((tm, tn), F32)],
        compiler_params=_params(("parallel", "parallel", "arbitrary")),
    )(a, b, *([] if dep is None else [dep]))


def _rms_fwd(x, g, name):
    R, Dm = x.shape
    tr = min(R, 256)

    def body(x_ref, g_ref, h_ref):
        xv = x_ref[...]
        r = lax.rsqrt(jnp.mean(xv * xv, axis=-1, keepdims=True) + EPS)
        h_ref[...] = (xv * r * g_ref[...]).astype(BF)

    return pl.pallas_call(
        body, name=name, grid=(R // tr,),
        in_specs=[pl.BlockSpec((tr, Dm), lambda i: (i, 0)), pl.BlockSpec((1, Dm), lambda i: (0, 0))],
        out_specs=pl.BlockSpec((tr, Dm), lambda i: (i, 0)), out_shape=jax.ShapeDtypeStruct((R, Dm), BF),
        compiler_params=_params(("parallel",)),
    )(x, g)


def _rms_gain_grad(dn, x, name):
    R, Dm = x.shape

    def body(dn_ref, x_ref, o_ref):
        xv = x_ref[...]
        r = lax.rsqrt(jnp.mean(xv * xv, axis=-1, keepdims=True) + EPS)
        o_ref[...] = jnp.sum(dn_ref[...] * xv * r, axis=0, keepdims=True)

    return pl.pallas_call(
        body, name=name, out_shape=jax.ShapeDtypeStruct((1, Dm), F32),
        compiler_params=_params(),
    )(dn, x)


def _shift_down(v, k, head8, row, T):
    if k == 0:
        return v
    r = pltpu.roll(v, k, 0)
    hr = pltpu.roll(head8, k, 0)
    top = jnp.where(row[:8] < k, hr, r[:8])
    return jnp.concatenate([top, r[8:]], axis=0)


def _shift_up(v, k, tail8, row, T):
    if k == 0:
        return v
    r = pltpu.roll(v, T - k, 0)
    tr = pltpu.roll(tail8, 8 - k, 0)
    bot = jnp.where(row[:8] >= 8 - k, tr, r[T - 8:])
    return jnp.concatenate([r[:T - 8], bot], axis=0)


def _rglru_gates(u, head8, grow, row, T, cw_ref, cb_ref, wa_ref, ba_ref, wx_ref, bx_ref, lam_ref):
    us = [_shift_down(u, k, head8, row, T) for k in range(CONV_W)]
    acc = us[0] * cw_ref[0:1, :]
    for k in range(1, CONV_W):
        acc = acc + us[k] * cw_ref[k:k + 1, :]
    conv = cb_ref[...] + acc
    cbf = conv.astype(BF)
    r_ = _sigmoid(_dot(cbf, wa_ref[0], NN) + ba_ref[...])
    i_ = _sigmoid(_dot(cbf, wx_ref[0], NN) + bx_ref[...])
    sp = _softplus(-lam_ref[...])
    la = -LRU_C * r_ * sp
    a = jnp.exp(la)
    mult_raw = jnp.sqrt(-_expm1(2.0 * la))
    mult = jnp.where(grow == 0, 1.0, mult_raw)
    return us, conv, cbf, r_, i_, sp, a, mult_raw, mult


def _rglru_specs(T, nt, rev):
    tmap = (lambda n, t: (nt - 1 - t, n)) if rev else (lambda n, t: (t, n))
    hmap = ((lambda n, t: (jnp.maximum((nt - 1 - t) * (T // 8) - 1, 0), n)) if rev
            else (lambda n, t: (jnp.maximum(t * (T // 8) - 1, 0), n)))
    tile = pl.BlockSpec((T, RNN_BLOCK), tmap)
    halo = pl.BlockSpec((8, RNN_BLOCK), hmap)
    vec = pl.BlockSpec((1, RNN_BLOCK), lambda n, t: (0, n))
    cw = pl.BlockSpec((CONV_W, RNN_BLOCK), lambda n, t: (0, n))
    wblk = pl.BlockSpec((1, RNN_BLOCK, RNN_BLOCK), lambda n, t: (n, 0, 0))
    return tile, halo, vec, cw, wblk


def _rglru_fwd(xr, g, cw, cb, wa, ba, wx, bx, lam, T):
    S = xr.shape[0]
    nt = S // T

    def body(u_ref, uh_ref, g_ref, cw_ref, cb_ref, wa_ref, ba_ref, wx_ref, bx_ref, lam_ref, h_ref, y_ref, carry):
        t = pl.program_id(1)

        @pl.when(t == 0)
        def _():
            carry[...] = jnp.zeros_like(carry)

        row = lax.broadcasted_iota(jnp.int32, (T, RNN_BLOCK), 0)
        grow = row + t * T
        head8 = jnp.where(t > 0, uh_ref[...], 0.0)
        _, conv, _, _, i_, _, a, _, mult = _rglru_gates(u_ref[...], head8, grow, row, T, cw_ref, cb_ref, wa_ref, ba_ref,
                                                         wx_ref, bx_ref, lam_ref)
        b = mult * i_ * conv
        s = 1
        while s < T:
            keep = row >= s
            a_s = jnp.where(keep, pltpu.roll(a, s, 0), 1.0)
            b_s = jnp.where(keep, pltpu.roll(b, s, 0), 0.0)
            b = a * b_s + b
            a = a * a_s
            s *= 2
        h = b + a * carry[0:1, :]
        carry[...] = jnp.broadcast_to(h[T - 1:T, :], carry.shape)
        h_ref[...] = h
        gv = g_ref[...]
        y_ref[...] = (h * (gv * _sigmoid(gv))).astype(BF)

    tile, halo, vec, cwspec, wblk = _rglru_specs(T, nt, False)
    return pl.pallas_call(
        body, name="rglru_fwd", grid=(RNN_BLOCKS, nt),
        in_specs=[tile, halo, tile, cwspec, vec, wblk, vec, wblk, vec, vec],
        out_specs=[tile, tile],
        out_shape=[jax.ShapeDtypeStruct((S, D_RNN), F32), jax.ShapeDtypeStruct((S, D_RNN), BF)],
        scratch_shapes=[pltpu.VMEM((8, RNN_BLOCK), F32)],
        compiler_params=_params(("parallel", "arbitrary")),
    )(xr, xr, g, cw, cb, wa, ba, wx, bx, lam)


def _rglru_bwd(xr, g, h, dy, cw, cb, wa, ba, wx, bx, lam, T):
    S = xr.shape[0]
    nt = S // T

    def body(u_ref, uh_ref, g_ref, h_ref, hh_ref, dy_ref, cw_ref, cb_ref, wa_ref, ba_ref, wx_ref, bx_ref, lam_ref,
             du_ref, dg_ref, dwa_ref, dwx_ref, dvec_ref, c_dhh, c_a, c_dconv):
        t = pl.program_id(1)
        tt = nt - 1 - t

        @pl.when(t == 0)
        def _():
            c_dhh[...] = jnp.zeros_like(c_dhh)
            c_a[...] = jnp.zeros_like(c_a)
            c_dconv[...] = jnp.zeros_like(c_dconv)
            dwa_ref[...] = jnp.zeros_like(dwa_ref)
            dwx_ref[...] = jnp.zeros_like(dwx_ref)
            dvec_ref[...] = jnp.zeros_like(dvec_ref)

        row = lax.broadcasted_iota(jnp.int32, (T, RNN_BLOCK), 0)
        row8 = row[:8]
        grow = row + tt * T
        head8 = jnp.where(tt > 0, uh_ref[...], 0.0)
        us, conv, cbf, r_, i_, sp, a, mult_raw, mult = _rglru_gates(
            u_ref[...], head8, grow, row, T, cw_ref, cb_ref, wa_ref, ba_ref, wx_ref, bx_ref, lam_ref)
        hv = h_ref[...]
        hprev = _shift_down(hv, 1, jnp.where(tt > 0, hh_ref[...], 0.0), row, T)
        gv = g_ref[...]
        sg = _sigmoid(gv)
        dyv = dy_ref[...]
        dg_ref[...] = (dyv * hv * (sg * (1.0 + gv * (1.0 - sg)))).astype(BF)
        d = dyv * (gv * sg)
        A = _shift_up(a, 1, c_a[...], row, T)
        s = 1
        while s < T:
            keep = row < T - s
            A_s = jnp.where(keep, pltpu.roll(A, T - s, 0), 1.0)
            d_s = jnp.where(keep, pltpu.roll(d, T - s, 0), 0.0)
            d = A * d_s + d
            A = A * A_s
            s *= 2
        dhh = d + A * c_dhh[0:1, :]
        da = dhh * hprev
        dconv = dhh * mult * i_
        di = dhh * mult * conv
        dmult = dhh * i_ * conv
        dla = da * a - jnp.where(grow == 0, 0.0, dmult * (a * a) / mult_raw)
        dr = dla * (-LRU_C * sp)
        dsp = jnp.sum(dla * (-LRU_C * r_), axis=0, keepdims=True)
        dza = dr * r_ * (1.0 - r_)
        dzx = di * i_ * (1.0 - i_)
        dza_b, dzx_b = dza.astype(BF), dzx.astype(BF)
        dconv = dconv + _dot(dza_b, wa_ref[0], NT) + _dot(dzx_b, wx_ref[0], NT)
        dwa_ref[0] += _dot(cbf, dza_b, TN)
        dwx_ref[0] += _dot(cbf, dzx_b, TN)
        lam = lam_ref[...]
        rows = [jnp.sum(dconv * us[k], axis=0, keepdims=True) for k in range(CONV_W)]
        rows += [jnp.sum(dconv, axis=0, keepdims=True), jnp.sum(dza, axis=0, keepdims=True),
                 jnp.sum(dzx, axis=0, keepdims=True), dsp * (-_sigmoid(-lam))]
        upd = jnp.zeros((8, RNN_BLOCK), F32)
        for j, rv in enumerate(rows):
            upd = upd + jnp.where(row8 == j, rv, 0.0)
        dvec_ref[...] += upd
        tail8 = c_dconv[...]
        du = dconv * cw_ref[0:1, :]
        for k in range(1, CONV_W):
            du = du + _shift_up(dconv, k, tail8, row, T) * cw_ref[k:k + 1, :]
        du_ref[...] = du.astype(BF)
        c_dhh[...] = jnp.broadcast_to(dhh[0:1, :], c_dhh.shape)
        c_a[...] = jnp.broadcast_to(a[0:1, :], c_a.shape)
        c_dconv[...] = dconv[:8]

    tile, halo, vec, cwspec, wblk = _rglru_specs(T, nt, True)
    acc8 = pl.BlockSpec((8, RNN_BLOCK), lambda n, t: (0, n))
    return pl.pallas_call(
        body, name="rglru_bwd", grid=(RNN_BLOCKS, nt),
        in_specs=[tile, halo, tile, tile, halo, tile, cwspec, vec, wblk, vec, wblk, vec, vec],
        out_specs=[tile, tile, wblk, wblk, acc8],
        out_shape=[jax.ShapeDtypeStruct((S, D_RNN), BF), jax.ShapeDtypeStruct((S, D_RNN), BF),
                   jax.ShapeDtypeStruct((RNN_BLOCKS, RNN_BLOCK, RNN_BLOCK), F32),
                   jax.ShapeDtypeStruct((RNN_BLOCKS, RNN_BLOCK, RNN_BLOCK), F32),
                   jax.ShapeDtypeStruct((8, D_RNN), F32)],
        scratch_shapes=[pltpu.VMEM((8, RNN_BLOCK), F32)] * 3,
        compiler_params=_params(("parallel", "arbitrary")),
    )(xr, xr, g, h, h, dy, cw, cb, wa, ba, wx, bx, lam)


def _rel_bucket_map():
    qi = np.arange(WINDOW)[:, None]
    kj = np.arange(2 * WINDOW)[None, :]
    dist = jnp.asarray(qi + WINDOW - kj, jnp.int32)
    n = jnp.maximum(dist, 0)
    max_exact = REL_BUCKETS // 2
    ratio = jnp.log(jnp.maximum(n, 1).astype(F32) / max_exact) / math.log(REL_MAX_DIST / max_exact)
    large = jnp.minimum(max_exact + (ratio * (REL_BUCKETS - max_exact)).astype(jnp.int32), REL_BUCKETS - 1)
    bucket = jnp.where(n < max_exact, n, large).astype(jnp.int32)
    j = np.arange(WINDOW)[None, :]
    return jnp.where(jnp.asarray(j > qi), bucket[:, :WINDOW], bucket[:, WINDOW:])


def _swa_common(n, kv_ref, bucket_ref, relb_ref, bias_scr):
    @pl.when(n == 0)
    def _():
        bk = bucket_ref[...]
        for h in range(SWA_HEADS):
            acc = jnp.zeros((WINDOW, WINDOW), F32)
            for b in range(REL_BUCKETS):
                acc = acc + jnp.where(bk == b, relb_ref[b, h], 0.0)
            bias_scr[h] = acc

    prev0 = pl.multiple_of(jnp.maximum(n - 1, 0) * WINDOW, WINDOW)
    cur0 = pl.multiple_of(n * WINDOW, WINDOW)
    kk = jnp.concatenate([kv_ref[pl.ds(prev0, WINDOW), :], kv_ref[pl.ds(cur0, WINDOW), :]], axis=0).astype(F32)
    rowi = lax.broadcasted_iota(jnp.int32, (WINDOW, WINDOW), 0)
    col = lax.broadcasted_iota(jnp.int32, (WINDOW, WINDOW), 1)
    from_prev = col > rowi
    return kk, from_prev, prev0, cur0


def _fold(full, from_prev):
    return jnp.where(from_prev, full[:, :WINDOW], full[:, WINDOW:])


def _unfold(sq, from_prev):
    return jnp.concatenate([jnp.where(from_prev, sq, 0.0), jnp.where(from_prev, 0.0, sq)], axis=1)


def _half_pair(part, kvh):
    lo = lax.broadcasted_iota(jnp.int32, part.shape, 1) < SWA_HD
    if kvh == 0:
        pa = jnp.where(lo, part, 0.0)
        pb = pltpu.roll(pa, SWA_HD, 1)
    else:
        pb = jnp.where(lo, 0.0, part)
        pa = pltpu.roll(pb, SWA_HD, 1)
    return pa.astype(BF), pb.astype(BF)


ALL_HEADS = SWA_HEADS * WINDOW


def _sink_column(sinks):
    return jnp.repeat(sinks.reshape(SWA_HEADS), WINDOW).reshape(ALL_HEADS, 1)


def _swa_operands(kk):
    return [(_half_pair(kk[:, :128], kvh), _half_pair(kk[:, 128:], kvh)) for kvh in range(SWA_KV_HEADS)]


def _swa_probs(n, q_ref, ops, bias_scr, sinkc_ref, from_prev):
    lgs = []
    for kvh in range(SWA_KV_HEADS):
        (ka, kb), _ = ops[kvh]
        for p in range(4):
            q2 = q_ref[:, kvh * 512 + p * 128:kvh * 512 + p * 128 + 128]
            lgs += [_fold(_dot(q2, ka, NT), from_prev), _fold(_dot(q2, kb, NT), from_prev)]
    lg = jnp.concatenate(lgs, axis=0) * (SWA_HD ** -0.5) + bias_scr[...].reshape(ALL_HEADS, WINDOW)
    rowi = jnp.bitwise_and(lax.broadcasted_iota(jnp.int32, (ALL_HEADS, WINDOW), 0), WINDOW - 1)
    col = lax.broadcasted_iota(jnp.int32, (ALL_HEADS, WINDOW), 1)
    no_prev = jnp.where(n > 0, 0, 4 * WINDOW)
    lg = jnp.where(jnp.logical_or(col <= rowi, col > rowi + no_prev), lg, NEG_INF)
    sink = sinkc_ref[...]
    m = jnp.maximum(jnp.max(lg, axis=-1, keepdims=True), sink)
    e = jnp.exp(lg - m)
    es = jnp.exp(sink - m)
    den = jnp.sum(e, axis=-1, keepdims=True) + es
    return e / den, es / den


def _swa_fwd(q, kv, g, bucket, rel_bias, sink_col):
    S = q.shape[0]
    nb = S // WINDOW

    def body(q_ref, kv_ref, g_ref, bucket_ref, relb_ref, sinkc_ref, o_ref, y_ref, bias_scr):
        n = pl.program_id(0)
        kk, from_prev, _, _ = _swa_common(n, kv_ref, bucket_ref, relb_ref, bias_scr)
        ops = _swa_operands(kk)
        pr, _ = _swa_probs(n, q_ref, ops, bias_scr, sinkc_ref, from_prev)
        for kvh in range(SWA_KV_HEADS):
            _, (va, vb) = ops[kvh]
            for p in range(4):
                c0 = kvh * 512 + p * 128
                r0 = (kvh * 8 + 2 * p) * WINDOW
                o2 = (_dot(_unfold(pr[r0:r0 + WINDOW], from_prev).astype(BF), va, NN)
                      + _dot(_unfold(pr[r0 + WINDOW:r0 + 2 * WINDOW], from_prev).astype(BF), vb, NN))
                o_ref[:, c0:c0 + 128] = o2
                gv = g_ref[:, c0:c0 + 128]
                y_ref[:, c0:c0 + 128] = (o2 * (gv * _sigmoid(gv))).astype(BF)

    blk = pl.BlockSpec((WINDOW, 1024), lambda n: (n, 0))
    smem = pl.BlockSpec(memory_space=pltpu.SMEM)
    sinkc = pl.BlockSpec((ALL_HEADS, 1), lambda n: (0, 0))
    return pl.pallas_call(
        body, name="swa_fwd", grid=(nb,),
        in_specs=[blk, pl.BlockSpec((S, 256), lambda n: (0, 0)), blk, pl.BlockSpec((WINDOW, WINDOW), lambda n: (0, 0)), smem, sinkc],
        out_specs=[blk, blk],
        out_shape=[jax.ShapeDtypeStruct((S, 1024), F32), jax.ShapeDtypeStruct((S, 1024), BF)],
        scratch_shapes=[pltpu.VMEM((SWA_HEADS, WINDOW, WINDOW), F32)],
        compiler_params=_params(("arbitrary",)),
    )(q, kv, g, bucket, rel_bias, sink_col)


def _swa_bwd(q, kv, g, o, dy, bucket, rel_bias, sink_col):
    S = q.shape[0]
    nb = S // WINDOW

    def body(q_ref, kv_ref, g_ref, o_ref, dy_ref, bucket_ref, relb_ref, sinkc_ref,
             dq_ref, dg_ref, dkv_ref, dsink_ref, drel_ref, bias_scr, dbias_scr, dsink_scr):
        n = pl.program_id(0)

        @pl.when(n == 0)
        def _():
            dbias_scr[...] = jnp.zeros_like(dbias_scr)
            dsink_scr[...] = jnp.zeros_like(dsink_scr)
            dkv_ref[...] = jnp.zeros_like(dkv_ref)

        kk, from_prev, prev0, cur0 = _swa_common(n, kv_ref, bucket_ref, relb_ref, bias_scr)
        ops = _swa_operands(kk)
        pr, ps = _swa_probs(n, q_ref, ops, bias_scr, sinkc_ref, from_prev)
        do2s, dps = [], []
        for kvh in range(SWA_KV_HEADS):
            _, (va, vb) = ops[kvh]
            for p in range(4):
                c0 = kvh * 512 + p * 128
                gv = g_ref[:, c0:c0 + 128]
                sg = _sigmoid(gv)
                dyv = dy_ref[:, c0:c0 + 128]
                dg_ref[:, c0:c0 + 128] = (dyv * o_ref[:, c0:c0 + 128] * (sg * (1.0 + gv * (1.0 - sg)))).astype(BF)
                do2 = (dyv * (gv * sg)).astype(BF)
                do2s.append(do2)
                dps += [_fold(_dot(do2, va, NT), from_prev), _fold(_dot(do2, vb, NT), from_prev)]
        dp = jnp.concatenate(dps, axis=0)
        delta = jnp.sum(pr * dp, axis=-1, keepdims=True)
        ds = pr * (dp - delta)
        dbias_scr[...] += ds.reshape(SWA_HEADS, WINDOW, WINDOW)
        dsink_scr[...] += ps * delta
        dsc = ds * (SWA_HD ** -0.5)
        lo256 = lax.broadcasted_iota(jnp.int32, (2 * WINDOW, 128), 1) < SWA_HD
        dks, dvs = [], []
        for kvh in range(SWA_KV_HEADS):
            (ka, kb), _ = ops[kvh]
            dka = jnp.zeros((2 * WINDOW, 128), F32)
            dkb, dva, dvb = dka, dka, dka
            for p in range(4):
                c0 = kvh * 512 + p * 128
                r0 = (kvh * 8 + 2 * p) * WINDOW
                q2 = q_ref[:, c0:c0 + 128]
                do2 = do2s[kvh * 4 + p]
                ds0 = _unfold(dsc[r0:r0 + WINDOW], from_prev).astype(BF)
                ds1 = _unfold(dsc[r0 + WINDOW:r0 + 2 * WINDOW], from_prev).astype(BF)
                dq_ref[:, c0:c0 + 128] = (_dot(ds0, ka, NN) + _dot(ds1, kb, NN)).astype(BF)
                dka = dka + _dot(ds0, q2, TN)
                dkb = dkb + _dot(ds1, q2, TN)
                dva = dva + _dot(_unfold(pr[r0:r0 + WINDOW], from_prev).astype(BF), do2, TN)
                dvb = dvb + _dot(_unfold(pr[r0 + WINDOW:r0 + 2 * WINDOW], from_prev).astype(BF), do2, TN)
            dks.append(jnp.where(lo256, dka, 0.0) + pltpu.roll(jnp.where(lo256, 0.0, dkb), SWA_HD, 1))
            dvs.append(jnp.where(lo256, dva, 0.0) + pltpu.roll(jnp.where(lo256, 0.0, dvb), SWA_HD, 1))
        dk = dks[0] + pltpu.roll(dks[1], SWA_HD, 1)
        dv = dvs[0] + pltpu.roll(dvs[1], SWA_HD, 1)
        dkv_ref[pl.ds(prev0, WINDOW), 0:128] += dk[:WINDOW]
        dkv_ref[pl.ds(prev0, WINDOW), 128:256] += dv[:WINDOW]
        dkv_ref[pl.ds(cur0, WINDOW), 0:128] += dk[WINDOW:]
        dkv_ref[pl.ds(cur0, WINDOW), 128:256] += dv[WINDOW:]

        @pl.when(n == nb - 1)
        def _():
            dsink_ref[...] = -jnp.sum(dsink_scr[...].reshape(SWA_HEADS, WINDOW, 1), axis=1)
            bk = bucket_ref[...]
            sums = []
            for b in range(REL_BUCKETS):
                sums.append(jnp.sum(jnp.where((bk == b)[None], dbias_scr[...], 0.0), axis=1))
            drel_ref[...] = jnp.sum(jnp.concatenate(sums, axis=0), axis=1, keepdims=True)

    blk = pl.BlockSpec((WINDOW, 1024), lambda n: (n, 0))
    smem = pl.BlockSpec(memory_space=pltpu.SMEM)
    whole = lambda shape: pl.BlockSpec(shape, lambda n: (0, 0))
    return pl.pallas_call(
        body, name="swa_bwd", grid=(nb,),
        in_specs=[blk, whole((S, 256)), blk, blk, blk, whole((WINDOW, WINDOW)), smem, whole((ALL_HEADS, 1))],
        out_specs=[blk, blk, whole((S, 256)), whole((SWA_HEADS, 1)), whole((REL_BUCKETS * SWA_HEADS, 1))],
        out_shape=[jax.ShapeDtypeStruct((S, 1024), BF), jax.ShapeDtypeStruct((S, 1024), BF),
                   jax.ShapeDtypeStruct((S, 256), F32), jax.ShapeDtypeStruct((SWA_HEADS, 1), F32),
                   jax.ShapeDtypeStruct((REL_BUCKETS * SWA_HEADS, 1), F32)],
        scratch_shapes=[pltpu.VMEM((SWA_HEADS, WINDOW, WINDOW), F32), pltpu.VMEM((SWA_HEADS, WINDOW, WINDOW), F32),
                        pltpu.VMEM((ALL_HEADS, 1), F32)],
        compiler_params=_params(("arbitrary",)),
    )(q, kv, g, o, dy, bucket, rel_bias, sink_col)


def _mem_probs(qh, mk):
    lg = _dot(qh, mk, NT) * (MEM_HD ** -0.5)
    e = jnp.exp(lg - jnp.max(lg, axis=-1, keepdims=True))
    return e / jnp.sum(e, axis=-1, keepdims=True)


def _mem_fwd(q, mkv, g):
    S = q.shape[0]
    M = mkv.shape[0]
    tq = 256

    def body(q_ref, mkv_ref, g_ref, o_ref, y_ref):
        for h in range(MEM_HEADS):
            c0 = h * MEM_HD
            pr = _mem_probs(q_ref[:, c0:c0 + MEM_HD], mkv_ref[:, c0:c0 + MEM_HD])
            o = _dot(pr.astype(BF), mkv_ref[:, D_MEM + c0:D_MEM + c0 + MEM_HD], NN)
            o_ref[:, c0:c0 + MEM_HD] = o
            gv = g_ref[:, c0:c0 + MEM_HD]
            y_ref[:, c0:c0 + MEM_HD] = (o * (gv * _sigmoid(gv))).astype(BF)

    blk = pl.BlockSpec((tq, D_MEM), lambda i: (i, 0))
    return pl.pallas_call(
        body, name="mem_fwd", grid=(S // tq,),
        in_specs=[blk, pl.BlockSpec((M, 2 * D_MEM), lambda i: (0, 0)), blk], out_specs=[blk, blk],
        out_shape=[jax.ShapeDtypeStruct((S, D_MEM), F32), jax.ShapeDtypeStruct((S, D_MEM), BF)],
        compiler_params=_params(("parallel",)),
    )(q, mkv, g)


def _mem_bwd(q, mkv, g, o, dy):
    S = q.shape[0]
    M = mkv.shape[0]
    tq = 256

    def body(q_ref, mkv_ref, g_ref, o_ref, dy_ref, dq_ref, dg_ref, dmkv_ref):
        @pl.when(pl.program_id(0) == 0)
        def _():
            dmkv_ref[...] = jnp.zeros_like(dmkv_ref)

        for h in range(MEM_HEADS):
            c0 = h * MEM_HD
            qh = q_ref[:, c0:c0 + MEM_HD]
            mk = mkv_ref[:, c0:c0 + MEM_HD]
            mv = mkv_ref[:, D_MEM + c0:D_MEM + c0 + MEM_HD]
            gv = g_ref[:, c0:c0 + MEM_HD]
            sg = _sigmoid(gv)
            dyv = dy_ref[:, c0:c0 + MEM_HD]
            dg_ref[:, c0:c0 + MEM_HD] = (dyv * o_ref[:, c0:c0 + MEM_HD] * (sg * (1.0 + gv * (1.0 - sg)))).astype(BF)
            do = (dyv * (gv * sg)).astype(BF)
            pr = _mem_probs(qh, mk)
            dp = _dot(do, mv, NT)
            ds = pr * (dp - jnp.sum(pr * dp, axis=-1, keepdims=True))
            dsb = (ds * (MEM_HD ** -0.5)).astype(BF)
            dq_ref[:, c0:c0 + MEM_HD] = _dot(dsb, mk, NN).astype(BF)
            dmkv_ref[:, c0:c0 + MEM_HD] += _dot(dsb, qh, TN)
            dmkv_ref[:, D_MEM + c0:D_MEM + c0 + MEM_HD] += _dot(pr.astype(BF), do, TN)

    blk = pl.BlockSpec((tq, D_MEM), lambda i: (i, 0))
    whole = pl.BlockSpec((M, 2 * D_MEM), lambda i: (0, 0))
    return pl.pallas_call(
        body, name="mem_bwd", grid=(S // tq,),
        in_specs=[blk, whole, blk, blk, blk], out_specs=[blk, blk, whole],
        out_shape=[jax.ShapeDtypeStruct((S, D_MEM), BF), jax.ShapeDtypeStruct((S, D_MEM), BF),
                   jax.ShapeDtypeStruct((M, 2 * D_MEM), F32)],
        compiler_params=_params(("arbitrary",)),
    )(q, mkv, g, o, dy)


MERGE_TN = 512


def _merge_specs(tm):
    ytile = pl.BlockSpec((tm, 1024), lambda i, j: (i, 0))
    wblk = pl.BlockSpec((MERGE_TN, 1024), lambda i, j: (j, 0))
    gls = [pl.BlockSpec((None, tm, MERGE_TN), (lambda i, j, br=br: (br, i, j))) for br in range(3)]
    otile = pl.BlockSpec((tm, MERGE_TN), lambda i, j: (i, j))
    return ytile, wblk, gls, otile


def _merge_fwd(ys, ws, gl, tm):
    S = gl.shape[1]

    def body(y0, y1, y2, w0, w1, w2, g0, g1, g2, o_ref):
        acc = None
        for y_ref, w_ref, g_ref in ((y0, w0, g0), (y1, w1, g1), (y2, w2, g2)):
            term = _sigmoid(g_ref[...]) * _dot(y_ref[...], w_ref[...], NT)
            acc = term if acc is None else acc + term
        o_ref[...] = acc.astype(BF)

    ytile, wblk, gls, otile = _merge_specs(tm)
    return pl.pallas_call(
        body, name="merge_fwd", grid=(S // tm, D_MODEL // MERGE_TN),
        in_specs=[ytile] * 3 + [wblk] * 3 + gls, out_specs=otile,
        out_shape=jax.ShapeDtypeStruct((S, D_MODEL), BF),
        compiler_params=_params(("parallel", "arbitrary")),
    )(*ys, *ws, gl, gl, gl)


def _merge_bwd(dout, w_out, ys, ws, gl, tm):
    S = gl.shape[1]

    def body(do_ref, wo_ref, y0, y1, y2, w0, w1, w2, g0, g1, g2, dg0, dg1, dg2, dp0, dp1, dp2):
        dm = _dot(do_ref[...], wo_ref[...], NT)
        for y_ref, w_ref, g_ref, dg_ref, dp_ref in ((y0, w0, g0, dg0, dp0), (y1, w1, g1, dg1, dp1), (y2, w2, g2, dg2, dp2)):
            gate = _sigmoid(g_ref[...])
            pv = _dot(y_ref[...], w_ref[...], NT)
            dg_ref[...] = (dm * pv * gate * (1.0 - gate)).astype(BF)
            dp_ref[...] = (dm * gate).astype(BF)

    ytile, wblk, gls, otile = _merge_specs(tm)
    out = jax.ShapeDtypeStruct((S, D_MODEL), BF)
    return pl.pallas_call(
        body, name="merge_bwd", grid=(S // tm, D_MODEL // MERGE_TN),
        in_specs=[pl.BlockSpec((tm, D_MODEL), lambda i, j: (i, 0)), pl.BlockSpec((MERGE_TN, D_MODEL), lambda i, j: (j, 0))]
        + [ytile] * 3 + [wblk] * 3 + gls,
        out_specs=[otile] * 6, out_shape=[out] * 6,
        compiler_params=_params(("parallel", "arbitrary")),
    )(dout, w_out, *ys, *ws, gl, gl, gl)


def _out_loss(merged, w_out, x, target, post_g, tm):
    S = x.shape[0]

    def body(m_ref, w_ref, x_ref, t_ref, g_ref, dout_ref, dy_ref, loss_ref, dpost_ref):
        @pl.when(pl.program_id(0) == 0)
        def _():
            loss_ref[...] = jnp.zeros_like(loss_ref)
            dpost_ref[...] = jnp.zeros_like(dpost_ref)

        out = _dot(m_ref[...], w_ref[...], NN)
        r = lax.rsqrt(jnp.mean(out * out, axis=-1, keepdims=True) + EPS)
        nrm = out * r
        gv = g_ref[...]
        err = (x_ref[...] + nrm * gv) - t_ref[...]
        sq = jnp.sum(jnp.sum(err * err, axis=1, keepdims=True), axis=0, keepdims=True)
        loss_ref[...] += sq * (0.5 / D_MODEL)
        dy = err * (1.0 / D_MODEL)
        dy_ref[...] = dy
        dpost_ref[...] += jnp.sum(dy * nrm, axis=0, keepdims=True)
        dn = dy * gv
        dout_ref[...] = (r * (dn - nrm * jnp.mean(dn * nrm, axis=-1, keepdims=True))).astype(BF)

    row = pl.BlockSpec((tm, D_MODEL), lambda i: (i, 0))
    return pl.pallas_call(
        body, name="out_loss", grid=(S // tm,),
        in_specs=[row, pl.BlockSpec((D_MODEL, D_MODEL), lambda i: (0, 0)), row, row, pl.BlockSpec((1, D_MODEL), lambda i: (0, 0))],
        out_specs=[row, row, pl.BlockSpec((8, 128), lambda i: (0, 0)), pl.BlockSpec((1, D_MODEL), lambda i: (0, 0))],
        out_shape=[jax.ShapeDtypeStruct((S, D_MODEL), BF), jax.ShapeDtypeStruct((S, D_MODEL), F32),
                   jax.ShapeDtypeStruct((8, 128), F32), jax.ShapeDtypeStruct((1, D_MODEL), F32)],
        compiler_params=_params(("arbitrary",)),
    )(merged, w_out, x, target, post_g)


DH_DX_CHUNK = 64


def _dh_dx(dproj, w_in, x, dy, pre_g, tm, tk):
    S = x.shape[0]
    nk = D_IN // tk

    def body(dp_ref, w_ref, x_ref, dy_ref, g_ref, dx_ref, dpre_ref, acc_ref):
        i, k = pl.program_id(0), pl.program_id(1)

        @pl.when(jnp.logical_and(i == 0, k == 0))
        def _():
            dpre_ref[...] = jnp.zeros_like(dpre_ref)

        @pl.when(k == 0)
        def _():
            acc_ref[...] = jnp.zeros_like(acc_ref)

        acc_ref[...] += _dot(dp_ref[...], w_ref[...], NN)

        @pl.when(k == nk - 1)
        def _():
            def chunk(c, carry):
                rows = pl.ds(pl.multiple_of(c * DH_DX_CHUNK, DH_DX_CHUNK), DH_DX_CHUNK)
                dh = acc_ref[rows, :]
                xv = x_ref[rows, :]
                r = lax.rsqrt(jnp.mean(xv * xv, axis=-1, keepdims=True) + EPS)
                nrm = xv * r
                dpre_ref[...] += jnp.sum(dh * nrm, axis=0, keepdims=True)
                dn = dh * g_ref[...]
                dx_ref[rows, :] = r * (dn - nrm * jnp.mean(dn * nrm, axis=-1, keepdims=True)) + dy_ref[rows, :]
                return carry
            lax.fori_loop(0, tm // DH_DX_CHUNK, chunk, 0)

    row = pl.BlockSpec((tm, D_MODEL), lambda i, k: (i, 0))
    vec = pl.BlockSpec((1, D_MODEL), lambda i, k: (0, 0))
    return pl.pallas_call(
        body, name="dh_dx", grid=(S // tm, nk),
        in_specs=[pl.BlockSpec((tm, tk), lambda i, k: (i, k)), pl.BlockSpec((tk, D_MODEL), lambda i, k: (k, 0)), row, row, vec],
        out_specs=[row, vec],
        out_shape=[jax.ShapeDtypeStruct((S, D_MODEL), F32), jax.ShapeDtypeStruct((1, D_MODEL), F32)],
        scratch_shapes=[pltpu.VMEM((tm, D_MODEL), F32)],
        compiler_params=_params(("arbitrary", "arbitrary"), large=True),
    )(dproj, w_in, x, dy, pre_g)


def _sum_parts(parts, name):
    P, R, C = parts.shape
    tr = max(t for t in range(8, 513, 8) if R % t == 0)

    def body(p_ref, o_ref):
        acc = p_ref[0]
        for j in range(1, P):
            acc = acc + p_ref[j]
        o_ref[...] = acc

    return pl.pallas_call(
        body, name=name, grid=(R // tr,),
        in_specs=[pl.BlockSpec((P, tr, C), lambda i: (0, i, 0))], out_specs=pl.BlockSpec((tr, C), lambda i: (i, 0)),
        out_shape=jax.ShapeDtypeStruct((R, C), F32), compiler_params=_params(("parallel",)),
    )(parts)


def _adamw(land, sums, chip, w, m, v, name, group=(0, 1), into=None):
    q, n_groups = group
    _, R, cols = land.shape
    C = cols * n_groups
    tr = max(t for t in range(16, 257, 16) if R % t == 0)
    c1 = 1.0 - ADAM_B1 ** ADAM_STEP
    c2 = 1.0 - ADAM_B2 ** ADAM_STEP
    n_into = 0 if into is None else 4

    def body(chip_ref, p0_ref, p1_ref, p2_ref, own_ref, w_ref, m_ref, v_ref, *refs):
        g_ref, d_ref, nm_ref, nv_ref = refs[n_into:]
        g = own_ref[...].astype(F32)
        for p_ref in (p0_ref, p1_ref, p2_ref):
            g = g + p_ref[...].astype(F32)
        nm = ADAM_B1 * m_ref[...] + (1.0 - ADAM_B1) * g
        nv = ADAM_B2 * v_ref[...] + (1.0 - ADAM_B2) * (g * g)
        g_ref[...] = g
        nm_ref[...] = nm
        nv_ref[...] = nv
        d_ref[...] = -ADAM_LR * ((nm / c1) / (jnp.sqrt(nv / c2) + ADAM_EPS) + ADAM_WD * w_ref[...])

    tile = pl.BlockSpec((None, tr, cols), lambda i, c_ref: (0, i, q))
    specs = [pl.BlockSpec((None, tr, cols), (lambda i, c_ref, k=k: (k + (c_ref[0] <= k).astype(jnp.int32), i, 0))) for k in range(3)]
    specs.append(pl.BlockSpec((None, tr, cols), (lambda i, c_ref: (c_ref[0], i, 0))))
    return pl.pallas_call(
        body, name=name,
        grid_spec=pltpu.PrefetchScalarGridSpec(num_scalar_prefetch=1, grid=(R // tr,),
                                               in_specs=specs + [tile, tile, tile] + [pl.BlockSpec(memory_space=pl.ANY)] * n_into,
                                               out_specs=[tile] * 4),
        out_shape=[jax.ShapeDtypeStruct((1, R, C), F32)] * 4,
        input_output_aliases={8 + k: k for k in range(n_into)},
        compiler_params=_params(("parallel",)),
    )(chip, land, land, land, sums, w, m, v, *(into or []))


def _adamw_small(gs, ws, ms, vs):
    n = len(ws)
    c1 = 1.0 - ADAM_B1 ** ADAM_STEP
    c2 = 1.0 - ADAM_B2 ** ADAM_STEP

    def flat2(a):
        return a.reshape(-1, a.shape[-1])

    def body(*refs):
        ins, outs = refs[:4 * n], refs[4 * n:]
        for a in range(n):
            g, w, m, v = (ins[k * n + a][...] for k in range(4))
            nm = ADAM_B1 * m + (1.0 - ADAM_B1) * g
            nv = ADAM_B2 * v + (1.0 - ADAM_B2) * (g * g)
            outs[a][...] = g
            outs[n + a][...] = -ADAM_LR * ((nm / c1) / (jnp.sqrt(nv / c2) + ADAM_EPS) + ADAM_WD * w)
            outs[2 * n + a][...] = nm
            outs[3 * n + a][...] = nv

    shapes = [flat2(w).shape for w in ws]
    out = pl.pallas_call(
        body, name="adamw_small", out_shape=[jax.ShapeDtypeStruct(sh, F32) for sh in shapes] * 4,
        compiler_params=_params(),
    )(*[g.reshape(sh) for g, sh in zip(gs, shapes)], *[flat2(a) for a in (*ws, *ms, *vs)])
    return [[out[k * n + a].reshape(ws[a].shape) for a in range(n)] for k in range(4)]


PROJ_ROWS = 512


def _project(h, w_t):
    S = h.shape[0]
    n_tiles = D_IN // SEG_TILE
    ranges = [(c0 // SEG_TILE, (c0 + width) // SEG_TILE) for _, c0, width, _ in SEGMENTS]

    def body(h_ref, w_ref, *outs):
        j = pl.program_id(0)
        for (j0, j1), (_, _, _, dt), o_ref in zip(ranges, SEGMENTS, outs):
            @pl.when(jnp.logical_and(j >= j0, j < j1))
            def _(o_ref=o_ref, dt=dt):
                for c in range(S // PROJ_ROWS):
                    rows = pl.ds(c * PROJ_ROWS, PROJ_ROWS)
                    o_ref[rows, :] = _dot(h_ref[rows, :], w_ref[...], NT).astype(dt)

    out_shapes, out_specs = [], []
    for (j0, j1), (name, _, width, dt) in zip(ranges, SEGMENTS):
        if name == "gl":
            per = (j1 - j0) // 3
            out_shapes.append(jax.ShapeDtypeStruct((3, S, width // 3), dt))
            out_specs.append(pl.BlockSpec((None, S, SEG_TILE), (lambda j, j0=j0, j1=j1, per=per: (
                jnp.clip(j - j0, 0, j1 - j0 - 1) // per, 0, jnp.clip(j - j0, 0, j1 - j0 - 1) % per))))
        else:
            out_shapes.append(jax.ShapeDtypeStruct((S, width), dt))
            out_specs.append(pl.BlockSpec((S, SEG_TILE), (lambda j, j0=j0, j1=j1: (0, jnp.clip(j - j0, 0, j1 - j0 - 1)))))
    outs = pl.pallas_call(
        body, name="proj", grid=(n_tiles,),
        in_specs=[pl.BlockSpec((S, D_MODEL), lambda j: (0, 0)), pl.BlockSpec((SEG_TILE, D_MODEL), lambda j: (j, 0))],
        out_specs=out_specs, out_shape=out_shapes,
        compiler_params=_params(("arbitrary",), large=True),
    )(h, w_t)
    return {name: o for (name, _, _, _), o in zip(SEGMENTS, outs)}


def _forward_a(x, mem, pre_g, mem_g, w_in, conv_w, conv_b, w_a, b_a, w_x, b_x, lam, sinks, rel_bias):
    S = x.shape[0]
    st = dict(T=min(512, S // 2), tm=min(512, S), bucket=_rel_bucket_map())
    st["h"] = _rms_fwd(x, pre_g, "pre_norm")
    st["memn"] = _rms_fwd(mem, mem_g, "mem_norm")
    seg = st["seg"] = _project(st["h"], w_in)
    st["h_rg"], st["y_rg"] = _rglru_fwd(seg["xr"], seg["g_rg"], conv_w, conv_b, w_a, b_a, w_x, b_x, lam, st["T"])
    st["o_swa"], st["y_swa"] = _swa_fwd(seg["q_s"], seg["kv"], seg["g_swa"], st["bucket"], rel_bias, _sink_column(sinks))
    return st


def _forward_b(st, x, target, post_g, w_memkv, wbr, w_out):
    S = x.shape[0]
    M = st["memn"].shape[0]
    seg = st["seg"]
    st["mkv"] = _matmul(st["memn"], w_memkv, "nn", M, 2 * D_MEM, D_MODEL, M, 512, D_MODEL, BF, "mem_kv")
    st["o_mem"], st["y_mem"] = _mem_fwd(seg["q_m"], st["mkv"], seg["g_mem"])
    st["ys"] = (st["y_rg"], st["y_swa"], st["y_mem"])
    st["merged"] = _merge_fwd(st["ys"], wbr, seg["gl"], st["tm"])
    st["dout"], st["dy"], st["loss"], st["dpost"] = _out_loss(st["merged"], w_out, x, target, post_g, min(256, S))
    return st


def _backward_a1(st, wbr, w_out):
    S = st["h"].shape[0]
    seg, ys, tm = st["seg"], st["ys"], st["tm"]
    st["dw_out"] = _matmul(st["merged"], st["dout"], "tn", D_MODEL, D_MODEL, S, 256, D_MODEL, S, BF, "dw_out", out_blocked="row")
    dgl0, dgl1, dgl2, dp0, dp1, dp2 = _merge_bwd(st["dout"], w_out, ys, wbr, seg["gl"], tm)
    st["dgl"] = (dgl0, dgl1, dgl2)
    dys, dwbr = [], []
    for i, dp in enumerate((dp0, dp1, dp2)):
        dys.append(_matmul(dp, wbr[i], "nn", S, 1024, D_MODEL, tm, 1024, D_MODEL, F32, "dy_br%d" % i))
        dwbr.append(_matmul(ys[i], dp, "tn", 1024, D_MODEL, S, 1024, 256, S, BF, "dw_br%d" % i, out_blocked="col"))
    st["dys"], st["dwbr"] = dys, dwbr
    return st


def _backward_a2(st, mem, w_memkv, conv_w, conv_b, w_a, b_a, w_x, b_x, lam):
    M = mem.shape[0]
    seg, dys = st["seg"], st["dys"]
    st["dq_m"], st["dg_mem"], dmkv = _mem_bwd(seg["q_m"], st["mkv"], seg["g_mem"], st["o_mem"], dys[2])
    st["dmkv"] = dmkv.astype(BF)
    st["dw_memkv"] = _matmul(st["memn"], st["dmkv"], "tn", D_MODEL, 2 * D_MEM, M, 256, 2 * D_MEM, M, BF, "dw_memkv", out_blocked="row")
    st["dxr"], st["dg_rg"], st["dw_a"], st["dw_x"], st["dvec"] = _rglru_bwd(
        seg["xr"], seg["g_rg"], st["h_rg"], dys[0], conv_w, conv_b, w_a, b_a, w_x, b_x, lam, st["T"])
    return st


def _mem_gain_grad(st, mem, w_memkv, dep=None):
    M = mem.shape[0]
    dmemn = _matmul(st["dmkv"], w_memkv, "nt", M, D_MODEL, 2 * D_MEM, M, 512, 2 * D_MEM, F32, "dmemn", dep=dep)
    return _rms_gain_grad(dmemn, mem, "dmem_gain")


def _backward_b(st, rel_bias, sinks):
    seg = st["seg"]
    dq_s, dg_swa, dkv, dsinks, drel = _swa_bwd(seg["q_s"], seg["kv"], seg["g_swa"], st["o_swa"], st["dys"][1],
                                               st["bucket"], rel_bias, _sink_column(sinks))
    st["dsinks"], st["drel"] = dsinks.reshape(1, SWA_HEADS), drel.reshape(REL_BUCKETS, SWA_HEADS)
    st["dproj"] = jnp.concatenate([st["dxr"], st["dg_rg"], dq_s, dkv.astype(BF), dg_swa, st["dq_m"], st["dg_mem"], *st["dgl"]], axis=1)
    return st


def _dw_in_half(st, half, dep=None):
    S = st["h"].shape[0]
    dw = _matmul(st["dproj"], st["h"], "tn", D_IN, D_MODEL // 2, S, D_IN_TILE, D_MODEL // 2, S, BF, "dw_in%d" % half, b_noff=half, dep=dep)
    return dw.reshape(N_DEV, D_IN // N_DEV, D_MODEL // 2)


def _owner_blocks(a):
    return jnp.swapaxes(a.reshape((4, 2) + a.shape[1:]), 0, 1)


def _pad_rows(a, rows):
    a = a.reshape(-1, 128) if a.shape[-1] % 128 == 0 else jnp.pad(a, ((0, 0), (0, 128 - a.shape[-1])))
    return jnp.pad(a, ((0, rows - a.shape[0]), (0, 0))) if a.shape[0] < rows else a


def kernel(x, mem, pre_norm_g, post_norm_g, mem_norm_g, w_in, conv_w, conv_b, w_rg_a, b_rg_a, w_rg_x, b_rg_x, lru_lambda, swa_sinks, rel_bias, w_mem_kv, w_br_rg, w_br_swa, w_br_mem, w_out, loss_target, m_pre_norm_g, m_post_norm_g, m_mem_norm_g, m_w_in, m_conv_w, m_conv_b, m_w_rg_a, m_b_rg_a, m_w_rg_x, m_b_rg_x, m_lru_lambda, m_swa_sinks, m_rel_bias, m_w_mem_kv, m_w_br_rg, m_w_br_swa, m_w_br_mem, m_w_out, v_pre_norm_g, v_post_norm_g, v_mem_norm_g, v_w_in, v_conv_w, v_conv_b, v_w_rg_a, v_b_rg_a, v_w_rg_x, v_b_rg_x, v_lru_lambda, v_swa_sinks, v_rel_bias, v_w_mem_kv, v_w_br_rg, v_w_br_swa, v_w_br_mem, v_w_out):
    cx, cy, cc = lax.axis_index("x"), lax.axis_index("y"), lax.axis_index("c")
    me = 4 * cx + 2 * cy + cc
    chip = 2 * cx + cy
    core = jnp.reshape(cc, (1,)).astype(jnp.int32)
    x0, mem0 = x[0], mem[0]
    w_a_b, w_x_b = w_rg_a[0].astype(BF), w_rg_x[0].astype(BF)

    def landing(own, slot, slots):
        return lax.dynamic_update_slice(lax.empty((slots,) + own.shape, own.dtype), own[None], (slot,) + (0,) * own.ndim)


    def swap_start(parts, tag):
        return _exchange_start(parts, [lax.empty((4,) + p.shape[-2:], p.dtype) for p in parts], _plan_swap([p.ndim for p in parts]),
                               "swap_%s_start" % tag)

    def scatter_start(swap, after, tag, prefill=()):
        s_send, s_recv, parts, got, _ = swap
        got = _exchange_wait(s_send, s_recv, parts, got, _plan_swap([p.ndim for p in parts]), after, "swap_%s_wait" % tag)
        sums = [_pair_sum(p, g, core, "scatter_%s_sum%d" % (tag, i)) for i, (p, g) in enumerate(zip(parts, got))]
        lands = [landing(lax.dynamic_index_in_dim(s, chip, 0, keepdims=False), chip, 4) if i in prefill
                 else lax.empty(s.shape, s.dtype) for i, s in enumerate(sums)]
        return _exchange_start(sums, lands, _plan_scatter(len(sums)), "scatter_%s_start" % tag)

    def corner(a):
        return a.reshape(-1, a.shape[-1])[:8, :128]

    def zero_after(a):
        return jnp.minimum(jnp.abs(a.reshape(-1)[0].astype(F32)), 0.0)

    g_in, g_cw = _all_gather_relayed([jnp.transpose(w_in[0]).astype(BF), conv_w[0]], [True, False], "gather_w_in")
    w_in_f = g_in.reshape(D_IN, D_MODEL)
    conv_w_f = jnp.transpose(g_cw, (1, 0, 2)).reshape(CONV_W, D_RNN)

    after_first = zero_after(g_cw).astype(BF)
    rest = [w.astype(BF) + after_first for w in (w_mem_kv[0], jnp.transpose(w_br_rg[0]), jnp.transpose(w_br_swa[0]),
                                                 jnp.transpose(w_br_mem[0]), w_out[0])]
    plan_g = _plan_gather(len(rest))
    zones = _place_own([lax.empty((N_DEV,) + w.shape, w.dtype) for w in rest], rest, jnp.reshape(me, (1,)).astype(jnp.int32), "gather_rest_own")
    g_send, g_recv, g_src, g_land, g_token = _exchange_start(rest, zones, plan_g, "gather_rest_start")
    st = _forward_a(x0, mem0, pre_norm_g + g_token[0:1, 0:1], mem_norm_g, w_in_f, conv_w_f, conv_b, w_a_b, b_rg_a, w_x_b, b_rg_x,
                    lru_lambda, swa_sinks, rel_bias)
    g_land = _exchange_wait(g_send, g_recv, g_src, g_land, plan_g, st["y_swa"], "gather_rest_wait")
    g_land = _forward_to_sibling(g_land, "gather_rest_forward")
    w_memkv_f = g_land[0].reshape(D_MODEL, 2 * D_MEM)
    wbr = tuple(g_land[i].reshape(D_MODEL, D_RNN) for i in (1, 2, 3))
    w_out_f = g_land[4].reshape(D_MODEL, D_MODEL)

    st = _forward_b(st, x0, loss_target[0], post_norm_g, w_memkv_f, wbr, w_out_f)
    st = _backward_a1(st, wbr, w_out_f)
    parts_a = [st["dw_out"], st["dwbr"][0], st["dwbr"][1], st["dwbr"][2]]
    plan_a = _plan_scatter(len(parts_a))
    swap_a = swap_start(parts_a, "a")
    st = _backward_a2(st, mem0, w_memkv_f, conv_w_f, conv_b + swap_a[4][0:1, 0:1], w_a_b, b_rg_a, w_x_b, b_rg_x, lru_lambda)
    a_send, a_recv, a_src, a_land, a_token = scatter_start(swap_a, st["dxr"], "a")
    parts_c = [st["dw_memkv"], _owner_blocks(st["dw_a"]), _owner_blocks(st["dw_x"])]
    plan_c = _plan_scatter(len(parts_c))
    swap_c = swap_start(parts_c, "c")

    st = _backward_b(st, rel_bias, swa_sinks + swap_c[4][0:1, 0:1] + a_token[0:1, 0:1])
    c_send, c_recv, c_src, c_land, c_token = scatter_start(swap_c, st["dsinks"], "c", prefill=(1, 2))
    plan_b = _plan_scatter(1)

    def dw_in_parts(half, dep):
        dwh = _dw_in_half(st, half, dep)
        return dwh, [dwh]

    dw0, parts_b0 = dw_in_parts(0, c_token)
    swap_b0 = swap_start(parts_b0, "b0")
    a_land = _exchange_wait(a_send, a_recv, a_src, a_land, plan_a, swap_b0[4], "scatter_a_wait")
    big = [None] * 6

    chip1 = jnp.reshape(chip, (1,)).astype(jnp.int32)

    def adamw_big(j, land, own, wt, mt, vt):
        big[j] = _adamw(land, own, chip1, wt, mt, vt, "adamw_big%d" % j)

    adamw_big(5, a_land[0], a_src[0], w_out, m_w_out, v_w_out)
    adamw_big(2, a_land[1], a_src[1], w_br_rg, m_w_br_rg, v_w_br_rg)
    adamw_big(3, a_land[2], a_src[2], w_br_swa, m_w_br_swa, v_w_br_swa)
    halves = [scatter_start(swap_b0, corner(big[3][1]), "b0")]
    dw1, parts_b1 = dw_in_parts(1, halves[0][4])
    swap_b1 = swap_start(parts_b1, "b1")
    c_land = _exchange_wait(c_send, c_recv, c_src, c_land, plan_c, swap_b1[4], "scatter_c_wait")
    g_wa_blk = _sum_parts(c_land[1], "sum_w_rg_a")
    g_wx_blk = _sum_parts(c_land[2], "sum_w_rg_x")
    adamw_big(4, a_land[3], a_src[3], w_br_mem, m_w_br_mem, v_w_br_mem)
    adamw_big(1, c_land[0], c_src[0], w_mem_kv, m_w_mem_kv, v_w_mem_kv)
    halves.append(scatter_start(swap_b1, corner(big[1][1]), "b1"))
    st["dmem_g"] = _mem_gain_grad(st, mem0, w_memkv_f, halves[1][4])
    grad_x, dpre = _dh_dx(st["dproj"], w_in_f, x0, st["dy"], pre_norm_g + halves[1][4][0:1, 0:1], st["tm"], D_IN_TILE)
    pack = jnp.concatenate([dpre.reshape(16, 128), st["dpost"].reshape(16, 128), st["dmem_g"].reshape(16, 128),
                            st["dvec"].reshape(64, 128), _pad_rows(st["dsinks"], 8), _pad_rows(st["drel"], 32), g_wa_blk, g_wx_blk,
                            st["loss"]], axis=0)
    plan_s = _plan_everyone(1)
    s_send, s_recv, s_src, s_land, s_token = _exchange_start([pack], [landing(pack, me, N_DEV)], plan_s, "gather_small_start")
    swap_last = lambda a: jnp.transpose(a, (0, 2, 1))
    after, big0_t = s_token, None
    for half, (b_send, b_recv, b_src, b_land, _) in enumerate(halves):
        b_land = _exchange_wait(b_send, b_recv, b_src, b_land, plan_b, after, "scatter_b%d_wait" % half)[0]
        big0_t = _adamw(b_land, b_src[0], chip1, swap_last(w_in), swap_last(m_w_in), swap_last(v_w_in), "adamw_big0_%d" % half,
                        group=(half, 2), into=big0_t)
        after = corner(big0_t[1])
    big[0] = [swap_last(a) for a in big0_t]
    gathered = _exchange_wait(s_send, s_recv, s_src, s_land, plan_s, corner(big0_t[1]), "gather_small_wait")[0]
    gs = _sum_parts(gathered, "sum_small")
    loss_total = gs[408, 0]
    g_pre, g_post, g_memg = gs[0:16].reshape(1, D_MODEL), gs[16:32].reshape(1, D_MODEL), gs[32:48].reshape(1, D_MODEL)
    gvec = gs[48:112].reshape(8, D_RNN)
    g_conv_w = lax.dynamic_slice(gvec[0:CONV_W], (0, me * RNN_BLOCK), (CONV_W, RNN_BLOCK))
    g_conv_b, g_b_a, g_b_x, g_lam = gvec[4:5], gvec[5:6], gvec[6:7], gvec[7:8]
    g_sinks = gs[112:113, :SWA_HEADS]
    g_rel = gs[120:152, :SWA_HEADS]
    g_w_a = gathered[:, 152:280]
    g_w_x = gathered[:, 280:408]

    g_small = (g_pre, g_post, g_memg, g_conv_b, g_b_a, g_b_x, g_lam, g_w_a, g_w_x, g_sinks, g_rel, g_conv_w)
    w_small = (pre_norm_g, post_norm_g, mem_norm_g, conv_b, b_rg_a, b_rg_x, lru_lambda, w_rg_a, w_rg_x, swa_sinks, rel_bias, conv_w)
    m_small = (m_pre_norm_g, m_post_norm_g, m_mem_norm_g, m_conv_b, m_b_rg_a, m_b_rg_x, m_lru_lambda, m_w_rg_a, m_w_rg_x, m_swa_sinks, m_rel_bias, m_conv_w)
    v_small = (v_pre_norm_g, v_post_norm_g, v_mem_norm_g, v_conv_b, v_b_rg_a, v_b_rg_x, v_lru_lambda, v_w_rg_a, v_w_rg_x, v_swa_sinks, v_rel_bias, v_conv_w)
    sm = _adamw_small(g_small, w_small, m_small, v_small)


    def leaves(k):
        s = sm[k]
        return [s[0], s[1], s[2], big[0][k], s[11], s[3], s[7], s[4], s[8], s[5], s[6], s[9], s[10],
                big[1][k], big[2][k], big[3][k], big[4][k], big[5][k]]

    return (loss_total, grad_x[None], *leaves(0), *leaves(1), *leaves(2), *leaves(3))
```

```python
import math

import jax
import jax.numpy as jnp
import numpy as np
from jax import lax
from jax.experimental import pallas as pl
from jax.experimental.pallas import tpu as pltpu

F32, BF = jnp.float32, jnp.bfloat16
MESH = pl.DeviceIdType.MESH
N_DEV = 8

D_MODEL = 2048
D_RNN = 1024
RNN_BLOCKS = 8
RNN_BLOCK = 128
CONV_W = 4
LRU_C = 8.0
SWA_HEADS = 16
SWA_KV_HEADS = 2
SWA_HD = 64
WINDOW = 128
MEM_HEADS = 4
MEM_HD = 256
D_MEM = 1024
REL_BUCKETS = 32
REL_MAX_DIST = 128
EPS = 1e-6
NEG_INF = -1e30
D_IN = 12544
SEGMENTS = (("xr", 0, 1024, F32), ("g_rg", 1024, 1024, F32), ("q_s", 2048, 1024, BF), ("kv", 3072, 256, BF),
            ("g_swa", 3328, 1024, F32), ("q_m", 4352, 1024, BF), ("g_mem", 5376, 1024, F32), ("gl", 6400, 6144, F32))
SEG_TILE = 256
D_IN_TILE = 7 * SEG_TILE

ADAM_LR, ADAM_B1, ADAM_B2, ADAM_EPS, ADAM_WD, ADAM_STEP = 0.001, 0.9, 0.999, 1e-08, 0.01, 10

NN = (((1,), (0,)), ((), ()))
NT = (((1,), (1,)), ((), ()))
TN = (((0,), (0,)), ((), ()))
MIB = 2 ** 20


def _dot(a, b, dn):
    return lax.dot_general(a, b, dn, preferred_element_type=F32)


VMEM_LIMIT_MIB = 48
VMEM_LIMIT_LARGE_MIB = 56


def _params(sem=None, large=False):
    return pltpu.CompilerParams(dimension_semantics=sem, vmem_limit_bytes=(VMEM_LIMIT_LARGE_MIB if large else VMEM_LIMIT_MIB) * MIB)


def _sigmoid(z):
    return 1.0 / (1.0 + jnp.exp(-z))


def _softplus(z):
    return jnp.maximum(z, 0.0) + jnp.log(1.0 + jnp.exp(-jnp.abs(z)))


def _expm1(z):
    p = z * (1.0 + z * (0.5 + z * (1.0 / 6 + z * (1.0 / 24 + z * (1.0 / 120 + z * (1.0 / 720 + z * (1.0 / 5040 + z / 40320)))))))
    return jnp.where(jnp.abs(z) < 0.3, p, jnp.exp(z) - 1.0)


def _flat(p):
    return 4 * p[0] + 2 * p[1] + p[2]


def _all_gather_relayed(arrs, relay, name):
    n = len(arrs)
    K = 9

    def body(*refs):
        ins, outs = refs[:n], refs[n:2 * n]
        send_sems, recv_sems, local_sems = refs[2 * n:]
        x, y, c = lax.axis_index("x"), lax.axis_index("y"), lax.axis_index("c")
        me, sib = (x, y, c), (x, y, 1 - c)
        xn, yn, dg = (1 - x, y, c), (x, 1 - y, c), (1 - x, 1 - y, c)

        def other(p):
            return (p[0], p[1], 1 - p[2])

        def rows(a, half):
            h = arrs[a].shape[0] // 2
            return pl.ds(half * h, h)

        def copy(a, k, block, to, half=None, src=None):
            dst = outs[a].at[_flat(block)]
            if half is not None:
                dst = dst.at[rows(a, half)]
            return pltpu.make_async_remote_copy(src_ref=dst if src is None else src, dst_ref=dst,
                                                send_sem=send_sems.at[a * K + k], recv_sem=recv_sems.at[a * K + k],
                                                device_id=to, device_id_type=MESH)

        mine = [pltpu.make_async_copy(ins[a], outs[a].at[_flat(me)], local_sems.at[a]) for a in range(n)]
        for cp in mine:
            cp.start()
        sends = []

        def start(cp):
            cp.start()
            sends.append(cp)

        for a in range(n):
            start(copy(a, 1, me, xn, src=ins[a]))
            start(copy(a, 2, me, yn, src=ins[a]))
            if not relay[a]:
                start(copy(a, 3, me, dg, src=ins[a]))
            start(copy(a, 0, me, sib, src=ins[a]))
        for a in range(n):
            copy(a, 1, xn, me).wait_recv()
            if relay[a]:
                start(copy(a, 3, xn, yn, half=0))
            start(copy(a, 5, xn, sib))
        for a in range(n):
            copy(a, 2, yn, me).wait_recv()
            if relay[a]:
                start(copy(a, 4, yn, xn, half=1))
            start(copy(a, 6, yn, sib))
        for a in range(n):
            if relay[a]:
                copy(a, 3, dg, me, half=0).wait_recv()
                start(copy(a, 7, dg, sib, half=0))
                copy(a, 4, dg, me, half=1).wait_recv()
                start(copy(a, 8, dg, sib, half=1))
            else:
                copy(a, 3, dg, me).wait_recv()
                start(copy(a, 7, dg, sib))
        for a in range(n):
            copy(a, 0, sib, me).wait_recv()
            copy(a, 5, other(xn), me).wait_recv()
            copy(a, 6, other(yn), me).wait_recv()
            if relay[a]:
                copy(a, 7, other(dg), me, half=0).wait_recv()
                copy(a, 8, other(dg), me, half=1).wait_recv()
            else:
                copy(a, 7, other(dg), me).wait_recv()
        for cp in sends:
            cp.wait_send()
        for cp in mine:
            cp.wait()

    any_spec = pl.BlockSpec(memory_space=pl.ANY)
    return pl.pallas_call(
        body, name=name,
        out_shape=[jax.ShapeDtypeStruct((N_DEV,) + a.shape, a.dtype) for a in arrs],
        in_specs=[any_spec] * n, out_specs=[any_spec] * n,
        scratch_shapes=[pltpu.SemaphoreType.DMA((K * n,)), pltpu.SemaphoreType.DMA((K * n,)), pltpu.SemaphoreType.DMA((n,))],
    )(*arrs)


def _chip_peers(x, y):
    return [(1 - x, y), (x, 1 - y), (1 - x, 1 - y)]


def _chip(p):
    return 2 * p[0] + p[1]


def _plan_gather(n):
    def plan(x, y, c):
        out = []
        for a in range(n):
            for peer in [(x, y, 1 - c)] + [(*ch, c) for ch in _chip_peers(x, y)]:
                out.append((a, None, ("lead", _flat((x, y, c))), peer, ("lead", _flat(peer))))
        return out
    return plan


def _plan_everyone(n):
    def plan(x, y, c):
        out = []
        for a in range(n):
            for r in range(1, N_DEV):
                peer = (1 - x if r & 4 else x, 1 - y if r & 2 else y, 1 - c if r & 1 else c)
                out.append((a, None, ("lead", _flat((x, y, c))), peer, ("lead", _flat(peer))))
        return out
    return plan


def _plan_swap(ndims):
    def plan(x, y, c):
        out = []
        for a, nd in enumerate(ndims):
            if nd == 4:
                out.append((a, 1 - c, ("all", 0), (x, y, 1 - c), ("all", 0)))
            else:
                out += [(a, 2 * j + 1 - c, ("lead", j), (x, y, 1 - c), ("lead", j)) for j in range(4)]
        return out
    return plan


def _slot(ref, where):
    kind, k = where
    return ref if kind == "all" else ref.at[k]


def _plan_scatter(n):
    def plan(x, y, c):
        out = []
        for a in range(n):
            for ch in _chip_peers(x, y):
                out.append((a, _chip(ch), ("lead", _chip((x, y))), (*ch, c), ("lead", _chip(ch))))
        return out
    return plan


HBM_SPEC = pl.BlockSpec(memory_space=pltpu.HBM)
SEM_SPEC = pl.BlockSpec(memory_space=pltpu.SEMAPHORE)


def _in_hbm(a):
    return pltpu.with_memory_space_constraint(a, pltpu.HBM)


def _exchange_start(srcs, lands, plan, name):
    n = len(srcs)
    count = len(plan(0, 0, 0))

    def body(*refs):
        src_refs, land_refs = refs[:n], refs[n:2 * n]
        send_sems, recv_sems = refs[2 * n], refs[2 * n + 1]
        token = refs[-1]
        x, y, c = lax.axis_index("x"), lax.axis_index("y"), lax.axis_index("c")
        for k, (a, si, di, peer, _) in enumerate(plan(x, y, c)):
            src = src_refs[a] if si is None else src_refs[a].at[si]
            pltpu.make_async_remote_copy(src_ref=src, dst_ref=_slot(land_refs[a], di), send_sem=send_sems.at[k],
                                         recv_sem=recv_sems.at[k], device_id=peer, device_id_type=MESH).start()
        token[...] = jnp.zeros_like(token)

    out = pl.pallas_call(
        body, name=name,
        out_shape=(pltpu.SemaphoreType.DMA((count,)), pltpu.SemaphoreType.DMA((count,)),
                   *[pltpu.HBM(a.shape, a.dtype) for a in lands], jax.ShapeDtypeStruct((8, 128), F32)),
        in_specs=[HBM_SPEC] * (2 * n),
        out_specs=(SEM_SPEC, SEM_SPEC, *([HBM_SPEC] * n), pl.BlockSpec(memory_space=pltpu.VMEM)),
        input_output_aliases={n + i: 2 + i for i in range(n)},
        compiler_params=pltpu.CompilerParams(has_side_effects=pltpu.SideEffectType.DATAFLOW_SIDE_EFFECTING),
    )(*[_in_hbm(a) for a in srcs], *[_in_hbm(a) for a in lands])
    return out[0], out[1], list(srcs), list(out[2:2 + n]), out[-1]


def _exchange_wait(send_sems, recv_sems, srcs, lands, plan, after, name):
    n = len(srcs)

    def body(*refs):
        src_refs, land_refs = refs[:n], refs[n:2 * n]
        send_sems, recv_sems = refs[2 * n], refs[2 * n + 1]
        x, y, c = lax.axis_index("x"), lax.axis_index("y"), lax.axis_index("c")
        for k, (a, si, _, peer, ri) in enumerate(plan(x, y, c)):
            src = src_refs[a] if si is None else src_refs[a].at[si]
            cp = pltpu.make_async_remote_copy(src_ref=src, dst_ref=_slot(land_refs[a], ri), send_sem=send_sems.at[k],
                                              recv_sem=recv_sems.at[k], device_id=peer, device_id_type=MESH)
            cp.wait_send()
            cp.wait_recv()

    out = pl.pallas_call(
        body, name=name,
        out_shape=tuple(pltpu.HBM(a.shape, a.dtype) for a in lands),
        in_specs=[HBM_SPEC] * (2 * n) + [SEM_SPEC, SEM_SPEC, pl.BlockSpec(memory_space=pl.ANY)],
        out_specs=tuple([HBM_SPEC] * n),
        input_output_aliases={n + i: i for i in range(n)},
        compiler_params=pltpu.CompilerParams(has_side_effects=pltpu.SideEffectType.DATAFLOW_SIDE_EFFECTING),
    )(*[_in_hbm(a) for a in srcs], *lands, send_sems, recv_sems, after)
    return list(out)


def _forward_to_sibling(lands, name):
    n = len(lands)

    def body(*refs):
        in_refs, out_refs = refs[:n], refs[n:2 * n]
        send_sems, recv_sems = refs[2 * n:]
        x, y, c = lax.axis_index("x"), lax.axis_index("y"), lax.axis_index("c")
        sibling = (x, y, 1 - c)

        def copy(a, j, slot):
            return pltpu.make_async_remote_copy(src_ref=in_refs[a].at[slot], dst_ref=out_refs[a].at[slot],
                                                send_sem=send_sems.at[a * 3 + j], recv_sem=recv_sems.at[a * 3 + j],
                                                device_id=sibling, device_id_type=MESH)

        sends = [copy(a, j, _flat((*ch, c))) for a in range(n) for j, ch in enumerate(_chip_peers(x, y))]
        for cp in sends:
            cp.start()
        for a in range(n):
            for j, ch in enumerate(_chip_peers(x, y)):
                copy(a, j, _flat((*ch, 1 - c))).wait_recv()
        for cp in sends:
            cp.wait_send()

    any_spec = pl.BlockSpec(memory_space=pl.ANY)
    return pl.pallas_call(
        body, name=name, out_shape=[jax.ShapeDtypeStruct(a.shape, a.dtype) for a in lands],
        in_specs=[any_spec] * n, out_specs=[any_spec] * n, input_output_aliases={a: a for a in range(n)},
        scratch_shapes=[pltpu.SemaphoreType.DMA((3 * n,)), pltpu.SemaphoreType.DMA((3 * n,))],
    )(*lands)


def _place_own(zones, owns, slot, name):
    n = len(zones)

    def body(slot_ref, *refs):
        for a in range(n):
            refs[2 * n + a][...] = refs[a][...]

    return pl.pallas_call(
        body, name=name,
        grid_spec=pltpu.PrefetchScalarGridSpec(
            num_scalar_prefetch=1, grid=(1,),
            in_specs=[pl.BlockSpec(o.shape, lambda i, s_ref: (0, 0)) for o in owns] + [pl.BlockSpec(memory_space=pl.ANY)] * n,
            out_specs=[pl.BlockSpec((None,) + o.shape, lambda i, s_ref: (s_ref[0], 0, 0)) for o in owns]),
        out_shape=[jax.ShapeDtypeStruct(z.shape, z.dtype) for z in zones],
        input_output_aliases={1 + n + a: a for a in range(n)},
        compiler_params=_params(("arbitrary",)),
    )(slot, *owns, *zones)


def _pair_sum(parts, got, core, name):
    R, C = parts.shape[-2:]
    tr = 256 if R % 256 == 0 else R
    mine = (pl.BlockSpec((None, None, tr, C), lambda j, i, c_ref: (c_ref[0], j, i, 0)) if parts.ndim == 4
            else pl.BlockSpec((None, tr, C), lambda j, i, c_ref: (2 * j + c_ref[0], i, 0)))

    def body(c_ref, p_ref, g_ref, o_ref):
        o_ref[...] = (p_ref[...].astype(F32) + g_ref[...].astype(F32)).astype(o_ref.dtype)

    return pl.pallas_call(
        body, name=name,
        grid_spec=pltpu.PrefetchScalarGridSpec(
            num_scalar_prefetch=1, grid=(4, R // tr),
            in_specs=[mine, pl.BlockSpec((None, tr, C), lambda j, i, c_ref: (j, i, 0))],
            out_specs=pl.BlockSpec((None, tr, C), lambda j, i, c_ref: (j, i, 0))),
        out_shape=jax.ShapeDtypeStruct((4, R, C), parts.dtype),
        compiler_params=_params(("parallel", "parallel")),
    )(core, parts, got)


def _matmul(a, b, mode, M, N, K, tm, tn, tk, out_dtype, name, b_noff=0, out_blocked=None, dep=None):
    nm, nn, nk = M // tm, N // tn, K // tk
    if mode == "nn":
        a_spec = pl.BlockSpec((tm, tk), lambda j, i, k: (i, k))
        b_spec = pl.BlockSpec((tk, tn), lambda j, i, k: (k, j + b_noff))
        dn = NN
    elif mode == "nt":
        a_spec = pl.BlockSpec((tm, tk), lambda j, i, k: (i, k))
        b_spec = pl.BlockSpec((tn, tk), lambda j, i, k: (j + b_noff, k))
        dn = NT
    else:
        a_spec = pl.BlockSpec((tk, tm), lambda j, i, k: (k, i))
        b_spec = pl.BlockSpec((tk, tn), lambda j, i, k: (k, j + b_noff))
        dn = TN
    if out_blocked == "col":
        out_shape = jax.ShapeDtypeStruct((2, 4, M, tn), out_dtype)
        out_spec = pl.BlockSpec((None, None, tm, tn), lambda j, i, k: (j % 2, j // 2, i, 0))
    elif out_blocked == "row":
        out_shape = jax.ShapeDtypeStruct((2, 4, tm, N), out_dtype)
        out_spec = pl.BlockSpec((None, None, tm, tn), lambda j, i, k: (i % 2, i // 2, 0, j))
    else:
        out_shape = jax.ShapeDtypeStruct((M, N), out_dtype)
        out_spec = pl.BlockSpec((tm, tn), lambda j, i, k: (i, j))

    n_extra = int(dep is not None)

    def body(a_ref, b_ref, *rest):
        o_ref, scratch = rest[n_extra], rest[n_extra + 1:]
        if nk == 1:
            o_ref[...] = _dot(a_ref[...], b_ref[...], dn).astype(out_dtype)
        else:
            acc_ref, = scratch
            k = pl.program_id(2)

            @pl.when(k == 0)
            def _():
                acc_ref[...] = jnp.zeros_like(acc_ref)

            acc_ref[...] += _dot(a_ref[...], b_ref[...], dn)

            @pl.when(k == nk - 1)
            def _():
                o_ref[...] = acc_ref[...].astype(out_dtype)

    return pl.pallas_call(
        body, name=name, grid=(nn, nm, nk),
        in_specs=[a_spec, b_spec] + ([] if dep is None else [pl.BlockSpec((8, 128), lambda j, i, k: (0, 0))]),
        out_specs=out_spec, out_shape=out_shape,
        scratch_shapes=[] if nk == 1 else [pltpu.VMEM((tm, tn), F32)],
        compiler_params=_params(("parallel", "parallel", "arbitrary")),
    )(a, b, *([] if dep is None else [dep]))


def _rms_fwd(x, g, name):
    R, Dm = x.shape
    tr = min(R, 256)

    def body(x_ref, g_ref, h_ref):
        xv = x_ref[...]
        r = lax.rsqrt(jnp.mean(xv * xv, axis=-1, keepdims=True) + EPS)
        h_ref[...] = (xv * r * g_ref[...]).astype(BF)

    return pl.pallas_call(
        body, name=name, grid=(R // tr,),
        in_specs=[pl.BlockSpec((tr, Dm), lambda i: (i, 0)), pl.BlockSpec((1, Dm), lambda i: (0, 0))],
        out_specs=pl.BlockSpec((tr, Dm), lambda i: (i, 0)), out_shape=jax.ShapeDtypeStruct((R, Dm), BF),
        compiler_params=_params(("parallel",)),
    )(x, g)


def _rms_gain_grad(dn, x, name):
    R, Dm = x.shape

    def body(dn_ref, x_ref, o_ref):
        xv = x_ref[...]
        r = lax.rsqrt(jnp.mean(xv * xv, axis=-1, keepdims=True) + EPS)
        o_ref[...] = jnp.sum(dn_ref[...] * xv * r, axis=0, keepdims=True)

    return pl.pallas_call(
        body, name=name, out_shape=jax.ShapeDtypeStruct((1, Dm), F32),
        compiler_params=_params(),
    )(dn, x)


def _shift_down(v, k, head8, row, T):
    if k == 0:
        return v
    r = pltpu.roll(v, k, 0)
    hr = pltpu.roll(head8, k, 0)
    top = jnp.where(row[:8] < k, hr, r[:8])
    return jnp.concatenate([top, r[8:]], axis=0)


def _shift_up(v, k, tail8, row, T):
    if k == 0:
        return v
    r = pltpu.roll(v, T - k, 0)
    tr = pltpu.roll(tail8, 8 - k, 0)
    bot = jnp.where(row[:8] >= 8 - k, tr, r[T - 8:])
    return jnp.concatenate([r[:T - 8], bot], axis=0)


def _rglru_gates(u, head8, grow, row, T, cw_ref, cb_ref, wa_ref, ba_ref, wx_ref, bx_ref, lam_ref):
    us = [_shift_down(u, k, head8, row, T) for k in range(CONV_W)]
    acc = us[0] * cw_ref[0:1, :]
    for k in range(1, CONV_W):
        acc = acc + us[k] * cw_ref[k:k + 1, :]
    conv = cb_ref[...] + acc
    cbf = conv.astype(BF)
    r_ = _sigmoid(_dot(cbf, wa_ref[0], NN) + ba_ref[...])
    i_ = _sigmoid(_dot(cbf, wx_ref[0], NN) + bx_ref[...])
    sp = _softplus(-lam_ref[...])
    la = -LRU_C * r_ * sp
    a = jnp.exp(la)
    mult_raw = jnp.sqrt(-_expm1(2.0 * la))
    mult = jnp.where(grow == 0, 1.0, mult_raw)
    return us, conv, cbf, r_, i_, sp, a, mult_raw, mult


def _rglru_specs(T, nt, rev):
    tmap = (lambda n, t: (nt - 1 - t, n)) if rev else (lambda n, t: (t, n))
    hmap = ((lambda n, t: (jnp.maximum((nt - 1 - t) * (T // 8) - 1, 0), n)) if rev
            else (lambda n, t: (jnp.maximum(t * (T // 8) - 1, 0), n)))
    tile = pl.BlockSpec((T, RNN_BLOCK), tmap)
    halo = pl.BlockSpec((8, RNN_BLOCK), hmap)
    vec = pl.BlockSpec((1, RNN_BLOCK), lambda n, t: (0, n))
    cw = pl.BlockSpec((CONV_W, RNN_BLOCK), lambda n, t: (0, n))
    wblk = pl.BlockSpec((1, RNN_BLOCK, RNN_BLOCK), lambda n, t: (n, 0, 0))
    return tile, halo, vec, cw, wblk


def _rglru_fwd(xr, g, cw, cb, wa, ba, wx, bx, lam, T):
    S = xr.shape[0]
    nt = S // T

    def body(u_ref, uh_ref, g_ref, cw_ref, cb_ref, wa_ref, ba_ref, wx_ref, bx_ref, lam_ref, h_ref, y_ref, carry):
        t = pl.program_id(1)

        @pl.when(t == 0)
        def _():
            carry[...] = jnp.zeros_like(carry)

        row = lax.broadcasted_iota(jnp.int32, (T, RNN_BLOCK), 0)
        grow = row + t * T
        head8 = jnp.where(t > 0, uh_ref[...], 0.0)
        _, conv, _, _, i_, _, a, _, mult = _rglru_gates(u_ref[...], head8, grow, row, T, cw_ref, cb_ref, wa_ref, ba_ref,
                                                         wx_ref, bx_ref, lam_ref)
        b = mult * i_ * conv
        s = 1
        while s < T:
            keep = row >= s
            a_s = jnp.where(keep, pltpu.roll(a, s, 0), 1.0)
            b_s = jnp.where(keep, pltpu.roll(b, s, 0), 0.0)
            b = a * b_s + b
            a = a * a_s
            s *= 2
        h = b + a * carry[0:1, :]
        carry[...] = jnp.broadcast_to(h[T - 1:T, :], carry.shape)
        h_ref[...] = h
        gv = g_ref[...]
        y_ref[...] = (h * (gv * _sigmoid(gv))).astype(BF)

    tile, halo, vec, cwspec, wblk = _rglru_specs(T, nt, False)
    return pl.pallas_call(
        body, name="rglru_fwd", grid=(RNN_BLOCKS, nt),
        in_specs=[tile, halo, tile, cwspec, vec, wblk, vec, wblk, vec, vec],
        out_specs=[tile, tile],
        out_shape=[jax.ShapeDtypeStruct((S, D_RNN), F32), jax.ShapeDtypeStruct((S, D_RNN), BF)],
        scratch_shapes=[pltpu.VMEM((8, RNN_BLOCK), F32)],
        compiler_params=_params(("parallel", "arbitrary")),
    )(xr, xr, g, cw, cb, wa, ba, wx, bx, lam)


def _rglru_bwd(xr, g, h, dy, cw, cb, wa, ba, wx, bx, lam, T):
    S = xr.shape[0]
    nt = S // T

    def body(u_ref, uh_ref, g_ref, h_ref, hh_ref, dy_ref, cw_ref, cb_ref, wa_ref, ba_ref, wx_ref, bx_ref, lam_ref,
             du_ref, dg_ref, dwa_ref, dwx_ref, dvec_ref, c_dhh, c_a, c_dconv):
        t = pl.program_id(1)
        tt = nt - 1 - t

        @pl.when(t == 0)
        def _():
            c_dhh[...] = jnp.zeros_like(c_dhh)
            c_a[...] = jnp.zeros_like(c_a)
            c_dconv[...] = jnp.zeros_like(c_dconv)
            dwa_ref[...] = jnp.zeros_like(dwa_ref)
            dwx_ref[...] = jnp.zeros_like(dwx_ref)
            dvec_ref[...] = jnp.zeros_like(dvec_ref)

        row = lax.broadcasted_iota(jnp.int32, (T, RNN_BLOCK), 0)
        row8 = row[:8]
        grow = row + tt * T
        head8 = jnp.where(tt > 0, uh_ref[...], 0.0)
        us, conv, cbf, r_, i_, sp, a, mult_raw, mult = _rglru_gates(
            u_ref[...], head8, grow, row, T, cw_ref, cb_ref, wa_ref, ba_ref, wx_ref, bx_ref, lam_ref)
        hv = h_ref[...]
        hprev = _shift_down(hv, 1, jnp.where(tt > 0, hh_ref[...], 0.0), row, T)
        gv = g_ref[...]
        sg = _sigmoid(gv)
        dyv = dy_ref[...]
        dg_ref[...] = (dyv * hv * (sg * (1.0 + gv * (1.0 - sg)))).astype(BF)
        d = dyv * (gv * sg)
        A = _shift_up(a, 1, c_a[...], row, T)
        s = 1
        while s < T:
            keep = row < T - s
            A_s = jnp.where(keep, pltpu.roll(A, T - s, 0), 1.0)
            d_s = jnp.where(keep, pltpu.roll(d, T - s, 0), 0.0)
            d = A * d_s + d
            A = A * A_s
            s *= 2
        dhh = d + A * c_dhh[0:1, :]
        da = dhh * hprev
        dconv = dhh * mult * i_
        di = dhh * mult * conv
        dmult = dhh * i_ * conv
        dla = da * a - jnp.where(grow == 0, 0.0, dmult * (a * a) / mult_raw)
        dr = dla * (-LRU_C * sp)
        dsp = jnp.sum(dla * (-LRU_C * r_), axis=0, keepdims=True)
        dza = dr * r_ * (1.0 - r_)
        dzx = di * i_ * (1.0 - i_)
        dza_b, dzx_b = dza.astype(BF), dzx.astype(BF)
        dconv = dconv + _dot(dza_b, wa_ref[0], NT) + _dot(dzx_b, wx_ref[0], NT)
        dwa_ref[0] += _dot(cbf, dza_b, TN)
        dwx_ref[0] += _dot(cbf, dzx_b, TN)
        lam = lam_ref[...]
        rows = [jnp.sum(dconv * us[k], axis=0, keepdims=True) for k in range(CONV_W)]
        rows += [jnp.sum(dconv, axis=0, keepdims=True), jnp.sum(dza, axis=0, keepdims=True),
                 jnp.sum(dzx, axis=0, keepdims=True), dsp * (-_sigmoid(-lam))]
        upd = jnp.zeros((8, RNN_BLOCK), F32)
        for j, rv in enumerate(rows):
            upd = upd + jnp.where(row8 == j, rv, 0.0)
        dvec_ref[...] += upd
        tail8 = c_dconv[...]
        du = dconv * cw_ref[0:1, :]
        for k in range(1, CONV_W):
            du = du + _shift_up(dconv, k, tail8, row, T) * cw_ref[k:k + 1, :]
        du_ref[...] = du.astype(BF)
        c_dhh[...] = jnp.broadcast_to(dhh[0:1, :], c_dhh.shape)
        c_a[...] = jnp.broadcast_to(a[0:1, :], c_a.shape)
        c_dconv[...] = dconv[:8]

    tile, halo, vec, cwspec, wblk = _rglru_specs(T, nt, True)
    acc8 = pl.BlockSpec((8, RNN_BLOCK), lambda n, t: (0, n))
    return pl.pallas_call(
        body, name="rglru_bwd", grid=(RNN_BLOCKS, nt),
        in_specs=[tile, halo, tile, tile, halo, tile, cwspec, vec, wblk, vec, wblk, vec, vec],
        out_specs=[tile, tile, wblk, wblk, acc8],
        out_shape=[jax.ShapeDtypeStruct((S, D_RNN), BF), jax.ShapeDtypeStruct((S, D_RNN), BF),
                   jax.ShapeDtypeStruct((RNN_BLOCKS, RNN_BLOCK, RNN_BLOCK), F32),
                   jax.ShapeDtypeStruct((RNN_BLOCKS, RNN_BLOCK, RNN_BLOCK), F32),
                   jax.ShapeDtypeStruct((8, D_RNN), F32)],
        scratch_shapes=[pltpu.VMEM((8, RNN_BLOCK), F32)] * 3,
        compiler_params=_params(("parallel", "arbitrary")),
    )(xr, xr, g, h, h, dy, cw, cb, wa, ba, wx, bx, lam)


def _rel_bucket_map():
    qi = np.arange(WINDOW)[:, None]
    kj = np.arange(2 * WINDOW)[None, :]
    dist = jnp.asarray(qi + WINDOW - kj, jnp.int32)
    n = jnp.maximum(dist, 0)
    max_exact = REL_BUCKETS // 2
    ratio = jnp.log(jnp.maximum(n, 1).astype(F32) / max_exact) / math.log(REL_MAX_DIST / max_exact)
    large = jnp.minimum(max_exact + (ratio * (REL_BUCKETS - max_exact)).astype(jnp.int32), REL_BUCKETS - 1)
    bucket = jnp.where(n < max_exact, n, large).astype(jnp.int32)
    j = np.arange(WINDOW)[None, :]
    return jnp.where(jnp.asarray(j > qi), bucket[:, :WINDOW], bucket[:, WINDOW:])


def _swa_common(n, kv_ref, bucket_ref, relb_ref, bias_scr):
    @pl.when(n == 0)
    def _():
        bk = bucket_ref[...]
        for h in range(SWA_HEADS):
            acc = jnp.zeros((WINDOW, WINDOW), F32)
            for b in range(REL_BUCKETS):
                acc = acc + jnp.where(bk == b, relb_ref[b, h], 0.0)
            bias_scr[h] = acc

    prev0 = pl.multiple_of(jnp.maximum(n - 1, 0) * WINDOW, WINDOW)
    cur0 = pl.multiple_of(n * WINDOW, WINDOW)
    kk = jnp.concatenate([kv_ref[pl.ds(prev0, WINDOW), :], kv_ref[pl.ds(cur0, WINDOW), :]], axis=0).astype(F32)
    rowi = lax.broadcasted_iota(jnp.int32, (WINDOW, WINDOW), 0)
    col = lax.broadcasted_iota(jnp.int32, (WINDOW, WINDOW), 1)
    from_prev = col > rowi
    return kk, from_prev, prev0, cur0


def _fold(full, from_prev):
    return jnp.where(from_prev, full[:, :WINDOW], full[:, WINDOW:])


def _unfold(sq, from_prev):
    return jnp.concatenate([jnp.where(from_prev, sq, 0.0), jnp.where(from_prev, 0.0, sq)], axis=1)


def _half_pair(part, kvh):
    lo = lax.broadcasted_iota(jnp.int32, part.shape, 1) < SWA_HD
    if kvh == 0:
        pa = jnp.where(lo, part, 0.0)
        pb = pltpu.roll(pa, SWA_HD, 1)
    else:
        pb = jnp.where(lo, 0.0, part)
        pa = pltpu.roll(pb, SWA_HD, 1)
    return pa.astype(BF), pb.astype(BF)


ALL_HEADS = SWA_HEADS * WINDOW


def _sink_column(sinks):
    return jnp.repeat(sinks.reshape(SWA_HEADS), WINDOW).reshape(ALL_HEADS, 1)


def _swa_operands(kk):
    return [(_half_pair(kk[:, :128], kvh), _half_pair(kk[:, 128:], kvh)) for kvh in range(SWA_KV_HEADS)]


def _swa_probs(n, q_ref, ops, bias_scr, sinkc_ref, from_prev):
    lgs = []
    for kvh in range(SWA_KV_HEADS):
        (ka, kb), _ = ops[kvh]
        for p in range(4):
            q2 = q_ref[:, kvh * 512 + p * 128:kvh * 512 + p * 128 + 128]
            lgs += [_fold(_dot(q2, ka, NT), from_prev), _fold(_dot(q2, kb, NT), from_prev)]
    lg = jnp.concatenate(lgs, axis=0) * (SWA_HD ** -0.5) + bias_scr[...].reshape(ALL_HEADS, WINDOW)
    rowi = jnp.bitwise_and(lax.broadcasted_iota(jnp.int32, (ALL_HEADS, WINDOW), 0), WINDOW - 1)
    col = lax.broadcasted_iota(jnp.int32, (ALL_HEADS, WINDOW), 1)
    no_prev = jnp.where(n > 0, 0, 4 * WINDOW)
    lg = jnp.where(jnp.logical_or(col <= rowi, col > rowi + no_prev), lg, NEG_INF)
    sink = sinkc_ref[...]
    m = jnp.maximum(jnp.max(lg, axis=-1, keepdims=True), sink)
    e = jnp.exp(lg - m)
    es = jnp.exp(sink - m)
    den = jnp.sum(e, axis=-1, keepdims=True) + es
    return e / den, es / den


def _swa_fwd(q, kv, g, bucket, rel_bias, sink_col):
    S = q.shape[0]
    nb = S // WINDOW

    def body(q_ref, kv_ref, g_ref, bucket_ref, relb_ref, sinkc_ref, o_ref, y_ref, bias_scr):
        n = pl.program_id(0)
        kk, from_prev, _, _ = _swa_common(n, kv_ref, bucket_ref, relb_ref, bias_scr)
        ops = _swa_operands(kk)
        pr, _ = _swa_probs(n, q_ref, ops, bias_scr, sinkc_ref, from_prev)
        for kvh in range(SWA_KV_HEADS):
            _, (va, vb) = ops[kvh]
            for p in range(4):
                c0 = kvh * 512 + p * 128
                r0 = (kvh * 8 + 2 * p) * WINDOW
                o2 = (_dot(_unfold(pr[r0:r0 + WINDOW], from_prev).astype(BF), va, NN)
                      + _dot(_unfold(pr[r0 + WINDOW:r0 + 2 * WINDOW], from_prev).astype(BF), vb, NN))
                o_ref[:, c0:c0 + 128] = o2
                gv = g_ref[:, c0:c0 + 128]
                y_ref[:, c0:c0 + 128] = (o2 * (gv * _sigmoid(gv))).astype(BF)

    blk = pl.BlockSpec((WINDOW, 1024), lambda n: (n, 0))
    smem = pl.BlockSpec(memory_space=pltpu.SMEM)
    sinkc = pl.BlockSpec((ALL_HEADS, 1), lambda n: (0, 0))
    return pl.pallas_call(
        body, name="swa_fwd", grid=(nb,),
        in_specs=[blk, pl.BlockSpec((S, 256), lambda n: (0, 0)), blk, pl.BlockSpec((WINDOW, WINDOW), lambda n: (0, 0)), smem, sinkc],
        out_specs=[blk, blk],
        out_shape=[jax.ShapeDtypeStruct((S, 1024), F32), jax.ShapeDtypeStruct((S, 1024), BF)],
        scratch_shapes=[pltpu.VMEM((SWA_HEADS, WINDOW, WINDOW), F32)],
        compiler_params=_params(("arbitrary",)),
    )(q, kv, g, bucket, rel_bias, sink_col)


def _swa_bwd(q, kv, g, o, dy, bucket, rel_bias, sink_col, others):
    S = q.shape[0]
    nb = S // WINDOW
    first_col = {name: c0 for name, c0, _, _ in SEGMENTS}
    col_q, col_g = first_col["q_s"], first_col["g_swa"]
    copies = []
    for name, c0, width, _ in SEGMENTS:
        if name not in ("q_s", "kv", "g_swa"):
            arrs = others[name] if name == "gl" else (others[name],)
            copies += [(a, c0 + i * (width // len(arrs))) for i, a in enumerate(arrs)]

    def body(q_ref, kv_ref, g_ref, o_ref, dy_ref, bucket_ref, relb_ref, sinkc_ref, *refs):
        copy_refs = refs[:len(copies)]
        dp_ref, dkv_ref, dsink_ref, drel_ref, bias_scr, dbias_scr, dsink_scr = refs[len(copies):]
        n = pl.program_id(0)
        for c_ref, (a, c0) in zip(copy_refs, copies):
            dp_ref[:, c0:c0 + a.shape[1]] = c_ref[...]

        @pl.when(n == 0)
        def _():
            dbias_scr[...] = jnp.zeros_like(dbias_scr)
            dsink_scr[...] = jnp.zeros_like(dsink_scr)
            dkv_ref[...] = jnp.zeros_like(dkv_ref)

        kk, from_prev, prev0, cur0 = _swa_common(n, kv_ref, bucket_ref, relb_ref, bias_scr)
        ops = _swa_operands(kk)
        pr, ps = _swa_probs(n, q_ref, ops, bias_scr, sinkc_ref, from_prev)
        do2s, dps = [], []
        for kvh in range(SWA_KV_HEADS):
            _, (va, vb) = ops[kvh]
            for p in range(4):
                c0 = kvh * 512 + p * 128
                gv = g_ref[:, c0:c0 + 128]
                sg = _sigmoid(gv)
                dyv = dy_ref[:, c0:c0 + 128]
                dp_ref[:, col_g + c0:col_g + c0 + 128] = (dyv * o_ref[:, c0:c0 + 128] * (sg * (1.0 + gv * (1.0 - sg)))).astype(BF)
                do2 = (dyv * (gv * sg)).astype(BF)
                do2s.append(do2)
                dps += [_fold(_dot(do2, va, NT), from_prev), _fold(_dot(do2, vb, NT), from_prev)]
        dp = jnp.concatenate(dps, axis=0)
        delta = jnp.sum(pr * dp, axis=-1, keepdims=True)
        ds = pr * (dp - delta)
        dbias_scr[...] += ds.reshape(SWA_HEADS, WINDOW, WINDOW)
        dsink_scr[...] += ps * delta
        dsc = ds * (SWA_HD ** -0.5)
        lo256 = lax.broadcasted_iota(jnp.int32, (2 * WINDOW, 128), 1) < SWA_HD
        dks, dvs = [], []
        for kvh in range(SWA_KV_HEADS):
            (ka, kb), _ = ops[kvh]
            dka = jnp.zeros((2 * WINDOW, 128), F32)
            dkb, dva, dvb = dka, dka, dka
            for p in range(4):
                c0 = kvh * 512 + p * 128
                r0 = (kvh * 8 + 2 * p) * WINDOW
                q2 = q_ref[:, c0:c0 + 128]
                do2 = do2s[kvh * 4 + p]
                ds0 = _unfold(dsc[r0:r0 + WINDOW], from_prev).astype(BF)
                ds1 = _unfold(dsc[r0 + WINDOW:r0 + 2 * WINDOW], from_prev).astype(BF)
                dp_ref[:, col_q + c0:col_q + c0 + 128] = (_dot(ds0, ka, NN) + _dot(ds1, kb, NN)).astype(BF)
                dka = dka + _dot(ds0, q2, TN)
                dkb = dkb + _dot(ds1, q2, TN)
                dva = dva + _dot(_unfold(pr[r0:r0 + WINDOW], from_prev).astype(BF), do2, TN)
                dvb = dvb + _dot(_unfold(pr[r0 + WINDOW:r0 + 2 * WINDOW], from_prev).astype(BF), do2, TN)
            dks.append(jnp.where(lo256, dka, 0.0) + pltpu.roll(jnp.where(lo256, 0.0, dkb), SWA_HD, 1))
            dvs.append(jnp.where(lo256, dva, 0.0) + pltpu.roll(jnp.where(lo256, 0.0, dvb), SWA_HD, 1))
        dk = dks[0] + pltpu.roll(dks[1], SWA_HD, 1)
        dv = dvs[0] + pltpu.roll(dvs[1], SWA_HD, 1)
        dkv_ref[pl.ds(prev0, WINDOW), 0:128] += dk[:WINDOW]
        dkv_ref[pl.ds(prev0, WINDOW), 128:256] += dv[:WINDOW]
        dkv_ref[pl.ds(cur0, WINDOW), 0:128] += dk[WINDOW:]
        dkv_ref[pl.ds(cur0, WINDOW), 128:256] += dv[WINDOW:]

        @pl.when(n == nb - 1)
        def _():
            dsink_ref[...] = -jnp.sum(dsink_scr[...].reshape(SWA_HEADS, WINDOW, 1), axis=1)
            bk = bucket_ref[...]
            sums = []
            for b in range(REL_BUCKETS):
                sums.append(jnp.sum(jnp.where((bk == b)[None], dbias_scr[...], 0.0), axis=1))
            drel_ref[...] = jnp.sum(jnp.concatenate(sums, axis=0), axis=1, keepdims=True)

    blk = pl.BlockSpec((WINDOW, 1024), lambda n: (n, 0))
    smem = pl.BlockSpec(memory_space=pltpu.SMEM)
    whole = lambda shape: pl.BlockSpec(shape, lambda n: (0, 0))
    return pl.pallas_call(
        body, name="swa_bwd", grid=(nb,),
        in_specs=[blk, whole((S, 256)), blk, blk, blk, whole((WINDOW, WINDOW)), smem, whole((ALL_HEADS, 1))]
        + [pl.BlockSpec((WINDOW, a.shape[1]), lambda n: (n, 0)) for a, _ in copies],
        out_specs=[pl.BlockSpec((WINDOW, D_IN), lambda n: (n, 0)), whole((S, 256)), whole((SWA_HEADS, 1)),
                   whole((REL_BUCKETS * SWA_HEADS, 1))],
        out_shape=[jax.ShapeDtypeStruct((S, D_IN), BF), jax.ShapeDtypeStruct((S, 256), F32), jax.ShapeDtypeStruct((SWA_HEADS, 1), F32),
                   jax.ShapeDtypeStruct((REL_BUCKETS * SWA_HEADS, 1), F32)],
        scratch_shapes=[pltpu.VMEM((SWA_HEADS, WINDOW, WINDOW), F32), pltpu.VMEM((SWA_HEADS, WINDOW, WINDOW), F32),
                        pltpu.VMEM((ALL_HEADS, 1), F32)],
        compiler_params=_params(("arbitrary",)),
    )(q, kv, g, o, dy, bucket, rel_bias, sink_col, *[a for a, _ in copies])


def _mem_probs(qh, mk):
    lg = _dot(qh, mk, NT) * (MEM_HD ** -0.5)
    e = jnp.exp(lg - jnp.max(lg, axis=-1, keepdims=True))
    return e / jnp.sum(e, axis=-1, keepdims=True)


def _mem_fwd(q, mkv, g):
    S = q.shape[0]
    M = mkv.shape[0]
    tq = 256

    def body(q_ref, mkv_ref, g_ref, o_ref, y_ref):
        for h in range(MEM_HEADS):
            c0 = h * MEM_HD
            pr = _mem_probs(q_ref[:, c0:c0 + MEM_HD], mkv_ref[:, c0:c0 + MEM_HD])
            o = _dot(pr.astype(BF), mkv_ref[:, D_MEM + c0:D_MEM + c0 + MEM_HD], NN)
            o_ref[:, c0:c0 + MEM_HD] = o
            gv = g_ref[:, c0:c0 + MEM_HD]
            y_ref[:, c0:c0 + MEM_HD] = (o * (gv * _sigmoid(gv))).astype(BF)

    blk = pl.BlockSpec((tq, D_MEM), lambda i: (i, 0))
    return pl.pallas_call(
        body, name="mem_fwd", grid=(S // tq,),
        in_specs=[blk, pl.BlockSpec((M, 2 * D_MEM), lambda i: (0, 0)), blk], out_specs=[blk, blk],
        out_shape=[jax.ShapeDtypeStruct((S, D_MEM), F32), jax.ShapeDtypeStruct((S, D_MEM), BF)],
        compiler_params=_params(("parallel",)),
    )(q, mkv, g)


def _mem_bwd(q, mkv, g, o, dy):
    S = q.shape[0]
    M = mkv.shape[0]
    tq = 256

    def body(q_ref, mkv_ref, g_ref, o_ref, dy_ref, dq_ref, dg_ref, dmkv_ref):
        @pl.when(pl.program_id(0) == 0)
        def _():
            dmkv_ref[...] = jnp.zeros_like(dmkv_ref)

        for h in range(MEM_HEADS):
            c0 = h * MEM_HD
            qh = q_ref[:, c0:c0 + MEM_HD]
            mk = mkv_ref[:, c0:c0 + MEM_HD]
            mv = mkv_ref[:, D_MEM + c0:D_MEM + c0 + MEM_HD]
            gv = g_ref[:, c0:c0 + MEM_HD]
            sg = _sigmoid(gv)
            dyv = dy_ref[:, c0:c0 + MEM_HD]
            dg_ref[:, c0:c0 + MEM_HD] = (dyv * o_ref[:, c0:c0 + MEM_HD] * (sg * (1.0 + gv * (1.0 - sg)))).astype(BF)
            do = (dyv * (gv * sg)).astype(BF)
            pr = _mem_probs(qh, mk)
            dp = _dot(do, mv, NT)
            ds = pr * (dp - jnp.sum(pr * dp, axis=-1, keepdims=True))
            dsb = (ds * (MEM_HD ** -0.5)).astype(BF)
            dq_ref[:, c0:c0 + MEM_HD] = _dot(dsb, mk, NN).astype(BF)
            dmkv_ref[:, c0:c0 + MEM_HD] += _dot(dsb, qh, TN)
            dmkv_ref[:, D_MEM + c0:D_MEM + c0 + MEM_HD] += _dot(pr.astype(BF), do, TN)

    blk = pl.BlockSpec((tq, D_MEM), lambda i: (i, 0))
    whole = pl.BlockSpec((M, 2 * D_MEM), lambda i: (0, 0))
    return pl.pallas_call(
        body, name="mem_bwd", grid=(S // tq,),
        in_specs=[blk, whole, blk, blk, blk], out_specs=[blk, blk, whole],
        out_shape=[jax.ShapeDtypeStruct((S, D_MEM), BF), jax.ShapeDtypeStruct((S, D_MEM), BF),
                   jax.ShapeDtypeStruct((M, 2 * D_MEM), F32)],
        compiler_params=_params(("arbitrary",)),
    )(q, mkv, g, o, dy)


MERGE_TN = 512


def _merge_specs(tm):
    ytile = pl.BlockSpec((tm, 1024), lambda i, j: (i, 0))
    wblk = pl.BlockSpec((MERGE_TN, 1024), lambda i, j: (j, 0))
    gls = [pl.BlockSpec((None, tm, MERGE_TN), (lambda i, j, br=br: (br, i, j))) for br in range(3)]
    otile = pl.BlockSpec((tm, MERGE_TN), lambda i, j: (i, j))
    return ytile, wblk, gls, otile


def _merge_fwd(ys, ws, gl, tm):
    S = gl.shape[1]

    def body(y0, y1, y2, w0, w1, w2, g0, g1, g2, o_ref):
        acc = None
        for y_ref, w_ref, g_ref in ((y0, w0, g0), (y1, w1, g1), (y2, w2, g2)):
            term = _sigmoid(g_ref[...]) * _dot(y_ref[...], w_ref[...], NT)
            acc = term if acc is None else acc + term
        o_ref[...] = acc.astype(BF)

    ytile, wblk, gls, otile = _merge_specs(tm)
    return pl.pallas_call(
        body, name="merge_fwd", grid=(S // tm, D_MODEL // MERGE_TN),
        in_specs=[ytile] * 3 + [wblk] * 3 + gls, out_specs=otile,
        out_shape=jax.ShapeDtypeStruct((S, D_MODEL), BF),
        compiler_params=_params(("parallel", "arbitrary")),
    )(*ys, *ws, gl, gl, gl)


def _merge_bwd(dout, w_out, ys, ws, gl, tm):
    S = gl.shape[1]

    def body(do_ref, wo_ref, y0, y1, y2, w0, w1, w2, g0, g1, g2, dg0, dg1, dg2, dp0, dp1, dp2):
        dm = _dot(do_ref[...], wo_ref[...], NT)
        for y_ref, w_ref, g_ref, dg_ref, dp_ref in ((y0, w0, g0, dg0, dp0), (y1, w1, g1, dg1, dp1), (y2, w2, g2, dg2, dp2)):
            gate = _sigmoid(g_ref[...])
            pv = _dot(y_ref[...], w_ref[...], NT)
            dg_ref[...] = (dm * pv * gate * (1.0 - gate)).astype(BF)
            dp_ref[...] = (dm * gate).astype(BF)

    ytile, wblk, gls, otile = _merge_specs(tm)
    out = jax.ShapeDtypeStruct((S, D_MODEL), BF)
    return pl.pallas_call(
        body, name="merge_bwd", grid=(S // tm, D_MODEL // MERGE_TN),
        in_specs=[pl.BlockSpec((tm, D_MODEL), lambda i, j: (i, 0)), pl.BlockSpec((MERGE_TN, D_MODEL), lambda i, j: (j, 0))]
        + [ytile] * 3 + [wblk] * 3 + gls,
        out_specs=[otile] * 6, out_shape=[out] * 6,
        compiler_params=_params(("parallel", "arbitrary")),
    )(dout, w_out, *ys, *ws, gl, gl, gl)


def _out_loss(merged, w_out, x, target, post_g, tm):
    S = x.shape[0]

    def body(m_ref, w_ref, x_ref, t_ref, g_ref, dout_ref, dy_ref, loss_ref, dpost_ref):
        @pl.when(pl.program_id(0) == 0)
        def _():
            loss_ref[...] = jnp.zeros_like(loss_ref)
            dpost_ref[...] = jnp.zeros_like(dpost_ref)

        out = _dot(m_ref[...], w_ref[...], NN)
        r = lax.rsqrt(jnp.mean(out * out, axis=-1, keepdims=True) + EPS)
        nrm = out * r
        gv = g_ref[...]
        err = (x_ref[...] + nrm * gv) - t_ref[...]
        sq = jnp.sum(jnp.sum(err * err, axis=1, keepdims=True), axis=0, keepdims=True)
        loss_ref[...] += sq * (0.5 / D_MODEL)
        dy = err * (1.0 / D_MODEL)
        dy_ref[...] = dy
        dpost_ref[...] += jnp.sum(dy * nrm, axis=0, keepdims=True)
        dn = dy * gv
        dout_ref[...] = (r * (dn - nrm * jnp.mean(dn * nrm, axis=-1, keepdims=True))).astype(BF)

    row = pl.BlockSpec((tm, D_MODEL), lambda i: (i, 0))
    return pl.pallas_call(
        body, name="out_loss", grid=(S // tm,),
        in_specs=[row, pl.BlockSpec((D_MODEL, D_MODEL), lambda i: (0, 0)), row, row, pl.BlockSpec((1, D_MODEL), lambda i: (0, 0))],
        out_specs=[row, row, pl.BlockSpec((8, 128), lambda i: (0, 0)), pl.BlockSpec((1, D_MODEL), lambda i: (0, 0))],
        out_shape=[jax.ShapeDtypeStruct((S, D_MODEL), BF), jax.ShapeDtypeStruct((S, D_MODEL), F32),
                   jax.ShapeDtypeStruct((8, 128), F32), jax.ShapeDtypeStruct((1, D_MODEL), F32)],
        compiler_params=_params(("arbitrary",)),
    )(merged, w_out, x, target, post_g)


DH_DX_CHUNK = 64


def _dh_dx(dproj, w_in, x, dy, pre_g, tm, tk):
    S = x.shape[0]
    nk = D_IN // tk

    def body(dp_ref, w_ref, x_ref, dy_ref, g_ref, dx_ref, dpre_ref, acc_ref):
        i, k = pl.program_id(0), pl.program_id(1)

        @pl.when(jnp.logical_and(i == 0, k == 0))
        def _():
            dpre_ref[...] = jnp.zeros_like(dpre_ref)

        @pl.when(k == 0)
        def _():
            acc_ref[...] = jnp.zeros_like(acc_ref)

        acc_ref[...] += _dot(dp_ref[...], w_ref[...], NN)

        @pl.when(k == nk - 1)
        def _():
            def chunk(c, carry):
                rows = pl.ds(pl.multiple_of(c * DH_DX_CHUNK, DH_DX_CHUNK), DH_DX_CHUNK)
                dh = acc_ref[rows, :]
                xv = x_ref[rows, :]
                r = lax.rsqrt(jnp.mean(xv * xv, axis=-1, keepdims=True) + EPS)
                nrm = xv * r
                dpre_ref[...] += jnp.sum(dh * nrm, axis=0, keepdims=True)
                dn = dh * g_ref[...]
                dx_ref[rows, :] = r * (dn - nrm * jnp.mean(dn * nrm, axis=-1, keepdims=True)) + dy_ref[rows, :]
                return carry
            lax.fori_loop(0, tm // DH_DX_CHUNK, chunk, 0)

    row = pl.BlockSpec((tm, D_MODEL), lambda i, k: (i, 0))
    vec = pl.BlockSpec((1, D_MODEL), lambda i, k: (0, 0))
    return pl.pallas_call(
        body, name="dh_dx", grid=(S // tm, nk),
        in_specs=[pl.BlockSpec((tm, tk), lambda i, k: (i, k)), pl.BlockSpec((tk, D_MODEL), lambda i, k: (k, 0)), row, row, vec],
        out_specs=[row, vec],
        out_shape=[jax.ShapeDtypeStruct((S, D_MODEL), F32), jax.ShapeDtypeStruct((1, D_MODEL), F32)],
        scratch_shapes=[pltpu.VMEM((tm, D_MODEL), F32)],
        compiler_params=_params(("arbitrary", "arbitrary"), large=True),
    )(dproj, w_in, x, dy, pre_g)


def _sum_parts(parts, name):
    P, R, C = parts.shape
    tr = max(t for t in range(8, 513, 8) if R % t == 0)

    def body(p_ref, o_ref):
        acc = p_ref[0]
        for j in range(1, P):
            acc = acc + p_ref[j]
        o_ref[...] = acc

    return pl.pallas_call(
        body, name=name, grid=(R // tr,),
        in_specs=[pl.BlockSpec((P, tr, C), lambda i: (0, i, 0))], out_specs=pl.BlockSpec((tr, C), lambda i: (i, 0)),
        out_shape=jax.ShapeDtypeStruct((R, C), F32), compiler_params=_params(("parallel",)),
    )(parts)


def _adamw(land, sums, chip, w, m, v, name, group=(0, 1), into=None):
    q, n_groups = group
    _, R, cols = land.shape
    C = cols * n_groups
    tr = max(t for t in range(16, 257, 16) if R % t == 0)
    c1 = 1.0 - ADAM_B1 ** ADAM_STEP
    c2 = 1.0 - ADAM_B2 ** ADAM_STEP
    n_into = 0 if into is None else 4

    def body(chip_ref, p0_ref, p1_ref, p2_ref, own_ref, w_ref, m_ref, v_ref, *refs):
        g_ref, d_ref, nm_ref, nv_ref = refs[n_into:]
        g = own_ref[...].astype(F32)
        for p_ref in (p0_ref, p1_ref, p2_ref):
            g = g + p_ref[...].astype(F32)
        nm = ADAM_B1 * m_ref[...] + (1.0 - ADAM_B1) * g
        nv = ADAM_B2 * v_ref[...] + (1.0 - ADAM_B2) * (g * g)
        g_ref[...] = g
        nm_ref[...] = nm
        nv_ref[...] = nv
        d_ref[...] = -ADAM_LR * ((nm / c1) / (jnp.sqrt(nv / c2) + ADAM_EPS) + ADAM_WD * w_ref[...])

    tile = pl.BlockSpec((None, tr, cols), lambda i, c_ref: (0, i, q))
    specs = [pl.BlockSpec((None, tr, cols), (lambda i, c_ref, k=k: (k + (c_ref[0] <= k).astype(jnp.int32), i, 0))) for k in range(3)]
    specs.append(pl.BlockSpec((None, tr, cols), (lambda i, c_ref: (c_ref[0], i, 0))))
    return pl.pallas_call(
        body, name=name,
        grid_spec=pltpu.PrefetchScalarGridSpec(num_scalar_prefetch=1, grid=(R // tr,),
                                               in_specs=specs + [tile, tile, tile] + [pl.BlockSpec(memory_space=pl.ANY)] * n_into,
                                               out_specs=[tile] * 4),
        out_shape=[jax.ShapeDtypeStruct((1, R, C), F32)] * 4,
        input_output_aliases={8 + k: k for k in range(n_into)},
        compiler_params=_params(("parallel",)),
    )(chip, land, land, land, sums, w, m, v, *(into or []))


def _adamw_small(gs, ws, ms, vs):
    n = len(ws)
    c1 = 1.0 - ADAM_B1 ** ADAM_STEP
    c2 = 1.0 - ADAM_B2 ** ADAM_STEP

    def flat2(a):
        return a.reshape(-1, a.shape[-1])

    def body(*refs):
        ins, outs = refs[:4 * n], refs[4 * n:]
        for a in range(n):
            g, w, m, v = (ins[k * n + a][...] for k in range(4))
            nm = ADAM_B1 * m + (1.0 - ADAM_B1) * g
            nv = ADAM_B2 * v + (1.0 - ADAM_B2) * (g * g)
            outs[a][...] = g
            outs[n + a][...] = -ADAM_LR * ((nm / c1) / (jnp.sqrt(nv / c2) + ADAM_EPS) + ADAM_WD * w)
            outs[2 * n + a][...] = nm
            outs[3 * n + a][...] = nv

    shapes = [flat2(w).shape for w in ws]
    out = pl.pallas_call(
        body, name="adamw_small", out_shape=[jax.ShapeDtypeStruct(sh, F32) for sh in shapes] * 4,
        compiler_params=_params(),
    )(*[g.reshape(sh) for g, sh in zip(gs, shapes)], *[flat2(a) for a in (*ws, *ms, *vs)])
    return [[out[k * n + a].reshape(ws[a].shape) for a in range(n)] for k in range(4)]


PROJ_ROWS = 512


def _project(h, w_t):
    S = h.shape[0]
    n_tiles = D_IN // SEG_TILE
    ranges = [(c0 // SEG_TILE, (c0 + width) // SEG_TILE) for _, c0, width, _ in SEGMENTS]

    def body(h_ref, w_ref, *outs):
        j = pl.program_id(0)
        for (j0, j1), (_, _, _, dt), o_ref in zip(ranges, SEGMENTS, outs):
            @pl.when(jnp.logical_and(j >= j0, j < j1))
            def _(o_ref=o_ref, dt=dt):
                for c in range(S // PROJ_ROWS):
                    rows = pl.ds(c * PROJ_ROWS, PROJ_ROWS)
                    o_ref[rows, :] = _dot(h_ref[rows, :], w_ref[...], NT).astype(dt)

    out_shapes, out_specs = [], []
    for (j0, j1), (name, _, width, dt) in zip(ranges, SEGMENTS):
        if name == "gl":
            per = (j1 - j0) // 3
            out_shapes.append(jax.ShapeDtypeStruct((3, S, width // 3), dt))
            out_specs.append(pl.BlockSpec((None, S, SEG_TILE), (lambda j, j0=j0, j1=j1, per=per: (
                jnp.clip(j - j0, 0, j1 - j0 - 1) // per, 0, jnp.clip(j - j0, 0, j1 - j0 - 1) % per))))
        else:
            out_shapes.append(jax.ShapeDtypeStruct((S, width), dt))
            out_specs.append(pl.BlockSpec((S, SEG_TILE), (lambda j, j0=j0, j1=j1: (0, jnp.clip(j - j0, 0, j1 - j0 - 1)))))
    outs = pl.pallas_call(
        body, name="proj", grid=(n_tiles,),
        in_specs=[pl.BlockSpec((S, D_MODEL), lambda j: (0, 0)), pl.BlockSpec((SEG_TILE, D_MODEL), lambda j: (j, 0))],
        out_specs=out_specs, out_shape=out_shapes,
        compiler_params=_params(("arbitrary",), large=True),
    )(h, w_t)
    return {name: o for (name, _, _, _), o in zip(SEGMENTS, outs)}


def _forward_a(x, mem, pre_g, mem_g, w_in, conv_w, conv_b, w_a, b_a, w_x, b_x, lam, sinks, rel_bias):
    S = x.shape[0]
    st = dict(T=min(512, S // 2), tm=min(512, S), bucket=_rel_bucket_map())
    st["h"] = _rms_fwd(x, pre_g, "pre_norm")
    st["memn"] = _rms_fwd(mem, mem_g, "mem_norm")
    seg = st["seg"] = _project(st["h"], w_in)
    st["h_rg"], st["y_rg"] = _rglru_fwd(seg["xr"], seg["g_rg"], conv_w, conv_b, w_a, b_a, w_x, b_x, lam, st["T"])
    st["o_swa"], st["y_swa"] = _swa_fwd(seg["q_s"], seg["kv"], seg["g_swa"], st["bucket"], rel_bias, _sink_column(sinks))
    return st


def _forward_b(st, x, target, post_g, w_memkv, wbr, w_out):
    S = x.shape[0]
    M = st["memn"].shape[0]
    seg = st["seg"]
    st["mkv"] = _matmul(st["memn"], w_memkv, "nn", M, 2 * D_MEM, D_MODEL, M, 512, D_MODEL, BF, "mem_kv")
    st["o_mem"], st["y_mem"] = _mem_fwd(seg["q_m"], st["mkv"], seg["g_mem"])
    st["ys"] = (st["y_rg"], st["y_swa"], st["y_mem"])
    st["merged"] = _merge_fwd(st["ys"], wbr, seg["gl"], st["tm"])
    st["dout"], st["dy"], st["loss"], st["dpost"] = _out_loss(st["merged"], w_out, x, target, post_g, min(256, S))
    return st


def _backward_a1(st, wbr, w_out):
    S = st["h"].shape[0]
    seg, ys, tm = st["seg"], st["ys"], st["tm"]
    st["dw_out"] = _matmul(st["merged"], st["dout"], "tn", D_MODEL, D_MODEL, S, 256, D_MODEL, S, BF, "dw_out", out_blocked="row")
    dgl0, dgl1, dgl2, dp0, dp1, dp2 = _merge_bwd(st["dout"], w_out, ys, wbr, seg["gl"], tm)
    st["dgl"] = (dgl0, dgl1, dgl2)
    dys, dwbr = [], []
    for i, dp in enumerate((dp0, dp1, dp2)):
        dys.append(_matmul(dp, wbr[i], "nn", S, 1024, D_MODEL, tm, 1024, D_MODEL, F32, "dy_br%d" % i))
        dwbr.append(_matmul(ys[i], dp, "tn", 1024, D_MODEL, S, 1024, 256, S, BF, "dw_br%d" % i, out_blocked="col"))
    st["dys"], st["dwbr"] = dys, dwbr
    return st


def _backward_a2(st, mem, w_memkv, conv_w, conv_b, w_a, b_a, w_x, b_x, lam):
    M = mem.shape[0]
    seg, dys = st["seg"], st["dys"]
    st["dq_m"], st["dg_mem"], dmkv = _mem_bwd(seg["q_m"], st["mkv"], seg["g_mem"], st["o_mem"], dys[2])
    st["dmkv"] = dmkv.astype(BF)
    st["dw_memkv"] = _matmul(st["memn"], st["dmkv"], "tn", D_MODEL, 2 * D_MEM, M, 256, 2 * D_MEM, M, BF, "dw_memkv", out_blocked="row")
    st["dxr"], st["dg_rg"], st["dw_a"], st["dw_x"], st["dvec"] = _rglru_bwd(
        seg["xr"], seg["g_rg"], st["h_rg"], dys[0], conv_w, conv_b, w_a, b_a, w_x, b_x, lam, st["T"])
    return st


def _mem_gain_grad(st, mem, w_memkv, dep=None):
    M = mem.shape[0]
    dmemn = _matmul(st["dmkv"], w_memkv, "nt", M, D_MODEL, 2 * D_MEM, M, 512, 2 * D_MEM, F32, "dmemn", dep=dep)
    return _rms_gain_grad(dmemn, mem, "dmem_gain")


def _backward_b(st, rel_bias, sinks):
    seg = st["seg"]
    others = {"xr": st["dxr"], "g_rg": st["dg_rg"], "q_m": st["dq_m"], "g_mem": st["dg_mem"], "gl": st["dgl"]}
    dproj, dkv, dsinks, drel = _swa_bwd(seg["q_s"], seg["kv"], seg["g_swa"], st["o_swa"], st["dys"][1],
                                        st["bucket"], rel_bias, _sink_column(sinks), others)
    st["dsinks"], st["drel"] = dsinks.reshape(1, SWA_HEADS), drel.reshape(REL_BUCKETS, SWA_HEADS)
    col_kv = [c0 for name, c0, _, _ in SEGMENTS if name == "kv"][0]
    st["dproj"] = lax.dynamic_update_slice(dproj, dkv.astype(BF), (0, col_kv))
    return st


def _dw_in_half(st, half, dep=None):
    S = st["h"].shape[0]
    dw = _matmul(st["dproj"], st["h"], "tn", D_IN, D_MODEL // 2, S, D_IN_TILE, D_MODEL // 2, S, BF, "dw_in%d" % half, b_noff=half, dep=dep)
    return dw.reshape(N_DEV, D_IN // N_DEV, D_MODEL // 2)


def _owner_blocks(a):
    return jnp.swapaxes(a.reshape((4, 2) + a.shape[1:]), 0, 1)


def _pad_rows(a, rows):
    a = a.reshape(-1, 128) if a.shape[-1] % 128 == 0 else jnp.pad(a, ((0, 0), (0, 128 - a.shape[-1])))
    return jnp.pad(a, ((0, rows - a.shape[0]), (0, 0))) if a.shape[0] < rows else a


def kernel(x, mem, pre_norm_g, post_norm_g, mem_norm_g, w_in, conv_w, conv_b, w_rg_a, b_rg_a, w_rg_x, b_rg_x, lru_lambda, swa_sinks, rel_bias, w_mem_kv, w_br_rg, w_br_swa, w_br_mem, w_out, loss_target, m_pre_norm_g, m_post_norm_g, m_mem_norm_g, m_w_in, m_conv_w, m_conv_b, m_w_rg_a, m_b_rg_a, m_w_rg_x, m_b_rg_x, m_lru_lambda, m_swa_sinks, m_rel_bias, m_w_mem_kv, m_w_br_rg, m_w_br_swa, m_w_br_mem, m_w_out, v_pre_norm_g, v_post_norm_g, v_mem_norm_g, v_w_in, v_conv_w, v_conv_b, v_w_rg_a, v_b_rg_a, v_w_rg_x, v_b_rg_x, v_lru_lambda, v_swa_sinks, v_rel_bias, v_w_mem_kv, v_w_br_rg, v_w_br_swa, v_w_br_mem, v_w_out):
    cx, cy, cc = lax.axis_index("x"), lax.axis_index("y"), lax.axis_index("c")
    me = 4 * cx + 2 * cy + cc
    chip = 2 * cx + cy
    core = jnp.reshape(cc, (1,)).astype(jnp.int32)
    x0, mem0 = x[0], mem[0]
    w_a_b, w_x_b = w_rg_a[0].astype(BF), w_rg_x[0].astype(BF)

    def landing(own, slot, slots):
        return lax.dynamic_update_slice(lax.empty((slots,) + own.shape, own.dtype), own[None], (slot,) + (0,) * own.ndim)


    def swap_start(parts, tag):
        return _exchange_start(parts, [lax.empty((4,) + p.shape[-2:], p.dtype) for p in parts], _plan_swap([p.ndim for p in parts]),
                               "swap_%s_start" % tag)

    def scatter_start(swap, after, tag, prefill=()):
        s_send, s_recv, parts, got, _ = swap
        got = _exchange_wait(s_send, s_recv, parts, got, _plan_swap([p.ndim for p in parts]), after, "swap_%s_wait" % tag)
        sums = [_pair_sum(p, g, core, "scatter_%s_sum%d" % (tag, i)) for i, (p, g) in enumerate(zip(parts, got))]
        lands = [landing(lax.dynamic_index_in_dim(s, chip, 0, keepdims=False), chip, 4) if i in prefill
                 else lax.empty(s.shape, s.dtype) for i, s in enumerate(sums)]
        return _exchange_start(sums, lands, _plan_scatter(len(sums)), "scatter_%s_start" % tag)

    def corner(a):
        return a.reshape(-1, a.shape[-1])[:8, :128]

    def zero_after(a):
        return jnp.minimum(jnp.abs(a.reshape(-1)[0].astype(F32)), 0.0)

    g_in, g_cw = _all_gather_relayed([jnp.transpose(w_in[0]).astype(BF), conv_w[0]], [True, False], "gather_w_in")
    w_in_f = g_in.reshape(D_IN, D_MODEL)
    conv_w_f = jnp.transpose(g_cw, (1, 0, 2)).reshape(CONV_W, D_RNN)

    after_first = zero_after(g_cw).astype(BF)
    rest = [w.astype(BF) + after_first for w in (w_mem_kv[0], jnp.transpose(w_br_rg[0]), jnp.transpose(w_br_swa[0]),
                                                 jnp.transpose(w_br_mem[0]), w_out[0])]
    plan_g = _plan_gather(len(rest))
    zones = _place_own([lax.empty((N_DEV,) + w.shape, w.dtype) for w in rest], rest, jnp.reshape(me, (1,)).astype(jnp.int32), "gather_rest_own")
    g_send, g_recv, g_src, g_land, g_token = _exchange_start(rest, zones, plan_g, "gather_rest_start")
    st = _forward_a(x0, mem0, pre_norm_g + g_token[0:1, 0:1], mem_norm_g, w_in_f, conv_w_f, conv_b, w_a_b, b_rg_a, w_x_b, b_rg_x,
                    lru_lambda, swa_sinks, rel_bias)
    g_land = _exchange_wait(g_send, g_recv, g_src, g_land, plan_g, st["y_swa"], "gather_rest_wait")
    g_land = _forward_to_sibling(g_land, "gather_rest_forward")
    w_memkv_f = g_land[0].reshape(D_MODEL, 2 * D_MEM)
    wbr = tuple(g_land[i].reshape(D_MODEL, D_RNN) for i in (1, 2, 3))
    w_out_f = g_land[4].reshape(D_MODEL, D_MODEL)

    st = _forward_b(st, x0, loss_target[0], post_norm_g, w_memkv_f, wbr, w_out_f)
    st = _backward_a1(st, wbr, w_out_f)
    parts_a = [st["dw_out"], st["dwbr"][0], st["dwbr"][1], st["dwbr"][2]]
    plan_a = _plan_scatter(len(parts_a))
    swap_a = swap_start(parts_a, "a")
    st = _backward_a2(st, mem0, w_memkv_f, conv_w_f, conv_b + swap_a[4][0:1, 0:1], w_a_b, b_rg_a, w_x_b, b_rg_x, lru_lambda)
    a_send, a_recv, a_src, a_land, a_token = scatter_start(swap_a, st["dxr"], "a")
    parts_c = [st["dw_memkv"], _owner_blocks(st["dw_a"]), _owner_blocks(st["dw_x"])]
    plan_c = _plan_scatter(len(parts_c))
    swap_c = swap_start(parts_c, "c")

    st = _backward_b(st, rel_bias, swa_sinks + swap_c[4][0:1, 0:1] + a_token[0:1, 0:1])
    c_send, c_recv, c_src, c_land, c_token = scatter_start(swap_c, st["dsinks"], "c", prefill=(1, 2))
    plan_b = _plan_scatter(1)

    def dw_in_parts(half, dep):
        dwh = _dw_in_half(st, half, dep)
        return dwh, [dwh]

    dw0, parts_b0 = dw_in_parts(0, c_token)
    swap_b0 = swap_start(parts_b0, "b0")
    a_land = _exchange_wait(a_send, a_recv, a_src, a_land, plan_a, swap_b0[4], "scatter_a_wait")
    big = [None] * 6

    chip1 = jnp.reshape(chip, (1,)).astype(jnp.int32)

    def adamw_big(j, land, own, wt, mt, vt):
        big[j] = _adamw(land, own, chip1, wt, mt, vt, "adamw_big%d" % j)

    adamw_big(5, a_land[0], a_src[0], w_out, m_w_out, v_w_out)
    adamw_big(2, a_land[1], a_src[1], w_br_rg, m_w_br_rg, v_w_br_rg)
    adamw_big(3, a_land[2], a_src[2], w_br_swa, m_w_br_swa, v_w_br_swa)
    halves = [scatter_start(swap_b0, corner(big[3][1]), "b0")]
    dw1, parts_b1 = dw_in_parts(1, halves[0][4])
    swap_b1 = swap_start(parts_b1, "b1")
    c_land = _exchange_wait(c_send, c_recv, c_src, c_land, plan_c, swap_b1[4], "scatter_c_wait")
    g_wa_blk = _sum_parts(c_land[1], "sum_w_rg_a")
    g_wx_blk = _sum_parts(c_land[2], "sum_w_rg_x")
    adamw_big(4, a_land[3], a_src[3], w_br_mem, m_w_br_mem, v_w_br_mem)
    adamw_big(1, c_land[0], c_src[0], w_mem_kv, m_w_mem_kv, v_w_mem_kv)
    halves.append(scatter_start(swap_b1, corner(big[1][1]), "b1"))
    st["dmem_g"] = _mem_gain_grad(st, mem0, w_memkv_f, halves[1][4])
    grad_x, dpre = _dh_dx(st["dproj"], w_in_f, x0, st["dy"], pre_norm_g + halves[1][4][0:1, 0:1], st["tm"], D_IN_TILE)
    pack = jnp.concatenate([dpre.reshape(16, 128), st["dpost"].reshape(16, 128), st["dmem_g"].reshape(16, 128),
                            st["dvec"].reshape(64, 128), _pad_rows(st["dsinks"], 8), _pad_rows(st["drel"], 32), g_wa_blk, g_wx_blk,
                            st["loss"]], axis=0)
    plan_s = _plan_everyone(1)
    s_send, s_recv, s_src, s_land, s_token = _exchange_start([pack], [landing(pack, me, N_DEV)], plan_s, "gather_small_start")
    swap_last = lambda a: jnp.transpose(a, (0, 2, 1))
    after, big0_t = s_token, None
    for half, (b_send, b_recv, b_src, b_land, _) in enumerate(halves):
        b_land = _exchange_wait(b_send, b_recv, b_src, b_land, plan_b, after, "scatter_b%d_wait" % half)[0]
        big0_t = _adamw(b_land, b_src[0], chip1, swap_last(w_in), swap_last(m_w_in), swap_last(v_w_in), "adamw_big0_%d" % half,
                        group=(half, 2), into=big0_t)
        after = corner(big0_t[1])
    big[0] = [swap_last(a) for a in big0_t]
    gathered = _exchange_wait(s_send, s_recv, s_src, s_land, plan_s, corner(big0_t[1]), "gather_small_wait")[0]
    gs = _sum_parts(gathered, "sum_small")
    loss_total = gs[408, 0]
    g_pre, g_post, g_memg = gs[0:16].reshape(1, D_MODEL), gs[16:32].reshape(1, D_MODEL), gs[32:48].reshape(1, D_MODEL)
    gvec = gs[48:112].reshape(8, D_RNN)
    g_conv_w = lax.dynamic_slice(gvec[0:CONV_W], (0, me * RNN_BLOCK), (CONV_W, RNN_BLOCK))
    g_conv_b, g_b_a, g_b_x, g_lam = gvec[4:5], gvec[5:6], gvec[6:7], gvec[7:8]
    g_sinks = gs[112:113, :SWA_HEADS]
    g_rel = gs[120:152, :SWA_HEADS]
    g_w_a = gathered[:, 152:280]
    g_w_x = gathered[:, 280:408]

    g_small = (g_pre, g_post, g_memg, g_conv_b, g_b_a, g_b_x, g_lam, g_w_a, g_w_x, g_sinks, g_rel, g_conv_w)
    w_small = (pre_norm_g, post_norm_g, mem_norm_g, conv_b, b_rg_a, b_rg_x, lru_lambda, w_rg_a, w_rg_x, swa_sinks, rel_bias, conv_w)
    m_small = (m_pre_norm_g, m_post_norm_g, m_mem_norm_g, m_conv_b, m_b_rg_a, m_b_rg_x, m_lru_lambda, m_w_rg_a, m_w_rg_x, m_swa_sinks, m_rel_bias, m_conv_w)
    v_small = (v_pre_norm_g, v_post_norm_g, v_mem_norm_g, v_conv_b, v_b_rg_a, v_b_rg_x, v_lru_lambda, v_w_rg_a, v_w_rg_x, v_swa_sinks, v_rel_bias, v_conv_w)
    sm = _adamw_small(g_small, w_small, m_small, v_small)


    def leaves(k):
        s = sm[k]
        return [s[0], s[1], s[2], big[0][k], s[11], s[3], s[7], s[4], s[8], s[5], s[6], s[9], s[10],
                big[1][k], big[2][k], big[3][k], big[4][k], big[5][k]]

    return (loss_total, grad_x[None], *leaves(0), *leaves(1), *leaves(2), *leaves(3))
```

```python
import math

import jax
import jax.numpy as jnp
import numpy as np
from jax import lax
from jax.experimental import pallas as pl
from jax.experimental.pallas import tpu as pltpu

F32, BF = jnp.float32, jnp.bfloat16
MESH = pl.DeviceIdType.MESH
N_DEV = 8

D_MODEL = 2048
D_RNN = 1024
RNN_BLOCKS = 8
RNN_BLOCK = 128
CONV_W = 4
LRU_C = 8.0
SWA_HEADS = 16
SWA_KV_HEADS = 2
SWA_HD = 64
WINDOW = 128
MEM_HEADS = 4
MEM_HD = 256
D_MEM = 1024
REL_BUCKETS = 32
REL_MAX_DIST = 128
EPS = 1e-6
NEG_INF = -1e30
D_IN = 12544
SEGMENTS = (("xr", 0, 1024, F32), ("g_rg", 1024, 1024, F32), ("q_s", 2048, 1024, BF), ("kv", 3072, 256, BF),
            ("g_swa", 3328, 1024, F32), ("q_m", 4352, 1024, BF), ("g_mem", 5376, 1024, F32), ("gl", 6400, 6144, F32))
SEG_TILE = 256
D_IN_TILE = 7 * SEG_TILE

ADAM_LR, ADAM_B1, ADAM_B2, ADAM_EPS, ADAM_WD, ADAM_STEP = 0.001, 0.9, 0.999, 1e-08, 0.01, 10

NN = (((1,), (0,)), ((), ()))
NT = (((1,), (1,)), ((), ()))
TN = (((0,), (0,)), ((), ()))
MIB = 2 ** 20


def _dot(a, b, dn):
    return lax.dot_general(a, b, dn, preferred_element_type=F32)


VMEM_LIMIT_MIB = 48
VMEM_LIMIT_LARGE_MIB = 56


def _params(sem=None, large=False):
    return pltpu.CompilerParams(dimension_semantics=sem, vmem_limit_bytes=(VMEM_LIMIT_LARGE_MIB if large else VMEM_LIMIT_MIB) * MIB)


def _sigmoid(z):
    return 1.0 / (1.0 + jnp.exp(-z))


def _softplus(z):
    return jnp.maximum(z, 0.0) + jnp.log(1.0 + jnp.exp(-jnp.abs(z)))


def _expm1(z):
    p = z * (1.0 + z * (0.5 + z * (1.0 / 6 + z * (1.0 / 24 + z * (1.0 / 120 + z * (1.0 / 720 + z * (1.0 / 5040 + z / 40320)))))))
    return jnp.where(jnp.abs(z) < 0.3, p, jnp.exp(z) - 1.0)


def _flat(p):
    return 4 * p[0] + 2 * p[1] + p[2]


def _all_gather_relayed(arrs, relay, name, side=None):
    n = len(arrs)
    K = 9
    work, side_ins, side_outs, side_scratch = side if side is not None else (None, [], [], [])
    n_in, n_out = n + len(side_ins), n + len(side_outs)

    def body(*refs):
        ins, outs = refs[:n], refs[n_in:n_in + n]
        send_sems, recv_sems, local_sems = refs[n_in + n_out:n_in + n_out + 3]
        x, y, c = lax.axis_index("x"), lax.axis_index("y"), lax.axis_index("c")
        me, sib = (x, y, c), (x, y, 1 - c)
        xn, yn, dg = (1 - x, y, c), (x, 1 - y, c), (1 - x, 1 - y, c)

        def other(p):
            return (p[0], p[1], 1 - p[2])

        def rows(a, half):
            h = arrs[a].shape[0] // 2
            return pl.ds(half * h, h)

        def copy(a, k, block, to, half=None, src=None):
            dst = outs[a].at[_flat(block)]
            if half is not None:
                dst = dst.at[rows(a, half)]
            return pltpu.make_async_remote_copy(src_ref=dst if src is None else src, dst_ref=dst,
                                                send_sem=send_sems.at[a * K + k], recv_sem=recv_sems.at[a * K + k],
                                                device_id=to, device_id_type=MESH)

        mine = [pltpu.make_async_copy(ins[a], outs[a].at[_flat(me)], local_sems.at[a]) for a in range(n)]
        for cp in mine:
            cp.start()
        sends = []

        def start(cp):
            cp.start()
            sends.append(cp)

        for a in range(n):
            start(copy(a, 1, me, xn, src=ins[a]))
            start(copy(a, 2, me, yn, src=ins[a]))
            if not relay[a]:
                start(copy(a, 3, me, dg, src=ins[a]))
            start(copy(a, 0, me, sib, src=ins[a]))
        if work is not None:
            work(refs[n:n_in], refs[n_in + n:n_in + n_out], refs[n_in + n_out + 3:])
        for a in range(n):
            copy(a, 1, xn, me).wait_recv()
            if relay[a]:
                start(copy(a, 3, xn, yn, half=0))
            start(copy(a, 5, xn, sib))
        for a in range(n):
            copy(a, 2, yn, me).wait_recv()
            if relay[a]:
                start(copy(a, 4, yn, xn, half=1))
            start(copy(a, 6, yn, sib))
        for a in range(n):
            if relay[a]:
                copy(a, 3, dg, me, half=0).wait_recv()
                start(copy(a, 7, dg, sib, half=0))
                copy(a, 4, dg, me, half=1).wait_recv()
                start(copy(a, 8, dg, sib, half=1))
            else:
                copy(a, 3, dg, me).wait_recv()
                start(copy(a, 7, dg, sib))
        for a in range(n):
            copy(a, 0, sib, me).wait_recv()
            copy(a, 5, other(xn), me).wait_recv()
            copy(a, 6, other(yn), me).wait_recv()
            if relay[a]:
                copy(a, 7, other(dg), me, half=0).wait_recv()
                copy(a, 8, other(dg), me, half=1).wait_recv()
            else:
                copy(a, 7, other(dg), me).wait_recv()
        for cp in sends:
            cp.wait_send()
        for cp in mine:
            cp.wait()

    any_spec = pl.BlockSpec(memory_space=pl.ANY)
    return pl.pallas_call(
        body, name=name,
        out_shape=[jax.ShapeDtypeStruct((N_DEV,) + a.shape, a.dtype) for a in arrs] + list(side_outs),
        in_specs=[any_spec] * n_in, out_specs=[any_spec] * n_out,
        scratch_shapes=[pltpu.SemaphoreType.DMA((K * n,)), pltpu.SemaphoreType.DMA((K * n,)), pltpu.SemaphoreType.DMA((n,))]
        + list(side_scratch),
        compiler_params=_params(),
    )(*arrs, *side_ins)


def _chip_peers(x, y):
    return [(1 - x, y), (x, 1 - y), (1 - x, 1 - y)]


def _chip(p):
    return 2 * p[0] + p[1]


def _plan_gather(n):
    def plan(x, y, c):
        out = []
        for a in range(n):
            for peer in [(x, y, 1 - c)] + [(*ch, c) for ch in _chip_peers(x, y)]:
                out.append((a, None, ("lead", _flat((x, y, c))), peer, ("lead", _flat(peer))))
        return out
    return plan


def _plan_everyone(n):
    def plan(x, y, c):
        out = []
        for a in range(n):
            for r in range(1, N_DEV):
                peer = (1 - x if r & 4 else x, 1 - y if r & 2 else y, 1 - c if r & 1 else c)
                out.append((a, None, ("lead", _flat((x, y, c))), peer, ("lead", _flat(peer))))
        return out
    return plan


def _plan_swap(ndims):
    def plan(x, y, c):
        out = []
        for a, nd in enumerate(ndims):
            if nd == 4:
                out.append((a, 1 - c, ("all", 0), (x, y, 1 - c), ("all", 0)))
            else:
                out += [(a, 2 * j + 1 - c, ("lead", j), (x, y, 1 - c), ("lead", j)) for j in range(4)]
        return out
    return plan


def _slot(ref, where):
    kind, k = where
    return ref if kind == "all" else ref.at[k]


def _plan_scatter(n):
    def plan(x, y, c):
        out = []
        for a in range(n):
            for ch in _chip_peers(x, y):
                out.append((a, _chip(ch), ("lead", _chip((x, y))), (*ch, c), ("lead", _chip(ch))))
        return out
    return plan


HBM_SPEC = pl.BlockSpec(memory_space=pltpu.HBM)
SEM_SPEC = pl.BlockSpec(memory_space=pltpu.SEMAPHORE)


def _in_hbm(a):
    return pltpu.with_memory_space_constraint(a, pltpu.HBM)


def _exchange_start(srcs, lands, plan, name):
    n = len(srcs)
    count = len(plan(0, 0, 0))

    def body(*refs):
        src_refs, land_refs = refs[:n], refs[n:2 * n]
        send_sems, recv_sems = refs[2 * n], refs[2 * n + 1]
        token = refs[-1]
        x, y, c = lax.axis_index("x"), lax.axis_index("y"), lax.axis_index("c")
        for k, (a, si, di, peer, _) in enumerate(plan(x, y, c)):
            src = src_refs[a] if si is None else src_refs[a].at[si]
            pltpu.make_async_remote_copy(src_ref=src, dst_ref=_slot(land_refs[a], di), send_sem=send_sems.at[k],
                                         recv_sem=recv_sems.at[k], device_id=peer, device_id_type=MESH).start()
        token[...] = jnp.zeros_like(token)

    out = pl.pallas_call(
        body, name=name,
        out_shape=(pltpu.SemaphoreType.DMA((count,)), pltpu.SemaphoreType.DMA((count,)),
                   *[pltpu.HBM(a.shape, a.dtype) for a in lands], jax.ShapeDtypeStruct((8, 128), F32)),
        in_specs=[HBM_SPEC] * (2 * n),
        out_specs=(SEM_SPEC, SEM_SPEC, *([HBM_SPEC] * n), pl.BlockSpec(memory_space=pltpu.VMEM)),
        input_output_aliases={n + i: 2 + i for i in range(n)},
        compiler_params=pltpu.CompilerParams(has_side_effects=pltpu.SideEffectType.DATAFLOW_SIDE_EFFECTING),
    )(*[_in_hbm(a) for a in srcs], *[_in_hbm(a) for a in lands])
    return out[0], out[1], list(srcs), list(out[2:2 + n]), out[-1]


def _exchange_wait(send_sems, recv_sems, srcs, lands, plan, after, name):
    n = len(srcs)

    def body(*refs):
        src_refs, land_refs = refs[:n], refs[n:2 * n]
        send_sems, recv_sems = refs[2 * n], refs[2 * n + 1]
        x, y, c = lax.axis_index("x"), lax.axis_index("y"), lax.axis_index("c")
        for k, (a, si, _, peer, ri) in enumerate(plan(x, y, c)):
            src = src_refs[a] if si is None else src_refs[a].at[si]
            cp = pltpu.make_async_remote_copy(src_ref=src, dst_ref=_slot(land_refs[a], ri), send_sem=send_sems.at[k],
                                              recv_sem=recv_sems.at[k], device_id=peer, device_id_type=MESH)
            cp.wait_send()
            cp.wait_recv()

    out = pl.pallas_call(
        body, name=name,
        out_shape=tuple(pltpu.HBM(a.shape, a.dtype) for a in lands),
        in_specs=[HBM_SPEC] * (2 * n) + [SEM_SPEC, SEM_SPEC, pl.BlockSpec(memory_space=pl.ANY)],
        out_specs=tuple([HBM_SPEC] * n),
        input_output_aliases={n + i: i for i in range(n)},
        compiler_params=pltpu.CompilerParams(has_side_effects=pltpu.SideEffectType.DATAFLOW_SIDE_EFFECTING),
    )(*[_in_hbm(a) for a in srcs], *lands, send_sems, recv_sems, after)
    return list(out)


def _forward_to_sibling(lands, name):
    n = len(lands)

    def body(*refs):
        in_refs, out_refs = refs[:n], refs[n:2 * n]
        send_sems, recv_sems = refs[2 * n:]
        x, y, c = lax.axis_index("x"), lax.axis_index("y"), lax.axis_index("c")
        sibling = (x, y, 1 - c)

        def copy(a, j, slot):
            return pltpu.make_async_remote_copy(src_ref=in_refs[a].at[slot], dst_ref=out_refs[a].at[slot],
                                                send_sem=send_sems.at[a * 3 + j], recv_sem=recv_sems.at[a * 3 + j],
                                                device_id=sibling, device_id_type=MESH)

        sends = [copy(a, j, _flat((*ch, c))) for a in range(n) for j, ch in enumerate(_chip_peers(x, y))]
        for cp in sends:
            cp.start()
        for a in range(n):
            for j, ch in enumerate(_chip_peers(x, y)):
                copy(a, j, _flat((*ch, 1 - c))).wait_recv()
        for cp in sends:
            cp.wait_send()

    any_spec = pl.BlockSpec(memory_space=pl.ANY)
    return pl.pallas_call(
        body, name=name, out_shape=[jax.ShapeDtypeStruct(a.shape, a.dtype) for a in lands],
        in_specs=[any_spec] * n, out_specs=[any_spec] * n, input_output_aliases={a: a for a in range(n)},
        scratch_shapes=[pltpu.SemaphoreType.DMA((3 * n,)), pltpu.SemaphoreType.DMA((3 * n,))],
    )(*lands)


def _place_own(zones, owns, slot, name):
    n = len(zones)

    def body(slot_ref, *refs):
        for a in range(n):
            refs[2 * n + a][...] = refs[a][...]

    return pl.pallas_call(
        body, name=name,
        grid_spec=pltpu.PrefetchScalarGridSpec(
            num_scalar_prefetch=1, grid=(1,),
            in_specs=[pl.BlockSpec(o.shape, lambda i, s_ref: (0, 0)) for o in owns] + [pl.BlockSpec(memory_space=pl.ANY)] * n,
            out_specs=[pl.BlockSpec((None,) + o.shape, lambda i, s_ref: (s_ref[0], 0, 0)) for o in owns]),
        out_shape=[jax.ShapeDtypeStruct(z.shape, z.dtype) for z in zones],
        input_output_aliases={1 + n + a: a for a in range(n)},
        compiler_params=_params(("arbitrary",)),
    )(slot, *owns, *zones)


def _pair_sum(parts, got, core, name):
    R, C = parts.shape[-2:]
    tr = 256 if R % 256 == 0 else R
    mine = (pl.BlockSpec((None, None, tr, C), lambda j, i, c_ref: (c_ref[0], j, i, 0)) if parts.ndim == 4
            else pl.BlockSpec((None, tr, C), lambda j, i, c_ref: (2 * j + c_ref[0], i, 0)))

    def body(c_ref, p_ref, g_ref, o_ref):
        o_ref[...] = (p_ref[...].astype(F32) + g_ref[...].astype(F32)).astype(o_ref.dtype)

    return pl.pallas_call(
        body, name=name,
        grid_spec=pltpu.PrefetchScalarGridSpec(
            num_scalar_prefetch=1, grid=(4, R // tr),
            in_specs=[mine, pl.BlockSpec((None, tr, C), lambda j, i, c_ref: (j, i, 0))],
            out_specs=pl.BlockSpec((None, tr, C), lambda j, i, c_ref: (j, i, 0))),
        out_shape=jax.ShapeDtypeStruct((4, R, C), parts.dtype),
        compiler_params=_params(("parallel", "parallel")),
    )(core, parts, got)


def _matmul(a, b, mode, M, N, K, tm, tn, tk, out_dtype, name, b_noff=0, out_blocked=None, dep=None):
    nm, nn, nk = M // tm, N // tn, K // tk
    if mode == "nn":
        a_spec = pl.BlockSpec((tm, tk), lambda j, i, k: (i, k))
        b_spec = pl.BlockSpec((tk, tn), lambda j, i, k: (k, j + b_noff))
        dn = NN
    elif mode == "nt":
        a_spec = pl.BlockSpec((tm, tk), lambda j, i, k: (i, k))
        b_spec = pl.BlockSpec((tn, tk), lambda j, i, k: (j + b_noff, k))
        dn = NT
    else:
        a_spec = pl.BlockSpec((tk, tm), lambda j, i, k: (k, i))
        b_spec = pl.BlockSpec((tk, tn), lambda j, i, k: (k, j + b_noff))
        dn = TN
    if out_blocked == "col":
        out_shape = jax.ShapeDtypeStruct((2, 4, M, tn), out_dtype)
        out_spec = pl.BlockSpec((None, None, tm, tn), lambda j, i, k: (j % 2, j // 2, i, 0))
    elif out_blocked == "row":
        out_shape = jax.ShapeDtypeStruct((2, 4, tm, N), out_dtype)
        out_spec = pl.BlockSpec((None, None, tm, tn), lambda j, i, k: (i % 2, i // 2, 0, j))
    else:
        out_shape = jax.ShapeDtypeStruct((M, N), out_dtype)
        out_spec = pl.BlockSpec((tm, tn), lambda j, i, k: (i, j))

    n_extra = int(dep is not None)

    def body(a_ref, b_ref, *rest):
        o_ref, scratch = rest[n_extra], rest[n_extra + 1:]
        if nk == 1:
            o_ref[...] = _dot(a_ref[...], b_ref[...], dn).astype(out_dtype)
        else:
            acc_ref, = scratch
            k = pl.program_id(2)

            @pl.when(k == 0)
            def _():
                acc_ref[...] = jnp.zeros_like(acc_ref)

            acc_ref[...] += _dot(a_ref[...], b_ref[...], dn)

            @pl.when(k == nk - 1)
            def _():
                o_ref[...] = acc_ref[...].astype(out_dtype)

    return pl.pallas_call(
        body, name=name, grid=(nn, nm, nk),
        in_specs=[a_spec, b_spec] + ([] if dep is None else [pl.BlockSpec((8, 128), lambda j, i, k: (0, 0))]),
        out_specs=out_spec, out_shape=out_shape,
        scratch_shapes=[] if nk == 1 else [pltpu.VMEM((tm, tn), F32)],
        compiler_params=_params(("parallel", "parallel", "arbitrary")),
    )(a, b, *([] if dep is None else [dep]))


RMS_SIDE_ROWS = 512


def _rms_side(pairs):
    chunks = [min(x.shape[0], RMS_SIDE_ROWS) for x, _ in pairs]

    def work(ins, outs, scratch):
        sem = scratch[-1]

        def move(src, dst):
            cp = pltpu.make_async_copy(src, dst, sem.at[0])
            cp.start()
            cp.wait()

        for p, ((x, _), tr) in enumerate(zip(pairs, chunks)):
            x_ref, g_ref, h_ref = ins[2 * p], ins[2 * p + 1], outs[p]
            xv, gv, hv = scratch[3 * p:3 * p + 3]
            move(g_ref, gv)
            for i in range(x.shape[0] // tr):
                rows = pl.ds(i * tr, tr)
                move(x_ref.at[rows], xv)
                v = xv[...]
                hv[...] = (v * lax.rsqrt(jnp.mean(v * v, axis=-1, keepdims=True) + EPS) * gv[...]).astype(BF)
                move(hv, h_ref.at[rows])

    scratch = [s for (x, g), tr in zip(pairs, chunks)
               for s in (pltpu.VMEM((tr, x.shape[1]), F32), pltpu.VMEM(g.shape, F32), pltpu.VMEM((tr, x.shape[1]), BF))]
    return (work, [a for pair in pairs for a in pair], [jax.ShapeDtypeStruct(x.shape, BF) for x, _ in pairs],
            scratch + [pltpu.SemaphoreType.DMA((1,))])


def _rms_gain_grad(dn, x, name):
    R, Dm = x.shape

    def body(dn_ref, x_ref, o_ref):
        xv = x_ref[...]
        r = lax.rsqrt(jnp.mean(xv * xv, axis=-1, keepdims=True) + EPS)
        o_ref[...] = jnp.sum(dn_ref[...] * xv * r, axis=0, keepdims=True)

    return pl.pallas_call(
        body, name=name, out_shape=jax.ShapeDtypeStruct((1, Dm), F32),
        compiler_params=_params(),
    )(dn, x)


def _shift_down(v, k, head8, row, T):
    if k == 0:
        return v
    r = pltpu.roll(v, k, 0)
    hr = pltpu.roll(head8, k, 0)
    top = jnp.where(row[:8] < k, hr, r[:8])
    return jnp.concatenate([top, r[8:]], axis=0)


def _shift_up(v, k, tail8, row, T):
    if k == 0:
        return v
    r = pltpu.roll(v, T - k, 0)
    tr = pltpu.roll(tail8, 8 - k, 0)
    bot = jnp.where(row[:8] >= 8 - k, tr, r[T - 8:])
    return jnp.concatenate([r[:T - 8], bot], axis=0)


def _rglru_gates(u, head8, grow, row, T, cw_ref, cb_ref, wa_ref, ba_ref, wx_ref, bx_ref, lam_ref):
    us = [_shift_down(u, k, head8, row, T) for k in range(CONV_W)]
    acc = us[0] * cw_ref[0:1, :]
    for k in range(1, CONV_W):
        acc = acc + us[k] * cw_ref[k:k + 1, :]
    conv = cb_ref[...] + acc
    cbf = conv.astype(BF)
    r_ = _sigmoid(_dot(cbf, wa_ref[0], NN) + ba_ref[...])
    i_ = _sigmoid(_dot(cbf, wx_ref[0], NN) + bx_ref[...])
    sp = _softplus(-lam_ref[...])
    la = -LRU_C * r_ * sp
    a = jnp.exp(la)
    mult_raw = jnp.sqrt(-_expm1(2.0 * la))
    mult = jnp.where(grow == 0, 1.0, mult_raw)
    return us, conv, cbf, r_, i_, sp, a, mult_raw, mult


def _rglru_specs(T, nt, rev):
    tmap = (lambda n, t: (nt - 1 - t, n)) if rev else (lambda n, t: (t, n))
    hmap = ((lambda n, t: (jnp.maximum((nt - 1 - t) * (T // 8) - 1, 0), n)) if rev
            else (lambda n, t: (jnp.maximum(t * (T // 8) - 1, 0), n)))
    tile = pl.BlockSpec((T, RNN_BLOCK), tmap)
    halo = pl.BlockSpec((8, RNN_BLOCK), hmap)
    vec = pl.BlockSpec((1, RNN_BLOCK), lambda n, t: (0, n))
    cw = pl.BlockSpec((CONV_W, RNN_BLOCK), lambda n, t: (0, n))
    wblk = pl.BlockSpec((1, RNN_BLOCK, RNN_BLOCK), lambda n, t: (n, 0, 0))
    return tile, halo, vec, cw, wblk


def _rglru_fwd(xr, g, cw, cb, wa, ba, wx, bx, lam, T):
    S = xr.shape[0]
    nt = S // T

    def body(u_ref, uh_ref, g_ref, cw_ref, cb_ref, wa_ref, ba_ref, wx_ref, bx_ref, lam_ref, h_ref, y_ref, carry):
        t = pl.program_id(1)

        @pl.when(t == 0)
        def _():
            carry[...] = jnp.zeros_like(carry)

        row = lax.broadcasted_iota(jnp.int32, (T, RNN_BLOCK), 0)
        grow = row + t * T
        head8 = jnp.where(t > 0, uh_ref[...], 0.0)
        _, conv, _, _, i_, _, a, _, mult = _rglru_gates(u_ref[...], head8, grow, row, T, cw_ref, cb_ref, wa_ref, ba_ref,
                                                         wx_ref, bx_ref, lam_ref)
        b = mult * i_ * conv
        s = 1
        while s < T:
            keep = row >= s
            a_s = jnp.where(keep, pltpu.roll(a, s, 0), 1.0)
            b_s = jnp.where(keep, pltpu.roll(b, s, 0), 0.0)
            b = a * b_s + b
            a = a * a_s
            s *= 2
        h = b + a * carry[0:1, :]
        carry[...] = jnp.broadcast_to(h[T - 1:T, :], carry.shape)
        h_ref[...] = h
        gv = g_ref[...]
        y_ref[...] = (h * (gv * _sigmoid(gv))).astype(BF)

    tile, halo, vec, cwspec, wblk = _rglru_specs(T, nt, False)
    return pl.pallas_call(
        body, name="rglru_fwd", grid=(RNN_BLOCKS, nt),
        in_specs=[tile, halo, tile, cwspec, vec, wblk, vec, wblk, vec, vec],
        out_specs=[tile, tile],
        out_shape=[jax.ShapeDtypeStruct((S, D_RNN), F32), jax.ShapeDtypeStruct((S, D_RNN), BF)],
        scratch_shapes=[pltpu.VMEM((8, RNN_BLOCK), F32)],
        compiler_params=_params(("parallel", "arbitrary")),
    )(xr, xr, g, cw, cb, wa, ba, wx, bx, lam)


def _rglru_bwd(xr, g, h, dy, cw, cb, wa, ba, wx, bx, lam, T):
    S = xr.shape[0]
    nt = S // T

    def body(u_ref, uh_ref, g_ref, h_ref, hh_ref, dy_ref, cw_ref, cb_ref, wa_ref, ba_ref, wx_ref, bx_ref, lam_ref,
             du_ref, dg_ref, dwa_ref, dwx_ref, dvec_ref, c_dhh, c_a, c_dconv):
        t = pl.program_id(1)
        tt = nt - 1 - t

        @pl.when(t == 0)
        def _():
            c_dhh[...] = jnp.zeros_like(c_dhh)
            c_a[...] = jnp.zeros_like(c_a)
            c_dconv[...] = jnp.zeros_like(c_dconv)
            dwa_ref[...] = jnp.zeros_like(dwa_ref)
            dwx_ref[...] = jnp.zeros_like(dwx_ref)
            dvec_ref[...] = jnp.zeros_like(dvec_ref)

        row = lax.broadcasted_iota(jnp.int32, (T, RNN_BLOCK), 0)
        row8 = row[:8]
        grow = row + tt * T
        head8 = jnp.where(tt > 0, uh_ref[...], 0.0)
        us, conv, cbf, r_, i_, sp, a, mult_raw, mult = _rglru_gates(
            u_ref[...], head8, grow, row, T, cw_ref, cb_ref, wa_ref, ba_ref, wx_ref, bx_ref, lam_ref)
        hv = h_ref[...]
        hprev = _shift_down(hv, 1, jnp.where(tt > 0, hh_ref[...], 0.0), row, T)
        gv = g_ref[...]
        sg = _sigmoid(gv)
        dyv = dy_ref[...]
        dg_ref[...] = (dyv * hv * (sg * (1.0 + gv * (1.0 - sg)))).astype(BF)
        d = dyv * (gv * sg)
        A = _shift_up(a, 1, c_a[...], row, T)
        s = 1
        while s < T:
            keep = row < T - s
            A_s = jnp.where(keep, pltpu.roll(A, T - s, 0), 1.0)
            d_s = jnp.where(keep, pltpu.roll(d, T - s, 0), 0.0)
            d = A * d_s + d
            A = A * A_s
            s *= 2
        dhh = d + A * c_dhh[0:1, :]
        da = dhh * hprev
        dconv = dhh * mult * i_
        di = dhh * mult * conv
        dmult = dhh * i_ * conv
        dla = da * a - jnp.where(grow == 0, 0.0, dmult * (a * a) / mult_raw)
        dr = dla * (-LRU_C * sp)
        dsp = jnp.sum(dla * (-LRU_C * r_), axis=0, keepdims=True)
        dza = dr * r_ * (1.0 - r_)
        dzx = di * i_ * (1.0 - i_)
        dza_b, dzx_b = dza.astype(BF), dzx.astype(BF)
        dconv = dconv + _dot(dza_b, wa_ref[0], NT) + _dot(dzx_b, wx_ref[0], NT)
        dwa_ref[0] += _dot(cbf, dza_b, TN)
        dwx_ref[0] += _dot(cbf, dzx_b, TN)
        lam = lam_ref[...]
        rows = [jnp.sum(dconv * us[k], axis=0, keepdims=True) for k in range(CONV_W)]
        rows += [jnp.sum(dconv, axis=0, keepdims=True), jnp.sum(dza, axis=0, keepdims=True),
                 jnp.sum(dzx, axis=0, keepdims=True), dsp * (-_sigmoid(-lam))]
        upd = jnp.zeros((8, RNN_BLOCK), F32)
        for j, rv in enumerate(rows):
            upd = upd + jnp.where(row8 == j, rv, 0.0)
        dvec_ref[...] += upd
        tail8 = c_dconv[...]
        du = dconv * cw_ref[0:1, :]
        for k in range(1, CONV_W):
            du = du + _shift_up(dconv, k, tail8, row, T) * cw_ref[k:k + 1, :]
        du_ref[...] = du.astype(BF)
        c_dhh[...] = jnp.broadcast_to(dhh[0:1, :], c_dhh.shape)
        c_a[...] = jnp.broadcast_to(a[0:1, :], c_a.shape)
        c_dconv[...] = dconv[:8]

    tile, halo, vec, cwspec, wblk = _rglru_specs(T, nt, True)
    acc8 = pl.BlockSpec((8, RNN_BLOCK), lambda n, t: (0, n))
    return pl.pallas_call(
        body, name="rglru_bwd", grid=(RNN_BLOCKS, nt),
        in_specs=[tile, halo, tile, tile, halo, tile, cwspec, vec, wblk, vec, wblk, vec, vec],
        out_specs=[tile, tile, wblk, wblk, acc8],
        out_shape=[jax.ShapeDtypeStruct((S, D_RNN), BF), jax.ShapeDtypeStruct((S, D_RNN), BF),
                   jax.ShapeDtypeStruct((RNN_BLOCKS, RNN_BLOCK, RNN_BLOCK), F32),
                   jax.ShapeDtypeStruct((RNN_BLOCKS, RNN_BLOCK, RNN_BLOCK), F32),
                   jax.ShapeDtypeStruct((8, D_RNN), F32)],
        scratch_shapes=[pltpu.VMEM((8, RNN_BLOCK), F32)] * 3,
        compiler_params=_params(("parallel", "arbitrary")),
    )(xr, xr, g, h, h, dy, cw, cb, wa, ba, wx, bx, lam)


def _rel_bucket_map():
    qi = np.arange(WINDOW)[:, None]
    kj = np.arange(2 * WINDOW)[None, :]
    dist = jnp.asarray(qi + WINDOW - kj, jnp.int32)
    n = jnp.maximum(dist, 0)
    max_exact = REL_BUCKETS // 2
    ratio = jnp.log(jnp.maximum(n, 1).astype(F32) / max_exact) / math.log(REL_MAX_DIST / max_exact)
    large = jnp.minimum(max_exact + (ratio * (REL_BUCKETS - max_exact)).astype(jnp.int32), REL_BUCKETS - 1)
    bucket = jnp.where(n < max_exact, n, large).astype(jnp.int32)
    j = np.arange(WINDOW)[None, :]
    return jnp.where(jnp.asarray(j > qi), bucket[:, :WINDOW], bucket[:, WINDOW:])


def _swa_common(n, kv_ref, bucket_ref, relb_ref, bias_scr):
    @pl.when(n == 0)
    def _():
        bk = bucket_ref[...]
        for h in range(SWA_HEADS):
            acc = jnp.zeros((WINDOW, WINDOW), F32)
            for b in range(REL_BUCKETS):
                acc = acc + jnp.where(bk == b, relb_ref[b, h], 0.0)
            bias_scr[h] = acc

    prev0 = pl.multiple_of(jnp.maximum(n - 1, 0) * WINDOW, WINDOW)
    cur0 = pl.multiple_of(n * WINDOW, WINDOW)
    kk = jnp.concatenate([kv_ref[pl.ds(prev0, WINDOW), :], kv_ref[pl.ds(cur0, WINDOW), :]], axis=0).astype(F32)
    rowi = lax.broadcasted_iota(jnp.int32, (WINDOW, WINDOW), 0)
    col = lax.broadcasted_iota(jnp.int32, (WINDOW, WINDOW), 1)
    from_prev = col > rowi
    return kk, from_prev, prev0, cur0


def _fold(full, from_prev):
    return jnp.where(from_prev, full[:, :WINDOW], full[:, WINDOW:])


def _unfold(sq, from_prev):
    return jnp.concatenate([jnp.where(from_prev, sq, 0.0), jnp.where(from_prev, 0.0, sq)], axis=1)


def _half_pair(part, kvh):
    lo = lax.broadcasted_iota(jnp.int32, part.shape, 1) < SWA_HD
    if kvh == 0:
        pa = jnp.where(lo, part, 0.0)
        pb = pltpu.roll(pa, SWA_HD, 1)
    else:
        pb = jnp.where(lo, 0.0, part)
        pa = pltpu.roll(pb, SWA_HD, 1)
    return pa.astype(BF), pb.astype(BF)


ALL_HEADS = SWA_HEADS * WINDOW


def _sink_column(sinks):
    return jnp.repeat(sinks.reshape(SWA_HEADS), WINDOW).reshape(ALL_HEADS, 1)


def _swa_operands(kk):
    return [(_half_pair(kk[:, :128], kvh), _half_pair(kk[:, 128:], kvh)) for kvh in range(SWA_KV_HEADS)]


def _swa_probs(n, q_ref, ops, bias_scr, sinkc_ref, from_prev):
    lgs = []
    for kvh in range(SWA_KV_HEADS):
        (ka, kb), _ = ops[kvh]
        for p in range(4):
            q2 = q_ref[:, kvh * 512 + p * 128:kvh * 512 + p * 128 + 128]
            lgs += [_fold(_dot(q2, ka, NT), from_prev), _fold(_dot(q2, kb, NT), from_prev)]
    lg = jnp.concatenate(lgs, axis=0) * (SWA_HD ** -0.5) + bias_scr[...].reshape(ALL_HEADS, WINDOW)
    rowi = jnp.bitwise_and(lax.broadcasted_iota(jnp.int32, (ALL_HEADS, WINDOW), 0), WINDOW - 1)
    col = lax.broadcasted_iota(jnp.int32, (ALL_HEADS, WINDOW), 1)
    no_prev = jnp.where(n > 0, 0, 4 * WINDOW)
    lg = jnp.where(jnp.logical_or(col <= rowi, col > rowi + no_prev), lg, NEG_INF)
    sink = sinkc_ref[...]
    m = jnp.maximum(jnp.max(lg, axis=-1, keepdims=True), sink)
    e = jnp.exp(lg - m)
    es = jnp.exp(sink - m)
    den = jnp.sum(e, axis=-1, keepdims=True) + es
    return e / den, es / den


def _swa_fwd(q, kv, g, bucket, rel_bias, sink_col):
    S = q.shape[0]
    nb = S // WINDOW

    def body(q_ref, kv_ref, g_ref, bucket_ref, relb_ref, sinkc_ref, o_ref, y_ref, bias_scr):
        n = pl.program_id(0)
        kk, from_prev, _, _ = _swa_common(n, kv_ref, bucket_ref, relb_ref, bias_scr)
        ops = _swa_operands(kk)
        pr, _ = _swa_probs(n, q_ref, ops, bias_scr, sinkc_ref, from_prev)
        for kvh in range(SWA_KV_HEADS):
            _, (va, vb) = ops[kvh]
            for p in range(4):
                c0 = kvh * 512 + p * 128
                r0 = (kvh * 8 + 2 * p) * WINDOW
                o2 = (_dot(_unfold(pr[r0:r0 + WINDOW], from_prev).astype(BF), va, NN)
                      + _dot(_unfold(pr[r0 + WINDOW:r0 + 2 * WINDOW], from_prev).astype(BF), vb, NN))
                o_ref[:, c0:c0 + 128] = o2
                gv = g_ref[:, c0:c0 + 128]
                y_ref[:, c0:c0 + 128] = (o2 * (gv * _sigmoid(gv))).astype(BF)

    blk = pl.BlockSpec((WINDOW, 1024), lambda n: (n, 0))
    smem = pl.BlockSpec(memory_space=pltpu.SMEM)
    sinkc = pl.BlockSpec((ALL_HEADS, 1), lambda n: (0, 0))
    return pl.pallas_call(
        body, name="swa_fwd", grid=(nb,),
        in_specs=[blk, pl.BlockSpec((S, 256), lambda n: (0, 0)), blk, pl.BlockSpec((WINDOW, WINDOW), lambda n: (0, 0)), smem, sinkc],
        out_specs=[blk, blk],
        out_shape=[jax.ShapeDtypeStruct((S, 1024), F32), jax.ShapeDtypeStruct((S, 1024), BF)],
        scratch_shapes=[pltpu.VMEM((SWA_HEADS, WINDOW, WINDOW), F32)],
        compiler_params=_params(("arbitrary",)),
    )(q, kv, g, bucket, rel_bias, sink_col)


def _swa_bwd(q, kv, g, o, dy, bucket, rel_bias, sink_col, others):
    S = q.shape[0]
    nb = S // WINDOW
    first_col = {name: c0 for name, c0, _, _ in SEGMENTS}
    col_q, col_g = first_col["q_s"], first_col["g_swa"]
    copies = []
    for name, c0, width, _ in SEGMENTS:
        if name not in ("q_s", "kv", "g_swa"):
            arrs = others[name] if name == "gl" else (others[name],)
            copies += [(a, c0 + i * (width // len(arrs))) for i, a in enumerate(arrs)]

    def body(q_ref, kv_ref, g_ref, o_ref, dy_ref, bucket_ref, relb_ref, sinkc_ref, *refs):
        copy_refs = refs[:len(copies)]
        dp_ref, dkv_ref, dsink_ref, drel_ref, bias_scr, dbias_scr, dsink_scr = refs[len(copies):]
        n = pl.program_id(0)
        for c_ref, (a, c0) in zip(copy_refs, copies):
            dp_ref[:, c0:c0 + a.shape[1]] = c_ref[...]

        @pl.when(n == 0)
        def _():
            dbias_scr[...] = jnp.zeros_like(dbias_scr)
            dsink_scr[...] = jnp.zeros_like(dsink_scr)
            dkv_ref[...] = jnp.zeros_like(dkv_ref)

        kk, from_prev, prev0, cur0 = _swa_common(n, kv_ref, bucket_ref, relb_ref, bias_scr)
        ops = _swa_operands(kk)
        pr, ps = _swa_probs(n, q_ref, ops, bias_scr, sinkc_ref, from_prev)
        do2s, dps = [], []
        for kvh in range(SWA_KV_HEADS):
            _, (va, vb) = ops[kvh]
            for p in range(4):
                c0 = kvh * 512 + p * 128
                gv = g_ref[:, c0:c0 + 128]
                sg = _sigmoid(gv)
                dyv = dy_ref[:, c0:c0 + 128]
                dp_ref[:, col_g + c0:col_g + c0 + 128] = (dyv * o_ref[:, c0:c0 + 128] * (sg * (1.0 + gv * (1.0 - sg)))).astype(BF)
                do2 = (dyv * (gv * sg)).astype(BF)
                do2s.append(do2)
                dps += [_fold(_dot(do2, va, NT), from_prev), _fold(_dot(do2, vb, NT), from_prev)]
        dp = jnp.concatenate(dps, axis=0)
        delta = jnp.sum(pr * dp, axis=-1, keepdims=True)
        ds = pr * (dp - delta)
        dbias_scr[...] += ds.reshape(SWA_HEADS, WINDOW, WINDOW)
        dsink_scr[...] += ps * delta
        dsc = ds * (SWA_HD ** -0.5)
        lo256 = lax.broadcasted_iota(jnp.int32, (2 * WINDOW, 128), 1) < SWA_HD
        dks, dvs = [], []
        for kvh in range(SWA_KV_HEADS):
            (ka, kb), _ = ops[kvh]
            dka = jnp.zeros((2 * WINDOW, 128), F32)
            dkb, dva, dvb = dka, dka, dka
            for p in range(4):
                c0 = kvh * 512 + p * 128
                r0 = (kvh * 8 + 2 * p) * WINDOW
                q2 = q_ref[:, c0:c0 + 128]
                do2 = do2s[kvh * 4 + p]
                ds0 = _unfold(dsc[r0:r0 + WINDOW], from_prev).astype(BF)
                ds1 = _unfold(dsc[r0 + WINDOW:r0 + 2 * WINDOW], from_prev).astype(BF)
                dp_ref[:, col_q + c0:col_q + c0 + 128] = (_dot(ds0, ka, NN) + _dot(ds1, kb, NN)).astype(BF)
                dka = dka + _dot(ds0, q2, TN)
                dkb = dkb + _dot(ds1, q2, TN)
                dva = dva + _dot(_unfold(pr[r0:r0 + WINDOW], from_prev).astype(BF), do2, TN)
                dvb = dvb + _dot(_unfold(pr[r0 + WINDOW:r0 + 2 * WINDOW], from_prev).astype(BF), do2, TN)
            dks.append(jnp.where(lo256, dka, 0.0) + pltpu.roll(jnp.where(lo256, 0.0, dkb), SWA_HD, 1))
            dvs.append(jnp.where(lo256, dva, 0.0) + pltpu.roll(jnp.where(lo256, 0.0, dvb), SWA_HD, 1))
        dk = dks[0] + pltpu.roll(dks[1], SWA_HD, 1)
        dv = dvs[0] + pltpu.roll(dvs[1], SWA_HD, 1)
        dkv_ref[pl.ds(prev0, WINDOW), 0:128] += dk[:WINDOW]
        dkv_ref[pl.ds(prev0, WINDOW), 128:256] += dv[:WINDOW]
        dkv_ref[pl.ds(cur0, WINDOW), 0:128] += dk[WINDOW:]
        dkv_ref[pl.ds(cur0, WINDOW), 128:256] += dv[WINDOW:]

        @pl.when(n == nb - 1)
        def _():
            dsink_ref[...] = -jnp.sum(dsink_scr[...].reshape(SWA_HEADS, WINDOW, 1), axis=1)
            bk = bucket_ref[...]
            sums = []
            for b in range(REL_BUCKETS):
                sums.append(jnp.sum(jnp.where((bk == b)[None], dbias_scr[...], 0.0), axis=1))
            drel_ref[...] = jnp.sum(jnp.concatenate(sums, axis=0), axis=1, keepdims=True)

    blk = pl.BlockSpec((WINDOW, 1024), lambda n: (n, 0))
    smem = pl.BlockSpec(memory_space=pltpu.SMEM)
    whole = lambda shape: pl.BlockSpec(shape, lambda n: (0, 0))
    return pl.pallas_call(
        body, name="swa_bwd", grid=(nb,),
        in_specs=[blk, whole((S, 256)), blk, blk, blk, whole((WINDOW, WINDOW)), smem, whole((ALL_HEADS, 1))]
        + [pl.BlockSpec((WINDOW, a.shape[1]), lambda n: (n, 0)) for a, _ in copies],
        out_specs=[pl.BlockSpec((WINDOW, D_IN), lambda n: (n, 0)), whole((S, 256)), whole((SWA_HEADS, 1)),
                   whole((REL_BUCKETS * SWA_HEADS, 1))],
        out_shape=[jax.ShapeDtypeStruct((S, D_IN), BF), jax.ShapeDtypeStruct((S, 256), F32), jax.ShapeDtypeStruct((SWA_HEADS, 1), F32),
                   jax.ShapeDtypeStruct((REL_BUCKETS * SWA_HEADS, 1), F32)],
        scratch_shapes=[pltpu.VMEM((SWA_HEADS, WINDOW, WINDOW), F32), pltpu.VMEM((SWA_HEADS, WINDOW, WINDOW), F32),
                        pltpu.VMEM((ALL_HEADS, 1), F32)],
        compiler_params=_params(("arbitrary",)),
    )(q, kv, g, o, dy, bucket, rel_bias, sink_col, *[a for a, _ in copies])


def _mem_probs(qh, mk):
    lg = _dot(qh, mk, NT) * (MEM_HD ** -0.5)
    e = jnp.exp(lg - jnp.max(lg, axis=-1, keepdims=True))
    return e / jnp.sum(e, axis=-1, keepdims=True)


def _mem_fwd(q, mkv, g):
    S = q.shape[0]
    M = mkv.shape[0]
    tq = 256

    def body(q_ref, mkv_ref, g_ref, o_ref, y_ref):
        for h in range(MEM_HEADS):
            c0 = h * MEM_HD
            pr = _mem_probs(q_ref[:, c0:c0 + MEM_HD], mkv_ref[:, c0:c0 + MEM_HD])
            o = _dot(pr.astype(BF), mkv_ref[:, D_MEM + c0:D_MEM + c0 + MEM_HD], NN)
            o_ref[:, c0:c0 + MEM_HD] = o
            gv = g_ref[:, c0:c0 + MEM_HD]
            y_ref[:, c0:c0 + MEM_HD] = (o * (gv * _sigmoid(gv))).astype(BF)

    blk = pl.BlockSpec((tq, D_MEM), lambda i: (i, 0))
    return pl.pallas_call(
        body, name="mem_fwd", grid=(S // tq,),
        in_specs=[blk, pl.BlockSpec((M, 2 * D_MEM), lambda i: (0, 0)), blk], out_specs=[blk, blk],
        out_shape=[jax.ShapeDtypeStruct((S, D_MEM), F32), jax.ShapeDtypeStruct((S, D_MEM), BF)],
        compiler_params=_params(("parallel",)),
    )(q, mkv, g)


def _mem_bwd(q, mkv, g, o, dy):
    S = q.shape[0]
    M = mkv.shape[0]
    tq = 256

    def body(q_ref, mkv_ref, g_ref, o_ref, dy_ref, dq_ref, dg_ref, dmkv_ref):
        @pl.when(pl.program_id(0) == 0)
        def _():
            dmkv_ref[...] = jnp.zeros_like(dmkv_ref)

        for h in range(MEM_HEADS):
            c0 = h * MEM_HD
            qh = q_ref[:, c0:c0 + MEM_HD]
            mk = mkv_ref[:, c0:c0 + MEM_HD]
            mv = mkv_ref[:, D_MEM + c0:D_MEM + c0 + MEM_HD]
            gv = g_ref[:, c0:c0 + MEM_HD]
            sg = _sigmoid(gv)
            dyv = dy_ref[:, c0:c0 + MEM_HD]
            dg_ref[:, c0:c0 + MEM_HD] = (dyv * o_ref[:, c0:c0 + MEM_HD] * (sg * (1.0 + gv * (1.0 - sg)))).astype(BF)
            do = (dyv * (gv * sg)).astype(BF)
            pr = _mem_probs(qh, mk)
            dp = _dot(do, mv, NT)
            ds = pr * (dp - jnp.sum(pr * dp, axis=-1, keepdims=True))
            dsb = (ds * (MEM_HD ** -0.5)).astype(BF)
            dq_ref[:, c0:c0 + MEM_HD] = _dot(dsb, mk, NN).astype(BF)
            dmkv_ref[:, c0:c0 + MEM_HD] += _dot(dsb, qh, TN)
            dmkv_ref[:, D_MEM + c0:D_MEM + c0 + MEM_HD] += _dot(pr.astype(BF), do, TN)

    blk = pl.BlockSpec((tq, D_MEM), lambda i: (i, 0))
    whole = pl.BlockSpec((M, 2 * D_MEM), lambda i: (0, 0))
    return pl.pallas_call(
        body, name="mem_bwd", grid=(S // tq,),
        in_specs=[blk, whole, blk, blk, blk], out_specs=[blk, blk, whole],
        out_shape=[jax.ShapeDtypeStruct((S, D_MEM), BF), jax.ShapeDtypeStruct((S, D_MEM), BF),
                   jax.ShapeDtypeStruct((M, 2 * D_MEM), F32)],
        compiler_params=_params(("arbitrary",)),
    )(q, mkv, g, o, dy)


MERGE_TN = 512


def _merge_specs(tm):
    ytile = pl.BlockSpec((tm, 1024), lambda i, j: (i, 0))
    wblk = pl.BlockSpec((MERGE_TN, 1024), lambda i, j: (j, 0))
    gls = [pl.BlockSpec((None, tm, MERGE_TN), (lambda i, j, br=br: (br, i, j))) for br in range(3)]
    otile = pl.BlockSpec((tm, MERGE_TN), lambda i, j: (i, j))
    return ytile, wblk, gls, otile


def _merge_fwd(ys, ws, gl, tm):
    S = gl.shape[1]

    def body(y0, y1, y2, w0, w1, w2, g0, g1, g2, o_ref):
        acc = None
        for y_ref, w_ref, g_ref in ((y0, w0, g0), (y1, w1, g1), (y2, w2, g2)):
            term = _sigmoid(g_ref[...]) * _dot(y_ref[...], w_ref[...], NT)
            acc = term if acc is None else acc + term
        o_ref[...] = acc.astype(BF)

    ytile, wblk, gls, otile = _merge_specs(tm)
    return pl.pallas_call(
        body, name="merge_fwd", grid=(S // tm, D_MODEL // MERGE_TN),
        in_specs=[ytile] * 3 + [wblk] * 3 + gls, out_specs=otile,
        out_shape=jax.ShapeDtypeStruct((S, D_MODEL), BF),
        compiler_params=_params(("parallel", "arbitrary")),
    )(*ys, *ws, gl, gl, gl)


def _merge_bwd(dout, w_out, ys, ws, gl, tm):
    S = gl.shape[1]

    def body(do_ref, wo_ref, y0, y1, y2, w0, w1, w2, g0, g1, g2, dg0, dg1, dg2, dp0, dp1, dp2):
        dm = _dot(do_ref[...], wo_ref[...], NT)
        for y_ref, w_ref, g_ref, dg_ref, dp_ref in ((y0, w0, g0, dg0, dp0), (y1, w1, g1, dg1, dp1), (y2, w2, g2, dg2, dp2)):
            gate = _sigmoid(g_ref[...])
            pv = _dot(y_ref[...], w_ref[...], NT)
            dg_ref[...] = (dm * pv * gate * (1.0 - gate)).astype(BF)
            dp_ref[...] = (dm * gate).astype(BF)

    ytile, wblk, gls, otile = _merge_specs(tm)
    out = jax.ShapeDtypeStruct((S, D_MODEL), BF)
    return pl.pallas_call(
        body, name="merge_bwd", grid=(S // tm, D_MODEL // MERGE_TN),
        in_specs=[pl.BlockSpec((tm, D_MODEL), lambda i, j: (i, 0)), pl.BlockSpec((MERGE_TN, D_MODEL), lambda i, j: (j, 0))]
        + [ytile] * 3 + [wblk] * 3 + gls,
        out_specs=[otile] * 6, out_shape=[out] * 6,
        compiler_params=_params(("parallel", "arbitrary")),
    )(dout, w_out, *ys, *ws, gl, gl, gl)


def _out_loss(merged, w_out, x, target, post_g, tm):
    S = x.shape[0]

    def body(m_ref, w_ref, x_ref, t_ref, g_ref, dout_ref, dy_ref, loss_ref, dpost_ref):
        @pl.when(pl.program_id(0) == 0)
        def _():
            loss_ref[...] = jnp.zeros_like(loss_ref)
            dpost_ref[...] = jnp.zeros_like(dpost_ref)

        out = _dot(m_ref[...], w_ref[...], NN)
        r = lax.rsqrt(jnp.mean(out * out, axis=-1, keepdims=True) + EPS)
        nrm = out * r
        gv = g_ref[...]
        err = (x_ref[...] + nrm * gv) - t_ref[...]
        sq = jnp.sum(jnp.sum(err * err, axis=1, keepdims=True), axis=0, keepdims=True)
        loss_ref[...] += sq * (0.5 / D_MODEL)
        dy = err * (1.0 / D_MODEL)
        dy_ref[...] = dy
        dpost_ref[...] += jnp.sum(dy * nrm, axis=0, keepdims=True)
        dn = dy * gv
        dout_ref[...] = (r * (dn - nrm * jnp.mean(dn * nrm, axis=-1, keepdims=True))).astype(BF)

    row = pl.BlockSpec((tm, D_MODEL), lambda i: (i, 0))
    return pl.pallas_call(
        body, name="out_loss", grid=(S // tm,),
        in_specs=[row, pl.BlockSpec((D_MODEL, D_MODEL), lambda i: (0, 0)), row, row, pl.BlockSpec((1, D_MODEL), lambda i: (0, 0))],
        out_specs=[row, row, pl.BlockSpec((8, 128), lambda i: (0, 0)), pl.BlockSpec((1, D_MODEL), lambda i: (0, 0))],
        out_shape=[jax.ShapeDtypeStruct((S, D_MODEL), BF), jax.ShapeDtypeStruct((S, D_MODEL), F32),
                   jax.ShapeDtypeStruct((8, 128), F32), jax.ShapeDtypeStruct((1, D_MODEL), F32)],
        compiler_params=_params(("arbitrary",)),
    )(merged, w_out, x, target, post_g)


DH_DX_CHUNK = 64


def _dh_dx(dproj, w_in, x, dy, pre_g, tm, tk):
    S = x.shape[0]
    nk = D_IN // tk

    def body(dp_ref, w_ref, x_ref, dy_ref, g_ref, dx_ref, dpre_ref, acc_ref):
        i, k = pl.program_id(0), pl.program_id(1)

        @pl.when(jnp.logical_and(i == 0, k == 0))
        def _():
            dpre_ref[...] = jnp.zeros_like(dpre_ref)

        @pl.when(k == 0)
        def _():
            acc_ref[...] = jnp.zeros_like(acc_ref)

        acc_ref[...] += _dot(dp_ref[...], w_ref[...], NN)

        @pl.when(k == nk - 1)
        def _():
            def chunk(c, carry):
                rows = pl.ds(pl.multiple_of(c * DH_DX_CHUNK, DH_DX_CHUNK), DH_DX_CHUNK)
                dh = acc_ref[rows, :]
                xv = x_ref[rows, :]
                r = lax.rsqrt(jnp.mean(xv * xv, axis=-1, keepdims=True) + EPS)
                nrm = xv * r
                dpre_ref[...] += jnp.sum(dh * nrm, axis=0, keepdims=True)
                dn = dh * g_ref[...]
                dx_ref[rows, :] = r * (dn - nrm * jnp.mean(dn * nrm, axis=-1, keepdims=True)) + dy_ref[rows, :]
                return carry
            lax.fori_loop(0, tm // DH_DX_CHUNK, chunk, 0)

    row = pl.BlockSpec((tm, D_MODEL), lambda i, k: (i, 0))
    vec = pl.BlockSpec((1, D_MODEL), lambda i, k: (0, 0))
    return pl.pallas_call(
        body, name="dh_dx", grid=(S // tm, nk),
        in_specs=[pl.BlockSpec((tm, tk), lambda i, k: (i, k)), pl.BlockSpec((tk, D_MODEL), lambda i, k: (k, 0)), row, row, vec],
        out_specs=[row, vec],
        out_shape=[jax.ShapeDtypeStruct((S, D_MODEL), F32), jax.ShapeDtypeStruct((1, D_MODEL), F32)],
        scratch_shapes=[pltpu.VMEM((tm, D_MODEL), F32)],
        compiler_params=_params(("arbitrary", "arbitrary"), large=True),
    )(dproj, w_in, x, dy, pre_g)


def _sum_parts(parts, name):
    P, R, C = parts.shape
    tr = max(t for t in range(8, 513, 8) if R % t == 0)

    def body(p_ref, o_ref):
        acc = p_ref[0]
        for j in range(1, P):
            acc = acc + p_ref[j]
        o_ref[...] = acc

    return pl.pallas_call(
        body, name=name, grid=(R // tr,),
        in_specs=[pl.BlockSpec((P, tr, C), lambda i: (0, i, 0))], out_specs=pl.BlockSpec((tr, C), lambda i: (i, 0)),
        out_shape=jax.ShapeDtypeStruct((R, C), F32), compiler_params=_params(("parallel",)),
    )(parts)


def _adamw(land, sums, chip, w, m, v, name, group=(0, 1), into=None):
    q, n_groups = group
    _, R, cols = land.shape
    C = cols * n_groups
    tr = max(t for t in range(16, 257, 16) if R % t == 0)
    c1 = 1.0 - ADAM_B1 ** ADAM_STEP
    c2 = 1.0 - ADAM_B2 ** ADAM_STEP
    n_into = 0 if into is None else 4

    def body(chip_ref, p0_ref, p1_ref, p2_ref, own_ref, w_ref, m_ref, v_ref, *refs):
        g_ref, d_ref, nm_ref, nv_ref = refs[n_into:]
        g = own_ref[...].astype(F32)
        for p_ref in (p0_ref, p1_ref, p2_ref):
            g = g + p_ref[...].astype(F32)
        nm = ADAM_B1 * m_ref[...] + (1.0 - ADAM_B1) * g
        nv = ADAM_B2 * v_ref[...] + (1.0 - ADAM_B2) * (g * g)
        g_ref[...] = g
        nm_ref[...] = nm
        nv_ref[...] = nv
        d_ref[...] = -ADAM_LR * ((nm / c1) / (jnp.sqrt(nv / c2) + ADAM_EPS) + ADAM_WD * w_ref[...])

    tile = pl.BlockSpec((None, tr, cols), lambda i, c_ref: (0, i, q))
    specs = [pl.BlockSpec((None, tr, cols), (lambda i, c_ref, k=k: (k + (c_ref[0] <= k).astype(jnp.int32), i, 0))) for k in range(3)]
    specs.append(pl.BlockSpec((None, tr, cols), (lambda i, c_ref: (c_ref[0], i, 0))))
    return pl.pallas_call(
        body, name=name,
        grid_spec=pltpu.PrefetchScalarGridSpec(num_scalar_prefetch=1, grid=(R // tr,),
                                               in_specs=specs + [tile, tile, tile] + [pl.BlockSpec(memory_space=pl.ANY)] * n_into,
                                               out_specs=[tile] * 4),
        out_shape=[jax.ShapeDtypeStruct((1, R, C), F32)] * 4,
        input_output_aliases={8 + k: k for k in range(n_into)},
        compiler_params=_params(("parallel",)),
    )(chip, land, land, land, sums, w, m, v, *(into or []))


def _adamw_small(gs, ws, ms, vs):
    n = len(ws)
    c1 = 1.0 - ADAM_B1 ** ADAM_STEP
    c2 = 1.0 - ADAM_B2 ** ADAM_STEP

    def flat2(a):
        return a.reshape(-1, a.shape[-1])

    def body(*refs):
        ins, outs = refs[:4 * n], refs[4 * n:]
        for a in range(n):
            g, w, m, v = (ins[k * n + a][...] for k in range(4))
            nm = ADAM_B1 * m + (1.0 - ADAM_B1) * g
            nv = ADAM_B2 * v + (1.0 - ADAM_B2) * (g * g)
            outs[a][...] = g
            outs[n + a][...] = -ADAM_LR * ((nm / c1) / (jnp.sqrt(nv / c2) + ADAM_EPS) + ADAM_WD * w)
            outs[2 * n + a][...] = nm
            outs[3 * n + a][...] = nv

    shapes = [flat2(w).shape for w in ws]
    out = pl.pallas_call(
        body, name="adamw_small", out_shape=[jax.ShapeDtypeStruct(sh, F32) for sh in shapes] * 4,
        compiler_params=_params(),
    )(*[g.reshape(sh) for g, sh in zip(gs, shapes)], *[flat2(a) for a in (*ws, *ms, *vs)])
    return [[out[k * n + a].reshape(ws[a].shape) for a in range(n)] for k in range(4)]


PROJ_ROWS = 512


def _project(h, w_t, dep=None):
    S = h.shape[0]
    n_tiles = D_IN // SEG_TILE
    ranges = [(c0 // SEG_TILE, (c0 + width) // SEG_TILE) for _, c0, width, _ in SEGMENTS]

    n_extra = int(dep is not None)

    def body(h_ref, w_ref, *outs):
        outs = outs[n_extra:]
        j = pl.program_id(0)
        for (j0, j1), (_, _, _, dt), o_ref in zip(ranges, SEGMENTS, outs):
            @pl.when(jnp.logical_and(j >= j0, j < j1))
            def _(o_ref=o_ref, dt=dt):
                for c in range(S // PROJ_ROWS):
                    rows = pl.ds(c * PROJ_ROWS, PROJ_ROWS)
                    o_ref[rows, :] = _dot(h_ref[rows, :], w_ref[...], NT).astype(dt)

    out_shapes, out_specs = [], []
    for (j0, j1), (name, _, width, dt) in zip(ranges, SEGMENTS):
        if name == "gl":
            per = (j1 - j0) // 3
            out_shapes.append(jax.ShapeDtypeStruct((3, S, width // 3), dt))
            out_specs.append(pl.BlockSpec((None, S, SEG_TILE), (lambda j, j0=j0, j1=j1, per=per: (
                jnp.clip(j - j0, 0, j1 - j0 - 1) // per, 0, jnp.clip(j - j0, 0, j1 - j0 - 1) % per))))
        else:
            out_shapes.append(jax.ShapeDtypeStruct((S, width), dt))
            out_specs.append(pl.BlockSpec((S, SEG_TILE), (lambda j, j0=j0, j1=j1: (0, jnp.clip(j - j0, 0, j1 - j0 - 1)))))
    outs = pl.pallas_call(
        body, name="proj", grid=(n_tiles,),
        in_specs=[pl.BlockSpec((S, D_MODEL), lambda j: (0, 0)), pl.BlockSpec((SEG_TILE, D_MODEL), lambda j: (j, 0))]
        + ([] if dep is None else [pl.BlockSpec((8, 128), lambda j: (0, 0))]),
        out_specs=out_specs, out_shape=out_shapes,
        compiler_params=_params(("arbitrary",), large=True),
    )(h, w_t, *([] if dep is None else [dep]))
    return {name: o for (name, _, _, _), o in zip(SEGMENTS, outs)}


def _forward_a(h, memn, w_in, conv_w, conv_b, w_a, b_a, w_x, b_x, lam, sinks, rel_bias, dep=None):
    S = h.shape[0]
    st = dict(T=min(512, S // 2), tm=min(512, S), bucket=_rel_bucket_map(), h=h, memn=memn)
    seg = st["seg"] = _project(h, w_in, dep)
    st["h_rg"], st["y_rg"] = _rglru_fwd(seg["xr"], seg["g_rg"], conv_w, conv_b, w_a, b_a, w_x, b_x, lam, st["T"])
    st["o_swa"], st["y_swa"] = _swa_fwd(seg["q_s"], seg["kv"], seg["g_swa"], st["bucket"], rel_bias, _sink_column(sinks))
    return st


def _forward_b(st, x, target, post_g, w_memkv, wbr, w_out):
    S = x.shape[0]
    M = st["memn"].shape[0]
    seg = st["seg"]
    st["mkv"] = _matmul(st["memn"], w_memkv, "nn", M, 2 * D_MEM, D_MODEL, M, 512, D_MODEL, BF, "mem_kv")
    st["o_mem"], st["y_mem"] = _mem_fwd(seg["q_m"], st["mkv"], seg["g_mem"])
    st["ys"] = (st["y_rg"], st["y_swa"], st["y_mem"])
    st["merged"] = _merge_fwd(st["ys"], wbr, seg["gl"], st["tm"])
    st["dout"], st["dy"], st["loss"], st["dpost"] = _out_loss(st["merged"], w_out, x, target, post_g, min(256, S))
    return st


def _backward_a1(st, wbr, w_out):
    S = st["h"].shape[0]
    seg, ys, tm = st["seg"], st["ys"], st["tm"]
    st["dw_out"] = _matmul(st["merged"], st["dout"], "tn", D_MODEL, D_MODEL, S, 256, D_MODEL, S, BF, "dw_out", out_blocked="row")
    dgl0, dgl1, dgl2, dp0, dp1, dp2 = _merge_bwd(st["dout"], w_out, ys, wbr, seg["gl"], tm)
    st["dgl"] = (dgl0, dgl1, dgl2)
    dys, dwbr = [], []
    for i, dp in enumerate((dp0, dp1, dp2)):
        dys.append(_matmul(dp, wbr[i], "nn", S, 1024, D_MODEL, tm, 1024, D_MODEL, F32, "dy_br%d" % i))
        dwbr.append(_matmul(ys[i], dp, "tn", 1024, D_MODEL, S, 1024, 256, S, BF, "dw_br%d" % i, out_blocked="col"))
    st["dys"], st["dwbr"] = dys, dwbr
    return st


def _backward_a2(st, mem, w_memkv, conv_w, conv_b, w_a, b_a, w_x, b_x, lam):
    M = mem.shape[0]
    seg, dys = st["seg"], st["dys"]
    st["dq_m"], st["dg_mem"], dmkv = _mem_bwd(seg["q_m"], st["mkv"], seg["g_mem"], st["o_mem"], dys[2])
    st["dmkv"] = dmkv.astype(BF)
    st["dw_memkv"] = _matmul(st["memn"], st["dmkv"], "tn", D_MODEL, 2 * D_MEM, M, 256, 2 * D_MEM, M, BF, "dw_memkv", out_blocked="row")
    st["dxr"], st["dg_rg"], st["dw_a"], st["dw_x"], st["dvec"] = _rglru_bwd(
        seg["xr"], seg["g_rg"], st["h_rg"], dys[0], conv_w, conv_b, w_a, b_a, w_x, b_x, lam, st["T"])
    return st


def _mem_gain_grad(st, mem, w_memkv, dep=None):
    M = mem.shape[0]
    dmemn = _matmul(st["dmkv"], w_memkv, "nt", M, D_MODEL, 2 * D_MEM, M, 512, 2 * D_MEM, F32, "dmemn", dep=dep)
    return _rms_gain_grad(dmemn, mem, "dmem_gain")


def _backward_b(st, rel_bias, sinks):
    seg = st["seg"]
    others = {"xr": st["dxr"], "g_rg": st["dg_rg"], "q_m": st["dq_m"], "g_mem": st["dg_mem"], "gl": st["dgl"]}
    dproj, dkv, dsinks, drel = _swa_bwd(seg["q_s"], seg["kv"], seg["g_swa"], st["o_swa"], st["dys"][1],
                                        st["bucket"], rel_bias, _sink_column(sinks), others)
    st["dsinks"], st["drel"] = dsinks.reshape(1, SWA_HEADS), drel.reshape(REL_BUCKETS, SWA_HEADS)
    col_kv = [c0 for name, c0, _, _ in SEGMENTS if name == "kv"][0]
    st["dproj"] = lax.dynamic_update_slice(dproj, dkv.astype(BF), (0, col_kv))
    return st


def _dw_in_half(st, half, dep=None):
    S = st["h"].shape[0]
    dw = _matmul(st["dproj"], st["h"], "tn", D_IN, D_MODEL // 2, S, D_IN_TILE, D_MODEL // 2, S, BF, "dw_in%d" % half, b_noff=half, dep=dep)
    return dw.reshape(N_DEV, D_IN // N_DEV, D_MODEL // 2)


def _owner_blocks(a):
    return jnp.swapaxes(a.reshape((4, 2) + a.shape[1:]), 0, 1)


def _pad_rows(a, rows):
    a = a.reshape(-1, 128) if a.shape[-1] % 128 == 0 else jnp.pad(a, ((0, 0), (0, 128 - a.shape[-1])))
    return jnp.pad(a, ((0, rows - a.shape[0]), (0, 0))) if a.shape[0] < rows else a


def kernel(x, mem, pre_norm_g, post_norm_g, mem_norm_g, w_in, conv_w, conv_b, w_rg_a, b_rg_a, w_rg_x, b_rg_x, lru_lambda, swa_sinks, rel_bias, w_mem_kv, w_br_rg, w_br_swa, w_br_mem, w_out, loss_target, m_pre_norm_g, m_post_norm_g, m_mem_norm_g, m_w_in, m_conv_w, m_conv_b, m_w_rg_a, m_b_rg_a, m_w_rg_x, m_b_rg_x, m_lru_lambda, m_swa_sinks, m_rel_bias, m_w_mem_kv, m_w_br_rg, m_w_br_swa, m_w_br_mem, m_w_out, v_pre_norm_g, v_post_norm_g, v_mem_norm_g, v_w_in, v_conv_w, v_conv_b, v_w_rg_a, v_b_rg_a, v_w_rg_x, v_b_rg_x, v_lru_lambda, v_swa_sinks, v_rel_bias, v_w_mem_kv, v_w_br_rg, v_w_br_swa, v_w_br_mem, v_w_out):
    cx, cy, cc = lax.axis_index("x"), lax.axis_index("y"), lax.axis_index("c")
    me = 4 * cx + 2 * cy + cc
    chip = 2 * cx + cy
    core = jnp.reshape(cc, (1,)).astype(jnp.int32)
    x0, mem0 = x[0], mem[0]
    w_a_b, w_x_b = w_rg_a[0].astype(BF), w_rg_x[0].astype(BF)

    def landing(own, slot, slots):
        return lax.dynamic_update_slice(lax.empty((slots,) + own.shape, own.dtype), own[None], (slot,) + (0,) * own.ndim)


    def swap_start(parts, tag):
        return _exchange_start(parts, [lax.empty((4,) + p.shape[-2:], p.dtype) for p in parts], _plan_swap([p.ndim for p in parts]),
                               "swap_%s_start" % tag)

    def scatter_start(swap, after, tag, prefill=()):
        s_send, s_recv, parts, got, _ = swap
        got = _exchange_wait(s_send, s_recv, parts, got, _plan_swap([p.ndim for p in parts]), after, "swap_%s_wait" % tag)
        sums = [_pair_sum(p, g, core, "scatter_%s_sum%d" % (tag, i)) for i, (p, g) in enumerate(zip(parts, got))]
        lands = [landing(lax.dynamic_index_in_dim(s, chip, 0, keepdims=False), chip, 4) if i in prefill
                 else lax.empty(s.shape, s.dtype) for i, s in enumerate(sums)]
        return _exchange_start(sums, lands, _plan_scatter(len(sums)), "scatter_%s_start" % tag)

    def corner(a):
        return a.reshape(-1, a.shape[-1])[:8, :128]

    def zero_after(a):
        return jnp.minimum(jnp.abs(a.reshape(-1)[0].astype(F32)), 0.0)

    g_in, g_cw, h0, memn0 = _all_gather_relayed([jnp.transpose(w_in[0]).astype(BF), conv_w[0]], [True, False], "gather_w_in",
                                                side=_rms_side([(x0, pre_norm_g), (mem0, mem_norm_g)]))
    w_in_f = g_in.reshape(D_IN, D_MODEL)
    conv_w_f = jnp.transpose(g_cw, (1, 0, 2)).reshape(CONV_W, D_RNN)

    after_first = zero_after(g_cw).astype(BF)
    rest = [w.astype(BF) + after_first for w in (w_mem_kv[0], jnp.transpose(w_br_rg[0]), jnp.transpose(w_br_swa[0]),
                                                 jnp.transpose(w_br_mem[0]), w_out[0])]
    plan_g = _plan_gather(len(rest))
    zones = _place_own([lax.empty((N_DEV,) + w.shape, w.dtype) for w in rest], rest, jnp.reshape(me, (1,)).astype(jnp.int32), "gather_rest_own")
    g_send, g_recv, g_src, g_land, g_token = _exchange_start(rest, zones, plan_g, "gather_rest_start")
    st = _forward_a(h0, memn0, w_in_f, conv_w_f, conv_b, w_a_b, b_rg_a, w_x_b, b_rg_x, lru_lambda, swa_sinks, rel_bias, dep=g_token)
    g_land = _exchange_wait(g_send, g_recv, g_src, g_land, plan_g, st["y_swa"], "gather_rest_wait")
    g_land = _forward_to_sibling(g_land, "gather_rest_forward")
    w_memkv_f = g_land[0].reshape(D_MODEL, 2 * D_MEM)
    wbr = tuple(g_land[i].reshape(D_MODEL, D_RNN) for i in (1, 2, 3))
    w_out_f = g_land[4].reshape(D_MODEL, D_MODEL)

    st = _forward_b(st, x0, loss_target[0], post_norm_g, w_memkv_f, wbr, w_out_f)
    st = _backward_a1(st, wbr, w_out_f)
    parts_a = [st["dw_out"], st["dwbr"][0], st["dwbr"][1], st["dwbr"][2]]
    plan_a = _plan_scatter(len(parts_a))
    swap_a = swap_start(parts_a, "a")
    st = _backward_a2(st, mem0, w_memkv_f, conv_w_f, conv_b + swap_a[4][0:1, 0:1], w_a_b, b_rg_a, w_x_b, b_rg_x, lru_lambda)
    a_send, a_recv, a_src, a_land, a_token = scatter_start(swap_a, st["dxr"], "a")
    parts_c = [st["dw_memkv"], _owner_blocks(st["dw_a"]), _owner_blocks(st["dw_x"])]
    plan_c = _plan_scatter(len(parts_c))
    swap_c = swap_start(parts_c, "c")

    st = _backward_b(st, rel_bias, swa_sinks + swap_c[4][0:1, 0:1] + a_token[0:1, 0:1])
    c_send, c_recv, c_src, c_land, c_token = scatter_start(swap_c, st["dsinks"], "c", prefill=(1, 2))
    plan_b = _plan_scatter(1)

    def dw_in_parts(half, dep):
        dwh = _dw_in_half(st, half, dep)
        return dwh, [dwh]

    dw0, parts_b0 = dw_in_parts(0, c_token)
    swap_b0 = swap_start(parts_b0, "b0")
    a_land = _exchange_wait(a_send, a_recv, a_src, a_land, plan_a, swap_b0[4], "scatter_a_wait")
    big = [None] * 6

    chip1 = jnp.reshape(chip, (1,)).astype(jnp.int32)

    def adamw_big(j, land, own, wt, mt, vt):
        big[j] = _adamw(land, own, chip1, wt, mt, vt, "adamw_big%d" % j)

    adamw_big(5, a_land[0], a_src[0], w_out, m_w_out, v_w_out)
    adamw_big(2, a_land[1], a_src[1], w_br_rg, m_w_br_rg, v_w_br_rg)
    adamw_big(3, a_land[2], a_src[2], w_br_swa, m_w_br_swa, v_w_br_swa)
    halves = [scatter_start(swap_b0, corner(big[5][1]) + corner(big[2][1]) + corner(big[3][1]), "b0")]
    dw1, parts_b1 = dw_in_parts(1, halves[0][4])
    swap_b1 = swap_start(parts_b1, "b1")
    c_land = _exchange_wait(c_send, c_recv, c_src, c_land, plan_c, swap_b1[4], "scatter_c_wait")
    g_wa_blk = _sum_parts(c_land[1], "sum_w_rg_a")
    g_wx_blk = _sum_parts(c_land[2], "sum_w_rg_x")
    adamw_big(4, a_land[3], a_src[3], w_br_mem, m_w_br_mem, v_w_br_mem)
    adamw_big(1, c_land[0], c_src[0], w_mem_kv, m_w_mem_kv, v_w_mem_kv)
    halves.append(scatter_start(swap_b1, corner(big[4][1]) + corner(big[1][1]), "b1"))
    st["dmem_g"] = _mem_gain_grad(st, mem0, w_memkv_f, halves[1][4])
    grad_x, dpre = _dh_dx(st["dproj"], w_in_f, x0, st["dy"], pre_norm_g + halves[1][4][0:1, 0:1], st["tm"], D_IN_TILE)
    pack = jnp.concatenate([dpre.reshape(16, 128), st["dpost"].reshape(16, 128), st["dmem_g"].reshape(16, 128),
                            st["dvec"].reshape(64, 128), _pad_rows(st["dsinks"], 8), _pad_rows(st["drel"], 32), g_wa_blk, g_wx_blk,
                            st["loss"]], axis=0)
    plan_s = _plan_everyone(1)
    s_send, s_recv, s_src, s_land, s_token = _exchange_start([pack], [landing(pack, me, N_DEV)], plan_s, "gather_small_start")
    swap_last = lambda a: jnp.transpose(a, (0, 2, 1))
    after, big0_t = s_token, None
    for half, (b_send, b_recv, b_src, b_land, _) in enumerate(halves):
        b_land = _exchange_wait(b_send, b_recv, b_src, b_land, plan_b, after, "scatter_b%d_wait" % half)[0]
        big0_t = _adamw(b_land, b_src[0], chip1, swap_last(w_in), swap_last(m_w_in), swap_last(v_w_in), "adamw_big0_%d" % half,
                        group=(half, 2), into=big0_t)
        after = corner(big0_t[1])
    big[0] = [swap_last(a) for a in big0_t]
    gathered = _exchange_wait(s_send, s_recv, s_src, s_land, plan_s, corner(big0_t[1]), "gather_small_wait")[0]
    gs = _sum_parts(gathered, "sum_small")
    loss_total = gs[408, 0]
    g_pre, g_post, g_memg = gs[0:16].reshape(1, D_MODEL), gs[16:32].reshape(1, D_MODEL), gs[32:48].reshape(1, D_MODEL)
    gvec = gs[48:112].reshape(8, D_RNN)
    g_conv_w = lax.dynamic_slice(gvec[0:CONV_W], (0, me * RNN_BLOCK), (CONV_W, RNN_BLOCK))
    g_conv_b, g_b_a, g_b_x, g_lam = gvec[4:5], gvec[5:6], gvec[6:7], gvec[7:8]
    g_sinks = gs[112:113, :SWA_HEADS]
    g_rel = gs[120:152, :SWA_HEADS]
    g_w_a = gathered[:, 152:280]
    g_w_x = gathered[:, 280:408]

    g_small = (g_pre, g_post, g_memg, g_conv_b, g_b_a, g_b_x, g_lam, g_w_a, g_w_x, g_sinks, g_rel, g_conv_w)
    w_small = (pre_norm_g, post_norm_g, mem_norm_g, conv_b, b_rg_a, b_rg_x, lru_lambda, w_rg_a, w_rg_x, swa_sinks, rel_bias, conv_w)
    m_small = (m_pre_norm_g, m_post_norm_g, m_mem_norm_g, m_conv_b, m_b_rg_a, m_b_rg_x, m_lru_lambda, m_w_rg_a, m_w_rg_x, m_swa_sinks, m_rel_bias, m_conv_w)
    v_small = (v_pre_norm_g, v_post_norm_g, v_mem_norm_g, v_conv_b, v_b_rg_a, v_b_rg_x, v_lru_lambda, v_w_rg_a, v_w_rg_x, v_swa_sinks, v_rel_bias, v_conv_w)
    sm = _adamw_small(g_small, w_small, m_small, v_small)


    def leaves(k):
        s = sm[k]
        return [s[0], s[1], s[2], big[0][k], s[11], s[3], s[7], s[4], s[8], s[5], s[6], s[9], s[10],
                big[1][k], big[2][k], big[3][k], big[4][k], big[5][k]]

    return (loss_total, grad_x[None], *leaves(0), *leaves(1), *leaves(2), *leaves(3))
```

```python
import math

import jax
import jax.numpy as jnp
import numpy as np
from jax import lax
from jax.experimental import pallas as pl
from jax.experimental.pallas import tpu as pltpu

F32, BF = jnp.float32, jnp.bfloat16
MESH = pl.DeviceIdType.MESH
N_DEV = 8

D_MODEL = 2048
D_RNN = 1024
RNN_BLOCKS = 8
RNN_BLOCK = 128
CONV_W = 4
LRU_C = 8.0
SWA_HEADS = 16
SWA_KV_HEADS = 2
SWA_HD = 64
WINDOW = 128
MEM_HEADS = 4
MEM_HD = 256
D_MEM = 1024
REL_BUCKETS = 32
REL_MAX_DIST = 128
EPS = 1e-6
NEG_INF = -1e30
D_IN = 12544
SEGMENTS = (("xr", 0, 1024, F32), ("g_rg", 1024, 1024, F32), ("q_s", 2048, 1024, BF), ("kv", 3072, 256, BF),
            ("g_swa", 3328, 1024, F32), ("q_m", 4352, 1024, BF), ("g_mem", 5376, 1024, F32), ("gl", 6400, 6144, F32))
SEG_TILE = 256
D_IN_TILE = 7 * SEG_TILE

ADAM_LR, ADAM_B1, ADAM_B2, ADAM_EPS, ADAM_WD, ADAM_STEP = 0.001, 0.9, 0.999, 1e-08, 0.01, 10

NN = (((1,), (0,)), ((), ()))
NT = (((1,), (1,)), ((), ()))
TN = (((0,), (0,)), ((), ()))
MIB = 2 ** 20


def _dot(a, b, dn):
    return lax.dot_general(a, b, dn, preferred_element_type=F32)


VMEM_LIMIT_MIB = 48
VMEM_LIMIT_LARGE_MIB = 56


def _params(sem=None, large=False):
    return pltpu.CompilerParams(dimension_semantics=sem, vmem_limit_bytes=(VMEM_LIMIT_LARGE_MIB if large else VMEM_LIMIT_MIB) * MIB)


def _sigmoid(z):
    return 1.0 / (1.0 + jnp.exp(-z))


def _softplus(z):
    return jnp.maximum(z, 0.0) + jnp.log(1.0 + jnp.exp(-jnp.abs(z)))


def _expm1(z):
    p = z * (1.0 + z * (0.5 + z * (1.0 / 6 + z * (1.0 / 24 + z * (1.0 / 120 + z * (1.0 / 720 + z * (1.0 / 5040 + z / 40320)))))))
    return jnp.where(jnp.abs(z) < 0.3, p, jnp.exp(z) - 1.0)


def _flat(p):
    return 4 * p[0] + 2 * p[1] + p[2]


def _all_gather_relayed(arrs, relay, name, side=None):
    n = len(arrs)
    K = 9
    work, side_ins, side_outs, side_scratch = side if side is not None else (None, [], [], [])
    n_in, n_out = n + len(side_ins), n + len(side_outs)

    def body(*refs):
        ins, outs = refs[:n], refs[n_in:n_in + n]
        send_sems, recv_sems, local_sems = refs[n_in + n_out:n_in + n_out + 3]
        x, y, c = lax.axis_index("x"), lax.axis_index("y"), lax.axis_index("c")
        me, sib = (x, y, c), (x, y, 1 - c)
        xn, yn, dg = (1 - x, y, c), (x, 1 - y, c), (1 - x, 1 - y, c)

        def other(p):
            return (p[0], p[1], 1 - p[2])

        def rows(a, half):
            h = arrs[a].shape[0] // 2
            return pl.ds(half * h, h)

        def copy(a, k, block, to, half=None, src=None):
            dst = outs[a].at[_flat(block)]
            if half is not None:
                dst = dst.at[rows(a, half)]
            return pltpu.make_async_remote_copy(src_ref=dst if src is None else src, dst_ref=dst,
                                                send_sem=send_sems.at[a * K + k], recv_sem=recv_sems.at[a * K + k],
                                                device_id=to, device_id_type=MESH)

        mine = [pltpu.make_async_copy(ins[a], outs[a].at[_flat(me)], local_sems.at[a]) for a in range(n)]
        for cp in mine:
            cp.start()
        sends = []

        def start(cp):
            cp.start()
            sends.append(cp)

        for a in range(n):
            start(copy(a, 1, me, xn, src=ins[a]))
            start(copy(a, 2, me, yn, src=ins[a]))
            if not relay[a]:
                start(copy(a, 3, me, dg, src=ins[a]))
            start(copy(a, 0, me, sib, src=ins[a]))
        if work is not None:
            work(refs[n:n_in], refs[n_in + n:n_in + n_out], refs[n_in + n_out + 3:])
        for a in range(n):
            copy(a, 1, xn, me).wait_recv()
            if relay[a]:
                start(copy(a, 3, xn, yn, half=0))
            start(copy(a, 5, xn, sib))
        for a in range(n):
            copy(a, 2, yn, me).wait_recv()
            if relay[a]:
                start(copy(a, 4, yn, xn, half=1))
            start(copy(a, 6, yn, sib))
        for a in range(n):
            if relay[a]:
                copy(a, 3, dg, me, half=0).wait_recv()
                start(copy(a, 7, dg, sib, half=0))
                copy(a, 4, dg, me, half=1).wait_recv()
                start(copy(a, 8, dg, sib, half=1))
            else:
                copy(a, 3, dg, me).wait_recv()
                start(copy(a, 7, dg, sib))
        for a in range(n):
            copy(a, 0, sib, me).wait_recv()
            copy(a, 5, other(xn), me).wait_recv()
            copy(a, 6, other(yn), me).wait_recv()
            if relay[a]:
                copy(a, 7, other(dg), me, half=0).wait_recv()
                copy(a, 8, other(dg), me, half=1).wait_recv()
            else:
                copy(a, 7, other(dg), me).wait_recv()
        for cp in sends:
            cp.wait_send()
        for cp in mine:
            cp.wait()

    any_spec = pl.BlockSpec(memory_space=pl.ANY)
    return pl.pallas_call(
        body, name=name,
        out_shape=[jax.ShapeDtypeStruct((N_DEV,) + a.shape, a.dtype) for a in arrs] + list(side_outs),
        in_specs=[any_spec] * n_in, out_specs=[any_spec] * n_out,
        scratch_shapes=[pltpu.SemaphoreType.DMA((K * n,)), pltpu.SemaphoreType.DMA((K * n,)), pltpu.SemaphoreType.DMA((n,))]
        + list(side_scratch),
        compiler_params=_params(),
    )(*arrs, *side_ins)


def _chip_peers(x, y):
    return [(1 - x, y), (x, 1 - y), (1 - x, 1 - y)]


def _chip(p):
    return 2 * p[0] + p[1]


def _plan_gather(n):
    def plan(x, y, c):
        out = []
        for a in range(n):
            for peer in [(x, y, 1 - c)] + [(*ch, c) for ch in _chip_peers(x, y)]:
                out.append((a, None, ("lead", _flat((x, y, c))), peer, ("lead", _flat(peer))))
        return out
    return plan


def _plan_everyone(n):
    def plan(x, y, c):
        out = []
        for a in range(n):
            for r in range(1, N_DEV):
                peer = (1 - x if r & 4 else x, 1 - y if r & 2 else y, 1 - c if r & 1 else c)
                out.append((a, None, ("lead", _flat((x, y, c))), peer, ("lead", _flat(peer))))
        return out
    return plan


def _plan_swap(ndims):
    def plan(x, y, c):
        out = []
        for a, nd in enumerate(ndims):
            if nd == 4:
                out.append((a, 1 - c, ("all", 0), (x, y, 1 - c), ("all", 0)))
            else:
                out += [(a, 2 * j + 1 - c, ("lead", j), (x, y, 1 - c), ("lead", j)) for j in range(4)]
        return out
    return plan


def _slot(ref, where):
    kind, k = where
    return ref if kind == "all" else ref.at[k]


def _plan_scatter(n):
    def plan(x, y, c):
        out = []
        for a in range(n):
            for ch in _chip_peers(x, y):
                out.append((a, _chip(ch), ("lead", _chip((x, y))), (*ch, c), ("lead", _chip(ch))))
        return out
    return plan


HBM_SPEC = pl.BlockSpec(memory_space=pltpu.HBM)
SEM_SPEC = pl.BlockSpec(memory_space=pltpu.SEMAPHORE)


def _in_hbm(a):
    return pltpu.with_memory_space_constraint(a, pltpu.HBM)


def _exchange_start(srcs, lands, plan, name):
    n = len(lands)
    ns = 0 if srcs is None else n
    count = len(plan(0, 0, 0))

    def body(*refs):
        land_refs = refs[ns:ns + n]
        src_refs = land_refs if srcs is None else refs[:n]
        send_sems, recv_sems = refs[ns + n], refs[ns + n + 1]
        token = refs[-1]
        x, y, c = lax.axis_index("x"), lax.axis_index("y"), lax.axis_index("c")
        for k, (a, si, di, peer, _) in enumerate(plan(x, y, c)):
            src = src_refs[a] if si is None else src_refs[a].at[si]
            pltpu.make_async_remote_copy(src_ref=src, dst_ref=_slot(land_refs[a], di), send_sem=send_sems.at[k],
                                         recv_sem=recv_sems.at[k], device_id=peer, device_id_type=MESH).start()
        token[...] = jnp.zeros_like(token)

    out = pl.pallas_call(
        body, name=name,
        out_shape=(pltpu.SemaphoreType.DMA((count,)), pltpu.SemaphoreType.DMA((count,)),
                   *[pltpu.HBM(a.shape, a.dtype) for a in lands], jax.ShapeDtypeStruct((8, 128), F32)),
        in_specs=[HBM_SPEC] * (ns + n),
        out_specs=(SEM_SPEC, SEM_SPEC, *([HBM_SPEC] * n), pl.BlockSpec(memory_space=pltpu.VMEM)),
        input_output_aliases={ns + i: 2 + i for i in range(n)},
        compiler_params=pltpu.CompilerParams(has_side_effects=pltpu.SideEffectType.DATAFLOW_SIDE_EFFECTING),
    )(*[_in_hbm(a) for a in (srcs or [])], *[_in_hbm(a) for a in lands])
    return out[0], out[1], srcs if srcs is None else list(srcs), list(out[2:2 + n]), out[-1]


def _exchange_wait(send_sems, recv_sems, srcs, lands, plan, after, name):
    n = len(lands)
    ns = 0 if srcs is None else n

    def body(*refs):
        land_refs = refs[ns:ns + n]
        src_refs = land_refs if srcs is None else refs[:n]
        send_sems, recv_sems = refs[ns + n], refs[ns + n + 1]
        x, y, c = lax.axis_index("x"), lax.axis_index("y"), lax.axis_index("c")
        for k, (a, si, _, peer, ri) in enumerate(plan(x, y, c)):
            src = src_refs[a] if si is None else src_refs[a].at[si]
            cp = pltpu.make_async_remote_copy(src_ref=src, dst_ref=_slot(land_refs[a], ri), send_sem=send_sems.at[k],
                                              recv_sem=recv_sems.at[k], device_id=peer, device_id_type=MESH)
            cp.wait_send()
            cp.wait_recv()

    out = pl.pallas_call(
        body, name=name,
        out_shape=tuple(pltpu.HBM(a.shape, a.dtype) for a in lands),
        in_specs=[HBM_SPEC] * (ns + n) + [SEM_SPEC, SEM_SPEC, pl.BlockSpec(memory_space=pl.ANY)],
        out_specs=tuple([HBM_SPEC] * n),
        input_output_aliases={ns + i: i for i in range(n)},
        compiler_params=pltpu.CompilerParams(has_side_effects=pltpu.SideEffectType.DATAFLOW_SIDE_EFFECTING),
    )(*[_in_hbm(a) for a in (srcs or [])], *lands, send_sems, recv_sems, after)
    return list(out)


def _plan_forward(n):
    def plan(x, y, c):
        return [(a, _flat((*ch, c)), ("lead", _flat((*ch, c))), (x, y, 1 - c), ("lead", _flat((*ch, 1 - c))))
                for a in range(n) for ch in _chip_peers(x, y)]
    return plan


def _place_own(zones, owns, slot, name):
    n = len(zones)

    def body(slot_ref, *refs):
        for a in range(n):
            refs[2 * n + a][...] = refs[a][...]

    return pl.pallas_call(
        body, name=name,
        grid_spec=pltpu.PrefetchScalarGridSpec(
            num_scalar_prefetch=1, grid=(1,),
            in_specs=[pl.BlockSpec(o.shape, lambda i, s_ref: (0, 0)) for o in owns] + [pl.BlockSpec(memory_space=pl.ANY)] * n,
            out_specs=[pl.BlockSpec((None,) + o.shape, lambda i, s_ref: (s_ref[0], 0, 0)) for o in owns]),
        out_shape=[jax.ShapeDtypeStruct(z.shape, z.dtype) for z in zones],
        input_output_aliases={1 + n + a: a for a in range(n)},
        compiler_params=_params(("arbitrary",)),
    )(slot, *owns, *zones)


def _pair_sum(parts, got, core, name):
    R, C = parts.shape[-2:]
    mine = (pl.BlockSpec((None, None, R, C), lambda j, c_ref: (c_ref[0], j, 0, 0)) if parts.ndim == 4
            else pl.BlockSpec((None, R, C), lambda j, c_ref: (2 * j + c_ref[0], 0, 0)))

    def body(c_ref, p_ref, g_ref, o_ref):
        o_ref[...] = (p_ref[...].astype(F32) + g_ref[...].astype(F32)).astype(o_ref.dtype)

    return pl.pallas_call(
        body, name=name,
        grid_spec=pltpu.PrefetchScalarGridSpec(
            num_scalar_prefetch=1, grid=(4,),
            in_specs=[mine, pl.BlockSpec((None, R, C), lambda j, c_ref: (j, 0, 0))],
            out_specs=pl.BlockSpec((None, R, C), lambda j, c_ref: (j, 0, 0))),
        out_shape=jax.ShapeDtypeStruct((4, R, C), parts.dtype),
        compiler_params=_params(("parallel",)),
    )(core, parts, got)


def _matmul(a, b, mode, M, N, K, tm, tn, tk, out_dtype, name, b_noff=0, out_blocked=None, dep=None):
    nm, nn, nk = M // tm, N // tn, K // tk
    if mode == "nn":
        a_spec = pl.BlockSpec((tm, tk), lambda j, i, k: (i, k))
        b_spec = pl.BlockSpec((tk, tn), lambda j, i, k: (k, j + b_noff))
        dn = NN
    elif mode == "nt":
        a_spec = pl.BlockSpec((tm, tk), lambda j, i, k: (i, k))
        b_spec = pl.BlockSpec((tn, tk), lambda j, i, k: (j + b_noff, k))
        dn = NT
    else:
        a_spec = pl.BlockSpec((tk, tm), lambda j, i, k: (k, i))
        b_spec = pl.BlockSpec((tk, tn), lambda j, i, k: (k, j + b_noff))
        dn = TN
    if out_blocked == "col":
        out_shape = jax.ShapeDtypeStruct((2, 4, M, tn), out_dtype)
        out_spec = pl.BlockSpec((None, None, tm, tn), lambda j, i, k: (j % 2, j // 2, i, 0))
    elif out_blocked == "row":
        out_shape = jax.ShapeDtypeStruct((2, 4, tm, N), out_dtype)
        out_spec = pl.BlockSpec((None, None, tm, tn), lambda j, i, k: (i % 2, i // 2, 0, j))
    else:
        out_shape = jax.ShapeDtypeStruct((M, N), out_dtype)
        out_spec = pl.BlockSpec((tm, tn), lambda j, i, k: (i, j))

    n_extra = int(dep is not None)

    def body(a_ref, b_ref, *rest):
        o_ref, scratch = rest[n_extra], rest[n_extra + 1:]
        if nk == 1:
            o_ref[...] = _dot(a_ref[...], b_ref[...], dn).astype(out_dtype)
        else:
            acc_ref, = scratch
            k = pl.program_id(2)

            @pl.when(k == 0)
            def _():
                acc_ref[...] = jnp.zeros_like(acc_ref)

            acc_ref[...] += _dot(a_ref[...], b_ref[...], dn)

            @pl.when(k == nk - 1)
            def _():
                o_ref[...] = acc_ref[...].astype(out_dtype)

    return pl.pallas_call(
        body, name=name, grid=(nn, nm, nk),
        in_specs=[a_spec, b_spec] + ([] if dep is None else [pl.BlockSpec((8, 128), lambda j, i, k: (0, 0))]),
        out_specs=out_spec, out_shape=out_shape,
        scratch_shapes=[] if nk == 1 else [pltpu.VMEM((tm, tn), F32)],
        compiler_params=_params(("parallel", "parallel", "arbitrary")),
    )(a, b, *([] if dep is None else [dep]))


RMS_SIDE_ROWS = 512


def _rms_side(pairs):
    chunks = [min(x.shape[0], RMS_SIDE_ROWS) for x, _ in pairs]

    def work(ins, outs, scratch):
        sem = scratch[-1]

        def move(src, dst):
            cp = pltpu.make_async_copy(src, dst, sem.at[0])
            cp.start()
            cp.wait()

        for p, ((x, _), tr) in enumerate(zip(pairs, chunks)):
            x_ref, g_ref, h_ref = ins[2 * p], ins[2 * p + 1], outs[p]
            xv, gv, hv = scratch[3 * p:3 * p + 3]
            move(g_ref, gv)
            for i in range(x.shape[0] // tr):
                rows = pl.ds(i * tr, tr)
                move(x_ref.at[rows], xv)
                v = xv[...]
                hv[...] = (v * lax.rsqrt(jnp.mean(v * v, axis=-1, keepdims=True) + EPS) * gv[...]).astype(BF)
                move(hv, h_ref.at[rows])

    scratch = [s for (x, g), tr in zip(pairs, chunks)
               for s in (pltpu.VMEM((tr, x.shape[1]), F32), pltpu.VMEM(g.shape, F32), pltpu.VMEM((tr, x.shape[1]), BF))]
    return (work, [a for pair in pairs for a in pair], [jax.ShapeDtypeStruct(x.shape, BF) for x, _ in pairs],
            scratch + [pltpu.SemaphoreType.DMA((1,))])


def _rms_gain_grad(dn, x, name):
    R, Dm = x.shape

    def body(dn_ref, x_ref, o_ref):
        xv = x_ref[...]
        r = lax.rsqrt(jnp.mean(xv * xv, axis=-1, keepdims=True) + EPS)
        o_ref[...] = jnp.sum(dn_ref[...] * xv * r, axis=0, keepdims=True)

    return pl.pallas_call(
        body, name=name, out_shape=jax.ShapeDtypeStruct((1, Dm), F32),
        compiler_params=_params(),
    )(dn, x)


def _shift_down(v, k, head8, row, T):
    if k == 0:
        return v
    r = pltpu.roll(v, k, 0)
    hr = pltpu.roll(head8, k, 0)
    top = jnp.where(row[:8] < k, hr, r[:8])
    return jnp.concatenate([top, r[8:]], axis=0)


def _shift_up(v, k, tail8, row, T):
    if k == 0:
        return v
    r = pltpu.roll(v, T - k, 0)
    tr = pltpu.roll(tail8, 8 - k, 0)
    bot = jnp.where(row[:8] >= 8 - k, tr, r[T - 8:])
    return jnp.concatenate([r[:T - 8], bot], axis=0)


def _rglru_gates(u, head8, grow, row, T, cw_ref, cb_ref, wa_ref, ba_ref, wx_ref, bx_ref, lam_ref):
    us = [_shift_down(u, k, head8, row, T) for k in range(CONV_W)]
    acc = us[0] * cw_ref[0:1, :]
    for k in range(1, CONV_W):
        acc = acc + us[k] * cw_ref[k:k + 1, :]
    conv = cb_ref[...] + acc
    cbf = conv.astype(BF)
    r_ = _sigmoid(_dot(cbf, wa_ref[0], NN) + ba_ref[...])
    i_ = _sigmoid(_dot(cbf, wx_ref[0], NN) + bx_ref[...])
    sp = _softplus(-lam_ref[...])
    la = -LRU_C * r_ * sp
    a = jnp.exp(la)
    mult_raw = jnp.sqrt(-_expm1(2.0 * la))
    mult = jnp.where(grow == 0, 1.0, mult_raw)
    return us, conv, cbf, r_, i_, sp, a, mult_raw, mult


def _rglru_specs(T, nt, rev):
    tmap = (lambda n, t: (nt - 1 - t, n)) if rev else (lambda n, t: (t, n))
    hmap = ((lambda n, t: (jnp.maximum((nt - 1 - t) * (T // 8) - 1, 0), n)) if rev
            else (lambda n, t: (jnp.maximum(t * (T // 8) - 1, 0), n)))
    tile = pl.BlockSpec((T, RNN_BLOCK), tmap)
    halo = pl.BlockSpec((8, RNN_BLOCK), hmap)
    vec = pl.BlockSpec((1, RNN_BLOCK), lambda n, t: (0, n))
    cw = pl.BlockSpec((CONV_W, RNN_BLOCK), lambda n, t: (0, n))
    wblk = pl.BlockSpec((1, RNN_BLOCK, RNN_BLOCK), lambda n, t: (n, 0, 0))
    return tile, halo, vec, cw, wblk


def _rglru_fwd(xr, g, cw, cb, wa, ba, wx, bx, lam, T):
    S = xr.shape[0]
    nt = S // T

    def body(u_ref, uh_ref, g_ref, cw_ref, cb_ref, wa_ref, ba_ref, wx_ref, bx_ref, lam_ref, h_ref, y_ref, carry):
        t = pl.program_id(1)

        @pl.when(t == 0)
        def _():
            carry[...] = jnp.zeros_like(carry)

        row = lax.broadcasted_iota(jnp.int32, (T, RNN_BLOCK), 0)
        grow = row + t * T
        head8 = jnp.where(t > 0, uh_ref[...], 0.0)
        _, conv, _, _, i_, _, a, _, mult = _rglru_gates(u_ref[...], head8, grow, row, T, cw_ref, cb_ref, wa_ref, ba_ref,
                                                         wx_ref, bx_ref, lam_ref)
        b = mult * i_ * conv
        s = 1
        while s < T:
            keep = row >= s
            a_s = jnp.where(keep, pltpu.roll(a, s, 0), 1.0)
            b_s = jnp.where(keep, pltpu.roll(b, s, 0), 0.0)
            b = a * b_s + b
            a = a * a_s
            s *= 2
        h = b + a * carry[0:1, :]
        carry[...] = jnp.broadcast_to(h[T - 1:T, :], carry.shape)
        h_ref[...] = h
        gv = g_ref[...]
        y_ref[...] = (h * (gv * _sigmoid(gv))).astype(BF)

    tile, halo, vec, cwspec, wblk = _rglru_specs(T, nt, False)
    return pl.pallas_call(
        body, name="rglru_fwd", grid=(RNN_BLOCKS, nt),
        in_specs=[tile, halo, tile, cwspec, vec, wblk, vec, wblk, vec, vec],
        out_specs=[tile, tile],
        out_shape=[jax.ShapeDtypeStruct((S, D_RNN), F32), jax.ShapeDtypeStruct((S, D_RNN), BF)],
        scratch_shapes=[pltpu.VMEM((8, RNN_BLOCK), F32)],
        compiler_params=_params(("parallel", "arbitrary")),
    )(xr, xr, g, cw, cb, wa, ba, wx, bx, lam)


def _rglru_bwd(xr, g, h, dy, cw, cb, wa, ba, wx, bx, lam, T):
    S = xr.shape[0]
    nt = S // T

    def body(u_ref, uh_ref, g_ref, h_ref, hh_ref, dy_ref, cw_ref, cb_ref, wa_ref, ba_ref, wx_ref, bx_ref, lam_ref,
             du_ref, dg_ref, dwa_ref, dwx_ref, dvec_ref, c_dhh, c_a, c_dconv):
        t = pl.program_id(1)
        tt = nt - 1 - t

        @pl.when(t == 0)
        def _():
            c_dhh[...] = jnp.zeros_like(c_dhh)
            c_a[...] = jnp.zeros_like(c_a)
            c_dconv[...] = jnp.zeros_like(c_dconv)
            dwa_ref[...] = jnp.zeros_like(dwa_ref)
            dwx_ref[...] = jnp.zeros_like(dwx_ref)
            dvec_ref[...] = jnp.zeros_like(dvec_ref)

        row = lax.broadcasted_iota(jnp.int32, (T, RNN_BLOCK), 0)
        row8 = row[:8]
        grow = row + tt * T
        head8 = jnp.where(tt > 0, uh_ref[...], 0.0)
        us, conv, cbf, r_, i_, sp, a, mult_raw, mult = _rglru_gates(
            u_ref[...], head8, grow, row, T, cw_ref, cb_ref, wa_ref, ba_ref, wx_ref, bx_ref, lam_ref)
        hv = h_ref[...]
        hprev = _shift_down(hv, 1, jnp.where(tt > 0, hh_ref[...], 0.0), row, T)
        gv = g_ref[...]
        sg = _sigmoid(gv)
        dyv = dy_ref[...]
        dg_ref[...] = (dyv * hv * (sg * (1.0 + gv * (1.0 - sg)))).astype(BF)
        d = dyv * (gv * sg)
        A = _shift_up(a, 1, c_a[...], row, T)
        s = 1
        while s < T:
            keep = row < T - s
            A_s = jnp.where(keep, pltpu.roll(A, T - s, 0), 1.0)
            d_s = jnp.where(keep, pltpu.roll(d, T - s, 0), 0.0)
            d = A * d_s + d
            A = A * A_s
            s *= 2
        dhh = d + A * c_dhh[0:1, :]
        da = dhh * hprev
        dconv = dhh * mult * i_
        di = dhh * mult * conv
        dmult = dhh * i_ * conv
        dla = da * a - jnp.where(grow == 0, 0.0, dmult * (a * a) / mult_raw)
        dr = dla * (-LRU_C * sp)
        dsp = jnp.sum(dla * (-LRU_C * r_), axis=0, keepdims=True)
        dza = dr * r_ * (1.0 - r_)
        dzx = di * i_ * (1.0 - i_)
        dza_b, dzx_b = dza.astype(BF), dzx.astype(BF)
        dconv = dconv + _dot(dza_b, wa_ref[0], NT) + _dot(dzx_b, wx_ref[0], NT)
        dwa_ref[0] += _dot(cbf, dza_b, TN)
        dwx_ref[0] += _dot(cbf, dzx_b, TN)
        lam = lam_ref[...]
        rows = [jnp.sum(dconv * us[k], axis=0, keepdims=True) for k in range(CONV_W)]
        rows += [jnp.sum(dconv, axis=0, keepdims=True), jnp.sum(dza, axis=0, keepdims=True),
                 jnp.sum(dzx, axis=0, keepdims=True), dsp * (-_sigmoid(-lam))]
        upd = jnp.zeros((8, RNN_BLOCK), F32)
        for j, rv in enumerate(rows):
            upd = upd + jnp.where(row8 == j, rv, 0.0)
        dvec_ref[...] += upd
        tail8 = c_dconv[...]
        du = dconv * cw_ref[0:1, :]
        for k in range(1, CONV_W):
            du = du + _shift_up(dconv, k, tail8, row, T) * cw_ref[k:k + 1, :]
        du_ref[...] = du.astype(BF)
        c_dhh[...] = jnp.broadcast_to(dhh[0:1, :], c_dhh.shape)
        c_a[...] = jnp.broadcast_to(a[0:1, :], c_a.shape)
        c_dconv[...] = dconv[:8]

    tile, halo, vec, cwspec, wblk = _rglru_specs(T, nt, True)
    acc8 = pl.BlockSpec((8, RNN_BLOCK), lambda n, t: (0, n))
    return pl.pallas_call(
        body, name="rglru_bwd", grid=(RNN_BLOCKS, nt),
        in_specs=[tile, halo, tile, tile, halo, tile, cwspec, vec, wblk, vec, wblk, vec, vec],
        out_specs=[tile, tile, wblk, wblk, acc8],
        out_shape=[jax.ShapeDtypeStruct((S, D_RNN), BF), jax.ShapeDtypeStruct((S, D_RNN), BF),
                   jax.ShapeDtypeStruct((RNN_BLOCKS, RNN_BLOCK, RNN_BLOCK), F32),
                   jax.ShapeDtypeStruct((RNN_BLOCKS, RNN_BLOCK, RNN_BLOCK), F32),
                   jax.ShapeDtypeStruct((8, D_RNN), F32)],
        scratch_shapes=[pltpu.VMEM((8, RNN_BLOCK), F32)] * 3,
        compiler_params=_params(("parallel", "arbitrary")),
    )(xr, xr, g, h, h, dy, cw, cb, wa, ba, wx, bx, lam)


def _rel_bucket_map():
    qi = np.arange(WINDOW)[:, None]
    kj = np.arange(2 * WINDOW)[None, :]
    dist = jnp.asarray(qi + WINDOW - kj, jnp.int32)
    n = jnp.maximum(dist, 0)
    max_exact = REL_BUCKETS // 2
    ratio = jnp.log(jnp.maximum(n, 1).astype(F32) / max_exact) / math.log(REL_MAX_DIST / max_exact)
    large = jnp.minimum(max_exact + (ratio * (REL_BUCKETS - max_exact)).astype(jnp.int32), REL_BUCKETS - 1)
    bucket = jnp.where(n < max_exact, n, large).astype(jnp.int32)
    j = np.arange(WINDOW)[None, :]
    return jnp.where(jnp.asarray(j > qi), bucket[:, :WINDOW], bucket[:, WINDOW:])


def _swa_common(n, kv_ref, bucket_ref, relb_ref, bias_scr):
    @pl.when(n == 0)
    def _():
        bk = bucket_ref[...]
        for h in range(SWA_HEADS):
            acc = jnp.zeros((WINDOW, WINDOW), F32)
            for b in range(REL_BUCKETS):
                acc = acc + jnp.where(bk == b, relb_ref[b, h], 0.0)
            bias_scr[h] = acc

    prev0 = pl.multiple_of(jnp.maximum(n - 1, 0) * WINDOW, WINDOW)
    cur0 = pl.multiple_of(n * WINDOW, WINDOW)
    kk = jnp.concatenate([kv_ref[pl.ds(prev0, WINDOW), :], kv_ref[pl.ds(cur0, WINDOW), :]], axis=0).astype(F32)
    rowi = lax.broadcasted_iota(jnp.int32, (WINDOW, WINDOW), 0)
    col = lax.broadcasted_iota(jnp.int32, (WINDOW, WINDOW), 1)
    from_prev = col > rowi
    return kk, from_prev, prev0, cur0


def _fold(full, from_prev):
    return jnp.where(from_prev, full[:, :WINDOW], full[:, WINDOW:])


def _unfold(sq, from_prev):
    return jnp.concatenate([jnp.where(from_prev, sq, 0.0), jnp.where(from_prev, 0.0, sq)], axis=1)


def _half_pair(part, kvh):
    lo = lax.broadcasted_iota(jnp.int32, part.shape, 1) < SWA_HD
    if kvh == 0:
        pa = jnp.where(lo, part, 0.0)
        pb = pltpu.roll(pa, SWA_HD, 1)
    else:
        pb = jnp.where(lo, 0.0, part)
        pa = pltpu.roll(pb, SWA_HD, 1)
    return pa.astype(BF), pb.astype(BF)


ALL_HEADS = SWA_HEADS * WINDOW


def _sink_column(sinks):
    return jnp.repeat(sinks.reshape(SWA_HEADS), WINDOW).reshape(ALL_HEADS, 1)


def _swa_operands(kk):
    return [(_half_pair(kk[:, :128], kvh), _half_pair(kk[:, 128:], kvh)) for kvh in range(SWA_KV_HEADS)]


def _swa_probs(n, q_ref, ops, bias_scr, sinkc_ref, from_prev):
    lgs = []
    for kvh in range(SWA_KV_HEADS):
        (ka, kb), _ = ops[kvh]
        for p in range(4):
            q2 = q_ref[:, kvh * 512 + p * 128:kvh * 512 + p * 128 + 128]
            lgs += [_fold(_dot(q2, ka, NT), from_prev), _fold(_dot(q2, kb, NT), from_prev)]
    lg = jnp.concatenate(lgs, axis=0) * (SWA_HD ** -0.5) + bias_scr[...].reshape(ALL_HEADS, WINDOW)
    rowi = jnp.bitwise_and(lax.broadcasted_iota(jnp.int32, (ALL_HEADS, WINDOW), 0), WINDOW - 1)
    col = lax.broadcasted_iota(jnp.int32, (ALL_HEADS, WINDOW), 1)
    no_prev = jnp.where(n > 0, 0, 4 * WINDOW)
    lg = jnp.where(jnp.logical_or(col <= rowi, col > rowi + no_prev), lg, NEG_INF)
    sink = sinkc_ref[...]
    m = jnp.maximum(jnp.max(lg, axis=-1, keepdims=True), sink)
    e = jnp.exp(lg - m)
    es = jnp.exp(sink - m)
    den = jnp.sum(e, axis=-1, keepdims=True) + es
    return e / den, es / den


def _swa_fwd(q, kv, g, bucket, rel_bias, sink_col):
    S = q.shape[0]
    nb = S // WINDOW

    def body(q_ref, kv_ref, g_ref, bucket_ref, relb_ref, sinkc_ref, o_ref, y_ref, bias_scr):
        n = pl.program_id(0)
        kk, from_prev, _, _ = _swa_common(n, kv_ref, bucket_ref, relb_ref, bias_scr)
        ops = _swa_operands(kk)
        pr, _ = _swa_probs(n, q_ref, ops, bias_scr, sinkc_ref, from_prev)
        for kvh in range(SWA_KV_HEADS):
            _, (va, vb) = ops[kvh]
            for p in range(4):
                c0 = kvh * 512 + p * 128
                r0 = (kvh * 8 + 2 * p) * WINDOW
                o2 = (_dot(_unfold(pr[r0:r0 + WINDOW], from_prev).astype(BF), va, NN)
                      + _dot(_unfold(pr[r0 + WINDOW:r0 + 2 * WINDOW], from_prev).astype(BF), vb, NN))
                o_ref[:, c0:c0 + 128] = o2
                gv = g_ref[:, c0:c0 + 128]
                y_ref[:, c0:c0 + 128] = (o2 * (gv * _sigmoid(gv))).astype(BF)

    blk = pl.BlockSpec((WINDOW, 1024), lambda n: (n, 0))
    smem = pl.BlockSpec(memory_space=pltpu.SMEM)
    sinkc = pl.BlockSpec((ALL_HEADS, 1), lambda n: (0, 0))
    return pl.pallas_call(
        body, name="swa_fwd", grid=(nb,),
        in_specs=[blk, pl.BlockSpec((S, 256), lambda n: (0, 0)), blk, pl.BlockSpec((WINDOW, WINDOW), lambda n: (0, 0)), smem, sinkc],
        out_specs=[blk, blk],
        out_shape=[jax.ShapeDtypeStruct((S, 1024), F32), jax.ShapeDtypeStruct((S, 1024), BF)],
        scratch_shapes=[pltpu.VMEM((SWA_HEADS, WINDOW, WINDOW), F32)],
        compiler_params=_params(("arbitrary",)),
    )(q, kv, g, bucket, rel_bias, sink_col)


def _swa_bwd(q, kv, g, o, dy, bucket, rel_bias, sink_col, others):
    S = q.shape[0]
    nb = S // WINDOW
    first_col = {name: c0 for name, c0, _, _ in SEGMENTS}
    col_q, col_g = first_col["q_s"], first_col["g_swa"]
    copies = []
    for name, c0, width, _ in SEGMENTS:
        if name not in ("q_s", "kv", "g_swa"):
            arrs = others[name] if name == "gl" else (others[name],)
            copies += [(a, c0 + i * (width // len(arrs))) for i, a in enumerate(arrs)]

    def body(q_ref, kv_ref, g_ref, o_ref, dy_ref, bucket_ref, relb_ref, sinkc_ref, *refs):
        copy_refs = refs[:len(copies)]
        dp_ref, dkv_ref, dsink_ref, drel_ref, bias_scr, dbias_scr, dsink_scr = refs[len(copies):]
        n = pl.program_id(0)
        for c_ref, (a, c0) in zip(copy_refs, copies):
            dp_ref[:, c0:c0 + a.shape[1]] = c_ref[...]

        @pl.when(n == 0)
        def _():
            dbias_scr[...] = jnp.zeros_like(dbias_scr)
            dsink_scr[...] = jnp.zeros_like(dsink_scr)
            dkv_ref[...] = jnp.zeros_like(dkv_ref)

        kk, from_prev, prev0, cur0 = _swa_common(n, kv_ref, bucket_ref, relb_ref, bias_scr)
        ops = _swa_operands(kk)
        pr, ps = _swa_probs(n, q_ref, ops, bias_scr, sinkc_ref, from_prev)
        do2s, dps = [], []
        for kvh in range(SWA_KV_HEADS):
            _, (va, vb) = ops[kvh]
            for p in range(4):
                c0 = kvh * 512 + p * 128
                gv = g_ref[:, c0:c0 + 128]
                sg = _sigmoid(gv)
                dyv = dy_ref[:, c0:c0 + 128]
                dp_ref[:, col_g + c0:col_g + c0 + 128] = (dyv * o_ref[:, c0:c0 + 128] * (sg * (1.0 + gv * (1.0 - sg)))).astype(BF)
                do2 = (dyv * (gv * sg)).astype(BF)
                do2s.append(do2)
                dps += [_fold(_dot(do2, va, NT), from_prev), _fold(_dot(do2, vb, NT), from_prev)]
        dp = jnp.concatenate(dps, axis=0)
        delta = jnp.sum(pr * dp, axis=-1, keepdims=True)
        ds = pr * (dp - delta)
        dbias_scr[...] += ds.reshape(SWA_HEADS, WINDOW, WINDOW)
        dsink_scr[...] += ps * delta
        dsc = ds * (SWA_HD ** -0.5)
        lo256 = lax.broadcasted_iota(jnp.int32, (2 * WINDOW, 128), 1) < SWA_HD
        dks, dvs = [], []
        for kvh in range(SWA_KV_HEADS):
            (ka, kb), _ = ops[kvh]
            dka = jnp.zeros((2 * WINDOW, 128), F32)
            dkb, dva, dvb = dka, dka, dka
            for p in range(4):
                c0 = kvh * 512 + p * 128
                r0 = (kvh * 8 + 2 * p) * WINDOW
                q2 = q_ref[:, c0:c0 + 128]
                do2 = do2s[kvh * 4 + p]
                ds0 = _unfold(dsc[r0:r0 + WINDOW], from_prev).astype(BF)
                ds1 = _unfold(dsc[r0 + WINDOW:r0 + 2 * WINDOW], from_prev).astype(BF)
                dp_ref[:, col_q + c0:col_q + c0 + 128] = (_dot(ds0, ka, NN) + _dot(ds1, kb, NN)).astype(BF)
                dka = dka + _dot(ds0, q2, TN)
                dkb = dkb + _dot(ds1, q2, TN)
                dva = dva + _dot(_unfold(pr[r0:r0 + WINDOW], from_prev).astype(BF), do2, TN)
                dvb = dvb + _dot(_unfold(pr[r0 + WINDOW:r0 + 2 * WINDOW], from_prev).astype(BF), do2, TN)
            dks.append(jnp.where(lo256, dka, 0.0) + pltpu.roll(jnp.where(lo256, 0.0, dkb), SWA_HD, 1))
            dvs.append(jnp.where(lo256, dva, 0.0) + pltpu.roll(jnp.where(lo256, 0.0, dvb), SWA_HD, 1))
        dk = dks[0] + pltpu.roll(dks[1], SWA_HD, 1)
        dv = dvs[0] + pltpu.roll(dvs[1], SWA_HD, 1)
        dkv_ref[pl.ds(prev0, WINDOW), 0:128] += dk[:WINDOW]
        dkv_ref[pl.ds(prev0, WINDOW), 128:256] += dv[:WINDOW]
        dkv_ref[pl.ds(cur0, WINDOW), 0:128] += dk[WINDOW:]
        dkv_ref[pl.ds(cur0, WINDOW), 128:256] += dv[WINDOW:]

        @pl.when(n == nb - 1)
        def _():
            dsink_ref[...] = -jnp.sum(dsink_scr[...].reshape(SWA_HEADS, WINDOW, 1), axis=1)
            bk = bucket_ref[...]
            sums = []
            for b in range(REL_BUCKETS):
                sums.append(jnp.sum(jnp.where((bk == b)[None], dbias_scr[...], 0.0), axis=1))
            drel_ref[...] = jnp.sum(jnp.concatenate(sums, axis=0), axis=1, keepdims=True)

    blk = pl.BlockSpec((WINDOW, 1024), lambda n: (n, 0))
    smem = pl.BlockSpec(memory_space=pltpu.SMEM)
    whole = lambda shape: pl.BlockSpec(shape, lambda n: (0, 0))
    return pl.pallas_call(
        body, name="swa_bwd", grid=(nb,),
        in_specs=[blk, whole((S, 256)), blk, blk, blk, whole((WINDOW, WINDOW)), smem, whole((ALL_HEADS, 1))]
        + [pl.BlockSpec((WINDOW, a.shape[1]), lambda n: (n, 0)) for a, _ in copies],
        out_specs=[pl.BlockSpec((WINDOW, D_IN), lambda n: (n, 0)), whole((S, 256)), whole((SWA_HEADS, 1)),
                   whole((REL_BUCKETS * SWA_HEADS, 1))],
        out_shape=[jax.ShapeDtypeStruct((S, D_IN), BF), jax.ShapeDtypeStruct((S, 256), F32), jax.ShapeDtypeStruct((SWA_HEADS, 1), F32),
                   jax.ShapeDtypeStruct((REL_BUCKETS * SWA_HEADS, 1), F32)],
        scratch_shapes=[pltpu.VMEM((SWA_HEADS, WINDOW, WINDOW), F32), pltpu.VMEM((SWA_HEADS, WINDOW, WINDOW), F32),
                        pltpu.VMEM((ALL_HEADS, 1), F32)],
        compiler_params=_params(("arbitrary",)),
    )(q, kv, g, o, dy, bucket, rel_bias, sink_col, *[a for a, _ in copies])


def _mem_probs(qh, mk):
    lg = _dot(qh, mk, NT) * (MEM_HD ** -0.5)
    e = jnp.exp(lg - jnp.max(lg, axis=-1, keepdims=True))
    return e / jnp.sum(e, axis=-1, keepdims=True)


def _mem_fwd(q, mkv, g):
    S = q.shape[0]
    M = mkv.shape[0]
    tq = 256

    def body(q_ref, mkv_ref, g_ref, o_ref, y_ref):
        for h in range(MEM_HEADS):
            c0 = h * MEM_HD
            pr = _mem_probs(q_ref[:, c0:c0 + MEM_HD], mkv_ref[:, c0:c0 + MEM_HD])
            o = _dot(pr.astype(BF), mkv_ref[:, D_MEM + c0:D_MEM + c0 + MEM_HD], NN)
            o_ref[:, c0:c0 + MEM_HD] = o
            gv = g_ref[:, c0:c0 + MEM_HD]
            y_ref[:, c0:c0 + MEM_HD] = (o * (gv * _sigmoid(gv))).astype(BF)

    blk = pl.BlockSpec((tq, D_MEM), lambda i: (i, 0))
    return pl.pallas_call(
        body, name="mem_fwd", grid=(S // tq,),
        in_specs=[blk, pl.BlockSpec((M, 2 * D_MEM), lambda i: (0, 0)), blk], out_specs=[blk, blk],
        out_shape=[jax.ShapeDtypeStruct((S, D_MEM), F32), jax.ShapeDtypeStruct((S, D_MEM), BF)],
        compiler_params=_params(("parallel",)),
    )(q, mkv, g)


def _mem_bwd(q, mkv, g, o, dy):
    S = q.shape[0]
    M = mkv.shape[0]
    tq = 256

    def body(q_ref, mkv_ref, g_ref, o_ref, dy_ref, dq_ref, dg_ref, dmkv_ref):
        @pl.when(pl.program_id(0) == 0)
        def _():
            dmkv_ref[...] = jnp.zeros_like(dmkv_ref)

        for h in range(MEM_HEADS):
            c0 = h * MEM_HD
            qh = q_ref[:, c0:c0 + MEM_HD]
            mk = mkv_ref[:, c0:c0 + MEM_HD]
            mv = mkv_ref[:, D_MEM + c0:D_MEM + c0 + MEM_HD]
            gv = g_ref[:, c0:c0 + MEM_HD]
            sg = _sigmoid(gv)
            dyv = dy_ref[:, c0:c0 + MEM_HD]
            dg_ref[:, c0:c0 + MEM_HD] = (dyv * o_ref[:, c0:c0 + MEM_HD] * (sg * (1.0 + gv * (1.0 - sg)))).astype(BF)
            do = (dyv * (gv * sg)).astype(BF)
            pr = _mem_probs(qh, mk)
            dp = _dot(do, mv, NT)
            ds = pr * (dp - jnp.sum(pr * dp, axis=-1, keepdims=True))
            dsb = (ds * (MEM_HD ** -0.5)).astype(BF)
            dq_ref[:, c0:c0 + MEM_HD] = _dot(dsb, mk, NN).astype(BF)
            dmkv_ref[:, c0:c0 + MEM_HD] += _dot(dsb, qh, TN)
            dmkv_ref[:, D_MEM + c0:D_MEM + c0 + MEM_HD] += _dot(pr.astype(BF), do, TN)

    blk = pl.BlockSpec((tq, D_MEM), lambda i: (i, 0))
    whole = pl.BlockSpec((M, 2 * D_MEM), lambda i: (0, 0))
    return pl.pallas_call(
        body, name="mem_bwd", grid=(S // tq,),
        in_specs=[blk, whole, blk, blk, blk], out_specs=[blk, blk, whole],
        out_shape=[jax.ShapeDtypeStruct((S, D_MEM), BF), jax.ShapeDtypeStruct((S, D_MEM), BF),
                   jax.ShapeDtypeStruct((M, 2 * D_MEM), F32)],
        compiler_params=_params(("arbitrary",)),
    )(q, mkv, g, o, dy)


MERGE_TN = 512


def _merge_specs(tm):
    ytile = pl.BlockSpec((tm, 1024), lambda i, j: (i, 0))
    wblk = pl.BlockSpec((MERGE_TN, 1024), lambda i, j: (j, 0))
    gls = [pl.BlockSpec((None, tm, MERGE_TN), (lambda i, j, br=br: (br, i, j))) for br in range(3)]
    otile = pl.BlockSpec((tm, MERGE_TN), lambda i, j: (i, j))
    return ytile, wblk, gls, otile


def _merge_fwd(ys, ws, gl, tm):
    S = gl.shape[1]

    def body(y0, y1, y2, w0, w1, w2, g0, g1, g2, o_ref):
        acc = None
        for y_ref, w_ref, g_ref in ((y0, w0, g0), (y1, w1, g1), (y2, w2, g2)):
            term = _sigmoid(g_ref[...]) * _dot(y_ref[...], w_ref[...], NT)
            acc = term if acc is None else acc + term
        o_ref[...] = acc.astype(BF)

    ytile, wblk, gls, otile = _merge_specs(tm)
    return pl.pallas_call(
        body, name="merge_fwd", grid=(S // tm, D_MODEL // MERGE_TN),
        in_specs=[ytile] * 3 + [wblk] * 3 + gls, out_specs=otile,
        out_shape=jax.ShapeDtypeStruct((S, D_MODEL), BF),
        compiler_params=_params(("parallel", "arbitrary")),
    )(*ys, *ws, gl, gl, gl)


def _merge_bwd(dout, w_out, ys, ws, gl, tm):
    S = gl.shape[1]

    def body(do_ref, wo_ref, y0, y1, y2, w0, w1, w2, g0, g1, g2, dg0, dg1, dg2, dp0, dp1, dp2):
        dm = _dot(do_ref[...], wo_ref[...], NT)
        for y_ref, w_ref, g_ref, dg_ref, dp_ref in ((y0, w0, g0, dg0, dp0), (y1, w1, g1, dg1, dp1), (y2, w2, g2, dg2, dp2)):
            gate = _sigmoid(g_ref[...])
            pv = _dot(y_ref[...], w_ref[...], NT)
            dg_ref[...] = (dm * pv * gate * (1.0 - gate)).astype(BF)
            dp_ref[...] = (dm * gate).astype(BF)

    ytile, wblk, gls, otile = _merge_specs(tm)
    out = jax.ShapeDtypeStruct((S, D_MODEL), BF)
    return pl.pallas_call(
        body, name="merge_bwd", grid=(S // tm, D_MODEL // MERGE_TN),
        in_specs=[pl.BlockSpec((tm, D_MODEL), lambda i, j: (i, 0)), pl.BlockSpec((MERGE_TN, D_MODEL), lambda i, j: (j, 0))]
        + [ytile] * 3 + [wblk] * 3 + gls,
        out_specs=[otile] * 6, out_shape=[out] * 6,
        compiler_params=_params(("parallel", "arbitrary")),
    )(dout, w_out, *ys, *ws, gl, gl, gl)


def _out_loss(merged, w_out, x, target, post_g, tm):
    S = x.shape[0]

    def body(m_ref, w_ref, x_ref, t_ref, g_ref, dout_ref, dy_ref, loss_ref, dpost_ref):
        @pl.when(pl.program_id(0) == 0)
        def _():
            loss_ref[...] = jnp.zeros_like(loss_ref)
            dpost_ref[...] = jnp.zeros_like(dpost_ref)

        out = _dot(m_ref[...], w_ref[...], NN)
        r = lax.rsqrt(jnp.mean(out * out, axis=-1, keepdims=True) + EPS)
        nrm = out * r
        gv = g_ref[...]
        err = (x_ref[...] + nrm * gv) - t_ref[...]
        sq = jnp.sum(jnp.sum(err * err, axis=1, keepdims=True), axis=0, keepdims=True)
        loss_ref[...] += sq * (0.5 / D_MODEL)
        dy = err * (1.0 / D_MODEL)
        dy_ref[...] = dy
        dpost_ref[...] += jnp.sum(dy * nrm, axis=0, keepdims=True)
        dn = dy * gv
        dout_ref[...] = (r * (dn - nrm * jnp.mean(dn * nrm, axis=-1, keepdims=True))).astype(BF)

    row = pl.BlockSpec((tm, D_MODEL), lambda i: (i, 0))
    return pl.pallas_call(
        body, name="out_loss", grid=(S // tm,),
        in_specs=[row, pl.BlockSpec((D_MODEL, D_MODEL), lambda i: (0, 0)), row, row, pl.BlockSpec((1, D_MODEL), lambda i: (0, 0))],
        out_specs=[row, row, pl.BlockSpec((8, 128), lambda i: (0, 0)), pl.BlockSpec((1, D_MODEL), lambda i: (0, 0))],
        out_shape=[jax.ShapeDtypeStruct((S, D_MODEL), BF), jax.ShapeDtypeStruct((S, D_MODEL), F32),
                   jax.ShapeDtypeStruct((8, 128), F32), jax.ShapeDtypeStruct((1, D_MODEL), F32)],
        compiler_params=_params(("arbitrary",)),
    )(merged, w_out, x, target, post_g)


DH_DX_CHUNK = 64


def _dh_dx(dproj, w_in, x, dy, pre_g, tm, tk):
    S = x.shape[0]
    nk = D_IN // tk

    def body(dp_ref, w_ref, x_ref, dy_ref, g_ref, dx_ref, dpre_ref, acc_ref):
        i, k = pl.program_id(0), pl.program_id(1)

        @pl.when(jnp.logical_and(i == 0, k == 0))
        def _():
            dpre_ref[...] = jnp.zeros_like(dpre_ref)

        @pl.when(k == 0)
        def _():
            acc_ref[...] = jnp.zeros_like(acc_ref)

        acc_ref[...] += _dot(dp_ref[...], w_ref[...], NN)

        @pl.when(k == nk - 1)
        def _():
            def chunk(c, carry):
                rows = pl.ds(pl.multiple_of(c * DH_DX_CHUNK, DH_DX_CHUNK), DH_DX_CHUNK)
                dh = acc_ref[rows, :]
                xv = x_ref[rows, :]
                r = lax.rsqrt(jnp.mean(xv * xv, axis=-1, keepdims=True) + EPS)
                nrm = xv * r
                dpre_ref[...] += jnp.sum(dh * nrm, axis=0, keepdims=True)
                dn = dh * g_ref[...]
                dx_ref[rows, :] = r * (dn - nrm * jnp.mean(dn * nrm, axis=-1, keepdims=True)) + dy_ref[rows, :]
                return carry
            lax.fori_loop(0, tm // DH_DX_CHUNK, chunk, 0)

    row = pl.BlockSpec((tm, D_MODEL), lambda i, k: (i, 0))
    vec = pl.BlockSpec((1, D_MODEL), lambda i, k: (0, 0))
    return pl.pallas_call(
        body, name="dh_dx", grid=(S // tm, nk),
        in_specs=[pl.BlockSpec((tm, tk), lambda i, k: (i, k)), pl.BlockSpec((tk, D_MODEL), lambda i, k: (k, 0)), row, row, vec],
        out_specs=[row, vec],
        out_shape=[jax.ShapeDtypeStruct((S, D_MODEL), F32), jax.ShapeDtypeStruct((1, D_MODEL), F32)],
        scratch_shapes=[pltpu.VMEM((tm, D_MODEL), F32)],
        compiler_params=_params(("arbitrary", "arbitrary"), large=True),
    )(dproj, w_in, x, dy, pre_g)


def _sum_parts(parts, name):
    P, R, C = parts.shape
    tr = max(t for t in range(8, 513, 8) if R % t == 0)

    def body(p_ref, o_ref):
        acc = p_ref[0]
        for j in range(1, P):
            acc = acc + p_ref[j]
        o_ref[...] = acc

    return pl.pallas_call(
        body, name=name, grid=(R // tr,),
        in_specs=[pl.BlockSpec((P, tr, C), lambda i: (0, i, 0))], out_specs=pl.BlockSpec((tr, C), lambda i: (i, 0)),
        out_shape=jax.ShapeDtypeStruct((R, C), F32), compiler_params=_params(("parallel",)),
    )(parts)


def _adamw(land, sums, chip, w, m, v, name, group=(0, 1), into=None):
    q, n_groups = group
    _, R, cols = land.shape
    C = cols * n_groups
    tr = max(t for t in range(16, 257, 16) if R % t == 0)
    c1 = 1.0 - ADAM_B1 ** ADAM_STEP
    c2 = 1.0 - ADAM_B2 ** ADAM_STEP
    n_into = 0 if into is None else 4

    def body(chip_ref, p0_ref, p1_ref, p2_ref, own_ref, w_ref, m_ref, v_ref, *refs):
        g_ref, d_ref, nm_ref, nv_ref = refs[n_into:]
        g = own_ref[...].astype(F32)
        for p_ref in (p0_ref, p1_ref, p2_ref):
            g = g + p_ref[...].astype(F32)
        nm = ADAM_B1 * m_ref[...] + (1.0 - ADAM_B1) * g
        nv = ADAM_B2 * v_ref[...] + (1.0 - ADAM_B2) * (g * g)
        g_ref[...] = g
        nm_ref[...] = nm
        nv_ref[...] = nv
        d_ref[...] = -ADAM_LR * ((nm / c1) / (jnp.sqrt(nv / c2) + ADAM_EPS) + ADAM_WD * w_ref[...])

    tile = pl.BlockSpec((None, tr, cols), lambda i, c_ref: (0, i, q))
    specs = [pl.BlockSpec((None, tr, cols), (lambda i, c_ref, k=k: (k + (c_ref[0] <= k).astype(jnp.int32), i, 0))) for k in range(3)]
    specs.append(pl.BlockSpec((None, tr, cols), (lambda i, c_ref: (c_ref[0], i, 0))))
    return pl.pallas_call(
        body, name=name,
        grid_spec=pltpu.PrefetchScalarGridSpec(num_scalar_prefetch=1, grid=(R // tr,),
                                               in_specs=specs + [tile, tile, tile] + [pl.BlockSpec(memory_space=pl.ANY)] * n_into,
                                               out_specs=[tile] * 4),
        out_shape=[jax.ShapeDtypeStruct((1, R, C), F32)] * 4,
        input_output_aliases={8 + k: k for k in range(n_into)},
        compiler_params=_params(("parallel",)),
    )(chip, land, land, land, sums, w, m, v, *(into or []))


def _adamw_small(gs, ws, ms, vs):
    n = len(ws)
    c1 = 1.0 - ADAM_B1 ** ADAM_STEP
    c2 = 1.0 - ADAM_B2 ** ADAM_STEP

    def flat2(a):
        return a.reshape(-1, a.shape[-1])

    def body(*refs):
        ins, outs = refs[:4 * n], refs[4 * n:]
        for a in range(n):
            g, w, m, v = (ins[k * n + a][...] for k in range(4))
            nm = ADAM_B1 * m + (1.0 - ADAM_B1) * g
            nv = ADAM_B2 * v + (1.0 - ADAM_B2) * (g * g)
            outs[a][...] = g
            outs[n + a][...] = -ADAM_LR * ((nm / c1) / (jnp.sqrt(nv / c2) + ADAM_EPS) + ADAM_WD * w)
            outs[2 * n + a][...] = nm
            outs[3 * n + a][...] = nv

    shapes = [flat2(w).shape for w in ws]
    out = pl.pallas_call(
        body, name="adamw_small", out_shape=[jax.ShapeDtypeStruct(sh, F32) for sh in shapes] * 4,
        compiler_params=_params(),
    )(*[g.reshape(sh) for g, sh in zip(gs, shapes)], *[flat2(a) for a in (*ws, *ms, *vs)])
    return [[out[k * n + a].reshape(ws[a].shape) for a in range(n)] for k in range(4)]


PROJ_ROWS = 512


def _project(h, w_t, dep=None):
    S = h.shape[0]
    n_tiles = D_IN // SEG_TILE
    ranges = [(c0 // SEG_TILE, (c0 + width) // SEG_TILE) for _, c0, width, _ in SEGMENTS]

    n_extra = int(dep is not None)

    def body(h_ref, w_ref, *outs):
        outs = outs[n_extra:]
        j = pl.program_id(0)
        for (j0, j1), (_, _, _, dt), o_ref in zip(ranges, SEGMENTS, outs):
            @pl.when(jnp.logical_and(j >= j0, j < j1))
            def _(o_ref=o_ref, dt=dt):
                for c in range(S // PROJ_ROWS):
                    rows = pl.ds(c * PROJ_ROWS, PROJ_ROWS)
                    o_ref[rows, :] = _dot(h_ref[rows, :], w_ref[...], NT).astype(dt)

    out_shapes, out_specs = [], []
    for (j0, j1), (name, _, width, dt) in zip(ranges, SEGMENTS):
        if name == "gl":
            per = (j1 - j0) // 3
            out_shapes.append(jax.ShapeDtypeStruct((3, S, width // 3), dt))
            out_specs.append(pl.BlockSpec((None, S, SEG_TILE), (lambda j, j0=j0, j1=j1, per=per: (
                jnp.clip(j - j0, 0, j1 - j0 - 1) // per, 0, jnp.clip(j - j0, 0, j1 - j0 - 1) % per))))
        else:
            out_shapes.append(jax.ShapeDtypeStruct((S, width), dt))
            out_specs.append(pl.BlockSpec((S, SEG_TILE), (lambda j, j0=j0, j1=j1: (0, jnp.clip(j - j0, 0, j1 - j0 - 1)))))
    outs = pl.pallas_call(
        body, name="proj", grid=(n_tiles,),
        in_specs=[pl.BlockSpec((S, D_MODEL), lambda j: (0, 0)), pl.BlockSpec((SEG_TILE, D_MODEL), lambda j: (j, 0))]
        + ([] if dep is None else [pl.BlockSpec((8, 128), lambda j: (0, 0))]),
        out_specs=out_specs, out_shape=out_shapes,
        compiler_params=_params(("arbitrary",), large=True),
    )(h, w_t, *([] if dep is None else [dep]))
    return {name: o for (name, _, _, _), o in zip(SEGMENTS, outs)}


def _forward_a(h, memn, w_in, sinks, rel_bias, dep=None):
    S = h.shape[0]
    st = dict(T=min(512, S // 2), tm=min(512, S), bucket=_rel_bucket_map(), h=h, memn=memn)
    seg = st["seg"] = _project(h, w_in, dep)
    st["o_swa"], st["y_swa"] = _swa_fwd(seg["q_s"], seg["kv"], seg["g_swa"], st["bucket"], rel_bias, _sink_column(sinks))
    return st


def _forward_rg(st, conv_w, conv_b, w_a, b_a, w_x, b_x, lam):
    seg = st["seg"]
    st["h_rg"], st["y_rg"] = _rglru_fwd(seg["xr"], seg["g_rg"], conv_w, conv_b, w_a, b_a, w_x, b_x, lam, st["T"])
    return st


def _forward_b(st, x, target, post_g, w_memkv, wbr, w_out):
    S = x.shape[0]
    M = st["memn"].shape[0]
    seg = st["seg"]
    st["mkv"] = _matmul(st["memn"], w_memkv, "nn", M, 2 * D_MEM, D_MODEL, M, 512, D_MODEL, BF, "mem_kv")
    st["o_mem"], st["y_mem"] = _mem_fwd(seg["q_m"], st["mkv"], seg["g_mem"])
    st["ys"] = (st["y_rg"], st["y_swa"], st["y_mem"])
    st["merged"] = _merge_fwd(st["ys"], wbr, seg["gl"], st["tm"])
    st["dout"], st["dy"], st["loss"], st["dpost"] = _out_loss(st["merged"], w_out, x, target, post_g, min(256, S))
    return st


def _backward_a1(st, wbr, w_out):
    S = st["h"].shape[0]
    seg, ys, tm = st["seg"], st["ys"], st["tm"]
    st["dw_out"] = _matmul(st["merged"], st["dout"], "tn", D_MODEL, D_MODEL, S, 256, D_MODEL, S, BF, "dw_out", out_blocked="row")
    dgl0, dgl1, dgl2, dp0, dp1, dp2 = _merge_bwd(st["dout"], w_out, ys, wbr, seg["gl"], tm)
    st["dgl"] = (dgl0, dgl1, dgl2)
    dys, dwbr = [], []
    for i, dp in enumerate((dp0, dp1, dp2)):
        dys.append(_matmul(dp, wbr[i], "nn", S, 1024, D_MODEL, tm, 1024, D_MODEL, F32, "dy_br%d" % i))
        dwbr.append(_matmul(ys[i], dp, "tn", 1024, D_MODEL, S, 1024, 256, S, BF, "dw_br%d" % i, out_blocked="col"))
    st["dys"], st["dwbr"] = dys, dwbr
    return st


def _backward_a2(st, mem, w_memkv, conv_w, conv_b, w_a, b_a, w_x, b_x, lam):
    M = mem.shape[0]
    seg, dys = st["seg"], st["dys"]
    st["dq_m"], st["dg_mem"], dmkv = _mem_bwd(seg["q_m"], st["mkv"], seg["g_mem"], st["o_mem"], dys[2])
    st["dmkv"] = dmkv.astype(BF)
    st["dw_memkv"] = _matmul(st["memn"], st["dmkv"], "tn", D_MODEL, 2 * D_MEM, M, 256, 2 * D_MEM, M, BF, "dw_memkv", out_blocked="row")
    st["dxr"], st["dg_rg"], st["dw_a"], st["dw_x"], st["dvec"] = _rglru_bwd(
        seg["xr"], seg["g_rg"], st["h_rg"], dys[0], conv_w, conv_b, w_a, b_a, w_x, b_x, lam, st["T"])
    return st


def _mem_gain_grad(st, mem, w_memkv, dep=None):
    M = mem.shape[0]
    dmemn = _matmul(st["dmkv"], w_memkv, "nt", M, D_MODEL, 2 * D_MEM, M, 512, 2 * D_MEM, F32, "dmemn", dep=dep)
    return _rms_gain_grad(dmemn, mem, "dmem_gain")


def _backward_b(st, rel_bias, sinks):
    seg = st["seg"]
    others = {"xr": st["dxr"], "g_rg": st["dg_rg"], "q_m": st["dq_m"], "g_mem": st["dg_mem"], "gl": st["dgl"]}
    dproj, dkv, dsinks, drel = _swa_bwd(seg["q_s"], seg["kv"], seg["g_swa"], st["o_swa"], st["dys"][1],
                                        st["bucket"], rel_bias, _sink_column(sinks), others)
    st["dsinks"], st["drel"] = dsinks.reshape(1, SWA_HEADS), drel.reshape(REL_BUCKETS, SWA_HEADS)
    col_kv = [c0 for name, c0, _, _ in SEGMENTS if name == "kv"][0]
    st["dproj"] = lax.dynamic_update_slice(dproj, dkv.astype(BF), (0, col_kv))
    return st


def _dw_in_half(st, half, dep=None):
    S = st["h"].shape[0]
    dw = _matmul(st["dproj"], st["h"], "tn", D_IN, D_MODEL // 2, S, D_IN_TILE, D_MODEL // 2, S, BF, "dw_in%d" % half, b_noff=half, dep=dep)
    return dw.reshape(N_DEV, D_IN // N_DEV, D_MODEL // 2)


def _owner_blocks(a):
    return jnp.swapaxes(a.reshape((4, 2) + a.shape[1:]), 0, 1)


def _pad_rows(a, rows):
    a = a.reshape(-1, 128) if a.shape[-1] % 128 == 0 else jnp.pad(a, ((0, 0), (0, 128 - a.shape[-1])))
    return jnp.pad(a, ((0, rows - a.shape[0]), (0, 0))) if a.shape[0] < rows else a


def kernel(x, mem, pre_norm_g, post_norm_g, mem_norm_g, w_in, conv_w, conv_b, w_rg_a, b_rg_a, w_rg_x, b_rg_x, lru_lambda, swa_sinks, rel_bias, w_mem_kv, w_br_rg, w_br_swa, w_br_mem, w_out, loss_target, m_pre_norm_g, m_post_norm_g, m_mem_norm_g, m_w_in, m_conv_w, m_conv_b, m_w_rg_a, m_b_rg_a, m_w_rg_x, m_b_rg_x, m_lru_lambda, m_swa_sinks, m_rel_bias, m_w_mem_kv, m_w_br_rg, m_w_br_swa, m_w_br_mem, m_w_out, v_pre_norm_g, v_post_norm_g, v_mem_norm_g, v_w_in, v_conv_w, v_conv_b, v_w_rg_a, v_b_rg_a, v_w_rg_x, v_b_rg_x, v_lru_lambda, v_swa_sinks, v_rel_bias, v_w_mem_kv, v_w_br_rg, v_w_br_swa, v_w_br_mem, v_w_out):
    cx, cy, cc = lax.axis_index("x"), lax.axis_index("y"), lax.axis_index("c")
    me = 4 * cx + 2 * cy + cc
    chip = 2 * cx + cy
    core = jnp.reshape(cc, (1,)).astype(jnp.int32)
    x0, mem0 = x[0], mem[0]
    w_a_b, w_x_b = w_rg_a[0].astype(BF), w_rg_x[0].astype(BF)

    def landing(own, slot, slots):
        return lax.dynamic_update_slice(lax.empty((slots,) + own.shape, own.dtype), own[None], (slot,) + (0,) * own.ndim)


    def swap_start(parts, tag):
        return _exchange_start(parts, [lax.empty((4,) + p.shape[-2:], p.dtype) for p in parts], _plan_swap([p.ndim for p in parts]),
                               "swap_%s_start" % tag)

    def scatter_start(swap, after, tag, prefill=()):
        s_send, s_recv, parts, got, _ = swap
        got = _exchange_wait(s_send, s_recv, parts, got, _plan_swap([p.ndim for p in parts]), after, "swap_%s_wait" % tag)
        sums = [_pair_sum(p, g, core, "scatter_%s_sum%d" % (tag, i)) for i, (p, g) in enumerate(zip(parts, got))]
        lands = [landing(lax.dynamic_index_in_dim(s, chip, 0, keepdims=False), chip, 4) if i in prefill
                 else lax.empty(s.shape, s.dtype) for i, s in enumerate(sums)]
        return _exchange_start(sums, lands, _plan_scatter(len(sums)), "scatter_%s_start" % tag)

    def corner(a):
        return a.reshape(-1, a.shape[-1])[:8, :128]

    def zero_after(a):
        return jnp.minimum(jnp.abs(a.reshape(-1)[0].astype(F32)), 0.0)

    g_in, g_cw, h0, memn0 = _all_gather_relayed([jnp.transpose(w_in[0]).astype(BF), conv_w[0]], [True, False], "gather_w_in",
                                                side=_rms_side([(x0, pre_norm_g), (mem0, mem_norm_g)]))
    w_in_f = g_in.reshape(D_IN, D_MODEL)
    conv_w_f = jnp.transpose(g_cw, (1, 0, 2)).reshape(CONV_W, D_RNN)

    after_first = zero_after(g_cw).astype(BF)
    rest = [w.astype(BF) + after_first for w in (w_mem_kv[0], jnp.transpose(w_br_rg[0]), jnp.transpose(w_br_swa[0]),
                                                 jnp.transpose(w_br_mem[0]), w_out[0])]
    plan_g = _plan_gather(len(rest))
    zones = _place_own([lax.empty((N_DEV,) + w.shape, w.dtype) for w in rest], rest, jnp.reshape(me, (1,)).astype(jnp.int32), "gather_rest_own")
    g_send, g_recv, g_src, g_land, g_token = _exchange_start(rest, zones, plan_g, "gather_rest_start")
    st = _forward_a(h0, memn0, w_in_f, swa_sinks, rel_bias, dep=g_token)
    g_land = _exchange_wait(g_send, g_recv, g_src, g_land, plan_g, st["y_swa"], "gather_rest_wait")
    plan_f = _plan_forward(len(rest))
    f_send, f_recv, _, g_land, f_token = _exchange_start(None, g_land, plan_f, "forward_rest_start")
    st = _forward_rg(st, conv_w_f, conv_b + f_token[0:1, 0:1], w_a_b, b_rg_a, w_x_b, b_rg_x, lru_lambda)
    g_land = _exchange_wait(f_send, f_recv, None, g_land, plan_f, corner(st["y_rg"]), "forward_rest_wait")
    w_memkv_f = g_land[0].reshape(D_MODEL, 2 * D_MEM)
    wbr = tuple(g_land[i].reshape(D_MODEL, D_RNN) for i in (1, 2, 3))
    w_out_f = g_land[4].reshape(D_MODEL, D_MODEL)

    st = _forward_b(st, x0, loss_target[0], post_norm_g, w_memkv_f, wbr, w_out_f)
    st = _backward_a1(st, wbr, w_out_f)
    parts_a = [st["dw_out"], st["dwbr"][0], st["dwbr"][1], st["dwbr"][2]]
    plan_a = _plan_scatter(len(parts_a))
    swap_a = swap_start(parts_a, "a")
    st = _backward_a2(st, mem0, w_memkv_f, conv_w_f, conv_b + swap_a[4][0:1, 0:1], w_a_b, b_rg_a, w_x_b, b_rg_x, lru_lambda)
    a_send, a_recv, a_src, a_land, a_token = scatter_start(swap_a, st["dxr"], "a")
    parts_c = [st["dw_memkv"], _owner_blocks(st["dw_a"]), _owner_blocks(st["dw_x"])]
    plan_c = _plan_scatter(len(parts_c))
    swap_c = swap_start(parts_c, "c")

    st = _backward_b(st, rel_bias, swa_sinks + swap_c[4][0:1, 0:1] + a_token[0:1, 0:1])
    c_send, c_recv, c_src, c_land, c_token = scatter_start(swap_c, st["dsinks"], "c", prefill=(1, 2))
    plan_b = _plan_scatter(1)

    def dw_in_parts(half, dep):
        dwh = _dw_in_half(st, half, dep)
        return dwh, [dwh]

    dw0, parts_b0 = dw_in_parts(0, c_token)
    swap_b0 = swap_start(parts_b0, "b0")
    a_land = _exchange_wait(a_send, a_recv, a_src, a_land, plan_a, swap_b0[4], "scatter_a_wait")
    big = [None] * 6

    chip1 = jnp.reshape(chip, (1,)).astype(jnp.int32)

    def adamw_big(j, land, own, wt, mt, vt):
        big[j] = _adamw(land, own, chip1, wt, mt, vt, "adamw_big%d" % j)

    adamw_big(5, a_land[0], a_src[0], w_out, m_w_out, v_w_out)
    adamw_big(2, a_land[1], a_src[1], w_br_rg, m_w_br_rg, v_w_br_rg)
    adamw_big(3, a_land[2], a_src[2], w_br_swa, m_w_br_swa, v_w_br_swa)
    halves = [scatter_start(swap_b0, corner(big[5][1]) + corner(big[2][1]) + corner(big[3][1]), "b0")]
    dw1, parts_b1 = dw_in_parts(1, halves[0][4])
    swap_b1 = swap_start(parts_b1, "b1")
    c_land = _exchange_wait(c_send, c_recv, c_src, c_land, plan_c, swap_b1[4], "scatter_c_wait")
    g_wa_blk = _sum_parts(c_land[1], "sum_w_rg_a")
    g_wx_blk = _sum_parts(c_land[2], "sum_w_rg_x")
    adamw_big(4, a_land[3], a_src[3], w_br_mem, m_w_br_mem, v_w_br_mem)
    adamw_big(1, c_land[0], c_src[0], w_mem_kv, m_w_mem_kv, v_w_mem_kv)
    halves.append(scatter_start(swap_b1, corner(big[4][1]) + corner(big[1][1]), "b1"))
    st["dmem_g"] = _mem_gain_grad(st, mem0, w_memkv_f, halves[1][4])
    grad_x, dpre = _dh_dx(st["dproj"], w_in_f, x0, st["dy"], pre_norm_g + halves[1][4][0:1, 0:1], st["tm"], D_IN_TILE)
    pack = jnp.concatenate([dpre.reshape(16, 128), st["dpost"].reshape(16, 128), st["dmem_g"].reshape(16, 128),
                            st["dvec"].reshape(64, 128), _pad_rows(st["dsinks"], 8), _pad_rows(st["drel"], 32), g_wa_blk, g_wx_blk,
                            st["loss"]], axis=0)
    plan_s = _plan_everyone(1)
    s_send, s_recv, s_src, s_land, s_token = _exchange_start([pack], [landing(pack, me, N_DEV)], plan_s, "gather_small_start")
    swap_last = lambda a: jnp.transpose(a, (0, 2, 1))
    after, big0_t = s_token, None
    for half, (b_send, b_recv, b_src, b_land, _) in enumerate(halves):
        b_land = _exchange_wait(b_send, b_recv, b_src, b_land, plan_b, after, "scatter_b%d_wait" % half)[0]
        big0_t = _adamw(b_land, b_src[0], chip1, swap_last(w_in), swap_last(m_w_in), swap_last(v_w_in), "adamw_big0_%d" % half,
                        group=(half, 2), into=big0_t)
        after = corner(big0_t[1])
    big[0] = [swap_last(a) for a in big0_t]
    gathered = _exchange_wait(s_send, s_recv, s_src, s_land, plan_s, corner(big0_t[1]), "gather_small_wait")[0]
    gs = _sum_parts(gathered, "sum_small")
    loss_total = gs[408, 0]
    g_pre, g_post, g_memg = gs[0:16].reshape(1, D_MODEL), gs[16:32].reshape(1, D_MODEL), gs[32:48].reshape(1, D_MODEL)
    gvec = gs[48:112].reshape(8, D_RNN)
    g_conv_w = lax.dynamic_slice(gvec[0:CONV_W], (0, me * RNN_BLOCK), (CONV_W, RNN_BLOCK))
    g_conv_b, g_b_a, g_b_x, g_lam = gvec[4:5], gvec[5:6], gvec[6:7], gvec[7:8]
    g_sinks = gs[112:113, :SWA_HEADS]
    g_rel = gs[120:152, :SWA_HEADS]
    g_w_a = gathered[:, 152:280]
    g_w_x = gathered[:, 280:408]

    g_small = (g_pre, g_post, g_memg, g_conv_b, g_b_a, g_b_x, g_lam, g_w_a, g_w_x, g_sinks, g_rel, g_conv_w)
    w_small = (pre_norm_g, post_norm_g, mem_norm_g, conv_b, b_rg_a, b_rg_x, lru_lambda, w_rg_a, w_rg_x, swa_sinks, rel_bias, conv_w)
    m_small = (m_pre_norm_g, m_post_norm_g, m_mem_norm_g, m_conv_b, m_b_rg_a, m_b_rg_x, m_lru_lambda, m_w_rg_a, m_w_rg_x, m_swa_sinks, m_rel_bias, m_conv_w)
    v_small = (v_pre_norm_g, v_post_norm_g, v_mem_norm_g, v_conv_b, v_b_rg_a, v_b_rg_x, v_lru_lambda, v_w_rg_a, v_w_rg_x, v_swa_sinks, v_rel_bias, v_conv_w)
    sm = _adamw_small(g_small, w_small, m_small, v_small)


    def leaves(k):
        s = sm[k]
        return [s[0], s[1], s[2], big[0][k], s[11], s[3], s[7], s[4], s[8], s[5], s[6], s[9], s[10],
                big[1][k], big[2][k], big[3][k], big[4][k], big[5][k]]

    return (loss_total, grad_x[None], *leaves(0), *leaves(1), *leaves(2), *leaves(3))
```

```python
import math

import jax
import jax.numpy as jnp
import numpy as np
from jax import lax
from jax.experimental import pallas as pl
from jax.experimental.pallas import tpu as pltpu

F32, BF = jnp.float32, jnp.bfloat16
MESH = pl.DeviceIdType.MESH
N_DEV = 8

D_MODEL = 2048
D_RNN = 1024
RNN_BLOCKS = 8
RNN_BLOCK = 128
CONV_W = 4
LRU_C = 8.0
SWA_HEADS = 16
SWA_KV_HEADS = 2
SWA_HD = 64
WINDOW = 128
MEM_HEADS = 4
MEM_HD = 256
D_MEM = 1024
REL_BUCKETS = 32
REL_MAX_DIST = 128
EPS = 1e-6
NEG_INF = -1e30
D_IN = 12544
SEGMENTS = (("xr", 0, 1024, F32), ("g_rg", 1024, 1024, F32), ("q_s", 2048, 1024, BF), ("kv", 3072, 256, BF),
            ("g_swa", 3328, 1024, F32), ("q_m", 4352, 1024, BF), ("g_mem", 5376, 1024, F32), ("gl", 6400, 6144, F32))
SEG_TILE = 256
D_IN_TILE = 7 * SEG_TILE

ADAM_LR, ADAM_B1, ADAM_B2, ADAM_EPS, ADAM_WD, ADAM_STEP = 0.001, 0.9, 0.999, 1e-08, 0.01, 10

NN = (((1,), (0,)), ((), ()))
NT = (((1,), (1,)), ((), ()))
TN = (((0,), (0,)), ((), ()))
MIB = 2 ** 20


def _dot(a, b, dn):
    return lax.dot_general(a, b, dn, preferred_element_type=F32)


VMEM_LIMIT_MIB = 48
VMEM_LIMIT_LARGE_MIB = 56


def _params(sem=None, large=False):
    return pltpu.CompilerParams(dimension_semantics=sem, vmem_limit_bytes=(VMEM_LIMIT_LARGE_MIB if large else VMEM_LIMIT_MIB) * MIB)


def _sigmoid(z):
    return 1.0 / (1.0 + jnp.exp(-z))


def _softplus(z):
    return jnp.maximum(z, 0.0) + jnp.log(1.0 + jnp.exp(-jnp.abs(z)))


def _expm1(z):
    p = z * (1.0 + z * (0.5 + z * (1.0 / 6 + z * (1.0 / 24 + z * (1.0 / 120 + z * (1.0 / 720 + z * (1.0 / 5040 + z / 40320)))))))
    return jnp.where(jnp.abs(z) < 0.3, p, jnp.exp(z) - 1.0)


def _flat(p):
    return 4 * p[0] + 2 * p[1] + p[2]


def _all_gather_relayed(arrs, relay, name, side=None):
    n = len(arrs)
    K = 9
    work, side_ins, side_outs, side_scratch = side if side is not None else (None, [], [], [])
    n_in, n_out = n + len(side_ins), n + len(side_outs)

    def body(*refs):
        ins, outs = refs[:n], refs[n_in:n_in + n]
        send_sems, recv_sems, local_sems = refs[n_in + n_out:n_in + n_out + 3]
        x, y, c = lax.axis_index("x"), lax.axis_index("y"), lax.axis_index("c")
        me, sib = (x, y, c), (x, y, 1 - c)
        xn, yn, dg = (1 - x, y, c), (x, 1 - y, c), (1 - x, 1 - y, c)

        def other(p):
            return (p[0], p[1], 1 - p[2])

        def rows(a, half):
            h = arrs[a].shape[0] // 2
            return pl.ds(half * h, h)

        def copy(a, k, block, to, half=None, src=None):
            dst = outs[a].at[_flat(block)]
            if half is not None:
                dst = dst.at[rows(a, half)]
            return pltpu.make_async_remote_copy(src_ref=dst if src is None else src, dst_ref=dst,
                                                send_sem=send_sems.at[a * K + k], recv_sem=recv_sems.at[a * K + k],
                                                device_id=to, device_id_type=MESH)

        mine = [pltpu.make_async_copy(ins[a], outs[a].at[_flat(me)], local_sems.at[a]) for a in range(n)]
        for cp in mine:
            cp.start()
        sends = []

        def start(cp):
            cp.start()
            sends.append(cp)

        for a in range(n):
            start(copy(a, 1, me, xn, src=ins[a]))
            start(copy(a, 2, me, yn, src=ins[a]))
            if not relay[a]:
                start(copy(a, 3, me, dg, src=ins[a]))
            start(copy(a, 0, me, sib, src=ins[a]))
        if work is not None:
            work(refs[n:n_in], refs[n_in + n:n_in + n_out], refs[n_in + n_out + 3:])
        for a in range(n):
            copy(a, 1, xn, me).wait_recv()
            if relay[a]:
                start(copy(a, 3, xn, yn, half=0))
            start(copy(a, 5, xn, sib))
        for a in range(n):
            copy(a, 2, yn, me).wait_recv()
            if relay[a]:
                start(copy(a, 4, yn, xn, half=1))
            start(copy(a, 6, yn, sib))
        for a in range(n):
            if relay[a]:
                copy(a, 3, dg, me, half=0).wait_recv()
                start(copy(a, 7, dg, sib, half=0))
                copy(a, 4, dg, me, half=1).wait_recv()
                start(copy(a, 8, dg, sib, half=1))
            else:
                copy(a, 3, dg, me).wait_recv()
                start(copy(a, 7, dg, sib))
        for a in range(n):
            copy(a, 0, sib, me).wait_recv()
            copy(a, 5, other(xn), me).wait_recv()
            copy(a, 6, other(yn), me).wait_recv()
            if relay[a]:
                copy(a, 7, other(dg), me, half=0).wait_recv()
                copy(a, 8, other(dg), me, half=1).wait_recv()
            else:
                copy(a, 7, other(dg), me).wait_recv()
        for cp in sends:
            cp.wait_send()
        for cp in mine:
            cp.wait()

    any_spec = pl.BlockSpec(memory_space=pl.ANY)
    return pl.pallas_call(
        body, name=name,
        out_shape=[jax.ShapeDtypeStruct((N_DEV,) + a.shape, a.dtype) for a in arrs] + list(side_outs),
        in_specs=[any_spec] * n_in, out_specs=[any_spec] * n_out,
        scratch_shapes=[pltpu.SemaphoreType.DMA((K * n,)), pltpu.SemaphoreType.DMA((K * n,)), pltpu.SemaphoreType.DMA((n,))]
        + list(side_scratch),
        compiler_params=_params(),
    )(*arrs, *side_ins)


def _chip_peers(x, y):
    return [(1 - x, y), (x, 1 - y), (1 - x, 1 - y)]


def _chip(p):
    return 2 * p[0] + p[1]


def _plan_gather(n):
    def plan(x, y, c):
        out = []
        for a in range(n):
            for peer in [(x, y, 1 - c)] + [(*ch, c) for ch in _chip_peers(x, y)]:
                out.append((a, None, ("lead", _flat((x, y, c))), peer, ("lead", _flat(peer))))
        return out
    return plan


def _plan_everyone(n):
    def plan(x, y, c):
        out = []
        for a in range(n):
            for r in range(1, N_DEV):
                peer = (1 - x if r & 4 else x, 1 - y if r & 2 else y, 1 - c if r & 1 else c)
                out.append((a, None, ("lead", _flat((x, y, c))), peer, ("lead", _flat(peer))))
        return out
    return plan


def _plan_swap(ndims):
    def plan(x, y, c):
        out = []
        for a, nd in enumerate(ndims):
            if nd == 4:
                out.append((a, 1 - c, ("all", 0), (x, y, 1 - c), ("all", 0)))
            else:
                out += [(a, 2 * j + 1 - c, ("lead", j), (x, y, 1 - c), ("lead", j)) for j in range(4)]
        return out
    return plan


def _slot(ref, where):
    kind, k = where
    return ref if kind == "all" else ref.at[k]


def _plan_scatter(n):
    def plan(x, y, c):
        out = []
        for a in range(n):
            for ch in _chip_peers(x, y):
                out.append((a, _chip(ch), ("lead", _chip((x, y))), (*ch, c), ("lead", _chip(ch))))
        return out
    return plan


HBM_SPEC = pl.BlockSpec(memory_space=pltpu.HBM)
SEM_SPEC = pl.BlockSpec(memory_space=pltpu.SEMAPHORE)


def _in_hbm(a):
    return pltpu.with_memory_space_constraint(a, pltpu.HBM)


def _exchange_start(srcs, lands, plan, name):
    n = len(lands)
    ns = 0 if srcs is None else n
    count = len(plan(0, 0, 0))

    def body(*refs):
        land_refs = refs[ns:ns + n]
        src_refs = land_refs if srcs is None else refs[:n]
        send_sems, recv_sems = refs[ns + n], refs[ns + n + 1]
        token = refs[-1]
        x, y, c = lax.axis_index("x"), lax.axis_index("y"), lax.axis_index("c")
        for k, (a, si, di, peer, _) in enumerate(plan(x, y, c)):
            src = src_refs[a] if si is None else src_refs[a].at[si]
            pltpu.make_async_remote_copy(src_ref=src, dst_ref=_slot(land_refs[a], di), send_sem=send_sems.at[k],
                                         recv_sem=recv_sems.at[k], device_id=peer, device_id_type=MESH).start()
        token[...] = jnp.zeros_like(token)

    out = pl.pallas_call(
        body, name=name,
        out_shape=(pltpu.SemaphoreType.DMA((count,)), pltpu.SemaphoreType.DMA((count,)),
                   *[pltpu.HBM(a.shape, a.dtype) for a in lands], jax.ShapeDtypeStruct((8, 128), F32)),
        in_specs=[HBM_SPEC] * (ns + n),
        out_specs=(SEM_SPEC, SEM_SPEC, *([HBM_SPEC] * n), pl.BlockSpec(memory_space=pltpu.VMEM)),
        input_output_aliases={ns + i: 2 + i for i in range(n)},
        compiler_params=pltpu.CompilerParams(has_side_effects=pltpu.SideEffectType.DATAFLOW_SIDE_EFFECTING),
    )(*[_in_hbm(a) for a in (srcs or [])], *[_in_hbm(a) for a in lands])
    return out[0], out[1], srcs if srcs is None else list(srcs), list(out[2:2 + n]), out[-1]


def _exchange_wait(send_sems, recv_sems, srcs, lands, plan, after, name):
    n = len(lands)
    ns = 0 if srcs is None else n

    def body(*refs):
        land_refs = refs[ns:ns + n]
        src_refs = land_refs if srcs is None else refs[:n]
        send_sems, recv_sems = refs[ns + n], refs[ns + n + 1]
        x, y, c = lax.axis_index("x"), lax.axis_index("y"), lax.axis_index("c")
        for k, (a, si, _, peer, ri) in enumerate(plan(x, y, c)):
            src = src_refs[a] if si is None else src_refs[a].at[si]
            cp = pltpu.make_async_remote_copy(src_ref=src, dst_ref=_slot(land_refs[a], ri), send_sem=send_sems.at[k],
                                              recv_sem=recv_sems.at[k], device_id=peer, device_id_type=MESH)
            cp.wait_send()
            cp.wait_recv()

    out = pl.pallas_call(
        body, name=name,
        out_shape=tuple(pltpu.HBM(a.shape, a.dtype) for a in lands),
        in_specs=[HBM_SPEC] * (ns + n) + [SEM_SPEC, SEM_SPEC, pl.BlockSpec(memory_space=pl.ANY)],
        out_specs=tuple([HBM_SPEC] * n),
        input_output_aliases={ns + i: i for i in range(n)},
        compiler_params=pltpu.CompilerParams(has_side_effects=pltpu.SideEffectType.DATAFLOW_SIDE_EFFECTING),
    )(*[_in_hbm(a) for a in (srcs or [])], *lands, send_sems, recv_sems, after)
    return list(out)


def _plan_forward(n):
    def plan(x, y, c):
        return [(a, _flat((*ch, c)), ("lead", _flat((*ch, c))), (x, y, 1 - c), ("lead", _flat((*ch, 1 - c))))
                for a in range(n) for ch in _chip_peers(x, y)]
    return plan


def _place_own(zones, owns, slot, name):
    n = len(zones)

    def body(slot_ref, *refs):
        for a in range(n):
            refs[2 * n + a][...] = refs[a][...]

    return pl.pallas_call(
        body, name=name,
        grid_spec=pltpu.PrefetchScalarGridSpec(
            num_scalar_prefetch=1, grid=(1,),
            in_specs=[pl.BlockSpec(o.shape, lambda i, s_ref: (0, 0)) for o in owns] + [pl.BlockSpec(memory_space=pl.ANY)] * n,
            out_specs=[pl.BlockSpec((None,) + o.shape, lambda i, s_ref: (s_ref[0], 0, 0)) for o in owns]),
        out_shape=[jax.ShapeDtypeStruct(z.shape, z.dtype) for z in zones],
        input_output_aliases={1 + n + a: a for a in range(n)},
        compiler_params=_params(("arbitrary",)),
    )(slot, *owns, *zones)


def _pair_sum(parts, got, core, name):
    R, C = parts.shape[-2:]
    mine = (pl.BlockSpec((None, None, R, C), lambda j, c_ref: (c_ref[0], j, 0, 0)) if parts.ndim == 4
            else pl.BlockSpec((None, R, C), lambda j, c_ref: (2 * j + c_ref[0], 0, 0)))

    def body(c_ref, p_ref, g_ref, o_ref):
        o_ref[...] = (p_ref[...].astype(F32) + g_ref[...].astype(F32)).astype(o_ref.dtype)

    return pl.pallas_call(
        body, name=name,
        grid_spec=pltpu.PrefetchScalarGridSpec(
            num_scalar_prefetch=1, grid=(4,),
            in_specs=[mine, pl.BlockSpec((None, R, C), lambda j, c_ref: (j, 0, 0))],
            out_specs=pl.BlockSpec((None, R, C), lambda j, c_ref: (j, 0, 0))),
        out_shape=jax.ShapeDtypeStruct((4, R, C), parts.dtype),
        compiler_params=_params(("parallel",)),
    )(core, parts, got)


def _matmul(a, b, mode, M, N, K, tm, tn, tk, out_dtype, name, b_noff=0, out_blocked=None, dep=None):
    nm, nn, nk = M // tm, N // tn, K // tk
    if mode == "nn":
        a_spec = pl.BlockSpec((tm, tk), lambda j, i, k: (i, k))
        b_spec = pl.BlockSpec((tk, tn), lambda j, i, k: (k, j + b_noff))
        dn = NN
    elif mode == "nt":
        a_spec = pl.BlockSpec((tm, tk), lambda j, i, k: (i, k))
        b_spec = pl.BlockSpec((tn, tk), lambda j, i, k: (j + b_noff, k))
        dn = NT
    else:
        a_spec = pl.BlockSpec((tk, tm), lambda j, i, k: (k, i))
        b_spec = pl.BlockSpec((tk, tn), lambda j, i, k: (k, j + b_noff))
        dn = TN
    if out_blocked == "col":
        out_shape = jax.ShapeDtypeStruct((2, 4, M, tn), out_dtype)
        out_spec = pl.BlockSpec((None, None, tm, tn), lambda j, i, k: (j % 2, j // 2, i, 0))
    elif out_blocked == "row":
        out_shape = jax.ShapeDtypeStruct((2, 4, tm, N), out_dtype)
        out_spec = pl.BlockSpec((None, None, tm, tn), lambda j, i, k: (i % 2, i // 2, 0, j))
    else:
        out_shape = jax.ShapeDtypeStruct((M, N), out_dtype)
        out_spec = pl.BlockSpec((tm, tn), lambda j, i, k: (i, j))

    n_extra = int(dep is not None)

    def body(a_ref, b_ref, *rest):
        o_ref, scratch = rest[n_extra], rest[n_extra + 1:]
        if nk == 1:
            o_ref[...] = _dot(a_ref[...], b_ref[...], dn).astype(out_dtype)
        else:
            acc_ref, = scratch
            k = pl.program_id(2)

            @pl.when(k == 0)
            def _():
                acc_ref[...] = jnp.zeros_like(acc_ref)

            acc_ref[...] += _dot(a_ref[...], b_ref[...], dn)

            @pl.when(k == nk - 1)
            def _():
                o_ref[...] = acc_ref[...].astype(out_dtype)

    return pl.pallas_call(
        body, name=name, grid=(nn, nm, nk),
        in_specs=[a_spec, b_spec] + ([] if dep is None else [pl.BlockSpec((8, 128), lambda j, i, k: (0, 0))]),
        out_specs=out_spec, out_shape=out_shape,
        scratch_shapes=[] if nk == 1 else [pltpu.VMEM((tm, tn), F32)],
        compiler_params=_params(("parallel", "parallel", "arbitrary")),
    )(a, b, *([] if dep is None else [dep]))


RMS_SIDE_ROWS = 512


def _rms_side(pairs):
    chunks = [min(x.shape[0], RMS_SIDE_ROWS) for x, _ in pairs]

    def work(ins, outs, scratch):
        sem = scratch[-1]

        def move(src, dst):
            cp = pltpu.make_async_copy(src, dst, sem.at[0])
            cp.start()
            cp.wait()

        for p, ((x, _), tr) in enumerate(zip(pairs, chunks)):
            x_ref, g_ref, h_ref = ins[2 * p], ins[2 * p + 1], outs[p]
            xv, gv, hv = scratch[3 * p:3 * p + 3]
            move(g_ref, gv)
            for i in range(x.shape[0] // tr):
                rows = pl.ds(i * tr, tr)
                move(x_ref.at[rows], xv)
                v = xv[...]
                hv[...] = (v * lax.rsqrt(jnp.mean(v * v, axis=-1, keepdims=True) + EPS) * gv[...]).astype(BF)
                move(hv, h_ref.at[rows])

    scratch = [s for (x, g), tr in zip(pairs, chunks)
               for s in (pltpu.VMEM((tr, x.shape[1]), F32), pltpu.VMEM(g.shape, F32), pltpu.VMEM((tr, x.shape[1]), BF))]
    return (work, [a for pair in pairs for a in pair], [jax.ShapeDtypeStruct(x.shape, BF) for x, _ in pairs],
            scratch + [pltpu.SemaphoreType.DMA((1,))])


def _rms_gain_grad(dn, x, name):
    R, Dm = x.shape

    def body(dn_ref, x_ref, o_ref):
        xv = x_ref[...]
        r = lax.rsqrt(jnp.mean(xv * xv, axis=-1, keepdims=True) + EPS)
        o_ref[...] = jnp.sum(dn_ref[...] * xv * r, axis=0, keepdims=True)

    return pl.pallas_call(
        body, name=name, out_shape=jax.ShapeDtypeStruct((1, Dm), F32),
        compiler_params=_params(),
    )(dn, x)


def _shift_down(v, k, head8, row, T):
    if k == 0:
        return v
    r = pltpu.roll(v, k, 0)
    hr = pltpu.roll(head8, k, 0)
    top = jnp.where(row[:8] < k, hr, r[:8])
    return jnp.concatenate([top, r[8:]], axis=0)


def _shift_up(v, k, tail8, row, T):
    if k == 0:
        return v
    r = pltpu.roll(v, T - k, 0)
    tr = pltpu.roll(tail8, 8 - k, 0)
    bot = jnp.where(row[:8] >= 8 - k, tr, r[T - 8:])
    return jnp.concatenate([r[:T - 8], bot], axis=0)


def _rglru_gates(u, head8, grow, row, T, cw_ref, cb_ref, wa_ref, ba_ref, wx_ref, bx_ref, lam_ref):
    us = [_shift_down(u, k, head8, row, T) for k in range(CONV_W)]
    acc = us[0] * cw_ref[0:1, :]
    for k in range(1, CONV_W):
        acc = acc + us[k] * cw_ref[k:k + 1, :]
    conv = cb_ref[...] + acc
    cbf = conv.astype(BF)
    r_ = _sigmoid(_dot(cbf, wa_ref[0], NN) + ba_ref[...])
    i_ = _sigmoid(_dot(cbf, wx_ref[0], NN) + bx_ref[...])
    sp = _softplus(-lam_ref[...])
    la = -LRU_C * r_ * sp
    a = jnp.exp(la)
    mult_raw = jnp.sqrt(-_expm1(2.0 * la))
    mult = jnp.where(grow == 0, 1.0, mult_raw)
    return us, conv, cbf, r_, i_, sp, a, mult_raw, mult


def _rglru_specs(T, nt, rev):
    tmap = (lambda n, t: (nt - 1 - t, n)) if rev else (lambda n, t: (t, n))
    hmap = ((lambda n, t: (jnp.maximum((nt - 1 - t) * (T // 8) - 1, 0), n)) if rev
            else (lambda n, t: (jnp.maximum(t * (T // 8) - 1, 0), n)))
    tile = pl.BlockSpec((T, RNN_BLOCK), tmap)
    halo = pl.BlockSpec((8, RNN_BLOCK), hmap)
    vec = pl.BlockSpec((1, RNN_BLOCK), lambda n, t: (0, n))
    cw = pl.BlockSpec((CONV_W, RNN_BLOCK), lambda n, t: (0, n))
    wblk = pl.BlockSpec((1, RNN_BLOCK, RNN_BLOCK), lambda n, t: (n, 0, 0))
    return tile, halo, vec, cw, wblk


def _rglru_fwd(xr, g, cw, cb, wa, ba, wx, bx, lam, T):
    S = xr.shape[0]
    nt = S // T

    def body(u_ref, uh_ref, g_ref, cw_ref, cb_ref, wa_ref, ba_ref, wx_ref, bx_ref, lam_ref, h_ref, y_ref, carry):
        t = pl.program_id(1)

        @pl.when(t == 0)
        def _():
            carry[...] = jnp.zeros_like(carry)

        row = lax.broadcasted_iota(jnp.int32, (T, RNN_BLOCK), 0)
        grow = row + t * T
        head8 = jnp.where(t > 0, uh_ref[...], 0.0)
        _, conv, _, _, i_, _, a, _, mult = _rglru_gates(u_ref[...], head8, grow, row, T, cw_ref, cb_ref, wa_ref, ba_ref,
                                                         wx_ref, bx_ref, lam_ref)
        b = mult * i_ * conv
        s = 1
        while s < T:
            keep = row >= s
            a_s = jnp.where(keep, pltpu.roll(a, s, 0), 1.0)
            b_s = jnp.where(keep, pltpu.roll(b, s, 0), 0.0)
            b = a * b_s + b
            a = a * a_s
            s *= 2
        h = b + a * carry[0:1, :]
        carry[...] = jnp.broadcast_to(h[T - 1:T, :], carry.shape)
        h_ref[...] = h
        gv = g_ref[...]
        y_ref[...] = (h * (gv * _sigmoid(gv))).astype(BF)

    tile, halo, vec, cwspec, wblk = _rglru_specs(T, nt, False)
    return pl.pallas_call(
        body, name="rglru_fwd", grid=(RNN_BLOCKS, nt),
        in_specs=[tile, halo, tile, cwspec, vec, wblk, vec, wblk, vec, vec],
        out_specs=[tile, tile],
        out_shape=[jax.ShapeDtypeStruct((S, D_RNN), F32), jax.ShapeDtypeStruct((S, D_RNN), BF)],
        scratch_shapes=[pltpu.VMEM((8, RNN_BLOCK), F32)],
        compiler_params=_params(("parallel", "arbitrary")),
    )(xr, xr, g, cw, cb, wa, ba, wx, bx, lam)


def _rglru_bwd(xr, g, h, dy, cw, cb, wa, ba, wx, bx, lam, T):
    S = xr.shape[0]
    nt = S // T

    def body(u_ref, uh_ref, g_ref, h_ref, hh_ref, dy_ref, cw_ref, cb_ref, wa_ref, ba_ref, wx_ref, bx_ref, lam_ref,
             du_ref, dg_ref, dwa_ref, dwx_ref, dvec_ref, c_dhh, c_a, c_dconv):
        t = pl.program_id(1)
        tt = nt - 1 - t

        @pl.when(t == 0)
        def _():
            c_dhh[...] = jnp.zeros_like(c_dhh)
            c_a[...] = jnp.zeros_like(c_a)
            c_dconv[...] = jnp.zeros_like(c_dconv)
            dwa_ref[...] = jnp.zeros_like(dwa_ref)
            dwx_ref[...] = jnp.zeros_like(dwx_ref)
            dvec_ref[...] = jnp.zeros_like(dvec_ref)

        row = lax.broadcasted_iota(jnp.int32, (T, RNN_BLOCK), 0)
        row8 = row[:8]
        grow = row + tt * T
        head8 = jnp.where(tt > 0, uh_ref[...], 0.0)
        us, conv, cbf, r_, i_, sp, a, mult_raw, mult = _rglru_gates(
            u_ref[...], head8, grow, row, T, cw_ref, cb_ref, wa_ref, ba_ref, wx_ref, bx_ref, lam_ref)
        hv = h_ref[...]
        hprev = _shift_down(hv, 1, jnp.where(tt > 0, hh_ref[...], 0.0), row, T)
        gv = g_ref[...]
        sg = _sigmoid(gv)
        dyv = dy_ref[...]
        dg_ref[...] = (dyv * hv * (sg * (1.0 + gv * (1.0 - sg)))).astype(BF)
        d = dyv * (gv * sg)
        A = _shift_up(a, 1, c_a[...], row, T)
        s = 1
        while s < T:
            keep = row < T - s
            A_s = jnp.where(keep, pltpu.roll(A, T - s, 0), 1.0)
            d_s = jnp.where(keep, pltpu.roll(d, T - s, 0), 0.0)
            d = A * d_s + d
            A = A * A_s
            s *= 2
        dhh = d + A * c_dhh[0:1, :]
        da = dhh * hprev
        dconv = dhh * mult * i_
        di = dhh * mult * conv
        dmult = dhh * i_ * conv
        dla = da * a - jnp.where(grow == 0, 0.0, dmult * (a * a) / mult_raw)
        dr = dla * (-LRU_C * sp)
        dsp = jnp.sum(dla * (-LRU_C * r_), axis=0, keepdims=True)
        dza = dr * r_ * (1.0 - r_)
        dzx = di * i_ * (1.0 - i_)
        dza_b, dzx_b = dza.astype(BF), dzx.astype(BF)
        dconv = dconv + _dot(dza_b, wa_ref[0], NT) + _dot(dzx_b, wx_ref[0], NT)
        dwa_ref[0] += _dot(cbf, dza_b, TN)
        dwx_ref[0] += _dot(cbf, dzx_b, TN)
        lam = lam_ref[...]
        rows = [jnp.sum(dconv * us[k], axis=0, keepdims=True) for k in range(CONV_W)]
        rows += [jnp.sum(dconv, axis=0, keepdims=True), jnp.sum(dza, axis=0, keepdims=True),
                 jnp.sum(dzx, axis=0, keepdims=True), dsp * (-_sigmoid(-lam))]
        upd = jnp.zeros((8, RNN_BLOCK), F32)
        for j, rv in enumerate(rows):
            upd = upd + jnp.where(row8 == j, rv, 0.0)
        dvec_ref[...] += upd
        tail8 = c_dconv[...]
        du = dconv * cw_ref[0:1, :]
        for k in range(1, CONV_W):
            du = du + _shift_up(dconv, k, tail8, row, T) * cw_ref[k:k + 1, :]
        du_ref[...] = du.astype(BF)
        c_dhh[...] = jnp.broadcast_to(dhh[0:1, :], c_dhh.shape)
        c_a[...] = jnp.broadcast_to(a[0:1, :], c_a.shape)
        c_dconv[...] = dconv[:8]

    tile, halo, vec, cwspec, wblk = _rglru_specs(T, nt, True)
    acc8 = pl.BlockSpec((8, RNN_BLOCK), lambda n, t: (0, n))
    return pl.pallas_call(
        body, name="rglru_bwd", grid=(RNN_BLOCKS, nt),
        in_specs=[tile, halo, tile, tile, halo, tile, cwspec, vec, wblk, vec, wblk, vec, vec],
        out_specs=[tile, tile, wblk, wblk, acc8],
        out_shape=[jax.ShapeDtypeStruct((S, D_RNN), BF), jax.ShapeDtypeStruct((S, D_RNN), BF),
                   jax.ShapeDtypeStruct((RNN_BLOCKS, RNN_BLOCK, RNN_BLOCK), F32),
                   jax.ShapeDtypeStruct((RNN_BLOCKS, RNN_BLOCK, RNN_BLOCK), F32),
                   jax.ShapeDtypeStruct((8, D_RNN), F32)],
        scratch_shapes=[pltpu.VMEM((8, RNN_BLOCK), F32)] * 3,
        compiler_params=_params(("parallel", "arbitrary")),
    )(xr, xr, g, h, h, dy, cw, cb, wa, ba, wx, bx, lam)


def _rel_bucket_map():
    qi = np.arange(WINDOW)[:, None]
    kj = np.arange(2 * WINDOW)[None, :]
    dist = jnp.asarray(qi + WINDOW - kj, jnp.int32)
    n = jnp.maximum(dist, 0)
    max_exact = REL_BUCKETS // 2
    ratio = jnp.log(jnp.maximum(n, 1).astype(F32) / max_exact) / math.log(REL_MAX_DIST / max_exact)
    large = jnp.minimum(max_exact + (ratio * (REL_BUCKETS - max_exact)).astype(jnp.int32), REL_BUCKETS - 1)
    bucket = jnp.where(n < max_exact, n, large).astype(jnp.int32)
    j = np.arange(WINDOW)[None, :]
    return jnp.where(jnp.asarray(j > qi), bucket[:, :WINDOW], bucket[:, WINDOW:])


def _swa_common(n, kv_ref, bucket_ref, relb_ref, bias_scr):
    @pl.when(n == 0)
    def _():
        bk = bucket_ref[...]
        for h in range(SWA_HEADS):
            acc = jnp.zeros((WINDOW, WINDOW), F32)
            for b in range(REL_BUCKETS):
                acc = acc + jnp.where(bk == b, relb_ref[b, h], 0.0)
            bias_scr[h] = acc

    prev0 = pl.multiple_of(jnp.maximum(n - 1, 0) * WINDOW, WINDOW)
    cur0 = pl.multiple_of(n * WINDOW, WINDOW)
    kk = jnp.concatenate([kv_ref[pl.ds(prev0, WINDOW), :], kv_ref[pl.ds(cur0, WINDOW), :]], axis=0).astype(F32)
    rowi = lax.broadcasted_iota(jnp.int32, (WINDOW, WINDOW), 0)
    col = lax.broadcasted_iota(jnp.int32, (WINDOW, WINDOW), 1)
    from_prev = col > rowi
    return kk, from_prev, prev0, cur0


def _fold(full, from_prev):
    return jnp.where(from_prev, full[:, :WINDOW], full[:, WINDOW:])


def _unfold(sq, from_prev):
    return jnp.concatenate([jnp.where(from_prev, sq, 0.0), jnp.where(from_prev, 0.0, sq)], axis=1)


def _half_pair(part, kvh):
    lo = lax.broadcasted_iota(jnp.int32, part.shape, 1) < SWA_HD
    if kvh == 0:
        pa = jnp.where(lo, part, 0.0)
        pb = pltpu.roll(pa, SWA_HD, 1)
    else:
        pb = jnp.where(lo, 0.0, part)
        pa = pltpu.roll(pb, SWA_HD, 1)
    return pa.astype(BF), pb.astype(BF)


ALL_HEADS = SWA_HEADS * WINDOW


def _sink_column(sinks):
    return jnp.repeat(sinks.reshape(SWA_HEADS), WINDOW).reshape(ALL_HEADS, 1)


def _swa_operands(kk):
    return [(_half_pair(kk[:, :128], kvh), _half_pair(kk[:, 128:], kvh)) for kvh in range(SWA_KV_HEADS)]


def _swa_probs(n, q_ref, ops, bias_scr, sinkc_ref, from_prev):
    lgs = []
    for kvh in range(SWA_KV_HEADS):
        (ka, kb), _ = ops[kvh]
        for p in range(4):
            q2 = q_ref[:, kvh * 512 + p * 128:kvh * 512 + p * 128 + 128]
            lgs += [_fold(_dot(q2, ka, NT), from_prev), _fold(_dot(q2, kb, NT), from_prev)]
    lg = jnp.concatenate(lgs, axis=0) * (SWA_HD ** -0.5) + bias_scr[...].reshape(ALL_HEADS, WINDOW)
    rowi = jnp.bitwise_and(lax.broadcasted_iota(jnp.int32, (ALL_HEADS, WINDOW), 0), WINDOW - 1)
    col = lax.broadcasted_iota(jnp.int32, (ALL_HEADS, WINDOW), 1)
    no_prev = jnp.where(n > 0, 0, 4 * WINDOW)
    lg = jnp.where(jnp.logical_or(col <= rowi, col > rowi + no_prev), lg, NEG_INF)
    sink = sinkc_ref[...]
    m = jnp.maximum(jnp.max(lg, axis=-1, keepdims=True), sink)
    e = jnp.exp(lg - m)
    es = jnp.exp(sink - m)
    den = jnp.sum(e, axis=-1, keepdims=True) + es
    return e / den, es / den


def _swa_fwd(q, kv, g, bucket, rel_bias, sink_col):
    S = q.shape[0]
    nb = S // WINDOW

    def body(q_ref, kv_ref, g_ref, bucket_ref, relb_ref, sinkc_ref, o_ref, y_ref, bias_scr):
        n = pl.program_id(0)
        kk, from_prev, _, _ = _swa_common(n, kv_ref, bucket_ref, relb_ref, bias_scr)
        ops = _swa_operands(kk)
        pr, _ = _swa_probs(n, q_ref, ops, bias_scr, sinkc_ref, from_prev)
        for kvh in range(SWA_KV_HEADS):
            _, (va, vb) = ops[kvh]
            for p in range(4):
                c0 = kvh * 512 + p * 128
                r0 = (kvh * 8 + 2 * p) * WINDOW
                o2 = (_dot(_unfold(pr[r0:r0 + WINDOW], from_prev).astype(BF), va, NN)
                      + _dot(_unfold(pr[r0 + WINDOW:r0 + 2 * WINDOW], from_prev).astype(BF), vb, NN))
                o_ref[:, c0:c0 + 128] = o2
                gv = g_ref[:, c0:c0 + 128]
                y_ref[:, c0:c0 + 128] = (o2 * (gv * _sigmoid(gv))).astype(BF)

    blk = pl.BlockSpec((WINDOW, 1024), lambda n: (n, 0))
    smem = pl.BlockSpec(memory_space=pltpu.SMEM)
    sinkc = pl.BlockSpec((ALL_HEADS, 1), lambda n: (0, 0))
    return pl.pallas_call(
        body, name="swa_fwd", grid=(nb,),
        in_specs=[blk, pl.BlockSpec((S, 256), lambda n: (0, 0)), blk, pl.BlockSpec((WINDOW, WINDOW), lambda n: (0, 0)), smem, sinkc],
        out_specs=[blk, blk],
        out_shape=[jax.ShapeDtypeStruct((S, 1024), F32), jax.ShapeDtypeStruct((S, 1024), BF)],
        scratch_shapes=[pltpu.VMEM((SWA_HEADS, WINDOW, WINDOW), F32)],
        compiler_params=_params(("arbitrary",)),
    )(q, kv, g, bucket, rel_bias, sink_col)


def _swa_bwd(q, kv, g, o, dy, bucket, rel_bias, sink_col, others):
    S = q.shape[0]
    nb = S // WINDOW
    first_col = {name: c0 for name, c0, _, _ in SEGMENTS}
    col_q, col_g = first_col["q_s"], first_col["g_swa"]
    copies = []
    for name, c0, width, _ in SEGMENTS:
        if name not in ("q_s", "kv", "g_swa"):
            arrs = others[name] if name == "gl" else (others[name],)
            copies += [(a, c0 + i * (width // len(arrs))) for i, a in enumerate(arrs)]

    def body(q_ref, kv_ref, g_ref, o_ref, dy_ref, bucket_ref, relb_ref, sinkc_ref, *refs):
        copy_refs = refs[:len(copies)]
        dp_ref, dkv_ref, dsink_ref, drel_ref, bias_scr, dbias_scr, dsink_scr = refs[len(copies):]
        n = pl.program_id(0)
        for c_ref, (a, c0) in zip(copy_refs, copies):
            dp_ref[:, c0:c0 + a.shape[1]] = c_ref[...]

        @pl.when(n == 0)
        def _():
            dbias_scr[...] = jnp.zeros_like(dbias_scr)
            dsink_scr[...] = jnp.zeros_like(dsink_scr)
            dkv_ref[...] = jnp.zeros_like(dkv_ref)

        kk, from_prev, prev0, cur0 = _swa_common(n, kv_ref, bucket_ref, relb_ref, bias_scr)
        ops = _swa_operands(kk)
        pr, ps = _swa_probs(n, q_ref, ops, bias_scr, sinkc_ref, from_prev)
        do2s, dps = [], []
        for kvh in range(SWA_KV_HEADS):
            _, (va, vb) = ops[kvh]
            for p in range(4):
                c0 = kvh * 512 + p * 128
                gv = g_ref[:, c0:c0 + 128]
                sg = _sigmoid(gv)
                dyv = dy_ref[:, c0:c0 + 128]
                dp_ref[:, col_g + c0:col_g + c0 + 128] = (dyv * o_ref[:, c0:c0 + 128] * (sg * (1.0 + gv * (1.0 - sg)))).astype(BF)
                do2 = (dyv * (gv * sg)).astype(BF)
                do2s.append(do2)
                dps += [_fold(_dot(do2, va, NT), from_prev), _fold(_dot(do2, vb, NT), from_prev)]
        dp = jnp.concatenate(dps, axis=0)
        delta = jnp.sum(pr * dp, axis=-1, keepdims=True)
        ds = pr * (dp - delta)
        dbias_scr[...] += ds.reshape(SWA_HEADS, WINDOW, WINDOW)
        dsink_scr[...] += ps * delta
        dsc = ds * (SWA_HD ** -0.5)
        lo256 = lax.broadcasted_iota(jnp.int32, (2 * WINDOW, 128), 1) < SWA_HD
        dks, dvs = [], []
        for kvh in range(SWA_KV_HEADS):
            (ka, kb), _ = ops[kvh]
            dka = jnp.zeros((2 * WINDOW, 128), F32)
            dkb, dva, dvb = dka, dka, dka
            for p in range(4):
                c0 = kvh * 512 + p * 128
                r0 = (kvh * 8 + 2 * p) * WINDOW
                q2 = q_ref[:, c0:c0 + 128]
                do2 = do2s[kvh * 4 + p]
                ds0 = _unfold(dsc[r0:r0 + WINDOW], from_prev).astype(BF)
                ds1 = _unfold(dsc[r0 + WINDOW:r0 + 2 * WINDOW], from_prev).astype(BF)
                dp_ref[:, col_q + c0:col_q + c0 + 128] = (_dot(ds0, ka, NN) + _dot(ds1, kb, NN)).astype(BF)
                dka = dka + _dot(ds0, q2, TN)
                dkb = dkb + _dot(ds1, q2, TN)
                dva = dva + _dot(_unfold(pr[r0:r0 + WINDOW], from_prev).astype(BF), do2, TN)
                dvb = dvb + _dot(_unfold(pr[r0 + WINDOW:r0 + 2 * WINDOW], from_prev).astype(BF), do2, TN)
            dks.append(jnp.where(lo256, dka, 0.0) + pltpu.roll(jnp.where(lo256, 0.0, dkb), SWA_HD, 1))
            dvs.append(jnp.where(lo256, dva, 0.0) + pltpu.roll(jnp.where(lo256, 0.0, dvb), SWA_HD, 1))
        dk = dks[0] + pltpu.roll(dks[1], SWA_HD, 1)
        dv = dvs[0] + pltpu.roll(dvs[1], SWA_HD, 1)
        dkv_ref[pl.ds(prev0, WINDOW), 0:128] += dk[:WINDOW]
        dkv_ref[pl.ds(prev0, WINDOW), 128:256] += dv[:WINDOW]
        dkv_ref[pl.ds(cur0, WINDOW), 0:128] += dk[WINDOW:]
        dkv_ref[pl.ds(cur0, WINDOW), 128:256] += dv[WINDOW:]

        @pl.when(n == nb - 1)
        def _():
            dsink_ref[...] = -jnp.sum(dsink_scr[...].reshape(SWA_HEADS, WINDOW, 1), axis=1)
            bk = bucket_ref[...]
            sums = []
            for b in range(REL_BUCKETS):
                sums.append(jnp.sum(jnp.where((bk == b)[None], dbias_scr[...], 0.0), axis=1))
            drel_ref[...] = jnp.sum(jnp.concatenate(sums, axis=0), axis=1, keepdims=True)

    blk = pl.BlockSpec((WINDOW, 1024), lambda n: (n, 0))
    smem = pl.BlockSpec(memory_space=pltpu.SMEM)
    whole = lambda shape: pl.BlockSpec(shape, lambda n: (0, 0))
    return pl.pallas_call(
        body, name="swa_bwd", grid=(nb,),
        in_specs=[blk, whole((S, 256)), blk, blk, blk, whole((WINDOW, WINDOW)), smem, whole((ALL_HEADS, 1))]
        + [pl.BlockSpec((WINDOW, a.shape[1]), lambda n: (n, 0)) for a, _ in copies],
        out_specs=[pl.BlockSpec((WINDOW, D_IN), lambda n: (n, 0)), whole((S, 256)), whole((SWA_HEADS, 1)),
                   whole((REL_BUCKETS * SWA_HEADS, 1))],
        out_shape=[jax.ShapeDtypeStruct((S, D_IN), BF), jax.ShapeDtypeStruct((S, 256), F32), jax.ShapeDtypeStruct((SWA_HEADS, 1), F32),
                   jax.ShapeDtypeStruct((REL_BUCKETS * SWA_HEADS, 1), F32)],
        scratch_shapes=[pltpu.VMEM((SWA_HEADS, WINDOW, WINDOW), F32), pltpu.VMEM((SWA_HEADS, WINDOW, WINDOW), F32),
                        pltpu.VMEM((ALL_HEADS, 1), F32)],
        compiler_params=_params(("arbitrary",)),
    )(q, kv, g, o, dy, bucket, rel_bias, sink_col, *[a for a, _ in copies])


def _mem_probs(qh, mk):
    lg = _dot(qh, mk, NT) * (MEM_HD ** -0.5)
    e = jnp.exp(lg - jnp.max(lg, axis=-1, keepdims=True))
    return e / jnp.sum(e, axis=-1, keepdims=True)


def _mem_fwd(q, mkv, g):
    S = q.shape[0]
    M = mkv.shape[0]
    tq = 256

    def body(q_ref, mkv_ref, g_ref, o_ref, y_ref):
        for h in range(MEM_HEADS):
            c0 = h * MEM_HD
            pr = _mem_probs(q_ref[:, c0:c0 + MEM_HD], mkv_ref[:, c0:c0 + MEM_HD])
            o = _dot(pr.astype(BF), mkv_ref[:, D_MEM + c0:D_MEM + c0 + MEM_HD], NN)
            o_ref[:, c0:c0 + MEM_HD] = o
            gv = g_ref[:, c0:c0 + MEM_HD]
            y_ref[:, c0:c0 + MEM_HD] = (o * (gv * _sigmoid(gv))).astype(BF)

    blk = pl.BlockSpec((tq, D_MEM), lambda i: (i, 0))
    return pl.pallas_call(
        body, name="mem_fwd", grid=(S // tq,),
        in_specs=[blk, pl.BlockSpec((M, 2 * D_MEM), lambda i: (0, 0)), blk], out_specs=[blk, blk],
        out_shape=[jax.ShapeDtypeStruct((S, D_MEM), F32), jax.ShapeDtypeStruct((S, D_MEM), BF)],
        compiler_params=_params(("parallel",)),
    )(q, mkv, g)


def _mem_bwd(q, mkv, g, o, dy):
    S = q.shape[0]
    M = mkv.shape[0]
    tq = 256

    def body(q_ref, mkv_ref, g_ref, o_ref, dy_ref, dq_ref, dg_ref, dmkv_ref):
        @pl.when(pl.program_id(0) == 0)
        def _():
            dmkv_ref[...] = jnp.zeros_like(dmkv_ref)

        for h in range(MEM_HEADS):
            c0 = h * MEM_HD
            qh = q_ref[:, c0:c0 + MEM_HD]
            mk = mkv_ref[:, c0:c0 + MEM_HD]
            mv = mkv_ref[:, D_MEM + c0:D_MEM + c0 + MEM_HD]
            gv = g_ref[:, c0:c0 + MEM_HD]
            sg = _sigmoid(gv)
            dyv = dy_ref[:, c0:c0 + MEM_HD]
            dg_ref[:, c0:c0 + MEM_HD] = (dyv * o_ref[:, c0:c0 + MEM_HD] * (sg * (1.0 + gv * (1.0 - sg)))).astype(BF)
            do = (dyv * (gv * sg)).astype(BF)
            pr = _mem_probs(qh, mk)
            dp = _dot(do, mv, NT)
            ds = pr * (dp - jnp.sum(pr * dp, axis=-1, keepdims=True))
            dsb = (ds * (MEM_HD ** -0.5)).astype(BF)
            dq_ref[:, c0:c0 + MEM_HD] = _dot(dsb, mk, NN).astype(BF)
            dmkv_ref[:, c0:c0 + MEM_HD] += _dot(dsb, qh, TN)
            dmkv_ref[:, D_MEM + c0:D_MEM + c0 + MEM_HD] += _dot(pr.astype(BF), do, TN)

    blk = pl.BlockSpec((tq, D_MEM), lambda i: (i, 0))
    whole = pl.BlockSpec((M, 2 * D_MEM), lambda i: (0, 0))
    return pl.pallas_call(
        body, name="mem_bwd", grid=(S // tq,),
        in_specs=[blk, whole, blk, blk, blk], out_specs=[blk, blk, whole],
        out_shape=[jax.ShapeDtypeStruct((S, D_MEM), BF), jax.ShapeDtypeStruct((S, D_MEM), BF),
                   jax.ShapeDtypeStruct((M, 2 * D_MEM), F32)],
        compiler_params=_params(("arbitrary",)),
    )(q, mkv, g, o, dy)


MERGE_TN = 512


def _merge_specs(tm):
    ytile = pl.BlockSpec((tm, 1024), lambda i, j: (i, 0))
    wblk = pl.BlockSpec((MERGE_TN, 1024), lambda i, j: (j, 0))
    gls = [pl.BlockSpec((None, tm, MERGE_TN), (lambda i, j, br=br: (br, i, j))) for br in range(3)]
    otile = pl.BlockSpec((tm, MERGE_TN), lambda i, j: (i, j))
    return ytile, wblk, gls, otile


def _merge_fwd(ys, ws, gl, tm):
    S = gl.shape[1]

    def body(y0, y1, y2, w0, w1, w2, g0, g1, g2, o_ref):
        acc = None
        for y_ref, w_ref, g_ref in ((y0, w0, g0), (y1, w1, g1), (y2, w2, g2)):
            term = _sigmoid(g_ref[...]) * _dot(y_ref[...], w_ref[...], NT)
            acc = term if acc is None else acc + term
        o_ref[...] = acc.astype(BF)

    ytile, wblk, gls, otile = _merge_specs(tm)
    return pl.pallas_call(
        body, name="merge_fwd", grid=(S // tm, D_MODEL // MERGE_TN),
        in_specs=[ytile] * 3 + [wblk] * 3 + gls, out_specs=otile,
        out_shape=jax.ShapeDtypeStruct((S, D_MODEL), BF),
        compiler_params=_params(("parallel", "arbitrary")),
    )(*ys, *ws, gl, gl, gl)


def _merge_bwd(dout, w_out, ys, ws, gl, tm):
    S = gl.shape[1]

    def body(do_ref, wo_ref, y0, y1, y2, w0, w1, w2, g0, g1, g2, dg0, dg1, dg2, dp0, dp1, dp2):
        dm = _dot(do_ref[...], wo_ref[...], NT)
        for y_ref, w_ref, g_ref, dg_ref, dp_ref in ((y0, w0, g0, dg0, dp0), (y1, w1, g1, dg1, dp1), (y2, w2, g2, dg2, dp2)):
            gate = _sigmoid(g_ref[...])
            pv = _dot(y_ref[...], w_ref[...], NT)
            dg_ref[...] = (dm * pv * gate * (1.0 - gate)).astype(BF)
            dp_ref[...] = (dm * gate).astype(BF)

    ytile, wblk, gls, otile = _merge_specs(tm)
    out = jax.ShapeDtypeStruct((S, D_MODEL), BF)
    return pl.pallas_call(
        body, name="merge_bwd", grid=(S // tm, D_MODEL // MERGE_TN),
        in_specs=[pl.BlockSpec((tm, D_MODEL), lambda i, j: (i, 0)), pl.BlockSpec((MERGE_TN, D_MODEL), lambda i, j: (j, 0))]
        + [ytile] * 3 + [wblk] * 3 + gls,
        out_specs=[otile] * 6, out_shape=[out] * 6,
        compiler_params=_params(("parallel", "arbitrary")),
    )(dout, w_out, *ys, *ws, gl, gl, gl)


def _out_loss(merged, w_out, x, target, post_g, tm):
    S = x.shape[0]

    def body(m_ref, w_ref, x_ref, t_ref, g_ref, dout_ref, dy_ref, loss_ref, dpost_ref):
        @pl.when(pl.program_id(0) == 0)
        def _():
            loss_ref[...] = jnp.zeros_like(loss_ref)
            dpost_ref[...] = jnp.zeros_like(dpost_ref)

        out = _dot(m_ref[...], w_ref[...], NN)
        r = lax.rsqrt(jnp.mean(out * out, axis=-1, keepdims=True) + EPS)
        nrm = out * r
        gv = g_ref[...]
        err = (x_ref[...] + nrm * gv) - t_ref[...]
        sq = jnp.sum(jnp.sum(err * err, axis=1, keepdims=True), axis=0, keepdims=True)
        loss_ref[...] += sq * (0.5 / D_MODEL)
        dy = err * (1.0 / D_MODEL)
        dy_ref[...] = dy
        dpost_ref[...] += jnp.sum(dy * nrm, axis=0, keepdims=True)
        dn = dy * gv
        dout_ref[...] = (r * (dn - nrm * jnp.mean(dn * nrm, axis=-1, keepdims=True))).astype(BF)

    row = pl.BlockSpec((tm, D_MODEL), lambda i: (i, 0))
    return pl.pallas_call(
        body, name="out_loss", grid=(S // tm,),
        in_specs=[row, pl.BlockSpec((D_MODEL, D_MODEL), lambda i: (0, 0)), row, row, pl.BlockSpec((1, D_MODEL), lambda i: (0, 0))],
        out_specs=[row, row, pl.BlockSpec((8, 128), lambda i: (0, 0)), pl.BlockSpec((1, D_MODEL), lambda i: (0, 0))],
        out_shape=[jax.ShapeDtypeStruct((S, D_MODEL), BF), jax.ShapeDtypeStruct((S, D_MODEL), F32),
                   jax.ShapeDtypeStruct((8, 128), F32), jax.ShapeDtypeStruct((1, D_MODEL), F32)],
        compiler_params=_params(("arbitrary",)),
    )(merged, w_out, x, target, post_g)


DH_DX_CHUNK = 64


def _dh_dx(dproj, w_in, x, dy, pre_g, tm, tk):
    S = x.shape[0]
    nk = D_IN // tk

    def body(dp_ref, w_ref, x_ref, dy_ref, g_ref, dx_ref, dpre_ref, acc_ref):
        i, k = pl.program_id(0), pl.program_id(1)

        @pl.when(jnp.logical_and(i == 0, k == 0))
        def _():
            dpre_ref[...] = jnp.zeros_like(dpre_ref)

        @pl.when(k == 0)
        def _():
            acc_ref[...] = jnp.zeros_like(acc_ref)

        acc_ref[...] += _dot(dp_ref[...], w_ref[...], NN)

        @pl.when(k == nk - 1)
        def _():
            def chunk(c, carry):
                rows = pl.ds(pl.multiple_of(c * DH_DX_CHUNK, DH_DX_CHUNK), DH_DX_CHUNK)
                dh = acc_ref[rows, :]
                xv = x_ref[rows, :]
                r = lax.rsqrt(jnp.mean(xv * xv, axis=-1, keepdims=True) + EPS)
                nrm = xv * r
                dpre_ref[...] += jnp.sum(dh * nrm, axis=0, keepdims=True)
                dn = dh * g_ref[...]
                dx_ref[rows, :] = r * (dn - nrm * jnp.mean(dn * nrm, axis=-1, keepdims=True)) + dy_ref[rows, :]
                return carry
            lax.fori_loop(0, tm // DH_DX_CHUNK, chunk, 0)

    row = pl.BlockSpec((tm, D_MODEL), lambda i, k: (i, 0))
    vec = pl.BlockSpec((1, D_MODEL), lambda i, k: (0, 0))
    return pl.pallas_call(
        body, name="dh_dx", grid=(S // tm, nk),
        in_specs=[pl.BlockSpec((tm, tk), lambda i, k: (i, k)), pl.BlockSpec((tk, D_MODEL), lambda i, k: (k, 0)), row, row, vec],
        out_specs=[row, vec],
        out_shape=[jax.ShapeDtypeStruct((S, D_MODEL), F32), jax.ShapeDtypeStruct((1, D_MODEL), F32)],
        scratch_shapes=[pltpu.VMEM((tm, D_MODEL), F32)],
        compiler_params=_params(("arbitrary", "arbitrary"), large=True),
    )(dproj, w_in, x, dy, pre_g)


def _sum_parts(parts, name):
    P, R, C = parts.shape
    tr = max(t for t in range(8, 513, 8) if R % t == 0)

    def body(p_ref, o_ref):
        acc = p_ref[0]
        for j in range(1, P):
            acc = acc + p_ref[j]
        o_ref[...] = acc

    return pl.pallas_call(
        body, name=name, grid=(R // tr,),
        in_specs=[pl.BlockSpec((P, tr, C), lambda i: (0, i, 0))], out_specs=pl.BlockSpec((tr, C), lambda i: (i, 0)),
        out_shape=jax.ShapeDtypeStruct((R, C), F32), compiler_params=_params(("parallel",)),
    )(parts)


def _adamw(land, sums, chip, w, m, v, name, group=(0, 1), into=None):
    q, n_groups = group
    _, R, cols = land.shape
    C = cols * n_groups
    tr = max(t for t in range(16, 257, 16) if R % t == 0)
    c1 = 1.0 - ADAM_B1 ** ADAM_STEP
    c2 = 1.0 - ADAM_B2 ** ADAM_STEP
    n_into = 0 if into is None else 4

    def body(chip_ref, p0_ref, p1_ref, p2_ref, own_ref, w_ref, m_ref, v_ref, *refs):
        g_ref, d_ref, nm_ref, nv_ref = refs[n_into:]
        g = own_ref[...].astype(F32)
        for p_ref in (p0_ref, p1_ref, p2_ref):
            g = g + p_ref[...].astype(F32)
        nm = ADAM_B1 * m_ref[...] + (1.0 - ADAM_B1) * g
        nv = ADAM_B2 * v_ref[...] + (1.0 - ADAM_B2) * (g * g)
        g_ref[...] = g
        nm_ref[...] = nm
        nv_ref[...] = nv
        d_ref[...] = -ADAM_LR * ((nm / c1) / (jnp.sqrt(nv / c2) + ADAM_EPS) + ADAM_WD * w_ref[...])

    tile = pl.BlockSpec((None, tr, cols), lambda i, c_ref: (0, i, q))
    specs = [pl.BlockSpec((None, tr, cols), (lambda i, c_ref, k=k: (k + (c_ref[0] <= k).astype(jnp.int32), i, 0))) for k in range(3)]
    specs.append(pl.BlockSpec((None, tr, cols), (lambda i, c_ref: (c_ref[0], i, 0))))
    return pl.pallas_call(
        body, name=name,
        grid_spec=pltpu.PrefetchScalarGridSpec(num_scalar_prefetch=1, grid=(R // tr,),
                                               in_specs=specs + [tile, tile, tile] + [pl.BlockSpec(memory_space=pl.ANY)] * n_into,
                                               out_specs=[tile] * 4),
        out_shape=[jax.ShapeDtypeStruct((1, R, C), F32)] * 4,
        input_output_aliases={8 + k: k for k in range(n_into)},
        compiler_params=_params(("parallel",)),
    )(chip, land, land, land, sums, w, m, v, *(into or []))


def _adamw_small(gs, ws, ms, vs):
    n = len(ws)
    c1 = 1.0 - ADAM_B1 ** ADAM_STEP
    c2 = 1.0 - ADAM_B2 ** ADAM_STEP

    def flat2(a):
        return a.reshape(-1, a.shape[-1])

    def body(*refs):
        ins, outs = refs[:4 * n], refs[4 * n:]
        for a in range(n):
            g, w, m, v = (ins[k * n + a][...] for k in range(4))
            nm = ADAM_B1 * m + (1.0 - ADAM_B1) * g
            nv = ADAM_B2 * v + (1.0 - ADAM_B2) * (g * g)
            outs[a][...] = g
            outs[n + a][...] = -ADAM_LR * ((nm / c1) / (jnp.sqrt(nv / c2) + ADAM_EPS) + ADAM_WD * w)
            outs[2 * n + a][...] = nm
            outs[3 * n + a][...] = nv

    shapes = [flat2(w).shape for w in ws]
    out = pl.pallas_call(
        body, name="adamw_small", out_shape=[jax.ShapeDtypeStruct(sh, F32) for sh in shapes] * 4,
        compiler_params=_params(),
    )(*[g.reshape(sh) for g, sh in zip(gs, shapes)], *[flat2(a) for a in (*ws, *ms, *vs)])
    return [[out[k * n + a].reshape(ws[a].shape) for a in range(n)] for k in range(4)]


PROJ_ROWS = 512


def _project(h, w_t, dep=None):
    S = h.shape[0]
    n_tiles = D_IN // SEG_TILE
    ranges = [(c0 // SEG_TILE, (c0 + width) // SEG_TILE) for _, c0, width, _ in SEGMENTS]
    dtypes = (F32, BF)
    n_extra = int(dep is not None)

    def of_dtype(j, dt):
        hit = False
        for (j0, j1), seg in zip(ranges, SEGMENTS):
            if seg[3] == dt:
                hit = jnp.logical_and(j >= j0, j < j1) | hit
        return hit

    def body(h_ref, w_ref, *refs):
        outs = refs[n_extra:n_extra + len(SEGMENTS)]
        stages, sems = refs[n_extra + len(SEGMENTS):-1], refs[-1]
        j = pl.program_id(0)
        slot = j % 2

        def copy_out(k, stage, dst):
            return pltpu.make_async_copy(stages[k].at[stage], dst, sems.at[stage])

        def wait_tile(jj, stage):
            for k, dt in enumerate(dtypes):
                o_ref = [o for o, seg in zip(outs, SEGMENTS) if seg[3] == dt and seg[0] != "gl"][0]
                @pl.when(of_dtype(jj, dt))
                def _(k=k, o_ref=o_ref):
                    copy_out(k, stage, o_ref.at[:, pl.ds(0, SEG_TILE)]).wait()

        @pl.when(j >= 2)
        def _():
            wait_tile(j - 2, slot)

        for k, dt in enumerate(dtypes):
            @pl.when(of_dtype(j, dt))
            def _(k=k, dt=dt):
                for c in range(S // PROJ_ROWS):
                    rows = pl.ds(c * PROJ_ROWS, PROJ_ROWS)
                    stages[k][slot, rows, :] = _dot(h_ref[rows, :], w_ref[...], NT).astype(dt)

        for (j0, j1), (name, _, _, dt), o_ref in zip(ranges, SEGMENTS, outs):
            @pl.when(jnp.logical_and(j >= j0, j < j1))
            def _(j0=j0, j1=j1, name=name, dt=dt, o_ref=o_ref):
                t = j - j0
                if name == "gl":
                    per = (j1 - j0) // 3
                    dst = o_ref.at[t // per, :, pl.ds(pl.multiple_of((t % per) * SEG_TILE, SEG_TILE), SEG_TILE)]
                else:
                    dst = o_ref.at[:, pl.ds(pl.multiple_of(t * SEG_TILE, SEG_TILE), SEG_TILE)]
                copy_out(dtypes.index(dt), slot, dst).start()

        @pl.when(j == n_tiles - 1)
        def _():
            wait_tile(j - 1, 1 - slot)
            wait_tile(j, slot)

    out_shapes = [jax.ShapeDtypeStruct((3, S, width // 3) if name == "gl" else (S, width), dt) for name, _, width, dt in SEGMENTS]
    outs = pl.pallas_call(
        body, name="proj", grid=(n_tiles,),
        in_specs=[pl.BlockSpec((S, D_MODEL), lambda j: (0, 0)), pl.BlockSpec((SEG_TILE, D_MODEL), lambda j: (j, 0))]
        + ([] if dep is None else [pl.BlockSpec((8, 128), lambda j: (0, 0))]),
        out_specs=[pl.BlockSpec(memory_space=pl.ANY)] * len(SEGMENTS), out_shape=out_shapes,
        scratch_shapes=[pltpu.VMEM((2, S, SEG_TILE), dt) for dt in dtypes] + [pltpu.SemaphoreType.DMA((2,))],
        compiler_params=_params(("arbitrary",), large=True),
    )(h, w_t, *([] if dep is None else [dep]))
    return {name: o for (name, _, _, _), o in zip(SEGMENTS, outs)}


def _forward_a(h, memn, w_in, sinks, rel_bias, dep=None):
    S = h.shape[0]
    st = dict(T=min(512, S // 2), tm=min(512, S), bucket=_rel_bucket_map(), h=h, memn=memn)
    seg = st["seg"] = _project(h, w_in, dep)
    st["o_swa"], st["y_swa"] = _swa_fwd(seg["q_s"], seg["kv"], seg["g_swa"], st["bucket"], rel_bias, _sink_column(sinks))
    return st


def _forward_rg(st, conv_w, conv_b, w_a, b_a, w_x, b_x, lam):
    seg = st["seg"]
    st["h_rg"], st["y_rg"] = _rglru_fwd(seg["xr"], seg["g_rg"], conv_w, conv_b, w_a, b_a, w_x, b_x, lam, st["T"])
    return st


def _forward_b(st, x, target, post_g, w_memkv, wbr, w_out):
    S = x.shape[0]
    M = st["memn"].shape[0]
    seg = st["seg"]
    st["mkv"] = _matmul(st["memn"], w_memkv, "nn", M, 2 * D_MEM, D_MODEL, M, 512, D_MODEL, BF, "mem_kv")
    st["o_mem"], st["y_mem"] = _mem_fwd(seg["q_m"], st["mkv"], seg["g_mem"])
    st["ys"] = (st["y_rg"], st["y_swa"], st["y_mem"])
    st["merged"] = _merge_fwd(st["ys"], wbr, seg["gl"], st["tm"])
    st["dout"], st["dy"], st["loss"], st["dpost"] = _out_loss(st["merged"], w_out, x, target, post_g, min(256, S))
    return st


def _backward_a1(st, wbr, w_out):
    S = st["h"].shape[0]
    seg, ys, tm = st["seg"], st["ys"], st["tm"]
    st["dw_out"] = _matmul(st["merged"], st["dout"], "tn", D_MODEL, D_MODEL, S, 256, D_MODEL, S, BF, "dw_out", out_blocked="row")
    dgl0, dgl1, dgl2, dp0, dp1, dp2 = _merge_bwd(st["dout"], w_out, ys, wbr, seg["gl"], tm)
    st["dgl"] = (dgl0, dgl1, dgl2)
    dys, dwbr = [], []
    for i, dp in enumerate((dp0, dp1, dp2)):
        dys.append(_matmul(dp, wbr[i], "nn", S, 1024, D_MODEL, tm, 1024, D_MODEL, F32, "dy_br%d" % i))
        dwbr.append(_matmul(ys[i], dp, "tn", 1024, D_MODEL, S, 1024, 256, S, BF, "dw_br%d" % i, out_blocked="col"))
    st["dys"], st["dwbr"] = dys, dwbr
    return st


def _backward_a2(st, mem, w_memkv, conv_w, conv_b, w_a, b_a, w_x, b_x, lam):
    M = mem.shape[0]
    seg, dys = st["seg"], st["dys"]
    st["dq_m"], st["dg_mem"], dmkv = _mem_bwd(seg["q_m"], st["mkv"], seg["g_mem"], st["o_mem"], dys[2])
    st["dmkv"] = dmkv.astype(BF)
    st["dw_memkv"] = _matmul(st["memn"], st["dmkv"], "tn", D_MODEL, 2 * D_MEM, M, 256, 2 * D_MEM, M, BF, "dw_memkv", out_blocked="row")
    st["dxr"], st["dg_rg"], st["dw_a"], st["dw_x"], st["dvec"] = _rglru_bwd(
        seg["xr"], seg["g_rg"], st["h_rg"], dys[0], conv_w, conv_b, w_a, b_a, w_x, b_x, lam, st["T"])
    return st


def _mem_gain_grad(st, mem, w_memkv, dep=None):
    M = mem.shape[0]
    dmemn = _matmul(st["dmkv"], w_memkv, "nt", M, D_MODEL, 2 * D_MEM, M, 512, 2 * D_MEM, F32, "dmemn", dep=dep)
    return _rms_gain_grad(dmemn, mem, "dmem_gain")


def _backward_b(st, rel_bias, sinks):
    seg = st["seg"]
    others = {"xr": st["dxr"], "g_rg": st["dg_rg"], "q_m": st["dq_m"], "g_mem": st["dg_mem"], "gl": st["dgl"]}
    dproj, dkv, dsinks, drel = _swa_bwd(seg["q_s"], seg["kv"], seg["g_swa"], st["o_swa"], st["dys"][1],
                                        st["bucket"], rel_bias, _sink_column(sinks), others)
    st["dsinks"], st["drel"] = dsinks.reshape(1, SWA_HEADS), drel.reshape(REL_BUCKETS, SWA_HEADS)
    col_kv = [c0 for name, c0, _, _ in SEGMENTS if name == "kv"][0]
    st["dproj"] = lax.dynamic_update_slice(dproj, dkv.astype(BF), (0, col_kv))
    return st


def _dw_in_half(st, half, dep=None):
    S = st["h"].shape[0]
    dw = _matmul(st["dproj"], st["h"], "tn", D_IN, D_MODEL // 2, S, D_IN_TILE, D_MODEL // 2, S, BF, "dw_in%d" % half, b_noff=half, dep=dep)
    return dw.reshape(N_DEV, D_IN // N_DEV, D_MODEL // 2)


def _owner_blocks(a):
    return jnp.swapaxes(a.reshape((4, 2) + a.shape[1:]), 0, 1)


def _pad_rows(a, rows):
    a = a.reshape(-1, 128) if a.shape[-1] % 128 == 0 else jnp.pad(a, ((0, 0), (0, 128 - a.shape[-1])))
    return jnp.pad(a, ((0, rows - a.shape[0]), (0, 0))) if a.shape[0] < rows else a


def kernel(x, mem, pre_norm_g, post_norm_g, mem_norm_g, w_in, conv_w, conv_b, w_rg_a, b_rg_a, w_rg_x, b_rg_x, lru_lambda, swa_sinks, rel_bias, w_mem_kv, w_br_rg, w_br_swa, w_br_mem, w_out, loss_target, m_pre_norm_g, m_post_norm_g, m_mem_norm_g, m_w_in, m_conv_w, m_conv_b, m_w_rg_a, m_b_rg_a, m_w_rg_x, m_b_rg_x, m_lru_lambda, m_swa_sinks, m_rel_bias, m_w_mem_kv, m_w_br_rg, m_w_br_swa, m_w_br_mem, m_w_out, v_pre_norm_g, v_post_norm_g, v_mem_norm_g, v_w_in, v_conv_w, v_conv_b, v_w_rg_a, v_b_rg_a, v_w_rg_x, v_b_rg_x, v_lru_lambda, v_swa_sinks, v_rel_bias, v_w_mem_kv, v_w_br_rg, v_w_br_swa, v_w_br_mem, v_w_out):
    cx, cy, cc = lax.axis_index("x"), lax.axis_index("y"), lax.axis_index("c")
    me = 4 * cx + 2 * cy + cc
    chip = 2 * cx + cy
    core = jnp.reshape(cc, (1,)).astype(jnp.int32)
    x0, mem0 = x[0], mem[0]
    w_a_b, w_x_b = w_rg_a[0].astype(BF), w_rg_x[0].astype(BF)

    def landing(own, slot, slots):
        return lax.dynamic_update_slice(lax.empty((slots,) + own.shape, own.dtype), own[None], (slot,) + (0,) * own.ndim)


    def swap_start(parts, tag):
        return _exchange_start(parts, [lax.empty((4,) + p.shape[-2:], p.dtype) for p in parts], _plan_swap([p.ndim for p in parts]),
                               "swap_%s_start" % tag)

    def scatter_start(swap, after, tag, prefill=()):
        s_send, s_recv, parts, got, _ = swap
        got = _exchange_wait(s_send, s_recv, parts, got, _plan_swap([p.ndim for p in parts]), after, "swap_%s_wait" % tag)
        sums = [_pair_sum(p, g, core, "scatter_%s_sum%d" % (tag, i)) for i, (p, g) in enumerate(zip(parts, got))]
        lands = [landing(lax.dynamic_index_in_dim(s, chip, 0, keepdims=False), chip, 4) if i in prefill
                 else lax.empty(s.shape, s.dtype) for i, s in enumerate(sums)]
        return _exchange_start(sums, lands, _plan_scatter(len(sums)), "scatter_%s_start" % tag)

    def corner(a):
        return a.reshape(-1, a.shape[-1])[:8, :128]

    def zero_after(a):
        return jnp.minimum(jnp.abs(a.reshape(-1)[0].astype(F32)), 0.0)

    g_in, g_cw, h0, memn0 = _all_gather_relayed([jnp.transpose(w_in[0]).astype(BF), conv_w[0]], [True, False], "gather_w_in",
                                                side=_rms_side([(x0, pre_norm_g), (mem0, mem_norm_g)]))
    w_in_f = g_in.reshape(D_IN, D_MODEL)
    conv_w_f = jnp.transpose(g_cw, (1, 0, 2)).reshape(CONV_W, D_RNN)

    after_first = zero_after(g_cw).astype(BF)
    rest = [w.astype(BF) + after_first for w in (w_mem_kv[0], jnp.transpose(w_br_rg[0]), jnp.transpose(w_br_swa[0]),
                                                 jnp.transpose(w_br_mem[0]), w_out[0])]
    plan_g = _plan_gather(len(rest))
    zones = _place_own([lax.empty((N_DEV,) + w.shape, w.dtype) for w in rest], rest, jnp.reshape(me, (1,)).astype(jnp.int32), "gather_rest_own")
    g_send, g_recv, g_src, g_land, g_token = _exchange_start(rest, zones, plan_g, "gather_rest_start")
    st = _forward_a(h0, memn0, w_in_f, swa_sinks, rel_bias, dep=g_token)
    g_land = _exchange_wait(g_send, g_recv, g_src, g_land, plan_g, st["y_swa"], "gather_rest_wait")
    plan_f = _plan_forward(len(rest))
    f_send, f_recv, _, g_land, f_token = _exchange_start(None, g_land, plan_f, "forward_rest_start")
    st = _forward_rg(st, conv_w_f, conv_b + f_token[0:1, 0:1], w_a_b, b_rg_a, w_x_b, b_rg_x, lru_lambda)
    g_land = _exchange_wait(f_send, f_recv, None, g_land, plan_f, corner(st["y_rg"]), "forward_rest_wait")
    w_memkv_f = g_land[0].reshape(D_MODEL, 2 * D_MEM)
    wbr = tuple(g_land[i].reshape(D_MODEL, D_RNN) for i in (1, 2, 3))
    w_out_f = g_land[4].reshape(D_MODEL, D_MODEL)

    st = _forward_b(st, x0, loss_target[0], post_norm_g, w_memkv_f, wbr, w_out_f)
    st = _backward_a1(st, wbr, w_out_f)
    parts_a = [st["dw_out"], st["dwbr"][0], st["dwbr"][1], st["dwbr"][2]]
    plan_a = _plan_scatter(len(parts_a))
    swap_a = swap_start(parts_a, "a")
    st = _backward_a2(st, mem0, w_memkv_f, conv_w_f, conv_b + swap_a[4][0:1, 0:1], w_a_b, b_rg_a, w_x_b, b_rg_x, lru_lambda)
    a_send, a_recv, a_src, a_land, a_token = scatter_start(swap_a, st["dxr"], "a")
    parts_c = [st["dw_memkv"], _owner_blocks(st["dw_a"]), _owner_blocks(st["dw_x"])]
    plan_c = _plan_scatter(len(parts_c))
    swap_c = swap_start(parts_c, "c")

    st = _backward_b(st, rel_bias, swa_sinks + swap_c[4][0:1, 0:1] + a_token[0:1, 0:1])
    c_send, c_recv, c_src, c_land, c_token = scatter_start(swap_c, st["dsinks"], "c", prefill=(1, 2))
    plan_b = _plan_scatter(1)

    def dw_in_parts(half, dep):
        dwh = _dw_in_half(st, half, dep)
        return dwh, [dwh]

    dw0, parts_b0 = dw_in_parts(0, c_token)
    swap_b0 = swap_start(parts_b0, "b0")
    a_land = _exchange_wait(a_send, a_recv, a_src, a_land, plan_a, swap_b0[4], "scatter_a_wait")
    big = [None] * 6

    chip1 = jnp.reshape(chip, (1,)).astype(jnp.int32)

    def adamw_big(j, land, own, wt, mt, vt):
        big[j] = _adamw(land, own, chip1, wt, mt, vt, "adamw_big%d" % j)

    adamw_big(5, a_land[0], a_src[0], w_out, m_w_out, v_w_out)
    adamw_big(2, a_land[1], a_src[1], w_br_rg, m_w_br_rg, v_w_br_rg)
    adamw_big(3, a_land[2], a_src[2], w_br_swa, m_w_br_swa, v_w_br_swa)
    halves = [scatter_start(swap_b0, corner(big[5][1]) + corner(big[2][1]) + corner(big[3][1]), "b0")]
    dw1, parts_b1 = dw_in_parts(1, halves[0][4])
    swap_b1 = swap_start(parts_b1, "b1")
    c_land = _exchange_wait(c_send, c_recv, c_src, c_land, plan_c, swap_b1[4], "scatter_c_wait")
    g_wa_blk = _sum_parts(c_land[1], "sum_w_rg_a")
    g_wx_blk = _sum_parts(c_land[2], "sum_w_rg_x")
    adamw_big(4, a_land[3], a_src[3], w_br_mem, m_w_br_mem, v_w_br_mem)
    adamw_big(1, c_land[0], c_src[0], w_mem_kv, m_w_mem_kv, v_w_mem_kv)
    halves.append(scatter_start(swap_b1, corner(big[4][1]) + corner(big[1][1]), "b1"))
    st["dmem_g"] = _mem_gain_grad(st, mem0, w_memkv_f, halves[1][4])
    grad_x, dpre = _dh_dx(st["dproj"], w_in_f, x0, st["dy"], pre_norm_g + halves[1][4][0:1, 0:1], st["tm"], D_IN_TILE)
    pack = jnp.concatenate([dpre.reshape(16, 128), st["dpost"].reshape(16, 128), st["dmem_g"].reshape(16, 128),
                            st["dvec"].reshape(64, 128), _pad_rows(st["dsinks"], 8), _pad_rows(st["drel"], 32), g_wa_blk, g_wx_blk,
                            st["loss"]], axis=0)
    plan_s = _plan_everyone(1)
    s_send, s_recv, s_src, s_land, s_token = _exchange_start([pack], [landing(pack, me, N_DEV)], plan_s, "gather_small_start")
    swap_last = lambda a: jnp.transpose(a, (0, 2, 1))
    after, big0_t = s_token, None
    for half, (b_send, b_recv, b_src, b_land, _) in enumerate(halves):
        b_land = _exchange_wait(b_send, b_recv, b_src, b_land, plan_b, after, "scatter_b%d_wait" % half)[0]
        big0_t = _adamw(b_land, b_src[0], chip1, swap_last(w_in), swap_last(m_w_in), swap_last(v_w_in), "adamw_big0_%d" % half,
                        group=(half, 2), into=big0_t)
        after = corner(big0_t[1])
    big[0] = [swap_last(a) for a in big0_t]
    gathered = _exchange_wait(s_send, s_recv, s_src, s_land, plan_s, corner(big0_t[1]), "gather_small_wait")[0]
    gs = _sum_parts(gathered, "sum_small")
    loss_total = gs[408, 0]
    g_pre, g_post, g_memg = gs[0:16].reshape(1, D_MODEL), gs[16:32].reshape(1, D_MODEL), gs[32:48].reshape(1, D_MODEL)
    gvec = gs[48:112].reshape(8, D_RNN)
    g_conv_w = lax.dynamic_slice(gvec[0:CONV_W], (0, me * RNN_BLOCK), (CONV_W, RNN_BLOCK))
    g_conv_b, g_b_a, g_b_x, g_lam = gvec[4:5], gvec[5:6], gvec[6:7], gvec[7:8]
    g_sinks = gs[112:113, :SWA_HEADS]
    g_rel = gs[120:152, :SWA_HEADS]
    g_w_a = gathered[:, 152:280]
    g_w_x = gathered[:, 280:408]

    g_small = (g_pre, g_post, g_memg, g_conv_b, g_b_a, g_b_x, g_lam, g_w_a, g_w_x, g_sinks, g_rel, g_conv_w)
    w_small = (pre_norm_g, post_norm_g, mem_norm_g, conv_b, b_rg_a, b_rg_x, lru_lambda, w_rg_a, w_rg_x, swa_sinks, rel_bias, conv_w)
    m_small = (m_pre_norm_g, m_post_norm_g, m_mem_norm_g, m_conv_b, m_b_rg_a, m_b_rg_x, m_lru_lambda, m_w_rg_a, m_w_rg_x, m_swa_sinks, m_rel_bias, m_conv_w)
    v_small = (v_pre_norm_g, v_post_norm_g, v_mem_norm_g, v_conv_b, v_b_rg_a, v_b_rg_x, v_lru_lambda, v_w_rg_a, v_w_rg_x, v_swa_sinks, v_rel_bias, v_conv_w)
    sm = _adamw_small(g_small, w_small, m_small, v_small)


    def leaves(k):
        s = sm[k]
        return [s[0], s[1], s[2], big[0][k], s[11], s[3], s[7], s[4], s[8], s[5], s[6], s[9], s[10],
                big[1][k], big[2][k], big[3][k], big[4][k], big[5][k]]

    return (loss_total, grad_x[None], *leaves(0), *leaves(1), *leaves(2), *leaves(3))
```

```python
import math

import jax
import jax.numpy as jnp
import numpy as np
from jax import lax
from jax.experimental import pallas as pl
from jax.experimental.pallas import tpu as pltpu

F32, BF = jnp.float32, jnp.bfloat16
MESH = pl.DeviceIdType.MESH
N_DEV = 8

D_MODEL = 2048
D_RNN = 1024
RNN_BLOCKS = 8
RNN_BLOCK = 128
CONV_W = 4
LRU_C = 8.0
SWA_HEADS = 16
SWA_KV_HEADS = 2
SWA_HD = 64
WINDOW = 128
MEM_HEADS = 4
MEM_HD = 256
D_MEM = 1024
REL_BUCKETS = 32
REL_MAX_DIST = 128
EPS = 1e-6
NEG_INF = -1e30
D_IN = 12544
SEGMENTS = (("xr", 0, 1024, F32), ("g_rg", 1024, 1024, F32), ("q_s", 2048, 1024, BF), ("kv", 3072, 256, BF),
            ("g_swa", 3328, 1024, F32), ("q_m", 4352, 1024, BF), ("g_mem", 5376, 1024, F32), ("gl", 6400, 6144, F32))
SEG_TILE = 256
D_IN_TILE = 7 * SEG_TILE

ADAM_LR, ADAM_B1, ADAM_B2, ADAM_EPS, ADAM_WD, ADAM_STEP = 0.001, 0.9, 0.999, 1e-08, 0.01, 10

NN = (((1,), (0,)), ((), ()))
NT = (((1,), (1,)), ((), ()))
TN = (((0,), (0,)), ((), ()))
MIB = 2 ** 20


def _dot(a, b, dn):
    return lax.dot_general(a, b, dn, preferred_element_type=F32)


VMEM_LIMIT_MIB = 48
VMEM_LIMIT_LARGE_MIB = 56


def _params(sem=None, large=False):
    return pltpu.CompilerParams(dimension_semantics=sem, vmem_limit_bytes=(VMEM_LIMIT_LARGE_MIB if large else VMEM_LIMIT_MIB) * MIB)


def _sigmoid(z):
    return 1.0 / (1.0 + jnp.exp(-z))


def _softplus(z):
    return jnp.maximum(z, 0.0) + jnp.log(1.0 + jnp.exp(-jnp.abs(z)))


def _expm1(z):
    p = z * (1.0 + z * (0.5 + z * (1.0 / 6 + z * (1.0 / 24 + z * (1.0 / 120 + z * (1.0 / 720 + z * (1.0 / 5040 + z / 40320)))))))
    return jnp.where(jnp.abs(z) < 0.3, p, jnp.exp(z) - 1.0)


def _flat(p):
    return 4 * p[0] + 2 * p[1] + p[2]


def _all_gather_relayed(arrs, relay, name, side=None):
    n = len(arrs)
    K = 9
    work, side_ins, side_outs, side_scratch = side if side is not None else (None, [], [], [])
    n_in, n_out = n + len(side_ins), n + len(side_outs)

    def body(*refs):
        ins, outs = refs[:n], refs[n_in:n_in + n]
        send_sems, recv_sems, local_sems = refs[n_in + n_out:n_in + n_out + 3]
        x, y, c = lax.axis_index("x"), lax.axis_index("y"), lax.axis_index("c")
        me, sib = (x, y, c), (x, y, 1 - c)
        xn, yn, dg = (1 - x, y, c), (x, 1 - y, c), (1 - x, 1 - y, c)

        def other(p):
            return (p[0], p[1], 1 - p[2])

        def rows(a, half):
            h = arrs[a].shape[0] // 2
            return pl.ds(half * h, h)

        def copy(a, k, block, to, half=None, src=None):
            dst = outs[a].at[_flat(block)]
            if half is not None:
                dst = dst.at[rows(a, half)]
            return pltpu.make_async_remote_copy(src_ref=dst if src is None else src, dst_ref=dst,
                                                send_sem=send_sems.at[a * K + k], recv_sem=recv_sems.at[a * K + k],
                                                device_id=to, device_id_type=MESH)

        mine = [pltpu.make_async_copy(ins[a], outs[a].at[_flat(me)], local_sems.at[a]) for a in range(n)]
        for cp in mine:
            cp.start()
        sends = []

        def start(cp):
            cp.start()
            sends.append(cp)

        for a in range(n):
            start(copy(a, 1, me, xn, src=ins[a]))
            start(copy(a, 2, me, yn, src=ins[a]))
            if not relay[a]:
                start(copy(a, 3, me, dg, src=ins[a]))
            start(copy(a, 0, me, sib, src=ins[a]))
        if work is not None:
            work(refs[n:n_in], refs[n_in + n:n_in + n_out], refs[n_in + n_out + 3:])
        for a in range(n):
            copy(a, 1, xn, me).wait_recv()
            if relay[a]:
                start(copy(a, 3, xn, yn, half=0))
            start(copy(a, 5, xn, sib))
        for a in range(n):
            copy(a, 2, yn, me).wait_recv()
            if relay[a]:
                start(copy(a, 4, yn, xn, half=1))
            start(copy(a, 6, yn, sib))
        for a in range(n):
            if relay[a]:
                copy(a, 3, dg, me, half=0).wait_recv()
                start(copy(a, 7, dg, sib, half=0))
                copy(a, 4, dg, me, half=1).wait_recv()
                start(copy(a, 8, dg, sib, half=1))
            else:
                copy(a, 3, dg, me).wait_recv()
                start(copy(a, 7, dg, sib))
        for a in range(n):
            copy(a, 0, sib, me).wait_recv()
            copy(a, 5, other(xn), me).wait_recv()
            copy(a, 6, other(yn), me).wait_recv()
            if relay[a]:
                copy(a, 7, other(dg), me, half=0).wait_recv()
                copy(a, 8, other(dg), me, half=1).wait_recv()
            else:
                copy(a, 7, other(dg), me).wait_recv()
        for cp in sends:
            cp.wait_send()
        for cp in mine:
            cp.wait()

    any_spec = pl.BlockSpec(memory_space=pl.ANY)
    return pl.pallas_call(
        body, name=name,
        out_shape=[jax.ShapeDtypeStruct((N_DEV,) + a.shape, a.dtype) for a in arrs] + list(side_outs),
        in_specs=[any_spec] * n_in, out_specs=[any_spec] * n_out,
        scratch_shapes=[pltpu.SemaphoreType.DMA((K * n,)), pltpu.SemaphoreType.DMA((K * n,)), pltpu.SemaphoreType.DMA((n,))]
        + list(side_scratch),
        compiler_params=_params(),
    )(*arrs, *side_ins)


def _chip_peers(x, y):
    return [(1 - x, y), (x, 1 - y), (1 - x, 1 - y)]


def _chip(p):
    return 2 * p[0] + p[1]


def _plan_gather(n):
    def plan(x, y, c):
        out = []
        for a in range(n):
            for peer in [(x, y, 1 - c)] + [(*ch, c) for ch in _chip_peers(x, y)]:
                out.append((a, None, ("lead", _flat((x, y, c))), peer, ("lead", _flat(peer))))
        return out
    return plan


def _plan_everyone(n):
    def plan(x, y, c):
        out = []
        for a in range(n):
            for r in range(1, N_DEV):
                peer = (1 - x if r & 4 else x, 1 - y if r & 2 else y, 1 - c if r & 1 else c)
                out.append((a, None, ("lead", _flat((x, y, c))), peer, ("lead", _flat(peer))))
        return out
    return plan


def _plan_swap(ndims):
    def plan(x, y, c):
        out = []
        for a, nd in enumerate(ndims):
            if nd == 4:
                out.append((a, 1 - c, ("all", 0), (x, y, 1 - c), ("all", 0)))
            else:
                out += [(a, 2 * j + 1 - c, ("lead", j), (x, y, 1 - c), ("lead", j)) for j in range(4)]
        return out
    return plan


def _slot(ref, where):
    kind, k = where
    return ref if kind == "all" else ref.at[k]


def _plan_scatter(n):
    def plan(x, y, c):
        out = []
        for a in range(n):
            for ch in _chip_peers(x, y):
                out.append((a, _chip(ch), ("lead", _chip((x, y))), (*ch, c), ("lead", _chip(ch))))
        return out
    return plan


HBM_SPEC = pl.BlockSpec(memory_space=pltpu.HBM)
SEM_SPEC = pl.BlockSpec(memory_space=pltpu.SEMAPHORE)


def _in_hbm(a):
    return pltpu.with_memory_space_constraint(a, pltpu.HBM)


def _exchange_start(srcs, lands, plan, name):
    n = len(lands)
    ns = 0 if srcs is None else n
    count = len(plan(0, 0, 0))

    def body(*refs):
        land_refs = refs[ns:ns + n]
        src_refs = land_refs if srcs is None else refs[:n]
        send_sems, recv_sems = refs[ns + n], refs[ns + n + 1]
        token = refs[-1]
        x, y, c = lax.axis_index("x"), lax.axis_index("y"), lax.axis_index("c")
        for k, (a, si, di, peer, _) in enumerate(plan(x, y, c)):
            src = src_refs[a] if si is None else src_refs[a].at[si]
            pltpu.make_async_remote_copy(src_ref=src, dst_ref=_slot(land_refs[a], di), send_sem=send_sems.at[k],
                                         recv_sem=recv_sems.at[k], device_id=peer, device_id_type=MESH).start()
        token[...] = jnp.zeros_like(token)

    out = pl.pallas_call(
        body, name=name,
        out_shape=(pltpu.SemaphoreType.DMA((count,)), pltpu.SemaphoreType.DMA((count,)),
                   *[pltpu.HBM(a.shape, a.dtype) for a in lands], jax.ShapeDtypeStruct((8, 128), F32)),
        in_specs=[HBM_SPEC] * (ns + n),
        out_specs=(SEM_SPEC, SEM_SPEC, *([HBM_SPEC] * n), pl.BlockSpec(memory_space=pltpu.VMEM)),
        input_output_aliases={ns + i: 2 + i for i in range(n)},
        compiler_params=pltpu.CompilerParams(has_side_effects=pltpu.SideEffectType.DATAFLOW_SIDE_EFFECTING),
    )(*[_in_hbm(a) for a in (srcs or [])], *[_in_hbm(a) for a in lands])
    return out[0], out[1], srcs if srcs is None else list(srcs), list(out[2:2 + n]), out[-1]


def _exchange_wait(send_sems, recv_sems, srcs, lands, plan, after, name):
    n = len(lands)
    ns = 0 if srcs is None else n

    def body(*refs):
        land_refs = refs[ns:ns + n]
        src_refs = land_refs if srcs is None else refs[:n]
        send_sems, recv_sems = refs[ns + n], refs[ns + n + 1]
        x, y, c = lax.axis_index("x"), lax.axis_index("y"), lax.axis_index("c")
        for k, (a, si, _, peer, ri) in enumerate(plan(x, y, c)):
            src = src_refs[a] if si is None else src_refs[a].at[si]
            cp = pltpu.make_async_remote_copy(src_ref=src, dst_ref=_slot(land_refs[a], ri), send_sem=send_sems.at[k],
                                              recv_sem=recv_sems.at[k], device_id=peer, device_id_type=MESH)
            cp.wait_send()
            cp.wait_recv()

    out = pl.pallas_call(
        body, name=name,
        out_shape=tuple(pltpu.HBM(a.shape, a.dtype) for a in lands),
        in_specs=[HBM_SPEC] * (ns + n) + [SEM_SPEC, SEM_SPEC, pl.BlockSpec(memory_space=pl.ANY)],
        out_specs=tuple([HBM_SPEC] * n),
        input_output_aliases={ns + i: i for i in range(n)},
        compiler_params=pltpu.CompilerParams(has_side_effects=pltpu.SideEffectType.DATAFLOW_SIDE_EFFECTING),
    )(*[_in_hbm(a) for a in (srcs or [])], *lands, send_sems, recv_sems, after)
    return list(out)


def _plan_forward(n):
    def plan(x, y, c):
        return [(a, _flat((*ch, c)), ("lead", _flat((*ch, c))), (x, y, 1 - c), ("lead", _flat((*ch, 1 - c))))
                for a in range(n) for ch in _chip_peers(x, y)]
    return plan


def _place_own(zones, owns, slot, name):
    n = len(zones)

    def body(slot_ref, *refs):
        for a in range(n):
            refs[2 * n + a][...] = refs[a][...]

    return pl.pallas_call(
        body, name=name,
        grid_spec=pltpu.PrefetchScalarGridSpec(
            num_scalar_prefetch=1, grid=(1,),
            in_specs=[pl.BlockSpec(o.shape, lambda i, s_ref: (0, 0)) for o in owns] + [pl.BlockSpec(memory_space=pl.ANY)] * n,
            out_specs=[pl.BlockSpec((None,) + o.shape, lambda i, s_ref: (s_ref[0], 0, 0)) for o in owns]),
        out_shape=[jax.ShapeDtypeStruct(z.shape, z.dtype) for z in zones],
        input_output_aliases={1 + n + a: a for a in range(n)},
        compiler_params=_params(("arbitrary",)),
    )(slot, *owns, *zones)


def _pair_sum(parts, got, core, name):
    R, C = parts.shape[-2:]
    mine = (pl.BlockSpec((None, None, R, C), lambda j, c_ref: (c_ref[0], j, 0, 0)) if parts.ndim == 4
            else pl.BlockSpec((None, R, C), lambda j, c_ref: (2 * j + c_ref[0], 0, 0)))

    def body(c_ref, p_ref, g_ref, o_ref):
        o_ref[...] = (p_ref[...].astype(F32) + g_ref[...].astype(F32)).astype(o_ref.dtype)

    return pl.pallas_call(
        body, name=name,
        grid_spec=pltpu.PrefetchScalarGridSpec(
            num_scalar_prefetch=1, grid=(4,),
            in_specs=[mine, pl.BlockSpec((None, R, C), lambda j, c_ref: (j, 0, 0))],
            out_specs=pl.BlockSpec((None, R, C), lambda j, c_ref: (j, 0, 0))),
        out_shape=pltpu.HBM((4, R, C), parts.dtype),
        compiler_params=_params(("parallel",)),
    )(core, parts, got)


def _matmul(a, b, mode, M, N, K, tm, tn, tk, out_dtype, name, b_noff=0, out_blocked=None, dep=None, out_hbm=False):
    nm, nn, nk = M // tm, N // tn, K // tk
    if mode == "nn":
        a_spec = pl.BlockSpec((tm, tk), lambda j, i, k: (i, k))
        b_spec = pl.BlockSpec((tk, tn), lambda j, i, k: (k, j + b_noff))
        dn = NN
    elif mode == "nt":
        a_spec = pl.BlockSpec((tm, tk), lambda j, i, k: (i, k))
        b_spec = pl.BlockSpec((tn, tk), lambda j, i, k: (j + b_noff, k))
        dn = NT
    else:
        a_spec = pl.BlockSpec((tk, tm), lambda j, i, k: (k, i))
        b_spec = pl.BlockSpec((tk, tn), lambda j, i, k: (k, j + b_noff))
        dn = TN
    if out_blocked == "col":
        out_shape = jax.ShapeDtypeStruct((2, 4, M, tn), out_dtype)
        out_spec = pl.BlockSpec((None, None, tm, tn), lambda j, i, k: (j % 2, j // 2, i, 0))
    elif out_blocked == "row":
        out_shape = jax.ShapeDtypeStruct((2, 4, tm, N), out_dtype)
        out_spec = pl.BlockSpec((None, None, tm, tn), lambda j, i, k: (i % 2, i // 2, 0, j))
    else:
        out_shape = jax.ShapeDtypeStruct((M, N), out_dtype)
        out_spec = pl.BlockSpec((tm, tn), lambda j, i, k: (i, j))
    if out_hbm or out_blocked is not None:
        out_shape = pltpu.HBM(out_shape.shape, out_shape.dtype)

    n_extra = int(dep is not None)

    def body(a_ref, b_ref, *rest):
        o_ref, scratch = rest[n_extra], rest[n_extra + 1:]
        if nk == 1:
            o_ref[...] = _dot(a_ref[...], b_ref[...], dn).astype(out_dtype)
        else:
            acc_ref, = scratch
            k = pl.program_id(2)

            @pl.when(k == 0)
            def _():
                acc_ref[...] = jnp.zeros_like(acc_ref)

            acc_ref[...] += _dot(a_ref[...], b_ref[...], dn)

            @pl.when(k == nk - 1)
            def _():
                o_ref[...] = acc_ref[...].astype(out_dtype)

    return pl.pallas_call(
        body, name=name, grid=(nn, nm, nk),
        in_specs=[a_spec, b_spec] + ([] if dep is None else [pl.BlockSpec((8, 128), lambda j, i, k: (0, 0))]),
        out_specs=out_spec, out_shape=out_shape,
        scratch_shapes=[] if nk == 1 else [pltpu.VMEM((tm, tn), F32)],
        compiler_params=_params(("parallel", "parallel", "arbitrary")),
    )(a, b, *([] if dep is None else [dep]))


RMS_SIDE_ROWS = 512


def _rms_side(pairs):
    chunks = [min(x.shape[0], RMS_SIDE_ROWS) for x, _ in pairs]

    def work(ins, outs, scratch):
        sem = scratch[-1]

        def move(src, dst):
            cp = pltpu.make_async_copy(src, dst, sem.at[0])
            cp.start()
            cp.wait()

        for p, ((x, _), tr) in enumerate(zip(pairs, chunks)):
            x_ref, g_ref, h_ref = ins[2 * p], ins[2 * p + 1], outs[p]
            xv, gv, hv = scratch[3 * p:3 * p + 3]
            move(g_ref, gv)
            for i in range(x.shape[0] // tr):
                rows = pl.ds(i * tr, tr)
                move(x_ref.at[rows], xv)
                v = xv[...]
                hv[...] = (v * lax.rsqrt(jnp.mean(v * v, axis=-1, keepdims=True) + EPS) * gv[...]).astype(BF)
                move(hv, h_ref.at[rows])

    scratch = [s for (x, g), tr in zip(pairs, chunks)
               for s in (pltpu.VMEM((tr, x.shape[1]), F32), pltpu.VMEM(g.shape, F32), pltpu.VMEM((tr, x.shape[1]), BF))]
    return (work, [a for pair in pairs for a in pair], [jax.ShapeDtypeStruct(x.shape, BF) for x, _ in pairs],
            scratch + [pltpu.SemaphoreType.DMA((1,))])


def _rms_gain_grad(dn, x, name):
    R, Dm = x.shape

    def body(dn_ref, x_ref, o_ref):
        xv = x_ref[...]
        r = lax.rsqrt(jnp.mean(xv * xv, axis=-1, keepdims=True) + EPS)
        o_ref[...] = jnp.sum(dn_ref[...] * xv * r, axis=0, keepdims=True)

    return pl.pallas_call(
        body, name=name, out_shape=jax.ShapeDtypeStruct((1, Dm), F32),
        compiler_params=_params(),
    )(dn, x)


def _shift_down(v, k, head8, row, T):
    if k == 0:
        return v
    r = pltpu.roll(v, k, 0)
    hr = pltpu.roll(head8, k, 0)
    top = jnp.where(row[:8] < k, hr, r[:8])
    return jnp.concatenate([top, r[8:]], axis=0)


def _shift_up(v, k, tail8, row, T):
    if k == 0:
        return v
    r = pltpu.roll(v, T - k, 0)
    tr = pltpu.roll(tail8, 8 - k, 0)
    bot = jnp.where(row[:8] >= 8 - k, tr, r[T - 8:])
    return jnp.concatenate([r[:T - 8], bot], axis=0)


def _rglru_gates(u, head8, grow, row, T, cw_ref, cb_ref, wa_ref, ba_ref, wx_ref, bx_ref, lam_ref):
    us = [_shift_down(u, k, head8, row, T) for k in range(CONV_W)]
    acc = us[0] * cw_ref[0:1, :]
    for k in range(1, CONV_W):
        acc = acc + us[k] * cw_ref[k:k + 1, :]
    conv = cb_ref[...] + acc
    cbf = conv.astype(BF)
    r_ = _sigmoid(_dot(cbf, wa_ref[0], NN) + ba_ref[...])
    i_ = _sigmoid(_dot(cbf, wx_ref[0], NN) + bx_ref[...])
    sp = _softplus(-lam_ref[...])
    la = -LRU_C * r_ * sp
    a = jnp.exp(la)
    mult_raw = jnp.sqrt(-_expm1(2.0 * la))
    mult = jnp.where(grow == 0, 1.0, mult_raw)
    return us, conv, cbf, r_, i_, sp, a, mult_raw, mult


def _rglru_specs(T, nt, rev):
    tmap = (lambda n, t: (nt - 1 - t, n)) if rev else (lambda n, t: (t, n))
    hmap = ((lambda n, t: (jnp.maximum((nt - 1 - t) * (T // 8) - 1, 0), n)) if rev
            else (lambda n, t: (jnp.maximum(t * (T // 8) - 1, 0), n)))
    tile = pl.BlockSpec((T, RNN_BLOCK), tmap)
    halo = pl.BlockSpec((8, RNN_BLOCK), hmap)
    vec = pl.BlockSpec((1, RNN_BLOCK), lambda n, t: (0, n))
    cw = pl.BlockSpec((CONV_W, RNN_BLOCK), lambda n, t: (0, n))
    wblk = pl.BlockSpec((1, RNN_BLOCK, RNN_BLOCK), lambda n, t: (n, 0, 0))
    return tile, halo, vec, cw, wblk


def _rglru_fwd(xr, g, cw, cb, wa, ba, wx, bx, lam, T):
    S = xr.shape[0]
    nt = S // T

    def body(u_ref, uh_ref, g_ref, cw_ref, cb_ref, wa_ref, ba_ref, wx_ref, bx_ref, lam_ref, h_ref, y_ref, carry):
        t = pl.program_id(1)

        @pl.when(t == 0)
        def _():
            carry[...] = jnp.zeros_like(carry)

        row = lax.broadcasted_iota(jnp.int32, (T, RNN_BLOCK), 0)
        grow = row + t * T
        head8 = jnp.where(t > 0, uh_ref[...], 0.0)
        _, conv, _, _, i_, _, a, _, mult = _rglru_gates(u_ref[...], head8, grow, row, T, cw_ref, cb_ref, wa_ref, ba_ref,
                                                         wx_ref, bx_ref, lam_ref)
        b = mult * i_ * conv
        s = 1
        while s < T:
            keep = row >= s
            a_s = jnp.where(keep, pltpu.roll(a, s, 0), 1.0)
            b_s = jnp.where(keep, pltpu.roll(b, s, 0), 0.0)
            b = a * b_s + b
            a = a * a_s
            s *= 2
        h = b + a * carry[0:1, :]
        carry[...] = jnp.broadcast_to(h[T - 1:T, :], carry.shape)
        h_ref[...] = h
        gv = g_ref[...]
        y_ref[...] = (h * (gv * _sigmoid(gv))).astype(BF)

    tile, halo, vec, cwspec, wblk = _rglru_specs(T, nt, False)
    return pl.pallas_call(
        body, name="rglru_fwd", grid=(RNN_BLOCKS, nt),
        in_specs=[tile, halo, tile, cwspec, vec, wblk, vec, wblk, vec, vec],
        out_specs=[tile, tile],
        out_shape=[jax.ShapeDtypeStruct((S, D_RNN), F32), jax.ShapeDtypeStruct((S, D_RNN), BF)],
        scratch_shapes=[pltpu.VMEM((8, RNN_BLOCK), F32)],
        compiler_params=_params(("parallel", "arbitrary")),
    )(xr, xr, g, cw, cb, wa, ba, wx, bx, lam)


def _rglru_bwd(xr, g, h, dy, cw, cb, wa, ba, wx, bx, lam, T):
    S = xr.shape[0]
    nt = S // T

    def body(u_ref, uh_ref, g_ref, h_ref, hh_ref, dy_ref, cw_ref, cb_ref, wa_ref, ba_ref, wx_ref, bx_ref, lam_ref,
             du_ref, dg_ref, dwa_ref, dwx_ref, dvec_ref, c_dhh, c_a, c_dconv):
        t = pl.program_id(1)
        tt = nt - 1 - t

        @pl.when(t == 0)
        def _():
            c_dhh[...] = jnp.zeros_like(c_dhh)
            c_a[...] = jnp.zeros_like(c_a)
            c_dconv[...] = jnp.zeros_like(c_dconv)
            dwa_ref[...] = jnp.zeros_like(dwa_ref)
            dwx_ref[...] = jnp.zeros_like(dwx_ref)
            dvec_ref[...] = jnp.zeros_like(dvec_ref)

        row = lax.broadcasted_iota(jnp.int32, (T, RNN_BLOCK), 0)
        row8 = row[:8]
        grow = row + tt * T
        head8 = jnp.where(tt > 0, uh_ref[...], 0.0)
        us, conv, cbf, r_, i_, sp, a, mult_raw, mult = _rglru_gates(
            u_ref[...], head8, grow, row, T, cw_ref, cb_ref, wa_ref, ba_ref, wx_ref, bx_ref, lam_ref)
        hv = h_ref[...]
        hprev = _shift_down(hv, 1, jnp.where(tt > 0, hh_ref[...], 0.0), row, T)
        gv = g_ref[...]
        sg = _sigmoid(gv)
        dyv = dy_ref[...]
        dg_ref[...] = (dyv * hv * (sg * (1.0 + gv * (1.0 - sg)))).astype(BF)
        d = dyv * (gv * sg)
        A = _shift_up(a, 1, c_a[...], row, T)
        s = 1
        while s < T:
            keep = row < T - s
            A_s = jnp.where(keep, pltpu.roll(A, T - s, 0), 1.0)
            d_s = jnp.where(keep, pltpu.roll(d, T - s, 0), 0.0)
            d = A * d_s + d
            A = A * A_s
            s *= 2
        dhh = d + A * c_dhh[0:1, :]
        da = dhh * hprev
        dconv = dhh * mult * i_
        di = dhh * mult * conv
        dmult = dhh * i_ * conv
        dla = da * a - jnp.where(grow == 0, 0.0, dmult * (a * a) / mult_raw)
        dr = dla * (-LRU_C * sp)
        dsp = jnp.sum(dla * (-LRU_C * r_), axis=0, keepdims=True)
        dza = dr * r_ * (1.0 - r_)
        dzx = di * i_ * (1.0 - i_)
        dza_b, dzx_b = dza.astype(BF), dzx.astype(BF)
        dconv = dconv + _dot(dza_b, wa_ref[0], NT) + _dot(dzx_b, wx_ref[0], NT)
        dwa_ref[0] += _dot(cbf, dza_b, TN)
        dwx_ref[0] += _dot(cbf, dzx_b, TN)
        lam = lam_ref[...]
        rows = [jnp.sum(dconv * us[k], axis=0, keepdims=True) for k in range(CONV_W)]
        rows += [jnp.sum(dconv, axis=0, keepdims=True), jnp.sum(dza, axis=0, keepdims=True),
                 jnp.sum(dzx, axis=0, keepdims=True), dsp * (-_sigmoid(-lam))]
        upd = jnp.zeros((8, RNN_BLOCK), F32)
        for j, rv in enumerate(rows):
            upd = upd + jnp.where(row8 == j, rv, 0.0)
        dvec_ref[...] += upd
        tail8 = c_dconv[...]
        du = dconv * cw_ref[0:1, :]
        for k in range(1, CONV_W):
            du = du + _shift_up(dconv, k, tail8, row, T) * cw_ref[k:k + 1, :]
        du_ref[...] = du.astype(BF)
        c_dhh[...] = jnp.broadcast_to(dhh[0:1, :], c_dhh.shape)
        c_a[...] = jnp.broadcast_to(a[0:1, :], c_a.shape)
        c_dconv[...] = dconv[:8]

    tile, halo, vec, cwspec, wblk = _rglru_specs(T, nt, True)
    acc8 = pl.BlockSpec((8, RNN_BLOCK), lambda n, t: (0, n))
    return pl.pallas_call(
        body, name="rglru_bwd", grid=(RNN_BLOCKS, nt),
        in_specs=[tile, halo, tile, tile, halo, tile, cwspec, vec, wblk, vec, wblk, vec, vec],
        out_specs=[tile, tile, wblk, wblk, acc8],
        out_shape=[jax.ShapeDtypeStruct((S, D_RNN), BF), jax.ShapeDtypeStruct((S, D_RNN), BF),
                   jax.ShapeDtypeStruct((RNN_BLOCKS, RNN_BLOCK, RNN_BLOCK), F32),
                   jax.ShapeDtypeStruct((RNN_BLOCKS, RNN_BLOCK, RNN_BLOCK), F32),
                   jax.ShapeDtypeStruct((8, D_RNN), F32)],
        scratch_shapes=[pltpu.VMEM((8, RNN_BLOCK), F32)] * 3,
        compiler_params=_params(("parallel", "arbitrary")),
    )(xr, xr, g, h, h, dy, cw, cb, wa, ba, wx, bx, lam)


def _rel_bucket_map():
    qi = np.arange(WINDOW)[:, None]
    kj = np.arange(2 * WINDOW)[None, :]
    dist = jnp.asarray(qi + WINDOW - kj, jnp.int32)
    n = jnp.maximum(dist, 0)
    max_exact = REL_BUCKETS // 2
    ratio = jnp.log(jnp.maximum(n, 1).astype(F32) / max_exact) / math.log(REL_MAX_DIST / max_exact)
    large = jnp.minimum(max_exact + (ratio * (REL_BUCKETS - max_exact)).astype(jnp.int32), REL_BUCKETS - 1)
    bucket = jnp.where(n < max_exact, n, large).astype(jnp.int32)
    j = np.arange(WINDOW)[None, :]
    return jnp.where(jnp.asarray(j > qi), bucket[:, :WINDOW], bucket[:, WINDOW:])


def _swa_common(n, kv_ref, bucket_ref, relb_ref, bias_scr):
    @pl.when(n == 0)
    def _():
        bk = bucket_ref[...]
        for h in range(SWA_HEADS):
            acc = jnp.zeros((WINDOW, WINDOW), F32)
            for b in range(REL_BUCKETS):
                acc = acc + jnp.where(bk == b, relb_ref[b, h], 0.0)
            bias_scr[h] = acc

    prev0 = pl.multiple_of(jnp.maximum(n - 1, 0) * WINDOW, WINDOW)
    cur0 = pl.multiple_of(n * WINDOW, WINDOW)
    kk = jnp.concatenate([kv_ref[pl.ds(prev0, WINDOW), :], kv_ref[pl.ds(cur0, WINDOW), :]], axis=0).astype(F32)
    rowi = lax.broadcasted_iota(jnp.int32, (WINDOW, WINDOW), 0)
    col = lax.broadcasted_iota(jnp.int32, (WINDOW, WINDOW), 1)
    from_prev = col > rowi
    return kk, from_prev, prev0, cur0


def _fold(full, from_prev):
    return jnp.where(from_prev, full[:, :WINDOW], full[:, WINDOW:])


def _unfold(sq, from_prev):
    return jnp.concatenate([jnp.where(from_prev, sq, 0.0), jnp.where(from_prev, 0.0, sq)], axis=1)


def _half_pair(part, kvh):
    lo = lax.broadcasted_iota(jnp.int32, part.shape, 1) < SWA_HD
    if kvh == 0:
        pa = jnp.where(lo, part, 0.0)
        pb = pltpu.roll(pa, SWA_HD, 1)
    else:
        pb = jnp.where(lo, 0.0, part)
        pa = pltpu.roll(pb, SWA_HD, 1)
    return pa.astype(BF), pb.astype(BF)


ALL_HEADS = SWA_HEADS * WINDOW


def _sink_column(sinks):
    return jnp.repeat(sinks.reshape(SWA_HEADS), WINDOW).reshape(ALL_HEADS, 1)


def _swa_operands(kk):
    return [(_half_pair(kk[:, :128], kvh), _half_pair(kk[:, 128:], kvh)) for kvh in range(SWA_KV_HEADS)]


def _swa_probs(n, q_ref, ops, bias_scr, sinkc_ref, from_prev):
    lgs = []
    for kvh in range(SWA_KV_HEADS):
        (ka, kb), _ = ops[kvh]
        for p in range(4):
            q2 = q_ref[:, kvh * 512 + p * 128:kvh * 512 + p * 128 + 128]
            lgs += [_fold(_dot(q2, ka, NT), from_prev), _fold(_dot(q2, kb, NT), from_prev)]
    lg = jnp.concatenate(lgs, axis=0) * (SWA_HD ** -0.5) + bias_scr[...].reshape(ALL_HEADS, WINDOW)
    rowi = jnp.bitwise_and(lax.broadcasted_iota(jnp.int32, (ALL_HEADS, WINDOW), 0), WINDOW - 1)
    col = lax.broadcasted_iota(jnp.int32, (ALL_HEADS, WINDOW), 1)
    no_prev = jnp.where(n > 0, 0, 4 * WINDOW)
    lg = jnp.where(jnp.logical_or(col <= rowi, col > rowi + no_prev), lg, NEG_INF)
    sink = sinkc_ref[...]
    m = jnp.maximum(jnp.max(lg, axis=-1, keepdims=True), sink)
    e = jnp.exp(lg - m)
    es = jnp.exp(sink - m)
    den = jnp.sum(e, axis=-1, keepdims=True) + es
    return e / den, es / den


def _swa_fwd(q, kv, g, bucket, rel_bias, sink_col):
    S = q.shape[0]
    nb = S // WINDOW

    def body(q_ref, kv_ref, g_ref, bucket_ref, relb_ref, sinkc_ref, o_ref, y_ref, bias_scr):
        n = pl.program_id(0)
        kk, from_prev, _, _ = _swa_common(n, kv_ref, bucket_ref, relb_ref, bias_scr)
        ops = _swa_operands(kk)
        pr, _ = _swa_probs(n, q_ref, ops, bias_scr, sinkc_ref, from_prev)
        for kvh in range(SWA_KV_HEADS):
            _, (va, vb) = ops[kvh]
            for p in range(4):
                c0 = kvh * 512 + p * 128
                r0 = (kvh * 8 + 2 * p) * WINDOW
                o2 = (_dot(_unfold(pr[r0:r0 + WINDOW], from_prev).astype(BF), va, NN)
                      + _dot(_unfold(pr[r0 + WINDOW:r0 + 2 * WINDOW], from_prev).astype(BF), vb, NN))
                o_ref[:, c0:c0 + 128] = o2
                gv = g_ref[:, c0:c0 + 128]
                y_ref[:, c0:c0 + 128] = (o2 * (gv * _sigmoid(gv))).astype(BF)

    blk = pl.BlockSpec((WINDOW, 1024), lambda n: (n, 0))
    smem = pl.BlockSpec(memory_space=pltpu.SMEM)
    sinkc = pl.BlockSpec((ALL_HEADS, 1), lambda n: (0, 0))
    return pl.pallas_call(
        body, name="swa_fwd", grid=(nb,),
        in_specs=[blk, pl.BlockSpec((S, 256), lambda n: (0, 0)), blk, pl.BlockSpec((WINDOW, WINDOW), lambda n: (0, 0)), smem, sinkc],
        out_specs=[blk, blk],
        out_shape=[jax.ShapeDtypeStruct((S, 1024), F32), jax.ShapeDtypeStruct((S, 1024), BF)],
        scratch_shapes=[pltpu.VMEM((SWA_HEADS, WINDOW, WINDOW), F32)],
        compiler_params=_params(("arbitrary",)),
    )(q, kv, g, bucket, rel_bias, sink_col)


def _swa_bwd(q, kv, g, o, dy, bucket, rel_bias, sink_col, others):
    S = q.shape[0]
    nb = S // WINDOW
    first_col = {name: c0 for name, c0, _, _ in SEGMENTS}
    col_q, col_g = first_col["q_s"], first_col["g_swa"]
    copies = []
    for name, c0, width, _ in SEGMENTS:
        if name not in ("q_s", "kv", "g_swa"):
            arrs = others[name] if name == "gl" else (others[name],)
            copies += [(a, c0 + i * (width // len(arrs))) for i, a in enumerate(arrs)]

    def body(q_ref, kv_ref, g_ref, o_ref, dy_ref, bucket_ref, relb_ref, sinkc_ref, *refs):
        copy_refs = refs[:len(copies)]
        dp_ref, dkv_ref, dsink_ref, drel_ref, bias_scr, dbias_scr, dsink_scr = refs[len(copies):]
        n = pl.program_id(0)
        for c_ref, (a, c0) in zip(copy_refs, copies):
            dp_ref[:, c0:c0 + a.shape[1]] = c_ref[...]

        @pl.when(n == 0)
        def _():
            dbias_scr[...] = jnp.zeros_like(dbias_scr)
            dsink_scr[...] = jnp.zeros_like(dsink_scr)
            dkv_ref[...] = jnp.zeros_like(dkv_ref)

        kk, from_prev, prev0, cur0 = _swa_common(n, kv_ref, bucket_ref, relb_ref, bias_scr)
        ops = _swa_operands(kk)
        pr, ps = _swa_probs(n, q_ref, ops, bias_scr, sinkc_ref, from_prev)
        do2s, dps = [], []
        for kvh in range(SWA_KV_HEADS):
            _, (va, vb) = ops[kvh]
            for p in range(4):
                c0 = kvh * 512 + p * 128
                gv = g_ref[:, c0:c0 + 128]
                sg = _sigmoid(gv)
                dyv = dy_ref[:, c0:c0 + 128]
                dp_ref[:, col_g + c0:col_g + c0 + 128] = (dyv * o_ref[:, c0:c0 + 128] * (sg * (1.0 + gv * (1.0 - sg)))).astype(BF)
                do2 = (dyv * (gv * sg)).astype(BF)
                do2s.append(do2)
                dps += [_fold(_dot(do2, va, NT), from_prev), _fold(_dot(do2, vb, NT), from_prev)]
        dp = jnp.concatenate(dps, axis=0)
        delta = jnp.sum(pr * dp, axis=-1, keepdims=True)
        ds = pr * (dp - delta)
        dbias_scr[...] += ds.reshape(SWA_HEADS, WINDOW, WINDOW)
        dsink_scr[...] += ps * delta
        dsc = ds * (SWA_HD ** -0.5)
        lo256 = lax.broadcasted_iota(jnp.int32, (2 * WINDOW, 128), 1) < SWA_HD
        dks, dvs = [], []
        for kvh in range(SWA_KV_HEADS):
            (ka, kb), _ = ops[kvh]
            dka = jnp.zeros((2 * WINDOW, 128), F32)
            dkb, dva, dvb = dka, dka, dka
            for p in range(4):
                c0 = kvh * 512 + p * 128
                r0 = (kvh * 8 + 2 * p) * WINDOW
                q2 = q_ref[:, c0:c0 + 128]
                do2 = do2s[kvh * 4 + p]
                ds0 = _unfold(dsc[r0:r0 + WINDOW], from_prev).astype(BF)
                ds1 = _unfold(dsc[r0 + WINDOW:r0 + 2 * WINDOW], from_prev).astype(BF)
                dp_ref[:, col_q + c0:col_q + c0 + 128] = (_dot(ds0, ka, NN) + _dot(ds1, kb, NN)).astype(BF)
                dka = dka + _dot(ds0, q2, TN)
                dkb = dkb + _dot(ds1, q2, TN)
                dva = dva + _dot(_unfold(pr[r0:r0 + WINDOW], from_prev).astype(BF), do2, TN)
                dvb = dvb + _dot(_unfold(pr[r0 + WINDOW:r0 + 2 * WINDOW], from_prev).astype(BF), do2, TN)
            dks.append(jnp.where(lo256, dka, 0.0) + pltpu.roll(jnp.where(lo256, 0.0, dkb), SWA_HD, 1))
            dvs.append(jnp.where(lo256, dva, 0.0) + pltpu.roll(jnp.where(lo256, 0.0, dvb), SWA_HD, 1))
        dk = dks[0] + pltpu.roll(dks[1], SWA_HD, 1)
        dv = dvs[0] + pltpu.roll(dvs[1], SWA_HD, 1)
        dkv_ref[pl.ds(prev0, WINDOW), 0:128] += dk[:WINDOW]
        dkv_ref[pl.ds(prev0, WINDOW), 128:256] += dv[:WINDOW]
        dkv_ref[pl.ds(cur0, WINDOW), 0:128] += dk[WINDOW:]
        dkv_ref[pl.ds(cur0, WINDOW), 128:256] += dv[WINDOW:]

        @pl.when(n == nb - 1)
        def _():
            dsink_ref[...] = -jnp.sum(dsink_scr[...].reshape(SWA_HEADS, WINDOW, 1), axis=1)
            bk = bucket_ref[...]
            sums = []
            for b in range(REL_BUCKETS):
                sums.append(jnp.sum(jnp.where((bk == b)[None], dbias_scr[...], 0.0), axis=1))
            drel_ref[...] = jnp.sum(jnp.concatenate(sums, axis=0), axis=1, keepdims=True)

    blk = pl.BlockSpec((WINDOW, 1024), lambda n: (n, 0))
    smem = pl.BlockSpec(memory_space=pltpu.SMEM)
    whole = lambda shape: pl.BlockSpec(shape, lambda n: (0, 0))
    return pl.pallas_call(
        body, name="swa_bwd", grid=(nb,),
        in_specs=[blk, whole((S, 256)), blk, blk, blk, whole((WINDOW, WINDOW)), smem, whole((ALL_HEADS, 1))]
        + [pl.BlockSpec((WINDOW, a.shape[1]), lambda n: (n, 0)) for a, _ in copies],
        out_specs=[pl.BlockSpec((WINDOW, D_IN), lambda n: (n, 0)), whole((S, 256)), whole((SWA_HEADS, 1)),
                   whole((REL_BUCKETS * SWA_HEADS, 1))],
        out_shape=[jax.ShapeDtypeStruct((S, D_IN), BF), jax.ShapeDtypeStruct((S, 256), F32), jax.ShapeDtypeStruct((SWA_HEADS, 1), F32),
                   jax.ShapeDtypeStruct((REL_BUCKETS * SWA_HEADS, 1), F32)],
        scratch_shapes=[pltpu.VMEM((SWA_HEADS, WINDOW, WINDOW), F32), pltpu.VMEM((SWA_HEADS, WINDOW, WINDOW), F32),
                        pltpu.VMEM((ALL_HEADS, 1), F32)],
        compiler_params=_params(("arbitrary",)),
    )(q, kv, g, o, dy, bucket, rel_bias, sink_col, *[a for a, _ in copies])


def _mem_probs(qh, mk):
    lg = _dot(qh, mk, NT) * (MEM_HD ** -0.5)
    e = jnp.exp(lg - jnp.max(lg, axis=-1, keepdims=True))
    return e / jnp.sum(e, axis=-1, keepdims=True)


def _mem_fwd(q, mkv, g):
    S = q.shape[0]
    M = mkv.shape[0]
    tq = 256

    def body(q_ref, mkv_ref, g_ref, o_ref, y_ref):
        for h in range(MEM_HEADS):
            c0 = h * MEM_HD
            pr = _mem_probs(q_ref[:, c0:c0 + MEM_HD], mkv_ref[:, c0:c0 + MEM_HD])
            o = _dot(pr.astype(BF), mkv_ref[:, D_MEM + c0:D_MEM + c0 + MEM_HD], NN)
            o_ref[:, c0:c0 + MEM_HD] = o
            gv = g_ref[:, c0:c0 + MEM_HD]
            y_ref[:, c0:c0 + MEM_HD] = (o * (gv * _sigmoid(gv))).astype(BF)

    blk = pl.BlockSpec((tq, D_MEM), lambda i: (i, 0))
    return pl.pallas_call(
        body, name="mem_fwd", grid=(S // tq,),
        in_specs=[blk, pl.BlockSpec((M, 2 * D_MEM), lambda i: (0, 0)), blk], out_specs=[blk, blk],
        out_shape=[jax.ShapeDtypeStruct((S, D_MEM), F32), jax.ShapeDtypeStruct((S, D_MEM), BF)],
        compiler_params=_params(("parallel",)),
    )(q, mkv, g)


def _mem_bwd(q, mkv, g, o, dy):
    S = q.shape[0]
    M = mkv.shape[0]
    tq = 256

    def body(q_ref, mkv_ref, g_ref, o_ref, dy_ref, dq_ref, dg_ref, dmkv_ref):
        @pl.when(pl.program_id(0) == 0)
        def _():
            dmkv_ref[...] = jnp.zeros_like(dmkv_ref)

        for h in range(MEM_HEADS):
            c0 = h * MEM_HD
            qh = q_ref[:, c0:c0 + MEM_HD]
            mk = mkv_ref[:, c0:c0 + MEM_HD]
            mv = mkv_ref[:, D_MEM + c0:D_MEM + c0 + MEM_HD]
            gv = g_ref[:, c0:c0 + MEM_HD]
            sg = _sigmoid(gv)
            dyv = dy_ref[:, c0:c0 + MEM_HD]
            dg_ref[:, c0:c0 + MEM_HD] = (dyv * o_ref[:, c0:c0 + MEM_HD] * (sg * (1.0 + gv * (1.0 - sg)))).astype(BF)
            do = (dyv * (gv * sg)).astype(BF)
            pr = _mem_probs(qh, mk)
            dp = _dot(do, mv, NT)
            ds = pr * (dp - jnp.sum(pr * dp, axis=-1, keepdims=True))
            dsb = (ds * (MEM_HD ** -0.5)).astype(BF)
            dq_ref[:, c0:c0 + MEM_HD] = _dot(dsb, mk, NN).astype(BF)
            dmkv_ref[:, c0:c0 + MEM_HD] += _dot(dsb, qh, TN)
            dmkv_ref[:, D_MEM + c0:D_MEM + c0 + MEM_HD] += _dot(pr.astype(BF), do, TN)

    blk = pl.BlockSpec((tq, D_MEM), lambda i: (i, 0))
    whole = pl.BlockSpec((M, 2 * D_MEM), lambda i: (0, 0))
    return pl.pallas_call(
        body, name="mem_bwd", grid=(S // tq,),
        in_specs=[blk, whole, blk, blk, blk], out_specs=[blk, blk, whole],
        out_shape=[jax.ShapeDtypeStruct((S, D_MEM), BF), jax.ShapeDtypeStruct((S, D_MEM), BF),
                   jax.ShapeDtypeStruct((M, 2 * D_MEM), F32)],
        compiler_params=_params(("arbitrary",)),
    )(q, mkv, g, o, dy)


MERGE_TN = 512


def _merge_specs(tm):
    ytile = pl.BlockSpec((tm, 1024), lambda i, j: (i, 0))
    wblk = pl.BlockSpec((MERGE_TN, 1024), lambda i, j: (j, 0))
    gls = [pl.BlockSpec((None, tm, MERGE_TN), (lambda i, j, br=br: (br, i, j))) for br in range(3)]
    otile = pl.BlockSpec((tm, MERGE_TN), lambda i, j: (i, j))
    return ytile, wblk, gls, otile


def _merge_fwd(ys, ws, gl, tm):
    S = gl.shape[1]

    def body(y0, y1, y2, w0, w1, w2, g0, g1, g2, o_ref):
        acc = None
        for y_ref, w_ref, g_ref in ((y0, w0, g0), (y1, w1, g1), (y2, w2, g2)):
            term = _sigmoid(g_ref[...]) * _dot(y_ref[...], w_ref[...], NT)
            acc = term if acc is None else acc + term
        o_ref[...] = acc.astype(BF)

    ytile, wblk, gls, otile = _merge_specs(tm)
    return pl.pallas_call(
        body, name="merge_fwd", grid=(S // tm, D_MODEL // MERGE_TN),
        in_specs=[ytile] * 3 + [wblk] * 3 + gls, out_specs=otile,
        out_shape=jax.ShapeDtypeStruct((S, D_MODEL), BF),
        compiler_params=_params(("parallel", "arbitrary")),
    )(*ys, *ws, gl, gl, gl)


def _merge_bwd(dout, w_out, ys, ws, gl, tm):
    S = gl.shape[1]

    def body(do_ref, wo_ref, y0, y1, y2, w0, w1, w2, g0, g1, g2, dg0, dg1, dg2, dp0, dp1, dp2):
        dm = _dot(do_ref[...], wo_ref[...], NT)
        for y_ref, w_ref, g_ref, dg_ref, dp_ref in ((y0, w0, g0, dg0, dp0), (y1, w1, g1, dg1, dp1), (y2, w2, g2, dg2, dp2)):
            gate = _sigmoid(g_ref[...])
            pv = _dot(y_ref[...], w_ref[...], NT)
            dg_ref[...] = (dm * pv * gate * (1.0 - gate)).astype(BF)
            dp_ref[...] = (dm * gate).astype(BF)

    ytile, wblk, gls, otile = _merge_specs(tm)
    out = jax.ShapeDtypeStruct((S, D_MODEL), BF)
    return pl.pallas_call(
        body, name="merge_bwd", grid=(S // tm, D_MODEL // MERGE_TN),
        in_specs=[pl.BlockSpec((tm, D_MODEL), lambda i, j: (i, 0)), pl.BlockSpec((MERGE_TN, D_MODEL), lambda i, j: (j, 0))]
        + [ytile] * 3 + [wblk] * 3 + gls,
        out_specs=[otile] * 6, out_shape=[out] * 6,
        compiler_params=_params(("parallel", "arbitrary")),
    )(dout, w_out, *ys, *ws, gl, gl, gl)


def _out_loss(merged, w_out, x, target, post_g, tm):
    S = x.shape[0]

    def body(m_ref, w_ref, x_ref, t_ref, g_ref, dout_ref, dy_ref, loss_ref, dpost_ref):
        @pl.when(pl.program_id(0) == 0)
        def _():
            loss_ref[...] = jnp.zeros_like(loss_ref)
            dpost_ref[...] = jnp.zeros_like(dpost_ref)

        out = _dot(m_ref[...], w_ref[...], NN)
        r = lax.rsqrt(jnp.mean(out * out, axis=-1, keepdims=True) + EPS)
        nrm = out * r
        gv = g_ref[...]
        err = (x_ref[...] + nrm * gv) - t_ref[...]
        sq = jnp.sum(jnp.sum(err * err, axis=1, keepdims=True), axis=0, keepdims=True)
        loss_ref[...] += sq * (0.5 / D_MODEL)
        dy = err * (1.0 / D_MODEL)
        dy_ref[...] = dy
        dpost_ref[...] += jnp.sum(dy * nrm, axis=0, keepdims=True)
        dn = dy * gv
        dout_ref[...] = (r * (dn - nrm * jnp.mean(dn * nrm, axis=-1, keepdims=True))).astype(BF)

    row = pl.BlockSpec((tm, D_MODEL), lambda i: (i, 0))
    return pl.pallas_call(
        body, name="out_loss", grid=(S // tm,),
        in_specs=[row, pl.BlockSpec((D_MODEL, D_MODEL), lambda i: (0, 0)), row, row, pl.BlockSpec((1, D_MODEL), lambda i: (0, 0))],
        out_specs=[row, row, pl.BlockSpec((8, 128), lambda i: (0, 0)), pl.BlockSpec((1, D_MODEL), lambda i: (0, 0))],
        out_shape=[jax.ShapeDtypeStruct((S, D_MODEL), BF), jax.ShapeDtypeStruct((S, D_MODEL), F32),
                   jax.ShapeDtypeStruct((8, 128), F32), jax.ShapeDtypeStruct((1, D_MODEL), F32)],
        compiler_params=_params(("arbitrary",)),
    )(merged, w_out, x, target, post_g)


DH_DX_CHUNK = 64


def _dh_dx(dproj, w_in, x, dy, pre_g, tm, tk):
    S = x.shape[0]
    nk = D_IN // tk

    def body(dp_ref, w_ref, x_ref, dy_ref, g_ref, dx_ref, dpre_ref, acc_ref):
        i, k = pl.program_id(0), pl.program_id(1)

        @pl.when(jnp.logical_and(i == 0, k == 0))
        def _():
            dpre_ref[...] = jnp.zeros_like(dpre_ref)

        @pl.when(k == 0)
        def _():
            acc_ref[...] = jnp.zeros_like(acc_ref)

        acc_ref[...] += _dot(dp_ref[...], w_ref[...], NN)

        @pl.when(k == nk - 1)
        def _():
            def chunk(c, carry):
                rows = pl.ds(pl.multiple_of(c * DH_DX_CHUNK, DH_DX_CHUNK), DH_DX_CHUNK)
                dh = acc_ref[rows, :]
                xv = x_ref[rows, :]
                r = lax.rsqrt(jnp.mean(xv * xv, axis=-1, keepdims=True) + EPS)
                nrm = xv * r
                dpre_ref[...] += jnp.sum(dh * nrm, axis=0, keepdims=True)
                dn = dh * g_ref[...]
                dx_ref[rows, :] = r * (dn - nrm * jnp.mean(dn * nrm, axis=-1, keepdims=True)) + dy_ref[rows, :]
                return carry
            lax.fori_loop(0, tm // DH_DX_CHUNK, chunk, 0)

    row = pl.BlockSpec((tm, D_MODEL), lambda i, k: (i, 0))
    vec = pl.BlockSpec((1, D_MODEL), lambda i, k: (0, 0))
    return pl.pallas_call(
        body, name="dh_dx", grid=(S // tm, nk),
        in_specs=[pl.BlockSpec((tm, tk), lambda i, k: (i, k)), pl.BlockSpec((tk, D_MODEL), lambda i, k: (k, 0)), row, row, vec],
        out_specs=[row, vec],
        out_shape=[jax.ShapeDtypeStruct((S, D_MODEL), F32), jax.ShapeDtypeStruct((1, D_MODEL), F32)],
        scratch_shapes=[pltpu.VMEM((tm, D_MODEL), F32)],
        compiler_params=_params(("arbitrary", "arbitrary"), large=True),
    )(dproj, w_in, x, dy, pre_g)


def _sum_parts(parts, name):
    P, R, C = parts.shape
    tr = max(t for t in range(8, 513, 8) if R % t == 0)

    def body(p_ref, o_ref):
        acc = p_ref[0]
        for j in range(1, P):
            acc = acc + p_ref[j]
        o_ref[...] = acc

    return pl.pallas_call(
        body, name=name, grid=(R // tr,),
        in_specs=[pl.BlockSpec((P, tr, C), lambda i: (0, i, 0))], out_specs=pl.BlockSpec((tr, C), lambda i: (i, 0)),
        out_shape=jax.ShapeDtypeStruct((R, C), F32), compiler_params=_params(("parallel",)),
    )(parts)


def _adamw(land, sums, chip, w, m, v, name, group=(0, 1), into=None):
    q, n_groups = group
    _, R, cols = land.shape
    C = cols * n_groups
    tr = max(t for t in range(16, 257, 16) if R % t == 0)
    c1 = 1.0 - ADAM_B1 ** ADAM_STEP
    c2 = 1.0 - ADAM_B2 ** ADAM_STEP
    n_into = 0 if into is None else 4

    def body(chip_ref, p0_ref, p1_ref, p2_ref, own_ref, w_ref, m_ref, v_ref, *refs):
        g_ref, d_ref, nm_ref, nv_ref = refs[n_into:]
        g = own_ref[...].astype(F32)
        for p_ref in (p0_ref, p1_ref, p2_ref):
            g = g + p_ref[...].astype(F32)
        nm = ADAM_B1 * m_ref[...] + (1.0 - ADAM_B1) * g
        nv = ADAM_B2 * v_ref[...] + (1.0 - ADAM_B2) * (g * g)
        g_ref[...] = g
        nm_ref[...] = nm
        nv_ref[...] = nv
        d_ref[...] = -ADAM_LR * ((nm / c1) / (jnp.sqrt(nv / c2) + ADAM_EPS) + ADAM_WD * w_ref[...])

    tile = pl.BlockSpec((None, tr, cols), lambda i, c_ref: (0, i, q))
    specs = [pl.BlockSpec((None, tr, cols), (lambda i, c_ref, k=k: (k + (c_ref[0] <= k).astype(jnp.int32), i, 0))) for k in range(3)]
    specs.append(pl.BlockSpec((None, tr, cols), (lambda i, c_ref: (c_ref[0], i, 0))))
    return pl.pallas_call(
        body, name=name,
        grid_spec=pltpu.PrefetchScalarGridSpec(num_scalar_prefetch=1, grid=(R // tr,),
                                               in_specs=specs + [tile, tile, tile] + [pl.BlockSpec(memory_space=pl.ANY)] * n_into,
                                               out_specs=[tile] * 4),
        out_shape=[jax.ShapeDtypeStruct((1, R, C), F32)] * 4,
        input_output_aliases={8 + k: k for k in range(n_into)},
        compiler_params=_params(("parallel",)),
    )(chip, land, land, land, sums, w, m, v, *(into or []))


def _adamw_small(gs, ws, ms, vs):
    n = len(ws)
    c1 = 1.0 - ADAM_B1 ** ADAM_STEP
    c2 = 1.0 - ADAM_B2 ** ADAM_STEP

    def flat2(a):
        return a.reshape(-1, a.shape[-1])

    def body(*refs):
        ins, outs = refs[:4 * n], refs[4 * n:]
        for a in range(n):
            g, w, m, v = (ins[k * n + a][...] for k in range(4))
            nm = ADAM_B1 * m + (1.0 - ADAM_B1) * g
            nv = ADAM_B2 * v + (1.0 - ADAM_B2) * (g * g)
            outs[a][...] = g
            outs[n + a][...] = -ADAM_LR * ((nm / c1) / (jnp.sqrt(nv / c2) + ADAM_EPS) + ADAM_WD * w)
            outs[2 * n + a][...] = nm
            outs[3 * n + a][...] = nv

    shapes = [flat2(w).shape for w in ws]
    out = pl.pallas_call(
        body, name="adamw_small", out_shape=[jax.ShapeDtypeStruct(sh, F32) for sh in shapes] * 4,
        compiler_params=_params(),
    )(*[g.reshape(sh) for g, sh in zip(gs, shapes)], *[flat2(a) for a in (*ws, *ms, *vs)])
    return [[out[k * n + a].reshape(ws[a].shape) for a in range(n)] for k in range(4)]


PROJ_ROWS = 512


def _project(h, w_t, dep=None):
    S = h.shape[0]
    n_tiles = D_IN // SEG_TILE
    ranges = [(c0 // SEG_TILE, (c0 + width) // SEG_TILE) for _, c0, width, _ in SEGMENTS]
    dtypes = (F32, BF)
    n_extra = int(dep is not None)

    def of_dtype(j, dt):
        hit = False
        for (j0, j1), seg in zip(ranges, SEGMENTS):
            if seg[3] == dt:
                hit = jnp.logical_and(j >= j0, j < j1) | hit
        return hit

    def body(h_ref, w_ref, *refs):
        outs = refs[n_extra:n_extra + len(SEGMENTS)]
        stages, sems = refs[n_extra + len(SEGMENTS):-1], refs[-1]
        j = pl.program_id(0)
        slot = j % 2

        def copy_out(k, stage, dst):
            return pltpu.make_async_copy(stages[k].at[stage], dst, sems.at[stage])

        def wait_tile(jj, stage):
            for k, dt in enumerate(dtypes):
                o_ref = [o for o, seg in zip(outs, SEGMENTS) if seg[3] == dt and seg[0] != "gl"][0]
                @pl.when(of_dtype(jj, dt))
                def _(k=k, o_ref=o_ref):
                    copy_out(k, stage, o_ref.at[:, pl.ds(0, SEG_TILE)]).wait()

        @pl.when(j >= 2)
        def _():
            wait_tile(j - 2, slot)

        for k, dt in enumerate(dtypes):
            @pl.when(of_dtype(j, dt))
            def _(k=k, dt=dt):
                for c in range(S // PROJ_ROWS):
                    rows = pl.ds(c * PROJ_ROWS, PROJ_ROWS)
                    stages[k][slot, rows, :] = _dot(h_ref[rows, :], w_ref[...], NT).astype(dt)

        for (j0, j1), (name, _, _, dt), o_ref in zip(ranges, SEGMENTS, outs):
            @pl.when(jnp.logical_and(j >= j0, j < j1))
            def _(j0=j0, j1=j1, name=name, dt=dt, o_ref=o_ref):
                t = j - j0
                if name == "gl":
                    per = (j1 - j0) // 3
                    dst = o_ref.at[t // per, :, pl.ds(pl.multiple_of((t % per) * SEG_TILE, SEG_TILE), SEG_TILE)]
                else:
                    dst = o_ref.at[:, pl.ds(pl.multiple_of(t * SEG_TILE, SEG_TILE), SEG_TILE)]
                copy_out(dtypes.index(dt), slot, dst).start()

        @pl.when(j == n_tiles - 1)
        def _():
            wait_tile(j - 1, 1 - slot)
            wait_tile(j, slot)

    out_shapes = [jax.ShapeDtypeStruct((3, S, width // 3) if name == "gl" else (S, width), dt) for name, _, width, dt in SEGMENTS]
    outs = pl.pallas_call(
        body, name="proj", grid=(n_tiles,),
        in_specs=[pl.BlockSpec((S, D_MODEL), lambda j: (0, 0)), pl.BlockSpec((SEG_TILE, D_MODEL), lambda j: (j, 0))]
        + ([] if dep is None else [pl.BlockSpec((8, 128), lambda j: (0, 0))]),
        out_specs=[pl.BlockSpec(memory_space=pl.ANY)] * len(SEGMENTS), out_shape=out_shapes,
        scratch_shapes=[pltpu.VMEM((2, S, SEG_TILE), dt) for dt in dtypes] + [pltpu.SemaphoreType.DMA((2,))],
        compiler_params=_params(("arbitrary",), large=True),
    )(h, w_t, *([] if dep is None else [dep]))
    return {name: o for (name, _, _, _), o in zip(SEGMENTS, outs)}


def _forward_a(h, memn, w_in, sinks, rel_bias, dep=None):
    S = h.shape[0]
    st = dict(T=min(512, S // 2), tm=min(512, S), bucket=_rel_bucket_map(), h=h, memn=memn)
    seg = st["seg"] = _project(h, w_in, dep)
    st["o_swa"], st["y_swa"] = _swa_fwd(seg["q_s"], seg["kv"], seg["g_swa"], st["bucket"], rel_bias, _sink_column(sinks))
    return st


def _forward_rg(st, conv_w, conv_b, w_a, b_a, w_x, b_x, lam):
    seg = st["seg"]
    st["h_rg"], st["y_rg"] = _rglru_fwd(seg["xr"], seg["g_rg"], conv_w, conv_b, w_a, b_a, w_x, b_x, lam, st["T"])
    return st


def _forward_b(st, x, target, post_g, w_memkv, wbr, w_out):
    S = x.shape[0]
    M = st["memn"].shape[0]
    seg = st["seg"]
    st["mkv"] = _matmul(st["memn"], w_memkv, "nn", M, 2 * D_MEM, D_MODEL, M, 512, D_MODEL, BF, "mem_kv")
    st["o_mem"], st["y_mem"] = _mem_fwd(seg["q_m"], st["mkv"], seg["g_mem"])
    st["ys"] = (st["y_rg"], st["y_swa"], st["y_mem"])
    st["merged"] = _merge_fwd(st["ys"], wbr, seg["gl"], st["tm"])
    st["dout"], st["dy"], st["loss"], st["dpost"] = _out_loss(st["merged"], w_out, x, target, post_g, min(256, S))
    return st


def _backward_a1(st, wbr, w_out):
    S = st["h"].shape[0]
    seg, ys, tm = st["seg"], st["ys"], st["tm"]
    st["dw_out"] = _matmul(st["merged"], st["dout"], "tn", D_MODEL, D_MODEL, S, 256, D_MODEL, S, BF, "dw_out", out_blocked="row")
    dgl0, dgl1, dgl2, dp0, dp1, dp2 = _merge_bwd(st["dout"], w_out, ys, wbr, seg["gl"], tm)
    st["dgl"] = (dgl0, dgl1, dgl2)
    dys, dwbr = [], []
    for i, dp in enumerate((dp0, dp1, dp2)):
        dys.append(_matmul(dp, wbr[i], "nn", S, 1024, D_MODEL, tm, 1024, D_MODEL, F32, "dy_br%d" % i))
        dwbr.append(_matmul(ys[i], dp, "tn", 1024, D_MODEL, S, 1024, 256, S, BF, "dw_br%d" % i, out_blocked="col"))
    st["dys"], st["dwbr"] = dys, dwbr
    return st


def _backward_a2(st, mem, w_memkv, conv_w, conv_b, w_a, b_a, w_x, b_x, lam):
    M = mem.shape[0]
    seg, dys = st["seg"], st["dys"]
    st["dq_m"], st["dg_mem"], dmkv = _mem_bwd(seg["q_m"], st["mkv"], seg["g_mem"], st["o_mem"], dys[2])
    st["dmkv"] = dmkv.astype(BF)
    st["dw_memkv"] = _matmul(st["memn"], st["dmkv"], "tn", D_MODEL, 2 * D_MEM, M, 256, 2 * D_MEM, M, BF, "dw_memkv", out_blocked="row")
    st["dxr"], st["dg_rg"], st["dw_a"], st["dw_x"], st["dvec"] = _rglru_bwd(
        seg["xr"], seg["g_rg"], st["h_rg"], dys[0], conv_w, conv_b, w_a, b_a, w_x, b_x, lam, st["T"])
    return st


def _mem_gain_grad(st, mem, w_memkv, dep=None):
    M = mem.shape[0]
    dmemn = _matmul(st["dmkv"], w_memkv, "nt", M, D_MODEL, 2 * D_MEM, M, 512, 2 * D_MEM, F32, "dmemn", dep=dep)
    return _rms_gain_grad(dmemn, mem, "dmem_gain")


def _backward_b(st, rel_bias, sinks):
    seg = st["seg"]
    others = {"xr": st["dxr"], "g_rg": st["dg_rg"], "q_m": st["dq_m"], "g_mem": st["dg_mem"], "gl": st["dgl"]}
    dproj, dkv, dsinks, drel = _swa_bwd(seg["q_s"], seg["kv"], seg["g_swa"], st["o_swa"], st["dys"][1],
                                        st["bucket"], rel_bias, _sink_column(sinks), others)
    st["dsinks"], st["drel"] = dsinks.reshape(1, SWA_HEADS), drel.reshape(REL_BUCKETS, SWA_HEADS)
    col_kv = [c0 for name, c0, _, _ in SEGMENTS if name == "kv"][0]
    st["dproj"] = lax.dynamic_update_slice(dproj, dkv.astype(BF), (0, col_kv))
    return st


def _dw_in_half(st, half, dep=None):
    S = st["h"].shape[0]
    dw = _matmul(st["dproj"], st["h"], "tn", D_IN, D_MODEL // 2, S, D_IN_TILE, D_MODEL // 2, S, BF, "dw_in%d" % half, b_noff=half, dep=dep, out_hbm=True)
    return dw.reshape(N_DEV, D_IN // N_DEV, D_MODEL // 2)


def _owner_blocks(a):
    return jnp.swapaxes(a.reshape((4, 2) + a.shape[1:]), 0, 1)


def _pad_rows(a, rows):
    a = a.reshape(-1, 128) if a.shape[-1] % 128 == 0 else jnp.pad(a, ((0, 0), (0, 128 - a.shape[-1])))
    return jnp.pad(a, ((0, rows - a.shape[0]), (0, 0))) if a.shape[0] < rows else a


def kernel(x, mem, pre_norm_g, post_norm_g, mem_norm_g, w_in, conv_w, conv_b, w_rg_a, b_rg_a, w_rg_x, b_rg_x, lru_lambda, swa_sinks, rel_bias, w_mem_kv, w_br_rg, w_br_swa, w_br_mem, w_out, loss_target, m_pre_norm_g, m_post_norm_g, m_mem_norm_g, m_w_in, m_conv_w, m_conv_b, m_w_rg_a, m_b_rg_a, m_w_rg_x, m_b_rg_x, m_lru_lambda, m_swa_sinks, m_rel_bias, m_w_mem_kv, m_w_br_rg, m_w_br_swa, m_w_br_mem, m_w_out, v_pre_norm_g, v_post_norm_g, v_mem_norm_g, v_w_in, v_conv_w, v_conv_b, v_w_rg_a, v_b_rg_a, v_w_rg_x, v_b_rg_x, v_lru_lambda, v_swa_sinks, v_rel_bias, v_w_mem_kv, v_w_br_rg, v_w_br_swa, v_w_br_mem, v_w_out):
    cx, cy, cc = lax.axis_index("x"), lax.axis_index("y"), lax.axis_index("c")
    me = 4 * cx + 2 * cy + cc
    chip = 2 * cx + cy
    core = jnp.reshape(cc, (1,)).astype(jnp.int32)
    x0, mem0 = x[0], mem[0]
    w_a_b, w_x_b = w_rg_a[0].astype(BF), w_rg_x[0].astype(BF)

    def landing(own, slot, slots):
        return lax.dynamic_update_slice(lax.empty((slots,) + own.shape, own.dtype), own[None], (slot,) + (0,) * own.ndim)


    def swap_start(parts, tag):
        return _exchange_start(parts, [lax.empty((4,) + p.shape[-2:], p.dtype) for p in parts], _plan_swap([p.ndim for p in parts]),
                               "swap_%s_start" % tag)

    def scatter_start(swap, after, tag, prefill=()):
        s_send, s_recv, parts, got, _ = swap
        got = _exchange_wait(s_send, s_recv, parts, got, _plan_swap([p.ndim for p in parts]), after, "swap_%s_wait" % tag)
        sums = [_pair_sum(p, g, core, "scatter_%s_sum%d" % (tag, i)) for i, (p, g) in enumerate(zip(parts, got))]
        lands = [landing(lax.dynamic_index_in_dim(s, chip, 0, keepdims=False), chip, 4) if i in prefill
                 else lax.empty(s.shape, s.dtype) for i, s in enumerate(sums)]
        return _exchange_start(sums, lands, _plan_scatter(len(sums)), "scatter_%s_start" % tag)

    def corner(a):
        return a.reshape(-1, a.shape[-1])[:8, :128]

    def zero_after(a):
        return jnp.minimum(jnp.abs(a.reshape(-1)[0].astype(F32)), 0.0)

    g_in, g_cw, h0, memn0 = _all_gather_relayed([jnp.transpose(w_in[0]).astype(BF), conv_w[0]], [True, False], "gather_w_in",
                                                side=_rms_side([(x0, pre_norm_g), (mem0, mem_norm_g)]))
    w_in_f = g_in.reshape(D_IN, D_MODEL)
    conv_w_f = jnp.transpose(g_cw, (1, 0, 2)).reshape(CONV_W, D_RNN)

    after_first = zero_after(g_cw).astype(BF)
    rest = [w.astype(BF) + after_first for w in (w_mem_kv[0], jnp.transpose(w_br_rg[0]), jnp.transpose(w_br_swa[0]),
                                                 jnp.transpose(w_br_mem[0]), w_out[0])]
    plan_g = _plan_gather(len(rest))
    zones = _place_own([lax.empty((N_DEV,) + w.shape, w.dtype) for w in rest], rest, jnp.reshape(me, (1,)).astype(jnp.int32), "gather_rest_own")
    g_send, g_recv, g_src, g_land, g_token = _exchange_start(rest, zones, plan_g, "gather_rest_start")
    st = _forward_a(h0, memn0, w_in_f, swa_sinks, rel_bias, dep=g_token)
    g_land = _exchange_wait(g_send, g_recv, g_src, g_land, plan_g, st["y_swa"], "gather_rest_wait")
    plan_f = _plan_forward(len(rest))
    f_send, f_recv, _, g_land, f_token = _exchange_start(None, g_land, plan_f, "forward_rest_start")
    st = _forward_rg(st, conv_w_f, conv_b + f_token[0:1, 0:1], w_a_b, b_rg_a, w_x_b, b_rg_x, lru_lambda)
    g_land = _exchange_wait(f_send, f_recv, None, g_land, plan_f, corner(st["y_rg"]), "forward_rest_wait")
    w_memkv_f = g_land[0].reshape(D_MODEL, 2 * D_MEM)
    wbr = tuple(g_land[i].reshape(D_MODEL, D_RNN) for i in (1, 2, 3))
    w_out_f = g_land[4].reshape(D_MODEL, D_MODEL)

    st = _forward_b(st, x0, loss_target[0], post_norm_g, w_memkv_f, wbr, w_out_f)
    st = _backward_a1(st, wbr, w_out_f)
    parts_a = [st["dw_out"], st["dwbr"][0], st["dwbr"][1], st["dwbr"][2]]
    plan_a = _plan_scatter(len(parts_a))
    swap_a = swap_start(parts_a, "a")
    st = _backward_a2(st, mem0, w_memkv_f, conv_w_f, conv_b + swap_a[4][0:1, 0:1], w_a_b, b_rg_a, w_x_b, b_rg_x, lru_lambda)
    a_send, a_recv, a_src, a_land, a_token = scatter_start(swap_a, st["dxr"], "a")
    parts_c = [st["dw_memkv"], _owner_blocks(st["dw_a"]), _owner_blocks(st["dw_x"])]
    plan_c = _plan_scatter(len(parts_c))
    swap_c = swap_start(parts_c, "c")

    st = _backward_b(st, rel_bias, swa_sinks + swap_c[4][0:1, 0:1] + a_token[0:1, 0:1])
    c_send, c_recv, c_src, c_land, c_token = scatter_start(swap_c, st["dsinks"], "c", prefill=(1, 2))
    plan_b = _plan_scatter(1)

    def dw_in_parts(half, dep):
        dwh = _dw_in_half(st, half, dep)
        return dwh, [dwh]

    dw0, parts_b0 = dw_in_parts(0, c_token)
    swap_b0 = swap_start(parts_b0, "b0")
    a_land = _exchange_wait(a_send, a_recv, a_src, a_land, plan_a, swap_b0[4], "scatter_a_wait")
    big = [None] * 6

    chip1 = jnp.reshape(chip, (1,)).astype(jnp.int32)

    def adamw_big(j, land, own, wt, mt, vt):
        big[j] = _adamw(land, own, chip1, wt, mt, vt, "adamw_big%d" % j)

    adamw_big(5, a_land[0], a_src[0], w_out, m_w_out, v_w_out)
    adamw_big(2, a_land[1], a_src[1], w_br_rg, m_w_br_rg, v_w_br_rg)
    adamw_big(3, a_land[2], a_src[2], w_br_swa, m_w_br_swa, v_w_br_swa)
    halves = [scatter_start(swap_b0, corner(big[5][1]) + corner(big[2][1]) + corner(big[3][1]), "b0")]
    dw1, parts_b1 = dw_in_parts(1, halves[0][4])
    swap_b1 = swap_start(parts_b1, "b1")
    c_land = _exchange_wait(c_send, c_recv, c_src, c_land, plan_c, swap_b1[4], "scatter_c_wait")
    g_wa_blk = _sum_parts(c_land[1], "sum_w_rg_a")
    g_wx_blk = _sum_parts(c_land[2], "sum_w_rg_x")
    adamw_big(4, a_land[3], a_src[3], w_br_mem, m_w_br_mem, v_w_br_mem)
    adamw_big(1, c_land[0], c_src[0], w_mem_kv, m_w_mem_kv, v_w_mem_kv)
    halves.append(scatter_start(swap_b1, corner(big[4][1]) + corner(big[1][1]), "b1"))
    st["dmem_g"] = _mem_gain_grad(st, mem0, w_memkv_f, halves[1][4])
    grad_x, dpre = _dh_dx(st["dproj"], w_in_f, x0, st["dy"], pre_norm_g + halves[1][4][0:1, 0:1], st["tm"], D_IN_TILE)
    pack = jnp.concatenate([dpre.reshape(16, 128), st["dpost"].reshape(16, 128), st["dmem_g"].reshape(16, 128),
                            st["dvec"].reshape(64, 128), _pad_rows(st["dsinks"], 8), _pad_rows(st["drel"], 32), g_wa_blk, g_wx_blk,
                            st["loss"]], axis=0)
    plan_s = _plan_everyone(1)
    s_send, s_recv, s_src, s_land, s_token = _exchange_start([pack], [landing(pack, me, N_DEV)], plan_s, "gather_small_start")
    swap_last = lambda a: jnp.transpose(a, (0, 2, 1))
    after, big0_t = s_token, None
    for half, (b_send, b_recv, b_src, b_land, _) in enumerate(halves):
        b_land = _exchange_wait(b_send, b_recv, b_src, b_land, plan_b, after, "scatter_b%d_wait" % half)[0]
        big0_t = _adamw(b_land, b_src[0], chip1, swap_last(w_in), swap_last(m_w_in), swap_last(v_w_in), "adamw_big0_%d" % half,
                        group=(half, 2), into=big0_t)
        after = corner(big0_t[1])
    big[0] = [swap_last(a) for a in big0_t]
    gathered = _exchange_wait(s_send, s_recv, s_src, s_land, plan_s, corner(big0_t[1]), "gather_small_wait")[0]
    gs = _sum_parts(gathered, "sum_small")
    loss_total = gs[408, 0]
    g_pre, g_post, g_memg = gs[0:16].reshape(1, D_MODEL), gs[16:32].reshape(1, D_MODEL), gs[32:48].reshape(1, D_MODEL)
    gvec = gs[48:112].reshape(8, D_RNN)
    g_conv_w = lax.dynamic_slice(gvec[0:CONV_W], (0, me * RNN_BLOCK), (CONV_W, RNN_BLOCK))
    g_conv_b, g_b_a, g_b_x, g_lam = gvec[4:5], gvec[5:6], gvec[6:7], gvec[7:8]
    g_sinks = gs[112:113, :SWA_HEADS]
    g_rel = gs[120:152, :SWA_HEADS]
    g_w_a = gathered[:, 152:280]
    g_w_x = gathered[:, 280:408]

    g_small = (g_pre, g_post, g_memg, g_conv_b, g_b_a, g_b_x, g_lam, g_w_a, g_w_x, g_sinks, g_rel, g_conv_w)
    w_small = (pre_norm_g, post_norm_g, mem_norm_g, conv_b, b_rg_a, b_rg_x, lru_lambda, w_rg_a, w_rg_x, swa_sinks, rel_bias, conv_w)
    m_small = (m_pre_norm_g, m_post_norm_g, m_mem_norm_g, m_conv_b, m_b_rg_a, m_b_rg_x, m_lru_lambda, m_w_rg_a, m_w_rg_x, m_swa_sinks, m_rel_bias, m_conv_w)
    v_small = (v_pre_norm_g, v_post_norm_g, v_mem_norm_g, v_conv_b, v_b_rg_a, v_b_rg_x, v_lru_lambda, v_w_rg_a, v_w_rg_x, v_swa_sinks, v_rel_bias, v_conv_w)
    sm = _adamw_small(g_small, w_small, m_small, v_small)


    def leaves(k):
        s = sm[k]
        return [s[0], s[1], s[2], big[0][k], s[11], s[3], s[7], s[4], s[8], s[5], s[6], s[9], s[10],
                big[1][k], big[2][k], big[3][k], big[4][k], big[5][k]]

    return (loss_total, grad_x[None], *leaves(0), *leaves(1), *leaves(2), *leaves(3))
```

```python
import math

import jax
import jax.numpy as jnp
import numpy as np
from jax import lax
from jax.experimental import pallas as pl
from jax.experimental.pallas import tpu as pltpu

F32, BF = jnp.float32, jnp.bfloat16
MESH = pl.DeviceIdType.MESH
N_DEV = 8

D_MODEL = 2048
D_RNN = 1024
RNN_BLOCKS = 8
RNN_BLOCK = 128
CONV_W = 4
LRU_C = 8.0
SWA_HEADS = 16
SWA_KV_HEADS = 2
SWA_HD = 64
WINDOW = 128
MEM_HEADS = 4
MEM_HD = 256
D_MEM = 1024
REL_BUCKETS = 32
REL_MAX_DIST = 128
EPS = 1e-6
NEG_INF = -1e30
D_IN = 12544
SEGMENTS = (("xr", 0, 1024, F32), ("g_rg", 1024, 1024, F32), ("q_s", 2048, 1024, BF), ("kv", 3072, 256, BF),
            ("g_swa", 3328, 1024, F32), ("q_m", 4352, 1024, BF), ("g_mem", 5376, 1024, F32), ("gl", 6400, 6144, F32))
SEG_TILE = 256
D_IN_TILE = 7 * SEG_TILE

ADAM_LR, ADAM_B1, ADAM_B2, ADAM_EPS, ADAM_WD, ADAM_STEP = 0.001, 0.9, 0.999, 1e-08, 0.01, 10

NN = (((1,), (0,)), ((), ()))
NT = (((1,), (1,)), ((), ()))
TN = (((0,), (0,)), ((), ()))
MIB = 2 ** 20


def _dot(a, b, dn):
    return lax.dot_general(a, b, dn, preferred_element_type=F32)


VMEM_LIMIT_MIB = 48
VMEM_LIMIT_LARGE_MIB = 56


def _params(sem=None, large=False):
    return pltpu.CompilerParams(dimension_semantics=sem, vmem_limit_bytes=(VMEM_LIMIT_LARGE_MIB if large else VMEM_LIMIT_MIB) * MIB)


def _sigmoid(z):
    return 1.0 / (1.0 + jnp.exp(-z))


def _softplus(z):
    return jnp.maximum(z, 0.0) + jnp.log(1.0 + jnp.exp(-jnp.abs(z)))


def _expm1(z):
    p = z * (1.0 + z * (0.5 + z * (1.0 / 6 + z * (1.0 / 24 + z * (1.0 / 120 + z * (1.0 / 720 + z * (1.0 / 5040 + z / 40320)))))))
    return jnp.where(jnp.abs(z) < 0.3, p, jnp.exp(z) - 1.0)


def _flat(p):
    return 4 * p[0] + 2 * p[1] + p[2]


def _all_gather_relayed(arrs, relay, name, side=None):
    n = len(arrs)
    K = 9
    work, side_ins, side_outs, side_scratch = side if side is not None else (None, [], [], [])
    n_in, n_out = n + len(side_ins), n + len(side_outs)

    def body(*refs):
        ins, outs = refs[:n], refs[n_in:n_in + n]
        send_sems, recv_sems, local_sems = refs[n_in + n_out:n_in + n_out + 3]
        x, y, c = lax.axis_index("x"), lax.axis_index("y"), lax.axis_index("c")
        me, sib = (x, y, c), (x, y, 1 - c)
        xn, yn, dg = (1 - x, y, c), (x, 1 - y, c), (1 - x, 1 - y, c)

        def other(p):
            return (p[0], p[1], 1 - p[2])

        def rows(a, half):
            h = arrs[a].shape[0] // 2
            return pl.ds(half * h, h)

        def copy(a, k, block, to, half=None, src=None):
            dst = outs[a].at[_flat(block)]
            if half is not None:
                dst = dst.at[rows(a, half)]
            return pltpu.make_async_remote_copy(src_ref=dst if src is None else src, dst_ref=dst,
                                                send_sem=send_sems.at[a * K + k], recv_sem=recv_sems.at[a * K + k],
                                                device_id=to, device_id_type=MESH)

        mine = [pltpu.make_async_copy(ins[a], outs[a].at[_flat(me)], local_sems.at[a]) for a in range(n)]
        for cp in mine:
            cp.start()
        sends = []

        def start(cp):
            cp.start()
            sends.append(cp)

        for a in range(n):
            start(copy(a, 1, me, xn, src=ins[a]))
            start(copy(a, 2, me, yn, src=ins[a]))
            if not relay[a]:
                start(copy(a, 3, me, dg, src=ins[a]))
            start(copy(a, 0, me, sib, src=ins[a]))
        if work is not None:
            work(refs[n:n_in], refs[n_in + n:n_in + n_out], refs[n_in + n_out + 3:])
        for a in range(n):
            copy(a, 1, xn, me).wait_recv()
            if relay[a]:
                start(copy(a, 3, xn, yn, half=0))
            start(copy(a, 5, xn, sib))
        for a in range(n):
            copy(a, 2, yn, me).wait_recv()
            if relay[a]:
                start(copy(a, 4, yn, xn, half=1))
            start(copy(a, 6, yn, sib))
        for a in range(n):
            if relay[a]:
                copy(a, 3, dg, me, half=0).wait_recv()
                start(copy(a, 7, dg, sib, half=0))
                copy(a, 4, dg, me, half=1).wait_recv()
                start(copy(a, 8, dg, sib, half=1))
            else:
                copy(a, 3, dg, me).wait_recv()
                start(copy(a, 7, dg, sib))
        for a in range(n):
            copy(a, 0, sib, me).wait_recv()
            copy(a, 5, other(xn), me).wait_recv()
            copy(a, 6, other(yn), me).wait_recv()
            if relay[a]:
                copy(a, 7, other(dg), me, half=0).wait_recv()
                copy(a, 8, other(dg), me, half=1).wait_recv()
            else:
                copy(a, 7, other(dg), me).wait_recv()
        for cp in sends:
            cp.wait_send()
        for cp in mine:
            cp.wait()

    any_spec = pl.BlockSpec(memory_space=pl.ANY)
    return pl.pallas_call(
        body, name=name,
        out_shape=[jax.ShapeDtypeStruct((N_DEV,) + a.shape, a.dtype) for a in arrs] + list(side_outs),
        in_specs=[any_spec] * n_in, out_specs=[any_spec] * n_out,
        scratch_shapes=[pltpu.SemaphoreType.DMA((K * n,)), pltpu.SemaphoreType.DMA((K * n,)), pltpu.SemaphoreType.DMA((n,))]
        + list(side_scratch),
        compiler_params=_params(),
    )(*arrs, *side_ins)


def _chip_peers(x, y):
    return [(1 - x, y), (x, 1 - y), (1 - x, 1 - y)]


def _chip(p):
    return 2 * p[0] + p[1]


def _plan_gather(n):
    def plan(x, y, c):
        out = []
        for a in range(n):
            for peer in [(x, y, 1 - c)] + [(*ch, c) for ch in _chip_peers(x, y)]:
                out.append((a, None, ("lead", _flat((x, y, c))), peer, ("lead", _flat(peer))))
        return out
    return plan


def _plan_everyone(n):
    def plan(x, y, c):
        out = []
        for a in range(n):
            for r in range(1, N_DEV):
                peer = (1 - x if r & 4 else x, 1 - y if r & 2 else y, 1 - c if r & 1 else c)
                out.append((a, None, ("lead", _flat((x, y, c))), peer, ("lead", _flat(peer))))
        return out
    return plan


def _plan_swap(ndims):
    def plan(x, y, c):
        out = []
        for a, nd in enumerate(ndims):
            if nd == 4:
                out.append((a, 1 - c, ("all", 0), (x, y, 1 - c), ("all", 0)))
            else:
                out += [(a, 2 * j + 1 - c, ("lead", j), (x, y, 1 - c), ("lead", j)) for j in range(4)]
        return out
    return plan


def _slot(ref, where):
    kind, k = where
    return ref if kind == "all" else ref.at[k]


def _plan_scatter(n):
    def plan(x, y, c):
        out = []
        for a in range(n):
            for ch in _chip_peers(x, y):
                out.append((a, _chip(ch), ("lead", _chip((x, y))), (*ch, c), ("lead", _chip(ch))))
        return out
    return plan


HBM_SPEC = pl.BlockSpec(memory_space=pltpu.HBM)
SEM_SPEC = pl.BlockSpec(memory_space=pltpu.SEMAPHORE)


def _in_hbm(a):
    return pltpu.with_memory_space_constraint(a, pltpu.HBM)


def _exchange_start(srcs, lands, plan, name):
    n = len(lands)
    ns = 0 if srcs is None else n
    count = len(plan(0, 0, 0))

    def body(*refs):
        land_refs = refs[ns:ns + n]
        src_refs = land_refs if srcs is None else refs[:n]
        send_sems, recv_sems = refs[ns + n], refs[ns + n + 1]
        token = refs[-1]
        x, y, c = lax.axis_index("x"), lax.axis_index("y"), lax.axis_index("c")
        for k, (a, si, di, peer, _) in enumerate(plan(x, y, c)):
            src = src_refs[a] if si is None else src_refs[a].at[si]
            pltpu.make_async_remote_copy(src_ref=src, dst_ref=_slot(land_refs[a], di), send_sem=send_sems.at[k],
                                         recv_sem=recv_sems.at[k], device_id=peer, device_id_type=MESH).start()
        token[...] = jnp.zeros_like(token)

    out = pl.pallas_call(
        body, name=name,
        out_shape=(pltpu.SemaphoreType.DMA((count,)), pltpu.SemaphoreType.DMA((count,)),
                   *[pltpu.HBM(a.shape, a.dtype) for a in lands], jax.ShapeDtypeStruct((8, 128), F32)),
        in_specs=[HBM_SPEC] * (ns + n),
        out_specs=(SEM_SPEC, SEM_SPEC, *([HBM_SPEC] * n), pl.BlockSpec(memory_space=pltpu.VMEM)),
        input_output_aliases={ns + i: 2 + i for i in range(n)},
        compiler_params=pltpu.CompilerParams(has_side_effects=pltpu.SideEffectType.DATAFLOW_SIDE_EFFECTING),
    )(*[_in_hbm(a) for a in (srcs or [])], *[_in_hbm(a) for a in lands])
    return out[0], out[1], srcs if srcs is None else list(srcs), list(out[2:2 + n]), out[-1]


def _exchange_wait(send_sems, recv_sems, srcs, lands, plan, after, name):
    n = len(lands)
    ns = 0 if srcs is None else n

    def body(*refs):
        land_refs = refs[ns:ns + n]
        src_refs = land_refs if srcs is None else refs[:n]
        send_sems, recv_sems = refs[ns + n], refs[ns + n + 1]
        x, y, c = lax.axis_index("x"), lax.axis_index("y"), lax.axis_index("c")
        for k, (a, si, _, peer, ri) in enumerate(plan(x, y, c)):
            src = src_refs[a] if si is None else src_refs[a].at[si]
            cp = pltpu.make_async_remote_copy(src_ref=src, dst_ref=_slot(land_refs[a], ri), send_sem=send_sems.at[k],
                                              recv_sem=recv_sems.at[k], device_id=peer, device_id_type=MESH)
            cp.wait_send()
            cp.wait_recv()

    out = pl.pallas_call(
        body, name=name,
        out_shape=tuple(pltpu.HBM(a.shape, a.dtype) for a in lands),
        in_specs=[HBM_SPEC] * (ns + n) + [SEM_SPEC, SEM_SPEC, pl.BlockSpec(memory_space=pl.ANY)],
        out_specs=tuple([HBM_SPEC] * n),
        input_output_aliases={ns + i: i for i in range(n)},
        compiler_params=pltpu.CompilerParams(has_side_effects=pltpu.SideEffectType.DATAFLOW_SIDE_EFFECTING),
    )(*[_in_hbm(a) for a in (srcs or [])], *lands, send_sems, recv_sems, after)
    return list(out)


def _plan_forward(n):
    def plan(x, y, c):
        return [(a, _flat((*ch, c)), ("lead", _flat((*ch, c))), (x, y, 1 - c), ("lead", _flat((*ch, 1 - c))))
                for a in range(n) for ch in _chip_peers(x, y)]
    return plan


def _place_own(zones, owns, slot, name):
    n = len(zones)

    def body(slot_ref, *refs):
        for a in range(n):
            refs[2 * n + a][...] = refs[a][...]

    return pl.pallas_call(
        body, name=name,
        grid_spec=pltpu.PrefetchScalarGridSpec(
            num_scalar_prefetch=1, grid=(1,),
            in_specs=[pl.BlockSpec(o.shape, lambda i, s_ref: (0, 0)) for o in owns] + [pl.BlockSpec(memory_space=pl.ANY)] * n,
            out_specs=[pl.BlockSpec((None,) + o.shape, lambda i, s_ref: (s_ref[0], 0, 0)) for o in owns]),
        out_shape=[jax.ShapeDtypeStruct(z.shape, z.dtype) for z in zones],
        input_output_aliases={1 + n + a: a for a in range(n)},
        compiler_params=_params(("arbitrary",)),
    )(slot, *owns, *zones)


def _pair_sum(parts, got, core, name):
    R, C = parts.shape[-2:]
    mine = (pl.BlockSpec((None, None, R, C), lambda j, c_ref: (c_ref[0], j, 0, 0)) if parts.ndim == 4
            else pl.BlockSpec((None, R, C), lambda j, c_ref: (2 * j + c_ref[0], 0, 0)))

    def body(c_ref, p_ref, g_ref, o_ref):
        o_ref[...] = (p_ref[...].astype(F32) + g_ref[...].astype(F32)).astype(o_ref.dtype)

    return pl.pallas_call(
        body, name=name,
        grid_spec=pltpu.PrefetchScalarGridSpec(
            num_scalar_prefetch=1, grid=(4,),
            in_specs=[mine, pl.BlockSpec((None, R, C), lambda j, c_ref: (j, 0, 0))],
            out_specs=pl.BlockSpec((None, R, C), lambda j, c_ref: (j, 0, 0))),
        out_shape=pltpu.HBM((4, R, C), parts.dtype),
        compiler_params=_params(("parallel",)),
    )(core, parts, _in_hbm(got))


def _matmul(a, b, mode, M, N, K, tm, tn, tk, out_dtype, name, b_noff=0, out_blocked=None, dep=None, out_hbm=False):
    nm, nn, nk = M // tm, N // tn, K // tk
    if mode == "nn":
        a_spec = pl.BlockSpec((tm, tk), lambda j, i, k: (i, k))
        b_spec = pl.BlockSpec((tk, tn), lambda j, i, k: (k, j + b_noff))
        dn = NN
    elif mode == "nt":
        a_spec = pl.BlockSpec((tm, tk), lambda j, i, k: (i, k))
        b_spec = pl.BlockSpec((tn, tk), lambda j, i, k: (j + b_noff, k))
        dn = NT
    else:
        a_spec = pl.BlockSpec((tk, tm), lambda j, i, k: (k, i))
        b_spec = pl.BlockSpec((tk, tn), lambda j, i, k: (k, j + b_noff))
        dn = TN
    if out_blocked == "col":
        out_shape = jax.ShapeDtypeStruct((2, 4, M, tn), out_dtype)
        out_spec = pl.BlockSpec((None, None, tm, tn), lambda j, i, k: (j % 2, j // 2, i, 0))
    elif out_blocked == "row":
        out_shape = jax.ShapeDtypeStruct((2, 4, tm, N), out_dtype)
        out_spec = pl.BlockSpec((None, None, tm, tn), lambda j, i, k: (i % 2, i // 2, 0, j))
    else:
        out_shape = jax.ShapeDtypeStruct((M, N), out_dtype)
        out_spec = pl.BlockSpec((tm, tn), lambda j, i, k: (i, j))
    if out_hbm or out_blocked is not None:
        out_shape = pltpu.HBM(out_shape.shape, out_shape.dtype)

    n_extra = int(dep is not None)

    def body(a_ref, b_ref, *rest):
        o_ref, scratch = rest[n_extra], rest[n_extra + 1:]
        if nk == 1:
            o_ref[...] = _dot(a_ref[...], b_ref[...], dn).astype(out_dtype)
        else:
            acc_ref, = scratch
            k = pl.program_id(2)

            @pl.when(k == 0)
            def _():
                acc_ref[...] = jnp.zeros_like(acc_ref)

            acc_ref[...] += _dot(a_ref[...], b_ref[...], dn)

            @pl.when(k == nk - 1)
            def _():
                o_ref[...] = acc_ref[...].astype(out_dtype)

    return pl.pallas_call(
        body, name=name, grid=(nn, nm, nk),
        in_specs=[a_spec, b_spec] + ([] if dep is None else [pl.BlockSpec((8, 128), lambda j, i, k: (0, 0))]),
        out_specs=out_spec, out_shape=out_shape,
        scratch_shapes=[] if nk == 1 else [pltpu.VMEM((tm, tn), F32)],
        compiler_params=_params(("parallel", "parallel", "arbitrary")),
    )(a, b, *([] if dep is None else [dep]))


RMS_SIDE_ROWS = 512


def _rms_side(pairs):
    chunks = [min(x.shape[0], RMS_SIDE_ROWS) for x, _ in pairs]

    def work(ins, outs, scratch):
        sem = scratch[-1]

        def move(src, dst):
            cp = pltpu.make_async_copy(src, dst, sem.at[0])
            cp.start()
            cp.wait()

        for p, ((x, _), tr) in enumerate(zip(pairs, chunks)):
            x_ref, g_ref, h_ref = ins[2 * p], ins[2 * p + 1], outs[p]
            xv, gv, hv = scratch[3 * p:3 * p + 3]
            move(g_ref, gv)
            for i in range(x.shape[0] // tr):
                rows = pl.ds(i * tr, tr)
                move(x_ref.at[rows], xv)
                v = xv[...]
                hv[...] = (v * lax.rsqrt(jnp.mean(v * v, axis=-1, keepdims=True) + EPS) * gv[...]).astype(BF)
                move(hv, h_ref.at[rows])

    scratch = [s for (x, g), tr in zip(pairs, chunks)
               for s in (pltpu.VMEM((tr, x.shape[1]), F32), pltpu.VMEM(g.shape, F32), pltpu.VMEM((tr, x.shape[1]), BF))]
    return (work, [a for pair in pairs for a in pair], [jax.ShapeDtypeStruct(x.shape, BF) for x, _ in pairs],
            scratch + [pltpu.SemaphoreType.DMA((1,))])


def _rms_gain_grad(dn, x, name):
    R, Dm = x.shape

    def body(dn_ref, x_ref, o_ref):
        xv = x_ref[...]
        r = lax.rsqrt(jnp.mean(xv * xv, axis=-1, keepdims=True) + EPS)
        o_ref[...] = jnp.sum(dn_ref[...] * xv * r, axis=0, keepdims=True)

    return pl.pallas_call(
        body, name=name, out_shape=jax.ShapeDtypeStruct((1, Dm), F32),
        compiler_params=_params(),
    )(dn, x)


def _shift_down(v, k, head8, row, T):
    if k == 0:
        return v
    r = pltpu.roll(v, k, 0)
    hr = pltpu.roll(head8, k, 0)
    top = jnp.where(row[:8] < k, hr, r[:8])
    return jnp.concatenate([top, r[8:]], axis=0)


def _shift_up(v, k, tail8, row, T):
    if k == 0:
        return v
    r = pltpu.roll(v, T - k, 0)
    tr = pltpu.roll(tail8, 8 - k, 0)
    bot = jnp.where(row[:8] >= 8 - k, tr, r[T - 8:])
    return jnp.concatenate([r[:T - 8], bot], axis=0)


def _rglru_gates(u, head8, grow, row, T, cw_ref, cb_ref, wa_ref, ba_ref, wx_ref, bx_ref, lam_ref):
    us = [_shift_down(u, k, head8, row, T) for k in range(CONV_W)]
    acc = us[0] * cw_ref[0:1, :]
    for k in range(1, CONV_W):
        acc = acc + us[k] * cw_ref[k:k + 1, :]
    conv = cb_ref[...] + acc
    cbf = conv.astype(BF)
    r_ = _sigmoid(_dot(cbf, wa_ref[0], NN) + ba_ref[...])
    i_ = _sigmoid(_dot(cbf, wx_ref[0], NN) + bx_ref[...])
    sp = _softplus(-lam_ref[...])
    la = -LRU_C * r_ * sp
    a = jnp.exp(la)
    mult_raw = jnp.sqrt(-_expm1(2.0 * la))
    mult = jnp.where(grow == 0, 1.0, mult_raw)
    return us, conv, cbf, r_, i_, sp, a, mult_raw, mult


def _rglru_specs(T, nt, rev):
    tmap = (lambda n, t: (nt - 1 - t, n)) if rev else (lambda n, t: (t, n))
    hmap = ((lambda n, t: (jnp.maximum((nt - 1 - t) * (T // 8) - 1, 0), n)) if rev
            else (lambda n, t: (jnp.maximum(t * (T // 8) - 1, 0), n)))
    tile = pl.BlockSpec((T, RNN_BLOCK), tmap)
    halo = pl.BlockSpec((8, RNN_BLOCK), hmap)
    vec = pl.BlockSpec((1, RNN_BLOCK), lambda n, t: (0, n))
    cw = pl.BlockSpec((CONV_W, RNN_BLOCK), lambda n, t: (0, n))
    wblk = pl.BlockSpec((1, RNN_BLOCK, RNN_BLOCK), lambda n, t: (n, 0, 0))
    return tile, halo, vec, cw, wblk


def _rglru_fwd(xr, g, cw, cb, wa, ba, wx, bx, lam, T):
    S = xr.shape[0]
    nt = S // T

    def body(u_ref, uh_ref, g_ref, cw_ref, cb_ref, wa_ref, ba_ref, wx_ref, bx_ref, lam_ref, h_ref, y_ref, carry):
        t = pl.program_id(1)

        @pl.when(t == 0)
        def _():
            carry[...] = jnp.zeros_like(carry)

        row = lax.broadcasted_iota(jnp.int32, (T, RNN_BLOCK), 0)
        grow = row + t * T
        head8 = jnp.where(t > 0, uh_ref[...], 0.0)
        _, conv, _, _, i_, _, a, _, mult = _rglru_gates(u_ref[...], head8, grow, row, T, cw_ref, cb_ref, wa_ref, ba_ref,
                                                         wx_ref, bx_ref, lam_ref)
        b = mult * i_ * conv
        s = 1
        while s < T:
            keep = row >= s
            a_s = jnp.where(keep, pltpu.roll(a, s, 0), 1.0)
            b_s = jnp.where(keep, pltpu.roll(b, s, 0), 0.0)
            b = a * b_s + b
            a = a * a_s
            s *= 2
        h = b + a * carry[0:1, :]
        carry[...] = jnp.broadcast_to(h[T - 1:T, :], carry.shape)
        h_ref[...] = h
        gv = g_ref[...]
        y_ref[...] = (h * (gv * _sigmoid(gv))).astype(BF)

    tile, halo, vec, cwspec, wblk = _rglru_specs(T, nt, False)
    return pl.pallas_call(
        body, name="rglru_fwd", grid=(RNN_BLOCKS, nt),
        in_specs=[tile, halo, tile, cwspec, vec, wblk, vec, wblk, vec, vec],
        out_specs=[tile, tile],
        out_shape=[jax.ShapeDtypeStruct((S, D_RNN), F32), jax.ShapeDtypeStruct((S, D_RNN), BF)],
        scratch_shapes=[pltpu.VMEM((8, RNN_BLOCK), F32)],
        compiler_params=_params(("parallel", "arbitrary")),
    )(xr, xr, g, cw, cb, wa, ba, wx, bx, lam)


def _rglru_bwd(xr, g, h, dy, cw, cb, wa, ba, wx, bx, lam, T):
    S = xr.shape[0]
    nt = S // T

    def body(u_ref, uh_ref, g_ref, h_ref, hh_ref, dy_ref, cw_ref, cb_ref, wa_ref, ba_ref, wx_ref, bx_ref, lam_ref,
             du_ref, dg_ref, dwa_ref, dwx_ref, dvec_ref, c_dhh, c_a, c_dconv):
        t = pl.program_id(1)
        tt = nt - 1 - t

        @pl.when(t == 0)
        def _():
            c_dhh[...] = jnp.zeros_like(c_dhh)
            c_a[...] = jnp.zeros_like(c_a)
            c_dconv[...] = jnp.zeros_like(c_dconv)
            dwa_ref[...] = jnp.zeros_like(dwa_ref)
            dwx_ref[...] = jnp.zeros_like(dwx_ref)
            dvec_ref[...] = jnp.zeros_like(dvec_ref)

        row = lax.broadcasted_iota(jnp.int32, (T, RNN_BLOCK), 0)
        row8 = row[:8]
        grow = row + tt * T
        head8 = jnp.where(tt > 0, uh_ref[...], 0.0)
        us, conv, cbf, r_, i_, sp, a, mult_raw, mult = _rglru_gates(
            u_ref[...], head8, grow, row, T, cw_ref, cb_ref, wa_ref, ba_ref, wx_ref, bx_ref, lam_ref)
        hv = h_ref[...]
        hprev = _shift_down(hv, 1, jnp.where(tt > 0, hh_ref[...], 0.0), row, T)
        gv = g_ref[...]
        sg = _sigmoid(gv)
        dyv = dy_ref[...]
        dg_ref[...] = (dyv * hv * (sg * (1.0 + gv * (1.0 - sg)))).astype(BF)
        d = dyv * (gv * sg)
        A = _shift_up(a, 1, c_a[...], row, T)
        s = 1
        while s < T:
            keep = row < T - s
            A_s = jnp.where(keep, pltpu.roll(A, T - s, 0), 1.0)
            d_s = jnp.where(keep, pltpu.roll(d, T - s, 0), 0.0)
            d = A * d_s + d
            A = A * A_s
            s *= 2
        dhh = d + A * c_dhh[0:1, :]
        da = dhh * hprev
        dconv = dhh * mult * i_
        di = dhh * mult * conv
        dmult = dhh * i_ * conv
        dla = da * a - jnp.where(grow == 0, 0.0, dmult * (a * a) / mult_raw)
        dr = dla * (-LRU_C * sp)
        dsp = jnp.sum(dla * (-LRU_C * r_), axis=0, keepdims=True)
        dza = dr * r_ * (1.0 - r_)
        dzx = di * i_ * (1.0 - i_)
        dza_b, dzx_b = dza.astype(BF), dzx.astype(BF)
        dconv = dconv + _dot(dza_b, wa_ref[0], NT) + _dot(dzx_b, wx_ref[0], NT)
        dwa_ref[0] += _dot(cbf, dza_b, TN)
        dwx_ref[0] += _dot(cbf, dzx_b, TN)
        lam = lam_ref[...]
        rows = [jnp.sum(dconv * us[k], axis=0, keepdims=True) for k in range(CONV_W)]
        rows += [jnp.sum(dconv, axis=0, keepdims=True), jnp.sum(dza, axis=0, keepdims=True),
                 jnp.sum(dzx, axis=0, keepdims=True), dsp * (-_sigmoid(-lam))]
        upd = jnp.zeros((8, RNN_BLOCK), F32)
        for j, rv in enumerate(rows):
            upd = upd + jnp.where(row8 == j, rv, 0.0)
        dvec_ref[...] += upd
        tail8 = c_dconv[...]
        du = dconv * cw_ref[0:1, :]
        for k in range(1, CONV_W):
            du = du + _shift_up(dconv, k, tail8, row, T) * cw_ref[k:k + 1, :]
        du_ref[...] = du.astype(BF)
        c_dhh[...] = jnp.broadcast_to(dhh[0:1, :], c_dhh.shape)
        c_a[...] = jnp.broadcast_to(a[0:1, :], c_a.shape)
        c_dconv[...] = dconv[:8]

    tile, halo, vec, cwspec, wblk = _rglru_specs(T, nt, True)
    acc8 = pl.BlockSpec((8, RNN_BLOCK), lambda n, t: (0, n))
    return pl.pallas_call(
        body, name="rglru_bwd", grid=(RNN_BLOCKS, nt),
        in_specs=[tile, halo, tile, tile, halo, tile, cwspec, vec, wblk, vec, wblk, vec, vec],
        out_specs=[tile, tile, wblk, wblk, acc8],
        out_shape=[jax.ShapeDtypeStruct((S, D_RNN), BF), jax.ShapeDtypeStruct((S, D_RNN), BF),
                   jax.ShapeDtypeStruct((RNN_BLOCKS, RNN_BLOCK, RNN_BLOCK), F32),
                   jax.ShapeDtypeStruct((RNN_BLOCKS, RNN_BLOCK, RNN_BLOCK), F32),
                   jax.ShapeDtypeStruct((8, D_RNN), F32)],
        scratch_shapes=[pltpu.VMEM((8, RNN_BLOCK), F32)] * 3,
        compiler_params=_params(("parallel", "arbitrary")),
    )(xr, xr, g, h, h, dy, cw, cb, wa, ba, wx, bx, lam)


def _rel_bucket_map():
    qi = np.arange(WINDOW)[:, None]
    kj = np.arange(2 * WINDOW)[None, :]
    dist = jnp.asarray(qi + WINDOW - kj, jnp.int32)
    n = jnp.maximum(dist, 0)
    max_exact = REL_BUCKETS // 2
    ratio = jnp.log(jnp.maximum(n, 1).astype(F32) / max_exact) / math.log(REL_MAX_DIST / max_exact)
    large = jnp.minimum(max_exact + (ratio * (REL_BUCKETS - max_exact)).astype(jnp.int32), REL_BUCKETS - 1)
    bucket = jnp.where(n < max_exact, n, large).astype(jnp.int32)
    j = np.arange(WINDOW)[None, :]
    return jnp.where(jnp.asarray(j > qi), bucket[:, :WINDOW], bucket[:, WINDOW:])


def _swa_common(n, kv_ref, bucket_ref, relb_ref, bias_scr):
    @pl.when(n == 0)
    def _():
        bk = bucket_ref[...]
        for h in range(SWA_HEADS):
            acc = jnp.zeros((WINDOW, WINDOW), F32)
            for b in range(REL_BUCKETS):
                acc = acc + jnp.where(bk == b, relb_ref[b, h], 0.0)
            bias_scr[h] = acc

    prev0 = pl.multiple_of(jnp.maximum(n - 1, 0) * WINDOW, WINDOW)
    cur0 = pl.multiple_of(n * WINDOW, WINDOW)
    kk = jnp.concatenate([kv_ref[pl.ds(prev0, WINDOW), :], kv_ref[pl.ds(cur0, WINDOW), :]], axis=0).astype(F32)
    rowi = lax.broadcasted_iota(jnp.int32, (WINDOW, WINDOW), 0)
    col = lax.broadcasted_iota(jnp.int32, (WINDOW, WINDOW), 1)
    from_prev = col > rowi
    return kk, from_prev, prev0, cur0


def _fold(full, from_prev):
    return jnp.where(from_prev, full[:, :WINDOW], full[:, WINDOW:])


def _unfold(sq, from_prev):
    return jnp.concatenate([jnp.where(from_prev, sq, 0.0), jnp.where(from_prev, 0.0, sq)], axis=1)


def _half_pair(part, kvh):
    lo = lax.broadcasted_iota(jnp.int32, part.shape, 1) < SWA_HD
    if kvh == 0:
        pa = jnp.where(lo, part, 0.0)
        pb = pltpu.roll(pa, SWA_HD, 1)
    else:
        pb = jnp.where(lo, 0.0, part)
        pa = pltpu.roll(pb, SWA_HD, 1)
    return pa.astype(BF), pb.astype(BF)


ALL_HEADS = SWA_HEADS * WINDOW


def _sink_column(sinks):
    return jnp.repeat(sinks.reshape(SWA_HEADS), WINDOW).reshape(ALL_HEADS, 1)


def _swa_operands(kk):
    return [(_half_pair(kk[:, :128], kvh), _half_pair(kk[:, 128:], kvh)) for kvh in range(SWA_KV_HEADS)]


def _swa_probs(n, q_ref, ops, bias_scr, sinkc_ref, from_prev):
    lgs = []
    for kvh in range(SWA_KV_HEADS):
        (ka, kb), _ = ops[kvh]
        for p in range(4):
            q2 = q_ref[:, kvh * 512 + p * 128:kvh * 512 + p * 128 + 128]
            lgs += [_fold(_dot(q2, ka, NT), from_prev), _fold(_dot(q2, kb, NT), from_prev)]
    lg = jnp.concatenate(lgs, axis=0) * (SWA_HD ** -0.5) + bias_scr[...].reshape(ALL_HEADS, WINDOW)
    rowi = jnp.bitwise_and(lax.broadcasted_iota(jnp.int32, (ALL_HEADS, WINDOW), 0), WINDOW - 1)
    col = lax.broadcasted_iota(jnp.int32, (ALL_HEADS, WINDOW), 1)
    no_prev = jnp.where(n > 0, 0, 4 * WINDOW)
    lg = jnp.where(jnp.logical_or(col <= rowi, col > rowi + no_prev), lg, NEG_INF)
    sink = sinkc_ref[...]
    m = jnp.maximum(jnp.max(lg, axis=-1, keepdims=True), sink)
    e = jnp.exp(lg - m)
    es = jnp.exp(sink - m)
    den = jnp.sum(e, axis=-1, keepdims=True) + es
    return e / den, es / den


def _swa_fwd(q, kv, g, bucket, rel_bias, sink_col):
    S = q.shape[0]
    nb = S // WINDOW

    def body(q_ref, kv_ref, g_ref, bucket_ref, relb_ref, sinkc_ref, o_ref, y_ref, bias_scr):
        n = pl.program_id(0)
        kk, from_prev, _, _ = _swa_common(n, kv_ref, bucket_ref, relb_ref, bias_scr)
        ops = _swa_operands(kk)
        pr, _ = _swa_probs(n, q_ref, ops, bias_scr, sinkc_ref, from_prev)
        for kvh in range(SWA_KV_HEADS):
            _, (va, vb) = ops[kvh]
            for p in range(4):
                c0 = kvh * 512 + p * 128
                r0 = (kvh * 8 + 2 * p) * WINDOW
                o2 = (_dot(_unfold(pr[r0:r0 + WINDOW], from_prev).astype(BF), va, NN)
                      + _dot(_unfold(pr[r0 + WINDOW:r0 + 2 * WINDOW], from_prev).astype(BF), vb, NN))
                o_ref[:, c0:c0 + 128] = o2
                gv = g_ref[:, c0:c0 + 128]
                y_ref[:, c0:c0 + 128] = (o2 * (gv * _sigmoid(gv))).astype(BF)

    blk = pl.BlockSpec((WINDOW, 1024), lambda n: (n, 0))
    smem = pl.BlockSpec(memory_space=pltpu.SMEM)
    sinkc = pl.BlockSpec((ALL_HEADS, 1), lambda n: (0, 0))
    return pl.pallas_call(
        body, name="swa_fwd", grid=(nb,),
        in_specs=[blk, pl.BlockSpec((S, 256), lambda n: (0, 0)), blk, pl.BlockSpec((WINDOW, WINDOW), lambda n: (0, 0)), smem, sinkc],
        out_specs=[blk, blk],
        out_shape=[jax.ShapeDtypeStruct((S, 1024), F32), jax.ShapeDtypeStruct((S, 1024), BF)],
        scratch_shapes=[pltpu.VMEM((SWA_HEADS, WINDOW, WINDOW), F32)],
        compiler_params=_params(("arbitrary",)),
    )(q, kv, g, bucket, rel_bias, sink_col)


def _swa_bwd(q, kv, g, o, dy, bucket, rel_bias, sink_col, others):
    S = q.shape[0]
    nb = S // WINDOW
    first_col = {name: c0 for name, c0, _, _ in SEGMENTS}
    col_q, col_g = first_col["q_s"], first_col["g_swa"]
    copies = []
    for name, c0, width, _ in SEGMENTS:
        if name not in ("q_s", "kv", "g_swa"):
            arrs = others[name] if name == "gl" else (others[name],)
            copies += [(a, c0 + i * (width // len(arrs))) for i, a in enumerate(arrs)]

    def body(q_ref, kv_ref, g_ref, o_ref, dy_ref, bucket_ref, relb_ref, sinkc_ref, *refs):
        copy_refs = refs[:len(copies)]
        dp_ref, dkv_ref, dsink_ref, drel_ref, bias_scr, dbias_scr, dsink_scr = refs[len(copies):]
        n = pl.program_id(0)
        for c_ref, (a, c0) in zip(copy_refs, copies):
            dp_ref[:, c0:c0 + a.shape[1]] = c_ref[...]

        @pl.when(n == 0)
        def _():
            dbias_scr[...] = jnp.zeros_like(dbias_scr)
            dsink_scr[...] = jnp.zeros_like(dsink_scr)
            dkv_ref[...] = jnp.zeros_like(dkv_ref)

        kk, from_prev, prev0, cur0 = _swa_common(n, kv_ref, bucket_ref, relb_ref, bias_scr)
        ops = _swa_operands(kk)
        pr, ps = _swa_probs(n, q_ref, ops, bias_scr, sinkc_ref, from_prev)
        do2s, dps = [], []
        for kvh in range(SWA_KV_HEADS):
            _, (va, vb) = ops[kvh]
            for p in range(4):
                c0 = kvh * 512 + p * 128
                gv = g_ref[:, c0:c0 + 128]
                sg = _sigmoid(gv)
                dyv = dy_ref[:, c0:c0 + 128]
                dp_ref[:, col_g + c0:col_g + c0 + 128] = (dyv * o_ref[:, c0:c0 + 128] * (sg * (1.0 + gv * (1.0 - sg)))).astype(BF)
                do2 = (dyv * (gv * sg)).astype(BF)
                do2s.append(do2)
                dps += [_fold(_dot(do2, va, NT), from_prev), _fold(_dot(do2, vb, NT), from_prev)]
        dp = jnp.concatenate(dps, axis=0)
        delta = jnp.sum(pr * dp, axis=-1, keepdims=True)
        ds = pr * (dp - delta)
        dbias_scr[...] += ds.reshape(SWA_HEADS, WINDOW, WINDOW)
        dsink_scr[...] += ps * delta
        dsc = ds * (SWA_HD ** -0.5)
        lo256 = lax.broadcasted_iota(jnp.int32, (2 * WINDOW, 128), 1) < SWA_HD
        dks, dvs = [], []
        for kvh in range(SWA_KV_HEADS):
            (ka, kb), _ = ops[kvh]
            dka = jnp.zeros((2 * WINDOW, 128), F32)
            dkb, dva, dvb = dka, dka, dka
            for p in range(4):
                c0 = kvh * 512 + p * 128
                r0 = (kvh * 8 + 2 * p) * WINDOW
                q2 = q_ref[:, c0:c0 + 128]
                do2 = do2s[kvh * 4 + p]
                ds0 = _unfold(dsc[r0:r0 + WINDOW], from_prev).astype(BF)
                ds1 = _unfold(dsc[r0 + WINDOW:r0 + 2 * WINDOW], from_prev).astype(BF)
                dp_ref[:, col_q + c0:col_q + c0 + 128] = (_dot(ds0, ka, NN) + _dot(ds1, kb, NN)).astype(BF)
                dka = dka + _dot(ds0, q2, TN)
                dkb = dkb + _dot(ds1, q2, TN)
                dva = dva + _dot(_unfold(pr[r0:r0 + WINDOW], from_prev).astype(BF), do2, TN)
                dvb = dvb + _dot(_unfold(pr[r0 + WINDOW:r0 + 2 * WINDOW], from_prev).astype(BF), do2, TN)
            dks.append(jnp.where(lo256, dka, 0.0) + pltpu.roll(jnp.where(lo256, 0.0, dkb), SWA_HD, 1))
            dvs.append(jnp.where(lo256, dva, 0.0) + pltpu.roll(jnp.where(lo256, 0.0, dvb), SWA_HD, 1))
        dk = dks[0] + pltpu.roll(dks[1], SWA_HD, 1)
        dv = dvs[0] + pltpu.roll(dvs[1], SWA_HD, 1)
        dkv_ref[pl.ds(prev0, WINDOW), 0:128] += dk[:WINDOW]
        dkv_ref[pl.ds(prev0, WINDOW), 128:256] += dv[:WINDOW]
        dkv_ref[pl.ds(cur0, WINDOW), 0:128] += dk[WINDOW:]
        dkv_ref[pl.ds(cur0, WINDOW), 128:256] += dv[WINDOW:]

        @pl.when(n == nb - 1)
        def _():
            dsink_ref[...] = -jnp.sum(dsink_scr[...].reshape(SWA_HEADS, WINDOW, 1), axis=1)
            bk = bucket_ref[...]
            sums = []
            for b in range(REL_BUCKETS):
                sums.append(jnp.sum(jnp.where((bk == b)[None], dbias_scr[...], 0.0), axis=1))
            drel_ref[...] = jnp.sum(jnp.concatenate(sums, axis=0), axis=1, keepdims=True)

    blk = pl.BlockSpec((WINDOW, 1024), lambda n: (n, 0))
    smem = pl.BlockSpec(memory_space=pltpu.SMEM)
    whole = lambda shape: pl.BlockSpec(shape, lambda n: (0, 0))
    return pl.pallas_call(
        body, name="swa_bwd", grid=(nb,),
        in_specs=[blk, whole((S, 256)), blk, blk, blk, whole((WINDOW, WINDOW)), smem, whole((ALL_HEADS, 1))]
        + [pl.BlockSpec((WINDOW, a.shape[1]), lambda n: (n, 0)) for a, _ in copies],
        out_specs=[pl.BlockSpec((WINDOW, D_IN), lambda n: (n, 0)), whole((S, 256)), whole((SWA_HEADS, 1)),
                   whole((REL_BUCKETS * SWA_HEADS, 1))],
        out_shape=[jax.ShapeDtypeStruct((S, D_IN), BF), jax.ShapeDtypeStruct((S, 256), F32), jax.ShapeDtypeStruct((SWA_HEADS, 1), F32),
                   jax.ShapeDtypeStruct((REL_BUCKETS * SWA_HEADS, 1), F32)],
        scratch_shapes=[pltpu.VMEM((SWA_HEADS, WINDOW, WINDOW), F32), pltpu.VMEM((SWA_HEADS, WINDOW, WINDOW), F32),
                        pltpu.VMEM((ALL_HEADS, 1), F32)],
        compiler_params=_params(("arbitrary",)),
    )(q, kv, g, o, dy, bucket, rel_bias, sink_col, *[a for a, _ in copies])


def _mem_probs(qh, mk):
    lg = _dot(qh, mk, NT) * (MEM_HD ** -0.5)
    e = jnp.exp(lg - jnp.max(lg, axis=-1, keepdims=True))
    return e / jnp.sum(e, axis=-1, keepdims=True)


def _mem_fwd(q, mkv, g):
    S = q.shape[0]
    M = mkv.shape[0]
    tq = 256

    def body(q_ref, mkv_ref, g_ref, o_ref, y_ref):
        for h in range(MEM_HEADS):
            c0 = h * MEM_HD
            pr = _mem_probs(q_ref[:, c0:c0 + MEM_HD], mkv_ref[:, c0:c0 + MEM_HD])
            o = _dot(pr.astype(BF), mkv_ref[:, D_MEM + c0:D_MEM + c0 + MEM_HD], NN)
            o_ref[:, c0:c0 + MEM_HD] = o
            gv = g_ref[:, c0:c0 + MEM_HD]
            y_ref[:, c0:c0 + MEM_HD] = (o * (gv * _sigmoid(gv))).astype(BF)

    blk = pl.BlockSpec((tq, D_MEM), lambda i: (i, 0))
    return pl.pallas_call(
        body, name="mem_fwd", grid=(S // tq,),
        in_specs=[blk, pl.BlockSpec((M, 2 * D_MEM), lambda i: (0, 0)), blk], out_specs=[blk, blk],
        out_shape=[jax.ShapeDtypeStruct((S, D_MEM), F32), jax.ShapeDtypeStruct((S, D_MEM), BF)],
        compiler_params=_params(("parallel",)),
    )(q, mkv, g)


def _mem_bwd(q, mkv, g, o, dy):
    S = q.shape[0]
    M = mkv.shape[0]
    tq = 256

    def body(q_ref, mkv_ref, g_ref, o_ref, dy_ref, dq_ref, dg_ref, dmkv_ref):
        @pl.when(pl.program_id(0) == 0)
        def _():
            dmkv_ref[...] = jnp.zeros_like(dmkv_ref)

        for h in range(MEM_HEADS):
            c0 = h * MEM_HD
            qh = q_ref[:, c0:c0 + MEM_HD]
            mk = mkv_ref[:, c0:c0 + MEM_HD]
            mv = mkv_ref[:, D_MEM + c0:D_MEM + c0 + MEM_HD]
            gv = g_ref[:, c0:c0 + MEM_HD]
            sg = _sigmoid(gv)
            dyv = dy_ref[:, c0:c0 + MEM_HD]
            dg_ref[:, c0:c0 + MEM_HD] = (dyv * o_ref[:, c0:c0 + MEM_HD] * (sg * (1.0 + gv * (1.0 - sg)))).astype(BF)
            do = (dyv * (gv * sg)).astype(BF)
            pr = _mem_probs(qh, mk)
            dp = _dot(do, mv, NT)
            ds = pr * (dp - jnp.sum(pr * dp, axis=-1, keepdims=True))
            dsb = (ds * (MEM_HD ** -0.5)).astype(BF)
            dq_ref[:, c0:c0 + MEM_HD] = _dot(dsb, mk, NN).astype(BF)
            dmkv_ref[:, c0:c0 + MEM_HD] += _dot(dsb, qh, TN)
            dmkv_ref[:, D_MEM + c0:D_MEM + c0 + MEM_HD] += _dot(pr.astype(BF), do, TN)

    blk = pl.BlockSpec((tq, D_MEM), lambda i: (i, 0))
    whole = pl.BlockSpec((M, 2 * D_MEM), lambda i: (0, 0))
    return pl.pallas_call(
        body, name="mem_bwd", grid=(S // tq,),
        in_specs=[blk, whole, blk, blk, blk], out_specs=[blk, blk, whole],
        out_shape=[jax.ShapeDtypeStruct((S, D_MEM), BF), jax.ShapeDtypeStruct((S, D_MEM), BF),
                   jax.ShapeDtypeStruct((M, 2 * D_MEM), F32)],
        compiler_params=_params(("arbitrary",)),
    )(q, mkv, g, o, dy)


MERGE_TN = 512


def _merge_specs(tm):
    ytile = pl.BlockSpec((tm, 1024), lambda i, j: (i, 0))
    wblk = pl.BlockSpec((MERGE_TN, 1024), lambda i, j: (j, 0))
    gls = [pl.BlockSpec((None, tm, MERGE_TN), (lambda i, j, br=br: (br, i, j))) for br in range(3)]
    otile = pl.BlockSpec((tm, MERGE_TN), lambda i, j: (i, j))
    return ytile, wblk, gls, otile


def _merge_fwd(ys, ws, gl, tm):
    S = gl.shape[1]

    def body(y0, y1, y2, w0, w1, w2, g0, g1, g2, o_ref):
        acc = None
        for y_ref, w_ref, g_ref in ((y0, w0, g0), (y1, w1, g1), (y2, w2, g2)):
            term = _sigmoid(g_ref[...]) * _dot(y_ref[...], w_ref[...], NT)
            acc = term if acc is None else acc + term
        o_ref[...] = acc.astype(BF)

    ytile, wblk, gls, otile = _merge_specs(tm)
    return pl.pallas_call(
        body, name="merge_fwd", grid=(S // tm, D_MODEL // MERGE_TN),
        in_specs=[ytile] * 3 + [wblk] * 3 + gls, out_specs=otile,
        out_shape=jax.ShapeDtypeStruct((S, D_MODEL), BF),
        compiler_params=_params(("parallel", "arbitrary")),
    )(*ys, *ws, gl, gl, gl)


def _merge_bwd(dout, w_out, ys, ws, gl, tm):
    S = gl.shape[1]

    def body(do_ref, wo_ref, y0, y1, y2, w0, w1, w2, g0, g1, g2, dg0, dg1, dg2, dp0, dp1, dp2):
        dm = _dot(do_ref[...], wo_ref[...], NT)
        for y_ref, w_ref, g_ref, dg_ref, dp_ref in ((y0, w0, g0, dg0, dp0), (y1, w1, g1, dg1, dp1), (y2, w2, g2, dg2, dp2)):
            gate = _sigmoid(g_ref[...])
            pv = _dot(y_ref[...], w_ref[...], NT)
            dg_ref[...] = (dm * pv * gate * (1.0 - gate)).astype(BF)
            dp_ref[...] = (dm * gate).astype(BF)

    ytile, wblk, gls, otile = _merge_specs(tm)
    out = jax.ShapeDtypeStruct((S, D_MODEL), BF)
    return pl.pallas_call(
        body, name="merge_bwd", grid=(S // tm, D_MODEL // MERGE_TN),
        in_specs=[pl.BlockSpec((tm, D_MODEL), lambda i, j: (i, 0)), pl.BlockSpec((MERGE_TN, D_MODEL), lambda i, j: (j, 0))]
        + [ytile] * 3 + [wblk] * 3 + gls,
        out_specs=[otile] * 6, out_shape=[out] * 6,
        compiler_params=_params(("parallel", "arbitrary")),
    )(dout, w_out, *ys, *ws, gl, gl, gl)


def _out_loss(merged, w_out, x, target, post_g, tm):
    S = x.shape[0]

    def body(m_ref, w_ref, x_ref, t_ref, g_ref, dout_ref, dy_ref, loss_ref, dpost_ref):
        @pl.when(pl.program_id(0) == 0)
        def _():
            loss_ref[...] = jnp.zeros_like(loss_ref)
            dpost_ref[...] = jnp.zeros_like(dpost_ref)

        out = _dot(m_ref[...], w_ref[...], NN)
        r = lax.rsqrt(jnp.mean(out * out, axis=-1, keepdims=True) + EPS)
        nrm = out * r
        gv = g_ref[...]
        err = (x_ref[...] + nrm * gv) - t_ref[...]
        sq = jnp.sum(jnp.sum(err * err, axis=1, keepdims=True), axis=0, keepdims=True)
        loss_ref[...] += sq * (0.5 / D_MODEL)
        dy = err * (1.0 / D_MODEL)
        dy_ref[...] = dy
        dpost_ref[...] += jnp.sum(dy * nrm, axis=0, keepdims=True)
        dn = dy * gv
        dout_ref[...] = (r * (dn - nrm * jnp.mean(dn * nrm, axis=-1, keepdims=True))).astype(BF)

    row = pl.BlockSpec((tm, D_MODEL), lambda i: (i, 0))
    return pl.pallas_call(
        body, name="out_loss", grid=(S // tm,),
        in_specs=[row, pl.BlockSpec((D_MODEL, D_MODEL), lambda i: (0, 0)), row, row, pl.BlockSpec((1, D_MODEL), lambda i: (0, 0))],
        out_specs=[row, row, pl.BlockSpec((8, 128), lambda i: (0, 0)), pl.BlockSpec((1, D_MODEL), lambda i: (0, 0))],
        out_shape=[jax.ShapeDtypeStruct((S, D_MODEL), BF), jax.ShapeDtypeStruct((S, D_MODEL), F32),
                   jax.ShapeDtypeStruct((8, 128), F32), jax.ShapeDtypeStruct((1, D_MODEL), F32)],
        compiler_params=_params(("arbitrary",)),
    )(merged, w_out, x, target, post_g)


DH_DX_CHUNK = 64


def _dh_dx(dproj, w_in, x, dy, pre_g, tm, tk):
    S = x.shape[0]
    nk = D_IN // tk

    def body(dp_ref, w_ref, x_ref, dy_ref, g_ref, dx_ref, dpre_ref, acc_ref):
        i, k = pl.program_id(0), pl.program_id(1)

        @pl.when(jnp.logical_and(i == 0, k == 0))
        def _():
            dpre_ref[...] = jnp.zeros_like(dpre_ref)

        @pl.when(k == 0)
        def _():
            acc_ref[...] = jnp.zeros_like(acc_ref)

        acc_ref[...] += _dot(dp_ref[...], w_ref[...], NN)

        @pl.when(k == nk - 1)
        def _():
            def chunk(c, carry):
                rows = pl.ds(pl.multiple_of(c * DH_DX_CHUNK, DH_DX_CHUNK), DH_DX_CHUNK)
                dh = acc_ref[rows, :]
                xv = x_ref[rows, :]
                r = lax.rsqrt(jnp.mean(xv * xv, axis=-1, keepdims=True) + EPS)
                nrm = xv * r
                dpre_ref[...] += jnp.sum(dh * nrm, axis=0, keepdims=True)
                dn = dh * g_ref[...]
                dx_ref[rows, :] = r * (dn - nrm * jnp.mean(dn * nrm, axis=-1, keepdims=True)) + dy_ref[rows, :]
                return carry
            lax.fori_loop(0, tm // DH_DX_CHUNK, chunk, 0)

    row = pl.BlockSpec((tm, D_MODEL), lambda i, k: (i, 0))
    vec = pl.BlockSpec((1, D_MODEL), lambda i, k: (0, 0))
    return pl.pallas_call(
        body, name="dh_dx", grid=(S // tm, nk),
        in_specs=[pl.BlockSpec((tm, tk), lambda i, k: (i, k)), pl.BlockSpec((tk, D_MODEL), lambda i, k: (k, 0)), row, row, vec],
        out_specs=[row, vec],
        out_shape=[jax.ShapeDtypeStruct((S, D_MODEL), F32), jax.ShapeDtypeStruct((1, D_MODEL), F32)],
        scratch_shapes=[pltpu.VMEM((tm, D_MODEL), F32)],
        compiler_params=_params(("arbitrary", "arbitrary"), large=True),
    )(dproj, w_in, x, dy, pre_g)


def _sum_parts(parts, name):
    P, R, C = parts.shape
    tr = max(t for t in range(8, 513, 8) if R % t == 0)

    def body(p_ref, o_ref):
        acc = p_ref[0]
        for j in range(1, P):
            acc = acc + p_ref[j]
        o_ref[...] = acc

    return pl.pallas_call(
        body, name=name, grid=(R // tr,),
        in_specs=[pl.BlockSpec((P, tr, C), lambda i: (0, i, 0))], out_specs=pl.BlockSpec((tr, C), lambda i: (i, 0)),
        out_shape=jax.ShapeDtypeStruct((R, C), F32), compiler_params=_params(("parallel",)),
    )(parts)


def _adamw(land, sums, chip, w, m, v, name, group=(0, 1), into=None):
    q, n_groups = group
    _, R, cols = land.shape
    C = cols * n_groups
    tr = max(t for t in range(16, 257, 16) if R % t == 0)
    c1 = 1.0 - ADAM_B1 ** ADAM_STEP
    c2 = 1.0 - ADAM_B2 ** ADAM_STEP
    n_into = 0 if into is None else 4

    def body(chip_ref, p0_ref, p1_ref, p2_ref, own_ref, w_ref, m_ref, v_ref, *refs):
        g_ref, d_ref, nm_ref, nv_ref = refs[n_into:]
        g = own_ref[...].astype(F32)
        for p_ref in (p0_ref, p1_ref, p2_ref):
            g = g + p_ref[...].astype(F32)
        nm = ADAM_B1 * m_ref[...] + (1.0 - ADAM_B1) * g
        nv = ADAM_B2 * v_ref[...] + (1.0 - ADAM_B2) * (g * g)
        g_ref[...] = g
        nm_ref[...] = nm
        nv_ref[...] = nv
        d_ref[...] = -ADAM_LR * ((nm / c1) / (jnp.sqrt(nv / c2) + ADAM_EPS) + ADAM_WD * w_ref[...])

    tile = pl.BlockSpec((None, tr, cols), lambda i, c_ref: (0, i, q))
    specs = [pl.BlockSpec((None, tr, cols), (lambda i, c_ref, k=k: (k + (c_ref[0] <= k).astype(jnp.int32), i, 0))) for k in range(3)]
    specs.append(pl.BlockSpec((None, tr, cols), (lambda i, c_ref: (c_ref[0], i, 0))))
    return pl.pallas_call(
        body, name=name,
        grid_spec=pltpu.PrefetchScalarGridSpec(num_scalar_prefetch=1, grid=(R // tr,),
                                               in_specs=specs + [tile, tile, tile] + [pl.BlockSpec(memory_space=pl.ANY)] * n_into,
                                               out_specs=[tile] * 4),
        out_shape=[jax.ShapeDtypeStruct((1, R, C), F32)] * 4,
        input_output_aliases={8 + k: k for k in range(n_into)},
        compiler_params=_params(("parallel",)),
    )(chip, *[_in_hbm(land)] * 3, sums, w, m, v, *(into or []))


def _adamw_small(gs, ws, ms, vs):
    n = len(ws)
    c1 = 1.0 - ADAM_B1 ** ADAM_STEP
    c2 = 1.0 - ADAM_B2 ** ADAM_STEP

    def flat2(a):
        return a.reshape(-1, a.shape[-1])

    def body(*refs):
        ins, outs = refs[:4 * n], refs[4 * n:]
        for a in range(n):
            g, w, m, v = (ins[k * n + a][...] for k in range(4))
            nm = ADAM_B1 * m + (1.0 - ADAM_B1) * g
            nv = ADAM_B2 * v + (1.0 - ADAM_B2) * (g * g)
            outs[a][...] = g
            outs[n + a][...] = -ADAM_LR * ((nm / c1) / (jnp.sqrt(nv / c2) + ADAM_EPS) + ADAM_WD * w)
            outs[2 * n + a][...] = nm
            outs[3 * n + a][...] = nv

    shapes = [flat2(w).shape for w in ws]
    out = pl.pallas_call(
        body, name="adamw_small", out_shape=[jax.ShapeDtypeStruct(sh, F32) for sh in shapes] * 4,
        compiler_params=_params(),
    )(*[g.reshape(sh) for g, sh in zip(gs, shapes)], *[flat2(a) for a in (*ws, *ms, *vs)])
    return [[out[k * n + a].reshape(ws[a].shape) for a in range(n)] for k in range(4)]


PROJ_ROWS = 512


def _project(h, w_t, dep=None):
    S = h.shape[0]
    n_tiles = D_IN // SEG_TILE
    ranges = [(c0 // SEG_TILE, (c0 + width) // SEG_TILE) for _, c0, width, _ in SEGMENTS]
    dtypes = (F32, BF)
    n_extra = int(dep is not None)

    def of_dtype(j, dt):
        hit = False
        for (j0, j1), seg in zip(ranges, SEGMENTS):
            if seg[3] == dt:
                hit = jnp.logical_and(j >= j0, j < j1) | hit
        return hit

    def body(h_ref, w_ref, *refs):
        outs = refs[n_extra:n_extra + len(SEGMENTS)]
        stages, sems = refs[n_extra + len(SEGMENTS):-1], refs[-1]
        j = pl.program_id(0)
        slot = j % 2

        def copy_out(k, stage, dst):
            return pltpu.make_async_copy(stages[k].at[stage], dst, sems.at[stage])

        def wait_tile(jj, stage):
            for k, dt in enumerate(dtypes):
                o_ref = [o for o, seg in zip(outs, SEGMENTS) if seg[3] == dt and seg[0] != "gl"][0]
                @pl.when(of_dtype(jj, dt))
                def _(k=k, o_ref=o_ref):
                    copy_out(k, stage, o_ref.at[:, pl.ds(0, SEG_TILE)]).wait()

        @pl.when(j >= 2)
        def _():
            wait_tile(j - 2, slot)

        for k, dt in enumerate(dtypes):
            @pl.when(of_dtype(j, dt))
            def _(k=k, dt=dt):
                for c in range(S // PROJ_ROWS):
                    rows = pl.ds(c * PROJ_ROWS, PROJ_ROWS)
                    stages[k][slot, rows, :] = _dot(h_ref[rows, :], w_ref[...], NT).astype(dt)

        for (j0, j1), (name, _, _, dt), o_ref in zip(ranges, SEGMENTS, outs):
            @pl.when(jnp.logical_and(j >= j0, j < j1))
            def _(j0=j0, j1=j1, name=name, dt=dt, o_ref=o_ref):
                t = j - j0
                if name == "gl":
                    per = (j1 - j0) // 3
                    dst = o_ref.at[t // per, :, pl.ds(pl.multiple_of((t % per) * SEG_TILE, SEG_TILE), SEG_TILE)]
                else:
                    dst = o_ref.at[:, pl.ds(pl.multiple_of(t * SEG_TILE, SEG_TILE), SEG_TILE)]
                copy_out(dtypes.index(dt), slot, dst).start()

        @pl.when(j == n_tiles - 1)
        def _():
            wait_tile(j - 1, 1 - slot)
            wait_tile(j, slot)

    out_shapes = [jax.ShapeDtypeStruct((3, S, width // 3) if name == "gl" else (S, width), dt) for name, _, width, dt in SEGMENTS]
    outs = pl.pallas_call(
        body, name="proj", grid=(n_tiles,),
        in_specs=[pl.BlockSpec((S, D_MODEL), lambda j: (0, 0)), pl.BlockSpec((SEG_TILE, D_MODEL), lambda j: (j, 0))]
        + ([] if dep is None else [pl.BlockSpec((8, 128), lambda j: (0, 0))]),
        out_specs=[pl.BlockSpec(memory_space=pl.ANY)] * len(SEGMENTS), out_shape=out_shapes,
        scratch_shapes=[pltpu.VMEM((2, S, SEG_TILE), dt) for dt in dtypes] + [pltpu.SemaphoreType.DMA((2,))],
        compiler_params=_params(("arbitrary",), large=True),
    )(h, w_t, *([] if dep is None else [dep]))
    return {name: o for (name, _, _, _), o in zip(SEGMENTS, outs)}


def _forward_a(h, memn, w_in, sinks, rel_bias, dep=None):
    S = h.shape[0]
    st = dict(T=min(512, S // 2), tm=min(512, S), bucket=_rel_bucket_map(), h=h, memn=memn)
    seg = st["seg"] = _project(h, w_in, dep)
    st["o_swa"], st["y_swa"] = _swa_fwd(seg["q_s"], seg["kv"], seg["g_swa"], st["bucket"], rel_bias, _sink_column(sinks))
    return st


def _forward_rg(st, conv_w, conv_b, w_a, b_a, w_x, b_x, lam):
    seg = st["seg"]
    st["h_rg"], st["y_rg"] = _rglru_fwd(seg["xr"], seg["g_rg"], conv_w, conv_b, w_a, b_a, w_x, b_x, lam, st["T"])
    return st


def _forward_b(st, x, target, post_g, w_memkv, wbr, w_out):
    S = x.shape[0]
    M = st["memn"].shape[0]
    seg = st["seg"]
    st["mkv"] = _matmul(st["memn"], w_memkv, "nn", M, 2 * D_MEM, D_MODEL, M, 512, D_MODEL, BF, "mem_kv")
    st["o_mem"], st["y_mem"] = _mem_fwd(seg["q_m"], st["mkv"], seg["g_mem"])
    st["ys"] = (st["y_rg"], st["y_swa"], st["y_mem"])
    st["merged"] = _merge_fwd(st["ys"], wbr, seg["gl"], st["tm"])
    st["dout"], st["dy"], st["loss"], st["dpost"] = _out_loss(st["merged"], w_out, x, target, post_g, min(256, S))
    return st


def _backward_a1(st, wbr, w_out):
    S = st["h"].shape[0]
    seg, ys, tm = st["seg"], st["ys"], st["tm"]
    st["dw_out"] = _matmul(st["merged"], st["dout"], "tn", D_MODEL, D_MODEL, S, 256, D_MODEL, S, BF, "dw_out", out_blocked="row")
    dgl0, dgl1, dgl2, dp0, dp1, dp2 = _merge_bwd(st["dout"], w_out, ys, wbr, seg["gl"], tm)
    st["dgl"] = (dgl0, dgl1, dgl2)
    dys, dwbr = [], []
    for i, dp in enumerate((dp0, dp1, dp2)):
        dys.append(_matmul(dp, wbr[i], "nn", S, 1024, D_MODEL, tm, 1024, D_MODEL, F32, "dy_br%d" % i))
        dwbr.append(_matmul(ys[i], dp, "tn", 1024, D_MODEL, S, 1024, 256, S, BF, "dw_br%d" % i, out_blocked="col"))
    st["dys"], st["dwbr"] = dys, dwbr
    return st


def _backward_a2(st, mem, w_memkv, conv_w, conv_b, w_a, b_a, w_x, b_x, lam):
    M = mem.shape[0]
    seg, dys = st["seg"], st["dys"]
    st["dq_m"], st["dg_mem"], dmkv = _mem_bwd(seg["q_m"], st["mkv"], seg["g_mem"], st["o_mem"], dys[2])
    st["dmkv"] = dmkv.astype(BF)
    st["dw_memkv"] = _matmul(st["memn"], st["dmkv"], "tn", D_MODEL, 2 * D_MEM, M, 256, 2 * D_MEM, M, BF, "dw_memkv", out_blocked="row")
    st["dxr"], st["dg_rg"], st["dw_a"], st["dw_x"], st["dvec"] = _rglru_bwd(
        seg["xr"], seg["g_rg"], st["h_rg"], dys[0], conv_w, conv_b, w_a, b_a, w_x, b_x, lam, st["T"])
    return st


def _mem_gain_grad(st, mem, w_memkv, dep=None):
    M = mem.shape[0]
    dmemn = _matmul(st["dmkv"], w_memkv, "nt", M, D_MODEL, 2 * D_MEM, M, 512, 2 * D_MEM, F32, "dmemn", dep=dep)
    return _rms_gain_grad(dmemn, mem, "dmem_gain")


def _backward_b(st, rel_bias, sinks):
    seg = st["seg"]
    others = {"xr": st["dxr"], "g_rg": st["dg_rg"], "q_m": st["dq_m"], "g_mem": st["dg_mem"], "gl": st["dgl"]}
    dproj, dkv, dsinks, drel = _swa_bwd(seg["q_s"], seg["kv"], seg["g_swa"], st["o_swa"], st["dys"][1],
                                        st["bucket"], rel_bias, _sink_column(sinks), others)
    st["dsinks"], st["drel"] = dsinks.reshape(1, SWA_HEADS), drel.reshape(REL_BUCKETS, SWA_HEADS)
    col_kv = [c0 for name, c0, _, _ in SEGMENTS if name == "kv"][0]
    st["dproj"] = lax.dynamic_update_slice(dproj, dkv.astype(BF), (0, col_kv))
    return st


def _dw_in_half(st, half, dep=None):
    S = st["h"].shape[0]
    dw = _matmul(st["dproj"], st["h"], "tn", D_IN, D_MODEL // 2, S, D_IN_TILE, D_MODEL // 2, S, BF, "dw_in%d" % half, b_noff=half, dep=dep, out_hbm=True)
    return dw.reshape(N_DEV, D_IN // N_DEV, D_MODEL // 2)


def _owner_blocks(a):
    return jnp.swapaxes(a.reshape((4, 2) + a.shape[1:]), 0, 1)


def _pad_rows(a, rows):
    a = a.reshape(-1, 128) if a.shape[-1] % 128 == 0 else jnp.pad(a, ((0, 0), (0, 128 - a.shape[-1])))
    return jnp.pad(a, ((0, rows - a.shape[0]), (0, 0))) if a.shape[0] < rows else a


def kernel(x, mem, pre_norm_g, post_norm_g, mem_norm_g, w_in, conv_w, conv_b, w_rg_a, b_rg_a, w_rg_x, b_rg_x, lru_lambda, swa_sinks, rel_bias, w_mem_kv, w_br_rg, w_br_swa, w_br_mem, w_out, loss_target, m_pre_norm_g, m_post_norm_g, m_mem_norm_g, m_w_in, m_conv_w, m_conv_b, m_w_rg_a, m_b_rg_a, m_w_rg_x, m_b_rg_x, m_lru_lambda, m_swa_sinks, m_rel_bias, m_w_mem_kv, m_w_br_rg, m_w_br_swa, m_w_br_mem, m_w_out, v_pre_norm_g, v_post_norm_g, v_mem_norm_g, v_w_in, v_conv_w, v_conv_b, v_w_rg_a, v_b_rg_a, v_w_rg_x, v_b_rg_x, v_lru_lambda, v_swa_sinks, v_rel_bias, v_w_mem_kv, v_w_br_rg, v_w_br_swa, v_w_br_mem, v_w_out):
    cx, cy, cc = lax.axis_index("x"), lax.axis_index("y"), lax.axis_index("c")
    me = 4 * cx + 2 * cy + cc
    chip = 2 * cx + cy
    core = jnp.reshape(cc, (1,)).astype(jnp.int32)
    x0, mem0 = x[0], mem[0]
    w_a_b, w_x_b = w_rg_a[0].astype(BF), w_rg_x[0].astype(BF)

    def landing(own, slot, slots):
        return lax.dynamic_update_slice(lax.empty((slots,) + own.shape, own.dtype), own[None], (slot,) + (0,) * own.ndim)


    def swap_start(parts, tag):
        return _exchange_start(parts, [lax.empty((4,) + p.shape[-2:], p.dtype) for p in parts], _plan_swap([p.ndim for p in parts]),
                               "swap_%s_start" % tag)

    def scatter_start(swap, after, tag, prefill=()):
        s_send, s_recv, parts, got, _ = swap
        got = _exchange_wait(s_send, s_recv, parts, got, _plan_swap([p.ndim for p in parts]), after, "swap_%s_wait" % tag)
        sums = [_pair_sum(p, g, core, "scatter_%s_sum%d" % (tag, i)) for i, (p, g) in enumerate(zip(parts, got))]
        lands = [landing(lax.dynamic_index_in_dim(s, chip, 0, keepdims=False), chip, 4) if i in prefill
                 else lax.empty(s.shape, s.dtype) for i, s in enumerate(sums)]
        return _exchange_start(sums, lands, _plan_scatter(len(sums)), "scatter_%s_start" % tag)

    def corner(a):
        return a.reshape(-1, a.shape[-1])[:8, :128]

    def zero_after(a):
        return jnp.minimum(jnp.abs(a.reshape(-1)[0].astype(F32)), 0.0)

    g_in, g_cw, h0, memn0 = _all_gather_relayed([jnp.transpose(w_in[0]).astype(BF), conv_w[0]], [True, False], "gather_w_in",
                                                side=_rms_side([(x0, pre_norm_g), (mem0, mem_norm_g)]))
    w_in_f = g_in.reshape(D_IN, D_MODEL)
    conv_w_f = jnp.transpose(g_cw, (1, 0, 2)).reshape(CONV_W, D_RNN)

    after_first = zero_after(g_cw).astype(BF)
    rest = [w.astype(BF) + after_first for w in (w_mem_kv[0], jnp.transpose(w_br_rg[0]), jnp.transpose(w_br_swa[0]),
                                                 jnp.transpose(w_br_mem[0]), w_out[0])]
    plan_g = _plan_gather(len(rest))
    zones = _place_own([lax.empty((N_DEV,) + w.shape, w.dtype) for w in rest], rest, jnp.reshape(me, (1,)).astype(jnp.int32), "gather_rest_own")
    g_send, g_recv, g_src, g_land, g_token = _exchange_start(rest, zones, plan_g, "gather_rest_start")
    st = _forward_a(h0, memn0, w_in_f, swa_sinks, rel_bias, dep=g_token)
    g_land = _exchange_wait(g_send, g_recv, g_src, g_land, plan_g, st["y_swa"], "gather_rest_wait")
    plan_f = _plan_forward(len(rest))
    f_send, f_recv, _, g_land, f_token = _exchange_start(None, g_land, plan_f, "forward_rest_start")
    st = _forward_rg(st, conv_w_f, conv_b + f_token[0:1, 0:1], w_a_b, b_rg_a, w_x_b, b_rg_x, lru_lambda)
    g_land = _exchange_wait(f_send, f_recv, None, g_land, plan_f, corner(st["y_rg"]), "forward_rest_wait")
    w_memkv_f = g_land[0].reshape(D_MODEL, 2 * D_MEM)
    wbr = tuple(g_land[i].reshape(D_MODEL, D_RNN) for i in (1, 2, 3))
    w_out_f = g_land[4].reshape(D_MODEL, D_MODEL)

    st = _forward_b(st, x0, loss_target[0], post_norm_g, w_memkv_f, wbr, w_out_f)
    st = _backward_a1(st, wbr, w_out_f)
    parts_a = [st["dw_out"], st["dwbr"][0], st["dwbr"][1], st["dwbr"][2]]
    plan_a = _plan_scatter(len(parts_a))
    swap_a = swap_start(parts_a, "a")
    st = _backward_a2(st, mem0, w_memkv_f, conv_w_f, conv_b + swap_a[4][0:1, 0:1], w_a_b, b_rg_a, w_x_b, b_rg_x, lru_lambda)
    a_send, a_recv, a_src, a_land, a_token = scatter_start(swap_a, st["dxr"], "a")
    parts_c = [st["dw_memkv"], _owner_blocks(st["dw_a"]), _owner_blocks(st["dw_x"])]
    plan_c = _plan_scatter(len(parts_c))
    swap_c = swap_start(parts_c, "c")

    st = _backward_b(st, rel_bias, swa_sinks + swap_c[4][0:1, 0:1] + a_token[0:1, 0:1])
    c_send, c_recv, c_src, c_land, c_token = scatter_start(swap_c, st["dsinks"], "c", prefill=(1, 2))
    plan_b = _plan_scatter(1)

    def dw_in_parts(half, dep):
        dwh = _dw_in_half(st, half, dep)
        return dwh, [dwh]

    dw0, parts_b0 = dw_in_parts(0, c_token)
    swap_b0 = swap_start(parts_b0, "b0")
    a_land = _exchange_wait(a_send, a_recv, a_src, a_land, plan_a, swap_b0[4], "scatter_a_wait")
    big = [None] * 6

    chip1 = jnp.reshape(chip, (1,)).astype(jnp.int32)

    def adamw_big(j, land, own, wt, mt, vt):
        big[j] = _adamw(land, own, chip1, wt, mt, vt, "adamw_big%d" % j)

    adamw_big(5, a_land[0], a_src[0], w_out, m_w_out, v_w_out)
    adamw_big(2, a_land[1], a_src[1], w_br_rg, m_w_br_rg, v_w_br_rg)
    adamw_big(3, a_land[2], a_src[2], w_br_swa, m_w_br_swa, v_w_br_swa)
    halves = [scatter_start(swap_b0, corner(big[5][1]) + corner(big[2][1]) + corner(big[3][1]), "b0")]
    dw1, parts_b1 = dw_in_parts(1, halves[0][4])
    swap_b1 = swap_start(parts_b1, "b1")
    c_land = _exchange_wait(c_send, c_recv, c_src, c_land, plan_c, swap_b1[4], "scatter_c_wait")
    g_wa_blk = _sum_parts(c_land[1], "sum_w_rg_a")
    g_wx_blk = _sum_parts(c_land[2], "sum_w_rg_x")
    adamw_big(4, a_land[3], a_src[3], w_br_mem, m_w_br_mem, v_w_br_mem)
    adamw_big(1, c_land[0], c_src[0], w_mem_kv, m_w_mem_kv, v_w_mem_kv)
    halves.append(scatter_start(swap_b1, corner(big[4][1]) + corner(big[1][1]), "b1"))
    st["dmem_g"] = _mem_gain_grad(st, mem0, w_memkv_f, halves[1][4])
    grad_x, dpre = _dh_dx(st["dproj"], w_in_f, x0, st["dy"], pre_norm_g + halves[1][4][0:1, 0:1], st["tm"], D_IN_TILE)
    pack = jnp.concatenate([dpre.reshape(16, 128), st["dpost"].reshape(16, 128), st["dmem_g"].reshape(16, 128),
                            st["dvec"].reshape(64, 128), _pad_rows(st["dsinks"], 8), _pad_rows(st["drel"], 32), g_wa_blk, g_wx_blk,
                            st["loss"]], axis=0)
    plan_s = _plan_everyone(1)
    s_send, s_recv, s_src, s_land, s_token = _exchange_start([pack], [landing(pack, me, N_DEV)], plan_s, "gather_small_start")
    swap_last = lambda a: jnp.transpose(a, (0, 2, 1))
    after, big0_t = s_token, None
    for half, (b_send, b_recv, b_src, b_land, _) in enumerate(halves):
        b_land = _exchange_wait(b_send, b_recv, b_src, b_land, plan_b, after, "scatter_b%d_wait" % half)[0]
        big0_t = _adamw(b_land, b_src[0], chip1, swap_last(w_in), swap_last(m_w_in), swap_last(v_w_in), "adamw_big0_%d" % half,
                        group=(half, 2), into=big0_t)
        after = corner(big0_t[1])
    big[0] = [swap_last(a) for a in big0_t]
    gathered = _exchange_wait(s_send, s_recv, s_src, s_land, plan_s, corner(big0_t[1]), "gather_small_wait")[0]
    gs = _sum_parts(gathered, "sum_small")
    loss_total = gs[408, 0]
    g_pre, g_post, g_memg = gs[0:16].reshape(1, D_MODEL), gs[16:32].reshape(1, D_MODEL), gs[32:48].reshape(1, D_MODEL)
    gvec = gs[48:112].reshape(8, D_RNN)
    g_conv_w = lax.dynamic_slice(gvec[0:CONV_W], (0, me * RNN_BLOCK), (CONV_W, RNN_BLOCK))
    g_conv_b, g_b_a, g_b_x, g_lam = gvec[4:5], gvec[5:6], gvec[6:7], gvec[7:8]
    g_sinks = gs[112:113, :SWA_HEADS]
    g_rel = gs[120:152, :SWA_HEADS]
    g_w_a = gathered[:, 152:280]
    g_w_x = gathered[:, 280:408]

    g_small = (g_pre, g_post, g_memg, g_conv_b, g_b_a, g_b_x, g_lam, g_w_a, g_w_x, g_sinks, g_rel, g_conv_w)
    w_small = (pre_norm_g, post_norm_g, mem_norm_g, conv_b, b_rg_a, b_rg_x, lru_lambda, w_rg_a, w_rg_x, swa_sinks, rel_bias, conv_w)
    m_small = (m_pre_norm_g, m_post_norm_g, m_mem_norm_g, m_conv_b, m_b_rg_a, m_b_rg_x, m_lru_lambda, m_w_rg_a, m_w_rg_x, m_swa_sinks, m_rel_bias, m_conv_w)
    v_small = (v_pre_norm_g, v_post_norm_g, v_mem_norm_g, v_conv_b, v_b_rg_a, v_b_rg_x, v_lru_lambda, v_w_rg_a, v_w_rg_x, v_swa_sinks, v_rel_bias, v_conv_w)
    sm = _adamw_small(g_small, w_small, m_small, v_small)


    def leaves(k):
        s = sm[k]
        return [s[0], s[1], s[2], big[0][k], s[11], s[3], s[7], s[4], s[8], s[5], s[6], s[9], s[10],
                big[1][k], big[2][k], big[3][k], big[4][k], big[5][k]]

    return (loss_total, grad_x[None], *leaves(0), *leaves(1), *leaves(2), *leaves(3))
```

```python
import math

import jax
import jax.numpy as jnp
import numpy as np
from jax import lax
from jax.experimental import pallas as pl
from jax.experimental.pallas import tpu as pltpu

F32, BF = jnp.float32, jnp.bfloat16
MESH = pl.DeviceIdType.MESH
N_DEV = 8

D_MODEL = 2048
D_RNN = 1024
RNN_BLOCKS = 8
RNN_BLOCK = 128
CONV_W = 4
LRU_C = 8.0
SWA_HEADS = 16
SWA_KV_HEADS = 2
SWA_HD = 64
WINDOW = 128
MEM_HEADS = 4
MEM_HD = 256
D_MEM = 1024
REL_BUCKETS = 32
REL_MAX_DIST = 128
EPS = 1e-6
NEG_INF = -1e30
D_IN = 12544
SEGMENTS = (("xr", 0, 1024, F32), ("g_rg", 1024, 1024, F32), ("q_s", 2048, 1024, BF), ("kv", 3072, 256, BF),
            ("g_swa", 3328, 1024, F32), ("q_m", 4352, 1024, BF), ("g_mem", 5376, 1024, F32), ("gl", 6400, 6144, F32))
SEG_TILE = 256
D_IN_TILE = 7 * SEG_TILE

ADAM_LR, ADAM_B1, ADAM_B2, ADAM_EPS, ADAM_WD, ADAM_STEP = 0.001, 0.9, 0.999, 1e-08, 0.01, 10

NN = (((1,), (0,)), ((), ()))
NT = (((1,), (1,)), ((), ()))
TN = (((0,), (0,)), ((), ()))
MIB = 2 ** 20


def _dot(a, b, dn):
    return lax.dot_general(a, b, dn, preferred_element_type=F32)


VMEM_LIMIT_MIB = 48
VMEM_LIMIT_LARGE_MIB = 56


def _params(sem=None, large=False):
    return pltpu.CompilerParams(dimension_semantics=sem, vmem_limit_bytes=(VMEM_LIMIT_LARGE_MIB if large else VMEM_LIMIT_MIB) * MIB)


def _sigmoid(z):
    return 1.0 / (1.0 + jnp.exp(-z))


def _softplus(z):
    return jnp.maximum(z, 0.0) + jnp.log(1.0 + jnp.exp(-jnp.abs(z)))


def _expm1(z):
    p = z * (1.0 + z * (0.5 + z * (1.0 / 6 + z * (1.0 / 24 + z * (1.0 / 120 + z * (1.0 / 720 + z * (1.0 / 5040 + z / 40320)))))))
    return jnp.where(jnp.abs(z) < 0.3, p, jnp.exp(z) - 1.0)


def _flat(p):
    return 4 * p[0] + 2 * p[1] + p[2]


def _all_gather_relayed(arrs, relay, name, side=None):
    n = len(arrs)
    K = 9
    work, side_ins, side_outs, side_scratch = side if side is not None else (None, [], [], [])
    n_in, n_out = n + len(side_ins), n + len(side_outs)

    def body(*refs):
        ins, outs = refs[:n], refs[n_in:n_in + n]
        send_sems, recv_sems, local_sems = refs[n_in + n_out:n_in + n_out + 3]
        x, y, c = lax.axis_index("x"), lax.axis_index("y"), lax.axis_index("c")
        me, sib = (x, y, c), (x, y, 1 - c)
        xn, yn, dg = (1 - x, y, c), (x, 1 - y, c), (1 - x, 1 - y, c)

        def other(p):
            return (p[0], p[1], 1 - p[2])

        def rows(a, half):
            h = arrs[a].shape[0] // 2
            return pl.ds(half * h, h)

        def copy(a, k, block, to, half=None, src=None):
            dst = outs[a].at[_flat(block)]
            if half is not None:
                dst = dst.at[rows(a, half)]
            return pltpu.make_async_remote_copy(src_ref=dst if src is None else src, dst_ref=dst,
                                                send_sem=send_sems.at[a * K + k], recv_sem=recv_sems.at[a * K + k],
                                                device_id=to, device_id_type=MESH)

        mine = [pltpu.make_async_copy(ins[a], outs[a].at[_flat(me)], local_sems.at[a]) for a in range(n)]
        for cp in mine:
            cp.start()
        sends = []

        def start(cp):
            cp.start()
            sends.append(cp)

        for a in range(n):
            start(copy(a, 1, me, xn, src=ins[a]))
            start(copy(a, 2, me, yn, src=ins[a]))
            if not relay[a]:
                start(copy(a, 3, me, dg, src=ins[a]))
            start(copy(a, 0, me, sib, src=ins[a]))
        if work is not None:
            work(refs[n:n_in], refs[n_in + n:n_in + n_out], refs[n_in + n_out + 3:])
        for a in range(n):
            copy(a, 1, xn, me).wait_recv()
            if relay[a]:
                start(copy(a, 3, xn, yn, half=0))
            start(copy(a, 5, xn, sib))
        for a in range(n):
            copy(a, 2, yn, me).wait_recv()
            if relay[a]:
                start(copy(a, 4, yn, xn, half=1))
            start(copy(a, 6, yn, sib))
        for a in range(n):
            if relay[a]:
                copy(a, 3, dg, me, half=0).wait_recv()
                start(copy(a, 7, dg, sib, half=0))
                copy(a, 4, dg, me, half=1).wait_recv()
                start(copy(a, 8, dg, sib, half=1))
            else:
                copy(a, 3, dg, me).wait_recv()
                start(copy(a, 7, dg, sib))
        for a in range(n):
            copy(a, 0, sib, me).wait_recv()
            copy(a, 5, other(xn), me).wait_recv()
            copy(a, 6, other(yn), me).wait_recv()
            if relay[a]:
                copy(a, 7, other(dg), me, half=0).wait_recv()
                copy(a, 8, other(dg), me, half=1).wait_recv()
            else:
                copy(a, 7, other(dg), me).wait_recv()
        for cp in sends:
            cp.wait_send()
        for cp in mine:
            cp.wait()

    any_spec = pl.BlockSpec(memory_space=pl.ANY)
    return pl.pallas_call(
        body, name=name,
        out_shape=[jax.ShapeDtypeStruct((N_DEV,) + a.shape, a.dtype) for a in arrs] + list(side_outs),
        in_specs=[any_spec] * n_in, out_specs=[any_spec] * n_out,
        scratch_shapes=[pltpu.SemaphoreType.DMA((K * n,)), pltpu.SemaphoreType.DMA((K * n,)), pltpu.SemaphoreType.DMA((n,))]
        + list(side_scratch),
        compiler_params=_params(),
    )(*arrs, *side_ins)


def _chip_peers(x, y):
    return [(1 - x, y), (x, 1 - y), (1 - x, 1 - y)]


def _chip(p):
    return 2 * p[0] + p[1]


def _plan_gather(n):
    def plan(x, y, c):
        out = []
        for a in range(n):
            for peer in [(x, y, 1 - c)] + [(*ch, c) for ch in _chip_peers(x, y)]:
                out.append((a, None, ("lead", _flat((x, y, c))), peer, ("lead", _flat(peer))))
        return out
    return plan


def _plan_everyone(n):
    def plan(x, y, c):
        out = []
        for a in range(n):
            for r in range(1, N_DEV):
                peer = (1 - x if r & 4 else x, 1 - y if r & 2 else y, 1 - c if r & 1 else c)
                out.append((a, None, ("lead", _flat((x, y, c))), peer, ("lead", _flat(peer))))
        return out
    return plan


def _plan_swap(ndims):
    def plan(x, y, c):
        out = []
        for a, nd in enumerate(ndims):
            if nd == 4:
                out.append((a, 1 - c, ("all", 0), (x, y, 1 - c), ("all", 0)))
            else:
                out += [(a, 2 * j + 1 - c, ("lead", j), (x, y, 1 - c), ("lead", j)) for j in range(4)]
        return out
    return plan


def _slot(ref, where):
    kind, k = where
    return ref if kind == "all" else ref.at[k]


def _plan_scatter(n):
    def plan(x, y, c):
        out = []
        for a in range(n):
            for ch in _chip_peers(x, y):
                out.append((a, _chip(ch), ("lead", _chip((x, y))), (*ch, c), ("lead", _chip(ch))))
        return out
    return plan


HBM_SPEC = pl.BlockSpec(memory_space=pltpu.HBM)
SEM_SPEC = pl.BlockSpec(memory_space=pltpu.SEMAPHORE)


def _in_hbm(a):
    return pltpu.with_memory_space_constraint(a, pltpu.HBM)


def _exchange_start(srcs, lands, plan, name):
    n = len(lands)
    ns = 0 if srcs is None else n
    count = len(plan(0, 0, 0))

    def body(*refs):
        land_refs = refs[ns:ns + n]
        src_refs = land_refs if srcs is None else refs[:n]
        send_sems, recv_sems = refs[ns + n], refs[ns + n + 1]
        token = refs[-1]
        x, y, c = lax.axis_index("x"), lax.axis_index("y"), lax.axis_index("c")
        for k, (a, si, di, peer, _) in enumerate(plan(x, y, c)):
            src = src_refs[a] if si is None else src_refs[a].at[si]
            pltpu.make_async_remote_copy(src_ref=src, dst_ref=_slot(land_refs[a], di), send_sem=send_sems.at[k],
                                         recv_sem=recv_sems.at[k], device_id=peer, device_id_type=MESH).start()
        token[...] = jnp.zeros_like(token)

    out = pl.pallas_call(
        body, name=name,
        out_shape=(pltpu.SemaphoreType.DMA((count,)), pltpu.SemaphoreType.DMA((count,)),
                   *[pltpu.HBM(a.shape, a.dtype) for a in lands], jax.ShapeDtypeStruct((8, 128), F32)),
        in_specs=[HBM_SPEC] * (ns + n),
        out_specs=(SEM_SPEC, SEM_SPEC, *([HBM_SPEC] * n), pl.BlockSpec(memory_space=pltpu.VMEM)),
        input_output_aliases={ns + i: 2 + i for i in range(n)},
        compiler_params=pltpu.CompilerParams(has_side_effects=pltpu.SideEffectType.DATAFLOW_SIDE_EFFECTING),
    )(*[_in_hbm(a) for a in (srcs or [])], *[_in_hbm(a) for a in lands])
    return out[0], out[1], srcs if srcs is None else list(srcs), list(out[2:2 + n]), out[-1]


def _exchange_wait(send_sems, recv_sems, srcs, lands, plan, after, name):
    n = len(lands)
    ns = 0 if srcs is None else n

    def body(*refs):
        land_refs = refs[ns:ns + n]
        src_refs = land_refs if srcs is None else refs[:n]
        send_sems, recv_sems = refs[ns + n], refs[ns + n + 1]
        x, y, c = lax.axis_index("x"), lax.axis_index("y"), lax.axis_index("c")
        for k, (a, si, _, peer, ri) in enumerate(plan(x, y, c)):
            src = src_refs[a] if si is None else src_refs[a].at[si]
            cp = pltpu.make_async_remote_copy(src_ref=src, dst_ref=_slot(land_refs[a], ri), send_sem=send_sems.at[k],
                                              recv_sem=recv_sems.at[k], device_id=peer, device_id_type=MESH)
            cp.wait_send()
            cp.wait_recv()

    out = pl.pallas_call(
        body, name=name,
        out_shape=tuple(pltpu.HBM(a.shape, a.dtype) for a in lands),
        in_specs=[HBM_SPEC] * (ns + n) + [SEM_SPEC, SEM_SPEC, pl.BlockSpec(memory_space=pl.ANY)],
        out_specs=tuple([HBM_SPEC] * n),
        input_output_aliases={ns + i: i for i in range(n)},
        compiler_params=pltpu.CompilerParams(has_side_effects=pltpu.SideEffectType.DATAFLOW_SIDE_EFFECTING),
    )(*[_in_hbm(a) for a in (srcs or [])], *lands, send_sems, recv_sems, after)
    return list(out)


def _plan_forward(n):
    def plan(x, y, c):
        return [(a, _flat((*ch, c)), ("lead", _flat((*ch, c))), (x, y, 1 - c), ("lead", _flat((*ch, 1 - c))))
                for a in range(n) for ch in _chip_peers(x, y)]
    return plan


def _place_own(zones, owns, slot, name):
    n = len(zones)

    def body(slot_ref, *refs):
        for a in range(n):
            refs[2 * n + a][...] = refs[a][...]

    return pl.pallas_call(
        body, name=name,
        grid_spec=pltpu.PrefetchScalarGridSpec(
            num_scalar_prefetch=1, grid=(1,),
            in_specs=[pl.BlockSpec(o.shape, lambda i, s_ref: (0, 0)) for o in owns] + [pl.BlockSpec(memory_space=pl.ANY)] * n,
            out_specs=[pl.BlockSpec((None,) + o.shape, lambda i, s_ref: (s_ref[0], 0, 0)) for o in owns]),
        out_shape=[jax.ShapeDtypeStruct(z.shape, z.dtype) for z in zones],
        input_output_aliases={1 + n + a: a for a in range(n)},
        compiler_params=_params(("arbitrary",)),
    )(slot, *owns, *zones)


def _pair_sum(parts, got, core, name):
    R, C = parts.shape[-2:]
    mine = (pl.BlockSpec((None, None, R, C), lambda j, c_ref: (c_ref[0], j, 0, 0)) if parts.ndim == 4
            else pl.BlockSpec((None, R, C), lambda j, c_ref: (2 * j + c_ref[0], 0, 0)))

    def body(c_ref, p_ref, g_ref, o_ref):
        o_ref[...] = (p_ref[...].astype(F32) + g_ref[...].astype(F32)).astype(o_ref.dtype)

    return pl.pallas_call(
        body, name=name,
        grid_spec=pltpu.PrefetchScalarGridSpec(
            num_scalar_prefetch=1, grid=(4,),
            in_specs=[mine, pl.BlockSpec((None, R, C), lambda j, c_ref: (j, 0, 0))],
            out_specs=pl.BlockSpec((None, R, C), lambda j, c_ref: (j, 0, 0))),
        out_shape=pltpu.HBM((4, R, C), parts.dtype),
        compiler_params=_params(("parallel",)),
    )(core, parts, got)


def _matmul(a, b, mode, M, N, K, tm, tn, tk, out_dtype, name, b_noff=0, out_blocked=None, dep=None, out_hbm=False):
    nm, nn, nk = M // tm, N // tn, K // tk
    if mode == "nn":
        a_spec = pl.BlockSpec((tm, tk), lambda j, i, k: (i, k))
        b_spec = pl.BlockSpec((tk, tn), lambda j, i, k: (k, j + b_noff))
        dn = NN
    elif mode == "nt":
        a_spec = pl.BlockSpec((tm, tk), lambda j, i, k: (i, k))
        b_spec = pl.BlockSpec((tn, tk), lambda j, i, k: (j + b_noff, k))
        dn = NT
    else:
        a_spec = pl.BlockSpec((tk, tm), lambda j, i, k: (k, i))
        b_spec = pl.BlockSpec((tk, tn), lambda j, i, k: (k, j + b_noff))
        dn = TN
    if out_blocked == "col":
        out_shape = jax.ShapeDtypeStruct((2, 4, M, tn), out_dtype)
        out_spec = pl.BlockSpec((None, None, tm, tn), lambda j, i, k: (j % 2, j // 2, i, 0))
    elif out_blocked == "row":
        out_shape = jax.ShapeDtypeStruct((2, 4, tm, N), out_dtype)
        out_spec = pl.BlockSpec((None, None, tm, tn), lambda j, i, k: (i % 2, i // 2, 0, j))
    else:
        out_shape = jax.ShapeDtypeStruct((M, N), out_dtype)
        out_spec = pl.BlockSpec((tm, tn), lambda j, i, k: (i, j))
    if out_hbm or out_blocked is not None:
        out_shape = pltpu.HBM(out_shape.shape, out_shape.dtype)

    n_extra = int(dep is not None)

    def body(a_ref, b_ref, *rest):
        o_ref, scratch = rest[n_extra], rest[n_extra + 1:]
        if nk == 1:
            o_ref[...] = _dot(a_ref[...], b_ref[...], dn).astype(out_dtype)
        else:
            acc_ref, = scratch
            k = pl.program_id(2)

            @pl.when(k == 0)
            def _():
                acc_ref[...] = jnp.zeros_like(acc_ref)

            acc_ref[...] += _dot(a_ref[...], b_ref[...], dn)

            @pl.when(k == nk - 1)
            def _():
                o_ref[...] = acc_ref[...].astype(out_dtype)

    return pl.pallas_call(
        body, name=name, grid=(nn, nm, nk),
        in_specs=[a_spec, b_spec] + ([] if dep is None else [pl.BlockSpec((8, 128), lambda j, i, k: (0, 0))]),
        out_specs=out_spec, out_shape=out_shape,
        scratch_shapes=[] if nk == 1 else [pltpu.VMEM((tm, tn), F32)],
        compiler_params=_params(("parallel", "parallel", "arbitrary")),
    )(a, b, *([] if dep is None else [dep]))


RMS_SIDE_ROWS = 512


def _rms_side(pairs):
    chunks = [min(x.shape[0], RMS_SIDE_ROWS) for x, _ in pairs]

    def work(ins, outs, scratch):
        sem = scratch[-1]

        def move(src, dst):
            cp = pltpu.make_async_copy(src, dst, sem.at[0])
            cp.start()
            cp.wait()

        for p, ((x, _), tr) in enumerate(zip(pairs, chunks)):
            x_ref, g_ref, h_ref = ins[2 * p], ins[2 * p + 1], outs[p]
            xv, gv, hv = scratch[3 * p:3 * p + 3]
            move(g_ref, gv)
            for i in range(x.shape[0] // tr):
                rows = pl.ds(i * tr, tr)
                move(x_ref.at[rows], xv)
                v = xv[...]
                hv[...] = (v * lax.rsqrt(jnp.mean(v * v, axis=-1, keepdims=True) + EPS) * gv[...]).astype(BF)
                move(hv, h_ref.at[rows])

    scratch = [s for (x, g), tr in zip(pairs, chunks)
               for s in (pltpu.VMEM((tr, x.shape[1]), F32), pltpu.VMEM(g.shape, F32), pltpu.VMEM((tr, x.shape[1]), BF))]
    return (work, [a for pair in pairs for a in pair], [jax.ShapeDtypeStruct(x.shape, BF) for x, _ in pairs],
            scratch + [pltpu.SemaphoreType.DMA((1,))])


def _rms_gain_grad(dn, x, name):
    R, Dm = x.shape

    def body(dn_ref, x_ref, o_ref):
        xv = x_ref[...]
        r = lax.rsqrt(jnp.mean(xv * xv, axis=-1, keepdims=True) + EPS)
        o_ref[...] = jnp.sum(dn_ref[...] * xv * r, axis=0, keepdims=True)

    return pl.pallas_call(
        body, name=name, out_shape=jax.ShapeDtypeStruct((1, Dm), F32),
        compiler_params=_params(),
    )(dn, x)


def _shift_down(v, k, head8, row, T):
    if k == 0:
        return v
    r = pltpu.roll(v, k, 0)
    hr = pltpu.roll(head8, k, 0)
    top = jnp.where(row[:8] < k, hr, r[:8])
    return jnp.concatenate([top, r[8:]], axis=0)


def _shift_up(v, k, tail8, row, T):
    if k == 0:
        return v
    r = pltpu.roll(v, T - k, 0)
    tr = pltpu.roll(tail8, 8 - k, 0)
    bot = jnp.where(row[:8] >= 8 - k, tr, r[T - 8:])
    return jnp.concatenate([r[:T - 8], bot], axis=0)


def _rglru_gates(u, head8, grow, row, T, cw_ref, cb_ref, wa_ref, ba_ref, wx_ref, bx_ref, lam_ref):
    us = [_shift_down(u, k, head8, row, T) for k in range(CONV_W)]
    acc = us[0] * cw_ref[0:1, :]
    for k in range(1, CONV_W):
        acc = acc + us[k] * cw_ref[k:k + 1, :]
    conv = cb_ref[...] + acc
    cbf = conv.astype(BF)
    r_ = _sigmoid(_dot(cbf, wa_ref[0], NN) + ba_ref[...])
    i_ = _sigmoid(_dot(cbf, wx_ref[0], NN) + bx_ref[...])
    sp = _softplus(-lam_ref[...])
    la = -LRU_C * r_ * sp
    a = jnp.exp(la)
    mult_raw = jnp.sqrt(-_expm1(2.0 * la))
    mult = jnp.where(grow == 0, 1.0, mult_raw)
    return us, conv, cbf, r_, i_, sp, a, mult_raw, mult


def _rglru_specs(T, nt, rev):
    tmap = (lambda n, t: (nt - 1 - t, n)) if rev else (lambda n, t: (t, n))
    hmap = ((lambda n, t: (jnp.maximum((nt - 1 - t) * (T // 8) - 1, 0), n)) if rev
            else (lambda n, t: (jnp.maximum(t * (T // 8) - 1, 0), n)))
    tile = pl.BlockSpec((T, RNN_BLOCK), tmap)
    halo = pl.BlockSpec((8, RNN_BLOCK), hmap)
    vec = pl.BlockSpec((1, RNN_BLOCK), lambda n, t: (0, n))
    cw = pl.BlockSpec((CONV_W, RNN_BLOCK), lambda n, t: (0, n))
    wblk = pl.BlockSpec((1, RNN_BLOCK, RNN_BLOCK), lambda n, t: (n, 0, 0))
    return tile, halo, vec, cw, wblk


def _rglru_fwd(xr, g, cw, cb, wa, ba, wx, bx, lam, T):
    S = xr.shape[0]
    nt = S // T

    def body(u_ref, uh_ref, g_ref, cw_ref, cb_ref, wa_ref, ba_ref, wx_ref, bx_ref, lam_ref, h_ref, y_ref, carry):
        t = pl.program_id(1)

        @pl.when(t == 0)
        def _():
            carry[...] = jnp.zeros_like(carry)

        row = lax.broadcasted_iota(jnp.int32, (T, RNN_BLOCK), 0)
        grow = row + t * T
        head8 = jnp.where(t > 0, uh_ref[...], 0.0)
        _, conv, _, _, i_, _, a, _, mult = _rglru_gates(u_ref[...], head8, grow, row, T, cw_ref, cb_ref, wa_ref, ba_ref,
                                                         wx_ref, bx_ref, lam_ref)
        b = mult * i_ * conv
        s = 1
        while s < T:
            keep = row >= s
            a_s = jnp.where(keep, pltpu.roll(a, s, 0), 1.0)
            b_s = jnp.where(keep, pltpu.roll(b, s, 0), 0.0)
            b = a * b_s + b
            a = a * a_s
            s *= 2
        h = b + a * carry[0:1, :]
        carry[...] = jnp.broadcast_to(h[T - 1:T, :], carry.shape)
        h_ref[...] = h
        gv = g_ref[...]
        y_ref[...] = (h * (gv * _sigmoid(gv))).astype(BF)

    tile, halo, vec, cwspec, wblk = _rglru_specs(T, nt, False)
    return pl.pallas_call(
        body, name="rglru_fwd", grid=(RNN_BLOCKS, nt),
        in_specs=[tile, halo, tile, cwspec, vec, wblk, vec, wblk, vec, vec],
        out_specs=[tile, tile],
        out_shape=[jax.ShapeDtypeStruct((S, D_RNN), F32), jax.ShapeDtypeStruct((S, D_RNN), BF)],
        scratch_shapes=[pltpu.VMEM((8, RNN_BLOCK), F32)],
        compiler_params=_params(("parallel", "arbitrary")),
    )(xr, xr, g, cw, cb, wa, ba, wx, bx, lam)


def _rglru_bwd(xr, g, h, dy, cw, cb, wa, ba, wx, bx, lam, T):
    S = xr.shape[0]
    nt = S // T

    def body(u_ref, uh_ref, g_ref, h_ref, hh_ref, dy_ref, cw_ref, cb_ref, wa_ref, ba_ref, wx_ref, bx_ref, lam_ref,
             du_ref, dg_ref, dwa_ref, dwx_ref, dvec_ref, c_dhh, c_a, c_dconv):
        t = pl.program_id(1)
        tt = nt - 1 - t

        @pl.when(t == 0)
        def _():
            c_dhh[...] = jnp.zeros_like(c_dhh)
            c_a[...] = jnp.zeros_like(c_a)
            c_dconv[...] = jnp.zeros_like(c_dconv)
            dwa_ref[...] = jnp.zeros_like(dwa_ref)
            dwx_ref[...] = jnp.zeros_like(dwx_ref)
            dvec_ref[...] = jnp.zeros_like(dvec_ref)

        row = lax.broadcasted_iota(jnp.int32, (T, RNN_BLOCK), 0)
        row8 = row[:8]
        grow = row + tt * T
        head8 = jnp.where(tt > 0, uh_ref[...], 0.0)
        us, conv, cbf, r_, i_, sp, a, mult_raw, mult = _rglru_gates(
            u_ref[...], head8, grow, row, T, cw_ref, cb_ref, wa_ref, ba_ref, wx_ref, bx_ref, lam_ref)
        hv = h_ref[...]
        hprev = _shift_down(hv, 1, jnp.where(tt > 0, hh_ref[...], 0.0), row, T)
        gv = g_ref[...]
        sg = _sigmoid(gv)
        dyv = dy_ref[...]
        dg_ref[...] = (dyv * hv * (sg * (1.0 + gv * (1.0 - sg)))).astype(BF)
        d = dyv * (gv * sg)
        A = _shift_up(a, 1, c_a[...], row, T)
        s = 1
        while s < T:
            keep = row < T - s
            A_s = jnp.where(keep, pltpu.roll(A, T - s, 0), 1.0)
            d_s = jnp.where(keep, pltpu.roll(d, T - s, 0), 0.0)
            d = A * d_s + d
            A = A * A_s
            s *= 2
        dhh = d + A * c_dhh[0:1, :]
        da = dhh * hprev
        dconv = dhh * mult * i_
        di = dhh * mult * conv
        dmult = dhh * i_ * conv
        dla = da * a - jnp.where(grow == 0, 0.0, dmult * (a * a) / mult_raw)
        dr = dla * (-LRU_C * sp)
        dsp = jnp.sum(dla * (-LRU_C * r_), axis=0, keepdims=True)
        dza = dr * r_ * (1.0 - r_)
        dzx = di * i_ * (1.0 - i_)
        dza_b, dzx_b = dza.astype(BF), dzx.astype(BF)
        dconv = dconv + _dot(dza_b, wa_ref[0], NT) + _dot(dzx_b, wx_ref[0], NT)
        dwa_ref[0] += _dot(cbf, dza_b, TN)
        dwx_ref[0] += _dot(cbf, dzx_b, TN)
        lam = lam_ref[...]
        rows = [jnp.sum(dconv * us[k], axis=0, keepdims=True) for k in range(CONV_W)]
        rows += [jnp.sum(dconv, axis=0, keepdims=True), jnp.sum(dza, axis=0, keepdims=True),
                 jnp.sum(dzx, axis=0, keepdims=True), dsp * (-_sigmoid(-lam))]
        upd = jnp.zeros((8, RNN_BLOCK), F32)
        for j, rv in enumerate(rows):
            upd = upd + jnp.where(row8 == j, rv, 0.0)
        dvec_ref[...] += upd
        tail8 = c_dconv[...]
        du = dconv * cw_ref[0:1, :]
        for k in range(1, CONV_W):
            du = du + _shift_up(dconv, k, tail8, row, T) * cw_ref[k:k + 1, :]
        du_ref[...] = du.astype(BF)
        c_dhh[...] = jnp.broadcast_to(dhh[0:1, :], c_dhh.shape)
        c_a[...] = jnp.broadcast_to(a[0:1, :], c_a.shape)
        c_dconv[...] = dconv[:8]

    tile, halo, vec, cwspec, wblk = _rglru_specs(T, nt, True)
    acc8 = pl.BlockSpec((8, RNN_BLOCK), lambda n, t: (0, n))
    return pl.pallas_call(
        body, name="rglru_bwd", grid=(RNN_BLOCKS, nt),
        in_specs=[tile, halo, tile, tile, halo, tile, cwspec, vec, wblk, vec, wblk, vec, vec],
        out_specs=[tile, tile, wblk, wblk, acc8],
        out_shape=[jax.ShapeDtypeStruct((S, D_RNN), BF), jax.ShapeDtypeStruct((S, D_RNN), BF),
                   jax.ShapeDtypeStruct((RNN_BLOCKS, RNN_BLOCK, RNN_BLOCK), F32),
                   jax.ShapeDtypeStruct((RNN_BLOCKS, RNN_BLOCK, RNN_BLOCK), F32),
                   jax.ShapeDtypeStruct((8, D_RNN), F32)],
        scratch_shapes=[pltpu.VMEM((8, RNN_BLOCK), F32)] * 3,
        compiler_params=_params(("parallel", "arbitrary")),
    )(xr, xr, g, h, h, dy, cw, cb, wa, ba, wx, bx, lam)


def _rel_bucket_map():
    qi = np.arange(WINDOW)[:, None]
    kj = np.arange(2 * WINDOW)[None, :]
    dist = jnp.asarray(qi + WINDOW - kj, jnp.int32)
    n = jnp.maximum(dist, 0)
    max_exact = REL_BUCKETS // 2
    ratio = jnp.log(jnp.maximum(n, 1).astype(F32) / max_exact) / math.log(REL_MAX_DIST / max_exact)
    large = jnp.minimum(max_exact + (ratio * (REL_BUCKETS - max_exact)).astype(jnp.int32), REL_BUCKETS - 1)
    bucket = jnp.where(n < max_exact, n, large).astype(jnp.int32)
    j = np.arange(WINDOW)[None, :]
    return jnp.where(jnp.asarray(j > qi), bucket[:, :WINDOW], bucket[:, WINDOW:])


def _swa_common(n, kv_ref, bucket_ref, relb_ref, bias_scr):
    @pl.when(n == 0)
    def _():
        bk = bucket_ref[...]
        for h in range(SWA_HEADS):
            acc = jnp.zeros((WINDOW, WINDOW), F32)
            for b in range(REL_BUCKETS):
                acc = acc + jnp.where(bk == b, relb_ref[b, h], 0.0)
            bias_scr[h] = acc

    prev0 = pl.multiple_of(jnp.maximum(n - 1, 0) * WINDOW, WINDOW)
    cur0 = pl.multiple_of(n * WINDOW, WINDOW)
    kk = jnp.concatenate([kv_ref[pl.ds(prev0, WINDOW), :], kv_ref[pl.ds(cur0, WINDOW), :]], axis=0).astype(F32)
    rowi = lax.broadcasted_iota(jnp.int32, (WINDOW, WINDOW), 0)
    col = lax.broadcasted_iota(jnp.int32, (WINDOW, WINDOW), 1)
    from_prev = col > rowi
    return kk, from_prev, prev0, cur0


def _fold(full, from_prev):
    return jnp.where(from_prev, full[:, :WINDOW], full[:, WINDOW:])


def _unfold(sq, from_prev):
    return jnp.concatenate([jnp.where(from_prev, sq, 0.0), jnp.where(from_prev, 0.0, sq)], axis=1)


def _half_pair(part, kvh):
    lo = lax.broadcasted_iota(jnp.int32, part.shape, 1) < SWA_HD
    if kvh == 0:
        pa = jnp.where(lo, part, 0.0)
        pb = pltpu.roll(pa, SWA_HD, 1)
    else:
        pb = jnp.where(lo, 0.0, part)
        pa = pltpu.roll(pb, SWA_HD, 1)
    return pa.astype(BF), pb.astype(BF)


ALL_HEADS = SWA_HEADS * WINDOW


def _sink_column(sinks):
    return jnp.repeat(sinks.reshape(SWA_HEADS), WINDOW).reshape(ALL_HEADS, 1)


def _swa_operands(kk):
    return [(_half_pair(kk[:, :128], kvh), _half_pair(kk[:, 128:], kvh)) for kvh in range(SWA_KV_HEADS)]


def _swa_probs(n, q_ref, ops, bias_scr, sinkc_ref, from_prev):
    lgs = []
    for kvh in range(SWA_KV_HEADS):
        (ka, kb), _ = ops[kvh]
        for p in range(4):
            q2 = q_ref[:, kvh * 512 + p * 128:kvh * 512 + p * 128 + 128]
            lgs += [_fold(_dot(q2, ka, NT), from_prev), _fold(_dot(q2, kb, NT), from_prev)]
    lg = jnp.concatenate(lgs, axis=0) * (SWA_HD ** -0.5) + bias_scr[...].reshape(ALL_HEADS, WINDOW)
    rowi = jnp.bitwise_and(lax.broadcasted_iota(jnp.int32, (ALL_HEADS, WINDOW), 0), WINDOW - 1)
    col = lax.broadcasted_iota(jnp.int32, (ALL_HEADS, WINDOW), 1)
    no_prev = jnp.where(n > 0, 0, 4 * WINDOW)
    lg = jnp.where(jnp.logical_or(col <= rowi, col > rowi + no_prev), lg, NEG_INF)
    sink = sinkc_ref[...]
    m = jnp.maximum(jnp.max(lg, axis=-1, keepdims=True), sink)
    e = jnp.exp(lg - m)
    es = jnp.exp(sink - m)
    den = jnp.sum(e, axis=-1, keepdims=True) + es
    return e / den, es / den


def _swa_fwd(q, kv, g, bucket, rel_bias, sink_col):
    S = q.shape[0]
    nb = S // WINDOW

    def body(q_ref, kv_ref, g_ref, bucket_ref, relb_ref, sinkc_ref, o_ref, y_ref, bias_scr):
        n = pl.program_id(0)
        kk, from_prev, _, _ = _swa_common(n, kv_ref, bucket_ref, relb_ref, bias_scr)
        ops = _swa_operands(kk)
        pr, _ = _swa_probs(n, q_ref, ops, bias_scr, sinkc_ref, from_prev)
        for kvh in range(SWA_KV_HEADS):
            _, (va, vb) = ops[kvh]
            for p in range(4):
                c0 = kvh * 512 + p * 128
                r0 = (kvh * 8 + 2 * p) * WINDOW
                o2 = (_dot(_unfold(pr[r0:r0 + WINDOW], from_prev).astype(BF), va, NN)
                      + _dot(_unfold(pr[r0 + WINDOW:r0 + 2 * WINDOW], from_prev).astype(BF), vb, NN))
                o_ref[:, c0:c0 + 128] = o2
                gv = g_ref[:, c0:c0 + 128]
                y_ref[:, c0:c0 + 128] = (o2 * (gv * _sigmoid(gv))).astype(BF)

    blk = pl.BlockSpec((WINDOW, 1024), lambda n: (n, 0))
    smem = pl.BlockSpec(memory_space=pltpu.SMEM)
    sinkc = pl.BlockSpec((ALL_HEADS, 1), lambda n: (0, 0))
    return pl.pallas_call(
        body, name="swa_fwd", grid=(nb,),
        in_specs=[blk, pl.BlockSpec((S, 256), lambda n: (0, 0)), blk, pl.BlockSpec((WINDOW, WINDOW), lambda n: (0, 0)), smem, sinkc],
        out_specs=[blk, blk],
        out_shape=[jax.ShapeDtypeStruct((S, 1024), F32), jax.ShapeDtypeStruct((S, 1024), BF)],
        scratch_shapes=[pltpu.VMEM((SWA_HEADS, WINDOW, WINDOW), F32)],
        compiler_params=_params(("arbitrary",)),
    )(q, kv, g, bucket, rel_bias, sink_col)


def _swa_bwd(q, kv, g, o, dy, bucket, rel_bias, sink_col, others):
    S = q.shape[0]
    nb = S // WINDOW
    first_col = {name: c0 for name, c0, _, _ in SEGMENTS}
    col_q, col_g = first_col["q_s"], first_col["g_swa"]
    copies = []
    for name, c0, width, _ in SEGMENTS:
        if name not in ("q_s", "kv", "g_swa"):
            arrs = others[name] if name == "gl" else (others[name],)
            copies += [(a, c0 + i * (width // len(arrs))) for i, a in enumerate(arrs)]

    def body(q_ref, kv_ref, g_ref, o_ref, dy_ref, bucket_ref, relb_ref, sinkc_ref, *refs):
        copy_refs = refs[:len(copies)]
        dp_ref, dkv_ref, dsink_ref, drel_ref, bias_scr, dbias_scr, dsink_scr = refs[len(copies):]
        n = pl.program_id(0)
        for c_ref, (a, c0) in zip(copy_refs, copies):
            dp_ref[:, c0:c0 + a.shape[1]] = c_ref[...]

        @pl.when(n == 0)
        def _():
            dbias_scr[...] = jnp.zeros_like(dbias_scr)
            dsink_scr[...] = jnp.zeros_like(dsink_scr)
            dkv_ref[...] = jnp.zeros_like(dkv_ref)

        kk, from_prev, prev0, cur0 = _swa_common(n, kv_ref, bucket_ref, relb_ref, bias_scr)
        ops = _swa_operands(kk)
        pr, ps = _swa_probs(n, q_ref, ops, bias_scr, sinkc_ref, from_prev)
        do2s, dps = [], []
        for kvh in range(SWA_KV_HEADS):
            _, (va, vb) = ops[kvh]
            for p in range(4):
                c0 = kvh * 512 + p * 128
                gv = g_ref[:, c0:c0 + 128]
                sg = _sigmoid(gv)
                dyv = dy_ref[:, c0:c0 + 128]
                dp_ref[:, col_g + c0:col_g + c0 + 128] = (dyv * o_ref[:, c0:c0 + 128] * (sg * (1.0 + gv * (1.0 - sg)))).astype(BF)
                do2 = (dyv * (gv * sg)).astype(BF)
                do2s.append(do2)
                dps += [_fold(_dot(do2, va, NT), from_prev), _fold(_dot(do2, vb, NT), from_prev)]
        dp = jnp.concatenate(dps, axis=0)
        delta = jnp.sum(pr * dp, axis=-1, keepdims=True)
        ds = pr * (dp - delta)
        dbias_scr[...] += ds.reshape(SWA_HEADS, WINDOW, WINDOW)
        dsink_scr[...] += ps * delta
        dsc = ds * (SWA_HD ** -0.5)
        lo256 = lax.broadcasted_iota(jnp.int32, (2 * WINDOW, 128), 1) < SWA_HD
        dks, dvs = [], []
        for kvh in range(SWA_KV_HEADS):
            (ka, kb), _ = ops[kvh]
            dka = jnp.zeros((2 * WINDOW, 128), F32)
            dkb, dva, dvb = dka, dka, dka
            for p in range(4):
                c0 = kvh * 512 + p * 128
                r0 = (kvh * 8 + 2 * p) * WINDOW
                q2 = q_ref[:, c0:c0 + 128]
                do2 = do2s[kvh * 4 + p]
                ds0 = _unfold(dsc[r0:r0 + WINDOW], from_prev).astype(BF)
                ds1 = _unfold(dsc[r0 + WINDOW:r0 + 2 * WINDOW], from_prev).astype(BF)
                dp_ref[:, col_q + c0:col_q + c0 + 128] = (_dot(ds0, ka, NN) + _dot(ds1, kb, NN)).astype(BF)
                dka = dka + _dot(ds0, q2, TN)
                dkb = dkb + _dot(ds1, q2, TN)
                dva = dva + _dot(_unfold(pr[r0:r0 + WINDOW], from_prev).astype(BF), do2, TN)
                dvb = dvb + _dot(_unfold(pr[r0 + WINDOW:r0 + 2 * WINDOW], from_prev).astype(BF), do2, TN)
            dks.append(jnp.where(lo256, dka, 0.0) + pltpu.roll(jnp.where(lo256, 0.0, dkb), SWA_HD, 1))
            dvs.append(jnp.where(lo256, dva, 0.0) + pltpu.roll(jnp.where(lo256, 0.0, dvb), SWA_HD, 1))
        dk = dks[0] + pltpu.roll(dks[1], SWA_HD, 1)
        dv = dvs[0] + pltpu.roll(dvs[1], SWA_HD, 1)
        dkv_ref[pl.ds(prev0, WINDOW), 0:128] += dk[:WINDOW]
        dkv_ref[pl.ds(prev0, WINDOW), 128:256] += dv[:WINDOW]
        dkv_ref[pl.ds(cur0, WINDOW), 0:128] += dk[WINDOW:]
        dkv_ref[pl.ds(cur0, WINDOW), 128:256] += dv[WINDOW:]

        @pl.when(n == nb - 1)
        def _():
            dsink_ref[...] = -jnp.sum(dsink_scr[...].reshape(SWA_HEADS, WINDOW, 1), axis=1)
            bk = bucket_ref[...]
            sums = []
            for b in range(REL_BUCKETS):
                sums.append(jnp.sum(jnp.where((bk == b)[None], dbias_scr[...], 0.0), axis=1))
            drel_ref[...] = jnp.sum(jnp.concatenate(sums, axis=0), axis=1, keepdims=True)

    blk = pl.BlockSpec((WINDOW, 1024), lambda n: (n, 0))
    smem = pl.BlockSpec(memory_space=pltpu.SMEM)
    whole = lambda shape: pl.BlockSpec(shape, lambda n: (0, 0))
    return pl.pallas_call(
        body, name="swa_bwd", grid=(nb,),
        in_specs=[blk, whole((S, 256)), blk, blk, blk, whole((WINDOW, WINDOW)), smem, whole((ALL_HEADS, 1))]
        + [pl.BlockSpec((WINDOW, a.shape[1]), lambda n: (n, 0)) for a, _ in copies],
        out_specs=[pl.BlockSpec((WINDOW, D_IN), lambda n: (n, 0)), whole((S, 256)), whole((SWA_HEADS, 1)),
                   whole((REL_BUCKETS * SWA_HEADS, 1))],
        out_shape=[jax.ShapeDtypeStruct((S, D_IN), BF), jax.ShapeDtypeStruct((S, 256), F32), jax.ShapeDtypeStruct((SWA_HEADS, 1), F32),
                   jax.ShapeDtypeStruct((REL_BUCKETS * SWA_HEADS, 1), F32)],
        scratch_shapes=[pltpu.VMEM((SWA_HEADS, WINDOW, WINDOW), F32), pltpu.VMEM((SWA_HEADS, WINDOW, WINDOW), F32),
                        pltpu.VMEM((ALL_HEADS, 1), F32)],
        compiler_params=_params(("arbitrary",)),
    )(q, kv, g, o, dy, bucket, rel_bias, sink_col, *[a for a, _ in copies])


MEM_TQ = 512


def _mem_probs(qh, mk):
    lg = _dot(qh, mk, NT) * (MEM_HD ** -0.5)
    e = jnp.exp(lg - jnp.max(lg, axis=-1, keepdims=True))
    return e / jnp.sum(e, axis=-1, keepdims=True)


def _mem_fwd(q, mkv, g):
    S = q.shape[0]
    M = mkv.shape[0]
    tq = min(S, MEM_TQ)

    def body(q_ref, mkv_ref, g_ref, o_ref, y_ref):
        for h in range(MEM_HEADS):
            c0 = h * MEM_HD
            pr = _mem_probs(q_ref[:, c0:c0 + MEM_HD], mkv_ref[:, c0:c0 + MEM_HD])
            o = _dot(pr.astype(BF), mkv_ref[:, D_MEM + c0:D_MEM + c0 + MEM_HD], NN)
            o_ref[:, c0:c0 + MEM_HD] = o
            gv = g_ref[:, c0:c0 + MEM_HD]
            y_ref[:, c0:c0 + MEM_HD] = (o * (gv * _sigmoid(gv))).astype(BF)

    blk = pl.BlockSpec((tq, D_MEM), lambda i: (i, 0))
    return pl.pallas_call(
        body, name="mem_fwd", grid=(S // tq,),
        in_specs=[blk, pl.BlockSpec((M, 2 * D_MEM), lambda i: (0, 0)), blk], out_specs=[blk, blk],
        out_shape=[jax.ShapeDtypeStruct((S, D_MEM), F32), jax.ShapeDtypeStruct((S, D_MEM), BF)],
        compiler_params=_params(("parallel",)),
    )(q, mkv, g)


def _mem_bwd(q, mkv, g, o, dy):
    S = q.shape[0]
    M = mkv.shape[0]
    tq = min(S, MEM_TQ)

    def body(q_ref, mkv_ref, g_ref, o_ref, dy_ref, dq_ref, dg_ref, dmkv_ref):
        @pl.when(pl.program_id(0) == 0)
        def _():
            dmkv_ref[...] = jnp.zeros_like(dmkv_ref)

        for h in range(MEM_HEADS):
            c0 = h * MEM_HD
            qh = q_ref[:, c0:c0 + MEM_HD]
            mk = mkv_ref[:, c0:c0 + MEM_HD]
            mv = mkv_ref[:, D_MEM + c0:D_MEM + c0 + MEM_HD]
            gv = g_ref[:, c0:c0 + MEM_HD]
            sg = _sigmoid(gv)
            dyv = dy_ref[:, c0:c0 + MEM_HD]
            dg_ref[:, c0:c0 + MEM_HD] = (dyv * o_ref[:, c0:c0 + MEM_HD] * (sg * (1.0 + gv * (1.0 - sg)))).astype(BF)
            do = (dyv * (gv * sg)).astype(BF)
            pr = _mem_probs(qh, mk)
            dp = _dot(do, mv, NT)
            ds = pr * (dp - jnp.sum(pr * dp, axis=-1, keepdims=True))
            dsb = (ds * (MEM_HD ** -0.5)).astype(BF)
            dq_ref[:, c0:c0 + MEM_HD] = _dot(dsb, mk, NN).astype(BF)
            dmkv_ref[:, c0:c0 + MEM_HD] += _dot(dsb, qh, TN)
            dmkv_ref[:, D_MEM + c0:D_MEM + c0 + MEM_HD] += _dot(pr.astype(BF), do, TN)

    blk = pl.BlockSpec((tq, D_MEM), lambda i: (i, 0))
    whole = pl.BlockSpec((M, 2 * D_MEM), lambda i: (0, 0))
    return pl.pallas_call(
        body, name="mem_bwd", grid=(S // tq,),
        in_specs=[blk, whole, blk, blk, blk], out_specs=[blk, blk, whole],
        out_shape=[jax.ShapeDtypeStruct((S, D_MEM), BF), jax.ShapeDtypeStruct((S, D_MEM), BF),
                   jax.ShapeDtypeStruct((M, 2 * D_MEM), F32)],
        compiler_params=_params(("arbitrary",)),
    )(q, mkv, g, o, dy)


MERGE_TN = 512


def _merge_specs(tm):
    ytile = pl.BlockSpec((tm, 1024), lambda i, j: (i, 0))
    wblk = pl.BlockSpec((MERGE_TN, 1024), lambda i, j: (j, 0))
    gls = [pl.BlockSpec((None, tm, MERGE_TN), (lambda i, j, br=br: (br, i, j))) for br in range(3)]
    otile = pl.BlockSpec((tm, MERGE_TN), lambda i, j: (i, j))
    return ytile, wblk, gls, otile


def _merge_fwd(ys, ws, gl, tm):
    S = gl.shape[1]

    def body(y0, y1, y2, w0, w1, w2, g0, g1, g2, o_ref):
        acc = None
        for y_ref, w_ref, g_ref in ((y0, w0, g0), (y1, w1, g1), (y2, w2, g2)):
            term = _sigmoid(g_ref[...]) * _dot(y_ref[...], w_ref[...], NT)
            acc = term if acc is None else acc + term
        o_ref[...] = acc.astype(BF)

    ytile, wblk, gls, otile = _merge_specs(tm)
    return pl.pallas_call(
        body, name="merge_fwd", grid=(S // tm, D_MODEL // MERGE_TN),
        in_specs=[ytile] * 3 + [wblk] * 3 + gls, out_specs=otile,
        out_shape=jax.ShapeDtypeStruct((S, D_MODEL), BF),
        compiler_params=_params(("parallel", "arbitrary")),
    )(*ys, *ws, gl, gl, gl)


def _merge_bwd(dout, w_out, ys, ws, gl, tm):
    S = gl.shape[1]

    def body(do_ref, wo_ref, y0, y1, y2, w0, w1, w2, g0, g1, g2, dg0, dg1, dg2, dp0, dp1, dp2):
        dm = _dot(do_ref[...], wo_ref[...], NT)
        for y_ref, w_ref, g_ref, dg_ref, dp_ref in ((y0, w0, g0, dg0, dp0), (y1, w1, g1, dg1, dp1), (y2, w2, g2, dg2, dp2)):
            gate = _sigmoid(g_ref[...])
            pv = _dot(y_ref[...], w_ref[...], NT)
            dg_ref[...] = (dm * pv * gate * (1.0 - gate)).astype(BF)
            dp_ref[...] = (dm * gate).astype(BF)

    ytile, wblk, gls, otile = _merge_specs(tm)
    out = jax.ShapeDtypeStruct((S, D_MODEL), BF)
    return pl.pallas_call(
        body, name="merge_bwd", grid=(S // tm, D_MODEL // MERGE_TN),
        in_specs=[pl.BlockSpec((tm, D_MODEL), lambda i, j: (i, 0)), pl.BlockSpec((MERGE_TN, D_MODEL), lambda i, j: (j, 0))]
        + [ytile] * 3 + [wblk] * 3 + gls,
        out_specs=[otile] * 6, out_shape=[out] * 6,
        compiler_params=_params(("parallel", "arbitrary")),
    )(dout, w_out, *ys, *ws, gl, gl, gl)


def _out_loss(merged, w_out, x, target, post_g, tm):
    S = x.shape[0]

    def body(m_ref, w_ref, x_ref, t_ref, g_ref, dout_ref, dy_ref, loss_ref, dpost_ref):
        @pl.when(pl.program_id(0) == 0)
        def _():
            loss_ref[...] = jnp.zeros_like(loss_ref)
            dpost_ref[...] = jnp.zeros_like(dpost_ref)

        out = _dot(m_ref[...], w_ref[...], NN)
        r = lax.rsqrt(jnp.mean(out * out, axis=-1, keepdims=True) + EPS)
        nrm = out * r
        gv = g_ref[...]
        err = (x_ref[...] + nrm * gv) - t_ref[...]
        sq = jnp.sum(jnp.sum(err * err, axis=1, keepdims=True), axis=0, keepdims=True)
        loss_ref[...] += sq * (0.5 / D_MODEL)
        dy = err * (1.0 / D_MODEL)
        dy_ref[...] = dy
        dpost_ref[...] += jnp.sum(dy * nrm, axis=0, keepdims=True)
        dn = dy * gv
        dout_ref[...] = (r * (dn - nrm * jnp.mean(dn * nrm, axis=-1, keepdims=True))).astype(BF)

    row = pl.BlockSpec((tm, D_MODEL), lambda i: (i, 0))
    return pl.pallas_call(
        body, name="out_loss", grid=(S // tm,),
        in_specs=[row, pl.BlockSpec((D_MODEL, D_MODEL), lambda i: (0, 0)), row, row, pl.BlockSpec((1, D_MODEL), lambda i: (0, 0))],
        out_specs=[row, row, pl.BlockSpec((8, 128), lambda i: (0, 0)), pl.BlockSpec((1, D_MODEL), lambda i: (0, 0))],
        out_shape=[jax.ShapeDtypeStruct((S, D_MODEL), BF), jax.ShapeDtypeStruct((S, D_MODEL), F32),
                   jax.ShapeDtypeStruct((8, 128), F32), jax.ShapeDtypeStruct((1, D_MODEL), F32)],
        compiler_params=_params(("arbitrary",)),
    )(merged, w_out, x, target, post_g)


DH_DX_CHUNK = 64


def _dh_dx(dproj, w_in, x, dy, pre_g, tm, tk):
    S = x.shape[0]
    nk = D_IN // tk

    def body(dp_ref, w_ref, x_ref, dy_ref, g_ref, dx_ref, dpre_ref, acc_ref):
        i, k = pl.program_id(0), pl.program_id(1)

        @pl.when(jnp.logical_and(i == 0, k == 0))
        def _():
            dpre_ref[...] = jnp.zeros_like(dpre_ref)

        @pl.when(k == 0)
        def _():
            acc_ref[...] = jnp.zeros_like(acc_ref)

        acc_ref[...] += _dot(dp_ref[...], w_ref[...], NN)

        @pl.when(k == nk - 1)
        def _():
            def chunk(c, carry):
                rows = pl.ds(pl.multiple_of(c * DH_DX_CHUNK, DH_DX_CHUNK), DH_DX_CHUNK)
                dh = acc_ref[rows, :]
                xv = x_ref[rows, :]
                r = lax.rsqrt(jnp.mean(xv * xv, axis=-1, keepdims=True) + EPS)
                nrm = xv * r
                dpre_ref[...] += jnp.sum(dh * nrm, axis=0, keepdims=True)
                dn = dh * g_ref[...]
                dx_ref[rows, :] = r * (dn - nrm * jnp.mean(dn * nrm, axis=-1, keepdims=True)) + dy_ref[rows, :]
                return carry
            lax.fori_loop(0, tm // DH_DX_CHUNK, chunk, 0)

    row = pl.BlockSpec((tm, D_MODEL), lambda i, k: (i, 0))
    vec = pl.BlockSpec((1, D_MODEL), lambda i, k: (0, 0))
    return pl.pallas_call(
        body, name="dh_dx", grid=(S // tm, nk),
        in_specs=[pl.BlockSpec((tm, tk), lambda i, k: (i, k)), pl.BlockSpec((tk, D_MODEL), lambda i, k: (k, 0)), row, row, vec],
        out_specs=[row, vec],
        out_shape=[jax.ShapeDtypeStruct((S, D_MODEL), F32), jax.ShapeDtypeStruct((1, D_MODEL), F32)],
        scratch_shapes=[pltpu.VMEM((tm, D_MODEL), F32)],
        compiler_params=_params(("arbitrary", "arbitrary"), large=True),
    )(dproj, w_in, x, dy, pre_g)


def _sum_parts(parts, name):
    P, R, C = parts.shape
    tr = max(t for t in range(8, 513, 8) if R % t == 0)

    def body(p_ref, o_ref):
        acc = p_ref[0]
        for j in range(1, P):
            acc = acc + p_ref[j]
        o_ref[...] = acc

    return pl.pallas_call(
        body, name=name, grid=(R // tr,),
        in_specs=[pl.BlockSpec((P, tr, C), lambda i: (0, i, 0))], out_specs=pl.BlockSpec((tr, C), lambda i: (i, 0)),
        out_shape=jax.ShapeDtypeStruct((R, C), F32), compiler_params=_params(("parallel",)),
    )(parts)


def _adamw(land, sums, chip, w, m, v, name, group=(0, 1), into=None):
    q, n_groups = group
    _, R, cols = land.shape
    C = cols * n_groups
    tr = max(t for t in range(16, 257, 16) if R % t == 0)
    c1 = 1.0 - ADAM_B1 ** ADAM_STEP
    c2 = 1.0 - ADAM_B2 ** ADAM_STEP
    n_into = 0 if into is None else 4

    def body(chip_ref, p0_ref, p1_ref, p2_ref, own_ref, w_ref, m_ref, v_ref, *refs):
        g_ref, d_ref, nm_ref, nv_ref = refs[n_into:]
        g = own_ref[...].astype(F32)
        for p_ref in (p0_ref, p1_ref, p2_ref):
            g = g + p_ref[...].astype(F32)
        nm = ADAM_B1 * m_ref[...] + (1.0 - ADAM_B1) * g
        nv = ADAM_B2 * v_ref[...] + (1.0 - ADAM_B2) * (g * g)
        g_ref[...] = g
        nm_ref[...] = nm
        nv_ref[...] = nv
        d_ref[...] = -ADAM_LR * ((nm / c1) / (jnp.sqrt(nv / c2) + ADAM_EPS) + ADAM_WD * w_ref[...])

    tile = pl.BlockSpec((None, tr, cols), lambda i, c_ref: (0, i, q))
    specs = [pl.BlockSpec((None, tr, cols), (lambda i, c_ref, k=k: (k + (c_ref[0] <= k).astype(jnp.int32), i, 0))) for k in range(3)]
    specs.append(pl.BlockSpec((None, tr, cols), (lambda i, c_ref: (c_ref[0], i, 0))))
    return pl.pallas_call(
        body, name=name,
        grid_spec=pltpu.PrefetchScalarGridSpec(num_scalar_prefetch=1, grid=(R // tr,),
                                               in_specs=specs + [tile, tile, tile] + [pl.BlockSpec(memory_space=pl.ANY)] * n_into,
                                               out_specs=[tile] * 4),
        out_shape=[jax.ShapeDtypeStruct((1, R, C), F32)] * 4,
        input_output_aliases={8 + k: k for k in range(n_into)},
        compiler_params=_params(("parallel",)),
    )(chip, land, land, land, sums, w, m, v, *(into or []))


def _adamw_small(gs, ws, ms, vs):
    n = len(ws)
    c1 = 1.0 - ADAM_B1 ** ADAM_STEP
    c2 = 1.0 - ADAM_B2 ** ADAM_STEP

    def flat2(a):
        return a.reshape(-1, a.shape[-1])

    def body(*refs):
        ins, outs = refs[:4 * n], refs[4 * n:]
        for a in range(n):
            g, w, m, v = (ins[k * n + a][...] for k in range(4))
            nm = ADAM_B1 * m + (1.0 - ADAM_B1) * g
            nv = ADAM_B2 * v + (1.0 - ADAM_B2) * (g * g)
            outs[a][...] = g
            outs[n + a][...] = -ADAM_LR * ((nm / c1) / (jnp.sqrt(nv / c2) + ADAM_EPS) + ADAM_WD * w)
            outs[2 * n + a][...] = nm
            outs[3 * n + a][...] = nv

    shapes = [flat2(w).shape for w in ws]
    out = pl.pallas_call(
        body, name="adamw_small", out_shape=[jax.ShapeDtypeStruct(sh, F32) for sh in shapes] * 4,
        compiler_params=_params(),
    )(*[g.reshape(sh) for g, sh in zip(gs, shapes)], *[flat2(a) for a in (*ws, *ms, *vs)])
    return [[out[k * n + a].reshape(ws[a].shape) for a in range(n)] for k in range(4)]


PROJ_ROWS = 512


def _project(h, w_t, dep=None):
    S = h.shape[0]
    n_tiles = D_IN // SEG_TILE
    ranges = [(c0 // SEG_TILE, (c0 + width) // SEG_TILE) for _, c0, width, _ in SEGMENTS]
    dtypes = (F32, BF)
    n_extra = int(dep is not None)

    def of_dtype(j, dt):
        hit = False
        for (j0, j1), seg in zip(ranges, SEGMENTS):
            if seg[3] == dt:
                hit = jnp.logical_and(j >= j0, j < j1) | hit
        return hit

    def body(h_ref, w_ref, *refs):
        outs = refs[n_extra:n_extra + len(SEGMENTS)]
        stages, sems = refs[n_extra + len(SEGMENTS):-1], refs[-1]
        j = pl.program_id(0)
        slot = j % 2

        def copy_out(k, stage, dst):
            return pltpu.make_async_copy(stages[k].at[stage], dst, sems.at[stage])

        def wait_tile(jj, stage):
            for k, dt in enumerate(dtypes):
                o_ref = [o for o, seg in zip(outs, SEGMENTS) if seg[3] == dt and seg[0] != "gl"][0]
                @pl.when(of_dtype(jj, dt))
                def _(k=k, o_ref=o_ref):
                    copy_out(k, stage, o_ref.at[:, pl.ds(0, SEG_TILE)]).wait()

        @pl.when(j >= 2)
        def _():
            wait_tile(j - 2, slot)

        for k, dt in enumerate(dtypes):
            @pl.when(of_dtype(j, dt))
            def _(k=k, dt=dt):
                for c in range(S // PROJ_ROWS):
                    rows = pl.ds(c * PROJ_ROWS, PROJ_ROWS)
                    stages[k][slot, rows, :] = _dot(h_ref[rows, :], w_ref[...], NT).astype(dt)

        for (j0, j1), (name, _, _, dt), o_ref in zip(ranges, SEGMENTS, outs):
            @pl.when(jnp.logical_and(j >= j0, j < j1))
            def _(j0=j0, j1=j1, name=name, dt=dt, o_ref=o_ref):
                t = j - j0
                if name == "gl":
                    per = (j1 - j0) // 3
                    dst = o_ref.at[t // per, :, pl.ds(pl.multiple_of((t % per) * SEG_TILE, SEG_TILE), SEG_TILE)]
                else:
                    dst = o_ref.at[:, pl.ds(pl.multiple_of(t * SEG_TILE, SEG_TILE), SEG_TILE)]
                copy_out(dtypes.index(dt), slot, dst).start()

        @pl.when(j == n_tiles - 1)
        def _():
            wait_tile(j - 1, 1 - slot)
            wait_tile(j, slot)

    out_shapes = [jax.ShapeDtypeStruct((3, S, width // 3) if name == "gl" else (S, width), dt) for name, _, width, dt in SEGMENTS]
    outs = pl.pallas_call(
        body, name="proj", grid=(n_tiles,),
        in_specs=[pl.BlockSpec((S, D_MODEL), lambda j: (0, 0)), pl.BlockSpec((SEG_TILE, D_MODEL), lambda j: (j, 0))]
        + ([] if dep is None else [pl.BlockSpec((8, 128), lambda j: (0, 0))]),
        out_specs=[pl.BlockSpec(memory_space=pl.ANY)] * len(SEGMENTS), out_shape=out_shapes,
        scratch_shapes=[pltpu.VMEM((2, S, SEG_TILE), dt) for dt in dtypes] + [pltpu.SemaphoreType.DMA((2,))],
        compiler_params=_params(("arbitrary",), large=True),
    )(h, w_t, *([] if dep is None else [dep]))
    return {name: o for (name, _, _, _), o in zip(SEGMENTS, outs)}


def _forward_a(h, memn, w_in, sinks, rel_bias, dep=None):
    S = h.shape[0]
    st = dict(T=min(512, S // 2), tm=min(512, S), bucket=_rel_bucket_map(), h=h, memn=memn)
    seg = st["seg"] = _project(h, w_in, dep)
    st["o_swa"], st["y_swa"] = _swa_fwd(seg["q_s"], seg["kv"], seg["g_swa"], st["bucket"], rel_bias, _sink_column(sinks))
    return st


def _forward_rg(st, conv_w, conv_b, w_a, b_a, w_x, b_x, lam):
    seg = st["seg"]
    st["h_rg"], st["y_rg"] = _rglru_fwd(seg["xr"], seg["g_rg"], conv_w, conv_b, w_a, b_a, w_x, b_x, lam, st["T"])
    return st


def _forward_b(st, x, target, post_g, w_memkv, wbr, w_out):
    S = x.shape[0]
    M = st["memn"].shape[0]
    seg = st["seg"]
    st["mkv"] = _matmul(st["memn"], w_memkv, "nn", M, 2 * D_MEM, D_MODEL, M, 512, D_MODEL, BF, "mem_kv")
    st["o_mem"], st["y_mem"] = _mem_fwd(seg["q_m"], st["mkv"], seg["g_mem"])
    st["ys"] = (st["y_rg"], st["y_swa"], st["y_mem"])
    st["merged"] = _merge_fwd(st["ys"], wbr, seg["gl"], st["tm"])
    st["dout"], st["dy"], st["loss"], st["dpost"] = _out_loss(st["merged"], w_out, x, target, post_g, min(256, S))
    return st


def _backward_a1(st, wbr, w_out):
    S = st["h"].shape[0]
    seg, ys, tm = st["seg"], st["ys"], st["tm"]
    st["dw_out"] = _matmul(st["merged"], st["dout"], "tn", D_MODEL, D_MODEL, S, 256, D_MODEL, S, BF, "dw_out", out_blocked="row")
    dgl0, dgl1, dgl2, dp0, dp1, dp2 = _merge_bwd(st["dout"], w_out, ys, wbr, seg["gl"], tm)
    st["dgl"] = (dgl0, dgl1, dgl2)
    dys, dwbr = [], []
    for i, dp in enumerate((dp0, dp1, dp2)):
        dys.append(_matmul(dp, wbr[i], "nn", S, 1024, D_MODEL, tm, 1024, D_MODEL, F32, "dy_br%d" % i))
        dwbr.append(_matmul(ys[i], dp, "tn", 1024, D_MODEL, S, 1024, 256, S, BF, "dw_br%d" % i, out_blocked="col"))
    st["dys"], st["dwbr"] = dys, dwbr
    return st


def _backward_a2(st, mem, w_memkv, conv_w, conv_b, w_a, b_a, w_x, b_x, lam):
    M = mem.shape[0]
    seg, dys = st["seg"], st["dys"]
    st["dq_m"], st["dg_mem"], dmkv = _mem_bwd(seg["q_m"], st["mkv"], seg["g_mem"], st["o_mem"], dys[2])
    st["dmkv"] = dmkv.astype(BF)
    st["dw_memkv"] = _matmul(st["memn"], st["dmkv"], "tn", D_MODEL, 2 * D_MEM, M, 256, 2 * D_MEM, M, BF, "dw_memkv", out_blocked="row")
    st["dxr"], st["dg_rg"], st["dw_a"], st["dw_x"], st["dvec"] = _rglru_bwd(
        seg["xr"], seg["g_rg"], st["h_rg"], dys[0], conv_w, conv_b, w_a, b_a, w_x, b_x, lam, st["T"])
    return st


def _mem_gain_grad(st, mem, w_memkv, dep=None):
    M = mem.shape[0]
    dmemn = _matmul(st["dmkv"], w_memkv, "nt", M, D_MODEL, 2 * D_MEM, M, 512, 2 * D_MEM, F32, "dmemn", dep=dep)
    return _rms_gain_grad(dmemn, mem, "dmem_gain")


def _backward_b(st, rel_bias, sinks):
    seg = st["seg"]
    others = {"xr": st["dxr"], "g_rg": st["dg_rg"], "q_m": st["dq_m"], "g_mem": st["dg_mem"], "gl": st["dgl"]}
    dproj, dkv, dsinks, drel = _swa_bwd(seg["q_s"], seg["kv"], seg["g_swa"], st["o_swa"], st["dys"][1],
                                        st["bucket"], rel_bias, _sink_column(sinks), others)
    st["dsinks"], st["drel"] = dsinks.reshape(1, SWA_HEADS), drel.reshape(REL_BUCKETS, SWA_HEADS)
    col_kv = [c0 for name, c0, _, _ in SEGMENTS if name == "kv"][0]
    st["dproj"] = lax.dynamic_update_slice(dproj, dkv.astype(BF), (0, col_kv))
    return st


def _dw_in_half(st, half, dep=None):
    S = st["h"].shape[0]
    dw = _matmul(st["dproj"], st["h"], "tn", D_IN, D_MODEL // 2, S, D_IN_TILE, D_MODEL // 2, S, BF, "dw_in%d" % half, b_noff=half, dep=dep, out_hbm=True)
    return dw.reshape(N_DEV, D_IN // N_DEV, D_MODEL // 2)


def _owner_blocks(a):
    return jnp.swapaxes(a.reshape((4, 2) + a.shape[1:]), 0, 1)


def _pad_rows(a, rows):
    a = a.reshape(-1, 128) if a.shape[-1] % 128 == 0 else jnp.pad(a, ((0, 0), (0, 128 - a.shape[-1])))
    return jnp.pad(a, ((0, rows - a.shape[0]), (0, 0))) if a.shape[0] < rows else a


def kernel(x, mem, pre_norm_g, post_norm_g, mem_norm_g, w_in, conv_w, conv_b, w_rg_a, b_rg_a, w_rg_x, b_rg_x, lru_lambda, swa_sinks, rel_bias, w_mem_kv, w_br_rg, w_br_swa, w_br_mem, w_out, loss_target, m_pre_norm_g, m_post_norm_g, m_mem_norm_g, m_w_in, m_conv_w, m_conv_b, m_w_rg_a, m_b_rg_a, m_w_rg_x, m_b_rg_x, m_lru_lambda, m_swa_sinks, m_rel_bias, m_w_mem_kv, m_w_br_rg, m_w_br_swa, m_w_br_mem, m_w_out, v_pre_norm_g, v_post_norm_g, v_mem_norm_g, v_w_in, v_conv_w, v_conv_b, v_w_rg_a, v_b_rg_a, v_w_rg_x, v_b_rg_x, v_lru_lambda, v_swa_sinks, v_rel_bias, v_w_mem_kv, v_w_br_rg, v_w_br_swa, v_w_br_mem, v_w_out):
    cx, cy, cc = lax.axis_index("x"), lax.axis_index("y"), lax.axis_index("c")
    me = 4 * cx + 2 * cy + cc
    chip = 2 * cx + cy
    core = jnp.reshape(cc, (1,)).astype(jnp.int32)
    x0, mem0 = x[0], mem[0]
    w_a_b, w_x_b = w_rg_a[0].astype(BF), w_rg_x[0].astype(BF)

    def landing(own, slot, slots):
        return lax.dynamic_update_slice(lax.empty((slots,) + own.shape, own.dtype), own[None], (slot,) + (0,) * own.ndim)


    def swap_start(parts, tag):
        return _exchange_start(parts, [lax.empty((4,) + p.shape[-2:], p.dtype) for p in parts], _plan_swap([p.ndim for p in parts]),
                               "swap_%s_start" % tag)

    def scatter_start(swap, after, tag, prefill=()):
        s_send, s_recv, parts, got, _ = swap
        got = _exchange_wait(s_send, s_recv, parts, got, _plan_swap([p.ndim for p in parts]), after, "swap_%s_wait" % tag)
        sums = [_pair_sum(p, g, core, "scatter_%s_sum%d" % (tag, i)) for i, (p, g) in enumerate(zip(parts, got))]
        lands = [landing(lax.dynamic_index_in_dim(s, chip, 0, keepdims=False), chip, 4) if i in prefill
                 else lax.empty(s.shape, s.dtype) for i, s in enumerate(sums)]
        return _exchange_start(sums, lands, _plan_scatter(len(sums)), "scatter_%s_start" % tag)

    def corner(a):
        return a.reshape(-1, a.shape[-1])[:8, :128]

    def zero_after(a):
        return jnp.minimum(jnp.abs(a.reshape(-1)[0].astype(F32)), 0.0)

    g_in, g_cw, h0, memn0 = _all_gather_relayed([jnp.transpose(w_in[0]).astype(BF), conv_w[0]], [True, False], "gather_w_in",
                                                side=_rms_side([(x0, pre_norm_g), (mem0, mem_norm_g)]))
    w_in_f = g_in.reshape(D_IN, D_MODEL)
    conv_w_f = jnp.transpose(g_cw, (1, 0, 2)).reshape(CONV_W, D_RNN)

    after_first = zero_after(g_cw).astype(BF)
    rest = [w.astype(BF) + after_first for w in (w_mem_kv[0], jnp.transpose(w_br_rg[0]), jnp.transpose(w_br_swa[0]),
                                                 jnp.transpose(w_br_mem[0]), w_out[0])]
    plan_g = _plan_gather(len(rest))
    zones = _place_own([lax.empty((N_DEV,) + w.shape, w.dtype) for w in rest], rest, jnp.reshape(me, (1,)).astype(jnp.int32), "gather_rest_own")
    g_send, g_recv, g_src, g_land, g_token = _exchange_start(rest, zones, plan_g, "gather_rest_start")
    st = _forward_a(h0, memn0, w_in_f, swa_sinks, rel_bias, dep=g_token)
    g_land = _exchange_wait(g_send, g_recv, g_src, g_land, plan_g, st["y_swa"], "gather_rest_wait")
    plan_f = _plan_forward(len(rest))
    f_send, f_recv, _, g_land, f_token = _exchange_start(None, g_land, plan_f, "forward_rest_start")
    st = _forward_rg(st, conv_w_f, conv_b + f_token[0:1, 0:1], w_a_b, b_rg_a, w_x_b, b_rg_x, lru_lambda)
    g_land = _exchange_wait(f_send, f_recv, None, g_land, plan_f, corner(st["y_rg"]), "forward_rest_wait")
    w_memkv_f = g_land[0].reshape(D_MODEL, 2 * D_MEM)
    wbr = tuple(g_land[i].reshape(D_MODEL, D_RNN) for i in (1, 2, 3))
    w_out_f = g_land[4].reshape(D_MODEL, D_MODEL)

    st = _forward_b(st, x0, loss_target[0], post_norm_g, w_memkv_f, wbr, w_out_f)
    st = _backward_a1(st, wbr, w_out_f)
    parts_a = [st["dw_out"], st["dwbr"][0], st["dwbr"][1], st["dwbr"][2]]
    plan_a = _plan_scatter(len(parts_a))
    swap_a = swap_start(parts_a, "a")
    st = _backward_a2(st, mem0, w_memkv_f, conv_w_f, conv_b + swap_a[4][0:1, 0:1], w_a_b, b_rg_a, w_x_b, b_rg_x, lru_lambda)
    a_send, a_recv, a_src, a_land, a_token = scatter_start(swap_a, st["dxr"], "a")
    parts_c = [st["dw_memkv"], _owner_blocks(st["dw_a"]), _owner_blocks(st["dw_x"])]
    plan_c = _plan_scatter(len(parts_c))
    swap_c = swap_start(parts_c, "c")

    st = _backward_b(st, rel_bias, swa_sinks + swap_c[4][0:1, 0:1] + a_token[0:1, 0:1])
    c_send, c_recv, c_src, c_land, c_token = scatter_start(swap_c, st["dsinks"], "c", prefill=(1, 2))
    plan_b = _plan_scatter(1)

    def dw_in_parts(half, dep):
        dwh = _dw_in_half(st, half, dep)
        return dwh, [dwh]

    dw0, parts_b0 = dw_in_parts(0, c_token)
    swap_b0 = swap_start(parts_b0, "b0")
    a_land = _exchange_wait(a_send, a_recv, a_src, a_land, plan_a, swap_b0[4], "scatter_a_wait")
    big = [None] * 6

    chip1 = jnp.reshape(chip, (1,)).astype(jnp.int32)

    def adamw_big(j, land, own, wt, mt, vt):
        big[j] = _adamw(land, own, chip1, wt, mt, vt, "adamw_big%d" % j)

    adamw_big(5, a_land[0], a_src[0], w_out, m_w_out, v_w_out)
    adamw_big(2, a_land[1], a_src[1], w_br_rg, m_w_br_rg, v_w_br_rg)
    adamw_big(3, a_land[2], a_src[2], w_br_swa, m_w_br_swa, v_w_br_swa)
    halves = [scatter_start(swap_b0, corner(big[5][1]) + corner(big[2][1]) + corner(big[3][1]), "b0")]
    dw1, parts_b1 = dw_in_parts(1, halves[0][4])
    swap_b1 = swap_start(parts_b1, "b1")
    c_land = _exchange_wait(c_send, c_recv, c_src, c_land, plan_c, swap_b1[4], "scatter_c_wait")
    g_wa_blk = _sum_parts(c_land[1], "sum_w_rg_a")
    g_wx_blk = _sum_parts(c_land[2], "sum_w_rg_x")
    adamw_big(4, a_land[3], a_src[3], w_br_mem, m_w_br_mem, v_w_br_mem)
    adamw_big(1, c_land[0], c_src[0], w_mem_kv, m_w_mem_kv, v_w_mem_kv)
    halves.append(scatter_start(swap_b1, corner(big[4][1]) + corner(big[1][1]), "b1"))
    st["dmem_g"] = _mem_gain_grad(st, mem0, w_memkv_f, halves[1][4])
    grad_x, dpre = _dh_dx(st["dproj"], w_in_f, x0, st["dy"], pre_norm_g + halves[1][4][0:1, 0:1], st["tm"], D_IN_TILE)
    pack = jnp.concatenate([dpre.reshape(16, 128), st["dpost"].reshape(16, 128), st["dmem_g"].reshape(16, 128),
                            st["dvec"].reshape(64, 128), _pad_rows(st["dsinks"], 8), _pad_rows(st["drel"], 32), g_wa_blk, g_wx_blk,
                            st["loss"]], axis=0)
    plan_s = _plan_everyone(1)
    s_send, s_recv, s_src, s_land, s_token = _exchange_start([pack], [landing(pack, me, N_DEV)], plan_s, "gather_small_start")
    swap_last = lambda a: jnp.transpose(a, (0, 2, 1))
    after, big0_t = s_token, None
    for half, (b_send, b_recv, b_src, b_land, _) in enumerate(halves):
        b_land = _exchange_wait(b_send, b_recv, b_src, b_land, plan_b, after, "scatter_b%d_wait" % half)[0]
        big0_t = _adamw(b_land, b_src[0], chip1, swap_last(w_in), swap_last(m_w_in), swap_last(v_w_in), "adamw_big0_%d" % half,
                        group=(half, 2), into=big0_t)
        after = corner(big0_t[1])
    big[0] = [swap_last(a) for a in big0_t]
    gathered = _exchange_wait(s_send, s_recv, s_src, s_land, plan_s, corner(big0_t[1]), "gather_small_wait")[0]
    gs = _sum_parts(gathered, "sum_small")
    loss_total = gs[408, 0]
    g_pre, g_post, g_memg = gs[0:16].reshape(1, D_MODEL), gs[16:32].reshape(1, D_MODEL), gs[32:48].reshape(1, D_MODEL)
    gvec = gs[48:112].reshape(8, D_RNN)
    g_conv_w = lax.dynamic_slice(gvec[0:CONV_W], (0, me * RNN_BLOCK), (CONV_W, RNN_BLOCK))
    g_conv_b, g_b_a, g_b_x, g_lam = gvec[4:5], gvec[5:6], gvec[6:7], gvec[7:8]
    g_sinks = gs[112:113, :SWA_HEADS]
    g_rel = gs[120:152, :SWA_HEADS]
    g_w_a = gathered[:, 152:280]
    g_w_x = gathered[:, 280:408]

    g_small = (g_pre, g_post, g_memg, g_conv_b, g_b_a, g_b_x, g_lam, g_w_a, g_w_x, g_sinks, g_rel, g_conv_w)
    w_small = (pre_norm_g, post_norm_g, mem_norm_g, conv_b, b_rg_a, b_rg_x, lru_lambda, w_rg_a, w_rg_x, swa_sinks, rel_bias, conv_w)
    m_small = (m_pre_norm_g, m_post_norm_g, m_mem_norm_g, m_conv_b, m_b_rg_a, m_b_rg_x, m_lru_lambda, m_w_rg_a, m_w_rg_x, m_swa_sinks, m_rel_bias, m_conv_w)
    v_small = (v_pre_norm_g, v_post_norm_g, v_mem_norm_g, v_conv_b, v_b_rg_a, v_b_rg_x, v_lru_lambda, v_w_rg_a, v_w_rg_x, v_swa_sinks, v_rel_bias, v_conv_w)
    sm = _adamw_small(g_small, w_small, m_small, v_small)


    def leaves(k):
        s = sm[k]
        return [s[0], s[1], s[2], big[0][k], s[11], s[3], s[7], s[4], s[8], s[5], s[6], s[9], s[10],
                big[1][k], big[2][k], big[3][k], big[4][k], big[5][k]]

    return (loss_total, grad_x[None], *leaves(0), *leaves(1), *leaves(2), *leaves(3))
```

```python
import math

import jax
import jax.numpy as jnp
import numpy as np
from jax import lax
from jax.experimental import pallas as pl
from jax.experimental.pallas import tpu as pltpu

F32, BF = jnp.float32, jnp.bfloat16
MESH = pl.DeviceIdType.MESH
N_DEV = 8

D_MODEL = 2048
D_RNN = 1024
RNN_BLOCKS = 8
RNN_BLOCK = 128
CONV_W = 4
LRU_C = 8.0
SWA_HEADS = 16
SWA_KV_HEADS = 2
SWA_HD = 64
WINDOW = 128
MEM_HEADS = 4
MEM_HD = 256
D_MEM = 1024
REL_BUCKETS = 32
REL_MAX_DIST = 128
EPS = 1e-6
NEG_INF = -1e30
D_IN = 12544
SEGMENTS = (("xr", 0, 1024, F32), ("g_rg", 1024, 1024, F32), ("q_s", 2048, 1024, BF), ("kv", 3072, 256, BF),
            ("g_swa", 3328, 1024, F32), ("q_m", 4352, 1024, BF), ("g_mem", 5376, 1024, F32), ("gl", 6400, 6144, F32))
SEG_TILE = 256
D_IN_TILE = 7 * SEG_TILE

ADAM_LR, ADAM_B1, ADAM_B2, ADAM_EPS, ADAM_WD, ADAM_STEP = 0.001, 0.9, 0.999, 1e-08, 0.01, 10

NN = (((1,), (0,)), ((), ()))
NT = (((1,), (1,)), ((), ()))
TN = (((0,), (0,)), ((), ()))
MIB = 2 ** 20


def _dot(a, b, dn):
    return lax.dot_general(a, b, dn, preferred_element_type=F32)


VMEM_LIMIT_MIB = 48
VMEM_LIMIT_LARGE_MIB = 56


def _params(sem=None, large=False):
    return pltpu.CompilerParams(dimension_semantics=sem, vmem_limit_bytes=(VMEM_LIMIT_LARGE_MIB if large else VMEM_LIMIT_MIB) * MIB)


def _sigmoid(z):
    return 1.0 / (1.0 + jnp.exp(-z))


def _softplus(z):
    return jnp.maximum(z, 0.0) + jnp.log(1.0 + jnp.exp(-jnp.abs(z)))


def _expm1(z):
    p = z * (1.0 + z * (0.5 + z * (1.0 / 6 + z * (1.0 / 24 + z * (1.0 / 120 + z * (1.0 / 720 + z * (1.0 / 5040 + z / 40320)))))))
    return jnp.where(jnp.abs(z) < 0.3, p, jnp.exp(z) - 1.0)


def _flat(p):
    return 4 * p[0] + 2 * p[1] + p[2]


def _all_gather_relayed(arrs, relay, name, side=None):
    n = len(arrs)
    K = 9
    work, side_ins, side_outs, side_scratch = side if side is not None else (None, [], [], [])
    n_in, n_out = n + len(side_ins), n + len(side_outs)

    def body(*refs):
        ins, outs = refs[:n], refs[n_in:n_in + n]
        send_sems, recv_sems, local_sems = refs[n_in + n_out:n_in + n_out + 3]
        x, y, c = lax.axis_index("x"), lax.axis_index("y"), lax.axis_index("c")
        me, sib = (x, y, c), (x, y, 1 - c)
        xn, yn, dg = (1 - x, y, c), (x, 1 - y, c), (1 - x, 1 - y, c)

        def other(p):
            return (p[0], p[1], 1 - p[2])

        def rows(a, half):
            h = arrs[a].shape[0] // 2
            return pl.ds(half * h, h)

        def copy(a, k, block, to, half=None, src=None):
            dst = outs[a].at[_flat(block)]
            if half is not None:
                dst = dst.at[rows(a, half)]
            return pltpu.make_async_remote_copy(src_ref=dst if src is None else src, dst_ref=dst,
                                                send_sem=send_sems.at[a * K + k], recv_sem=recv_sems.at[a * K + k],
                                                device_id=to, device_id_type=MESH)

        mine = [pltpu.make_async_copy(ins[a], outs[a].at[_flat(me)], local_sems.at[a]) for a in range(n)]
        for cp in mine:
            cp.start()
        sends = []

        def start(cp):
            cp.start()
            sends.append(cp)

        for a in range(n):
            start(copy(a, 1, me, xn, src=ins[a]))
            start(copy(a, 2, me, yn, src=ins[a]))
            if not relay[a]:
                start(copy(a, 3, me, dg, src=ins[a]))
            start(copy(a, 0, me, sib, src=ins[a]))
        if work is not None:
            work(refs[n:n_in], refs[n_in + n:n_in + n_out], refs[n_in + n_out + 3:])
        for a in range(n):
            copy(a, 1, xn, me).wait_recv()
            if relay[a]:
                start(copy(a, 3, xn, yn, half=0))
            start(copy(a, 5, xn, sib))
        for a in range(n):
            copy(a, 2, yn, me).wait_recv()
            if relay[a]:
                start(copy(a, 4, yn, xn, half=1))
            start(copy(a, 6, yn, sib))
        for a in range(n):
            if relay[a]:
                copy(a, 3, dg, me, half=0).wait_recv()
                start(copy(a, 7, dg, sib, half=0))
                copy(a, 4, dg, me, half=1).wait_recv()
                start(copy(a, 8, dg, sib, half=1))
            else:
                copy(a, 3, dg, me).wait_recv()
                start(copy(a, 7, dg, sib))
        for a in range(n):
            copy(a, 0, sib, me).wait_recv()
            copy(a, 5, other(xn), me).wait_recv()
            copy(a, 6, other(yn), me).wait_recv()
            if relay[a]:
                copy(a, 7, other(dg), me, half=0).wait_recv()
                copy(a, 8, other(dg), me, half=1).wait_recv()
            else:
                copy(a, 7, other(dg), me).wait_recv()
        for cp in sends:
            cp.wait_send()
        for cp in mine:
            cp.wait()

    any_spec = pl.BlockSpec(memory_space=pl.ANY)
    return pl.pallas_call(
        body, name=name,
        out_shape=[jax.ShapeDtypeStruct((N_DEV,) + a.shape, a.dtype) for a in arrs] + list(side_outs),
        in_specs=[any_spec] * n_in, out_specs=[any_spec] * n_out,
        scratch_shapes=[pltpu.SemaphoreType.DMA((K * n,)), pltpu.SemaphoreType.DMA((K * n,)), pltpu.SemaphoreType.DMA((n,))]
        + list(side_scratch),
        compiler_params=_params(),
    )(*arrs, *side_ins)


def _chip_peers(x, y):
    return [(1 - x, y), (x, 1 - y), (1 - x, 1 - y)]


def _chip(p):
    return 2 * p[0] + p[1]


def _plan_gather(n):
    def plan(x, y, c):
        out = []
        for a in range(n):
            for peer in [(x, y, 1 - c)] + [(*ch, c) for ch in _chip_peers(x, y)]:
                out.append((a, None, ("lead", _flat((x, y, c))), peer, ("lead", _flat(peer))))
        return out
    return plan


def _plan_everyone(n):
    def plan(x, y, c):
        out = []
        for a in range(n):
            for r in range(1, N_DEV):
                peer = (1 - x if r & 4 else x, 1 - y if r & 2 else y, 1 - c if r & 1 else c)
                out.append((a, None, ("lead", _flat((x, y, c))), peer, ("lead", _flat(peer))))
        return out
    return plan


def _plan_swap(ndims):
    def plan(x, y, c):
        out = []
        for a, nd in enumerate(ndims):
            if nd == 4:
                out.append((a, 1 - c, ("all", 0), (x, y, 1 - c), ("all", 0)))
            else:
                out += [(a, 2 * j + 1 - c, ("lead", j), (x, y, 1 - c), ("lead", j)) for j in range(4)]
        return out
    return plan


def _slot(ref, where):
    kind, k = where
    return ref if kind == "all" else ref.at[k]


def _plan_scatter(n):
    def plan(x, y, c):
        out = []
        for a in range(n):
            for ch in _chip_peers(x, y):
                out.append((a, _chip(ch), ("lead", _chip((x, y))), (*ch, c), ("lead", _chip(ch))))
        return out
    return plan


HBM_SPEC = pl.BlockSpec(memory_space=pltpu.HBM)
SEM_SPEC = pl.BlockSpec(memory_space=pltpu.SEMAPHORE)


def _in_hbm(a):
    return pltpu.with_memory_space_constraint(a, pltpu.HBM)


def _exchange_start(srcs, lands, plan, name):
    n = len(lands)
    ns = 0 if srcs is None else n
    count = len(plan(0, 0, 0))

    def body(*refs):
        land_refs = refs[ns:ns + n]
        src_refs = land_refs if srcs is None else refs[:n]
        send_sems, recv_sems = refs[ns + n], refs[ns + n + 1]
        token = refs[-1]
        x, y, c = lax.axis_index("x"), lax.axis_index("y"), lax.axis_index("c")
        for k, (a, si, di, peer, _) in enumerate(plan(x, y, c)):
            src = src_refs[a] if si is None else src_refs[a].at[si]
            pltpu.make_async_remote_copy(src_ref=src, dst_ref=_slot(land_refs[a], di), send_sem=send_sems.at[k],
                                         recv_sem=recv_sems.at[k], device_id=peer, device_id_type=MESH).start()
        token[...] = jnp.zeros_like(token)

    out = pl.pallas_call(
        body, name=name,
        out_shape=(pltpu.SemaphoreType.DMA((count,)), pltpu.SemaphoreType.DMA((count,)),
                   *[pltpu.HBM(a.shape, a.dtype) for a in lands], jax.ShapeDtypeStruct((8, 128), F32)),
        in_specs=[HBM_SPEC] * (ns + n),
        out_specs=(SEM_SPEC, SEM_SPEC, *([HBM_SPEC] * n), pl.BlockSpec(memory_space=pltpu.VMEM)),
        input_output_aliases={ns + i: 2 + i for i in range(n)},
        compiler_params=pltpu.CompilerParams(has_side_effects=pltpu.SideEffectType.DATAFLOW_SIDE_EFFECTING),
    )(*[_in_hbm(a) for a in (srcs or [])], *[_in_hbm(a) for a in lands])
    return out[0], out[1], srcs if srcs is None else list(srcs), list(out[2:2 + n]), out[-1]


def _exchange_wait(send_sems, recv_sems, srcs, lands, plan, after, name):
    n = len(lands)
    ns = 0 if srcs is None else n

    def body(*refs):
        land_refs = refs[ns:ns + n]
        src_refs = land_refs if srcs is None else refs[:n]
        send_sems, recv_sems = refs[ns + n], refs[ns + n + 1]
        x, y, c = lax.axis_index("x"), lax.axis_index("y"), lax.axis_index("c")
        for k, (a, si, _, peer, ri) in enumerate(plan(x, y, c)):
            src = src_refs[a] if si is None else src_refs[a].at[si]
            cp = pltpu.make_async_remote_copy(src_ref=src, dst_ref=_slot(land_refs[a], ri), send_sem=send_sems.at[k],
                                              recv_sem=recv_sems.at[k], device_id=peer, device_id_type=MESH)
            cp.wait_send()
            cp.wait_recv()

    out = pl.pallas_call(
        body, name=name,
        out_shape=tuple(pltpu.HBM(a.shape, a.dtype) for a in lands),
        in_specs=[HBM_SPEC] * (ns + n) + [SEM_SPEC, SEM_SPEC, pl.BlockSpec(memory_space=pl.ANY)],
        out_specs=tuple([HBM_SPEC] * n),
        input_output_aliases={ns + i: i for i in range(n)},
        compiler_params=pltpu.CompilerParams(has_side_effects=pltpu.SideEffectType.DATAFLOW_SIDE_EFFECTING),
    )(*[_in_hbm(a) for a in (srcs or [])], *lands, send_sems, recv_sems, after)
    return list(out)


def _plan_forward(n):
    def plan(x, y, c):
        return [(a, _flat((*ch, c)), ("lead", _flat((*ch, c))), (x, y, 1 - c), ("lead", _flat((*ch, 1 - c))))
                for a in range(n) for ch in _chip_peers(x, y)]
    return plan


def _place_own(zones, owns, slot, name):
    n = len(zones)

    def body(slot_ref, *refs):
        for a in range(n):
            refs[2 * n + a][...] = refs[a][...]

    return pl.pallas_call(
        body, name=name,
        grid_spec=pltpu.PrefetchScalarGridSpec(
            num_scalar_prefetch=1, grid=(1,),
            in_specs=[pl.BlockSpec(o.shape, lambda i, s_ref: (0, 0)) for o in owns] + [pl.BlockSpec(memory_space=pl.ANY)] * n,
            out_specs=[pl.BlockSpec((None,) + o.shape, lambda i, s_ref: (s_ref[0], 0, 0)) for o in owns]),
        out_shape=[jax.ShapeDtypeStruct(z.shape, z.dtype) for z in zones],
        input_output_aliases={1 + n + a: a for a in range(n)},
        compiler_params=_params(("arbitrary",)),
    )(slot, *owns, *zones)


def _pair_sum(parts, got, core, name):
    R, C = parts.shape[-2:]
    mine = (pl.BlockSpec((None, None, R, C), lambda j, c_ref: (c_ref[0], j, 0, 0)) if parts.ndim == 4
            else pl.BlockSpec((None, R, C), lambda j, c_ref: (2 * j + c_ref[0], 0, 0)))

    def body(c_ref, p_ref, g_ref, o_ref):
        o_ref[...] = (p_ref[...].astype(F32) + g_ref[...].astype(F32)).astype(o_ref.dtype)

    return pl.pallas_call(
        body, name=name,
        grid_spec=pltpu.PrefetchScalarGridSpec(
            num_scalar_prefetch=1, grid=(4,),
            in_specs=[mine, pl.BlockSpec((None, R, C), lambda j, c_ref: (j, 0, 0))],
            out_specs=pl.BlockSpec((None, R, C), lambda j, c_ref: (j, 0, 0))),
        out_shape=pltpu.HBM((4, R, C), parts.dtype),
        compiler_params=_params(("parallel",)),
    )(core, parts, got)


def _matmul(a, b, mode, M, N, K, tm, tn, tk, out_dtype, name, b_noff=0, out_blocked=None, dep=None, out_hbm=False):
    nm, nn, nk = M // tm, N // tn, K // tk
    if mode == "nn":
        a_spec = pl.BlockSpec((tm, tk), lambda j, i, k: (i, k))
        b_spec = pl.BlockSpec((tk, tn), lambda j, i, k: (k, j + b_noff))
        dn = NN
    elif mode == "nt":
        a_spec = pl.BlockSpec((tm, tk), lambda j, i, k: (i, k))
        b_spec = pl.BlockSpec((tn, tk), lambda j, i, k: (j + b_noff, k))
        dn = NT
    else:
        a_spec = pl.BlockSpec((tk, tm), lambda j, i, k: (k, i))
        b_spec = pl.BlockSpec((tk, tn), lambda j, i, k: (k, j + b_noff))
        dn = TN
    if out_blocked == "col":
        out_shape = jax.ShapeDtypeStruct((2, 4, M, tn), out_dtype)
        out_spec = pl.BlockSpec((None, None, tm, tn), lambda j, i, k: (j % 2, j // 2, i, 0))
    elif out_blocked == "row":
        out_shape = jax.ShapeDtypeStruct((2, 4, tm, N), out_dtype)
        out_spec = pl.BlockSpec((None, None, tm, tn), lambda j, i, k: (i % 2, i // 2, 0, j))
    else:
        out_shape = jax.ShapeDtypeStruct((M, N), out_dtype)
        out_spec = pl.BlockSpec((tm, tn), lambda j, i, k: (i, j))
    if out_hbm or out_blocked is not None:
        out_shape = pltpu.HBM(out_shape.shape, out_shape.dtype)

    n_extra = int(dep is not None)

    def body(a_ref, b_ref, *rest):
        o_ref, scratch = rest[n_extra], rest[n_extra + 1:]
        if nk == 1:
            o_ref[...] = _dot(a_ref[...], b_ref[...], dn).astype(out_dtype)
        else:
            acc_ref, = scratch
            k = pl.program_id(2)

            @pl.when(k == 0)
            def _():
                acc_ref[...] = jnp.zeros_like(acc_ref)

            acc_ref[...] += _dot(a_ref[...], b_ref[...], dn)

            @pl.when(k == nk - 1)
            def _():
                o_ref[...] = acc_ref[...].astype(out_dtype)

    return pl.pallas_call(
        body, name=name, grid=(nn, nm, nk),
        in_specs=[a_spec, b_spec] + ([] if dep is None else [pl.BlockSpec((8, 128), lambda j, i, k: (0, 0))]),
        out_specs=out_spec, out_shape=out_shape,
        scratch_shapes=[] if nk == 1 else [pltpu.VMEM((tm, tn), F32)],
        compiler_params=_params(("parallel", "parallel", "arbitrary")),
    )(a, b, *([] if dep is None else [dep]))


RMS_SIDE_ROWS = 512


def _rms_side(pairs):
    chunks = [min(x.shape[0], RMS_SIDE_ROWS) for x, _ in pairs]

    def work(ins, outs, scratch):
        sem = scratch[-1]

        def move(src, dst):
            cp = pltpu.make_async_copy(src, dst, sem.at[0])
            cp.start()
            cp.wait()

        for p, ((x, _), tr) in enumerate(zip(pairs, chunks)):
            x_ref, g_ref, h_ref = ins[2 * p], ins[2 * p + 1], outs[p]
            xv, gv, hv = scratch[3 * p:3 * p + 3]
            move(g_ref, gv)
            for i in range(x.shape[0] // tr):
                rows = pl.ds(i * tr, tr)
                move(x_ref.at[rows], xv)
                v = xv[...]
                hv[...] = (v * lax.rsqrt(jnp.mean(v * v, axis=-1, keepdims=True) + EPS) * gv[...]).astype(BF)
                move(hv, h_ref.at[rows])

    scratch = [s for (x, g), tr in zip(pairs, chunks)
               for s in (pltpu.VMEM((tr, x.shape[1]), F32), pltpu.VMEM(g.shape, F32), pltpu.VMEM((tr, x.shape[1]), BF))]
    return (work, [a for pair in pairs for a in pair], [jax.ShapeDtypeStruct(x.shape, BF) for x, _ in pairs],
            scratch + [pltpu.SemaphoreType.DMA((1,))])


def _rms_gain_grad(dn, x, name):
    R, Dm = x.shape

    def body(dn_ref, x_ref, o_ref):
        xv = x_ref[...]
        r = lax.rsqrt(jnp.mean(xv * xv, axis=-1, keepdims=True) + EPS)
        o_ref[...] = jnp.sum(dn_ref[...] * xv * r, axis=0, keepdims=True)

    return pl.pallas_call(
        body, name=name, out_shape=jax.ShapeDtypeStruct((1, Dm), F32),
        compiler_params=_params(),
    )(dn, x)


def _shift_down(v, k, head8, row, T):
    if k == 0:
        return v
    r = pltpu.roll(v, k, 0)
    hr = pltpu.roll(head8, k, 0)
    top = jnp.where(row[:8] < k, hr, r[:8])
    return jnp.concatenate([top, r[8:]], axis=0)


def _shift_up(v, k, tail8, row, T):
    if k == 0:
        return v
    r = pltpu.roll(v, T - k, 0)
    tr = pltpu.roll(tail8, 8 - k, 0)
    bot = jnp.where(row[:8] >= 8 - k, tr, r[T - 8:])
    return jnp.concatenate([r[:T - 8], bot], axis=0)


def _rglru_gates(u, head8, grow, row, T, cw_ref, cb_ref, wa_ref, ba_ref, wx_ref, bx_ref, lam_ref):
    us = [_shift_down(u, k, head8, row, T) for k in range(CONV_W)]
    acc = us[0] * cw_ref[0:1, :]
    for k in range(1, CONV_W):
        acc = acc + us[k] * cw_ref[k:k + 1, :]
    conv = cb_ref[...] + acc
    cbf = conv.astype(BF)
    r_ = _sigmoid(_dot(cbf, wa_ref[0], NN) + ba_ref[...])
    i_ = _sigmoid(_dot(cbf, wx_ref[0], NN) + bx_ref[...])
    sp = _softplus(-lam_ref[...])
    la = -LRU_C * r_ * sp
    a = jnp.exp(la)
    mult_raw = jnp.sqrt(-_expm1(2.0 * la))
    mult = jnp.where(grow == 0, 1.0, mult_raw)
    return us, conv, cbf, r_, i_, sp, a, mult_raw, mult


def _rglru_specs(T, nt, rev):
    tmap = (lambda n, t: (nt - 1 - t, n)) if rev else (lambda n, t: (t, n))
    hmap = ((lambda n, t: (jnp.maximum((nt - 1 - t) * (T // 8) - 1, 0), n)) if rev
            else (lambda n, t: (jnp.maximum(t * (T // 8) - 1, 0), n)))
    tile = pl.BlockSpec((T, RNN_BLOCK), tmap)
    halo = pl.BlockSpec((8, RNN_BLOCK), hmap)
    vec = pl.BlockSpec((1, RNN_BLOCK), lambda n, t: (0, n))
    cw = pl.BlockSpec((CONV_W, RNN_BLOCK), lambda n, t: (0, n))
    wblk = pl.BlockSpec((1, RNN_BLOCK, RNN_BLOCK), lambda n, t: (n, 0, 0))
    return tile, halo, vec, cw, wblk


def _rglru_fwd(xr, g, cw, cb, wa, ba, wx, bx, lam, T):
    S = xr.shape[0]
    nt = S // T

    def body(u_ref, uh_ref, g_ref, cw_ref, cb_ref, wa_ref, ba_ref, wx_ref, bx_ref, lam_ref, h_ref, y_ref, carry):
        t = pl.program_id(1)

        @pl.when(t == 0)
        def _():
            carry[...] = jnp.zeros_like(carry)

        row = lax.broadcasted_iota(jnp.int32, (T, RNN_BLOCK), 0)
        grow = row + t * T
        head8 = jnp.where(t > 0, uh_ref[...], 0.0)
        _, conv, _, _, i_, _, a, _, mult = _rglru_gates(u_ref[...], head8, grow, row, T, cw_ref, cb_ref, wa_ref, ba_ref,
                                                         wx_ref, bx_ref, lam_ref)
        b = mult * i_ * conv
        s = 1
        while s < T:
            keep = row >= s
            a_s = jnp.where(keep, pltpu.roll(a, s, 0), 1.0)
            b_s = jnp.where(keep, pltpu.roll(b, s, 0), 0.0)
            b = a * b_s + b
            a = a * a_s
            s *= 2
        h = b + a * carry[0:1, :]
        carry[...] = jnp.broadcast_to(h[T - 1:T, :], carry.shape)
        h_ref[...] = h
        gv = g_ref[...]
        y_ref[...] = (h * (gv * _sigmoid(gv))).astype(BF)

    tile, halo, vec, cwspec, wblk = _rglru_specs(T, nt, False)
    return pl.pallas_call(
        body, name="rglru_fwd", grid=(RNN_BLOCKS, nt),
        in_specs=[tile, halo, tile, cwspec, vec, wblk, vec, wblk, vec, vec],
        out_specs=[tile, tile],
        out_shape=[jax.ShapeDtypeStruct((S, D_RNN), F32), jax.ShapeDtypeStruct((S, D_RNN), BF)],
        scratch_shapes=[pltpu.VMEM((8, RNN_BLOCK), F32)],
        compiler_params=_params(("parallel", "arbitrary")),
    )(xr, xr, g, cw, cb, wa, ba, wx, bx, lam)


def _rglru_bwd(xr, g, h, dy, cw, cb, wa, ba, wx, bx, lam, T):
    S = xr.shape[0]
    nt = S // T

    def body(u_ref, uh_ref, g_ref, h_ref, hh_ref, dy_ref, cw_ref, cb_ref, wa_ref, ba_ref, wx_ref, bx_ref, lam_ref,
             du_ref, dg_ref, dwa_ref, dwx_ref, dvec_ref, c_dhh, c_a, c_dconv):
        t = pl.program_id(1)
        tt = nt - 1 - t

        @pl.when(t == 0)
        def _():
            c_dhh[...] = jnp.zeros_like(c_dhh)
            c_a[...] = jnp.zeros_like(c_a)
            c_dconv[...] = jnp.zeros_like(c_dconv)
            dwa_ref[...] = jnp.zeros_like(dwa_ref)
            dwx_ref[...] = jnp.zeros_like(dwx_ref)
            dvec_ref[...] = jnp.zeros_like(dvec_ref)

        row = lax.broadcasted_iota(jnp.int32, (T, RNN_BLOCK), 0)
        row8 = row[:8]
        grow = row + tt * T
        head8 = jnp.where(tt > 0, uh_ref[...], 0.0)
        us, conv, cbf, r_, i_, sp, a, mult_raw, mult = _rglru_gates(
            u_ref[...], head8, grow, row, T, cw_ref, cb_ref, wa_ref, ba_ref, wx_ref, bx_ref, lam_ref)
        hv = h_ref[...]
        hprev = _shift_down(hv, 1, jnp.where(tt > 0, hh_ref[...], 0.0), row, T)
        gv = g_ref[...]
        sg = _sigmoid(gv)
        dyv = dy_ref[...]
        dg_ref[...] = (dyv * hv * (sg * (1.0 + gv * (1.0 - sg)))).astype(BF)
        d = dyv * (gv * sg)
        A = _shift_up(a, 1, c_a[...], row, T)
        s = 1
        while s < T:
            keep = row < T - s
            A_s = jnp.where(keep, pltpu.roll(A, T - s, 0), 1.0)
            d_s = jnp.where(keep, pltpu.roll(d, T - s, 0), 0.0)
            d = A * d_s + d
            A = A * A_s
            s *= 2
        dhh = d + A * c_dhh[0:1, :]
        da = dhh * hprev
        dconv = dhh * mult * i_
        di = dhh * mult * conv
        dmult = dhh * i_ * conv
        dla = da * a - jnp.where(grow == 0, 0.0, dmult * (a * a) / mult_raw)
        dr = dla * (-LRU_C * sp)
        dsp = jnp.sum(dla * (-LRU_C * r_), axis=0, keepdims=True)
        dza = dr * r_ * (1.0 - r_)
        dzx = di * i_ * (1.0 - i_)
        dza_b, dzx_b = dza.astype(BF), dzx.astype(BF)
        dconv = dconv + _dot(dza_b, wa_ref[0], NT) + _dot(dzx_b, wx_ref[0], NT)
        dwa_ref[0] += _dot(cbf, dza_b, TN)
        dwx_ref[0] += _dot(cbf, dzx_b, TN)
        lam = lam_ref[...]
        rows = [jnp.sum(dconv * us[k], axis=0, keepdims=True) for k in range(CONV_W)]
        rows += [jnp.sum(dconv, axis=0, keepdims=True), jnp.sum(dza, axis=0, keepdims=True),
                 jnp.sum(dzx, axis=0, keepdims=True), dsp * (-_sigmoid(-lam))]
        upd = jnp.zeros((8, RNN_BLOCK), F32)
        for j, rv in enumerate(rows):
            upd = upd + jnp.where(row8 == j, rv, 0.0)
        dvec_ref[...] += upd
        tail8 = c_dconv[...]
        du = dconv * cw_ref[0:1, :]
        for k in range(1, CONV_W):
            du = du + _shift_up(dconv, k, tail8, row, T) * cw_ref[k:k + 1, :]
        du_ref[...] = du.astype(BF)
        c_dhh[...] = jnp.broadcast_to(dhh[0:1, :], c_dhh.shape)
        c_a[...] = jnp.broadcast_to(a[0:1, :], c_a.shape)
        c_dconv[...] = dconv[:8]

    tile, halo, vec, cwspec, wblk = _rglru_specs(T, nt, True)
    acc8 = pl.BlockSpec((8, RNN_BLOCK), lambda n, t: (0, n))
    return pl.pallas_call(
        body, name="rglru_bwd", grid=(RNN_BLOCKS, nt),
        in_specs=[tile, halo, tile, tile, halo, tile, cwspec, vec, wblk, vec, wblk, vec, vec],
        out_specs=[tile, tile, wblk, wblk, acc8],
        out_shape=[jax.ShapeDtypeStruct((S, D_RNN), BF), jax.ShapeDtypeStruct((S, D_RNN), BF),
                   jax.ShapeDtypeStruct((RNN_BLOCKS, RNN_BLOCK, RNN_BLOCK), F32),
                   jax.ShapeDtypeStruct((RNN_BLOCKS, RNN_BLOCK, RNN_BLOCK), F32),
                   jax.ShapeDtypeStruct((8, D_RNN), F32)],
        scratch_shapes=[pltpu.VMEM((8, RNN_BLOCK), F32)] * 3,
        compiler_params=_params(("parallel", "arbitrary")),
    )(xr, xr, g, h, h, dy, cw, cb, wa, ba, wx, bx, lam)


def _rel_bucket_map():
    qi = np.arange(WINDOW)[:, None]
    kj = np.arange(2 * WINDOW)[None, :]
    dist = jnp.asarray(qi + WINDOW - kj, jnp.int32)
    n = jnp.maximum(dist, 0)
    max_exact = REL_BUCKETS // 2
    ratio = jnp.log(jnp.maximum(n, 1).astype(F32) / max_exact) / math.log(REL_MAX_DIST / max_exact)
    large = jnp.minimum(max_exact + (ratio * (REL_BUCKETS - max_exact)).astype(jnp.int32), REL_BUCKETS - 1)
    bucket = jnp.where(n < max_exact, n, large).astype(jnp.int32)
    j = np.arange(WINDOW)[None, :]
    return jnp.where(jnp.asarray(j > qi), bucket[:, :WINDOW], bucket[:, WINDOW:])


def _swa_common(n, kv_ref, bucket_ref, relb_ref, bias_scr):
    @pl.when(n == 0)
    def _():
        bk = bucket_ref[...]
        for h in range(SWA_HEADS):
            acc = jnp.zeros((WINDOW, WINDOW), F32)
            for b in range(REL_BUCKETS):
                acc = acc + jnp.where(bk == b, relb_ref[b, h], 0.0)
            bias_scr[h] = acc

    prev0 = pl.multiple_of(jnp.maximum(n - 1, 0) * WINDOW, WINDOW)
    cur0 = pl.multiple_of(n * WINDOW, WINDOW)
    kk = jnp.concatenate([kv_ref[pl.ds(prev0, WINDOW), :], kv_ref[pl.ds(cur0, WINDOW), :]], axis=0).astype(F32)
    rowi = lax.broadcasted_iota(jnp.int32, (WINDOW, WINDOW), 0)
    col = lax.broadcasted_iota(jnp.int32, (WINDOW, WINDOW), 1)
    from_prev = col > rowi
    return kk, from_prev, prev0, cur0


def _fold(full, from_prev):
    return jnp.where(from_prev, full[:, :WINDOW], full[:, WINDOW:])


def _unfold(sq, from_prev):
    return jnp.concatenate([jnp.where(from_prev, sq, 0.0), jnp.where(from_prev, 0.0, sq)], axis=1)


def _half_pair(part, kvh):
    lo = lax.broadcasted_iota(jnp.int32, part.shape, 1) < SWA_HD
    if kvh == 0:
        pa = jnp.where(lo, part, 0.0)
        pb = pltpu.roll(pa, SWA_HD, 1)
    else:
        pb = jnp.where(lo, 0.0, part)
        pa = pltpu.roll(pb, SWA_HD, 1)
    return pa.astype(BF), pb.astype(BF)


ALL_HEADS = SWA_HEADS * WINDOW


def _sink_column(sinks):
    return jnp.repeat(sinks.reshape(SWA_HEADS), WINDOW).reshape(ALL_HEADS, 1)


def _swa_operands(kk):
    return [(_half_pair(kk[:, :128], kvh), _half_pair(kk[:, 128:], kvh)) for kvh in range(SWA_KV_HEADS)]


def _swa_probs(n, q_ref, ops, bias_scr, sinkc_ref, from_prev):
    lgs = []
    for kvh in range(SWA_KV_HEADS):
        (ka, kb), _ = ops[kvh]
        for p in range(4):
            q2 = q_ref[:, kvh * 512 + p * 128:kvh * 512 + p * 128 + 128]
            lgs += [_fold(_dot(q2, ka, NT), from_prev), _fold(_dot(q2, kb, NT), from_prev)]
    lg = jnp.concatenate(lgs, axis=0) * (SWA_HD ** -0.5) + bias_scr[...].reshape(ALL_HEADS, WINDOW)
    rowi = jnp.bitwise_and(lax.broadcasted_iota(jnp.int32, (ALL_HEADS, WINDOW), 0), WINDOW - 1)
    col = lax.broadcasted_iota(jnp.int32, (ALL_HEADS, WINDOW), 1)
    no_prev = jnp.where(n > 0, 0, 4 * WINDOW)
    lg = jnp.where(jnp.logical_or(col <= rowi, col > rowi + no_prev), lg, NEG_INF)
    sink = sinkc_ref[...]
    m = jnp.maximum(jnp.max(lg, axis=-1, keepdims=True), sink)
    e = jnp.exp(lg - m)
    es = jnp.exp(sink - m)
    den = jnp.sum(e, axis=-1, keepdims=True) + es
    return e / den, es / den


def _swa_fwd(q, kv, g, bucket, rel_bias, sink_col):
    S = q.shape[0]
    nb = S // WINDOW

    def body(q_ref, kv_ref, g_ref, bucket_ref, relb_ref, sinkc_ref, o_ref, y_ref, bias_scr):
        n = pl.program_id(0)
        kk, from_prev, _, _ = _swa_common(n, kv_ref, bucket_ref, relb_ref, bias_scr)
        ops = _swa_operands(kk)
        pr, _ = _swa_probs(n, q_ref, ops, bias_scr, sinkc_ref, from_prev)
        for kvh in range(SWA_KV_HEADS):
            _, (va, vb) = ops[kvh]
            for p in range(4):
                c0 = kvh * 512 + p * 128
                r0 = (kvh * 8 + 2 * p) * WINDOW
                o2 = (_dot(_unfold(pr[r0:r0 + WINDOW], from_prev).astype(BF), va, NN)
                      + _dot(_unfold(pr[r0 + WINDOW:r0 + 2 * WINDOW], from_prev).astype(BF), vb, NN))
                o_ref[:, c0:c0 + 128] = o2
                gv = g_ref[:, c0:c0 + 128]
                y_ref[:, c0:c0 + 128] = (o2 * (gv * _sigmoid(gv))).astype(BF)

    blk = pl.BlockSpec((WINDOW, 1024), lambda n: (n, 0))
    smem = pl.BlockSpec(memory_space=pltpu.SMEM)
    sinkc = pl.BlockSpec((ALL_HEADS, 1), lambda n: (0, 0))
    return pl.pallas_call(
        body, name="swa_fwd", grid=(nb,),
        in_specs=[blk, pl.BlockSpec((S, 256), lambda n: (0, 0)), blk, pl.BlockSpec((WINDOW, WINDOW), lambda n: (0, 0)), smem, sinkc],
        out_specs=[blk, blk],
        out_shape=[jax.ShapeDtypeStruct((S, 1024), F32), jax.ShapeDtypeStruct((S, 1024), BF)],
        scratch_shapes=[pltpu.VMEM((SWA_HEADS, WINDOW, WINDOW), F32)],
        compiler_params=_params(("arbitrary",)),
    )(q, kv, g, bucket, rel_bias, sink_col)


def _swa_bwd(q, kv, g, o, dy, bucket, rel_bias, sink_col, others):
    S = q.shape[0]
    nb = S // WINDOW
    first_col = {name: c0 for name, c0, _, _ in SEGMENTS}
    col_q, col_g = first_col["q_s"], first_col["g_swa"]
    copies = []
    for name, c0, width, _ in SEGMENTS:
        if name not in ("q_s", "kv", "g_swa"):
            arrs = others[name] if name == "gl" else (others[name],)
            copies += [(a, c0 + i * (width // len(arrs))) for i, a in enumerate(arrs)]

    def body(q_ref, kv_ref, g_ref, o_ref, dy_ref, bucket_ref, relb_ref, sinkc_ref, *refs):
        copy_refs = refs[:len(copies)]
        dp_ref, dkv_ref, dsink_ref, drel_ref, bias_scr, dbias_scr, dsink_scr = refs[len(copies):]
        n = pl.program_id(0)
        for c_ref, (a, c0) in zip(copy_refs, copies):
            dp_ref[:, c0:c0 + a.shape[1]] = c_ref[...]

        @pl.when(n == 0)
        def _():
            dbias_scr[...] = jnp.zeros_like(dbias_scr)
            dsink_scr[...] = jnp.zeros_like(dsink_scr)
            dkv_ref[...] = jnp.zeros_like(dkv_ref)

        kk, from_prev, prev0, cur0 = _swa_common(n, kv_ref, bucket_ref, relb_ref, bias_scr)
        ops = _swa_operands(kk)
        pr, ps = _swa_probs(n, q_ref, ops, bias_scr, sinkc_ref, from_prev)
        do2s, dps = [], []
        for kvh in range(SWA_KV_HEADS):
            _, (va, vb) = ops[kvh]
            for p in range(4):
                c0 = kvh * 512 + p * 128
                gv = g_ref[:, c0:c0 + 128]
                sg = _sigmoid(gv)
                dyv = dy_ref[:, c0:c0 + 128]
                dp_ref[:, col_g + c0:col_g + c0 + 128] = (dyv * o_ref[:, c0:c0 + 128] * (sg * (1.0 + gv * (1.0 - sg)))).astype(BF)
                do2 = (dyv * (gv * sg)).astype(BF)
                do2s.append(do2)
                dps += [_fold(_dot(do2, va, NT), from_prev), _fold(_dot(do2, vb, NT), from_prev)]
        dp = jnp.concatenate(dps, axis=0)
        delta = jnp.sum(pr * dp, axis=-1, keepdims=True)
        ds = pr * (dp - delta)
        dbias_scr[...] += ds.reshape(SWA_HEADS, WINDOW, WINDOW)
        dsink_scr[...] += ps * delta
        dsc = ds * (SWA_HD ** -0.5)
        lo256 = lax.broadcasted_iota(jnp.int32, (2 * WINDOW, 128), 1) < SWA_HD
        dks, dvs = [], []
        for kvh in range(SWA_KV_HEADS):
            (ka, kb), _ = ops[kvh]
            dka = jnp.zeros((2 * WINDOW, 128), F32)
            dkb, dva, dvb = dka, dka, dka
            for p in range(4):
                c0 = kvh * 512 + p * 128
                r0 = (kvh * 8 + 2 * p) * WINDOW
                q2 = q_ref[:, c0:c0 + 128]
                do2 = do2s[kvh * 4 + p]
                ds0 = _unfold(dsc[r0:r0 + WINDOW], from_prev).astype(BF)
                ds1 = _unfold(dsc[r0 + WINDOW:r0 + 2 * WINDOW], from_prev).astype(BF)
                dp_ref[:, col_q + c0:col_q + c0 + 128] = (_dot(ds0, ka, NN) + _dot(ds1, kb, NN)).astype(BF)
                dka = dka + _dot(ds0, q2, TN)
                dkb = dkb + _dot(ds1, q2, TN)
                dva = dva + _dot(_unfold(pr[r0:r0 + WINDOW], from_prev).astype(BF), do2, TN)
                dvb = dvb + _dot(_unfold(pr[r0 + WINDOW:r0 + 2 * WINDOW], from_prev).astype(BF), do2, TN)
            dks.append(jnp.where(lo256, dka, 0.0) + pltpu.roll(jnp.where(lo256, 0.0, dkb), SWA_HD, 1))
            dvs.append(jnp.where(lo256, dva, 0.0) + pltpu.roll(jnp.where(lo256, 0.0, dvb), SWA_HD, 1))
        dk = dks[0] + pltpu.roll(dks[1], SWA_HD, 1)
        dv = dvs[0] + pltpu.roll(dvs[1], SWA_HD, 1)
        dkv_ref[pl.ds(prev0, WINDOW), 0:128] += dk[:WINDOW]
        dkv_ref[pl.ds(prev0, WINDOW), 128:256] += dv[:WINDOW]
        dkv_ref[pl.ds(cur0, WINDOW), 0:128] += dk[WINDOW:]
        dkv_ref[pl.ds(cur0, WINDOW), 128:256] += dv[WINDOW:]

        @pl.when(n == nb - 1)
        def _():
            dsink_ref[...] = -jnp.sum(dsink_scr[...].reshape(SWA_HEADS, WINDOW, 1), axis=1)
            bk = bucket_ref[...]
            sums = []
            for b in range(REL_BUCKETS):
                sums.append(jnp.sum(jnp.where((bk == b)[None], dbias_scr[...], 0.0), axis=1))
            drel_ref[...] = jnp.sum(jnp.concatenate(sums, axis=0), axis=1, keepdims=True)

    blk = pl.BlockSpec((WINDOW, 1024), lambda n: (n, 0))
    smem = pl.BlockSpec(memory_space=pltpu.SMEM)
    whole = lambda shape: pl.BlockSpec(shape, lambda n: (0, 0))
    return pl.pallas_call(
        body, name="swa_bwd", grid=(nb,),
        in_specs=[blk, whole((S, 256)), blk, blk, blk, whole((WINDOW, WINDOW)), smem, whole((ALL_HEADS, 1))]
        + [pl.BlockSpec((WINDOW, a.shape[1]), lambda n: (n, 0)) for a, _ in copies],
        out_specs=[pl.BlockSpec((WINDOW, D_IN), lambda n: (n, 0)), whole((S, 256)), whole((SWA_HEADS, 1)),
                   whole((REL_BUCKETS * SWA_HEADS, 1))],
        out_shape=[jax.ShapeDtypeStruct((S, D_IN), BF), jax.ShapeDtypeStruct((S, 256), F32), jax.ShapeDtypeStruct((SWA_HEADS, 1), F32),
                   jax.ShapeDtypeStruct((REL_BUCKETS * SWA_HEADS, 1), F32)],
        scratch_shapes=[pltpu.VMEM((SWA_HEADS, WINDOW, WINDOW), F32), pltpu.VMEM((SWA_HEADS, WINDOW, WINDOW), F32),
                        pltpu.VMEM((ALL_HEADS, 1), F32)],
        compiler_params=_params(("arbitrary",)),
    )(q, kv, g, o, dy, bucket, rel_bias, sink_col, *[a for a, _ in copies])


MEM_TQ = 512


def _mem_probs(qh, mk):
    lg = _dot(qh, mk, NT) * (MEM_HD ** -0.5)
    e = jnp.exp(lg - jnp.max(lg, axis=-1, keepdims=True))
    return e / jnp.sum(e, axis=-1, keepdims=True)


def _mem_fwd(q, mkv, g):
    S = q.shape[0]
    M = mkv.shape[0]
    tq = min(S, MEM_TQ)

    def body(q_ref, mkv_ref, g_ref, o_ref, y_ref):
        for h in range(MEM_HEADS):
            c0 = h * MEM_HD
            pr = _mem_probs(q_ref[:, c0:c0 + MEM_HD], mkv_ref[:, c0:c0 + MEM_HD])
            o = _dot(pr.astype(BF), mkv_ref[:, D_MEM + c0:D_MEM + c0 + MEM_HD], NN)
            o_ref[:, c0:c0 + MEM_HD] = o
            gv = g_ref[:, c0:c0 + MEM_HD]
            y_ref[:, c0:c0 + MEM_HD] = (o * (gv * _sigmoid(gv))).astype(BF)

    blk = pl.BlockSpec((tq, D_MEM), lambda i: (i, 0))
    return pl.pallas_call(
        body, name="mem_fwd", grid=(S // tq,),
        in_specs=[blk, pl.BlockSpec((M, 2 * D_MEM), lambda i: (0, 0)), blk], out_specs=[blk, blk],
        out_shape=[jax.ShapeDtypeStruct((S, D_MEM), F32), jax.ShapeDtypeStruct((S, D_MEM), BF)],
        compiler_params=_params(("parallel",)),
    )(q, mkv, g)


def _mem_bwd(q, mkv, g, o, dy):
    S = q.shape[0]
    M = mkv.shape[0]
    tq = min(S, MEM_TQ)

    def body(q_ref, mkv_ref, g_ref, o_ref, dy_ref, dq_ref, dg_ref, dmkv_ref):
        @pl.when(pl.program_id(0) == 0)
        def _():
            dmkv_ref[...] = jnp.zeros_like(dmkv_ref)

        for h in range(MEM_HEADS):
            c0 = h * MEM_HD
            qh = q_ref[:, c0:c0 + MEM_HD]
            mk = mkv_ref[:, c0:c0 + MEM_HD]
            mv = mkv_ref[:, D_MEM + c0:D_MEM + c0 + MEM_HD]
            gv = g_ref[:, c0:c0 + MEM_HD]
            sg = _sigmoid(gv)
            dyv = dy_ref[:, c0:c0 + MEM_HD]
            dg_ref[:, c0:c0 + MEM_HD] = (dyv * o_ref[:, c0:c0 + MEM_HD] * (sg * (1.0 + gv * (1.0 - sg)))).astype(BF)
            do = (dyv * (gv * sg)).astype(BF)
            pr = _mem_probs(qh, mk)
            dp = _dot(do, mv, NT)
            ds = pr * (dp - jnp.sum(pr * dp, axis=-1, keepdims=True))
            dsb = (ds * (MEM_HD ** -0.5)).astype(BF)
            dq_ref[:, c0:c0 + MEM_HD] = _dot(dsb, mk, NN).astype(BF)
            dmkv_ref[:, c0:c0 + MEM_HD] += _dot(dsb, qh, TN)
            dmkv_ref[:, D_MEM + c0:D_MEM + c0 + MEM_HD] += _dot(pr.astype(BF), do, TN)

    blk = pl.BlockSpec((tq, D_MEM), lambda i: (i, 0))
    whole = pl.BlockSpec((M, 2 * D_MEM), lambda i: (0, 0))
    return pl.pallas_call(
        body, name="mem_bwd", grid=(S // tq,),
        in_specs=[blk, whole, blk, blk, blk], out_specs=[blk, blk, whole],
        out_shape=[jax.ShapeDtypeStruct((S, D_MEM), BF), jax.ShapeDtypeStruct((S, D_MEM), BF),
                   jax.ShapeDtypeStruct((M, 2 * D_MEM), F32)],
        compiler_params=_params(("arbitrary",)),
    )(q, mkv, g, o, dy)


MERGE_TN = 512
MERGE_FWD_TM = 1024


def _merge_specs(tm):
    ytile = pl.BlockSpec((tm, 1024), lambda i, j: (i, 0))
    wblk = pl.BlockSpec((MERGE_TN, 1024), lambda i, j: (j, 0))
    gls = [pl.BlockSpec((None, tm, MERGE_TN), (lambda i, j, br=br: (br, i, j))) for br in range(3)]
    otile = pl.BlockSpec((tm, MERGE_TN), lambda i, j: (i, j))
    return ytile, wblk, gls, otile


def _merge_fwd(ys, ws, gl, tm):
    S = gl.shape[1]

    def body(y0, y1, y2, w0, w1, w2, g0, g1, g2, o_ref):
        acc = None
        for y_ref, w_ref, g_ref in ((y0, w0, g0), (y1, w1, g1), (y2, w2, g2)):
            term = _sigmoid(g_ref[...]) * _dot(y_ref[...], w_ref[...], NT)
            acc = term if acc is None else acc + term
        o_ref[...] = acc.astype(BF)

    ytile, wblk, gls, otile = _merge_specs(tm)
    return pl.pallas_call(
        body, name="merge_fwd", grid=(S // tm, D_MODEL // MERGE_TN),
        in_specs=[ytile] * 3 + [wblk] * 3 + gls, out_specs=otile,
        out_shape=jax.ShapeDtypeStruct((S, D_MODEL), BF),
        compiler_params=_params(("parallel", "arbitrary")),
    )(*ys, *ws, gl, gl, gl)


def _merge_bwd(dout, w_out, ys, ws, gl, tm):
    S = gl.shape[1]

    def body(do_ref, wo_ref, y0, y1, y2, w0, w1, w2, g0, g1, g2, dg0, dg1, dg2, dp0, dp1, dp2):
        dm = _dot(do_ref[...], wo_ref[...], NT)
        for y_ref, w_ref, g_ref, dg_ref, dp_ref in ((y0, w0, g0, dg0, dp0), (y1, w1, g1, dg1, dp1), (y2, w2, g2, dg2, dp2)):
            gate = _sigmoid(g_ref[...])
            pv = _dot(y_ref[...], w_ref[...], NT)
            dg_ref[...] = (dm * pv * gate * (1.0 - gate)).astype(BF)
            dp_ref[...] = (dm * gate).astype(BF)

    ytile, wblk, gls, otile = _merge_specs(tm)
    out = jax.ShapeDtypeStruct((S, D_MODEL), BF)
    return pl.pallas_call(
        body, name="merge_bwd", grid=(S // tm, D_MODEL // MERGE_TN),
        in_specs=[pl.BlockSpec((tm, D_MODEL), lambda i, j: (i, 0)), pl.BlockSpec((MERGE_TN, D_MODEL), lambda i, j: (j, 0))]
        + [ytile] * 3 + [wblk] * 3 + gls,
        out_specs=[otile] * 6, out_shape=[out] * 6,
        compiler_params=_params(("parallel", "arbitrary")),
    )(dout, w_out, *ys, *ws, gl, gl, gl)


def _out_loss(merged, w_out, x, target, post_g, tm):
    S = x.shape[0]

    def body(m_ref, w_ref, x_ref, t_ref, g_ref, dout_ref, dy_ref, loss_ref, dpost_ref):
        @pl.when(pl.program_id(0) == 0)
        def _():
            loss_ref[...] = jnp.zeros_like(loss_ref)
            dpost_ref[...] = jnp.zeros_like(dpost_ref)

        out = _dot(m_ref[...], w_ref[...], NN)
        r = lax.rsqrt(jnp.mean(out * out, axis=-1, keepdims=True) + EPS)
        nrm = out * r
        gv = g_ref[...]
        err = (x_ref[...] + nrm * gv) - t_ref[...]
        sq = jnp.sum(jnp.sum(err * err, axis=1, keepdims=True), axis=0, keepdims=True)
        loss_ref[...] += sq * (0.5 / D_MODEL)
        dy = err * (1.0 / D_MODEL)
        dy_ref[...] = dy
        dpost_ref[...] += jnp.sum(dy * nrm, axis=0, keepdims=True)
        dn = dy * gv
        dout_ref[...] = (r * (dn - nrm * jnp.mean(dn * nrm, axis=-1, keepdims=True))).astype(BF)

    row = pl.BlockSpec((tm, D_MODEL), lambda i: (i, 0))
    return pl.pallas_call(
        body, name="out_loss", grid=(S // tm,),
        in_specs=[row, pl.BlockSpec((D_MODEL, D_MODEL), lambda i: (0, 0)), row, row, pl.BlockSpec((1, D_MODEL), lambda i: (0, 0))],
        out_specs=[row, row, pl.BlockSpec((8, 128), lambda i: (0, 0)), pl.BlockSpec((1, D_MODEL), lambda i: (0, 0))],
        out_shape=[jax.ShapeDtypeStruct((S, D_MODEL), BF), jax.ShapeDtypeStruct((S, D_MODEL), F32),
                   jax.ShapeDtypeStruct((8, 128), F32), jax.ShapeDtypeStruct((1, D_MODEL), F32)],
        compiler_params=_params(("arbitrary",)),
    )(merged, w_out, x, target, post_g)


DH_DX_CHUNK = 64


def _dh_dx(dproj, w_in, x, dy, pre_g, tm, tk):
    S = x.shape[0]
    nk = D_IN // tk

    def body(dp_ref, w_ref, x_ref, dy_ref, g_ref, dx_ref, dpre_ref, acc_ref):
        i, k = pl.program_id(0), pl.program_id(1)

        @pl.when(jnp.logical_and(i == 0, k == 0))
        def _():
            dpre_ref[...] = jnp.zeros_like(dpre_ref)

        @pl.when(k == 0)
        def _():
            acc_ref[...] = jnp.zeros_like(acc_ref)

        acc_ref[...] += _dot(dp_ref[...], w_ref[...], NN)

        @pl.when(k == nk - 1)
        def _():
            def chunk(c, carry):
                rows = pl.ds(pl.multiple_of(c * DH_DX_CHUNK, DH_DX_CHUNK), DH_DX_CHUNK)
                dh = acc_ref[rows, :]
                xv = x_ref[rows, :]
                r = lax.rsqrt(jnp.mean(xv * xv, axis=-1, keepdims=True) + EPS)
                nrm = xv * r
                dpre_ref[...] += jnp.sum(dh * nrm, axis=0, keepdims=True)
                dn = dh * g_ref[...]
                dx_ref[rows, :] = r * (dn - nrm * jnp.mean(dn * nrm, axis=-1, keepdims=True)) + dy_ref[rows, :]
                return carry
            lax.fori_loop(0, tm // DH_DX_CHUNK, chunk, 0)

    row = pl.BlockSpec((tm, D_MODEL), lambda i, k: (i, 0))
    vec = pl.BlockSpec((1, D_MODEL), lambda i, k: (0, 0))
    return pl.pallas_call(
        body, name="dh_dx", grid=(S // tm, nk),
        in_specs=[pl.BlockSpec((tm, tk), lambda i, k: (i, k)), pl.BlockSpec((tk, D_MODEL), lambda i, k: (k, 0)), row, row, vec],
        out_specs=[row, vec],
        out_shape=[jax.ShapeDtypeStruct((S, D_MODEL), F32), jax.ShapeDtypeStruct((1, D_MODEL), F32)],
        scratch_shapes=[pltpu.VMEM((tm, D_MODEL), F32)],
        compiler_params=_params(("arbitrary", "arbitrary"), large=True),
    )(dproj, w_in, x, dy, pre_g)


def _sum_parts(parts, name):
    P, R, C = parts.shape
    tr = max(t for t in range(8, 513, 8) if R % t == 0)

    def body(p_ref, o_ref):
        acc = p_ref[0]
        for j in range(1, P):
            acc = acc + p_ref[j]
        o_ref[...] = acc

    return pl.pallas_call(
        body, name=name, grid=(R // tr,),
        in_specs=[pl.BlockSpec((P, tr, C), lambda i: (0, i, 0))], out_specs=pl.BlockSpec((tr, C), lambda i: (i, 0)),
        out_shape=jax.ShapeDtypeStruct((R, C), F32), compiler_params=_params(("parallel",)),
    )(parts)


def _adamw(land, sums, chip, w, m, v, name, group=(0, 1), into=None):
    q, n_groups = group
    _, R, cols = land.shape
    C = cols * n_groups
    tr = max(t for t in range(16, 257, 16) if R % t == 0)
    c1 = 1.0 - ADAM_B1 ** ADAM_STEP
    c2 = 1.0 - ADAM_B2 ** ADAM_STEP
    n_into = 0 if into is None else 4

    def body(chip_ref, p0_ref, p1_ref, p2_ref, own_ref, w_ref, m_ref, v_ref, *refs):
        g_ref, d_ref, nm_ref, nv_ref = refs[n_into:]
        g = own_ref[...].astype(F32)
        for p_ref in (p0_ref, p1_ref, p2_ref):
            g = g + p_ref[...].astype(F32)
        nm = ADAM_B1 * m_ref[...] + (1.0 - ADAM_B1) * g
        nv = ADAM_B2 * v_ref[...] + (1.0 - ADAM_B2) * (g * g)
        g_ref[...] = g
        nm_ref[...] = nm
        nv_ref[...] = nv
        d_ref[...] = -ADAM_LR * ((nm / c1) / (jnp.sqrt(nv / c2) + ADAM_EPS) + ADAM_WD * w_ref[...])

    tile = pl.BlockSpec((None, tr, cols), lambda i, c_ref: (0, i, q))
    specs = [pl.BlockSpec((None, tr, cols), (lambda i, c_ref, k=k: (k + (c_ref[0] <= k).astype(jnp.int32), i, 0))) for k in range(3)]
    specs.append(pl.BlockSpec((None, tr, cols), (lambda i, c_ref: (c_ref[0], i, 0))))
    return pl.pallas_call(
        body, name=name,
        grid_spec=pltpu.PrefetchScalarGridSpec(num_scalar_prefetch=1, grid=(R // tr,),
                                               in_specs=specs + [tile, tile, tile] + [pl.BlockSpec(memory_space=pl.ANY)] * n_into,
                                               out_specs=[tile] * 4),
        out_shape=[jax.ShapeDtypeStruct((1, R, C), F32)] * 4,
        input_output_aliases={8 + k: k for k in range(n_into)},
        compiler_params=_params(("parallel",)),
    )(chip, land, land, land, sums, w, m, v, *(into or []))


def _adamw_small(gs, ws, ms, vs):
    n = len(ws)
    c1 = 1.0 - ADAM_B1 ** ADAM_STEP
    c2 = 1.0 - ADAM_B2 ** ADAM_STEP

    def flat2(a):
        return a.reshape(-1, a.shape[-1])

    def body(*refs):
        ins, outs = refs[:4 * n], refs[4 * n:]
        for a in range(n):
            g, w, m, v = (ins[k * n + a][...] for k in range(4))
            nm = ADAM_B1 * m + (1.0 - ADAM_B1) * g
            nv = ADAM_B2 * v + (1.0 - ADAM_B2) * (g * g)
            outs[a][...] = g
            outs[n + a][...] = -ADAM_LR * ((nm / c1) / (jnp.sqrt(nv / c2) + ADAM_EPS) + ADAM_WD * w)
            outs[2 * n + a][...] = nm
            outs[3 * n + a][...] = nv

    shapes = [flat2(w).shape for w in ws]
    out = pl.pallas_call(
        body, name="adamw_small", out_shape=[jax.ShapeDtypeStruct(sh, F32) for sh in shapes] * 4,
        compiler_params=_params(),
    )(*[g.reshape(sh) for g, sh in zip(gs, shapes)], *[flat2(a) for a in (*ws, *ms, *vs)])
    return [[out[k * n + a].reshape(ws[a].shape) for a in range(n)] for k in range(4)]


PROJ_ROWS = 512


def _project(h, w_t, dep=None):
    S = h.shape[0]
    n_tiles = D_IN // SEG_TILE
    ranges = [(c0 // SEG_TILE, (c0 + width) // SEG_TILE) for _, c0, width, _ in SEGMENTS]
    dtypes = (F32, BF)
    n_extra = int(dep is not None)

    def of_dtype(j, dt):
        hit = False
        for (j0, j1), seg in zip(ranges, SEGMENTS):
            if seg[3] == dt:
                hit = jnp.logical_and(j >= j0, j < j1) | hit
        return hit

    def body(h_ref, w_ref, *refs):
        outs = refs[n_extra:n_extra + len(SEGMENTS)]
        stages, sems = refs[n_extra + len(SEGMENTS):-1], refs[-1]
        j = pl.program_id(0)
        slot = j % 2

        def copy_out(k, stage, dst):
            return pltpu.make_async_copy(stages[k].at[stage], dst, sems.at[stage])

        def wait_tile(jj, stage):
            for k, dt in enumerate(dtypes):
                o_ref = [o for o, seg in zip(outs, SEGMENTS) if seg[3] == dt and seg[0] != "gl"][0]
                @pl.when(of_dtype(jj, dt))
                def _(k=k, o_ref=o_ref):
                    copy_out(k, stage, o_ref.at[:, pl.ds(0, SEG_TILE)]).wait()

        @pl.when(j >= 2)
        def _():
            wait_tile(j - 2, slot)

        for k, dt in enumerate(dtypes):
            @pl.when(of_dtype(j, dt))
            def _(k=k, dt=dt):
                for c in range(S // PROJ_ROWS):
                    rows = pl.ds(c * PROJ_ROWS, PROJ_ROWS)
                    stages[k][slot, rows, :] = _dot(h_ref[rows, :], w_ref[...], NT).astype(dt)

        for (j0, j1), (name, _, _, dt), o_ref in zip(ranges, SEGMENTS, outs):
            @pl.when(jnp.logical_and(j >= j0, j < j1))
            def _(j0=j0, j1=j1, name=name, dt=dt, o_ref=o_ref):
                t = j - j0
                if name == "gl":
                    per = (j1 - j0) // 3
                    dst = o_ref.at[t // per, :, pl.ds(pl.multiple_of((t % per) * SEG_TILE, SEG_TILE), SEG_TILE)]
                else:
                    dst = o_ref.at[:, pl.ds(pl.multiple_of(t * SEG_TILE, SEG_TILE), SEG_TILE)]
                copy_out(dtypes.index(dt), slot, dst).start()

        @pl.when(j == n_tiles - 1)
        def _():
            wait_tile(j - 1, 1 - slot)
            wait_tile(j, slot)

    out_shapes = [jax.ShapeDtypeStruct((3, S, width // 3) if name == "gl" else (S, width), dt) for name, _, width, dt in SEGMENTS]
    outs = pl.pallas_call(
        body, name="proj", grid=(n_tiles,),
        in_specs=[pl.BlockSpec((S, D_MODEL), lambda j: (0, 0)), pl.BlockSpec((SEG_TILE, D_MODEL), lambda j: (j, 0))]
        + ([] if dep is None else [pl.BlockSpec((8, 128), lambda j: (0, 0))]),
        out_specs=[pl.BlockSpec(memory_space=pl.ANY)] * len(SEGMENTS), out_shape=out_shapes,
        scratch_shapes=[pltpu.VMEM((2, S, SEG_TILE), dt) for dt in dtypes] + [pltpu.SemaphoreType.DMA((2,))],
        compiler_params=_params(("arbitrary",), large=True),
    )(h, w_t, *([] if dep is None else [dep]))
    return {name: o for (name, _, _, _), o in zip(SEGMENTS, outs)}


def _forward_a(h, memn, w_in, sinks, rel_bias, dep=None):
    S = h.shape[0]
    st = dict(T=min(512, S // 2), tm=min(512, S), bucket=_rel_bucket_map(), h=h, memn=memn)
    seg = st["seg"] = _project(h, w_in, dep)
    st["o_swa"], st["y_swa"] = _swa_fwd(seg["q_s"], seg["kv"], seg["g_swa"], st["bucket"], rel_bias, _sink_column(sinks))
    return st


def _forward_rg(st, conv_w, conv_b, w_a, b_a, w_x, b_x, lam):
    seg = st["seg"]
    st["h_rg"], st["y_rg"] = _rglru_fwd(seg["xr"], seg["g_rg"], conv_w, conv_b, w_a, b_a, w_x, b_x, lam, st["T"])
    return st


def _forward_b(st, x, target, post_g, w_memkv, wbr, w_out):
    S = x.shape[0]
    M = st["memn"].shape[0]
    seg = st["seg"]
    st["mkv"] = _matmul(st["memn"], w_memkv, "nn", M, 2 * D_MEM, D_MODEL, M, 512, D_MODEL, BF, "mem_kv")
    st["o_mem"], st["y_mem"] = _mem_fwd(seg["q_m"], st["mkv"], seg["g_mem"])
    st["ys"] = (st["y_rg"], st["y_swa"], st["y_mem"])
    st["merged"] = _merge_fwd(st["ys"], wbr, seg["gl"], min(S, MERGE_FWD_TM))
    st["dout"], st["dy"], st["loss"], st["dpost"] = _out_loss(st["merged"], w_out, x, target, post_g, min(256, S))
    return st


def _backward_a1(st, wbr, w_out):
    S = st["h"].shape[0]
    seg, ys, tm = st["seg"], st["ys"], st["tm"]
    st["dw_out"] = _matmul(st["merged"], st["dout"], "tn", D_MODEL, D_MODEL, S, 256, D_MODEL, S, BF, "dw_out", out_blocked="row")
    dgl0, dgl1, dgl2, dp0, dp1, dp2 = _merge_bwd(st["dout"], w_out, ys, wbr, seg["gl"], tm)
    st["dgl"] = (dgl0, dgl1, dgl2)
    dys, dwbr = [], []
    for i, dp in enumerate((dp0, dp1, dp2)):
        dys.append(_matmul(dp, wbr[i], "nn", S, 1024, D_MODEL, tm, 1024, D_MODEL, F32, "dy_br%d" % i))
        dwbr.append(_matmul(ys[i], dp, "tn", 1024, D_MODEL, S, 1024, 256, S, BF, "dw_br%d" % i, out_blocked="col"))
    st["dys"], st["dwbr"] = dys, dwbr
    return st


def _backward_a2(st, mem, w_memkv, conv_w, conv_b, w_a, b_a, w_x, b_x, lam):
    M = mem.shape[0]
    seg, dys = st["seg"], st["dys"]
    st["dq_m"], st["dg_mem"], dmkv = _mem_bwd(seg["q_m"], st["mkv"], seg["g_mem"], st["o_mem"], dys[2])
    st["dmkv"] = dmkv.astype(BF)
    st["dw_memkv"] = _matmul(st["memn"], st["dmkv"], "tn", D_MODEL, 2 * D_MEM, M, 256, 2 * D_MEM, M, BF, "dw_memkv", out_blocked="row")
    st["dxr"], st["dg_rg"], st["dw_a"], st["dw_x"], st["dvec"] = _rglru_bwd(
        seg["xr"], seg["g_rg"], st["h_rg"], dys[0], conv_w, conv_b, w_a, b_a, w_x, b_x, lam, st["T"])
    return st


def _mem_gain_grad(st, mem, w_memkv, dep=None):
    M = mem.shape[0]
    dmemn = _matmul(st["dmkv"], w_memkv, "nt", M, D_MODEL, 2 * D_MEM, M, 512, 2 * D_MEM, F32, "dmemn", dep=dep)
    return _rms_gain_grad(dmemn, mem, "dmem_gain")


def _backward_b(st, rel_bias, sinks):
    seg = st["seg"]
    others = {"xr": st["dxr"], "g_rg": st["dg_rg"], "q_m": st["dq_m"], "g_mem": st["dg_mem"], "gl": st["dgl"]}
    dproj, dkv, dsinks, drel = _swa_bwd(seg["q_s"], seg["kv"], seg["g_swa"], st["o_swa"], st["dys"][1],
                                        st["bucket"], rel_bias, _sink_column(sinks), others)
    st["dsinks"], st["drel"] = dsinks.reshape(1, SWA_HEADS), drel.reshape(REL_BUCKETS, SWA_HEADS)
    col_kv = [c0 for name, c0, _, _ in SEGMENTS if name == "kv"][0]
    st["dproj"] = lax.dynamic_update_slice(dproj, dkv.astype(BF), (0, col_kv))
    return st


def _dw_in_half(st, half, dep=None):
    S = st["h"].shape[0]
    dw = _matmul(st["dproj"], st["h"], "tn", D_IN, D_MODEL // 2, S, D_IN_TILE, D_MODEL // 2, S, BF, "dw_in%d" % half, b_noff=half, dep=dep, out_hbm=True)
    return dw.reshape(N_DEV, D_IN // N_DEV, D_MODEL // 2)


def _owner_blocks(a):
    return jnp.swapaxes(a.reshape((4, 2) + a.shape[1:]), 0, 1)


def _pad_rows(a, rows):
    a = a.reshape(-1, 128) if a.shape[-1] % 128 == 0 else jnp.pad(a, ((0, 0), (0, 128 - a.shape[-1])))
    return jnp.pad(a, ((0, rows - a.shape[0]), (0, 0))) if a.shape[0] < rows else a


def kernel(x, mem, pre_norm_g, post_norm_g, mem_norm_g, w_in, conv_w, conv_b, w_rg_a, b_rg_a, w_rg_x, b_rg_x, lru_lambda, swa_sinks, rel_bias, w_mem_kv, w_br_rg, w_br_swa, w_br_mem, w_out, loss_target, m_pre_norm_g, m_post_norm_g, m_mem_norm_g, m_w_in, m_conv_w, m_conv_b, m_w_rg_a, m_b_rg_a, m_w_rg_x, m_b_rg_x, m_lru_lambda, m_swa_sinks, m_rel_bias, m_w_mem_kv, m_w_br_rg, m_w_br_swa, m_w_br_mem, m_w_out, v_pre_norm_g, v_post_norm_g, v_mem_norm_g, v_w_in, v_conv_w, v_conv_b, v_w_rg_a, v_b_rg_a, v_w_rg_x, v_b_rg_x, v_lru_lambda, v_swa_sinks, v_rel_bias, v_w_mem_kv, v_w_br_rg, v_w_br_swa, v_w_br_mem, v_w_out):
    cx, cy, cc = lax.axis_index("x"), lax.axis_index("y"), lax.axis_index("c")
    me = 4 * cx + 2 * cy + cc
    chip = 2 * cx + cy
    core = jnp.reshape(cc, (1,)).astype(jnp.int32)
    x0, mem0 = x[0], mem[0]
    w_a_b, w_x_b = w_rg_a[0].astype(BF), w_rg_x[0].astype(BF)

    def landing(own, slot, slots):
        return lax.dynamic_update_slice(lax.empty((slots,) + own.shape, own.dtype), own[None], (slot,) + (0,) * own.ndim)


    def swap_start(parts, tag):
        return _exchange_start(parts, [lax.empty((4,) + p.shape[-2:], p.dtype) for p in parts], _plan_swap([p.ndim for p in parts]),
                               "swap_%s_start" % tag)

    def scatter_start(swap, after, tag, prefill=()):
        s_send, s_recv, parts, got, _ = swap
        got = _exchange_wait(s_send, s_recv, parts, got, _plan_swap([p.ndim for p in parts]), after, "swap_%s_wait" % tag)
        sums = [_pair_sum(p, g, core, "scatter_%s_sum%d" % (tag, i)) for i, (p, g) in enumerate(zip(parts, got))]
        lands = [landing(lax.dynamic_index_in_dim(s, chip, 0, keepdims=False), chip, 4) if i in prefill
                 else lax.empty(s.shape, s.dtype) for i, s in enumerate(sums)]
        return _exchange_start(sums, lands, _plan_scatter(len(sums)), "scatter_%s_start" % tag)

    def corner(a):
        return a.reshape(-1, a.shape[-1])[:8, :128]

    def zero_after(a):
        return jnp.minimum(jnp.abs(a.reshape(-1)[0].astype(F32)), 0.0)

    g_in, g_cw, h0, memn0 = _all_gather_relayed([jnp.transpose(w_in[0]).astype(BF), conv_w[0]], [True, False], "gather_w_in",
                                                side=_rms_side([(x0, pre_norm_g), (mem0, mem_norm_g)]))
    w_in_f = g_in.reshape(D_IN, D_MODEL)
    conv_w_f = jnp.transpose(g_cw, (1, 0, 2)).reshape(CONV_W, D_RNN)

    after_first = zero_after(g_cw).astype(BF)
    rest = [w.astype(BF) + after_first for w in (w_mem_kv[0], jnp.transpose(w_br_rg[0]), jnp.transpose(w_br_swa[0]),
                                                 jnp.transpose(w_br_mem[0]), w_out[0])]
    plan_g = _plan_gather(len(rest))
    zones = _place_own([lax.empty((N_DEV,) + w.shape, w.dtype) for w in rest], rest, jnp.reshape(me, (1,)).astype(jnp.int32), "gather_rest_own")
    g_send, g_recv, g_src, g_land, g_token = _exchange_start(rest, zones, plan_g, "gather_rest_start")
    st = _forward_a(h0, memn0, w_in_f, swa_sinks, rel_bias, dep=g_token)
    g_land = _exchange_wait(g_send, g_recv, g_src, g_land, plan_g, st["y_swa"], "gather_rest_wait")
    plan_f = _plan_forward(len(rest))
    f_send, f_recv, _, g_land, f_token = _exchange_start(None, g_land, plan_f, "forward_rest_start")
    st = _forward_rg(st, conv_w_f, conv_b + f_token[0:1, 0:1], w_a_b, b_rg_a, w_x_b, b_rg_x, lru_lambda)
    g_land = _exchange_wait(f_send, f_recv, None, g_land, plan_f, corner(st["y_rg"]), "forward_rest_wait")
    w_memkv_f = g_land[0].reshape(D_MODEL, 2 * D_MEM)
    wbr = tuple(g_land[i].reshape(D_MODEL, D_RNN) for i in (1, 2, 3))
    w_out_f = g_land[4].reshape(D_MODEL, D_MODEL)

    st = _forward_b(st, x0, loss_target[0], post_norm_g, w_memkv_f, wbr, w_out_f)
    st = _backward_a1(st, wbr, w_out_f)
    parts_a = [st["dw_out"], st["dwbr"][0], st["dwbr"][1], st["dwbr"][2]]
    plan_a = _plan_scatter(len(parts_a))
    swap_a = swap_start(parts_a, "a")
    st = _backward_a2(st, mem0, w_memkv_f, conv_w_f, conv_b + swap_a[4][0:1, 0:1], w_a_b, b_rg_a, w_x_b, b_rg_x, lru_lambda)
    a_send, a_recv, a_src, a_land, a_token = scatter_start(swap_a, st["dxr"], "a")
    parts_c = [st["dw_memkv"], _owner_blocks(st["dw_a"]), _owner_blocks(st["dw_x"])]
    plan_c = _plan_scatter(len(parts_c))
    swap_c = swap_start(parts_c, "c")

    st = _backward_b(st, rel_bias, swa_sinks + swap_c[4][0:1, 0:1] + a_token[0:1, 0:1])
    c_send, c_recv, c_src, c_land, c_token = scatter_start(swap_c, st["dsinks"], "c", prefill=(1, 2))
    plan_b = _plan_scatter(1)

    def dw_in_parts(half, dep):
        dwh = _dw_in_half(st, half, dep)
        return dwh, [dwh]

    dw0, parts_b0 = dw_in_parts(0, c_token)
    swap_b0 = swap_start(parts_b0, "b0")
    a_land = _exchange_wait(a_send, a_recv, a_src, a_land, plan_a, swap_b0[4], "scatter_a_wait")
    big = [None] * 6

    chip1 = jnp.reshape(chip, (1,)).astype(jnp.int32)

    def adamw_big(j, land, own, wt, mt, vt):
        big[j] = _adamw(land, own, chip1, wt, mt, vt, "adamw_big%d" % j)

    adamw_big(5, a_land[0], a_src[0], w_out, m_w_out, v_w_out)
    adamw_big(2, a_land[1], a_src[1], w_br_rg, m_w_br_rg, v_w_br_rg)
    adamw_big(3, a_land[2], a_src[2], w_br_swa, m_w_br_swa, v_w_br_swa)
    halves = [scatter_start(swap_b0, corner(big[5][1]) + corner(big[2][1]) + corner(big[3][1]), "b0")]
    dw1, parts_b1 = dw_in_parts(1, halves[0][4])
    swap_b1 = swap_start(parts_b1, "b1")
    c_land = _exchange_wait(c_send, c_recv, c_src, c_land, plan_c, swap_b1[4], "scatter_c_wait")
    g_wa_blk = _sum_parts(c_land[1], "sum_w_rg_a")
    g_wx_blk = _sum_parts(c_land[2], "sum_w_rg_x")
    adamw_big(4, a_land[3], a_src[3], w_br_mem, m_w_br_mem, v_w_br_mem)
    adamw_big(1, c_land[0], c_src[0], w_mem_kv, m_w_mem_kv, v_w_mem_kv)
    halves.append(scatter_start(swap_b1, corner(big[4][1]) + corner(big[1][1]), "b1"))
    st["dmem_g"] = _mem_gain_grad(st, mem0, w_memkv_f, halves[1][4])
    grad_x, dpre = _dh_dx(st["dproj"], w_in_f, x0, st["dy"], pre_norm_g + halves[1][4][0:1, 0:1], st["tm"], D_IN_TILE)
    pack = jnp.concatenate([dpre.reshape(16, 128), st["dpost"].reshape(16, 128), st["dmem_g"].reshape(16, 128),
                            st["dvec"].reshape(64, 128), _pad_rows(st["dsinks"], 8), _pad_rows(st["drel"], 32), g_wa_blk, g_wx_blk,
                            st["loss"]], axis=0)
    plan_s = _plan_everyone(1)
    s_send, s_recv, s_src, s_land, s_token = _exchange_start([pack], [landing(pack, me, N_DEV)], plan_s, "gather_small_start")
    swap_last = lambda a: jnp.transpose(a, (0, 2, 1))
    after, big0_t = s_token, None
    for half, (b_send, b_recv, b_src, b_land, _) in enumerate(halves):
        b_land = _exchange_wait(b_send, b_recv, b_src, b_land, plan_b, after, "scatter_b%d_wait" % half)[0]
        big0_t = _adamw(b_land, b_src[0], chip1, swap_last(w_in), swap_last(m_w_in), swap_last(v_w_in), "adamw_big0_%d" % half,
                        group=(half, 2), into=big0_t)
        after = corner(big0_t[1])
    big[0] = [swap_last(a) for a in big0_t]
    gathered = _exchange_wait(s_send, s_recv, s_src, s_land, plan_s, corner(big0_t[1]), "gather_small_wait")[0]
    gs = _sum_parts(gathered, "sum_small")
    loss_total = gs[408, 0]
    g_pre, g_post, g_memg = gs[0:16].reshape(1, D_MODEL), gs[16:32].reshape(1, D_MODEL), gs[32:48].reshape(1, D_MODEL)
    gvec = gs[48:112].reshape(8, D_RNN)
    g_conv_w = lax.dynamic_slice(gvec[0:CONV_W], (0, me * RNN_BLOCK), (CONV_W, RNN_BLOCK))
    g_conv_b, g_b_a, g_b_x, g_lam = gvec[4:5], gvec[5:6], gvec[6:7], gvec[7:8]
    g_sinks = gs[112:113, :SWA_HEADS]
    g_rel = gs[120:152, :SWA_HEADS]
    g_w_a = gathered[:, 152:280]
    g_w_x = gathered[:, 280:408]

    g_small = (g_pre, g_post, g_memg, g_conv_b, g_b_a, g_b_x, g_lam, g_w_a, g_w_x, g_sinks, g_rel, g_conv_w)
    w_small = (pre_norm_g, post_norm_g, mem_norm_g, conv_b, b_rg_a, b_rg_x, lru_lambda, w_rg_a, w_rg_x, swa_sinks, rel_bias, conv_w)
    m_small = (m_pre_norm_g, m_post_norm_g, m_mem_norm_g, m_conv_b, m_b_rg_a, m_b_rg_x, m_lru_lambda, m_w_rg_a, m_w_rg_x, m_swa_sinks, m_rel_bias, m_conv_w)
    v_small = (v_pre_norm_g, v_post_norm_g, v_mem_norm_g, v_conv_b, v_b_rg_a, v_b_rg_x, v_lru_lambda, v_w_rg_a, v_w_rg_x, v_swa_sinks, v_rel_bias, v_conv_w)
    sm = _adamw_small(g_small, w_small, m_small, v_small)


    def leaves(k):
        s = sm[k]
        return [s[0], s[1], s[2], big[0][k], s[11], s[3], s[7], s[4], s[8], s[5], s[6], s[9], s[10],
                big[1][k], big[2][k], big[3][k], big[4][k], big[5][k]]

    return (loss_total, grad_x[None], *leaves(0), *leaves(1), *leaves(2), *leaves(3))
```

```python
import math

import jax
import jax.numpy as jnp
import numpy as np
from jax import lax
from jax.experimental import pallas as pl
from jax.experimental.pallas import tpu as pltpu

F32, BF = jnp.float32, jnp.bfloat16
MESH = pl.DeviceIdType.MESH
N_DEV = 8

D_MODEL = 2048
D_RNN = 1024
RNN_BLOCKS = 8
RNN_BLOCK = 128
CONV_W = 4
LRU_C = 8.0
SWA_HEADS = 16
SWA_KV_HEADS = 2
SWA_HD = 64
WINDOW = 128
MEM_HEADS = 4
MEM_HD = 256
D_MEM = 1024
REL_BUCKETS = 32
REL_MAX_DIST = 128
EPS = 1e-6
NEG_INF = -1e30
D_IN = 12544
SEGMENTS = (("xr", 0, 1024, F32), ("g_rg", 1024, 1024, F32), ("q_s", 2048, 1024, BF), ("kv", 3072, 256, BF),
            ("g_swa", 3328, 1024, F32), ("q_m", 4352, 1024, BF), ("g_mem", 5376, 1024, F32), ("gl", 6400, 6144, F32))
SEG_TILE = 256
D_IN_TILE = 7 * SEG_TILE

ADAM_LR, ADAM_B1, ADAM_B2, ADAM_EPS, ADAM_WD, ADAM_STEP = 0.001, 0.9, 0.999, 1e-08, 0.01, 10

NN = (((1,), (0,)), ((), ()))
NT = (((1,), (1,)), ((), ()))
TN = (((0,), (0,)), ((), ()))
MIB = 2 ** 20


def _dot(a, b, dn):
    return lax.dot_general(a, b, dn, preferred_element_type=F32)


VMEM_LIMIT_MIB = 48
VMEM_LIMIT_LARGE_MIB = 56


def _params(sem=None, large=False):
    return pltpu.CompilerParams(dimension_semantics=sem, vmem_limit_bytes=(VMEM_LIMIT_LARGE_MIB if large else VMEM_LIMIT_MIB) * MIB)


def _sigmoid(z):
    return 1.0 / (1.0 + jnp.exp(-z))


def _softplus(z):
    return jnp.maximum(z, 0.0) + jnp.log(1.0 + jnp.exp(-jnp.abs(z)))


def _expm1(z):
    p = z * (1.0 + z * (0.5 + z * (1.0 / 6 + z * (1.0 / 24 + z * (1.0 / 120 + z * (1.0 / 720 + z * (1.0 / 5040 + z / 40320)))))))
    return jnp.where(jnp.abs(z) < 0.3, p, jnp.exp(z) - 1.0)


def _flat(p):
    return 4 * p[0] + 2 * p[1] + p[2]


def _all_gather_relayed(arrs, relay, name, side=None):
    n = len(arrs)
    K = 9
    work, side_ins, side_outs, side_scratch = side if side is not None else (None, [], [], [])
    n_in, n_out = n + len(side_ins), n + len(side_outs)

    def body(*refs):
        ins, outs = refs[:n], refs[n_in:n_in + n]
        send_sems, recv_sems, local_sems = refs[n_in + n_out:n_in + n_out + 3]
        x, y, c = lax.axis_index("x"), lax.axis_index("y"), lax.axis_index("c")
        me, sib = (x, y, c), (x, y, 1 - c)
        xn, yn, dg = (1 - x, y, c), (x, 1 - y, c), (1 - x, 1 - y, c)

        def other(p):
            return (p[0], p[1], 1 - p[2])

        def rows(a, half):
            h = arrs[a].shape[0] // 2
            return pl.ds(half * h, h)

        def copy(a, k, block, to, half=None, src=None):
            dst = outs[a].at[_flat(block)]
            if half is not None:
                dst = dst.at[rows(a, half)]
            return pltpu.make_async_remote_copy(src_ref=dst if src is None else src, dst_ref=dst,
                                                send_sem=send_sems.at[a * K + k], recv_sem=recv_sems.at[a * K + k],
                                                device_id=to, device_id_type=MESH)

        mine = [pltpu.make_async_copy(ins[a], outs[a].at[_flat(me)], local_sems.at[a]) for a in range(n)]
        for cp in mine:
            cp.start()
        sends = []

        def start(cp):
            cp.start()
            sends.append(cp)

        for a in range(n):
            start(copy(a, 1, me, xn, src=ins[a]))
            start(copy(a, 2, me, yn, src=ins[a]))
            if not relay[a]:
                start(copy(a, 3, me, dg, src=ins[a]))
            start(copy(a, 0, me, sib, src=ins[a]))
        if work is not None:
            work(refs[n:n_in], refs[n_in + n:n_in + n_out], refs[n_in + n_out + 3:])
        for a in range(n):
            copy(a, 1, xn, me).wait_recv()
            if relay[a]:
                start(copy(a, 3, xn, yn, half=0))
            start(copy(a, 5, xn, sib))
        for a in range(n):
            copy(a, 2, yn, me).wait_recv()
            if relay[a]:
                start(copy(a, 4, yn, xn, half=1))
            start(copy(a, 6, yn, sib))
        for a in range(n):
            if relay[a]:
                copy(a, 3, dg, me, half=0).wait_recv()
                start(copy(a, 7, dg, sib, half=0))
                copy(a, 4, dg, me, half=1).wait_recv()
                start(copy(a, 8, dg, sib, half=1))
            else:
                copy(a, 3, dg, me).wait_recv()
                start(copy(a, 7, dg, sib))
        for a in range(n):
            copy(a, 0, sib, me).wait_recv()
            copy(a, 5, other(xn), me).wait_recv()
            copy(a, 6, other(yn), me).wait_recv()
            if relay[a]:
                copy(a, 7, other(dg), me, half=0).wait_recv()
                copy(a, 8, other(dg), me, half=1).wait_recv()
            else:
                copy(a, 7, other(dg), me).wait_recv()
        for cp in sends:
            cp.wait_send()
        for cp in mine:
            cp.wait()

    any_spec = pl.BlockSpec(memory_space=pl.ANY)
    return pl.pallas_call(
        body, name=name,
        out_shape=[jax.ShapeDtypeStruct((N_DEV,) + a.shape, a.dtype) for a in arrs] + list(side_outs),
        in_specs=[any_spec] * n_in, out_specs=[any_spec] * n_out,
        scratch_shapes=[pltpu.SemaphoreType.DMA((K * n,)), pltpu.SemaphoreType.DMA((K * n,)), pltpu.SemaphoreType.DMA((n,))]
        + list(side_scratch),
        compiler_params=_params(),
    )(*arrs, *side_ins)


def _chip_peers(x, y):
    return [(1 - x, y), (x, 1 - y), (1 - x, 1 - y)]


def _chip(p):
    return 2 * p[0] + p[1]


def _plan_gather(n):
    def plan(x, y, c):
        out = []
        for a in range(n):
            for peer in [(x, y, 1 - c)] + [(*ch, c) for ch in _chip_peers(x, y)]:
                out.append((a, None, ("lead", _flat((x, y, c))), peer, ("lead", _flat(peer))))
        return out
    return plan


def _plan_everyone(n):
    def plan(x, y, c):
        out = []
        for a in range(n):
            for r in range(1, N_DEV):
                peer = (1 - x if r & 4 else x, 1 - y if r & 2 else y, 1 - c if r & 1 else c)
                out.append((a, None, ("lead", _flat((x, y, c))), peer, ("lead", _flat(peer))))
        return out
    return plan


def _plan_swap(ndims):
    def plan(x, y, c):
        out = []
        for a, nd in enumerate(ndims):
            if nd == 4:
                out.append((a, 1 - c, ("all", 0), (x, y, 1 - c), ("all", 0)))
            else:
                out += [(a, 2 * j + 1 - c, ("lead", j), (x, y, 1 - c), ("lead", j)) for j in range(4)]
        return out
    return plan


def _slot(ref, where):
    kind, k = where
    return ref if kind == "all" else ref.at[k]


def _plan_scatter(n):
    def plan(x, y, c):
        out = []
        for a in range(n):
            for ch in _chip_peers(x, y):
                out.append((a, _chip(ch), ("lead", _chip((x, y))), (*ch, c), ("lead", _chip(ch))))
        return out
    return plan


HBM_SPEC = pl.BlockSpec(memory_space=pltpu.HBM)
SEM_SPEC = pl.BlockSpec(memory_space=pltpu.SEMAPHORE)


def _in_hbm(a):
    return pltpu.with_memory_space_constraint(a, pltpu.HBM)


def _exchange_start(srcs, lands, plan, name):
    n = len(lands)
    ns = 0 if srcs is None else n
    count = len(plan(0, 0, 0))

    def body(*refs):
        land_refs = refs[ns:ns + n]
        src_refs = land_refs if srcs is None else refs[:n]
        send_sems, recv_sems = refs[ns + n], refs[ns + n + 1]
        token = refs[-1]
        x, y, c = lax.axis_index("x"), lax.axis_index("y"), lax.axis_index("c")
        for k, (a, si, di, peer, _) in enumerate(plan(x, y, c)):
            src = src_refs[a] if si is None else src_refs[a].at[si]
            pltpu.make_async_remote_copy(src_ref=src, dst_ref=_slot(land_refs[a], di), send_sem=send_sems.at[k],
                                         recv_sem=recv_sems.at[k], device_id=peer, device_id_type=MESH).start()
        token[...] = jnp.zeros_like(token)

    out = pl.pallas_call(
        body, name=name,
        out_shape=(pltpu.SemaphoreType.DMA((count,)), pltpu.SemaphoreType.DMA((count,)),
                   *[pltpu.HBM(a.shape, a.dtype) for a in lands], jax.ShapeDtypeStruct((8, 128), F32)),
        in_specs=[HBM_SPEC] * (ns + n),
        out_specs=(SEM_SPEC, SEM_SPEC, *([HBM_SPEC] * n), pl.BlockSpec(memory_space=pltpu.VMEM)),
        input_output_aliases={ns + i: 2 + i for i in range(n)},
        compiler_params=pltpu.CompilerParams(has_side_effects=pltpu.SideEffectType.DATAFLOW_SIDE_EFFECTING),
    )(*[_in_hbm(a) for a in (srcs or [])], *[_in_hbm(a) for a in lands])
    return out[0], out[1], srcs if srcs is None else list(srcs), list(out[2:2 + n]), out[-1]


def _exchange_wait(send_sems, recv_sems, srcs, lands, plan, after, name):
    n = len(lands)
    ns = 0 if srcs is None else n

    def body(*refs):
        land_refs = refs[ns:ns + n]
        src_refs = land_refs if srcs is None else refs[:n]
        send_sems, recv_sems = refs[ns + n], refs[ns + n + 1]
        x, y, c = lax.axis_index("x"), lax.axis_index("y"), lax.axis_index("c")
        for k, (a, si, _, peer, ri) in enumerate(plan(x, y, c)):
            src = src_refs[a] if si is None else src_refs[a].at[si]
            cp = pltpu.make_async_remote_copy(src_ref=src, dst_ref=_slot(land_refs[a], ri), send_sem=send_sems.at[k],
                                              recv_sem=recv_sems.at[k], device_id=peer, device_id_type=MESH)
            cp.wait_send()
            cp.wait_recv()

    out = pl.pallas_call(
        body, name=name,
        out_shape=tuple(pltpu.HBM(a.shape, a.dtype) for a in lands),
        in_specs=[HBM_SPEC] * (ns + n) + [SEM_SPEC, SEM_SPEC, pl.BlockSpec(memory_space=pl.ANY)],
        out_specs=tuple([HBM_SPEC] * n),
        input_output_aliases={ns + i: i for i in range(n)},
        compiler_params=pltpu.CompilerParams(has_side_effects=pltpu.SideEffectType.DATAFLOW_SIDE_EFFECTING),
    )(*[_in_hbm(a) for a in (srcs or [])], *lands, send_sems, recv_sems, after)
    return list(out)


def _plan_forward(n):
    def plan(x, y, c):
        return [(a, _flat((*ch, c)), ("lead", _flat((*ch, c))), (x, y, 1 - c), ("lead", _flat((*ch, 1 - c))))
                for a in range(n) for ch in _chip_peers(x, y)]
    return plan


def _place_own(zones, owns, slot, name):
    n = len(zones)

    def body(slot_ref, *refs):
        for a in range(n):
            refs[2 * n + a][...] = refs[a][...]

    return pl.pallas_call(
        body, name=name,
        grid_spec=pltpu.PrefetchScalarGridSpec(
            num_scalar_prefetch=1, grid=(1,),
            in_specs=[pl.BlockSpec(o.shape, lambda i, s_ref: (0, 0)) for o in owns] + [pl.BlockSpec(memory_space=pl.ANY)] * n,
            out_specs=[pl.BlockSpec((None,) + o.shape, lambda i, s_ref: (s_ref[0], 0, 0)) for o in owns]),
        out_shape=[jax.ShapeDtypeStruct(z.shape, z.dtype) for z in zones],
        input_output_aliases={1 + n + a: a for a in range(n)},
        compiler_params=_params(("arbitrary",)),
    )(slot, *owns, *zones)


def _pair_sum(parts, got, core, name):
    R, C = parts.shape[-2:]
    mine = (pl.BlockSpec((None, None, R, C), lambda j, c_ref: (c_ref[0], j, 0, 0)) if parts.ndim == 4
            else pl.BlockSpec((None, R, C), lambda j, c_ref: (2 * j + c_ref[0], 0, 0)))

    def body(c_ref, p_ref, g_ref, o_ref):
        o_ref[...] = (p_ref[...].astype(F32) + g_ref[...].astype(F32)).astype(o_ref.dtype)

    return pl.pallas_call(
        body, name=name,
        grid_spec=pltpu.PrefetchScalarGridSpec(
            num_scalar_prefetch=1, grid=(4,),
            in_specs=[mine, pl.BlockSpec((None, R, C), lambda j, c_ref: (j, 0, 0))],
            out_specs=pl.BlockSpec((None, R, C), lambda j, c_ref: (j, 0, 0))),
        out_shape=pltpu.HBM((4, R, C), parts.dtype),
        compiler_params=_params(("parallel",)),
    )(core, parts, got)


def _matmul(a, b, mode, M, N, K, tm, tn, tk, out_dtype, name, b_noff=0, out_blocked=None, dep=None, out_hbm=False):
    nm, nn, nk = M // tm, N // tn, K // tk
    if mode == "nn":
        a_spec = pl.BlockSpec((tm, tk), lambda j, i, k: (i, k))
        b_spec = pl.BlockSpec((tk, tn), lambda j, i, k: (k, j + b_noff))
        dn = NN
    elif mode == "nt":
        a_spec = pl.BlockSpec((tm, tk), lambda j, i, k: (i, k))
        b_spec = pl.BlockSpec((tn, tk), lambda j, i, k: (j + b_noff, k))
        dn = NT
    else:
        a_spec = pl.BlockSpec((tk, tm), lambda j, i, k: (k, i))
        b_spec = pl.BlockSpec((tk, tn), lambda j, i, k: (k, j + b_noff))
        dn = TN
    if out_blocked == "col":
        out_shape = jax.ShapeDtypeStruct((2, 4, M, tn), out_dtype)
        out_spec = pl.BlockSpec((None, None, tm, tn), lambda j, i, k: (j % 2, j // 2, i, 0))
    elif out_blocked == "row":
        out_shape = jax.ShapeDtypeStruct((2, 4, tm, N), out_dtype)
        out_spec = pl.BlockSpec((None, None, tm, tn), lambda j, i, k: (i % 2, i // 2, 0, j))
    else:
        out_shape = jax.ShapeDtypeStruct((M, N), out_dtype)
        out_spec = pl.BlockSpec((tm, tn), lambda j, i, k: (i, j))
    if out_hbm or out_blocked is not None:
        out_shape = pltpu.HBM(out_shape.shape, out_shape.dtype)

    n_extra = int(dep is not None)

    def body(a_ref, b_ref, *rest):
        o_ref, scratch = rest[n_extra], rest[n_extra + 1:]
        if nk == 1:
            o_ref[...] = _dot(a_ref[...], b_ref[...], dn).astype(out_dtype)
        else:
            acc_ref, = scratch
            k = pl.program_id(2)

            @pl.when(k == 0)
            def _():
                acc_ref[...] = jnp.zeros_like(acc_ref)

            acc_ref[...] += _dot(a_ref[...], b_ref[...], dn)

            @pl.when(k == nk - 1)
            def _():
                o_ref[...] = acc_ref[...].astype(out_dtype)

    return pl.pallas_call(
        body, name=name, grid=(nn, nm, nk),
        in_specs=[a_spec, b_spec] + ([] if dep is None else [pl.BlockSpec((8, 128), lambda j, i, k: (0, 0))]),
        out_specs=out_spec, out_shape=out_shape,
        scratch_shapes=[] if nk == 1 else [pltpu.VMEM((tm, tn), F32)],
        compiler_params=_params(("parallel", "parallel", "arbitrary")),
    )(a, b, *([] if dep is None else [dep]))


RMS_SIDE_ROWS = 512


def _rms_side(pairs):
    chunks = [min(x.shape[0], RMS_SIDE_ROWS) for x, _ in pairs]

    def work(ins, outs, scratch):
        sem = scratch[-1]

        def move(src, dst):
            cp = pltpu.make_async_copy(src, dst, sem.at[0])
            cp.start()
            cp.wait()

        for p, ((x, _), tr) in enumerate(zip(pairs, chunks)):
            x_ref, g_ref, h_ref = ins[2 * p], ins[2 * p + 1], outs[p]
            xv, gv, hv = scratch[3 * p:3 * p + 3]
            move(g_ref, gv)
            for i in range(x.shape[0] // tr):
                rows = pl.ds(i * tr, tr)
                move(x_ref.at[rows], xv)
                v = xv[...]
                hv[...] = (v * lax.rsqrt(jnp.mean(v * v, axis=-1, keepdims=True) + EPS) * gv[...]).astype(BF)
                move(hv, h_ref.at[rows])

    scratch = [s for (x, g), tr in zip(pairs, chunks)
               for s in (pltpu.VMEM((tr, x.shape[1]), F32), pltpu.VMEM(g.shape, F32), pltpu.VMEM((tr, x.shape[1]), BF))]
    return (work, [a for pair in pairs for a in pair], [jax.ShapeDtypeStruct(x.shape, BF) for x, _ in pairs],
            scratch + [pltpu.SemaphoreType.DMA((1,))])


def _rms_gain_grad(dn, x, name):
    R, Dm = x.shape

    def body(dn_ref, x_ref, o_ref):
        xv = x_ref[...]
        r = lax.rsqrt(jnp.mean(xv * xv, axis=-1, keepdims=True) + EPS)
        o_ref[...] = jnp.sum(dn_ref[...] * xv * r, axis=0, keepdims=True)

    return pl.pallas_call(
        body, name=name, out_shape=jax.ShapeDtypeStruct((1, Dm), F32),
        compiler_params=_params(),
    )(dn, x)


def _shift_down(v, k, head8, row, T):
    if k == 0:
        return v
    r = pltpu.roll(v, k, 0)
    hr = pltpu.roll(head8, k, 0)
    top = jnp.where(row[:8] < k, hr, r[:8])
    return jnp.concatenate([top, r[8:]], axis=0)


def _shift_up(v, k, tail8, row, T):
    if k == 0:
        return v
    r = pltpu.roll(v, T - k, 0)
    tr = pltpu.roll(tail8, 8 - k, 0)
    bot = jnp.where(row[:8] >= 8 - k, tr, r[T - 8:])
    return jnp.concatenate([r[:T - 8], bot], axis=0)


def _rglru_gates(u, head8, grow, row, T, cw_ref, cb_ref, wa_ref, ba_ref, wx_ref, bx_ref, lam_ref):
    us = [_shift_down(u, k, head8, row, T) for k in range(CONV_W)]
    acc = us[0] * cw_ref[0:1, :]
    for k in range(1, CONV_W):
        acc = acc + us[k] * cw_ref[k:k + 1, :]
    conv = cb_ref[...] + acc
    cbf = conv.astype(BF)
    r_ = _sigmoid(_dot(cbf, wa_ref[0], NN) + ba_ref[...])
    i_ = _sigmoid(_dot(cbf, wx_ref[0], NN) + bx_ref[...])
    sp = _softplus(-lam_ref[...])
    la = -LRU_C * r_ * sp
    a = jnp.exp(la)
    mult_raw = jnp.sqrt(-_expm1(2.0 * la))
    mult = jnp.where(grow == 0, 1.0, mult_raw)
    return us, conv, cbf, r_, i_, sp, a, mult_raw, mult


def _rglru_specs(T, nt, rev):
    tmap = (lambda n, t: (nt - 1 - t, n)) if rev else (lambda n, t: (t, n))
    hmap = ((lambda n, t: (jnp.maximum((nt - 1 - t) * (T // 8) - 1, 0), n)) if rev
            else (lambda n, t: (jnp.maximum(t * (T // 8) - 1, 0), n)))
    tile = pl.BlockSpec((T, RNN_BLOCK), tmap)
    halo = pl.BlockSpec((8, RNN_BLOCK), hmap)
    vec = pl.BlockSpec((1, RNN_BLOCK), lambda n, t: (0, n))
    cw = pl.BlockSpec((CONV_W, RNN_BLOCK), lambda n, t: (0, n))
    wblk = pl.BlockSpec((1, RNN_BLOCK, RNN_BLOCK), lambda n, t: (n, 0, 0))
    return tile, halo, vec, cw, wblk


def _rglru_fwd(xr, g, cw, cb, wa, ba, wx, bx, lam, T):
    S = xr.shape[0]
    nt = S // T

    def body(u_ref, uh_ref, g_ref, cw_ref, cb_ref, wa_ref, ba_ref, wx_ref, bx_ref, lam_ref, h_ref, y_ref, carry):
        t = pl.program_id(1)

        @pl.when(t == 0)
        def _():
            carry[...] = jnp.zeros_like(carry)

        row = lax.broadcasted_iota(jnp.int32, (T, RNN_BLOCK), 0)
        grow = row + t * T
        head8 = jnp.where(t > 0, uh_ref[...], 0.0)
        _, conv, _, _, i_, _, a, _, mult = _rglru_gates(u_ref[...], head8, grow, row, T, cw_ref, cb_ref, wa_ref, ba_ref,
                                                         wx_ref, bx_ref, lam_ref)
        b = mult * i_ * conv
        s = 1
        while s < T:
            keep = row >= s
            a_s = jnp.where(keep, pltpu.roll(a, s, 0), 1.0)
            b_s = jnp.where(keep, pltpu.roll(b, s, 0), 0.0)
            b = a * b_s + b
            a = a * a_s
            s *= 2
        h = b + a * carry[0:1, :]
        carry[...] = jnp.broadcast_to(h[T - 1:T, :], carry.shape)
        h_ref[...] = h
        gv = g_ref[...]
        y_ref[...] = (h * (gv * _sigmoid(gv))).astype(BF)

    tile, halo, vec, cwspec, wblk = _rglru_specs(T, nt, False)
    return pl.pallas_call(
        body, name="rglru_fwd", grid=(RNN_BLOCKS, nt),
        in_specs=[tile, halo, tile, cwspec, vec, wblk, vec, wblk, vec, vec],
        out_specs=[tile, tile],
        out_shape=[jax.ShapeDtypeStruct((S, D_RNN), F32), jax.ShapeDtypeStruct((S, D_RNN), BF)],
        scratch_shapes=[pltpu.VMEM((8, RNN_BLOCK), F32)],
        compiler_params=_params(("parallel", "arbitrary")),
    )(xr, xr, g, cw, cb, wa, ba, wx, bx, lam)


def _rglru_bwd(xr, g, h, dy, cw, cb, wa, ba, wx, bx, lam, T):
    S = xr.shape[0]
    nt = S // T

    def body(u_ref, uh_ref, g_ref, h_ref, hh_ref, dy_ref, cw_ref, cb_ref, wa_ref, ba_ref, wx_ref, bx_ref, lam_ref,
             du_ref, dg_ref, dwa_ref, dwx_ref, dvec_ref, c_dhh, c_a, c_dconv):
        t = pl.program_id(1)
        tt = nt - 1 - t

        @pl.when(t == 0)
        def _():
            c_dhh[...] = jnp.zeros_like(c_dhh)
            c_a[...] = jnp.zeros_like(c_a)
            c_dconv[...] = jnp.zeros_like(c_dconv)
            dwa_ref[...] = jnp.zeros_like(dwa_ref)
            dwx_ref[...] = jnp.zeros_like(dwx_ref)
            dvec_ref[...] = jnp.zeros_like(dvec_ref)

        row = lax.broadcasted_iota(jnp.int32, (T, RNN_BLOCK), 0)
        row8 = row[:8]
        grow = row + tt * T
        head8 = jnp.where(tt > 0, uh_ref[...], 0.0)
        us, conv, cbf, r_, i_, sp, a, mult_raw, mult = _rglru_gates(
            u_ref[...], head8, grow, row, T, cw_ref, cb_ref, wa_ref, ba_ref, wx_ref, bx_ref, lam_ref)
        hv = h_ref[...]
        hprev = _shift_down(hv, 1, jnp.where(tt > 0, hh_ref[...], 0.0), row, T)
        gv = g_ref[...]
        sg = _sigmoid(gv)
        dyv = dy_ref[...]
        dg_ref[...] = (dyv * hv * (sg * (1.0 + gv * (1.0 - sg)))).astype(BF)
        d = dyv * (gv * sg)
        A = _shift_up(a, 1, c_a[...], row, T)
        s = 1
        while s < T:
            keep = row < T - s
            A_s = jnp.where(keep, pltpu.roll(A, T - s, 0), 1.0)
            d_s = jnp.where(keep, pltpu.roll(d, T - s, 0), 0.0)
            d = A * d_s + d
            A = A * A_s
            s *= 2
        dhh = d + A * c_dhh[0:1, :]
        da = dhh * hprev
        dconv = dhh * mult * i_
        di = dhh * mult * conv
        dmult = dhh * i_ * conv
        dla = da * a - jnp.where(grow == 0, 0.0, dmult * (a * a) / mult_raw)
        dr = dla * (-LRU_C * sp)
        dsp = jnp.sum(dla * (-LRU_C * r_), axis=0, keepdims=True)
        dza = dr * r_ * (1.0 - r_)
        dzx = di * i_ * (1.0 - i_)
        dza_b, dzx_b = dza.astype(BF), dzx.astype(BF)
        dconv = dconv + _dot(dza_b, wa_ref[0], NT) + _dot(dzx_b, wx_ref[0], NT)
        dwa_ref[0] += _dot(cbf, dza_b, TN)
        dwx_ref[0] += _dot(cbf, dzx_b, TN)
        lam = lam_ref[...]
        rows = [jnp.sum(dconv * us[k], axis=0, keepdims=True) for k in range(CONV_W)]
        rows += [jnp.sum(dconv, axis=0, keepdims=True), jnp.sum(dza, axis=0, keepdims=True),
                 jnp.sum(dzx, axis=0, keepdims=True), dsp * (-_sigmoid(-lam))]
        upd = jnp.zeros((8, RNN_BLOCK), F32)
        for j, rv in enumerate(rows):
            upd = upd + jnp.where(row8 == j, rv, 0.0)
        dvec_ref[...] += upd
        tail8 = c_dconv[...]
        du = dconv * cw_ref[0:1, :]
        for k in range(1, CONV_W):
            du = du + _shift_up(dconv, k, tail8, row, T) * cw_ref[k:k + 1, :]
        du_ref[...] = du.astype(BF)
        c_dhh[...] = jnp.broadcast_to(dhh[0:1, :], c_dhh.shape)
        c_a[...] = jnp.broadcast_to(a[0:1, :], c_a.shape)
        c_dconv[...] = dconv[:8]

    tile, halo, vec, cwspec, wblk = _rglru_specs(T, nt, True)
    acc8 = pl.BlockSpec((8, RNN_BLOCK), lambda n, t: (0, n))
    return pl.pallas_call(
        body, name="rglru_bwd", grid=(RNN_BLOCKS, nt),
        in_specs=[tile, halo, tile, tile, halo, tile, cwspec, vec, wblk, vec, wblk, vec, vec],
        out_specs=[tile, tile, wblk, wblk, acc8],
        out_shape=[jax.ShapeDtypeStruct((S, D_RNN), BF), jax.ShapeDtypeStruct((S, D_RNN), BF),
                   jax.ShapeDtypeStruct((RNN_BLOCKS, RNN_BLOCK, RNN_BLOCK), F32),
                   jax.ShapeDtypeStruct((RNN_BLOCKS, RNN_BLOCK, RNN_BLOCK), F32),
                   jax.ShapeDtypeStruct((8, D_RNN), F32)],
        scratch_shapes=[pltpu.VMEM((8, RNN_BLOCK), F32)] * 3,
        compiler_params=_params(("parallel", "arbitrary")),
    )(xr, xr, g, h, h, dy, cw, cb, wa, ba, wx, bx, lam)


def _rel_bucket_map():
    qi = np.arange(WINDOW)[:, None]
    kj = np.arange(2 * WINDOW)[None, :]
    dist = jnp.asarray(qi + WINDOW - kj, jnp.int32)
    n = jnp.maximum(dist, 0)
    max_exact = REL_BUCKETS // 2
    ratio = jnp.log(jnp.maximum(n, 1).astype(F32) / max_exact) / math.log(REL_MAX_DIST / max_exact)
    large = jnp.minimum(max_exact + (ratio * (REL_BUCKETS - max_exact)).astype(jnp.int32), REL_BUCKETS - 1)
    bucket = jnp.where(n < max_exact, n, large).astype(jnp.int32)
    j = np.arange(WINDOW)[None, :]
    return jnp.where(jnp.asarray(j > qi), bucket[:, :WINDOW], bucket[:, WINDOW:])


def _swa_common(n, kv_ref, bucket_ref, relb_ref, bias_scr):
    @pl.when(n == 0)
    def _():
        bk = bucket_ref[...]
        for h in range(SWA_HEADS):
            acc = jnp.zeros((WINDOW, WINDOW), F32)
            for b in range(REL_BUCKETS):
                acc = acc + jnp.where(bk == b, relb_ref[b, h], 0.0)
            bias_scr[h] = acc

    prev0 = pl.multiple_of(jnp.maximum(n - 1, 0) * WINDOW, WINDOW)
    cur0 = pl.multiple_of(n * WINDOW, WINDOW)
    kk = jnp.concatenate([kv_ref[pl.ds(prev0, WINDOW), :], kv_ref[pl.ds(cur0, WINDOW), :]], axis=0).astype(F32)
    rowi = lax.broadcasted_iota(jnp.int32, (WINDOW, WINDOW), 0)
    col = lax.broadcasted_iota(jnp.int32, (WINDOW, WINDOW), 1)
    from_prev = col > rowi
    return kk, from_prev, prev0, cur0


def _fold(full, from_prev):
    return jnp.where(from_prev, full[:, :WINDOW], full[:, WINDOW:])


def _unfold(sq, from_prev):
    return jnp.concatenate([jnp.where(from_prev, sq, 0.0), jnp.where(from_prev, 0.0, sq)], axis=1)


def _half_pair(part, kvh):
    lo = lax.broadcasted_iota(jnp.int32, part.shape, 1) < SWA_HD
    if kvh == 0:
        pa = jnp.where(lo, part, 0.0)
        pb = pltpu.roll(pa, SWA_HD, 1)
    else:
        pb = jnp.where(lo, 0.0, part)
        pa = pltpu.roll(pb, SWA_HD, 1)
    return pa.astype(BF), pb.astype(BF)


ALL_HEADS = SWA_HEADS * WINDOW


def _sink_column(sinks):
    return jnp.repeat(sinks.reshape(SWA_HEADS), WINDOW).reshape(ALL_HEADS, 1)


def _swa_operands(kk):
    return [(_half_pair(kk[:, :128], kvh), _half_pair(kk[:, 128:], kvh)) for kvh in range(SWA_KV_HEADS)]


def _swa_probs(n, q_ref, ops, bias_scr, sinkc_ref, from_prev):
    lgs = []
    for kvh in range(SWA_KV_HEADS):
        (ka, kb), _ = ops[kvh]
        for p in range(4):
            q2 = q_ref[:, kvh * 512 + p * 128:kvh * 512 + p * 128 + 128]
            lgs += [_fold(_dot(q2, ka, NT), from_prev), _fold(_dot(q2, kb, NT), from_prev)]
    lg = jnp.concatenate(lgs, axis=0) * (SWA_HD ** -0.5) + bias_scr[...].reshape(ALL_HEADS, WINDOW)
    rowi = jnp.bitwise_and(lax.broadcasted_iota(jnp.int32, (ALL_HEADS, WINDOW), 0), WINDOW - 1)
    col = lax.broadcasted_iota(jnp.int32, (ALL_HEADS, WINDOW), 1)
    no_prev = jnp.where(n > 0, 0, 4 * WINDOW)
    lg = jnp.where(jnp.logical_or(col <= rowi, col > rowi + no_prev), lg, NEG_INF)
    sink = sinkc_ref[...]
    m = jnp.maximum(jnp.max(lg, axis=-1, keepdims=True), sink)
    e = jnp.exp(lg - m)
    es = jnp.exp(sink - m)
    den = jnp.sum(e, axis=-1, keepdims=True) + es
    return e / den, es / den


def _swa_fwd(q, kv, g, bucket, rel_bias, sink_col):
    S = q.shape[0]
    nb = S // WINDOW

    def body(q_ref, kv_ref, g_ref, bucket_ref, relb_ref, sinkc_ref, o_ref, y_ref, bias_scr):
        n = pl.program_id(0)
        kk, from_prev, _, _ = _swa_common(n, kv_ref, bucket_ref, relb_ref, bias_scr)
        ops = _swa_operands(kk)
        pr, _ = _swa_probs(n, q_ref, ops, bias_scr, sinkc_ref, from_prev)
        for kvh in range(SWA_KV_HEADS):
            _, (va, vb) = ops[kvh]
            for p in range(4):
                c0 = kvh * 512 + p * 128
                r0 = (kvh * 8 + 2 * p) * WINDOW
                o2 = (_dot(_unfold(pr[r0:r0 + WINDOW], from_prev).astype(BF), va, NN)
                      + _dot(_unfold(pr[r0 + WINDOW:r0 + 2 * WINDOW], from_prev).astype(BF), vb, NN))
                o_ref[:, c0:c0 + 128] = o2
                gv = g_ref[:, c0:c0 + 128]
                y_ref[:, c0:c0 + 128] = (o2 * (gv * _sigmoid(gv))).astype(BF)

    blk = pl.BlockSpec((WINDOW, 1024), lambda n: (n, 0))
    smem = pl.BlockSpec(memory_space=pltpu.SMEM)
    sinkc = pl.BlockSpec((ALL_HEADS, 1), lambda n: (0, 0))
    return pl.pallas_call(
        body, name="swa_fwd", grid=(nb,),
        in_specs=[blk, pl.BlockSpec((S, 256), lambda n: (0, 0)), blk, pl.BlockSpec((WINDOW, WINDOW), lambda n: (0, 0)), smem, sinkc],
        out_specs=[blk, blk],
        out_shape=[jax.ShapeDtypeStruct((S, 1024), F32), jax.ShapeDtypeStruct((S, 1024), BF)],
        scratch_shapes=[pltpu.VMEM((SWA_HEADS, WINDOW, WINDOW), F32)],
        compiler_params=_params(("arbitrary",)),
    )(q, kv, g, bucket, rel_bias, sink_col)


def _swa_bwd(q, kv, g, o, dy, bucket, rel_bias, sink_col, others):
    S = q.shape[0]
    nb = S // WINDOW
    first_col = {name: c0 for name, c0, _, _ in SEGMENTS}
    col_q, col_g = first_col["q_s"], first_col["g_swa"]
    copies = []
    for name, c0, width, _ in SEGMENTS:
        if name not in ("q_s", "kv", "g_swa"):
            arrs = others[name] if name == "gl" else (others[name],)
            copies += [(a, c0 + i * (width // len(arrs))) for i, a in enumerate(arrs)]

    def body(q_ref, kv_ref, g_ref, o_ref, dy_ref, bucket_ref, relb_ref, sinkc_ref, *refs):
        copy_refs = refs[:len(copies)]
        dp_ref, dkv_ref, dsink_ref, drel_ref, bias_scr, dbias_scr, dsink_scr = refs[len(copies):]
        n = pl.program_id(0)
        for c_ref, (a, c0) in zip(copy_refs, copies):
            dp_ref[:, c0:c0 + a.shape[1]] = c_ref[...]

        @pl.when(n == 0)
        def _():
            dbias_scr[...] = jnp.zeros_like(dbias_scr)
            dsink_scr[...] = jnp.zeros_like(dsink_scr)
            dkv_ref[...] = jnp.zeros_like(dkv_ref)

        kk, from_prev, prev0, cur0 = _swa_common(n, kv_ref, bucket_ref, relb_ref, bias_scr)
        ops = _swa_operands(kk)
        pr, ps = _swa_probs(n, q_ref, ops, bias_scr, sinkc_ref, from_prev)
        do2s, dps = [], []
        for kvh in range(SWA_KV_HEADS):
            _, (va, vb) = ops[kvh]
            for p in range(4):
                c0 = kvh * 512 + p * 128
                gv = g_ref[:, c0:c0 + 128]
                sg = _sigmoid(gv)
                dyv = dy_ref[:, c0:c0 + 128]
                dp_ref[:, col_g + c0:col_g + c0 + 128] = (dyv * o_ref[:, c0:c0 + 128] * (sg * (1.0 + gv * (1.0 - sg)))).astype(BF)
                do2 = (dyv * (gv * sg)).astype(BF)
                do2s.append(do2)
                dps += [_fold(_dot(do2, va, NT), from_prev), _fold(_dot(do2, vb, NT), from_prev)]
        dp = jnp.concatenate(dps, axis=0)
        delta = jnp.sum(pr * dp, axis=-1, keepdims=True)
        ds = pr * (dp - delta)
        dbias_scr[...] += ds.reshape(SWA_HEADS, WINDOW, WINDOW)
        dsink_scr[...] += ps * delta
        dsc = ds * (SWA_HD ** -0.5)
        lo256 = lax.broadcasted_iota(jnp.int32, (2 * WINDOW, 128), 1) < SWA_HD
        dks, dvs = [], []
        for kvh in range(SWA_KV_HEADS):
            (ka, kb), _ = ops[kvh]
            dka = jnp.zeros((2 * WINDOW, 128), F32)
            dkb, dva, dvb = dka, dka, dka
            for p in range(4):
                c0 = kvh * 512 + p * 128
                r0 = (kvh * 8 + 2 * p) * WINDOW
                q2 = q_ref[:, c0:c0 + 128]
                do2 = do2s[kvh * 4 + p]
                ds0 = _unfold(dsc[r0:r0 + WINDOW], from_prev).astype(BF)
                ds1 = _unfold(dsc[r0 + WINDOW:r0 + 2 * WINDOW], from_prev).astype(BF)
                dp_ref[:, col_q + c0:col_q + c0 + 128] = (_dot(ds0, ka, NN) + _dot(ds1, kb, NN)).astype(BF)
                dka = dka + _dot(ds0, q2, TN)
                dkb = dkb + _dot(ds1, q2, TN)
                dva = dva + _dot(_unfold(pr[r0:r0 + WINDOW], from_prev).astype(BF), do2, TN)
                dvb = dvb + _dot(_unfold(pr[r0 + WINDOW:r0 + 2 * WINDOW], from_prev).astype(BF), do2, TN)
            dks.append(jnp.where(lo256, dka, 0.0) + pltpu.roll(jnp.where(lo256, 0.0, dkb), SWA_HD, 1))
            dvs.append(jnp.where(lo256, dva, 0.0) + pltpu.roll(jnp.where(lo256, 0.0, dvb), SWA_HD, 1))
        dk = dks[0] + pltpu.roll(dks[1], SWA_HD, 1)
        dv = dvs[0] + pltpu.roll(dvs[1], SWA_HD, 1)
        dkv_ref[pl.ds(prev0, WINDOW), 0:128] += dk[:WINDOW]
        dkv_ref[pl.ds(prev0, WINDOW), 128:256] += dv[:WINDOW]
        dkv_ref[pl.ds(cur0, WINDOW), 0:128] += dk[WINDOW:]
        dkv_ref[pl.ds(cur0, WINDOW), 128:256] += dv[WINDOW:]

        @pl.when(n == nb - 1)
        def _():
            dsink_ref[...] = -jnp.sum(dsink_scr[...].reshape(SWA_HEADS, WINDOW, 1), axis=1)
            bk = bucket_ref[...]
            sums = []
            for b in range(REL_BUCKETS):
                sums.append(jnp.sum(jnp.where((bk == b)[None], dbias_scr[...], 0.0), axis=1))
            drel_ref[...] = jnp.sum(jnp.concatenate(sums, axis=0), axis=1, keepdims=True)

    blk = pl.BlockSpec((WINDOW, 1024), lambda n: (n, 0))
    smem = pl.BlockSpec(memory_space=pltpu.SMEM)
    whole = lambda shape: pl.BlockSpec(shape, lambda n: (0, 0))
    return pl.pallas_call(
        body, name="swa_bwd", grid=(nb,),
        in_specs=[blk, whole((S, 256)), blk, blk, blk, whole((WINDOW, WINDOW)), smem, whole((ALL_HEADS, 1))]
        + [pl.BlockSpec((WINDOW, a.shape[1]), lambda n: (n, 0)) for a, _ in copies],
        out_specs=[pl.BlockSpec((WINDOW, D_IN), lambda n: (n, 0)), whole((S, 256)), whole((SWA_HEADS, 1)),
                   whole((REL_BUCKETS * SWA_HEADS, 1))],
        out_shape=[jax.ShapeDtypeStruct((S, D_IN), BF), jax.ShapeDtypeStruct((S, 256), F32), jax.ShapeDtypeStruct((SWA_HEADS, 1), F32),
                   jax.ShapeDtypeStruct((REL_BUCKETS * SWA_HEADS, 1), F32)],
        scratch_shapes=[pltpu.VMEM((SWA_HEADS, WINDOW, WINDOW), F32), pltpu.VMEM((SWA_HEADS, WINDOW, WINDOW), F32),
                        pltpu.VMEM((ALL_HEADS, 1), F32)],
        compiler_params=_params(("arbitrary",)),
    )(q, kv, g, o, dy, bucket, rel_bias, sink_col, *[a for a, _ in copies])


MEM_TQ = 512


def _mem_probs(qh, mk):
    lg = _dot(qh, mk, NT) * (MEM_HD ** -0.5)
    e = jnp.exp(lg - jnp.max(lg, axis=-1, keepdims=True))
    return e / jnp.sum(e, axis=-1, keepdims=True)


def _mem_fwd(q, mkv, g):
    S = q.shape[0]
    M = mkv.shape[0]
    tq = min(S, MEM_TQ)

    def body(q_ref, mkv_ref, g_ref, o_ref, y_ref):
        for h in range(MEM_HEADS):
            c0 = h * MEM_HD
            pr = _mem_probs(q_ref[:, c0:c0 + MEM_HD], mkv_ref[:, c0:c0 + MEM_HD])
            o = _dot(pr.astype(BF), mkv_ref[:, D_MEM + c0:D_MEM + c0 + MEM_HD], NN)
            o_ref[:, c0:c0 + MEM_HD] = o
            gv = g_ref[:, c0:c0 + MEM_HD]
            y_ref[:, c0:c0 + MEM_HD] = (o * (gv * _sigmoid(gv))).astype(BF)

    blk = pl.BlockSpec((tq, D_MEM), lambda i: (i, 0))
    return pl.pallas_call(
        body, name="mem_fwd", grid=(S // tq,),
        in_specs=[blk, pl.BlockSpec((M, 2 * D_MEM), lambda i: (0, 0)), blk], out_specs=[blk, blk],
        out_shape=[jax.ShapeDtypeStruct((S, D_MEM), F32), jax.ShapeDtypeStruct((S, D_MEM), BF)],
        compiler_params=_params(("parallel",)),
    )(q, mkv, g)


def _mem_bwd(q, mkv, g, o, dy):
    S = q.shape[0]
    M = mkv.shape[0]
    tq = min(S, MEM_TQ)

    def body(q_ref, mkv_ref, g_ref, o_ref, dy_ref, dq_ref, dg_ref, dmkv_ref):
        @pl.when(pl.program_id(0) == 0)
        def _():
            dmkv_ref[...] = jnp.zeros_like(dmkv_ref)

        for h in range(MEM_HEADS):
            c0 = h * MEM_HD
            qh = q_ref[:, c0:c0 + MEM_HD]
            mk = mkv_ref[:, c0:c0 + MEM_HD]
            mv = mkv_ref[:, D_MEM + c0:D_MEM + c0 + MEM_HD]
            gv = g_ref[:, c0:c0 + MEM_HD]
            sg = _sigmoid(gv)
            dyv = dy_ref[:, c0:c0 + MEM_HD]
            dg_ref[:, c0:c0 + MEM_HD] = (dyv * o_ref[:, c0:c0 + MEM_HD] * (sg * (1.0 + gv * (1.0 - sg)))).astype(BF)
            do = (dyv * (gv * sg)).astype(BF)
            pr = _mem_probs(qh, mk)
            dp = _dot(do, mv, NT)
            ds = pr * (dp - jnp.sum(pr * dp, axis=-1, keepdims=True))
            dsb = (ds * (MEM_HD ** -0.5)).astype(BF)
            dq_ref[:, c0:c0 + MEM_HD] = _dot(dsb, mk, NN).astype(BF)
            dmkv_ref[:, c0:c0 + MEM_HD] += _dot(dsb, qh, TN)
            dmkv_ref[:, D_MEM + c0:D_MEM + c0 + MEM_HD] += _dot(pr.astype(BF), do, TN)

    blk = pl.BlockSpec((tq, D_MEM), lambda i: (i, 0))
    whole = pl.BlockSpec((M, 2 * D_MEM), lambda i: (0, 0))
    return pl.pallas_call(
        body, name="mem_bwd", grid=(S // tq,),
        in_specs=[blk, whole, blk, blk, blk], out_specs=[blk, blk, whole],
        out_shape=[jax.ShapeDtypeStruct((S, D_MEM), BF), jax.ShapeDtypeStruct((S, D_MEM), BF),
                   jax.ShapeDtypeStruct((M, 2 * D_MEM), F32)],
        compiler_params=_params(("arbitrary",)),
    )(q, mkv, g, o, dy)


MERGE_TN = 512
MERGE_TM = 1024
MERGE_BWD_TN = 256


def _merge_specs(tm, tn):
    ytile = pl.BlockSpec((tm, 1024), lambda i, j: (i, 0))
    wblk = pl.BlockSpec((tn, 1024), lambda i, j: (j, 0))
    gls = [pl.BlockSpec((None, tm, tn), (lambda i, j, br=br: (br, i, j))) for br in range(3)]
    otile = pl.BlockSpec((tm, tn), lambda i, j: (i, j))
    return ytile, wblk, gls, otile


def _merge_fwd(ys, ws, gl, tm):
    S = gl.shape[1]

    def body(y0, y1, y2, w0, w1, w2, g0, g1, g2, o_ref):
        acc = None
        for y_ref, w_ref, g_ref in ((y0, w0, g0), (y1, w1, g1), (y2, w2, g2)):
            term = _sigmoid(g_ref[...]) * _dot(y_ref[...], w_ref[...], NT)
            acc = term if acc is None else acc + term
        o_ref[...] = acc.astype(BF)

    ytile, wblk, gls, otile = _merge_specs(tm, MERGE_TN)
    return pl.pallas_call(
        body, name="merge_fwd", grid=(S // tm, D_MODEL // MERGE_TN),
        in_specs=[ytile] * 3 + [wblk] * 3 + gls, out_specs=otile,
        out_shape=jax.ShapeDtypeStruct((S, D_MODEL), BF),
        compiler_params=_params(("parallel", "arbitrary")),
    )(*ys, *ws, gl, gl, gl)


def _merge_bwd(dout, w_out, ys, ws, gl, tm):
    S = gl.shape[1]

    def body(do_ref, wo_ref, y0, y1, y2, w0, w1, w2, g0, g1, g2, dg0, dg1, dg2, dp0, dp1, dp2):
        dm = _dot(do_ref[...], wo_ref[...], NT)
        for y_ref, w_ref, g_ref, dg_ref, dp_ref in ((y0, w0, g0, dg0, dp0), (y1, w1, g1, dg1, dp1), (y2, w2, g2, dg2, dp2)):
            gate = _sigmoid(g_ref[...])
            pv = _dot(y_ref[...], w_ref[...], NT)
            dg_ref[...] = (dm * pv * gate * (1.0 - gate)).astype(BF)
            dp_ref[...] = (dm * gate).astype(BF)

    ytile, wblk, gls, otile = _merge_specs(tm, MERGE_BWD_TN)
    out = jax.ShapeDtypeStruct((S, D_MODEL), BF)
    return pl.pallas_call(
        body, name="merge_bwd", grid=(S // tm, D_MODEL // MERGE_BWD_TN),
        in_specs=[pl.BlockSpec((tm, D_MODEL), lambda i, j: (i, 0)), pl.BlockSpec((MERGE_BWD_TN, D_MODEL), lambda i, j: (j, 0))]
        + [ytile] * 3 + [wblk] * 3 + gls,
        out_specs=[otile] * 6, out_shape=[out] * 6,
        compiler_params=_params(("parallel", "arbitrary")),
    )(dout, w_out, *ys, *ws, gl, gl, gl)


def _out_loss(merged, w_out, x, target, post_g, tm):
    S = x.shape[0]

    def body(m_ref, w_ref, x_ref, t_ref, g_ref, dout_ref, dy_ref, loss_ref, dpost_ref):
        @pl.when(pl.program_id(0) == 0)
        def _():
            loss_ref[...] = jnp.zeros_like(loss_ref)
            dpost_ref[...] = jnp.zeros_like(dpost_ref)

        out = _dot(m_ref[...], w_ref[...], NN)
        r = lax.rsqrt(jnp.mean(out * out, axis=-1, keepdims=True) + EPS)
        nrm = out * r
        gv = g_ref[...]
        err = (x_ref[...] + nrm * gv) - t_ref[...]
        sq = jnp.sum(jnp.sum(err * err, axis=1, keepdims=True), axis=0, keepdims=True)
        loss_ref[...] += sq * (0.5 / D_MODEL)
        dy = err * (1.0 / D_MODEL)
        dy_ref[...] = dy
        dpost_ref[...] += jnp.sum(dy * nrm, axis=0, keepdims=True)
        dn = dy * gv
        dout_ref[...] = (r * (dn - nrm * jnp.mean(dn * nrm, axis=-1, keepdims=True))).astype(BF)

    row = pl.BlockSpec((tm, D_MODEL), lambda i: (i, 0))
    return pl.pallas_call(
        body, name="out_loss", grid=(S // tm,),
        in_specs=[row, pl.BlockSpec((D_MODEL, D_MODEL), lambda i: (0, 0)), row, row, pl.BlockSpec((1, D_MODEL), lambda i: (0, 0))],
        out_specs=[row, row, pl.BlockSpec((8, 128), lambda i: (0, 0)), pl.BlockSpec((1, D_MODEL), lambda i: (0, 0))],
        out_shape=[jax.ShapeDtypeStruct((S, D_MODEL), BF), jax.ShapeDtypeStruct((S, D_MODEL), F32),
                   jax.ShapeDtypeStruct((8, 128), F32), jax.ShapeDtypeStruct((1, D_MODEL), F32)],
        compiler_params=_params(("arbitrary",)),
    )(merged, w_out, x, target, post_g)


DH_DX_CHUNK = 64


def _dh_dx(dproj, w_in, x, dy, pre_g, tm, tk):
    S = x.shape[0]
    nk = D_IN // tk

    def body(dp_ref, w_ref, x_ref, dy_ref, g_ref, dx_ref, dpre_ref, acc_ref):
        i, k = pl.program_id(0), pl.program_id(1)

        @pl.when(jnp.logical_and(i == 0, k == 0))
        def _():
            dpre_ref[...] = jnp.zeros_like(dpre_ref)

        @pl.when(k == 0)
        def _():
            acc_ref[...] = jnp.zeros_like(acc_ref)

        acc_ref[...] += _dot(dp_ref[...], w_ref[...], NN)

        @pl.when(k == nk - 1)
        def _():
            def chunk(c, carry):
                rows = pl.ds(pl.multiple_of(c * DH_DX_CHUNK, DH_DX_CHUNK), DH_DX_CHUNK)
                dh = acc_ref[rows, :]
                xv = x_ref[rows, :]
                r = lax.rsqrt(jnp.mean(xv * xv, axis=-1, keepdims=True) + EPS)
                nrm = xv * r
                dpre_ref[...] += jnp.sum(dh * nrm, axis=0, keepdims=True)
                dn = dh * g_ref[...]
                dx_ref[rows, :] = r * (dn - nrm * jnp.mean(dn * nrm, axis=-1, keepdims=True)) + dy_ref[rows, :]
                return carry
            lax.fori_loop(0, tm // DH_DX_CHUNK, chunk, 0)

    row = pl.BlockSpec((tm, D_MODEL), lambda i, k: (i, 0))
    vec = pl.BlockSpec((1, D_MODEL), lambda i, k: (0, 0))
    return pl.pallas_call(
        body, name="dh_dx", grid=(S // tm, nk),
        in_specs=[pl.BlockSpec((tm, tk), lambda i, k: (i, k)), pl.BlockSpec((tk, D_MODEL), lambda i, k: (k, 0)), row, row, vec],
        out_specs=[row, vec],
        out_shape=[jax.ShapeDtypeStruct((S, D_MODEL), F32), jax.ShapeDtypeStruct((1, D_MODEL), F32)],
        scratch_shapes=[pltpu.VMEM((tm, D_MODEL), F32)],
        compiler_params=_params(("arbitrary", "arbitrary"), large=True),
    )(dproj, w_in, x, dy, pre_g)


def _sum_parts(parts, name):
    P, R, C = parts.shape
    tr = max(t for t in range(8, 513, 8) if R % t == 0)

    def body(p_ref, o_ref):
        acc = p_ref[0]
        for j in range(1, P):
            acc = acc + p_ref[j]
        o_ref[...] = acc

    return pl.pallas_call(
        body, name=name, grid=(R // tr,),
        in_specs=[pl.BlockSpec((P, tr, C), lambda i: (0, i, 0))], out_specs=pl.BlockSpec((tr, C), lambda i: (i, 0)),
        out_shape=jax.ShapeDtypeStruct((R, C), F32), compiler_params=_params(("parallel",)),
    )(parts)


def _adamw(land, sums, chip, w, m, v, name, group=(0, 1), into=None):
    q, n_groups = group
    _, R, cols = land.shape
    C = cols * n_groups
    tr = max(t for t in range(16, 257, 16) if R % t == 0)
    c1 = 1.0 - ADAM_B1 ** ADAM_STEP
    c2 = 1.0 - ADAM_B2 ** ADAM_STEP
    n_into = 0 if into is None else 4

    def body(chip_ref, p0_ref, p1_ref, p2_ref, own_ref, w_ref, m_ref, v_ref, *refs):
        g_ref, d_ref, nm_ref, nv_ref = refs[n_into:]
        g = own_ref[...].astype(F32)
        for p_ref in (p0_ref, p1_ref, p2_ref):
            g = g + p_ref[...].astype(F32)
        nm = ADAM_B1 * m_ref[...] + (1.0 - ADAM_B1) * g
        nv = ADAM_B2 * v_ref[...] + (1.0 - ADAM_B2) * (g * g)
        g_ref[...] = g
        nm_ref[...] = nm
        nv_ref[...] = nv
        d_ref[...] = -ADAM_LR * ((nm / c1) / (jnp.sqrt(nv / c2) + ADAM_EPS) + ADAM_WD * w_ref[...])

    tile = pl.BlockSpec((None, tr, cols), lambda i, c_ref: (0, i, q))
    specs = [pl.BlockSpec((None, tr, cols), (lambda i, c_ref, k=k: (k + (c_ref[0] <= k).astype(jnp.int32), i, 0))) for k in range(3)]
    specs.append(pl.BlockSpec((None, tr, cols), (lambda i, c_ref: (c_ref[0], i, 0))))
    return pl.pallas_call(
        body, name=name,
        grid_spec=pltpu.PrefetchScalarGridSpec(num_scalar_prefetch=1, grid=(R // tr,),
                                               in_specs=specs + [tile, tile, tile] + [pl.BlockSpec(memory_space=pl.ANY)] * n_into,
                                               out_specs=[tile] * 4),
        out_shape=[jax.ShapeDtypeStruct((1, R, C), F32)] * 4,
        input_output_aliases={8 + k: k for k in range(n_into)},
        compiler_params=_params(("parallel",)),
    )(chip, land, land, land, sums, w, m, v, *(into or []))


def _adamw_small(gs, ws, ms, vs):
    n = len(ws)
    c1 = 1.0 - ADAM_B1 ** ADAM_STEP
    c2 = 1.0 - ADAM_B2 ** ADAM_STEP

    def flat2(a):
        return a.reshape(-1, a.shape[-1])

    def body(*refs):
        ins, outs = refs[:4 * n], refs[4 * n:]
        for a in range(n):
            g, w, m, v = (ins[k * n + a][...] for k in range(4))
            nm = ADAM_B1 * m + (1.0 - ADAM_B1) * g
            nv = ADAM_B2 * v + (1.0 - ADAM_B2) * (g * g)
            outs[a][...] = g
            outs[n + a][...] = -ADAM_LR * ((nm / c1) / (jnp.sqrt(nv / c2) + ADAM_EPS) + ADAM_WD * w)
            outs[2 * n + a][...] = nm
            outs[3 * n + a][...] = nv

    shapes = [flat2(w).shape for w in ws]
    out = pl.pallas_call(
        body, name="adamw_small", out_shape=[jax.ShapeDtypeStruct(sh, F32) for sh in shapes] * 4,
        compiler_params=_params(),
    )(*[g.reshape(sh) for g, sh in zip(gs, shapes)], *[flat2(a) for a in (*ws, *ms, *vs)])
    return [[out[k * n + a].reshape(ws[a].shape) for a in range(n)] for k in range(4)]


PROJ_ROWS = 512


def _project(h, w_t, dep=None):
    S = h.shape[0]
    n_tiles = D_IN // SEG_TILE
    ranges = [(c0 // SEG_TILE, (c0 + width) // SEG_TILE) for _, c0, width, _ in SEGMENTS]
    dtypes = (F32, BF)
    n_extra = int(dep is not None)

    def of_dtype(j, dt):
        hit = False
        for (j0, j1), seg in zip(ranges, SEGMENTS):
            if seg[3] == dt:
                hit = jnp.logical_and(j >= j0, j < j1) | hit
        return hit

    def body(h_ref, w_ref, *refs):
        outs = refs[n_extra:n_extra + len(SEGMENTS)]
        stages, sems = refs[n_extra + len(SEGMENTS):-1], refs[-1]
        j = pl.program_id(0)
        slot = j % 2

        def copy_out(k, stage, dst):
            return pltpu.make_async_copy(stages[k].at[stage], dst, sems.at[stage])

        def wait_tile(jj, stage):
            for k, dt in enumerate(dtypes):
                o_ref = [o for o, seg in zip(outs, SEGMENTS) if seg[3] == dt and seg[0] != "gl"][0]
                @pl.when(of_dtype(jj, dt))
                def _(k=k, o_ref=o_ref):
                    copy_out(k, stage, o_ref.at[:, pl.ds(0, SEG_TILE)]).wait()

        @pl.when(j >= 2)
        def _():
            wait_tile(j - 2, slot)

        for k, dt in enumerate(dtypes):
            @pl.when(of_dtype(j, dt))
            def _(k=k, dt=dt):
                for c in range(S // PROJ_ROWS):
                    rows = pl.ds(c * PROJ_ROWS, PROJ_ROWS)
                    stages[k][slot, rows, :] = _dot(h_ref[rows, :], w_ref[...], NT).astype(dt)

        for (j0, j1), (name, _, _, dt), o_ref in zip(ranges, SEGMENTS, outs):
            @pl.when(jnp.logical_and(j >= j0, j < j1))
            def _(j0=j0, j1=j1, name=name, dt=dt, o_ref=o_ref):
                t = j - j0
                if name == "gl":
                    per = (j1 - j0) // 3
                    dst = o_ref.at[t // per, :, pl.ds(pl.multiple_of((t % per) * SEG_TILE, SEG_TILE), SEG_TILE)]
                else:
                    dst = o_ref.at[:, pl.ds(pl.multiple_of(t * SEG_TILE, SEG_TILE), SEG_TILE)]
                copy_out(dtypes.index(dt), slot, dst).start()

        @pl.when(j == n_tiles - 1)
        def _():
            wait_tile(j - 1, 1 - slot)
            wait_tile(j, slot)

    out_shapes = [jax.ShapeDtypeStruct((3, S, width // 3) if name == "gl" else (S, width), dt) for name, _, width, dt in SEGMENTS]
    outs = pl.pallas_call(
        body, name="proj", grid=(n_tiles,),
        in_specs=[pl.BlockSpec((S, D_MODEL), lambda j: (0, 0)), pl.BlockSpec((SEG_TILE, D_MODEL), lambda j: (j, 0))]
        + ([] if dep is None else [pl.BlockSpec((8, 128), lambda j: (0, 0))]),
        out_specs=[pl.BlockSpec(memory_space=pl.ANY)] * len(SEGMENTS), out_shape=out_shapes,
        scratch_shapes=[pltpu.VMEM((2, S, SEG_TILE), dt) for dt in dtypes] + [pltpu.SemaphoreType.DMA((2,))],
        compiler_params=_params(("arbitrary",), large=True),
    )(h, w_t, *([] if dep is None else [dep]))
    return {name: o for (name, _, _, _), o in zip(SEGMENTS, outs)}


def _forward_a(h, memn, w_in, sinks, rel_bias, dep=None):
    S = h.shape[0]
    st = dict(T=min(512, S // 2), tm=min(512, S), bucket=_rel_bucket_map(), h=h, memn=memn)
    seg = st["seg"] = _project(h, w_in, dep)
    st["o_swa"], st["y_swa"] = _swa_fwd(seg["q_s"], seg["kv"], seg["g_swa"], st["bucket"], rel_bias, _sink_column(sinks))
    return st


def _forward_rg(st, conv_w, conv_b, w_a, b_a, w_x, b_x, lam):
    seg = st["seg"]
    st["h_rg"], st["y_rg"] = _rglru_fwd(seg["xr"], seg["g_rg"], conv_w, conv_b, w_a, b_a, w_x, b_x, lam, st["T"])
    return st


def _forward_b(st, x, target, post_g, w_memkv, wbr, w_out):
    S = x.shape[0]
    M = st["memn"].shape[0]
    seg = st["seg"]
    st["mkv"] = _matmul(st["memn"], w_memkv, "nn", M, 2 * D_MEM, D_MODEL, M, 512, D_MODEL, BF, "mem_kv")
    st["o_mem"], st["y_mem"] = _mem_fwd(seg["q_m"], st["mkv"], seg["g_mem"])
    st["ys"] = (st["y_rg"], st["y_swa"], st["y_mem"])
    st["merged"] = _merge_fwd(st["ys"], wbr, seg["gl"], min(S, MERGE_TM))
    st["dout"], st["dy"], st["loss"], st["dpost"] = _out_loss(st["merged"], w_out, x, target, post_g, min(256, S))
    return st


def _backward_a1(st, wbr, w_out):
    S = st["h"].shape[0]
    seg, ys, tm = st["seg"], st["ys"], st["tm"]
    st["dw_out"] = _matmul(st["merged"], st["dout"], "tn", D_MODEL, D_MODEL, S, 256, D_MODEL, S, BF, "dw_out", out_blocked="row")
    dgl0, dgl1, dgl2, dp0, dp1, dp2 = _merge_bwd(st["dout"], w_out, ys, wbr, seg["gl"], min(S, MERGE_TM))
    st["dgl"] = (dgl0, dgl1, dgl2)
    dys, dwbr = [], []
    for i, dp in enumerate((dp0, dp1, dp2)):
        dys.append(_matmul(dp, wbr[i], "nn", S, 1024, D_MODEL, tm, 1024, D_MODEL, F32, "dy_br%d" % i))
        dwbr.append(_matmul(ys[i], dp, "tn", 1024, D_MODEL, S, 1024, 256, S, BF, "dw_br%d" % i, out_blocked="col"))
    st["dys"], st["dwbr"] = dys, dwbr
    return st


def _backward_a2(st, mem, w_memkv, conv_w, conv_b, w_a, b_a, w_x, b_x, lam):
    M = mem.shape[0]
    seg, dys = st["seg"], st["dys"]
    st["dq_m"], st["dg_mem"], dmkv = _mem_bwd(seg["q_m"], st["mkv"], seg["g_mem"], st["o_mem"], dys[2])
    st["dmkv"] = dmkv.astype(BF)
    st["dw_memkv"] = _matmul(st["memn"], st["dmkv"], "tn", D_MODEL, 2 * D_MEM, M, 256, 2 * D_MEM, M, BF, "dw_memkv", out_blocked="row")
    st["dxr"], st["dg_rg"], st["dw_a"], st["dw_x"], st["dvec"] = _rglru_bwd(
        seg["xr"], seg["g_rg"], st["h_rg"], dys[0], conv_w, conv_b, w_a, b_a, w_x, b_x, lam, st["T"])
    return st


def _mem_gain_grad(st, mem, w_memkv, dep=None):
    M = mem.shape[0]
    dmemn = _matmul(st["dmkv"], w_memkv, "nt", M, D_MODEL, 2 * D_MEM, M, 512, 2 * D_MEM, F32, "dmemn", dep=dep)
    return _rms_gain_grad(dmemn, mem, "dmem_gain")


def _backward_b(st, rel_bias, sinks):
    seg = st["seg"]
    others = {"xr": st["dxr"], "g_rg": st["dg_rg"], "q_m": st["dq_m"], "g_mem": st["dg_mem"], "gl": st["dgl"]}
    dproj, dkv, dsinks, drel = _swa_bwd(seg["q_s"], seg["kv"], seg["g_swa"], st["o_swa"], st["dys"][1],
                                        st["bucket"], rel_bias, _sink_column(sinks), others)
    st["dsinks"], st["drel"] = dsinks.reshape(1, SWA_HEADS), drel.reshape(REL_BUCKETS, SWA_HEADS)
    col_kv = [c0 for name, c0, _, _ in SEGMENTS if name == "kv"][0]
    st["dproj"] = lax.dynamic_update_slice(dproj, dkv.astype(BF), (0, col_kv))
    return st


def _dw_in_half(st, half, dep=None):
    S = st["h"].shape[0]
    dw = _matmul(st["dproj"], st["h"], "tn", D_IN, D_MODEL // 2, S, D_IN_TILE, D_MODEL // 2, S, BF, "dw_in%d" % half, b_noff=half, dep=dep, out_hbm=True)
    return dw.reshape(N_DEV, D_IN // N_DEV, D_MODEL // 2)


def _owner_blocks(a):
    return jnp.swapaxes(a.reshape((4, 2) + a.shape[1:]), 0, 1)


def _pad_rows(a, rows):
    a = a.reshape(-1, 128) if a.shape[-1] % 128 == 0 else jnp.pad(a, ((0, 0), (0, 128 - a.shape[-1])))
    return jnp.pad(a, ((0, rows - a.shape[0]), (0, 0))) if a.shape[0] < rows else a


def kernel(x, mem, pre_norm_g, post_norm_g, mem_norm_g, w_in, conv_w, conv_b, w_rg_a, b_rg_a, w_rg_x, b_rg_x, lru_lambda, swa_sinks, rel_bias, w_mem_kv, w_br_rg, w_br_swa, w_br_mem, w_out, loss_target, m_pre_norm_g, m_post_norm_g, m_mem_norm_g, m_w_in, m_conv_w, m_conv_b, m_w_rg_a, m_b_rg_a, m_w_rg_x, m_b_rg_x, m_lru_lambda, m_swa_sinks, m_rel_bias, m_w_mem_kv, m_w_br_rg, m_w_br_swa, m_w_br_mem, m_w_out, v_pre_norm_g, v_post_norm_g, v_mem_norm_g, v_w_in, v_conv_w, v_conv_b, v_w_rg_a, v_b_rg_a, v_w_rg_x, v_b_rg_x, v_lru_lambda, v_swa_sinks, v_rel_bias, v_w_mem_kv, v_w_br_rg, v_w_br_swa, v_w_br_mem, v_w_out):
    cx, cy, cc = lax.axis_index("x"), lax.axis_index("y"), lax.axis_index("c")
    me = 4 * cx + 2 * cy + cc
    chip = 2 * cx + cy
    core = jnp.reshape(cc, (1,)).astype(jnp.int32)
    x0, mem0 = x[0], mem[0]
    w_a_b, w_x_b = w_rg_a[0].astype(BF), w_rg_x[0].astype(BF)

    def landing(own, slot, slots):
        return lax.dynamic_update_slice(lax.empty((slots,) + own.shape, own.dtype), own[None], (slot,) + (0,) * own.ndim)


    def swap_start(parts, tag):
        return _exchange_start(parts, [lax.empty((4,) + p.shape[-2:], p.dtype) for p in parts], _plan_swap([p.ndim for p in parts]),
                               "swap_%s_start" % tag)

    def scatter_start(swap, after, tag, prefill=()):
        s_send, s_recv, parts, got, _ = swap
        got = _exchange_wait(s_send, s_recv, parts, got, _plan_swap([p.ndim for p in parts]), after, "swap_%s_wait" % tag)
        sums = [_pair_sum(p, g, core, "scatter_%s_sum%d" % (tag, i)) for i, (p, g) in enumerate(zip(parts, got))]
        lands = [landing(lax.dynamic_index_in_dim(s, chip, 0, keepdims=False), chip, 4) if i in prefill
                 else lax.empty(s.shape, s.dtype) for i, s in enumerate(sums)]
        return _exchange_start(sums, lands, _plan_scatter(len(sums)), "scatter_%s_start" % tag)

    def corner(a):
        return a.reshape(-1, a.shape[-1])[:8, :128]

    def zero_after(a):
        return jnp.minimum(jnp.abs(a.reshape(-1)[0].astype(F32)), 0.0)

    g_in, g_cw, h0, memn0 = _all_gather_relayed([jnp.transpose(w_in[0]).astype(BF), conv_w[0]], [True, False], "gather_w_in",
                                                side=_rms_side([(x0, pre_norm_g), (mem0, mem_norm_g)]))
    w_in_f = g_in.reshape(D_IN, D_MODEL)
    conv_w_f = jnp.transpose(g_cw, (1, 0, 2)).reshape(CONV_W, D_RNN)

    after_first = zero_after(g_cw).astype(BF)
    rest = [w.astype(BF) + after_first for w in (w_mem_kv[0], jnp.transpose(w_br_rg[0]), jnp.transpose(w_br_swa[0]),
                                                 jnp.transpose(w_br_mem[0]), w_out[0])]
    plan_g = _plan_gather(len(rest))
    zones = _place_own([lax.empty((N_DEV,) + w.shape, w.dtype) for w in rest], rest, jnp.reshape(me, (1,)).astype(jnp.int32), "gather_rest_own")
    g_send, g_recv, g_src, g_land, g_token = _exchange_start(rest, zones, plan_g, "gather_rest_start")
    st = _forward_a(h0, memn0, w_in_f, swa_sinks, rel_bias, dep=g_token)
    g_land = _exchange_wait(g_send, g_recv, g_src, g_land, plan_g, st["y_swa"], "gather_rest_wait")
    plan_f = _plan_forward(len(rest))
    f_send, f_recv, _, g_land, f_token = _exchange_start(None, g_land, plan_f, "forward_rest_start")
    st = _forward_rg(st, conv_w_f, conv_b + f_token[0:1, 0:1], w_a_b, b_rg_a, w_x_b, b_rg_x, lru_lambda)
    g_land = _exchange_wait(f_send, f_recv, None, g_land, plan_f, corner(st["y_rg"]), "forward_rest_wait")
    w_memkv_f = g_land[0].reshape(D_MODEL, 2 * D_MEM)
    wbr = tuple(g_land[i].reshape(D_MODEL, D_RNN) for i in (1, 2, 3))
    w_out_f = g_land[4].reshape(D_MODEL, D_MODEL)

    st = _forward_b(st, x0, loss_target[0], post_norm_g, w_memkv_f, wbr, w_out_f)
    st = _backward_a1(st, wbr, w_out_f)
    parts_a = [st["dw_out"], st["dwbr"][0], st["dwbr"][1], st["dwbr"][2]]
    plan_a = _plan_scatter(len(parts_a))
    swap_a = swap_start(parts_a, "a")
    st = _backward_a2(st, mem0, w_memkv_f, conv_w_f, conv_b + swap_a[4][0:1, 0:1], w_a_b, b_rg_a, w_x_b, b_rg_x, lru_lambda)
    a_send, a_recv, a_src, a_land, a_token = scatter_start(swap_a, st["dxr"], "a")
    parts_c = [st["dw_memkv"], _owner_blocks(st["dw_a"]), _owner_blocks(st["dw_x"])]
    plan_c = _plan_scatter(len(parts_c))
    swap_c = swap_start(parts_c, "c")

    st = _backward_b(st, rel_bias, swa_sinks + swap_c[4][0:1, 0:1] + a_token[0:1, 0:1])
    c_send, c_recv, c_src, c_land, c_token = scatter_start(swap_c, st["dsinks"], "c", prefill=(1, 2))
    plan_b = _plan_scatter(1)

    def dw_in_parts(half, dep):
        dwh = _dw_in_half(st, half, dep)
        return dwh, [dwh]

    dw0, parts_b0 = dw_in_parts(0, c_token)
    swap_b0 = swap_start(parts_b0, "b0")
    a_land = _exchange_wait(a_send, a_recv, a_src, a_land, plan_a, swap_b0[4], "scatter_a_wait")
    big = [None] * 6

    chip1 = jnp.reshape(chip, (1,)).astype(jnp.int32)

    def adamw_big(j, land, own, wt, mt, vt):
        big[j] = _adamw(land, own, chip1, wt, mt, vt, "adamw_big%d" % j)

    adamw_big(5, a_land[0], a_src[0], w_out, m_w_out, v_w_out)
    adamw_big(2, a_land[1], a_src[1], w_br_rg, m_w_br_rg, v_w_br_rg)
    adamw_big(3, a_land[2], a_src[2], w_br_swa, m_w_br_swa, v_w_br_swa)
    halves = [scatter_start(swap_b0, corner(big[5][1]) + corner(big[2][1]) + corner(big[3][1]), "b0")]
    dw1, parts_b1 = dw_in_parts(1, halves[0][4])
    swap_b1 = swap_start(parts_b1, "b1")
    c_land = _exchange_wait(c_send, c_recv, c_src, c_land, plan_c, swap_b1[4], "scatter_c_wait")
    g_wa_blk = _sum_parts(c_land[1], "sum_w_rg_a")
    g_wx_blk = _sum_parts(c_land[2], "sum_w_rg_x")
    adamw_big(4, a_land[3], a_src[3], w_br_mem, m_w_br_mem, v_w_br_mem)
    adamw_big(1, c_land[0], c_src[0], w_mem_kv, m_w_mem_kv, v_w_mem_kv)
    halves.append(scatter_start(swap_b1, corner(big[4][1]) + corner(big[1][1]), "b1"))
    st["dmem_g"] = _mem_gain_grad(st, mem0, w_memkv_f, halves[1][4])
    grad_x, dpre = _dh_dx(st["dproj"], w_in_f, x0, st["dy"], pre_norm_g + halves[1][4][0:1, 0:1], st["tm"], D_IN_TILE)
    pack = jnp.concatenate([dpre.reshape(16, 128), st["dpost"].reshape(16, 128), st["dmem_g"].reshape(16, 128),
                            st["dvec"].reshape(64, 128), _pad_rows(st["dsinks"], 8), _pad_rows(st["drel"], 32), g_wa_blk, g_wx_blk,
                            st["loss"]], axis=0)
    plan_s = _plan_everyone(1)
    s_send, s_recv, s_src, s_land, s_token = _exchange_start([pack], [landing(pack, me, N_DEV)], plan_s, "gather_small_start")
    swap_last = lambda a: jnp.transpose(a, (0, 2, 1))
    after, big0_t = s_token, None
    for half, (b_send, b_recv, b_src, b_land, _) in enumerate(halves):
        b_land = _exchange_wait(b_send, b_recv, b_src, b_land, plan_b, after, "scatter_b%d_wait" % half)[0]
        big0_t = _adamw(b_land, b_src[0], chip1, swap_last(w_in), swap_last(m_w_in), swap_last(v_w_in), "adamw_big0_%d" % half,
                        group=(half, 2), into=big0_t)
        after = corner(big0_t[1])
    big[0] = [swap_last(a) for a in big0_t]
    gathered = _exchange_wait(s_send, s_recv, s_src, s_land, plan_s, corner(big0_t[1]), "gather_small_wait")[0]
    gs = _sum_parts(gathered, "sum_small")
    loss_total = gs[408, 0]
    g_pre, g_post, g_memg = gs[0:16].reshape(1, D_MODEL), gs[16:32].reshape(1, D_MODEL), gs[32:48].reshape(1, D_MODEL)
    gvec = gs[48:112].reshape(8, D_RNN)
    g_conv_w = lax.dynamic_slice(gvec[0:CONV_W], (0, me * RNN_BLOCK), (CONV_W, RNN_BLOCK))
    g_conv_b, g_b_a, g_b_x, g_lam = gvec[4:5], gvec[5:6], gvec[6:7], gvec[7:8]
    g_sinks = gs[112:113, :SWA_HEADS]
    g_rel = gs[120:152, :SWA_HEADS]
    g_w_a = gathered[:, 152:280]
    g_w_x = gathered[:, 280:408]

    g_small = (g_pre, g_post, g_memg, g_conv_b, g_b_a, g_b_x, g_lam, g_w_a, g_w_x, g_sinks, g_rel, g_conv_w)
    w_small = (pre_norm_g, post_norm_g, mem_norm_g, conv_b, b_rg_a, b_rg_x, lru_lambda, w_rg_a, w_rg_x, swa_sinks, rel_bias, conv_w)
    m_small = (m_pre_norm_g, m_post_norm_g, m_mem_norm_g, m_conv_b, m_b_rg_a, m_b_rg_x, m_lru_lambda, m_w_rg_a, m_w_rg_x, m_swa_sinks, m_rel_bias, m_conv_w)
    v_small = (v_pre_norm_g, v_post_norm_g, v_mem_norm_g, v_conv_b, v_b_rg_a, v_b_rg_x, v_lru_lambda, v_w_rg_a, v_w_rg_x, v_swa_sinks, v_rel_bias, v_conv_w)
    sm = _adamw_small(g_small, w_small, m_small, v_small)


    def leaves(k):
        s = sm[k]
        return [s[0], s[1], s[2], big[0][k], s[11], s[3], s[7], s[4], s[8], s[5], s[6], s[9], s[10],
                big[1][k], big[2][k], big[3][k], big[4][k], big[5][k]]

    return (loss_total, grad_x[None], *leaves(0), *leaves(1), *leaves(2), *leaves(3))
```

```python
import math

import jax
import jax.numpy as jnp
import numpy as np
from jax import lax
from jax.experimental import pallas as pl
from jax.experimental.pallas import tpu as pltpu

F32, BF = jnp.float32, jnp.bfloat16
MESH = pl.DeviceIdType.MESH
N_DEV = 8

D_MODEL = 2048
D_RNN = 1024
RNN_BLOCKS = 8
RNN_BLOCK = 128
CONV_W = 4
LRU_C = 8.0
SWA_HEADS = 16
SWA_KV_HEADS = 2
SWA_HD = 64
WINDOW = 128
MEM_HEADS = 4
MEM_HD = 256
D_MEM = 1024
REL_BUCKETS = 32
REL_MAX_DIST = 128
EPS = 1e-6
NEG_INF = -1e30
D_IN = 12544
SEGMENTS = (("xr", 0, 1024, F32), ("g_rg", 1024, 1024, F32), ("q_s", 2048, 1024, BF), ("kv", 3072, 256, BF),
            ("g_swa", 3328, 1024, F32), ("q_m", 4352, 1024, BF), ("g_mem", 5376, 1024, F32), ("gl", 6400, 6144, F32))
SEG_TILE = 256
D_IN_TILE = 7 * SEG_TILE

ADAM_LR, ADAM_B1, ADAM_B2, ADAM_EPS, ADAM_WD, ADAM_STEP = 0.001, 0.9, 0.999, 1e-08, 0.01, 10

NN = (((1,), (0,)), ((), ()))
NT = (((1,), (1,)), ((), ()))
TN = (((0,), (0,)), ((), ()))
MIB = 2 ** 20


def _dot(a, b, dn):
    return lax.dot_general(a, b, dn, preferred_element_type=F32)


VMEM_LIMIT_MIB = 48
VMEM_LIMIT_LARGE_MIB = 56


def _params(sem=None, large=False):
    return pltpu.CompilerParams(dimension_semantics=sem, vmem_limit_bytes=(VMEM_LIMIT_LARGE_MIB if large else VMEM_LIMIT_MIB) * MIB)


def _sigmoid(z):
    return 1.0 / (1.0 + jnp.exp(-z))


def _softplus(z):
    return jnp.maximum(z, 0.0) + jnp.log(1.0 + jnp.exp(-jnp.abs(z)))


def _expm1(z):
    p = z * (1.0 + z * (0.5 + z * (1.0 / 6 + z * (1.0 / 24 + z * (1.0 / 120 + z * (1.0 / 720 + z * (1.0 / 5040 + z / 40320)))))))
    return jnp.where(jnp.abs(z) < 0.3, p, jnp.exp(z) - 1.0)


def _flat(p):
    return 4 * p[0] + 2 * p[1] + p[2]


def _all_gather_relayed(arrs, relay, name, side=None):
    n = len(arrs)
    K = 9
    work, side_ins, side_outs, side_scratch = side if side is not None else (None, [], [], [])
    n_in, n_out = n + len(side_ins), n + len(side_outs)

    def body(*refs):
        ins, outs = refs[:n], refs[n_in:n_in + n]
        send_sems, recv_sems, local_sems = refs[n_in + n_out:n_in + n_out + 3]
        x, y, c = lax.axis_index("x"), lax.axis_index("y"), lax.axis_index("c")
        me, sib = (x, y, c), (x, y, 1 - c)
        xn, yn, dg = (1 - x, y, c), (x, 1 - y, c), (1 - x, 1 - y, c)

        def other(p):
            return (p[0], p[1], 1 - p[2])

        def rows(a, half):
            h = arrs[a].shape[0] // 2
            return pl.ds(half * h, h)

        def copy(a, k, block, to, half=None, src=None):
            dst = outs[a].at[_flat(block)]
            if half is not None:
                dst = dst.at[rows(a, half)]
            return pltpu.make_async_remote_copy(src_ref=dst if src is None else src, dst_ref=dst,
                                                send_sem=send_sems.at[a * K + k], recv_sem=recv_sems.at[a * K + k],
                                                device_id=to, device_id_type=MESH)

        mine = [pltpu.make_async_copy(ins[a], outs[a].at[_flat(me)], local_sems.at[a]) for a in range(n)]
        for cp in mine:
            cp.start()
        sends = []

        def start(cp):
            cp.start()
            sends.append(cp)

        for a in range(n):
            start(copy(a, 1, me, xn, src=ins[a]))
            start(copy(a, 2, me, yn, src=ins[a]))
            if not relay[a]:
                start(copy(a, 3, me, dg, src=ins[a]))
            start(copy(a, 0, me, sib, src=ins[a]))
        if work is not None:
            work(refs[n:n_in], refs[n_in + n:n_in + n_out], refs[n_in + n_out + 3:])
        for a in range(n):
            copy(a, 1, xn, me).wait_recv()
            if relay[a]:
                start(copy(a, 3, xn, yn, half=0))
            start(copy(a, 5, xn, sib))
        for a in range(n):
            copy(a, 2, yn, me).wait_recv()
            if relay[a]:
                start(copy(a, 4, yn, xn, half=1))
            start(copy(a, 6, yn, sib))
        for a in range(n):
            if relay[a]:
                copy(a, 3, dg, me, half=0).wait_recv()
                start(copy(a, 7, dg, sib, half=0))
                copy(a, 4, dg, me, half=1).wait_recv()
                start(copy(a, 8, dg, sib, half=1))
            else:
                copy(a, 3, dg, me).wait_recv()
                start(copy(a, 7, dg, sib))
        for a in range(n):
            copy(a, 0, sib, me).wait_recv()
            copy(a, 5, other(xn), me).wait_recv()
            copy(a, 6, other(yn), me).wait_recv()
            if relay[a]:
                copy(a, 7, other(dg), me, half=0).wait_recv()
                copy(a, 8, other(dg), me, half=1).wait_recv()
            else:
                copy(a, 7, other(dg), me).wait_recv()
        for cp in sends:
            cp.wait_send()
        for cp in mine:
            cp.wait()

    any_spec = pl.BlockSpec(memory_space=pl.ANY)
    return pl.pallas_call(
        body, name=name,
        out_shape=[jax.ShapeDtypeStruct((N_DEV,) + a.shape, a.dtype) for a in arrs] + list(side_outs),
        in_specs=[any_spec] * n_in, out_specs=[any_spec] * n_out,
        scratch_shapes=[pltpu.SemaphoreType.DMA((K * n,)), pltpu.SemaphoreType.DMA((K * n,)), pltpu.SemaphoreType.DMA((n,))]
        + list(side_scratch),
        compiler_params=_params(),
    )(*arrs, *side_ins)


def _chip_peers(x, y):
    return [(1 - x, y), (x, 1 - y), (1 - x, 1 - y)]


def _chip(p):
    return 2 * p[0] + p[1]


def _plan_gather(n):
    def plan(x, y, c):
        out = []
        for a in range(n):
            for peer in [(x, y, 1 - c)] + [(*ch, c) for ch in _chip_peers(x, y)]:
                out.append((a, None, ("lead", _flat((x, y, c))), peer, ("lead", _flat(peer))))
        return out
    return plan


def _plan_everyone(n):
    def plan(x, y, c):
        out = []
        for a in range(n):
            for r in range(1, N_DEV):
                peer = (1 - x if r & 4 else x, 1 - y if r & 2 else y, 1 - c if r & 1 else c)
                out.append((a, None, ("lead", _flat((x, y, c))), peer, ("lead", _flat(peer))))
        return out
    return plan


def _plan_swap(ndims):
    def plan(x, y, c):
        out = []
        for a, nd in enumerate(ndims):
            if nd == 4:
                out.append((a, 1 - c, ("all", 0), (x, y, 1 - c), ("all", 0)))
            else:
                out += [(a, 2 * j + 1 - c, ("lead", j), (x, y, 1 - c), ("lead", j)) for j in range(4)]
        return out
    return plan


def _slot(ref, where):
    kind, k = where
    return ref if kind == "all" else ref.at[k]


def _plan_scatter(n):
    def plan(x, y, c):
        out = []
        for a in range(n):
            for ch in _chip_peers(x, y):
                out.append((a, _chip(ch), ("lead", _chip((x, y))), (*ch, c), ("lead", _chip(ch))))
        return out
    return plan


HBM_SPEC = pl.BlockSpec(memory_space=pltpu.HBM)
SEM_SPEC = pl.BlockSpec(memory_space=pltpu.SEMAPHORE)


def _in_hbm(a):
    return pltpu.with_memory_space_constraint(a, pltpu.HBM)


def _exchange_start(srcs, lands, plan, name):
    n = len(lands)
    ns = 0 if srcs is None else n
    count = len(plan(0, 0, 0))

    def body(*refs):
        land_refs = refs[ns:ns + n]
        src_refs = land_refs if srcs is None else refs[:n]
        send_sems, recv_sems = refs[ns + n], refs[ns + n + 1]
        token = refs[-1]
        x, y, c = lax.axis_index("x"), lax.axis_index("y"), lax.axis_index("c")
        for k, (a, si, di, peer, _) in enumerate(plan(x, y, c)):
            src = src_refs[a] if si is None else src_refs[a].at[si]
            pltpu.make_async_remote_copy(src_ref=src, dst_ref=_slot(land_refs[a], di), send_sem=send_sems.at[k],
                                         recv_sem=recv_sems.at[k], device_id=peer, device_id_type=MESH).start()
        token[...] = jnp.zeros_like(token)

    out = pl.pallas_call(
        body, name=name,
        out_shape=(pltpu.SemaphoreType.DMA((count,)), pltpu.SemaphoreType.DMA((count,)),
                   *[pltpu.HBM(a.shape, a.dtype) for a in lands], jax.ShapeDtypeStruct((8, 128), F32)),
        in_specs=[HBM_SPEC] * (ns + n),
        out_specs=(SEM_SPEC, SEM_SPEC, *([HBM_SPEC] * n), pl.BlockSpec(memory_space=pltpu.VMEM)),
        input_output_aliases={ns + i: 2 + i for i in range(n)},
        compiler_params=pltpu.CompilerParams(has_side_effects=pltpu.SideEffectType.DATAFLOW_SIDE_EFFECTING),
    )(*[_in_hbm(a) for a in (srcs or [])], *[_in_hbm(a) for a in lands])
    return out[0], out[1], srcs if srcs is None else list(srcs), list(out[2:2 + n]), out[-1]


def _exchange_wait(send_sems, recv_sems, srcs, lands, plan, after, name):
    n = len(lands)
    ns = 0 if srcs is None else n

    def body(*refs):
        land_refs = refs[ns:ns + n]
        src_refs = land_refs if srcs is None else refs[:n]
        send_sems, recv_sems = refs[ns + n], refs[ns + n + 1]
        x, y, c = lax.axis_index("x"), lax.axis_index("y"), lax.axis_index("c")
        for k, (a, si, _, peer, ri) in enumerate(plan(x, y, c)):
            src = src_refs[a] if si is None else src_refs[a].at[si]
            cp = pltpu.make_async_remote_copy(src_ref=src, dst_ref=_slot(land_refs[a], ri), send_sem=send_sems.at[k],
                                              recv_sem=recv_sems.at[k], device_id=peer, device_id_type=MESH)
            cp.wait_send()
            cp.wait_recv()

    out = pl.pallas_call(
        body, name=name,
        out_shape=tuple(pltpu.HBM(a.shape, a.dtype) for a in lands),
        in_specs=[HBM_SPEC] * (ns + n) + [SEM_SPEC, SEM_SPEC, pl.BlockSpec(memory_space=pl.ANY)],
        out_specs=tuple([HBM_SPEC] * n),
        input_output_aliases={ns + i: i for i in range(n)},
        compiler_params=pltpu.CompilerParams(has_side_effects=pltpu.SideEffectType.DATAFLOW_SIDE_EFFECTING),
    )(*[_in_hbm(a) for a in (srcs or [])], *lands, send_sems, recv_sems, after)
    return list(out)


def _plan_forward(n):
    def plan(x, y, c):
        return [(a, _flat((*ch, c)), ("lead", _flat((*ch, c))), (x, y, 1 - c), ("lead", _flat((*ch, 1 - c))))
                for a in range(n) for ch in _chip_peers(x, y)]
    return plan


def _place_own(zones, owns, slot, name):
    n = len(zones)

    def body(slot_ref, *refs):
        for a in range(n):
            refs[2 * n + a][...] = refs[a][...]

    return pl.pallas_call(
        body, name=name,
        grid_spec=pltpu.PrefetchScalarGridSpec(
            num_scalar_prefetch=1, grid=(1,),
            in_specs=[pl.BlockSpec(o.shape, lambda i, s_ref: (0, 0)) for o in owns] + [pl.BlockSpec(memory_space=pl.ANY)] * n,
            out_specs=[pl.BlockSpec((None,) + o.shape, lambda i, s_ref: (s_ref[0], 0, 0)) for o in owns]),
        out_shape=[jax.ShapeDtypeStruct(z.shape, z.dtype) for z in zones],
        input_output_aliases={1 + n + a: a for a in range(n)},
        compiler_params=_params(("arbitrary",)),
    )(slot, *owns, *zones)


def _pair_sum(parts, got, core, name):
    R, C = parts.shape[-2:]
    mine = (pl.BlockSpec((None, None, R, C), lambda j, c_ref: (c_ref[0], j, 0, 0)) if parts.ndim == 4
            else pl.BlockSpec((None, R, C), lambda j, c_ref: (2 * j + c_ref[0], 0, 0)))

    def body(c_ref, p_ref, g_ref, o_ref):
        o_ref[...] = (p_ref[...].astype(F32) + g_ref[...].astype(F32)).astype(o_ref.dtype)

    return pl.pallas_call(
        body, name=name,
        grid_spec=pltpu.PrefetchScalarGridSpec(
            num_scalar_prefetch=1, grid=(4,),
            in_specs=[mine, pl.BlockSpec((None, R, C), lambda j, c_ref: (j, 0, 0))],
            out_specs=pl.BlockSpec((None, R, C), lambda j, c_ref: (j, 0, 0))),
        out_shape=pltpu.HBM((4, R, C), parts.dtype),
        compiler_params=_params(("parallel",)),
    )(core, parts, got)


def _matmul(a, b, mode, M, N, K, tm, tn, tk, out_dtype, name, b_noff=0, out_blocked=None, dep=None, out_hbm=False):
    nm, nn, nk = M // tm, N // tn, K // tk
    if mode == "nn":
        a_spec = pl.BlockSpec((tm, tk), lambda j, i, k: (i, k))
        b_spec = pl.BlockSpec((tk, tn), lambda j, i, k: (k, j + b_noff))
        dn = NN
    elif mode == "nt":
        a_spec = pl.BlockSpec((tm, tk), lambda j, i, k: (i, k))
        b_spec = pl.BlockSpec((tn, tk), lambda j, i, k: (j + b_noff, k))
        dn = NT
    else:
        a_spec = pl.BlockSpec((tk, tm), lambda j, i, k: (k, i))
        b_spec = pl.BlockSpec((tk, tn), lambda j, i, k: (k, j + b_noff))
        dn = TN
    if out_blocked == "col":
        out_shape = jax.ShapeDtypeStruct((2, 4, M, tn), out_dtype)
        out_spec = pl.BlockSpec((None, None, tm, tn), lambda j, i, k: (j % 2, j // 2, i, 0))
    elif out_blocked == "row":
        out_shape = jax.ShapeDtypeStruct((2, 4, tm, N), out_dtype)
        out_spec = pl.BlockSpec((None, None, tm, tn), lambda j, i, k: (i % 2, i // 2, 0, j))
    else:
        out_shape = jax.ShapeDtypeStruct((M, N), out_dtype)
        out_spec = pl.BlockSpec((tm, tn), lambda j, i, k: (i, j))
    if out_hbm or out_blocked is not None:
        out_shape = pltpu.HBM(out_shape.shape, out_shape.dtype)

    n_extra = int(dep is not None)

    def body(a_ref, b_ref, *rest):
        o_ref, scratch = rest[n_extra], rest[n_extra + 1:]
        if nk == 1:
            o_ref[...] = _dot(a_ref[...], b_ref[...], dn).astype(out_dtype)
        else:
            acc_ref, = scratch
            k = pl.program_id(2)

            @pl.when(k == 0)
            def _():
                acc_ref[...] = jnp.zeros_like(acc_ref)

            acc_ref[...] += _dot(a_ref[...], b_ref[...], dn)

            @pl.when(k == nk - 1)
            def _():
                o_ref[...] = acc_ref[...].astype(out_dtype)

    return pl.pallas_call(
        body, name=name, grid=(nn, nm, nk),
        in_specs=[a_spec, b_spec] + ([] if dep is None else [pl.BlockSpec((8, 128), lambda j, i, k: (0, 0))]),
        out_specs=out_spec, out_shape=out_shape,
        scratch_shapes=[] if nk == 1 else [pltpu.VMEM((tm, tn), F32)],
        compiler_params=_params(("parallel", "parallel", "arbitrary")),
    )(a, b, *([] if dep is None else [dep]))


RMS_SIDE_ROWS = 512


def _rms_side(pairs):
    chunks = [min(x.shape[0], RMS_SIDE_ROWS) for x, _ in pairs]

    def work(ins, outs, scratch):
        sem = scratch[-1]

        def move(src, dst):
            cp = pltpu.make_async_copy(src, dst, sem.at[0])
            cp.start()
            cp.wait()

        for p, ((x, _), tr) in enumerate(zip(pairs, chunks)):
            x_ref, g_ref, h_ref = ins[2 * p], ins[2 * p + 1], outs[p]
            xv, gv, hv = scratch[3 * p:3 * p + 3]
            move(g_ref, gv)
            for i in range(x.shape[0] // tr):
                rows = pl.ds(i * tr, tr)
                move(x_ref.at[rows], xv)
                v = xv[...]
                hv[...] = (v * lax.rsqrt(jnp.mean(v * v, axis=-1, keepdims=True) + EPS) * gv[...]).astype(BF)
                move(hv, h_ref.at[rows])

    scratch = [s for (x, g), tr in zip(pairs, chunks)
               for s in (pltpu.VMEM((tr, x.shape[1]), F32), pltpu.VMEM(g.shape, F32), pltpu.VMEM((tr, x.shape[1]), BF))]
    return (work, [a for pair in pairs for a in pair], [jax.ShapeDtypeStruct(x.shape, BF) for x, _ in pairs],
            scratch + [pltpu.SemaphoreType.DMA((1,))])


def _rms_gain_grad(dn, x, name):
    R, Dm = x.shape

    def body(dn_ref, x_ref, o_ref):
        xv = x_ref[...]
        r = lax.rsqrt(jnp.mean(xv * xv, axis=-1, keepdims=True) + EPS)
        o_ref[...] = jnp.sum(dn_ref[...] * xv * r, axis=0, keepdims=True)

    return pl.pallas_call(
        body, name=name, out_shape=jax.ShapeDtypeStruct((1, Dm), F32),
        compiler_params=_params(),
    )(dn, x)


def _shift_down(v, k, head8, row, T):
    if k == 0:
        return v
    r = pltpu.roll(v, k, 0)
    hr = pltpu.roll(head8, k, 0)
    top = jnp.where(row[:8] < k, hr, r[:8])
    return jnp.concatenate([top, r[8:]], axis=0)


def _shift_up(v, k, tail8, row, T):
    if k == 0:
        return v
    r = pltpu.roll(v, T - k, 0)
    tr = pltpu.roll(tail8, 8 - k, 0)
    bot = jnp.where(row[:8] >= 8 - k, tr, r[T - 8:])
    return jnp.concatenate([r[:T - 8], bot], axis=0)


def _rglru_gates(u, head8, grow, row, T, cw_ref, cb_ref, wa_ref, ba_ref, wx_ref, bx_ref, lam_ref):
    us = [_shift_down(u, k, head8, row, T) for k in range(CONV_W)]
    acc = us[0] * cw_ref[0:1, :]
    for k in range(1, CONV_W):
        acc = acc + us[k] * cw_ref[k:k + 1, :]
    conv = cb_ref[...] + acc
    cbf = conv.astype(BF)
    r_ = _sigmoid(_dot(cbf, wa_ref[0], NN) + ba_ref[...])
    i_ = _sigmoid(_dot(cbf, wx_ref[0], NN) + bx_ref[...])
    sp = _softplus(-lam_ref[...])
    la = -LRU_C * r_ * sp
    a = jnp.exp(la)
    mult_raw = jnp.sqrt(-_expm1(2.0 * la))
    mult = jnp.where(grow == 0, 1.0, mult_raw)
    return us, conv, cbf, r_, i_, sp, a, mult_raw, mult


def _rglru_specs(T, nt, rev):
    tmap = (lambda n, t: (nt - 1 - t, n)) if rev else (lambda n, t: (t, n))
    hmap = ((lambda n, t: (jnp.maximum((nt - 1 - t) * (T // 8) - 1, 0), n)) if rev
            else (lambda n, t: (jnp.maximum(t * (T // 8) - 1, 0), n)))
    tile = pl.BlockSpec((T, RNN_BLOCK), tmap)
    halo = pl.BlockSpec((8, RNN_BLOCK), hmap)
    vec = pl.BlockSpec((1, RNN_BLOCK), lambda n, t: (0, n))
    cw = pl.BlockSpec((CONV_W, RNN_BLOCK), lambda n, t: (0, n))
    wblk = pl.BlockSpec((1, RNN_BLOCK, RNN_BLOCK), lambda n, t: (n, 0, 0))
    return tile, halo, vec, cw, wblk


def _rglru_fwd(xr, g, cw, cb, wa, ba, wx, bx, lam, T):
    S = xr.shape[0]
    nt = S // T

    def body(u_ref, uh_ref, g_ref, cw_ref, cb_ref, wa_ref, ba_ref, wx_ref, bx_ref, lam_ref, h_ref, y_ref, carry):
        t = pl.program_id(1)

        @pl.when(t == 0)
        def _():
            carry[...] = jnp.zeros_like(carry)

        row = lax.broadcasted_iota(jnp.int32, (T, RNN_BLOCK), 0)
        grow = row + t * T
        head8 = jnp.where(t > 0, uh_ref[...], 0.0)
        _, conv, _, _, i_, _, a, _, mult = _rglru_gates(u_ref[...], head8, grow, row, T, cw_ref, cb_ref, wa_ref, ba_ref,
                                                         wx_ref, bx_ref, lam_ref)
        b = mult * i_ * conv
        s = 1
        while s < T:
            keep = row >= s
            a_s = jnp.where(keep, pltpu.roll(a, s, 0), 1.0)
            b_s = jnp.where(keep, pltpu.roll(b, s, 0), 0.0)
            b = a * b_s + b
            a = a * a_s
            s *= 2
        h = b + a * carry[0:1, :]
        carry[...] = jnp.broadcast_to(h[T - 1:T, :], carry.shape)
        h_ref[...] = h
        gv = g_ref[...]
        y_ref[...] = (h * (gv * _sigmoid(gv))).astype(BF)

    tile, halo, vec, cwspec, wblk = _rglru_specs(T, nt, False)
    return pl.pallas_call(
        body, name="rglru_fwd", grid=(RNN_BLOCKS, nt),
        in_specs=[tile, halo, tile, cwspec, vec, wblk, vec, wblk, vec, vec],
        out_specs=[tile, tile],
        out_shape=[jax.ShapeDtypeStruct((S, D_RNN), F32), jax.ShapeDtypeStruct((S, D_RNN), BF)],
        scratch_shapes=[pltpu.VMEM((8, RNN_BLOCK), F32)],
        compiler_params=_params(("parallel", "arbitrary")),
    )(xr, xr, g, cw, cb, wa, ba, wx, bx, lam)


def _rglru_bwd(xr, g, h, dy, cw, cb, wa, ba, wx, bx, lam, T):
    S = xr.shape[0]
    nt = S // T

    def body(u_ref, uh_ref, g_ref, h_ref, hh_ref, dy_ref, cw_ref, cb_ref, wa_ref, ba_ref, wx_ref, bx_ref, lam_ref,
             du_ref, dg_ref, dwa_ref, dwx_ref, dvec_ref, c_dhh, c_a, c_dconv):
        t = pl.program_id(1)
        tt = nt - 1 - t

        @pl.when(t == 0)
        def _():
            c_dhh[...] = jnp.zeros_like(c_dhh)
            c_a[...] = jnp.zeros_like(c_a)
            c_dconv[...] = jnp.zeros_like(c_dconv)
            dwa_ref[...] = jnp.zeros_like(dwa_ref)
            dwx_ref[...] = jnp.zeros_like(dwx_ref)
            dvec_ref[...] = jnp.zeros_like(dvec_ref)

        row = lax.broadcasted_iota(jnp.int32, (T, RNN_BLOCK), 0)
        row8 = row[:8]
        grow = row + tt * T
        head8 = jnp.where(tt > 0, uh_ref[...], 0.0)
        us, conv, cbf, r_, i_, sp, a, mult_raw, mult = _rglru_gates(
            u_ref[...], head8, grow, row, T, cw_ref, cb_ref, wa_ref, ba_ref, wx_ref, bx_ref, lam_ref)
        hv = h_ref[...]
        hprev = _shift_down(hv, 1, jnp.where(tt > 0, hh_ref[...], 0.0), row, T)
        gv = g_ref[...]
        sg = _sigmoid(gv)
        dyv = dy_ref[...]
        dg_ref[...] = (dyv * hv * (sg * (1.0 + gv * (1.0 - sg)))).astype(BF)
        d = dyv * (gv * sg)
        A = _shift_up(a, 1, c_a[...], row, T)
        s = 1
        while s < T:
            keep = row < T - s
            A_s = jnp.where(keep, pltpu.roll(A, T - s, 0), 1.0)
            d_s = jnp.where(keep, pltpu.roll(d, T - s, 0), 0.0)
            d = A * d_s + d
            A = A * A_s
            s *= 2
        dhh = d + A * c_dhh[0:1, :]
        da = dhh * hprev
        dconv = dhh * mult * i_
        di = dhh * mult * conv
        dmult = dhh * i_ * conv
        dla = da * a - jnp.where(grow == 0, 0.0, dmult * (a * a) / mult_raw)
        dr = dla * (-LRU_C * sp)
        dsp = jnp.sum(dla * (-LRU_C * r_), axis=0, keepdims=True)
        dza = dr * r_ * (1.0 - r_)
        dzx = di * i_ * (1.0 - i_)
        dza_b, dzx_b = dza.astype(BF), dzx.astype(BF)
        dconv = dconv + _dot(dza_b, wa_ref[0], NT) + _dot(dzx_b, wx_ref[0], NT)
        dwa_ref[0] += _dot(cbf, dza_b, TN)
        dwx_ref[0] += _dot(cbf, dzx_b, TN)
        lam = lam_ref[...]
        rows = [jnp.sum(dconv * us[k], axis=0, keepdims=True) for k in range(CONV_W)]
        rows += [jnp.sum(dconv, axis=0, keepdims=True), jnp.sum(dza, axis=0, keepdims=True),
                 jnp.sum(dzx, axis=0, keepdims=True), dsp * (-_sigmoid(-lam))]
        upd = jnp.zeros((8, RNN_BLOCK), F32)
        for j, rv in enumerate(rows):
            upd = upd + jnp.where(row8 == j, rv, 0.0)
        dvec_ref[...] += upd
        tail8 = c_dconv[...]
        du = dconv * cw_ref[0:1, :]
        for k in range(1, CONV_W):
            du = du + _shift_up(dconv, k, tail8, row, T) * cw_ref[k:k + 1, :]
        du_ref[...] = du.astype(BF)
        c_dhh[...] = jnp.broadcast_to(dhh[0:1, :], c_dhh.shape)
        c_a[...] = jnp.broadcast_to(a[0:1, :], c_a.shape)
        c_dconv[...] = dconv[:8]

    tile, halo, vec, cwspec, wblk = _rglru_specs(T, nt, True)
    acc8 = pl.BlockSpec((8, RNN_BLOCK), lambda n, t: (0, n))
    return pl.pallas_call(
        body, name="rglru_bwd", grid=(RNN_BLOCKS, nt),
        in_specs=[tile, halo, tile, tile, halo, tile, cwspec, vec, wblk, vec, wblk, vec, vec],
        out_specs=[tile, tile, wblk, wblk, acc8],
        out_shape=[jax.ShapeDtypeStruct((S, D_RNN), BF), jax.ShapeDtypeStruct((S, D_RNN), BF),
                   jax.ShapeDtypeStruct((RNN_BLOCKS, RNN_BLOCK, RNN_BLOCK), F32),
                   jax.ShapeDtypeStruct((RNN_BLOCKS, RNN_BLOCK, RNN_BLOCK), F32),
                   jax.ShapeDtypeStruct((8, D_RNN), F32)],
        scratch_shapes=[pltpu.VMEM((8, RNN_BLOCK), F32)] * 3,
        compiler_params=_params(("parallel", "arbitrary")),
    )(xr, xr, g, h, h, dy, cw, cb, wa, ba, wx, bx, lam)


def _rel_bucket_map():
    qi = np.arange(WINDOW)[:, None]
    kj = np.arange(2 * WINDOW)[None, :]
    dist = jnp.asarray(qi + WINDOW - kj, jnp.int32)
    n = jnp.maximum(dist, 0)
    max_exact = REL_BUCKETS // 2
    ratio = jnp.log(jnp.maximum(n, 1).astype(F32) / max_exact) / math.log(REL_MAX_DIST / max_exact)
    large = jnp.minimum(max_exact + (ratio * (REL_BUCKETS - max_exact)).astype(jnp.int32), REL_BUCKETS - 1)
    bucket = jnp.where(n < max_exact, n, large).astype(jnp.int32)
    j = np.arange(WINDOW)[None, :]
    return jnp.where(jnp.asarray(j > qi), bucket[:, :WINDOW], bucket[:, WINDOW:])


def _swa_common(n, kv_ref, bucket_ref, relb_ref, bias_scr):
    @pl.when(n == 0)
    def _():
        bk = bucket_ref[...]
        for h in range(SWA_HEADS):
            acc = jnp.zeros((WINDOW, WINDOW), F32)
            for b in range(REL_BUCKETS):
                acc = acc + jnp.where(bk == b, relb_ref[b, h], 0.0)
            bias_scr[h] = acc

    prev0 = pl.multiple_of(jnp.maximum(n - 1, 0) * WINDOW, WINDOW)
    cur0 = pl.multiple_of(n * WINDOW, WINDOW)
    kk = jnp.concatenate([kv_ref[pl.ds(prev0, WINDOW), :], kv_ref[pl.ds(cur0, WINDOW), :]], axis=0).astype(F32)
    rowi = lax.broadcasted_iota(jnp.int32, (WINDOW, WINDOW), 0)
    col = lax.broadcasted_iota(jnp.int32, (WINDOW, WINDOW), 1)
    from_prev = col > rowi
    return kk, from_prev, prev0, cur0


def _fold(full, from_prev):
    return jnp.where(from_prev, full[:, :WINDOW], full[:, WINDOW:])


def _unfold(sq, from_prev):
    return jnp.concatenate([jnp.where(from_prev, sq, 0.0), jnp.where(from_prev, 0.0, sq)], axis=1)


def _half_pair(part, kvh):
    lo = lax.broadcasted_iota(jnp.int32, part.shape, 1) < SWA_HD
    if kvh == 0:
        pa = jnp.where(lo, part, 0.0)
        pb = pltpu.roll(pa, SWA_HD, 1)
    else:
        pb = jnp.where(lo, 0.0, part)
        pa = pltpu.roll(pb, SWA_HD, 1)
    return pa.astype(BF), pb.astype(BF)


ALL_HEADS = SWA_HEADS * WINDOW


def _sink_column(sinks):
    return jnp.repeat(sinks.reshape(SWA_HEADS), WINDOW).reshape(ALL_HEADS, 1)


def _swa_operands(kk):
    return [(_half_pair(kk[:, :128], kvh), _half_pair(kk[:, 128:], kvh)) for kvh in range(SWA_KV_HEADS)]


def _swa_probs(n, q_ref, ops, bias_scr, sinkc_ref, from_prev):
    lgs = []
    for kvh in range(SWA_KV_HEADS):
        (ka, kb), _ = ops[kvh]
        for p in range(4):
            q2 = q_ref[:, kvh * 512 + p * 128:kvh * 512 + p * 128 + 128]
            lgs += [_fold(_dot(q2, ka, NT), from_prev), _fold(_dot(q2, kb, NT), from_prev)]
    lg = jnp.concatenate(lgs, axis=0) * (SWA_HD ** -0.5) + bias_scr[...].reshape(ALL_HEADS, WINDOW)
    rowi = jnp.bitwise_and(lax.broadcasted_iota(jnp.int32, (ALL_HEADS, WINDOW), 0), WINDOW - 1)
    col = lax.broadcasted_iota(jnp.int32, (ALL_HEADS, WINDOW), 1)
    no_prev = jnp.where(n > 0, 0, 4 * WINDOW)
    lg = jnp.where(jnp.logical_or(col <= rowi, col > rowi + no_prev), lg, NEG_INF)
    sink = sinkc_ref[...]
    m = jnp.maximum(jnp.max(lg, axis=-1, keepdims=True), sink)
    e = jnp.exp(lg - m)
    es = jnp.exp(sink - m)
    den = jnp.sum(e, axis=-1, keepdims=True) + es
    return e / den, es / den


def _swa_fwd(q, kv, g, bucket, rel_bias, sink_col):
    S = q.shape[0]
    nb = S // WINDOW

    def body(q_ref, kv_ref, g_ref, bucket_ref, relb_ref, sinkc_ref, o_ref, y_ref, bias_scr):
        n = pl.program_id(0)
        kk, from_prev, _, _ = _swa_common(n, kv_ref, bucket_ref, relb_ref, bias_scr)
        ops = _swa_operands(kk)
        pr, _ = _swa_probs(n, q_ref, ops, bias_scr, sinkc_ref, from_prev)
        for kvh in range(SWA_KV_HEADS):
            _, (va, vb) = ops[kvh]
            for p in range(4):
                c0 = kvh * 512 + p * 128
                r0 = (kvh * 8 + 2 * p) * WINDOW
                o2 = (_dot(_unfold(pr[r0:r0 + WINDOW], from_prev).astype(BF), va, NN)
                      + _dot(_unfold(pr[r0 + WINDOW:r0 + 2 * WINDOW], from_prev).astype(BF), vb, NN))
                o_ref[:, c0:c0 + 128] = o2
                gv = g_ref[:, c0:c0 + 128]
                y_ref[:, c0:c0 + 128] = (o2 * (gv * _sigmoid(gv))).astype(BF)

    blk = pl.BlockSpec((WINDOW, 1024), lambda n: (n, 0))
    smem = pl.BlockSpec(memory_space=pltpu.SMEM)
    sinkc = pl.BlockSpec((ALL_HEADS, 1), lambda n: (0, 0))
    return pl.pallas_call(
        body, name="swa_fwd", grid=(nb,),
        in_specs=[blk, pl.BlockSpec((S, 256), lambda n: (0, 0)), blk, pl.BlockSpec((WINDOW, WINDOW), lambda n: (0, 0)), smem, sinkc],
        out_specs=[blk, blk],
        out_shape=[jax.ShapeDtypeStruct((S, 1024), F32), jax.ShapeDtypeStruct((S, 1024), BF)],
        scratch_shapes=[pltpu.VMEM((SWA_HEADS, WINDOW, WINDOW), F32)],
        compiler_params=_params(("arbitrary",)),
    )(q, kv, g, bucket, rel_bias, sink_col)


def _swa_bwd(q, kv, g, o, dy, bucket, rel_bias, sink_col, others):
    S = q.shape[0]
    nb = S // WINDOW
    first_col = {name: c0 for name, c0, _, _ in SEGMENTS}
    col_q, col_g = first_col["q_s"], first_col["g_swa"]
    copies = []
    for name, c0, width, _ in SEGMENTS:
        if name not in ("q_s", "kv", "g_swa"):
            arrs = others[name] if name == "gl" else (others[name],)
            copies += [(a, c0 + i * (width // len(arrs))) for i, a in enumerate(arrs)]

    def body(q_ref, kv_ref, g_ref, o_ref, dy_ref, bucket_ref, relb_ref, sinkc_ref, *refs):
        copy_refs = refs[:len(copies)]
        dp_ref, dkv_ref, dsink_ref, drel_ref, bias_scr, dbias_scr, dsink_scr = refs[len(copies):]
        n = pl.program_id(0)
        for c_ref, (a, c0) in zip(copy_refs, copies):
            dp_ref[:, c0:c0 + a.shape[1]] = c_ref[...]

        @pl.when(n == 0)
        def _():
            dbias_scr[...] = jnp.zeros_like(dbias_scr)
            dsink_scr[...] = jnp.zeros_like(dsink_scr)
            dkv_ref[...] = jnp.zeros_like(dkv_ref)

        kk, from_prev, prev0, cur0 = _swa_common(n, kv_ref, bucket_ref, relb_ref, bias_scr)
        ops = _swa_operands(kk)
        pr, ps = _swa_probs(n, q_ref, ops, bias_scr, sinkc_ref, from_prev)
        do2s, dps = [], []
        for kvh in range(SWA_KV_HEADS):
            _, (va, vb) = ops[kvh]
            for p in range(4):
                c0 = kvh * 512 + p * 128
                gv = g_ref[:, c0:c0 + 128]
                sg = _sigmoid(gv)
                dyv = dy_ref[:, c0:c0 + 128]
                dp_ref[:, col_g + c0:col_g + c0 + 128] = (dyv * o_ref[:, c0:c0 + 128] * (sg * (1.0 + gv * (1.0 - sg)))).astype(BF)
                do2 = (dyv * (gv * sg)).astype(BF)
                do2s.append(do2)
                dps += [_fold(_dot(do2, va, NT), from_prev), _fold(_dot(do2, vb, NT), from_prev)]
        dp = jnp.concatenate(dps, axis=0)
        delta = jnp.sum(pr * dp, axis=-1, keepdims=True)
        ds = pr * (dp - delta)
        dbias_scr[...] += ds.reshape(SWA_HEADS, WINDOW, WINDOW)
        dsink_scr[...] += ps * delta
        dsc = ds * (SWA_HD ** -0.5)
        lo256 = lax.broadcasted_iota(jnp.int32, (2 * WINDOW, 128), 1) < SWA_HD
        dks, dvs = [], []
        for kvh in range(SWA_KV_HEADS):
            (ka, kb), _ = ops[kvh]
            dka = jnp.zeros((2 * WINDOW, 128), F32)
            dkb, dva, dvb = dka, dka, dka
            for p in range(4):
                c0 = kvh * 512 + p * 128
                r0 = (kvh * 8 + 2 * p) * WINDOW
                q2 = q_ref[:, c0:c0 + 128]
                do2 = do2s[kvh * 4 + p]
                ds0 = _unfold(dsc[r0:r0 + WINDOW], from_prev).astype(BF)
                ds1 = _unfold(dsc[r0 + WINDOW:r0 + 2 * WINDOW], from_prev).astype(BF)
                dp_ref[:, col_q + c0:col_q + c0 + 128] = (_dot(ds0, ka, NN) + _dot(ds1, kb, NN)).astype(BF)
                dka = dka + _dot(ds0, q2, TN)
                dkb = dkb + _dot(ds1, q2, TN)
                dva = dva + _dot(_unfold(pr[r0:r0 + WINDOW], from_prev).astype(BF), do2, TN)
                dvb = dvb + _dot(_unfold(pr[r0 + WINDOW:r0 + 2 * WINDOW], from_prev).astype(BF), do2, TN)
            dks.append(jnp.where(lo256, dka, 0.0) + pltpu.roll(jnp.where(lo256, 0.0, dkb), SWA_HD, 1))
            dvs.append(jnp.where(lo256, dva, 0.0) + pltpu.roll(jnp.where(lo256, 0.0, dvb), SWA_HD, 1))
        dk = dks[0] + pltpu.roll(dks[1], SWA_HD, 1)
        dv = dvs[0] + pltpu.roll(dvs[1], SWA_HD, 1)
        dkv_ref[pl.ds(prev0, WINDOW), 0:128] += dk[:WINDOW]
        dkv_ref[pl.ds(prev0, WINDOW), 128:256] += dv[:WINDOW]
        dkv_ref[pl.ds(cur0, WINDOW), 0:128] += dk[WINDOW:]
        dkv_ref[pl.ds(cur0, WINDOW), 128:256] += dv[WINDOW:]

        @pl.when(n == nb - 1)
        def _():
            dsink_ref[...] = -jnp.sum(dsink_scr[...].reshape(SWA_HEADS, WINDOW, 1), axis=1)
            bk = bucket_ref[...]
            sums = []
            for b in range(REL_BUCKETS):
                sums.append(jnp.sum(jnp.where((bk == b)[None], dbias_scr[...], 0.0), axis=1))
            drel_ref[...] = jnp.sum(jnp.concatenate(sums, axis=0), axis=1, keepdims=True)

    blk = pl.BlockSpec((WINDOW, 1024), lambda n: (n, 0))
    smem = pl.BlockSpec(memory_space=pltpu.SMEM)
    whole = lambda shape: pl.BlockSpec(shape, lambda n: (0, 0))
    return pl.pallas_call(
        body, name="swa_bwd", grid=(nb,),
        in_specs=[blk, whole((S, 256)), blk, blk, blk, whole((WINDOW, WINDOW)), smem, whole((ALL_HEADS, 1))]
        + [pl.BlockSpec((WINDOW, a.shape[1]), lambda n: (n, 0)) for a, _ in copies],
        out_specs=[pl.BlockSpec((WINDOW, D_IN), lambda n: (n, 0)), whole((S, 256)), whole((SWA_HEADS, 1)),
                   whole((REL_BUCKETS * SWA_HEADS, 1))],
        out_shape=[jax.ShapeDtypeStruct((S, D_IN), BF), jax.ShapeDtypeStruct((S, 256), F32), jax.ShapeDtypeStruct((SWA_HEADS, 1), F32),
                   jax.ShapeDtypeStruct((REL_BUCKETS * SWA_HEADS, 1), F32)],
        scratch_shapes=[pltpu.VMEM((SWA_HEADS, WINDOW, WINDOW), F32), pltpu.VMEM((SWA_HEADS, WINDOW, WINDOW), F32),
                        pltpu.VMEM((ALL_HEADS, 1), F32)],
        compiler_params=_params(("arbitrary",)),
    )(q, kv, g, o, dy, bucket, rel_bias, sink_col, *[a for a, _ in copies])


MEM_TQ = 512


def _mem_probs(qh, mk):
    lg = _dot(qh, mk, NT) * (MEM_HD ** -0.5)
    e = jnp.exp(lg - jnp.max(lg, axis=-1, keepdims=True))
    return e / jnp.sum(e, axis=-1, keepdims=True)


def _mem_fwd(q, mkv, g):
    S = q.shape[0]
    M = mkv.shape[0]
    tq = min(S, MEM_TQ)

    def body(q_ref, mkv_ref, g_ref, o_ref, y_ref):
        for h in range(MEM_HEADS):
            c0 = h * MEM_HD
            pr = _mem_probs(q_ref[:, c0:c0 + MEM_HD], mkv_ref[:, c0:c0 + MEM_HD])
            o = _dot(pr.astype(BF), mkv_ref[:, D_MEM + c0:D_MEM + c0 + MEM_HD], NN)
            o_ref[:, c0:c0 + MEM_HD] = o
            gv = g_ref[:, c0:c0 + MEM_HD]
            y_ref[:, c0:c0 + MEM_HD] = (o * (gv * _sigmoid(gv))).astype(BF)

    blk = pl.BlockSpec((tq, D_MEM), lambda i: (i, 0))
    return pl.pallas_call(
        body, name="mem_fwd", grid=(S // tq,),
        in_specs=[blk, pl.BlockSpec((M, 2 * D_MEM), lambda i: (0, 0)), blk], out_specs=[blk, blk],
        out_shape=[jax.ShapeDtypeStruct((S, D_MEM), F32), jax.ShapeDtypeStruct((S, D_MEM), BF)],
        compiler_params=_params(("parallel",)),
    )(q, mkv, g)


def _mem_bwd(q, mkv, g, o, dy):
    S = q.shape[0]
    M = mkv.shape[0]
    tq = min(S, MEM_TQ)

    def body(q_ref, mkv_ref, g_ref, o_ref, dy_ref, dq_ref, dg_ref, dmkv_ref):
        @pl.when(pl.program_id(0) == 0)
        def _():
            dmkv_ref[...] = jnp.zeros_like(dmkv_ref)

        for h in range(MEM_HEADS):
            c0 = h * MEM_HD
            qh = q_ref[:, c0:c0 + MEM_HD]
            mk = mkv_ref[:, c0:c0 + MEM_HD]
            mv = mkv_ref[:, D_MEM + c0:D_MEM + c0 + MEM_HD]
            gv = g_ref[:, c0:c0 + MEM_HD]
            sg = _sigmoid(gv)
            dyv = dy_ref[:, c0:c0 + MEM_HD]
            dg_ref[:, c0:c0 + MEM_HD] = (dyv * o_ref[:, c0:c0 + MEM_HD] * (sg * (1.0 + gv * (1.0 - sg)))).astype(BF)
            do = (dyv * (gv * sg)).astype(BF)
            pr = _mem_probs(qh, mk)
            dp = _dot(do, mv, NT)
            ds = pr * (dp - jnp.sum(pr * dp, axis=-1, keepdims=True))
            dsb = (ds * (MEM_HD ** -0.5)).astype(BF)
            dq_ref[:, c0:c0 + MEM_HD] = _dot(dsb, mk, NN).astype(BF)
            dmkv_ref[:, c0:c0 + MEM_HD] += _dot(dsb, qh, TN)
            dmkv_ref[:, D_MEM + c0:D_MEM + c0 + MEM_HD] += _dot(pr.astype(BF), do, TN)

    blk = pl.BlockSpec((tq, D_MEM), lambda i: (i, 0))
    whole = pl.BlockSpec((M, 2 * D_MEM), lambda i: (0, 0))
    return pl.pallas_call(
        body, name="mem_bwd", grid=(S // tq,),
        in_specs=[blk, whole, blk, blk, blk], out_specs=[blk, blk, whole],
        out_shape=[jax.ShapeDtypeStruct((S, D_MEM), BF), jax.ShapeDtypeStruct((S, D_MEM), BF),
                   jax.ShapeDtypeStruct((M, 2 * D_MEM), F32)],
        compiler_params=_params(("arbitrary",)),
    )(q, mkv, g, o, dy)


MERGE_TN = 512
MERGE_TM = 1024
MERGE_BWD_TN = 256


def _merge_specs(tm, tn):
    ytile = pl.BlockSpec((tm, 1024), lambda i, j: (i, 0))
    wblk = pl.BlockSpec((tn, 1024), lambda i, j: (j, 0))
    gls = [pl.BlockSpec((None, tm, tn), (lambda i, j, br=br: (br, i, j))) for br in range(3)]
    otile = pl.BlockSpec((tm, tn), lambda i, j: (i, j))
    return ytile, wblk, gls, otile


def _merge_fwd(ys, ws, gl, tm):
    S = gl.shape[1]

    def body(y0, y1, y2, w0, w1, w2, g0, g1, g2, o_ref):
        acc = None
        for y_ref, w_ref, g_ref in ((y0, w0, g0), (y1, w1, g1), (y2, w2, g2)):
            term = _sigmoid(g_ref[...]) * _dot(y_ref[...], w_ref[...], NT)
            acc = term if acc is None else acc + term
        o_ref[...] = acc.astype(BF)

    ytile, wblk, gls, otile = _merge_specs(tm, MERGE_TN)
    return pl.pallas_call(
        body, name="merge_fwd", grid=(S // tm, D_MODEL // MERGE_TN),
        in_specs=[ytile] * 3 + [wblk] * 3 + gls, out_specs=otile,
        out_shape=jax.ShapeDtypeStruct((S, D_MODEL), BF),
        compiler_params=_params(("parallel", "arbitrary")),
    )(*ys, *ws, gl, gl, gl)


def _merge_bwd(dout, w_out, ys, ws, gl, tm):
    S = gl.shape[1]

    def body(do_ref, wo_ref, y0, y1, y2, w0, w1, w2, g0, g1, g2, dg0, dg1, dg2, dp0, dp1, dp2):
        dm = _dot(do_ref[...], wo_ref[...], NT)
        for y_ref, w_ref, g_ref, dg_ref, dp_ref in ((y0, w0, g0, dg0, dp0), (y1, w1, g1, dg1, dp1), (y2, w2, g2, dg2, dp2)):
            gate = _sigmoid(g_ref[...])
            pv = _dot(y_ref[...], w_ref[...], NT)
            dg_ref[...] = (dm * pv * gate * (1.0 - gate)).astype(BF)
            dp_ref[...] = (dm * gate).astype(BF)

    ytile, wblk, gls, otile = _merge_specs(tm, MERGE_BWD_TN)
    out = jax.ShapeDtypeStruct((S, D_MODEL), BF)
    return pl.pallas_call(
        body, name="merge_bwd", grid=(S // tm, D_MODEL // MERGE_BWD_TN),
        in_specs=[pl.BlockSpec((tm, D_MODEL), lambda i, j: (i, 0)), pl.BlockSpec((MERGE_BWD_TN, D_MODEL), lambda i, j: (j, 0))]
        + [ytile] * 3 + [wblk] * 3 + gls,
        out_specs=[otile] * 6, out_shape=[out] * 6,
        compiler_params=_params(("parallel", "arbitrary")),
    )(dout, w_out, *ys, *ws, gl, gl, gl)


def _dy_branches(dps, ws, tm):
    S = dps[0].shape[0]
    n = S // tm

    def body(*refs):
        b = pl.program_id(0)
        for k in range(3):
            @pl.when(b == k)
            def _(k=k):
                refs[6 + k][...] = _dot(refs[k][...], refs[3 + k][...], NN)

    def rows(k):
        return lambda b, i: (jnp.where(b == k, i, jnp.where(b > k, n - 1, 0)), 0)

    return pl.pallas_call(
        body, name="dy_br", grid=(3, n),
        in_specs=[pl.BlockSpec((tm, D_MODEL), rows(k)) for k in range(3)] + [pl.BlockSpec((D_MODEL, 1024), lambda b, i: (0, 0))] * 3,
        out_specs=[pl.BlockSpec((tm, 1024), rows(k)) for k in range(3)],
        out_shape=[jax.ShapeDtypeStruct((S, 1024), F32)] * 3,
        compiler_params=_params(("arbitrary", "arbitrary"), large=True),
    )(*dps, *ws)


def _out_loss(merged, w_out, x, target, post_g, tm):
    S = x.shape[0]

    def body(m_ref, w_ref, x_ref, t_ref, g_ref, dout_ref, dy_ref, loss_ref, dpost_ref):
        @pl.when(pl.program_id(0) == 0)
        def _():
            loss_ref[...] = jnp.zeros_like(loss_ref)
            dpost_ref[...] = jnp.zeros_like(dpost_ref)

        out = _dot(m_ref[...], w_ref[...], NN)
        r = lax.rsqrt(jnp.mean(out * out, axis=-1, keepdims=True) + EPS)
        nrm = out * r
        gv = g_ref[...]
        err = (x_ref[...] + nrm * gv) - t_ref[...]
        sq = jnp.sum(jnp.sum(err * err, axis=1, keepdims=True), axis=0, keepdims=True)
        loss_ref[...] += sq * (0.5 / D_MODEL)
        dy = err * (1.0 / D_MODEL)
        dy_ref[...] = dy
        dpost_ref[...] += jnp.sum(dy * nrm, axis=0, keepdims=True)
        dn = dy * gv
        dout_ref[...] = (r * (dn - nrm * jnp.mean(dn * nrm, axis=-1, keepdims=True))).astype(BF)

    row = pl.BlockSpec((tm, D_MODEL), lambda i: (i, 0))
    return pl.pallas_call(
        body, name="out_loss", grid=(S // tm,),
        in_specs=[row, pl.BlockSpec((D_MODEL, D_MODEL), lambda i: (0, 0)), row, row, pl.BlockSpec((1, D_MODEL), lambda i: (0, 0))],
        out_specs=[row, row, pl.BlockSpec((8, 128), lambda i: (0, 0)), pl.BlockSpec((1, D_MODEL), lambda i: (0, 0))],
        out_shape=[jax.ShapeDtypeStruct((S, D_MODEL), BF), jax.ShapeDtypeStruct((S, D_MODEL), F32),
                   jax.ShapeDtypeStruct((8, 128), F32), jax.ShapeDtypeStruct((1, D_MODEL), F32)],
        compiler_params=_params(("arbitrary",)),
    )(merged, w_out, x, target, post_g)


DH_DX_CHUNK = 64


def _dh_dx(dproj, w_in, x, dy, pre_g, tm, tk):
    S = x.shape[0]
    nk = D_IN // tk

    def body(dp_ref, w_ref, x_ref, dy_ref, g_ref, dx_ref, dpre_ref, acc_ref):
        i, k = pl.program_id(0), pl.program_id(1)

        @pl.when(jnp.logical_and(i == 0, k == 0))
        def _():
            dpre_ref[...] = jnp.zeros_like(dpre_ref)

        @pl.when(k == 0)
        def _():
            acc_ref[...] = jnp.zeros_like(acc_ref)

        acc_ref[...] += _dot(dp_ref[...], w_ref[...], NN)

        @pl.when(k == nk - 1)
        def _():
            def chunk(c, carry):
                rows = pl.ds(pl.multiple_of(c * DH_DX_CHUNK, DH_DX_CHUNK), DH_DX_CHUNK)
                dh = acc_ref[rows, :]
                xv = x_ref[rows, :]
                r = lax.rsqrt(jnp.mean(xv * xv, axis=-1, keepdims=True) + EPS)
                nrm = xv * r
                dpre_ref[...] += jnp.sum(dh * nrm, axis=0, keepdims=True)
                dn = dh * g_ref[...]
                dx_ref[rows, :] = r * (dn - nrm * jnp.mean(dn * nrm, axis=-1, keepdims=True)) + dy_ref[rows, :]
                return carry
            lax.fori_loop(0, tm // DH_DX_CHUNK, chunk, 0)

    row = pl.BlockSpec((tm, D_MODEL), lambda i, k: (i, 0))
    vec = pl.BlockSpec((1, D_MODEL), lambda i, k: (0, 0))
    return pl.pallas_call(
        body, name="dh_dx", grid=(S // tm, nk),
        in_specs=[pl.BlockSpec((tm, tk), lambda i, k: (i, k)), pl.BlockSpec((tk, D_MODEL), lambda i, k: (k, 0)), row, row, vec],
        out_specs=[row, vec],
        out_shape=[jax.ShapeDtypeStruct((S, D_MODEL), F32), jax.ShapeDtypeStruct((1, D_MODEL), F32)],
        scratch_shapes=[pltpu.VMEM((tm, D_MODEL), F32)],
        compiler_params=_params(("arbitrary", "arbitrary"), large=True),
    )(dproj, w_in, x, dy, pre_g)


def _sum_parts(parts, name):
    P, R, C = parts.shape
    tr = max(t for t in range(8, 513, 8) if R % t == 0)

    def body(p_ref, o_ref):
        acc = p_ref[0]
        for j in range(1, P):
            acc = acc + p_ref[j]
        o_ref[...] = acc

    return pl.pallas_call(
        body, name=name, grid=(R // tr,),
        in_specs=[pl.BlockSpec((P, tr, C), lambda i: (0, i, 0))], out_specs=pl.BlockSpec((tr, C), lambda i: (i, 0)),
        out_shape=jax.ShapeDtypeStruct((R, C), F32), compiler_params=_params(("parallel",)),
    )(parts)


def _adamw(land, sums, chip, w, m, v, name, group=(0, 1), into=None):
    q, n_groups = group
    _, R, cols = land.shape
    C = cols * n_groups
    tr = max(t for t in range(16, 257, 16) if R % t == 0)
    c1 = 1.0 - ADAM_B1 ** ADAM_STEP
    c2 = 1.0 - ADAM_B2 ** ADAM_STEP
    n_into = 0 if into is None else 4

    def body(chip_ref, p0_ref, p1_ref, p2_ref, own_ref, w_ref, m_ref, v_ref, *refs):
        g_ref, d_ref, nm_ref, nv_ref = refs[n_into:]
        g = own_ref[...].astype(F32)
        for p_ref in (p0_ref, p1_ref, p2_ref):
            g = g + p_ref[...].astype(F32)
        nm = ADAM_B1 * m_ref[...] + (1.0 - ADAM_B1) * g
        nv = ADAM_B2 * v_ref[...] + (1.0 - ADAM_B2) * (g * g)
        g_ref[...] = g
        nm_ref[...] = nm
        nv_ref[...] = nv
        d_ref[...] = -ADAM_LR * ((nm / c1) / (jnp.sqrt(nv / c2) + ADAM_EPS) + ADAM_WD * w_ref[...])

    tile = pl.BlockSpec((None, tr, cols), lambda i, c_ref: (0, i, q))
    specs = [pl.BlockSpec((None, tr, cols), (lambda i, c_ref, k=k: (k + (c_ref[0] <= k).astype(jnp.int32), i, 0))) for k in range(3)]
    specs.append(pl.BlockSpec((None, tr, cols), (lambda i, c_ref: (c_ref[0], i, 0))))
    return pl.pallas_call(
        body, name=name,
        grid_spec=pltpu.PrefetchScalarGridSpec(num_scalar_prefetch=1, grid=(R // tr,),
                                               in_specs=specs + [tile, tile, tile] + [pl.BlockSpec(memory_space=pl.ANY)] * n_into,
                                               out_specs=[tile] * 4),
        out_shape=[jax.ShapeDtypeStruct((1, R, C), F32)] * 4,
        input_output_aliases={8 + k: k for k in range(n_into)},
        compiler_params=_params(("parallel",)),
    )(chip, land, land, land, sums, w, m, v, *(into or []))


def _adamw_small(gs, ws, ms, vs):
    n = len(ws)
    c1 = 1.0 - ADAM_B1 ** ADAM_STEP
    c2 = 1.0 - ADAM_B2 ** ADAM_STEP

    def flat2(a):
        return a.reshape(-1, a.shape[-1])

    def body(*refs):
        ins, outs = refs[:4 * n], refs[4 * n:]
        for a in range(n):
            g, w, m, v = (ins[k * n + a][...] for k in range(4))
            nm = ADAM_B1 * m + (1.0 - ADAM_B1) * g
            nv = ADAM_B2 * v + (1.0 - ADAM_B2) * (g * g)
            outs[a][...] = g
            outs[n + a][...] = -ADAM_LR * ((nm / c1) / (jnp.sqrt(nv / c2) + ADAM_EPS) + ADAM_WD * w)
            outs[2 * n + a][...] = nm
            outs[3 * n + a][...] = nv

    shapes = [flat2(w).shape for w in ws]
    out = pl.pallas_call(
        body, name="adamw_small", out_shape=[jax.ShapeDtypeStruct(sh, F32) for sh in shapes] * 4,
        compiler_params=_params(),
    )(*[g.reshape(sh) for g, sh in zip(gs, shapes)], *[flat2(a) for a in (*ws, *ms, *vs)])
    return [[out[k * n + a].reshape(ws[a].shape) for a in range(n)] for k in range(4)]


PROJ_ROWS = 512


def _project(h, w_t, dep=None):
    S = h.shape[0]
    n_tiles = D_IN // SEG_TILE
    ranges = [(c0 // SEG_TILE, (c0 + width) // SEG_TILE) for _, c0, width, _ in SEGMENTS]
    dtypes = (F32, BF)
    n_extra = int(dep is not None)

    def of_dtype(j, dt):
        hit = False
        for (j0, j1), seg in zip(ranges, SEGMENTS):
            if seg[3] == dt:
                hit = jnp.logical_and(j >= j0, j < j1) | hit
        return hit

    def body(h_ref, w_ref, *refs):
        outs = refs[n_extra:n_extra + len(SEGMENTS)]
        stages, sems = refs[n_extra + len(SEGMENTS):-1], refs[-1]
        j = pl.program_id(0)
        slot = j % 2

        def copy_out(k, stage, dst):
            return pltpu.make_async_copy(stages[k].at[stage], dst, sems.at[stage])

        def wait_tile(jj, stage):
            for k, dt in enumerate(dtypes):
                o_ref = [o for o, seg in zip(outs, SEGMENTS) if seg[3] == dt and seg[0] != "gl"][0]
                @pl.when(of_dtype(jj, dt))
                def _(k=k, o_ref=o_ref):
                    copy_out(k, stage, o_ref.at[:, pl.ds(0, SEG_TILE)]).wait()

        @pl.when(j >= 2)
        def _():
            wait_tile(j - 2, slot)

        for k, dt in enumerate(dtypes):
            @pl.when(of_dtype(j, dt))
            def _(k=k, dt=dt):
                for c in range(S // PROJ_ROWS):
                    rows = pl.ds(c * PROJ_ROWS, PROJ_ROWS)
                    stages[k][slot, rows, :] = _dot(h_ref[rows, :], w_ref[...], NT).astype(dt)

        for (j0, j1), (name, _, _, dt), o_ref in zip(ranges, SEGMENTS, outs):
            @pl.when(jnp.logical_and(j >= j0, j < j1))
            def _(j0=j0, j1=j1, name=name, dt=dt, o_ref=o_ref):
                t = j - j0
                if name == "gl":
                    per = (j1 - j0) // 3
                    dst = o_ref.at[t // per, :, pl.ds(pl.multiple_of((t % per) * SEG_TILE, SEG_TILE), SEG_TILE)]
                else:
                    dst = o_ref.at[:, pl.ds(pl.multiple_of(t * SEG_TILE, SEG_TILE), SEG_TILE)]
                copy_out(dtypes.index(dt), slot, dst).start()

        @pl.when(j == n_tiles - 1)
        def _():
            wait_tile(j - 1, 1 - slot)
            wait_tile(j, slot)

    out_shapes = [jax.ShapeDtypeStruct((3, S, width // 3) if name == "gl" else (S, width), dt) for name, _, width, dt in SEGMENTS]
    outs = pl.pallas_call(
        body, name="proj", grid=(n_tiles,),
        in_specs=[pl.BlockSpec((S, D_MODEL), lambda j: (0, 0)), pl.BlockSpec((SEG_TILE, D_MODEL), lambda j: (j, 0))]
        + ([] if dep is None else [pl.BlockSpec((8, 128), lambda j: (0, 0))]),
        out_specs=[pl.BlockSpec(memory_space=pl.ANY)] * len(SEGMENTS), out_shape=out_shapes,
        scratch_shapes=[pltpu.VMEM((2, S, SEG_TILE), dt) for dt in dtypes] + [pltpu.SemaphoreType.DMA((2,))],
        compiler_params=_params(("arbitrary",), large=True),
    )(h, w_t, *([] if dep is None else [dep]))
    return {name: o for (name, _, _, _), o in zip(SEGMENTS, outs)}


def _forward_a(h, memn, w_in, sinks, rel_bias, dep=None):
    S = h.shape[0]
    st = dict(T=min(512, S // 2), tm=min(512, S), bucket=_rel_bucket_map(), h=h, memn=memn)
    seg = st["seg"] = _project(h, w_in, dep)
    st["o_swa"], st["y_swa"] = _swa_fwd(seg["q_s"], seg["kv"], seg["g_swa"], st["bucket"], rel_bias, _sink_column(sinks))
    return st


def _forward_rg(st, conv_w, conv_b, w_a, b_a, w_x, b_x, lam):
    seg = st["seg"]
    st["h_rg"], st["y_rg"] = _rglru_fwd(seg["xr"], seg["g_rg"], conv_w, conv_b, w_a, b_a, w_x, b_x, lam, st["T"])
    return st


def _forward_b(st, x, target, post_g, w_memkv, wbr, w_out):
    S = x.shape[0]
    M = st["memn"].shape[0]
    seg = st["seg"]
    st["mkv"] = _matmul(st["memn"], w_memkv, "nn", M, 2 * D_MEM, D_MODEL, M, 512, D_MODEL, BF, "mem_kv")
    st["o_mem"], st["y_mem"] = _mem_fwd(seg["q_m"], st["mkv"], seg["g_mem"])
    st["ys"] = (st["y_rg"], st["y_swa"], st["y_mem"])
    st["merged"] = _merge_fwd(st["ys"], wbr, seg["gl"], min(S, MERGE_TM))
    st["dout"], st["dy"], st["loss"], st["dpost"] = _out_loss(st["merged"], w_out, x, target, post_g, min(256, S))
    return st


def _backward_a1(st, wbr, w_out):
    S = st["h"].shape[0]
    seg, ys, tm = st["seg"], st["ys"], st["tm"]
    st["dw_out"] = _matmul(st["merged"], st["dout"], "tn", D_MODEL, D_MODEL, S, 256, D_MODEL, S, BF, "dw_out", out_blocked="row")
    dgl0, dgl1, dgl2, dp0, dp1, dp2 = _merge_bwd(st["dout"], w_out, ys, wbr, seg["gl"], min(S, MERGE_TM))
    st["dgl"] = (dgl0, dgl1, dgl2)
    dys, dwbr = _dy_branches((dp0, dp1, dp2), wbr, tm), []
    for i, dp in enumerate((dp0, dp1, dp2)):
        dwbr.append(_matmul(ys[i], dp, "tn", 1024, D_MODEL, S, 1024, 256, S, BF, "dw_br%d" % i, out_blocked="col"))
    st["dys"], st["dwbr"] = dys, dwbr
    return st


def _backward_a2(st, mem, w_memkv, conv_w, conv_b, w_a, b_a, w_x, b_x, lam):
    M = mem.shape[0]
    seg, dys = st["seg"], st["dys"]
    st["dq_m"], st["dg_mem"], dmkv = _mem_bwd(seg["q_m"], st["mkv"], seg["g_mem"], st["o_mem"], dys[2])
    st["dmkv"] = dmkv.astype(BF)
    st["dw_memkv"] = _matmul(st["memn"], st["dmkv"], "tn", D_MODEL, 2 * D_MEM, M, 256, 2 * D_MEM, M, BF, "dw_memkv", out_blocked="row")
    st["dxr"], st["dg_rg"], st["dw_a"], st["dw_x"], st["dvec"] = _rglru_bwd(
        seg["xr"], seg["g_rg"], st["h_rg"], dys[0], conv_w, conv_b, w_a, b_a, w_x, b_x, lam, st["T"])
    return st


def _mem_gain_grad(st, mem, w_memkv, dep=None):
    M = mem.shape[0]
    dmemn = _matmul(st["dmkv"], w_memkv, "nt", M, D_MODEL, 2 * D_MEM, M, 512, 2 * D_MEM, F32, "dmemn", dep=dep)
    return _rms_gain_grad(dmemn, mem, "dmem_gain")


def _backward_b(st, rel_bias, sinks):
    seg = st["seg"]
    others = {"xr": st["dxr"], "g_rg": st["dg_rg"], "q_m": st["dq_m"], "g_mem": st["dg_mem"], "gl": st["dgl"]}
    dproj, dkv, dsinks, drel = _swa_bwd(seg["q_s"], seg["kv"], seg["g_swa"], st["o_swa"], st["dys"][1],
                                        st["bucket"], rel_bias, _sink_column(sinks), others)
    st["dsinks"], st["drel"] = dsinks.reshape(1, SWA_HEADS), drel.reshape(REL_BUCKETS, SWA_HEADS)
    col_kv = [c0 for name, c0, _, _ in SEGMENTS if name == "kv"][0]
    st["dproj"] = lax.dynamic_update_slice(dproj, dkv.astype(BF), (0, col_kv))
    return st


def _dw_in_half(st, half, dep=None):
    S = st["h"].shape[0]
    dw = _matmul(st["dproj"], st["h"], "tn", D_IN, D_MODEL // 2, S, D_IN_TILE, D_MODEL // 2, S, BF, "dw_in%d" % half, b_noff=half, dep=dep, out_hbm=True)
    return dw.reshape(N_DEV, D_IN // N_DEV, D_MODEL // 2)


def _owner_blocks(a):
    return jnp.swapaxes(a.reshape((4, 2) + a.shape[1:]), 0, 1)


def _pad_rows(a, rows):
    a = a.reshape(-1, 128) if a.shape[-1] % 128 == 0 else jnp.pad(a, ((0, 0), (0, 128 - a.shape[-1])))
    return jnp.pad(a, ((0, rows - a.shape[0]), (0, 0))) if a.shape[0] < rows else a


def kernel(x, mem, pre_norm_g, post_norm_g, mem_norm_g, w_in, conv_w, conv_b, w_rg_a, b_rg_a, w_rg_x, b_rg_x, lru_lambda, swa_sinks, rel_bias, w_mem_kv, w_br_rg, w_br_swa, w_br_mem, w_out, loss_target, m_pre_norm_g, m_post_norm_g, m_mem_norm_g, m_w_in, m_conv_w, m_conv_b, m_w_rg_a, m_b_rg_a, m_w_rg_x, m_b_rg_x, m_lru_lambda, m_swa_sinks, m_rel_bias, m_w_mem_kv, m_w_br_rg, m_w_br_swa, m_w_br_mem, m_w_out, v_pre_norm_g, v_post_norm_g, v_mem_norm_g, v_w_in, v_conv_w, v_conv_b, v_w_rg_a, v_b_rg_a, v_w_rg_x, v_b_rg_x, v_lru_lambda, v_swa_sinks, v_rel_bias, v_w_mem_kv, v_w_br_rg, v_w_br_swa, v_w_br_mem, v_w_out):
    cx, cy, cc = lax.axis_index("x"), lax.axis_index("y"), lax.axis_index("c")
    me = 4 * cx + 2 * cy + cc
    chip = 2 * cx + cy
    core = jnp.reshape(cc, (1,)).astype(jnp.int32)
    x0, mem0 = x[0], mem[0]
    w_a_b, w_x_b = w_rg_a[0].astype(BF), w_rg_x[0].astype(BF)

    def landing(own, slot, slots):
        return lax.dynamic_update_slice(lax.empty((slots,) + own.shape, own.dtype), own[None], (slot,) + (0,) * own.ndim)


    def swap_start(parts, tag):
        return _exchange_start(parts, [lax.empty((4,) + p.shape[-2:], p.dtype) for p in parts], _plan_swap([p.ndim for p in parts]),
                               "swap_%s_start" % tag)

    def scatter_start(swap, after, tag, prefill=()):
        s_send, s_recv, parts, got, _ = swap
        got = _exchange_wait(s_send, s_recv, parts, got, _plan_swap([p.ndim for p in parts]), after, "swap_%s_wait" % tag)
        sums = [_pair_sum(p, g, core, "scatter_%s_sum%d" % (tag, i)) for i, (p, g) in enumerate(zip(parts, got))]
        lands = [landing(lax.dynamic_index_in_dim(s, chip, 0, keepdims=False), chip, 4) if i in prefill
                 else lax.empty(s.shape, s.dtype) for i, s in enumerate(sums)]
        return _exchange_start(sums, lands, _plan_scatter(len(sums)), "scatter_%s_start" % tag)

    def corner(a):
        return a.reshape(-1, a.shape[-1])[:8, :128]

    def zero_after(a):
        return jnp.minimum(jnp.abs(a.reshape(-1)[0].astype(F32)), 0.0)

    g_in, g_cw, h0, memn0 = _all_gather_relayed([jnp.transpose(w_in[0]).astype(BF), conv_w[0]], [True, False], "gather_w_in",
                                                side=_rms_side([(x0, pre_norm_g), (mem0, mem_norm_g)]))
    w_in_f = g_in.reshape(D_IN, D_MODEL)
    conv_w_f = jnp.transpose(g_cw, (1, 0, 2)).reshape(CONV_W, D_RNN)

    after_first = zero_after(g_cw).astype(BF)
    rest = [w.astype(BF) + after_first for w in (w_mem_kv[0], jnp.transpose(w_br_rg[0]), jnp.transpose(w_br_swa[0]),
                                                 jnp.transpose(w_br_mem[0]), w_out[0])]
    plan_g = _plan_gather(len(rest))
    zones = _place_own([lax.empty((N_DEV,) + w.shape, w.dtype) for w in rest], rest, jnp.reshape(me, (1,)).astype(jnp.int32), "gather_rest_own")
    g_send, g_recv, g_src, g_land, g_token = _exchange_start(rest, zones, plan_g, "gather_rest_start")
    st = _forward_a(h0, memn0, w_in_f, swa_sinks, rel_bias, dep=g_token)
    g_land = _exchange_wait(g_send, g_recv, g_src, g_land, plan_g, st["y_swa"], "gather_rest_wait")
    plan_f = _plan_forward(len(rest))
    f_send, f_recv, _, g_land, f_token = _exchange_start(None, g_land, plan_f, "forward_rest_start")
    st = _forward_rg(st, conv_w_f, conv_b + f_token[0:1, 0:1], w_a_b, b_rg_a, w_x_b, b_rg_x, lru_lambda)
    g_land = _exchange_wait(f_send, f_recv, None, g_land, plan_f, corner(st["y_rg"]), "forward_rest_wait")
    w_memkv_f = g_land[0].reshape(D_MODEL, 2 * D_MEM)
    wbr = tuple(g_land[i].reshape(D_MODEL, D_RNN) for i in (1, 2, 3))
    w_out_f = g_land[4].reshape(D_MODEL, D_MODEL)

    st = _forward_b(st, x0, loss_target[0], post_norm_g, w_memkv_f, wbr, w_out_f)
    st = _backward_a1(st, wbr, w_out_f)
    parts_a = [st["dw_out"], st["dwbr"][0], st["dwbr"][1], st["dwbr"][2]]
    plan_a = _plan_scatter(len(parts_a))
    swap_a = swap_start(parts_a, "a")
    st = _backward_a2(st, mem0, w_memkv_f, conv_w_f, conv_b + swap_a[4][0:1, 0:1], w_a_b, b_rg_a, w_x_b, b_rg_x, lru_lambda)
    a_send, a_recv, a_src, a_land, a_token = scatter_start(swap_a, st["dxr"], "a")
    parts_c = [st["dw_memkv"], _owner_blocks(st["dw_a"]), _owner_blocks(st["dw_x"])]
    plan_c = _plan_scatter(len(parts_c))
    swap_c = swap_start(parts_c, "c")

    st = _backward_b(st, rel_bias, swa_sinks + swap_c[4][0:1, 0:1] + a_token[0:1, 0:1])
    c_send, c_recv, c_src, c_land, c_token = scatter_start(swap_c, st["dsinks"], "c", prefill=(1, 2))
    plan_b = _plan_scatter(1)

    def dw_in_parts(half, dep):
        dwh = _dw_in_half(st, half, dep)
        return dwh, [dwh]

    dw0, parts_b0 = dw_in_parts(0, c_token)
    swap_b0 = swap_start(parts_b0, "b0")
    a_land = _exchange_wait(a_send, a_recv, a_src, a_land, plan_a, swap_b0[4], "scatter_a_wait")
    big = [None] * 6

    chip1 = jnp.reshape(chip, (1,)).astype(jnp.int32)

    def adamw_big(j, land, own, wt, mt, vt):
        big[j] = _adamw(land, own, chip1, wt, mt, vt, "adamw_big%d" % j)

    adamw_big(5, a_land[0], a_src[0], w_out, m_w_out, v_w_out)
    adamw_big(2, a_land[1], a_src[1], w_br_rg, m_w_br_rg, v_w_br_rg)
    adamw_big(3, a_land[2], a_src[2], w_br_swa, m_w_br_swa, v_w_br_swa)
    halves = [scatter_start(swap_b0, corner(big[5][1]) + corner(big[2][1]) + corner(big[3][1]), "b0")]
    dw1, parts_b1 = dw_in_parts(1, halves[0][4])
    swap_b1 = swap_start(parts_b1, "b1")
    c_land = _exchange_wait(c_send, c_recv, c_src, c_land, plan_c, swap_b1[4], "scatter_c_wait")
    g_wa_blk = _sum_parts(c_land[1], "sum_w_rg_a")
    g_wx_blk = _sum_parts(c_land[2], "sum_w_rg_x")
    adamw_big(4, a_land[3], a_src[3], w_br_mem, m_w_br_mem, v_w_br_mem)
    adamw_big(1, c_land[0], c_src[0], w_mem_kv, m_w_mem_kv, v_w_mem_kv)
    halves.append(scatter_start(swap_b1, corner(big[4][1]) + corner(big[1][1]), "b1"))
    st["dmem_g"] = _mem_gain_grad(st, mem0, w_memkv_f, halves[1][4])
    grad_x, dpre = _dh_dx(st["dproj"], w_in_f, x0, st["dy"], pre_norm_g + halves[1][4][0:1, 0:1], st["tm"], D_IN_TILE)
    pack = jnp.concatenate([dpre.reshape(16, 128), st["dpost"].reshape(16, 128), st["dmem_g"].reshape(16, 128),
                            st["dvec"].reshape(64, 128), _pad_rows(st["dsinks"], 8), _pad_rows(st["drel"], 32), g_wa_blk, g_wx_blk,
                            st["loss"]], axis=0)
    plan_s = _plan_everyone(1)
    s_send, s_recv, s_src, s_land, s_token = _exchange_start([pack], [landing(pack, me, N_DEV)], plan_s, "gather_small_start")
    swap_last = lambda a: jnp.transpose(a, (0, 2, 1))
    after, big0_t = s_token, None
    for half, (b_send, b_recv, b_src, b_land, _) in enumerate(halves):
        b_land = _exchange_wait(b_send, b_recv, b_src, b_land, plan_b, after, "scatter_b%d_wait" % half)[0]
        big0_t = _adamw(b_land, b_src[0], chip1, swap_last(w_in), swap_last(m_w_in), swap_last(v_w_in), "adamw_big0_%d" % half,
                        group=(half, 2), into=big0_t)
        after = corner(big0_t[1])
    big[0] = [swap_last(a) for a in big0_t]
    gathered = _exchange_wait(s_send, s_recv, s_src, s_land, plan_s, corner(big0_t[1]), "gather_small_wait")[0]
    gs = _sum_parts(gathered, "sum_small")
    loss_total = gs[408, 0]
    g_pre, g_post, g_memg = gs[0:16].reshape(1, D_MODEL), gs[16:32].reshape(1, D_MODEL), gs[32:48].reshape(1, D_MODEL)
    gvec = gs[48:112].reshape(8, D_RNN)
    g_conv_w = lax.dynamic_slice(gvec[0:CONV_W], (0, me * RNN_BLOCK), (CONV_W, RNN_BLOCK))
    g_conv_b, g_b_a, g_b_x, g_lam = gvec[4:5], gvec[5:6], gvec[6:7], gvec[7:8]
    g_sinks = gs[112:113, :SWA_HEADS]
    g_rel = gs[120:152, :SWA_HEADS]
    g_w_a = gathered[:, 152:280]
    g_w_x = gathered[:, 280:408]

    g_small = (g_pre, g_post, g_memg, g_conv_b, g_b_a, g_b_x, g_lam, g_w_a, g_w_x, g_sinks, g_rel, g_conv_w)
    w_small = (pre_norm_g, post_norm_g, mem_norm_g, conv_b, b_rg_a, b_rg_x, lru_lambda, w_rg_a, w_rg_x, swa_sinks, rel_bias, conv_w)
    m_small = (m_pre_norm_g, m_post_norm_g, m_mem_norm_g, m_conv_b, m_b_rg_a, m_b_rg_x, m_lru_lambda, m_w_rg_a, m_w_rg_x, m_swa_sinks, m_rel_bias, m_conv_w)
    v_small = (v_pre_norm_g, v_post_norm_g, v_mem_norm_g, v_conv_b, v_b_rg_a, v_b_rg_x, v_lru_lambda, v_w_rg_a, v_w_rg_x, v_swa_sinks, v_rel_bias, v_conv_w)
    sm = _adamw_small(g_small, w_small, m_small, v_small)


    def leaves(k):
        s = sm[k]
        return [s[0], s[1], s[2], big[0][k], s[11], s[3], s[7], s[4], s[8], s[5], s[6], s[9], s[10],
                big[1][k], big[2][k], big[3][k], big[4][k], big[5][k]]

    return (loss_total, grad_x[None], *leaves(0), *leaves(1), *leaves(2), *leaves(3))
```

```python
import math

import jax
import jax.numpy as jnp
import numpy as np
from jax import lax
from jax.experimental import pallas as pl
from jax.experimental.pallas import tpu as pltpu

F32, BF = jnp.float32, jnp.bfloat16
MESH = pl.DeviceIdType.MESH
N_DEV = 8

D_MODEL = 2048
D_RNN = 1024
RNN_BLOCKS = 8
RNN_BLOCK = 128
CONV_W = 4
LRU_C = 8.0
SWA_HEADS = 16
SWA_KV_HEADS = 2
SWA_HD = 64
WINDOW = 128
MEM_HEADS = 4
MEM_HD = 256
D_MEM = 1024
REL_BUCKETS = 32
REL_MAX_DIST = 128
EPS = 1e-6
NEG_INF = -1e30
D_IN = 12544
SEGMENTS = (("xr", 0, 1024, F32), ("g_rg", 1024, 1024, F32), ("q_s", 2048, 1024, BF), ("kv", 3072, 256, BF),
            ("g_swa", 3328, 1024, F32), ("q_m", 4352, 1024, BF), ("g_mem", 5376, 1024, F32), ("gl", 6400, 6144, F32))
SEG_TILE = 256
D_IN_TILE = 7 * SEG_TILE

ADAM_LR, ADAM_B1, ADAM_B2, ADAM_EPS, ADAM_WD, ADAM_STEP = 0.001, 0.9, 0.999, 1e-08, 0.01, 10

NN = (((1,), (0,)), ((), ()))
NT = (((1,), (1,)), ((), ()))
TN = (((0,), (0,)), ((), ()))
MIB = 2 ** 20


def _dot(a, b, dn):
    return lax.dot_general(a, b, dn, preferred_element_type=F32)


VMEM_LIMIT_MIB = 48
VMEM_LIMIT_LARGE_MIB = 56


def _params(sem=None, large=False):
    return pltpu.CompilerParams(dimension_semantics=sem, vmem_limit_bytes=(VMEM_LIMIT_LARGE_MIB if large else VMEM_LIMIT_MIB) * MIB)


def _sigmoid(z):
    return 1.0 / (1.0 + jnp.exp(-z))


def _softplus(z):
    return jnp.maximum(z, 0.0) + jnp.log(1.0 + jnp.exp(-jnp.abs(z)))


def _expm1(z):
    p = z * (1.0 + z * (0.5 + z * (1.0 / 6 + z * (1.0 / 24 + z * (1.0 / 120 + z * (1.0 / 720 + z * (1.0 / 5040 + z / 40320)))))))
    return jnp.where(jnp.abs(z) < 0.3, p, jnp.exp(z) - 1.0)


def _flat(p):
    return 4 * p[0] + 2 * p[1] + p[2]


def _all_gather_relayed(arrs, relay, name, side=None):
    n = len(arrs)
    K = 9
    work, side_ins, side_outs, side_scratch = side if side is not None else (None, [], [], [])
    n_in, n_out = n + len(side_ins), n + len(side_outs)

    def body(*refs):
        ins, outs = refs[:n], refs[n_in:n_in + n]
        send_sems, recv_sems, local_sems = refs[n_in + n_out:n_in + n_out + 3]
        x, y, c = lax.axis_index("x"), lax.axis_index("y"), lax.axis_index("c")
        me, sib = (x, y, c), (x, y, 1 - c)
        xn, yn, dg = (1 - x, y, c), (x, 1 - y, c), (1 - x, 1 - y, c)

        def other(p):
            return (p[0], p[1], 1 - p[2])

        def rows(a, half):
            h = arrs[a].shape[0] // 2
            return pl.ds(half * h, h)

        def copy(a, k, block, to, half=None, src=None):
            dst = outs[a].at[_flat(block)]
            if half is not None:
                dst = dst.at[rows(a, half)]
            return pltpu.make_async_remote_copy(src_ref=dst if src is None else src, dst_ref=dst,
                                                send_sem=send_sems.at[a * K + k], recv_sem=recv_sems.at[a * K + k],
                                                device_id=to, device_id_type=MESH)

        mine = [pltpu.make_async_copy(ins[a], outs[a].at[_flat(me)], local_sems.at[a]) for a in range(n)]
        for cp in mine:
            cp.start()
        sends = []

        def start(cp):
            cp.start()
            sends.append(cp)

        for a in range(n):
            start(copy(a, 1, me, xn, src=ins[a]))
            start(copy(a, 2, me, yn, src=ins[a]))
            if not relay[a]:
                start(copy(a, 3, me, dg, src=ins[a]))
            start(copy(a, 0, me, sib, src=ins[a]))
        if work is not None:
            work(refs[n:n_in], refs[n_in + n:n_in + n_out], refs[n_in + n_out + 3:])
        for a in range(n):
            copy(a, 1, xn, me).wait_recv()
            if relay[a]:
                start(copy(a, 3, xn, yn, half=0))
            start(copy(a, 5, xn, sib))
        for a in range(n):
            copy(a, 2, yn, me).wait_recv()
            if relay[a]:
                start(copy(a, 4, yn, xn, half=1))
            start(copy(a, 6, yn, sib))
        for a in range(n):
            if relay[a]:
                copy(a, 3, dg, me, half=0).wait_recv()
                start(copy(a, 7, dg, sib, half=0))
                copy(a, 4, dg, me, half=1).wait_recv()
                start(copy(a, 8, dg, sib, half=1))
            else:
                copy(a, 3, dg, me).wait_recv()
                start(copy(a, 7, dg, sib))
        for a in range(n):
            copy(a, 0, sib, me).wait_recv()
            copy(a, 5, other(xn), me).wait_recv()
            copy(a, 6, other(yn), me).wait_recv()
            if relay[a]:
                copy(a, 7, other(dg), me, half=0).wait_recv()
                copy(a, 8, other(dg), me, half=1).wait_recv()
            else:
                copy(a, 7, other(dg), me).wait_recv()
        for cp in sends:
            cp.wait_send()
        for cp in mine:
            cp.wait()

    any_spec = pl.BlockSpec(memory_space=pl.ANY)
    return pl.pallas_call(
        body, name=name,
        out_shape=[jax.ShapeDtypeStruct((N_DEV,) + a.shape, a.dtype) for a in arrs] + list(side_outs),
        in_specs=[any_spec] * n_in, out_specs=[any_spec] * n_out,
        scratch_shapes=[pltpu.SemaphoreType.DMA((K * n,)), pltpu.SemaphoreType.DMA((K * n,)), pltpu.SemaphoreType.DMA((n,))]
        + list(side_scratch),
        compiler_params=_params(),
    )(*arrs, *side_ins)


def _chip_peers(x, y):
    return [(1 - x, y), (x, 1 - y), (1 - x, 1 - y)]


def _chip(p):
    return 2 * p[0] + p[1]


def _plan_gather(n):
    def plan(x, y, c):
        out = []
        for a in range(n):
            for peer in [(x, y, 1 - c)] + [(*ch, c) for ch in _chip_peers(x, y)]:
                out.append((a, None, ("lead", _flat((x, y, c))), peer, ("lead", _flat(peer))))
        return out
    return plan


def _plan_everyone(n):
    def plan(x, y, c):
        out = []
        for a in range(n):
            for r in range(1, N_DEV):
                peer = (1 - x if r & 4 else x, 1 - y if r & 2 else y, 1 - c if r & 1 else c)
                out.append((a, None, ("lead", _flat((x, y, c))), peer, ("lead", _flat(peer))))
        return out
    return plan


def _plan_swap(ndims):
    def plan(x, y, c):
        out = []
        for a, nd in enumerate(ndims):
            if nd == 4:
                out.append((a, 1 - c, ("all", 0), (x, y, 1 - c), ("all", 0)))
            else:
                out += [(a, 2 * j + 1 - c, ("lead", j), (x, y, 1 - c), ("lead", j)) for j in range(4)]
        return out
    return plan


def _slot(ref, where):
    kind, k = where
    return ref if kind == "all" else ref.at[k]


def _plan_scatter(n):
    def plan(x, y, c):
        out = []
        for a in range(n):
            for ch in _chip_peers(x, y):
                out.append((a, _chip(ch), ("lead", _chip((x, y))), (*ch, c), ("lead", _chip(ch))))
        return out
    return plan


HBM_SPEC = pl.BlockSpec(memory_space=pltpu.HBM)
SEM_SPEC = pl.BlockSpec(memory_space=pltpu.SEMAPHORE)


def _in_hbm(a):
    return pltpu.with_memory_space_constraint(a, pltpu.HBM)


def _exchange_start(srcs, lands, plan, name):
    n = len(lands)
    ns = 0 if srcs is None else n
    count = len(plan(0, 0, 0))

    def body(*refs):
        land_refs = refs[ns:ns + n]
        src_refs = land_refs if srcs is None else refs[:n]
        send_sems, recv_sems = refs[ns + n], refs[ns + n + 1]
        token = refs[-1]
        x, y, c = lax.axis_index("x"), lax.axis_index("y"), lax.axis_index("c")
        for k, (a, si, di, peer, _) in enumerate(plan(x, y, c)):
            src = src_refs[a] if si is None else src_refs[a].at[si]
            pltpu.make_async_remote_copy(src_ref=src, dst_ref=_slot(land_refs[a], di), send_sem=send_sems.at[k],
                                         recv_sem=recv_sems.at[k], device_id=peer, device_id_type=MESH).start()
        token[...] = jnp.zeros_like(token)

    out = pl.pallas_call(
        body, name=name,
        out_shape=(pltpu.SemaphoreType.DMA((count,)), pltpu.SemaphoreType.DMA((count,)),
                   *[pltpu.HBM(a.shape, a.dtype) for a in lands], jax.ShapeDtypeStruct((8, 128), F32)),
        in_specs=[HBM_SPEC] * (ns + n),
        out_specs=(SEM_SPEC, SEM_SPEC, *([HBM_SPEC] * n), pl.BlockSpec(memory_space=pltpu.VMEM)),
        input_output_aliases={ns + i: 2 + i for i in range(n)},
        compiler_params=pltpu.CompilerParams(has_side_effects=pltpu.SideEffectType.DATAFLOW_SIDE_EFFECTING),
    )(*[_in_hbm(a) for a in (srcs or [])], *[_in_hbm(a) for a in lands])
    return out[0], out[1], srcs if srcs is None else list(srcs), list(out[2:2 + n]), out[-1]


def _exchange_wait(send_sems, recv_sems, srcs, lands, plan, after, name):
    n = len(lands)
    ns = 0 if srcs is None else n

    def body(*refs):
        land_refs = refs[ns:ns + n]
        src_refs = land_refs if srcs is None else refs[:n]
        send_sems, recv_sems = refs[ns + n], refs[ns + n + 1]
        x, y, c = lax.axis_index("x"), lax.axis_index("y"), lax.axis_index("c")
        for k, (a, si, _, peer, ri) in enumerate(plan(x, y, c)):
            src = src_refs[a] if si is None else src_refs[a].at[si]
            cp = pltpu.make_async_remote_copy(src_ref=src, dst_ref=_slot(land_refs[a], ri), send_sem=send_sems.at[k],
                                              recv_sem=recv_sems.at[k], device_id=peer, device_id_type=MESH)
            cp.wait_send()
            cp.wait_recv()

    out = pl.pallas_call(
        body, name=name,
        out_shape=tuple(pltpu.HBM(a.shape, a.dtype) for a in lands),
        in_specs=[HBM_SPEC] * (ns + n) + [SEM_SPEC, SEM_SPEC, pl.BlockSpec(memory_space=pl.ANY)],
        out_specs=tuple([HBM_SPEC] * n),
        input_output_aliases={ns + i: i for i in range(n)},
        compiler_params=pltpu.CompilerParams(has_side_effects=pltpu.SideEffectType.DATAFLOW_SIDE_EFFECTING),
    )(*[_in_hbm(a) for a in (srcs or [])], *lands, send_sems, recv_sems, after)
    return list(out)


def _plan_forward(n):
    def plan(x, y, c):
        return [(a, _flat((*ch, c)), ("lead", _flat((*ch, c))), (x, y, 1 - c), ("lead", _flat((*ch, 1 - c))))
                for a in range(n) for ch in _chip_peers(x, y)]
    return plan


def _place_own(zones, owns, slot, name):
    n = len(zones)

    def body(slot_ref, *refs):
        for a in range(n):
            refs[2 * n + a][...] = refs[a][...]

    return pl.pallas_call(
        body, name=name,
        grid_spec=pltpu.PrefetchScalarGridSpec(
            num_scalar_prefetch=1, grid=(1,),
            in_specs=[pl.BlockSpec(o.shape, lambda i, s_ref: (0, 0)) for o in owns] + [pl.BlockSpec(memory_space=pl.ANY)] * n,
            out_specs=[pl.BlockSpec((None,) + o.shape, lambda i, s_ref: (s_ref[0], 0, 0)) for o in owns]),
        out_shape=[jax.ShapeDtypeStruct(z.shape, z.dtype) for z in zones],
        input_output_aliases={1 + n + a: a for a in range(n)},
        compiler_params=_params(("arbitrary",)),
    )(slot, *owns, *zones)


def _pair_sum(parts, got, core, name):
    R, C = parts.shape[-2:]
    mine = (pl.BlockSpec((None, None, R, C), lambda j, c_ref: (c_ref[0], j, 0, 0)) if parts.ndim == 4
            else pl.BlockSpec((None, R, C), lambda j, c_ref: (2 * j + c_ref[0], 0, 0)))

    def body(c_ref, p_ref, g_ref, o_ref):
        o_ref[...] = (p_ref[...].astype(F32) + g_ref[...].astype(F32)).astype(o_ref.dtype)

    return pl.pallas_call(
        body, name=name,
        grid_spec=pltpu.PrefetchScalarGridSpec(
            num_scalar_prefetch=1, grid=(4,),
            in_specs=[mine, pl.BlockSpec((None, R, C), lambda j, c_ref: (j, 0, 0))],
            out_specs=pl.BlockSpec((None, R, C), lambda j, c_ref: (j, 0, 0))),
        out_shape=pltpu.HBM((4, R, C), parts.dtype),
        compiler_params=_params(("parallel",)),
    )(core, parts, got)


def _matmul(a, b, mode, M, N, K, tm, tn, tk, out_dtype, name, b_noff=0, out_blocked=None, dep=None, out_hbm=False):
    nm, nn, nk = M // tm, N // tn, K // tk
    if mode == "nn":
        a_spec = pl.BlockSpec((tm, tk), lambda j, i, k: (i, k))
        b_spec = pl.BlockSpec((tk, tn), lambda j, i, k: (k, j + b_noff))
        dn = NN
    elif mode == "nt":
        a_spec = pl.BlockSpec((tm, tk), lambda j, i, k: (i, k))
        b_spec = pl.BlockSpec((tn, tk), lambda j, i, k: (j + b_noff, k))
        dn = NT
    else:
        a_spec = pl.BlockSpec((tk, tm), lambda j, i, k: (k, i))
        b_spec = pl.BlockSpec((tk, tn), lambda j, i, k: (k, j + b_noff))
        dn = TN
    if out_blocked == "col":
        out_shape = jax.ShapeDtypeStruct((2, 4, M, tn), out_dtype)
        out_spec = pl.BlockSpec((None, None, tm, tn), lambda j, i, k: (j % 2, j // 2, i, 0))
    elif out_blocked == "row":
        out_shape = jax.ShapeDtypeStruct((2, 4, tm, N), out_dtype)
        out_spec = pl.BlockSpec((None, None, tm, tn), lambda j, i, k: (i % 2, i // 2, 0, j))
    else:
        out_shape = jax.ShapeDtypeStruct((M, N), out_dtype)
        out_spec = pl.BlockSpec((tm, tn), lambda j, i, k: (i, j))
    if out_hbm or out_blocked is not None:
        out_shape = pltpu.HBM(out_shape.shape, out_shape.dtype)

    n_extra = int(dep is not None)

    def body(a_ref, b_ref, *rest):
        o_ref, scratch = rest[n_extra], rest[n_extra + 1:]
        if nk == 1:
            o_ref[...] = _dot(a_ref[...], b_ref[...], dn).astype(out_dtype)
        else:
            acc_ref, = scratch
            k = pl.program_id(2)

            @pl.when(k == 0)
            def _():
                acc_ref[...] = jnp.zeros_like(acc_ref)

            acc_ref[...] += _dot(a_ref[...], b_ref[...], dn)

            @pl.when(k == nk - 1)
            def _():
                o_ref[...] = acc_ref[...].astype(out_dtype)

    return pl.pallas_call(
        body, name=name, grid=(nn, nm, nk),
        in_specs=[a_spec, b_spec] + ([] if dep is None else [pl.BlockSpec((8, 128), lambda j, i, k: (0, 0))]),
        out_specs=out_spec, out_shape=out_shape,
        scratch_shapes=[] if nk == 1 else [pltpu.VMEM((tm, tn), F32)],
        compiler_params=_params(("parallel", "parallel", "arbitrary")),
    )(a, b, *([] if dep is None else [dep]))


RMS_SIDE_ROWS = 512


def _rms_side(pairs):
    chunks = [min(x.shape[0], RMS_SIDE_ROWS) for x, _ in pairs]

    def work(ins, outs, scratch):
        sem = scratch[-1]

        def move(src, dst):
            cp = pltpu.make_async_copy(src, dst, sem.at[0])
            cp.start()
            cp.wait()

        for p, ((x, _), tr) in enumerate(zip(pairs, chunks)):
            x_ref, g_ref, h_ref = ins[2 * p], ins[2 * p + 1], outs[p]
            xv, gv, hv = scratch[3 * p:3 * p + 3]
            move(g_ref, gv)
            for i in range(x.shape[0] // tr):
                rows = pl.ds(i * tr, tr)
                move(x_ref.at[rows], xv)
                v = xv[...]
                hv[...] = (v * lax.rsqrt(jnp.mean(v * v, axis=-1, keepdims=True) + EPS) * gv[...]).astype(BF)
                move(hv, h_ref.at[rows])

    scratch = [s for (x, g), tr in zip(pairs, chunks)
               for s in (pltpu.VMEM((tr, x.shape[1]), F32), pltpu.VMEM(g.shape, F32), pltpu.VMEM((tr, x.shape[1]), BF))]
    return (work, [a for pair in pairs for a in pair], [jax.ShapeDtypeStruct(x.shape, BF) for x, _ in pairs],
            scratch + [pltpu.SemaphoreType.DMA((1,))])


def _rms_gain_grad(dn, x, name):
    R, Dm = x.shape

    def body(dn_ref, x_ref, o_ref):
        xv = x_ref[...]
        r = lax.rsqrt(jnp.mean(xv * xv, axis=-1, keepdims=True) + EPS)
        o_ref[...] = jnp.sum(dn_ref[...] * xv * r, axis=0, keepdims=True)

    return pl.pallas_call(
        body, name=name, out_shape=jax.ShapeDtypeStruct((1, Dm), F32),
        compiler_params=_params(),
    )(dn, x)


def _shift_down(v, k, head8, row, T):
    if k == 0:
        return v
    r = pltpu.roll(v, k, 0)
    hr = pltpu.roll(head8, k, 0)
    top = jnp.where(row[:8] < k, hr, r[:8])
    return jnp.concatenate([top, r[8:]], axis=0)


def _shift_up(v, k, tail8, row, T):
    if k == 0:
        return v
    r = pltpu.roll(v, T - k, 0)
    tr = pltpu.roll(tail8, 8 - k, 0)
    bot = jnp.where(row[:8] >= 8 - k, tr, r[T - 8:])
    return jnp.concatenate([r[:T - 8], bot], axis=0)


def _rglru_gates(u, head8, grow, row, T, cw_ref, cb_ref, wa_ref, ba_ref, wx_ref, bx_ref, lam_ref):
    us = [_shift_down(u, k, head8, row, T) for k in range(CONV_W)]
    acc = us[0] * cw_ref[0:1, :]
    for k in range(1, CONV_W):
        acc = acc + us[k] * cw_ref[k:k + 1, :]
    conv = cb_ref[...] + acc
    cbf = conv.astype(BF)
    r_ = _sigmoid(_dot(cbf, wa_ref[0], NN) + ba_ref[...])
    i_ = _sigmoid(_dot(cbf, wx_ref[0], NN) + bx_ref[...])
    sp = _softplus(-lam_ref[...])
    la = -LRU_C * r_ * sp
    a = jnp.exp(la)
    mult_raw = jnp.sqrt(-_expm1(2.0 * la))
    mult = jnp.where(grow == 0, 1.0, mult_raw)
    return us, conv, cbf, r_, i_, sp, a, mult_raw, mult


def _rglru_specs(T, nt, rev):
    tmap = (lambda n, t: (nt - 1 - t, n)) if rev else (lambda n, t: (t, n))
    hmap = ((lambda n, t: (jnp.maximum((nt - 1 - t) * (T // 8) - 1, 0), n)) if rev
            else (lambda n, t: (jnp.maximum(t * (T // 8) - 1, 0), n)))
    tile = pl.BlockSpec((T, RNN_BLOCK), tmap)
    halo = pl.BlockSpec((8, RNN_BLOCK), hmap)
    vec = pl.BlockSpec((1, RNN_BLOCK), lambda n, t: (0, n))
    cw = pl.BlockSpec((CONV_W, RNN_BLOCK), lambda n, t: (0, n))
    wblk = pl.BlockSpec((1, RNN_BLOCK, RNN_BLOCK), lambda n, t: (n, 0, 0))
    return tile, halo, vec, cw, wblk


def _rglru_fwd(xr, g, cw, cb, wa, ba, wx, bx, lam, T):
    S = xr.shape[0]
    nt = S // T

    def body(u_ref, uh_ref, g_ref, cw_ref, cb_ref, wa_ref, ba_ref, wx_ref, bx_ref, lam_ref, h_ref, y_ref, carry):
        t = pl.program_id(1)

        @pl.when(t == 0)
        def _():
            carry[...] = jnp.zeros_like(carry)

        row = lax.broadcasted_iota(jnp.int32, (T, RNN_BLOCK), 0)
        grow = row + t * T
        head8 = jnp.where(t > 0, uh_ref[...], 0.0)
        _, conv, _, _, i_, _, a, _, mult = _rglru_gates(u_ref[...], head8, grow, row, T, cw_ref, cb_ref, wa_ref, ba_ref,
                                                         wx_ref, bx_ref, lam_ref)
        b = mult * i_ * conv
        s = 1
        while s < T:
            keep = row >= s
            a_s = jnp.where(keep, pltpu.roll(a, s, 0), 1.0)
            b_s = jnp.where(keep, pltpu.roll(b, s, 0), 0.0)
            b = a * b_s + b
            a = a * a_s
            s *= 2
        h = b + a * carry[0:1, :]
        carry[...] = jnp.broadcast_to(h[T - 1:T, :], carry.shape)
        h_ref[...] = h
        gv = g_ref[...]
        y_ref[...] = (h * (gv * _sigmoid(gv))).astype(BF)

    tile, halo, vec, cwspec, wblk = _rglru_specs(T, nt, False)
    return pl.pallas_call(
        body, name="rglru_fwd", grid=(RNN_BLOCKS, nt),
        in_specs=[tile, halo, tile, cwspec, vec, wblk, vec, wblk, vec, vec],
        out_specs=[tile, tile],
        out_shape=[jax.ShapeDtypeStruct((S, D_RNN), F32), jax.ShapeDtypeStruct((S, D_RNN), BF)],
        scratch_shapes=[pltpu.VMEM((8, RNN_BLOCK), F32)],
        compiler_params=_params(("parallel", "arbitrary")),
    )(xr, xr, g, cw, cb, wa, ba, wx, bx, lam)


def _rglru_bwd(xr, g, h, dy, cw, cb, wa, ba, wx, bx, lam, T):
    S = xr.shape[0]
    nt = S // T

    def body(u_ref, uh_ref, g_ref, h_ref, hh_ref, dy_ref, cw_ref, cb_ref, wa_ref, ba_ref, wx_ref, bx_ref, lam_ref,
             du_ref, dg_ref, dwa_ref, dwx_ref, dvec_ref, c_dhh, c_a, c_dconv):
        t = pl.program_id(1)
        tt = nt - 1 - t

        @pl.when(t == 0)
        def _():
            c_dhh[...] = jnp.zeros_like(c_dhh)
            c_a[...] = jnp.zeros_like(c_a)
            c_dconv[...] = jnp.zeros_like(c_dconv)
            dwa_ref[...] = jnp.zeros_like(dwa_ref)
            dwx_ref[...] = jnp.zeros_like(dwx_ref)
            dvec_ref[...] = jnp.zeros_like(dvec_ref)

        row = lax.broadcasted_iota(jnp.int32, (T, RNN_BLOCK), 0)
        row8 = row[:8]
        grow = row + tt * T
        head8 = jnp.where(tt > 0, uh_ref[...], 0.0)
        us, conv, cbf, r_, i_, sp, a, mult_raw, mult = _rglru_gates(
            u_ref[...], head8, grow, row, T, cw_ref, cb_ref, wa_ref, ba_ref, wx_ref, bx_ref, lam_ref)
        hv = h_ref[...]
        hprev = _shift_down(hv, 1, jnp.where(tt > 0, hh_ref[...], 0.0), row, T)
        gv = g_ref[...]
        sg = _sigmoid(gv)
        dyv = dy_ref[...]
        dg_ref[...] = (dyv * hv * (sg * (1.0 + gv * (1.0 - sg)))).astype(BF)
        d = dyv * (gv * sg)
        A = _shift_up(a, 1, c_a[...], row, T)
        s = 1
        while s < T:
            keep = row < T - s
            A_s = jnp.where(keep, pltpu.roll(A, T - s, 0), 1.0)
            d_s = jnp.where(keep, pltpu.roll(d, T - s, 0), 0.0)
            d = A * d_s + d
            A = A * A_s
            s *= 2
        dhh = d + A * c_dhh[0:1, :]
        da = dhh * hprev
        dconv = dhh * mult * i_
        di = dhh * mult * conv
        dmult = dhh * i_ * conv
        dla = da * a - jnp.where(grow == 0, 0.0, dmult * (a * a) / mult_raw)
        dr = dla * (-LRU_C * sp)
        dsp = jnp.sum(dla * (-LRU_C * r_), axis=0, keepdims=True)
        dza = dr * r_ * (1.0 - r_)
        dzx = di * i_ * (1.0 - i_)
        dza_b, dzx_b = dza.astype(BF), dzx.astype(BF)
        dconv = dconv + _dot(dza_b, wa_ref[0], NT) + _dot(dzx_b, wx_ref[0], NT)
        dwa_ref[0] += _dot(cbf, dza_b, TN)
        dwx_ref[0] += _dot(cbf, dzx_b, TN)
        lam = lam_ref[...]
        rows = [jnp.sum(dconv * us[k], axis=0, keepdims=True) for k in range(CONV_W)]
        rows += [jnp.sum(dconv, axis=0, keepdims=True), jnp.sum(dza, axis=0, keepdims=True),
                 jnp.sum(dzx, axis=0, keepdims=True), dsp * (-_sigmoid(-lam))]
        upd = jnp.zeros((8, RNN_BLOCK), F32)
        for j, rv in enumerate(rows):
            upd = upd + jnp.where(row8 == j, rv, 0.0)
        dvec_ref[...] += upd
        tail8 = c_dconv[...]
        du = dconv * cw_ref[0:1, :]
        for k in range(1, CONV_W):
            du = du + _shift_up(dconv, k, tail8, row, T) * cw_ref[k:k + 1, :]
        du_ref[...] = du.astype(BF)
        c_dhh[...] = jnp.broadcast_to(dhh[0:1, :], c_dhh.shape)
        c_a[...] = jnp.broadcast_to(a[0:1, :], c_a.shape)
        c_dconv[...] = dconv[:8]

    tile, halo, vec, cwspec, wblk = _rglru_specs(T, nt, True)
    acc8 = pl.BlockSpec((8, RNN_BLOCK), lambda n, t: (0, n))
    return pl.pallas_call(
        body, name="rglru_bwd", grid=(RNN_BLOCKS, nt),
        in_specs=[tile, halo, tile, tile, halo, tile, cwspec, vec, wblk, vec, wblk, vec, vec],
        out_specs=[tile, tile, wblk, wblk, acc8],
        out_shape=[jax.ShapeDtypeStruct((S, D_RNN), BF), jax.ShapeDtypeStruct((S, D_RNN), BF),
                   jax.ShapeDtypeStruct((RNN_BLOCKS, RNN_BLOCK, RNN_BLOCK), F32),
                   jax.ShapeDtypeStruct((RNN_BLOCKS, RNN_BLOCK, RNN_BLOCK), F32),
                   jax.ShapeDtypeStruct((8, D_RNN), F32)],
        scratch_shapes=[pltpu.VMEM((8, RNN_BLOCK), F32)] * 3,
        compiler_params=_params(("parallel", "arbitrary")),
    )(xr, xr, g, h, h, dy, cw, cb, wa, ba, wx, bx, lam)


def _rel_bucket_map():
    qi = np.arange(WINDOW)[:, None]
    kj = np.arange(2 * WINDOW)[None, :]
    dist = jnp.asarray(qi + WINDOW - kj, jnp.int32)
    n = jnp.maximum(dist, 0)
    max_exact = REL_BUCKETS // 2
    ratio = jnp.log(jnp.maximum(n, 1).astype(F32) / max_exact) / math.log(REL_MAX_DIST / max_exact)
    large = jnp.minimum(max_exact + (ratio * (REL_BUCKETS - max_exact)).astype(jnp.int32), REL_BUCKETS - 1)
    bucket = jnp.where(n < max_exact, n, large).astype(jnp.int32)
    j = np.arange(WINDOW)[None, :]
    return jnp.where(jnp.asarray(j > qi), bucket[:, :WINDOW], bucket[:, WINDOW:])


def _swa_common(n, kv_ref, bucket_ref, relb_ref, bias_scr):
    @pl.when(n == 0)
    def _():
        bk = bucket_ref[...]
        for h in range(SWA_HEADS):
            acc = jnp.zeros((WINDOW, WINDOW), F32)
            for b in range(REL_BUCKETS):
                acc = acc + jnp.where(bk == b, relb_ref[b, h], 0.0)
            bias_scr[h] = acc

    prev0 = pl.multiple_of(jnp.maximum(n - 1, 0) * WINDOW, WINDOW)
    cur0 = pl.multiple_of(n * WINDOW, WINDOW)
    kk = jnp.concatenate([kv_ref[pl.ds(prev0, WINDOW), :], kv_ref[pl.ds(cur0, WINDOW), :]], axis=0).astype(F32)
    rowi = lax.broadcasted_iota(jnp.int32, (WINDOW, WINDOW), 0)
    col = lax.broadcasted_iota(jnp.int32, (WINDOW, WINDOW), 1)
    from_prev = col > rowi
    return kk, from_prev, prev0, cur0


def _fold(full, from_prev):
    return jnp.where(from_prev, full[:, :WINDOW], full[:, WINDOW:])


def _unfold(sq, from_prev):
    return jnp.concatenate([jnp.where(from_prev, sq, 0.0), jnp.where(from_prev, 0.0, sq)], axis=1)


def _half_pair(part, kvh):
    lo = lax.broadcasted_iota(jnp.int32, part.shape, 1) < SWA_HD
    if kvh == 0:
        pa = jnp.where(lo, part, 0.0)
        pb = pltpu.roll(pa, SWA_HD, 1)
    else:
        pb = jnp.where(lo, 0.0, part)
        pa = pltpu.roll(pb, SWA_HD, 1)
    return pa.astype(BF), pb.astype(BF)


ALL_HEADS = SWA_HEADS * WINDOW


def _sink_column(sinks):
    return jnp.repeat(sinks.reshape(SWA_HEADS), WINDOW).reshape(ALL_HEADS, 1)


def _swa_operands(kk):
    return [(_half_pair(kk[:, :128], kvh), _half_pair(kk[:, 128:], kvh)) for kvh in range(SWA_KV_HEADS)]


def _swa_probs(n, q_ref, ops, bias_scr, sinkc_ref, from_prev):
    lgs = []
    for kvh in range(SWA_KV_HEADS):
        (ka, kb), _ = ops[kvh]
        for p in range(4):
            q2 = q_ref[:, kvh * 512 + p * 128:kvh * 512 + p * 128 + 128]
            lgs += [_fold(_dot(q2, ka, NT), from_prev), _fold(_dot(q2, kb, NT), from_prev)]
    lg = jnp.concatenate(lgs, axis=0) * (SWA_HD ** -0.5) + bias_scr[...].reshape(ALL_HEADS, WINDOW)
    rowi = jnp.bitwise_and(lax.broadcasted_iota(jnp.int32, (ALL_HEADS, WINDOW), 0), WINDOW - 1)
    col = lax.broadcasted_iota(jnp.int32, (ALL_HEADS, WINDOW), 1)
    no_prev = jnp.where(n > 0, 0, 4 * WINDOW)
    lg = jnp.where(jnp.logical_or(col <= rowi, col > rowi + no_prev), lg, NEG_INF)
    sink = sinkc_ref[...]
    m = jnp.maximum(jnp.max(lg, axis=-1, keepdims=True), sink)
    e = jnp.exp(lg - m)
    es = jnp.exp(sink - m)
    den = jnp.sum(e, axis=-1, keepdims=True) + es
    return e / den, es / den


def _swa_fwd(q, kv, g, bucket, rel_bias, sink_col):
    S = q.shape[0]
    nb = S // WINDOW

    def body(q_ref, kv_ref, g_ref, bucket_ref, relb_ref, sinkc_ref, o_ref, y_ref, bias_scr):
        n = pl.program_id(0)
        kk, from_prev, _, _ = _swa_common(n, kv_ref, bucket_ref, relb_ref, bias_scr)
        ops = _swa_operands(kk)
        pr, _ = _swa_probs(n, q_ref, ops, bias_scr, sinkc_ref, from_prev)
        for kvh in range(SWA_KV_HEADS):
            _, (va, vb) = ops[kvh]
            for p in range(4):
                c0 = kvh * 512 + p * 128
                r0 = (kvh * 8 + 2 * p) * WINDOW
                o2 = (_dot(_unfold(pr[r0:r0 + WINDOW], from_prev).astype(BF), va, NN)
                      + _dot(_unfold(pr[r0 + WINDOW:r0 + 2 * WINDOW], from_prev).astype(BF), vb, NN))
                o_ref[:, c0:c0 + 128] = o2
                gv = g_ref[:, c0:c0 + 128]
                y_ref[:, c0:c0 + 128] = (o2 * (gv * _sigmoid(gv))).astype(BF)

    blk = pl.BlockSpec((WINDOW, 1024), lambda n: (n, 0))
    smem = pl.BlockSpec(memory_space=pltpu.SMEM)
    sinkc = pl.BlockSpec((ALL_HEADS, 1), lambda n: (0, 0))
    return pl.pallas_call(
        body, name="swa_fwd", grid=(nb,),
        in_specs=[blk, pl.BlockSpec((S, 256), lambda n: (0, 0)), blk, pl.BlockSpec((WINDOW, WINDOW), lambda n: (0, 0)), smem, sinkc],
        out_specs=[blk, blk],
        out_shape=[jax.ShapeDtypeStruct((S, 1024), F32), jax.ShapeDtypeStruct((S, 1024), BF)],
        scratch_shapes=[pltpu.VMEM((SWA_HEADS, WINDOW, WINDOW), F32)],
        compiler_params=_params(("arbitrary",)),
    )(q, kv, g, bucket, rel_bias, sink_col)


def _swa_bwd(q, kv, g, o, dy, bucket, rel_bias, sink_col, others):
    S = q.shape[0]
    nb = S // WINDOW
    first_col = {name: c0 for name, c0, _, _ in SEGMENTS}
    col_q, col_g = first_col["q_s"], first_col["g_swa"]
    copies = []
    for name, c0, width, _ in SEGMENTS:
        if name not in ("q_s", "kv", "g_swa"):
            arrs = others[name] if name == "gl" else (others[name],)
            copies += [(a, c0 + i * (width // len(arrs))) for i, a in enumerate(arrs)]

    def body(q_ref, kv_ref, g_ref, o_ref, dy_ref, bucket_ref, relb_ref, sinkc_ref, *refs):
        copy_refs = refs[:len(copies)]
        dp_ref, dkv_ref, dsink_ref, drel_ref, bias_scr, dbias_scr, dsink_scr = refs[len(copies):]
        n = pl.program_id(0)
        for c_ref, (a, c0) in zip(copy_refs, copies):
            dp_ref[:, c0:c0 + a.shape[1]] = c_ref[...]

        @pl.when(n == 0)
        def _():
            dbias_scr[...] = jnp.zeros_like(dbias_scr)
            dsink_scr[...] = jnp.zeros_like(dsink_scr)
            dkv_ref[...] = jnp.zeros_like(dkv_ref)

        kk, from_prev, prev0, cur0 = _swa_common(n, kv_ref, bucket_ref, relb_ref, bias_scr)
        ops = _swa_operands(kk)
        pr, ps = _swa_probs(n, q_ref, ops, bias_scr, sinkc_ref, from_prev)
        do2s, dps = [], []
        for kvh in range(SWA_KV_HEADS):
            _, (va, vb) = ops[kvh]
            for p in range(4):
                c0 = kvh * 512 + p * 128
                gv = g_ref[:, c0:c0 + 128]
                sg = _sigmoid(gv)
                dyv = dy_ref[:, c0:c0 + 128]
                dp_ref[:, col_g + c0:col_g + c0 + 128] = (dyv * o_ref[:, c0:c0 + 128] * (sg * (1.0 + gv * (1.0 - sg)))).astype(BF)
                do2 = (dyv * (gv * sg)).astype(BF)
                do2s.append(do2)
                dps += [_fold(_dot(do2, va, NT), from_prev), _fold(_dot(do2, vb, NT), from_prev)]
        dp = jnp.concatenate(dps, axis=0)
        delta = jnp.sum(pr * dp, axis=-1, keepdims=True)
        ds = pr * (dp - delta)
        dbias_scr[...] += ds.reshape(SWA_HEADS, WINDOW, WINDOW)
        dsink_scr[...] += ps * delta
        dsc = ds * (SWA_HD ** -0.5)
        lo256 = lax.broadcasted_iota(jnp.int32, (2 * WINDOW, 128), 1) < SWA_HD
        dks, dvs = [], []
        for kvh in range(SWA_KV_HEADS):
            (ka, kb), _ = ops[kvh]
            dka = jnp.zeros((2 * WINDOW, 128), F32)
            dkb, dva, dvb = dka, dka, dka
            for p in range(4):
                c0 = kvh * 512 + p * 128
                r0 = (kvh * 8 + 2 * p) * WINDOW
                q2 = q_ref[:, c0:c0 + 128]
                do2 = do2s[kvh * 4 + p]
                ds0 = _unfold(dsc[r0:r0 + WINDOW], from_prev).astype(BF)
                ds1 = _unfold(dsc[r0 + WINDOW:r0 + 2 * WINDOW], from_prev).astype(BF)
                dp_ref[:, col_q + c0:col_q + c0 + 128] = (_dot(ds0, ka, NN) + _dot(ds1, kb, NN)).astype(BF)
                dka = dka + _dot(ds0, q2, TN)
                dkb = dkb + _dot(ds1, q2, TN)
                dva = dva + _dot(_unfold(pr[r0:r0 + WINDOW], from_prev).astype(BF), do2, TN)
                dvb = dvb + _dot(_unfold(pr[r0 + WINDOW:r0 + 2 * WINDOW], from_prev).astype(BF), do2, TN)
            dks.append(jnp.where(lo256, dka, 0.0) + pltpu.roll(jnp.where(lo256, 0.0, dkb), SWA_HD, 1))
            dvs.append(jnp.where(lo256, dva, 0.0) + pltpu.roll(jnp.where(lo256, 0.0, dvb), SWA_HD, 1))
        dk = dks[0] + pltpu.roll(dks[1], SWA_HD, 1)
        dv = dvs[0] + pltpu.roll(dvs[1], SWA_HD, 1)
        dkv_ref[pl.ds(prev0, WINDOW), 0:128] += dk[:WINDOW]
        dkv_ref[pl.ds(prev0, WINDOW), 128:256] += dv[:WINDOW]
        dkv_ref[pl.ds(cur0, WINDOW), 0:128] += dk[WINDOW:]
        dkv_ref[pl.ds(cur0, WINDOW), 128:256] += dv[WINDOW:]

        @pl.when(n == nb - 1)
        def _():
            dsink_ref[...] = -jnp.sum(dsink_scr[...].reshape(SWA_HEADS, WINDOW, 1), axis=1)
            bk = bucket_ref[...]
            sums = []
            for b in range(REL_BUCKETS):
                sums.append(jnp.sum(jnp.where((bk == b)[None], dbias_scr[...], 0.0), axis=1))
            drel_ref[...] = jnp.sum(jnp.concatenate(sums, axis=0), axis=1, keepdims=True)

    blk = pl.BlockSpec((WINDOW, 1024), lambda n: (n, 0))
    smem = pl.BlockSpec(memory_space=pltpu.SMEM)
    whole = lambda shape: pl.BlockSpec(shape, lambda n: (0, 0))
    return pl.pallas_call(
        body, name="swa_bwd", grid=(nb,),
        in_specs=[blk, whole((S, 256)), blk, blk, blk, whole((WINDOW, WINDOW)), smem, whole((ALL_HEADS, 1))]
        + [pl.BlockSpec((WINDOW, a.shape[1]), lambda n: (n, 0)) for a, _ in copies],
        out_specs=[pl.BlockSpec((WINDOW, D_IN), lambda n: (n, 0)), whole((S, 256)), whole((SWA_HEADS, 1)),
                   whole((REL_BUCKETS * SWA_HEADS, 1))],
        out_shape=[jax.ShapeDtypeStruct((S, D_IN), BF), jax.ShapeDtypeStruct((S, 256), F32), jax.ShapeDtypeStruct((SWA_HEADS, 1), F32),
                   jax.ShapeDtypeStruct((REL_BUCKETS * SWA_HEADS, 1), F32)],
        scratch_shapes=[pltpu.VMEM((SWA_HEADS, WINDOW, WINDOW), F32), pltpu.VMEM((SWA_HEADS, WINDOW, WINDOW), F32),
                        pltpu.VMEM((ALL_HEADS, 1), F32)],
        compiler_params=_params(("arbitrary",)),
    )(q, kv, g, o, dy, bucket, rel_bias, sink_col, *[a for a, _ in copies])


MEM_TQ = 512


def _mem_probs(qh, mk):
    lg = _dot(qh, mk, NT) * (MEM_HD ** -0.5)
    e = jnp.exp(lg - jnp.max(lg, axis=-1, keepdims=True))
    return e / jnp.sum(e, axis=-1, keepdims=True)


def _mem_fwd(q, mkv, g):
    S = q.shape[0]
    M = mkv.shape[0]
    tq = min(S, MEM_TQ)

    def body(q_ref, mkv_ref, g_ref, o_ref, y_ref):
        for h in range(MEM_HEADS):
            c0 = h * MEM_HD
            pr = _mem_probs(q_ref[:, c0:c0 + MEM_HD], mkv_ref[:, c0:c0 + MEM_HD])
            o = _dot(pr.astype(BF), mkv_ref[:, D_MEM + c0:D_MEM + c0 + MEM_HD], NN)
            o_ref[:, c0:c0 + MEM_HD] = o
            gv = g_ref[:, c0:c0 + MEM_HD]
            y_ref[:, c0:c0 + MEM_HD] = (o * (gv * _sigmoid(gv))).astype(BF)

    blk = pl.BlockSpec((tq, D_MEM), lambda i: (i, 0))
    return pl.pallas_call(
        body, name="mem_fwd", grid=(S // tq,),
        in_specs=[blk, pl.BlockSpec((M, 2 * D_MEM), lambda i: (0, 0)), blk], out_specs=[blk, blk],
        out_shape=[jax.ShapeDtypeStruct((S, D_MEM), F32), jax.ShapeDtypeStruct((S, D_MEM), BF)],
        compiler_params=_params(("parallel",)),
    )(q, mkv, g)


def _mem_bwd(q, mkv, g, o, dy):
    S = q.shape[0]
    M = mkv.shape[0]
    tq = min(S, MEM_TQ)

    def body(q_ref, mkv_ref, g_ref, o_ref, dy_ref, dq_ref, dg_ref, dmkv_ref):
        @pl.when(pl.program_id(0) == 0)
        def _():
            dmkv_ref[...] = jnp.zeros_like(dmkv_ref)

        for h in range(MEM_HEADS):
            c0 = h * MEM_HD
            qh = q_ref[:, c0:c0 + MEM_HD]
            mk = mkv_ref[:, c0:c0 + MEM_HD]
            mv = mkv_ref[:, D_MEM + c0:D_MEM + c0 + MEM_HD]
            gv = g_ref[:, c0:c0 + MEM_HD]
            sg = _sigmoid(gv)
            dyv = dy_ref[:, c0:c0 + MEM_HD]
            dg_ref[:, c0:c0 + MEM_HD] = (dyv * o_ref[:, c0:c0 + MEM_HD] * (sg * (1.0 + gv * (1.0 - sg)))).astype(BF)
            do = (dyv * (gv * sg)).astype(BF)
            pr = _mem_probs(qh, mk)
            dp = _dot(do, mv, NT)
            ds = pr * (dp - jnp.sum(pr * dp, axis=-1, keepdims=True))
            dsb = (ds * (MEM_HD ** -0.5)).astype(BF)
            dq_ref[:, c0:c0 + MEM_HD] = _dot(dsb, mk, NN).astype(BF)
            dmkv_ref[:, c0:c0 + MEM_HD] += _dot(dsb, qh, TN)
            dmkv_ref[:, D_MEM + c0:D_MEM + c0 + MEM_HD] += _dot(pr.astype(BF), do, TN)

    blk = pl.BlockSpec((tq, D_MEM), lambda i: (i, 0))
    whole = pl.BlockSpec((M, 2 * D_MEM), lambda i: (0, 0))
    return pl.pallas_call(
        body, name="mem_bwd", grid=(S // tq,),
        in_specs=[blk, whole, blk, blk, blk], out_specs=[blk, blk, whole],
        out_shape=[jax.ShapeDtypeStruct((S, D_MEM), BF), jax.ShapeDtypeStruct((S, D_MEM), BF),
                   jax.ShapeDtypeStruct((M, 2 * D_MEM), F32)],
        compiler_params=_params(("arbitrary",)),
    )(q, mkv, g, o, dy)


MERGE_TN = 512
MERGE_TM = 1024
MERGE_BWD_TN = 256


def _merge_specs(tm, tn):
    ytile = pl.BlockSpec((tm, 1024), lambda i, j: (i, 0))
    wblk = pl.BlockSpec((tn, 1024), lambda i, j: (j, 0))
    gls = [pl.BlockSpec((None, tm, tn), (lambda i, j, br=br: (br, i, j))) for br in range(3)]
    otile = pl.BlockSpec((tm, tn), lambda i, j: (i, j))
    return ytile, wblk, gls, otile


def _merge_fwd(ys, ws, gl, tm):
    S = gl.shape[1]

    def body(y0, y1, y2, w0, w1, w2, g0, g1, g2, o_ref, p0, p1, p2):
        acc = None
        for y_ref, w_ref, g_ref, p_ref in ((y0, w0, g0, p0), (y1, w1, g1, p1), (y2, w2, g2, p2)):
            pv = _dot(y_ref[...], w_ref[...], NT)
            p_ref[...] = pv.astype(BF)
            term = _sigmoid(g_ref[...]) * pv
            acc = term if acc is None else acc + term
        o_ref[...] = acc.astype(BF)

    ytile, wblk, gls, otile = _merge_specs(tm, MERGE_TN)
    return pl.pallas_call(
        body, name="merge_fwd", grid=(S // tm, D_MODEL // MERGE_TN),
        in_specs=[ytile] * 3 + [wblk] * 3 + gls, out_specs=[otile] * 4,
        out_shape=[jax.ShapeDtypeStruct((S, D_MODEL), BF)] * 4,
        compiler_params=_params(("parallel", "arbitrary")),
    )(*ys, *ws, gl, gl, gl)


def _merge_bwd(dout, w_out, ps, gl, tm):
    S = gl.shape[1]

    def body(do_ref, wo_ref, p0, p1, p2, g0, g1, g2, dg0, dg1, dg2, dp0, dp1, dp2):
        dm = _dot(do_ref[...], wo_ref[...], NT)
        for p_ref, g_ref, dg_ref, dp_ref in ((p0, g0, dg0, dp0), (p1, g1, dg1, dp1), (p2, g2, dg2, dp2)):
            gate = _sigmoid(g_ref[...])
            dg_ref[...] = (dm * p_ref[...].astype(F32) * gate * (1.0 - gate)).astype(BF)
            dp_ref[...] = (dm * gate).astype(BF)

    _, _, gls, otile = _merge_specs(tm, MERGE_BWD_TN)
    out = jax.ShapeDtypeStruct((S, D_MODEL), BF)
    return pl.pallas_call(
        body, name="merge_bwd", grid=(S // tm, D_MODEL // MERGE_BWD_TN),
        in_specs=[pl.BlockSpec((tm, D_MODEL), lambda i, j: (i, 0)), pl.BlockSpec((MERGE_BWD_TN, D_MODEL), lambda i, j: (j, 0))]
        + [otile] * 3 + gls,
        out_specs=[otile] * 6, out_shape=[out] * 6,
        compiler_params=_params(("parallel", "arbitrary")),
    )(dout, w_out, *ps, gl, gl, gl)


def _dy_branches(dps, ws, tm):
    S = dps[0].shape[0]
    n = S // tm

    def body(*refs):
        b = pl.program_id(0)
        for k in range(3):
            @pl.when(b == k)
            def _(k=k):
                refs[6 + k][...] = _dot(refs[k][...], refs[3 + k][...], NN)

    def rows(k):
        return lambda b, i: (jnp.where(b == k, i, jnp.where(b > k, n - 1, 0)), 0)

    return pl.pallas_call(
        body, name="dy_br", grid=(3, n),
        in_specs=[pl.BlockSpec((tm, D_MODEL), rows(k)) for k in range(3)] + [pl.BlockSpec((D_MODEL, 1024), lambda b, i: (0, 0))] * 3,
        out_specs=[pl.BlockSpec((tm, 1024), rows(k)) for k in range(3)],
        out_shape=[jax.ShapeDtypeStruct((S, 1024), F32)] * 3,
        compiler_params=_params(("arbitrary", "arbitrary"), large=True),
    )(*dps, *ws)


def _out_loss(merged, w_out, x, target, post_g, tm):
    S = x.shape[0]

    def body(m_ref, w_ref, x_ref, t_ref, g_ref, dout_ref, dy_ref, loss_ref, dpost_ref):
        @pl.when(pl.program_id(0) == 0)
        def _():
            loss_ref[...] = jnp.zeros_like(loss_ref)
            dpost_ref[...] = jnp.zeros_like(dpost_ref)

        out = _dot(m_ref[...], w_ref[...], NN)
        r = lax.rsqrt(jnp.mean(out * out, axis=-1, keepdims=True) + EPS)
        nrm = out * r
        gv = g_ref[...]
        err = (x_ref[...] + nrm * gv) - t_ref[...]
        sq = jnp.sum(jnp.sum(err * err, axis=1, keepdims=True), axis=0, keepdims=True)
        loss_ref[...] += sq * (0.5 / D_MODEL)
        dy = err * (1.0 / D_MODEL)
        dy_ref[...] = dy
        dpost_ref[...] += jnp.sum(dy * nrm, axis=0, keepdims=True)
        dn = dy * gv
        dout_ref[...] = (r * (dn - nrm * jnp.mean(dn * nrm, axis=-1, keepdims=True))).astype(BF)

    row = pl.BlockSpec((tm, D_MODEL), lambda i: (i, 0))
    return pl.pallas_call(
        body, name="out_loss", grid=(S // tm,),
        in_specs=[row, pl.BlockSpec((D_MODEL, D_MODEL), lambda i: (0, 0)), row, row, pl.BlockSpec((1, D_MODEL), lambda i: (0, 0))],
        out_specs=[row, row, pl.BlockSpec((8, 128), lambda i: (0, 0)), pl.BlockSpec((1, D_MODEL), lambda i: (0, 0))],
        out_shape=[jax.ShapeDtypeStruct((S, D_MODEL), BF), jax.ShapeDtypeStruct((S, D_MODEL), F32),
                   jax.ShapeDtypeStruct((8, 128), F32), jax.ShapeDtypeStruct((1, D_MODEL), F32)],
        compiler_params=_params(("arbitrary",)),
    )(merged, w_out, x, target, post_g)


DH_DX_CHUNK = 64


def _dh_dx(dproj, w_in, x, dy, pre_g, tm, tk):
    S = x.shape[0]
    nk = D_IN // tk

    def body(dp_ref, w_ref, x_ref, dy_ref, g_ref, dx_ref, dpre_ref, acc_ref):
        i, k = pl.program_id(0), pl.program_id(1)

        @pl.when(jnp.logical_and(i == 0, k == 0))
        def _():
            dpre_ref[...] = jnp.zeros_like(dpre_ref)

        @pl.when(k == 0)
        def _():
            acc_ref[...] = jnp.zeros_like(acc_ref)

        acc_ref[...] += _dot(dp_ref[...], w_ref[...], NN)

        @pl.when(k == nk - 1)
        def _():
            def chunk(c, carry):
                rows = pl.ds(pl.multiple_of(c * DH_DX_CHUNK, DH_DX_CHUNK), DH_DX_CHUNK)
                dh = acc_ref[rows, :]
                xv = x_ref[rows, :]
                r = lax.rsqrt(jnp.mean(xv * xv, axis=-1, keepdims=True) + EPS)
                nrm = xv * r
                dpre_ref[...] += jnp.sum(dh * nrm, axis=0, keepdims=True)
                dn = dh * g_ref[...]
                dx_ref[rows, :] = r * (dn - nrm * jnp.mean(dn * nrm, axis=-1, keepdims=True)) + dy_ref[rows, :]
                return carry
            lax.fori_loop(0, tm // DH_DX_CHUNK, chunk, 0)

    row = pl.BlockSpec((tm, D_MODEL), lambda i, k: (i, 0))
    vec = pl.BlockSpec((1, D_MODEL), lambda i, k: (0, 0))
    return pl.pallas_call(
        body, name="dh_dx", grid=(S // tm, nk),
        in_specs=[pl.BlockSpec((tm, tk), lambda i, k: (i, k)), pl.BlockSpec((tk, D_MODEL), lambda i, k: (k, 0)), row, row, vec],
        out_specs=[row, vec],
        out_shape=[jax.ShapeDtypeStruct((S, D_MODEL), F32), jax.ShapeDtypeStruct((1, D_MODEL), F32)],
        scratch_shapes=[pltpu.VMEM((tm, D_MODEL), F32)],
        compiler_params=_params(("arbitrary", "arbitrary"), large=True),
    )(dproj, w_in, x, dy, pre_g)


def _sum_parts(parts, name):
    P, R, C = parts.shape
    tr = max(t for t in range(8, 513, 8) if R % t == 0)

    def body(p_ref, o_ref):
        acc = p_ref[0]
        for j in range(1, P):
            acc = acc + p_ref[j]
        o_ref[...] = acc

    return pl.pallas_call(
        body, name=name, grid=(R // tr,),
        in_specs=[pl.BlockSpec((P, tr, C), lambda i: (0, i, 0))], out_specs=pl.BlockSpec((tr, C), lambda i: (i, 0)),
        out_shape=jax.ShapeDtypeStruct((R, C), F32), compiler_params=_params(("parallel",)),
    )(parts)


def _adamw(land, sums, chip, w, m, v, name, group=(0, 1), into=None):
    q, n_groups = group
    _, R, cols = land.shape
    C = cols * n_groups
    tr = max(t for t in range(16, 257, 16) if R % t == 0)
    c1 = 1.0 - ADAM_B1 ** ADAM_STEP
    c2 = 1.0 - ADAM_B2 ** ADAM_STEP
    n_into = 0 if into is None else 4

    def body(chip_ref, p0_ref, p1_ref, p2_ref, own_ref, w_ref, m_ref, v_ref, *refs):
        g_ref, d_ref, nm_ref, nv_ref = refs[n_into:]
        g = own_ref[...].astype(F32)
        for p_ref in (p0_ref, p1_ref, p2_ref):
            g = g + p_ref[...].astype(F32)
        nm = ADAM_B1 * m_ref[...] + (1.0 - ADAM_B1) * g
        nv = ADAM_B2 * v_ref[...] + (1.0 - ADAM_B2) * (g * g)
        g_ref[...] = g
        nm_ref[...] = nm
        nv_ref[...] = nv
        d_ref[...] = -ADAM_LR * ((nm / c1) / (jnp.sqrt(nv / c2) + ADAM_EPS) + ADAM_WD * w_ref[...])

    tile = pl.BlockSpec((None, tr, cols), lambda i, c_ref: (0, i, q))
    specs = [pl.BlockSpec((None, tr, cols), (lambda i, c_ref, k=k: (k + (c_ref[0] <= k).astype(jnp.int32), i, 0))) for k in range(3)]
    specs.append(pl.BlockSpec((None, tr, cols), (lambda i, c_ref: (c_ref[0], i, 0))))
    return pl.pallas_call(
        body, name=name,
        grid_spec=pltpu.PrefetchScalarGridSpec(num_scalar_prefetch=1, grid=(R // tr,),
                                               in_specs=specs + [tile, tile, tile] + [pl.BlockSpec(memory_space=pl.ANY)] * n_into,
                                               out_specs=[tile] * 4),
        out_shape=[jax.ShapeDtypeStruct((1, R, C), F32)] * 4,
        input_output_aliases={8 + k: k for k in range(n_into)},
        compiler_params=_params(("parallel",)),
    )(chip, land, land, land, sums, w, m, v, *(into or []))


def _adamw_small(gs, ws, ms, vs):
    n = len(ws)
    c1 = 1.0 - ADAM_B1 ** ADAM_STEP
    c2 = 1.0 - ADAM_B2 ** ADAM_STEP

    def flat2(a):
        return a.reshape(-1, a.shape[-1])

    def body(*refs):
        ins, outs = refs[:4 * n], refs[4 * n:]
        for a in range(n):
            g, w, m, v = (ins[k * n + a][...] for k in range(4))
            nm = ADAM_B1 * m + (1.0 - ADAM_B1) * g
            nv = ADAM_B2 * v + (1.0 - ADAM_B2) * (g * g)
            outs[a][...] = g
            outs[n + a][...] = -ADAM_LR * ((nm / c1) / (jnp.sqrt(nv / c2) + ADAM_EPS) + ADAM_WD * w)
            outs[2 * n + a][...] = nm
            outs[3 * n + a][...] = nv

    shapes = [flat2(w).shape for w in ws]
    out = pl.pallas_call(
        body, name="adamw_small", out_shape=[jax.ShapeDtypeStruct(sh, F32) for sh in shapes] * 4,
        compiler_params=_params(),
    )(*[g.reshape(sh) for g, sh in zip(gs, shapes)], *[flat2(a) for a in (*ws, *ms, *vs)])
    return [[out[k * n + a].reshape(ws[a].shape) for a in range(n)] for k in range(4)]


PROJ_ROWS = 512


def _project(h, w_t, dep=None):
    S = h.shape[0]
    n_tiles = D_IN // SEG_TILE
    ranges = [(c0 // SEG_TILE, (c0 + width) // SEG_TILE) for _, c0, width, _ in SEGMENTS]
    dtypes = (F32, BF)
    n_extra = int(dep is not None)

    def of_dtype(j, dt):
        hit = False
        for (j0, j1), seg in zip(ranges, SEGMENTS):
            if seg[3] == dt:
                hit = jnp.logical_and(j >= j0, j < j1) | hit
        return hit

    def body(h_ref, w_ref, *refs):
        outs = refs[n_extra:n_extra + len(SEGMENTS)]
        stages, sems = refs[n_extra + len(SEGMENTS):-1], refs[-1]
        j = pl.program_id(0)
        slot = j % 2

        def copy_out(k, stage, dst):
            return pltpu.make_async_copy(stages[k].at[stage], dst, sems.at[stage])

        def wait_tile(jj, stage):
            for k, dt in enumerate(dtypes):
                o_ref = [o for o, seg in zip(outs, SEGMENTS) if seg[3] == dt and seg[0] != "gl"][0]
                @pl.when(of_dtype(jj, dt))
                def _(k=k, o_ref=o_ref):
                    copy_out(k, stage, o_ref.at[:, pl.ds(0, SEG_TILE)]).wait()

        @pl.when(j >= 2)
        def _():
            wait_tile(j - 2, slot)

        for k, dt in enumerate(dtypes):
            @pl.when(of_dtype(j, dt))
            def _(k=k, dt=dt):
                for c in range(S // PROJ_ROWS):
                    rows = pl.ds(c * PROJ_ROWS, PROJ_ROWS)
                    stages[k][slot, rows, :] = _dot(h_ref[rows, :], w_ref[...], NT).astype(dt)

        for (j0, j1), (name, _, _, dt), o_ref in zip(ranges, SEGMENTS, outs):
            @pl.when(jnp.logical_and(j >= j0, j < j1))
            def _(j0=j0, j1=j1, name=name, dt=dt, o_ref=o_ref):
                t = j - j0
                if name == "gl":
                    per = (j1 - j0) // 3
                    dst = o_ref.at[t // per, :, pl.ds(pl.multiple_of((t % per) * SEG_TILE, SEG_TILE), SEG_TILE)]
                else:
                    dst = o_ref.at[:, pl.ds(pl.multiple_of(t * SEG_TILE, SEG_TILE), SEG_TILE)]
                copy_out(dtypes.index(dt), slot, dst).start()

        @pl.when(j == n_tiles - 1)
        def _():
            wait_tile(j - 1, 1 - slot)
            wait_tile(j, slot)

    out_shapes = [jax.ShapeDtypeStruct((3, S, width // 3) if name == "gl" else (S, width), dt) for name, _, width, dt in SEGMENTS]
    outs = pl.pallas_call(
        body, name="proj", grid=(n_tiles,),
        in_specs=[pl.BlockSpec((S, D_MODEL), lambda j: (0, 0)), pl.BlockSpec((SEG_TILE, D_MODEL), lambda j: (j, 0))]
        + ([] if dep is None else [pl.BlockSpec((8, 128), lambda j: (0, 0))]),
        out_specs=[pl.BlockSpec(memory_space=pl.ANY)] * len(SEGMENTS), out_shape=out_shapes,
        scratch_shapes=[pltpu.VMEM((2, S, SEG_TILE), dt) for dt in dtypes] + [pltpu.SemaphoreType.DMA((2,))],
        compiler_params=_params(("arbitrary",), large=True),
    )(h, w_t, *([] if dep is None else [dep]))
    return {name: o for (name, _, _, _), o in zip(SEGMENTS, outs)}


def _forward_a(h, memn, w_in, sinks, rel_bias, dep=None):
    S = h.shape[0]
    st = dict(T=min(512, S // 2), tm=min(512, S), bucket=_rel_bucket_map(), h=h, memn=memn)
    seg = st["seg"] = _project(h, w_in, dep)
    st["o_swa"], st["y_swa"] = _swa_fwd(seg["q_s"], seg["kv"], seg["g_swa"], st["bucket"], rel_bias, _sink_column(sinks))
    return st


def _forward_rg(st, conv_w, conv_b, w_a, b_a, w_x, b_x, lam):
    seg = st["seg"]
    st["h_rg"], st["y_rg"] = _rglru_fwd(seg["xr"], seg["g_rg"], conv_w, conv_b, w_a, b_a, w_x, b_x, lam, st["T"])
    return st


def _forward_b(st, x, target, post_g, w_memkv, wbr, w_out):
    S = x.shape[0]
    M = st["memn"].shape[0]
    seg = st["seg"]
    st["mkv"] = _matmul(st["memn"], w_memkv, "nn", M, 2 * D_MEM, D_MODEL, M, 512, D_MODEL, BF, "mem_kv")
    st["o_mem"], st["y_mem"] = _mem_fwd(seg["q_m"], st["mkv"], seg["g_mem"])
    st["ys"] = (st["y_rg"], st["y_swa"], st["y_mem"])
    st["merged"], *st["ps"] = _merge_fwd(st["ys"], wbr, seg["gl"], min(S, MERGE_TM))
    st["dout"], st["dy"], st["loss"], st["dpost"] = _out_loss(st["merged"], w_out, x, target, post_g, min(256, S))
    return st


def _backward_a1(st, wbr, w_out):
    S = st["h"].shape[0]
    seg, ys, tm = st["seg"], st["ys"], st["tm"]
    st["dw_out"] = _matmul(st["merged"], st["dout"], "tn", D_MODEL, D_MODEL, S, 256, D_MODEL, S, BF, "dw_out", out_blocked="row")
    dgl0, dgl1, dgl2, dp0, dp1, dp2 = _merge_bwd(st["dout"], w_out, st["ps"], seg["gl"], min(S, MERGE_TM))
    st["dgl"] = (dgl0, dgl1, dgl2)
    dys, dwbr = _dy_branches((dp0, dp1, dp2), wbr, tm), []
    for i, dp in enumerate((dp0, dp1, dp2)):
        dwbr.append(_matmul(ys[i], dp, "tn", 1024, D_MODEL, S, 1024, 256, S, BF, "dw_br%d" % i, out_blocked="col"))
    st["dys"], st["dwbr"] = dys, dwbr
    return st


def _backward_a2(st, mem, w_memkv, conv_w, conv_b, w_a, b_a, w_x, b_x, lam):
    M = mem.shape[0]
    seg, dys = st["seg"], st["dys"]
    st["dq_m"], st["dg_mem"], dmkv = _mem_bwd(seg["q_m"], st["mkv"], seg["g_mem"], st["o_mem"], dys[2])
    st["dmkv"] = dmkv.astype(BF)
    st["dw_memkv"] = _matmul(st["memn"], st["dmkv"], "tn", D_MODEL, 2 * D_MEM, M, 256, 2 * D_MEM, M, BF, "dw_memkv", out_blocked="row")
    st["dxr"], st["dg_rg"], st["dw_a"], st["dw_x"], st["dvec"] = _rglru_bwd(
        seg["xr"], seg["g_rg"], st["h_rg"], dys[0], conv_w, conv_b, w_a, b_a, w_x, b_x, lam, st["T"])
    return st


def _mem_gain_grad(st, mem, w_memkv, dep=None):
    M = mem.shape[0]
    dmemn = _matmul(st["dmkv"], w_memkv, "nt", M, D_MODEL, 2 * D_MEM, M, 512, 2 * D_MEM, F32, "dmemn", dep=dep)
    return _rms_gain_grad(dmemn, mem, "dmem_gain")


def _backward_b(st, rel_bias, sinks):
    seg = st["seg"]
    others = {"xr": st["dxr"], "g_rg": st["dg_rg"], "q_m": st["dq_m"], "g_mem": st["dg_mem"], "gl": st["dgl"]}
    dproj, dkv, dsinks, drel = _swa_bwd(seg["q_s"], seg["kv"], seg["g_swa"], st["o_swa"], st["dys"][1],
                                        st["bucket"], rel_bias, _sink_column(sinks), others)
    st["dsinks"], st["drel"] = dsinks.reshape(1, SWA_HEADS), drel.reshape(REL_BUCKETS, SWA_HEADS)
    col_kv = [c0 for name, c0, _, _ in SEGMENTS if name == "kv"][0]
    st["dproj"] = lax.dynamic_update_slice(dproj, dkv.astype(BF), (0, col_kv))
    return st


def _dw_in_half(st, half, dep=None):
    S = st["h"].shape[0]
    dw = _matmul(st["dproj"], st["h"], "tn", D_IN, D_MODEL // 2, S, D_IN_TILE, D_MODEL // 2, S, BF, "dw_in%d" % half, b_noff=half, dep=dep, out_hbm=True)
    return dw.reshape(N_DEV, D_IN // N_DEV, D_MODEL // 2)


def _owner_blocks(a):
    return jnp.swapaxes(a.reshape((4, 2) + a.shape[1:]), 0, 1)


def _pad_rows(a, rows):
    a = a.reshape(-1, 128) if a.shape[-1] % 128 == 0 else jnp.pad(a, ((0, 0), (0, 128 - a.shape[-1])))
    return jnp.pad(a, ((0, rows - a.shape[0]), (0, 0))) if a.shape[0] < rows else a


def kernel(x, mem, pre_norm_g, post_norm_g, mem_norm_g, w_in, conv_w, conv_b, w_rg_a, b_rg_a, w_rg_x, b_rg_x, lru_lambda, swa_sinks, rel_bias, w_mem_kv, w_br_rg, w_br_swa, w_br_mem, w_out, loss_target, m_pre_norm_g, m_post_norm_g, m_mem_norm_g, m_w_in, m_conv_w, m_conv_b, m_w_rg_a, m_b_rg_a, m_w_rg_x, m_b_rg_x, m_lru_lambda, m_swa_sinks, m_rel_bias, m_w_mem_kv, m_w_br_rg, m_w_br_swa, m_w_br_mem, m_w_out, v_pre_norm_g, v_post_norm_g, v_mem_norm_g, v_w_in, v_conv_w, v_conv_b, v_w_rg_a, v_b_rg_a, v_w_rg_x, v_b_rg_x, v_lru_lambda, v_swa_sinks, v_rel_bias, v_w_mem_kv, v_w_br_rg, v_w_br_swa, v_w_br_mem, v_w_out):
    cx, cy, cc = lax.axis_index("x"), lax.axis_index("y"), lax.axis_index("c")
    me = 4 * cx + 2 * cy + cc
    chip = 2 * cx + cy
    core = jnp.reshape(cc, (1,)).astype(jnp.int32)
    x0, mem0 = x[0], mem[0]
    w_a_b, w_x_b = w_rg_a[0].astype(BF), w_rg_x[0].astype(BF)

    def landing(own, slot, slots):
        return lax.dynamic_update_slice(lax.empty((slots,) + own.shape, own.dtype), own[None], (slot,) + (0,) * own.ndim)


    def swap_start(parts, tag):
        return _exchange_start(parts, [lax.empty((4,) + p.shape[-2:], p.dtype) for p in parts], _plan_swap([p.ndim for p in parts]),
                               "swap_%s_start" % tag)

    def scatter_start(swap, after, tag, prefill=()):
        s_send, s_recv, parts, got, _ = swap
        got = _exchange_wait(s_send, s_recv, parts, got, _plan_swap([p.ndim for p in parts]), after, "swap_%s_wait" % tag)
        sums = [_pair_sum(p, g, core, "scatter_%s_sum%d" % (tag, i)) for i, (p, g) in enumerate(zip(parts, got))]
        lands = [landing(lax.dynamic_index_in_dim(s, chip, 0, keepdims=False), chip, 4) if i in prefill
                 else lax.empty(s.shape, s.dtype) for i, s in enumerate(sums)]
        return _exchange_start(sums, lands, _plan_scatter(len(sums)), "scatter_%s_start" % tag)

    def corner(a):
        return a.reshape(-1, a.shape[-1])[:8, :128]

    def zero_after(a):
        return jnp.minimum(jnp.abs(a.reshape(-1)[0].astype(F32)), 0.0)

    g_in, g_cw, h0, memn0 = _all_gather_relayed([jnp.transpose(w_in[0]).astype(BF), conv_w[0]], [True, False], "gather_w_in",
                                                side=_rms_side([(x0, pre_norm_g), (mem0, mem_norm_g)]))
    w_in_f = g_in.reshape(D_IN, D_MODEL)
    conv_w_f = jnp.transpose(g_cw, (1, 0, 2)).reshape(CONV_W, D_RNN)

    after_first = zero_after(g_cw).astype(BF)
    rest = [w.astype(BF) + after_first for w in (w_mem_kv[0], jnp.transpose(w_br_rg[0]), jnp.transpose(w_br_swa[0]),
                                                 jnp.transpose(w_br_mem[0]), w_out[0])]
    plan_g = _plan_gather(len(rest))
    zones = _place_own([lax.empty((N_DEV,) + w.shape, w.dtype) for w in rest], rest, jnp.reshape(me, (1,)).astype(jnp.int32), "gather_rest_own")
    g_send, g_recv, g_src, g_land, g_token = _exchange_start(rest, zones, plan_g, "gather_rest_start")
    st = _forward_a(h0, memn0, w_in_f, swa_sinks, rel_bias, dep=g_token)
    g_land = _exchange_wait(g_send, g_recv, g_src, g_land, plan_g, st["y_swa"], "gather_rest_wait")
    plan_f = _plan_forward(len(rest))
    f_send, f_recv, _, g_land, f_token = _exchange_start(None, g_land, plan_f, "forward_rest_start")
    st = _forward_rg(st, conv_w_f, conv_b + f_token[0:1, 0:1], w_a_b, b_rg_a, w_x_b, b_rg_x, lru_lambda)
    g_land = _exchange_wait(f_send, f_recv, None, g_land, plan_f, corner(st["y_rg"]), "forward_rest_wait")
    w_memkv_f = g_land[0].reshape(D_MODEL, 2 * D_MEM)
    wbr = tuple(g_land[i].reshape(D_MODEL, D_RNN) for i in (1, 2, 3))
    w_out_f = g_land[4].reshape(D_MODEL, D_MODEL)

    st = _forward_b(st, x0, loss_target[0], post_norm_g, w_memkv_f, wbr, w_out_f)
    st = _backward_a1(st, wbr, w_out_f)
    parts_a = [st["dw_out"], st["dwbr"][0], st["dwbr"][1], st["dwbr"][2]]
    plan_a = _plan_scatter(len(parts_a))
    swap_a = swap_start(parts_a, "a")
    st = _backward_a2(st, mem0, w_memkv_f, conv_w_f, conv_b + swap_a[4][0:1, 0:1], w_a_b, b_rg_a, w_x_b, b_rg_x, lru_lambda)
    a_send, a_recv, a_src, a_land, a_token = scatter_start(swap_a, st["dxr"], "a")
    parts_c = [st["dw_memkv"], _owner_blocks(st["dw_a"]), _owner_blocks(st["dw_x"])]
    plan_c = _plan_scatter(len(parts_c))
    swap_c = swap_start(parts_c, "c")

    st = _backward_b(st, rel_bias, swa_sinks + swap_c[4][0:1, 0:1] + a_token[0:1, 0:1])
    c_send, c_recv, c_src, c_land, c_token = scatter_start(swap_c, st["dsinks"], "c", prefill=(1, 2))
    plan_b = _plan_scatter(1)

    def dw_in_parts(half, dep):
        dwh = _dw_in_half(st, half, dep)
        return dwh, [dwh]

    dw0, parts_b0 = dw_in_parts(0, c_token)
    swap_b0 = swap_start(parts_b0, "b0")
    a_land = _exchange_wait(a_send, a_recv, a_src, a_land, plan_a, swap_b0[4], "scatter_a_wait")
    big = [None] * 6

    chip1 = jnp.reshape(chip, (1,)).astype(jnp.int32)

    def adamw_big(j, land, own, wt, mt, vt):
        big[j] = _adamw(land, own, chip1, wt, mt, vt, "adamw_big%d" % j)

    adamw_big(5, a_land[0], a_src[0], w_out, m_w_out, v_w_out)
    adamw_big(2, a_land[1], a_src[1], w_br_rg, m_w_br_rg, v_w_br_rg)
    adamw_big(3, a_land[2], a_src[2], w_br_swa, m_w_br_swa, v_w_br_swa)
    halves = [scatter_start(swap_b0, corner(big[5][1]) + corner(big[2][1]) + corner(big[3][1]), "b0")]
    dw1, parts_b1 = dw_in_parts(1, halves[0][4])
    swap_b1 = swap_start(parts_b1, "b1")
    c_land = _exchange_wait(c_send, c_recv, c_src, c_land, plan_c, swap_b1[4], "scatter_c_wait")
    g_wa_blk = _sum_parts(c_land[1], "sum_w_rg_a")
    g_wx_blk = _sum_parts(c_land[2], "sum_w_rg_x")
    adamw_big(4, a_land[3], a_src[3], w_br_mem, m_w_br_mem, v_w_br_mem)
    adamw_big(1, c_land[0], c_src[0], w_mem_kv, m_w_mem_kv, v_w_mem_kv)
    halves.append(scatter_start(swap_b1, corner(big[4][1]) + corner(big[1][1]), "b1"))
    st["dmem_g"] = _mem_gain_grad(st, mem0, w_memkv_f, halves[1][4])
    grad_x, dpre = _dh_dx(st["dproj"], w_in_f, x0, st["dy"], pre_norm_g + halves[1][4][0:1, 0:1], st["tm"], D_IN_TILE)
    pack = jnp.concatenate([dpre.reshape(16, 128), st["dpost"].reshape(16, 128), st["dmem_g"].reshape(16, 128),
                            st["dvec"].reshape(64, 128), _pad_rows(st["dsinks"], 8), _pad_rows(st["drel"], 32), g_wa_blk, g_wx_blk,
                            st["loss"]], axis=0)
    plan_s = _plan_everyone(1)
    s_send, s_recv, s_src, s_land, s_token = _exchange_start([pack], [landing(pack, me, N_DEV)], plan_s, "gather_small_start")
    swap_last = lambda a: jnp.transpose(a, (0, 2, 1))
    after, big0_t = s_token, None
    for half, (b_send, b_recv, b_src, b_land, _) in enumerate(halves):
        b_land = _exchange_wait(b_send, b_recv, b_src, b_land, plan_b, after, "scatter_b%d_wait" % half)[0]
        big0_t = _adamw(b_land, b_src[0], chip1, swap_last(w_in), swap_last(m_w_in), swap_last(v_w_in), "adamw_big0_%d" % half,
                        group=(half, 2), into=big0_t)
        after = corner(big0_t[1])
    big[0] = [swap_last(a) for a in big0_t]
    gathered = _exchange_wait(s_send, s_recv, s_src, s_land, plan_s, corner(big0_t[1]), "gather_small_wait")[0]
    gs = _sum_parts(gathered, "sum_small")
    loss_total = gs[408, 0]
    g_pre, g_post, g_memg = gs[0:16].reshape(1, D_MODEL), gs[16:32].reshape(1, D_MODEL), gs[32:48].reshape(1, D_MODEL)
    gvec = gs[48:112].reshape(8, D_RNN)
    g_conv_w = lax.dynamic_slice(gvec[0:CONV_W], (0, me * RNN_BLOCK), (CONV_W, RNN_BLOCK))
    g_conv_b, g_b_a, g_b_x, g_lam = gvec[4:5], gvec[5:6], gvec[6:7], gvec[7:8]
    g_sinks = gs[112:113, :SWA_HEADS]
    g_rel = gs[120:152, :SWA_HEADS]
    g_w_a = gathered[:, 152:280]
    g_w_x = gathered[:, 280:408]

    g_small = (g_pre, g_post, g_memg, g_conv_b, g_b_a, g_b_x, g_lam, g_w_a, g_w_x, g_sinks, g_rel, g_conv_w)
    w_small = (pre_norm_g, post_norm_g, mem_norm_g, conv_b, b_rg_a, b_rg_x, lru_lambda, w_rg_a, w_rg_x, swa_sinks, rel_bias, conv_w)
    m_small = (m_pre_norm_g, m_post_norm_g, m_mem_norm_g, m_conv_b, m_b_rg_a, m_b_rg_x, m_lru_lambda, m_w_rg_a, m_w_rg_x, m_swa_sinks, m_rel_bias, m_conv_w)
    v_small = (v_pre_norm_g, v_post_norm_g, v_mem_norm_g, v_conv_b, v_b_rg_a, v_b_rg_x, v_lru_lambda, v_w_rg_a, v_w_rg_x, v_swa_sinks, v_rel_bias, v_conv_w)
    sm = _adamw_small(g_small, w_small, m_small, v_small)


    def leaves(k):
        s = sm[k]
        return [s[0], s[1], s[2], big[0][k], s[11], s[3], s[7], s[4], s[8], s[5], s[6], s[9], s[10],
                big[1][k], big[2][k], big[3][k], big[4][k], big[5][k]]

    return (loss_total, grad_x[None], *leaves(0), *leaves(1), *leaves(2), *leaves(3))
```
